```python
import jax, jax.numpy as jnp
from jax import lax
import numpy as np

D_MODEL = 1024
BATCH = 8
SEQ = 8192
DEPTH = 2

N_MIXERS = 2
EXPAND = 2
D_INNER = EXPAND * D_MODEL
CONV_WIDTH = 3
CHUNK = 128
GMLP_GROUPS = 8
GROUP_WIDTH = D_INNER // GMLP_GROUPS
N_CONV_LAYERS = (DEPTH + 1) // 2
N_GMLP_LAYERS = DEPTH // 2
RMS_EPS = 1e-6
LN_EPS = 1e-5

kernel_name = "hybrid_shortconv_chunked_gmlp_adaln"


def rms_norm(x, g):
    xf = x.astype(jnp.float32)
    y = xf * lax.rsqrt(jnp.mean(xf * xf, axis=-1, keepdims=True) + RMS_EPS)
    return (y * g.astype(jnp.float32)).astype(x.dtype)


def layer_norm(x, g, b):
    xf = x.astype(jnp.float32)
    mu = jnp.mean(xf, axis=-1, keepdims=True)
    var = jnp.mean(jnp.square(xf - mu), axis=-1, keepdims=True)
    y = (xf - mu) * lax.rsqrt(var + LN_EPS)
    return (y * g.astype(jnp.float32) + b.astype(jnp.float32)).astype(x.dtype)


def short_conv_mixer(h, w_in, conv_w, conv_b, w_out):
    seq = h.shape[1]
    proj = h @ w_in
    b_gate, c_gate, xin, z = jnp.split(proj, 4, axis=-1)
    cx = c_gate * xin
    padded = jnp.pad(cx, ((0, 0), (CONV_WIDTH - 1, 0), (0, 0)))
    conv = conv_b + conv_w[CONV_WIDTH - 1] * cx
    for k in range(CONV_WIDTH - 1):
        conv = conv + conv_w[k] * padded[:, k:k + seq]
    y = jax.nn.silu(z) * b_gate * conv
    return y @ w_out


def chunked_gmlp_mixer(h, w_in, ln_g, ln_b, w_s, b_s, w_out):
    bsz, seq, _ = h.shape
    proj = h @ w_in
    uv, z = proj[..., :2 * D_INNER], proj[..., 2 * D_INNER:]
    u, v = jnp.split(jax.nn.gelu(uv, approximate=False), 2, axis=-1)
    v = layer_norm(v, ln_g, ln_b)
    n_chunks = seq // CHUNK
    v = v.reshape(bsz, n_chunks, CHUNK, GMLP_GROUPS, GROUP_WIDTH)
    causal = jnp.tril(jnp.ones((CHUNK, CHUNK), dtype=bool))
    w = jnp.where(causal[None], w_s, jnp.zeros_like(w_s)).astype(v.dtype)
    mixed = jnp.einsum('gts,bnsgc->bntgc', w, v)
    mixed = mixed + jnp.transpose(b_s)[None, None, :, :, None].astype(v.dtype)
    s = u * mixed.reshape(bsz, seq, D_INNER)
    y = jax.nn.silu(z) * s
    return y @ w_out


def _fwd_setup_inputs(seed: int = 0) -> dict:
    key = jax.random.key(seed)
    ks = jax.random.split(key, 20)
    nrm = jax.random.normal
    d, e = D_MODEL, D_INNER
    return {
        "x": nrm(ks[0], (BATCH, SEQ, d), jnp.float32),
        "c": nrm(ks[1], (BATCH, d), jnp.float32),
        "mod_w": nrm(ks[2], (DEPTH, d, 3 * d), jnp.float32) * (0.5 * d ** -0.5),
        "mod_b": nrm(ks[3], (DEPTH, 3 * d), jnp.float32) * 0.02,
        "norm_g": 1.0 + 0.05 * nrm(ks[4], (DEPTH, d), jnp.float32),
        "a_w_in": nrm(ks[5], (N_CONV_LAYERS, d, 4 * e), jnp.float32) * d ** -0.5,
        "a_conv_w": nrm(ks[6], (N_CONV_LAYERS, CONV_WIDTH, e), jnp.float32) * CONV_WIDTH ** -0.5,
        "a_conv_b": nrm(ks[7], (N_CONV_LAYERS, e), jnp.float32) * 0.02,
        "a_w_out": nrm(ks[8], (N_CONV_LAYERS, e, d), jnp.float32) * e ** -0.5,
        "b_w_in": nrm(ks[9], (N_GMLP_LAYERS, d, 3 * e), jnp.float32) * d ** -0.5,
        "b_ln_g": 1.0 + 0.05 * nrm(ks[10], (N_GMLP_LAYERS, e), jnp.float32),
        "b_ln_b": 0.02 * nrm(ks[11], (N_GMLP_LAYERS, e), jnp.float32),
        "b_w_s": nrm(ks[12], (N_GMLP_LAYERS, GMLP_GROUPS, CHUNK, CHUNK), jnp.float32) * CHUNK ** -0.5,
        "b_b_s": 1.0 + 0.1 * nrm(ks[13], (N_GMLP_LAYERS, GMLP_GROUPS, CHUNK), jnp.float32),
        "b_w_out": nrm(ks[14], (N_GMLP_LAYERS, e, d), jnp.float32) * e ** -0.5,
        "final_g": 1.0 + 0.05 * nrm(ks[15], (d,), jnp.float32),
    }


def _fwd_reference(x, c, mod_w, mod_b, norm_g, a_w_in, a_conv_w, a_conv_b, a_w_out,
              b_w_in, b_ln_g, b_ln_b, b_w_s, b_b_s, b_w_out, final_g):
    c_act = jax.nn.silu(c)
    for i in range(DEPTH):
        mod = c_act @ mod_w[i] + mod_b[i]
        shift, scale, gate = jnp.split(mod[:, None, :], 3, axis=-1)
        h = rms_norm(x, norm_g[i]) * (1.0 + scale) + shift
        j = i // N_MIXERS
        if i % N_MIXERS == 0:
            branch = short_conv_mixer(h, a_w_in[j], a_conv_w[j], a_conv_b[j], a_w_out[j])
        else:
            branch = chunked_gmlp_mixer(h, b_w_in[j], b_ln_g[j], b_ln_b[j],
                                        b_w_s[j], b_b_s[j], b_w_out[j])
        x = x + gate * branch
    return rms_norm(x, final_g)


import jax as _jax
import jax.numpy as _jnp

TWIN_FORMAT = 'train_step'
FWD_PARAMS = ['x', 'c', 'mod_w', 'mod_b', 'norm_g', 'a_w_in', 'a_conv_w', 'a_conv_b', 'a_w_out', 'b_w_in', 'b_ln_g', 'b_ln_b', 'b_w_s', 'b_b_s', 'b_w_out', 'final_g']
TWIN_WEIGHTS = ['mod_w', 'mod_b', 'norm_g', 'a_w_in', 'a_conv_w', 'a_conv_b', 'a_w_out', 'b_w_in', 'b_ln_g', 'b_ln_b', 'b_w_s', 'b_b_s', 'b_w_out', 'final_g']
TWIN_DIFF_INPUT = 'x'
TWIN_INPUTS = ['x', 'c', 'mod_w', 'mod_b', 'norm_g', 'a_w_in', 'a_conv_w', 'a_conv_b', 'a_w_out', 'b_w_in', 'b_ln_g', 'b_ln_b', 'b_w_s', 'b_b_s', 'b_w_out', 'final_g', 'loss_target', 'm_mod_w', 'm_mod_b', 'm_norm_g', 'm_a_w_in', 'm_a_conv_w', 'm_a_conv_b', 'm_a_w_out', 'm_b_w_in', 'm_b_ln_g', 'm_b_ln_b', 'm_b_w_s', 'm_b_b_s', 'm_b_w_out', 'm_final_g', 'v_mod_w', 'v_mod_b', 'v_norm_g', 'v_a_w_in', 'v_a_conv_w', 'v_a_conv_b', 'v_a_w_out', 'v_b_w_in', 'v_b_ln_g', 'v_b_ln_b', 'v_b_w_s', 'v_b_b_s', 'v_b_w_out', 'v_final_g']
TWIN_OUTPUTS = ['loss', 'grad_x', 'grad_mod_w', 'grad_mod_b', 'grad_norm_g', 'grad_a_w_in', 'grad_a_conv_w', 'grad_a_conv_b', 'grad_a_w_out', 'grad_b_w_in', 'grad_b_ln_g', 'grad_b_ln_b', 'grad_b_w_s', 'grad_b_b_s', 'grad_b_w_out', 'grad_final_g', 'delta_mod_w', 'delta_mod_b', 'delta_norm_g', 'delta_a_w_in', 'delta_a_conv_w', 'delta_a_conv_b', 'delta_a_w_out', 'delta_b_w_in', 'delta_b_ln_g', 'delta_b_ln_b', 'delta_b_w_s', 'delta_b_b_s', 'delta_b_w_out', 'delta_final_g', 'new_m_mod_w', 'new_m_mod_b', 'new_m_norm_g', 'new_m_a_w_in', 'new_m_a_conv_w', 'new_m_a_conv_b', 'new_m_a_w_out', 'new_m_b_w_in', 'new_m_b_ln_g', 'new_m_b_ln_b', 'new_m_b_w_s', 'new_m_b_b_s', 'new_m_b_w_out', 'new_m_final_g', 'new_v_mod_w', 'new_v_mod_b', 'new_v_norm_g', 'new_v_a_w_in', 'new_v_a_conv_w', 'new_v_a_conv_b', 'new_v_a_w_out', 'new_v_b_w_in', 'new_v_b_ln_g', 'new_v_b_ln_b', 'new_v_b_w_s', 'new_v_b_b_s', 'new_v_b_w_out', 'new_v_final_g']
TWIN_LEAF_KINDS = {'loss': 'loss', 'grad_x': 'grad_x', 'grad_mod_w': 'grad_w', 'grad_mod_b': 'grad_w', 'grad_norm_g': 'grad_w', 'grad_a_w_in': 'grad_w', 'grad_a_conv_w': 'grad_w', 'grad_a_conv_b': 'grad_w', 'grad_a_w_out': 'grad_w', 'grad_b_w_in': 'grad_w', 'grad_b_ln_g': 'grad_w', 'grad_b_ln_b': 'grad_w', 'grad_b_w_s': 'grad_w', 'grad_b_b_s': 'grad_w', 'grad_b_w_out': 'grad_w', 'grad_final_g': 'grad_w', 'delta_mod_w': 'delta_w', 'delta_mod_b': 'delta_w', 'delta_norm_g': 'delta_w', 'delta_a_w_in': 'delta_w', 'delta_a_conv_w': 'delta_w', 'delta_a_conv_b': 'delta_w', 'delta_a_w_out': 'delta_w', 'delta_b_w_in': 'delta_w', 'delta_b_ln_g': 'delta_w', 'delta_b_ln_b': 'delta_w', 'delta_b_w_s': 'delta_w', 'delta_b_b_s': 'delta_w', 'delta_b_w_out': 'delta_w', 'delta_final_g': 'delta_w', 'new_m_mod_w': 'new_m', 'new_m_mod_b': 'new_m', 'new_m_norm_g': 'new_m', 'new_m_a_w_in': 'new_m', 'new_m_a_conv_w': 'new_m', 'new_m_a_conv_b': 'new_m', 'new_m_a_w_out': 'new_m', 'new_m_b_w_in': 'new_m', 'new_m_b_ln_g': 'new_m', 'new_m_b_ln_b': 'new_m', 'new_m_b_w_s': 'new_m', 'new_m_b_b_s': 'new_m', 'new_m_b_w_out': 'new_m', 'new_m_final_g': 'new_m', 'new_v_mod_w': 'new_v', 'new_v_mod_b': 'new_v', 'new_v_norm_g': 'new_v', 'new_v_a_w_in': 'new_v', 'new_v_a_conv_w': 'new_v', 'new_v_a_conv_b': 'new_v', 'new_v_a_w_out': 'new_v', 'new_v_b_w_in': 'new_v', 'new_v_b_ln_g': 'new_v', 'new_v_b_ln_b': 'new_v', 'new_v_b_w_s': 'new_v', 'new_v_b_b_s': 'new_v', 'new_v_b_w_out': 'new_v', 'new_v_final_g': 'new_v'}


def _forward(args):
    return _fwd_reference(*[args[k] for k in FWD_PARAMS])


def _output_shape():
    def fwd():
        inp = _fwd_setup_inputs(0)
        return _fwd_reference(*[inp[k] for k in FWD_PARAMS])
    out = _jax.eval_shape(fwd)
    return out.shape, out.dtype

N_MICROBATCH = 1
ADAM_LR = 0.001
ADAM_B1 = 0.9
ADAM_B2 = 0.999
ADAM_EPS = 1e-08
ADAM_WD = 0.01
ADAM_STEP = 10
PER_EXAMPLE_BATCH_AXIS = {'x': 0, 'c': 0, 'loss_target': 0}
SHARED_INPUTS = []
_WEIGHT_DTYPES = {'mod_w': _jnp.float32, 'mod_b': _jnp.float32, 'norm_g': _jnp.float32, 'a_w_in': _jnp.float32, 'a_conv_w': _jnp.float32, 'a_conv_b': _jnp.float32, 'a_w_out': _jnp.float32, 'b_w_in': _jnp.float32, 'b_ln_g': _jnp.float32, 'b_ln_b': _jnp.float32, 'b_w_s': _jnp.float32, 'b_b_s': _jnp.float32, 'b_w_out': _jnp.float32, 'final_g': _jnp.float32}
MOMENT_SCALE = {'mod_w': 1.074890e-01, 'mod_b': 2.028609e-01, 'norm_g': 1.004076e-01, 'a_w_in': 4.624137e-02, 'a_conv_w': 4.581676e-02, 'a_conv_b': 3.893790e-02, 'a_w_out': 6.540208e-02, 'b_w_in': 2.786065e-02, 'b_ln_g': 1.736596e-02, 'b_ln_b': 1.793193e-02, 'b_w_s': 2.455376e-02, 'b_b_s': 3.473014e-02, 'b_w_out': 4.247517e-02, 'final_g': 6.402670e+01}


def _to_microbatches(a, axis):
    t = _jnp.moveaxis(a, axis, 0)
    t = t.reshape((N_MICROBATCH, t.shape[0] // N_MICROBATCH) + t.shape[1:])
    return _jnp.moveaxis(t, 1, axis + 1)


def setup_inputs(seed: int = 0) -> dict:
    inp = _fwd_setup_inputs(seed)
    key = _jax.random.fold_in(_jax.random.key(seed), 7919)
    shape, _ = _output_shape()
    out = dict(inp)
    out["loss_target"] = _jax.random.normal(_jax.random.fold_in(key, 0), shape, _jnp.float32)
    for i, name in enumerate(TWIN_WEIGHTS):
        w = inp[name].astype(_jnp.float32)
        if MOMENT_SCALE is None:
            s = _jnp.sqrt(_jnp.mean(_jnp.square(w)) + 1e-30)
        else:
            s = MOMENT_SCALE[name]
        km, kv = _jax.random.split(_jax.random.fold_in(key, i + 1))
        out[name] = w
        out["m_" + name] = s * _jax.random.normal(km, w.shape, _jnp.float32)
        out["v_" + name] = (s * s) * _jax.random.uniform(kv, w.shape, _jnp.float32, 0.5, 1.5)
    if N_MICROBATCH > 1:
        for name, axis in PER_EXAMPLE_BATCH_AXIS.items():
            out[name] = _to_microbatches(out[name], axis)
    return {'x': out['x'], 'c': out['c'], 'mod_w': out['mod_w'], 'mod_b': out['mod_b'], 'norm_g': out['norm_g'], 'a_w_in': out['a_w_in'], 'a_conv_w': out['a_conv_w'], 'a_conv_b': out['a_conv_b'], 'a_w_out': out['a_w_out'], 'b_w_in': out['b_w_in'], 'b_ln_g': out['b_ln_g'], 'b_ln_b': out['b_ln_b'], 'b_w_s': out['b_w_s'], 'b_b_s': out['b_b_s'], 'b_w_out': out['b_w_out'], 'final_g': out['final_g'], 'loss_target': out['loss_target'], 'm_mod_w': out['m_mod_w'], 'm_mod_b': out['m_mod_b'], 'm_norm_g': out['m_norm_g'], 'm_a_w_in': out['m_a_w_in'], 'm_a_conv_w': out['m_a_conv_w'], 'm_a_conv_b': out['m_a_conv_b'], 'm_a_w_out': out['m_a_w_out'], 'm_b_w_in': out['m_b_w_in'], 'm_b_ln_g': out['m_b_ln_g'], 'm_b_ln_b': out['m_b_ln_b'], 'm_b_w_s': out['m_b_w_s'], 'm_b_b_s': out['m_b_b_s'], 'm_b_w_out': out['m_b_w_out'], 'm_final_g': out['m_final_g'], 'v_mod_w': out['v_mod_w'], 'v_mod_b': out['v_mod_b'], 'v_norm_g': out['v_norm_g'], 'v_a_w_in': out['v_a_w_in'], 'v_a_conv_w': out['v_a_conv_w'], 'v_a_conv_b': out['v_a_conv_b'], 'v_a_w_out': out['v_a_w_out'], 'v_b_w_in': out['v_b_w_in'], 'v_b_ln_g': out['v_b_ln_g'], 'v_b_ln_b': out['v_b_ln_b'], 'v_b_w_s': out['v_b_w_s'], 'v_b_b_s': out['v_b_b_s'], 'v_b_w_out': out['v_b_w_out'], 'v_final_g': out['v_final_g']}


def _loss(weights, diff, rest, loss_target):
    with _jax.named_scope("forward"):
        args = {**rest, TWIN_DIFF_INPUT: diff, **{k: w.astype(_WEIGHT_DTYPES[k]) for k, w in weights.items()}}
        y = _forward(args)
    with _jax.named_scope("loss_head"):
        err = _jnp.square(y.astype(_jnp.float32) - loss_target)
        return 0.5 * _jnp.sum(_jnp.mean(err, axis=-1)) if err.ndim else 0.5 * err


def _adamw(w, g, m, v):
    m = ADAM_B1 * m + (1.0 - ADAM_B1) * g
    v = ADAM_B2 * v + (1.0 - ADAM_B2) * _jnp.square(g)
    m_hat = m / (1.0 - ADAM_B1 ** ADAM_STEP)
    v_hat = v / (1.0 - ADAM_B2 ** ADAM_STEP)
    delta = -ADAM_LR * (m_hat / (_jnp.sqrt(v_hat) + ADAM_EPS) + ADAM_WD * w)
    return delta, m, v


def reference(x, c, mod_w, mod_b, norm_g, a_w_in, a_conv_w, a_conv_b, a_w_out, b_w_in, b_ln_g, b_ln_b, b_w_s, b_b_s, b_w_out, final_g, loss_target, m_mod_w, m_mod_b, m_norm_g, m_a_w_in, m_a_conv_w, m_a_conv_b, m_a_w_out, m_b_w_in, m_b_ln_g, m_b_ln_b, m_b_w_s, m_b_b_s, m_b_w_out, m_final_g, v_mod_w, v_mod_b, v_norm_g, v_a_w_in, v_a_conv_w, v_a_conv_b, v_a_w_out, v_b_w_in, v_b_ln_g, v_b_ln_b, v_b_w_s, v_b_b_s, v_b_w_out, v_final_g):
    given = dict(x=x, c=c, mod_w=mod_w, mod_b=mod_b, norm_g=norm_g, a_w_in=a_w_in, a_conv_w=a_conv_w, a_conv_b=a_conv_b, a_w_out=a_w_out, b_w_in=b_w_in, b_ln_g=b_ln_g, b_ln_b=b_ln_b, b_w_s=b_w_s, b_b_s=b_b_s, b_w_out=b_w_out, final_g=final_g, loss_target=loss_target, m_mod_w=m_mod_w, m_mod_b=m_mod_b, m_norm_g=m_norm_g, m_a_w_in=m_a_w_in, m_a_conv_w=m_a_conv_w, m_a_conv_b=m_a_conv_b, m_a_w_out=m_a_w_out, m_b_w_in=m_b_w_in, m_b_ln_g=m_b_ln_g, m_b_ln_b=m_b_ln_b, m_b_w_s=m_b_w_s, m_b_b_s=m_b_b_s, m_b_w_out=m_b_w_out, m_final_g=m_final_g, v_mod_w=v_mod_w, v_mod_b=v_mod_b, v_norm_g=v_norm_g, v_a_w_in=v_a_w_in, v_a_conv_w=v_a_conv_w, v_a_conv_b=v_a_conv_b, v_a_w_out=v_a_w_out, v_b_w_in=v_b_w_in, v_b_ln_g=v_b_ln_g, v_b_ln_b=v_b_ln_b, v_b_w_s=v_b_w_s, v_b_b_s=v_b_b_s, v_b_w_out=v_b_w_out, v_final_g=v_final_g)
    weights = {n: given[n] for n in TWIN_WEIGHTS}
    shared = {n: given[n] for n in SHARED_INPUTS}
    per_example = {n: given[n] for n in ['x', 'c']}
    grad_fn = _jax.value_and_grad(_loss, argnums=(0, 1))

    def one_microbatch(ex, loss_target):
        ex = dict(ex)
        diff = ex.pop(TWIN_DIFF_INPUT)
        return grad_fn(weights, diff, {**shared, **ex}, loss_target)

    if N_MICROBATCH == 1:
        loss, (grad_w, grad_x) = one_microbatch(per_example, given["loss_target"])
    else:
        def body(carry, xs):
            loss_sum, grad_sum = carry
            l_k, (gw_k, gx_k) = one_microbatch(xs[0], xs[1])
            with _jax.named_scope("update"):
                return (loss_sum + l_k, _jax.tree.map(_jnp.add, grad_sum, gw_k)), gx_k

        init = (_jnp.zeros((), _jnp.float32), _jax.tree.map(_jnp.zeros_like, weights))
        (loss, grad_w), grad_x = _jax.lax.scan(body, init, (per_example, given["loss_target"]))
    with _jax.named_scope("update"):
        delta_w, new_m, new_v = {}, {}, {}
        for n in TWIN_WEIGHTS:
            delta_w[n], new_m[n], new_v[n] = _adamw(weights[n], grad_w[n], given["m_" + n], given["v_" + n])
    return (loss, grad_x, *[grad_w[n] for n in TWIN_WEIGHTS], *[delta_w[n] for n in TWIN_WEIGHTS],
            *[new_m[n] for n in TWIN_WEIGHTS], *[new_v[n] for n in TWIN_WEIGHTS])
```

```python
import functools
import math

import jax
import jax.numpy as jnp
from jax import lax
from jax.experimental import pallas as pl
from jax.experimental.pallas import tpu as pltpu

F32 = jnp.float32
BF16 = jnp.bfloat16
MESH = pl.DeviceIdType.MESH

N_DEV = 8
N_CHIP = 4
SUBLANE = 8
RMS_EPS = 1e-6
LN_EPS = 1e-5
CHUNK = 128
GROUPS = 8
ADAM_LR = 0.001
ADAM_B1 = 0.9
ADAM_B2 = 0.999
ADAM_EPS = 1e-08
ADAM_WD = 0.01
ADAM_STEP = 10
VMEM_LIMIT = 56 << 20

HBM_SPEC = pl.BlockSpec(memory_space=pltpu.HBM)
VMEM_SPEC = pl.BlockSpec(memory_space=pltpu.VMEM)


def _params(*sem):
    return pltpu.CompilerParams(dimension_semantics=sem, vmem_limit_bytes=VMEM_LIMIT)


def _tile(n, want):
    if n <= want:
        return n
    t = want
    while n % t:
        t -= 128
    return t


def _sigmoid(x):
    return 1.0 / (1.0 + jnp.exp(-x))


def _silu_and_grad(x):
    s = _sigmoid(x)
    return x * s, s * (1.0 + x * (1.0 - s))


def _gelu_and_grad(x):
    cdf = 0.5 * (1.0 + lax.erf(x * (1.0 / math.sqrt(2.0))))
    pdf = jnp.exp(-0.5 * x * x) * (1.0 / math.sqrt(2.0 * math.pi))
    return x * cdf, cdf + x * pdf


def _gelu(x):
    return x * (0.5 * (1.0 + lax.erf(x * (1.0 / math.sqrt(2.0)))))


def _rms(x):
    r = lax.rsqrt(jnp.mean(x * x, axis=-1, keepdims=True) + RMS_EPS)
    return x * r, r


def _rms_bwd(dxn, xn, r):
    return r * (dxn - xn * jnp.mean(dxn * xn, axis=-1, keepdims=True))


def _colsum(a):
    return jnp.sum(a, axis=0, keepdims=True)


def _shift_down(cur, before, k):
    rolled = pltpu.roll(cur, k, 0)
    row = lax.broadcasted_iota(jnp.int32, before.shape, 0)
    head = jnp.where(row < k, pltpu.roll(before, k, 0), rolled[:SUBLANE])
    return jnp.concatenate([head, rolled[SUBLANE:]], axis=0)


def _shift_up(cur, after, k):
    n = cur.shape[0]
    rolled = pltpu.roll(cur, n - k, 0)
    row = lax.broadcasted_iota(jnp.int32, after.shape, 0)
    tail = jnp.where(row >= SUBLANE - k, pltpu.roll(after, SUBLANE - k, 0), rolled[n - SUBLANE:])
    return jnp.concatenate([rolled[:n - SUBLANE], tail], axis=0)


def _mm_proj(h, w, n_split, name):
    s, d = h.shape
    e = w.shape[1] // n_split
    tm, tn = _tile(s, 1024), _tile(e, 1024)
    nj = e // tn

    def body(h_ref, w_ref, o_ref):
        o_ref[...] = jnp.dot(h_ref[...], w_ref[...], preferred_element_type=F32)

    return pl.pallas_call(
        body, name=name,
        out_shape=jax.ShapeDtypeStruct((n_split, s, e), F32),
        grid=(s // tm, n_split * nj),
        in_specs=[pl.BlockSpec((tm, d), lambda i, j: (i, 0)), pl.BlockSpec((d, tn), lambda i, j: (0, j))],
        out_specs=pl.BlockSpec((None, tm, tn), lambda i, j: (j // nj, i, j % nj)),
        compiler_params=_params("parallel", "parallel"),
    )(h, w)


def _mm_nn(a, b, name):
    m, k = a.shape
    n = b.shape[1]
    tm, tn = _tile(m, 1024), _tile(n, 1024)

    def body(a_ref, b_ref, o_ref):
        o_ref[...] = jnp.dot(a_ref[...], b_ref[...], preferred_element_type=F32)

    return pl.pallas_call(
        body, name=name,
        out_shape=jax.ShapeDtypeStruct((m, n), F32),
        grid=(m // tm, n // tn),
        in_specs=[pl.BlockSpec((tm, k), lambda i, j: (i, 0)), pl.BlockSpec((k, tn), lambda i, j: (0, j))],
        out_specs=pl.BlockSpec((tm, tn), lambda i, j: (i, j)),
        compiler_params=_params("parallel", "parallel"),
    )(a, b)


def _mm_nt(a, b, name):
    m, k = a.shape
    n = b.shape[0]
    tm, tn = _tile(m, 1024), _tile(n, 1024)

    def body(a_ref, b_ref, o_ref):
        o_ref[...] = lax.dot_general(a_ref[...], b_ref[...], (((1,), (1,)), ((), ())), preferred_element_type=F32)

    return pl.pallas_call(
        body, name=name,
        out_shape=jax.ShapeDtypeStruct((m, n), F32),
        grid=(m // tm, n // tn),
        in_specs=[pl.BlockSpec((tm, k), lambda i, j: (i, 0)), pl.BlockSpec((tn, k), lambda i, j: (j, 0))],
        out_specs=pl.BlockSpec((tm, tn), lambda i, j: (i, j)),
        compiler_params=_params("parallel", "parallel"),
    )(a, b)


def _mm_dh(dp, w, name):
    nq, s, e = dp.shape
    d = w.shape[0]
    tm, tn, tk = _tile(s, 1024), _tile(d, 1024), _tile(e, 1024)
    nkq = e // tk
    nk = nq * nkq

    def body(a_ref, b_ref, o_ref):
        @pl.when(pl.program_id(2) == 0)
        def _():
            o_ref[...] = jnp.zeros_like(o_ref)
        o_ref[...] += lax.dot_general(a_ref[...], b_ref[...], (((1,), (1,)), ((), ())), preferred_element_type=F32)

    return pl.pallas_call(
        body, name=name,
        out_shape=jax.ShapeDtypeStruct((s, d), F32),
        grid=(s // tm, d // tn, nk),
        in_specs=[pl.BlockSpec((None, tm, tk), lambda i, j, k: (k // nkq, i, k % nkq)),
                  pl.BlockSpec((tn, tk), lambda i, j, k: (j, k))],
        out_specs=pl.BlockSpec((tm, tn), lambda i, j, k: (i, j)),
        compiler_params=_params("parallel", "parallel", "arbitrary"),
    )(dp, w)


def _mm_dw_in(h, dp, name):
    s, d = h.shape
    nq, _, e = dp.shape
    tm, tn, tt = _tile(d, 1024), _tile(e, 1024), _tile(s, 512)
    nj = e // tn

    def body(a_ref, b_ref, o_ref):
        @pl.when(pl.program_id(2) == 0)
        def _():
            o_ref[...] = jnp.zeros_like(o_ref)
        o_ref[...] += lax.dot_general(a_ref[...], b_ref[...], (((0,), (0,)), ((), ())), preferred_element_type=F32)

    return pl.pallas_call(
        body, name=name,
        out_shape=jax.ShapeDtypeStruct((d, nq * e), F32),
        grid=(d // tm, nq * nj, s // tt),
        in_specs=[pl.BlockSpec((tt, tm), lambda i, j, t: (t, i)),
                  pl.BlockSpec((None, tt, tn), lambda i, j, t: (j // nj, t, j % nj))],
        out_specs=pl.BlockSpec((tm, tn), lambda i, j, t: (i, j)),
        compiler_params=_params("parallel", "parallel", "arbitrary"),
    )(h, dp)


def _mm_dw_out(y, dbr, name):
    s, e = y.shape
    d = dbr.shape[1]
    tm, tn, tt = _tile(e, 1024), _tile(d, 1024), _tile(s, 512)

    def body(a_ref, b_ref, o_ref):
        @pl.when(pl.program_id(2) == 0)
        def _():
            o_ref[...] = jnp.zeros_like(o_ref)
        o_ref[...] += lax.dot_general(a_ref[...], b_ref[...], (((0,), (0,)), ((), ())), preferred_element_type=F32)

    return pl.pallas_call(
        body, name=name,
        out_shape=jax.ShapeDtypeStruct((e, d), F32),
        grid=(e // tm, d // tn, s // tt),
        in_specs=[pl.BlockSpec((tt, tm), lambda i, j, t: (t, i)), pl.BlockSpec((tt, tn), lambda i, j, t: (t, j))],
        out_specs=pl.BlockSpec((tm, tn), lambda i, j, t: (i, j)),
        compiler_params=_params("parallel", "parallel", "arbitrary"),
    )(y, dbr)


def _row_spec(ts, d):
    return pl.BlockSpec((ts, d), lambda i: (i, 0))


def _vec_spec(d):
    return pl.BlockSpec((1, d), lambda i: (0, 0))


def _norm_mod(x, g, scale, shift, name):
    s, d = x.shape
    ts = _tile(s, 512)

    def body(x_ref, g_ref, sc_ref, sh_ref, h_ref):
        xn, _ = _rms(x_ref[...])
        h_ref[...] = ((xn * g_ref[...]) * (1.0 + sc_ref[...]) + sh_ref[...]).astype(BF16)

    return pl.pallas_call(
        body, name=name, out_shape=jax.ShapeDtypeStruct((s, d), BF16), grid=(s // ts,),
        in_specs=[_row_spec(ts, d), _vec_spec(d), _vec_spec(d), _vec_spec(d)],
        out_specs=_row_spec(ts, d), compiler_params=_params("parallel"),
    )(x, g, scale, shift)


def _residual_norm_mod(x, br, gate, g, scale, shift, name):
    s, d = x.shape
    ts = _tile(s, 512)

    def body(x_ref, br_ref, gate_ref, g_ref, sc_ref, sh_ref, x1_ref, h_ref):
        x1 = x_ref[...] + gate_ref[...] * br_ref[...]
        x1_ref[...] = x1
        xn, _ = _rms(x1)
        h_ref[...] = ((xn * g_ref[...]) * (1.0 + sc_ref[...]) + sh_ref[...]).astype(BF16)

    return pl.pallas_call(
        body, name=name,
        out_shape=(jax.ShapeDtypeStruct((s, d), F32), jax.ShapeDtypeStruct((s, d), BF16)), grid=(s // ts,),
        in_specs=[_row_spec(ts, d), _row_spec(ts, d)] + [_vec_spec(d)] * 4,
        out_specs=(_row_spec(ts, d), _row_spec(ts, d)), compiler_params=_params("parallel"),
    )(x, br, gate, g, scale, shift)


def _head(x1, br, gate, gf, target, name):
    s, d = x1.shape
    ts = _tile(s, 512)

    def body(x1_ref, br_ref, gate_ref, gf_ref, tg_ref, dx_ref, dbr_ref, loss_ref, dgf_ref, dgate_ref):
        @pl.when(pl.program_id(0) == 0)
        def _():
            loss_ref[...] = jnp.zeros_like(loss_ref)
            dgf_ref[...] = jnp.zeros_like(dgf_ref)
            dgate_ref[...] = jnp.zeros_like(dgate_ref)
        br = br_ref[...]
        gate = gate_ref[...]
        gf = gf_ref[...]
        xn, r = _rms(x1_ref[...] + gate * br)
        err = xn * gf - tg_ref[...]
        loss_ref[...] += 0.5 * jnp.sum(jnp.mean(err * err, axis=-1, keepdims=True))
        dout = err * (1.0 / d)
        dgf_ref[...] += _colsum(dout * xn)
        dx = _rms_bwd(dout * gf, xn, r)
        dx_ref[...] = dx
        dgate_ref[...] += _colsum(dx * br)
        dbr_ref[...] = (gate * dx).astype(BF16)

    return pl.pallas_call(
        body, name=name,
        out_shape=(jax.ShapeDtypeStruct((s, d), F32), jax.ShapeDtypeStruct((s, d), BF16),
                   jax.ShapeDtypeStruct((SUBLANE, 128), F32), jax.ShapeDtypeStruct((1, d), F32),
                   jax.ShapeDtypeStruct((1, d), F32)),
        grid=(s // ts,),
        in_specs=[_row_spec(ts, d), _row_spec(ts, d), _vec_spec(d), _vec_spec(d), _row_spec(ts, d)],
        out_specs=(_row_spec(ts, d), _row_spec(ts, d), pl.BlockSpec((SUBLANE, 128), lambda i: (0, 0)),
                   _vec_spec(d), _vec_spec(d)),
        compiler_params=_params("arbitrary"),
    )(x1, br, gate, gf, target)


def _norm_mod_bwd(dh, x, dx_in, g, scale, name, br=None, gate=None):
    s, d = x.shape
    ts = _tile(s, 512)
    has_branch = br is not None

    def body(*refs):
        if has_branch:
            (dh_ref, x_ref, dxin_ref, g_ref, sc_ref, br_ref, gate_ref,
             dx_ref, dsh_ref, dsc_ref, dg_ref, dbr_ref, dgate_ref) = refs
        else:
            dh_ref, x_ref, dxin_ref, g_ref, sc_ref, dx_ref, dsh_ref, dsc_ref, dg_ref = refs

        @pl.when(pl.program_id(0) == 0)
        def _():
            dsh_ref[...] = jnp.zeros_like(dsh_ref)
            dsc_ref[...] = jnp.zeros_like(dsc_ref)
            dg_ref[...] = jnp.zeros_like(dg_ref)
            if has_branch:
                dgate_ref[...] = jnp.zeros_like(dgate_ref)
        dh = dh_ref[...]
        g = g_ref[...]
        xn, r = _rms(x_ref[...])
        dsh_ref[...] += _colsum(dh)
        dsc_ref[...] += _colsum(dh * (xn * g))
        da = dh * (1.0 + sc_ref[...])
        dg_ref[...] += _colsum(da * xn)
        dx = dxin_ref[...] + _rms_bwd(da * g, xn, r)
        dx_ref[...] = dx
        if has_branch:
            dgate_ref[...] += _colsum(dx * br_ref[...])
            dbr_ref[...] = (gate_ref[...] * dx).astype(BF16)

    vec_out = jax.ShapeDtypeStruct((1, d), F32)
    out_shape = [jax.ShapeDtypeStruct((s, d), F32), vec_out, vec_out, vec_out]
    out_specs = [_row_spec(ts, d), _vec_spec(d), _vec_spec(d), _vec_spec(d)]
    in_specs = [_row_spec(ts, d)] * 3 + [_vec_spec(d)] * 2
    args = [dh, x, dx_in, g, scale]
    if has_branch:
        out_shape += [jax.ShapeDtypeStruct((s, d), BF16), vec_out]
        out_specs += [_row_spec(ts, d), _vec_spec(d)]
        in_specs += [_row_spec(ts, d), _vec_spec(d)]
        args += [br, gate]
    return pl.pallas_call(
        body, name=name, out_shape=tuple(out_shape), grid=(s // ts,),
        in_specs=in_specs, out_specs=tuple(out_specs), compiler_params=_params("arbitrary"),
    )(*args)


def _conv_tiles(s, e):
    return _tile(s, 512), _tile(e, 512)


def _conv_fwd(proj, conv_w, conv_b, name):
    _, s, e = proj.shape
    ts, te = _conv_tiles(s, e)
    hb = ts // SUBLANE

    def body(bg_ref, cg_ref, xi_ref, z_ref, cgp_ref, xip_ref, w0_ref, w1_ref, w2_ref, b_ref, y_ref):
        cx = cg_ref[...] * xi_ref[...]
        before = jnp.where(pl.program_id(0) > 0, cgp_ref[...] * xip_ref[...], 0.0)
        conv = b_ref[...] + w2_ref[...] * cx
        conv = conv + w0_ref[...] * _shift_down(cx, before, 2)
        conv = conv + w1_ref[...] * _shift_down(cx, before, 1)
        z = z_ref[...]
        y_ref[...] = ((z * _sigmoid(z)) * bg_ref[...] * conv).astype(BF16)

    def part(q):
        return pl.BlockSpec((None, ts, te), lambda i, j: (q, i, j))

    def halo_before(q):
        return pl.BlockSpec((None, SUBLANE, te), lambda i, j: (q, jnp.maximum(i * hb - 1, 0), j))

    return pl.pallas_call(
        body, name=name, out_shape=jax.ShapeDtypeStruct((s, e), BF16), grid=(s // ts, e // te),
        in_specs=[part(0), part(1), part(2), part(3), halo_before(1), halo_before(2)]
        + [pl.BlockSpec((1, te), lambda i, j: (0, j))] * 4,
        out_specs=pl.BlockSpec((ts, te), lambda i, j: (i, j)),
        compiler_params=_params("parallel", "parallel"),
    )(proj, proj, proj, proj, proj, proj, *conv_w, conv_b)


def _conv_bwd(proj, dy, conv_w, conv_b, name):
    _, s, e = proj.shape
    ts, te = _conv_tiles(s, e)
    hb = ts // SUBLANE
    n_i = s // ts
    last_halo = s // SUBLANE - 1

    def body(bg_ref, cg_ref, xi_ref, z_ref, dy_ref, cgp_ref, xip_ref, bgn_ref, zn_ref, dyn_ref,
             w0_ref, w1_ref, w2_ref, b_ref, dp_ref, dw0_ref, dw1_ref, dw2_ref, db_ref):
        i = pl.program_id(1)

        @pl.when(i == 0)
        def _():
            for acc in (dw0_ref, dw1_ref, dw2_ref, db_ref):
                acc[...] = jnp.zeros_like(acc)
        bg, cg, xi, z, dy = bg_ref[...], cg_ref[...], xi_ref[...], z_ref[...], dy_ref[...]
        w0, w1, w2 = w0_ref[...], w1_ref[...], w2_ref[...]
        cx = cg * xi
        before = jnp.where(i > 0, cgp_ref[...] * xip_ref[...], 0.0)
        cx1 = _shift_down(cx, before, 1)
        cx2 = _shift_down(cx, before, 2)
        conv = b_ref[...] + w2 * cx
        conv = conv + w0 * cx2
        conv = conv + w1 * cx1
        sz, dsz = _silu_and_grad(z)
        dp_ref[3] = (dy * bg * conv * dsz).astype(BF16)
        dp_ref[0] = (dy * sz * conv).astype(BF16)
        dconv = dy * sz * bg
        zn = zn_ref[...]
        after = jnp.where(i < n_i - 1, dyn_ref[...] * (zn * _sigmoid(zn)) * bgn_ref[...], 0.0)
        db_ref[...] += _colsum(dconv)
        dw2_ref[...] += _colsum(dconv * cx)
        dw1_ref[...] += _colsum(dconv * cx1)
        dw0_ref[...] += _colsum(dconv * cx2)
        dcx = w2 * dconv + w1 * _shift_up(dconv, after, 1) + w0 * _shift_up(dconv, after, 2)
        dp_ref[1] = (dcx * xi).astype(BF16)
        dp_ref[2] = (dcx * cg).astype(BF16)

    def part(q):
        return pl.BlockSpec((None, ts, te), lambda j, i: (q, i, j))

    def halo_before(q):
        return pl.BlockSpec((None, SUBLANE, te), lambda j, i: (q, jnp.maximum(i * hb - 1, 0), j))

    def halo_after(q):
        return pl.BlockSpec((None, SUBLANE, te), lambda j, i: (q, jnp.minimum((i + 1) * hb, last_halo), j))

    return pl.pallas_call(
        body, name=name,
        out_shape=(jax.ShapeDtypeStruct((4, s, e), BF16),) + (jax.ShapeDtypeStruct((1, e), F32),) * 4,
        grid=(e // te, n_i),
        in_specs=[part(0), part(1), part(2), part(3), pl.BlockSpec((ts, te), lambda j, i: (i, j)),
                  halo_before(1), halo_before(2), halo_after(0), halo_after(3),
                  pl.BlockSpec((SUBLANE, te), lambda j, i: (jnp.minimum((i + 1) * hb, last_halo), j))]
        + [pl.BlockSpec((1, te), lambda j, i: (0, j))] * 4,
        out_specs=(pl.BlockSpec((4, ts, te), lambda j, i: (0, i, j)),) + (pl.BlockSpec((1, te), lambda j, i: (0, j)),) * 4,
        compiler_params=_params("parallel", "arbitrary"),
    )(proj, proj, proj, proj, dy, proj, proj, proj, proj, dy, *conv_w, conv_b)


def _tril(w):
    row = lax.broadcasted_iota(jnp.int32, w.shape, 0)
    col = lax.broadcasted_iota(jnp.int32, w.shape, 1)
    return jnp.where(row >= col, w, 0.0)


def _triu(w):
    row = lax.broadcasted_iota(jnp.int32, w.shape, 0)
    col = lax.broadcasted_iota(jnp.int32, w.shape, 1)
    return jnp.where(row <= col, w, 0.0)


def _layer_norm_fwd(v, g, b):
    mu = jnp.mean(v, axis=-1, keepdims=True)
    vc = v - mu
    rstd = lax.rsqrt(jnp.mean(vc * vc, axis=-1, keepdims=True) + LN_EPS)
    vhat = vc * rstd
    return vhat * g + b, vhat, rstd


def _gmlp_fwd(proj, ln_g, ln_b, w_s, b_s_t, name):
    _, s, e = proj.shape
    gw = e // GROUPS

    def body(pu_ref, pv_ref, pz_ref, g_ref, b_ref, ws_ref, bs_ref, y_ref, vn_scr, mix_scr):
        vn, _, _ = _layer_norm_fwd(_gelu(pv_ref[...]), g_ref[...], b_ref[...])
        vn_scr[...] = vn.astype(BF16)
        for gi in range(GROUPS):
            cols = slice(gi * gw, (gi + 1) * gw)
            wt = _tril(ws_ref[gi]).astype(BF16)
            mix_scr[:, cols] = jnp.dot(wt, vn_scr[:, cols], preferred_element_type=F32) + bs_ref[:, gi:gi + 1]
        z = pz_ref[...]
        y_ref[...] = ((z * _sigmoid(z)) * (_gelu(pu_ref[...]) * mix_scr[...])).astype(BF16)

    def part(q):
        return pl.BlockSpec((None, CHUNK, e), lambda i: (q, i, 0))

    return pl.pallas_call(
        body, name=name, out_shape=jax.ShapeDtypeStruct((s, e), BF16), grid=(s // CHUNK,),
        in_specs=[part(0), part(1), part(2), _vec_spec(e), _vec_spec(e),
                  pl.BlockSpec((GROUPS, CHUNK, CHUNK), lambda i: (0, 0, 0)),
                  pl.BlockSpec((CHUNK, GROUPS), lambda i: (0, 0))],
        out_specs=pl.BlockSpec((CHUNK, e), lambda i: (i, 0)),
        scratch_shapes=[pltpu.VMEM((CHUNK, e), BF16), pltpu.VMEM((CHUNK, e), F32)],
        compiler_params=_params("parallel"),
    )(proj, proj, proj, ln_g, ln_b, w_s, b_s_t)


def _gmlp_bwd(proj, dy, ln_g, ln_b, w_s, w_s_t, b_s_t, name):
    _, s, e = proj.shape
    gw = e // GROUPS
    n_i = s // CHUNK

    def body(pu_ref, pv_ref, pz_ref, dy_ref, g_ref, b_ref, ws_ref, wst_ref, bs_ref,
             dp_ref, dws_ref, dbs_ref, dlg_ref, dlb_ref, vn_scr, mix_scr, dm_scr, dvn_scr, dmacc_scr):
        i = pl.program_id(0)

        @pl.when(i == 0)
        def _():
            dws_ref[...] = jnp.zeros_like(dws_ref)
            dlg_ref[...] = jnp.zeros_like(dlg_ref)
            dlb_ref[...] = jnp.zeros_like(dlb_ref)
            dmacc_scr[...] = jnp.zeros_like(dmacc_scr)
        ln_g = g_ref[...]
        u, du_dpu = _gelu_and_grad(pu_ref[...])
        v, dv_dpv = _gelu_and_grad(pv_ref[...])
        vn, vhat, rstd = _layer_norm_fwd(v, ln_g, b_ref[...])
        vn_scr[...] = vn.astype(BF16)
        for gi in range(GROUPS):
            cols = slice(gi * gw, (gi + 1) * gw)
            wt = _tril(ws_ref[gi]).astype(BF16)
            mix_scr[:, cols] = jnp.dot(wt, vn_scr[:, cols], preferred_element_type=F32) + bs_ref[:, gi:gi + 1]
        mixed = mix_scr[...]
        dy = dy_ref[...]
        sz, dsz = _silu_and_grad(pz_ref[...])
        ds = dy * sz
        dp_ref[2] = (dy * (u * mixed) * dsz).astype(BF16)
        dp_ref[0] = (ds * mixed * du_dpu).astype(BF16)
        dm = ds * u
        dmacc_scr[...] += dm
        dm_scr[...] = dm.astype(BF16)
        for gi in range(GROUPS):
            cols = slice(gi * gw, (gi + 1) * gw)
            dmg = dm_scr[:, cols]
            dws_ref[gi] += lax.dot_general(dmg, vn_scr[:, cols], (((1,), (1,)), ((), ())), preferred_element_type=F32)
            wtt = _triu(wst_ref[gi]).astype(BF16)
            dvn_scr[:, cols] = jnp.dot(wtt, dmg, preferred_element_type=F32)
        dvn = dvn_scr[...]
        dlg_ref[...] += _colsum(dvn * vhat)
        dlb_ref[...] += _colsum(dvn)
        dvh = dvn * ln_g
        dv = rstd * (dvh - jnp.mean(dvh, axis=-1, keepdims=True) - vhat * jnp.mean(dvh * vhat, axis=-1, keepdims=True))
        dp_ref[1] = (dv * dv_dpv).astype(BF16)

        @pl.when(i == n_i - 1)
        def _():
            for gi in range(GROUPS):
                dws_ref[gi] = _tril(dws_ref[gi])
                dbs_ref[:, gi:gi + 1] = jnp.sum(dmacc_scr[:, gi * gw:(gi + 1) * gw], axis=1, keepdims=True)

    def part(q):
        return pl.BlockSpec((None, CHUNK, e), lambda i: (q, i, 0))

    w_spec = pl.BlockSpec((GROUPS, CHUNK, CHUNK), lambda i: (0, 0, 0))
    bs_spec = pl.BlockSpec((CHUNK, GROUPS), lambda i: (0, 0))
    return pl.pallas_call(
        body, name=name,
        out_shape=(jax.ShapeDtypeStruct((3, s, e), BF16), jax.ShapeDtypeStruct((GROUPS, CHUNK, CHUNK), F32),
                   jax.ShapeDtypeStruct((CHUNK, GROUPS), F32), jax.ShapeDtypeStruct((1, e), F32),
                   jax.ShapeDtypeStruct((1, e), F32)),
        grid=(n_i,),
        in_specs=[part(0), part(1), part(2), pl.BlockSpec((CHUNK, e), lambda i: (i, 0)), _vec_spec(e), _vec_spec(e),
                  w_spec, w_spec, bs_spec],
        out_specs=(pl.BlockSpec((3, CHUNK, e), lambda i: (0, i, 0)), w_spec, bs_spec, _vec_spec(e), _vec_spec(e)),
        scratch_shapes=[pltpu.VMEM((CHUNK, e), BF16), pltpu.VMEM((CHUNK, e), F32), pltpu.VMEM((CHUNK, e), BF16),
                        pltpu.VMEM((CHUNK, e), F32), pltpu.VMEM((CHUNK, e), F32)],
        compiler_params=_params("arbitrary"),
    )(proj, proj, proj, dy, ln_g, ln_b, w_s, w_s_t, b_s_t)


def _mod_fwd(c_all, mod_w, mod_b_cols, name):
    n_layer, d, w = mod_w.shape

    def body(c_ref, w_ref, b_ref, ca_ref, o_ref):
        c = c_ref[...]
        ca = c * _sigmoid(c)
        ca_ref[...] = ca
        for li in range(n_layer):
            o_ref[li * N_DEV:(li + 1) * N_DEV, :] = (
                jnp.dot(ca, w_ref[li], preferred_element_type=F32, precision=lax.Precision.HIGHEST) + b_ref[li])

    return pl.pallas_call(
        body, name=name,
        out_shape=(jax.ShapeDtypeStruct((N_DEV, d), F32), jax.ShapeDtypeStruct((n_layer * N_DEV, w), F32)),
        in_specs=[VMEM_SPEC] * 3, out_specs=(VMEM_SPEC, VMEM_SPEC),
        compiler_params=pltpu.CompilerParams(vmem_limit_bytes=VMEM_LIMIT),
    )(c_all, mod_w, mod_b_cols)


def _adamw(w, g, m, v):
    m = ADAM_B1 * m + (1.0 - ADAM_B1) * g
    v = ADAM_B2 * v + (1.0 - ADAM_B2) * (g * g)
    m_hat = m / (1.0 - ADAM_B1 ** ADAM_STEP)
    v_hat = v / (1.0 - ADAM_B2 ** ADAM_STEP)
    delta = -ADAM_LR * (m_hat / (jnp.sqrt(v_hat) + ADAM_EPS) + ADAM_WD * w)
    return delta, m, v


def _adamw_2d(w, g, m, v, name):
    r, c = w.shape
    tr, tc = _tile(r, 512), _tile(c, 1024)

    def body(w_ref, g_ref, m_ref, v_ref, d_ref, nm_ref, nv_ref):
        d_ref[...], nm_ref[...], nv_ref[...] = _adamw(w_ref[...], g_ref[...], m_ref[...], v_ref[...])

    spec = pl.BlockSpec((tr, tc), lambda i, j: (i, j))
    shape = jax.ShapeDtypeStruct((r, c), F32)
    return pl.pallas_call(
        body, name=name, out_shape=(shape, shape, shape), grid=(r // tr, c // tc),
        in_specs=[spec] * 4, out_specs=(spec, spec, spec), compiler_params=_params("parallel", "parallel"),
    )(w, g, m, v)


def _mod_w_update(ca_t, dmod_cols, w, m, v, name):
    n_layer, d, wd = w.shape
    tr = _tile(d, 256)

    def body(ca_ref, dm_ref, w_ref, m_ref, v_ref, g_ref, d_ref, nm_ref, nv_ref):
        ca = ca_ref[...]
        dm = dm_ref[...]
        g = ca[:, 0:1] * dm[0:1, :]
        for b in range(1, N_DEV):
            g = g + ca[:, b:b + 1] * dm[b:b + 1, :]
        g_ref[...] = g
        d_ref[...], nm_ref[...], nv_ref[...] = _adamw(w_ref[...], g, m_ref[...], v_ref[...])

    spec = pl.BlockSpec((None, tr, wd), lambda l, i: (l, i, 0))
    shape = jax.ShapeDtypeStruct((n_layer, d, wd), F32)
    return pl.pallas_call(
        body, name=name, out_shape=(shape,) * 4, grid=(n_layer, d // tr),
        in_specs=[pl.BlockSpec((tr, N_DEV), lambda l, i: (i, 0)), pl.BlockSpec((None, N_DEV, wd), lambda l, i: (l, 0, 0)),
                  spec, spec, spec],
        out_specs=(spec,) * 4, compiler_params=_params("parallel", "parallel"),
    )(ca_t, dmod_cols, w, m, v)


def _adamw_small(ws, gs, ms, vs, name):
    n = len(ws)

    def body(*refs):
        ins, outs = refs[:4 * n], refs[4 * n:]
        for k in range(n):
            delta, nm, nv = _adamw(ins[k][...], ins[n + k][...], ins[2 * n + k][...], ins[3 * n + k][...])
            outs[3 * k][...] = delta
            outs[3 * k + 1][...] = nm
            outs[3 * k + 2][...] = nv

    out_shape = []
    for w in ws:
        out_shape += [jax.ShapeDtypeStruct(w.shape, F32)] * 3
    outs = pl.pallas_call(
        body, name=name, out_shape=tuple(out_shape),
        in_specs=[VMEM_SPEC] * (4 * n), out_specs=tuple([VMEM_SPEC] * (3 * n)),
        compiler_params=pltpu.CompilerParams(vmem_limit_bytes=VMEM_LIMIT),
    )(*ws, *gs, *ms, *vs)
    return [tuple(outs[3 * k:3 * k + 3]) for k in range(n)]


def _place():
    return lax.axis_index("x"), lax.axis_index("y"), lax.axis_index("c")


def _other_chips(x, y):
    return [(1 - x, y), (x, 1 - y), (1 - x, 1 - y)]


def _all_gather(block, name):
    m_per, n = block.shape

    def body(x_ref, out_ref, send_sems, recv_sems, local_sem):
        x, y, c = _place()
        me, sibling = (x, y, c), (x, y, 1 - c)
        chips = _other_chips(x, y)

        def rows(px, py, pc):
            return out_ref.at[pl.ds((4 * px + 2 * py + pc) * m_per, m_per), :]

        def copy(k, blk, to, src=None):
            return pltpu.make_async_remote_copy(
                src_ref=rows(*blk) if src is None else src, dst_ref=rows(*blk),
                send_sem=send_sems.at[k], recv_sem=recv_sems.at[k], device_id=to, device_id_type=MESH)

        mine = pltpu.make_async_copy(x_ref, rows(*me), local_sem)
        mine.start()
        first = [copy(0, me, sibling, src=x_ref)]
        first += [copy(1 + j, me, (*chip, c), src=x_ref) for j, chip in enumerate(chips)]
        for cp in first:
            cp.start()
        passed = [copy(4 + j, (*chip, c), sibling) for j, chip in enumerate(chips)]
        for j, chip in enumerate(chips):
            copy(1 + j, (*chip, c), me).wait_recv()
            passed[j].start()
        copy(0, sibling, me).wait_recv()
        for j, chip in enumerate(chips):
            copy(4 + j, (*chip, 1 - c), me).wait_recv()
        for cp in first + passed:
            cp.wait_send()
        mine.wait()

    return pl.pallas_call(
        body, name=name, out_shape=jax.ShapeDtypeStruct((N_DEV * m_per, n), F32),
        in_specs=[VMEM_SPEC], out_specs=VMEM_SPEC,
        scratch_shapes=[pltpu.SemaphoreType.DMA((7,)), pltpu.SemaphoreType.DMA((7,)), pltpu.SemaphoreType.DMA],
        compiler_params=pltpu.CompilerParams(vmem_limit_bytes=VMEM_LIMIT),
    )(block)


def _gather_weights(shards, kinds, name):
    n = len(shards)
    out_shape = []
    for sh, kind in zip(shards, kinds):
        r, cdim = sh.shape
        out_shape.append(jax.ShapeDtypeStruct((r, N_CHIP * cdim) if kind == "col" else (N_CHIP * r, cdim), sh.dtype))

    def body(*refs):
        ins, outs = refs[:n], refs[n:2 * n]
        send_sems, recv_sems, fwd_send_sems, fwd_recv_sems, local_sems = refs[2 * n:]
        x, y, c = _place()
        sibling = (x, y, 1 - c)
        chips = _other_chips(x, y)

        def window(w, chip, half):
            r, cdim = shards[w].shape
            hr = r // 2
            k = 2 * chip[0] + chip[1]
            if kinds[w] == "col":
                return outs[w].at[pl.ds(half * hr, hr), pl.ds(pl.multiple_of(k * cdim, 128), cdim)]
            return outs[w].at[pl.ds(pl.multiple_of(k * r + half * hr, SUBLANE), hr), :]

        def own_half(w, half):
            hr = shards[w].shape[0] // 2
            return ins[w].at[pl.ds(half * hr, hr), :]

        local, sends, fwds = [], [], []
        for w in range(n):
            for half in range(2):
                cp = pltpu.make_async_copy(own_half(w, half), window(w, (x, y), half), local_sems.at[2 * w + half])
                cp.start()
                local.append(cp)
            for j, chip in enumerate(chips):
                cp = pltpu.make_async_remote_copy(
                    src_ref=own_half(w, c), dst_ref=window(w, (x, y), c),
                    send_sem=send_sems.at[3 * w + j], recv_sem=recv_sems.at[3 * w + j],
                    device_id=(*chip, c), device_id_type=MESH)
                cp.start()
                sends.append(cp)
        for w in range(n):
            for j, chip in enumerate(chips):
                pltpu.make_async_remote_copy(
                    src_ref=own_half(w, c), dst_ref=window(w, chip, c),
                    send_sem=send_sems.at[3 * w + j], recv_sem=recv_sems.at[3 * w + j],
                    device_id=(*chip, c), device_id_type=MESH).wait_recv()
                cp = pltpu.make_async_remote_copy(
                    src_ref=window(w, chip, c), dst_ref=window(w, chip, c),
                    send_sem=fwd_send_sems.at[3 * w + j], recv_sem=fwd_recv_sems.at[3 * w + j],
                    device_id=sibling, device_id_type=MESH)
                cp.start()
                fwds.append(cp)
        for w in range(n):
            for j, chip in enumerate(chips):
                pltpu.make_async_remote_copy(
                    src_ref=window(w, chip, 1 - c), dst_ref=window(w, chip, 1 - c),
                    send_sem=fwd_send_sems.at[3 * w + j], recv_sem=fwd_recv_sems.at[3 * w + j],
                    device_id=sibling, device_id_type=MESH).wait_recv()
        for cp in sends + fwds:
            cp.wait_send()
        for cp in local:
            cp.wait()

    return pl.pallas_call(
        body, name=name, out_shape=tuple(out_shape),
        in_specs=[HBM_SPEC] * n, out_specs=tuple([HBM_SPEC] * n),
        scratch_shapes=[pltpu.SemaphoreType.DMA((3 * n,))] * 4 + [pltpu.SemaphoreType.DMA((2 * n,))],
    )(*shards)


def _exchange_halves(grads, name):
    n = len(grads)
    out_shape = tuple(jax.ShapeDtypeStruct((g.shape[0], g.shape[1] // 2, g.shape[2]), g.dtype) for g in grads)

    def body(*refs):
        ins, outs = refs[:n], refs[n:2 * n]
        send_sems, recv_sems = refs[2 * n:]
        x, y, c = _place()
        copies = []
        for w in range(n):
            h = grads[w].shape[1] // 2
            cp = pltpu.make_async_remote_copy(
                src_ref=ins[w].at[:, pl.ds((1 - c) * h, h), :], dst_ref=outs[w],
                send_sem=send_sems.at[w], recv_sem=recv_sems.at[w], device_id=(x, y, 1 - c), device_id_type=MESH)
            cp.start()
            copies.append(cp)
        for cp in copies:
            cp.wait()

    return pl.pallas_call(
        body, name=name, out_shape=out_shape, in_specs=[HBM_SPEC] * n, out_specs=tuple([HBM_SPEC] * n),
        scratch_shapes=[pltpu.SemaphoreType.DMA((n,)), pltpu.SemaphoreType.DMA((n,))],
    )(*grads)


def _add_own_half(g, r, c_idx, name):
    a, h, n = r.shape
    tr, tc = _tile(h, 256), _tile(n, 2048)
    nb = h // tr

    def body(c_ref, g_ref, r_ref, o_ref):
        o_ref[...] = g_ref[...] + r_ref[...]

    return pl.pallas_call(
        body, name=name, out_shape=jax.ShapeDtypeStruct((a, h, n), F32),
        grid_spec=pltpu.PrefetchScalarGridSpec(
            num_scalar_prefetch=1, grid=(a, nb, n // tc),
            in_specs=[pl.BlockSpec((None, tr, tc), lambda ai, i, j, cr: (ai, cr[0] * nb + i, j)),
                      pl.BlockSpec((None, tr, tc), lambda ai, i, j, cr: (ai, i, j))],
            out_specs=pl.BlockSpec((None, tr, tc), lambda ai, i, j, cr: (ai, i, j))),
        compiler_params=_params("parallel", "parallel", "parallel"),
    )(c_idx, g, r)


def _exchange_chip_partials(parts, kinds, name):
    n = len(parts)
    out_shape = []
    for p, kind in zip(parts, kinds):
        _, h, width = p.shape
        out_shape.append(jax.ShapeDtypeStruct((3, h, width // N_CHIP if kind == "col" else width), p.dtype))

    def body(*refs):
        ins, outs = refs[:n], refs[n:2 * n]
        send_sems, recv_sems = refs[2 * n:]
        x, y, c = _place()
        copies = []
        for w in range(n):
            for j, chip in enumerate(_other_chips(x, y)):
                k = 2 * chip[0] + chip[1]
                if kinds[w] == "col":
                    cdim = parts[w].shape[2] // N_CHIP
                    src = ins[w].at[0, :, pl.ds(pl.multiple_of(k * cdim, 128), cdim)]
                else:
                    src = ins[w].at[k]
                cp = pltpu.make_async_remote_copy(
                    src_ref=src, dst_ref=outs[w].at[j],
                    send_sem=send_sems.at[3 * w + j], recv_sem=recv_sems.at[3 * w + j],
                    device_id=(*chip, c), device_id_type=MESH)
                cp.start()
                copies.append(cp)
        for cp in copies:
            cp.wait()

    return pl.pallas_call(
        body, name=name, out_shape=tuple(out_shape), in_specs=[HBM_SPEC] * n, out_specs=tuple([HBM_SPEC] * n),
        scratch_shapes=[pltpu.SemaphoreType.DMA((3 * n,)), pltpu.SemaphoreType.DMA((3 * n,))],
    )(*parts)


def _add_chip_pieces(part, recv, kind, k_idx, name):
    _, h, width = recv.shape
    tr, tc = _tile(h, 256), _tile(width, 2048)
    ncb = width // tc

    def body(k_ref, p_ref, r_ref, o_ref):
        o_ref[...] = ((p_ref[...] + r_ref[0]) + r_ref[1]) + r_ref[2]

    if kind == "col":
        p_spec = pl.BlockSpec((None, tr, tc), lambda i, j, kr: (0, i, kr[0] * ncb + j))
    else:
        p_spec = pl.BlockSpec((None, tr, tc), lambda i, j, kr: (kr[0], i, j))
    return pl.pallas_call(
        body, name=name, out_shape=jax.ShapeDtypeStruct((h, width), F32),
        grid_spec=pltpu.PrefetchScalarGridSpec(
            num_scalar_prefetch=1, grid=(h // tr, ncb),
            in_specs=[p_spec, pl.BlockSpec((3, tr, tc), lambda i, j, kr: (0, i, j))],
            out_specs=pl.BlockSpec((tr, tc), lambda i, j, kr: (i, j))),
        compiler_params=_params("parallel", "parallel"),
    )(k_idx, part, recv)


def _join_halves(halves, name):
    n = len(halves)
    out_shape = tuple(jax.ShapeDtypeStruct((2 * hf.shape[0], hf.shape[1]), hf.dtype) for hf in halves)

    def body(*refs):
        ins, outs = refs[:n], refs[n:2 * n]
        send_sems, recv_sems, local_sems = refs[2 * n:]
        x, y, c = _place()
        copies, local = [], []
        for w in range(n):
            h = halves[w].shape[0]
            mine = outs[w].at[pl.ds(c * h, h), :]
            lc = pltpu.make_async_copy(ins[w], mine, local_sems.at[w])
            lc.start()
            local.append(lc)
            cp = pltpu.make_async_remote_copy(
                src_ref=ins[w], dst_ref=mine, send_sem=send_sems.at[w], recv_sem=recv_sems.at[w],
                device_id=(x, y, 1 - c), device_id_type=MESH)
            cp.start()
            copies.append(cp)
        for w in range(n):
            h = halves[w].shape[0]
            theirs = outs[w].at[pl.ds((1 - c) * h, h), :]
            pltpu.make_async_remote_copy(
                src_ref=ins[w], dst_ref=theirs, send_sem=send_sems.at[w], recv_sem=recv_sems.at[w],
                device_id=(x, y, 1 - c), device_id_type=MESH).wait_recv()
        for cp in copies:
            cp.wait_send()
        for lc in local:
            lc.wait()

    return pl.pallas_call(
        body, name=name, out_shape=out_shape, in_specs=[HBM_SPEC] * n, out_specs=tuple([HBM_SPEC] * n),
        scratch_shapes=[pltpu.SemaphoreType.DMA((n,))] * 3,
    )(*halves)


def _sum_devices(gathered, rows, name):
    n = gathered.shape[1]

    def body(g_ref, o_ref):
        acc = g_ref[0:rows, :]
        for dev in range(1, N_DEV):
            acc = acc + g_ref[dev * rows:(dev + 1) * rows, :]
        o_ref[...] = acc

    return pl.pallas_call(
        body, name=name, out_shape=jax.ShapeDtypeStruct((rows, n), F32),
        in_specs=[VMEM_SPEC], out_specs=VMEM_SPEC,
        compiler_params=pltpu.CompilerParams(vmem_limit_bytes=VMEM_LIMIT),
    )(gathered)


def _pack(vectors, width):
    flat = [v.reshape(-1) for v in vectors]
    offsets, total = [], 0
    for f in flat:
        offsets.append(total)
        total += f.shape[0]
    rows = -(-total // (width * SUBLANE)) * SUBLANE
    flat.append(jnp.zeros((rows * width - total,), F32))
    return jnp.concatenate(flat).reshape(rows, width), offsets


def kernel(x, c, mod_w, mod_b, norm_g, a_w_in, a_conv_w, a_conv_b, a_w_out, b_w_in, b_ln_g, b_ln_b, b_w_s, b_b_s, b_w_out, final_g, loss_target, m_mod_w, m_mod_b, m_norm_g, m_a_w_in, m_a_conv_w, m_a_conv_b, m_a_w_out, m_b_w_in, m_b_ln_g, m_b_ln_b, m_b_w_s, m_b_b_s, m_b_w_out, m_final_g, v_mod_w, v_mod_b, v_norm_g, v_a_w_in, v_a_conv_w, v_a_conv_b, v_a_w_out, v_b_w_in, v_b_ln_g, v_b_ln_b, v_b_w_s, v_b_b_s, v_b_w_out, v_final_g):
    seq, d = x.shape[1], x.shape[2]
    e = a_conv_b.shape[1]
    wd = mod_w.shape[2]
    ax, ay, ac = _place()
    chip = 2 * ax + ay
    dev = 2 * chip + ac
    chip_idx = jnp.reshape(chip, (1,)).astype(jnp.int32)
    core_idx = jnp.reshape(ac, (1,)).astype(jnp.int32)

    x2d = x[0]
    target = loss_target[0]

    es = e // N_CHIP
    packed0, offs0 = _pack([c, a_conv_w, b_ln_g, b_ln_b], 1024)
    gathered0 = _all_gather(packed0, "gather_params").reshape(N_DEV, -1)
    c_all = gathered0[:, :d]
    per_chip = gathered0[0::2]

    def from_chips_cols(k, rows_):
        got = per_chip[:, offs0[k]:offs0[k] + rows_ * es].reshape(N_CHIP, rows_, es)
        return jnp.transpose(got, (1, 0, 2)).reshape(rows_, e)

    conv_w_full = from_chips_cols(1, 3)
    conv_w = [conv_w_full[k:k + 1] for k in range(3)]
    ln_g, ln_b = from_chips_cols(2, 1), from_chips_cols(3, 1)
    mod_b_cols = lax.dynamic_slice_in_dim(mod_b, chip * wd, wd, axis=1)[:, None, :]
    c_act, mod_part = _mod_fwd(c_all, mod_w, mod_b_cols, "mod_fwd")
    n_layer = mod_w.shape[0]
    mod_all = _all_gather(mod_part, "gather_mod").reshape(N_CHIP, 2, n_layer, N_DEV, wd)[:, 0]
    mod_all = jnp.transpose(mod_all, (1, 2, 0, 3)).reshape(n_layer, N_DEV, N_CHIP * wd)
    mod_me = lax.dynamic_index_in_dim(mod_all, dev, axis=1, keepdims=False)
    shift = [mod_me[l:l + 1, 0:d] for l in range(n_layer)]
    scale = [mod_me[l:l + 1, d:2 * d] for l in range(n_layer)]
    gate = [mod_me[l:l + 1, 2 * d:3 * d] for l in range(n_layer)]

    wa_in, wb_in, wa_out, wb_out = _gather_weights(
        [a_w_in[0].astype(BF16), b_w_in[0].astype(BF16), a_w_out[0].astype(BF16), b_w_out[0].astype(BF16)],
        ["col", "col", "row", "row"], "gather_weights")

    g0, g1, gf = norm_g[0:1], norm_g[1:2], final_g[None, :]
    h0 = _norm_mod(x2d, g0, scale[0], shift[0], "norm_mod0")
    proj0 = _mm_proj(h0, wa_in, 4, "a_proj")
    y0 = _conv_fwd(proj0, conv_w, a_conv_b, "conv_fwd")
    br0 = _mm_nn(y0, wa_out, "a_out")
    x1, h1 = _residual_norm_mod(x2d, br0, gate[0], g1, scale[1], shift[1], "residual_norm_mod1")
    proj1 = _mm_proj(h1, wb_in, 3, "b_proj")
    b_s_t = jnp.transpose(b_b_s[0])
    y1 = _gmlp_fwd(proj1, ln_g, ln_b, b_w_s[0], b_s_t, "gmlp_fwd")
    br1 = _mm_nn(y1, wb_out, "b_out")
    dx2, dbr1, loss_part, g_final_g, dgate1 = _head(x1, br1, gate[1], gf, target, "head")

    dy1 = _mm_nt(dbr1, wb_out, "b_out_dx")
    gw_b_out = _mm_dw_out(y1, dbr1, "b_out_dw")
    dproj1, g_w_s, g_b_s_t, g_ln_g, g_ln_b = _gmlp_bwd(
        proj1, dy1, ln_g, ln_b, b_w_s[0], jnp.swapaxes(b_w_s[0], 1, 2), b_s_t, "gmlp_bwd")
    dh1 = _mm_dh(dproj1, wb_in, "b_proj_dx")
    gw_b_in = _mm_dw_in(h1, dproj1, "b_proj_dw")
    dx1, dshift1, dscale1, g_g1, dbr0, dgate0 = _norm_mod_bwd(
        dh1, x1, dx2, g1, scale[1], "norm_mod1_bwd", br=br0, gate=gate[0])

    dy0 = _mm_nt(dbr0, wa_out, "a_out_dx")
    gw_a_out = _mm_dw_out(y0, dbr0, "a_out_dw")
    dproj0, g_w0, g_w1, g_w2, g_conv_b = _conv_bwd(proj0, dy0, conv_w, a_conv_b, "conv_bwd")
    dh0 = _mm_dh(dproj0, wa_in, "a_proj_dx")
    gw_a_in = _mm_dw_in(h0, dproj0, "a_proj_dw")
    grad_x, dshift0, dscale0, g_g0 = _norm_mod_bwd(dh0, x2d, dx1, g0, scale[0], "norm_mod0_bwd")

    dmod = jnp.concatenate([dshift0, dscale0, dgate0, dshift1, dscale1, dgate1], axis=1)
    small = [loss_part[0, 0:1], g_final_g, g_g0, g_g1, jnp.concatenate([g_w0, g_w1, g_w2], axis=0), g_conv_b, g_ln_g, g_ln_b,
             jnp.transpose(g_b_s_t), g_w_s, dmod]
    packed, offs = _pack(small, 1024)
    rows = packed.shape[0]
    gathered = _all_gather(packed, "gather_small")
    summed = _sum_devices(gathered, rows, "sum_small").reshape(-1)

    def take(k, shape):
        size = math.prod(shape)
        return summed[offs[k]:offs[k] + size].reshape(shape)

    loss = take(0, ())
    grad_final_g = take(1, (d,))
    grad_norm_g = jnp.concatenate([take(2, (1, d)), take(3, (1, d))], axis=0)
    grad_conv_w_full = take(4, (3, e))
    grad_a_conv_b = take(5, (1, e))
    grad_ln_g_full = take(6, (1, e))
    grad_ln_b_full = take(7, (1, e))
    grad_b_b_s = take(8, (1, GROUPS, CHUNK))
    grad_b_w_s = take(9, (1, GROUPS, CHUNK, CHUNK))
    grad_mod_b = take(10, (n_layer, 3 * d))
    grad_a_conv_w = lax.dynamic_slice_in_dim(grad_conv_w_full, chip * es, es, axis=1)[None]
    grad_b_ln_g = lax.dynamic_slice_in_dim(grad_ln_g_full, chip * es, es, axis=1)
    grad_b_ln_b = lax.dynamic_slice_in_dim(grad_ln_b_full, chip * es, es, axis=1)
    dmod_all = gathered.reshape(N_DEV, rows * 1024)[:, offs[10]:offs[10] + n_layer * 3 * d]
    dmod_all = dmod_all.reshape(N_DEV, n_layer, 3 * d)
    dmod_cols = jnp.transpose(lax.dynamic_slice_in_dim(dmod_all, chip * wd, wd, axis=2), (1, 0, 2))

    big = [gw_a_in[None], gw_b_in[None], gw_a_out.reshape(N_CHIP, e // N_CHIP, d), gw_b_out.reshape(N_CHIP, e // N_CHIP, d)]
    kinds = ["col", "col", "row", "row"]
    names = ["a_w_in", "b_w_in", "a_w_out", "b_w_out"]
    from_sibling = _exchange_halves(big, "reduce_cores")
    parts = [_add_own_half(g, r, core_idx, "add_cores_" + nm) for g, r, nm in zip(big, from_sibling, names)]
    from_chips = _exchange_chip_partials(parts, kinds, "reduce_chips")
    halves = [_add_chip_pieces(p, r, kind, chip_idx, "add_chips_" + nm)
              for p, r, kind, nm in zip(parts, from_chips, kinds, names)]
    g_a_w_in, g_b_w_in, g_a_w_out, g_b_w_out = _join_halves(halves, "join_halves")

    grad_mod_w, delta_mod_w, new_m_mod_w, new_v_mod_w = _mod_w_update(
        jnp.transpose(c_act), dmod_cols, mod_w, m_mod_w, v_mod_w, "mod_w_update")
    upd = {}
    for nm, w, g, m, v in [("a_w_in", a_w_in, g_a_w_in, m_a_w_in, v_a_w_in), ("a_w_out", a_w_out, g_a_w_out, m_a_w_out, v_a_w_out),
                           ("b_w_in", b_w_in, g_b_w_in, m_b_w_in, v_b_w_in), ("b_w_out", b_w_out, g_b_w_out, m_b_w_out, v_b_w_out)]:
        upd[nm] = tuple(o[None] for o in _adamw_2d(w[0], g, m[0], v[0], "adamw_" + nm))
    small_w = [("mod_b", mod_b, grad_mod_b, m_mod_b, v_mod_b), ("norm_g", norm_g, grad_norm_g, m_norm_g, v_norm_g),
               ("a_conv_w", a_conv_w, grad_a_conv_w, m_a_conv_w, v_a_conv_w),
               ("a_conv_b", a_conv_b, grad_a_conv_b, m_a_conv_b, v_a_conv_b),
               ("b_ln_g", b_ln_g, grad_b_ln_g, m_b_ln_g, v_b_ln_g), ("b_ln_b", b_ln_b, grad_b_ln_b, m_b_ln_b, v_b_ln_b),
               ("b_w_s", b_w_s, grad_b_w_s, m_b_w_s, v_b_w_s), ("b_b_s", b_b_s, grad_b_b_s, m_b_b_s, v_b_b_s),
               ("final_g", final_g, grad_final_g, m_final_g, v_final_g)]

    def flat2d(a):
        return a.reshape(-1, a.shape[-1])

    res = _adamw_small([flat2d(t[1]) for t in small_w], [flat2d(t[2]) for t in small_w],
                       [flat2d(t[3]) for t in small_w], [flat2d(t[4]) for t in small_w], "adamw_small")
    for (nm, w, _, _, _), r3 in zip(small_w, res):
        upd[nm] = tuple(o.reshape(w.shape) for o in r3)
    upd["mod_w"] = (delta_mod_w, new_m_mod_w, new_v_mod_w)

    grads = {"mod_w": grad_mod_w, "mod_b": grad_mod_b, "norm_g": grad_norm_g, "a_w_in": g_a_w_in[None],
             "a_conv_w": grad_a_conv_w, "a_conv_b": grad_a_conv_b, "a_w_out": g_a_w_out[None], "b_w_in": g_b_w_in[None],
             "b_ln_g": grad_b_ln_g, "b_ln_b": grad_b_ln_b, "b_w_s": grad_b_w_s, "b_b_s": grad_b_b_s,
             "b_w_out": g_b_w_out[None], "final_g": grad_final_g}
    order = ["mod_w", "mod_b", "norm_g", "a_w_in", "a_conv_w", "a_conv_b", "a_w_out", "b_w_in", "b_ln_g", "b_ln_b",
             "b_w_s", "b_b_s", "b_w_out", "final_g"]
    return (loss, grad_x[None], *[grads[k] for k in order], *[upd[k][0] for k in order],
            *[upd[k][1] for k in order], *[upd[k][2] for k in order])
```

```python
import functools
import math

import jax
import jax.numpy as jnp
from jax import lax
from jax.experimental import pallas as pl
from jax.experimental.pallas import tpu as pltpu

F32 = jnp.float32
BF16 = jnp.bfloat16
MESH = pl.DeviceIdType.MESH

N_DEV = 8
N_CHIP = 4
SUBLANE = 8
RMS_EPS = 1e-6
LN_EPS = 1e-5
CHUNK = 128
GROUPS = 8
ADAM_LR = 0.001
ADAM_B1 = 0.9
ADAM_B2 = 0.999
ADAM_EPS = 1e-08
ADAM_WD = 0.01
ADAM_STEP = 10
VMEM_LIMIT = 56 << 20

HBM_SPEC = pl.BlockSpec(memory_space=pltpu.HBM)
VMEM_SPEC = pl.BlockSpec(memory_space=pltpu.VMEM)
SEM_SPEC = pl.BlockSpec(memory_space=pltpu.SEMAPHORE)
ANY_SPEC = pl.BlockSpec(memory_space=pl.ANY)
EFFECT = pltpu.SideEffectType.DATAFLOW_SIDE_EFFECTING


def _params(*sem):
    return pltpu.CompilerParams(dimension_semantics=sem, vmem_limit_bytes=VMEM_LIMIT)


def _tile(n, want):
    if n <= want:
        return n
    t = want
    while n % t:
        t -= 128
    return t


def _sigmoid(x):
    return 1.0 / (1.0 + jnp.exp(-x))


def _silu_and_grad(x):
    s = _sigmoid(x)
    return x * s, s * (1.0 + x * (1.0 - s))


def _gelu_and_grad(x):
    cdf = 0.5 * (1.0 + lax.erf(x * (1.0 / math.sqrt(2.0))))
    pdf = jnp.exp(-0.5 * x * x) * (1.0 / math.sqrt(2.0 * math.pi))
    return x * cdf, cdf + x * pdf


def _gelu(x):
    return x * (0.5 * (1.0 + lax.erf(x * (1.0 / math.sqrt(2.0)))))


def _rms(x):
    r = lax.rsqrt(jnp.mean(x * x, axis=-1, keepdims=True) + RMS_EPS)
    return x * r, r


def _rms_bwd(dxn, xn, r):
    return r * (dxn - xn * jnp.mean(dxn * xn, axis=-1, keepdims=True))


def _colsum(a):
    return jnp.sum(a, axis=0, keepdims=True)


def _shift_down(cur, before, k):
    rolled = pltpu.roll(cur, k, 0)
    row = lax.broadcasted_iota(jnp.int32, before.shape, 0)
    head = jnp.where(row < k, pltpu.roll(before, k, 0), rolled[:SUBLANE])
    return jnp.concatenate([head, rolled[SUBLANE:]], axis=0)


def _shift_up(cur, after, k):
    n = cur.shape[0]
    rolled = pltpu.roll(cur, n - k, 0)
    row = lax.broadcasted_iota(jnp.int32, after.shape, 0)
    tail = jnp.where(row >= SUBLANE - k, pltpu.roll(after, SUBLANE - k, 0), rolled[n - SUBLANE:])
    return jnp.concatenate([rolled[:n - SUBLANE], tail], axis=0)


def _after_spec():
    return pl.BlockSpec((SUBLANE, 128), lambda *_: (0, 0))


def _mm_proj(h, w, n_split, name, after):
    s, d = h.shape
    e = w.shape[1] // n_split
    tm, tn = _tile(s, 1024), _tile(e, 1024)
    nj = e // tn

    def body(h_ref, w_ref, after_ref, o_ref):
        o_ref[...] = jnp.dot(h_ref[...], w_ref[...], preferred_element_type=F32)

    return pl.pallas_call(
        body, name=name,
        out_shape=jax.ShapeDtypeStruct((n_split, s, e), F32),
        grid=(s // tm, n_split * nj),
        in_specs=[pl.BlockSpec((tm, d), lambda i, j: (i, 0)), pl.BlockSpec((d, tn), lambda i, j: (0, j)),
                  _after_spec()],
        out_specs=pl.BlockSpec((None, tm, tn), lambda i, j: (j // nj, i, j % nj)),
        compiler_params=_params("parallel", "parallel"),
    )(h, w, after)


def _mm_nn(a, b, name):
    m, k = a.shape
    n = b.shape[1]
    tm, tn = _tile(m, 1024), _tile(n, 1024)

    def body(a_ref, b_ref, o_ref):
        o_ref[...] = jnp.dot(a_ref[...], b_ref[...], preferred_element_type=F32)

    return pl.pallas_call(
        body, name=name,
        out_shape=jax.ShapeDtypeStruct((m, n), F32),
        grid=(m // tm, n // tn),
        in_specs=[pl.BlockSpec((tm, k), lambda i, j: (i, 0)), pl.BlockSpec((k, tn), lambda i, j: (0, j))],
        out_specs=pl.BlockSpec((tm, tn), lambda i, j: (i, j)),
        compiler_params=_params("parallel", "parallel"),
    )(a, b)


def _mm_nt(a, b, name):
    m, k = a.shape
    n = b.shape[0]
    tm, tn = _tile(m, 1024), _tile(n, 1024)

    def body(a_ref, b_ref, o_ref):
        o_ref[...] = lax.dot_general(a_ref[...], b_ref[...], (((1,), (1,)), ((), ())), preferred_element_type=F32)

    return pl.pallas_call(
        body, name=name,
        out_shape=jax.ShapeDtypeStruct((m, n), F32),
        grid=(m // tm, n // tn),
        in_specs=[pl.BlockSpec((tm, k), lambda i, j: (i, 0)), pl.BlockSpec((tn, k), lambda i, j: (j, 0))],
        out_specs=pl.BlockSpec((tm, tn), lambda i, j: (i, j)),
        compiler_params=_params("parallel", "parallel"),
    )(a, b)


def _mm_dh(dp, w, name, after):
    nq, s, e = dp.shape
    d = w.shape[0]
    tm, tn, tk = _tile(s, 1024), _tile(d, 1024), _tile(e, 1024)
    nkq = e // tk
    nk = nq * nkq

    def body(a_ref, b_ref, after_ref, o_ref):
        @pl.when(pl.program_id(2) == 0)
        def _():
            o_ref[...] = jnp.zeros_like(o_ref)
        o_ref[...] += lax.dot_general(a_ref[...], b_ref[...], (((1,), (1,)), ((), ())), preferred_element_type=F32)

    return pl.pallas_call(
        body, name=name,
        out_shape=jax.ShapeDtypeStruct((s, d), F32),
        grid=(s // tm, d // tn, nk),
        in_specs=[pl.BlockSpec((None, tm, tk), lambda i, j, k: (k // nkq, i, k % nkq)),
                  pl.BlockSpec((tn, tk), lambda i, j, k: (j, k)), _after_spec()],
        out_specs=pl.BlockSpec((tm, tn), lambda i, j, k: (i, j)),
        compiler_params=_params("parallel", "parallel", "arbitrary"),
    )(dp, w, after)


def _dw_body(n_t):
    def body(a_ref, b_ref, o_ref, acc_ref):
        t = pl.program_id(2)

        @pl.when(t == 0)
        def _():
            acc_ref[...] = jnp.zeros_like(acc_ref)
        acc_ref[...] += lax.dot_general(a_ref[...], b_ref[...], (((0,), (0,)), ((), ())), preferred_element_type=F32)

        @pl.when(t == n_t - 1)
        def _():
            o_ref[...] = acc_ref[...].astype(o_ref.dtype)
    return body


def _mm_dw_in(h, dp, name):
    s, d = h.shape
    nq, _, e = dp.shape
    tm, tn, tt = _tile(d, 1024), _tile(e, 1024), _tile(s, 512)
    nj = e // tn

    return pl.pallas_call(
        _dw_body(s // tt), name=name,
        out_shape=jax.ShapeDtypeStruct((d, nq * e), BF16),
        grid=(d // tm, nq * nj, s // tt),
        in_specs=[pl.BlockSpec((tt, tm), lambda i, j, t: (t, i)),
                  pl.BlockSpec((None, tt, tn), lambda i, j, t: (j // nj, t, j % nj))],
        out_specs=pl.BlockSpec((tm, tn), lambda i, j, t: (i, j)),
        scratch_shapes=[pltpu.VMEM((tm, tn), F32)],
        compiler_params=_params("parallel", "parallel", "arbitrary"),
    )(h, dp)


def _mm_dw_out(y, dbr, name):
    s, e = y.shape
    d = dbr.shape[1]
    tm, tn, tt = _tile(e, 1024), _tile(d, 1024), _tile(s, 512)

    return pl.pallas_call(
        _dw_body(s // tt), name=name,
        out_shape=jax.ShapeDtypeStruct((e, d), BF16),
        grid=(e // tm, d // tn, s // tt),
        in_specs=[pl.BlockSpec((tt, tm), lambda i, j, t: (t, i)), pl.BlockSpec((tt, tn), lambda i, j, t: (t, j))],
        out_specs=pl.BlockSpec((tm, tn), lambda i, j, t: (i, j)),
        scratch_shapes=[pltpu.VMEM((tm, tn), F32)],
        compiler_params=_params("parallel", "parallel", "arbitrary"),
    )(y, dbr)


def _row_spec(ts, d):
    return pl.BlockSpec((ts, d), lambda i: (i, 0))


def _vec_spec(d):
    return pl.BlockSpec((1, d), lambda i: (0, 0))


def _norm_mod(x, g, scale, shift, name):
    s, d = x.shape
    ts = _tile(s, 512)

    def body(x_ref, g_ref, sc_ref, sh_ref, h_ref):
        xn, _ = _rms(x_ref[...])
        h_ref[...] = ((xn * g_ref[...]) * (1.0 + sc_ref[...]) + sh_ref[...]).astype(BF16)

    return pl.pallas_call(
        body, name=name, out_shape=jax.ShapeDtypeStruct((s, d), BF16), grid=(s // ts,),
        in_specs=[_row_spec(ts, d), _vec_spec(d), _vec_spec(d), _vec_spec(d)],
        out_specs=_row_spec(ts, d), compiler_params=_params("parallel"),
    )(x, g, scale, shift)


def _residual_norm_mod(x, br, gate, g, scale, shift, name):
    s, d = x.shape
    ts = _tile(s, 512)

    def body(x_ref, br_ref, gate_ref, g_ref, sc_ref, sh_ref, x1_ref, h_ref):
        x1 = x_ref[...] + gate_ref[...] * br_ref[...]
        x1_ref[...] = x1
        xn, _ = _rms(x1)
        h_ref[...] = ((xn * g_ref[...]) * (1.0 + sc_ref[...]) + sh_ref[...]).astype(BF16)

    return pl.pallas_call(
        body, name=name,
        out_shape=(jax.ShapeDtypeStruct((s, d), F32), jax.ShapeDtypeStruct((s, d), BF16)), grid=(s // ts,),
        in_specs=[_row_spec(ts, d), _row_spec(ts, d)] + [_vec_spec(d)] * 4,
        out_specs=(_row_spec(ts, d), _row_spec(ts, d)), compiler_params=_params("parallel"),
    )(x, br, gate, g, scale, shift)


def _head(x1, br, gate, gf, target, name):
    s, d = x1.shape
    ts = _tile(s, 512)

    def body(x1_ref, br_ref, gate_ref, gf_ref, tg_ref, dx_ref, dbr_ref, loss_ref, dgf_ref, dgate_ref):
        @pl.when(pl.program_id(0) == 0)
        def _():
            loss_ref[...] = jnp.zeros_like(loss_ref)
            dgf_ref[...] = jnp.zeros_like(dgf_ref)
            dgate_ref[...] = jnp.zeros_like(dgate_ref)
        br = br_ref[...]
        gate = gate_ref[...]
        gf = gf_ref[...]
        xn, r = _rms(x1_ref[...] + gate * br)
        err = xn * gf - tg_ref[...]
        loss_ref[...] += 0.5 * jnp.sum(jnp.mean(err * err, axis=-1, keepdims=True))
        dout = err * (1.0 / d)
        dgf_ref[...] += _colsum(dout * xn)
        dx = _rms_bwd(dout * gf, xn, r)
        dx_ref[...] = dx
        dgate_ref[...] += _colsum(dx * br)
        dbr_ref[...] = (gate * dx).astype(BF16)

    return pl.pallas_call(
        body, name=name,
        out_shape=(jax.ShapeDtypeStruct((s, d), F32), jax.ShapeDtypeStruct((s, d), BF16),
                   jax.ShapeDtypeStruct((SUBLANE, 128), F32), jax.ShapeDtypeStruct((1, d), F32),
                   jax.ShapeDtypeStruct((1, d), F32)),
        grid=(s // ts,),
        in_specs=[_row_spec(ts, d), _row_spec(ts, d), _vec_spec(d), _vec_spec(d), _row_spec(ts, d)],
        out_specs=(_row_spec(ts, d), _row_spec(ts, d), pl.BlockSpec((SUBLANE, 128), lambda i: (0, 0)),
                   _vec_spec(d), _vec_spec(d)),
        compiler_params=_params("arbitrary"),
    )(x1, br, gate, gf, target)


def _norm_mod_bwd(dh, x, dx_in, g, scale, name, br=None, gate=None):
    s, d = x.shape
    ts = _tile(s, 512)
    has_branch = br is not None

    def body(*refs):
        if has_branch:
            (dh_ref, x_ref, dxin_ref, g_ref, sc_ref, br_ref, gate_ref,
             dx_ref, dsh_ref, dsc_ref, dg_ref, dbr_ref, dgate_ref) = refs
        else:
            dh_ref, x_ref, dxin_ref, g_ref, sc_ref, dx_ref, dsh_ref, dsc_ref, dg_ref = refs

        @pl.when(pl.program_id(0) == 0)
        def _():
            dsh_ref[...] = jnp.zeros_like(dsh_ref)
            dsc_ref[...] = jnp.zeros_like(dsc_ref)
            dg_ref[...] = jnp.zeros_like(dg_ref)
            if has_branch:
                dgate_ref[...] = jnp.zeros_like(dgate_ref)
        dh = dh_ref[...]
        g = g_ref[...]
        xn, r = _rms(x_ref[...])
        dsh_ref[...] += _colsum(dh)
        dsc_ref[...] += _colsum(dh * (xn * g))
        da = dh * (1.0 + sc_ref[...])
        dg_ref[...] += _colsum(da * xn)
        dx = dxin_ref[...] + _rms_bwd(da * g, xn, r)
        dx_ref[...] = dx
        if has_branch:
            dgate_ref[...] += _colsum(dx * br_ref[...])
            dbr_ref[...] = (gate_ref[...] * dx).astype(BF16)

    vec_out = jax.ShapeDtypeStruct((1, d), F32)
    out_shape = [jax.ShapeDtypeStruct((s, d), F32), vec_out, vec_out, vec_out]
    out_specs = [_row_spec(ts, d), _vec_spec(d), _vec_spec(d), _vec_spec(d)]
    in_specs = [_row_spec(ts, d)] * 3 + [_vec_spec(d)] * 2
    args = [dh, x, dx_in, g, scale]
    if has_branch:
        out_shape += [jax.ShapeDtypeStruct((s, d), BF16), vec_out]
        out_specs += [_row_spec(ts, d), _vec_spec(d)]
        in_specs += [_row_spec(ts, d), _vec_spec(d)]
        args += [br, gate]
    return pl.pallas_call(
        body, name=name, out_shape=tuple(out_shape), grid=(s // ts,),
        in_specs=in_specs, out_specs=tuple(out_specs), compiler_params=_params("arbitrary"),
    )(*args)


def _conv_tiles(s, e):
    return _tile(s, 512), _tile(e, 512)


def _conv_fwd(proj, conv_w, conv_b, name):
    _, s, e = proj.shape
    ts, te = _conv_tiles(s, e)
    hb = ts // SUBLANE

    def body(bg_ref, cg_ref, xi_ref, z_ref, cgp_ref, xip_ref, w0_ref, w1_ref, w2_ref, b_ref, y_ref):
        cx = cg_ref[...] * xi_ref[...]
        before = jnp.where(pl.program_id(0) > 0, cgp_ref[...] * xip_ref[...], 0.0)
        conv = b_ref[...] + w2_ref[...] * cx
        conv = conv + w0_ref[...] * _shift_down(cx, before, 2)
        conv = conv + w1_ref[...] * _shift_down(cx, before, 1)
        z = z_ref[...]
        y_ref[...] = ((z * _sigmoid(z)) * bg_ref[...] * conv).astype(BF16)

    def part(q):
        return pl.BlockSpec((None, ts, te), lambda i, j: (q, i, j))

    def halo_before(q):
        return pl.BlockSpec((None, SUBLANE, te), lambda i, j: (q, jnp.maximum(i * hb - 1, 0), j))

    return pl.pallas_call(
        body, name=name, out_shape=jax.ShapeDtypeStruct((s, e), BF16), grid=(s // ts, e // te),
        in_specs=[part(0), part(1), part(2), part(3), halo_before(1), halo_before(2)]
        + [pl.BlockSpec((1, te), lambda i, j: (0, j))] * 4,
        out_specs=pl.BlockSpec((ts, te), lambda i, j: (i, j)),
        compiler_params=_params("parallel", "parallel"),
    )(proj, proj, proj, proj, proj, proj, *conv_w, conv_b)


def _conv_bwd(proj, dy, conv_w, conv_b, name):
    _, s, e = proj.shape
    ts, te = _conv_tiles(s, e)
    hb = ts // SUBLANE
    n_i = s // ts
    last_halo = s // SUBLANE - 1

    def body(bg_ref, cg_ref, xi_ref, z_ref, dy_ref, cgp_ref, xip_ref, bgn_ref, zn_ref, dyn_ref,
             w0_ref, w1_ref, w2_ref, b_ref, dp_ref, dw0_ref, dw1_ref, dw2_ref, db_ref):
        i = pl.program_id(1)

        @pl.when(i == 0)
        def _():
            for acc in (dw0_ref, dw1_ref, dw2_ref, db_ref):
                acc[...] = jnp.zeros_like(acc)
        bg, cg, xi, z, dy = bg_ref[...], cg_ref[...], xi_ref[...], z_ref[...], dy_ref[...]
        w0, w1, w2 = w0_ref[...], w1_ref[...], w2_ref[...]
        cx = cg * xi
        before = jnp.where(i > 0, cgp_ref[...] * xip_ref[...], 0.0)
        cx1 = _shift_down(cx, before, 1)
        cx2 = _shift_down(cx, before, 2)
        conv = b_ref[...] + w2 * cx
        conv = conv + w0 * cx2
        conv = conv + w1 * cx1
        sz, dsz = _silu_and_grad(z)
        dp_ref[3] = (dy * bg * conv * dsz).astype(BF16)
        dp_ref[0] = (dy * sz * conv).astype(BF16)
        dconv = dy * sz * bg
        zn = zn_ref[...]
        after = jnp.where(i < n_i - 1, dyn_ref[...] * (zn * _sigmoid(zn)) * bgn_ref[...], 0.0)
        db_ref[...] += _colsum(dconv)
        dw2_ref[...] += _colsum(dconv * cx)
        dw1_ref[...] += _colsum(dconv * cx1)
        dw0_ref[...] += _colsum(dconv * cx2)
        dcx = w2 * dconv + w1 * _shift_up(dconv, after, 1) + w0 * _shift_up(dconv, after, 2)
        dp_ref[1] = (dcx * xi).astype(BF16)
        dp_ref[2] = (dcx * cg).astype(BF16)

    def part(q):
        return pl.BlockSpec((None, ts, te), lambda j, i: (q, i, j))

    def halo_before(q):
        return pl.BlockSpec((None, SUBLANE, te), lambda j, i: (q, jnp.maximum(i * hb - 1, 0), j))

    def halo_after(q):
        return pl.BlockSpec((None, SUBLANE, te), lambda j, i: (q, jnp.minimum((i + 1) * hb, last_halo), j))

    return pl.pallas_call(
        body, name=name,
        out_shape=(jax.ShapeDtypeStruct((4, s, e), BF16),) + (jax.ShapeDtypeStruct((1, e), F32),) * 4,
        grid=(e // te, n_i),
        in_specs=[part(0), part(1), part(2), part(3), pl.BlockSpec((ts, te), lambda j, i: (i, j)),
                  halo_before(1), halo_before(2), halo_after(0), halo_after(3),
                  pl.BlockSpec((SUBLANE, te), lambda j, i: (jnp.minimum((i + 1) * hb, last_halo), j))]
        + [pl.BlockSpec((1, te), lambda j, i: (0, j))] * 4,
        out_specs=(pl.BlockSpec((4, ts, te), lambda j, i: (0, i, j)),) + (pl.BlockSpec((1, te), lambda j, i: (0, j)),) * 4,
        compiler_params=_params("parallel", "arbitrary"),
    )(proj, proj, proj, proj, dy, proj, proj, proj, proj, dy, *conv_w, conv_b)


def _tril(w):
    row = lax.broadcasted_iota(jnp.int32, w.shape, 0)
    col = lax.broadcasted_iota(jnp.int32, w.shape, 1)
    return jnp.where(row >= col, w, 0.0)


def _triu(w):
    row = lax.broadcasted_iota(jnp.int32, w.shape, 0)
    col = lax.broadcasted_iota(jnp.int32, w.shape, 1)
    return jnp.where(row <= col, w, 0.0)


def _layer_norm_fwd(v, g, b):
    mu = jnp.mean(v, axis=-1, keepdims=True)
    vc = v - mu
    rstd = lax.rsqrt(jnp.mean(vc * vc, axis=-1, keepdims=True) + LN_EPS)
    vhat = vc * rstd
    return vhat * g + b, vhat, rstd


def _gmlp_fwd(proj, ln_g, ln_b, w_s, b_s_t, name):
    _, s, e = proj.shape
    gw = e // GROUPS

    def body(pu_ref, pv_ref, pz_ref, g_ref, b_ref, ws_ref, bs_ref, y_ref, vn_scr, mix_scr):
        vn, _, _ = _layer_norm_fwd(_gelu(pv_ref[...]), g_ref[...], b_ref[...])
        vn_scr[...] = vn.astype(BF16)
        for gi in range(GROUPS):
            cols = slice(gi * gw, (gi + 1) * gw)
            wt = _tril(ws_ref[gi]).astype(BF16)
            mix_scr[:, cols] = jnp.dot(wt, vn_scr[:, cols], preferred_element_type=F32) + bs_ref[:, gi:gi + 1]
        z = pz_ref[...]
        y_ref[...] = ((z * _sigmoid(z)) * (_gelu(pu_ref[...]) * mix_scr[...])).astype(BF16)

    def part(q):
        return pl.BlockSpec((None, CHUNK, e), lambda i: (q, i, 0))

    return pl.pallas_call(
        body, name=name, out_shape=jax.ShapeDtypeStruct((s, e), BF16), grid=(s // CHUNK,),
        in_specs=[part(0), part(1), part(2), _vec_spec(e), _vec_spec(e),
                  pl.BlockSpec((GROUPS, CHUNK, CHUNK), lambda i: (0, 0, 0)),
                  pl.BlockSpec((CHUNK, GROUPS), lambda i: (0, 0))],
        out_specs=pl.BlockSpec((CHUNK, e), lambda i: (i, 0)),
        scratch_shapes=[pltpu.VMEM((CHUNK, e), BF16), pltpu.VMEM((CHUNK, e), F32)],
        compiler_params=_params("parallel"),
    )(proj, proj, proj, ln_g, ln_b, w_s, b_s_t)


def _gmlp_bwd(proj, dy, ln_g, ln_b, w_s, w_s_t, b_s_t, name):
    _, s, e = proj.shape
    gw = e // GROUPS
    n_i = s // CHUNK

    def body(pu_ref, pv_ref, pz_ref, dy_ref, g_ref, b_ref, ws_ref, wst_ref, bs_ref,
             dp_ref, dws_ref, dbs_ref, dlg_ref, dlb_ref, vn_scr, mix_scr, dm_scr, dvn_scr, dmacc_scr):
        i = pl.program_id(0)

        @pl.when(i == 0)
        def _():
            dws_ref[...] = jnp.zeros_like(dws_ref)
            dlg_ref[...] = jnp.zeros_like(dlg_ref)
            dlb_ref[...] = jnp.zeros_like(dlb_ref)
            dmacc_scr[...] = jnp.zeros_like(dmacc_scr)
        ln_g = g_ref[...]
        u, du_dpu = _gelu_and_grad(pu_ref[...])
        v, dv_dpv = _gelu_and_grad(pv_ref[...])
        vn, vhat, rstd = _layer_norm_fwd(v, ln_g, b_ref[...])
        vn_scr[...] = vn.astype(BF16)
        for gi in range(GROUPS):
            cols = slice(gi * gw, (gi + 1) * gw)
            wt = _tril(ws_ref[gi]).astype(BF16)
            mix_scr[:, cols] = jnp.dot(wt, vn_scr[:, cols], preferred_element_type=F32) + bs_ref[:, gi:gi + 1]
        mixed = mix_scr[...]
        dy = dy_ref[...]
        sz, dsz = _silu_and_grad(pz_ref[...])
        ds = dy * sz
        dp_ref[2] = (dy * (u * mixed) * dsz).astype(BF16)
        dp_ref[0] = (ds * mixed * du_dpu).astype(BF16)
        dm = ds * u
        dmacc_scr[...] += dm
        dm_scr[...] = dm.astype(BF16)
        for gi in range(GROUPS):
            cols = slice(gi * gw, (gi + 1) * gw)
            dmg = dm_scr[:, cols]
            dws_ref[gi] += lax.dot_general(dmg, vn_scr[:, cols], (((1,), (1,)), ((), ())), preferred_element_type=F32)
            wtt = _triu(wst_ref[gi]).astype(BF16)
            dvn_scr[:, cols] = jnp.dot(wtt, dmg, preferred_element_type=F32)
        dvn = dvn_scr[...]
        dlg_ref[...] += _colsum(dvn * vhat)
        dlb_ref[...] += _colsum(dvn)
        dvh = dvn * ln_g
        dv = rstd * (dvh - jnp.mean(dvh, axis=-1, keepdims=True) - vhat * jnp.mean(dvh * vhat, axis=-1, keepdims=True))
        dp_ref[1] = (dv * dv_dpv).astype(BF16)

        @pl.when(i == n_i - 1)
        def _():
            for gi in range(GROUPS):
                dws_ref[gi] = _tril(dws_ref[gi])
                dbs_ref[:, gi:gi + 1] = jnp.sum(dmacc_scr[:, gi * gw:(gi + 1) * gw], axis=1, keepdims=True)

    def part(q):
        return pl.BlockSpec((None, CHUNK, e), lambda i: (q, i, 0))

    w_spec = pl.BlockSpec((GROUPS, CHUNK, CHUNK), lambda i: (0, 0, 0))
    bs_spec = pl.BlockSpec((CHUNK, GROUPS), lambda i: (0, 0))
    return pl.pallas_call(
        body, name=name,
        out_shape=(jax.ShapeDtypeStruct((3, s, e), BF16), jax.ShapeDtypeStruct((GROUPS, CHUNK, CHUNK), F32),
                   jax.ShapeDtypeStruct((CHUNK, GROUPS), F32), jax.ShapeDtypeStruct((1, e), F32),
                   jax.ShapeDtypeStruct((1, e), F32)),
        grid=(n_i,),
        in_specs=[part(0), part(1), part(2), pl.BlockSpec((CHUNK, e), lambda i: (i, 0)), _vec_spec(e), _vec_spec(e),
                  w_spec, w_spec, bs_spec],
        out_specs=(pl.BlockSpec((3, CHUNK, e), lambda i: (0, i, 0)), w_spec, bs_spec, _vec_spec(e), _vec_spec(e)),
        scratch_shapes=[pltpu.VMEM((CHUNK, e), BF16), pltpu.VMEM((CHUNK, e), F32), pltpu.VMEM((CHUNK, e), BF16),
                        pltpu.VMEM((CHUNK, e), F32), pltpu.VMEM((CHUNK, e), F32)],
        compiler_params=_params("arbitrary"),
    )(proj, proj, proj, dy, ln_g, ln_b, w_s, w_s_t, b_s_t)


def _mod_fwd(c_all, mod_w, mod_b_cols, name):
    n_layer, d, w = mod_w.shape

    def body(c_ref, w_ref, b_ref, ca_ref, o_ref):
        c = c_ref[...]
        ca = c * _sigmoid(c)
        ca_ref[...] = ca
        for li in range(n_layer):
            o_ref[li * N_DEV:(li + 1) * N_DEV, :] = (
                jnp.dot(ca, w_ref[li], preferred_element_type=F32, precision=lax.Precision.HIGHEST) + b_ref[li])

    return pl.pallas_call(
        body, name=name,
        out_shape=(jax.ShapeDtypeStruct((N_DEV, d), F32), jax.ShapeDtypeStruct((n_layer * N_DEV, w), F32)),
        in_specs=[VMEM_SPEC] * 3, out_specs=(VMEM_SPEC, VMEM_SPEC),
        compiler_params=pltpu.CompilerParams(vmem_limit_bytes=VMEM_LIMIT),
    )(c_all, mod_w, mod_b_cols)


def _adamw(w, g, m, v):
    m = ADAM_B1 * m + (1.0 - ADAM_B1) * g
    v = ADAM_B2 * v + (1.0 - ADAM_B2) * (g * g)
    m_hat = m / (1.0 - ADAM_B1 ** ADAM_STEP)
    v_hat = v / (1.0 - ADAM_B2 ** ADAM_STEP)
    delta = -ADAM_LR * (m_hat / (jnp.sqrt(v_hat) + ADAM_EPS) + ADAM_WD * w)
    return delta, m, v


def _adamw_2d(w, g, m, v, name):
    r, c = w.shape
    tr, tc = _tile(r, 512), _tile(c, 1024)

    def body(w_ref, g_ref, m_ref, v_ref, d_ref, nm_ref, nv_ref):
        d_ref[...], nm_ref[...], nv_ref[...] = _adamw(w_ref[...], g_ref[...], m_ref[...], v_ref[...])

    spec = pl.BlockSpec((tr, tc), lambda i, j: (i, j))
    shape = jax.ShapeDtypeStruct((r, c), F32)
    return pl.pallas_call(
        body, name=name, out_shape=(shape, shape, shape), grid=(r // tr, c // tc),
        in_specs=[spec] * 4, out_specs=(spec, spec, spec), compiler_params=_params("parallel", "parallel"),
    )(w, g, m, v)


def _mod_w_update(ca_t, dmod_cols, w, m, v, name):
    n_layer, d, wd = w.shape
    tr = _tile(d, 256)

    def body(ca_ref, dm_ref, w_ref, m_ref, v_ref, g_ref, d_ref, nm_ref, nv_ref):
        ca = ca_ref[...]
        dm = dm_ref[...]
        g = ca[:, 0:1] * dm[0:1, :]
        for b in range(1, N_DEV):
            g = g + ca[:, b:b + 1] * dm[b:b + 1, :]
        g_ref[...] = g
        d_ref[...], nm_ref[...], nv_ref[...] = _adamw(w_ref[...], g, m_ref[...], v_ref[...])

    spec = pl.BlockSpec((None, tr, wd), lambda l, i: (l, i, 0))
    shape = jax.ShapeDtypeStruct((n_layer, d, wd), F32)
    return pl.pallas_call(
        body, name=name, out_shape=(shape,) * 4, grid=(n_layer, d // tr),
        in_specs=[pl.BlockSpec((tr, N_DEV), lambda l, i: (i, 0)), pl.BlockSpec((None, N_DEV, wd), lambda l, i: (l, 0, 0)),
                  spec, spec, spec],
        out_specs=(spec,) * 4, compiler_params=_params("parallel", "parallel"),
    )(ca_t, dmod_cols, w, m, v)


def _adamw_small(ws, gs, ms, vs, name):
    n = len(ws)

    def body(*refs):
        ins, outs = refs[:4 * n], refs[4 * n:]
        for k in range(n):
            delta, nm, nv = _adamw(ins[k][...], ins[n + k][...], ins[2 * n + k][...], ins[3 * n + k][...])
            outs[3 * k][...] = delta
            outs[3 * k + 1][...] = nm
            outs[3 * k + 2][...] = nv

    out_shape = []
    for w in ws:
        out_shape += [jax.ShapeDtypeStruct(w.shape, F32)] * 3
    outs = pl.pallas_call(
        body, name=name, out_shape=tuple(out_shape),
        in_specs=[VMEM_SPEC] * (4 * n), out_specs=tuple([VMEM_SPEC] * (3 * n)),
        compiler_params=pltpu.CompilerParams(vmem_limit_bytes=VMEM_LIMIT),
    )(*ws, *gs, *ms, *vs)
    return [tuple(outs[3 * k:3 * k + 3]) for k in range(n)]


def _place():
    return lax.axis_index("x"), lax.axis_index("y"), lax.axis_index("c")


def _other_chips(x, y):
    return [(1 - x, y), (x, 1 - y), (1 - x, 1 - y)]


def _all_gather(block, name):
    m_per, n = block.shape

    def body(x_ref, out_ref, send_sems, recv_sems, local_sem):
        x, y, c = _place()
        me, sibling = (x, y, c), (x, y, 1 - c)
        chips = _other_chips(x, y)

        def rows(px, py, pc):
            return out_ref.at[pl.ds((4 * px + 2 * py + pc) * m_per, m_per), :]

        def copy(k, blk, to, src=None):
            return pltpu.make_async_remote_copy(
                src_ref=rows(*blk) if src is None else src, dst_ref=rows(*blk),
                send_sem=send_sems.at[k], recv_sem=recv_sems.at[k], device_id=to, device_id_type=MESH)

        mine = pltpu.make_async_copy(x_ref, rows(*me), local_sem)
        mine.start()
        first = [copy(0, me, sibling, src=x_ref)]
        first += [copy(1 + j, me, (*chip, c), src=x_ref) for j, chip in enumerate(chips)]
        for cp in first:
            cp.start()
        passed = [copy(4 + j, (*chip, c), sibling) for j, chip in enumerate(chips)]
        for j, chip in enumerate(chips):
            copy(1 + j, (*chip, c), me).wait_recv()
            passed[j].start()
        copy(0, sibling, me).wait_recv()
        for j, chip in enumerate(chips):
            copy(4 + j, (*chip, 1 - c), me).wait_recv()
        for cp in first + passed:
            cp.wait_send()
        mine.wait()

    return pl.pallas_call(
        body, name=name, out_shape=jax.ShapeDtypeStruct((N_DEV * m_per, n), F32),
        in_specs=[VMEM_SPEC], out_specs=VMEM_SPEC,
        scratch_shapes=[pltpu.SemaphoreType.DMA((7,)), pltpu.SemaphoreType.DMA((7,)), pltpu.SemaphoreType.DMA],
        compiler_params=pltpu.CompilerParams(vmem_limit_bytes=VMEM_LIMIT),
    )(block)


def _hbm(a):
    return pltpu.with_memory_space_constraint(a, pltpu.HBM)


def _place_own(shard, kind, chip_idx, name):
    r, cdim = shard.shape
    tr, tc = _tile(r, 512), _tile(cdim, 1024)
    nrb, ncb = r // tr, cdim // tc

    def body(k_ref, s_ref, o_ref):
        o_ref[...] = s_ref[...].astype(BF16)

    if kind == "col":
        full, o_map = (r, N_CHIP * cdim), lambda i, j, kr: (i, kr[0] * ncb + j)
    else:
        full, o_map = (N_CHIP * r, cdim), lambda i, j, kr: (kr[0] * nrb + i, j)
    return pl.pallas_call(
        body, name=name, out_shape=jax.ShapeDtypeStruct(full, BF16),
        grid_spec=pltpu.PrefetchScalarGridSpec(
            num_scalar_prefetch=1, grid=(nrb, ncb),
            in_specs=[pl.BlockSpec((tr, tc), lambda i, j, kr: (i, j))],
            out_specs=pl.BlockSpec((tr, tc), o_map)),
        compiler_params=_params("parallel", "parallel"),
    )(chip_idx, shard)


def _weight_window(ref, kind, shard_shape, k, half):
    r, cdim = shard_shape
    hr = r // 2
    if kind == "col":
        return ref.at[pl.ds(half * hr, hr), pl.ds(pl.multiple_of(k * cdim, 128), cdim)]
    return ref.at[pl.ds(pl.multiple_of(k * r + half * hr, 2 * SUBLANE), hr), :]


def _gather_weights(fulls, kinds, shard_shapes, name):
    n = len(fulls)

    def body(*refs):
        ins, outs = refs[:n], refs[n:2 * n]
        send_sems, recv_sems, fwd_send_sems, fwd_recv_sems = refs[2 * n:]
        x, y, c = _place()
        sibling = (x, y, 1 - c)
        chips = _other_chips(x, y)

        def window(ref, w, chip, half):
            return _weight_window(ref, kinds[w], shard_shapes[w], 2 * chip[0] + chip[1], half)

        sends, fwds = [], []
        for w in range(n):
            for j, chip in enumerate(chips):
                cp = pltpu.make_async_remote_copy(
                    src_ref=window(ins[w], w, (x, y), c), dst_ref=window(outs[w], w, (x, y), c),
                    send_sem=send_sems.at[3 * w + j], recv_sem=recv_sems.at[3 * w + j],
                    device_id=(*chip, c), device_id_type=MESH)
                cp.start()
                sends.append(cp)
        for w in range(n):
            for j, chip in enumerate(chips):
                landed = window(outs[w], w, chip, c)
                pltpu.make_async_remote_copy(
                    src_ref=landed, dst_ref=landed,
                    send_sem=send_sems.at[3 * w + j], recv_sem=recv_sems.at[3 * w + j],
                    device_id=(*chip, c), device_id_type=MESH).wait_recv()
                cp = pltpu.make_async_remote_copy(
                    src_ref=landed, dst_ref=landed,
                    send_sem=fwd_send_sems.at[3 * w + j], recv_sem=fwd_recv_sems.at[3 * w + j],
                    device_id=sibling, device_id_type=MESH)
                cp.start()
                fwds.append(cp)
        for w in range(n):
            for j, chip in enumerate(chips):
                theirs = window(outs[w], w, chip, 1 - c)
                pltpu.make_async_remote_copy(
                    src_ref=theirs, dst_ref=theirs,
                    send_sem=fwd_send_sems.at[3 * w + j], recv_sem=fwd_recv_sems.at[3 * w + j],
                    device_id=sibling, device_id_type=MESH).wait_recv()
        for cp in sends + fwds:
            cp.wait_send()

    return pl.pallas_call(
        body, name=name, out_shape=tuple(jax.ShapeDtypeStruct(f.shape, f.dtype) for f in fulls),
        in_specs=[HBM_SPEC] * n, out_specs=tuple([HBM_SPEC] * n),
        input_output_aliases={w: w for w in range(n)},
        scratch_shapes=[pltpu.SemaphoreType.DMA((3 * n,))] * 4,
    )(*fulls)


def _gather_weights_start(fulls, kinds, shard_shapes, name):
    n = len(fulls)

    def body(*refs):
        ins = refs[:n]
        send_sems, recv_sems = refs[n], refs[n + 1]
        token = refs[2 * n + 2]
        x, y, c = _place()
        for w in range(n):
            own = _weight_window(ins[w], kinds[w], shard_shapes[w], 2 * x + y, c)
            for j, chip in enumerate(_other_chips(x, y)):
                for cc in range(2):
                    pltpu.make_async_remote_copy(
                        src_ref=own, dst_ref=own,
                        send_sem=send_sems.at[6 * w + 2 * j + cc], recv_sem=recv_sems.at[6 * w + 2 * j + c],
                        device_id=(*chip, cc), device_id_type=MESH).start()
        token[...] = jnp.zeros_like(token)

    sems = pltpu.SemaphoreType.DMA((6 * n,))
    outs = pl.pallas_call(
        body, name=name,
        out_shape=(sems, sems) + tuple(pltpu.HBM(f.shape, f.dtype) for f in fulls)
        + (jax.ShapeDtypeStruct((SUBLANE, 128), F32),),
        in_specs=[HBM_SPEC] * n, out_specs=(SEM_SPEC, SEM_SPEC) + (HBM_SPEC,) * n + (VMEM_SPEC,),
        input_output_aliases={w: 2 + w for w in range(n)},
        compiler_params=pltpu.CompilerParams(has_side_effects=EFFECT),
    )(*[_hbm(f) for f in fulls])
    return outs[0], outs[1], list(outs[2:2 + n]), outs[2 + n]


def _gather_weights_wait(send_sems, recv_sems, fulls, kinds, shard_shapes, after, name):
    n = len(fulls)

    def body(*refs):
        ins = refs[:n]
        send_sems, recv_sems = refs[n], refs[n + 1]
        x, y, c = _place()
        for w in range(n):
            own = _weight_window(ins[w], kinds[w], shard_shapes[w], 2 * x + y, c)
            for j, chip in enumerate(_other_chips(x, y)):
                for cc in range(2):
                    landed = _weight_window(ins[w], kinds[w], shard_shapes[w], 2 * chip[0] + chip[1], cc)
                    pltpu.make_async_remote_copy(
                        src_ref=own, dst_ref=own, send_sem=send_sems.at[6 * w + 2 * j + cc],
                        recv_sem=recv_sems.at[6 * w + 2 * j + cc], device_id=(*chip, cc), device_id_type=MESH).wait_send()
                    pltpu.make_async_remote_copy(
                        src_ref=landed, dst_ref=landed, send_sem=send_sems.at[6 * w + 2 * j + cc],
                        recv_sem=recv_sems.at[6 * w + 2 * j + cc], device_id=(*chip, cc), device_id_type=MESH).wait_recv()

    return pl.pallas_call(
        body, name=name, out_shape=tuple(pltpu.HBM(f.shape, f.dtype) for f in fulls),
        in_specs=[HBM_SPEC] * n + [SEM_SPEC, SEM_SPEC, ANY_SPEC], out_specs=(HBM_SPEC,) * n,
        input_output_aliases={w: w for w in range(n)},
        compiler_params=pltpu.CompilerParams(has_side_effects=EFFECT),
    )(*fulls, send_sems, recv_sems, after)


def _grad_piece(ref, kind, h, cdim, k, half):
    if kind == "col":
        return ref.at[pl.ds(half * h, h), pl.ds(pl.multiple_of(k * cdim, 128), cdim)]
    return ref.at[k, pl.ds(half * h, h), :]


def _grad_dims(g, kind):
    return (g.shape[0] // 2, g.shape[1] // N_CHIP) if kind == "col" else (g.shape[1] // 2, g.shape[2])


def _reduce_start(grads, kinds, name):
    n = len(grads)
    dims = [_grad_dims(g, kind) for g, kind in zip(grads, kinds)]
    lands = [lax.empty((N_DEV - 1, h, cdim), g.dtype) for g, (h, cdim) in zip(grads, dims)]

    def body(*refs):
        g_ins, land_ins = refs[:n], refs[n:2 * n]
        send_sems, recv_sems = refs[2 * n], refs[2 * n + 1]
        token = refs[4 * n + 2]
        x, y, c = _place()
        for w in range(n):
            h, cdim = dims[w]
            base = (N_DEV - 1) * w
            pltpu.make_async_remote_copy(
                src_ref=_grad_piece(g_ins[w], kinds[w], h, cdim, 2 * x + y, 1 - c), dst_ref=land_ins[w].at[0],
                send_sem=send_sems.at[base], recv_sem=recv_sems.at[base],
                device_id=(x, y, 1 - c), device_id_type=MESH).start()
            for j, chip in enumerate(_other_chips(x, y)):
                for cc in range(2):
                    pltpu.make_async_remote_copy(
                        src_ref=_grad_piece(g_ins[w], kinds[w], h, cdim, 2 * chip[0] + chip[1], cc),
                        dst_ref=land_ins[w].at[1 + 2 * j + c],
                        send_sem=send_sems.at[base + 1 + 2 * j + cc], recv_sem=recv_sems.at[base + 1 + 2 * j + c],
                        device_id=(*chip, cc), device_id_type=MESH).start()
        token[...] = jnp.zeros_like(token)

    sems = pltpu.SemaphoreType.DMA(((N_DEV - 1) * n,))
    outs = pl.pallas_call(
        body, name=name,
        out_shape=(sems, sems) + tuple(pltpu.HBM(a.shape, a.dtype) for a in list(grads) + lands)
        + (jax.ShapeDtypeStruct((SUBLANE, 128), F32),),
        in_specs=[HBM_SPEC] * (2 * n), out_specs=(SEM_SPEC, SEM_SPEC) + (HBM_SPEC,) * (2 * n) + (VMEM_SPEC,),
        input_output_aliases={i: 2 + i for i in range(2 * n)},
        compiler_params=pltpu.CompilerParams(has_side_effects=EFFECT),
    )(*[_hbm(a) for a in list(grads) + lands])
    return outs[0], outs[1], list(outs[2:2 + n]), list(outs[2 + n:2 + 2 * n]), outs[2 + 2 * n]


def _reduce_wait(send_sems, recv_sems, grads, lands, kinds, after, name):
    n = len(grads)
    dims = [_grad_dims(g, kind) for g, kind in zip(grads, kinds)]

    def body(*refs):
        g_ins, land_ins = refs[:n], refs[n:2 * n]
        send_sems, recv_sems = refs[2 * n], refs[2 * n + 1]
        x, y, c = _place()
        for w in range(n):
            h, cdim = dims[w]
            piece = _grad_piece(g_ins[w], kinds[w], h, cdim, 2 * x + y, c)
            for s in range(N_DEV - 1):
                k = (N_DEV - 1) * w + s
                slot = land_ins[w].at[s]
                pltpu.make_async_remote_copy(
                    src_ref=piece, dst_ref=slot, send_sem=send_sems.at[k], recv_sem=recv_sems.at[k],
                    device_id=(x, y, 1 - c), device_id_type=MESH).wait_send()
                pltpu.make_async_remote_copy(
                    src_ref=piece, dst_ref=slot, send_sem=send_sems.at[k], recv_sem=recv_sems.at[k],
                    device_id=(x, y, 1 - c), device_id_type=MESH).wait_recv()

    outs = pl.pallas_call(
        body, name=name, out_shape=tuple(pltpu.HBM(a.shape, a.dtype) for a in list(grads) + list(lands)),
        in_specs=[HBM_SPEC] * (2 * n) + [SEM_SPEC, SEM_SPEC, ANY_SPEC], out_specs=(HBM_SPEC,) * (2 * n),
        input_output_aliases={i: i for i in range(2 * n)},
        compiler_params=pltpu.CompilerParams(has_side_effects=EFFECT),
    )(*grads, *lands, send_sems, recv_sems, after)
    return list(outs[:n]), list(outs[n:])


def _add_pieces(g, land, kind, chip_idx, core_idx, name):
    _, h, cdim = land.shape
    tr, tc = _tile(h, 256), _tile(cdim, 2048)
    nrb, ncb = h // tr, cdim // tc

    def body(k_ref, c_ref, g_ref, l_ref, o_ref):
        acc = g_ref[...].astype(F32)
        for s in range(N_DEV - 1):
            acc = acc + l_ref[s].astype(F32)
        o_ref[...] = acc

    if kind == "col":
        g_spec = pl.BlockSpec((tr, tc), lambda i, j, kr, cr: (cr[0] * nrb + i, kr[0] * ncb + j))
    else:
        g_spec = pl.BlockSpec((None, tr, tc), lambda i, j, kr, cr: (kr[0], cr[0] * nrb + i, j))
    return pl.pallas_call(
        body, name=name, out_shape=jax.ShapeDtypeStruct((2 * h, cdim), F32),
        grid_spec=pltpu.PrefetchScalarGridSpec(
            num_scalar_prefetch=2, grid=(nrb, ncb),
            in_specs=[g_spec, pl.BlockSpec((N_DEV - 1, tr, tc), lambda i, j, kr, cr: (0, i, j))],
            out_specs=pl.BlockSpec((tr, tc), lambda i, j, kr, cr: (cr[0] * nrb + i, j))),
        compiler_params=_params("parallel", "parallel"),
    )(chip_idx, core_idx, g, land)


def _join_halves(shards, name):
    n = len(shards)

    def body(*refs):
        ins, outs = refs[:n], refs[n:2 * n]
        send_sems, recv_sems = refs[2 * n:]
        x, y, c = _place()
        copies = []
        for w in range(n):
            h = shards[w].shape[0] // 2
            cp = pltpu.make_async_remote_copy(
                src_ref=ins[w].at[pl.ds(c * h, h), :], dst_ref=outs[w].at[pl.ds(c * h, h), :],
                send_sem=send_sems.at[w], recv_sem=recv_sems.at[w], device_id=(x, y, 1 - c), device_id_type=MESH)
            cp.start()
            copies.append(cp)
        for w in range(n):
            h = shards[w].shape[0] // 2
            theirs = outs[w].at[pl.ds((1 - c) * h, h), :]
            pltpu.make_async_remote_copy(
                src_ref=theirs, dst_ref=theirs, send_sem=send_sems.at[w], recv_sem=recv_sems.at[w],
                device_id=(x, y, 1 - c), device_id_type=MESH).wait_recv()
        for cp in copies:
            cp.wait_send()

    return pl.pallas_call(
        body, name=name, out_shape=tuple(jax.ShapeDtypeStruct(s.shape, s.dtype) for s in shards),
        in_specs=[HBM_SPEC] * n, out_specs=tuple([HBM_SPEC] * n),
        input_output_aliases={w: w for w in range(n)},
        scratch_shapes=[pltpu.SemaphoreType.DMA((n,))] * 2,
    )(*shards)


def _sum_devices(gathered, rows, name):
    n = gathered.shape[1]

    def body(g_ref, o_ref):
        acc = g_ref[0:rows, :]
        for dev in range(1, N_DEV):
            acc = acc + g_ref[dev * rows:(dev + 1) * rows, :]
        o_ref[...] = acc

    return pl.pallas_call(
        body, name=name, out_shape=jax.ShapeDtypeStruct((rows, n), F32),
        in_specs=[VMEM_SPEC], out_specs=VMEM_SPEC,
        compiler_params=pltpu.CompilerParams(vmem_limit_bytes=VMEM_LIMIT),
    )(gathered)


def _pack(vectors, width):
    flat = [v.reshape(-1) for v in vectors]
    offsets, total = [], 0
    for f in flat:
        offsets.append(total)
        total += f.shape[0]
    rows = -(-total // (width * SUBLANE)) * SUBLANE
    flat.append(jnp.zeros((rows * width - total,), F32))
    return jnp.concatenate(flat).reshape(rows, width), offsets


def kernel(x, c, mod_w, mod_b, norm_g, a_w_in, a_conv_w, a_conv_b, a_w_out, b_w_in, b_ln_g, b_ln_b, b_w_s, b_b_s, b_w_out, final_g, loss_target, m_mod_w, m_mod_b, m_norm_g, m_a_w_in, m_a_conv_w, m_a_conv_b, m_a_w_out, m_b_w_in, m_b_ln_g, m_b_ln_b, m_b_w_s, m_b_b_s, m_b_w_out, m_final_g, v_mod_w, v_mod_b, v_norm_g, v_a_w_in, v_a_conv_w, v_a_conv_b, v_a_w_out, v_b_w_in, v_b_ln_g, v_b_ln_b, v_b_w_s, v_b_b_s, v_b_w_out, v_final_g):
    seq, d = x.shape[1], x.shape[2]
    e = a_conv_b.shape[1]
    wd = mod_w.shape[2]
    ax, ay, ac = _place()
    chip = 2 * ax + ay
    dev = 2 * chip + ac
    chip_idx = jnp.reshape(chip, (1,)).astype(jnp.int32)
    core_idx = jnp.reshape(ac, (1,)).astype(jnp.int32)

    x2d = x[0]
    target = loss_target[0]

    es = e // N_CHIP
    packed0, offs0 = _pack([c, a_conv_w, b_ln_g, b_ln_b], 1024)
    gathered0 = _all_gather(packed0, "gather_params").reshape(N_DEV, -1)
    c_all = gathered0[:, :d]
    per_chip = gathered0[0::2]

    def from_chips_cols(k, rows_):
        got = per_chip[:, offs0[k]:offs0[k] + rows_ * es].reshape(N_CHIP, rows_, es)
        return jnp.transpose(got, (1, 0, 2)).reshape(rows_, e)

    conv_w_full = from_chips_cols(1, 3)
    conv_w = [conv_w_full[k:k + 1] for k in range(3)]
    ln_g, ln_b = from_chips_cols(2, 1), from_chips_cols(3, 1)
    mod_b_cols = lax.dynamic_slice_in_dim(mod_b, chip * wd, wd, axis=1)[:, None, :]
    c_act, mod_part = _mod_fwd(c_all, mod_w, mod_b_cols, "mod_fwd")
    n_layer = mod_w.shape[0]
    mod_all = _all_gather(mod_part, "gather_mod").reshape(N_CHIP, 2, n_layer, N_DEV, wd)[:, 0]
    mod_all = jnp.transpose(mod_all, (1, 2, 0, 3)).reshape(n_layer, N_DEV, N_CHIP * wd)
    mod_me = lax.dynamic_index_in_dim(mod_all, dev, axis=1, keepdims=False)
    shift = [mod_me[l:l + 1, 0:d] for l in range(n_layer)]
    scale = [mod_me[l:l + 1, d:2 * d] for l in range(n_layer)]
    gate = [mod_me[l:l + 1, 2 * d:3 * d] for l in range(n_layer)]

    kinds = {"a_w_in": "col", "b_w_in": "col", "a_w_out": "row", "b_w_out": "row"}
    shards = {"a_w_in": a_w_in[0], "b_w_in": b_w_in[0], "a_w_out": a_w_out[0], "b_w_out": b_w_out[0]}
    placed = {nm: _place_own(sh, kinds[nm], chip_idx, "place_" + nm) for nm, sh in shards.items()}
    (wa_in,) = _gather_weights([placed["a_w_in"]], [kinds["a_w_in"]], [shards["a_w_in"].shape], "gather_a_w_in")
    later = ["a_w_out", "b_w_in", "b_w_out"]
    later_kinds, later_shapes = [kinds[nm] for nm in later], [shards[nm].shape for nm in later]
    w_send, w_recv, w_flight, w_token = _gather_weights_start(
        [placed[nm] for nm in later], later_kinds, later_shapes, "gather_weights_start")

    g0, g1, gf = norm_g[0:1], norm_g[1:2], final_g[None, :]
    h0 = _norm_mod(x2d, g0, scale[0], shift[0], "norm_mod0")
    proj0 = _mm_proj(h0, wa_in, 4, "a_proj", w_token)
    wa_out, wb_in, wb_out = _gather_weights_wait(
        w_send, w_recv, w_flight, later_kinds, later_shapes, proj0, "gather_weights_wait")
    y0 = _conv_fwd(proj0, conv_w, a_conv_b, "conv_fwd")
    br0 = _mm_nn(y0, wa_out, "a_out")
    x1, h1 = _residual_norm_mod(x2d, br0, gate[0], g1, scale[1], shift[1], "residual_norm_mod1")
    proj1 = _mm_proj(h1, wb_in, 3, "b_proj", w_token)
    b_s_t = jnp.transpose(b_b_s[0])
    y1 = _gmlp_fwd(proj1, ln_g, ln_b, b_w_s[0], b_s_t, "gmlp_fwd")
    br1 = _mm_nn(y1, wb_out, "b_out")
    dx2, dbr1, loss_part, g_final_g, dgate1 = _head(x1, br1, gate[1], gf, target, "head")

    dy1 = _mm_nt(dbr1, wb_out, "b_out_dx")
    gw_b_out = _mm_dw_out(y1, dbr1, "b_out_dw")
    dproj1, g_w_s, g_b_s_t, g_ln_g, g_ln_b = _gmlp_bwd(
        proj1, dy1, ln_g, ln_b, b_w_s[0], jnp.swapaxes(b_w_s[0], 1, 2), b_s_t, "gmlp_bwd")
    gw_b_in = _mm_dw_in(h1, dproj1, "b_proj_dw")
    b_kinds = ["col", "row"]
    b_send, b_recv, b_grads, b_lands, b_token = _reduce_start(
        [gw_b_in, gw_b_out.reshape(N_CHIP, e // N_CHIP, d)], b_kinds, "reduce_b_start")
    dh1 = _mm_dh(dproj1, wb_in, "b_proj_dx", b_token)
    dx1, dshift1, dscale1, g_g1, dbr0, dgate0 = _norm_mod_bwd(
        dh1, x1, dx2, g1, scale[1], "norm_mod1_bwd", br=br0, gate=gate[0])

    dy0 = _mm_nt(dbr0, wa_out, "a_out_dx")
    gw_a_out = _mm_dw_out(y0, dbr0, "a_out_dw")
    dproj0, g_w0, g_w1, g_w2, g_conv_b = _conv_bwd(proj0, dy0, conv_w, a_conv_b, "conv_bwd")
    gw_a_in = _mm_dw_in(h0, dproj0, "a_proj_dw")
    a_kinds = ["col", "row"]
    a_send, a_recv, a_grads, a_lands, a_token = _reduce_start(
        [gw_a_in, gw_a_out.reshape(N_CHIP, e // N_CHIP, d)], a_kinds, "reduce_a_start")
    dh0 = _mm_dh(dproj0, wa_in, "a_proj_dx", a_token)
    grad_x, dshift0, dscale0, g_g0 = _norm_mod_bwd(dh0, x2d, dx1, g0, scale[0], "norm_mod0_bwd")

    dmod = jnp.concatenate([dshift0, dscale0, dgate0, dshift1, dscale1, dgate1], axis=1)
    small = [loss_part[0, 0:1], g_final_g, g_g0, g_g1, jnp.concatenate([g_w0, g_w1, g_w2], axis=0), g_conv_b, g_ln_g, g_ln_b,
             jnp.transpose(g_b_s_t), g_w_s, dmod]
    packed, offs = _pack(small, 1024)
    rows = packed.shape[0]
    gathered = _all_gather(packed, "gather_small")
    summed = _sum_devices(gathered, rows, "sum_small").reshape(-1)

    def take(k, shape):
        size = math.prod(shape)
        return summed[offs[k]:offs[k] + size].reshape(shape)

    loss = take(0, ())
    grad_final_g = take(1, (d,))
    grad_norm_g = jnp.concatenate([take(2, (1, d)), take(3, (1, d))], axis=0)
    grad_conv_w_full = take(4, (3, e))
    grad_a_conv_b = take(5, (1, e))
    grad_ln_g_full = take(6, (1, e))
    grad_ln_b_full = take(7, (1, e))
    grad_b_b_s = take(8, (1, GROUPS, CHUNK))
    grad_b_w_s = take(9, (1, GROUPS, CHUNK, CHUNK))
    grad_mod_b = take(10, (n_layer, 3 * d))
    grad_a_conv_w = lax.dynamic_slice_in_dim(grad_conv_w_full, chip * es, es, axis=1)[None]
    grad_b_ln_g = lax.dynamic_slice_in_dim(grad_ln_g_full, chip * es, es, axis=1)
    grad_b_ln_b = lax.dynamic_slice_in_dim(grad_ln_b_full, chip * es, es, axis=1)
    dmod_all = gathered.reshape(N_DEV, rows * 1024)[:, offs[10]:offs[10] + n_layer * 3 * d]
    dmod_all = dmod_all.reshape(N_DEV, n_layer, 3 * d)
    dmod_cols = jnp.transpose(lax.dynamic_slice_in_dim(dmod_all, chip * wd, wd, axis=2), (1, 0, 2))

    b_grads, b_lands = _reduce_wait(b_send, b_recv, b_grads, b_lands, b_kinds, summed, "reduce_b_wait")
    a_grads, a_lands = _reduce_wait(a_send, a_recv, a_grads, a_lands, a_kinds, summed, "reduce_a_wait")
    summed_halves = [
        _add_pieces(g, land, kind, chip_idx, core_idx, "add_pieces_" + nm)
        for g, land, kind, nm in [(a_grads[0], a_lands[0], "col", "a_w_in"), (b_grads[0], b_lands[0], "col", "b_w_in"),
                                  (a_grads[1], a_lands[1], "row", "a_w_out"), (b_grads[1], b_lands[1], "row", "b_w_out")]]
    g_a_w_in, g_b_w_in, g_a_w_out, g_b_w_out = _join_halves(summed_halves, "join_halves")

    grad_mod_w, delta_mod_w, new_m_mod_w, new_v_mod_w = _mod_w_update(
        jnp.transpose(c_act), dmod_cols, mod_w, m_mod_w, v_mod_w, "mod_w_update")
    upd = {}
    for nm, w, g, m, v in [("a_w_in", a_w_in, g_a_w_in, m_a_w_in, v_a_w_in), ("a_w_out", a_w_out, g_a_w_out, m_a_w_out, v_a_w_out),
                           ("b_w_in", b_w_in, g_b_w_in, m_b_w_in, v_b_w_in), ("b_w_out", b_w_out, g_b_w_out, m_b_w_out, v_b_w_out)]:
        upd[nm] = tuple(o[None] for o in _adamw_2d(w[0], g, m[0], v[0], "adamw_" + nm))
    small_w = [("mod_b", mod_b, grad_mod_b, m_mod_b, v_mod_b), ("norm_g", norm_g, grad_norm_g, m_norm_g, v_norm_g),
               ("a_conv_w", a_conv_w, grad_a_conv_w, m_a_conv_w, v_a_conv_w),
               ("a_conv_b", a_conv_b, grad_a_conv_b, m_a_conv_b, v_a_conv_b),
               ("b_ln_g", b_ln_g, grad_b_ln_g, m_b_ln_g, v_b_ln_g), ("b_ln_b", b_ln_b, grad_b_ln_b, m_b_ln_b, v_b_ln_b),
               ("b_w_s", b_w_s, grad_b_w_s, m_b_w_s, v_b_w_s), ("b_b_s", b_b_s, grad_b_b_s, m_b_b_s, v_b_b_s),
               ("final_g", final_g, grad_final_g, m_final_g, v_final_g)]

    def flat2d(a):
        return a.reshape(-1, a.shape[-1])

    res = _adamw_small([flat2d(t[1]) for t in small_w], [flat2d(t[2]) for t in small_w],
                       [flat2d(t[3]) for t in small_w], [flat2d(t[4]) for t in small_w], "adamw_small")
    for (nm, w, _, _, _), r3 in zip(small_w, res):
        upd[nm] = tuple(o.reshape(w.shape) for o in r3)
    upd["mod_w"] = (delta_mod_w, new_m_mod_w, new_v_mod_w)

    grads = {"mod_w": grad_mod_w, "mod_b": grad_mod_b, "norm_g": grad_norm_g, "a_w_in": g_a_w_in[None],
             "a_conv_w": grad_a_conv_w, "a_conv_b": grad_a_conv_b, "a_w_out": g_a_w_out[None], "b_w_in": g_b_w_in[None],
             "b_ln_g": grad_b_ln_g, "b_ln_b": grad_b_ln_b, "b_w_s": grad_b_w_s, "b_b_s": grad_b_b_s,
             "b_w_out": g_b_w_out[None], "final_g": grad_final_g}
    order = ["mod_w", "mod_b", "norm_g", "a_w_in", "a_conv_w", "a_conv_b", "a_w_out", "b_w_in", "b_ln_g", "b_ln_b",
             "b_w_s", "b_b_s", "b_w_out", "final_g"]
    return (loss, grad_x[None], *[grads[k] for k in order], *[upd[k][0] for k in order],
            *[upd[k][1] for k in order], *[upd[k][2] for k in order])
```

```python
import functools
import math

import jax
import jax.numpy as jnp
from jax import lax
from jax.experimental import pallas as pl
from jax.experimental.pallas import tpu as pltpu

F32 = jnp.float32
BF16 = jnp.bfloat16
MESH = pl.DeviceIdType.MESH

N_DEV = 8
N_CHIP = 4
SUBLANE = 8
RMS_EPS = 1e-6
LN_EPS = 1e-5
CHUNK = 128
GROUPS = 8
ADAM_LR = 0.001
ADAM_B1 = 0.9
ADAM_B2 = 0.999
ADAM_EPS = 1e-08
ADAM_WD = 0.01
ADAM_STEP = 10
VMEM_LIMIT = 56 << 20

HBM_SPEC = pl.BlockSpec(memory_space=pltpu.HBM)
VMEM_SPEC = pl.BlockSpec(memory_space=pltpu.VMEM)
SEM_SPEC = pl.BlockSpec(memory_space=pltpu.SEMAPHORE)
ANY_SPEC = pl.BlockSpec(memory_space=pl.ANY)
EFFECT = pltpu.SideEffectType.DATAFLOW_SIDE_EFFECTING


def _params(*sem):
    return pltpu.CompilerParams(dimension_semantics=sem, vmem_limit_bytes=VMEM_LIMIT)


def _tile(n, want):
    if n <= want:
        return n
    t = want
    while n % t:
        t -= 128
    return t


def _sigmoid(x):
    return 1.0 / (1.0 + jnp.exp(-x))


def _silu_and_grad(x):
    s = _sigmoid(x)
    return x * s, s * (1.0 + x * (1.0 - s))


def _gelu_and_grad(x):
    cdf = 0.5 * (1.0 + lax.erf(x * (1.0 / math.sqrt(2.0))))
    pdf = jnp.exp(-0.5 * x * x) * (1.0 / math.sqrt(2.0 * math.pi))
    return x * cdf, cdf + x * pdf


def _gelu(x):
    return x * (0.5 * (1.0 + lax.erf(x * (1.0 / math.sqrt(2.0)))))


def _rms(x):
    r = lax.rsqrt(jnp.mean(x * x, axis=-1, keepdims=True) + RMS_EPS)
    return x * r, r


def _rms_bwd(dxn, xn, r):
    return r * (dxn - xn * jnp.mean(dxn * xn, axis=-1, keepdims=True))


def _colsum(a):
    return jnp.sum(a, axis=0, keepdims=True)


def _shift_down(cur, before, k):
    rolled = pltpu.roll(cur, k, 0)
    row = lax.broadcasted_iota(jnp.int32, before.shape, 0)
    head = jnp.where(row < k, pltpu.roll(before, k, 0), rolled[:SUBLANE])
    return jnp.concatenate([head, rolled[SUBLANE:]], axis=0)


def _shift_up(cur, after, k):
    n = cur.shape[0]
    rolled = pltpu.roll(cur, n - k, 0)
    row = lax.broadcasted_iota(jnp.int32, after.shape, 0)
    tail = jnp.where(row >= SUBLANE - k, pltpu.roll(after, SUBLANE - k, 0), rolled[n - SUBLANE:])
    return jnp.concatenate([rolled[:n - SUBLANE], tail], axis=0)


def _after_spec():
    return pl.BlockSpec((SUBLANE, 128), lambda *_: (0, 0))


def _mm_proj(h, w, n_split, name, after):
    s, d = h.shape
    e = w.shape[1] // n_split
    tm, tn = _tile(s, 1024), _tile(e, 1024)
    nj = e // tn

    def body(h_ref, w_ref, after_ref, o_ref):
        o_ref[...] = jnp.dot(h_ref[...], w_ref[...], preferred_element_type=F32)

    return pl.pallas_call(
        body, name=name,
        out_shape=jax.ShapeDtypeStruct((n_split, s, e), F32),
        grid=(s // tm, n_split * nj),
        in_specs=[pl.BlockSpec((tm, d), lambda i, j: (i, 0)), pl.BlockSpec((d, tn), lambda i, j: (0, j)),
                  _after_spec()],
        out_specs=pl.BlockSpec((None, tm, tn), lambda i, j: (j // nj, i, j % nj)),
        compiler_params=_params("parallel", "parallel"),
    )(h, w, after)


def _mm_proj_slab(h, w, proj, q_idx, n_split, name, after):
    s, d = h.shape
    e = w.shape[1] // n_split
    tm, tn = _tile(s, 1024), _tile(e, 1024)
    nj = e // tn

    def body(q_ref, h_ref, w_ref, *rest):
        o_ref = rest[-1]
        o_ref[...] = jnp.dot(h_ref[...], w_ref[...], preferred_element_type=F32)

    in_specs = [pl.BlockSpec((tm, d), lambda i, j, qr: (i, 0)), pl.BlockSpec((d, tn), lambda i, j, qr: (0, qr[0] * nj + j)),
                _after_spec()]
    args = [q_idx, h, w, after]
    aliases = {}
    if proj is not None:
        in_specs.append(ANY_SPEC)
        args.append(proj)
        aliases = {4: 0}
    return pl.pallas_call(
        body, name=name,
        out_shape=jax.ShapeDtypeStruct((n_split, s, e), F32),
        grid_spec=pltpu.PrefetchScalarGridSpec(
            num_scalar_prefetch=1, grid=(s // tm, nj), in_specs=in_specs,
            out_specs=pl.BlockSpec((None, tm, tn), lambda i, j, qr: (qr[0], i, j))),
        input_output_aliases=aliases,
        compiler_params=_params("parallel", "parallel"),
    )(*args)


def _mm_nn(a, b, name):
    m, k = a.shape
    n = b.shape[1]
    tm, tn = _tile(m, 1024), _tile(n, 1024)

    def body(a_ref, b_ref, o_ref):
        o_ref[...] = jnp.dot(a_ref[...], b_ref[...], preferred_element_type=F32)

    return pl.pallas_call(
        body, name=name,
        out_shape=jax.ShapeDtypeStruct((m, n), F32),
        grid=(m // tm, n // tn),
        in_specs=[pl.BlockSpec((tm, k), lambda i, j: (i, 0)), pl.BlockSpec((k, tn), lambda i, j: (0, j))],
        out_specs=pl.BlockSpec((tm, tn), lambda i, j: (i, j)),
        compiler_params=_params("parallel", "parallel"),
    )(a, b)


def _mm_nt(a, b, name):
    m, k = a.shape
    n = b.shape[0]
    tm, tn = _tile(m, 1024), _tile(n, 1024)

    def body(a_ref, b_ref, o_ref):
        o_ref[...] = lax.dot_general(a_ref[...], b_ref[...], (((1,), (1,)), ((), ())), preferred_element_type=F32)

    return pl.pallas_call(
        body, name=name,
        out_shape=jax.ShapeDtypeStruct((m, n), F32),
        grid=(m // tm, n // tn),
        in_specs=[pl.BlockSpec((tm, k), lambda i, j: (i, 0)), pl.BlockSpec((tn, k), lambda i, j: (j, 0))],
        out_specs=pl.BlockSpec((tm, tn), lambda i, j: (i, j)),
        compiler_params=_params("parallel", "parallel"),
    )(a, b)


def _mm_dh(dp, w, name, after):
    nq, s, e = dp.shape
    d = w.shape[0]
    tm, tn, tk = _tile(s, 1024), _tile(d, 1024), _tile(e, 1024)
    nkq = e // tk
    nk = nq * nkq

    def body(a_ref, b_ref, after_ref, o_ref):
        @pl.when(pl.program_id(2) == 0)
        def _():
            o_ref[...] = jnp.zeros_like(o_ref)
        o_ref[...] += lax.dot_general(a_ref[...], b_ref[...], (((1,), (1,)), ((), ())), preferred_element_type=F32)

    return pl.pallas_call(
        body, name=name,
        out_shape=jax.ShapeDtypeStruct((s, d), F32),
        grid=(s // tm, d // tn, nk),
        in_specs=[pl.BlockSpec((None, tm, tk), lambda i, j, k: (k // nkq, i, k % nkq)),
                  pl.BlockSpec((tn, tk), lambda i, j, k: (j, k)), _after_spec()],
        out_specs=pl.BlockSpec((tm, tn), lambda i, j, k: (i, j)),
        compiler_params=_params("parallel", "parallel", "arbitrary"),
    )(dp, w, after)


def _dw_body(n_t):
    def body(a_ref, b_ref, o_ref, acc_ref):
        t = pl.program_id(2)

        @pl.when(t == 0)
        def _():
            acc_ref[...] = jnp.zeros_like(acc_ref)
        acc_ref[...] += lax.dot_general(a_ref[...], b_ref[...], (((0,), (0,)), ((), ())), preferred_element_type=F32)

        @pl.when(t == n_t - 1)
        def _():
            o_ref[...] = acc_ref[...].astype(o_ref.dtype)
    return body


def _mm_dw_in(h, dp, name):
    s, d = h.shape
    nq, _, e = dp.shape
    tm, tn, tt = _tile(d, 1024), _tile(e, 1024), _tile(s, 512)
    nj = e // tn

    return pl.pallas_call(
        _dw_body(s // tt), name=name,
        out_shape=jax.ShapeDtypeStruct((d, nq * e), BF16),
        grid=(d // tm, nq * nj, s // tt),
        in_specs=[pl.BlockSpec((tt, tm), lambda i, j, t: (t, i)),
                  pl.BlockSpec((None, tt, tn), lambda i, j, t: (j // nj, t, j % nj))],
        out_specs=pl.BlockSpec((tm, tn), lambda i, j, t: (i, j)),
        scratch_shapes=[pltpu.VMEM((tm, tn), F32)],
        compiler_params=_params("parallel", "parallel", "arbitrary"),
    )(h, dp)


def _mm_dw_out(y, dbr, name):
    s, e = y.shape
    d = dbr.shape[1]
    tm, tn, tt = _tile(e, 1024), _tile(d, 1024), _tile(s, 512)

    return pl.pallas_call(
        _dw_body(s // tt), name=name,
        out_shape=jax.ShapeDtypeStruct((e, d), BF16),
        grid=(e // tm, d // tn, s // tt),
        in_specs=[pl.BlockSpec((tt, tm), lambda i, j, t: (t, i)), pl.BlockSpec((tt, tn), lambda i, j, t: (t, j))],
        out_specs=pl.BlockSpec((tm, tn), lambda i, j, t: (i, j)),
        scratch_shapes=[pltpu.VMEM((tm, tn), F32)],
        compiler_params=_params("parallel", "parallel", "arbitrary"),
    )(y, dbr)


def _row_spec(ts, d):
    return pl.BlockSpec((ts, d), lambda i: (i, 0))


def _vec_spec(d):
    return pl.BlockSpec((1, d), lambda i: (0, 0))


def _norm_mod(x, g, scale, shift, name):
    s, d = x.shape
    ts = _tile(s, 512)

    def body(x_ref, g_ref, sc_ref, sh_ref, h_ref):
        xn, _ = _rms(x_ref[...])
        h_ref[...] = ((xn * g_ref[...]) * (1.0 + sc_ref[...]) + sh_ref[...]).astype(BF16)

    return pl.pallas_call(
        body, name=name, out_shape=jax.ShapeDtypeStruct((s, d), BF16), grid=(s // ts,),
        in_specs=[_row_spec(ts, d), _vec_spec(d), _vec_spec(d), _vec_spec(d)],
        out_specs=_row_spec(ts, d), compiler_params=_params("parallel"),
    )(x, g, scale, shift)


def _residual_norm_mod(x, br, gate, g, scale, shift, name):
    s, d = x.shape
    ts = _tile(s, 512)

    def body(x_ref, br_ref, gate_ref, g_ref, sc_ref, sh_ref, x1_ref, h_ref):
        x1 = x_ref[...] + gate_ref[...] * br_ref[...]
        x1_ref[...] = x1
        xn, _ = _rms(x1)
        h_ref[...] = ((xn * g_ref[...]) * (1.0 + sc_ref[...]) + sh_ref[...]).astype(BF16)

    return pl.pallas_call(
        body, name=name,
        out_shape=(jax.ShapeDtypeStruct((s, d), F32), jax.ShapeDtypeStruct((s, d), BF16)), grid=(s // ts,),
        in_specs=[_row_spec(ts, d), _row_spec(ts, d)] + [_vec_spec(d)] * 4,
        out_specs=(_row_spec(ts, d), _row_spec(ts, d)), compiler_params=_params("parallel"),
    )(x, br, gate, g, scale, shift)


def _head(x1, br, gate, gf, target, name):
    s, d = x1.shape
    ts = _tile(s, 512)

    def body(x1_ref, br_ref, gate_ref, gf_ref, tg_ref, dx_ref, dbr_ref, loss_ref, dgf_ref, dgate_ref):
        @pl.when(pl.program_id(0) == 0)
        def _():
            loss_ref[...] = jnp.zeros_like(loss_ref)
            dgf_ref[...] = jnp.zeros_like(dgf_ref)
            dgate_ref[...] = jnp.zeros_like(dgate_ref)
        br = br_ref[...]
        gate = gate_ref[...]
        gf = gf_ref[...]
        xn, r = _rms(x1_ref[...] + gate * br)
        err = xn * gf - tg_ref[...]
        loss_ref[...] += 0.5 * jnp.sum(jnp.mean(err * err, axis=-1, keepdims=True))
        dout = err * (1.0 / d)
        dgf_ref[...] += _colsum(dout * xn)
        dx = _rms_bwd(dout * gf, xn, r)
        dx_ref[...] = dx
        dgate_ref[...] += _colsum(dx * br)
        dbr_ref[...] = (gate * dx).astype(BF16)

    return pl.pallas_call(
        body, name=name,
        out_shape=(jax.ShapeDtypeStruct((s, d), F32), jax.ShapeDtypeStruct((s, d), BF16),
                   jax.ShapeDtypeStruct((SUBLANE, 128), F32), jax.ShapeDtypeStruct((1, d), F32),
                   jax.ShapeDtypeStruct((1, d), F32)),
        grid=(s // ts,),
        in_specs=[_row_spec(ts, d), _row_spec(ts, d), _vec_spec(d), _vec_spec(d), _row_spec(ts, d)],
        out_specs=(_row_spec(ts, d), _row_spec(ts, d), pl.BlockSpec((SUBLANE, 128), lambda i: (0, 0)),
                   _vec_spec(d), _vec_spec(d)),
        compiler_params=_params("arbitrary"),
    )(x1, br, gate, gf, target)


def _norm_mod_bwd(dh, x, dx_in, g, scale, name, br=None, gate=None):
    s, d = x.shape
    ts = _tile(s, 512)
    has_branch = br is not None

    def body(*refs):
        if has_branch:
            (dh_ref, x_ref, dxin_ref, g_ref, sc_ref, br_ref, gate_ref,
             dx_ref, dsh_ref, dsc_ref, dg_ref, dbr_ref, dgate_ref) = refs
        else:
            dh_ref, x_ref, dxin_ref, g_ref, sc_ref, dx_ref, dsh_ref, dsc_ref, dg_ref = refs

        @pl.when(pl.program_id(0) == 0)
        def _():
            dsh_ref[...] = jnp.zeros_like(dsh_ref)
            dsc_ref[...] = jnp.zeros_like(dsc_ref)
            dg_ref[...] = jnp.zeros_like(dg_ref)
            if has_branch:
                dgate_ref[...] = jnp.zeros_like(dgate_ref)
        dh = dh_ref[...]
        g = g_ref[...]
        xn, r = _rms(x_ref[...])
        dsh_ref[...] += _colsum(dh)
        dsc_ref[...] += _colsum(dh * (xn * g))
        da = dh * (1.0 + sc_ref[...])
        dg_ref[...] += _colsum(da * xn)
        dx = dxin_ref[...] + _rms_bwd(da * g, xn, r)
        dx_ref[...] = dx
        if has_branch:
            dgate_ref[...] += _colsum(dx * br_ref[...])
            dbr_ref[...] = (gate_ref[...] * dx).astype(BF16)

    vec_out = jax.ShapeDtypeStruct((1, d), F32)
    out_shape = [jax.ShapeDtypeStruct((s, d), F32), vec_out, vec_out, vec_out]
    out_specs = [_row_spec(ts, d), _vec_spec(d), _vec_spec(d), _vec_spec(d)]
    in_specs = [_row_spec(ts, d)] * 3 + [_vec_spec(d)] * 2
    args = [dh, x, dx_in, g, scale]
    if has_branch:
        out_shape += [jax.ShapeDtypeStruct((s, d), BF16), vec_out]
        out_specs += [_row_spec(ts, d), _vec_spec(d)]
        in_specs += [_row_spec(ts, d), _vec_spec(d)]
        args += [br, gate]
    return pl.pallas_call(
        body, name=name, out_shape=tuple(out_shape), grid=(s // ts,),
        in_specs=in_specs, out_specs=tuple(out_specs), compiler_params=_params("arbitrary"),
    )(*args)


def _conv_tiles(s, e):
    return _tile(s, 512), _tile(e, 512)


def _conv_fwd(proj, conv_w, conv_b, name):
    _, s, e = proj.shape
    ts, te = _conv_tiles(s, e)
    hb = ts // SUBLANE

    def body(bg_ref, cg_ref, xi_ref, z_ref, cgp_ref, xip_ref, w0_ref, w1_ref, w2_ref, b_ref, y_ref):
        cx = cg_ref[...] * xi_ref[...]
        before = jnp.where(pl.program_id(0) > 0, cgp_ref[...] * xip_ref[...], 0.0)
        conv = b_ref[...] + w2_ref[...] * cx
        conv = conv + w0_ref[...] * _shift_down(cx, before, 2)
        conv = conv + w1_ref[...] * _shift_down(cx, before, 1)
        z = z_ref[...]
        y_ref[...] = ((z * _sigmoid(z)) * bg_ref[...] * conv).astype(BF16)

    def part(q):
        return pl.BlockSpec((None, ts, te), lambda i, j: (q, i, j))

    def halo_before(q):
        return pl.BlockSpec((None, SUBLANE, te), lambda i, j: (q, jnp.maximum(i * hb - 1, 0), j))

    return pl.pallas_call(
        body, name=name, out_shape=jax.ShapeDtypeStruct((s, e), BF16), grid=(s // ts, e // te),
        in_specs=[part(0), part(1), part(2), part(3), halo_before(1), halo_before(2)]
        + [pl.BlockSpec((1, te), lambda i, j: (0, j))] * 4,
        out_specs=pl.BlockSpec((ts, te), lambda i, j: (i, j)),
        compiler_params=_params("parallel", "parallel"),
    )(proj, proj, proj, proj, proj, proj, *conv_w, conv_b)


def _conv_bwd(proj, dy, conv_w, conv_b, name):
    _, s, e = proj.shape
    ts, te = _conv_tiles(s, e)
    hb = ts // SUBLANE
    n_i = s // ts
    last_halo = s // SUBLANE - 1

    def body(bg_ref, cg_ref, xi_ref, z_ref, dy_ref, cgp_ref, xip_ref, bgn_ref, zn_ref, dyn_ref,
             w0_ref, w1_ref, w2_ref, b_ref, dp_ref, dw0_ref, dw1_ref, dw2_ref, db_ref):
        i = pl.program_id(1)

        @pl.when(i == 0)
        def _():
            for acc in (dw0_ref, dw1_ref, dw2_ref, db_ref):
                acc[...] = jnp.zeros_like(acc)
        bg, cg, xi, z, dy = bg_ref[...], cg_ref[...], xi_ref[...], z_ref[...], dy_ref[...]
        w0, w1, w2 = w0_ref[...], w1_ref[...], w2_ref[...]
        cx = cg * xi
        before = jnp.where(i > 0, cgp_ref[...] * xip_ref[...], 0.0)
        cx1 = _shift_down(cx, before, 1)
        cx2 = _shift_down(cx, before, 2)
        conv = b_ref[...] + w2 * cx
        conv = conv + w0 * cx2
        conv = conv + w1 * cx1
        sz, dsz = _silu_and_grad(z)
        dp_ref[3] = (dy * bg * conv * dsz).astype(BF16)
        dp_ref[0] = (dy * sz * conv).astype(BF16)
        dconv = dy * sz * bg
        zn = zn_ref[...]
        after = jnp.where(i < n_i - 1, dyn_ref[...] * (zn * _sigmoid(zn)) * bgn_ref[...], 0.0)
        db_ref[...] += _colsum(dconv)
        dw2_ref[...] += _colsum(dconv * cx)
        dw1_ref[...] += _colsum(dconv * cx1)
        dw0_ref[...] += _colsum(dconv * cx2)
        dcx = w2 * dconv + w1 * _shift_up(dconv, after, 1) + w0 * _shift_up(dconv, after, 2)
        dp_ref[1] = (dcx * xi).astype(BF16)
        dp_ref[2] = (dcx * cg).astype(BF16)

    def part(q):
        return pl.BlockSpec((None, ts, te), lambda j, i: (q, i, j))

    def halo_before(q):
        return pl.BlockSpec((None, SUBLANE, te), lambda j, i: (q, jnp.maximum(i * hb - 1, 0), j))

    def halo_after(q):
        return pl.BlockSpec((None, SUBLANE, te), lambda j, i: (q, jnp.minimum((i + 1) * hb, last_halo), j))

    return pl.pallas_call(
        body, name=name,
        out_shape=(jax.ShapeDtypeStruct((4, s, e), BF16),) + (jax.ShapeDtypeStruct((1, e), F32),) * 4,
        grid=(e // te, n_i),
        in_specs=[part(0), part(1), part(2), part(3), pl.BlockSpec((ts, te), lambda j, i: (i, j)),
                  halo_before(1), halo_before(2), halo_after(0), halo_after(3),
                  pl.BlockSpec((SUBLANE, te), lambda j, i: (jnp.minimum((i + 1) * hb, last_halo), j))]
        + [pl.BlockSpec((1, te), lambda j, i: (0, j))] * 4,
        out_specs=(pl.BlockSpec((4, ts, te), lambda j, i: (0, i, j)),) + (pl.BlockSpec((1, te), lambda j, i: (0, j)),) * 4,
        compiler_params=_params("parallel", "arbitrary"),
    )(proj, proj, proj, proj, dy, proj, proj, proj, proj, dy, *conv_w, conv_b)


def _tril(w):
    row = lax.broadcasted_iota(jnp.int32, w.shape, 0)
    col = lax.broadcasted_iota(jnp.int32, w.shape, 1)
    return jnp.where(row >= col, w, 0.0)


def _triu(w):
    row = lax.broadcasted_iota(jnp.int32, w.shape, 0)
    col = lax.broadcasted_iota(jnp.int32, w.shape, 1)
    return jnp.where(row <= col, w, 0.0)


def _layer_norm_fwd(v, g, b):
    mu = jnp.mean(v, axis=-1, keepdims=True)
    vc = v - mu
    rstd = lax.rsqrt(jnp.mean(vc * vc, axis=-1, keepdims=True) + LN_EPS)
    vhat = vc * rstd
    return vhat * g + b, vhat, rstd


def _gmlp_fwd(proj, ln_g, ln_b, w_s, b_s_t, name):
    _, s, e = proj.shape
    gw = e // GROUPS

    def body(pu_ref, pv_ref, pz_ref, g_ref, b_ref, ws_ref, bs_ref, y_ref, vn_scr, mix_scr):
        vn, _, _ = _layer_norm_fwd(_gelu(pv_ref[...]), g_ref[...], b_ref[...])
        vn_scr[...] = vn.astype(BF16)
        for gi in range(GROUPS):
            cols = slice(gi * gw, (gi + 1) * gw)
            wt = _tril(ws_ref[gi]).astype(BF16)
            mix_scr[:, cols] = jnp.dot(wt, vn_scr[:, cols], preferred_element_type=F32) + bs_ref[:, gi:gi + 1]
        z = pz_ref[...]
        y_ref[...] = ((z * _sigmoid(z)) * (_gelu(pu_ref[...]) * mix_scr[...])).astype(BF16)

    def part(q):
        return pl.BlockSpec((None, CHUNK, e), lambda i: (q, i, 0))

    return pl.pallas_call(
        body, name=name, out_shape=jax.ShapeDtypeStruct((s, e), BF16), grid=(s // CHUNK,),
        in_specs=[part(0), part(1), part(2), _vec_spec(e), _vec_spec(e),
                  pl.BlockSpec((GROUPS, CHUNK, CHUNK), lambda i: (0, 0, 0)),
                  pl.BlockSpec((CHUNK, GROUPS), lambda i: (0, 0))],
        out_specs=pl.BlockSpec((CHUNK, e), lambda i: (i, 0)),
        scratch_shapes=[pltpu.VMEM((CHUNK, e), BF16), pltpu.VMEM((CHUNK, e), F32)],
        compiler_params=_params("parallel"),
    )(proj, proj, proj, ln_g, ln_b, w_s, b_s_t)


def _gmlp_bwd(proj, dy, ln_g, ln_b, w_s, w_s_t, b_s_t, name):
    _, s, e = proj.shape
    gw = e // GROUPS
    n_i = s // CHUNK

    def body(pu_ref, pv_ref, pz_ref, dy_ref, g_ref, b_ref, ws_ref, wst_ref, bs_ref,
             dp_ref, dws_ref, dbs_ref, dlg_ref, dlb_ref, vn_scr, mix_scr, dm_scr, dvn_scr, dmacc_scr):
        i = pl.program_id(0)

        @pl.when(i == 0)
        def _():
            dws_ref[...] = jnp.zeros_like(dws_ref)
            dlg_ref[...] = jnp.zeros_like(dlg_ref)
            dlb_ref[...] = jnp.zeros_like(dlb_ref)
            dmacc_scr[...] = jnp.zeros_like(dmacc_scr)
        ln_g = g_ref[...]
        u, du_dpu = _gelu_and_grad(pu_ref[...])
        v, dv_dpv = _gelu_and_grad(pv_ref[...])
        vn, vhat, rstd = _layer_norm_fwd(v, ln_g, b_ref[...])
        vn_scr[...] = vn.astype(BF16)
        for gi in range(GROUPS):
            cols = slice(gi * gw, (gi + 1) * gw)
            wt = _tril(ws_ref[gi]).astype(BF16)
            mix_scr[:, cols] = jnp.dot(wt, vn_scr[:, cols], preferred_element_type=F32) + bs_ref[:, gi:gi + 1]
        mixed = mix_scr[...]
        dy = dy_ref[...]
        sz, dsz = _silu_and_grad(pz_ref[...])
        ds = dy * sz
        dp_ref[2] = (dy * (u * mixed) * dsz).astype(BF16)
        dp_ref[0] = (ds * mixed * du_dpu).astype(BF16)
        dm = ds * u
        dmacc_scr[...] += dm
        dm_scr[...] = dm.astype(BF16)
        for gi in range(GROUPS):
            cols = slice(gi * gw, (gi + 1) * gw)
            dmg = dm_scr[:, cols]
            dws_ref[gi] += lax.dot_general(dmg, vn_scr[:, cols], (((1,), (1,)), ((), ())), preferred_element_type=F32)
            wtt = _triu(wst_ref[gi]).astype(BF16)
            dvn_scr[:, cols] = jnp.dot(wtt, dmg, preferred_element_type=F32)
        dvn = dvn_scr[...]
        dlg_ref[...] += _colsum(dvn * vhat)
        dlb_ref[...] += _colsum(dvn)
        dvh = dvn * ln_g
        dv = rstd * (dvh - jnp.mean(dvh, axis=-1, keepdims=True) - vhat * jnp.mean(dvh * vhat, axis=-1, keepdims=True))
        dp_ref[1] = (dv * dv_dpv).astype(BF16)

        @pl.when(i == n_i - 1)
        def _():
            for gi in range(GROUPS):
                dws_ref[gi] = _tril(dws_ref[gi])
                dbs_ref[:, gi:gi + 1] = jnp.sum(dmacc_scr[:, gi * gw:(gi + 1) * gw], axis=1, keepdims=True)

    def part(q):
        return pl.BlockSpec((None, CHUNK, e), lambda i: (q, i, 0))

    w_spec = pl.BlockSpec((GROUPS, CHUNK, CHUNK), lambda i: (0, 0, 0))
    bs_spec = pl.BlockSpec((CHUNK, GROUPS), lambda i: (0, 0))
    return pl.pallas_call(
        body, name=name,
        out_shape=(jax.ShapeDtypeStruct((3, s, e), BF16), jax.ShapeDtypeStruct((GROUPS, CHUNK, CHUNK), F32),
                   jax.ShapeDtypeStruct((CHUNK, GROUPS), F32), jax.ShapeDtypeStruct((1, e), F32),
                   jax.ShapeDtypeStruct((1, e), F32)),
        grid=(n_i,),
        in_specs=[part(0), part(1), part(2), pl.BlockSpec((CHUNK, e), lambda i: (i, 0)), _vec_spec(e), _vec_spec(e),
                  w_spec, w_spec, bs_spec],
        out_specs=(pl.BlockSpec((3, CHUNK, e), lambda i: (0, i, 0)), w_spec, bs_spec, _vec_spec(e), _vec_spec(e)),
        scratch_shapes=[pltpu.VMEM((CHUNK, e), BF16), pltpu.VMEM((CHUNK, e), F32), pltpu.VMEM((CHUNK, e), BF16),
                        pltpu.VMEM((CHUNK, e), F32), pltpu.VMEM((CHUNK, e), F32)],
        compiler_params=_params("arbitrary"),
    )(proj, proj, proj, dy, ln_g, ln_b, w_s, w_s_t, b_s_t)


def _mod_fwd(c_all, mod_w, mod_b_cols, name):
    n_layer, d, w = mod_w.shape

    def body(c_ref, w_ref, b_ref, ca_ref, o_ref):
        c = c_ref[...]
        ca = c * _sigmoid(c)
        ca_ref[...] = ca
        for li in range(n_layer):
            o_ref[li * N_DEV:(li + 1) * N_DEV, :] = (
                jnp.dot(ca, w_ref[li], preferred_element_type=F32, precision=lax.Precision.HIGHEST) + b_ref[li])

    return pl.pallas_call(
        body, name=name,
        out_shape=(jax.ShapeDtypeStruct((N_DEV, d), F32), jax.ShapeDtypeStruct((n_layer * N_DEV, w), F32)),
        in_specs=[VMEM_SPEC] * 3, out_specs=(VMEM_SPEC, VMEM_SPEC),
        compiler_params=pltpu.CompilerParams(vmem_limit_bytes=VMEM_LIMIT),
    )(c_all, mod_w, mod_b_cols)


def _adamw(w, g, m, v):
    m = ADAM_B1 * m + (1.0 - ADAM_B1) * g
    v = ADAM_B2 * v + (1.0 - ADAM_B2) * (g * g)
    m_hat = m / (1.0 - ADAM_B1 ** ADAM_STEP)
    v_hat = v / (1.0 - ADAM_B2 ** ADAM_STEP)
    delta = -ADAM_LR * (m_hat / (jnp.sqrt(v_hat) + ADAM_EPS) + ADAM_WD * w)
    return delta, m, v


def _adamw_2d(w, g, m, v, name):
    r, c = w.shape
    tr, tc = _tile(r, 512), _tile(c, 1024)

    def body(w_ref, g_ref, m_ref, v_ref, d_ref, nm_ref, nv_ref):
        d_ref[...], nm_ref[...], nv_ref[...] = _adamw(w_ref[...], g_ref[...], m_ref[...], v_ref[...])

    spec = pl.BlockSpec((tr, tc), lambda i, j: (i, j))
    shape = jax.ShapeDtypeStruct((r, c), F32)
    return pl.pallas_call(
        body, name=name, out_shape=(shape, shape, shape), grid=(r // tr, c // tc),
        in_specs=[spec] * 4, out_specs=(spec, spec, spec), compiler_params=_params("parallel", "parallel"),
    )(w, g, m, v)


def _mod_w_update(ca_t, dmod_cols, w, m, v, name):
    n_layer, d, wd = w.shape
    tr = _tile(d, 256)

    def body(ca_ref, dm_ref, w_ref, m_ref, v_ref, g_ref, d_ref, nm_ref, nv_ref):
        ca = ca_ref[...]
        dm = dm_ref[...]
        g = ca[:, 0:1] * dm[0:1, :]
        for b in range(1, N_DEV):
            g = g + ca[:, b:b + 1] * dm[b:b + 1, :]
        g_ref[...] = g
        d_ref[...], nm_ref[...], nv_ref[...] = _adamw(w_ref[...], g, m_ref[...], v_ref[...])

    spec = pl.BlockSpec((None, tr, wd), lambda l, i: (l, i, 0))
    shape = jax.ShapeDtypeStruct((n_layer, d, wd), F32)
    return pl.pallas_call(
        body, name=name, out_shape=(shape,) * 4, grid=(n_layer, d // tr),
        in_specs=[pl.BlockSpec((tr, N_DEV), lambda l, i: (i, 0)), pl.BlockSpec((None, N_DEV, wd), lambda l, i: (l, 0, 0)),
                  spec, spec, spec],
        out_specs=(spec,) * 4, compiler_params=_params("parallel", "parallel"),
    )(ca_t, dmod_cols, w, m, v)


def _adamw_small(ws, gs, ms, vs, name):
    n = len(ws)

    def body(*refs):
        ins, outs = refs[:4 * n], refs[4 * n:]
        for k in range(n):
            delta, nm, nv = _adamw(ins[k][...], ins[n + k][...], ins[2 * n + k][...], ins[3 * n + k][...])
            outs[3 * k][...] = delta
            outs[3 * k + 1][...] = nm
            outs[3 * k + 2][...] = nv

    out_shape = []
    for w in ws:
        out_shape += [jax.ShapeDtypeStruct(w.shape, F32)] * 3
    outs = pl.pallas_call(
        body, name=name, out_shape=tuple(out_shape),
        in_specs=[VMEM_SPEC] * (4 * n), out_specs=tuple([VMEM_SPEC] * (3 * n)),
        compiler_params=pltpu.CompilerParams(vmem_limit_bytes=VMEM_LIMIT),
    )(*ws, *gs, *ms, *vs)
    return [tuple(outs[3 * k:3 * k + 3]) for k in range(n)]


def _place():
    return lax.axis_index("x"), lax.axis_index("y"), lax.axis_index("c")


def _other_chips(x, y):
    return [(1 - x, y), (x, 1 - y), (1 - x, 1 - y)]


def _all_gather(block, name):
    m_per, n = block.shape

    def body(x_ref, out_ref, send_sems, recv_sems, local_sem):
        x, y, c = _place()
        me, sibling = (x, y, c), (x, y, 1 - c)
        chips = _other_chips(x, y)

        def rows(px, py, pc):
            return out_ref.at[pl.ds((4 * px + 2 * py + pc) * m_per, m_per), :]

        def copy(k, blk, to, src=None):
            return pltpu.make_async_remote_copy(
                src_ref=rows(*blk) if src is None else src, dst_ref=rows(*blk),
                send_sem=send_sems.at[k], recv_sem=recv_sems.at[k], device_id=to, device_id_type=MESH)

        mine = pltpu.make_async_copy(x_ref, rows(*me), local_sem)
        mine.start()
        first = [copy(0, me, sibling, src=x_ref)]
        first += [copy(1 + j, me, (*chip, c), src=x_ref) for j, chip in enumerate(chips)]
        for cp in first:
            cp.start()
        passed = [copy(4 + j, (*chip, c), sibling) for j, chip in enumerate(chips)]
        for j, chip in enumerate(chips):
            copy(1 + j, (*chip, c), me).wait_recv()
            passed[j].start()
        copy(0, sibling, me).wait_recv()
        for j, chip in enumerate(chips):
            copy(4 + j, (*chip, 1 - c), me).wait_recv()
        for cp in first + passed:
            cp.wait_send()
        mine.wait()

    return pl.pallas_call(
        body, name=name, out_shape=jax.ShapeDtypeStruct((N_DEV * m_per, n), F32),
        in_specs=[VMEM_SPEC], out_specs=VMEM_SPEC,
        scratch_shapes=[pltpu.SemaphoreType.DMA((7,)), pltpu.SemaphoreType.DMA((7,)), pltpu.SemaphoreType.DMA],
        compiler_params=pltpu.CompilerParams(vmem_limit_bytes=VMEM_LIMIT),
    )(block)


def _hbm(a):
    return pltpu.with_memory_space_constraint(a, pltpu.HBM)


def _place_own(shard, kind, chip_idx, name):
    r, cdim = shard.shape
    tr, tc = _tile(r, 512), _tile(cdim, 1024)
    nrb, ncb = r // tr, cdim // tc

    def body(k_ref, s_ref, o_ref):
        o_ref[...] = s_ref[...].astype(BF16)

    if kind == "col":
        full, o_map = (r, N_CHIP * cdim), lambda i, j, kr: (i, kr[0] * ncb + j)
    else:
        full, o_map = (N_CHIP * r, cdim), lambda i, j, kr: (kr[0] * nrb + i, j)
    return pl.pallas_call(
        body, name=name, out_shape=jax.ShapeDtypeStruct(full, BF16),
        grid_spec=pltpu.PrefetchScalarGridSpec(
            num_scalar_prefetch=1, grid=(nrb, ncb),
            in_specs=[pl.BlockSpec((tr, tc), lambda i, j, kr: (i, j))],
            out_specs=pl.BlockSpec((tr, tc), o_map)),
        compiler_params=_params("parallel", "parallel"),
    )(chip_idx, shard)


def _weight_window(ref, kind, shard_shape, k, half):
    r, cdim = shard_shape
    hr = r // 2
    if kind == "col":
        return ref.at[pl.ds(half * hr, hr), pl.ds(pl.multiple_of(k * cdim, 128), cdim)]
    return ref.at[pl.ds(pl.multiple_of(k * r + half * hr, 2 * SUBLANE), hr), :]


def _gather_weights_start(fulls, kinds, shard_shapes, groups, name):
    n, ng = len(fulls), len(groups)

    def body(*refs):
        ins = refs[:n]
        sems = refs[n:n + 2 * ng]
        token = refs[2 * n + 2 * ng]
        x, y, c = _place()
        chips = _other_chips(x, y)
        for g, group in enumerate(groups):
            for pos, (w, j) in enumerate(group):
                own = _weight_window(ins[w], kinds[w], shard_shapes[w], 2 * x + y, c)
                for cc in range(2):
                    pltpu.make_async_remote_copy(
                        src_ref=own, dst_ref=own,
                        send_sem=sems[2 * g].at[2 * pos + cc], recv_sem=sems[2 * g + 1].at[2 * pos + c],
                        device_id=(*chips[j], cc), device_id_type=MESH).start()
        token[...] = jnp.zeros_like(token)

    sem_shapes = []
    for group in groups:
        sem_shapes += [pltpu.SemaphoreType.DMA((2 * len(group),))] * 2
    outs = pl.pallas_call(
        body, name=name,
        out_shape=tuple(sem_shapes) + tuple(pltpu.HBM(f.shape, f.dtype) for f in fulls)
        + (jax.ShapeDtypeStruct((SUBLANE, 128), F32),),
        in_specs=[HBM_SPEC] * n, out_specs=(SEM_SPEC,) * (2 * ng) + (HBM_SPEC,) * n + (VMEM_SPEC,),
        input_output_aliases={w: 2 * ng + w for w in range(n)},
        compiler_params=pltpu.CompilerParams(has_side_effects=EFFECT),
    )(*[_hbm(f) for f in fulls])
    sems = [(outs[2 * g], outs[2 * g + 1]) for g in range(ng)]
    return sems, list(outs[2 * ng:2 * ng + n]), outs[2 * ng + n]


def _gather_weights_wait(sems, group, fulls, kinds, shard_shapes, after, name):
    n = len(fulls)

    def body(*refs):
        ins = refs[:n]
        send_sems, recv_sems = refs[n], refs[n + 1]
        x, y, c = _place()
        chips = _other_chips(x, y)
        for pos, (w, j) in enumerate(group):
            own = _weight_window(ins[w], kinds[w], shard_shapes[w], 2 * x + y, c)
            for cc in range(2):
                landed = _weight_window(ins[w], kinds[w], shard_shapes[w], 2 * chips[j][0] + chips[j][1], cc)
                k = 2 * pos + cc
                pltpu.make_async_remote_copy(
                    src_ref=own, dst_ref=own, send_sem=send_sems.at[k], recv_sem=recv_sems.at[k],
                    device_id=(*chips[j], cc), device_id_type=MESH).wait_send()
                pltpu.make_async_remote_copy(
                    src_ref=landed, dst_ref=landed, send_sem=send_sems.at[k], recv_sem=recv_sems.at[k],
                    device_id=(*chips[j], cc), device_id_type=MESH).wait_recv()

    return list(pl.pallas_call(
        body, name=name, out_shape=tuple(pltpu.HBM(f.shape, f.dtype) for f in fulls),
        in_specs=[HBM_SPEC] * n + [SEM_SPEC, SEM_SPEC, ANY_SPEC], out_specs=(HBM_SPEC,) * n,
        input_output_aliases={w: w for w in range(n)},
        compiler_params=pltpu.CompilerParams(has_side_effects=EFFECT),
    )(*fulls, sems[0], sems[1], after))


def _grad_piece(ref, kind, h, cdim, k, half):
    if kind == "col":
        return ref.at[pl.ds(half * h, h), pl.ds(pl.multiple_of(k * cdim, 128), cdim)]
    return ref.at[k, pl.ds(half * h, h), :]


def _grad_dims(g, kind):
    return (g.shape[0] // 2, g.shape[1] // N_CHIP) if kind == "col" else (g.shape[1] // 2, g.shape[2])


def _reduce_start(grads, kinds, name):
    n = len(grads)
    dims = [_grad_dims(g, kind) for g, kind in zip(grads, kinds)]
    lands = [lax.empty((N_DEV - 1, h, cdim), g.dtype) for g, (h, cdim) in zip(grads, dims)]

    def body(*refs):
        g_ins, land_ins = refs[:n], refs[n:2 * n]
        send_sems, recv_sems = refs[2 * n], refs[2 * n + 1]
        token = refs[4 * n + 2]
        x, y, c = _place()
        for w in range(n):
            h, cdim = dims[w]
            base = (N_DEV - 1) * w
            pltpu.make_async_remote_copy(
                src_ref=_grad_piece(g_ins[w], kinds[w], h, cdim, 2 * x + y, 1 - c), dst_ref=land_ins[w].at[0],
                send_sem=send_sems.at[base], recv_sem=recv_sems.at[base],
                device_id=(x, y, 1 - c), device_id_type=MESH).start()
            for j, chip in enumerate(_other_chips(x, y)):
                for cc in range(2):
                    pltpu.make_async_remote_copy(
                        src_ref=_grad_piece(g_ins[w], kinds[w], h, cdim, 2 * chip[0] + chip[1], cc),
                        dst_ref=land_ins[w].at[1 + 2 * j + c],
                        send_sem=send_sems.at[base + 1 + 2 * j + cc], recv_sem=recv_sems.at[base + 1 + 2 * j + c],
                        device_id=(*chip, cc), device_id_type=MESH).start()
        token[...] = jnp.zeros_like(token)

    sems = pltpu.SemaphoreType.DMA(((N_DEV - 1) * n,))
    outs = pl.pallas_call(
        body, name=name,
        out_shape=(sems, sems) + tuple(pltpu.HBM(a.shape, a.dtype) for a in list(grads) + lands)
        + (jax.ShapeDtypeStruct((SUBLANE, 128), F32),),
        in_specs=[HBM_SPEC] * (2 * n), out_specs=(SEM_SPEC, SEM_SPEC) + (HBM_SPEC,) * (2 * n) + (VMEM_SPEC,),
        input_output_aliases={i: 2 + i for i in range(2 * n)},
        compiler_params=pltpu.CompilerParams(has_side_effects=EFFECT),
    )(*[_hbm(a) for a in list(grads) + lands])
    return outs[0], outs[1], list(outs[2:2 + n]), list(outs[2 + n:2 + 2 * n]), outs[2 + 2 * n]


def _reduce_wait(send_sems, recv_sems, grads, lands, kinds, after, name):
    n = len(grads)
    dims = [_grad_dims(g, kind) for g, kind in zip(grads, kinds)]

    def body(*refs):
        g_ins, land_ins = refs[:n], refs[n:2 * n]
        send_sems, recv_sems = refs[2 * n], refs[2 * n + 1]
        x, y, c = _place()
        for w in range(n):
            h, cdim = dims[w]
            piece = _grad_piece(g_ins[w], kinds[w], h, cdim, 2 * x + y, c)
            for s in range(N_DEV - 1):
                k = (N_DEV - 1) * w + s
                slot = land_ins[w].at[s]
                pltpu.make_async_remote_copy(
                    src_ref=piece, dst_ref=slot, send_sem=send_sems.at[k], recv_sem=recv_sems.at[k],
                    device_id=(x, y, 1 - c), device_id_type=MESH).wait_send()
                pltpu.make_async_remote_copy(
                    src_ref=piece, dst_ref=slot, send_sem=send_sems.at[k], recv_sem=recv_sems.at[k],
                    device_id=(x, y, 1 - c), device_id_type=MESH).wait_recv()

    outs = pl.pallas_call(
        body, name=name, out_shape=tuple(pltpu.HBM(a.shape, a.dtype) for a in list(grads) + list(lands)),
        in_specs=[HBM_SPEC] * (2 * n) + [SEM_SPEC, SEM_SPEC, ANY_SPEC], out_specs=(HBM_SPEC,) * (2 * n),
        input_output_aliases={i: i for i in range(2 * n)},
        compiler_params=pltpu.CompilerParams(has_side_effects=EFFECT),
    )(*grads, *lands, send_sems, recv_sems, after)
    return list(outs[:n]), list(outs[n:])


def _add_pieces(g, land, kind, chip_idx, core_idx, name):
    _, h, cdim = land.shape
    tr, tc = _tile(h, 256), _tile(cdim, 2048)
    nrb, ncb = h // tr, cdim // tc

    def body(k_ref, c_ref, g_ref, l_ref, o_ref):
        acc = g_ref[...].astype(F32)
        for s in range(N_DEV - 1):
            acc = acc + l_ref[s].astype(F32)
        o_ref[...] = acc

    if kind == "col":
        g_spec = pl.BlockSpec((tr, tc), lambda i, j, kr, cr: (cr[0] * nrb + i, kr[0] * ncb + j))
    else:
        g_spec = pl.BlockSpec((None, tr, tc), lambda i, j, kr, cr: (kr[0], cr[0] * nrb + i, j))
    return pl.pallas_call(
        body, name=name, out_shape=jax.ShapeDtypeStruct((2 * h, cdim), F32),
        grid_spec=pltpu.PrefetchScalarGridSpec(
            num_scalar_prefetch=2, grid=(nrb, ncb),
            in_specs=[g_spec, pl.BlockSpec((N_DEV - 1, tr, tc), lambda i, j, kr, cr: (0, i, j))],
            out_specs=pl.BlockSpec((tr, tc), lambda i, j, kr, cr: (cr[0] * nrb + i, j))),
        compiler_params=_params("parallel", "parallel"),
    )(chip_idx, core_idx, g, land)


def _join_halves(shards, name):
    n = len(shards)

    def body(*refs):
        ins, outs = refs[:n], refs[n:2 * n]
        send_sems, recv_sems = refs[2 * n:]
        x, y, c = _place()
        copies = []
        for w in range(n):
            h = shards[w].shape[0] // 2
            cp = pltpu.make_async_remote_copy(
                src_ref=ins[w].at[pl.ds(c * h, h), :], dst_ref=outs[w].at[pl.ds(c * h, h), :],
                send_sem=send_sems.at[w], recv_sem=recv_sems.at[w], device_id=(x, y, 1 - c), device_id_type=MESH)
            cp.start()
            copies.append(cp)
        for w in range(n):
            h = shards[w].shape[0] // 2
            theirs = outs[w].at[pl.ds((1 - c) * h, h), :]
            pltpu.make_async_remote_copy(
                src_ref=theirs, dst_ref=theirs, send_sem=send_sems.at[w], recv_sem=recv_sems.at[w],
                device_id=(x, y, 1 - c), device_id_type=MESH).wait_recv()
        for cp in copies:
            cp.wait_send()

    return pl.pallas_call(
        body, name=name, out_shape=tuple(jax.ShapeDtypeStruct(s.shape, s.dtype) for s in shards),
        in_specs=[HBM_SPEC] * n, out_specs=tuple([HBM_SPEC] * n),
        input_output_aliases={w: w for w in range(n)},
        scratch_shapes=[pltpu.SemaphoreType.DMA((n,))] * 2,
    )(*shards)


def _sum_devices(gathered, rows, name):
    n = gathered.shape[1]

    def body(g_ref, o_ref):
        acc = g_ref[0:rows, :]
        for dev in range(1, N_DEV):
            acc = acc + g_ref[dev * rows:(dev + 1) * rows, :]
        o_ref[...] = acc

    return pl.pallas_call(
        body, name=name, out_shape=jax.ShapeDtypeStruct((rows, n), F32),
        in_specs=[VMEM_SPEC], out_specs=VMEM_SPEC,
        compiler_params=pltpu.CompilerParams(vmem_limit_bytes=VMEM_LIMIT),
    )(gathered)


def _pack(vectors, width):
    flat = [v.reshape(-1) for v in vectors]
    offsets, total = [], 0
    for f in flat:
        offsets.append(total)
        total += f.shape[0]
    rows = -(-total // (width * SUBLANE)) * SUBLANE
    flat.append(jnp.zeros((rows * width - total,), F32))
    return jnp.concatenate(flat).reshape(rows, width), offsets


def kernel(x, c, mod_w, mod_b, norm_g, a_w_in, a_conv_w, a_conv_b, a_w_out, b_w_in, b_ln_g, b_ln_b, b_w_s, b_b_s, b_w_out, final_g, loss_target, m_mod_w, m_mod_b, m_norm_g, m_a_w_in, m_a_conv_w, m_a_conv_b, m_a_w_out, m_b_w_in, m_b_ln_g, m_b_ln_b, m_b_w_s, m_b_b_s, m_b_w_out, m_final_g, v_mod_w, v_mod_b, v_norm_g, v_a_w_in, v_a_conv_w, v_a_conv_b, v_a_w_out, v_b_w_in, v_b_ln_g, v_b_ln_b, v_b_w_s, v_b_b_s, v_b_w_out, v_final_g):
    seq, d = x.shape[1], x.shape[2]
    e = a_conv_b.shape[1]
    wd = mod_w.shape[2]
    ax, ay, ac = _place()
    chip = 2 * ax + ay
    dev = 2 * chip + ac
    chip_idx = jnp.reshape(chip, (1,)).astype(jnp.int32)
    core_idx = jnp.reshape(ac, (1,)).astype(jnp.int32)

    x2d = x[0]
    target = loss_target[0]

    w_names = ["a_w_in", "a_w_out", "b_w_in", "b_w_out"]
    w_kinds = ["col", "row", "col", "row"]
    w_shards = [a_w_in[0], a_w_out[0], b_w_in[0], b_w_out[0]]
    w_shapes = [sh.shape for sh in w_shards]
    placed = [_place_own(sh, kind, chip_idx, "place_" + nm) for nm, sh, kind in zip(w_names, w_shards, w_kinds)]
    w_groups = [[(0, 0), (0, 1)], [(0, 2)], [(w, j) for w in (1, 2, 3) for j in range(3)]]
    w_sems, w_flight, w_token = _gather_weights_start(placed, w_kinds, w_shapes, w_groups, "gather_weights_start")

    es = e // N_CHIP
    packed0, offs0 = _pack([c, a_conv_w, b_ln_g, b_ln_b], 1024)
    packed0 = packed0 + w_token[0:1, 0:1]
    gathered0 = _all_gather(packed0, "gather_params").reshape(N_DEV, -1)
    c_all = gathered0[:, :d]
    per_chip = gathered0[0::2]

    def from_chips_cols(k, rows_):
        got = per_chip[:, offs0[k]:offs0[k] + rows_ * es].reshape(N_CHIP, rows_, es)
        return jnp.transpose(got, (1, 0, 2)).reshape(rows_, e)

    conv_w_full = from_chips_cols(1, 3)
    conv_w = [conv_w_full[k:k + 1] for k in range(3)]
    ln_g, ln_b = from_chips_cols(2, 1), from_chips_cols(3, 1)
    mod_b_cols = lax.dynamic_slice_in_dim(mod_b, chip * wd, wd, axis=1)[:, None, :]
    c_act, mod_part = _mod_fwd(c_all, mod_w, mod_b_cols, "mod_fwd")
    n_layer = mod_w.shape[0]
    mod_all = _all_gather(mod_part, "gather_mod").reshape(N_CHIP, 2, n_layer, N_DEV, wd)[:, 0]
    mod_all = jnp.transpose(mod_all, (1, 2, 0, 3)).reshape(n_layer, N_DEV, N_CHIP * wd)
    mod_me = lax.dynamic_index_in_dim(mod_all, dev, axis=1, keepdims=False)
    shift = [mod_me[l:l + 1, 0:d] for l in range(n_layer)]
    scale = [mod_me[l:l + 1, d:2 * d] for l in range(n_layer)]
    gate = [mod_me[l:l + 1, 2 * d:3 * d] for l in range(n_layer)]

    g0, g1, gf = norm_g[0:1], norm_g[1:2], final_g[None, :]
    h0 = _norm_mod(x2d, g0, scale[0], shift[0], "norm_mod0")

    def slab(r):
        return jnp.bitwise_xor(chip_idx, r)

    proj0 = _mm_proj_slab(h0, w_flight[0], None, slab(0), 4, "a_proj_own", w_token)
    w_flight = _gather_weights_wait(w_sems[0], w_groups[0], w_flight, w_kinds, w_shapes, proj0, "gather_wait_near")
    proj0 = _mm_proj_slab(h0, w_flight[0], proj0, slab(2), 4, "a_proj_x", w_token)
    proj0 = _mm_proj_slab(h0, w_flight[0], proj0, slab(1), 4, "a_proj_y", w_token)
    w_flight = _gather_weights_wait(w_sems[1], w_groups[1], w_flight, w_kinds, w_shapes, proj0, "gather_wait_far")
    proj0 = _mm_proj_slab(h0, w_flight[0], proj0, slab(3), 4, "a_proj_far", w_token)
    wa_in, wa_out, wb_in, wb_out = _gather_weights_wait(
        w_sems[2], w_groups[2], w_flight, w_kinds, w_shapes, proj0, "gather_wait_rest")
    y0 = _conv_fwd(proj0, conv_w, a_conv_b, "conv_fwd")
    br0 = _mm_nn(y0, wa_out, "a_out")
    x1, h1 = _residual_norm_mod(x2d, br0, gate[0], g1, scale[1], shift[1], "residual_norm_mod1")
    proj1 = _mm_proj(h1, wb_in, 3, "b_proj", w_token)
    b_s_t = jnp.transpose(b_b_s[0])
    y1 = _gmlp_fwd(proj1, ln_g, ln_b, b_w_s[0], b_s_t, "gmlp_fwd")
    br1 = _mm_nn(y1, wb_out, "b_out")
    dx2, dbr1, loss_part, g_final_g, dgate1 = _head(x1, br1, gate[1], gf, target, "head")

    dy1 = _mm_nt(dbr1, wb_out, "b_out_dx")
    gw_b_out = _mm_dw_out(y1, dbr1, "b_out_dw")
    dproj1, g_w_s, g_b_s_t, g_ln_g, g_ln_b = _gmlp_bwd(
        proj1, dy1, ln_g, ln_b, b_w_s[0], jnp.swapaxes(b_w_s[0], 1, 2), b_s_t, "gmlp_bwd")
    gw_b_in = _mm_dw_in(h1, dproj1, "b_proj_dw")
    b_kinds = ["col", "row"]
    b_send, b_recv, b_grads, b_lands, b_token = _reduce_start(
        [gw_b_in, gw_b_out.reshape(N_CHIP, e // N_CHIP, d)], b_kinds, "reduce_b_start")
    dh1 = _mm_dh(dproj1, wb_in, "b_proj_dx", b_token)
    dx1, dshift1, dscale1, g_g1, dbr0, dgate0 = _norm_mod_bwd(
        dh1, x1, dx2, g1, scale[1], "norm_mod1_bwd", br=br0, gate=gate[0])

    dy0 = _mm_nt(dbr0, wa_out, "a_out_dx")
    gw_a_out = _mm_dw_out(y0, dbr0, "a_out_dw")
    dproj0, g_w0, g_w1, g_w2, g_conv_b = _conv_bwd(proj0, dy0, conv_w, a_conv_b, "conv_bwd")
    gw_a_in = _mm_dw_in(h0, dproj0, "a_proj_dw")
    a_kinds = ["col", "row"]
    a_send, a_recv, a_grads, a_lands, a_token = _reduce_start(
        [gw_a_in, gw_a_out.reshape(N_CHIP, e // N_CHIP, d)], a_kinds, "reduce_a_start")
    dh0 = _mm_dh(dproj0, wa_in, "a_proj_dx", a_token)
    grad_x, dshift0, dscale0, g_g0 = _norm_mod_bwd(dh0, x2d, dx1, g0, scale[0], "norm_mod0_bwd")

    dmod = jnp.concatenate([dshift0, dscale0, dgate0, dshift1, dscale1, dgate1], axis=1)
    small = [loss_part[0, 0:1], g_final_g, g_g0, g_g1, jnp.concatenate([g_w0, g_w1, g_w2], axis=0), g_conv_b, g_ln_g, g_ln_b,
             jnp.transpose(g_b_s_t), g_w_s, dmod]
    packed, offs = _pack(small, 1024)
    rows = packed.shape[0]
    gathered = _all_gather(packed, "gather_small")
    summed = _sum_devices(gathered, rows, "sum_small").reshape(-1)

    def take(k, shape):
        size = math.prod(shape)
        return summed[offs[k]:offs[k] + size].reshape(shape)

    loss = take(0, ())
    grad_final_g = take(1, (d,))
    grad_norm_g = jnp.concatenate([take(2, (1, d)), take(3, (1, d))], axis=0)
    grad_conv_w_full = take(4, (3, e))
    grad_a_conv_b = take(5, (1, e))
    grad_ln_g_full = take(6, (1, e))
    grad_ln_b_full = take(7, (1, e))
    grad_b_b_s = take(8, (1, GROUPS, CHUNK))
    grad_b_w_s = take(9, (1, GROUPS, CHUNK, CHUNK))
    grad_mod_b = take(10, (n_layer, 3 * d))
    grad_a_conv_w = lax.dynamic_slice_in_dim(grad_conv_w_full, chip * es, es, axis=1)[None]
    grad_b_ln_g = lax.dynamic_slice_in_dim(grad_ln_g_full, chip * es, es, axis=1)
    grad_b_ln_b = lax.dynamic_slice_in_dim(grad_ln_b_full, chip * es, es, axis=1)
    dmod_all = gathered.reshape(N_DEV, rows * 1024)[:, offs[10]:offs[10] + n_layer * 3 * d]
    dmod_all = dmod_all.reshape(N_DEV, n_layer, 3 * d)
    dmod_cols = jnp.transpose(lax.dynamic_slice_in_dim(dmod_all, chip * wd, wd, axis=2), (1, 0, 2))

    b_grads, b_lands = _reduce_wait(b_send, b_recv, b_grads, b_lands, b_kinds, summed, "reduce_b_wait")
    a_grads, a_lands = _reduce_wait(a_send, a_recv, a_grads, a_lands, a_kinds, summed, "reduce_a_wait")
    summed_halves = [
        _add_pieces(g, land, kind, chip_idx, core_idx, "add_pieces_" + nm)
        for g, land, kind, nm in [(a_grads[0], a_lands[0], "col", "a_w_in"), (b_grads[0], b_lands[0], "col", "b_w_in"),
                                  (a_grads[1], a_lands[1], "row", "a_w_out"), (b_grads[1], b_lands[1], "row", "b_w_out")]]
    g_a_w_in, g_b_w_in, g_a_w_out, g_b_w_out = _join_halves(summed_halves, "join_halves")

    grad_mod_w, delta_mod_w, new_m_mod_w, new_v_mod_w = _mod_w_update(
        jnp.transpose(c_act), dmod_cols, mod_w, m_mod_w, v_mod_w, "mod_w_update")
    upd = {}
    for nm, w, g, m, v in [("a_w_in", a_w_in, g_a_w_in, m_a_w_in, v_a_w_in), ("a_w_out", a_w_out, g_a_w_out, m_a_w_out, v_a_w_out),
                           ("b_w_in", b_w_in, g_b_w_in, m_b_w_in, v_b_w_in), ("b_w_out", b_w_out, g_b_w_out, m_b_w_out, v_b_w_out)]:
        upd[nm] = tuple(o[None] for o in _adamw_2d(w[0], g, m[0], v[0], "adamw_" + nm))
    small_w = [("mod_b", mod_b, grad_mod_b, m_mod_b, v_mod_b), ("norm_g", norm_g, grad_norm_g, m_norm_g, v_norm_g),
               ("a_conv_w", a_conv_w, grad_a_conv_w, m_a_conv_w, v_a_conv_w),
               ("a_conv_b", a_conv_b, grad_a_conv_b, m_a_conv_b, v_a_conv_b),
               ("b_ln_g", b_ln_g, grad_b_ln_g, m_b_ln_g, v_b_ln_g), ("b_ln_b", b_ln_b, grad_b_ln_b, m_b_ln_b, v_b_ln_b),
               ("b_w_s", b_w_s, grad_b_w_s, m_b_w_s, v_b_w_s), ("b_b_s", b_b_s, grad_b_b_s, m_b_b_s, v_b_b_s),
               ("final_g", final_g, grad_final_g, m_final_g, v_final_g)]

    def flat2d(a):
        return a.reshape(-1, a.shape[-1])

    res = _adamw_small([flat2d(t[1]) for t in small_w], [flat2d(t[2]) for t in small_w],
                       [flat2d(t[3]) for t in small_w], [flat2d(t[4]) for t in small_w], "adamw_small")
    for (nm, w, _, _, _), r3 in zip(small_w, res):
        upd[nm] = tuple(o.reshape(w.shape) for o in r3)
    upd["mod_w"] = (delta_mod_w, new_m_mod_w, new_v_mod_w)

    grads = {"mod_w": grad_mod_w, "mod_b": grad_mod_b, "norm_g": grad_norm_g, "a_w_in": g_a_w_in[None],
             "a_conv_w": grad_a_conv_w, "a_conv_b": grad_a_conv_b, "a_w_out": g_a_w_out[None], "b_w_in": g_b_w_in[None],
             "b_ln_g": grad_b_ln_g, "b_ln_b": grad_b_ln_b, "b_w_s": grad_b_w_s, "b_b_s": grad_b_b_s,
             "b_w_out": g_b_w_out[None], "final_g": grad_final_g}
    order = ["mod_w", "mod_b", "norm_g", "a_w_in", "a_conv_w", "a_conv_b", "a_w_out", "b_w_in", "b_ln_g", "b_ln_b",
             "b_w_s", "b_b_s", "b_w_out", "final_g"]
    return (loss, grad_x[None], *[grads[k] for k in order], *[upd[k][0] for k in order],
            *[upd[k][1] for k in order], *[upd[k][2] for k in order])
```

```python
import functools
import math

import jax
import jax.numpy as jnp
from jax import lax
from jax.experimental import pallas as pl
from jax.experimental.pallas import tpu as pltpu

F32 = jnp.float32
BF16 = jnp.bfloat16
MESH = pl.DeviceIdType.MESH

N_DEV = 8
N_CHIP = 4
SUBLANE = 8
RMS_EPS = 1e-6
LN_EPS = 1e-5
CHUNK = 128
GROUPS = 8
ADAM_LR = 0.001
ADAM_B1 = 0.9
ADAM_B2 = 0.999
ADAM_EPS = 1e-08
ADAM_WD = 0.01
ADAM_STEP = 10
VMEM_LIMIT = 56 << 20

HBM_SPEC = pl.BlockSpec(memory_space=pltpu.HBM)
VMEM_SPEC = pl.BlockSpec(memory_space=pltpu.VMEM)
SEM_SPEC = pl.BlockSpec(memory_space=pltpu.SEMAPHORE)
ANY_SPEC = pl.BlockSpec(memory_space=pl.ANY)
EFFECT = pltpu.SideEffectType.DATAFLOW_SIDE_EFFECTING


def _params(*sem):
    return pltpu.CompilerParams(dimension_semantics=sem, vmem_limit_bytes=VMEM_LIMIT)


def _tile(n, want):
    if n <= want:
        return n
    t = want
    while n % t:
        t -= 128
    return t


def _sigmoid(x):
    return 1.0 / (1.0 + jnp.exp(-x))


def _silu_and_grad(x):
    s = _sigmoid(x)
    return x * s, s * (1.0 + x * (1.0 - s))


def _gelu_and_grad(x):
    cdf = 0.5 * (1.0 + lax.erf(x * (1.0 / math.sqrt(2.0))))
    pdf = jnp.exp(-0.5 * x * x) * (1.0 / math.sqrt(2.0 * math.pi))
    return x * cdf, cdf + x * pdf


def _gelu(x):
    return x * (0.5 * (1.0 + lax.erf(x * (1.0 / math.sqrt(2.0)))))


def _rms(x):
    r = lax.rsqrt(jnp.mean(x * x, axis=-1, keepdims=True) + RMS_EPS)
    return x * r, r


def _rms_bwd(dxn, xn, r):
    return r * (dxn - xn * jnp.mean(dxn * xn, axis=-1, keepdims=True))


def _colsum(a):
    return jnp.sum(a, axis=0, keepdims=True)


def _shift_down(cur, before, k):
    rolled = pltpu.roll(cur, k, 0)
    row = lax.broadcasted_iota(jnp.int32, before.shape, 0)
    head = jnp.where(row < k, pltpu.roll(before, k, 0), rolled[:SUBLANE])
    return jnp.concatenate([head, rolled[SUBLANE:]], axis=0)


def _shift_up(cur, after, k):
    n = cur.shape[0]
    rolled = pltpu.roll(cur, n - k, 0)
    row = lax.broadcasted_iota(jnp.int32, after.shape, 0)
    tail = jnp.where(row >= SUBLANE - k, pltpu.roll(after, SUBLANE - k, 0), rolled[n - SUBLANE:])
    return jnp.concatenate([rolled[:n - SUBLANE], tail], axis=0)


def _after_spec():
    return pl.BlockSpec((SUBLANE, 128), lambda *_: (0, 0))


def _mm_proj(h, w, n_split, name, after):
    s, d = h.shape
    e = w.shape[1] // n_split
    tm, tn = _tile(s, 1024), _tile(e, 1024)
    nj = e // tn

    def body(h_ref, w_ref, after_ref, o_ref):
        o_ref[...] = jnp.dot(h_ref[...], w_ref[...], preferred_element_type=F32)

    return pl.pallas_call(
        body, name=name,
        out_shape=jax.ShapeDtypeStruct((n_split, s, e), F32),
        grid=(s // tm, n_split * nj),
        in_specs=[pl.BlockSpec((tm, d), lambda i, j: (i, 0)), pl.BlockSpec((d, tn), lambda i, j: (0, j)),
                  _after_spec()],
        out_specs=pl.BlockSpec((None, tm, tn), lambda i, j: (j // nj, i, j % nj)),
        compiler_params=_params("parallel", "parallel"),
    )(h, w, after)


def _mm_proj_slab(h, w, proj, q_idx, n_split, name, after):
    s, d = h.shape
    e = w.shape[1] // n_split
    tm, tn = _tile(s, 1024), _tile(e, 1024)
    nj = e // tn

    def body(q_ref, h_ref, w_ref, *rest):
        o_ref = rest[-1]
        o_ref[...] = jnp.dot(h_ref[...], w_ref[...], preferred_element_type=F32)

    in_specs = [pl.BlockSpec((tm, d), lambda i, j, qr: (i, 0)), pl.BlockSpec((d, tn), lambda i, j, qr: (0, qr[0] * nj + j)),
                _after_spec()]
    args = [q_idx, h, w, after]
    aliases = {}
    if proj is not None:
        in_specs.append(ANY_SPEC)
        args.append(proj)
        aliases = {4: 0}
    return pl.pallas_call(
        body, name=name,
        out_shape=jax.ShapeDtypeStruct((n_split, s, e), F32),
        grid_spec=pltpu.PrefetchScalarGridSpec(
            num_scalar_prefetch=1, grid=(s // tm, nj), in_specs=in_specs,
            out_specs=pl.BlockSpec((None, tm, tn), lambda i, j, qr: (qr[0], i, j))),
        input_output_aliases=aliases,
        compiler_params=_params("parallel", "parallel"),
    )(*args)


def _mm_nn(a, b, name):
    m, k = a.shape
    n = b.shape[1]
    tm, tn = _tile(m, 1024), _tile(n, 1024)

    def body(a_ref, b_ref, o_ref):
        o_ref[...] = jnp.dot(a_ref[...], b_ref[...], preferred_element_type=F32)

    return pl.pallas_call(
        body, name=name,
        out_shape=jax.ShapeDtypeStruct((m, n), F32),
        grid=(m // tm, n // tn),
        in_specs=[pl.BlockSpec((tm, k), lambda i, j: (i, 0)), pl.BlockSpec((k, tn), lambda i, j: (0, j))],
        out_specs=pl.BlockSpec((tm, tn), lambda i, j: (i, j)),
        compiler_params=_params("parallel", "parallel"),
    )(a, b)


def _mm_nt(a, b, name):
    m, k = a.shape
    n = b.shape[0]
    tm, tn = _tile(m, 1024), _tile(n, 1024)

    def body(a_ref, b_ref, o_ref):
        o_ref[...] = lax.dot_general(a_ref[...], b_ref[...], (((1,), (1,)), ((), ())), preferred_element_type=F32)

    return pl.pallas_call(
        body, name=name,
        out_shape=jax.ShapeDtypeStruct((m, n), F32),
        grid=(m // tm, n // tn),
        in_specs=[pl.BlockSpec((tm, k), lambda i, j: (i, 0)), pl.BlockSpec((tn, k), lambda i, j: (j, 0))],
        out_specs=pl.BlockSpec((tm, tn), lambda i, j: (i, j)),
        compiler_params=_params("parallel", "parallel"),
    )(a, b)


def _mm_dh(dp, w, name, after):
    nq, s, e = dp.shape
    d = w.shape[0]
    tm, tn, tk = _tile(s, 1024), _tile(d, 1024), _tile(e, 1024)
    nkq = e // tk
    nk = nq * nkq

    def body(a_ref, b_ref, after_ref, o_ref):
        @pl.when(pl.program_id(2) == 0)
        def _():
            o_ref[...] = jnp.zeros_like(o_ref)
        o_ref[...] += lax.dot_general(a_ref[...], b_ref[...], (((1,), (1,)), ((), ())), preferred_element_type=F32)

    return pl.pallas_call(
        body, name=name,
        out_shape=jax.ShapeDtypeStruct((s, d), F32),
        grid=(s // tm, d // tn, nk),
        in_specs=[pl.BlockSpec((None, tm, tk), lambda i, j, k: (k // nkq, i, k % nkq)),
                  pl.BlockSpec((tn, tk), lambda i, j, k: (j, k)), _after_spec()],
        out_specs=pl.BlockSpec((tm, tn), lambda i, j, k: (i, j)),
        compiler_params=_params("parallel", "parallel", "arbitrary"),
    )(dp, w, after)


def _dw_body(n_t):
    def body(a_ref, b_ref, o_ref, acc_ref):
        t = pl.program_id(2)

        @pl.when(t == 0)
        def _():
            acc_ref[...] = jnp.zeros_like(acc_ref)
        acc_ref[...] += lax.dot_general(a_ref[...], b_ref[...], (((0,), (0,)), ((), ())), preferred_element_type=F32)

        @pl.when(t == n_t - 1)
        def _():
            o_ref[...] = acc_ref[...].astype(o_ref.dtype)
    return body


def _mm_dw_in(h, dp, name):
    s, d = h.shape
    nq, _, e = dp.shape
    tm, tn, tt = _tile(d, 1024), _tile(e, 1024), _tile(s, 512)
    nj = e // tn

    return pl.pallas_call(
        _dw_body(s // tt), name=name,
        out_shape=jax.ShapeDtypeStruct((d, nq * e), BF16),
        grid=(d // tm, nq * nj, s // tt),
        in_specs=[pl.BlockSpec((tt, tm), lambda i, j, t: (t, i)),
                  pl.BlockSpec((None, tt, tn), lambda i, j, t: (j // nj, t, j % nj))],
        out_specs=pl.BlockSpec((tm, tn), lambda i, j, t: (i, j)),
        scratch_shapes=[pltpu.VMEM((tm, tn), F32)],
        compiler_params=_params("parallel", "parallel", "arbitrary"),
    )(h, dp)


def _mm_dw_out(y, dbr, name):
    s, e = y.shape
    d = dbr.shape[1]
    tm, tn, tt = _tile(e, 1024), _tile(d, 1024), _tile(s, 512)

    return pl.pallas_call(
        _dw_body(s // tt), name=name,
        out_shape=jax.ShapeDtypeStruct((e, d), BF16),
        grid=(e // tm, d // tn, s // tt),
        in_specs=[pl.BlockSpec((tt, tm), lambda i, j, t: (t, i)), pl.BlockSpec((tt, tn), lambda i, j, t: (t, j))],
        out_specs=pl.BlockSpec((tm, tn), lambda i, j, t: (i, j)),
        scratch_shapes=[pltpu.VMEM((tm, tn), F32)],
        compiler_params=_params("parallel", "parallel", "arbitrary"),
    )(y, dbr)


def _row_spec(ts, d):
    return pl.BlockSpec((ts, d), lambda i: (i, 0))


def _vec_spec(d):
    return pl.BlockSpec((1, d), lambda i: (0, 0))


def _norm_mod(x, g, scale, shift, name):
    s, d = x.shape
    ts = _tile(s, 512)

    def body(x_ref, g_ref, sc_ref, sh_ref, h_ref):
        xn, _ = _rms(x_ref[...])
        h_ref[...] = ((xn * g_ref[...]) * (1.0 + sc_ref[...]) + sh_ref[...]).astype(BF16)

    return pl.pallas_call(
        body, name=name, out_shape=jax.ShapeDtypeStruct((s, d), BF16), grid=(s // ts,),
        in_specs=[_row_spec(ts, d), _vec_spec(d), _vec_spec(d), _vec_spec(d)],
        out_specs=_row_spec(ts, d), compiler_params=_params("parallel"),
    )(x, g, scale, shift)


def _residual_norm_mod(x, br, gate, g, scale, shift, name):
    s, d = x.shape
    ts = _tile(s, 512)

    def body(x_ref, br_ref, gate_ref, g_ref, sc_ref, sh_ref, x1_ref, h_ref):
        x1 = x_ref[...] + gate_ref[...] * br_ref[...]
        x1_ref[...] = x1
        xn, _ = _rms(x1)
        h_ref[...] = ((xn * g_ref[...]) * (1.0 + sc_ref[...]) + sh_ref[...]).astype(BF16)

    return pl.pallas_call(
        body, name=name,
        out_shape=(jax.ShapeDtypeStruct((s, d), F32), jax.ShapeDtypeStruct((s, d), BF16)), grid=(s // ts,),
        in_specs=[_row_spec(ts, d), _row_spec(ts, d)] + [_vec_spec(d)] * 4,
        out_specs=(_row_spec(ts, d), _row_spec(ts, d)), compiler_params=_params("parallel"),
    )(x, br, gate, g, scale, shift)


def _head(x1, br, gate, gf, target, name):
    s, d = x1.shape
    ts = _tile(s, 512)

    def body(x1_ref, br_ref, gate_ref, gf_ref, tg_ref, dx_ref, dbr_ref, loss_ref, dgf_ref, dgate_ref):
        @pl.when(pl.program_id(0) == 0)
        def _():
            loss_ref[...] = jnp.zeros_like(loss_ref)
            dgf_ref[...] = jnp.zeros_like(dgf_ref)
            dgate_ref[...] = jnp.zeros_like(dgate_ref)
        br = br_ref[...]
        gate = gate_ref[...]
        gf = gf_ref[...]
        xn, r = _rms(x1_ref[...] + gate * br)
        err = xn * gf - tg_ref[...]
        loss_ref[...] += 0.5 * jnp.sum(jnp.mean(err * err, axis=-1, keepdims=True))
        dout = err * (1.0 / d)
        dgf_ref[...] += _colsum(dout * xn)
        dx = _rms_bwd(dout * gf, xn, r)
        dx_ref[...] = dx
        dgate_ref[...] += _colsum(dx * br)
        dbr_ref[...] = (gate * dx).astype(BF16)

    return pl.pallas_call(
        body, name=name,
        out_shape=(jax.ShapeDtypeStruct((s, d), F32), jax.ShapeDtypeStruct((s, d), BF16),
                   jax.ShapeDtypeStruct((SUBLANE, 128), F32), jax.ShapeDtypeStruct((1, d), F32),
                   jax.ShapeDtypeStruct((1, d), F32)),
        grid=(s // ts,),
        in_specs=[_row_spec(ts, d), _row_spec(ts, d), _vec_spec(d), _vec_spec(d), _row_spec(ts, d)],
        out_specs=(_row_spec(ts, d), _row_spec(ts, d), pl.BlockSpec((SUBLANE, 128), lambda i: (0, 0)),
                   _vec_spec(d), _vec_spec(d)),
        compiler_params=_params("arbitrary"),
    )(x1, br, gate, gf, target)


def _norm_mod_bwd(dh, x, dx_in, g, scale, name, br=None, gate=None):
    s, d = x.shape
    ts = _tile(s, 512)
    has_branch = br is not None

    def body(*refs):
        if has_branch:
            (dh_ref, x_ref, dxin_ref, g_ref, sc_ref, br_ref, gate_ref,
             dx_ref, dsh_ref, dsc_ref, dg_ref, dbr_ref, dgate_ref) = refs
        else:
            dh_ref, x_ref, dxin_ref, g_ref, sc_ref, dx_ref, dsh_ref, dsc_ref, dg_ref = refs

        @pl.when(pl.program_id(0) == 0)
        def _():
            dsh_ref[...] = jnp.zeros_like(dsh_ref)
            dsc_ref[...] = jnp.zeros_like(dsc_ref)
            dg_ref[...] = jnp.zeros_like(dg_ref)
            if has_branch:
                dgate_ref[...] = jnp.zeros_like(dgate_ref)
        dh = dh_ref[...]
        g = g_ref[...]
        xn, r = _rms(x_ref[...])
        dsh_ref[...] += _colsum(dh)
        dsc_ref[...] += _colsum(dh * (xn * g))
        da = dh * (1.0 + sc_ref[...])
        dg_ref[...] += _colsum(da * xn)
        dx = dxin_ref[...] + _rms_bwd(da * g, xn, r)
        dx_ref[...] = dx
        if has_branch:
            dgate_ref[...] += _colsum(dx * br_ref[...])
            dbr_ref[...] = (gate_ref[...] * dx).astype(BF16)

    vec_out = jax.ShapeDtypeStruct((1, d), F32)
    out_shape = [jax.ShapeDtypeStruct((s, d), F32), vec_out, vec_out, vec_out]
    out_specs = [_row_spec(ts, d), _vec_spec(d), _vec_spec(d), _vec_spec(d)]
    in_specs = [_row_spec(ts, d)] * 3 + [_vec_spec(d)] * 2
    args = [dh, x, dx_in, g, scale]
    if has_branch:
        out_shape += [jax.ShapeDtypeStruct((s, d), BF16), vec_out]
        out_specs += [_row_spec(ts, d), _vec_spec(d)]
        in_specs += [_row_spec(ts, d), _vec_spec(d)]
        args += [br, gate]
    return pl.pallas_call(
        body, name=name, out_shape=tuple(out_shape), grid=(s // ts,),
        in_specs=in_specs, out_specs=tuple(out_specs), compiler_params=_params("arbitrary"),
    )(*args)


def _conv_tiles(s, e):
    return _tile(s, 512), _tile(e, 512)


def _conv_fwd(proj, conv_w, conv_b, name):
    _, s, e = proj.shape
    ts, te = _conv_tiles(s, e)
    hb = ts // SUBLANE

    def body(bg_ref, cg_ref, xi_ref, z_ref, cgp_ref, xip_ref, w0_ref, w1_ref, w2_ref, b_ref, y_ref):
        cx = cg_ref[...] * xi_ref[...]
        before = jnp.where(pl.program_id(0) > 0, cgp_ref[...] * xip_ref[...], 0.0)
        conv = b_ref[...] + w2_ref[...] * cx
        conv = conv + w0_ref[...] * _shift_down(cx, before, 2)
        conv = conv + w1_ref[...] * _shift_down(cx, before, 1)
        z = z_ref[...]
        y_ref[...] = ((z * _sigmoid(z)) * bg_ref[...] * conv).astype(BF16)

    def part(q):
        return pl.BlockSpec((None, ts, te), lambda i, j: (q, i, j))

    def halo_before(q):
        return pl.BlockSpec((None, SUBLANE, te), lambda i, j: (q, jnp.maximum(i * hb - 1, 0), j))

    return pl.pallas_call(
        body, name=name, out_shape=jax.ShapeDtypeStruct((s, e), BF16), grid=(s // ts, e // te),
        in_specs=[part(0), part(1), part(2), part(3), halo_before(1), halo_before(2)]
        + [pl.BlockSpec((1, te), lambda i, j: (0, j))] * 4,
        out_specs=pl.BlockSpec((ts, te), lambda i, j: (i, j)),
        compiler_params=_params("parallel", "parallel"),
    )(proj, proj, proj, proj, proj, proj, *conv_w, conv_b)


def _conv_bwd(proj, dy, conv_w, conv_b, name):
    _, s, e = proj.shape
    ts, te = _conv_tiles(s, e)
    hb = ts // SUBLANE
    n_i = s // ts
    last_halo = s // SUBLANE - 1

    def body(bg_ref, cg_ref, xi_ref, z_ref, dy_ref, cgp_ref, xip_ref, bgn_ref, zn_ref, dyn_ref,
             w0_ref, w1_ref, w2_ref, b_ref, dp_ref, dw0_ref, dw1_ref, dw2_ref, db_ref):
        i = pl.program_id(1)

        @pl.when(i == 0)
        def _():
            for acc in (dw0_ref, dw1_ref, dw2_ref, db_ref):
                acc[...] = jnp.zeros_like(acc)
        bg, cg, xi, z, dy = bg_ref[...], cg_ref[...], xi_ref[...], z_ref[...], dy_ref[...]
        w0, w1, w2 = w0_ref[...], w1_ref[...], w2_ref[...]
        cx = cg * xi
        before = jnp.where(i > 0, cgp_ref[...] * xip_ref[...], 0.0)
        cx1 = _shift_down(cx, before, 1)
        cx2 = _shift_down(cx, before, 2)
        conv = b_ref[...] + w2 * cx
        conv = conv + w0 * cx2
        conv = conv + w1 * cx1
        sz, dsz = _silu_and_grad(z)
        dp_ref[3] = (dy * bg * conv * dsz).astype(BF16)
        dp_ref[0] = (dy * sz * conv).astype(BF16)
        dconv = dy * sz * bg
        zn = zn_ref[...]
        after = jnp.where(i < n_i - 1, dyn_ref[...] * (zn * _sigmoid(zn)) * bgn_ref[...], 0.0)
        db_ref[...] += _colsum(dconv)
        dw2_ref[...] += _colsum(dconv * cx)
        dw1_ref[...] += _colsum(dconv * cx1)
        dw0_ref[...] += _colsum(dconv * cx2)
        dcx = w2 * dconv + w1 * _shift_up(dconv, after, 1) + w0 * _shift_up(dconv, after, 2)
        dp_ref[1] = (dcx * xi).astype(BF16)
        dp_ref[2] = (dcx * cg).astype(BF16)

    def part(q):
        return pl.BlockSpec((None, ts, te), lambda j, i: (q, i, j))

    def halo_before(q):
        return pl.BlockSpec((None, SUBLANE, te), lambda j, i: (q, jnp.maximum(i * hb - 1, 0), j))

    def halo_after(q):
        return pl.BlockSpec((None, SUBLANE, te), lambda j, i: (q, jnp.minimum((i + 1) * hb, last_halo), j))

    return pl.pallas_call(
        body, name=name,
        out_shape=(jax.ShapeDtypeStruct((4, s, e), BF16),) + (jax.ShapeDtypeStruct((1, e), F32),) * 4,
        grid=(e // te, n_i),
        in_specs=[part(0), part(1), part(2), part(3), pl.BlockSpec((ts, te), lambda j, i: (i, j)),
                  halo_before(1), halo_before(2), halo_after(0), halo_after(3),
                  pl.BlockSpec((SUBLANE, te), lambda j, i: (jnp.minimum((i + 1) * hb, last_halo), j))]
        + [pl.BlockSpec((1, te), lambda j, i: (0, j))] * 4,
        out_specs=(pl.BlockSpec((4, ts, te), lambda j, i: (0, i, j)),) + (pl.BlockSpec((1, te), lambda j, i: (0, j)),) * 4,
        compiler_params=_params("parallel", "arbitrary"),
    )(proj, proj, proj, proj, dy, proj, proj, proj, proj, dy, *conv_w, conv_b)


def _tril(w):
    row = lax.broadcasted_iota(jnp.int32, w.shape, 0)
    col = lax.broadcasted_iota(jnp.int32, w.shape, 1)
    return jnp.where(row >= col, w, 0.0)


def _triu(w):
    row = lax.broadcasted_iota(jnp.int32, w.shape, 0)
    col = lax.broadcasted_iota(jnp.int32, w.shape, 1)
    return jnp.where(row <= col, w, 0.0)


def _layer_norm_fwd(v, g, b):
    mu = jnp.mean(v, axis=-1, keepdims=True)
    vc = v - mu
    rstd = lax.rsqrt(jnp.mean(vc * vc, axis=-1, keepdims=True) + LN_EPS)
    vhat = vc * rstd
    return vhat * g + b, vhat, rstd


def _gmlp_fwd(proj, ln_g, ln_b, w_s, b_s_t, name):
    _, s, e = proj.shape
    gw = e // GROUPS

    def body(pu_ref, pv_ref, pz_ref, g_ref, b_ref, ws_ref, bs_ref, y_ref, vn_scr, mix_scr):
        vn, _, _ = _layer_norm_fwd(_gelu(pv_ref[...]), g_ref[...], b_ref[...])
        vn_scr[...] = vn.astype(BF16)
        for gi in range(GROUPS):
            cols = slice(gi * gw, (gi + 1) * gw)
            wt = _tril(ws_ref[gi]).astype(BF16)
            mix_scr[:, cols] = jnp.dot(wt, vn_scr[:, cols], preferred_element_type=F32) + bs_ref[:, gi:gi + 1]
        z = pz_ref[...]
        y_ref[...] = ((z * _sigmoid(z)) * (_gelu(pu_ref[...]) * mix_scr[...])).astype(BF16)

    def part(q):
        return pl.BlockSpec((None, CHUNK, e), lambda i: (q, i, 0))

    return pl.pallas_call(
        body, name=name, out_shape=jax.ShapeDtypeStruct((s, e), BF16), grid=(s // CHUNK,),
        in_specs=[part(0), part(1), part(2), _vec_spec(e), _vec_spec(e),
                  pl.BlockSpec((GROUPS, CHUNK, CHUNK), lambda i: (0, 0, 0)),
                  pl.BlockSpec((CHUNK, GROUPS), lambda i: (0, 0))],
        out_specs=pl.BlockSpec((CHUNK, e), lambda i: (i, 0)),
        scratch_shapes=[pltpu.VMEM((CHUNK, e), BF16), pltpu.VMEM((CHUNK, e), F32)],
        compiler_params=_params("parallel"),
    )(proj, proj, proj, ln_g, ln_b, w_s, b_s_t)


def _gmlp_bwd(proj, dy, ln_g, ln_b, w_s, w_s_t, b_s_t, name):
    _, s, e = proj.shape
    gw = e // GROUPS
    n_i = s // CHUNK

    def body(pu_ref, pv_ref, pz_ref, dy_ref, g_ref, b_ref, ws_ref, wst_ref, bs_ref,
             dp_ref, dws_ref, dbs_ref, dlg_ref, dlb_ref, vn_scr, mix_scr, dm_scr, dvn_scr, dmacc_scr):
        i = pl.program_id(0)

        @pl.when(i == 0)
        def _():
            dws_ref[...] = jnp.zeros_like(dws_ref)
            dlg_ref[...] = jnp.zeros_like(dlg_ref)
            dlb_ref[...] = jnp.zeros_like(dlb_ref)
            dmacc_scr[...] = jnp.zeros_like(dmacc_scr)
        ln_g = g_ref[...]
        u, du_dpu = _gelu_and_grad(pu_ref[...])
        v, dv_dpv = _gelu_and_grad(pv_ref[...])
        vn, vhat, rstd = _layer_norm_fwd(v, ln_g, b_ref[...])
        vn_scr[...] = vn.astype(BF16)
        for gi in range(GROUPS):
            cols = slice(gi * gw, (gi + 1) * gw)
            wt = _tril(ws_ref[gi]).astype(BF16)
            mix_scr[:, cols] = jnp.dot(wt, vn_scr[:, cols], preferred_element_type=F32) + bs_ref[:, gi:gi + 1]
        mixed = mix_scr[...]
        dy = dy_ref[...]
        sz, dsz = _silu_and_grad(pz_ref[...])
        ds = dy * sz
        dp_ref[2] = (dy * (u * mixed) * dsz).astype(BF16)
        dp_ref[0] = (ds * mixed * du_dpu).astype(BF16)
        dm = ds * u
        dmacc_scr[...] += dm
        dm_scr[...] = dm.astype(BF16)
        for gi in range(GROUPS):
            cols = slice(gi * gw, (gi + 1) * gw)
            dmg = dm_scr[:, cols]
            dws_ref[gi] += lax.dot_general(dmg, vn_scr[:, cols], (((1,), (1,)), ((), ())), preferred_element_type=F32)
            wtt = _triu(wst_ref[gi]).astype(BF16)
            dvn_scr[:, cols] = jnp.dot(wtt, dmg, preferred_element_type=F32)
        dvn = dvn_scr[...]
        dlg_ref[...] += _colsum(dvn * vhat)
        dlb_ref[...] += _colsum(dvn)
        dvh = dvn * ln_g
        dv = rstd * (dvh - jnp.mean(dvh, axis=-1, keepdims=True) - vhat * jnp.mean(dvh * vhat, axis=-1, keepdims=True))
        dp_ref[1] = (dv * dv_dpv).astype(BF16)

        @pl.when(i == n_i - 1)
        def _():
            for gi in range(GROUPS):
                dws_ref[gi] = _tril(dws_ref[gi])
                dbs_ref[:, gi:gi + 1] = jnp.sum(dmacc_scr[:, gi * gw:(gi + 1) * gw], axis=1, keepdims=True)

    def part(q):
        return pl.BlockSpec((None, CHUNK, e), lambda i: (q, i, 0))

    w_spec = pl.BlockSpec((GROUPS, CHUNK, CHUNK), lambda i: (0, 0, 0))
    bs_spec = pl.BlockSpec((CHUNK, GROUPS), lambda i: (0, 0))
    return pl.pallas_call(
        body, name=name,
        out_shape=(jax.ShapeDtypeStruct((3, s, e), BF16), jax.ShapeDtypeStruct((GROUPS, CHUNK, CHUNK), F32),
                   jax.ShapeDtypeStruct((CHUNK, GROUPS), F32), jax.ShapeDtypeStruct((1, e), F32),
                   jax.ShapeDtypeStruct((1, e), F32)),
        grid=(n_i,),
        in_specs=[part(0), part(1), part(2), pl.BlockSpec((CHUNK, e), lambda i: (i, 0)), _vec_spec(e), _vec_spec(e),
                  w_spec, w_spec, bs_spec],
        out_specs=(pl.BlockSpec((3, CHUNK, e), lambda i: (0, i, 0)), w_spec, bs_spec, _vec_spec(e), _vec_spec(e)),
        scratch_shapes=[pltpu.VMEM((CHUNK, e), BF16), pltpu.VMEM((CHUNK, e), F32), pltpu.VMEM((CHUNK, e), BF16),
                        pltpu.VMEM((CHUNK, e), F32), pltpu.VMEM((CHUNK, e), F32)],
        compiler_params=_params("arbitrary"),
    )(proj, proj, proj, dy, ln_g, ln_b, w_s, w_s_t, b_s_t)


def _mod_fwd(c_all, mod_w, mod_b_cols, name):
    n_layer, d, w = mod_w.shape

    def body(c_ref, w_ref, b_ref, ca_ref, o_ref):
        c = c_ref[...]
        ca = c * _sigmoid(c)
        ca_ref[...] = ca
        for li in range(n_layer):
            o_ref[li * N_DEV:(li + 1) * N_DEV, :] = (
                jnp.dot(ca, w_ref[li], preferred_element_type=F32, precision=lax.Precision.HIGHEST) + b_ref[li])

    return pl.pallas_call(
        body, name=name,
        out_shape=(jax.ShapeDtypeStruct((N_DEV, d), F32), jax.ShapeDtypeStruct((n_layer * N_DEV, w), F32)),
        in_specs=[VMEM_SPEC] * 3, out_specs=(VMEM_SPEC, VMEM_SPEC),
        compiler_params=pltpu.CompilerParams(vmem_limit_bytes=VMEM_LIMIT),
    )(c_all, mod_w, mod_b_cols)


def _adamw(w, g, m, v):
    m = ADAM_B1 * m + (1.0 - ADAM_B1) * g
    v = ADAM_B2 * v + (1.0 - ADAM_B2) * (g * g)
    m_hat = m / (1.0 - ADAM_B1 ** ADAM_STEP)
    v_hat = v / (1.0 - ADAM_B2 ** ADAM_STEP)
    delta = -ADAM_LR * (m_hat / (jnp.sqrt(v_hat) + ADAM_EPS) + ADAM_WD * w)
    return delta, m, v


def _adamw_2d(w, g, m, v, name):
    r, c = w.shape
    tr, tc = _tile(r, 512), _tile(c, 1024)

    def body(w_ref, g_ref, m_ref, v_ref, d_ref, nm_ref, nv_ref):
        d_ref[...], nm_ref[...], nv_ref[...] = _adamw(w_ref[...], g_ref[...], m_ref[...], v_ref[...])

    spec = pl.BlockSpec((tr, tc), lambda i, j: (i, j))
    shape = jax.ShapeDtypeStruct((r, c), F32)
    return pl.pallas_call(
        body, name=name, out_shape=(shape, shape, shape), grid=(r // tr, c // tc),
        in_specs=[spec] * 4, out_specs=(spec, spec, spec), compiler_params=_params("parallel", "parallel"),
    )(w, g, m, v)


def _mod_w_update(ca_t, dmod_cols, w, m, v, name):
    n_layer, d, wd = w.shape
    tr = _tile(d, 256)

    def body(ca_ref, dm_ref, w_ref, m_ref, v_ref, g_ref, d_ref, nm_ref, nv_ref):
        ca = ca_ref[...]
        dm = dm_ref[...]
        g = ca[:, 0:1] * dm[0:1, :]
        for b in range(1, N_DEV):
            g = g + ca[:, b:b + 1] * dm[b:b + 1, :]
        g_ref[...] = g
        d_ref[...], nm_ref[...], nv_ref[...] = _adamw(w_ref[...], g, m_ref[...], v_ref[...])

    spec = pl.BlockSpec((None, tr, wd), lambda l, i: (l, i, 0))
    shape = jax.ShapeDtypeStruct((n_layer, d, wd), F32)
    return pl.pallas_call(
        body, name=name, out_shape=(shape,) * 4, grid=(n_layer, d // tr),
        in_specs=[pl.BlockSpec((tr, N_DEV), lambda l, i: (i, 0)), pl.BlockSpec((None, N_DEV, wd), lambda l, i: (l, 0, 0)),
                  spec, spec, spec],
        out_specs=(spec,) * 4, compiler_params=_params("parallel", "parallel"),
    )(ca_t, dmod_cols, w, m, v)


def _adamw_small(ws, gs, ms, vs, name):
    n = len(ws)

    def body(*refs):
        ins, outs = refs[:4 * n], refs[4 * n:]
        for k in range(n):
            delta, nm, nv = _adamw(ins[k][...], ins[n + k][...], ins[2 * n + k][...], ins[3 * n + k][...])
            outs[3 * k][...] = delta
            outs[3 * k + 1][...] = nm
            outs[3 * k + 2][...] = nv

    out_shape = []
    for w in ws:
        out_shape += [jax.ShapeDtypeStruct(w.shape, F32)] * 3
    outs = pl.pallas_call(
        body, name=name, out_shape=tuple(out_shape),
        in_specs=[VMEM_SPEC] * (4 * n), out_specs=tuple([VMEM_SPEC] * (3 * n)),
        compiler_params=pltpu.CompilerParams(vmem_limit_bytes=VMEM_LIMIT),
    )(*ws, *gs, *ms, *vs)
    return [tuple(outs[3 * k:3 * k + 3]) for k in range(n)]


def _place():
    return lax.axis_index("x"), lax.axis_index("y"), lax.axis_index("c")


def _other_chips(x, y):
    return [(1 - x, y), (x, 1 - y), (1 - x, 1 - y)]


def _all_gather(block, name):
    m_per, n = block.shape

    def body(x_ref, out_ref, send_sems, recv_sems, local_sem):
        x, y, c = _place()
        me, sibling = (x, y, c), (x, y, 1 - c)
        chips = _other_chips(x, y)

        def rows(px, py, pc):
            return out_ref.at[pl.ds((4 * px + 2 * py + pc) * m_per, m_per), :]

        def copy(k, blk, to, src=None):
            return pltpu.make_async_remote_copy(
                src_ref=rows(*blk) if src is None else src, dst_ref=rows(*blk),
                send_sem=send_sems.at[k], recv_sem=recv_sems.at[k], device_id=to, device_id_type=MESH)

        mine = pltpu.make_async_copy(x_ref, rows(*me), local_sem)
        mine.start()
        first = [copy(0, me, sibling, src=x_ref)]
        first += [copy(1 + j, me, (*chip, c), src=x_ref) for j, chip in enumerate(chips)]
        for cp in first:
            cp.start()
        passed = [copy(4 + j, (*chip, c), sibling) for j, chip in enumerate(chips)]
        for j, chip in enumerate(chips):
            copy(1 + j, (*chip, c), me).wait_recv()
            passed[j].start()
        copy(0, sibling, me).wait_recv()
        for j, chip in enumerate(chips):
            copy(4 + j, (*chip, 1 - c), me).wait_recv()
        for cp in first + passed:
            cp.wait_send()
        mine.wait()

    return pl.pallas_call(
        body, name=name, out_shape=jax.ShapeDtypeStruct((N_DEV * m_per, n), F32),
        in_specs=[VMEM_SPEC], out_specs=VMEM_SPEC,
        scratch_shapes=[pltpu.SemaphoreType.DMA((7,)), pltpu.SemaphoreType.DMA((7,)), pltpu.SemaphoreType.DMA],
        compiler_params=pltpu.CompilerParams(vmem_limit_bytes=VMEM_LIMIT),
    )(block)


def _hbm(a):
    return pltpu.with_memory_space_constraint(a, pltpu.HBM)


def _place_own(shard, kind, chip_idx, name):
    r, cdim = shard.shape
    tr, tc = _tile(r, 512), _tile(cdim, 1024)
    nrb, ncb = r // tr, cdim // tc

    def body(k_ref, s_ref, o_ref):
        o_ref[...] = s_ref[...].astype(BF16)

    if kind == "col":
        full, o_map = (r, N_CHIP * cdim), lambda i, j, kr: (i, kr[0] * ncb + j)
    else:
        full, o_map = (N_CHIP * r, cdim), lambda i, j, kr: (kr[0] * nrb + i, j)
    return pl.pallas_call(
        body, name=name, out_shape=jax.ShapeDtypeStruct(full, BF16),
        grid_spec=pltpu.PrefetchScalarGridSpec(
            num_scalar_prefetch=1, grid=(nrb, ncb),
            in_specs=[pl.BlockSpec((tr, tc), lambda i, j, kr: (i, j))],
            out_specs=pl.BlockSpec((tr, tc), o_map)),
        compiler_params=_params("parallel", "parallel"),
    )(chip_idx, shard)


def _weight_window(ref, kind, shard_shape, k, half):
    r, cdim = shard_shape
    hr = r // 2
    if kind == "col":
        return ref.at[pl.ds(half * hr, hr), pl.ds(pl.multiple_of(k * cdim, 128), cdim)]
    return ref.at[pl.ds(pl.multiple_of(k * r + half * hr, 2 * SUBLANE), hr), :]


def _gather_weights_start(fulls, kinds, shard_shapes, groups, after, name):
    n, ng = len(fulls), len(groups)

    def body(*refs):
        ins = refs[:n]
        sems = refs[n + 1:n + 1 + 2 * ng]
        token = refs[2 * n + 1 + 2 * ng]
        x, y, c = _place()
        chips = _other_chips(x, y)
        for g, group in enumerate(groups):
            for pos, (w, j) in enumerate(group):
                own = _weight_window(ins[w], kinds[w], shard_shapes[w], 2 * x + y, c)
                for cc in range(2):
                    pltpu.make_async_remote_copy(
                        src_ref=own, dst_ref=own,
                        send_sem=sems[2 * g].at[2 * pos + cc], recv_sem=sems[2 * g + 1].at[2 * pos + c],
                        device_id=(*chips[j], cc), device_id_type=MESH).start()
        token[...] = jnp.zeros_like(token)

    sem_shapes = []
    for group in groups:
        sem_shapes += [pltpu.SemaphoreType.DMA((2 * len(group),))] * 2
    outs = pl.pallas_call(
        body, name=name,
        out_shape=tuple(sem_shapes) + tuple(pltpu.HBM(f.shape, f.dtype) for f in fulls)
        + (jax.ShapeDtypeStruct((SUBLANE, 128), F32),),
        in_specs=[HBM_SPEC] * n + [ANY_SPEC], out_specs=(SEM_SPEC,) * (2 * ng) + (HBM_SPEC,) * n + (VMEM_SPEC,),
        input_output_aliases={w: 2 * ng + w for w in range(n)},
        compiler_params=pltpu.CompilerParams(has_side_effects=EFFECT),
    )(*[_hbm(f) for f in fulls], after)
    sems = [(outs[2 * g], outs[2 * g + 1]) for g in range(ng)]
    return sems, list(outs[2 * ng:2 * ng + n]), outs[2 * ng + n]


def _gather_weights_wait(sems, group, fulls, kinds, shard_shapes, after, name):
    n = len(fulls)

    def body(*refs):
        ins = refs[:n]
        send_sems, recv_sems = refs[n], refs[n + 1]
        x, y, c = _place()
        chips = _other_chips(x, y)
        for pos, (w, j) in enumerate(group):
            own = _weight_window(ins[w], kinds[w], shard_shapes[w], 2 * x + y, c)
            for cc in range(2):
                landed = _weight_window(ins[w], kinds[w], shard_shapes[w], 2 * chips[j][0] + chips[j][1], cc)
                k = 2 * pos + cc
                pltpu.make_async_remote_copy(
                    src_ref=own, dst_ref=own, send_sem=send_sems.at[k], recv_sem=recv_sems.at[k],
                    device_id=(*chips[j], cc), device_id_type=MESH).wait_send()
                pltpu.make_async_remote_copy(
                    src_ref=landed, dst_ref=landed, send_sem=send_sems.at[k], recv_sem=recv_sems.at[k],
                    device_id=(*chips[j], cc), device_id_type=MESH).wait_recv()

    return list(pl.pallas_call(
        body, name=name, out_shape=tuple(pltpu.HBM(f.shape, f.dtype) for f in fulls),
        in_specs=[HBM_SPEC] * n + [SEM_SPEC, SEM_SPEC, ANY_SPEC], out_specs=(HBM_SPEC,) * n,
        input_output_aliases={w: w for w in range(n)},
        compiler_params=pltpu.CompilerParams(has_side_effects=EFFECT),
    )(*fulls, sems[0], sems[1], after))


def _grad_piece(ref, kind, h, cdim, k, half):
    if kind == "col":
        return ref.at[pl.ds(half * h, h), pl.ds(pl.multiple_of(k * cdim, 128), cdim)]
    return ref.at[k, pl.ds(half * h, h), :]


def _grad_dims(g, kind):
    return (g.shape[0] // 2, g.shape[1] // N_CHIP) if kind == "col" else (g.shape[1] // 2, g.shape[2])


def _reduce_start(grads, kinds, name):
    n = len(grads)
    dims = [_grad_dims(g, kind) for g, kind in zip(grads, kinds)]
    lands = [lax.empty((N_DEV - 1, h, cdim), g.dtype) for g, (h, cdim) in zip(grads, dims)]

    def body(*refs):
        g_ins, land_ins = refs[:n], refs[n:2 * n]
        send_sems, recv_sems = refs[2 * n], refs[2 * n + 1]
        token = refs[4 * n + 2]
        x, y, c = _place()
        for w in range(n):
            h, cdim = dims[w]
            base = (N_DEV - 1) * w
            pltpu.make_async_remote_copy(
                src_ref=_grad_piece(g_ins[w], kinds[w], h, cdim, 2 * x + y, 1 - c), dst_ref=land_ins[w].at[0],
                send_sem=send_sems.at[base], recv_sem=recv_sems.at[base],
                device_id=(x, y, 1 - c), device_id_type=MESH).start()
            for j, chip in enumerate(_other_chips(x, y)):
                for cc in range(2):
                    pltpu.make_async_remote_copy(
                        src_ref=_grad_piece(g_ins[w], kinds[w], h, cdim, 2 * chip[0] + chip[1], cc),
                        dst_ref=land_ins[w].at[1 + 2 * j + c],
                        send_sem=send_sems.at[base + 1 + 2 * j + cc], recv_sem=recv_sems.at[base + 1 + 2 * j + c],
                        device_id=(*chip, cc), device_id_type=MESH).start()
        token[...] = jnp.zeros_like(token)

    sems = pltpu.SemaphoreType.DMA(((N_DEV - 1) * n,))
    outs = pl.pallas_call(
        body, name=name,
        out_shape=(sems, sems) + tuple(pltpu.HBM(a.shape, a.dtype) for a in list(grads) + lands)
        + (jax.ShapeDtypeStruct((SUBLANE, 128), F32),),
        in_specs=[HBM_SPEC] * (2 * n), out_specs=(SEM_SPEC, SEM_SPEC) + (HBM_SPEC,) * (2 * n) + (VMEM_SPEC,),
        input_output_aliases={i: 2 + i for i in range(2 * n)},
        compiler_params=pltpu.CompilerParams(has_side_effects=EFFECT),
    )(*[_hbm(a) for a in list(grads) + lands])
    return outs[0], outs[1], list(outs[2:2 + n]), list(outs[2 + n:2 + 2 * n]), outs[2 + 2 * n]


def _reduce_wait(send_sems, recv_sems, grads, lands, kinds, after, name):
    n = len(grads)
    dims = [_grad_dims(g, kind) for g, kind in zip(grads, kinds)]

    def body(*refs):
        g_ins, land_ins = refs[:n], refs[n:2 * n]
        send_sems, recv_sems = refs[2 * n], refs[2 * n + 1]
        x, y, c = _place()
        for w in range(n):
            h, cdim = dims[w]
            piece = _grad_piece(g_ins[w], kinds[w], h, cdim, 2 * x + y, c)
            for s in range(N_DEV - 1):
                k = (N_DEV - 1) * w + s
                slot = land_ins[w].at[s]
                pltpu.make_async_remote_copy(
                    src_ref=piece, dst_ref=slot, send_sem=send_sems.at[k], recv_sem=recv_sems.at[k],
                    device_id=(x, y, 1 - c), device_id_type=MESH).wait_send()
                pltpu.make_async_remote_copy(
                    src_ref=piece, dst_ref=slot, send_sem=send_sems.at[k], recv_sem=recv_sems.at[k],
                    device_id=(x, y, 1 - c), device_id_type=MESH).wait_recv()

    outs = pl.pallas_call(
        body, name=name, out_shape=tuple(pltpu.HBM(a.shape, a.dtype) for a in list(grads) + list(lands)),
        in_specs=[HBM_SPEC] * (2 * n) + [SEM_SPEC, SEM_SPEC, ANY_SPEC], out_specs=(HBM_SPEC,) * (2 * n),
        input_output_aliases={i: i for i in range(2 * n)},
        compiler_params=pltpu.CompilerParams(has_side_effects=EFFECT),
    )(*grads, *lands, send_sems, recv_sems, after)
    return list(outs[:n]), list(outs[n:])


def _add_pieces(g, land, kind, chip_idx, core_idx, name):
    _, h, cdim = land.shape
    tr, tc = _tile(h, 256), _tile(cdim, 2048)
    nrb, ncb = h // tr, cdim // tc

    def body(k_ref, c_ref, g_ref, l_ref, o_ref):
        acc = g_ref[...].astype(F32)
        for s in range(N_DEV - 1):
            acc = acc + l_ref[s].astype(F32)
        o_ref[...] = acc

    if kind == "col":
        g_spec = pl.BlockSpec((tr, tc), lambda i, j, kr, cr: (cr[0] * nrb + i, kr[0] * ncb + j))
    else:
        g_spec = pl.BlockSpec((None, tr, tc), lambda i, j, kr, cr: (kr[0], cr[0] * nrb + i, j))
    return pl.pallas_call(
        body, name=name, out_shape=jax.ShapeDtypeStruct((2 * h, cdim), F32),
        grid_spec=pltpu.PrefetchScalarGridSpec(
            num_scalar_prefetch=2, grid=(nrb, ncb),
            in_specs=[g_spec, pl.BlockSpec((N_DEV - 1, tr, tc), lambda i, j, kr, cr: (0, i, j))],
            out_specs=pl.BlockSpec((tr, tc), lambda i, j, kr, cr: (cr[0] * nrb + i, j))),
        compiler_params=_params("parallel", "parallel"),
    )(chip_idx, core_idx, g, land)


def _join_halves(shards, name):
    n = len(shards)

    def body(*refs):
        ins, outs = refs[:n], refs[n:2 * n]
        send_sems, recv_sems = refs[2 * n:]
        x, y, c = _place()
        copies = []
        for w in range(n):
            h = shards[w].shape[0] // 2
            cp = pltpu.make_async_remote_copy(
                src_ref=ins[w].at[pl.ds(c * h, h), :], dst_ref=outs[w].at[pl.ds(c * h, h), :],
                send_sem=send_sems.at[w], recv_sem=recv_sems.at[w], device_id=(x, y, 1 - c), device_id_type=MESH)
            cp.start()
            copies.append(cp)
        for w in range(n):
            h = shards[w].shape[0] // 2
            theirs = outs[w].at[pl.ds((1 - c) * h, h), :]
            pltpu.make_async_remote_copy(
                src_ref=theirs, dst_ref=theirs, send_sem=send_sems.at[w], recv_sem=recv_sems.at[w],
                device_id=(x, y, 1 - c), device_id_type=MESH).wait_recv()
        for cp in copies:
            cp.wait_send()

    return pl.pallas_call(
        body, name=name, out_shape=tuple(jax.ShapeDtypeStruct(s.shape, s.dtype) for s in shards),
        in_specs=[HBM_SPEC] * n, out_specs=tuple([HBM_SPEC] * n),
        input_output_aliases={w: w for w in range(n)},
        scratch_shapes=[pltpu.SemaphoreType.DMA((n,))] * 2,
    )(*shards)


def _sum_devices(gathered, rows, name):
    n = gathered.shape[1]

    def body(g_ref, o_ref):
        acc = g_ref[0:rows, :]
        for dev in range(1, N_DEV):
            acc = acc + g_ref[dev * rows:(dev + 1) * rows, :]
        o_ref[...] = acc

    return pl.pallas_call(
        body, name=name, out_shape=jax.ShapeDtypeStruct((rows, n), F32),
        in_specs=[VMEM_SPEC], out_specs=VMEM_SPEC,
        compiler_params=pltpu.CompilerParams(vmem_limit_bytes=VMEM_LIMIT),
    )(gathered)


def _pack(vectors, width):
    flat = [v.reshape(-1) for v in vectors]
    offsets, total = [], 0
    for f in flat:
        offsets.append(total)
        total += f.shape[0]
    rows = -(-total // (width * SUBLANE)) * SUBLANE
    flat.append(jnp.zeros((rows * width - total,), F32))
    return jnp.concatenate(flat).reshape(rows, width), offsets


def kernel(x, c, mod_w, mod_b, norm_g, a_w_in, a_conv_w, a_conv_b, a_w_out, b_w_in, b_ln_g, b_ln_b, b_w_s, b_b_s, b_w_out, final_g, loss_target, m_mod_w, m_mod_b, m_norm_g, m_a_w_in, m_a_conv_w, m_a_conv_b, m_a_w_out, m_b_w_in, m_b_ln_g, m_b_ln_b, m_b_w_s, m_b_b_s, m_b_w_out, m_final_g, v_mod_w, v_mod_b, v_norm_g, v_a_w_in, v_a_conv_w, v_a_conv_b, v_a_w_out, v_b_w_in, v_b_ln_g, v_b_ln_b, v_b_w_s, v_b_b_s, v_b_w_out, v_final_g):
    seq, d = x.shape[1], x.shape[2]
    e = a_conv_b.shape[1]
    wd = mod_w.shape[2]
    ax, ay, ac = _place()
    chip = 2 * ax + ay
    dev = 2 * chip + ac
    chip_idx = jnp.reshape(chip, (1,)).astype(jnp.int32)
    core_idx = jnp.reshape(ac, (1,)).astype(jnp.int32)

    x2d = x[0]
    target = loss_target[0]

    w_names = ["a_w_in", "a_w_out", "b_w_in", "b_w_out"]
    w_kinds = ["col", "row", "col", "row"]
    w_shards = [a_w_in[0], a_w_out[0], b_w_in[0], b_w_out[0]]
    w_shapes = [sh.shape for sh in w_shards]
    placed = [_place_own(sh, kind, chip_idx, "place_" + nm) for nm, sh, kind in zip(w_names, w_shards, w_kinds)]
    w_groups = [[(0, 0), (0, 1)], [(0, 2)], [(w, j) for w in (1, 2, 3) for j in range(3)]]

    es = e // N_CHIP
    packed0, offs0 = _pack([c, a_conv_w, b_ln_g, b_ln_b], 1024)
    gathered0 = _all_gather(packed0, "gather_params").reshape(N_DEV, -1)
    c_all = gathered0[:, :d]
    per_chip = gathered0[0::2]

    def from_chips_cols(k, rows_):
        got = per_chip[:, offs0[k]:offs0[k] + rows_ * es].reshape(N_CHIP, rows_, es)
        return jnp.transpose(got, (1, 0, 2)).reshape(rows_, e)

    conv_w_full = from_chips_cols(1, 3)
    conv_w = [conv_w_full[k:k + 1] for k in range(3)]
    ln_g, ln_b = from_chips_cols(2, 1), from_chips_cols(3, 1)
    mod_b_cols = lax.dynamic_slice_in_dim(mod_b, chip * wd, wd, axis=1)[:, None, :]
    c_act, mod_part = _mod_fwd(c_all, mod_w, mod_b_cols, "mod_fwd")
    n_layer = mod_w.shape[0]
    mod_gathered = _all_gather(mod_part, "gather_mod")
    w_sems, w_flight, w_token = _gather_weights_start(
        placed, w_kinds, w_shapes, w_groups, mod_gathered, "gather_weights_start")
    mod_all = mod_gathered.reshape(N_CHIP, 2, n_layer, N_DEV, wd)[:, 0]
    mod_all = jnp.transpose(mod_all, (1, 2, 0, 3)).reshape(n_layer, N_DEV, N_CHIP * wd)
    mod_me = lax.dynamic_index_in_dim(mod_all, dev, axis=1, keepdims=False)
    shift = [mod_me[l:l + 1, 0:d] for l in range(n_layer)]
    scale = [mod_me[l:l + 1, d:2 * d] for l in range(n_layer)]
    gate = [mod_me[l:l + 1, 2 * d:3 * d] for l in range(n_layer)]

    g0, g1, gf = norm_g[0:1], norm_g[1:2], final_g[None, :]
    h0 = _norm_mod(x2d, g0, scale[0], shift[0], "norm_mod0")

    def slab(r):
        return jnp.bitwise_xor(chip_idx, r)

    proj0 = _mm_proj_slab(h0, w_flight[0], None, slab(0), 4, "a_proj_own", w_token)
    w_flight = _gather_weights_wait(w_sems[0], w_groups[0], w_flight, w_kinds, w_shapes, proj0, "gather_wait_near")
    proj0 = _mm_proj_slab(h0, w_flight[0], proj0, slab(2), 4, "a_proj_x", w_token)
    proj0 = _mm_proj_slab(h0, w_flight[0], proj0, slab(1), 4, "a_proj_y", w_token)
    w_flight = _gather_weights_wait(w_sems[1], w_groups[1], w_flight, w_kinds, w_shapes, proj0, "gather_wait_far")
    proj0 = _mm_proj_slab(h0, w_flight[0], proj0, slab(3), 4, "a_proj_far", w_token)
    wa_in, wa_out, wb_in, wb_out = _gather_weights_wait(
        w_sems[2], w_groups[2], w_flight, w_kinds, w_shapes, proj0, "gather_wait_rest")
    y0 = _conv_fwd(proj0, conv_w, a_conv_b, "conv_fwd")
    br0 = _mm_nn(y0, wa_out, "a_out")
    x1, h1 = _residual_norm_mod(x2d, br0, gate[0], g1, scale[1], shift[1], "residual_norm_mod1")
    proj1 = _mm_proj(h1, wb_in, 3, "b_proj", w_token)
    b_s_t = jnp.transpose(b_b_s[0])
    y1 = _gmlp_fwd(proj1, ln_g, ln_b, b_w_s[0], b_s_t, "gmlp_fwd")
    br1 = _mm_nn(y1, wb_out, "b_out")
    dx2, dbr1, loss_part, g_final_g, dgate1 = _head(x1, br1, gate[1], gf, target, "head")

    dy1 = _mm_nt(dbr1, wb_out, "b_out_dx")
    gw_b_out = _mm_dw_out(y1, dbr1, "b_out_dw")
    dproj1, g_w_s, g_b_s_t, g_ln_g, g_ln_b = _gmlp_bwd(
        proj1, dy1, ln_g, ln_b, b_w_s[0], jnp.swapaxes(b_w_s[0], 1, 2), b_s_t, "gmlp_bwd")
    gw_b_in = _mm_dw_in(h1, dproj1, "b_proj_dw")
    b_kinds = ["col", "row"]
    b_send, b_recv, b_grads, b_lands, b_token = _reduce_start(
        [gw_b_in, gw_b_out.reshape(N_CHIP, e // N_CHIP, d)], b_kinds, "reduce_b_start")
    dh1 = _mm_dh(dproj1, wb_in, "b_proj_dx", b_token)
    dx1, dshift1, dscale1, g_g1, dbr0, dgate0 = _norm_mod_bwd(
        dh1, x1, dx2, g1, scale[1], "norm_mod1_bwd", br=br0, gate=gate[0])

    dy0 = _mm_nt(dbr0, wa_out, "a_out_dx")
    gw_a_out = _mm_dw_out(y0, dbr0, "a_out_dw")
    dproj0, g_w0, g_w1, g_w2, g_conv_b = _conv_bwd(proj0, dy0, conv_w, a_conv_b, "conv_bwd")
    gw_a_in = _mm_dw_in(h0, dproj0, "a_proj_dw")
    a_kinds = ["col", "row"]
    a_send, a_recv, a_grads, a_lands, a_token = _reduce_start(
        [gw_a_in, gw_a_out.reshape(N_CHIP, e // N_CHIP, d)], a_kinds, "reduce_a_start")
    dh0 = _mm_dh(dproj0, wa_in, "a_proj_dx", a_token)
    grad_x, dshift0, dscale0, g_g0 = _norm_mod_bwd(dh0, x2d, dx1, g0, scale[0], "norm_mod0_bwd")

    dmod = jnp.concatenate([dshift0, dscale0, dgate0, dshift1, dscale1, dgate1], axis=1)
    small = [loss_part[0, 0:1], g_final_g, g_g0, g_g1, jnp.concatenate([g_w0, g_w1, g_w2], axis=0), g_conv_b, g_ln_g, g_ln_b,
             jnp.transpose(g_b_s_t), g_w_s, dmod]
    packed, offs = _pack(small, 1024)
    rows = packed.shape[0]
    gathered = _all_gather(packed, "gather_small")
    summed = _sum_devices(gathered, rows, "sum_small").reshape(-1)

    def take(k, shape):
        size = math.prod(shape)
        return summed[offs[k]:offs[k] + size].reshape(shape)

    loss = take(0, ())
    grad_final_g = take(1, (d,))
    grad_norm_g = jnp.concatenate([take(2, (1, d)), take(3, (1, d))], axis=0)
    grad_conv_w_full = take(4, (3, e))
    grad_a_conv_b = take(5, (1, e))
    grad_ln_g_full = take(6, (1, e))
    grad_ln_b_full = take(7, (1, e))
    grad_b_b_s = take(8, (1, GROUPS, CHUNK))
    grad_b_w_s = take(9, (1, GROUPS, CHUNK, CHUNK))
    grad_mod_b = take(10, (n_layer, 3 * d))
    grad_a_conv_w = lax.dynamic_slice_in_dim(grad_conv_w_full, chip * es, es, axis=1)[None]
    grad_b_ln_g = lax.dynamic_slice_in_dim(grad_ln_g_full, chip * es, es, axis=1)
    grad_b_ln_b = lax.dynamic_slice_in_dim(grad_ln_b_full, chip * es, es, axis=1)
    dmod_all = gathered.reshape(N_DEV, rows * 1024)[:, offs[10]:offs[10] + n_layer * 3 * d]
    dmod_all = dmod_all.reshape(N_DEV, n_layer, 3 * d)
    dmod_cols = jnp.transpose(lax.dynamic_slice_in_dim(dmod_all, chip * wd, wd, axis=2), (1, 0, 2))

    b_grads, b_lands = _reduce_wait(b_send, b_recv, b_grads, b_lands, b_kinds, summed, "reduce_b_wait")
    a_grads, a_lands = _reduce_wait(a_send, a_recv, a_grads, a_lands, a_kinds, summed, "reduce_a_wait")
    summed_halves = [
        _add_pieces(g, land, kind, chip_idx, core_idx, "add_pieces_" + nm)
        for g, land, kind, nm in [(a_grads[0], a_lands[0], "col", "a_w_in"), (b_grads[0], b_lands[0], "col", "b_w_in"),
                                  (a_grads[1], a_lands[1], "row", "a_w_out"), (b_grads[1], b_lands[1], "row", "b_w_out")]]
    g_a_w_in, g_b_w_in, g_a_w_out, g_b_w_out = _join_halves(summed_halves, "join_halves")

    grad_mod_w, delta_mod_w, new_m_mod_w, new_v_mod_w = _mod_w_update(
        jnp.transpose(c_act), dmod_cols, mod_w, m_mod_w, v_mod_w, "mod_w_update")
    upd = {}
    for nm, w, g, m, v in [("a_w_in", a_w_in, g_a_w_in, m_a_w_in, v_a_w_in), ("a_w_out", a_w_out, g_a_w_out, m_a_w_out, v_a_w_out),
                           ("b_w_in", b_w_in, g_b_w_in, m_b_w_in, v_b_w_in), ("b_w_out", b_w_out, g_b_w_out, m_b_w_out, v_b_w_out)]:
        upd[nm] = tuple(o[None] for o in _adamw_2d(w[0], g, m[0], v[0], "adamw_" + nm))
    small_w = [("mod_b", mod_b, grad_mod_b, m_mod_b, v_mod_b), ("norm_g", norm_g, grad_norm_g, m_norm_g, v_norm_g),
               ("a_conv_w", a_conv_w, grad_a_conv_w, m_a_conv_w, v_a_conv_w),
               ("a_conv_b", a_conv_b, grad_a_conv_b, m_a_conv_b, v_a_conv_b),
               ("b_ln_g", b_ln_g, grad_b_ln_g, m_b_ln_g, v_b_ln_g), ("b_ln_b", b_ln_b, grad_b_ln_b, m_b_ln_b, v_b_ln_b),
               ("b_w_s", b_w_s, grad_b_w_s, m_b_w_s, v_b_w_s), ("b_b_s", b_b_s, grad_b_b_s, m_b_b_s, v_b_b_s),
               ("final_g", final_g, grad_final_g, m_final_g, v_final_g)]

    def flat2d(a):
        return a.reshape(-1, a.shape[-1])

    res = _adamw_small([flat2d(t[1]) for t in small_w], [flat2d(t[2]) for t in small_w],
                       [flat2d(t[3]) for t in small_w], [flat2d(t[4]) for t in small_w], "adamw_small")
    for (nm, w, _, _, _), r3 in zip(small_w, res):
        upd[nm] = tuple(o.reshape(w.shape) for o in r3)
    upd["mod_w"] = (delta_mod_w, new_m_mod_w, new_v_mod_w)

    grads = {"mod_w": grad_mod_w, "mod_b": grad_mod_b, "norm_g": grad_norm_g, "a_w_in": g_a_w_in[None],
             "a_conv_w": grad_a_conv_w, "a_conv_b": grad_a_conv_b, "a_w_out": g_a_w_out[None], "b_w_in": g_b_w_in[None],
             "b_ln_g": grad_b_ln_g, "b_ln_b": grad_b_ln_b, "b_w_s": grad_b_w_s, "b_b_s": grad_b_b_s,
             "b_w_out": g_b_w_out[None], "final_g": grad_final_g}
    order = ["mod_w", "mod_b", "norm_g", "a_w_in", "a_conv_w", "a_conv_b", "a_w_out", "b_w_in", "b_ln_g", "b_ln_b",
             "b_w_s", "b_b_s", "b_w_out", "final_g"]
    return (loss, grad_x[None], *[grads[k] for k in order], *[upd[k][0] for k in order],
            *[upd[k][1] for k in order], *[upd[k][2] for k in order])
```

```python
import functools
import math

import jax
import jax.numpy as jnp
from jax import lax
from jax.experimental import pallas as pl
from jax.experimental.pallas import tpu as pltpu

F32 = jnp.float32
BF16 = jnp.bfloat16
MESH = pl.DeviceIdType.MESH

N_DEV = 8
N_CHIP = 4
SUBLANE = 8
RMS_EPS = 1e-6
LN_EPS = 1e-5
CHUNK = 128
GROUPS = 8
ADAM_LR = 0.001
ADAM_B1 = 0.9
ADAM_B2 = 0.999
ADAM_EPS = 1e-08
ADAM_WD = 0.01
ADAM_STEP = 10
VMEM_LIMIT = 56 << 20

HBM_SPEC = pl.BlockSpec(memory_space=pltpu.HBM)
VMEM_SPEC = pl.BlockSpec(memory_space=pltpu.VMEM)
SEM_SPEC = pl.BlockSpec(memory_space=pltpu.SEMAPHORE)
ANY_SPEC = pl.BlockSpec(memory_space=pl.ANY)
EFFECT = pltpu.SideEffectType.DATAFLOW_SIDE_EFFECTING


def _params(*sem):
    return pltpu.CompilerParams(dimension_semantics=sem, vmem_limit_bytes=VMEM_LIMIT)


def _tile(n, want):
    if n <= want:
        return n
    t = want
    while n % t:
        t -= 128
    return t


def _sigmoid(x):
    return 1.0 / (1.0 + jnp.exp(-x))


def _silu_and_grad(x):
    s = _sigmoid(x)
    return x * s, s * (1.0 + x * (1.0 - s))


def _gelu_and_grad(x):
    cdf = 0.5 * (1.0 + lax.erf(x * (1.0 / math.sqrt(2.0))))
    pdf = jnp.exp(-0.5 * x * x) * (1.0 / math.sqrt(2.0 * math.pi))
    return x * cdf, cdf + x * pdf


def _gelu(x):
    return x * (0.5 * (1.0 + lax.erf(x * (1.0 / math.sqrt(2.0)))))


def _rms(x):
    r = lax.rsqrt(jnp.mean(x * x, axis=-1, keepdims=True) + RMS_EPS)
    return x * r, r


def _rms_bwd(dxn, xn, r):
    return r * (dxn - xn * jnp.mean(dxn * xn, axis=-1, keepdims=True))


def _colsum(a):
    return jnp.sum(a, axis=0, keepdims=True)


def _shift_down(cur, before, k):
    rolled = pltpu.roll(cur, k, 0)
    row = lax.broadcasted_iota(jnp.int32, before.shape, 0)
    head = jnp.where(row < k, pltpu.roll(before, k, 0), rolled[:SUBLANE])
    return jnp.concatenate([head, rolled[SUBLANE:]], axis=0)


def _shift_up(cur, after, k):
    n = cur.shape[0]
    rolled = pltpu.roll(cur, n - k, 0)
    row = lax.broadcasted_iota(jnp.int32, after.shape, 0)
    tail = jnp.where(row >= SUBLANE - k, pltpu.roll(after, SUBLANE - k, 0), rolled[n - SUBLANE:])
    return jnp.concatenate([rolled[:n - SUBLANE], tail], axis=0)


def _after_spec():
    return pl.BlockSpec((SUBLANE, 128), lambda *_: (0, 0))


def _mm_proj(h, w, n_split, name, after):
    s, d = h.shape
    e = w.shape[1] // n_split
    tm, tn = _tile(s, 1024), _tile(e, 1024)
    nj = e // tn

    def body(h_ref, w_ref, after_ref, o_ref):
        o_ref[...] = jnp.dot(h_ref[...], w_ref[...], preferred_element_type=F32)

    return pl.pallas_call(
        body, name=name,
        out_shape=jax.ShapeDtypeStruct((n_split, s, e), F32),
        grid=(s // tm, n_split * nj),
        in_specs=[pl.BlockSpec((tm, d), lambda i, j: (i, 0)), pl.BlockSpec((d, tn), lambda i, j: (0, j)),
                  _after_spec()],
        out_specs=pl.BlockSpec((None, tm, tn), lambda i, j: (j // nj, i, j % nj)),
        compiler_params=_params("parallel", "parallel"),
    )(h, w, after)


def _mm_proj_slab(h, w, proj, q_idx, n_split, name, after):
    s, d = h.shape
    e = w.shape[1] // n_split
    tm, tn = _tile(s, 1024), _tile(e, 1024)
    nj = e // tn

    def body(q_ref, h_ref, w_ref, *rest):
        o_ref = rest[-1]
        o_ref[...] = jnp.dot(h_ref[...], w_ref[...], preferred_element_type=F32)

    in_specs = [pl.BlockSpec((tm, d), lambda i, j, qr: (i, 0)), pl.BlockSpec((d, tn), lambda i, j, qr: (0, qr[0] * nj + j)),
                _after_spec()]
    args = [q_idx, h, w, after]
    aliases = {}
    if proj is not None:
        in_specs.append(ANY_SPEC)
        args.append(proj)
        aliases = {4: 0}
    return pl.pallas_call(
        body, name=name,
        out_shape=jax.ShapeDtypeStruct((n_split, s, e), F32),
        grid_spec=pltpu.PrefetchScalarGridSpec(
            num_scalar_prefetch=1, grid=(s // tm, nj), in_specs=in_specs,
            out_specs=pl.BlockSpec((None, tm, tn), lambda i, j, qr: (qr[0], i, j))),
        input_output_aliases=aliases,
        compiler_params=_params("parallel", "parallel"),
    )(*args)


def _mm_nn(a, b, name):
    m, k = a.shape
    n = b.shape[1]
    tm, tn = _tile(m, 1024), _tile(n, 1024)

    def body(a_ref, b_ref, o_ref):
        o_ref[...] = jnp.dot(a_ref[...], b_ref[...], preferred_element_type=F32)

    return pl.pallas_call(
        body, name=name,
        out_shape=jax.ShapeDtypeStruct((m, n), F32),
        grid=(m // tm, n // tn),
        in_specs=[pl.BlockSpec((tm, k), lambda i, j: (i, 0)), pl.BlockSpec((k, tn), lambda i, j: (0, j))],
        out_specs=pl.BlockSpec((tm, tn), lambda i, j: (i, j)),
        compiler_params=_params("parallel", "parallel"),
    )(a, b)


def _mm_nt(a, b, name):
    m, k = a.shape
    n = b.shape[0]
    tm, tn = _tile(m, 1024), _tile(n, 1024)

    def body(a_ref, b_ref, o_ref):
        o_ref[...] = lax.dot_general(a_ref[...], b_ref[...], (((1,), (1,)), ((), ())), preferred_element_type=F32)

    return pl.pallas_call(
        body, name=name,
        out_shape=jax.ShapeDtypeStruct((m, n), F32),
        grid=(m // tm, n // tn),
        in_specs=[pl.BlockSpec((tm, k), lambda i, j: (i, 0)), pl.BlockSpec((tn, k), lambda i, j: (j, 0))],
        out_specs=pl.BlockSpec((tm, tn), lambda i, j: (i, j)),
        compiler_params=_params("parallel", "parallel"),
    )(a, b)


def _mm_dh(dp, w, name, after):
    nq, s, e = dp.shape
    d = w.shape[0]
    tm, tn, tk = _tile(s, 1024), _tile(d, 1024), _tile(e, 1024)
    nkq = e // tk
    nk = nq * nkq

    def body(a_ref, b_ref, after_ref, o_ref):
        @pl.when(pl.program_id(2) == 0)
        def _():
            o_ref[...] = jnp.zeros_like(o_ref)
        o_ref[...] += lax.dot_general(a_ref[...], b_ref[...], (((1,), (1,)), ((), ())), preferred_element_type=F32)

    return pl.pallas_call(
        body, name=name,
        out_shape=jax.ShapeDtypeStruct((s, d), F32),
        grid=(s // tm, d // tn, nk),
        in_specs=[pl.BlockSpec((None, tm, tk), lambda i, j, k: (k // nkq, i, k % nkq)),
                  pl.BlockSpec((tn, tk), lambda i, j, k: (j, k)), _after_spec()],
        out_specs=pl.BlockSpec((tm, tn), lambda i, j, k: (i, j)),
        compiler_params=_params("parallel", "parallel", "arbitrary"),
    )(dp, w, after)


def _dw_body(n_t):
    def body(a_ref, b_ref, o_ref, acc_ref):
        t = pl.program_id(2)

        @pl.when(t == 0)
        def _():
            acc_ref[...] = jnp.zeros_like(acc_ref)
        acc_ref[...] += lax.dot_general(a_ref[...], b_ref[...], (((0,), (0,)), ((), ())), preferred_element_type=F32)

        @pl.when(t == n_t - 1)
        def _():
            o_ref[...] = acc_ref[...].astype(o_ref.dtype)
    return body


def _mm_dw_in(h, dp, name):
    s, d = h.shape
    nq, _, e = dp.shape
    tm, tn, tt = _tile(d, 1024), _tile(e, 1024), _tile(s, 512)
    nj = e // tn

    return pl.pallas_call(
        _dw_body(s // tt), name=name,
        out_shape=jax.ShapeDtypeStruct((d, nq * e), BF16),
        grid=(d // tm, nq * nj, s // tt),
        in_specs=[pl.BlockSpec((tt, tm), lambda i, j, t: (t, i)),
                  pl.BlockSpec((None, tt, tn), lambda i, j, t: (j // nj, t, j % nj))],
        out_specs=pl.BlockSpec((tm, tn), lambda i, j, t: (i, j)),
        scratch_shapes=[pltpu.VMEM((tm, tn), F32)],
        compiler_params=_params("parallel", "parallel", "arbitrary"),
    )(h, dp)


def _mm_dw_out(y, dbr, name):
    s, e = y.shape
    d = dbr.shape[1]
    tm, tn, tt = _tile(e, 1024), _tile(d, 1024), _tile(s, 512)

    return pl.pallas_call(
        _dw_body(s // tt), name=name,
        out_shape=jax.ShapeDtypeStruct((e, d), BF16),
        grid=(e // tm, d // tn, s // tt),
        in_specs=[pl.BlockSpec((tt, tm), lambda i, j, t: (t, i)), pl.BlockSpec((tt, tn), lambda i, j, t: (t, j))],
        out_specs=pl.BlockSpec((tm, tn), lambda i, j, t: (i, j)),
        scratch_shapes=[pltpu.VMEM((tm, tn), F32)],
        compiler_params=_params("parallel", "parallel", "arbitrary"),
    )(y, dbr)


def _row_spec(ts, d):
    return pl.BlockSpec((ts, d), lambda i: (i, 0))


def _vec_spec(d):
    return pl.BlockSpec((1, d), lambda i: (0, 0))


def _norm_mod(x, g, scale, shift, name):
    s, d = x.shape
    ts = _tile(s, 512)

    def body(x_ref, g_ref, sc_ref, sh_ref, h_ref):
        xn, _ = _rms(x_ref[...])
        h_ref[...] = ((xn * g_ref[...]) * (1.0 + sc_ref[...]) + sh_ref[...]).astype(BF16)

    return pl.pallas_call(
        body, name=name, out_shape=jax.ShapeDtypeStruct((s, d), BF16), grid=(s // ts,),
        in_specs=[_row_spec(ts, d), _vec_spec(d), _vec_spec(d), _vec_spec(d)],
        out_specs=_row_spec(ts, d), compiler_params=_params("parallel"),
    )(x, g, scale, shift)


def _residual_norm_mod(x, br, gate, g, scale, shift, name):
    s, d = x.shape
    ts = _tile(s, 512)

    def body(x_ref, br_ref, gate_ref, g_ref, sc_ref, sh_ref, x1_ref, h_ref):
        x1 = x_ref[...] + gate_ref[...] * br_ref[...]
        x1_ref[...] = x1
        xn, _ = _rms(x1)
        h_ref[...] = ((xn * g_ref[...]) * (1.0 + sc_ref[...]) + sh_ref[...]).astype(BF16)

    return pl.pallas_call(
        body, name=name,
        out_shape=(jax.ShapeDtypeStruct((s, d), F32), jax.ShapeDtypeStruct((s, d), BF16)), grid=(s // ts,),
        in_specs=[_row_spec(ts, d), _row_spec(ts, d)] + [_vec_spec(d)] * 4,
        out_specs=(_row_spec(ts, d), _row_spec(ts, d)), compiler_params=_params("parallel"),
    )(x, br, gate, g, scale, shift)


def _head(x1, br, gate, gf, target, name):
    s, d = x1.shape
    ts = _tile(s, 512)

    def body(x1_ref, br_ref, gate_ref, gf_ref, tg_ref, dx_ref, dbr_ref, loss_ref, dgf_ref, dgate_ref):
        @pl.when(pl.program_id(0) == 0)
        def _():
            loss_ref[...] = jnp.zeros_like(loss_ref)
            dgf_ref[...] = jnp.zeros_like(dgf_ref)
            dgate_ref[...] = jnp.zeros_like(dgate_ref)
        br = br_ref[...]
        gate = gate_ref[...]
        gf = gf_ref[...]
        xn, r = _rms(x1_ref[...] + gate * br)
        err = xn * gf - tg_ref[...]
        loss_ref[...] += 0.5 * jnp.sum(jnp.mean(err * err, axis=-1, keepdims=True))
        dout = err * (1.0 / d)
        dgf_ref[...] += _colsum(dout * xn)
        dx = _rms_bwd(dout * gf, xn, r)
        dx_ref[...] = dx
        dgate_ref[...] += _colsum(dx * br)
        dbr_ref[...] = (gate * dx).astype(BF16)

    return pl.pallas_call(
        body, name=name,
        out_shape=(jax.ShapeDtypeStruct((s, d), F32), jax.ShapeDtypeStruct((s, d), BF16),
                   jax.ShapeDtypeStruct((SUBLANE, 128), F32), jax.ShapeDtypeStruct((1, d), F32),
                   jax.ShapeDtypeStruct((1, d), F32)),
        grid=(s // ts,),
        in_specs=[_row_spec(ts, d), _row_spec(ts, d), _vec_spec(d), _vec_spec(d), _row_spec(ts, d)],
        out_specs=(_row_spec(ts, d), _row_spec(ts, d), pl.BlockSpec((SUBLANE, 128), lambda i: (0, 0)),
                   _vec_spec(d), _vec_spec(d)),
        compiler_params=_params("arbitrary"),
    )(x1, br, gate, gf, target)


def _norm_mod_bwd(dh, x, dx_in, g, scale, name, br=None, gate=None):
    s, d = x.shape
    ts = _tile(s, 512)
    has_branch = br is not None

    def body(*refs):
        if has_branch:
            (dh_ref, x_ref, dxin_ref, g_ref, sc_ref, br_ref, gate_ref,
             dx_ref, dsh_ref, dsc_ref, dg_ref, dbr_ref, dgate_ref) = refs
        else:
            dh_ref, x_ref, dxin_ref, g_ref, sc_ref, dx_ref, dsh_ref, dsc_ref, dg_ref = refs

        @pl.when(pl.program_id(0) == 0)
        def _():
            dsh_ref[...] = jnp.zeros_like(dsh_ref)
            dsc_ref[...] = jnp.zeros_like(dsc_ref)
            dg_ref[...] = jnp.zeros_like(dg_ref)
            if has_branch:
                dgate_ref[...] = jnp.zeros_like(dgate_ref)
        dh = dh_ref[...]
        g = g_ref[...]
        xn, r = _rms(x_ref[...])
        dsh_ref[...] += _colsum(dh)
        dsc_ref[...] += _colsum(dh * (xn * g))
        da = dh * (1.0 + sc_ref[...])
        dg_ref[...] += _colsum(da * xn)
        dx = dxin_ref[...] + _rms_bwd(da * g, xn, r)
        dx_ref[...] = dx
        if has_branch:
            dgate_ref[...] += _colsum(dx * br_ref[...])
            dbr_ref[...] = (gate_ref[...] * dx).astype(BF16)

    vec_out = jax.ShapeDtypeStruct((1, d), F32)
    out_shape = [jax.ShapeDtypeStruct((s, d), F32), vec_out, vec_out, vec_out]
    out_specs = [_row_spec(ts, d), _vec_spec(d), _vec_spec(d), _vec_spec(d)]
    in_specs = [_row_spec(ts, d)] * 3 + [_vec_spec(d)] * 2
    args = [dh, x, dx_in, g, scale]
    if has_branch:
        out_shape += [jax.ShapeDtypeStruct((s, d), BF16), vec_out]
        out_specs += [_row_spec(ts, d), _vec_spec(d)]
        in_specs += [_row_spec(ts, d), _vec_spec(d)]
        args += [br, gate]
    return pl.pallas_call(
        body, name=name, out_shape=tuple(out_shape), grid=(s // ts,),
        in_specs=in_specs, out_specs=tuple(out_specs), compiler_params=_params("arbitrary"),
    )(*args)


def _conv_tiles(s, e):
    return _tile(s, 512), _tile(e, 512)


def _conv_fwd(proj, conv_w, conv_b, name):
    _, s, e = proj.shape
    ts, te = _conv_tiles(s, e)
    hb = ts // SUBLANE

    def body(bg_ref, cg_ref, xi_ref, z_ref, cgp_ref, xip_ref, w0_ref, w1_ref, w2_ref, b_ref, y_ref):
        cx = cg_ref[...] * xi_ref[...]
        before = jnp.where(pl.program_id(0) > 0, cgp_ref[...] * xip_ref[...], 0.0)
        conv = b_ref[...] + w2_ref[...] * cx
        conv = conv + w0_ref[...] * _shift_down(cx, before, 2)
        conv = conv + w1_ref[...] * _shift_down(cx, before, 1)
        z = z_ref[...]
        y_ref[...] = ((z * _sigmoid(z)) * bg_ref[...] * conv).astype(BF16)

    def part(q):
        return pl.BlockSpec((None, ts, te), lambda i, j: (q, i, j))

    def halo_before(q):
        return pl.BlockSpec((None, SUBLANE, te), lambda i, j: (q, jnp.maximum(i * hb - 1, 0), j))

    return pl.pallas_call(
        body, name=name, out_shape=jax.ShapeDtypeStruct((s, e), BF16), grid=(s // ts, e // te),
        in_specs=[part(0), part(1), part(2), part(3), halo_before(1), halo_before(2)]
        + [pl.BlockSpec((1, te), lambda i, j: (0, j))] * 4,
        out_specs=pl.BlockSpec((ts, te), lambda i, j: (i, j)),
        compiler_params=_params("parallel", "parallel"),
    )(proj, proj, proj, proj, proj, proj, *conv_w, conv_b)


def _conv_bwd(proj, dy, conv_w, conv_b, name):
    _, s, e = proj.shape
    ts, te = _conv_tiles(s, e)
    hb = ts // SUBLANE
    n_i = s // ts
    last_halo = s // SUBLANE - 1

    def body(bg_ref, cg_ref, xi_ref, z_ref, dy_ref, cgp_ref, xip_ref, bgn_ref, zn_ref, dyn_ref,
             w0_ref, w1_ref, w2_ref, b_ref, dp_ref, dw0_ref, dw1_ref, dw2_ref, db_ref):
        i = pl.program_id(1)

        @pl.when(i == 0)
        def _():
            for acc in (dw0_ref, dw1_ref, dw2_ref, db_ref):
                acc[...] = jnp.zeros_like(acc)
        bg, cg, xi, z, dy = bg_ref[...], cg_ref[...], xi_ref[...], z_ref[...], dy_ref[...]
        w0, w1, w2 = w0_ref[...], w1_ref[...], w2_ref[...]
        cx = cg * xi
        before = jnp.where(i > 0, cgp_ref[...] * xip_ref[...], 0.0)
        cx1 = _shift_down(cx, before, 1)
        cx2 = _shift_down(cx, before, 2)
        conv = b_ref[...] + w2 * cx
        conv = conv + w0 * cx2
        conv = conv + w1 * cx1
        sz, dsz = _silu_and_grad(z)
        dp_ref[3] = (dy * bg * conv * dsz).astype(BF16)
        dp_ref[0] = (dy * sz * conv).astype(BF16)
        dconv = dy * sz * bg
        zn = zn_ref[...]
        after = jnp.where(i < n_i - 1, dyn_ref[...] * (zn * _sigmoid(zn)) * bgn_ref[...], 0.0)
        db_ref[...] += _colsum(dconv)
        dw2_ref[...] += _colsum(dconv * cx)
        dw1_ref[...] += _colsum(dconv * cx1)
        dw0_ref[...] += _colsum(dconv * cx2)
        dcx = w2 * dconv + w1 * _shift_up(dconv, after, 1) + w0 * _shift_up(dconv, after, 2)
        dp_ref[1] = (dcx * xi).astype(BF16)
        dp_ref[2] = (dcx * cg).astype(BF16)

    def part(q):
        return pl.BlockSpec((None, ts, te), lambda j, i: (q, i, j))

    def halo_before(q):
        return pl.BlockSpec((None, SUBLANE, te), lambda j, i: (q, jnp.maximum(i * hb - 1, 0), j))

    def halo_after(q):
        return pl.BlockSpec((None, SUBLANE, te), lambda j, i: (q, jnp.minimum((i + 1) * hb, last_halo), j))

    return pl.pallas_call(
        body, name=name,
        out_shape=(jax.ShapeDtypeStruct((4, s, e), BF16),) + (jax.ShapeDtypeStruct((1, e), F32),) * 4,
        grid=(e // te, n_i),
        in_specs=[part(0), part(1), part(2), part(3), pl.BlockSpec((ts, te), lambda j, i: (i, j)),
                  halo_before(1), halo_before(2), halo_after(0), halo_after(3),
                  pl.BlockSpec((SUBLANE, te), lambda j, i: (jnp.minimum((i + 1) * hb, last_halo), j))]
        + [pl.BlockSpec((1, te), lambda j, i: (0, j))] * 4,
        out_specs=(pl.BlockSpec((4, ts, te), lambda j, i: (0, i, j)),) + (pl.BlockSpec((1, te), lambda j, i: (0, j)),) * 4,
        compiler_params=_params("parallel", "arbitrary"),
    )(proj, proj, proj, proj, dy, proj, proj, proj, proj, dy, *conv_w, conv_b)


def _tril(w):
    row = lax.broadcasted_iota(jnp.int32, w.shape, 0)
    col = lax.broadcasted_iota(jnp.int32, w.shape, 1)
    return jnp.where(row >= col, w, 0.0)


def _triu(w):
    row = lax.broadcasted_iota(jnp.int32, w.shape, 0)
    col = lax.broadcasted_iota(jnp.int32, w.shape, 1)
    return jnp.where(row <= col, w, 0.0)


def _layer_norm_fwd(v, g, b):
    mu = jnp.mean(v, axis=-1, keepdims=True)
    vc = v - mu
    rstd = lax.rsqrt(jnp.mean(vc * vc, axis=-1, keepdims=True) + LN_EPS)
    vhat = vc * rstd
    return vhat * g + b, vhat, rstd


def _gmlp_fwd(proj, ln_g, ln_b, w_s, b_s_t, name):
    _, s, e = proj.shape
    gw = e // GROUPS

    def body(pu_ref, pv_ref, pz_ref, g_ref, b_ref, ws_ref, bs_ref, y_ref, vn_scr, mix_scr):
        vn, _, _ = _layer_norm_fwd(_gelu(pv_ref[...]), g_ref[...], b_ref[...])
        vn_scr[...] = vn.astype(BF16)
        for gi in range(GROUPS):
            cols = slice(gi * gw, (gi + 1) * gw)
            wt = _tril(ws_ref[gi]).astype(BF16)
            mix_scr[:, cols] = jnp.dot(wt, vn_scr[:, cols], preferred_element_type=F32) + bs_ref[:, gi:gi + 1]
        z = pz_ref[...]
        y_ref[...] = ((z * _sigmoid(z)) * (_gelu(pu_ref[...]) * mix_scr[...])).astype(BF16)

    def part(q):
        return pl.BlockSpec((None, CHUNK, e), lambda i: (q, i, 0))

    return pl.pallas_call(
        body, name=name, out_shape=jax.ShapeDtypeStruct((s, e), BF16), grid=(s // CHUNK,),
        in_specs=[part(0), part(1), part(2), _vec_spec(e), _vec_spec(e),
                  pl.BlockSpec((GROUPS, CHUNK, CHUNK), lambda i: (0, 0, 0)),
                  pl.BlockSpec((CHUNK, GROUPS), lambda i: (0, 0))],
        out_specs=pl.BlockSpec((CHUNK, e), lambda i: (i, 0)),
        scratch_shapes=[pltpu.VMEM((CHUNK, e), BF16), pltpu.VMEM((CHUNK, e), F32)],
        compiler_params=_params("parallel"),
    )(proj, proj, proj, ln_g, ln_b, w_s, b_s_t)


def _gmlp_bwd(proj, dy, ln_g, ln_b, w_s, w_s_t, b_s_t, name):
    _, s, e = proj.shape
    gw = e // GROUPS
    n_i = s // CHUNK

    def body(pu_ref, pv_ref, pz_ref, dy_ref, g_ref, b_ref, ws_ref, wst_ref, bs_ref,
             dp_ref, dws_ref, dbs_ref, dlg_ref, dlb_ref, vn_scr, mix_scr, dm_scr, dvn_scr, dmacc_scr):
        i = pl.program_id(0)

        @pl.when(i == 0)
        def _():
            dws_ref[...] = jnp.zeros_like(dws_ref)
            dlg_ref[...] = jnp.zeros_like(dlg_ref)
            dlb_ref[...] = jnp.zeros_like(dlb_ref)
            dmacc_scr[...] = jnp.zeros_like(dmacc_scr)
        ln_g = g_ref[...]
        u, du_dpu = _gelu_and_grad(pu_ref[...])
        v, dv_dpv = _gelu_and_grad(pv_ref[...])
        vn, vhat, rstd = _layer_norm_fwd(v, ln_g, b_ref[...])
        vn_scr[...] = vn.astype(BF16)
        for gi in range(GROUPS):
            cols = slice(gi * gw, (gi + 1) * gw)
            wt = _tril(ws_ref[gi]).astype(BF16)
            mix_scr[:, cols] = jnp.dot(wt, vn_scr[:, cols], preferred_element_type=F32) + bs_ref[:, gi:gi + 1]
        mixed = mix_scr[...]
        dy = dy_ref[...]
        sz, dsz = _silu_and_grad(pz_ref[...])
        ds = dy * sz
        dp_ref[2] = (dy * (u * mixed) * dsz).astype(BF16)
        dp_ref[0] = (ds * mixed * du_dpu).astype(BF16)
        dm = ds * u
        dmacc_scr[...] += dm
        dm_scr[...] = dm.astype(BF16)
        for gi in range(GROUPS):
            cols = slice(gi * gw, (gi + 1) * gw)
            dmg = dm_scr[:, cols]
            dws_ref[gi] += lax.dot_general(dmg, vn_scr[:, cols], (((1,), (1,)), ((), ())), preferred_element_type=F32)
            wtt = _triu(wst_ref[gi]).astype(BF16)
            dvn_scr[:, cols] = jnp.dot(wtt, dmg, preferred_element_type=F32)
        dvn = dvn_scr[...]
        dlg_ref[...] += _colsum(dvn * vhat)
        dlb_ref[...] += _colsum(dvn)
        dvh = dvn * ln_g
        dv = rstd * (dvh - jnp.mean(dvh, axis=-1, keepdims=True) - vhat * jnp.mean(dvh * vhat, axis=-1, keepdims=True))
        dp_ref[1] = (dv * dv_dpv).astype(BF16)

        @pl.when(i == n_i - 1)
        def _():
            for gi in range(GROUPS):
                dws_ref[gi] = _tril(dws_ref[gi])
                dbs_ref[:, gi:gi + 1] = jnp.sum(dmacc_scr[:, gi * gw:(gi + 1) * gw], axis=1, keepdims=True)

    def part(q):
        return pl.BlockSpec((None, CHUNK, e), lambda i: (q, i, 0))

    w_spec = pl.BlockSpec((GROUPS, CHUNK, CHUNK), lambda i: (0, 0, 0))
    bs_spec = pl.BlockSpec((CHUNK, GROUPS), lambda i: (0, 0))
    return pl.pallas_call(
        body, name=name,
        out_shape=(jax.ShapeDtypeStruct((3, s, e), BF16), jax.ShapeDtypeStruct((GROUPS, CHUNK, CHUNK), F32),
                   jax.ShapeDtypeStruct((CHUNK, GROUPS), F32), jax.ShapeDtypeStruct((1, e), F32),
                   jax.ShapeDtypeStruct((1, e), F32)),
        grid=(n_i,),
        in_specs=[part(0), part(1), part(2), pl.BlockSpec((CHUNK, e), lambda i: (i, 0)), _vec_spec(e), _vec_spec(e),
                  w_spec, w_spec, bs_spec],
        out_specs=(pl.BlockSpec((3, CHUNK, e), lambda i: (0, i, 0)), w_spec, bs_spec, _vec_spec(e), _vec_spec(e)),
        scratch_shapes=[pltpu.VMEM((CHUNK, e), BF16), pltpu.VMEM((CHUNK, e), F32), pltpu.VMEM((CHUNK, e), BF16),
                        pltpu.VMEM((CHUNK, e), F32), pltpu.VMEM((CHUNK, e), F32)],
        compiler_params=_params("arbitrary"),
    )(proj, proj, proj, dy, ln_g, ln_b, w_s, w_s_t, b_s_t)


def _mod_fwd(c_all, mod_w, mod_b_cols, name):
    n_layer, d, w = mod_w.shape

    def body(c_ref, w_ref, b_ref, ca_ref, o_ref):
        c = c_ref[...]
        ca = c * _sigmoid(c)
        ca_ref[...] = ca
        for li in range(n_layer):
            o_ref[li * N_DEV:(li + 1) * N_DEV, :] = (
                jnp.dot(ca, w_ref[li], preferred_element_type=F32, precision=lax.Precision.HIGHEST) + b_ref[li])

    return pl.pallas_call(
        body, name=name,
        out_shape=(jax.ShapeDtypeStruct((N_DEV, d), F32), jax.ShapeDtypeStruct((n_layer * N_DEV, w), F32)),
        in_specs=[VMEM_SPEC] * 3, out_specs=(VMEM_SPEC, VMEM_SPEC),
        compiler_params=pltpu.CompilerParams(vmem_limit_bytes=VMEM_LIMIT),
    )(c_all, mod_w, mod_b_cols)


def _adamw(w, g, m, v):
    m = ADAM_B1 * m + (1.0 - ADAM_B1) * g
    v = ADAM_B2 * v + (1.0 - ADAM_B2) * (g * g)
    m_hat = m / (1.0 - ADAM_B1 ** ADAM_STEP)
    v_hat = v / (1.0 - ADAM_B2 ** ADAM_STEP)
    delta = -ADAM_LR * (m_hat / (jnp.sqrt(v_hat) + ADAM_EPS) + ADAM_WD * w)
    return delta, m, v


def _adamw_2d(w, g, m, v, name):
    r, c = w.shape
    tr, tc = _tile(r, 512), _tile(c, 1024)

    def body(w_ref, g_ref, m_ref, v_ref, d_ref, nm_ref, nv_ref):
        d_ref[...], nm_ref[...], nv_ref[...] = _adamw(w_ref[...], g_ref[...], m_ref[...], v_ref[...])

    spec = pl.BlockSpec((tr, tc), lambda i, j: (i, j))
    shape = jax.ShapeDtypeStruct((r, c), F32)
    return pl.pallas_call(
        body, name=name, out_shape=(shape, shape, shape), grid=(r // tr, c // tc),
        in_specs=[spec] * 4, out_specs=(spec, spec, spec), compiler_params=_params("parallel", "parallel"),
    )(w, g, m, v)


def _mod_w_update(ca_t, dmod_cols, w, m, v, name):
    n_layer, d, wd = w.shape
    tr = _tile(d, 256)

    def body(ca_ref, dm_ref, w_ref, m_ref, v_ref, g_ref, d_ref, nm_ref, nv_ref):
        ca = ca_ref[...]
        dm = dm_ref[...]
        g = ca[:, 0:1] * dm[0:1, :]
        for b in range(1, N_DEV):
            g = g + ca[:, b:b + 1] * dm[b:b + 1, :]
        g_ref[...] = g
        d_ref[...], nm_ref[...], nv_ref[...] = _adamw(w_ref[...], g, m_ref[...], v_ref[...])

    spec = pl.BlockSpec((None, tr, wd), lambda l, i: (l, i, 0))
    shape = jax.ShapeDtypeStruct((n_layer, d, wd), F32)
    return pl.pallas_call(
        body, name=name, out_shape=(shape,) * 4, grid=(n_layer, d // tr),
        in_specs=[pl.BlockSpec((tr, N_DEV), lambda l, i: (i, 0)), pl.BlockSpec((None, N_DEV, wd), lambda l, i: (l, 0, 0)),
                  spec, spec, spec],
        out_specs=(spec,) * 4, compiler_params=_params("parallel", "parallel"),
    )(ca_t, dmod_cols, w, m, v)


def _adamw_small(ws, gs, ms, vs, name):
    n = len(ws)

    def body(*refs):
        ins, outs = refs[:4 * n], refs[4 * n:]
        for k in range(n):
            delta, nm, nv = _adamw(ins[k][...], ins[n + k][...], ins[2 * n + k][...], ins[3 * n + k][...])
            outs[3 * k][...] = delta
            outs[3 * k + 1][...] = nm
            outs[3 * k + 2][...] = nv

    out_shape = []
    for w in ws:
        out_shape += [jax.ShapeDtypeStruct(w.shape, F32)] * 3
    outs = pl.pallas_call(
        body, name=name, out_shape=tuple(out_shape),
        in_specs=[VMEM_SPEC] * (4 * n), out_specs=tuple([VMEM_SPEC] * (3 * n)),
        compiler_params=pltpu.CompilerParams(vmem_limit_bytes=VMEM_LIMIT),
    )(*ws, *gs, *ms, *vs)
    return [tuple(outs[3 * k:3 * k + 3]) for k in range(n)]


def _place():
    return lax.axis_index("x"), lax.axis_index("y"), lax.axis_index("c")


def _other_chips(x, y):
    return [(1 - x, y), (x, 1 - y), (1 - x, 1 - y)]


def _all_gather(block, name):
    m_per, n = block.shape

    def body(x_ref, out_ref, send_sems, recv_sems, local_sem):
        x, y, c = _place()
        me, sibling = (x, y, c), (x, y, 1 - c)
        chips = _other_chips(x, y)

        def rows(px, py, pc):
            return out_ref.at[pl.ds((4 * px + 2 * py + pc) * m_per, m_per), :]

        def copy(k, blk, to, src=None):
            return pltpu.make_async_remote_copy(
                src_ref=rows(*blk) if src is None else src, dst_ref=rows(*blk),
                send_sem=send_sems.at[k], recv_sem=recv_sems.at[k], device_id=to, device_id_type=MESH)

        mine = pltpu.make_async_copy(x_ref, rows(*me), local_sem)
        mine.start()
        first = [copy(0, me, sibling, src=x_ref)]
        first += [copy(1 + j, me, (*chip, c), src=x_ref) for j, chip in enumerate(chips)]
        for cp in first:
            cp.start()
        passed = [copy(4 + j, (*chip, c), sibling) for j, chip in enumerate(chips)]
        for j, chip in enumerate(chips):
            copy(1 + j, (*chip, c), me).wait_recv()
            passed[j].start()
        copy(0, sibling, me).wait_recv()
        for j, chip in enumerate(chips):
            copy(4 + j, (*chip, 1 - c), me).wait_recv()
        for cp in first + passed:
            cp.wait_send()
        mine.wait()

    return pl.pallas_call(
        body, name=name, out_shape=jax.ShapeDtypeStruct((N_DEV * m_per, n), F32),
        in_specs=[VMEM_SPEC], out_specs=VMEM_SPEC,
        scratch_shapes=[pltpu.SemaphoreType.DMA((7,)), pltpu.SemaphoreType.DMA((7,)), pltpu.SemaphoreType.DMA],
        compiler_params=pltpu.CompilerParams(vmem_limit_bytes=VMEM_LIMIT),
    )(block)


def _hbm(a):
    return pltpu.with_memory_space_constraint(a, pltpu.HBM)


def _place_own(shard, kind, chip_idx, name):
    r, cdim = shard.shape
    tr, tc = _tile(r, 512), _tile(cdim, 1024)
    nrb, ncb = r // tr, cdim // tc

    def body(k_ref, s_ref, o_ref):
        o_ref[...] = s_ref[...].astype(BF16)

    if kind == "col":
        full, o_map = (r, N_CHIP * cdim), lambda i, j, kr: (i, kr[0] * ncb + j)
    else:
        full, o_map = (N_CHIP * r, cdim), lambda i, j, kr: (kr[0] * nrb + i, j)
    return pl.pallas_call(
        body, name=name, out_shape=jax.ShapeDtypeStruct(full, BF16),
        grid_spec=pltpu.PrefetchScalarGridSpec(
            num_scalar_prefetch=1, grid=(nrb, ncb),
            in_specs=[pl.BlockSpec((tr, tc), lambda i, j, kr: (i, j))],
            out_specs=pl.BlockSpec((tr, tc), o_map)),
        compiler_params=_params("parallel", "parallel"),
    )(chip_idx, shard)


def _weight_window(ref, kind, shard_shape, k, half):
    r, cdim = shard_shape
    hr = r // 2
    if kind == "col":
        return ref.at[pl.ds(half * hr, hr), pl.ds(pl.multiple_of(k * cdim, 128), cdim)]
    return ref.at[pl.ds(pl.multiple_of(k * r + half * hr, 2 * SUBLANE), hr), :]


def _gather_weights_start(fulls, kinds, shard_shapes, groups, after, name):
    n, ng = len(fulls), len(groups)

    def body(*refs):
        ins = refs[:n]
        sems = refs[n + 1:n + 1 + 2 * ng]
        token = refs[2 * n + 1 + 2 * ng]
        x, y, c = _place()
        chips = _other_chips(x, y)
        for g, group in enumerate(groups):
            for pos, (w, j) in enumerate(group):
                own = _weight_window(ins[w], kinds[w], shard_shapes[w], 2 * x + y, c)
                for cc in range(2):
                    pltpu.make_async_remote_copy(
                        src_ref=own, dst_ref=own,
                        send_sem=sems[2 * g].at[2 * pos + cc], recv_sem=sems[2 * g + 1].at[2 * pos + c],
                        device_id=(*chips[j], cc), device_id_type=MESH).start()
        token[...] = jnp.zeros_like(token)

    sem_shapes = []
    for group in groups:
        sem_shapes += [pltpu.SemaphoreType.DMA((2 * len(group),))] * 2
    outs = pl.pallas_call(
        body, name=name,
        out_shape=tuple(sem_shapes) + tuple(pltpu.HBM(f.shape, f.dtype) for f in fulls)
        + (jax.ShapeDtypeStruct((SUBLANE, 128), F32),),
        in_specs=[HBM_SPEC] * n + [ANY_SPEC], out_specs=(SEM_SPEC,) * (2 * ng) + (HBM_SPEC,) * n + (VMEM_SPEC,),
        input_output_aliases={w: 2 * ng + w for w in range(n)},
        compiler_params=pltpu.CompilerParams(has_side_effects=EFFECT),
    )(*[_hbm(f) for f in fulls], after)
    sems = [(outs[2 * g], outs[2 * g + 1]) for g in range(ng)]
    return sems, list(outs[2 * ng:2 * ng + n]), outs[2 * ng + n]


def _gather_weights_wait(sems, group, fulls, kinds, shard_shapes, after, name):
    n = len(fulls)

    def body(*refs):
        ins = refs[:n]
        send_sems, recv_sems = refs[n], refs[n + 1]
        x, y, c = _place()
        chips = _other_chips(x, y)
        for pos, (w, j) in enumerate(group):
            own = _weight_window(ins[w], kinds[w], shard_shapes[w], 2 * x + y, c)
            for cc in range(2):
                landed = _weight_window(ins[w], kinds[w], shard_shapes[w], 2 * chips[j][0] + chips[j][1], cc)
                k = 2 * pos + cc
                pltpu.make_async_remote_copy(
                    src_ref=own, dst_ref=own, send_sem=send_sems.at[k], recv_sem=recv_sems.at[k],
                    device_id=(*chips[j], cc), device_id_type=MESH).wait_send()
                pltpu.make_async_remote_copy(
                    src_ref=landed, dst_ref=landed, send_sem=send_sems.at[k], recv_sem=recv_sems.at[k],
                    device_id=(*chips[j], cc), device_id_type=MESH).wait_recv()

    return list(pl.pallas_call(
        body, name=name, out_shape=tuple(pltpu.HBM(f.shape, f.dtype) for f in fulls),
        in_specs=[HBM_SPEC] * n + [SEM_SPEC, SEM_SPEC, ANY_SPEC], out_specs=(HBM_SPEC,) * n,
        input_output_aliases={w: w for w in range(n)},
        compiler_params=pltpu.CompilerParams(has_side_effects=EFFECT),
    )(*fulls, sems[0], sems[1], after))


def _grad_piece(ref, kind, h, cdim, k, half):
    if kind == "col":
        return ref.at[pl.ds(half * h, h), pl.ds(pl.multiple_of(k * cdim, 128), cdim)]
    return ref.at[k, pl.ds(half * h, h), :]


def _grad_dims(g, kind):
    return (g.shape[0] // 2, g.shape[1] // N_CHIP) if kind == "col" else (g.shape[1] // 2, g.shape[2])


def _reduce_start(grads, kinds, name):
    n = len(grads)
    dims = [_grad_dims(g, kind) for g, kind in zip(grads, kinds)]
    lands = [lax.empty((N_DEV - 1, h, cdim), g.dtype) for g, (h, cdim) in zip(grads, dims)]

    def body(*refs):
        g_ins, land_ins = refs[:n], refs[n:2 * n]
        send_sems, recv_sems = refs[2 * n], refs[2 * n + 1]
        token = refs[4 * n + 2]
        x, y, c = _place()
        for w in range(n):
            h, cdim = dims[w]
            base = (N_DEV - 1) * w
            pltpu.make_async_remote_copy(
                src_ref=_grad_piece(g_ins[w], kinds[w], h, cdim, 2 * x + y, 1 - c), dst_ref=land_ins[w].at[0],
                send_sem=send_sems.at[base], recv_sem=recv_sems.at[base],
                device_id=(x, y, 1 - c), device_id_type=MESH).start()
            for j, chip in enumerate(_other_chips(x, y)):
                for cc in range(2):
                    pltpu.make_async_remote_copy(
                        src_ref=_grad_piece(g_ins[w], kinds[w], h, cdim, 2 * chip[0] + chip[1], cc),
                        dst_ref=land_ins[w].at[1 + 2 * j + c],
                        send_sem=send_sems.at[base + 1 + 2 * j + cc], recv_sem=recv_sems.at[base + 1 + 2 * j + c],
                        device_id=(*chip, cc), device_id_type=MESH).start()
        token[...] = jnp.zeros_like(token)

    sems = pltpu.SemaphoreType.DMA(((N_DEV - 1) * n,))
    outs = pl.pallas_call(
        body, name=name,
        out_shape=(sems, sems) + tuple(pltpu.HBM(a.shape, a.dtype) for a in list(grads) + lands)
        + (jax.ShapeDtypeStruct((SUBLANE, 128), F32),),
        in_specs=[HBM_SPEC] * (2 * n), out_specs=(SEM_SPEC, SEM_SPEC) + (HBM_SPEC,) * (2 * n) + (VMEM_SPEC,),
        input_output_aliases={i: 2 + i for i in range(2 * n)},
        compiler_params=pltpu.CompilerParams(has_side_effects=EFFECT),
    )(*[_hbm(a) for a in list(grads) + lands])
    return outs[0], outs[1], list(outs[2:2 + n]), list(outs[2 + n:2 + 2 * n]), outs[2 + 2 * n]


def _reduce_wait(send_sems, recv_sems, grads, lands, kinds, after, name):
    n = len(grads)
    dims = [_grad_dims(g, kind) for g, kind in zip(grads, kinds)]

    def body(*refs):
        g_ins, land_ins = refs[:n], refs[n:2 * n]
        send_sems, recv_sems = refs[2 * n], refs[2 * n + 1]
        x, y, c = _place()
        for w in range(n):
            h, cdim = dims[w]
            piece = _grad_piece(g_ins[w], kinds[w], h, cdim, 2 * x + y, c)
            for s in range(N_DEV - 1):
                k = (N_DEV - 1) * w + s
                slot = land_ins[w].at[s]
                pltpu.make_async_remote_copy(
                    src_ref=piece, dst_ref=slot, send_sem=send_sems.at[k], recv_sem=recv_sems.at[k],
                    device_id=(x, y, 1 - c), device_id_type=MESH).wait_send()
                pltpu.make_async_remote_copy(
                    src_ref=piece, dst_ref=slot, send_sem=send_sems.at[k], recv_sem=recv_sems.at[k],
                    device_id=(x, y, 1 - c), device_id_type=MESH).wait_recv()

    outs = pl.pallas_call(
        body, name=name, out_shape=tuple(pltpu.HBM(a.shape, a.dtype) for a in list(grads) + list(lands)),
        in_specs=[HBM_SPEC] * (2 * n) + [SEM_SPEC, SEM_SPEC, ANY_SPEC], out_specs=(HBM_SPEC,) * (2 * n),
        input_output_aliases={i: i for i in range(2 * n)},
        compiler_params=pltpu.CompilerParams(has_side_effects=EFFECT),
    )(*grads, *lands, send_sems, recv_sems, after)
    return list(outs[:n]), list(outs[n:])


def _add_pieces(g, land, kind, chip_idx, core_idx, name):
    _, h, cdim = land.shape
    tr, tc = _tile(h, 256), _tile(cdim, 2048)
    nrb, ncb = h // tr, cdim // tc

    def body(k_ref, c_ref, g_ref, l_ref, o_ref):
        acc = g_ref[...].astype(F32)
        for s in range(N_DEV - 1):
            acc = acc + l_ref[s].astype(F32)
        o_ref[...] = acc

    if kind == "col":
        g_spec = pl.BlockSpec((tr, tc), lambda i, j, kr, cr: (cr[0] * nrb + i, kr[0] * ncb + j))
    else:
        g_spec = pl.BlockSpec((None, tr, tc), lambda i, j, kr, cr: (kr[0], cr[0] * nrb + i, j))
    return pl.pallas_call(
        body, name=name, out_shape=jax.ShapeDtypeStruct((2 * h, cdim), F32),
        grid_spec=pltpu.PrefetchScalarGridSpec(
            num_scalar_prefetch=2, grid=(nrb, ncb),
            in_specs=[g_spec, pl.BlockSpec((N_DEV - 1, tr, tc), lambda i, j, kr, cr: (0, i, j))],
            out_specs=pl.BlockSpec((tr, tc), lambda i, j, kr, cr: (cr[0] * nrb + i, j))),
        compiler_params=_params("parallel", "parallel"),
    )(chip_idx, core_idx, g, land)


def _join_halves(shards, name):
    n = len(shards)

    def body(*refs):
        ins, outs = refs[:n], refs[n:2 * n]
        send_sems, recv_sems = refs[2 * n:]
        x, y, c = _place()
        copies = []
        for w in range(n):
            h = shards[w].shape[0] // 2
            cp = pltpu.make_async_remote_copy(
                src_ref=ins[w].at[pl.ds(c * h, h), :], dst_ref=outs[w].at[pl.ds(c * h, h), :],
                send_sem=send_sems.at[w], recv_sem=recv_sems.at[w], device_id=(x, y, 1 - c), device_id_type=MESH)
            cp.start()
            copies.append(cp)
        for w in range(n):
            h = shards[w].shape[0] // 2
            theirs = outs[w].at[pl.ds((1 - c) * h, h), :]
            pltpu.make_async_remote_copy(
                src_ref=theirs, dst_ref=theirs, send_sem=send_sems.at[w], recv_sem=recv_sems.at[w],
                device_id=(x, y, 1 - c), device_id_type=MESH).wait_recv()
        for cp in copies:
            cp.wait_send()

    return pl.pallas_call(
        body, name=name, out_shape=tuple(jax.ShapeDtypeStruct(s.shape, s.dtype) for s in shards),
        in_specs=[HBM_SPEC] * n, out_specs=tuple([HBM_SPEC] * n),
        input_output_aliases={w: w for w in range(n)},
        scratch_shapes=[pltpu.SemaphoreType.DMA((n,))] * 2,
    )(*shards)


def _sum_devices(gathered, rows, name):
    n = gathered.shape[1]

    def body(g_ref, o_ref):
        acc = g_ref[0:rows, :]
        for dev in range(1, N_DEV):
            acc = acc + g_ref[dev * rows:(dev + 1) * rows, :]
        o_ref[...] = acc

    return pl.pallas_call(
        body, name=name, out_shape=jax.ShapeDtypeStruct((rows, n), F32),
        in_specs=[VMEM_SPEC], out_specs=VMEM_SPEC,
        compiler_params=pltpu.CompilerParams(vmem_limit_bytes=VMEM_LIMIT),
    )(gathered)


def _pack(vectors, width):
    flat = [v.reshape(-1) for v in vectors]
    offsets, total = [], 0
    for f in flat:
        offsets.append(total)
        total += f.shape[0]
    rows = -(-total // (width * SUBLANE)) * SUBLANE
    flat.append(jnp.zeros((rows * width - total,), F32))
    return jnp.concatenate(flat).reshape(rows, width), offsets


def kernel(x, c, mod_w, mod_b, norm_g, a_w_in, a_conv_w, a_conv_b, a_w_out, b_w_in, b_ln_g, b_ln_b, b_w_s, b_b_s, b_w_out, final_g, loss_target, m_mod_w, m_mod_b, m_norm_g, m_a_w_in, m_a_conv_w, m_a_conv_b, m_a_w_out, m_b_w_in, m_b_ln_g, m_b_ln_b, m_b_w_s, m_b_b_s, m_b_w_out, m_final_g, v_mod_w, v_mod_b, v_norm_g, v_a_w_in, v_a_conv_w, v_a_conv_b, v_a_w_out, v_b_w_in, v_b_ln_g, v_b_ln_b, v_b_w_s, v_b_b_s, v_b_w_out, v_final_g):
    seq, d = x.shape[1], x.shape[2]
    e = a_conv_b.shape[1]
    wd = mod_w.shape[2]
    ax, ay, ac = _place()
    chip = 2 * ax + ay
    dev = 2 * chip + ac
    chip_idx = jnp.reshape(chip, (1,)).astype(jnp.int32)
    core_idx = jnp.reshape(ac, (1,)).astype(jnp.int32)

    x2d = x[0]
    target = loss_target[0]

    w_names = ["a_w_in", "a_w_out", "b_w_in", "b_w_out"]
    w_kinds = ["col", "row", "col", "row"]
    w_shards = [a_w_in[0], a_w_out[0], b_w_in[0], b_w_out[0]]
    w_shapes = [sh.shape for sh in w_shards]
    placed = [_place_own(sh, kind, chip_idx, "place_" + nm) for nm, sh, kind in zip(w_names, w_shards, w_kinds)]
    w_groups = [[(0, 0), (0, 1)], [(0, 2)]] + [[(w, j) for j in range(3)] for w in (1, 2, 3)]

    es = e // N_CHIP
    packed0, offs0 = _pack([c, a_conv_w, b_ln_g, b_ln_b], 1024)
    gathered0 = _all_gather(packed0, "gather_params").reshape(N_DEV, -1)
    c_all = gathered0[:, :d]
    per_chip = gathered0[0::2]

    def from_chips_cols(k, rows_):
        got = per_chip[:, offs0[k]:offs0[k] + rows_ * es].reshape(N_CHIP, rows_, es)
        return jnp.transpose(got, (1, 0, 2)).reshape(rows_, e)

    conv_w_full = from_chips_cols(1, 3)
    conv_w = [conv_w_full[k:k + 1] for k in range(3)]
    ln_g, ln_b = from_chips_cols(2, 1), from_chips_cols(3, 1)
    mod_b_cols = lax.dynamic_slice_in_dim(mod_b, chip * wd, wd, axis=1)[:, None, :]
    c_act, mod_part = _mod_fwd(c_all, mod_w, mod_b_cols, "mod_fwd")
    n_layer = mod_w.shape[0]
    mod_gathered = _all_gather(mod_part, "gather_mod")
    w_sems, w_flight, w_token = _gather_weights_start(
        placed, w_kinds, w_shapes, w_groups, mod_gathered, "gather_weights_start")
    mod_all = mod_gathered.reshape(N_CHIP, 2, n_layer, N_DEV, wd)[:, 0]
    mod_all = jnp.transpose(mod_all, (1, 2, 0, 3)).reshape(n_layer, N_DEV, N_CHIP * wd)
    mod_me = lax.dynamic_index_in_dim(mod_all, dev, axis=1, keepdims=False)
    shift = [mod_me[l:l + 1, 0:d] for l in range(n_layer)]
    scale = [mod_me[l:l + 1, d:2 * d] for l in range(n_layer)]
    gate = [mod_me[l:l + 1, 2 * d:3 * d] for l in range(n_layer)]

    g0, g1, gf = norm_g[0:1], norm_g[1:2], final_g[None, :]
    h0 = _norm_mod(x2d, g0, scale[0], shift[0], "norm_mod0")

    def slab(r):
        return jnp.bitwise_xor(chip_idx, r)

    def arrived(g, after, name):
        w = w_groups[g][0][0]
        return _gather_weights_wait(w_sems[g], [(0, j) for _, j in w_groups[g]], [w_flight[w]], [w_kinds[w]],
                                    [w_shapes[w]], after, name)[0]

    proj0 = _mm_proj_slab(h0, w_flight[0], None, slab(0), 4, "a_proj_own", w_token)
    w_flight[0] = arrived(0, proj0, "gather_wait_near")
    proj0 = _mm_proj_slab(h0, w_flight[0], proj0, slab(2), 4, "a_proj_x", w_token)
    proj0 = _mm_proj_slab(h0, w_flight[0], proj0, slab(1), 4, "a_proj_y", w_token)
    wa_in = arrived(1, proj0, "gather_wait_far")
    proj0 = _mm_proj_slab(h0, wa_in, proj0, slab(3), 4, "a_proj_far", w_token)
    y0 = _conv_fwd(proj0, conv_w, a_conv_b, "conv_fwd")
    wa_out = arrived(2, y0, "gather_wait_a_w_out")
    br0 = _mm_nn(y0, wa_out, "a_out")
    x1, h1 = _residual_norm_mod(x2d, br0, gate[0], g1, scale[1], shift[1], "residual_norm_mod1")
    wb_in = arrived(3, h1, "gather_wait_b_w_in")
    proj1 = _mm_proj(h1, wb_in, 3, "b_proj", w_token)
    b_s_t = jnp.transpose(b_b_s[0])
    y1 = _gmlp_fwd(proj1, ln_g, ln_b, b_w_s[0], b_s_t, "gmlp_fwd")
    wb_out = arrived(4, y1, "gather_wait_b_w_out")
    br1 = _mm_nn(y1, wb_out, "b_out")
    dx2, dbr1, loss_part, g_final_g, dgate1 = _head(x1, br1, gate[1], gf, target, "head")

    dy1 = _mm_nt(dbr1, wb_out, "b_out_dx")
    gw_b_out = _mm_dw_out(y1, dbr1, "b_out_dw")
    dproj1, g_w_s, g_b_s_t, g_ln_g, g_ln_b = _gmlp_bwd(
        proj1, dy1, ln_g, ln_b, b_w_s[0], jnp.swapaxes(b_w_s[0], 1, 2), b_s_t, "gmlp_bwd")
    gw_b_in = _mm_dw_in(h1, dproj1, "b_proj_dw")
    b_kinds = ["col", "row"]
    b_send, b_recv, b_grads, b_lands, b_token = _reduce_start(
        [gw_b_in, gw_b_out.reshape(N_CHIP, e // N_CHIP, d)], b_kinds, "reduce_b_start")
    dh1 = _mm_dh(dproj1, wb_in, "b_proj_dx", b_token)
    dx1, dshift1, dscale1, g_g1, dbr0, dgate0 = _norm_mod_bwd(
        dh1, x1, dx2, g1, scale[1], "norm_mod1_bwd", br=br0, gate=gate[0])

    dy0 = _mm_nt(dbr0, wa_out, "a_out_dx")
    gw_a_out = _mm_dw_out(y0, dbr0, "a_out_dw")
    dproj0, g_w0, g_w1, g_w2, g_conv_b = _conv_bwd(proj0, dy0, conv_w, a_conv_b, "conv_bwd")
    gw_a_in = _mm_dw_in(h0, dproj0, "a_proj_dw")
    a_kinds = ["col", "row"]
    a_send, a_recv, a_grads, a_lands, a_token = _reduce_start(
        [gw_a_in, gw_a_out.reshape(N_CHIP, e // N_CHIP, d)], a_kinds, "reduce_a_start")
    dh0 = _mm_dh(dproj0, wa_in, "a_proj_dx", a_token)
    grad_x, dshift0, dscale0, g_g0 = _norm_mod_bwd(dh0, x2d, dx1, g0, scale[0], "norm_mod0_bwd")

    dmod = jnp.concatenate([dshift0, dscale0, dgate0, dshift1, dscale1, dgate1], axis=1)
    small = [loss_part[0, 0:1], g_final_g, g_g0, g_g1, jnp.concatenate([g_w0, g_w1, g_w2], axis=0), g_conv_b, g_ln_g, g_ln_b,
             jnp.transpose(g_b_s_t), g_w_s, dmod]
    packed, offs = _pack(small, 1024)
    rows = packed.shape[0]
    gathered = _all_gather(packed, "gather_small")
    summed = _sum_devices(gathered, rows, "sum_small").reshape(-1)

    def take(k, shape):
        size = math.prod(shape)
        return summed[offs[k]:offs[k] + size].reshape(shape)

    loss = take(0, ())
    grad_final_g = take(1, (d,))
    grad_norm_g = jnp.concatenate([take(2, (1, d)), take(3, (1, d))], axis=0)
    grad_conv_w_full = take(4, (3, e))
    grad_a_conv_b = take(5, (1, e))
    grad_ln_g_full = take(6, (1, e))
    grad_ln_b_full = take(7, (1, e))
    grad_b_b_s = take(8, (1, GROUPS, CHUNK))
    grad_b_w_s = take(9, (1, GROUPS, CHUNK, CHUNK))
    grad_mod_b = take(10, (n_layer, 3 * d))
    grad_a_conv_w = lax.dynamic_slice_in_dim(grad_conv_w_full, chip * es, es, axis=1)[None]
    grad_b_ln_g = lax.dynamic_slice_in_dim(grad_ln_g_full, chip * es, es, axis=1)
    grad_b_ln_b = lax.dynamic_slice_in_dim(grad_ln_b_full, chip * es, es, axis=1)
    dmod_all = gathered.reshape(N_DEV, rows * 1024)[:, offs[10]:offs[10] + n_layer * 3 * d]
    dmod_all = dmod_all.reshape(N_DEV, n_layer, 3 * d)
    dmod_cols = jnp.transpose(lax.dynamic_slice_in_dim(dmod_all, chip * wd, wd, axis=2), (1, 0, 2))

    b_grads, b_lands = _reduce_wait(b_send, b_recv, b_grads, b_lands, b_kinds, summed, "reduce_b_wait")
    a_grads, a_lands = _reduce_wait(a_send, a_recv, a_grads, a_lands, a_kinds, summed, "reduce_a_wait")
    summed_halves = [
        _add_pieces(g, land, kind, chip_idx, core_idx, "add_pieces_" + nm)
        for g, land, kind, nm in [(a_grads[0], a_lands[0], "col", "a_w_in"), (b_grads[0], b_lands[0], "col", "b_w_in"),
                                  (a_grads[1], a_lands[1], "row", "a_w_out"), (b_grads[1], b_lands[1], "row", "b_w_out")]]
    g_a_w_in, g_b_w_in, g_a_w_out, g_b_w_out = _join_halves(summed_halves, "join_halves")

    grad_mod_w, delta_mod_w, new_m_mod_w, new_v_mod_w = _mod_w_update(
        jnp.transpose(c_act), dmod_cols, mod_w, m_mod_w, v_mod_w, "mod_w_update")
    upd = {}
    for nm, w, g, m, v in [("a_w_in", a_w_in, g_a_w_in, m_a_w_in, v_a_w_in), ("a_w_out", a_w_out, g_a_w_out, m_a_w_out, v_a_w_out),
                           ("b_w_in", b_w_in, g_b_w_in, m_b_w_in, v_b_w_in), ("b_w_out", b_w_out, g_b_w_out, m_b_w_out, v_b_w_out)]:
        upd[nm] = tuple(o[None] for o in _adamw_2d(w[0], g, m[0], v[0], "adamw_" + nm))
    small_w = [("mod_b", mod_b, grad_mod_b, m_mod_b, v_mod_b), ("norm_g", norm_g, grad_norm_g, m_norm_g, v_norm_g),
               ("a_conv_w", a_conv_w, grad_a_conv_w, m_a_conv_w, v_a_conv_w),
               ("a_conv_b", a_conv_b, grad_a_conv_b, m_a_conv_b, v_a_conv_b),
               ("b_ln_g", b_ln_g, grad_b_ln_g, m_b_ln_g, v_b_ln_g), ("b_ln_b", b_ln_b, grad_b_ln_b, m_b_ln_b, v_b_ln_b),
               ("b_w_s", b_w_s, grad_b_w_s, m_b_w_s, v_b_w_s), ("b_b_s", b_b_s, grad_b_b_s, m_b_b_s, v_b_b_s),
               ("final_g", final_g, grad_final_g, m_final_g, v_final_g)]

    def flat2d(a):
        return a.reshape(-1, a.shape[-1])

    res = _adamw_small([flat2d(t[1]) for t in small_w], [flat2d(t[2]) for t in small_w],
                       [flat2d(t[3]) for t in small_w], [flat2d(t[4]) for t in small_w], "adamw_small")
    for (nm, w, _, _, _), r3 in zip(small_w, res):
        upd[nm] = tuple(o.reshape(w.shape) for o in r3)
    upd["mod_w"] = (delta_mod_w, new_m_mod_w, new_v_mod_w)

    grads = {"mod_w": grad_mod_w, "mod_b": grad_mod_b, "norm_g": grad_norm_g, "a_w_in": g_a_w_in[None],
             "a_conv_w": grad_a_conv_w, "a_conv_b": grad_a_conv_b, "a_w_out": g_a_w_out[None], "b_w_in": g_b_w_in[None],
             "b_ln_g": grad_b_ln_g, "b_ln_b": grad_b_ln_b, "b_w_s": grad_b_w_s, "b_b_s": grad_b_b_s,
             "b_w_out": g_b_w_out[None], "final_g": grad_final_g}
    order = ["mod_w", "mod_b", "norm_g", "a_w_in", "a_conv_w", "a_conv_b", "a_w_out", "b_w_in", "b_ln_g", "b_ln_b",
             "b_w_s", "b_b_s", "b_w_out", "final_g"]
    return (loss, grad_x[None], *[grads[k] for k in order], *[upd[k][0] for k in order],
            *[upd[k][1] for k in order], *[upd[k][2] for k in order])
```

```python
import functools
import math

import jax
import jax.numpy as jnp
from jax import lax
from jax.experimental import pallas as pl
from jax.experimental.pallas import tpu as pltpu

F32 = jnp.float32
BF16 = jnp.bfloat16
MESH = pl.DeviceIdType.MESH

N_DEV = 8
N_CHIP = 4
SUBLANE = 8
RMS_EPS = 1e-6
LN_EPS = 1e-5
CHUNK = 128
GROUPS = 8
ADAM_LR = 0.001
ADAM_B1 = 0.9
ADAM_B2 = 0.999
ADAM_EPS = 1e-08
ADAM_WD = 0.01
ADAM_STEP = 10
VMEM_LIMIT = 56 << 20

HBM_SPEC = pl.BlockSpec(memory_space=pltpu.HBM)
VMEM_SPEC = pl.BlockSpec(memory_space=pltpu.VMEM)
SEM_SPEC = pl.BlockSpec(memory_space=pltpu.SEMAPHORE)
ANY_SPEC = pl.BlockSpec(memory_space=pl.ANY)
EFFECT = pltpu.SideEffectType.DATAFLOW_SIDE_EFFECTING


def _params(*sem):
    return pltpu.CompilerParams(dimension_semantics=sem, vmem_limit_bytes=VMEM_LIMIT)


def _tile(n, want):
    if n <= want:
        return n
    t = want
    while n % t:
        t -= 128
    return t


def _sigmoid(x):
    return 1.0 / (1.0 + jnp.exp(-x))


def _silu_and_grad(x):
    s = _sigmoid(x)
    return x * s, s * (1.0 + x * (1.0 - s))


def _gelu_and_grad(x):
    cdf = 0.5 * (1.0 + lax.erf(x * (1.0 / math.sqrt(2.0))))
    pdf = jnp.exp(-0.5 * x * x) * (1.0 / math.sqrt(2.0 * math.pi))
    return x * cdf, cdf + x * pdf


def _gelu(x):
    return x * (0.5 * (1.0 + lax.erf(x * (1.0 / math.sqrt(2.0)))))


def _rms(x):
    r = lax.rsqrt(jnp.mean(x * x, axis=-1, keepdims=True) + RMS_EPS)
    return x * r, r


def _rms_bwd(dxn, xn, r):
    return r * (dxn - xn * jnp.mean(dxn * xn, axis=-1, keepdims=True))


def _colsum(a):
    return jnp.sum(a, axis=0, keepdims=True)


def _shift_down(cur, before, k):
    rolled = pltpu.roll(cur, k, 0)
    row = lax.broadcasted_iota(jnp.int32, before.shape, 0)
    head = jnp.where(row < k, pltpu.roll(before, k, 0), rolled[:SUBLANE])
    return jnp.concatenate([head, rolled[SUBLANE:]], axis=0)


def _shift_up(cur, after, k):
    n = cur.shape[0]
    rolled = pltpu.roll(cur, n - k, 0)
    row = lax.broadcasted_iota(jnp.int32, after.shape, 0)
    tail = jnp.where(row >= SUBLANE - k, pltpu.roll(after, SUBLANE - k, 0), rolled[n - SUBLANE:])
    return jnp.concatenate([rolled[:n - SUBLANE], tail], axis=0)


def _after_spec():
    return pl.BlockSpec((SUBLANE, 128), lambda *_: (0, 0))


def _mm_proj(h, w, n_split, name, after):
    s, d = h.shape
    e = w.shape[1] // n_split
    tm, tn = _tile(s, 1024), _tile(e, 1024)
    nj = e // tn

    def body(h_ref, w_ref, after_ref, o_ref):
        o_ref[...] = jnp.dot(h_ref[...], w_ref[...], preferred_element_type=F32)

    return pl.pallas_call(
        body, name=name,
        out_shape=jax.ShapeDtypeStruct((n_split, s, e), F32),
        grid=(s // tm, n_split * nj),
        in_specs=[pl.BlockSpec((tm, d), lambda i, j: (i, 0)), pl.BlockSpec((d, tn), lambda i, j: (0, j)),
                  _after_spec()],
        out_specs=pl.BlockSpec((None, tm, tn), lambda i, j: (j // nj, i, j % nj)),
        compiler_params=_params("parallel", "parallel"),
    )(h, w, after)


def _mm_proj_slab(h, w, proj, q_idx, n_split, name, after):
    s, d = h.shape
    e = w.shape[1] // n_split
    tm, tn = _tile(s, 1024), _tile(e, 1024)
    nj = e // tn

    def body(q_ref, h_ref, w_ref, *rest):
        o_ref = rest[-1]
        o_ref[...] = jnp.dot(h_ref[...], w_ref[...], preferred_element_type=F32)

    in_specs = [pl.BlockSpec((tm, d), lambda i, j, qr: (i, 0)), pl.BlockSpec((d, tn), lambda i, j, qr: (0, qr[0] * nj + j)),
                _after_spec()]
    args = [q_idx, h, w, after]
    aliases = {}
    if proj is not None:
        in_specs.append(ANY_SPEC)
        args.append(proj)
        aliases = {4: 0}
    return pl.pallas_call(
        body, name=name,
        out_shape=jax.ShapeDtypeStruct((n_split, s, e), F32),
        grid_spec=pltpu.PrefetchScalarGridSpec(
            num_scalar_prefetch=1, grid=(s // tm, nj), in_specs=in_specs,
            out_specs=pl.BlockSpec((None, tm, tn), lambda i, j, qr: (qr[0], i, j))),
        input_output_aliases=aliases,
        compiler_params=_params("parallel", "parallel"),
    )(*args)


def _mm_nn(a, b, name):
    m, k = a.shape
    n = b.shape[1]
    tm, tn = _tile(m, 1024), _tile(n, 1024)

    def body(a_ref, b_ref, o_ref):
        o_ref[...] = jnp.dot(a_ref[...], b_ref[...], preferred_element_type=F32)

    return pl.pallas_call(
        body, name=name,
        out_shape=jax.ShapeDtypeStruct((m, n), F32),
        grid=(m // tm, n // tn),
        in_specs=[pl.BlockSpec((tm, k), lambda i, j: (i, 0)), pl.BlockSpec((k, tn), lambda i, j: (0, j))],
        out_specs=pl.BlockSpec((tm, tn), lambda i, j: (i, j)),
        compiler_params=_params("parallel", "parallel"),
    )(a, b)


def _mm_nt(a, b, name):
    m, k = a.shape
    n = b.shape[0]
    tm, tn = _tile(m, 1024), _tile(n, 1024)

    def body(a_ref, b_ref, o_ref):
        o_ref[...] = lax.dot_general(a_ref[...], b_ref[...], (((1,), (1,)), ((), ())), preferred_element_type=F32)

    return pl.pallas_call(
        body, name=name,
        out_shape=jax.ShapeDtypeStruct((m, n), F32),
        grid=(m // tm, n // tn),
        in_specs=[pl.BlockSpec((tm, k), lambda i, j: (i, 0)), pl.BlockSpec((tn, k), lambda i, j: (j, 0))],
        out_specs=pl.BlockSpec((tm, tn), lambda i, j: (i, j)),
        compiler_params=_params("parallel", "parallel"),
    )(a, b)


def _mm_dh(dp, w, name, after):
    nq, s, e = dp.shape
    d = w.shape[0]
    tm, tn, tk = _tile(s, 1024), _tile(d, 1024), _tile(e, 1024)
    nkq = e // tk
    nk = nq * nkq

    def body(a_ref, b_ref, after_ref, o_ref):
        @pl.when(pl.program_id(2) == 0)
        def _():
            o_ref[...] = jnp.zeros_like(o_ref)
        o_ref[...] += lax.dot_general(a_ref[...], b_ref[...], (((1,), (1,)), ((), ())), preferred_element_type=F32)

    return pl.pallas_call(
        body, name=name,
        out_shape=jax.ShapeDtypeStruct((s, d), F32),
        grid=(s // tm, d // tn, nk),
        in_specs=[pl.BlockSpec((None, tm, tk), lambda i, j, k: (k // nkq, i, k % nkq)),
                  pl.BlockSpec((tn, tk), lambda i, j, k: (j, k)), _after_spec()],
        out_specs=pl.BlockSpec((tm, tn), lambda i, j, k: (i, j)),
        compiler_params=_params("parallel", "parallel", "arbitrary"),
    )(dp, w, after)


def _dw_body(n_t):
    def body(a_ref, b_ref, o_ref, acc_ref):
        t = pl.program_id(2)

        @pl.when(t == 0)
        def _():
            acc_ref[...] = jnp.zeros_like(acc_ref)
        acc_ref[...] += lax.dot_general(a_ref[...], b_ref[...], (((0,), (0,)), ((), ())), preferred_element_type=F32)

        @pl.when(t == n_t - 1)
        def _():
            o_ref[...] = acc_ref[...].astype(o_ref.dtype)
    return body


def _mm_dw_in(h, dp, name):
    s, d = h.shape
    nq, _, e = dp.shape
    tm, tn, tt = _tile(d, 1024), _tile(e, 1024), _tile(s, 512)
    nj = e // tn

    return pl.pallas_call(
        _dw_body(s // tt), name=name,
        out_shape=jax.ShapeDtypeStruct((d, nq * e), BF16),
        grid=(d // tm, nq * nj, s // tt),
        in_specs=[pl.BlockSpec((tt, tm), lambda i, j, t: (t, i)),
                  pl.BlockSpec((None, tt, tn), lambda i, j, t: (j // nj, t, j % nj))],
        out_specs=pl.BlockSpec((tm, tn), lambda i, j, t: (i, j)),
        scratch_shapes=[pltpu.VMEM((tm, tn), F32)],
        compiler_params=_params("parallel", "parallel", "arbitrary"),
    )(h, dp)


def _mm_dw_out(y, dbr, name):
    s, e = y.shape
    d = dbr.shape[1]
    tm, tn, tt = _tile(e, 1024), _tile(d, 1024), _tile(s, 512)

    return pl.pallas_call(
        _dw_body(s // tt), name=name,
        out_shape=jax.ShapeDtypeStruct((e, d), BF16),
        grid=(e // tm, d // tn, s // tt),
        in_specs=[pl.BlockSpec((tt, tm), lambda i, j, t: (t, i)), pl.BlockSpec((tt, tn), lambda i, j, t: (t, j))],
        out_specs=pl.BlockSpec((tm, tn), lambda i, j, t: (i, j)),
        scratch_shapes=[pltpu.VMEM((tm, tn), F32)],
        compiler_params=_params("parallel", "parallel", "arbitrary"),
    )(y, dbr)


def _row_spec(ts, d):
    return pl.BlockSpec((ts, d), lambda i: (i, 0))


def _vec_spec(d):
    return pl.BlockSpec((1, d), lambda i: (0, 0))


def _norm_mod(x, g, scale, shift, name):
    s, d = x.shape
    ts = _tile(s, 512)

    def body(x_ref, g_ref, sc_ref, sh_ref, h_ref):
        xn, _ = _rms(x_ref[...])
        h_ref[...] = ((xn * g_ref[...]) * (1.0 + sc_ref[...]) + sh_ref[...]).astype(BF16)

    return pl.pallas_call(
        body, name=name, out_shape=jax.ShapeDtypeStruct((s, d), BF16), grid=(s // ts,),
        in_specs=[_row_spec(ts, d), _vec_spec(d), _vec_spec(d), _vec_spec(d)],
        out_specs=_row_spec(ts, d), compiler_params=_params("parallel"),
    )(x, g, scale, shift)


def _residual_norm_mod(x, br, gate, g, scale, shift, name):
    s, d = x.shape
    ts = _tile(s, 512)

    def body(x_ref, br_ref, gate_ref, g_ref, sc_ref, sh_ref, x1_ref, h_ref):
        x1 = x_ref[...] + gate_ref[...] * br_ref[...]
        x1_ref[...] = x1
        xn, _ = _rms(x1)
        h_ref[...] = ((xn * g_ref[...]) * (1.0 + sc_ref[...]) + sh_ref[...]).astype(BF16)

    return pl.pallas_call(
        body, name=name,
        out_shape=(jax.ShapeDtypeStruct((s, d), F32), jax.ShapeDtypeStruct((s, d), BF16)), grid=(s // ts,),
        in_specs=[_row_spec(ts, d), _row_spec(ts, d)] + [_vec_spec(d)] * 4,
        out_specs=(_row_spec(ts, d), _row_spec(ts, d)), compiler_params=_params("parallel"),
    )(x, br, gate, g, scale, shift)


def _head(x1, br, gate, gf, target, name):
    s, d = x1.shape
    ts = _tile(s, 512)

    def body(x1_ref, br_ref, gate_ref, gf_ref, tg_ref, dx_ref, dbr_ref, loss_ref, dgf_ref, dgate_ref):
        @pl.when(pl.program_id(0) == 0)
        def _():
            loss_ref[...] = jnp.zeros_like(loss_ref)
            dgf_ref[...] = jnp.zeros_like(dgf_ref)
            dgate_ref[...] = jnp.zeros_like(dgate_ref)
        br = br_ref[...]
        gate = gate_ref[...]
        gf = gf_ref[...]
        xn, r = _rms(x1_ref[...] + gate * br)
        err = xn * gf - tg_ref[...]
        loss_ref[...] += 0.5 * jnp.sum(jnp.mean(err * err, axis=-1, keepdims=True))
        dout = err * (1.0 / d)
        dgf_ref[...] += _colsum(dout * xn)
        dx = _rms_bwd(dout * gf, xn, r)
        dx_ref[...] = dx
        dgate_ref[...] += _colsum(dx * br)
        dbr_ref[...] = (gate * dx).astype(BF16)

    return pl.pallas_call(
        body, name=name,
        out_shape=(jax.ShapeDtypeStruct((s, d), F32), jax.ShapeDtypeStruct((s, d), BF16),
                   jax.ShapeDtypeStruct((SUBLANE, 128), F32), jax.ShapeDtypeStruct((1, d), F32),
                   jax.ShapeDtypeStruct((1, d), F32)),
        grid=(s // ts,),
        in_specs=[_row_spec(ts, d), _row_spec(ts, d), _vec_spec(d), _vec_spec(d), _row_spec(ts, d)],
        out_specs=(_row_spec(ts, d), _row_spec(ts, d), pl.BlockSpec((SUBLANE, 128), lambda i: (0, 0)),
                   _vec_spec(d), _vec_spec(d)),
        compiler_params=_params("arbitrary"),
    )(x1, br, gate, gf, target)


def _norm_mod_bwd(dh, x, dx_in, g, scale, name, br=None, gate=None):
    s, d = x.shape
    ts = _tile(s, 512)
    has_branch = br is not None

    def body(*refs):
        if has_branch:
            (dh_ref, x_ref, dxin_ref, g_ref, sc_ref, br_ref, gate_ref,
             dx_ref, dsh_ref, dsc_ref, dg_ref, dbr_ref, dgate_ref) = refs
        else:
            dh_ref, x_ref, dxin_ref, g_ref, sc_ref, dx_ref, dsh_ref, dsc_ref, dg_ref = refs

        @pl.when(pl.program_id(0) == 0)
        def _():
            dsh_ref[...] = jnp.zeros_like(dsh_ref)
            dsc_ref[...] = jnp.zeros_like(dsc_ref)
            dg_ref[...] = jnp.zeros_like(dg_ref)
            if has_branch:
                dgate_ref[...] = jnp.zeros_like(dgate_ref)
        dh = dh_ref[...]
        g = g_ref[...]
        xn, r = _rms(x_ref[...])
        dsh_ref[...] += _colsum(dh)
        dsc_ref[...] += _colsum(dh * (xn * g))
        da = dh * (1.0 + sc_ref[...])
        dg_ref[...] += _colsum(da * xn)
        dx = dxin_ref[...] + _rms_bwd(da * g, xn, r)
        dx_ref[...] = dx
        if has_branch:
            dgate_ref[...] += _colsum(dx * br_ref[...])
            dbr_ref[...] = (gate_ref[...] * dx).astype(BF16)

    vec_out = jax.ShapeDtypeStruct((1, d), F32)
    out_shape = [jax.ShapeDtypeStruct((s, d), F32), vec_out, vec_out, vec_out]
    out_specs = [_row_spec(ts, d), _vec_spec(d), _vec_spec(d), _vec_spec(d)]
    in_specs = [_row_spec(ts, d)] * 3 + [_vec_spec(d)] * 2
    args = [dh, x, dx_in, g, scale]
    if has_branch:
        out_shape += [jax.ShapeDtypeStruct((s, d), BF16), vec_out]
        out_specs += [_row_spec(ts, d), _vec_spec(d)]
        in_specs += [_row_spec(ts, d), _vec_spec(d)]
        args += [br, gate]
    return pl.pallas_call(
        body, name=name, out_shape=tuple(out_shape), grid=(s // ts,),
        in_specs=in_specs, out_specs=tuple(out_specs), compiler_params=_params("arbitrary"),
    )(*args)


def _conv_tiles(s, e):
    return _tile(s, 512), _tile(e, 512)


def _conv_fwd(proj, conv_w, conv_b, name):
    _, s, e = proj.shape
    ts, te = _conv_tiles(s, e)
    hb = ts // SUBLANE

    def body(bg_ref, cg_ref, xi_ref, z_ref, cgp_ref, xip_ref, w0_ref, w1_ref, w2_ref, b_ref, y_ref):
        cx = cg_ref[...] * xi_ref[...]
        before = jnp.where(pl.program_id(0) > 0, cgp_ref[...] * xip_ref[...], 0.0)
        conv = b_ref[...] + w2_ref[...] * cx
        conv = conv + w0_ref[...] * _shift_down(cx, before, 2)
        conv = conv + w1_ref[...] * _shift_down(cx, before, 1)
        z = z_ref[...]
        y_ref[...] = ((z * _sigmoid(z)) * bg_ref[...] * conv).astype(BF16)

    def part(q):
        return pl.BlockSpec((None, ts, te), lambda i, j: (q, i, j))

    def halo_before(q):
        return pl.BlockSpec((None, SUBLANE, te), lambda i, j: (q, jnp.maximum(i * hb - 1, 0), j))

    return pl.pallas_call(
        body, name=name, out_shape=jax.ShapeDtypeStruct((s, e), BF16), grid=(s // ts, e // te),
        in_specs=[part(0), part(1), part(2), part(3), halo_before(1), halo_before(2)]
        + [pl.BlockSpec((1, te), lambda i, j: (0, j))] * 4,
        out_specs=pl.BlockSpec((ts, te), lambda i, j: (i, j)),
        compiler_params=_params("parallel", "parallel"),
    )(proj, proj, proj, proj, proj, proj, *conv_w, conv_b)


def _conv_bwd(proj, dy, conv_w, conv_b, name):
    _, s, e = proj.shape
    ts, te = _conv_tiles(s, e)
    hb = ts // SUBLANE
    n_i = s // ts
    last_halo = s // SUBLANE - 1

    def body(bg_ref, cg_ref, xi_ref, z_ref, dy_ref, cgp_ref, xip_ref, bgn_ref, zn_ref, dyn_ref,
             w0_ref, w1_ref, w2_ref, b_ref, dp_ref, dw0_ref, dw1_ref, dw2_ref, db_ref):
        i = pl.program_id(1)

        @pl.when(i == 0)
        def _():
            for acc in (dw0_ref, dw1_ref, dw2_ref, db_ref):
                acc[...] = jnp.zeros_like(acc)
        bg, cg, xi, z, dy = bg_ref[...], cg_ref[...], xi_ref[...], z_ref[...], dy_ref[...]
        w0, w1, w2 = w0_ref[...], w1_ref[...], w2_ref[...]
        cx = cg * xi
        before = jnp.where(i > 0, cgp_ref[...] * xip_ref[...], 0.0)
        cx1 = _shift_down(cx, before, 1)
        cx2 = _shift_down(cx, before, 2)
        conv = b_ref[...] + w2 * cx
        conv = conv + w0 * cx2
        conv = conv + w1 * cx1
        sz, dsz = _silu_and_grad(z)
        dp_ref[3] = (dy * bg * conv * dsz).astype(BF16)
        dp_ref[0] = (dy * sz * conv).astype(BF16)
        dconv = dy * sz * bg
        zn = zn_ref[...]
        after = jnp.where(i < n_i - 1, dyn_ref[...] * (zn * _sigmoid(zn)) * bgn_ref[...], 0.0)
        db_ref[...] += _colsum(dconv)
        dw2_ref[...] += _colsum(dconv * cx)
        dw1_ref[...] += _colsum(dconv * cx1)
        dw0_ref[...] += _colsum(dconv * cx2)
        dcx = w2 * dconv + w1 * _shift_up(dconv, after, 1) + w0 * _shift_up(dconv, after, 2)
        dp_ref[1] = (dcx * xi).astype(BF16)
        dp_ref[2] = (dcx * cg).astype(BF16)

    def part(q):
        return pl.BlockSpec((None, ts, te), lambda j, i: (q, i, j))

    def halo_before(q):
        return pl.BlockSpec((None, SUBLANE, te), lambda j, i: (q, jnp.maximum(i * hb - 1, 0), j))

    def halo_after(q):
        return pl.BlockSpec((None, SUBLANE, te), lambda j, i: (q, jnp.minimum((i + 1) * hb, last_halo), j))

    return pl.pallas_call(
        body, name=name,
        out_shape=(jax.ShapeDtypeStruct((4, s, e), BF16),) + (jax.ShapeDtypeStruct((1, e), F32),) * 4,
        grid=(e // te, n_i),
        in_specs=[part(0), part(1), part(2), part(3), pl.BlockSpec((ts, te), lambda j, i: (i, j)),
                  halo_before(1), halo_before(2), halo_after(0), halo_after(3),
                  pl.BlockSpec((SUBLANE, te), lambda j, i: (jnp.minimum((i + 1) * hb, last_halo), j))]
        + [pl.BlockSpec((1, te), lambda j, i: (0, j))] * 4,
        out_specs=(pl.BlockSpec((4, ts, te), lambda j, i: (0, i, j)),) + (pl.BlockSpec((1, te), lambda j, i: (0, j)),) * 4,
        compiler_params=_params("parallel", "arbitrary"),
    )(proj, proj, proj, proj, dy, proj, proj, proj, proj, dy, *conv_w, conv_b)


def _tril(w):
    row = lax.broadcasted_iota(jnp.int32, w.shape, 0)
    col = lax.broadcasted_iota(jnp.int32, w.shape, 1)
    return jnp.where(row >= col, w, 0.0)


def _triu(w):
    row = lax.broadcasted_iota(jnp.int32, w.shape, 0)
    col = lax.broadcasted_iota(jnp.int32, w.shape, 1)
    return jnp.where(row <= col, w, 0.0)


def _layer_norm_fwd(v, g, b):
    mu = jnp.mean(v, axis=-1, keepdims=True)
    vc = v - mu
    rstd = lax.rsqrt(jnp.mean(vc * vc, axis=-1, keepdims=True) + LN_EPS)
    vhat = vc * rstd
    return vhat * g + b, vhat, rstd


def _gmlp_fwd(proj, ln_g, ln_b, w_s, b_s_t, name):
    _, s, e = proj.shape
    gw = e // GROUPS

    def body(pu_ref, pv_ref, pz_ref, g_ref, b_ref, ws_ref, bs_ref, y_ref, vn_scr, mix_scr):
        vn, _, _ = _layer_norm_fwd(_gelu(pv_ref[...]), g_ref[...], b_ref[...])
        vn_scr[...] = vn.astype(BF16)
        for gi in range(GROUPS):
            cols = slice(gi * gw, (gi + 1) * gw)
            wt = _tril(ws_ref[gi]).astype(BF16)
            mix_scr[:, cols] = jnp.dot(wt, vn_scr[:, cols], preferred_element_type=F32) + bs_ref[:, gi:gi + 1]
        z = pz_ref[...]
        y_ref[...] = ((z * _sigmoid(z)) * (_gelu(pu_ref[...]) * mix_scr[...])).astype(BF16)

    def part(q):
        return pl.BlockSpec((None, CHUNK, e), lambda i: (q, i, 0))

    return pl.pallas_call(
        body, name=name, out_shape=jax.ShapeDtypeStruct((s, e), BF16), grid=(s // CHUNK,),
        in_specs=[part(0), part(1), part(2), _vec_spec(e), _vec_spec(e),
                  pl.BlockSpec((GROUPS, CHUNK, CHUNK), lambda i: (0, 0, 0)),
                  pl.BlockSpec((CHUNK, GROUPS), lambda i: (0, 0))],
        out_specs=pl.BlockSpec((CHUNK, e), lambda i: (i, 0)),
        scratch_shapes=[pltpu.VMEM((CHUNK, e), BF16), pltpu.VMEM((CHUNK, e), F32)],
        compiler_params=_params("parallel"),
    )(proj, proj, proj, ln_g, ln_b, w_s, b_s_t)


def _gmlp_bwd(proj, dy, ln_g, ln_b, w_s, w_s_t, b_s_t, name):
    _, s, e = proj.shape
    gw = e // GROUPS
    n_i = s // CHUNK

    def body(pu_ref, pv_ref, pz_ref, dy_ref, g_ref, b_ref, ws_ref, wst_ref, bs_ref,
             dp_ref, dws_ref, dbs_ref, dlg_ref, dlb_ref, vn_scr, mix_scr, dm_scr, dvn_scr, dmacc_scr):
        i = pl.program_id(0)

        @pl.when(i == 0)
        def _():
            dws_ref[...] = jnp.zeros_like(dws_ref)
            dlg_ref[...] = jnp.zeros_like(dlg_ref)
            dlb_ref[...] = jnp.zeros_like(dlb_ref)
            dmacc_scr[...] = jnp.zeros_like(dmacc_scr)
        ln_g = g_ref[...]
        u, du_dpu = _gelu_and_grad(pu_ref[...])
        v, dv_dpv = _gelu_and_grad(pv_ref[...])
        vn, vhat, rstd = _layer_norm_fwd(v, ln_g, b_ref[...])
        vn_scr[...] = vn.astype(BF16)
        for gi in range(GROUPS):
            cols = slice(gi * gw, (gi + 1) * gw)
            wt = _tril(ws_ref[gi]).astype(BF16)
            mix_scr[:, cols] = jnp.dot(wt, vn_scr[:, cols], preferred_element_type=F32) + bs_ref[:, gi:gi + 1]
        mixed = mix_scr[...]
        dy = dy_ref[...]
        sz, dsz = _silu_and_grad(pz_ref[...])
        ds = dy * sz
        dp_ref[2] = (dy * (u * mixed) * dsz).astype(BF16)
        dp_ref[0] = (ds * mixed * du_dpu).astype(BF16)
        dm = ds * u
        dmacc_scr[...] += dm
        dm_scr[...] = dm.astype(BF16)
        for gi in range(GROUPS):
            cols = slice(gi * gw, (gi + 1) * gw)
            dmg = dm_scr[:, cols]
            dws_ref[gi] += lax.dot_general(dmg, vn_scr[:, cols], (((1,), (1,)), ((), ())), preferred_element_type=F32)
            wtt = _triu(wst_ref[gi]).astype(BF16)
            dvn_scr[:, cols] = jnp.dot(wtt, dmg, preferred_element_type=F32)
        dvn = dvn_scr[...]
        dlg_ref[...] += _colsum(dvn * vhat)
        dlb_ref[...] += _colsum(dvn)
        dvh = dvn * ln_g
        dv = rstd * (dvh - jnp.mean(dvh, axis=-1, keepdims=True) - vhat * jnp.mean(dvh * vhat, axis=-1, keepdims=True))
        dp_ref[1] = (dv * dv_dpv).astype(BF16)

        @pl.when(i == n_i - 1)
        def _():
            for gi in range(GROUPS):
                dws_ref[gi] = _tril(dws_ref[gi])
                dbs_ref[:, gi:gi + 1] = jnp.sum(dmacc_scr[:, gi * gw:(gi + 1) * gw], axis=1, keepdims=True)

    def part(q):
        return pl.BlockSpec((None, CHUNK, e), lambda i: (q, i, 0))

    w_spec = pl.BlockSpec((GROUPS, CHUNK, CHUNK), lambda i: (0, 0, 0))
    bs_spec = pl.BlockSpec((CHUNK, GROUPS), lambda i: (0, 0))
    return pl.pallas_call(
        body, name=name,
        out_shape=(jax.ShapeDtypeStruct((3, s, e), BF16), jax.ShapeDtypeStruct((GROUPS, CHUNK, CHUNK), F32),
                   jax.ShapeDtypeStruct((CHUNK, GROUPS), F32), jax.ShapeDtypeStruct((1, e), F32),
                   jax.ShapeDtypeStruct((1, e), F32)),
        grid=(n_i,),
        in_specs=[part(0), part(1), part(2), pl.BlockSpec((CHUNK, e), lambda i: (i, 0)), _vec_spec(e), _vec_spec(e),
                  w_spec, w_spec, bs_spec],
        out_specs=(pl.BlockSpec((3, CHUNK, e), lambda i: (0, i, 0)), w_spec, bs_spec, _vec_spec(e), _vec_spec(e)),
        scratch_shapes=[pltpu.VMEM((CHUNK, e), BF16), pltpu.VMEM((CHUNK, e), F32), pltpu.VMEM((CHUNK, e), BF16),
                        pltpu.VMEM((CHUNK, e), F32), pltpu.VMEM((CHUNK, e), F32)],
        compiler_params=_params("arbitrary"),
    )(proj, proj, proj, dy, ln_g, ln_b, w_s, w_s_t, b_s_t)


def _mod_fwd(c_all, mod_w, mod_b_cols, name):
    n_layer, d, w = mod_w.shape

    def body(c_ref, w_ref, b_ref, ca_ref, o_ref):
        c = c_ref[...]
        ca = c * _sigmoid(c)
        ca_ref[...] = ca
        for li in range(n_layer):
            o_ref[li * N_DEV:(li + 1) * N_DEV, :] = (
                jnp.dot(ca, w_ref[li], preferred_element_type=F32, precision=lax.Precision.HIGHEST) + b_ref[li])

    return pl.pallas_call(
        body, name=name,
        out_shape=(jax.ShapeDtypeStruct((N_DEV, d), F32), jax.ShapeDtypeStruct((n_layer * N_DEV, w), F32)),
        in_specs=[VMEM_SPEC] * 3, out_specs=(VMEM_SPEC, VMEM_SPEC),
        compiler_params=pltpu.CompilerParams(vmem_limit_bytes=VMEM_LIMIT),
    )(c_all, mod_w, mod_b_cols)


def _adamw(w, g, m, v):
    m = ADAM_B1 * m + (1.0 - ADAM_B1) * g
    v = ADAM_B2 * v + (1.0 - ADAM_B2) * (g * g)
    m_hat = m / (1.0 - ADAM_B1 ** ADAM_STEP)
    v_hat = v / (1.0 - ADAM_B2 ** ADAM_STEP)
    delta = -ADAM_LR * (m_hat / (jnp.sqrt(v_hat) + ADAM_EPS) + ADAM_WD * w)
    return delta, m, v


def _adamw_2d(w, g, m, v, name):
    r, c = w.shape
    tr, tc = _tile(r, 512), _tile(c, 1024)

    def body(w_ref, g_ref, m_ref, v_ref, d_ref, nm_ref, nv_ref):
        d_ref[...], nm_ref[...], nv_ref[...] = _adamw(w_ref[...], g_ref[...], m_ref[...], v_ref[...])

    spec = pl.BlockSpec((tr, tc), lambda i, j: (i, j))
    shape = jax.ShapeDtypeStruct((r, c), F32)
    return pl.pallas_call(
        body, name=name, out_shape=(shape, shape, shape), grid=(r // tr, c // tc),
        in_specs=[spec] * 4, out_specs=(spec, spec, spec), compiler_params=_params("parallel", "parallel"),
    )(w, g, m, v)


def _mod_w_update(ca_t, dmod_cols, w, m, v, name):
    n_layer, d, wd = w.shape
    tr = _tile(d, 256)

    def body(ca_ref, dm_ref, w_ref, m_ref, v_ref, g_ref, d_ref, nm_ref, nv_ref):
        ca = ca_ref[...]
        dm = dm_ref[...]
        g = ca[:, 0:1] * dm[0:1, :]
        for b in range(1, N_DEV):
            g = g + ca[:, b:b + 1] * dm[b:b + 1, :]
        g_ref[...] = g
        d_ref[...], nm_ref[...], nv_ref[...] = _adamw(w_ref[...], g, m_ref[...], v_ref[...])

    spec = pl.BlockSpec((None, tr, wd), lambda l, i: (l, i, 0))
    shape = jax.ShapeDtypeStruct((n_layer, d, wd), F32)
    return pl.pallas_call(
        body, name=name, out_shape=(shape,) * 4, grid=(n_layer, d // tr),
        in_specs=[pl.BlockSpec((tr, N_DEV), lambda l, i: (i, 0)), pl.BlockSpec((None, N_DEV, wd), lambda l, i: (l, 0, 0)),
                  spec, spec, spec],
        out_specs=(spec,) * 4, compiler_params=_params("parallel", "parallel"),
    )(ca_t, dmod_cols, w, m, v)


def _adamw_small(ws, gs, ms, vs, name):
    n = len(ws)

    def body(*refs):
        ins, outs = refs[:4 * n], refs[4 * n:]
        for k in range(n):
            delta, nm, nv = _adamw(ins[k][...], ins[n + k][...], ins[2 * n + k][...], ins[3 * n + k][...])
            outs[3 * k][...] = delta
            outs[3 * k + 1][...] = nm
            outs[3 * k + 2][...] = nv

    out_shape = []
    for w in ws:
        out_shape += [jax.ShapeDtypeStruct(w.shape, F32)] * 3
    outs = pl.pallas_call(
        body, name=name, out_shape=tuple(out_shape),
        in_specs=[VMEM_SPEC] * (4 * n), out_specs=tuple([VMEM_SPEC] * (3 * n)),
        compiler_params=pltpu.CompilerParams(vmem_limit_bytes=VMEM_LIMIT),
    )(*ws, *gs, *ms, *vs)
    return [tuple(outs[3 * k:3 * k + 3]) for k in range(n)]


def _place():
    return lax.axis_index("x"), lax.axis_index("y"), lax.axis_index("c")


def _other_chips(x, y):
    return [(1 - x, y), (x, 1 - y), (1 - x, 1 - y)]


def _all_gather(block, name):
    m_per, n = block.shape

    def body(x_ref, out_ref, send_sems, recv_sems, local_sem):
        x, y, c = _place()
        me, sibling = (x, y, c), (x, y, 1 - c)
        chips = _other_chips(x, y)

        def rows(px, py, pc):
            return out_ref.at[pl.ds((4 * px + 2 * py + pc) * m_per, m_per), :]

        def copy(k, blk, to, src=None):
            return pltpu.make_async_remote_copy(
                src_ref=rows(*blk) if src is None else src, dst_ref=rows(*blk),
                send_sem=send_sems.at[k], recv_sem=recv_sems.at[k], device_id=to, device_id_type=MESH)

        mine = pltpu.make_async_copy(x_ref, rows(*me), local_sem)
        mine.start()
        first = [copy(0, me, sibling, src=x_ref)]
        first += [copy(1 + j, me, (*chip, c), src=x_ref) for j, chip in enumerate(chips)]
        for cp in first:
            cp.start()
        passed = [copy(4 + j, (*chip, c), sibling) for j, chip in enumerate(chips)]
        for j, chip in enumerate(chips):
            copy(1 + j, (*chip, c), me).wait_recv()
            passed[j].start()
        copy(0, sibling, me).wait_recv()
        for j, chip in enumerate(chips):
            copy(4 + j, (*chip, 1 - c), me).wait_recv()
        for cp in first + passed:
            cp.wait_send()
        mine.wait()

    return pl.pallas_call(
        body, name=name, out_shape=jax.ShapeDtypeStruct((N_DEV * m_per, n), F32),
        in_specs=[VMEM_SPEC], out_specs=VMEM_SPEC,
        scratch_shapes=[pltpu.SemaphoreType.DMA((7,)), pltpu.SemaphoreType.DMA((7,)), pltpu.SemaphoreType.DMA],
        compiler_params=pltpu.CompilerParams(vmem_limit_bytes=VMEM_LIMIT),
    )(block)


def _hbm(a):
    return pltpu.with_memory_space_constraint(a, pltpu.HBM)


def _place_own(shard, kind, chip_idx, name):
    r, cdim = shard.shape
    tr, tc = _tile(r, 512), _tile(cdim, 1024)
    nrb, ncb = r // tr, cdim // tc

    def body(k_ref, s_ref, o_ref):
        o_ref[...] = s_ref[...].astype(BF16)

    if kind == "col":
        full, o_map = (r, N_CHIP * cdim), lambda i, j, kr: (i, kr[0] * ncb + j)
    else:
        full, o_map = (N_CHIP * r, cdim), lambda i, j, kr: (kr[0] * nrb + i, j)
    return pl.pallas_call(
        body, name=name, out_shape=jax.ShapeDtypeStruct(full, BF16),
        grid_spec=pltpu.PrefetchScalarGridSpec(
            num_scalar_prefetch=1, grid=(nrb, ncb),
            in_specs=[pl.BlockSpec((tr, tc), lambda i, j, kr: (i, j))],
            out_specs=pl.BlockSpec((tr, tc), o_map)),
        compiler_params=_params("parallel", "parallel"),
    )(chip_idx, shard)


def _weight_window(ref, kind, shard_shape, k, half):
    r, cdim = shard_shape
    hr = r // 2
    if kind == "col":
        return ref.at[pl.ds(half * hr, hr), pl.ds(pl.multiple_of(k * cdim, 128), cdim)]
    return ref.at[pl.ds(pl.multiple_of(k * r + half * hr, 2 * SUBLANE), hr), :]


def _gather_weights_start(fulls, kinds, shard_shapes, groups, after, name):
    n, ng = len(fulls), len(groups)

    def body(*refs):
        ins = refs[:n]
        sems = refs[n + 1:n + 1 + 2 * ng]
        token = refs[2 * n + 1 + 2 * ng]
        x, y, c = _place()
        chips = _other_chips(x, y)
        for g, group in enumerate(groups):
            for pos, (w, j) in enumerate(group):
                own = _weight_window(ins[w], kinds[w], shard_shapes[w], 2 * x + y, c)
                for cc in range(2):
                    pltpu.make_async_remote_copy(
                        src_ref=own, dst_ref=own,
                        send_sem=sems[2 * g].at[2 * pos + cc], recv_sem=sems[2 * g + 1].at[2 * pos + c],
                        device_id=(*chips[j], cc), device_id_type=MESH).start()
        token[...] = jnp.zeros_like(token)

    sem_shapes = []
    for group in groups:
        sem_shapes += [pltpu.SemaphoreType.DMA((2 * len(group),))] * 2
    outs = pl.pallas_call(
        body, name=name,
        out_shape=tuple(sem_shapes) + tuple(pltpu.HBM(f.shape, f.dtype) for f in fulls)
        + (jax.ShapeDtypeStruct((SUBLANE, 128), F32),),
        in_specs=[HBM_SPEC] * n + [ANY_SPEC], out_specs=(SEM_SPEC,) * (2 * ng) + (HBM_SPEC,) * n + (VMEM_SPEC,),
        input_output_aliases={w: 2 * ng + w for w in range(n)},
        compiler_params=pltpu.CompilerParams(has_side_effects=EFFECT),
    )(*[_hbm(f) for f in fulls], after)
    sems = [(outs[2 * g], outs[2 * g + 1]) for g in range(ng)]
    return sems, list(outs[2 * ng:2 * ng + n]), outs[2 * ng + n]


def _gather_weights_wait(sems, group, fulls, kinds, shard_shapes, after, name):
    n = len(fulls)

    def body(*refs):
        ins = refs[:n]
        send_sems, recv_sems = refs[n], refs[n + 1]
        x, y, c = _place()
        chips = _other_chips(x, y)
        for pos, (w, j) in enumerate(group):
            own = _weight_window(ins[w], kinds[w], shard_shapes[w], 2 * x + y, c)
            for cc in range(2):
                landed = _weight_window(ins[w], kinds[w], shard_shapes[w], 2 * chips[j][0] + chips[j][1], cc)
                k = 2 * pos + cc
                pltpu.make_async_remote_copy(
                    src_ref=own, dst_ref=own, send_sem=send_sems.at[k], recv_sem=recv_sems.at[k],
                    device_id=(*chips[j], cc), device_id_type=MESH).wait_send()
                pltpu.make_async_remote_copy(
                    src_ref=landed, dst_ref=landed, send_sem=send_sems.at[k], recv_sem=recv_sems.at[k],
                    device_id=(*chips[j], cc), device_id_type=MESH).wait_recv()

    return list(pl.pallas_call(
        body, name=name, out_shape=tuple(pltpu.HBM(f.shape, f.dtype) for f in fulls),
        in_specs=[HBM_SPEC] * n + [SEM_SPEC, SEM_SPEC, ANY_SPEC], out_specs=(HBM_SPEC,) * n,
        input_output_aliases={w: w for w in range(n)},
        compiler_params=pltpu.CompilerParams(has_side_effects=EFFECT),
    )(*fulls, sems[0], sems[1], after))


def _grad_piece(ref, kind, h, cdim, k, half):
    if kind == "col":
        return ref.at[pl.ds(half * h, h), pl.ds(pl.multiple_of(k * cdim, 128), cdim)]
    return ref.at[k, pl.ds(half * h, h), :]


def _grad_dims(g, kind):
    return (g.shape[0] // 2, g.shape[1] // N_CHIP) if kind == "col" else (g.shape[1] // 2, g.shape[2])


def _reduce_start(grads, kinds, name):
    n = len(grads)
    dims = [_grad_dims(g, kind) for g, kind in zip(grads, kinds)]
    lands = [lax.empty((N_DEV - 1, h, cdim), g.dtype) for g, (h, cdim) in zip(grads, dims)]

    def body(*refs):
        g_ins, land_ins = refs[:n], refs[n:2 * n]
        send_sems, recv_sems = refs[2 * n], refs[2 * n + 1]
        token = refs[4 * n + 2]
        x, y, c = _place()
        for w in range(n):
            h, cdim = dims[w]
            base = (N_DEV - 1) * w
            pltpu.make_async_remote_copy(
                src_ref=_grad_piece(g_ins[w], kinds[w], h, cdim, 2 * x + y, 1 - c), dst_ref=land_ins[w].at[0],
                send_sem=send_sems.at[base], recv_sem=recv_sems.at[base],
                device_id=(x, y, 1 - c), device_id_type=MESH).start()
            for j, chip in enumerate(_other_chips(x, y)):
                for cc in range(2):
                    pltpu.make_async_remote_copy(
                        src_ref=_grad_piece(g_ins[w], kinds[w], h, cdim, 2 * chip[0] + chip[1], cc),
                        dst_ref=land_ins[w].at[1 + 2 * j + c],
                        send_sem=send_sems.at[base + 1 + 2 * j + cc], recv_sem=recv_sems.at[base + 1 + 2 * j + c],
                        device_id=(*chip, cc), device_id_type=MESH).start()
        token[...] = jnp.zeros_like(token)

    sems = pltpu.SemaphoreType.DMA(((N_DEV - 1) * n,))
    outs = pl.pallas_call(
        body, name=name,
        out_shape=(sems, sems) + tuple(pltpu.HBM(a.shape, a.dtype) for a in list(grads) + lands)
        + (jax.ShapeDtypeStruct((SUBLANE, 128), F32),),
        in_specs=[HBM_SPEC] * (2 * n), out_specs=(SEM_SPEC, SEM_SPEC) + (HBM_SPEC,) * (2 * n) + (VMEM_SPEC,),
        input_output_aliases={i: 2 + i for i in range(2 * n)},
        compiler_params=pltpu.CompilerParams(has_side_effects=EFFECT),
    )(*[_hbm(a) for a in list(grads) + lands])
    return outs[0], outs[1], list(outs[2:2 + n]), list(outs[2 + n:2 + 2 * n]), outs[2 + 2 * n]


def _reduce_wait(send_sems, recv_sems, grads, lands, kinds, after, name):
    n = len(grads)
    dims = [_grad_dims(g, kind) for g, kind in zip(grads, kinds)]

    def body(*refs):
        g_ins, land_ins = refs[:n], refs[n:2 * n]
        send_sems, recv_sems = refs[2 * n], refs[2 * n + 1]
        x, y, c = _place()
        for w in range(n):
            h, cdim = dims[w]
            piece = _grad_piece(g_ins[w], kinds[w], h, cdim, 2 * x + y, c)
            for s in range(N_DEV - 1):
                k = (N_DEV - 1) * w + s
                slot = land_ins[w].at[s]
                pltpu.make_async_remote_copy(
                    src_ref=piece, dst_ref=slot, send_sem=send_sems.at[k], recv_sem=recv_sems.at[k],
                    device_id=(x, y, 1 - c), device_id_type=MESH).wait_send()
                pltpu.make_async_remote_copy(
                    src_ref=piece, dst_ref=slot, send_sem=send_sems.at[k], recv_sem=recv_sems.at[k],
                    device_id=(x, y, 1 - c), device_id_type=MESH).wait_recv()

    outs = pl.pallas_call(
        body, name=name, out_shape=tuple(pltpu.HBM(a.shape, a.dtype) for a in list(grads) + list(lands)),
        in_specs=[HBM_SPEC] * (2 * n) + [SEM_SPEC, SEM_SPEC, ANY_SPEC], out_specs=(HBM_SPEC,) * (2 * n),
        input_output_aliases={i: i for i in range(2 * n)},
        compiler_params=pltpu.CompilerParams(has_side_effects=EFFECT),
    )(*grads, *lands, send_sems, recv_sems, after)
    return list(outs[:n]), list(outs[n:])


def _add_pieces(g, land, kind, chip_idx, core_idx, name):
    _, h, cdim = land.shape
    tr, tc = _tile(h, 256), _tile(cdim, 2048)
    nrb, ncb = h // tr, cdim // tc

    def body(k_ref, c_ref, g_ref, l_ref, o_ref):
        acc = g_ref[...].astype(F32)
        for s in range(N_DEV - 1):
            acc = acc + l_ref[s].astype(F32)
        o_ref[...] = acc

    if kind == "col":
        g_spec = pl.BlockSpec((tr, tc), lambda i, j, kr, cr: (cr[0] * nrb + i, kr[0] * ncb + j))
    else:
        g_spec = pl.BlockSpec((None, tr, tc), lambda i, j, kr, cr: (kr[0], cr[0] * nrb + i, j))
    return pl.pallas_call(
        body, name=name, out_shape=jax.ShapeDtypeStruct((2 * h, cdim), F32),
        grid_spec=pltpu.PrefetchScalarGridSpec(
            num_scalar_prefetch=2, grid=(nrb, ncb),
            in_specs=[g_spec, pl.BlockSpec((N_DEV - 1, tr, tc), lambda i, j, kr, cr: (0, i, j))],
            out_specs=pl.BlockSpec((tr, tc), lambda i, j, kr, cr: (cr[0] * nrb + i, j))),
        compiler_params=_params("parallel", "parallel"),
    )(chip_idx, core_idx, g, land)


def _join_halves(shards, name):
    n = len(shards)

    def body(*refs):
        ins, outs = refs[:n], refs[n:2 * n]
        send_sems, recv_sems = refs[2 * n:]
        x, y, c = _place()
        copies = []
        for w in range(n):
            h = shards[w].shape[0] // 2
            cp = pltpu.make_async_remote_copy(
                src_ref=ins[w].at[pl.ds(c * h, h), :], dst_ref=outs[w].at[pl.ds(c * h, h), :],
                send_sem=send_sems.at[w], recv_sem=recv_sems.at[w], device_id=(x, y, 1 - c), device_id_type=MESH)
            cp.start()
            copies.append(cp)
        for w in range(n):
            h = shards[w].shape[0] // 2
            theirs = outs[w].at[pl.ds((1 - c) * h, h), :]
            pltpu.make_async_remote_copy(
                src_ref=theirs, dst_ref=theirs, send_sem=send_sems.at[w], recv_sem=recv_sems.at[w],
                device_id=(x, y, 1 - c), device_id_type=MESH).wait_recv()
        for cp in copies:
            cp.wait_send()

    return pl.pallas_call(
        body, name=name, out_shape=tuple(jax.ShapeDtypeStruct(s.shape, s.dtype) for s in shards),
        in_specs=[HBM_SPEC] * n, out_specs=tuple([HBM_SPEC] * n),
        input_output_aliases={w: w for w in range(n)},
        scratch_shapes=[pltpu.SemaphoreType.DMA((n,))] * 2,
    )(*shards)


def _sum_devices(gathered, rows, name):
    n = gathered.shape[1]

    def body(g_ref, o_ref):
        acc = g_ref[0:rows, :]
        for dev in range(1, N_DEV):
            acc = acc + g_ref[dev * rows:(dev + 1) * rows, :]
        o_ref[...] = acc

    return pl.pallas_call(
        body, name=name, out_shape=jax.ShapeDtypeStruct((rows, n), F32),
        in_specs=[VMEM_SPEC], out_specs=VMEM_SPEC,
        compiler_params=pltpu.CompilerParams(vmem_limit_bytes=VMEM_LIMIT),
    )(gathered)


def _pack(vectors, width):
    flat = [v.reshape(-1) for v in vectors]
    offsets, total = [], 0
    for f in flat:
        offsets.append(total)
        total += f.shape[0]
    rows = -(-total // (width * SUBLANE)) * SUBLANE
    flat.append(jnp.zeros((rows * width - total,), F32))
    return jnp.concatenate(flat).reshape(rows, width), offsets


def kernel(x, c, mod_w, mod_b, norm_g, a_w_in, a_conv_w, a_conv_b, a_w_out, b_w_in, b_ln_g, b_ln_b, b_w_s, b_b_s, b_w_out, final_g, loss_target, m_mod_w, m_mod_b, m_norm_g, m_a_w_in, m_a_conv_w, m_a_conv_b, m_a_w_out, m_b_w_in, m_b_ln_g, m_b_ln_b, m_b_w_s, m_b_b_s, m_b_w_out, m_final_g, v_mod_w, v_mod_b, v_norm_g, v_a_w_in, v_a_conv_w, v_a_conv_b, v_a_w_out, v_b_w_in, v_b_ln_g, v_b_ln_b, v_b_w_s, v_b_b_s, v_b_w_out, v_final_g):
    seq, d = x.shape[1], x.shape[2]
    e = a_conv_b.shape[1]
    wd = mod_w.shape[2]
    ax, ay, ac = _place()
    chip = 2 * ax + ay
    dev = 2 * chip + ac
    chip_idx = jnp.reshape(chip, (1,)).astype(jnp.int32)
    core_idx = jnp.reshape(ac, (1,)).astype(jnp.int32)

    x2d = x[0]
    target = loss_target[0]

    w_names = ["a_w_in", "a_w_out", "b_w_in", "b_w_out"]
    w_kinds = ["col", "row", "col", "row"]
    w_shards = [a_w_in[0], a_w_out[0], b_w_in[0], b_w_out[0]]
    w_shapes = [sh.shape for sh in w_shards]
    placed = [_place_own(sh, kind, chip_idx, "place_" + nm) for nm, sh, kind in zip(w_names, w_shards, w_kinds)]
    a_groups = [[(0, 0), (0, 1)], [(0, 2)]]
    rest_groups = [[(w, j) for j in range(3)] for w in range(3)]

    es = e // N_CHIP
    packed0, offs0 = _pack([c, a_conv_w, b_ln_g, b_ln_b], 1024)
    gathered0 = _all_gather(packed0, "gather_params").reshape(N_DEV, -1)
    c_all = gathered0[:, :d]
    per_chip = gathered0[0::2]

    def from_chips_cols(k, rows_):
        got = per_chip[:, offs0[k]:offs0[k] + rows_ * es].reshape(N_CHIP, rows_, es)
        return jnp.transpose(got, (1, 0, 2)).reshape(rows_, e)

    conv_w_full = from_chips_cols(1, 3)
    conv_w = [conv_w_full[k:k + 1] for k in range(3)]
    ln_g, ln_b = from_chips_cols(2, 1), from_chips_cols(3, 1)
    mod_b_cols = lax.dynamic_slice_in_dim(mod_b, chip * wd, wd, axis=1)[:, None, :]
    c_act, mod_part = _mod_fwd(c_all, mod_w, mod_b_cols, "mod_fwd")
    n_layer = mod_w.shape[0]
    mod_gathered = _all_gather(mod_part, "gather_mod")
    a_sems, (wa_in,), a_token = _gather_weights_start(
        placed[:1], w_kinds[:1], w_shapes[:1], a_groups, mod_gathered, "gather_a_w_in_start")
    mod_all = mod_gathered.reshape(N_CHIP, 2, n_layer, N_DEV, wd)[:, 0]
    mod_all = jnp.transpose(mod_all, (1, 2, 0, 3)).reshape(n_layer, N_DEV, N_CHIP * wd)
    mod_me = lax.dynamic_index_in_dim(mod_all, dev, axis=1, keepdims=False)
    shift = [mod_me[l:l + 1, 0:d] for l in range(n_layer)]
    scale = [mod_me[l:l + 1, d:2 * d] for l in range(n_layer)]
    gate = [mod_me[l:l + 1, 2 * d:3 * d] for l in range(n_layer)]

    g0, g1, gf = norm_g[0:1], norm_g[1:2], final_g[None, :]
    h0 = _norm_mod(x2d, g0, scale[0], shift[0], "norm_mod0")

    def slab(r):
        return jnp.bitwise_xor(chip_idx, r)

    def arrived(sems, group, weight, w, after, name):
        return _gather_weights_wait(sems, [(0, j) for _, j in group], [weight], [w_kinds[w]], [w_shapes[w]],
                                    after, name)[0]

    proj0 = _mm_proj_slab(h0, wa_in, None, slab(0), 4, "a_proj_own", a_token)
    wa_in = arrived(a_sems[0], a_groups[0], wa_in, 0, proj0, "gather_wait_near")
    proj0 = _mm_proj_slab(h0, wa_in, proj0, slab(2), 4, "a_proj_x", a_token)
    proj0 = _mm_proj_slab(h0, wa_in, proj0, slab(1), 4, "a_proj_y", a_token)
    wa_in = arrived(a_sems[1], a_groups[1], wa_in, 0, proj0, "gather_wait_far")
    rest_sems, rest_flight, rest_token = _gather_weights_start(
        placed[1:], w_kinds[1:], w_shapes[1:], rest_groups, wa_in, "gather_rest_start")
    proj0 = _mm_proj_slab(h0, wa_in, proj0, slab(3), 4, "a_proj_far", rest_token)
    y0 = _conv_fwd(proj0, conv_w, a_conv_b, "conv_fwd")
    wa_out = arrived(rest_sems[0], rest_groups[0], rest_flight[0], 1, y0, "gather_wait_a_w_out")
    br0 = _mm_nn(y0, wa_out, "a_out")
    x1, h1 = _residual_norm_mod(x2d, br0, gate[0], g1, scale[1], shift[1], "residual_norm_mod1")
    wb_in = arrived(rest_sems[1], rest_groups[1], rest_flight[1], 2, h1, "gather_wait_b_w_in")
    proj1 = _mm_proj(h1, wb_in, 3, "b_proj", rest_token)
    b_s_t = jnp.transpose(b_b_s[0])
    y1 = _gmlp_fwd(proj1, ln_g, ln_b, b_w_s[0], b_s_t, "gmlp_fwd")
    wb_out = arrived(rest_sems[2], rest_groups[2], rest_flight[2], 3, y1, "gather_wait_b_w_out")
    br1 = _mm_nn(y1, wb_out, "b_out")
    dx2, dbr1, loss_part, g_final_g, dgate1 = _head(x1, br1, gate[1], gf, target, "head")

    dy1 = _mm_nt(dbr1, wb_out, "b_out_dx")
    gw_b_out = _mm_dw_out(y1, dbr1, "b_out_dw")
    dproj1, g_w_s, g_b_s_t, g_ln_g, g_ln_b = _gmlp_bwd(
        proj1, dy1, ln_g, ln_b, b_w_s[0], jnp.swapaxes(b_w_s[0], 1, 2), b_s_t, "gmlp_bwd")
    gw_b_in = _mm_dw_in(h1, dproj1, "b_proj_dw")
    b_kinds = ["col", "row"]
    b_send, b_recv, b_grads, b_lands, b_token = _reduce_start(
        [gw_b_in, gw_b_out.reshape(N_CHIP, e // N_CHIP, d)], b_kinds, "reduce_b_start")
    dh1 = _mm_dh(dproj1, wb_in, "b_proj_dx", b_token)
    dx1, dshift1, dscale1, g_g1, dbr0, dgate0 = _norm_mod_bwd(
        dh1, x1, dx2, g1, scale[1], "norm_mod1_bwd", br=br0, gate=gate[0])

    dy0 = _mm_nt(dbr0, wa_out, "a_out_dx")
    gw_a_out = _mm_dw_out(y0, dbr0, "a_out_dw")
    dproj0, g_w0, g_w1, g_w2, g_conv_b = _conv_bwd(proj0, dy0, conv_w, a_conv_b, "conv_bwd")
    gw_a_in = _mm_dw_in(h0, dproj0, "a_proj_dw")
    a_kinds = ["col", "row"]
    a_send, a_recv, a_grads, a_lands, a_token = _reduce_start(
        [gw_a_in, gw_a_out.reshape(N_CHIP, e // N_CHIP, d)], a_kinds, "reduce_a_start")
    dh0 = _mm_dh(dproj0, wa_in, "a_proj_dx", a_token)
    grad_x, dshift0, dscale0, g_g0 = _norm_mod_bwd(dh0, x2d, dx1, g0, scale[0], "norm_mod0_bwd")

    dmod = jnp.concatenate([dshift0, dscale0, dgate0, dshift1, dscale1, dgate1], axis=1)
    small = [loss_part[0, 0:1], g_final_g, g_g0, g_g1, jnp.concatenate([g_w0, g_w1, g_w2], axis=0), g_conv_b, g_ln_g, g_ln_b,
             jnp.transpose(g_b_s_t), g_w_s, dmod]
    packed, offs = _pack(small, 1024)
    rows = packed.shape[0]
    gathered = _all_gather(packed, "gather_small")
    summed = _sum_devices(gathered, rows, "sum_small").reshape(-1)

    def take(k, shape):
        size = math.prod(shape)
        return summed[offs[k]:offs[k] + size].reshape(shape)

    loss = take(0, ())
    grad_final_g = take(1, (d,))
    grad_norm_g = jnp.concatenate([take(2, (1, d)), take(3, (1, d))], axis=0)
    grad_conv_w_full = take(4, (3, e))
    grad_a_conv_b = take(5, (1, e))
    grad_ln_g_full = take(6, (1, e))
    grad_ln_b_full = take(7, (1, e))
    grad_b_b_s = take(8, (1, GROUPS, CHUNK))
    grad_b_w_s = take(9, (1, GROUPS, CHUNK, CHUNK))
    grad_mod_b = take(10, (n_layer, 3 * d))
    grad_a_conv_w = lax.dynamic_slice_in_dim(grad_conv_w_full, chip * es, es, axis=1)[None]
    grad_b_ln_g = lax.dynamic_slice_in_dim(grad_ln_g_full, chip * es, es, axis=1)
    grad_b_ln_b = lax.dynamic_slice_in_dim(grad_ln_b_full, chip * es, es, axis=1)
    dmod_all = gathered.reshape(N_DEV, rows * 1024)[:, offs[10]:offs[10] + n_layer * 3 * d]
    dmod_all = dmod_all.reshape(N_DEV, n_layer, 3 * d)
    dmod_cols = jnp.transpose(lax.dynamic_slice_in_dim(dmod_all, chip * wd, wd, axis=2), (1, 0, 2))

    b_grads, b_lands = _reduce_wait(b_send, b_recv, b_grads, b_lands, b_kinds, summed, "reduce_b_wait")
    a_grads, a_lands = _reduce_wait(a_send, a_recv, a_grads, a_lands, a_kinds, summed, "reduce_a_wait")
    summed_halves = [
        _add_pieces(g, land, kind, chip_idx, core_idx, "add_pieces_" + nm)
        for g, land, kind, nm in [(a_grads[0], a_lands[0], "col", "a_w_in"), (b_grads[0], b_lands[0], "col", "b_w_in"),
                                  (a_grads[1], a_lands[1], "row", "a_w_out"), (b_grads[1], b_lands[1], "row", "b_w_out")]]
    g_a_w_in, g_b_w_in, g_a_w_out, g_b_w_out = _join_halves(summed_halves, "join_halves")

    grad_mod_w, delta_mod_w, new_m_mod_w, new_v_mod_w = _mod_w_update(
        jnp.transpose(c_act), dmod_cols, mod_w, m_mod_w, v_mod_w, "mod_w_update")
    upd = {}
    for nm, w, g, m, v in [("a_w_in", a_w_in, g_a_w_in, m_a_w_in, v_a_w_in), ("a_w_out", a_w_out, g_a_w_out, m_a_w_out, v_a_w_out),
                           ("b_w_in", b_w_in, g_b_w_in, m_b_w_in, v_b_w_in), ("b_w_out", b_w_out, g_b_w_out, m_b_w_out, v_b_w_out)]:
        upd[nm] = tuple(o[None] for o in _adamw_2d(w[0], g, m[0], v[0], "adamw_" + nm))
    small_w = [("mod_b", mod_b, grad_mod_b, m_mod_b, v_mod_b), ("norm_g", norm_g, grad_norm_g, m_norm_g, v_norm_g),
               ("a_conv_w", a_conv_w, grad_a_conv_w, m_a_conv_w, v_a_conv_w),
               ("a_conv_b", a_conv_b, grad_a_conv_b, m_a_conv_b, v_a_conv_b),
               ("b_ln_g", b_ln_g, grad_b_ln_g, m_b_ln_g, v_b_ln_g), ("b_ln_b", b_ln_b, grad_b_ln_b, m_b_ln_b, v_b_ln_b),
               ("b_w_s", b_w_s, grad_b_w_s, m_b_w_s, v_b_w_s), ("b_b_s", b_b_s, grad_b_b_s, m_b_b_s, v_b_b_s),
               ("final_g", final_g, grad_final_g, m_final_g, v_final_g)]

    def flat2d(a):
        return a.reshape(-1, a.shape[-1])

    res = _adamw_small([flat2d(t[1]) for t in small_w], [flat2d(t[2]) for t in small_w],
                       [flat2d(t[3]) for t in small_w], [flat2d(t[4]) for t in small_w], "adamw_small")
    for (nm, w, _, _, _), r3 in zip(small_w, res):
        upd[nm] = tuple(o.reshape(w.shape) for o in r3)
    upd["mod_w"] = (delta_mod_w, new_m_mod_w, new_v_mod_w)

    grads = {"mod_w": grad_mod_w, "mod_b": grad_mod_b, "norm_g": grad_norm_g, "a_w_in": g_a_w_in[None],
             "a_conv_w": grad_a_conv_w, "a_conv_b": grad_a_conv_b, "a_w_out": g_a_w_out[None], "b_w_in": g_b_w_in[None],
             "b_ln_g": grad_b_ln_g, "b_ln_b": grad_b_ln_b, "b_w_s": grad_b_w_s, "b_b_s": grad_b_b_s,
             "b_w_out": g_b_w_out[None], "final_g": grad_final_g}
    order = ["mod_w", "mod_b", "norm_g", "a_w_in", "a_conv_w", "a_conv_b", "a_w_out", "b_w_in", "b_ln_g", "b_ln_b",
             "b_w_s", "b_b_s", "b_w_out", "final_g"]
    return (loss, grad_x[None], *[grads[k] for k in order], *[upd[k][0] for k in order],
            *[upd[k][1] for k in order], *[upd[k][2] for k in order])
```

```python
import functools
import math

import jax
import jax.numpy as jnp
from jax import lax
from jax.experimental import pallas as pl
from jax.experimental.pallas import tpu as pltpu

F32 = jnp.float32
BF16 = jnp.bfloat16
MESH = pl.DeviceIdType.MESH

N_DEV = 8
N_CHIP = 4
SUBLANE = 8
RMS_EPS = 1e-6
LN_EPS = 1e-5
CHUNK = 128
GROUPS = 8
ADAM_LR = 0.001
ADAM_B1 = 0.9
ADAM_B2 = 0.999
ADAM_EPS = 1e-08
ADAM_WD = 0.01
ADAM_STEP = 10
VMEM_LIMIT = 56 << 20

HBM_SPEC = pl.BlockSpec(memory_space=pltpu.HBM)
VMEM_SPEC = pl.BlockSpec(memory_space=pltpu.VMEM)
SEM_SPEC = pl.BlockSpec(memory_space=pltpu.SEMAPHORE)
ANY_SPEC = pl.BlockSpec(memory_space=pl.ANY)
EFFECT = pltpu.SideEffectType.DATAFLOW_SIDE_EFFECTING


def _params(*sem):
    return pltpu.CompilerParams(dimension_semantics=sem, vmem_limit_bytes=VMEM_LIMIT)


def _tile(n, want):
    if n <= want:
        return n
    t = want
    while n % t:
        t -= 128
    return t


def _sigmoid(x):
    return 1.0 / (1.0 + jnp.exp(-x))


def _silu_and_grad(x):
    s = _sigmoid(x)
    return x * s, s * (1.0 + x * (1.0 - s))


def _gelu_and_grad(x):
    cdf = 0.5 * (1.0 + lax.erf(x * (1.0 / math.sqrt(2.0))))
    pdf = jnp.exp(-0.5 * x * x) * (1.0 / math.sqrt(2.0 * math.pi))
    return x * cdf, cdf + x * pdf


def _gelu(x):
    return x * (0.5 * (1.0 + lax.erf(x * (1.0 / math.sqrt(2.0)))))


def _rms(x):
    r = lax.rsqrt(jnp.mean(x * x, axis=-1, keepdims=True) + RMS_EPS)
    return x * r, r


def _rms_bwd(dxn, xn, r):
    return r * (dxn - xn * jnp.mean(dxn * xn, axis=-1, keepdims=True))


def _colsum(a):
    return jnp.sum(a, axis=0, keepdims=True)


def _shift_down(cur, before, k):
    rolled = pltpu.roll(cur, k, 0)
    row = lax.broadcasted_iota(jnp.int32, before.shape, 0)
    head = jnp.where(row < k, pltpu.roll(before, k, 0), rolled[:SUBLANE])
    return jnp.concatenate([head, rolled[SUBLANE:]], axis=0)


def _shift_up(cur, after, k):
    n = cur.shape[0]
    rolled = pltpu.roll(cur, n - k, 0)
    row = lax.broadcasted_iota(jnp.int32, after.shape, 0)
    tail = jnp.where(row >= SUBLANE - k, pltpu.roll(after, SUBLANE - k, 0), rolled[n - SUBLANE:])
    return jnp.concatenate([rolled[:n - SUBLANE], tail], axis=0)


def _after_spec():
    return pl.BlockSpec((SUBLANE, 128), lambda *_: (0, 0))


def _mm_proj(h, w, n_split, name, after):
    s, d = h.shape
    e = w.shape[1] // n_split
    tm, tn = _tile(s, 1024), _tile(e, 1024)
    nj = e // tn

    def body(h_ref, w_ref, after_ref, o_ref):
        o_ref[...] = jnp.dot(h_ref[...], w_ref[...], preferred_element_type=F32)

    return pl.pallas_call(
        body, name=name,
        out_shape=jax.ShapeDtypeStruct((n_split, s, e), F32),
        grid=(s // tm, n_split * nj),
        in_specs=[pl.BlockSpec((tm, d), lambda i, j: (i, 0)), pl.BlockSpec((d, tn), lambda i, j: (0, j)),
                  _after_spec()],
        out_specs=pl.BlockSpec((None, tm, tn), lambda i, j: (j // nj, i, j % nj)),
        compiler_params=_params("parallel", "parallel"),
    )(h, w, after)


def _mm_proj_slab(h, w, proj, q_idx, n_split, name, after):
    s, d = h.shape
    e = w.shape[1] // n_split
    tm, tn = _tile(s, 1024), _tile(e, 1024)
    nj = e // tn

    def body(q_ref, h_ref, w_ref, *rest):
        o_ref = rest[-1]
        o_ref[...] = jnp.dot(h_ref[...], w_ref[...], preferred_element_type=F32)

    in_specs = [pl.BlockSpec((tm, d), lambda i, j, qr: (i, 0)), pl.BlockSpec((d, tn), lambda i, j, qr: (0, qr[0] * nj + j)),
                _after_spec()]
    args = [q_idx, h, w, after]
    aliases = {}
    if proj is not None:
        in_specs.append(ANY_SPEC)
        args.append(proj)
        aliases = {4: 0}
    return pl.pallas_call(
        body, name=name,
        out_shape=jax.ShapeDtypeStruct((n_split, s, e), F32),
        grid_spec=pltpu.PrefetchScalarGridSpec(
            num_scalar_prefetch=1, grid=(s // tm, nj), in_specs=in_specs,
            out_specs=pl.BlockSpec((None, tm, tn), lambda i, j, qr: (qr[0], i, j))),
        input_output_aliases=aliases,
        compiler_params=_params("parallel", "parallel"),
    )(*args)


def _mm_nn(a, b, name):
    m, k = a.shape
    n = b.shape[1]
    tm, tn = _tile(m, 1024), _tile(n, 1024)

    def body(a_ref, b_ref, o_ref):
        o_ref[...] = jnp.dot(a_ref[...], b_ref[...], preferred_element_type=F32)

    return pl.pallas_call(
        body, name=name,
        out_shape=jax.ShapeDtypeStruct((m, n), F32),
        grid=(m // tm, n // tn),
        in_specs=[pl.BlockSpec((tm, k), lambda i, j: (i, 0)), pl.BlockSpec((k, tn), lambda i, j: (0, j))],
        out_specs=pl.BlockSpec((tm, tn), lambda i, j: (i, j)),
        compiler_params=_params("parallel", "parallel"),
    )(a, b)


def _mm_nt(a, b, name):
    m, k = a.shape
    n = b.shape[0]
    tm, tn = _tile(m, 1024), _tile(n, 1024)

    def body(a_ref, b_ref, o_ref):
        o_ref[...] = lax.dot_general(a_ref[...], b_ref[...], (((1,), (1,)), ((), ())), preferred_element_type=F32)

    return pl.pallas_call(
        body, name=name,
        out_shape=jax.ShapeDtypeStruct((m, n), F32),
        grid=(m // tm, n // tn),
        in_specs=[pl.BlockSpec((tm, k), lambda i, j: (i, 0)), pl.BlockSpec((tn, k), lambda i, j: (j, 0))],
        out_specs=pl.BlockSpec((tm, tn), lambda i, j: (i, j)),
        compiler_params=_params("parallel", "parallel"),
    )(a, b)


def _mm_dh(dp, w, name, after):
    nq, s, e = dp.shape
    d = w.shape[0]
    tm, tn, tk = _tile(s, 1024), _tile(d, 1024), _tile(e, 1024)
    nkq = e // tk
    nk = nq * nkq

    def body(a_ref, b_ref, after_ref, o_ref):
        @pl.when(pl.program_id(2) == 0)
        def _():
            o_ref[...] = jnp.zeros_like(o_ref)
        o_ref[...] += lax.dot_general(a_ref[...], b_ref[...], (((1,), (1,)), ((), ())), preferred_element_type=F32)

    return pl.pallas_call(
        body, name=name,
        out_shape=jax.ShapeDtypeStruct((s, d), F32),
        grid=(s // tm, d // tn, nk),
        in_specs=[pl.BlockSpec((None, tm, tk), lambda i, j, k: (k // nkq, i, k % nkq)),
                  pl.BlockSpec((tn, tk), lambda i, j, k: (j, k)), _after_spec()],
        out_specs=pl.BlockSpec((tm, tn), lambda i, j, k: (i, j)),
        compiler_params=_params("parallel", "parallel", "arbitrary"),
    )(dp, w, after)


def _dw_body(n_t):
    def body(a_ref, b_ref, o_ref, acc_ref):
        t = pl.program_id(2)

        @pl.when(t == 0)
        def _():
            acc_ref[...] = jnp.zeros_like(acc_ref)
        acc_ref[...] += lax.dot_general(a_ref[...], b_ref[...], (((0,), (0,)), ((), ())), preferred_element_type=F32)

        @pl.when(t == n_t - 1)
        def _():
            o_ref[...] = acc_ref[...].astype(o_ref.dtype)
    return body


def _mm_dw_in(h, dp, name):
    s, d = h.shape
    nq, _, e = dp.shape
    tm, tn, tt = _tile(d, 1024), _tile(e, 1024), _tile(s, 512)
    nj = e // tn

    return pl.pallas_call(
        _dw_body(s // tt), name=name,
        out_shape=jax.ShapeDtypeStruct((d, nq * e), BF16),
        grid=(d // tm, nq * nj, s // tt),
        in_specs=[pl.BlockSpec((tt, tm), lambda i, j, t: (t, i)),
                  pl.BlockSpec((None, tt, tn), lambda i, j, t: (j // nj, t, j % nj))],
        out_specs=pl.BlockSpec((tm, tn), lambda i, j, t: (i, j)),
        scratch_shapes=[pltpu.VMEM((tm, tn), F32)],
        compiler_params=_params("parallel", "parallel", "arbitrary"),
    )(h, dp)


def _mm_dw_out(y, dbr, name):
    s, e = y.shape
    d = dbr.shape[1]
    tm, tn, tt = _tile(e, 1024), _tile(d, 1024), _tile(s, 512)

    return pl.pallas_call(
        _dw_body(s // tt), name=name,
        out_shape=jax.ShapeDtypeStruct((e, d), BF16),
        grid=(e // tm, d // tn, s // tt),
        in_specs=[pl.BlockSpec((tt, tm), lambda i, j, t: (t, i)), pl.BlockSpec((tt, tn), lambda i, j, t: (t, j))],
        out_specs=pl.BlockSpec((tm, tn), lambda i, j, t: (i, j)),
        scratch_shapes=[pltpu.VMEM((tm, tn), F32)],
        compiler_params=_params("parallel", "parallel", "arbitrary"),
    )(y, dbr)


def _row_spec(ts, d):
    return pl.BlockSpec((ts, d), lambda i: (i, 0))


def _vec_spec(d):
    return pl.BlockSpec((1, d), lambda i: (0, 0))


def _norm_mod(x, g, scale, shift, name):
    s, d = x.shape
    ts = _tile(s, 512)

    def body(x_ref, g_ref, sc_ref, sh_ref, h_ref):
        xn, _ = _rms(x_ref[...])
        h_ref[...] = ((xn * g_ref[...]) * (1.0 + sc_ref[...]) + sh_ref[...]).astype(BF16)

    return pl.pallas_call(
        body, name=name, out_shape=jax.ShapeDtypeStruct((s, d), BF16), grid=(s // ts,),
        in_specs=[_row_spec(ts, d), _vec_spec(d), _vec_spec(d), _vec_spec(d)],
        out_specs=_row_spec(ts, d), compiler_params=_params("parallel"),
    )(x, g, scale, shift)


def _residual_norm_mod(x, br, gate, g, scale, shift, name):
    s, d = x.shape
    ts = _tile(s, 512)

    def body(x_ref, br_ref, gate_ref, g_ref, sc_ref, sh_ref, x1_ref, h_ref):
        x1 = x_ref[...] + gate_ref[...] * br_ref[...]
        x1_ref[...] = x1
        xn, _ = _rms(x1)
        h_ref[...] = ((xn * g_ref[...]) * (1.0 + sc_ref[...]) + sh_ref[...]).astype(BF16)

    return pl.pallas_call(
        body, name=name,
        out_shape=(jax.ShapeDtypeStruct((s, d), F32), jax.ShapeDtypeStruct((s, d), BF16)), grid=(s // ts,),
        in_specs=[_row_spec(ts, d), _row_spec(ts, d)] + [_vec_spec(d)] * 4,
        out_specs=(_row_spec(ts, d), _row_spec(ts, d)), compiler_params=_params("parallel"),
    )(x, br, gate, g, scale, shift)


def _head(x1, br, gate, gf, target, name):
    s, d = x1.shape
    ts = _tile(s, 512)

    def body(x1_ref, br_ref, gate_ref, gf_ref, tg_ref, dx_ref, dbr_ref, loss_ref, dgf_ref, dgate_ref):
        @pl.when(pl.program_id(0) == 0)
        def _():
            loss_ref[...] = jnp.zeros_like(loss_ref)
            dgf_ref[...] = jnp.zeros_like(dgf_ref)
            dgate_ref[...] = jnp.zeros_like(dgate_ref)
        br = br_ref[...]
        gate = gate_ref[...]
        gf = gf_ref[...]
        xn, r = _rms(x1_ref[...] + gate * br)
        err = xn * gf - tg_ref[...]
        loss_ref[...] += 0.5 * jnp.sum(jnp.mean(err * err, axis=-1, keepdims=True))
        dout = err * (1.0 / d)
        dgf_ref[...] += _colsum(dout * xn)
        dx = _rms_bwd(dout * gf, xn, r)
        dx_ref[...] = dx
        dgate_ref[...] += _colsum(dx * br)
        dbr_ref[...] = (gate * dx).astype(BF16)

    return pl.pallas_call(
        body, name=name,
        out_shape=(jax.ShapeDtypeStruct((s, d), F32), jax.ShapeDtypeStruct((s, d), BF16),
                   jax.ShapeDtypeStruct((SUBLANE, 128), F32), jax.ShapeDtypeStruct((1, d), F32),
                   jax.ShapeDtypeStruct((1, d), F32)),
        grid=(s // ts,),
        in_specs=[_row_spec(ts, d), _row_spec(ts, d), _vec_spec(d), _vec_spec(d), _row_spec(ts, d)],
        out_specs=(_row_spec(ts, d), _row_spec(ts, d), pl.BlockSpec((SUBLANE, 128), lambda i: (0, 0)),
                   _vec_spec(d), _vec_spec(d)),
        compiler_params=_params("arbitrary"),
    )(x1, br, gate, gf, target)


def _norm_mod_bwd(dh, x, dx_in, g, scale, name, br=None, gate=None):
    s, d = x.shape
    ts = _tile(s, 512)
    has_branch = br is not None

    def body(*refs):
        if has_branch:
            (dh_ref, x_ref, dxin_ref, g_ref, sc_ref, br_ref, gate_ref,
             dx_ref, dsh_ref, dsc_ref, dg_ref, dbr_ref, dgate_ref) = refs
        else:
            dh_ref, x_ref, dxin_ref, g_ref, sc_ref, dx_ref, dsh_ref, dsc_ref, dg_ref = refs

        @pl.when(pl.program_id(0) == 0)
        def _():
            dsh_ref[...] = jnp.zeros_like(dsh_ref)
            dsc_ref[...] = jnp.zeros_like(dsc_ref)
            dg_ref[...] = jnp.zeros_like(dg_ref)
            if has_branch:
                dgate_ref[...] = jnp.zeros_like(dgate_ref)
        dh = dh_ref[...]
        g = g_ref[...]
        xn, r = _rms(x_ref[...])
        dsh_ref[...] += _colsum(dh)
        dsc_ref[...] += _colsum(dh * (xn * g))
        da = dh * (1.0 + sc_ref[...])
        dg_ref[...] += _colsum(da * xn)
        dx = dxin_ref[...] + _rms_bwd(da * g, xn, r)
        dx_ref[...] = dx
        if has_branch:
            dgate_ref[...] += _colsum(dx * br_ref[...])
            dbr_ref[...] = (gate_ref[...] * dx).astype(BF16)

    vec_out = jax.ShapeDtypeStruct((1, d), F32)
    out_shape = [jax.ShapeDtypeStruct((s, d), F32), vec_out, vec_out, vec_out]
    out_specs = [_row_spec(ts, d), _vec_spec(d), _vec_spec(d), _vec_spec(d)]
    in_specs = [_row_spec(ts, d)] * 3 + [_vec_spec(d)] * 2
    args = [dh, x, dx_in, g, scale]
    if has_branch:
        out_shape += [jax.ShapeDtypeStruct((s, d), BF16), vec_out]
        out_specs += [_row_spec(ts, d), _vec_spec(d)]
        in_specs += [_row_spec(ts, d), _vec_spec(d)]
        args += [br, gate]
    return pl.pallas_call(
        body, name=name, out_shape=tuple(out_shape), grid=(s // ts,),
        in_specs=in_specs, out_specs=tuple(out_specs), compiler_params=_params("arbitrary"),
    )(*args)


def _conv_tiles(s, e):
    return _tile(s, 512), _tile(e, 512)


def _conv_fwd(proj, conv_w, conv_b, name):
    _, s, e = proj.shape
    ts, te = _conv_tiles(s, e)
    hb = ts // SUBLANE

    def body(bg_ref, cg_ref, xi_ref, z_ref, cgp_ref, xip_ref, w0_ref, w1_ref, w2_ref, b_ref, y_ref):
        cx = cg_ref[...] * xi_ref[...]
        before = jnp.where(pl.program_id(0) > 0, cgp_ref[...] * xip_ref[...], 0.0)
        conv = b_ref[...] + w2_ref[...] * cx
        conv = conv + w0_ref[...] * _shift_down(cx, before, 2)
        conv = conv + w1_ref[...] * _shift_down(cx, before, 1)
        z = z_ref[...]
        y_ref[...] = ((z * _sigmoid(z)) * bg_ref[...] * conv).astype(BF16)

    def part(q):
        return pl.BlockSpec((None, ts, te), lambda i, j: (q, i, j))

    def halo_before(q):
        return pl.BlockSpec((None, SUBLANE, te), lambda i, j: (q, jnp.maximum(i * hb - 1, 0), j))

    return pl.pallas_call(
        body, name=name, out_shape=jax.ShapeDtypeStruct((s, e), BF16), grid=(s // ts, e // te),
        in_specs=[part(0), part(1), part(2), part(3), halo_before(1), halo_before(2)]
        + [pl.BlockSpec((1, te), lambda i, j: (0, j))] * 4,
        out_specs=pl.BlockSpec((ts, te), lambda i, j: (i, j)),
        compiler_params=_params("parallel", "parallel"),
    )(proj, proj, proj, proj, proj, proj, *conv_w, conv_b)


def _conv_bwd(proj, dy, conv_w, conv_b, name):
    _, s, e = proj.shape
    ts, te = _conv_tiles(s, e)
    hb = ts // SUBLANE
    n_i = s // ts
    last_halo = s // SUBLANE - 1

    def body(bg_ref, cg_ref, xi_ref, z_ref, dy_ref, cgp_ref, xip_ref, bgn_ref, zn_ref, dyn_ref,
             w0_ref, w1_ref, w2_ref, b_ref, dp_ref, dw0_ref, dw1_ref, dw2_ref, db_ref):
        i = pl.program_id(1)

        @pl.when(i == 0)
        def _():
            for acc in (dw0_ref, dw1_ref, dw2_ref, db_ref):
                acc[...] = jnp.zeros_like(acc)
        bg, cg, xi, z, dy = bg_ref[...], cg_ref[...], xi_ref[...], z_ref[...], dy_ref[...]
        w0, w1, w2 = w0_ref[...], w1_ref[...], w2_ref[...]
        cx = cg * xi
        before = jnp.where(i > 0, cgp_ref[...] * xip_ref[...], 0.0)
        cx1 = _shift_down(cx, before, 1)
        cx2 = _shift_down(cx, before, 2)
        conv = b_ref[...] + w2 * cx
        conv = conv + w0 * cx2
        conv = conv + w1 * cx1
        sz, dsz = _silu_and_grad(z)
        dp_ref[3] = (dy * bg * conv * dsz).astype(BF16)
        dp_ref[0] = (dy * sz * conv).astype(BF16)
        dconv = dy * sz * bg
        zn = zn_ref[...]
        after = jnp.where(i < n_i - 1, dyn_ref[...] * (zn * _sigmoid(zn)) * bgn_ref[...], 0.0)
        db_ref[...] += _colsum(dconv)
        dw2_ref[...] += _colsum(dconv * cx)
        dw1_ref[...] += _colsum(dconv * cx1)
        dw0_ref[...] += _colsum(dconv * cx2)
        dcx = w2 * dconv + w1 * _shift_up(dconv, after, 1) + w0 * _shift_up(dconv, after, 2)
        dp_ref[1] = (dcx * xi).astype(BF16)
        dp_ref[2] = (dcx * cg).astype(BF16)

    def part(q):
        return pl.BlockSpec((None, ts, te), lambda j, i: (q, i, j))

    def halo_before(q):
        return pl.BlockSpec((None, SUBLANE, te), lambda j, i: (q, jnp.maximum(i * hb - 1, 0), j))

    def halo_after(q):
        return pl.BlockSpec((None, SUBLANE, te), lambda j, i: (q, jnp.minimum((i + 1) * hb, last_halo), j))

    return pl.pallas_call(
        body, name=name,
        out_shape=(jax.ShapeDtypeStruct((4, s, e), BF16),) + (jax.ShapeDtypeStruct((1, e), F32),) * 4,
        grid=(e // te, n_i),
        in_specs=[part(0), part(1), part(2), part(3), pl.BlockSpec((ts, te), lambda j, i: (i, j)),
                  halo_before(1), halo_before(2), halo_after(0), halo_after(3),
                  pl.BlockSpec((SUBLANE, te), lambda j, i: (jnp.minimum((i + 1) * hb, last_halo), j))]
        + [pl.BlockSpec((1, te), lambda j, i: (0, j))] * 4,
        out_specs=(pl.BlockSpec((4, ts, te), lambda j, i: (0, i, j)),) + (pl.BlockSpec((1, te), lambda j, i: (0, j)),) * 4,
        compiler_params=_params("parallel", "arbitrary"),
    )(proj, proj, proj, proj, dy, proj, proj, proj, proj, dy, *conv_w, conv_b)


def _tril(w):
    row = lax.broadcasted_iota(jnp.int32, w.shape, 0)
    col = lax.broadcasted_iota(jnp.int32, w.shape, 1)
    return jnp.where(row >= col, w, 0.0)


def _triu(w):
    row = lax.broadcasted_iota(jnp.int32, w.shape, 0)
    col = lax.broadcasted_iota(jnp.int32, w.shape, 1)
    return jnp.where(row <= col, w, 0.0)


def _layer_norm_fwd(v, g, b):
    mu = jnp.mean(v, axis=-1, keepdims=True)
    vc = v - mu
    rstd = lax.rsqrt(jnp.mean(vc * vc, axis=-1, keepdims=True) + LN_EPS)
    vhat = vc * rstd
    return vhat * g + b, vhat, rstd


def _gmlp_fwd(proj, ln_g, ln_b, w_s, b_s_t, name):
    _, s, e = proj.shape
    gw = e // GROUPS

    def body(pu_ref, pv_ref, pz_ref, g_ref, b_ref, ws_ref, bs_ref, y_ref, vn_scr, mix_scr):
        vn, _, _ = _layer_norm_fwd(_gelu(pv_ref[...]), g_ref[...], b_ref[...])
        vn_scr[...] = vn.astype(BF16)
        for gi in range(GROUPS):
            cols = slice(gi * gw, (gi + 1) * gw)
            wt = _tril(ws_ref[gi]).astype(BF16)
            mix_scr[:, cols] = jnp.dot(wt, vn_scr[:, cols], preferred_element_type=F32) + bs_ref[:, gi:gi + 1]
        z = pz_ref[...]
        y_ref[...] = ((z * _sigmoid(z)) * (_gelu(pu_ref[...]) * mix_scr[...])).astype(BF16)

    def part(q):
        return pl.BlockSpec((None, CHUNK, e), lambda i: (q, i, 0))

    return pl.pallas_call(
        body, name=name, out_shape=jax.ShapeDtypeStruct((s, e), BF16), grid=(s // CHUNK,),
        in_specs=[part(0), part(1), part(2), _vec_spec(e), _vec_spec(e),
                  pl.BlockSpec((GROUPS, CHUNK, CHUNK), lambda i: (0, 0, 0)),
                  pl.BlockSpec((CHUNK, GROUPS), lambda i: (0, 0))],
        out_specs=pl.BlockSpec((CHUNK, e), lambda i: (i, 0)),
        scratch_shapes=[pltpu.VMEM((CHUNK, e), BF16), pltpu.VMEM((CHUNK, e), F32)],
        compiler_params=_params("parallel"),
    )(proj, proj, proj, ln_g, ln_b, w_s, b_s_t)


def _gmlp_bwd(proj, dy, ln_g, ln_b, w_s, w_s_t, b_s_t, name):
    _, s, e = proj.shape
    gw = e // GROUPS
    n_i = s // CHUNK

    def body(pu_ref, pv_ref, pz_ref, dy_ref, g_ref, b_ref, ws_ref, wst_ref, bs_ref,
             dp_ref, dws_ref, dbs_ref, dlg_ref, dlb_ref, vn_scr, mix_scr, dm_scr, dvn_scr, dmacc_scr):
        i = pl.program_id(0)

        @pl.when(i == 0)
        def _():
            dws_ref[...] = jnp.zeros_like(dws_ref)
            dlg_ref[...] = jnp.zeros_like(dlg_ref)
            dlb_ref[...] = jnp.zeros_like(dlb_ref)
            dmacc_scr[...] = jnp.zeros_like(dmacc_scr)
        ln_g = g_ref[...]
        u, du_dpu = _gelu_and_grad(pu_ref[...])
        v, dv_dpv = _gelu_and_grad(pv_ref[...])
        vn, vhat, rstd = _layer_norm_fwd(v, ln_g, b_ref[...])
        vn_scr[...] = vn.astype(BF16)
        for gi in range(GROUPS):
            cols = slice(gi * gw, (gi + 1) * gw)
            wt = _tril(ws_ref[gi]).astype(BF16)
            mix_scr[:, cols] = jnp.dot(wt, vn_scr[:, cols], preferred_element_type=F32) + bs_ref[:, gi:gi + 1]
        mixed = mix_scr[...]
        dy = dy_ref[...]
        sz, dsz = _silu_and_grad(pz_ref[...])
        ds = dy * sz
        dp_ref[2] = (dy * (u * mixed) * dsz).astype(BF16)
        dp_ref[0] = (ds * mixed * du_dpu).astype(BF16)
        dm = ds * u
        dmacc_scr[...] += dm
        dm_scr[...] = dm.astype(BF16)
        for gi in range(GROUPS):
            cols = slice(gi * gw, (gi + 1) * gw)
            dmg = dm_scr[:, cols]
            dws_ref[gi] += lax.dot_general(dmg, vn_scr[:, cols], (((1,), (1,)), ((), ())), preferred_element_type=F32)
            wtt = _triu(wst_ref[gi]).astype(BF16)
            dvn_scr[:, cols] = jnp.dot(wtt, dmg, preferred_element_type=F32)
        dvn = dvn_scr[...]
        dlg_ref[...] += _colsum(dvn * vhat)
        dlb_ref[...] += _colsum(dvn)
        dvh = dvn * ln_g
        dv = rstd * (dvh - jnp.mean(dvh, axis=-1, keepdims=True) - vhat * jnp.mean(dvh * vhat, axis=-1, keepdims=True))
        dp_ref[1] = (dv * dv_dpv).astype(BF16)

        @pl.when(i == n_i - 1)
        def _():
            for gi in range(GROUPS):
                dws_ref[gi] = _tril(dws_ref[gi])
                dbs_ref[:, gi:gi + 1] = jnp.sum(dmacc_scr[:, gi * gw:(gi + 1) * gw], axis=1, keepdims=True)

    def part(q):
        return pl.BlockSpec((None, CHUNK, e), lambda i: (q, i, 0))

    w_spec = pl.BlockSpec((GROUPS, CHUNK, CHUNK), lambda i: (0, 0, 0))
    bs_spec = pl.BlockSpec((CHUNK, GROUPS), lambda i: (0, 0))
    return pl.pallas_call(
        body, name=name,
        out_shape=(jax.ShapeDtypeStruct((3, s, e), BF16), jax.ShapeDtypeStruct((GROUPS, CHUNK, CHUNK), F32),
                   jax.ShapeDtypeStruct((CHUNK, GROUPS), F32), jax.ShapeDtypeStruct((1, e), F32),
                   jax.ShapeDtypeStruct((1, e), F32)),
        grid=(n_i,),
        in_specs=[part(0), part(1), part(2), pl.BlockSpec((CHUNK, e), lambda i: (i, 0)), _vec_spec(e), _vec_spec(e),
                  w_spec, w_spec, bs_spec],
        out_specs=(pl.BlockSpec((3, CHUNK, e), lambda i: (0, i, 0)), w_spec, bs_spec, _vec_spec(e), _vec_spec(e)),
        scratch_shapes=[pltpu.VMEM((CHUNK, e), BF16), pltpu.VMEM((CHUNK, e), F32), pltpu.VMEM((CHUNK, e), BF16),
                        pltpu.VMEM((CHUNK, e), F32), pltpu.VMEM((CHUNK, e), F32)],
        compiler_params=_params("arbitrary"),
    )(proj, proj, proj, dy, ln_g, ln_b, w_s, w_s_t, b_s_t)


def _mod_fwd(c_all, mod_w, mod_b_cols, name):
    n_layer, d, w = mod_w.shape

    def body(c_ref, w_ref, b_ref, ca_ref, o_ref):
        c = c_ref[...]
        ca = c * _sigmoid(c)
        ca_ref[...] = ca
        for li in range(n_layer):
            o_ref[li * N_DEV:(li + 1) * N_DEV, :] = (
                jnp.dot(ca, w_ref[li], preferred_element_type=F32, precision=lax.Precision.HIGHEST) + b_ref[li])

    return pl.pallas_call(
        body, name=name,
        out_shape=(jax.ShapeDtypeStruct((N_DEV, d), F32), jax.ShapeDtypeStruct((n_layer * N_DEV, w), F32)),
        in_specs=[VMEM_SPEC] * 3, out_specs=(VMEM_SPEC, VMEM_SPEC),
        compiler_params=pltpu.CompilerParams(vmem_limit_bytes=VMEM_LIMIT),
    )(c_all, mod_w, mod_b_cols)


def _adamw(w, g, m, v):
    m = ADAM_B1 * m + (1.0 - ADAM_B1) * g
    v = ADAM_B2 * v + (1.0 - ADAM_B2) * (g * g)
    m_hat = m / (1.0 - ADAM_B1 ** ADAM_STEP)
    v_hat = v / (1.0 - ADAM_B2 ** ADAM_STEP)
    delta = -ADAM_LR * (m_hat / (jnp.sqrt(v_hat) + ADAM_EPS) + ADAM_WD * w)
    return delta, m, v


def _adamw_2d(w, g, m, v, name):
    r, c = w.shape
    tr, tc = _tile(r, 512), _tile(c, 1024)

    def body(w_ref, g_ref, m_ref, v_ref, d_ref, nm_ref, nv_ref):
        d_ref[...], nm_ref[...], nv_ref[...] = _adamw(w_ref[...], g_ref[...], m_ref[...], v_ref[...])

    spec = pl.BlockSpec((tr, tc), lambda i, j: (i, j))
    shape = jax.ShapeDtypeStruct((r, c), F32)
    return pl.pallas_call(
        body, name=name, out_shape=(shape, shape, shape), grid=(r // tr, c // tc),
        in_specs=[spec] * 4, out_specs=(spec, spec, spec), compiler_params=_params("parallel", "parallel"),
    )(w, g, m, v)


def _mod_w_update(ca_t, dmod_cols, w, m, v, name):
    n_layer, d, wd = w.shape
    tr = _tile(d, 256)

    def body(ca_ref, dm_ref, w_ref, m_ref, v_ref, g_ref, d_ref, nm_ref, nv_ref):
        ca = ca_ref[...]
        dm = dm_ref[...]
        g = ca[:, 0:1] * dm[0:1, :]
        for b in range(1, N_DEV):
            g = g + ca[:, b:b + 1] * dm[b:b + 1, :]
        g_ref[...] = g
        d_ref[...], nm_ref[...], nv_ref[...] = _adamw(w_ref[...], g, m_ref[...], v_ref[...])

    spec = pl.BlockSpec((None, tr, wd), lambda l, i: (l, i, 0))
    shape = jax.ShapeDtypeStruct((n_layer, d, wd), F32)
    return pl.pallas_call(
        body, name=name, out_shape=(shape,) * 4, grid=(n_layer, d // tr),
        in_specs=[pl.BlockSpec((tr, N_DEV), lambda l, i: (i, 0)), pl.BlockSpec((None, N_DEV, wd), lambda l, i: (l, 0, 0)),
                  spec, spec, spec],
        out_specs=(spec,) * 4, compiler_params=_params("parallel", "parallel"),
    )(ca_t, dmod_cols, w, m, v)


def _adamw_small(ws, gs, ms, vs, name):
    n = len(ws)

    def body(*refs):
        ins, outs = refs[:4 * n], refs[4 * n:]
        for k in range(n):
            delta, nm, nv = _adamw(ins[k][...], ins[n + k][...], ins[2 * n + k][...], ins[3 * n + k][...])
            outs[3 * k][...] = delta
            outs[3 * k + 1][...] = nm
            outs[3 * k + 2][...] = nv

    out_shape = []
    for w in ws:
        out_shape += [jax.ShapeDtypeStruct(w.shape, F32)] * 3
    outs = pl.pallas_call(
        body, name=name, out_shape=tuple(out_shape),
        in_specs=[VMEM_SPEC] * (4 * n), out_specs=tuple([VMEM_SPEC] * (3 * n)),
        compiler_params=pltpu.CompilerParams(vmem_limit_bytes=VMEM_LIMIT),
    )(*ws, *gs, *ms, *vs)
    return [tuple(outs[3 * k:3 * k + 3]) for k in range(n)]


def _place():
    return lax.axis_index("x"), lax.axis_index("y"), lax.axis_index("c")


def _other_chips(x, y):
    return [(1 - x, y), (x, 1 - y), (1 - x, 1 - y)]


def _all_gather(block, name):
    m_per, n = block.shape

    def body(x_ref, out_ref, send_sems, recv_sems, local_sem):
        x, y, c = _place()
        me, sibling = (x, y, c), (x, y, 1 - c)
        chips = _other_chips(x, y)

        def rows(px, py, pc):
            return out_ref.at[pl.ds((4 * px + 2 * py + pc) * m_per, m_per), :]

        def copy(k, blk, to, src=None):
            return pltpu.make_async_remote_copy(
                src_ref=rows(*blk) if src is None else src, dst_ref=rows(*blk),
                send_sem=send_sems.at[k], recv_sem=recv_sems.at[k], device_id=to, device_id_type=MESH)

        mine = pltpu.make_async_copy(x_ref, rows(*me), local_sem)
        mine.start()
        first = [copy(0, me, sibling, src=x_ref)]
        first += [copy(1 + j, me, (*chip, c), src=x_ref) for j, chip in enumerate(chips)]
        for cp in first:
            cp.start()
        passed = [copy(4 + j, (*chip, c), sibling) for j, chip in enumerate(chips)]
        for j, chip in enumerate(chips):
            copy(1 + j, (*chip, c), me).wait_recv()
            passed[j].start()
        copy(0, sibling, me).wait_recv()
        for j, chip in enumerate(chips):
            copy(4 + j, (*chip, 1 - c), me).wait_recv()
        for cp in first + passed:
            cp.wait_send()
        mine.wait()

    return pl.pallas_call(
        body, name=name, out_shape=jax.ShapeDtypeStruct((N_DEV * m_per, n), F32),
        in_specs=[VMEM_SPEC], out_specs=VMEM_SPEC,
        scratch_shapes=[pltpu.SemaphoreType.DMA((7,)), pltpu.SemaphoreType.DMA((7,)), pltpu.SemaphoreType.DMA],
        compiler_params=pltpu.CompilerParams(vmem_limit_bytes=VMEM_LIMIT),
    )(block)


def _hbm(a):
    return pltpu.with_memory_space_constraint(a, pltpu.HBM)


def _place_own(shard, kind, chip_idx, name):
    r, cdim = shard.shape
    tr, tc = _tile(r, 512), _tile(cdim, 1024)
    nrb, ncb = r // tr, cdim // tc

    def body(k_ref, s_ref, o_ref):
        o_ref[...] = s_ref[...].astype(BF16)

    if kind == "col":
        full, o_map = (r, N_CHIP * cdim), lambda i, j, kr: (i, kr[0] * ncb + j)
    else:
        full, o_map = (N_CHIP * r, cdim), lambda i, j, kr: (kr[0] * nrb + i, j)
    return pl.pallas_call(
        body, name=name, out_shape=jax.ShapeDtypeStruct(full, BF16),
        grid_spec=pltpu.PrefetchScalarGridSpec(
            num_scalar_prefetch=1, grid=(nrb, ncb),
            in_specs=[pl.BlockSpec((tr, tc), lambda i, j, kr: (i, j))],
            out_specs=pl.BlockSpec((tr, tc), o_map)),
        compiler_params=_params("parallel", "parallel"),
    )(chip_idx, shard)


def _weight_window(ref, kind, shard_shape, k, half):
    r, cdim = shard_shape
    hr = r // 2
    if kind == "col":
        return ref.at[pl.ds(half * hr, hr), pl.ds(pl.multiple_of(k * cdim, 128), cdim)]
    return ref.at[pl.ds(pl.multiple_of(k * r + half * hr, 2 * SUBLANE), hr), :]


def _gather_weights_start(fulls, kinds, shard_shapes, groups, after, name):
    n, ng = len(fulls), len(groups)

    def body(*refs):
        ins = refs[:n]
        sems = refs[n + 1:n + 1 + 2 * ng]
        token = refs[2 * n + 1 + 2 * ng]
        x, y, c = _place()
        chips = _other_chips(x, y)
        for g, group in enumerate(groups):
            for pos, (w, j) in enumerate(group):
                own = _weight_window(ins[w], kinds[w], shard_shapes[w], 2 * x + y, c)
                for cc in range(2):
                    pltpu.make_async_remote_copy(
                        src_ref=own, dst_ref=own,
                        send_sem=sems[2 * g].at[2 * pos + cc], recv_sem=sems[2 * g + 1].at[2 * pos + c],
                        device_id=(*chips[j], cc), device_id_type=MESH).start()
        token[...] = jnp.zeros_like(token)

    sem_shapes = []
    for group in groups:
        sem_shapes += [pltpu.SemaphoreType.DMA((2 * len(group),))] * 2
    outs = pl.pallas_call(
        body, name=name,
        out_shape=tuple(sem_shapes) + tuple(pltpu.HBM(f.shape, f.dtype) for f in fulls)
        + (jax.ShapeDtypeStruct((SUBLANE, 128), F32),),
        in_specs=[HBM_SPEC] * n + [ANY_SPEC], out_specs=(SEM_SPEC,) * (2 * ng) + (HBM_SPEC,) * n + (VMEM_SPEC,),
        input_output_aliases={w: 2 * ng + w for w in range(n)},
        compiler_params=pltpu.CompilerParams(has_side_effects=EFFECT),
    )(*[_hbm(f) for f in fulls], after)
    sems = [(outs[2 * g], outs[2 * g + 1]) for g in range(ng)]
    return sems, list(outs[2 * ng:2 * ng + n]), outs[2 * ng + n]


def _gather_weights_wait(sems, group, fulls, kinds, shard_shapes, after, name):
    n = len(fulls)

    def body(*refs):
        ins = refs[:n]
        send_sems, recv_sems = refs[n], refs[n + 1]
        x, y, c = _place()
        chips = _other_chips(x, y)
        for pos, (w, j) in enumerate(group):
            own = _weight_window(ins[w], kinds[w], shard_shapes[w], 2 * x + y, c)
            for cc in range(2):
                landed = _weight_window(ins[w], kinds[w], shard_shapes[w], 2 * chips[j][0] + chips[j][1], cc)
                k = 2 * pos + cc
                pltpu.make_async_remote_copy(
                    src_ref=own, dst_ref=own, send_sem=send_sems.at[k], recv_sem=recv_sems.at[k],
                    device_id=(*chips[j], cc), device_id_type=MESH).wait_send()
                pltpu.make_async_remote_copy(
                    src_ref=landed, dst_ref=landed, send_sem=send_sems.at[k], recv_sem=recv_sems.at[k],
                    device_id=(*chips[j], cc), device_id_type=MESH).wait_recv()

    return list(pl.pallas_call(
        body, name=name, out_shape=tuple(pltpu.HBM(f.shape, f.dtype) for f in fulls),
        in_specs=[HBM_SPEC] * n + [SEM_SPEC, SEM_SPEC, ANY_SPEC], out_specs=(HBM_SPEC,) * n,
        input_output_aliases={w: w for w in range(n)},
        compiler_params=pltpu.CompilerParams(has_side_effects=EFFECT),
    )(*fulls, sems[0], sems[1], after))


def _grad_piece(ref, kind, h, cdim, k, half):
    if kind == "col":
        return ref.at[pl.ds(half * h, h), pl.ds(pl.multiple_of(k * cdim, 128), cdim)]
    return ref.at[k, pl.ds(half * h, h), :]


def _grad_dims(g, kind):
    return (g.shape[0] // 2, g.shape[1] // N_CHIP) if kind == "col" else (g.shape[1] // 2, g.shape[2])


def _reduce_start(grads, kinds, name):
    n = len(grads)
    dims = [_grad_dims(g, kind) for g, kind in zip(grads, kinds)]
    lands = [lax.empty((N_DEV - 1, h, cdim), g.dtype) for g, (h, cdim) in zip(grads, dims)]

    def body(*refs):
        g_ins, land_ins = refs[:n], refs[n:2 * n]
        send_sems, recv_sems = refs[2 * n], refs[2 * n + 1]
        token = refs[4 * n + 2]
        x, y, c = _place()
        for w in range(n):
            h, cdim = dims[w]
            base = (N_DEV - 1) * w
            pltpu.make_async_remote_copy(
                src_ref=_grad_piece(g_ins[w], kinds[w], h, cdim, 2 * x + y, 1 - c), dst_ref=land_ins[w].at[0],
                send_sem=send_sems.at[base], recv_sem=recv_sems.at[base],
                device_id=(x, y, 1 - c), device_id_type=MESH).start()
            for j, chip in enumerate(_other_chips(x, y)):
                for cc in range(2):
                    pltpu.make_async_remote_copy(
                        src_ref=_grad_piece(g_ins[w], kinds[w], h, cdim, 2 * chip[0] + chip[1], cc),
                        dst_ref=land_ins[w].at[1 + 2 * j + c],
                        send_sem=send_sems.at[base + 1 + 2 * j + cc], recv_sem=recv_sems.at[base + 1 + 2 * j + c],
                        device_id=(*chip, cc), device_id_type=MESH).start()
        token[...] = jnp.zeros_like(token)

    sems = pltpu.SemaphoreType.DMA(((N_DEV - 1) * n,))
    outs = pl.pallas_call(
        body, name=name,
        out_shape=(sems, sems) + tuple(pltpu.HBM(a.shape, a.dtype) for a in list(grads) + lands)
        + (jax.ShapeDtypeStruct((SUBLANE, 128), F32),),
        in_specs=[HBM_SPEC] * (2 * n), out_specs=(SEM_SPEC, SEM_SPEC) + (HBM_SPEC,) * (2 * n) + (VMEM_SPEC,),
        input_output_aliases={i: 2 + i for i in range(2 * n)},
        compiler_params=pltpu.CompilerParams(has_side_effects=EFFECT),
    )(*[_hbm(a) for a in list(grads) + lands])
    return outs[0], outs[1], list(outs[2:2 + n]), list(outs[2 + n:2 + 2 * n]), outs[2 + 2 * n]


def _reduce_wait(send_sems, recv_sems, grads, lands, kinds, after, name):
    n = len(grads)
    dims = [_grad_dims(g, kind) for g, kind in zip(grads, kinds)]

    def body(*refs):
        g_ins, land_ins = refs[:n], refs[n:2 * n]
        send_sems, recv_sems = refs[2 * n], refs[2 * n + 1]
        x, y, c = _place()
        for w in range(n):
            h, cdim = dims[w]
            piece = _grad_piece(g_ins[w], kinds[w], h, cdim, 2 * x + y, c)
            for s in range(N_DEV - 1):
                k = (N_DEV - 1) * w + s
                slot = land_ins[w].at[s]
                pltpu.make_async_remote_copy(
                    src_ref=piece, dst_ref=slot, send_sem=send_sems.at[k], recv_sem=recv_sems.at[k],
                    device_id=(x, y, 1 - c), device_id_type=MESH).wait_send()
                pltpu.make_async_remote_copy(
                    src_ref=piece, dst_ref=slot, send_sem=send_sems.at[k], recv_sem=recv_sems.at[k],
                    device_id=(x, y, 1 - c), device_id_type=MESH).wait_recv()

    outs = pl.pallas_call(
        body, name=name, out_shape=tuple(pltpu.HBM(a.shape, a.dtype) for a in list(grads) + list(lands)),
        in_specs=[HBM_SPEC] * (2 * n) + [SEM_SPEC, SEM_SPEC, ANY_SPEC], out_specs=(HBM_SPEC,) * (2 * n),
        input_output_aliases={i: i for i in range(2 * n)},
        compiler_params=pltpu.CompilerParams(has_side_effects=EFFECT),
    )(*grads, *lands, send_sems, recv_sems, after)
    return list(outs[:n]), list(outs[n:])


def _add_pieces(g, land, kind, chip_idx, core_idx, name):
    _, h, cdim = land.shape
    tr, tc = _tile(h, 256), _tile(cdim, 2048)
    nrb, ncb = h // tr, cdim // tc

    def body(k_ref, c_ref, g_ref, l_ref, o_ref):
        acc = g_ref[...].astype(F32)
        for s in range(N_DEV - 1):
            acc = acc + l_ref[s].astype(F32)
        o_ref[...] = acc

    if kind == "col":
        g_spec = pl.BlockSpec((tr, tc), lambda i, j, kr, cr: (cr[0] * nrb + i, kr[0] * ncb + j))
    else:
        g_spec = pl.BlockSpec((None, tr, tc), lambda i, j, kr, cr: (kr[0], cr[0] * nrb + i, j))
    return pl.pallas_call(
        body, name=name, out_shape=jax.ShapeDtypeStruct((2 * h, cdim), F32),
        grid_spec=pltpu.PrefetchScalarGridSpec(
            num_scalar_prefetch=2, grid=(nrb, ncb),
            in_specs=[g_spec, pl.BlockSpec((N_DEV - 1, tr, tc), lambda i, j, kr, cr: (0, i, j))],
            out_specs=pl.BlockSpec((tr, tc), lambda i, j, kr, cr: (cr[0] * nrb + i, j))),
        compiler_params=_params("parallel", "parallel"),
    )(chip_idx, core_idx, g, land)


def _join_halves(shards, name):
    n = len(shards)

    def body(*refs):
        ins, outs = refs[:n], refs[n:2 * n]
        send_sems, recv_sems = refs[2 * n:]
        x, y, c = _place()
        copies = []
        for w in range(n):
            h = shards[w].shape[0] // 2
            cp = pltpu.make_async_remote_copy(
                src_ref=ins[w].at[pl.ds(c * h, h), :], dst_ref=outs[w].at[pl.ds(c * h, h), :],
                send_sem=send_sems.at[w], recv_sem=recv_sems.at[w], device_id=(x, y, 1 - c), device_id_type=MESH)
            cp.start()
            copies.append(cp)
        for w in range(n):
            h = shards[w].shape[0] // 2
            theirs = outs[w].at[pl.ds((1 - c) * h, h), :]
            pltpu.make_async_remote_copy(
                src_ref=theirs, dst_ref=theirs, send_sem=send_sems.at[w], recv_sem=recv_sems.at[w],
                device_id=(x, y, 1 - c), device_id_type=MESH).wait_recv()
        for cp in copies:
            cp.wait_send()

    return pl.pallas_call(
        body, name=name, out_shape=tuple(jax.ShapeDtypeStruct(s.shape, s.dtype) for s in shards),
        in_specs=[HBM_SPEC] * n, out_specs=tuple([HBM_SPEC] * n),
        input_output_aliases={w: w for w in range(n)},
        scratch_shapes=[pltpu.SemaphoreType.DMA((n,))] * 2,
    )(*shards)


def _sum_devices(gathered, rows, name):
    n = gathered.shape[1]

    def body(g_ref, o_ref):
        acc = g_ref[0:rows, :]
        for dev in range(1, N_DEV):
            acc = acc + g_ref[dev * rows:(dev + 1) * rows, :]
        o_ref[...] = acc

    return pl.pallas_call(
        body, name=name, out_shape=jax.ShapeDtypeStruct((rows, n), F32),
        in_specs=[VMEM_SPEC], out_specs=VMEM_SPEC,
        compiler_params=pltpu.CompilerParams(vmem_limit_bytes=VMEM_LIMIT),
    )(gathered)


def _pack(vectors, width):
    flat = [v.reshape(-1) for v in vectors]
    offsets, total = [], 0
    for f in flat:
        offsets.append(total)
        total += f.shape[0]
    rows = -(-total // (width * SUBLANE)) * SUBLANE
    flat.append(jnp.zeros((rows * width - total,), F32))
    return jnp.concatenate(flat).reshape(rows, width), offsets


def kernel(x, c, mod_w, mod_b, norm_g, a_w_in, a_conv_w, a_conv_b, a_w_out, b_w_in, b_ln_g, b_ln_b, b_w_s, b_b_s, b_w_out, final_g, loss_target, m_mod_w, m_mod_b, m_norm_g, m_a_w_in, m_a_conv_w, m_a_conv_b, m_a_w_out, m_b_w_in, m_b_ln_g, m_b_ln_b, m_b_w_s, m_b_b_s, m_b_w_out, m_final_g, v_mod_w, v_mod_b, v_norm_g, v_a_w_in, v_a_conv_w, v_a_conv_b, v_a_w_out, v_b_w_in, v_b_ln_g, v_b_ln_b, v_b_w_s, v_b_b_s, v_b_w_out, v_final_g):
    seq, d = x.shape[1], x.shape[2]
    e = a_conv_b.shape[1]
    wd = mod_w.shape[2]
    ax, ay, ac = _place()
    chip = 2 * ax + ay
    dev = 2 * chip + ac
    chip_idx = jnp.reshape(chip, (1,)).astype(jnp.int32)
    core_idx = jnp.reshape(ac, (1,)).astype(jnp.int32)

    x2d = x[0]
    target = loss_target[0]

    w_names = ["a_w_in", "a_w_out", "b_w_in", "b_w_out"]
    w_kinds = ["col", "row", "col", "row"]
    w_shards = [a_w_in[0], a_w_out[0], b_w_in[0], b_w_out[0]]
    w_shapes = [sh.shape for sh in w_shards]
    placed = [_place_own(sh, kind, chip_idx, "place_" + nm) for nm, sh, kind in zip(w_names, w_shards, w_kinds)]
    a_groups, far_groups = [[(0, 0), (0, 1)]], [[(0, 2)]]
    rest_groups = [[(w, j) for j in range(3)] for w in range(3)]

    es = e // N_CHIP
    packed0, offs0 = _pack([c, a_conv_w, b_ln_g, b_ln_b], 1024)
    gathered0 = _all_gather(packed0, "gather_params").reshape(N_DEV, -1)
    c_all = gathered0[:, :d]
    per_chip = gathered0[0::2]

    def from_chips_cols(k, rows_):
        got = per_chip[:, offs0[k]:offs0[k] + rows_ * es].reshape(N_CHIP, rows_, es)
        return jnp.transpose(got, (1, 0, 2)).reshape(rows_, e)

    conv_w_full = from_chips_cols(1, 3)
    conv_w = [conv_w_full[k:k + 1] for k in range(3)]
    ln_g, ln_b = from_chips_cols(2, 1), from_chips_cols(3, 1)
    mod_b_cols = lax.dynamic_slice_in_dim(mod_b, chip * wd, wd, axis=1)[:, None, :]
    c_act, mod_part = _mod_fwd(c_all, mod_w, mod_b_cols, "mod_fwd")
    n_layer = mod_w.shape[0]
    mod_gathered = _all_gather(mod_part, "gather_mod")
    a_sems, (wa_in,), a_token = _gather_weights_start(
        placed[:1], w_kinds[:1], w_shapes[:1], a_groups, mod_gathered, "gather_a_w_in_start")
    mod_all = mod_gathered.reshape(N_CHIP, 2, n_layer, N_DEV, wd)[:, 0]
    mod_all = jnp.transpose(mod_all, (1, 2, 0, 3)).reshape(n_layer, N_DEV, N_CHIP * wd)
    mod_me = lax.dynamic_index_in_dim(mod_all, dev, axis=1, keepdims=False)
    shift = [mod_me[l:l + 1, 0:d] for l in range(n_layer)]
    scale = [mod_me[l:l + 1, d:2 * d] for l in range(n_layer)]
    gate = [mod_me[l:l + 1, 2 * d:3 * d] for l in range(n_layer)]

    g0, g1, gf = norm_g[0:1], norm_g[1:2], final_g[None, :]
    h0 = _norm_mod(x2d, g0, scale[0], shift[0], "norm_mod0")

    def slab(r):
        return jnp.bitwise_xor(chip_idx, r)

    def arrived(sems, group, weight, w, after, name):
        return _gather_weights_wait(sems, [(0, j) for _, j in group], [weight], [w_kinds[w]], [w_shapes[w]],
                                    after, name)[0]

    proj0 = _mm_proj_slab(h0, wa_in, None, slab(0), 4, "a_proj_own", a_token)
    wa_in = arrived(a_sems[0], a_groups[0], wa_in, 0, proj0, "gather_wait_near")
    far_sems, (wa_in,), far_token = _gather_weights_start(
        [wa_in], w_kinds[:1], w_shapes[:1], far_groups, wa_in, "gather_a_w_in_far_start")
    proj0 = _mm_proj_slab(h0, wa_in, proj0, slab(2), 4, "a_proj_x", far_token)
    proj0 = _mm_proj_slab(h0, wa_in, proj0, slab(1), 4, "a_proj_y", far_token)
    wa_in = arrived(far_sems[0], far_groups[0], wa_in, 0, proj0, "gather_wait_far")
    rest_sems, rest_flight, rest_token = _gather_weights_start(
        placed[1:], w_kinds[1:], w_shapes[1:], rest_groups, wa_in, "gather_rest_start")
    proj0 = _mm_proj_slab(h0, wa_in, proj0, slab(3), 4, "a_proj_far", rest_token)
    y0 = _conv_fwd(proj0, conv_w, a_conv_b, "conv_fwd")
    wa_out = arrived(rest_sems[0], rest_groups[0], rest_flight[0], 1, y0, "gather_wait_a_w_out")
    br0 = _mm_nn(y0, wa_out, "a_out")
    x1, h1 = _residual_norm_mod(x2d, br0, gate[0], g1, scale[1], shift[1], "residual_norm_mod1")
    wb_in = arrived(rest_sems[1], rest_groups[1], rest_flight[1], 2, h1, "gather_wait_b_w_in")
    proj1 = _mm_proj(h1, wb_in, 3, "b_proj", rest_token)
    b_s_t = jnp.transpose(b_b_s[0])
    y1 = _gmlp_fwd(proj1, ln_g, ln_b, b_w_s[0], b_s_t, "gmlp_fwd")
    wb_out = arrived(rest_sems[2], rest_groups[2], rest_flight[2], 3, y1, "gather_wait_b_w_out")
    br1 = _mm_nn(y1, wb_out, "b_out")
    dx2, dbr1, loss_part, g_final_g, dgate1 = _head(x1, br1, gate[1], gf, target, "head")

    dy1 = _mm_nt(dbr1, wb_out, "b_out_dx")
    gw_b_out = _mm_dw_out(y1, dbr1, "b_out_dw")
    dproj1, g_w_s, g_b_s_t, g_ln_g, g_ln_b = _gmlp_bwd(
        proj1, dy1, ln_g, ln_b, b_w_s[0], jnp.swapaxes(b_w_s[0], 1, 2), b_s_t, "gmlp_bwd")
    gw_b_in = _mm_dw_in(h1, dproj1, "b_proj_dw")
    b_kinds = ["col", "row"]
    b_send, b_recv, b_grads, b_lands, b_token = _reduce_start(
        [gw_b_in, gw_b_out.reshape(N_CHIP, e // N_CHIP, d)], b_kinds, "reduce_b_start")
    dh1 = _mm_dh(dproj1, wb_in, "b_proj_dx", b_token)
    dx1, dshift1, dscale1, g_g1, dbr0, dgate0 = _norm_mod_bwd(
        dh1, x1, dx2, g1, scale[1], "norm_mod1_bwd", br=br0, gate=gate[0])

    dy0 = _mm_nt(dbr0, wa_out, "a_out_dx")
    gw_a_out = _mm_dw_out(y0, dbr0, "a_out_dw")
    dproj0, g_w0, g_w1, g_w2, g_conv_b = _conv_bwd(proj0, dy0, conv_w, a_conv_b, "conv_bwd")
    gw_a_in = _mm_dw_in(h0, dproj0, "a_proj_dw")
    a_kinds = ["col", "row"]
    a_send, a_recv, a_grads, a_lands, a_token = _reduce_start(
        [gw_a_in, gw_a_out.reshape(N_CHIP, e // N_CHIP, d)], a_kinds, "reduce_a_start")
    dh0 = _mm_dh(dproj0, wa_in, "a_proj_dx", a_token)
    grad_x, dshift0, dscale0, g_g0 = _norm_mod_bwd(dh0, x2d, dx1, g0, scale[0], "norm_mod0_bwd")

    dmod = jnp.concatenate([dshift0, dscale0, dgate0, dshift1, dscale1, dgate1], axis=1)
    small = [loss_part[0, 0:1], g_final_g, g_g0, g_g1, jnp.concatenate([g_w0, g_w1, g_w2], axis=0), g_conv_b, g_ln_g, g_ln_b,
             jnp.transpose(g_b_s_t), g_w_s, dmod]
    packed, offs = _pack(small, 1024)
    rows = packed.shape[0]
    gathered = _all_gather(packed, "gather_small")
    summed = _sum_devices(gathered, rows, "sum_small").reshape(-1)

    def take(k, shape):
        size = math.prod(shape)
        return summed[offs[k]:offs[k] + size].reshape(shape)

    loss = take(0, ())
    grad_final_g = take(1, (d,))
    grad_norm_g = jnp.concatenate([take(2, (1, d)), take(3, (1, d))], axis=0)
    grad_conv_w_full = take(4, (3, e))
    grad_a_conv_b = take(5, (1, e))
    grad_ln_g_full = take(6, (1, e))
    grad_ln_b_full = take(7, (1, e))
    grad_b_b_s = take(8, (1, GROUPS, CHUNK))
    grad_b_w_s = take(9, (1, GROUPS, CHUNK, CHUNK))
    grad_mod_b = take(10, (n_layer, 3 * d))
    grad_a_conv_w = lax.dynamic_slice_in_dim(grad_conv_w_full, chip * es, es, axis=1)[None]
    grad_b_ln_g = lax.dynamic_slice_in_dim(grad_ln_g_full, chip * es, es, axis=1)
    grad_b_ln_b = lax.dynamic_slice_in_dim(grad_ln_b_full, chip * es, es, axis=1)
    dmod_all = gathered.reshape(N_DEV, rows * 1024)[:, offs[10]:offs[10] + n_layer * 3 * d]
    dmod_all = dmod_all.reshape(N_DEV, n_layer, 3 * d)
    dmod_cols = jnp.transpose(lax.dynamic_slice_in_dim(dmod_all, chip * wd, wd, axis=2), (1, 0, 2))

    b_grads, b_lands = _reduce_wait(b_send, b_recv, b_grads, b_lands, b_kinds, summed, "reduce_b_wait")
    a_grads, a_lands = _reduce_wait(a_send, a_recv, a_grads, a_lands, a_kinds, summed, "reduce_a_wait")
    summed_halves = [
        _add_pieces(g, land, kind, chip_idx, core_idx, "add_pieces_" + nm)
        for g, land, kind, nm in [(a_grads[0], a_lands[0], "col", "a_w_in"), (b_grads[0], b_lands[0], "col", "b_w_in"),
                                  (a_grads[1], a_lands[1], "row", "a_w_out"), (b_grads[1], b_lands[1], "row", "b_w_out")]]
    g_a_w_in, g_b_w_in, g_a_w_out, g_b_w_out = _join_halves(summed_halves, "join_halves")

    grad_mod_w, delta_mod_w, new_m_mod_w, new_v_mod_w = _mod_w_update(
        jnp.transpose(c_act), dmod_cols, mod_w, m_mod_w, v_mod_w, "mod_w_update")
    upd = {}
    for nm, w, g, m, v in [("a_w_in", a_w_in, g_a_w_in, m_a_w_in, v_a_w_in), ("a_w_out", a_w_out, g_a_w_out, m_a_w_out, v_a_w_out),
                           ("b_w_in", b_w_in, g_b_w_in, m_b_w_in, v_b_w_in), ("b_w_out", b_w_out, g_b_w_out, m_b_w_out, v_b_w_out)]:
        upd[nm] = tuple(o[None] for o in _adamw_2d(w[0], g, m[0], v[0], "adamw_" + nm))
    small_w = [("mod_b", mod_b, grad_mod_b, m_mod_b, v_mod_b), ("norm_g", norm_g, grad_norm_g, m_norm_g, v_norm_g),
               ("a_conv_w", a_conv_w, grad_a_conv_w, m_a_conv_w, v_a_conv_w),
               ("a_conv_b", a_conv_b, grad_a_conv_b, m_a_conv_b, v_a_conv_b),
               ("b_ln_g", b_ln_g, grad_b_ln_g, m_b_ln_g, v_b_ln_g), ("b_ln_b", b_ln_b, grad_b_ln_b, m_b_ln_b, v_b_ln_b),
               ("b_w_s", b_w_s, grad_b_w_s, m_b_w_s, v_b_w_s), ("b_b_s", b_b_s, grad_b_b_s, m_b_b_s, v_b_b_s),
               ("final_g", final_g, grad_final_g, m_final_g, v_final_g)]

    def flat2d(a):
        return a.reshape(-1, a.shape[-1])

    res = _adamw_small([flat2d(t[1]) for t in small_w], [flat2d(t[2]) for t in small_w],
                       [flat2d(t[3]) for t in small_w], [flat2d(t[4]) for t in small_w], "adamw_small")
    for (nm, w, _, _, _), r3 in zip(small_w, res):
        upd[nm] = tuple(o.reshape(w.shape) for o in r3)
    upd["mod_w"] = (delta_mod_w, new_m_mod_w, new_v_mod_w)

    grads = {"mod_w": grad_mod_w, "mod_b": grad_mod_b, "norm_g": grad_norm_g, "a_w_in": g_a_w_in[None],
             "a_conv_w": grad_a_conv_w, "a_conv_b": grad_a_conv_b, "a_w_out": g_a_w_out[None], "b_w_in": g_b_w_in[None],
             "b_ln_g": grad_b_ln_g, "b_ln_b": grad_b_ln_b, "b_w_s": grad_b_w_s, "b_b_s": grad_b_b_s,
             "b_w_out": g_b_w_out[None], "final_g": grad_final_g}
    order = ["mod_w", "mod_b", "norm_g", "a_w_in", "a_conv_w", "a_conv_b", "a_w_out", "b_w_in", "b_ln_g", "b_ln_b",
             "b_w_s", "b_b_s", "b_w_out", "final_g"]
    return (loss, grad_x[None], *[grads[k] for k in order], *[upd[k][0] for k in order],
            *[upd[k][1] for k in order], *[upd[k][2] for k in order])
```

```python
import functools
import math

import jax
import jax.numpy as jnp
from jax import lax
from jax.experimental import pallas as pl
from jax.experimental.pallas import tpu as pltpu

F32 = jnp.float32
BF16 = jnp.bfloat16
MESH = pl.DeviceIdType.MESH

N_DEV = 8
N_CHIP = 4
SUBLANE = 8
RMS_EPS = 1e-6
LN_EPS = 1e-5
CHUNK = 128
GROUPS = 8
ADAM_LR = 0.001
ADAM_B1 = 0.9
ADAM_B2 = 0.999
ADAM_EPS = 1e-08
ADAM_WD = 0.01
ADAM_STEP = 10
VMEM_LIMIT = 56 << 20

HBM_SPEC = pl.BlockSpec(memory_space=pltpu.HBM)
VMEM_SPEC = pl.BlockSpec(memory_space=pltpu.VMEM)
SEM_SPEC = pl.BlockSpec(memory_space=pltpu.SEMAPHORE)
ANY_SPEC = pl.BlockSpec(memory_space=pl.ANY)
EFFECT = pltpu.SideEffectType.DATAFLOW_SIDE_EFFECTING


def _params(*sem):
    return pltpu.CompilerParams(dimension_semantics=sem, vmem_limit_bytes=VMEM_LIMIT)


def _tile(n, want):
    if n <= want:
        return n
    t = want
    while n % t:
        t -= 128
    return t


def _sigmoid(x):
    return 1.0 / (1.0 + jnp.exp(-x))


def _silu_and_grad(x):
    s = _sigmoid(x)
    return x * s, s * (1.0 + x * (1.0 - s))


def _gelu_and_grad(x):
    cdf = 0.5 * (1.0 + lax.erf(x * (1.0 / math.sqrt(2.0))))
    pdf = jnp.exp(-0.5 * x * x) * (1.0 / math.sqrt(2.0 * math.pi))
    return x * cdf, cdf + x * pdf


def _gelu(x):
    return x * (0.5 * (1.0 + lax.erf(x * (1.0 / math.sqrt(2.0)))))


def _rms(x):
    r = lax.rsqrt(jnp.mean(x * x, axis=-1, keepdims=True) + RMS_EPS)
    return x * r, r


def _rms_bwd(dxn, xn, r):
    return r * (dxn - xn * jnp.mean(dxn * xn, axis=-1, keepdims=True))


def _colsum(a):
    return jnp.sum(a, axis=0, keepdims=True)


def _shift_down(cur, before, k):
    rolled = pltpu.roll(cur, k, 0)
    row = lax.broadcasted_iota(jnp.int32, before.shape, 0)
    head = jnp.where(row < k, pltpu.roll(before, k, 0), rolled[:SUBLANE])
    return jnp.concatenate([head, rolled[SUBLANE:]], axis=0)


def _shift_up(cur, after, k):
    n = cur.shape[0]
    rolled = pltpu.roll(cur, n - k, 0)
    row = lax.broadcasted_iota(jnp.int32, after.shape, 0)
    tail = jnp.where(row >= SUBLANE - k, pltpu.roll(after, SUBLANE - k, 0), rolled[n - SUBLANE:])
    return jnp.concatenate([rolled[:n - SUBLANE], tail], axis=0)


def _after_spec():
    return pl.BlockSpec((SUBLANE, 128), lambda *_: (0, 0))


def _mm_proj(h, w, n_split, name, after):
    s, d = h.shape
    e = w.shape[1] // n_split
    tm, tn = _tile(s, 1024), _tile(e, 1024)
    nj = e // tn

    def body(h_ref, w_ref, after_ref, o_ref):
        o_ref[...] = jnp.dot(h_ref[...], w_ref[...], preferred_element_type=F32)

    return pl.pallas_call(
        body, name=name,
        out_shape=jax.ShapeDtypeStruct((n_split, s, e), F32),
        grid=(s // tm, n_split * nj),
        in_specs=[pl.BlockSpec((tm, d), lambda i, j: (i, 0)), pl.BlockSpec((d, tn), lambda i, j: (0, j)),
                  _after_spec()],
        out_specs=pl.BlockSpec((None, tm, tn), lambda i, j: (j // nj, i, j % nj)),
        compiler_params=_params("parallel", "parallel"),
    )(h, w, after)


def _mm_proj_slab(h, w, proj, q_idx, n_split, name, after):
    s, d = h.shape
    e = w.shape[1] // n_split
    tm, tn = _tile(s, 1024), _tile(e, 1024)
    nj = e // tn

    def body(q_ref, h_ref, w_ref, *rest):
        o_ref = rest[-1]
        o_ref[...] = jnp.dot(h_ref[...], w_ref[...], preferred_element_type=F32)

    in_specs = [pl.BlockSpec((tm, d), lambda i, j, qr: (i, 0)), pl.BlockSpec((d, tn), lambda i, j, qr: (0, qr[0] * nj + j)),
                _after_spec()]
    args = [q_idx, h, w, after]
    aliases = {}
    if proj is not None:
        in_specs.append(ANY_SPEC)
        args.append(proj)
        aliases = {4: 0}
    return pl.pallas_call(
        body, name=name,
        out_shape=jax.ShapeDtypeStruct((n_split, s, e), F32),
        grid_spec=pltpu.PrefetchScalarGridSpec(
            num_scalar_prefetch=1, grid=(s // tm, nj), in_specs=in_specs,
            out_specs=pl.BlockSpec((None, tm, tn), lambda i, j, qr: (qr[0], i, j))),
        input_output_aliases=aliases,
        compiler_params=_params("parallel", "parallel"),
    )(*args)


def _mm_dh(dp, w, name, after):
    nq, s, e = dp.shape
    d = w.shape[0]
    tm, tn, tk = _tile(s, 1024), _tile(d, 1024), _tile(e, 1024)
    nkq = e // tk
    nk = nq * nkq

    def body(a_ref, b_ref, after_ref, o_ref):
        @pl.when(pl.program_id(2) == 0)
        def _():
            o_ref[...] = jnp.zeros_like(o_ref)
        o_ref[...] += lax.dot_general(a_ref[...], b_ref[...], (((1,), (1,)), ((), ())), preferred_element_type=F32)

    return pl.pallas_call(
        body, name=name,
        out_shape=jax.ShapeDtypeStruct((s, d), F32),
        grid=(s // tm, d // tn, nk),
        in_specs=[pl.BlockSpec((None, tm, tk), lambda i, j, k: (k // nkq, i, k % nkq)),
                  pl.BlockSpec((tn, tk), lambda i, j, k: (j, k)), _after_spec()],
        out_specs=pl.BlockSpec((tm, tn), lambda i, j, k: (i, j)),
        compiler_params=_params("parallel", "parallel", "arbitrary"),
    )(dp, w, after)


def _dw_body(n_t):
    def body(a_ref, b_ref, o_ref, acc_ref):
        t = pl.program_id(2)

        @pl.when(t == 0)
        def _():
            acc_ref[...] = jnp.zeros_like(acc_ref)
        acc_ref[...] += lax.dot_general(a_ref[...], b_ref[...], (((0,), (0,)), ((), ())), preferred_element_type=F32)

        @pl.when(t == n_t - 1)
        def _():
            o_ref[...] = acc_ref[...].astype(o_ref.dtype)
    return body


def _mm_dw_in(h, dp, name):
    s, d = h.shape
    nq, _, e = dp.shape
    tm, tn, tt = _tile(d, 1024), _tile(e, 1024), _tile(s, 512)
    nj = e // tn

    return pl.pallas_call(
        _dw_body(s // tt), name=name,
        out_shape=jax.ShapeDtypeStruct((d, nq * e), BF16),
        grid=(d // tm, nq * nj, s // tt),
        in_specs=[pl.BlockSpec((tt, tm), lambda i, j, t: (t, i)),
                  pl.BlockSpec((None, tt, tn), lambda i, j, t: (j // nj, t, j % nj))],
        out_specs=pl.BlockSpec((tm, tn), lambda i, j, t: (i, j)),
        scratch_shapes=[pltpu.VMEM((tm, tn), F32)],
        compiler_params=_params("parallel", "parallel", "arbitrary"),
    )(h, dp)


def _mm_dw_out(y, dbr, name):
    s, e = y.shape
    d = dbr.shape[1]
    tm, tn, tt = _tile(e, 1024), _tile(d, 1024), _tile(s, 512)

    return pl.pallas_call(
        _dw_body(s // tt), name=name,
        out_shape=jax.ShapeDtypeStruct((e, d), BF16),
        grid=(e // tm, d // tn, s // tt),
        in_specs=[pl.BlockSpec((tt, tm), lambda i, j, t: (t, i)), pl.BlockSpec((tt, tn), lambda i, j, t: (t, j))],
        out_specs=pl.BlockSpec((tm, tn), lambda i, j, t: (i, j)),
        scratch_shapes=[pltpu.VMEM((tm, tn), F32)],
        compiler_params=_params("parallel", "parallel", "arbitrary"),
    )(y, dbr)


def _row_spec(ts, d):
    return pl.BlockSpec((ts, d), lambda i: (i, 0))


def _vec_spec(d):
    return pl.BlockSpec((1, d), lambda i: (0, 0))


def _norm_mod(x, g, scale, shift, name):
    s, d = x.shape
    ts = _tile(s, 512)

    def body(x_ref, g_ref, sc_ref, sh_ref, h_ref):
        xn, _ = _rms(x_ref[...])
        h_ref[...] = ((xn * g_ref[...]) * (1.0 + sc_ref[...]) + sh_ref[...]).astype(BF16)

    return pl.pallas_call(
        body, name=name, out_shape=jax.ShapeDtypeStruct((s, d), BF16), grid=(s // ts,),
        in_specs=[_row_spec(ts, d), _vec_spec(d), _vec_spec(d), _vec_spec(d)],
        out_specs=_row_spec(ts, d), compiler_params=_params("parallel"),
    )(x, g, scale, shift)


def _residual_norm_mod(x, br, gate, g, scale, shift, name):
    s, d = x.shape
    ts = _tile(s, 512)

    def body(x_ref, br_ref, gate_ref, g_ref, sc_ref, sh_ref, x1_ref, h_ref):
        x1 = x_ref[...] + gate_ref[...] * br_ref[...]
        x1_ref[...] = x1
        xn, _ = _rms(x1)
        h_ref[...] = ((xn * g_ref[...]) * (1.0 + sc_ref[...]) + sh_ref[...]).astype(BF16)

    return pl.pallas_call(
        body, name=name,
        out_shape=(jax.ShapeDtypeStruct((s, d), F32), jax.ShapeDtypeStruct((s, d), BF16)), grid=(s // ts,),
        in_specs=[_row_spec(ts, d), _row_spec(ts, d)] + [_vec_spec(d)] * 4,
        out_specs=(_row_spec(ts, d), _row_spec(ts, d)), compiler_params=_params("parallel"),
    )(x, br, gate, g, scale, shift)


def _head(x1, br, gate, gf, target, name):
    s, d = x1.shape
    ts = _tile(s, 512)

    def body(x1_ref, br_ref, gate_ref, gf_ref, tg_ref, dx_ref, dbr_ref, loss_ref, dgf_ref, dgate_ref):
        @pl.when(pl.program_id(0) == 0)
        def _():
            loss_ref[...] = jnp.zeros_like(loss_ref)
            dgf_ref[...] = jnp.zeros_like(dgf_ref)
            dgate_ref[...] = jnp.zeros_like(dgate_ref)
        br = br_ref[...]
        gate = gate_ref[...]
        gf = gf_ref[...]
        xn, r = _rms(x1_ref[...] + gate * br)
        err = xn * gf - tg_ref[...]
        loss_ref[...] += 0.5 * jnp.sum(jnp.mean(err * err, axis=-1, keepdims=True))
        dout = err * (1.0 / d)
        dgf_ref[...] += _colsum(dout * xn)
        dx = _rms_bwd(dout * gf, xn, r)
        dx_ref[...] = dx
        dgate_ref[...] += _colsum(dx * br)
        dbr_ref[...] = (gate * dx).astype(BF16)

    return pl.pallas_call(
        body, name=name,
        out_shape=(jax.ShapeDtypeStruct((s, d), F32), jax.ShapeDtypeStruct((s, d), BF16),
                   jax.ShapeDtypeStruct((SUBLANE, 128), F32), jax.ShapeDtypeStruct((1, d), F32),
                   jax.ShapeDtypeStruct((1, d), F32)),
        grid=(s // ts,),
        in_specs=[_row_spec(ts, d), _row_spec(ts, d), _vec_spec(d), _vec_spec(d), _row_spec(ts, d)],
        out_specs=(_row_spec(ts, d), _row_spec(ts, d), pl.BlockSpec((SUBLANE, 128), lambda i: (0, 0)),
                   _vec_spec(d), _vec_spec(d)),
        compiler_params=_params("arbitrary"),
    )(x1, br, gate, gf, target)


def _norm_mod_bwd(dh, x, dx_in, g, scale, name, br=None, gate=None):
    s, d = x.shape
    ts = _tile(s, 512)
    has_branch = br is not None

    def body(*refs):
        if has_branch:
            (dh_ref, x_ref, dxin_ref, g_ref, sc_ref, br_ref, gate_ref,
             dx_ref, dsh_ref, dsc_ref, dg_ref, dbr_ref, dgate_ref) = refs
        else:
            dh_ref, x_ref, dxin_ref, g_ref, sc_ref, dx_ref, dsh_ref, dsc_ref, dg_ref = refs

        @pl.when(pl.program_id(0) == 0)
        def _():
            dsh_ref[...] = jnp.zeros_like(dsh_ref)
            dsc_ref[...] = jnp.zeros_like(dsc_ref)
            dg_ref[...] = jnp.zeros_like(dg_ref)
            if has_branch:
                dgate_ref[...] = jnp.zeros_like(dgate_ref)
        dh = dh_ref[...]
        g = g_ref[...]
        xn, r = _rms(x_ref[...])
        dsh_ref[...] += _colsum(dh)
        dsc_ref[...] += _colsum(dh * (xn * g))
        da = dh * (1.0 + sc_ref[...])
        dg_ref[...] += _colsum(da * xn)
        dx = dxin_ref[...] + _rms_bwd(da * g, xn, r)
        dx_ref[...] = dx
        if has_branch:
            dgate_ref[...] += _colsum(dx * br_ref[...])
            dbr_ref[...] = (gate_ref[...] * dx).astype(BF16)

    vec_out = jax.ShapeDtypeStruct((1, d), F32)
    out_shape = [jax.ShapeDtypeStruct((s, d), F32), vec_out, vec_out, vec_out]
    out_specs = [_row_spec(ts, d), _vec_spec(d), _vec_spec(d), _vec_spec(d)]
    in_specs = [_row_spec(ts, d)] * 3 + [_vec_spec(d)] * 2
    args = [dh, x, dx_in, g, scale]
    if has_branch:
        out_shape += [jax.ShapeDtypeStruct((s, d), BF16), vec_out]
        out_specs += [_row_spec(ts, d), _vec_spec(d)]
        in_specs += [_row_spec(ts, d), _vec_spec(d)]
        args += [br, gate]
    return pl.pallas_call(
        body, name=name, out_shape=tuple(out_shape), grid=(s // ts,),
        in_specs=in_specs, out_specs=tuple(out_specs), compiler_params=_params("arbitrary"),
    )(*args)


def _conv_tiles(s, e):
    return _tile(s, 512), _tile(e, 512)


def _conv_fwd(proj, conv_w, conv_b, w_out, name, after):
    _, s, e = proj.shape
    d = w_out.shape[1]
    ts, te = _conv_tiles(s, e)
    hb = ts // SUBLANE

    def body(bg_ref, cg_ref, xi_ref, z_ref, cgp_ref, xip_ref, w0_ref, w1_ref, w2_ref, b_ref, wo_ref, after_ref,
             y_ref, br_ref):
        cx = cg_ref[...] * xi_ref[...]
        before = jnp.where(pl.program_id(0) > 0, cgp_ref[...] * xip_ref[...], 0.0)
        conv = b_ref[...] + w2_ref[...] * cx
        conv = conv + w0_ref[...] * _shift_down(cx, before, 2)
        conv = conv + w1_ref[...] * _shift_down(cx, before, 1)
        z = z_ref[...]
        y = ((z * _sigmoid(z)) * bg_ref[...] * conv).astype(BF16)
        y_ref[...] = y

        @pl.when(pl.program_id(1) == 0)
        def _():
            br_ref[...] = jnp.zeros_like(br_ref)
        br_ref[...] += jnp.dot(y, wo_ref[...], preferred_element_type=F32)

    def part(q):
        return pl.BlockSpec((None, ts, te), lambda i, j: (q, i, j))

    def halo_before(q):
        return pl.BlockSpec((None, SUBLANE, te), lambda i, j: (q, jnp.maximum(i * hb - 1, 0), j))

    return pl.pallas_call(
        body, name=name,
        out_shape=(jax.ShapeDtypeStruct((s, e), BF16), jax.ShapeDtypeStruct((s, d), F32)), grid=(s // ts, e // te),
        in_specs=[part(0), part(1), part(2), part(3), halo_before(1), halo_before(2)]
        + [pl.BlockSpec((1, te), lambda i, j: (0, j))] * 4 + [pl.BlockSpec((te, d), lambda i, j: (j, 0)), _after_spec()],
        out_specs=(pl.BlockSpec((ts, te), lambda i, j: (i, j)), pl.BlockSpec((ts, d), lambda i, j: (i, 0))),
        compiler_params=_params("parallel", "arbitrary"),
    )(proj, proj, proj, proj, proj, proj, *conv_w, conv_b, w_out, after)


def _conv_bwd(proj, dbr, w_out, conv_w, conv_b, name):
    _, s, e = proj.shape
    d = dbr.shape[1]
    ts, te = _conv_tiles(s, e)
    hb = ts // SUBLANE
    n_i = s // ts
    last_halo = s // SUBLANE - 1
    pack = 2 * SUBLANE
    last_pack = s // pack - 1
    nt = (((1,), (1,)), ((), ()))

    def body(bg_ref, cg_ref, xi_ref, z_ref, dbr_ref, cgp_ref, xip_ref, bgn_ref, zn_ref, dbrn_ref, wo_ref,
             w0_ref, w1_ref, w2_ref, b_ref, dp_ref, dw0_ref, dw1_ref, dw2_ref, db_ref):
        i = pl.program_id(1)

        @pl.when(i == 0)
        def _():
            for acc in (dw0_ref, dw1_ref, dw2_ref, db_ref):
                acc[...] = jnp.zeros_like(acc)
        wo = wo_ref[...]
        dy = lax.dot_general(dbr_ref[...], wo, nt, preferred_element_type=F32)
        dyn = lax.dot_general(dbrn_ref[...], wo, nt, preferred_element_type=F32)[:SUBLANE]
        bg, cg, xi, z = bg_ref[...], cg_ref[...], xi_ref[...], z_ref[...]
        w0, w1, w2 = w0_ref[...], w1_ref[...], w2_ref[...]
        cx = cg * xi
        before = jnp.where(i > 0, cgp_ref[...] * xip_ref[...], 0.0)
        cx1 = _shift_down(cx, before, 1)
        cx2 = _shift_down(cx, before, 2)
        conv = b_ref[...] + w2 * cx
        conv = conv + w0 * cx2
        conv = conv + w1 * cx1
        sz, dsz = _silu_and_grad(z)
        dp_ref[3] = (dy * bg * conv * dsz).astype(BF16)
        dp_ref[0] = (dy * sz * conv).astype(BF16)
        dconv = dy * sz * bg
        zn = zn_ref[...]
        after = jnp.where(i < n_i - 1, dyn * (zn * _sigmoid(zn)) * bgn_ref[...], 0.0)
        db_ref[...] += _colsum(dconv)
        dw2_ref[...] += _colsum(dconv * cx)
        dw1_ref[...] += _colsum(dconv * cx1)
        dw0_ref[...] += _colsum(dconv * cx2)
        dcx = w2 * dconv + w1 * _shift_up(dconv, after, 1) + w0 * _shift_up(dconv, after, 2)
        dp_ref[1] = (dcx * xi).astype(BF16)
        dp_ref[2] = (dcx * cg).astype(BF16)

    def part(q):
        return pl.BlockSpec((None, ts, te), lambda j, i: (q, i, j))

    def halo_before(q):
        return pl.BlockSpec((None, SUBLANE, te), lambda j, i: (q, jnp.maximum(i * hb - 1, 0), j))

    def halo_after(q):
        return pl.BlockSpec((None, SUBLANE, te), lambda j, i: (q, jnp.minimum((i + 1) * hb, last_halo), j))

    return pl.pallas_call(
        body, name=name,
        out_shape=(jax.ShapeDtypeStruct((4, s, e), BF16),) + (jax.ShapeDtypeStruct((1, e), F32),) * 4,
        grid=(e // te, n_i),
        in_specs=[part(0), part(1), part(2), part(3), pl.BlockSpec((ts, d), lambda j, i: (i, 0)),
                  halo_before(1), halo_before(2), halo_after(0), halo_after(3),
                  pl.BlockSpec((pack, d), lambda j, i: (jnp.minimum((i + 1) * (ts // pack), last_pack), 0)),
                  pl.BlockSpec((te, d), lambda j, i: (j, 0))]
        + [pl.BlockSpec((1, te), lambda j, i: (0, j))] * 4,
        out_specs=(pl.BlockSpec((4, ts, te), lambda j, i: (0, i, j)),) + (pl.BlockSpec((1, te), lambda j, i: (0, j)),) * 4,
        compiler_params=_params("parallel", "arbitrary"),
    )(proj, proj, proj, proj, dbr, proj, proj, proj, proj, dbr, w_out, *conv_w, conv_b)


def _tril(w):
    row = lax.broadcasted_iota(jnp.int32, w.shape, 0)
    col = lax.broadcasted_iota(jnp.int32, w.shape, 1)
    return jnp.where(row >= col, w, 0.0)


def _triu(w):
    row = lax.broadcasted_iota(jnp.int32, w.shape, 0)
    col = lax.broadcasted_iota(jnp.int32, w.shape, 1)
    return jnp.where(row <= col, w, 0.0)


def _layer_norm_fwd(v, g, b):
    mu = jnp.mean(v, axis=-1, keepdims=True)
    vc = v - mu
    rstd = lax.rsqrt(jnp.mean(vc * vc, axis=-1, keepdims=True) + LN_EPS)
    vhat = vc * rstd
    return vhat * g + b, vhat, rstd


def _gmlp_fwd(proj, ln_g, ln_b, w_s, b_s_t, w_out, name):
    _, s, e = proj.shape
    d = w_out.shape[1]
    gw = e // GROUPS

    def body(pu_ref, pv_ref, pz_ref, g_ref, b_ref, ws_ref, bs_ref, wo_ref, y_ref, br_ref, vn_scr, mix_scr):
        vn, _, _ = _layer_norm_fwd(_gelu(pv_ref[...]), g_ref[...], b_ref[...])
        vn_scr[...] = vn.astype(BF16)
        for gi in range(GROUPS):
            cols = slice(gi * gw, (gi + 1) * gw)
            wt = _tril(ws_ref[gi]).astype(BF16)
            mix_scr[:, cols] = jnp.dot(wt, vn_scr[:, cols], preferred_element_type=F32) + bs_ref[:, gi:gi + 1]
        z = pz_ref[...]
        y = ((z * _sigmoid(z)) * (_gelu(pu_ref[...]) * mix_scr[...])).astype(BF16)
        y_ref[...] = y
        br_ref[...] = jnp.dot(y, wo_ref[...], preferred_element_type=F32)

    def part(q):
        return pl.BlockSpec((None, CHUNK, e), lambda i: (q, i, 0))

    return pl.pallas_call(
        body, name=name,
        out_shape=(jax.ShapeDtypeStruct((s, e), BF16), jax.ShapeDtypeStruct((s, d), F32)), grid=(s // CHUNK,),
        in_specs=[part(0), part(1), part(2), _vec_spec(e), _vec_spec(e),
                  pl.BlockSpec((GROUPS, CHUNK, CHUNK), lambda i: (0, 0, 0)),
                  pl.BlockSpec((CHUNK, GROUPS), lambda i: (0, 0)), pl.BlockSpec((e, d), lambda i: (0, 0))],
        out_specs=(pl.BlockSpec((CHUNK, e), lambda i: (i, 0)), pl.BlockSpec((CHUNK, d), lambda i: (i, 0))),
        scratch_shapes=[pltpu.VMEM((CHUNK, e), BF16), pltpu.VMEM((CHUNK, e), F32)],
        compiler_params=_params("parallel"),
    )(proj, proj, proj, ln_g, ln_b, w_s, b_s_t, w_out)


def _gmlp_bwd(proj, dbr, w_out, ln_g, ln_b, w_s, w_s_t, b_s_t, name):
    _, s, e = proj.shape
    d = dbr.shape[1]
    gw = e // GROUPS
    n_i = s // CHUNK

    def body(pu_ref, pv_ref, pz_ref, dbr_ref, wo_ref, g_ref, b_ref, ws_ref, wst_ref, bs_ref,
             dp_ref, dws_ref, dbs_ref, dlg_ref, dlb_ref, vn_scr, mix_scr, dm_scr, dvn_scr, dmacc_scr):
        i = pl.program_id(0)

        @pl.when(i == 0)
        def _():
            dws_ref[...] = jnp.zeros_like(dws_ref)
            dlg_ref[...] = jnp.zeros_like(dlg_ref)
            dlb_ref[...] = jnp.zeros_like(dlb_ref)
            dmacc_scr[...] = jnp.zeros_like(dmacc_scr)
        ln_g = g_ref[...]
        u, du_dpu = _gelu_and_grad(pu_ref[...])
        v, dv_dpv = _gelu_and_grad(pv_ref[...])
        vn, vhat, rstd = _layer_norm_fwd(v, ln_g, b_ref[...])
        vn_scr[...] = vn.astype(BF16)
        for gi in range(GROUPS):
            cols = slice(gi * gw, (gi + 1) * gw)
            wt = _tril(ws_ref[gi]).astype(BF16)
            mix_scr[:, cols] = jnp.dot(wt, vn_scr[:, cols], preferred_element_type=F32) + bs_ref[:, gi:gi + 1]
        mixed = mix_scr[...]
        dy = lax.dot_general(dbr_ref[...], wo_ref[...], (((1,), (1,)), ((), ())), preferred_element_type=F32)
        sz, dsz = _silu_and_grad(pz_ref[...])
        ds = dy * sz
        dp_ref[2] = (dy * (u * mixed) * dsz).astype(BF16)
        dp_ref[0] = (ds * mixed * du_dpu).astype(BF16)
        dm = ds * u
        dmacc_scr[...] += dm
        dm_scr[...] = dm.astype(BF16)
        for gi in range(GROUPS):
            cols = slice(gi * gw, (gi + 1) * gw)
            dmg = dm_scr[:, cols]
            dws_ref[gi] += lax.dot_general(dmg, vn_scr[:, cols], (((1,), (1,)), ((), ())), preferred_element_type=F32)
            wtt = _triu(wst_ref[gi]).astype(BF16)
            dvn_scr[:, cols] = jnp.dot(wtt, dmg, preferred_element_type=F32)
        dvn = dvn_scr[...]
        dlg_ref[...] += _colsum(dvn * vhat)
        dlb_ref[...] += _colsum(dvn)
        dvh = dvn * ln_g
        dv = rstd * (dvh - jnp.mean(dvh, axis=-1, keepdims=True) - vhat * jnp.mean(dvh * vhat, axis=-1, keepdims=True))
        dp_ref[1] = (dv * dv_dpv).astype(BF16)

        @pl.when(i == n_i - 1)
        def _():
            for gi in range(GROUPS):
                dws_ref[gi] = _tril(dws_ref[gi])
                dbs_ref[:, gi:gi + 1] = jnp.sum(dmacc_scr[:, gi * gw:(gi + 1) * gw], axis=1, keepdims=True)

    def part(q):
        return pl.BlockSpec((None, CHUNK, e), lambda i: (q, i, 0))

    w_spec = pl.BlockSpec((GROUPS, CHUNK, CHUNK), lambda i: (0, 0, 0))
    bs_spec = pl.BlockSpec((CHUNK, GROUPS), lambda i: (0, 0))
    return pl.pallas_call(
        body, name=name,
        out_shape=(jax.ShapeDtypeStruct((3, s, e), BF16), jax.ShapeDtypeStruct((GROUPS, CHUNK, CHUNK), F32),
                   jax.ShapeDtypeStruct((CHUNK, GROUPS), F32), jax.ShapeDtypeStruct((1, e), F32),
                   jax.ShapeDtypeStruct((1, e), F32)),
        grid=(n_i,),
        in_specs=[part(0), part(1), part(2), pl.BlockSpec((CHUNK, d), lambda i: (i, 0)),
                  pl.BlockSpec((e, d), lambda i: (0, 0)), _vec_spec(e), _vec_spec(e), w_spec, w_spec, bs_spec],
        out_specs=(pl.BlockSpec((3, CHUNK, e), lambda i: (0, i, 0)), w_spec, bs_spec, _vec_spec(e), _vec_spec(e)),
        scratch_shapes=[pltpu.VMEM((CHUNK, e), BF16), pltpu.VMEM((CHUNK, e), F32), pltpu.VMEM((CHUNK, e), BF16),
                        pltpu.VMEM((CHUNK, e), F32), pltpu.VMEM((CHUNK, e), F32)],
        compiler_params=_params("arbitrary"),
    )(proj, proj, proj, dbr, w_out, ln_g, ln_b, w_s, w_s_t, b_s_t)


def _mod_fwd(c_all, mod_w, mod_b_cols, name):
    n_layer, d, w = mod_w.shape

    def body(c_ref, w_ref, b_ref, ca_ref, o_ref):
        c = c_ref[...]
        ca = c * _sigmoid(c)
        ca_ref[...] = ca
        for li in range(n_layer):
            o_ref[li * N_DEV:(li + 1) * N_DEV, :] = (
                jnp.dot(ca, w_ref[li], preferred_element_type=F32, precision=lax.Precision.HIGHEST) + b_ref[li])

    return pl.pallas_call(
        body, name=name,
        out_shape=(jax.ShapeDtypeStruct((N_DEV, d), F32), jax.ShapeDtypeStruct((n_layer * N_DEV, w), F32)),
        in_specs=[VMEM_SPEC] * 3, out_specs=(VMEM_SPEC, VMEM_SPEC),
        compiler_params=pltpu.CompilerParams(vmem_limit_bytes=VMEM_LIMIT),
    )(c_all, mod_w, mod_b_cols)


def _adamw(w, g, m, v):
    m = ADAM_B1 * m + (1.0 - ADAM_B1) * g
    v = ADAM_B2 * v + (1.0 - ADAM_B2) * (g * g)
    m_hat = m / (1.0 - ADAM_B1 ** ADAM_STEP)
    v_hat = v / (1.0 - ADAM_B2 ** ADAM_STEP)
    delta = -ADAM_LR * (m_hat / (jnp.sqrt(v_hat) + ADAM_EPS) + ADAM_WD * w)
    return delta, m, v


def _adamw_2d(w, g, m, v, name):
    r, c = w.shape
    tr, tc = _tile(r, 512), _tile(c, 1024)

    def body(w_ref, g_ref, m_ref, v_ref, d_ref, nm_ref, nv_ref):
        d_ref[...], nm_ref[...], nv_ref[...] = _adamw(w_ref[...], g_ref[...], m_ref[...], v_ref[...])

    spec = pl.BlockSpec((tr, tc), lambda i, j: (i, j))
    shape = jax.ShapeDtypeStruct((r, c), F32)
    return pl.pallas_call(
        body, name=name, out_shape=(shape, shape, shape), grid=(r // tr, c // tc),
        in_specs=[spec] * 4, out_specs=(spec, spec, spec), compiler_params=_params("parallel", "parallel"),
    )(w, g, m, v)


def _mod_w_update(ca_t, dmod_cols, w, m, v, name):
    n_layer, d, wd = w.shape
    tr = _tile(d, 256)

    def body(ca_ref, dm_ref, w_ref, m_ref, v_ref, g_ref, d_ref, nm_ref, nv_ref):
        ca = ca_ref[...]
        dm = dm_ref[...]
        g = ca[:, 0:1] * dm[0:1, :]
        for b in range(1, N_DEV):
            g = g + ca[:, b:b + 1] * dm[b:b + 1, :]
        g_ref[...] = g
        d_ref[...], nm_ref[...], nv_ref[...] = _adamw(w_ref[...], g, m_ref[...], v_ref[...])

    spec = pl.BlockSpec((None, tr, wd), lambda l, i: (l, i, 0))
    shape = jax.ShapeDtypeStruct((n_layer, d, wd), F32)
    return pl.pallas_call(
        body, name=name, out_shape=(shape,) * 4, grid=(n_layer, d // tr),
        in_specs=[pl.BlockSpec((tr, N_DEV), lambda l, i: (i, 0)), pl.BlockSpec((None, N_DEV, wd), lambda l, i: (l, 0, 0)),
                  spec, spec, spec],
        out_specs=(spec,) * 4, compiler_params=_params("parallel", "parallel"),
    )(ca_t, dmod_cols, w, m, v)


def _adamw_small(ws, gs, ms, vs, name):
    n = len(ws)

    def body(*refs):
        ins, outs = refs[:4 * n], refs[4 * n:]
        for k in range(n):
            delta, nm, nv = _adamw(ins[k][...], ins[n + k][...], ins[2 * n + k][...], ins[3 * n + k][...])
            outs[3 * k][...] = delta
            outs[3 * k + 1][...] = nm
            outs[3 * k + 2][...] = nv

    out_shape = []
    for w in ws:
        out_shape += [jax.ShapeDtypeStruct(w.shape, F32)] * 3
    outs = pl.pallas_call(
        body, name=name, out_shape=tuple(out_shape),
        in_specs=[VMEM_SPEC] * (4 * n), out_specs=tuple([VMEM_SPEC] * (3 * n)),
        compiler_params=pltpu.CompilerParams(vmem_limit_bytes=VMEM_LIMIT),
    )(*ws, *gs, *ms, *vs)
    return [tuple(outs[3 * k:3 * k + 3]) for k in range(n)]


def _place():
    return lax.axis_index("x"), lax.axis_index("y"), lax.axis_index("c")


def _other_chips(x, y):
    return [(1 - x, y), (x, 1 - y), (1 - x, 1 - y)]


def _all_gather(block, name):
    m_per, n = block.shape

    def body(x_ref, out_ref, send_sems, recv_sems, local_sem):
        x, y, c = _place()
        me, sibling = (x, y, c), (x, y, 1 - c)
        chips = _other_chips(x, y)

        def rows(px, py, pc):
            return out_ref.at[pl.ds((4 * px + 2 * py + pc) * m_per, m_per), :]

        def copy(k, blk, to, src=None):
            return pltpu.make_async_remote_copy(
                src_ref=rows(*blk) if src is None else src, dst_ref=rows(*blk),
                send_sem=send_sems.at[k], recv_sem=recv_sems.at[k], device_id=to, device_id_type=MESH)

        mine = pltpu.make_async_copy(x_ref, rows(*me), local_sem)
        mine.start()
        first = [copy(0, me, sibling, src=x_ref)]
        first += [copy(1 + j, me, (*chip, c), src=x_ref) for j, chip in enumerate(chips)]
        for cp in first:
            cp.start()
        passed = [copy(4 + j, (*chip, c), sibling) for j, chip in enumerate(chips)]
        for j, chip in enumerate(chips):
            copy(1 + j, (*chip, c), me).wait_recv()
            passed[j].start()
        copy(0, sibling, me).wait_recv()
        for j, chip in enumerate(chips):
            copy(4 + j, (*chip, 1 - c), me).wait_recv()
        for cp in first + passed:
            cp.wait_send()
        mine.wait()

    return pl.pallas_call(
        body, name=name, out_shape=jax.ShapeDtypeStruct((N_DEV * m_per, n), F32),
        in_specs=[VMEM_SPEC], out_specs=VMEM_SPEC,
        scratch_shapes=[pltpu.SemaphoreType.DMA((7,)), pltpu.SemaphoreType.DMA((7,)), pltpu.SemaphoreType.DMA],
        compiler_params=pltpu.CompilerParams(vmem_limit_bytes=VMEM_LIMIT),
    )(block)


def _hbm(a):
    return pltpu.with_memory_space_constraint(a, pltpu.HBM)


def _place_own(shard, kind, chip_idx, name):
    r, cdim = shard.shape
    tr, tc = _tile(r, 512), _tile(cdim, 1024)
    nrb, ncb = r // tr, cdim // tc

    def body(k_ref, s_ref, o_ref):
        o_ref[...] = s_ref[...].astype(BF16)

    if kind == "col":
        full, o_map = (r, N_CHIP * cdim), lambda i, j, kr: (i, kr[0] * ncb + j)
    else:
        full, o_map = (N_CHIP * r, cdim), lambda i, j, kr: (kr[0] * nrb + i, j)
    return pl.pallas_call(
        body, name=name, out_shape=jax.ShapeDtypeStruct(full, BF16),
        grid_spec=pltpu.PrefetchScalarGridSpec(
            num_scalar_prefetch=1, grid=(nrb, ncb),
            in_specs=[pl.BlockSpec((tr, tc), lambda i, j, kr: (i, j))],
            out_specs=pl.BlockSpec((tr, tc), o_map)),
        compiler_params=_params("parallel", "parallel"),
    )(chip_idx, shard)


def _weight_window(ref, kind, shard_shape, k, half):
    r, cdim = shard_shape
    hr = r // 2
    if kind == "col":
        return ref.at[pl.ds(half * hr, hr), pl.ds(pl.multiple_of(k * cdim, 128), cdim)]
    return ref.at[pl.ds(pl.multiple_of(k * r + half * hr, 2 * SUBLANE), hr), :]


def _gather_weights_start(fulls, kinds, shard_shapes, groups, after, name):
    n, ng = len(fulls), len(groups)

    def body(*refs):
        ins = refs[:n]
        sems = refs[n + 1:n + 1 + 2 * ng]
        token = refs[2 * n + 1 + 2 * ng]
        x, y, c = _place()
        chips = _other_chips(x, y)
        for g, group in enumerate(groups):
            for pos, (w, j) in enumerate(group):
                own = _weight_window(ins[w], kinds[w], shard_shapes[w], 2 * x + y, c)
                for cc in range(2):
                    pltpu.make_async_remote_copy(
                        src_ref=own, dst_ref=own,
                        send_sem=sems[2 * g].at[2 * pos + cc], recv_sem=sems[2 * g + 1].at[2 * pos + c],
                        device_id=(*chips[j], cc), device_id_type=MESH).start()
        token[...] = jnp.zeros_like(token)

    sem_shapes = []
    for group in groups:
        sem_shapes += [pltpu.SemaphoreType.DMA((2 * len(group),))] * 2
    outs = pl.pallas_call(
        body, name=name,
        out_shape=tuple(sem_shapes) + tuple(pltpu.HBM(f.shape, f.dtype) for f in fulls)
        + (jax.ShapeDtypeStruct((SUBLANE, 128), F32),),
        in_specs=[HBM_SPEC] * n + [ANY_SPEC], out_specs=(SEM_SPEC,) * (2 * ng) + (HBM_SPEC,) * n + (VMEM_SPEC,),
        input_output_aliases={w: 2 * ng + w for w in range(n)},
        compiler_params=pltpu.CompilerParams(has_side_effects=EFFECT),
    )(*[_hbm(f) for f in fulls], after)
    sems = [(outs[2 * g], outs[2 * g + 1]) for g in range(ng)]
    return sems, list(outs[2 * ng:2 * ng + n]), outs[2 * ng + n]


def _gather_weights_wait(sems, group, fulls, kinds, shard_shapes, after, name):
    n = len(fulls)

    def body(*refs):
        ins = refs[:n]
        send_sems, recv_sems = refs[n], refs[n + 1]
        x, y, c = _place()
        chips = _other_chips(x, y)
        for pos, (w, j) in enumerate(group):
            own = _weight_window(ins[w], kinds[w], shard_shapes[w], 2 * x + y, c)
            for cc in range(2):
                landed = _weight_window(ins[w], kinds[w], shard_shapes[w], 2 * chips[j][0] + chips[j][1], cc)
                k = 2 * pos + cc
                pltpu.make_async_remote_copy(
                    src_ref=own, dst_ref=own, send_sem=send_sems.at[k], recv_sem=recv_sems.at[k],
                    device_id=(*chips[j], cc), device_id_type=MESH).wait_send()
                pltpu.make_async_remote_copy(
                    src_ref=landed, dst_ref=landed, send_sem=send_sems.at[k], recv_sem=recv_sems.at[k],
                    device_id=(*chips[j], cc), device_id_type=MESH).wait_recv()

    return list(pl.pallas_call(
        body, name=name, out_shape=tuple(pltpu.HBM(f.shape, f.dtype) for f in fulls),
        in_specs=[HBM_SPEC] * n + [SEM_SPEC, SEM_SPEC, ANY_SPEC], out_specs=(HBM_SPEC,) * n,
        input_output_aliases={w: w for w in range(n)},
        compiler_params=pltpu.CompilerParams(has_side_effects=EFFECT),
    )(*fulls, sems[0], sems[1], after))


def _grad_piece(ref, kind, h, cdim, k, half):
    if kind == "col":
        return ref.at[pl.ds(half * h, h), pl.ds(pl.multiple_of(k * cdim, 128), cdim)]
    return ref.at[k, pl.ds(half * h, h), :]


def _grad_dims(g, kind):
    return (g.shape[0] // 2, g.shape[1] // N_CHIP) if kind == "col" else (g.shape[1] // 2, g.shape[2])


def _reduce_start(grads, kinds, name):
    n = len(grads)
    dims = [_grad_dims(g, kind) for g, kind in zip(grads, kinds)]
    lands = [lax.empty((N_DEV - 1, h, cdim), g.dtype) for g, (h, cdim) in zip(grads, dims)]

    def body(*refs):
        g_ins, land_ins = refs[:n], refs[n:2 * n]
        send_sems, recv_sems = refs[2 * n], refs[2 * n + 1]
        token = refs[4 * n + 2]
        x, y, c = _place()
        for w in range(n):
            h, cdim = dims[w]
            base = (N_DEV - 1) * w
            pltpu.make_async_remote_copy(
                src_ref=_grad_piece(g_ins[w], kinds[w], h, cdim, 2 * x + y, 1 - c), dst_ref=land_ins[w].at[0],
                send_sem=send_sems.at[base], recv_sem=recv_sems.at[base],
                device_id=(x, y, 1 - c), device_id_type=MESH).start()
            for j, chip in enumerate(_other_chips(x, y)):
                for cc in range(2):
                    pltpu.make_async_remote_copy(
                        src_ref=_grad_piece(g_ins[w], kinds[w], h, cdim, 2 * chip[0] + chip[1], cc),
                        dst_ref=land_ins[w].at[1 + 2 * j + c],
                        send_sem=send_sems.at[base + 1 + 2 * j + cc], recv_sem=recv_sems.at[base + 1 + 2 * j + c],
                        device_id=(*chip, cc), device_id_type=MESH).start()
        token[...] = jnp.zeros_like(token)

    sems = pltpu.SemaphoreType.DMA(((N_DEV - 1) * n,))
    outs = pl.pallas_call(
        body, name=name,
        out_shape=(sems, sems) + tuple(pltpu.HBM(a.shape, a.dtype) for a in list(grads) + lands)
        + (jax.ShapeDtypeStruct((SUBLANE, 128), F32),),
        in_specs=[HBM_SPEC] * (2 * n), out_specs=(SEM_SPEC, SEM_SPEC) + (HBM_SPEC,) * (2 * n) + (VMEM_SPEC,),
        input_output_aliases={i: 2 + i for i in range(2 * n)},
        compiler_params=pltpu.CompilerParams(has_side_effects=EFFECT),
    )(*[_hbm(a) for a in list(grads) + lands])
    return outs[0], outs[1], list(outs[2:2 + n]), list(outs[2 + n:2 + 2 * n]), outs[2 + 2 * n]


def _reduce_wait(send_sems, recv_sems, grads, lands, kinds, after, name):
    n = len(grads)
    dims = [_grad_dims(g, kind) for g, kind in zip(grads, kinds)]

    def body(*refs):
        g_ins, land_ins = refs[:n], refs[n:2 * n]
        send_sems, recv_sems = refs[2 * n], refs[2 * n + 1]
        x, y, c = _place()
        for w in range(n):
            h, cdim = dims[w]
            piece = _grad_piece(g_ins[w], kinds[w], h, cdim, 2 * x + y, c)
            for s in range(N_DEV - 1):
                k = (N_DEV - 1) * w + s
                slot = land_ins[w].at[s]
                pltpu.make_async_remote_copy(
                    src_ref=piece, dst_ref=slot, send_sem=send_sems.at[k], recv_sem=recv_sems.at[k],
                    device_id=(x, y, 1 - c), device_id_type=MESH).wait_send()
                pltpu.make_async_remote_copy(
                    src_ref=piece, dst_ref=slot, send_sem=send_sems.at[k], recv_sem=recv_sems.at[k],
                    device_id=(x, y, 1 - c), device_id_type=MESH).wait_recv()

    outs = pl.pallas_call(
        body, name=name, out_shape=tuple(pltpu.HBM(a.shape, a.dtype) for a in list(grads) + list(lands)),
        in_specs=[HBM_SPEC] * (2 * n) + [SEM_SPEC, SEM_SPEC, ANY_SPEC], out_specs=(HBM_SPEC,) * (2 * n),
        input_output_aliases={i: i for i in range(2 * n)},
        compiler_params=pltpu.CompilerParams(has_side_effects=EFFECT),
    )(*grads, *lands, send_sems, recv_sems, after)
    return list(outs[:n]), list(outs[n:])


def _add_pieces(g, land, kind, chip_idx, core_idx, name):
    _, h, cdim = land.shape
    tr, tc = _tile(h, 256), _tile(cdim, 2048)
    nrb, ncb = h // tr, cdim // tc

    def body(k_ref, c_ref, g_ref, l_ref, o_ref):
        acc = g_ref[...].astype(F32)
        for s in range(N_DEV - 1):
            acc = acc + l_ref[s].astype(F32)
        o_ref[...] = acc

    if kind == "col":
        g_spec = pl.BlockSpec((tr, tc), lambda i, j, kr, cr: (cr[0] * nrb + i, kr[0] * ncb + j))
    else:
        g_spec = pl.BlockSpec((None, tr, tc), lambda i, j, kr, cr: (kr[0], cr[0] * nrb + i, j))
    return pl.pallas_call(
        body, name=name, out_shape=jax.ShapeDtypeStruct((2 * h, cdim), F32),
        grid_spec=pltpu.PrefetchScalarGridSpec(
            num_scalar_prefetch=2, grid=(nrb, ncb),
            in_specs=[g_spec, pl.BlockSpec((N_DEV - 1, tr, tc), lambda i, j, kr, cr: (0, i, j))],
            out_specs=pl.BlockSpec((tr, tc), lambda i, j, kr, cr: (cr[0] * nrb + i, j))),
        compiler_params=_params("parallel", "parallel"),
    )(chip_idx, core_idx, g, land)


def _join_halves(shards, name):
    n = len(shards)

    def body(*refs):
        ins, outs = refs[:n], refs[n:2 * n]
        send_sems, recv_sems = refs[2 * n:]
        x, y, c = _place()
        copies = []
        for w in range(n):
            h = shards[w].shape[0] // 2
            cp = pltpu.make_async_remote_copy(
                src_ref=ins[w].at[pl.ds(c * h, h), :], dst_ref=outs[w].at[pl.ds(c * h, h), :],
                send_sem=send_sems.at[w], recv_sem=recv_sems.at[w], device_id=(x, y, 1 - c), device_id_type=MESH)
            cp.start()
            copies.append(cp)
        for w in range(n):
            h = shards[w].shape[0] // 2
            theirs = outs[w].at[pl.ds((1 - c) * h, h), :]
            pltpu.make_async_remote_copy(
                src_ref=theirs, dst_ref=theirs, send_sem=send_sems.at[w], recv_sem=recv_sems.at[w],
                device_id=(x, y, 1 - c), device_id_type=MESH).wait_recv()
        for cp in copies:
            cp.wait_send()

    return pl.pallas_call(
        body, name=name, out_shape=tuple(jax.ShapeDtypeStruct(s.shape, s.dtype) for s in shards),
        in_specs=[HBM_SPEC] * n, out_specs=tuple([HBM_SPEC] * n),
        input_output_aliases={w: w for w in range(n)},
        scratch_shapes=[pltpu.SemaphoreType.DMA((n,))] * 2,
    )(*shards)


def _sum_devices(gathered, rows, name):
    n = gathered.shape[1]

    def body(g_ref, o_ref):
        acc = g_ref[0:rows, :]
        for dev in range(1, N_DEV):
            acc = acc + g_ref[dev * rows:(dev + 1) * rows, :]
        o_ref[...] = acc

    return pl.pallas_call(
        body, name=name, out_shape=jax.ShapeDtypeStruct((rows, n), F32),
        in_specs=[VMEM_SPEC], out_specs=VMEM_SPEC,
        compiler_params=pltpu.CompilerParams(vmem_limit_bytes=VMEM_LIMIT),
    )(gathered)


def _pack(vectors, width):
    flat = [v.reshape(-1) for v in vectors]
    offsets, total = [], 0
    for f in flat:
        offsets.append(total)
        total += f.shape[0]
    rows = -(-total // (width * SUBLANE)) * SUBLANE
    flat.append(jnp.zeros((rows * width - total,), F32))
    return jnp.concatenate(flat).reshape(rows, width), offsets


def kernel(x, c, mod_w, mod_b, norm_g, a_w_in, a_conv_w, a_conv_b, a_w_out, b_w_in, b_ln_g, b_ln_b, b_w_s, b_b_s, b_w_out, final_g, loss_target, m_mod_w, m_mod_b, m_norm_g, m_a_w_in, m_a_conv_w, m_a_conv_b, m_a_w_out, m_b_w_in, m_b_ln_g, m_b_ln_b, m_b_w_s, m_b_b_s, m_b_w_out, m_final_g, v_mod_w, v_mod_b, v_norm_g, v_a_w_in, v_a_conv_w, v_a_conv_b, v_a_w_out, v_b_w_in, v_b_ln_g, v_b_ln_b, v_b_w_s, v_b_b_s, v_b_w_out, v_final_g):
    seq, d = x.shape[1], x.shape[2]
    e = a_conv_b.shape[1]
    wd = mod_w.shape[2]
    ax, ay, ac = _place()
    chip = 2 * ax + ay
    dev = 2 * chip + ac
    chip_idx = jnp.reshape(chip, (1,)).astype(jnp.int32)
    core_idx = jnp.reshape(ac, (1,)).astype(jnp.int32)

    x2d = x[0]
    target = loss_target[0]

    w_names = ["a_w_in", "a_w_out", "b_w_in", "b_w_out"]
    w_kinds = ["col", "row", "col", "row"]
    w_shards = [a_w_in[0], a_w_out[0], b_w_in[0], b_w_out[0]]
    w_shapes = [sh.shape for sh in w_shards]
    placed = [_place_own(sh, kind, chip_idx, "place_" + nm) for nm, sh, kind in zip(w_names, w_shards, w_kinds)]
    a_groups, far_groups = [[(0, 0), (0, 1)]], [[(0, 2)]]
    whole_group = [[(0, 0), (0, 1), (0, 2)]]

    es = e // N_CHIP
    packed0, offs0 = _pack([c, a_conv_w, b_ln_g, b_ln_b], 1024)
    gathered0 = _all_gather(packed0, "gather_params").reshape(N_DEV, -1)
    c_all = gathered0[:, :d]
    per_chip = gathered0[0::2]

    def from_chips_cols(k, rows_):
        got = per_chip[:, offs0[k]:offs0[k] + rows_ * es].reshape(N_CHIP, rows_, es)
        return jnp.transpose(got, (1, 0, 2)).reshape(rows_, e)

    conv_w_full = from_chips_cols(1, 3)
    conv_w = [conv_w_full[k:k + 1] for k in range(3)]
    ln_g, ln_b = from_chips_cols(2, 1), from_chips_cols(3, 1)
    mod_b_cols = lax.dynamic_slice_in_dim(mod_b, chip * wd, wd, axis=1)[:, None, :]
    c_act, mod_part = _mod_fwd(c_all, mod_w, mod_b_cols, "mod_fwd")
    n_layer = mod_w.shape[0]
    mod_gathered = _all_gather(mod_part, "gather_mod")
    a_sems, (wa_in,), a_token = _gather_weights_start(
        placed[:1], w_kinds[:1], w_shapes[:1], a_groups, mod_gathered, "gather_a_w_in_start")
    mod_all = mod_gathered.reshape(N_CHIP, 2, n_layer, N_DEV, wd)[:, 0]
    mod_all = jnp.transpose(mod_all, (1, 2, 0, 3)).reshape(n_layer, N_DEV, N_CHIP * wd)
    mod_me = lax.dynamic_index_in_dim(mod_all, dev, axis=1, keepdims=False)
    shift = [mod_me[l:l + 1, 0:d] for l in range(n_layer)]
    scale = [mod_me[l:l + 1, d:2 * d] for l in range(n_layer)]
    gate = [mod_me[l:l + 1, 2 * d:3 * d] for l in range(n_layer)]

    g0, g1, gf = norm_g[0:1], norm_g[1:2], final_g[None, :]
    h0 = _norm_mod(x2d, g0, scale[0], shift[0], "norm_mod0")

    def slab(r):
        return jnp.bitwise_xor(chip_idx, r)

    def arrived(sems, group, weight, w, after, name):
        return _gather_weights_wait(sems, [(0, j) for _, j in group], [weight], [w_kinds[w]], [w_shapes[w]],
                                    after, name)[0]

    proj0 = _mm_proj_slab(h0, wa_in, None, slab(0), 4, "a_proj_own", a_token)
    wa_in = arrived(a_sems[0], a_groups[0], wa_in, 0, proj0, "gather_wait_near")
    far_sems, (wa_in,), far_token = _gather_weights_start(
        [wa_in], w_kinds[:1], w_shapes[:1], far_groups, wa_in, "gather_a_w_in_far_start")
    proj0 = _mm_proj_slab(h0, wa_in, proj0, slab(2), 4, "a_proj_x", far_token)
    proj0 = _mm_proj_slab(h0, wa_in, proj0, slab(1), 4, "a_proj_y", far_token)
    wa_in = arrived(far_sems[0], far_groups[0], wa_in, 0, proj0, "gather_wait_far")
    def start_whole(w, after, name):
        sems, (flight,), token = _gather_weights_start(
            placed[w:w + 1], w_kinds[w:w + 1], w_shapes[w:w + 1], whole_group, after, name)
        return sems[0], flight, token

    ao_sems, wa_out, ao_token = start_whole(1, wa_in, "gather_a_w_out_start")
    proj0 = _mm_proj_slab(h0, wa_in, proj0, slab(3), 4, "a_proj_far", ao_token)
    wa_out = arrived(ao_sems, whole_group[0], wa_out, 1, proj0, "gather_wait_a_w_out")
    bi_sems, wb_in, bi_token = start_whole(2, wa_out, "gather_b_w_in_start")
    y0, br0 = _conv_fwd(proj0, conv_w, a_conv_b, wa_out, "conv_fwd", bi_token)
    wb_in = arrived(bi_sems, whole_group[0], wb_in, 2, br0, "gather_wait_b_w_in")
    bo_sems, wb_out, bo_token = start_whole(3, wb_in, "gather_b_w_out_start")
    x1, h1 = _residual_norm_mod(x2d, br0, gate[0], g1, scale[1], shift[1], "residual_norm_mod1")
    proj1 = _mm_proj(h1, wb_in, 3, "b_proj", bo_token)
    b_s_t = jnp.transpose(b_b_s[0])
    wb_out = arrived(bo_sems, whole_group[0], wb_out, 3, proj1, "gather_wait_b_w_out")
    y1, br1 = _gmlp_fwd(proj1, ln_g, ln_b, b_w_s[0], b_s_t, wb_out, "gmlp_fwd")
    dx2, dbr1, loss_part, g_final_g, dgate1 = _head(x1, br1, gate[1], gf, target, "head")

    gw_b_out = _mm_dw_out(y1, dbr1, "b_out_dw")
    dproj1, g_w_s, g_b_s_t, g_ln_g, g_ln_b = _gmlp_bwd(
        proj1, dbr1, wb_out, ln_g, ln_b, b_w_s[0], jnp.swapaxes(b_w_s[0], 1, 2), b_s_t, "gmlp_bwd")
    gw_b_in = _mm_dw_in(h1, dproj1, "b_proj_dw")
    b_kinds = ["col", "row"]
    b_send, b_recv, b_grads, b_lands, b_token = _reduce_start(
        [gw_b_in, gw_b_out.reshape(N_CHIP, e // N_CHIP, d)], b_kinds, "reduce_b_start")
    dh1 = _mm_dh(dproj1, wb_in, "b_proj_dx", b_token)
    dx1, dshift1, dscale1, g_g1, dbr0, dgate0 = _norm_mod_bwd(
        dh1, x1, dx2, g1, scale[1], "norm_mod1_bwd", br=br0, gate=gate[0])

    gw_a_out = _mm_dw_out(y0, dbr0, "a_out_dw")
    dproj0, g_w0, g_w1, g_w2, g_conv_b = _conv_bwd(proj0, dbr0, wa_out, conv_w, a_conv_b, "conv_bwd")
    gw_a_in = _mm_dw_in(h0, dproj0, "a_proj_dw")
    a_kinds = ["col", "row"]
    a_send, a_recv, a_grads, a_lands, a_token = _reduce_start(
        [gw_a_in, gw_a_out.reshape(N_CHIP, e // N_CHIP, d)], a_kinds, "reduce_a_start")
    dh0 = _mm_dh(dproj0, wa_in, "a_proj_dx", a_token)
    grad_x, dshift0, dscale0, g_g0 = _norm_mod_bwd(dh0, x2d, dx1, g0, scale[0], "norm_mod0_bwd")

    dmod = jnp.concatenate([dshift0, dscale0, dgate0, dshift1, dscale1, dgate1], axis=1)
    small = [loss_part[0, 0:1], g_final_g, g_g0, g_g1, jnp.concatenate([g_w0, g_w1, g_w2], axis=0), g_conv_b, g_ln_g, g_ln_b,
             jnp.transpose(g_b_s_t), g_w_s, dmod]
    packed, offs = _pack(small, 1024)
    rows = packed.shape[0]
    gathered = _all_gather(packed, "gather_small")
    summed = _sum_devices(gathered, rows, "sum_small").reshape(-1)

    def take(k, shape):
        size = math.prod(shape)
        return summed[offs[k]:offs[k] + size].reshape(shape)

    loss = take(0, ())
    grad_final_g = take(1, (d,))
    grad_norm_g = jnp.concatenate([take(2, (1, d)), take(3, (1, d))], axis=0)
    grad_conv_w_full = take(4, (3, e))
    grad_a_conv_b = take(5, (1, e))
    grad_ln_g_full = take(6, (1, e))
    grad_ln_b_full = take(7, (1, e))
    grad_b_b_s = take(8, (1, GROUPS, CHUNK))
    grad_b_w_s = take(9, (1, GROUPS, CHUNK, CHUNK))
    grad_mod_b = take(10, (n_layer, 3 * d))
    grad_a_conv_w = lax.dynamic_slice_in_dim(grad_conv_w_full, chip * es, es, axis=1)[None]
    grad_b_ln_g = lax.dynamic_slice_in_dim(grad_ln_g_full, chip * es, es, axis=1)
    grad_b_ln_b = lax.dynamic_slice_in_dim(grad_ln_b_full, chip * es, es, axis=1)
    dmod_all = gathered.reshape(N_DEV, rows * 1024)[:, offs[10]:offs[10] + n_layer * 3 * d]
    dmod_all = dmod_all.reshape(N_DEV, n_layer, 3 * d)
    dmod_cols = jnp.transpose(lax.dynamic_slice_in_dim(dmod_all, chip * wd, wd, axis=2), (1, 0, 2))

    b_grads, b_lands = _reduce_wait(b_send, b_recv, b_grads, b_lands, b_kinds, summed, "reduce_b_wait")
    a_grads, a_lands = _reduce_wait(a_send, a_recv, a_grads, a_lands, a_kinds, summed, "reduce_a_wait")
    summed_halves = [
        _add_pieces(g, land, kind, chip_idx, core_idx, "add_pieces_" + nm)
        for g, land, kind, nm in [(a_grads[0], a_lands[0], "col", "a_w_in"), (b_grads[0], b_lands[0], "col", "b_w_in"),
                                  (a_grads[1], a_lands[1], "row", "a_w_out"), (b_grads[1], b_lands[1], "row", "b_w_out")]]
    g_a_w_in, g_b_w_in, g_a_w_out, g_b_w_out = _join_halves(summed_halves, "join_halves")

    grad_mod_w, delta_mod_w, new_m_mod_w, new_v_mod_w = _mod_w_update(
        jnp.transpose(c_act), dmod_cols, mod_w, m_mod_w, v_mod_w, "mod_w_update")
    upd = {}
    for nm, w, g, m, v in [("a_w_in", a_w_in, g_a_w_in, m_a_w_in, v_a_w_in), ("a_w_out", a_w_out, g_a_w_out, m_a_w_out, v_a_w_out),
                           ("b_w_in", b_w_in, g_b_w_in, m_b_w_in, v_b_w_in), ("b_w_out", b_w_out, g_b_w_out, m_b_w_out, v_b_w_out)]:
        upd[nm] = tuple(o[None] for o in _adamw_2d(w[0], g, m[0], v[0], "adamw_" + nm))
    small_w = [("mod_b", mod_b, grad_mod_b, m_mod_b, v_mod_b), ("norm_g", norm_g, grad_norm_g, m_norm_g, v_norm_g),
               ("a_conv_w", a_conv_w, grad_a_conv_w, m_a_conv_w, v_a_conv_w),
               ("a_conv_b", a_conv_b, grad_a_conv_b, m_a_conv_b, v_a_conv_b),
               ("b_ln_g", b_ln_g, grad_b_ln_g, m_b_ln_g, v_b_ln_g), ("b_ln_b", b_ln_b, grad_b_ln_b, m_b_ln_b, v_b_ln_b),
               ("b_w_s", b_w_s, grad_b_w_s, m_b_w_s, v_b_w_s), ("b_b_s", b_b_s, grad_b_b_s, m_b_b_s, v_b_b_s),
               ("final_g", final_g, grad_final_g, m_final_g, v_final_g)]

    def flat2d(a):
        return a.reshape(-1, a.shape[-1])

    res = _adamw_small([flat2d(t[1]) for t in small_w], [flat2d(t[2]) for t in small_w],
                       [flat2d(t[3]) for t in small_w], [flat2d(t[4]) for t in small_w], "adamw_small")
    for (nm, w, _, _, _), r3 in zip(small_w, res):
        upd[nm] = tuple(o.reshape(w.shape) for o in r3)
    upd["mod_w"] = (delta_mod_w, new_m_mod_w, new_v_mod_w)

    grads = {"mod_w": grad_mod_w, "mod_b": grad_mod_b, "norm_g": grad_norm_g, "a_w_in": g_a_w_in[None],
             "a_conv_w": grad_a_conv_w, "a_conv_b": grad_a_conv_b, "a_w_out": g_a_w_out[None], "b_w_in": g_b_w_in[None],
             "b_ln_g": grad_b_ln_g, "b_ln_b": grad_b_ln_b, "b_w_s": grad_b_w_s, "b_b_s": grad_b_b_s,
             "b_w_out": g_b_w_out[None], "final_g": grad_final_g}
    order = ["mod_w", "mod_b", "norm_g", "a_w_in", "a_conv_w", "a_conv_b", "a_w_out", "b_w_in", "b_ln_g", "b_ln_b",
             "b_w_s", "b_b_s", "b_w_out", "final_g"]
    return (loss, grad_x[None], *[grads[k] for k in order], *[upd[k][0] for k in order],
            *[upd[k][1] for k in order], *[upd[k][2] for k in order])
```

```python
import functools
import math

import jax
import jax.numpy as jnp
from jax import lax
from jax.experimental import pallas as pl
from jax.experimental.pallas import tpu as pltpu

F32 = jnp.float32
BF16 = jnp.bfloat16
MESH = pl.DeviceIdType.MESH

N_DEV = 8
N_CHIP = 4
SUBLANE = 8
RMS_EPS = 1e-6
LN_EPS = 1e-5
CHUNK = 128
GROUPS = 8
ADAM_LR = 0.001
ADAM_B1 = 0.9
ADAM_B2 = 0.999
ADAM_EPS = 1e-08
ADAM_WD = 0.01
ADAM_STEP = 10
VMEM_LIMIT = 56 << 20

HBM_SPEC = pl.BlockSpec(memory_space=pltpu.HBM)
VMEM_SPEC = pl.BlockSpec(memory_space=pltpu.VMEM)
SEM_SPEC = pl.BlockSpec(memory_space=pltpu.SEMAPHORE)
ANY_SPEC = pl.BlockSpec(memory_space=pl.ANY)
EFFECT = pltpu.SideEffectType.DATAFLOW_SIDE_EFFECTING


def _params(*sem):
    return pltpu.CompilerParams(dimension_semantics=sem, vmem_limit_bytes=VMEM_LIMIT)


def _tile(n, want):
    if n <= want:
        return n
    t = want
    while n % t:
        t -= 128
    return t


def _sigmoid(x):
    return 1.0 / (1.0 + jnp.exp(-x))


def _silu_and_grad(x):
    s = _sigmoid(x)
    return x * s, s * (1.0 + x * (1.0 - s))


def _gelu_and_grad(x):
    cdf = 0.5 * (1.0 + lax.erf(x * (1.0 / math.sqrt(2.0))))
    pdf = jnp.exp(-0.5 * x * x) * (1.0 / math.sqrt(2.0 * math.pi))
    return x * cdf, cdf + x * pdf


def _gelu(x):
    return x * (0.5 * (1.0 + lax.erf(x * (1.0 / math.sqrt(2.0)))))


def _rms(x):
    r = lax.rsqrt(jnp.mean(x * x, axis=-1, keepdims=True) + RMS_EPS)
    return x * r, r


def _rms_bwd(dxn, xn, r):
    return r * (dxn - xn * jnp.mean(dxn * xn, axis=-1, keepdims=True))


def _colsum(a):
    return jnp.sum(a, axis=0, keepdims=True)


def _shift_down(cur, before, k):
    rolled = pltpu.roll(cur, k, 0)
    row = lax.broadcasted_iota(jnp.int32, before.shape, 0)
    head = jnp.where(row < k, pltpu.roll(before, k, 0), rolled[:SUBLANE])
    return jnp.concatenate([head, rolled[SUBLANE:]], axis=0)


def _shift_up(cur, after, k):
    n = cur.shape[0]
    rolled = pltpu.roll(cur, n - k, 0)
    row = lax.broadcasted_iota(jnp.int32, after.shape, 0)
    tail = jnp.where(row >= SUBLANE - k, pltpu.roll(after, SUBLANE - k, 0), rolled[n - SUBLANE:])
    return jnp.concatenate([rolled[:n - SUBLANE], tail], axis=0)


def _after_spec():
    return pl.BlockSpec((SUBLANE, 128), lambda *_: (0, 0))


def _mm_proj(h, w, n_split, name, after):
    s, d = h.shape
    e = w.shape[1] // n_split
    tm, tn = _tile(s, 1024), _tile(e, 1024)
    nj = e // tn

    def body(h_ref, w_ref, after_ref, o_ref):
        o_ref[...] = jnp.dot(h_ref[...], w_ref[...], preferred_element_type=F32)

    return pl.pallas_call(
        body, name=name,
        out_shape=jax.ShapeDtypeStruct((n_split, s, e), F32),
        grid=(s // tm, n_split * nj),
        in_specs=[pl.BlockSpec((tm, d), lambda i, j: (i, 0)), pl.BlockSpec((d, tn), lambda i, j: (0, j)),
                  _after_spec()],
        out_specs=pl.BlockSpec((None, tm, tn), lambda i, j: (j // nj, i, j % nj)),
        compiler_params=_params("parallel", "parallel"),
    )(h, w, after)


def _mm_proj_slab(h, w, proj, q_idx, n_split, name, after):
    s, d = h.shape
    e = w.shape[1] // n_split
    tm, tn = _tile(s, 1024), _tile(e, 2048)
    nj = e // tn

    def body(q_ref, h_ref, w_ref, *rest):
        o_ref = rest[-1]
        o_ref[...] = jnp.dot(h_ref[...], w_ref[...], preferred_element_type=F32)

    in_specs = [pl.BlockSpec((tm, d), lambda i, j, qr: (i, 0)), pl.BlockSpec((d, tn), lambda i, j, qr: (0, qr[0] * nj + j)),
                _after_spec()]
    args = [q_idx, h, w, after]
    aliases = {}
    if proj is not None:
        in_specs.append(ANY_SPEC)
        args.append(proj)
        aliases = {4: 0}
    return pl.pallas_call(
        body, name=name,
        out_shape=jax.ShapeDtypeStruct((n_split, s, e), F32),
        grid_spec=pltpu.PrefetchScalarGridSpec(
            num_scalar_prefetch=1, grid=(s // tm, nj), in_specs=in_specs,
            out_specs=pl.BlockSpec((None, tm, tn), lambda i, j, qr: (qr[0], i, j))),
        input_output_aliases=aliases,
        compiler_params=_params("parallel", "parallel"),
    )(*args)


def _mm_dh(dp, w, name, after):
    nq, s, e = dp.shape
    d = w.shape[0]
    tm, tn, tk = _tile(s, 1024), _tile(d, 1024), _tile(e, 1024)
    nkq = e // tk
    nk = nq * nkq

    def body(a_ref, b_ref, after_ref, o_ref):
        @pl.when(pl.program_id(2) == 0)
        def _():
            o_ref[...] = jnp.zeros_like(o_ref)
        o_ref[...] += lax.dot_general(a_ref[...], b_ref[...], (((1,), (1,)), ((), ())), preferred_element_type=F32)

    return pl.pallas_call(
        body, name=name,
        out_shape=jax.ShapeDtypeStruct((s, d), F32),
        grid=(s // tm, d // tn, nk),
        in_specs=[pl.BlockSpec((None, tm, tk), lambda i, j, k: (k // nkq, i, k % nkq)),
                  pl.BlockSpec((tn, tk), lambda i, j, k: (j, k)), _after_spec()],
        out_specs=pl.BlockSpec((tm, tn), lambda i, j, k: (i, j)),
        compiler_params=_params("parallel", "parallel", "arbitrary"),
    )(dp, w, after)


def _dw_body(n_t):
    def body(a_ref, b_ref, o_ref, acc_ref):
        t = pl.program_id(2)

        @pl.when(t == 0)
        def _():
            acc_ref[...] = jnp.zeros_like(acc_ref)
        acc_ref[...] += lax.dot_general(a_ref[...], b_ref[...], (((0,), (0,)), ((), ())), preferred_element_type=F32)

        @pl.when(t == n_t - 1)
        def _():
            o_ref[...] = acc_ref[...].astype(o_ref.dtype)
    return body


def _mm_dw_in(h, dp, name):
    s, d = h.shape
    nq, _, e = dp.shape
    tm, tn, tt = _tile(d, 1024), _tile(e, 1024), _tile(s, 1024)
    nj = e // tn

    return pl.pallas_call(
        _dw_body(s // tt), name=name,
        out_shape=jax.ShapeDtypeStruct((d, nq * e), BF16),
        grid=(d // tm, nq * nj, s // tt),
        in_specs=[pl.BlockSpec((tt, tm), lambda i, j, t: (t, i)),
                  pl.BlockSpec((None, tt, tn), lambda i, j, t: (j // nj, t, j % nj))],
        out_specs=pl.BlockSpec((tm, tn), lambda i, j, t: (i, j)),
        scratch_shapes=[pltpu.VMEM((tm, tn), F32)],
        compiler_params=_params("parallel", "parallel", "arbitrary"),
    )(h, dp)


def _mm_dw_out(y, dbr, name):
    s, e = y.shape
    d = dbr.shape[1]
    tm, tn, tt = _tile(e, 1024), _tile(d, 1024), _tile(s, 1024)

    return pl.pallas_call(
        _dw_body(s // tt), name=name,
        out_shape=jax.ShapeDtypeStruct((e, d), BF16),
        grid=(e // tm, d // tn, s // tt),
        in_specs=[pl.BlockSpec((tt, tm), lambda i, j, t: (t, i)), pl.BlockSpec((tt, tn), lambda i, j, t: (t, j))],
        out_specs=pl.BlockSpec((tm, tn), lambda i, j, t: (i, j)),
        scratch_shapes=[pltpu.VMEM((tm, tn), F32)],
        compiler_params=_params("parallel", "parallel", "arbitrary"),
    )(y, dbr)


def _row_spec(ts, d):
    return pl.BlockSpec((ts, d), lambda i: (i, 0))


def _vec_spec(d):
    return pl.BlockSpec((1, d), lambda i: (0, 0))


def _norm_mod(x, g, scale, shift, name):
    s, d = x.shape
    ts = _tile(s, 512)

    def body(x_ref, g_ref, sc_ref, sh_ref, h_ref):
        xn, _ = _rms(x_ref[...])
        h_ref[...] = ((xn * g_ref[...]) * (1.0 + sc_ref[...]) + sh_ref[...]).astype(BF16)

    return pl.pallas_call(
        body, name=name, out_shape=jax.ShapeDtypeStruct((s, d), BF16), grid=(s // ts,),
        in_specs=[_row_spec(ts, d), _vec_spec(d), _vec_spec(d), _vec_spec(d)],
        out_specs=_row_spec(ts, d), compiler_params=_params("parallel"),
    )(x, g, scale, shift)


def _residual_norm_mod(x, br, gate, g, scale, shift, name):
    s, d = x.shape
    ts = _tile(s, 512)

    def body(x_ref, br_ref, gate_ref, g_ref, sc_ref, sh_ref, x1_ref, h_ref):
        x1 = x_ref[...] + gate_ref[...] * br_ref[...]
        x1_ref[...] = x1
        xn, _ = _rms(x1)
        h_ref[...] = ((xn * g_ref[...]) * (1.0 + sc_ref[...]) + sh_ref[...]).astype(BF16)

    return pl.pallas_call(
        body, name=name,
        out_shape=(jax.ShapeDtypeStruct((s, d), F32), jax.ShapeDtypeStruct((s, d), BF16)), grid=(s // ts,),
        in_specs=[_row_spec(ts, d), _row_spec(ts, d)] + [_vec_spec(d)] * 4,
        out_specs=(_row_spec(ts, d), _row_spec(ts, d)), compiler_params=_params("parallel"),
    )(x, br, gate, g, scale, shift)


def _head(x1, br, gate, gf, target, name):
    s, d = x1.shape
    ts = _tile(s, 512)

    def body(x1_ref, br_ref, gate_ref, gf_ref, tg_ref, dx_ref, dbr_ref, loss_ref, dgf_ref, dgate_ref):
        @pl.when(pl.program_id(0) == 0)
        def _():
            loss_ref[...] = jnp.zeros_like(loss_ref)
            dgf_ref[...] = jnp.zeros_like(dgf_ref)
            dgate_ref[...] = jnp.zeros_like(dgate_ref)
        br = br_ref[...]
        gate = gate_ref[...]
        gf = gf_ref[...]
        xn, r = _rms(x1_ref[...] + gate * br)
        err = xn * gf - tg_ref[...]
        loss_ref[...] += 0.5 * jnp.sum(jnp.mean(err * err, axis=-1, keepdims=True))
        dout = err * (1.0 / d)
        dgf_ref[...] += _colsum(dout * xn)
        dx = _rms_bwd(dout * gf, xn, r)
        dx_ref[...] = dx
        dgate_ref[...] += _colsum(dx * br)
        dbr_ref[...] = (gate * dx).astype(BF16)

    return pl.pallas_call(
        body, name=name,
        out_shape=(jax.ShapeDtypeStruct((s, d), F32), jax.ShapeDtypeStruct((s, d), BF16),
                   jax.ShapeDtypeStruct((SUBLANE, 128), F32), jax.ShapeDtypeStruct((1, d), F32),
                   jax.ShapeDtypeStruct((1, d), F32)),
        grid=(s // ts,),
        in_specs=[_row_spec(ts, d), _row_spec(ts, d), _vec_spec(d), _vec_spec(d), _row_spec(ts, d)],
        out_specs=(_row_spec(ts, d), _row_spec(ts, d), pl.BlockSpec((SUBLANE, 128), lambda i: (0, 0)),
                   _vec_spec(d), _vec_spec(d)),
        compiler_params=_params("arbitrary"),
    )(x1, br, gate, gf, target)


def _norm_mod_bwd(dh, x, dx_in, g, scale, name, br=None, gate=None):
    s, d = x.shape
    ts = _tile(s, 512)
    has_branch = br is not None

    def body(*refs):
        if has_branch:
            (dh_ref, x_ref, dxin_ref, g_ref, sc_ref, br_ref, gate_ref,
             dx_ref, dsh_ref, dsc_ref, dg_ref, dbr_ref, dgate_ref) = refs
        else:
            dh_ref, x_ref, dxin_ref, g_ref, sc_ref, dx_ref, dsh_ref, dsc_ref, dg_ref = refs

        @pl.when(pl.program_id(0) == 0)
        def _():
            dsh_ref[...] = jnp.zeros_like(dsh_ref)
            dsc_ref[...] = jnp.zeros_like(dsc_ref)
            dg_ref[...] = jnp.zeros_like(dg_ref)
            if has_branch:
                dgate_ref[...] = jnp.zeros_like(dgate_ref)
        dh = dh_ref[...]
        g = g_ref[...]
        xn, r = _rms(x_ref[...])
        dsh_ref[...] += _colsum(dh)
        dsc_ref[...] += _colsum(dh * (xn * g))
        da = dh * (1.0 + sc_ref[...])
        dg_ref[...] += _colsum(da * xn)
        dx = dxin_ref[...] + _rms_bwd(da * g, xn, r)
        dx_ref[...] = dx
        if has_branch:
            dgate_ref[...] += _colsum(dx * br_ref[...])
            dbr_ref[...] = (gate_ref[...] * dx).astype(BF16)

    vec_out = jax.ShapeDtypeStruct((1, d), F32)
    out_shape = [jax.ShapeDtypeStruct((s, d), F32), vec_out, vec_out, vec_out]
    out_specs = [_row_spec(ts, d), _vec_spec(d), _vec_spec(d), _vec_spec(d)]
    in_specs = [_row_spec(ts, d)] * 3 + [_vec_spec(d)] * 2
    args = [dh, x, dx_in, g, scale]
    if has_branch:
        out_shape += [jax.ShapeDtypeStruct((s, d), BF16), vec_out]
        out_specs += [_row_spec(ts, d), _vec_spec(d)]
        in_specs += [_row_spec(ts, d), _vec_spec(d)]
        args += [br, gate]
    return pl.pallas_call(
        body, name=name, out_shape=tuple(out_shape), grid=(s // ts,),
        in_specs=in_specs, out_specs=tuple(out_specs), compiler_params=_params("arbitrary"),
    )(*args)


def _conv_tiles(s, e):
    return _tile(s, 512), _tile(e, 512)


def _conv_fwd(proj, conv_w, conv_b, w_out, name, after):
    _, s, e = proj.shape
    d = w_out.shape[1]
    ts, te = _conv_tiles(s, e)
    hb = ts // SUBLANE

    def body(bg_ref, cg_ref, xi_ref, z_ref, cgp_ref, xip_ref, w0_ref, w1_ref, w2_ref, b_ref, wo_ref, after_ref,
             y_ref, br_ref):
        cx = cg_ref[...] * xi_ref[...]
        before = jnp.where(pl.program_id(0) > 0, cgp_ref[...] * xip_ref[...], 0.0)
        conv = b_ref[...] + w2_ref[...] * cx
        conv = conv + w0_ref[...] * _shift_down(cx, before, 2)
        conv = conv + w1_ref[...] * _shift_down(cx, before, 1)
        z = z_ref[...]
        y = ((z * _sigmoid(z)) * bg_ref[...] * conv).astype(BF16)
        y_ref[...] = y

        @pl.when(pl.program_id(1) == 0)
        def _():
            br_ref[...] = jnp.zeros_like(br_ref)
        br_ref[...] += jnp.dot(y, wo_ref[...], preferred_element_type=F32)

    def part(q):
        return pl.BlockSpec((None, ts, te), lambda i, j: (q, i, j))

    def halo_before(q):
        return pl.BlockSpec((None, SUBLANE, te), lambda i, j: (q, jnp.maximum(i * hb - 1, 0), j))

    return pl.pallas_call(
        body, name=name,
        out_shape=(jax.ShapeDtypeStruct((s, e), BF16), jax.ShapeDtypeStruct((s, d), F32)), grid=(s // ts, e // te),
        in_specs=[part(0), part(1), part(2), part(3), halo_before(1), halo_before(2)]
        + [pl.BlockSpec((1, te), lambda i, j: (0, j))] * 4 + [pl.BlockSpec((te, d), lambda i, j: (j, 0)), _after_spec()],
        out_specs=(pl.BlockSpec((ts, te), lambda i, j: (i, j)), pl.BlockSpec((ts, d), lambda i, j: (i, 0))),
        compiler_params=_params("parallel", "arbitrary"),
    )(proj, proj, proj, proj, proj, proj, *conv_w, conv_b, w_out, after)


def _conv_bwd(proj, dbr, w_out, conv_w, conv_b, name):
    _, s, e = proj.shape
    d = dbr.shape[1]
    ts, te = _conv_tiles(s, e)
    hb = ts // SUBLANE
    n_i = s // ts
    last_halo = s // SUBLANE - 1
    pack = 2 * SUBLANE
    last_pack = s // pack - 1
    nt = (((1,), (1,)), ((), ()))

    def body(bg_ref, cg_ref, xi_ref, z_ref, dbr_ref, cgp_ref, xip_ref, bgn_ref, zn_ref, dbrn_ref, wo_ref,
             w0_ref, w1_ref, w2_ref, b_ref, dp_ref, dw0_ref, dw1_ref, dw2_ref, db_ref):
        i = pl.program_id(1)

        @pl.when(i == 0)
        def _():
            for acc in (dw0_ref, dw1_ref, dw2_ref, db_ref):
                acc[...] = jnp.zeros_like(acc)
        wo = wo_ref[...]
        dy = lax.dot_general(dbr_ref[...], wo, nt, preferred_element_type=F32)
        dyn = lax.dot_general(dbrn_ref[...], wo, nt, preferred_element_type=F32)[:SUBLANE]
        bg, cg, xi, z = bg_ref[...], cg_ref[...], xi_ref[...], z_ref[...]
        w0, w1, w2 = w0_ref[...], w1_ref[...], w2_ref[...]
        cx = cg * xi
        before = jnp.where(i > 0, cgp_ref[...] * xip_ref[...], 0.0)
        cx1 = _shift_down(cx, before, 1)
        cx2 = _shift_down(cx, before, 2)
        conv = b_ref[...] + w2 * cx
        conv = conv + w0 * cx2
        conv = conv + w1 * cx1
        sz, dsz = _silu_and_grad(z)
        dp_ref[3] = (dy * bg * conv * dsz).astype(BF16)
        dp_ref[0] = (dy * sz * conv).astype(BF16)
        dconv = dy * sz * bg
        zn = zn_ref[...]
        after = jnp.where(i < n_i - 1, dyn * (zn * _sigmoid(zn)) * bgn_ref[...], 0.0)
        db_ref[...] += _colsum(dconv)
        dw2_ref[...] += _colsum(dconv * cx)
        dw1_ref[...] += _colsum(dconv * cx1)
        dw0_ref[...] += _colsum(dconv * cx2)
        dcx = w2 * dconv + w1 * _shift_up(dconv, after, 1) + w0 * _shift_up(dconv, after, 2)
        dp_ref[1] = (dcx * xi).astype(BF16)
        dp_ref[2] = (dcx * cg).astype(BF16)

    def part(q):
        return pl.BlockSpec((None, ts, te), lambda j, i: (q, i, j))

    def halo_before(q):
        return pl.BlockSpec((None, SUBLANE, te), lambda j, i: (q, jnp.maximum(i * hb - 1, 0), j))

    def halo_after(q):
        return pl.BlockSpec((None, SUBLANE, te), lambda j, i: (q, jnp.minimum((i + 1) * hb, last_halo), j))

    return pl.pallas_call(
        body, name=name,
        out_shape=(jax.ShapeDtypeStruct((4, s, e), BF16),) + (jax.ShapeDtypeStruct((1, e), F32),) * 4,
        grid=(e // te, n_i),
        in_specs=[part(0), part(1), part(2), part(3), pl.BlockSpec((ts, d), lambda j, i: (i, 0)),
                  halo_before(1), halo_before(2), halo_after(0), halo_after(3),
                  pl.BlockSpec((pack, d), lambda j, i: (jnp.minimum((i + 1) * (ts // pack), last_pack), 0)),
                  pl.BlockSpec((te, d), lambda j, i: (j, 0))]
        + [pl.BlockSpec((1, te), lambda j, i: (0, j))] * 4,
        out_specs=(pl.BlockSpec((4, ts, te), lambda j, i: (0, i, j)),) + (pl.BlockSpec((1, te), lambda j, i: (0, j)),) * 4,
        compiler_params=_params("parallel", "arbitrary"),
    )(proj, proj, proj, proj, dbr, proj, proj, proj, proj, dbr, w_out, *conv_w, conv_b)


def _tril(w):
    row = lax.broadcasted_iota(jnp.int32, w.shape, 0)
    col = lax.broadcasted_iota(jnp.int32, w.shape, 1)
    return jnp.where(row >= col, w, 0.0)


def _triu(w):
    row = lax.broadcasted_iota(jnp.int32, w.shape, 0)
    col = lax.broadcasted_iota(jnp.int32, w.shape, 1)
    return jnp.where(row <= col, w, 0.0)


def _layer_norm_fwd(v, g, b):
    mu = jnp.mean(v, axis=-1, keepdims=True)
    vc = v - mu
    rstd = lax.rsqrt(jnp.mean(vc * vc, axis=-1, keepdims=True) + LN_EPS)
    vhat = vc * rstd
    return vhat * g + b, vhat, rstd


GMLP_CHUNKS_PER_STEP = 2


def _gmlp_rows(s):
    return CHUNK * min(GMLP_CHUNKS_PER_STEP, s // CHUNK)


def _mix_positions(w_ref, src_scr, dst_scr, gw, mask, bias_ref=None):
    for gi in range(GROUPS):
        cols = slice(gi * gw, (gi + 1) * gw)
        wm = mask(w_ref[gi]).astype(BF16)
        for n in range(src_scr.shape[0] // CHUNK):
            rows = slice(n * CHUNK, (n + 1) * CHUNK)
            out = jnp.dot(wm, src_scr[rows, cols], preferred_element_type=F32)
            if bias_ref is not None:
                out = out + bias_ref[:, gi:gi + 1]
            dst_scr[rows, cols] = out


def _gmlp_fwd(proj, ln_g, ln_b, w_s, b_s_t, w_out, name):
    _, s, e = proj.shape
    d = w_out.shape[1]
    gw = e // GROUPS
    ts = _gmlp_rows(s)

    def body(pu_ref, pv_ref, pz_ref, g_ref, b_ref, ws_ref, bs_ref, wo_ref, y_ref, br_ref, vn_scr, mix_scr):
        vn, _, _ = _layer_norm_fwd(_gelu(pv_ref[...]), g_ref[...], b_ref[...])
        vn_scr[...] = vn.astype(BF16)
        _mix_positions(ws_ref, vn_scr, mix_scr, gw, _tril, bs_ref)
        z = pz_ref[...]
        y = ((z * _sigmoid(z)) * (_gelu(pu_ref[...]) * mix_scr[...])).astype(BF16)
        y_ref[...] = y
        br_ref[...] = jnp.dot(y, wo_ref[...], preferred_element_type=F32)

    def part(q):
        return pl.BlockSpec((None, ts, e), lambda i: (q, i, 0))

    return pl.pallas_call(
        body, name=name,
        out_shape=(jax.ShapeDtypeStruct((s, e), BF16), jax.ShapeDtypeStruct((s, d), F32)), grid=(s // ts,),
        in_specs=[part(0), part(1), part(2), _vec_spec(e), _vec_spec(e),
                  pl.BlockSpec((GROUPS, CHUNK, CHUNK), lambda i: (0, 0, 0)),
                  pl.BlockSpec((CHUNK, GROUPS), lambda i: (0, 0)), pl.BlockSpec((e, d), lambda i: (0, 0))],
        out_specs=(pl.BlockSpec((ts, e), lambda i: (i, 0)), pl.BlockSpec((ts, d), lambda i: (i, 0))),
        scratch_shapes=[pltpu.VMEM((ts, e), BF16), pltpu.VMEM((ts, e), F32)],
        compiler_params=_params("parallel"),
    )(proj, proj, proj, ln_g, ln_b, w_s, b_s_t, w_out)


def _gmlp_bwd(proj, dbr, w_out, ln_g, ln_b, w_s, w_s_t, b_s_t, name):
    _, s, e = proj.shape
    d = dbr.shape[1]
    gw = e // GROUPS
    ts = _gmlp_rows(s)
    n_i = s // ts

    def body(pu_ref, pv_ref, pz_ref, dbr_ref, wo_ref, g_ref, b_ref, ws_ref, wst_ref, bs_ref,
             dp_ref, dws_ref, dbs_ref, dlg_ref, dlb_ref, vn_scr, mix_scr, dm_scr, dvn_scr, dmacc_scr):
        i = pl.program_id(0)

        @pl.when(i == 0)
        def _():
            dws_ref[...] = jnp.zeros_like(dws_ref)
            dlg_ref[...] = jnp.zeros_like(dlg_ref)
            dlb_ref[...] = jnp.zeros_like(dlb_ref)
            dmacc_scr[...] = jnp.zeros_like(dmacc_scr)
        ln_g = g_ref[...]
        u, du_dpu = _gelu_and_grad(pu_ref[...])
        v, dv_dpv = _gelu_and_grad(pv_ref[...])
        vn, vhat, rstd = _layer_norm_fwd(v, ln_g, b_ref[...])
        vn_scr[...] = vn.astype(BF16)
        _mix_positions(ws_ref, vn_scr, mix_scr, gw, _tril, bs_ref)
        mixed = mix_scr[...]
        dy = lax.dot_general(dbr_ref[...], wo_ref[...], (((1,), (1,)), ((), ())), preferred_element_type=F32)
        sz, dsz = _silu_and_grad(pz_ref[...])
        ds = dy * sz
        dp_ref[2] = (dy * (u * mixed) * dsz).astype(BF16)
        dp_ref[0] = (ds * mixed * du_dpu).astype(BF16)
        dm = ds * u
        dm_scr[...] = dm.astype(BF16)
        for n in range(ts // CHUNK):
            rows = slice(n * CHUNK, (n + 1) * CHUNK)
            dmacc_scr[...] += dm[rows]
            for gi in range(GROUPS):
                cols = slice(gi * gw, (gi + 1) * gw)
                dws_ref[gi] += lax.dot_general(dm_scr[rows, cols], vn_scr[rows, cols], (((1,), (1,)), ((), ())),
                                               preferred_element_type=F32)
        _mix_positions(wst_ref, dm_scr, dvn_scr, gw, _triu)
        dvn = dvn_scr[...]
        dlg_ref[...] += _colsum(dvn * vhat)
        dlb_ref[...] += _colsum(dvn)
        dvh = dvn * ln_g
        dv = rstd * (dvh - jnp.mean(dvh, axis=-1, keepdims=True) - vhat * jnp.mean(dvh * vhat, axis=-1, keepdims=True))
        dp_ref[1] = (dv * dv_dpv).astype(BF16)

        @pl.when(i == n_i - 1)
        def _():
            for gi in range(GROUPS):
                dws_ref[gi] = _tril(dws_ref[gi])
                dbs_ref[:, gi:gi + 1] = jnp.sum(dmacc_scr[:, gi * gw:(gi + 1) * gw], axis=1, keepdims=True)

    def part(q):
        return pl.BlockSpec((None, ts, e), lambda i: (q, i, 0))

    w_spec = pl.BlockSpec((GROUPS, CHUNK, CHUNK), lambda i: (0, 0, 0))
    bs_spec = pl.BlockSpec((CHUNK, GROUPS), lambda i: (0, 0))
    return pl.pallas_call(
        body, name=name,
        out_shape=(jax.ShapeDtypeStruct((3, s, e), BF16), jax.ShapeDtypeStruct((GROUPS, CHUNK, CHUNK), F32),
                   jax.ShapeDtypeStruct((CHUNK, GROUPS), F32), jax.ShapeDtypeStruct((1, e), F32),
                   jax.ShapeDtypeStruct((1, e), F32)),
        grid=(n_i,),
        in_specs=[part(0), part(1), part(2), pl.BlockSpec((ts, d), lambda i: (i, 0)),
                  pl.BlockSpec((e, d), lambda i: (0, 0)), _vec_spec(e), _vec_spec(e), w_spec, w_spec, bs_spec],
        out_specs=(pl.BlockSpec((3, ts, e), lambda i: (0, i, 0)), w_spec, bs_spec, _vec_spec(e), _vec_spec(e)),
        scratch_shapes=[pltpu.VMEM((ts, e), BF16), pltpu.VMEM((ts, e), F32), pltpu.VMEM((ts, e), BF16),
                        pltpu.VMEM((ts, e), F32), pltpu.VMEM((CHUNK, e), F32)],
        compiler_params=_params("arbitrary"),
    )(proj, proj, proj, dbr, w_out, ln_g, ln_b, w_s, w_s_t, b_s_t)


def _mod_fwd(c_all, mod_w, mod_b_cols, name):
    n_layer, d, w = mod_w.shape

    def body(c_ref, w_ref, b_ref, ca_ref, o_ref):
        c = c_ref[...]
        ca = c * _sigmoid(c)
        ca_ref[...] = ca
        for li in range(n_layer):
            o_ref[li * N_DEV:(li + 1) * N_DEV, :] = (
                jnp.dot(ca, w_ref[li], preferred_element_type=F32, precision=lax.Precision.HIGHEST) + b_ref[li])

    return pl.pallas_call(
        body, name=name,
        out_shape=(jax.ShapeDtypeStruct((N_DEV, d), F32), jax.ShapeDtypeStruct((n_layer * N_DEV, w), F32)),
        in_specs=[VMEM_SPEC] * 3, out_specs=(VMEM_SPEC, VMEM_SPEC),
        compiler_params=pltpu.CompilerParams(vmem_limit_bytes=VMEM_LIMIT),
    )(c_all, mod_w, mod_b_cols)


def _adamw(w, g, m, v):
    m = ADAM_B1 * m + (1.0 - ADAM_B1) * g
    v = ADAM_B2 * v + (1.0 - ADAM_B2) * (g * g)
    m_hat = m / (1.0 - ADAM_B1 ** ADAM_STEP)
    v_hat = v / (1.0 - ADAM_B2 ** ADAM_STEP)
    delta = -ADAM_LR * (m_hat / (jnp.sqrt(v_hat) + ADAM_EPS) + ADAM_WD * w)
    return delta, m, v


def _adamw_2d(w, g, m, v, name):
    r, c = w.shape
    tr, tc = _tile(r, 512), _tile(c, 1024)

    def body(w_ref, g_ref, m_ref, v_ref, d_ref, nm_ref, nv_ref):
        d_ref[...], nm_ref[...], nv_ref[...] = _adamw(w_ref[...], g_ref[...], m_ref[...], v_ref[...])

    spec = pl.BlockSpec((tr, tc), lambda i, j: (i, j))
    shape = jax.ShapeDtypeStruct((r, c), F32)
    return pl.pallas_call(
        body, name=name, out_shape=(shape, shape, shape), grid=(r // tr, c // tc),
        in_specs=[spec] * 4, out_specs=(spec, spec, spec), compiler_params=_params("parallel", "parallel"),
    )(w, g, m, v)


def _mod_w_update(ca_t, dmod_cols, w, m, v, name):
    n_layer, d, wd = w.shape
    tr = _tile(d, 256)

    def body(ca_ref, dm_ref, w_ref, m_ref, v_ref, g_ref, d_ref, nm_ref, nv_ref):
        ca = ca_ref[...]
        dm = dm_ref[...]
        g = ca[:, 0:1] * dm[0:1, :]
        for b in range(1, N_DEV):
            g = g + ca[:, b:b + 1] * dm[b:b + 1, :]
        g_ref[...] = g
        d_ref[...], nm_ref[...], nv_ref[...] = _adamw(w_ref[...], g, m_ref[...], v_ref[...])

    spec = pl.BlockSpec((None, tr, wd), lambda l, i: (l, i, 0))
    shape = jax.ShapeDtypeStruct((n_layer, d, wd), F32)
    return pl.pallas_call(
        body, name=name, out_shape=(shape,) * 4, grid=(n_layer, d // tr),
        in_specs=[pl.BlockSpec((tr, N_DEV), lambda l, i: (i, 0)), pl.BlockSpec((None, N_DEV, wd), lambda l, i: (l, 0, 0)),
                  spec, spec, spec],
        out_specs=(spec,) * 4, compiler_params=_params("parallel", "parallel"),
    )(ca_t, dmod_cols, w, m, v)


def _adamw_small(ws, gs, ms, vs, name):
    n = len(ws)

    def body(*refs):
        ins, outs = refs[:4 * n], refs[4 * n:]
        for k in range(n):
            delta, nm, nv = _adamw(ins[k][...], ins[n + k][...], ins[2 * n + k][...], ins[3 * n + k][...])
            outs[3 * k][...] = delta
            outs[3 * k + 1][...] = nm
            outs[3 * k + 2][...] = nv

    out_shape = []
    for w in ws:
        out_shape += [jax.ShapeDtypeStruct(w.shape, F32)] * 3
    outs = pl.pallas_call(
        body, name=name, out_shape=tuple(out_shape),
        in_specs=[VMEM_SPEC] * (4 * n), out_specs=tuple([VMEM_SPEC] * (3 * n)),
        compiler_params=pltpu.CompilerParams(vmem_limit_bytes=VMEM_LIMIT),
    )(*ws, *gs, *ms, *vs)
    return [tuple(outs[3 * k:3 * k + 3]) for k in range(n)]


def _place():
    return lax.axis_index("x"), lax.axis_index("y"), lax.axis_index("c")


def _other_chips(x, y):
    return [(1 - x, y), (x, 1 - y), (1 - x, 1 - y)]


def _all_gather(block, name):
    m_per, n = block.shape

    def body(x_ref, out_ref, send_sems, recv_sems, local_sem):
        x, y, c = _place()
        me, sibling = (x, y, c), (x, y, 1 - c)
        chips = _other_chips(x, y)

        def rows(px, py, pc):
            return out_ref.at[pl.ds((4 * px + 2 * py + pc) * m_per, m_per), :]

        def copy(k, blk, to, src=None):
            return pltpu.make_async_remote_copy(
                src_ref=rows(*blk) if src is None else src, dst_ref=rows(*blk),
                send_sem=send_sems.at[k], recv_sem=recv_sems.at[k], device_id=to, device_id_type=MESH)

        mine = pltpu.make_async_copy(x_ref, rows(*me), local_sem)
        mine.start()
        first = [copy(0, me, sibling, src=x_ref)]
        first += [copy(1 + j, me, (*chip, c), src=x_ref) for j, chip in enumerate(chips)]
        for cp in first:
            cp.start()
        passed = [copy(4 + j, (*chip, c), sibling) for j, chip in enumerate(chips)]
        for j, chip in enumerate(chips):
            copy(1 + j, (*chip, c), me).wait_recv()
            passed[j].start()
        copy(0, sibling, me).wait_recv()
        for j, chip in enumerate(chips):
            copy(4 + j, (*chip, 1 - c), me).wait_recv()
        for cp in first + passed:
            cp.wait_send()
        mine.wait()

    return pl.pallas_call(
        body, name=name, out_shape=jax.ShapeDtypeStruct((N_DEV * m_per, n), F32),
        in_specs=[VMEM_SPEC], out_specs=VMEM_SPEC,
        scratch_shapes=[pltpu.SemaphoreType.DMA((7,)), pltpu.SemaphoreType.DMA((7,)), pltpu.SemaphoreType.DMA],
        compiler_params=pltpu.CompilerParams(vmem_limit_bytes=VMEM_LIMIT),
    )(block)


def _hbm(a):
    return pltpu.with_memory_space_constraint(a, pltpu.HBM)


def _place_own(shard, kind, chip_idx, name):
    r, cdim = shard.shape
    tr, tc = _tile(r, 512), _tile(cdim, 1024)
    nrb, ncb = r // tr, cdim // tc

    def body(k_ref, s_ref, o_ref):
        o_ref[...] = s_ref[...].astype(BF16)

    if kind == "col":
        full, o_map = (r, N_CHIP * cdim), lambda i, j, kr: (i, kr[0] * ncb + j)
    else:
        full, o_map = (N_CHIP * r, cdim), lambda i, j, kr: (kr[0] * nrb + i, j)
    return pl.pallas_call(
        body, name=name, out_shape=jax.ShapeDtypeStruct(full, BF16),
        grid_spec=pltpu.PrefetchScalarGridSpec(
            num_scalar_prefetch=1, grid=(nrb, ncb),
            in_specs=[pl.BlockSpec((tr, tc), lambda i, j, kr: (i, j))],
            out_specs=pl.BlockSpec((tr, tc), o_map)),
        compiler_params=_params("parallel", "parallel"),
    )(chip_idx, shard)


def _weight_window(ref, kind, shard_shape, k, half):
    r, cdim = shard_shape
    hr = r // 2
    if kind == "col":
        return ref.at[pl.ds(half * hr, hr), pl.ds(pl.multiple_of(k * cdim, 128), cdim)]
    return ref.at[pl.ds(pl.multiple_of(k * r + half * hr, 2 * SUBLANE), hr), :]


def _gather_weights_start(fulls, kinds, shard_shapes, groups, after, name):
    n, ng = len(fulls), len(groups)

    def body(*refs):
        ins = refs[:n]
        sems = refs[n + 1:n + 1 + 2 * ng]
        token = refs[2 * n + 1 + 2 * ng]
        x, y, c = _place()
        chips = _other_chips(x, y)
        for g, group in enumerate(groups):
            for pos, (w, j) in enumerate(group):
                own = _weight_window(ins[w], kinds[w], shard_shapes[w], 2 * x + y, c)
                for cc in range(2):
                    pltpu.make_async_remote_copy(
                        src_ref=own, dst_ref=own,
                        send_sem=sems[2 * g].at[2 * pos + cc], recv_sem=sems[2 * g + 1].at[2 * pos + c],
                        device_id=(*chips[j], cc), device_id_type=MESH).start()
        token[...] = jnp.zeros_like(token)

    sem_shapes = []
    for group in groups:
        sem_shapes += [pltpu.SemaphoreType.DMA((2 * len(group),))] * 2
    outs = pl.pallas_call(
        body, name=name,
        out_shape=tuple(sem_shapes) + tuple(pltpu.HBM(f.shape, f.dtype) for f in fulls)
        + (jax.ShapeDtypeStruct((SUBLANE, 128), F32),),
        in_specs=[HBM_SPEC] * n + [ANY_SPEC], out_specs=(SEM_SPEC,) * (2 * ng) + (HBM_SPEC,) * n + (VMEM_SPEC,),
        input_output_aliases={w: 2 * ng + w for w in range(n)},
        compiler_params=pltpu.CompilerParams(has_side_effects=EFFECT),
    )(*[_hbm(f) for f in fulls], after)
    sems = [(outs[2 * g], outs[2 * g + 1]) for g in range(ng)]
    return sems, list(outs[2 * ng:2 * ng + n]), outs[2 * ng + n]


def _gather_weights_wait(sems, group, fulls, kinds, shard_shapes, after, name):
    n = len(fulls)

    def body(*refs):
        ins = refs[:n]
        send_sems, recv_sems = refs[n], refs[n + 1]
        x, y, c = _place()
        chips = _other_chips(x, y)
        for pos, (w, j) in enumerate(group):
            own = _weight_window(ins[w], kinds[w], shard_shapes[w], 2 * x + y, c)
            for cc in range(2):
                landed = _weight_window(ins[w], kinds[w], shard_shapes[w], 2 * chips[j][0] + chips[j][1], cc)
                k = 2 * pos + cc
                pltpu.make_async_remote_copy(
                    src_ref=own, dst_ref=own, send_sem=send_sems.at[k], recv_sem=recv_sems.at[k],
                    device_id=(*chips[j], cc), device_id_type=MESH).wait_send()
                pltpu.make_async_remote_copy(
                    src_ref=landed, dst_ref=landed, send_sem=send_sems.at[k], recv_sem=recv_sems.at[k],
                    device_id=(*chips[j], cc), device_id_type=MESH).wait_recv()

    return list(pl.pallas_call(
        body, name=name, out_shape=tuple(pltpu.HBM(f.shape, f.dtype) for f in fulls),
        in_specs=[HBM_SPEC] * n + [SEM_SPEC, SEM_SPEC, ANY_SPEC], out_specs=(HBM_SPEC,) * n,
        input_output_aliases={w: w for w in range(n)},
        compiler_params=pltpu.CompilerParams(has_side_effects=EFFECT),
    )(*fulls, sems[0], sems[1], after))


def _grad_piece(ref, kind, h, cdim, k, half):
    if kind == "col":
        return ref.at[pl.ds(half * h, h), pl.ds(pl.multiple_of(k * cdim, 128), cdim)]
    return ref.at[k, pl.ds(half * h, h), :]


def _grad_dims(g, kind):
    return (g.shape[0] // 2, g.shape[1] // N_CHIP) if kind == "col" else (g.shape[1] // 2, g.shape[2])


def _reduce_start(grads, kinds, name):
    n = len(grads)
    dims = [_grad_dims(g, kind) for g, kind in zip(grads, kinds)]
    lands = [lax.empty((N_DEV - 1, h, cdim), g.dtype) for g, (h, cdim) in zip(grads, dims)]

    def body(*refs):
        g_ins, land_ins = refs[:n], refs[n:2 * n]
        send_sems, recv_sems = refs[2 * n], refs[2 * n + 1]
        token = refs[4 * n + 2]
        x, y, c = _place()
        for w in range(n):
            h, cdim = dims[w]
            base = (N_DEV - 1) * w
            pltpu.make_async_remote_copy(
                src_ref=_grad_piece(g_ins[w], kinds[w], h, cdim, 2 * x + y, 1 - c), dst_ref=land_ins[w].at[0],
                send_sem=send_sems.at[base], recv_sem=recv_sems.at[base],
                device_id=(x, y, 1 - c), device_id_type=MESH).start()
            for j, chip in enumerate(_other_chips(x, y)):
                for cc in range(2):
                    pltpu.make_async_remote_copy(
                        src_ref=_grad_piece(g_ins[w], kinds[w], h, cdim, 2 * chip[0] + chip[1], cc),
                        dst_ref=land_ins[w].at[1 + 2 * j + c],
                        send_sem=send_sems.at[base + 1 + 2 * j + cc], recv_sem=recv_sems.at[base + 1 + 2 * j + c],
                        device_id=(*chip, cc), device_id_type=MESH).start()
        token[...] = jnp.zeros_like(token)

    sems = pltpu.SemaphoreType.DMA(((N_DEV - 1) * n,))
    outs = pl.pallas_call(
        body, name=name,
        out_shape=(sems, sems) + tuple(pltpu.HBM(a.shape, a.dtype) for a in list(grads) + lands)
        + (jax.ShapeDtypeStruct((SUBLANE, 128), F32),),
        in_specs=[HBM_SPEC] * (2 * n), out_specs=(SEM_SPEC, SEM_SPEC) + (HBM_SPEC,) * (2 * n) + (VMEM_SPEC,),
        input_output_aliases={i: 2 + i for i in range(2 * n)},
        compiler_params=pltpu.CompilerParams(has_side_effects=EFFECT),
    )(*[_hbm(a) for a in list(grads) + lands])
    return outs[0], outs[1], list(outs[2:2 + n]), list(outs[2 + n:2 + 2 * n]), outs[2 + 2 * n]


def _reduce_wait(send_sems, recv_sems, grads, lands, kinds, after, name):
    n = len(grads)
    dims = [_grad_dims(g, kind) for g, kind in zip(grads, kinds)]

    def body(*refs):
        g_ins, land_ins = refs[:n], refs[n:2 * n]
        send_sems, recv_sems = refs[2 * n], refs[2 * n + 1]
        x, y, c = _place()
        for w in range(n):
            h, cdim = dims[w]
            piece = _grad_piece(g_ins[w], kinds[w], h, cdim, 2 * x + y, c)
            for s in range(N_DEV - 1):
                k = (N_DEV - 1) * w + s
                slot = land_ins[w].at[s]
                pltpu.make_async_remote_copy(
                    src_ref=piece, dst_ref=slot, send_sem=send_sems.at[k], recv_sem=recv_sems.at[k],
                    device_id=(x, y, 1 - c), device_id_type=MESH).wait_send()
                pltpu.make_async_remote_copy(
                    src_ref=piece, dst_ref=slot, send_sem=send_sems.at[k], recv_sem=recv_sems.at[k],
                    device_id=(x, y, 1 - c), device_id_type=MESH).wait_recv()

    outs = pl.pallas_call(
        body, name=name, out_shape=tuple(pltpu.HBM(a.shape, a.dtype) for a in list(grads) + list(lands)),
        in_specs=[HBM_SPEC] * (2 * n) + [SEM_SPEC, SEM_SPEC, ANY_SPEC], out_specs=(HBM_SPEC,) * (2 * n),
        input_output_aliases={i: i for i in range(2 * n)},
        compiler_params=pltpu.CompilerParams(has_side_effects=EFFECT),
    )(*grads, *lands, send_sems, recv_sems, after)
    return list(outs[:n]), list(outs[n:])


def _add_pieces(g, land, kind, chip_idx, core_idx, name):
    _, h, cdim = land.shape
    tr, tc = _tile(h, 256), _tile(cdim, 2048)
    nrb, ncb = h // tr, cdim // tc

    def body(k_ref, c_ref, g_ref, l_ref, o_ref):
        acc = g_ref[...].astype(F32)
        for s in range(N_DEV - 1):
            acc = acc + l_ref[s].astype(F32)
        o_ref[...] = acc

    if kind == "col":
        g_spec = pl.BlockSpec((tr, tc), lambda i, j, kr, cr: (cr[0] * nrb + i, kr[0] * ncb + j))
    else:
        g_spec = pl.BlockSpec((None, tr, tc), lambda i, j, kr, cr: (kr[0], cr[0] * nrb + i, j))
    return pl.pallas_call(
        body, name=name, out_shape=jax.ShapeDtypeStruct((2 * h, cdim), F32),
        grid_spec=pltpu.PrefetchScalarGridSpec(
            num_scalar_prefetch=2, grid=(nrb, ncb),
            in_specs=[g_spec, pl.BlockSpec((N_DEV - 1, tr, tc), lambda i, j, kr, cr: (0, i, j))],
            out_specs=pl.BlockSpec((tr, tc), lambda i, j, kr, cr: (cr[0] * nrb + i, j))),
        compiler_params=_params("parallel", "parallel"),
    )(chip_idx, core_idx, g, land)


def _join_halves(shards, name):
    n = len(shards)

    def body(*refs):
        ins, outs = refs[:n], refs[n:2 * n]
        send_sems, recv_sems = refs[2 * n:]
        x, y, c = _place()
        copies = []
        for w in range(n):
            h = shards[w].shape[0] // 2
            cp = pltpu.make_async_remote_copy(
                src_ref=ins[w].at[pl.ds(c * h, h), :], dst_ref=outs[w].at[pl.ds(c * h, h), :],
                send_sem=send_sems.at[w], recv_sem=recv_sems.at[w], device_id=(x, y, 1 - c), device_id_type=MESH)
            cp.start()
            copies.append(cp)
        for w in range(n):
            h = shards[w].shape[0] // 2
            theirs = outs[w].at[pl.ds((1 - c) * h, h), :]
            pltpu.make_async_remote_copy(
                src_ref=theirs, dst_ref=theirs, send_sem=send_sems.at[w], recv_sem=recv_sems.at[w],
                device_id=(x, y, 1 - c), device_id_type=MESH).wait_recv()
        for cp in copies:
            cp.wait_send()

    return pl.pallas_call(
        body, name=name, out_shape=tuple(jax.ShapeDtypeStruct(s.shape, s.dtype) for s in shards),
        in_specs=[HBM_SPEC] * n, out_specs=tuple([HBM_SPEC] * n),
        input_output_aliases={w: w for w in range(n)},
        scratch_shapes=[pltpu.SemaphoreType.DMA((n,))] * 2,
    )(*shards)


def _sum_devices(gathered, rows, name):
    n = gathered.shape[1]

    def body(g_ref, o_ref):
        acc = g_ref[0:rows, :]
        for dev in range(1, N_DEV):
            acc = acc + g_ref[dev * rows:(dev + 1) * rows, :]
        o_ref[...] = acc

    return pl.pallas_call(
        body, name=name, out_shape=jax.ShapeDtypeStruct((rows, n), F32),
        in_specs=[VMEM_SPEC], out_specs=VMEM_SPEC,
        compiler_params=pltpu.CompilerParams(vmem_limit_bytes=VMEM_LIMIT),
    )(gathered)


def _pack(vectors, width):
    flat = [v.reshape(-1) for v in vectors]
    offsets, total = [], 0
    for f in flat:
        offsets.append(total)
        total += f.shape[0]
    rows = -(-total // (width * SUBLANE)) * SUBLANE
    flat.append(jnp.zeros((rows * width - total,), F32))
    return jnp.concatenate(flat).reshape(rows, width), offsets


def kernel(x, c, mod_w, mod_b, norm_g, a_w_in, a_conv_w, a_conv_b, a_w_out, b_w_in, b_ln_g, b_ln_b, b_w_s, b_b_s, b_w_out, final_g, loss_target, m_mod_w, m_mod_b, m_norm_g, m_a_w_in, m_a_conv_w, m_a_conv_b, m_a_w_out, m_b_w_in, m_b_ln_g, m_b_ln_b, m_b_w_s, m_b_b_s, m_b_w_out, m_final_g, v_mod_w, v_mod_b, v_norm_g, v_a_w_in, v_a_conv_w, v_a_conv_b, v_a_w_out, v_b_w_in, v_b_ln_g, v_b_ln_b, v_b_w_s, v_b_b_s, v_b_w_out, v_final_g):
    seq, d = x.shape[1], x.shape[2]
    e = a_conv_b.shape[1]
    wd = mod_w.shape[2]
    ax, ay, ac = _place()
    chip = 2 * ax + ay
    dev = 2 * chip + ac
    chip_idx = jnp.reshape(chip, (1,)).astype(jnp.int32)
    core_idx = jnp.reshape(ac, (1,)).astype(jnp.int32)

    x2d = x[0]
    target = loss_target[0]

    w_names = ["a_w_in", "a_w_out", "b_w_in", "b_w_out"]
    w_kinds = ["col", "row", "col", "row"]
    w_shards = [a_w_in[0], a_w_out[0], b_w_in[0], b_w_out[0]]
    w_shapes = [sh.shape for sh in w_shards]
    placed = [_place_own(sh, kind, chip_idx, "place_" + nm) for nm, sh, kind in zip(w_names, w_shards, w_kinds)]
    a_groups, far_groups = [[(0, 0), (0, 1)]], [[(0, 2)]]
    whole_group = [[(0, 0), (0, 1), (0, 2)]]

    es = e // N_CHIP
    packed0, offs0 = _pack([c, a_conv_w, b_ln_g, b_ln_b], 1024)
    gathered0 = _all_gather(packed0, "gather_params").reshape(N_DEV, -1)
    c_all = gathered0[:, :d]
    per_chip = gathered0[0::2]

    def from_chips_cols(k, rows_):
        got = per_chip[:, offs0[k]:offs0[k] + rows_ * es].reshape(N_CHIP, rows_, es)
        return jnp.transpose(got, (1, 0, 2)).reshape(rows_, e)

    conv_w_full = from_chips_cols(1, 3)
    conv_w = [conv_w_full[k:k + 1] for k in range(3)]
    ln_g, ln_b = from_chips_cols(2, 1), from_chips_cols(3, 1)
    mod_b_cols = lax.dynamic_slice_in_dim(mod_b, chip * wd, wd, axis=1)[:, None, :]
    c_act, mod_part = _mod_fwd(c_all, mod_w, mod_b_cols, "mod_fwd")
    n_layer = mod_w.shape[0]
    mod_gathered = _all_gather(mod_part, "gather_mod")
    a_sems, (wa_in,), a_token = _gather_weights_start(
        placed[:1], w_kinds[:1], w_shapes[:1], a_groups, mod_gathered, "gather_a_w_in_start")
    mod_all = mod_gathered.reshape(N_CHIP, 2, n_layer, N_DEV, wd)[:, 0]
    mod_all = jnp.transpose(mod_all, (1, 2, 0, 3)).reshape(n_layer, N_DEV, N_CHIP * wd)
    mod_me = lax.dynamic_index_in_dim(mod_all, dev, axis=1, keepdims=False)
    shift = [mod_me[l:l + 1, 0:d] for l in range(n_layer)]
    scale = [mod_me[l:l + 1, d:2 * d] for l in range(n_layer)]
    gate = [mod_me[l:l + 1, 2 * d:3 * d] for l in range(n_layer)]

    g0, g1, gf = norm_g[0:1], norm_g[1:2], final_g[None, :]
    h0 = _norm_mod(x2d, g0, scale[0], shift[0], "norm_mod0")

    def slab(r):
        return jnp.bitwise_xor(chip_idx, r)

    def arrived(sems, group, weight, w, after, name):
        return _gather_weights_wait(sems, [(0, j) for _, j in group], [weight], [w_kinds[w]], [w_shapes[w]],
                                    after, name)[0]

    proj0 = _mm_proj_slab(h0, wa_in, None, slab(0), 4, "a_proj_own", a_token)
    wa_in = arrived(a_sems[0], a_groups[0], wa_in, 0, proj0, "gather_wait_near")
    far_sems, (wa_in,), far_token = _gather_weights_start(
        [wa_in], w_kinds[:1], w_shapes[:1], far_groups, wa_in, "gather_a_w_in_far_start")
    proj0 = _mm_proj_slab(h0, wa_in, proj0, slab(2), 4, "a_proj_x", far_token)
    proj0 = _mm_proj_slab(h0, wa_in, proj0, slab(1), 4, "a_proj_y", far_token)
    wa_in = arrived(far_sems[0], far_groups[0], wa_in, 0, proj0, "gather_wait_far")
    def start_whole(w, after, name):
        sems, (flight,), token = _gather_weights_start(
            placed[w:w + 1], w_kinds[w:w + 1], w_shapes[w:w + 1], whole_group, after, name)
        return sems[0], flight, token

    ao_sems, wa_out, ao_token = start_whole(1, wa_in, "gather_a_w_out_start")
    proj0 = _mm_proj_slab(h0, wa_in, proj0, slab(3), 4, "a_proj_far", ao_token)
    wa_out = arrived(ao_sems, whole_group[0], wa_out, 1, proj0, "gather_wait_a_w_out")
    bi_sems, wb_in, bi_token = start_whole(2, wa_out, "gather_b_w_in_start")
    y0, br0 = _conv_fwd(proj0, conv_w, a_conv_b, wa_out, "conv_fwd", bi_token)
    wb_in = arrived(bi_sems, whole_group[0], wb_in, 2, br0, "gather_wait_b_w_in")
    bo_sems, wb_out, bo_token = start_whole(3, wb_in, "gather_b_w_out_start")
    x1, h1 = _residual_norm_mod(x2d, br0, gate[0], g1, scale[1], shift[1], "residual_norm_mod1")
    proj1 = _mm_proj(h1, wb_in, 3, "b_proj", bo_token)
    b_s_t = jnp.transpose(b_b_s[0])
    wb_out = arrived(bo_sems, whole_group[0], wb_out, 3, proj1, "gather_wait_b_w_out")
    y1, br1 = _gmlp_fwd(proj1, ln_g, ln_b, b_w_s[0], b_s_t, wb_out, "gmlp_fwd")
    dx2, dbr1, loss_part, g_final_g, dgate1 = _head(x1, br1, gate[1], gf, target, "head")

    gw_b_out = _mm_dw_out(y1, dbr1, "b_out_dw")
    dproj1, g_w_s, g_b_s_t, g_ln_g, g_ln_b = _gmlp_bwd(
        proj1, dbr1, wb_out, ln_g, ln_b, b_w_s[0], jnp.swapaxes(b_w_s[0], 1, 2), b_s_t, "gmlp_bwd")
    gw_b_in = _mm_dw_in(h1, dproj1, "b_proj_dw")
    b_kinds = ["col", "row"]
    b_send, b_recv, b_grads, b_lands, b_token = _reduce_start(
        [gw_b_in, gw_b_out.reshape(N_CHIP, e // N_CHIP, d)], b_kinds, "reduce_b_start")
    dh1 = _mm_dh(dproj1, wb_in, "b_proj_dx", b_token)
    dx1, dshift1, dscale1, g_g1, dbr0, dgate0 = _norm_mod_bwd(
        dh1, x1, dx2, g1, scale[1], "norm_mod1_bwd", br=br0, gate=gate[0])

    gw_a_out = _mm_dw_out(y0, dbr0, "a_out_dw")
    dproj0, g_w0, g_w1, g_w2, g_conv_b = _conv_bwd(proj0, dbr0, wa_out, conv_w, a_conv_b, "conv_bwd")
    gw_a_in = _mm_dw_in(h0, dproj0, "a_proj_dw")
    a_kinds = ["col", "row"]
    a_send, a_recv, a_grads, a_lands, a_token = _reduce_start(
        [gw_a_in, gw_a_out.reshape(N_CHIP, e // N_CHIP, d)], a_kinds, "reduce_a_start")
    dh0 = _mm_dh(dproj0, wa_in, "a_proj_dx", a_token)
    grad_x, dshift0, dscale0, g_g0 = _norm_mod_bwd(dh0, x2d, dx1, g0, scale[0], "norm_mod0_bwd")

    dmod = jnp.concatenate([dshift0, dscale0, dgate0, dshift1, dscale1, dgate1], axis=1)
    small = [loss_part[0, 0:1], g_final_g, g_g0, g_g1, jnp.concatenate([g_w0, g_w1, g_w2], axis=0), g_conv_b, g_ln_g, g_ln_b,
             jnp.transpose(g_b_s_t), g_w_s, dmod]
    packed, offs = _pack(small, 1024)
    rows = packed.shape[0]
    gathered = _all_gather(packed, "gather_small")
    summed = _sum_devices(gathered, rows, "sum_small").reshape(-1)

    def take(k, shape):
        size = math.prod(shape)
        return summed[offs[k]:offs[k] + size].reshape(shape)

    loss = take(0, ())
    grad_final_g = take(1, (d,))
    grad_norm_g = jnp.concatenate([take(2, (1, d)), take(3, (1, d))], axis=0)
    grad_conv_w_full = take(4, (3, e))
    grad_a_conv_b = take(5, (1, e))
    grad_ln_g_full = take(6, (1, e))
    grad_ln_b_full = take(7, (1, e))
    grad_b_b_s = take(8, (1, GROUPS, CHUNK))
    grad_b_w_s = take(9, (1, GROUPS, CHUNK, CHUNK))
    grad_mod_b = take(10, (n_layer, 3 * d))
    grad_a_conv_w = lax.dynamic_slice_in_dim(grad_conv_w_full, chip * es, es, axis=1)[None]
    grad_b_ln_g = lax.dynamic_slice_in_dim(grad_ln_g_full, chip * es, es, axis=1)
    grad_b_ln_b = lax.dynamic_slice_in_dim(grad_ln_b_full, chip * es, es, axis=1)
    dmod_all = gathered.reshape(N_DEV, rows * 1024)[:, offs[10]:offs[10] + n_layer * 3 * d]
    dmod_all = dmod_all.reshape(N_DEV, n_layer, 3 * d)
    dmod_cols = jnp.transpose(lax.dynamic_slice_in_dim(dmod_all, chip * wd, wd, axis=2), (1, 0, 2))

    b_grads, b_lands = _reduce_wait(b_send, b_recv, b_grads, b_lands, b_kinds, summed, "reduce_b_wait")
    a_grads, a_lands = _reduce_wait(a_send, a_recv, a_grads, a_lands, a_kinds, summed, "reduce_a_wait")
    summed_halves = [
        _add_pieces(g, land, kind, chip_idx, core_idx, "add_pieces_" + nm)
        for g, land, kind, nm in [(a_grads[0], a_lands[0], "col", "a_w_in"), (b_grads[0], b_lands[0], "col", "b_w_in"),
                                  (a_grads[1], a_lands[1], "row", "a_w_out"), (b_grads[1], b_lands[1], "row", "b_w_out")]]
    g_a_w_in, g_b_w_in, g_a_w_out, g_b_w_out = _join_halves(summed_halves, "join_halves")

    grad_mod_w, delta_mod_w, new_m_mod_w, new_v_mod_w = _mod_w_update(
        jnp.transpose(c_act), dmod_cols, mod_w, m_mod_w, v_mod_w, "mod_w_update")
    upd = {}
    for nm, w, g, m, v in [("a_w_in", a_w_in, g_a_w_in, m_a_w_in, v_a_w_in), ("a_w_out", a_w_out, g_a_w_out, m_a_w_out, v_a_w_out),
                           ("b_w_in", b_w_in, g_b_w_in, m_b_w_in, v_b_w_in), ("b_w_out", b_w_out, g_b_w_out, m_b_w_out, v_b_w_out)]:
        upd[nm] = tuple(o[None] for o in _adamw_2d(w[0], g, m[0], v[0], "adamw_" + nm))
    small_w = [("mod_b", mod_b, grad_mod_b, m_mod_b, v_mod_b), ("norm_g", norm_g, grad_norm_g, m_norm_g, v_norm_g),
               ("a_conv_w", a_conv_w, grad_a_conv_w, m_a_conv_w, v_a_conv_w),
               ("a_conv_b", a_conv_b, grad_a_conv_b, m_a_conv_b, v_a_conv_b),
               ("b_ln_g", b_ln_g, grad_b_ln_g, m_b_ln_g, v_b_ln_g), ("b_ln_b", b_ln_b, grad_b_ln_b, m_b_ln_b, v_b_ln_b),
               ("b_w_s", b_w_s, grad_b_w_s, m_b_w_s, v_b_w_s), ("b_b_s", b_b_s, grad_b_b_s, m_b_b_s, v_b_b_s),
               ("final_g", final_g, grad_final_g, m_final_g, v_final_g)]

    def flat2d(a):
        return a.reshape(-1, a.shape[-1])

    res = _adamw_small([flat2d(t[1]) for t in small_w], [flat2d(t[2]) for t in small_w],
                       [flat2d(t[3]) for t in small_w], [flat2d(t[4]) for t in small_w], "adamw_small")
    for (nm, w, _, _, _), r3 in zip(small_w, res):
        upd[nm] = tuple(o.reshape(w.shape) for o in r3)
    upd["mod_w"] = (delta_mod_w, new_m_mod_w, new_v_mod_w)

    grads = {"mod_w": grad_mod_w, "mod_b": grad_mod_b, "norm_g": grad_norm_g, "a_w_in": g_a_w_in[None],
             "a_conv_w": grad_a_conv_w, "a_conv_b": grad_a_conv_b, "a_w_out": g_a_w_out[None], "b_w_in": g_b_w_in[None],
             "b_ln_g": grad_b_ln_g, "b_ln_b": grad_b_ln_b, "b_w_s": grad_b_w_s, "b_b_s": grad_b_b_s,
             "b_w_out": g_b_w_out[None], "final_g": grad_final_g}
    order = ["mod_w", "mod_b", "norm_g", "a_w_in", "a_conv_w", "a_conv_b", "a_w_out", "b_w_in", "b_ln_g", "b_ln_b",
             "b_w_s", "b_b_s", "b_w_out", "final_g"]
    return (loss, grad_x[None], *[grads[k] for k in order], *[upd[k][0] for k in order],
            *[upd[k][1] for k in order], *[upd[k][2] for k in order])
```

```python
import functools
import math

import jax
import jax.numpy as jnp
from jax import lax
from jax.experimental import pallas as pl
from jax.experimental.pallas import tpu as pltpu

F32 = jnp.float32
BF16 = jnp.bfloat16
MESH = pl.DeviceIdType.MESH

N_DEV = 8
N_CHIP = 4
SUBLANE = 8
RMS_EPS = 1e-6
LN_EPS = 1e-5
CHUNK = 128
GROUPS = 8
ADAM_LR = 0.001
ADAM_B1 = 0.9
ADAM_B2 = 0.999
ADAM_EPS = 1e-08
ADAM_WD = 0.01
ADAM_STEP = 10
VMEM_LIMIT = 56 << 20

HBM_SPEC = pl.BlockSpec(memory_space=pltpu.HBM)
VMEM_SPEC = pl.BlockSpec(memory_space=pltpu.VMEM)
SEM_SPEC = pl.BlockSpec(memory_space=pltpu.SEMAPHORE)
ANY_SPEC = pl.BlockSpec(memory_space=pl.ANY)
EFFECT = pltpu.SideEffectType.DATAFLOW_SIDE_EFFECTING


def _params(*sem):
    return pltpu.CompilerParams(dimension_semantics=sem, vmem_limit_bytes=VMEM_LIMIT)


def _tile(n, want):
    if n <= want:
        return n
    t = want
    while n % t:
        t -= 128
    return t


def _sigmoid(x):
    return 1.0 / (1.0 + jnp.exp(-x))


def _silu_and_grad(x):
    s = _sigmoid(x)
    return x * s, s * (1.0 + x * (1.0 - s))


def _gelu_and_grad(x):
    cdf = 0.5 * (1.0 + lax.erf(x * (1.0 / math.sqrt(2.0))))
    pdf = jnp.exp(-0.5 * x * x) * (1.0 / math.sqrt(2.0 * math.pi))
    return x * cdf, cdf + x * pdf


def _gelu(x):
    return x * (0.5 * (1.0 + lax.erf(x * (1.0 / math.sqrt(2.0)))))


def _rms(x):
    r = lax.rsqrt(jnp.mean(x * x, axis=-1, keepdims=True) + RMS_EPS)
    return x * r, r


def _rms_bwd(dxn, xn, r):
    return r * (dxn - xn * jnp.mean(dxn * xn, axis=-1, keepdims=True))


def _colsum(a):
    return jnp.sum(a, axis=0, keepdims=True)


def _shift_down(cur, before, k):
    rolled = pltpu.roll(cur, k, 0)
    row = lax.broadcasted_iota(jnp.int32, before.shape, 0)
    head = jnp.where(row < k, pltpu.roll(before, k, 0), rolled[:SUBLANE])
    return jnp.concatenate([head, rolled[SUBLANE:]], axis=0)


def _shift_up(cur, after, k):
    n = cur.shape[0]
    rolled = pltpu.roll(cur, n - k, 0)
    row = lax.broadcasted_iota(jnp.int32, after.shape, 0)
    tail = jnp.where(row >= SUBLANE - k, pltpu.roll(after, SUBLANE - k, 0), rolled[n - SUBLANE:])
    return jnp.concatenate([rolled[:n - SUBLANE], tail], axis=0)


def _after_spec():
    return pl.BlockSpec((SUBLANE, 128), lambda *_: (0, 0))


def _mm_proj(h, w, n_split, name, after):
    s, d = h.shape
    e = w.shape[1] // n_split
    tm, tn = _tile(s, 1024), _tile(e, 2048)
    nj = e // tn

    def body(h_ref, w_ref, after_ref, o_ref):
        o_ref[...] = jnp.dot(h_ref[...], w_ref[...], preferred_element_type=F32)

    return pl.pallas_call(
        body, name=name,
        out_shape=jax.ShapeDtypeStruct((n_split, s, e), F32),
        grid=(s // tm, n_split * nj),
        in_specs=[pl.BlockSpec((tm, d), lambda i, j: (i, 0)), pl.BlockSpec((d, tn), lambda i, j: (0, j)),
                  _after_spec()],
        out_specs=pl.BlockSpec((None, tm, tn), lambda i, j: (j // nj, i, j % nj)),
        compiler_params=_params("parallel", "parallel"),
    )(h, w, after)


def _mm_proj_slab(h, w, proj, q_idx, n_split, name, after):
    s, d = h.shape
    e = w.shape[1] // n_split
    tm, tn = _tile(s, 1024), _tile(e, 2048)
    nj = e // tn

    def body(q_ref, h_ref, w_ref, *rest):
        o_ref = rest[-1]
        o_ref[...] = jnp.dot(h_ref[...], w_ref[...], preferred_element_type=F32)

    in_specs = [pl.BlockSpec((tm, d), lambda i, j, qr: (i, 0)), pl.BlockSpec((d, tn), lambda i, j, qr: (0, qr[0] * nj + j)),
                _after_spec()]
    args = [q_idx, h, w, after]
    aliases = {}
    if proj is not None:
        in_specs.append(ANY_SPEC)
        args.append(proj)
        aliases = {4: 0}
    return pl.pallas_call(
        body, name=name,
        out_shape=jax.ShapeDtypeStruct((n_split, s, e), F32),
        grid_spec=pltpu.PrefetchScalarGridSpec(
            num_scalar_prefetch=1, grid=(s // tm, nj), in_specs=in_specs,
            out_specs=pl.BlockSpec((None, tm, tn), lambda i, j, qr: (qr[0], i, j))),
        input_output_aliases=aliases,
        compiler_params=_params("parallel", "parallel"),
    )(*args)


def _mm_dh(dp, w, name, after):
    nq, s, e = dp.shape
    d = w.shape[0]
    tm, tn, tk = _tile(s, 1024), _tile(d, 1024), _tile(e, 2048)
    nkq = e // tk
    nk = nq * nkq

    def body(a_ref, b_ref, after_ref, o_ref):
        @pl.when(pl.program_id(2) == 0)
        def _():
            o_ref[...] = jnp.zeros_like(o_ref)
        o_ref[...] += lax.dot_general(a_ref[...], b_ref[...], (((1,), (1,)), ((), ())), preferred_element_type=F32)

    return pl.pallas_call(
        body, name=name,
        out_shape=jax.ShapeDtypeStruct((s, d), F32),
        grid=(s // tm, d // tn, nk),
        in_specs=[pl.BlockSpec((None, tm, tk), lambda i, j, k: (k // nkq, i, k % nkq)),
                  pl.BlockSpec((tn, tk), lambda i, j, k: (j, k)), _after_spec()],
        out_specs=pl.BlockSpec((tm, tn), lambda i, j, k: (i, j)),
        compiler_params=_params("parallel", "parallel", "arbitrary"),
    )(dp, w, after)


def _dw_body(n_t):
    def body(a_ref, b_ref, o_ref, acc_ref):
        t = pl.program_id(2)

        @pl.when(t == 0)
        def _():
            acc_ref[...] = jnp.zeros_like(acc_ref)
        acc_ref[...] += lax.dot_general(a_ref[...], b_ref[...], (((0,), (0,)), ((), ())), preferred_element_type=F32)

        @pl.when(t == n_t - 1)
        def _():
            o_ref[...] = acc_ref[...].astype(o_ref.dtype)
    return body


def _mm_dw_in(h, dp, name):
    s, d = h.shape
    nq, _, e = dp.shape
    tm, tn, tt = _tile(d, 1024), _tile(e, 1024), _tile(s, 2048)
    nj = e // tn

    return pl.pallas_call(
        _dw_body(s // tt), name=name,
        out_shape=jax.ShapeDtypeStruct((d, nq * e), BF16),
        grid=(d // tm, nq * nj, s // tt),
        in_specs=[pl.BlockSpec((tt, tm), lambda i, j, t: (t, i)),
                  pl.BlockSpec((None, tt, tn), lambda i, j, t: (j // nj, t, j % nj))],
        out_specs=pl.BlockSpec((tm, tn), lambda i, j, t: (i, j)),
        scratch_shapes=[pltpu.VMEM((tm, tn), F32)],
        compiler_params=_params("parallel", "parallel", "arbitrary"),
    )(h, dp)


def _mm_dw_out(y, dbr, name):
    s, e = y.shape
    d = dbr.shape[1]
    tm, tn, tt = _tile(e, 1024), _tile(d, 1024), _tile(s, 2048)

    return pl.pallas_call(
        _dw_body(s // tt), name=name,
        out_shape=jax.ShapeDtypeStruct((e, d), BF16),
        grid=(e // tm, d // tn, s // tt),
        in_specs=[pl.BlockSpec((tt, tm), lambda i, j, t: (t, i)), pl.BlockSpec((tt, tn), lambda i, j, t: (t, j))],
        out_specs=pl.BlockSpec((tm, tn), lambda i, j, t: (i, j)),
        scratch_shapes=[pltpu.VMEM((tm, tn), F32)],
        compiler_params=_params("parallel", "parallel", "arbitrary"),
    )(y, dbr)


def _row_spec(ts, d):
    return pl.BlockSpec((ts, d), lambda i: (i, 0))


def _vec_spec(d):
    return pl.BlockSpec((1, d), lambda i: (0, 0))


def _norm_mod(x, g, scale, shift, name):
    s, d = x.shape
    ts = _tile(s, 512)

    def body(x_ref, g_ref, sc_ref, sh_ref, h_ref):
        xn, _ = _rms(x_ref[...])
        h_ref[...] = ((xn * g_ref[...]) * (1.0 + sc_ref[...]) + sh_ref[...]).astype(BF16)

    return pl.pallas_call(
        body, name=name, out_shape=jax.ShapeDtypeStruct((s, d), BF16), grid=(s // ts,),
        in_specs=[_row_spec(ts, d), _vec_spec(d), _vec_spec(d), _vec_spec(d)],
        out_specs=_row_spec(ts, d), compiler_params=_params("parallel"),
    )(x, g, scale, shift)


def _residual_norm_mod(x, br, gate, g, scale, shift, name):
    s, d = x.shape
    ts = _tile(s, 512)

    def body(x_ref, br_ref, gate_ref, g_ref, sc_ref, sh_ref, x1_ref, h_ref):
        x1 = x_ref[...] + gate_ref[...] * br_ref[...]
        x1_ref[...] = x1
        xn, _ = _rms(x1)
        h_ref[...] = ((xn * g_ref[...]) * (1.0 + sc_ref[...]) + sh_ref[...]).astype(BF16)

    return pl.pallas_call(
        body, name=name,
        out_shape=(jax.ShapeDtypeStruct((s, d), F32), jax.ShapeDtypeStruct((s, d), BF16)), grid=(s // ts,),
        in_specs=[_row_spec(ts, d), _row_spec(ts, d)] + [_vec_spec(d)] * 4,
        out_specs=(_row_spec(ts, d), _row_spec(ts, d)), compiler_params=_params("parallel"),
    )(x, br, gate, g, scale, shift)


def _head(x1, br, gate, gf, target, name):
    s, d = x1.shape
    ts = _tile(s, 512)

    def body(x1_ref, br_ref, gate_ref, gf_ref, tg_ref, dx_ref, dbr_ref, loss_ref, dgf_ref, dgate_ref):
        @pl.when(pl.program_id(0) == 0)
        def _():
            loss_ref[...] = jnp.zeros_like(loss_ref)
            dgf_ref[...] = jnp.zeros_like(dgf_ref)
            dgate_ref[...] = jnp.zeros_like(dgate_ref)
        br = br_ref[...]
        gate = gate_ref[...]
        gf = gf_ref[...]
        xn, r = _rms(x1_ref[...] + gate * br)
        err = xn * gf - tg_ref[...]
        loss_ref[...] += 0.5 * jnp.sum(jnp.mean(err * err, axis=-1, keepdims=True))
        dout = err * (1.0 / d)
        dgf_ref[...] += _colsum(dout * xn)
        dx = _rms_bwd(dout * gf, xn, r)
        dx_ref[...] = dx
        dgate_ref[...] += _colsum(dx * br)
        dbr_ref[...] = (gate * dx).astype(BF16)

    return pl.pallas_call(
        body, name=name,
        out_shape=(jax.ShapeDtypeStruct((s, d), F32), jax.ShapeDtypeStruct((s, d), BF16),
                   jax.ShapeDtypeStruct((SUBLANE, 128), F32), jax.ShapeDtypeStruct((1, d), F32),
                   jax.ShapeDtypeStruct((1, d), F32)),
        grid=(s // ts,),
        in_specs=[_row_spec(ts, d), _row_spec(ts, d), _vec_spec(d), _vec_spec(d), _row_spec(ts, d)],
        out_specs=(_row_spec(ts, d), _row_spec(ts, d), pl.BlockSpec((SUBLANE, 128), lambda i: (0, 0)),
                   _vec_spec(d), _vec_spec(d)),
        compiler_params=_params("arbitrary"),
    )(x1, br, gate, gf, target)


def _norm_mod_bwd(dh, x, dx_in, g, scale, name, br=None, gate=None):
    s, d = x.shape
    ts = _tile(s, 512)
    has_branch = br is not None

    def body(*refs):
        if has_branch:
            (dh_ref, x_ref, dxin_ref, g_ref, sc_ref, br_ref, gate_ref,
             dx_ref, dsh_ref, dsc_ref, dg_ref, dbr_ref, dgate_ref) = refs
        else:
            dh_ref, x_ref, dxin_ref, g_ref, sc_ref, dx_ref, dsh_ref, dsc_ref, dg_ref = refs

        @pl.when(pl.program_id(0) == 0)
        def _():
            dsh_ref[...] = jnp.zeros_like(dsh_ref)
            dsc_ref[...] = jnp.zeros_like(dsc_ref)
            dg_ref[...] = jnp.zeros_like(dg_ref)
            if has_branch:
                dgate_ref[...] = jnp.zeros_like(dgate_ref)
        dh = dh_ref[...]
        g = g_ref[...]
        xn, r = _rms(x_ref[...])
        dsh_ref[...] += _colsum(dh)
        dsc_ref[...] += _colsum(dh * (xn * g))
        da = dh * (1.0 + sc_ref[...])
        dg_ref[...] += _colsum(da * xn)
        dx = dxin_ref[...] + _rms_bwd(da * g, xn, r)
        dx_ref[...] = dx
        if has_branch:
            dgate_ref[...] += _colsum(dx * br_ref[...])
            dbr_ref[...] = (gate_ref[...] * dx).astype(BF16)

    vec_out = jax.ShapeDtypeStruct((1, d), F32)
    out_shape = [jax.ShapeDtypeStruct((s, d), F32), vec_out, vec_out, vec_out]
    out_specs = [_row_spec(ts, d), _vec_spec(d), _vec_spec(d), _vec_spec(d)]
    in_specs = [_row_spec(ts, d)] * 3 + [_vec_spec(d)] * 2
    args = [dh, x, dx_in, g, scale]
    if has_branch:
        out_shape += [jax.ShapeDtypeStruct((s, d), BF16), vec_out]
        out_specs += [_row_spec(ts, d), _vec_spec(d)]
        in_specs += [_row_spec(ts, d), _vec_spec(d)]
        args += [br, gate]
    return pl.pallas_call(
        body, name=name, out_shape=tuple(out_shape), grid=(s // ts,),
        in_specs=in_specs, out_specs=tuple(out_specs), compiler_params=_params("arbitrary"),
    )(*args)


def _conv_tiles(s, e):
    return _tile(s, 512), _tile(e, 512)


def _conv_fwd(proj, conv_w, conv_b, w_out, name, after):
    _, s, e = proj.shape
    d = w_out.shape[1]
    ts, te = _tile(s, 256), e
    hb = ts // SUBLANE

    def body(bg_ref, cg_ref, xi_ref, z_ref, cgp_ref, xip_ref, w0_ref, w1_ref, w2_ref, b_ref, wo_ref, after_ref,
             y_ref, br_ref):
        cx = cg_ref[...] * xi_ref[...]
        before = jnp.where(pl.program_id(0) > 0, cgp_ref[...] * xip_ref[...], 0.0)
        conv = b_ref[...] + w2_ref[...] * cx
        conv = conv + w0_ref[...] * _shift_down(cx, before, 2)
        conv = conv + w1_ref[...] * _shift_down(cx, before, 1)
        z = z_ref[...]
        y = ((z * _sigmoid(z)) * bg_ref[...] * conv).astype(BF16)
        y_ref[...] = y
        br_ref[...] = jnp.dot(y, wo_ref[...], preferred_element_type=F32)

    def part(q):
        return pl.BlockSpec((None, ts, te), lambda i: (q, i, 0))

    def halo_before(q):
        return pl.BlockSpec((None, SUBLANE, te), lambda i: (q, jnp.maximum(i * hb - 1, 0), 0))

    return pl.pallas_call(
        body, name=name,
        out_shape=(jax.ShapeDtypeStruct((s, e), BF16), jax.ShapeDtypeStruct((s, d), F32)), grid=(s // ts,),
        in_specs=[part(0), part(1), part(2), part(3), halo_before(1), halo_before(2)]
        + [_vec_spec(e)] * 4 + [pl.BlockSpec((e, d), lambda i: (0, 0)), _after_spec()],
        out_specs=(_row_spec(ts, e), _row_spec(ts, d)),
        compiler_params=_params("parallel"),
    )(proj, proj, proj, proj, proj, proj, *conv_w, conv_b, w_out, after)


def _conv_bwd(proj, dbr, w_out, conv_w, conv_b, name):
    _, s, e = proj.shape
    d = dbr.shape[1]
    ts, te = _conv_tiles(s, e)
    hb = ts // SUBLANE
    n_i = s // ts
    last_halo = s // SUBLANE - 1
    pack = 2 * SUBLANE
    last_pack = s // pack - 1
    nt = (((1,), (1,)), ((), ()))

    def body(bg_ref, cg_ref, xi_ref, z_ref, dbr_ref, cgp_ref, xip_ref, bgn_ref, zn_ref, dbrn_ref, wo_ref,
             w0_ref, w1_ref, w2_ref, b_ref, dp_ref, dw0_ref, dw1_ref, dw2_ref, db_ref):
        i = pl.program_id(1)

        @pl.when(i == 0)
        def _():
            for acc in (dw0_ref, dw1_ref, dw2_ref, db_ref):
                acc[...] = jnp.zeros_like(acc)
        wo = wo_ref[...]
        dy = lax.dot_general(dbr_ref[...], wo, nt, preferred_element_type=F32)
        dyn = lax.dot_general(dbrn_ref[...], wo, nt, preferred_element_type=F32)[:SUBLANE]
        bg, cg, xi, z = bg_ref[...], cg_ref[...], xi_ref[...], z_ref[...]
        w0, w1, w2 = w0_ref[...], w1_ref[...], w2_ref[...]
        cx = cg * xi
        before = jnp.where(i > 0, cgp_ref[...] * xip_ref[...], 0.0)
        cx1 = _shift_down(cx, before, 1)
        cx2 = _shift_down(cx, before, 2)
        conv = b_ref[...] + w2 * cx
        conv = conv + w0 * cx2
        conv = conv + w1 * cx1
        sz, dsz = _silu_and_grad(z)
        dp_ref[3] = (dy * bg * conv * dsz).astype(BF16)
        dp_ref[0] = (dy * sz * conv).astype(BF16)
        dconv = dy * sz * bg
        zn = zn_ref[...]
        after = jnp.where(i < n_i - 1, dyn * (zn * _sigmoid(zn)) * bgn_ref[...], 0.0)
        db_ref[...] += _colsum(dconv)
        dw2_ref[...] += _colsum(dconv * cx)
        dw1_ref[...] += _colsum(dconv * cx1)
        dw0_ref[...] += _colsum(dconv * cx2)
        dcx = w2 * dconv + w1 * _shift_up(dconv, after, 1) + w0 * _shift_up(dconv, after, 2)
        dp_ref[1] = (dcx * xi).astype(BF16)
        dp_ref[2] = (dcx * cg).astype(BF16)

    def part(q):
        return pl.BlockSpec((None, ts, te), lambda j, i: (q, i, j))

    def halo_before(q):
        return pl.BlockSpec((None, SUBLANE, te), lambda j, i: (q, jnp.maximum(i * hb - 1, 0), j))

    def halo_after(q):
        return pl.BlockSpec((None, SUBLANE, te), lambda j, i: (q, jnp.minimum((i + 1) * hb, last_halo), j))

    return pl.pallas_call(
        body, name=name,
        out_shape=(jax.ShapeDtypeStruct((4, s, e), BF16),) + (jax.ShapeDtypeStruct((1, e), F32),) * 4,
        grid=(e // te, n_i),
        in_specs=[part(0), part(1), part(2), part(3), pl.BlockSpec((ts, d), lambda j, i: (i, 0)),
                  halo_before(1), halo_before(2), halo_after(0), halo_after(3),
                  pl.BlockSpec((pack, d), lambda j, i: (jnp.minimum((i + 1) * (ts // pack), last_pack), 0)),
                  pl.BlockSpec((te, d), lambda j, i: (j, 0))]
        + [pl.BlockSpec((1, te), lambda j, i: (0, j))] * 4,
        out_specs=(pl.BlockSpec((4, ts, te), lambda j, i: (0, i, j)),) + (pl.BlockSpec((1, te), lambda j, i: (0, j)),) * 4,
        compiler_params=_params("parallel", "arbitrary"),
    )(proj, proj, proj, proj, dbr, proj, proj, proj, proj, dbr, w_out, *conv_w, conv_b)


def _tril(w):
    row = lax.broadcasted_iota(jnp.int32, w.shape, 0)
    col = lax.broadcasted_iota(jnp.int32, w.shape, 1)
    return jnp.where(row >= col, w, 0.0)


def _triu(w):
    row = lax.broadcasted_iota(jnp.int32, w.shape, 0)
    col = lax.broadcasted_iota(jnp.int32, w.shape, 1)
    return jnp.where(row <= col, w, 0.0)


def _layer_norm_fwd(v, g, b):
    mu = jnp.mean(v, axis=-1, keepdims=True)
    vc = v - mu
    rstd = lax.rsqrt(jnp.mean(vc * vc, axis=-1, keepdims=True) + LN_EPS)
    vhat = vc * rstd
    return vhat * g + b, vhat, rstd


GMLP_CHUNKS_PER_STEP = 2


def _gmlp_rows(s):
    return CHUNK * min(GMLP_CHUNKS_PER_STEP, s // CHUNK)


def _mix_positions(w_ref, src_scr, dst_scr, gw, mask, bias_ref=None):
    for gi in range(GROUPS):
        cols = slice(gi * gw, (gi + 1) * gw)
        wm = mask(w_ref[gi]).astype(BF16)
        for n in range(src_scr.shape[0] // CHUNK):
            rows = slice(n * CHUNK, (n + 1) * CHUNK)
            out = jnp.dot(wm, src_scr[rows, cols], preferred_element_type=F32)
            if bias_ref is not None:
                out = out + bias_ref[:, gi:gi + 1]
            dst_scr[rows, cols] = out


def _gmlp_fwd(proj, ln_g, ln_b, w_s, b_s_t, w_out, name):
    _, s, e = proj.shape
    d = w_out.shape[1]
    gw = e // GROUPS
    ts = _gmlp_rows(s)

    def body(pu_ref, pv_ref, pz_ref, g_ref, b_ref, ws_ref, bs_ref, wo_ref, y_ref, br_ref, vn_scr, mix_scr):
        vn, _, _ = _layer_norm_fwd(_gelu(pv_ref[...]), g_ref[...], b_ref[...])
        vn_scr[...] = vn.astype(BF16)
        _mix_positions(ws_ref, vn_scr, mix_scr, gw, _tril, bs_ref)
        z = pz_ref[...]
        y = ((z * _sigmoid(z)) * (_gelu(pu_ref[...]) * mix_scr[...])).astype(BF16)
        y_ref[...] = y
        br_ref[...] = jnp.dot(y, wo_ref[...], preferred_element_type=F32)

    def part(q):
        return pl.BlockSpec((None, ts, e), lambda i: (q, i, 0))

    return pl.pallas_call(
        body, name=name,
        out_shape=(jax.ShapeDtypeStruct((s, e), BF16), jax.ShapeDtypeStruct((s, d), F32)), grid=(s // ts,),
        in_specs=[part(0), part(1), part(2), _vec_spec(e), _vec_spec(e),
                  pl.BlockSpec((GROUPS, CHUNK, CHUNK), lambda i: (0, 0, 0)),
                  pl.BlockSpec((CHUNK, GROUPS), lambda i: (0, 0)), pl.BlockSpec((e, d), lambda i: (0, 0))],
        out_specs=(pl.BlockSpec((ts, e), lambda i: (i, 0)), pl.BlockSpec((ts, d), lambda i: (i, 0))),
        scratch_shapes=[pltpu.VMEM((ts, e), BF16), pltpu.VMEM((ts, e), F32)],
        compiler_params=_params("parallel"),
    )(proj, proj, proj, ln_g, ln_b, w_s, b_s_t, w_out)


def _gmlp_bwd(proj, dbr, w_out, ln_g, ln_b, w_s, w_s_t, b_s_t, name):
    _, s, e = proj.shape
    d = dbr.shape[1]
    gw = e // GROUPS
    ts = _gmlp_rows(s)
    n_i = s // ts

    def body(pu_ref, pv_ref, pz_ref, dbr_ref, wo_ref, g_ref, b_ref, ws_ref, wst_ref, bs_ref,
             dp_ref, dws_ref, dbs_ref, dlg_ref, dlb_ref, vn_scr, mix_scr, dm_scr, dvn_scr, dmacc_scr):
        i = pl.program_id(0)

        @pl.when(i == 0)
        def _():
            dws_ref[...] = jnp.zeros_like(dws_ref)
            dlg_ref[...] = jnp.zeros_like(dlg_ref)
            dlb_ref[...] = jnp.zeros_like(dlb_ref)
            dmacc_scr[...] = jnp.zeros_like(dmacc_scr)
        ln_g = g_ref[...]
        u, du_dpu = _gelu_and_grad(pu_ref[...])
        v, dv_dpv = _gelu_and_grad(pv_ref[...])
        vn, vhat, rstd = _layer_norm_fwd(v, ln_g, b_ref[...])
        vn_scr[...] = vn.astype(BF16)
        _mix_positions(ws_ref, vn_scr, mix_scr, gw, _tril, bs_ref)
        mixed = mix_scr[...]
        dy = lax.dot_general(dbr_ref[...], wo_ref[...], (((1,), (1,)), ((), ())), preferred_element_type=F32)
        sz, dsz = _silu_and_grad(pz_ref[...])
        ds = dy * sz
        dp_ref[2] = (dy * (u * mixed) * dsz).astype(BF16)
        dp_ref[0] = (ds * mixed * du_dpu).astype(BF16)
        dm = ds * u
        dm_scr[...] = dm.astype(BF16)
        for n in range(ts // CHUNK):
            rows = slice(n * CHUNK, (n + 1) * CHUNK)
            dmacc_scr[...] += dm[rows]
            for gi in range(GROUPS):
                cols = slice(gi * gw, (gi + 1) * gw)
                dws_ref[gi] += lax.dot_general(dm_scr[rows, cols], vn_scr[rows, cols], (((1,), (1,)), ((), ())),
                                               preferred_element_type=F32)
        _mix_positions(wst_ref, dm_scr, dvn_scr, gw, _triu)
        dvn = dvn_scr[...]
        dlg_ref[...] += _colsum(dvn * vhat)
        dlb_ref[...] += _colsum(dvn)
        dvh = dvn * ln_g
        dv = rstd * (dvh - jnp.mean(dvh, axis=-1, keepdims=True) - vhat * jnp.mean(dvh * vhat, axis=-1, keepdims=True))
        dp_ref[1] = (dv * dv_dpv).astype(BF16)

        @pl.when(i == n_i - 1)
        def _():
            for gi in range(GROUPS):
                dws_ref[gi] = _tril(dws_ref[gi])
                dbs_ref[:, gi:gi + 1] = jnp.sum(dmacc_scr[:, gi * gw:(gi + 1) * gw], axis=1, keepdims=True)

    def part(q):
        return pl.BlockSpec((None, ts, e), lambda i: (q, i, 0))

    w_spec = pl.BlockSpec((GROUPS, CHUNK, CHUNK), lambda i: (0, 0, 0))
    bs_spec = pl.BlockSpec((CHUNK, GROUPS), lambda i: (0, 0))
    return pl.pallas_call(
        body, name=name,
        out_shape=(jax.ShapeDtypeStruct((3, s, e), BF16), jax.ShapeDtypeStruct((GROUPS, CHUNK, CHUNK), F32),
                   jax.ShapeDtypeStruct((CHUNK, GROUPS), F32), jax.ShapeDtypeStruct((1, e), F32),
                   jax.ShapeDtypeStruct((1, e), F32)),
        grid=(n_i,),
        in_specs=[part(0), part(1), part(2), pl.BlockSpec((ts, d), lambda i: (i, 0)),
                  pl.BlockSpec((e, d), lambda i: (0, 0)), _vec_spec(e), _vec_spec(e), w_spec, w_spec, bs_spec],
        out_specs=(pl.BlockSpec((3, ts, e), lambda i: (0, i, 0)), w_spec, bs_spec, _vec_spec(e), _vec_spec(e)),
        scratch_shapes=[pltpu.VMEM((ts, e), BF16), pltpu.VMEM((ts, e), F32), pltpu.VMEM((ts, e), BF16),
                        pltpu.VMEM((ts, e), F32), pltpu.VMEM((CHUNK, e), F32)],
        compiler_params=_params("arbitrary"),
    )(proj, proj, proj, dbr, w_out, ln_g, ln_b, w_s, w_s_t, b_s_t)


def _mod_fwd(c_all, mod_w, mod_b_cols, name):
    n_layer, d, w = mod_w.shape

    def body(c_ref, w_ref, b_ref, ca_ref, o_ref):
        c = c_ref[...]
        ca = c * _sigmoid(c)
        ca_ref[...] = ca
        for li in range(n_layer):
            o_ref[li * N_DEV:(li + 1) * N_DEV, :] = (
                jnp.dot(ca, w_ref[li], preferred_element_type=F32, precision=lax.Precision.HIGHEST) + b_ref[li])

    return pl.pallas_call(
        body, name=name,
        out_shape=(jax.ShapeDtypeStruct((N_DEV, d), F32), jax.ShapeDtypeStruct((n_layer * N_DEV, w), F32)),
        in_specs=[VMEM_SPEC] * 3, out_specs=(VMEM_SPEC, VMEM_SPEC),
        compiler_params=pltpu.CompilerParams(vmem_limit_bytes=VMEM_LIMIT),
    )(c_all, mod_w, mod_b_cols)


def _adamw(w, g, m, v):
    m = ADAM_B1 * m + (1.0 - ADAM_B1) * g
    v = ADAM_B2 * v + (1.0 - ADAM_B2) * (g * g)
    m_hat = m / (1.0 - ADAM_B1 ** ADAM_STEP)
    v_hat = v / (1.0 - ADAM_B2 ** ADAM_STEP)
    delta = -ADAM_LR * (m_hat / (jnp.sqrt(v_hat) + ADAM_EPS) + ADAM_WD * w)
    return delta, m, v


def _adamw_2d(w, g, m, v, name):
    r, c = w.shape
    tr, tc = _tile(r, 512), _tile(c, 1024)

    def body(w_ref, g_ref, m_ref, v_ref, d_ref, nm_ref, nv_ref):
        d_ref[...], nm_ref[...], nv_ref[...] = _adamw(w_ref[...], g_ref[...], m_ref[...], v_ref[...])

    spec = pl.BlockSpec((tr, tc), lambda i, j: (i, j))
    shape = jax.ShapeDtypeStruct((r, c), F32)
    return pl.pallas_call(
        body, name=name, out_shape=(shape, shape, shape), grid=(r // tr, c // tc),
        in_specs=[spec] * 4, out_specs=(spec, spec, spec), compiler_params=_params("parallel", "parallel"),
    )(w, g, m, v)


def _mod_w_update(ca_t, dmod_cols, w, m, v, name):
    n_layer, d, wd = w.shape
    tr = _tile(d, 256)

    def body(ca_ref, dm_ref, w_ref, m_ref, v_ref, g_ref, d_ref, nm_ref, nv_ref):
        ca = ca_ref[...]
        dm = dm_ref[...]
        g = ca[:, 0:1] * dm[0:1, :]
        for b in range(1, N_DEV):
            g = g + ca[:, b:b + 1] * dm[b:b + 1, :]
        g_ref[...] = g
        d_ref[...], nm_ref[...], nv_ref[...] = _adamw(w_ref[...], g, m_ref[...], v_ref[...])

    spec = pl.BlockSpec((None, tr, wd), lambda l, i: (l, i, 0))
    shape = jax.ShapeDtypeStruct((n_layer, d, wd), F32)
    return pl.pallas_call(
        body, name=name, out_shape=(shape,) * 4, grid=(n_layer, d // tr),
        in_specs=[pl.BlockSpec((tr, N_DEV), lambda l, i: (i, 0)), pl.BlockSpec((None, N_DEV, wd), lambda l, i: (l, 0, 0)),
                  spec, spec, spec],
        out_specs=(spec,) * 4, compiler_params=_params("parallel", "parallel"),
    )(ca_t, dmod_cols, w, m, v)


def _adamw_small(ws, gs, ms, vs, name):
    n = len(ws)

    def body(*refs):
        ins, outs = refs[:4 * n], refs[4 * n:]
        for k in range(n):
            delta, nm, nv = _adamw(ins[k][...], ins[n + k][...], ins[2 * n + k][...], ins[3 * n + k][...])
            outs[3 * k][...] = delta
            outs[3 * k + 1][...] = nm
            outs[3 * k + 2][...] = nv

    out_shape = []
    for w in ws:
        out_shape += [jax.ShapeDtypeStruct(w.shape, F32)] * 3
    outs = pl.pallas_call(
        body, name=name, out_shape=tuple(out_shape),
        in_specs=[VMEM_SPEC] * (4 * n), out_specs=tuple([VMEM_SPEC] * (3 * n)),
        compiler_params=pltpu.CompilerParams(vmem_limit_bytes=VMEM_LIMIT),
    )(*ws, *gs, *ms, *vs)
    return [tuple(outs[3 * k:3 * k + 3]) for k in range(n)]


def _place():
    return lax.axis_index("x"), lax.axis_index("y"), lax.axis_index("c")


def _other_chips(x, y):
    return [(1 - x, y), (x, 1 - y), (1 - x, 1 - y)]


def _all_gather(block, name):
    m_per, n = block.shape

    def body(x_ref, out_ref, send_sems, recv_sems, local_sem):
        x, y, c = _place()
        me, sibling = (x, y, c), (x, y, 1 - c)
        chips = _other_chips(x, y)

        def rows(px, py, pc):
            return out_ref.at[pl.ds((4 * px + 2 * py + pc) * m_per, m_per), :]

        def copy(k, blk, to, src=None):
            return pltpu.make_async_remote_copy(
                src_ref=rows(*blk) if src is None else src, dst_ref=rows(*blk),
                send_sem=send_sems.at[k], recv_sem=recv_sems.at[k], device_id=to, device_id_type=MESH)

        mine = pltpu.make_async_copy(x_ref, rows(*me), local_sem)
        mine.start()
        first = [copy(0, me, sibling, src=x_ref)]
        first += [copy(1 + j, me, (*chip, c), src=x_ref) for j, chip in enumerate(chips)]
        for cp in first:
            cp.start()
        passed = [copy(4 + j, (*chip, c), sibling) for j, chip in enumerate(chips)]
        for j, chip in enumerate(chips):
            copy(1 + j, (*chip, c), me).wait_recv()
            passed[j].start()
        copy(0, sibling, me).wait_recv()
        for j, chip in enumerate(chips):
            copy(4 + j, (*chip, 1 - c), me).wait_recv()
        for cp in first + passed:
            cp.wait_send()
        mine.wait()

    return pl.pallas_call(
        body, name=name, out_shape=jax.ShapeDtypeStruct((N_DEV * m_per, n), F32),
        in_specs=[VMEM_SPEC], out_specs=VMEM_SPEC,
        scratch_shapes=[pltpu.SemaphoreType.DMA((7,)), pltpu.SemaphoreType.DMA((7,)), pltpu.SemaphoreType.DMA],
        compiler_params=pltpu.CompilerParams(vmem_limit_bytes=VMEM_LIMIT),
    )(block)


def _hbm(a):
    return pltpu.with_memory_space_constraint(a, pltpu.HBM)


def _place_own(shard, kind, chip_idx, name):
    r, cdim = shard.shape
    tr, tc = _tile(r, 512), _tile(cdim, 1024)
    nrb, ncb = r // tr, cdim // tc

    def body(k_ref, s_ref, o_ref):
        o_ref[...] = s_ref[...].astype(BF16)

    if kind == "col":
        full, o_map = (r, N_CHIP * cdim), lambda i, j, kr: (i, kr[0] * ncb + j)
    else:
        full, o_map = (N_CHIP * r, cdim), lambda i, j, kr: (kr[0] * nrb + i, j)
    return pl.pallas_call(
        body, name=name, out_shape=jax.ShapeDtypeStruct(full, BF16),
        grid_spec=pltpu.PrefetchScalarGridSpec(
            num_scalar_prefetch=1, grid=(nrb, ncb),
            in_specs=[pl.BlockSpec((tr, tc), lambda i, j, kr: (i, j))],
            out_specs=pl.BlockSpec((tr, tc), o_map)),
        compiler_params=_params("parallel", "parallel"),
    )(chip_idx, shard)


def _weight_window(ref, kind, shard_shape, k, half):
    r, cdim = shard_shape
    hr = r // 2
    if kind == "col":
        return ref.at[pl.ds(half * hr, hr), pl.ds(pl.multiple_of(k * cdim, 128), cdim)]
    return ref.at[pl.ds(pl.multiple_of(k * r + half * hr, 2 * SUBLANE), hr), :]


def _gather_weights_start(fulls, kinds, shard_shapes, groups, after, name):
    n, ng = len(fulls), len(groups)

    def body(*refs):
        ins = refs[:n]
        sems = refs[n + 1:n + 1 + 2 * ng]
        token = refs[2 * n + 1 + 2 * ng]
        x, y, c = _place()
        chips = _other_chips(x, y)
        for g, group in enumerate(groups):
            for pos, (w, j) in enumerate(group):
                own = _weight_window(ins[w], kinds[w], shard_shapes[w], 2 * x + y, c)
                for cc in range(2):
                    pltpu.make_async_remote_copy(
                        src_ref=own, dst_ref=own,
                        send_sem=sems[2 * g].at[2 * pos + cc], recv_sem=sems[2 * g + 1].at[2 * pos + c],
                        device_id=(*chips[j], cc), device_id_type=MESH).start()
        token[...] = jnp.zeros_like(token)

    sem_shapes = []
    for group in groups:
        sem_shapes += [pltpu.SemaphoreType.DMA((2 * len(group),))] * 2
    outs = pl.pallas_call(
        body, name=name,
        out_shape=tuple(sem_shapes) + tuple(pltpu.HBM(f.shape, f.dtype) for f in fulls)
        + (jax.ShapeDtypeStruct((SUBLANE, 128), F32),),
        in_specs=[HBM_SPEC] * n + [ANY_SPEC], out_specs=(SEM_SPEC,) * (2 * ng) + (HBM_SPEC,) * n + (VMEM_SPEC,),
        input_output_aliases={w: 2 * ng + w for w in range(n)},
        compiler_params=pltpu.CompilerParams(has_side_effects=EFFECT),
    )(*[_hbm(f) for f in fulls], after)
    sems = [(outs[2 * g], outs[2 * g + 1]) for g in range(ng)]
    return sems, list(outs[2 * ng:2 * ng + n]), outs[2 * ng + n]


def _gather_weights_wait(sems, group, fulls, kinds, shard_shapes, after, name):
    n = len(fulls)

    def body(*refs):
        ins = refs[:n]
        send_sems, recv_sems = refs[n], refs[n + 1]
        x, y, c = _place()
        chips = _other_chips(x, y)
        for pos, (w, j) in enumerate(group):
            own = _weight_window(ins[w], kinds[w], shard_shapes[w], 2 * x + y, c)
            for cc in range(2):
                landed = _weight_window(ins[w], kinds[w], shard_shapes[w], 2 * chips[j][0] + chips[j][1], cc)
                k = 2 * pos + cc
                pltpu.make_async_remote_copy(
                    src_ref=own, dst_ref=own, send_sem=send_sems.at[k], recv_sem=recv_sems.at[k],
                    device_id=(*chips[j], cc), device_id_type=MESH).wait_send()
                pltpu.make_async_remote_copy(
                    src_ref=landed, dst_ref=landed, send_sem=send_sems.at[k], recv_sem=recv_sems.at[k],
                    device_id=(*chips[j], cc), device_id_type=MESH).wait_recv()

    return list(pl.pallas_call(
        body, name=name, out_shape=tuple(pltpu.HBM(f.shape, f.dtype) for f in fulls),
        in_specs=[HBM_SPEC] * n + [SEM_SPEC, SEM_SPEC, ANY_SPEC], out_specs=(HBM_SPEC,) * n,
        input_output_aliases={w: w for w in range(n)},
        compiler_params=pltpu.CompilerParams(has_side_effects=EFFECT),
    )(*fulls, sems[0], sems[1], after))


def _grad_piece(ref, kind, h, cdim, k, half):
    if kind == "col":
        return ref.at[pl.ds(half * h, h), pl.ds(pl.multiple_of(k * cdim, 128), cdim)]
    return ref.at[k, pl.ds(half * h, h), :]


def _grad_dims(g, kind):
    return (g.shape[0] // 2, g.shape[1] // N_CHIP) if kind == "col" else (g.shape[1] // 2, g.shape[2])


def _reduce_start(grads, kinds, name):
    n = len(grads)
    dims = [_grad_dims(g, kind) for g, kind in zip(grads, kinds)]
    lands = [lax.empty((N_DEV - 1, h, cdim), g.dtype) for g, (h, cdim) in zip(grads, dims)]

    def body(*refs):
        g_ins, land_ins = refs[:n], refs[n:2 * n]
        send_sems, recv_sems = refs[2 * n], refs[2 * n + 1]
        token = refs[4 * n + 2]
        x, y, c = _place()
        for w in range(n):
            h, cdim = dims[w]
            base = (N_DEV - 1) * w
            pltpu.make_async_remote_copy(
                src_ref=_grad_piece(g_ins[w], kinds[w], h, cdim, 2 * x + y, 1 - c), dst_ref=land_ins[w].at[0],
                send_sem=send_sems.at[base], recv_sem=recv_sems.at[base],
                device_id=(x, y, 1 - c), device_id_type=MESH).start()
            for j, chip in enumerate(_other_chips(x, y)):
                for cc in range(2):
                    pltpu.make_async_remote_copy(
                        src_ref=_grad_piece(g_ins[w], kinds[w], h, cdim, 2 * chip[0] + chip[1], cc),
                        dst_ref=land_ins[w].at[1 + 2 * j + c],
                        send_sem=send_sems.at[base + 1 + 2 * j + cc], recv_sem=recv_sems.at[base + 1 + 2 * j + c],
                        device_id=(*chip, cc), device_id_type=MESH).start()
        token[...] = jnp.zeros_like(token)

    sems = pltpu.SemaphoreType.DMA(((N_DEV - 1) * n,))
    outs = pl.pallas_call(
        body, name=name,
        out_shape=(sems, sems) + tuple(pltpu.HBM(a.shape, a.dtype) for a in list(grads) + lands)
        + (jax.ShapeDtypeStruct((SUBLANE, 128), F32),),
        in_specs=[HBM_SPEC] * (2 * n), out_specs=(SEM_SPEC, SEM_SPEC) + (HBM_SPEC,) * (2 * n) + (VMEM_SPEC,),
        input_output_aliases={i: 2 + i for i in range(2 * n)},
        compiler_params=pltpu.CompilerParams(has_side_effects=EFFECT),
    )(*[_hbm(a) for a in list(grads) + lands])
    return outs[0], outs[1], list(outs[2:2 + n]), list(outs[2 + n:2 + 2 * n]), outs[2 + 2 * n]


def _reduce_wait(send_sems, recv_sems, grads, lands, kinds, after, name):
    n = len(grads)
    dims = [_grad_dims(g, kind) for g, kind in zip(grads, kinds)]

    def body(*refs):
        g_ins, land_ins = refs[:n], refs[n:2 * n]
        send_sems, recv_sems = refs[2 * n], refs[2 * n + 1]
        x, y, c = _place()
        for w in range(n):
            h, cdim = dims[w]
            piece = _grad_piece(g_ins[w], kinds[w], h, cdim, 2 * x + y, c)
            for s in range(N_DEV - 1):
                k = (N_DEV - 1) * w + s
                slot = land_ins[w].at[s]
                pltpu.make_async_remote_copy(
                    src_ref=piece, dst_ref=slot, send_sem=send_sems.at[k], recv_sem=recv_sems.at[k],
                    device_id=(x, y, 1 - c), device_id_type=MESH).wait_send()
                pltpu.make_async_remote_copy(
                    src_ref=piece, dst_ref=slot, send_sem=send_sems.at[k], recv_sem=recv_sems.at[k],
                    device_id=(x, y, 1 - c), device_id_type=MESH).wait_recv()

    outs = pl.pallas_call(
        body, name=name, out_shape=tuple(pltpu.HBM(a.shape, a.dtype) for a in list(grads) + list(lands)),
        in_specs=[HBM_SPEC] * (2 * n) + [SEM_SPEC, SEM_SPEC, ANY_SPEC], out_specs=(HBM_SPEC,) * (2 * n),
        input_output_aliases={i: i for i in range(2 * n)},
        compiler_params=pltpu.CompilerParams(has_side_effects=EFFECT),
    )(*grads, *lands, send_sems, recv_sems, after)
    return list(outs[:n]), list(outs[n:])


def _add_pieces(g, land, kind, chip_idx, core_idx, name):
    _, h, cdim = land.shape
    tr, tc = _tile(h, 256), _tile(cdim, 2048)
    nrb, ncb = h // tr, cdim // tc

    def body(k_ref, c_ref, g_ref, l_ref, o_ref):
        acc = g_ref[...].astype(F32)
        for s in range(N_DEV - 1):
            acc = acc + l_ref[s].astype(F32)
        o_ref[...] = acc

    if kind == "col":
        g_spec = pl.BlockSpec((tr, tc), lambda i, j, kr, cr: (cr[0] * nrb + i, kr[0] * ncb + j))
    else:
        g_spec = pl.BlockSpec((None, tr, tc), lambda i, j, kr, cr: (kr[0], cr[0] * nrb + i, j))
    return pl.pallas_call(
        body, name=name, out_shape=jax.ShapeDtypeStruct((2 * h, cdim), F32),
        grid_spec=pltpu.PrefetchScalarGridSpec(
            num_scalar_prefetch=2, grid=(nrb, ncb),
            in_specs=[g_spec, pl.BlockSpec((N_DEV - 1, tr, tc), lambda i, j, kr, cr: (0, i, j))],
            out_specs=pl.BlockSpec((tr, tc), lambda i, j, kr, cr: (cr[0] * nrb + i, j))),
        compiler_params=_params("parallel", "parallel"),
    )(chip_idx, core_idx, g, land)


def _join_halves(shards, name):
    n = len(shards)

    def body(*refs):
        ins, outs = refs[:n], refs[n:2 * n]
        send_sems, recv_sems = refs[2 * n:]
        x, y, c = _place()
        copies = []
        for w in range(n):
            h = shards[w].shape[0] // 2
            cp = pltpu.make_async_remote_copy(
                src_ref=ins[w].at[pl.ds(c * h, h), :], dst_ref=outs[w].at[pl.ds(c * h, h), :],
                send_sem=send_sems.at[w], recv_sem=recv_sems.at[w], device_id=(x, y, 1 - c), device_id_type=MESH)
            cp.start()
            copies.append(cp)
        for w in range(n):
            h = shards[w].shape[0] // 2
            theirs = outs[w].at[pl.ds((1 - c) * h, h), :]
            pltpu.make_async_remote_copy(
                src_ref=theirs, dst_ref=theirs, send_sem=send_sems.at[w], recv_sem=recv_sems.at[w],
                device_id=(x, y, 1 - c), device_id_type=MESH).wait_recv()
        for cp in copies:
            cp.wait_send()

    return pl.pallas_call(
        body, name=name, out_shape=tuple(jax.ShapeDtypeStruct(s.shape, s.dtype) for s in shards),
        in_specs=[HBM_SPEC] * n, out_specs=tuple([HBM_SPEC] * n),
        input_output_aliases={w: w for w in range(n)},
        scratch_shapes=[pltpu.SemaphoreType.DMA((n,))] * 2,
    )(*shards)


def _sum_devices(gathered, rows, name):
    n = gathered.shape[1]

    def body(g_ref, o_ref):
        acc = g_ref[0:rows, :]
        for dev in range(1, N_DEV):
            acc = acc + g_ref[dev * rows:(dev + 1) * rows, :]
        o_ref[...] = acc

    return pl.pallas_call(
        body, name=name, out_shape=jax.ShapeDtypeStruct((rows, n), F32),
        in_specs=[VMEM_SPEC], out_specs=VMEM_SPEC,
        compiler_params=pltpu.CompilerParams(vmem_limit_bytes=VMEM_LIMIT),
    )(gathered)


def _pack(vectors, width):
    flat = [v.reshape(-1) for v in vectors]
    offsets, total = [], 0
    for f in flat:
        offsets.append(total)
        total += f.shape[0]
    rows = -(-total // (width * SUBLANE)) * SUBLANE
    flat.append(jnp.zeros((rows * width - total,), F32))
    return jnp.concatenate(flat).reshape(rows, width), offsets


def kernel(x, c, mod_w, mod_b, norm_g, a_w_in, a_conv_w, a_conv_b, a_w_out, b_w_in, b_ln_g, b_ln_b, b_w_s, b_b_s, b_w_out, final_g, loss_target, m_mod_w, m_mod_b, m_norm_g, m_a_w_in, m_a_conv_w, m_a_conv_b, m_a_w_out, m_b_w_in, m_b_ln_g, m_b_ln_b, m_b_w_s, m_b_b_s, m_b_w_out, m_final_g, v_mod_w, v_mod_b, v_norm_g, v_a_w_in, v_a_conv_w, v_a_conv_b, v_a_w_out, v_b_w_in, v_b_ln_g, v_b_ln_b, v_b_w_s, v_b_b_s, v_b_w_out, v_final_g):
    seq, d = x.shape[1], x.shape[2]
    e = a_conv_b.shape[1]
    wd = mod_w.shape[2]
    ax, ay, ac = _place()
    chip = 2 * ax + ay
    dev = 2 * chip + ac
    chip_idx = jnp.reshape(chip, (1,)).astype(jnp.int32)
    core_idx = jnp.reshape(ac, (1,)).astype(jnp.int32)

    x2d = x[0]
    target = loss_target[0]

    w_names = ["a_w_in", "a_w_out", "b_w_in", "b_w_out"]
    w_kinds = ["col", "row", "col", "row"]
    w_shards = [a_w_in[0], a_w_out[0], b_w_in[0], b_w_out[0]]
    w_shapes = [sh.shape for sh in w_shards]
    placed = [_place_own(sh, kind, chip_idx, "place_" + nm) for nm, sh, kind in zip(w_names, w_shards, w_kinds)]
    a_groups, far_groups = [[(0, 0), (0, 1)]], [[(0, 2)]]
    whole_group = [[(0, 0), (0, 1), (0, 2)]]

    es = e // N_CHIP
    packed0, offs0 = _pack([c, a_conv_w, b_ln_g, b_ln_b], 1024)
    gathered0 = _all_gather(packed0, "gather_params").reshape(N_DEV, -1)
    c_all = gathered0[:, :d]
    per_chip = gathered0[0::2]

    def from_chips_cols(k, rows_):
        got = per_chip[:, offs0[k]:offs0[k] + rows_ * es].reshape(N_CHIP, rows_, es)
        return jnp.transpose(got, (1, 0, 2)).reshape(rows_, e)

    conv_w_full = from_chips_cols(1, 3)
    conv_w = [conv_w_full[k:k + 1] for k in range(3)]
    ln_g, ln_b = from_chips_cols(2, 1), from_chips_cols(3, 1)
    mod_b_cols = lax.dynamic_slice_in_dim(mod_b, chip * wd, wd, axis=1)[:, None, :]
    c_act, mod_part = _mod_fwd(c_all, mod_w, mod_b_cols, "mod_fwd")
    n_layer = mod_w.shape[0]
    mod_gathered = _all_gather(mod_part, "gather_mod")
    a_sems, (wa_in,), a_token = _gather_weights_start(
        placed[:1], w_kinds[:1], w_shapes[:1], a_groups, mod_gathered, "gather_a_w_in_start")
    mod_all = mod_gathered.reshape(N_CHIP, 2, n_layer, N_DEV, wd)[:, 0]
    mod_all = jnp.transpose(mod_all, (1, 2, 0, 3)).reshape(n_layer, N_DEV, N_CHIP * wd)
    mod_me = lax.dynamic_index_in_dim(mod_all, dev, axis=1, keepdims=False)
    shift = [mod_me[l:l + 1, 0:d] for l in range(n_layer)]
    scale = [mod_me[l:l + 1, d:2 * d] for l in range(n_layer)]
    gate = [mod_me[l:l + 1, 2 * d:3 * d] for l in range(n_layer)]

    g0, g1, gf = norm_g[0:1], norm_g[1:2], final_g[None, :]
    h0 = _norm_mod(x2d, g0, scale[0], shift[0], "norm_mod0")

    def slab(r):
        return jnp.bitwise_xor(chip_idx, r)

    def arrived(sems, group, weight, w, after, name):
        return _gather_weights_wait(sems, [(0, j) for _, j in group], [weight], [w_kinds[w]], [w_shapes[w]],
                                    after, name)[0]

    proj0 = _mm_proj_slab(h0, wa_in, None, slab(0), 4, "a_proj_own", a_token)
    wa_in = arrived(a_sems[0], a_groups[0], wa_in, 0, proj0, "gather_wait_near")
    far_sems, (wa_in,), far_token = _gather_weights_start(
        [wa_in], w_kinds[:1], w_shapes[:1], far_groups, wa_in, "gather_a_w_in_far_start")
    proj0 = _mm_proj_slab(h0, wa_in, proj0, slab(2), 4, "a_proj_x", far_token)
    proj0 = _mm_proj_slab(h0, wa_in, proj0, slab(1), 4, "a_proj_y", far_token)
    wa_in = arrived(far_sems[0], far_groups[0], wa_in, 0, proj0, "gather_wait_far")
    def start_whole(w, after, name):
        sems, (flight,), token = _gather_weights_start(
            placed[w:w + 1], w_kinds[w:w + 1], w_shapes[w:w + 1], whole_group, after, name)
        return sems[0], flight, token

    ao_sems, wa_out, ao_token = start_whole(1, wa_in, "gather_a_w_out_start")
    proj0 = _mm_proj_slab(h0, wa_in, proj0, slab(3), 4, "a_proj_far", ao_token)
    wa_out = arrived(ao_sems, whole_group[0], wa_out, 1, proj0, "gather_wait_a_w_out")
    bi_sems, wb_in, bi_token = start_whole(2, wa_out, "gather_b_w_in_start")
    y0, br0 = _conv_fwd(proj0, conv_w, a_conv_b, wa_out, "conv_fwd", bi_token)
    wb_in = arrived(bi_sems, whole_group[0], wb_in, 2, br0, "gather_wait_b_w_in")
    bo_sems, wb_out, bo_token = start_whole(3, wb_in, "gather_b_w_out_start")
    x1, h1 = _residual_norm_mod(x2d, br0, gate[0], g1, scale[1], shift[1], "residual_norm_mod1")
    proj1 = _mm_proj(h1, wb_in, 3, "b_proj", bo_token)
    b_s_t = jnp.transpose(b_b_s[0])
    wb_out = arrived(bo_sems, whole_group[0], wb_out, 3, proj1, "gather_wait_b_w_out")
    y1, br1 = _gmlp_fwd(proj1, ln_g, ln_b, b_w_s[0], b_s_t, wb_out, "gmlp_fwd")
    dx2, dbr1, loss_part, g_final_g, dgate1 = _head(x1, br1, gate[1], gf, target, "head")

    gw_b_out = _mm_dw_out(y1, dbr1, "b_out_dw")
    dproj1, g_w_s, g_b_s_t, g_ln_g, g_ln_b = _gmlp_bwd(
        proj1, dbr1, wb_out, ln_g, ln_b, b_w_s[0], jnp.swapaxes(b_w_s[0], 1, 2), b_s_t, "gmlp_bwd")
    gw_b_in = _mm_dw_in(h1, dproj1, "b_proj_dw")
    b_kinds = ["col", "row"]
    b_send, b_recv, b_grads, b_lands, b_token = _reduce_start(
        [gw_b_in, gw_b_out.reshape(N_CHIP, e // N_CHIP, d)], b_kinds, "reduce_b_start")
    dh1 = _mm_dh(dproj1, wb_in, "b_proj_dx", b_token)
    dx1, dshift1, dscale1, g_g1, dbr0, dgate0 = _norm_mod_bwd(
        dh1, x1, dx2, g1, scale[1], "norm_mod1_bwd", br=br0, gate=gate[0])

    gw_a_out = _mm_dw_out(y0, dbr0, "a_out_dw")
    dproj0, g_w0, g_w1, g_w2, g_conv_b = _conv_bwd(proj0, dbr0, wa_out, conv_w, a_conv_b, "conv_bwd")
    gw_a_in = _mm_dw_in(h0, dproj0, "a_proj_dw")
    a_kinds = ["col", "row"]
    a_send, a_recv, a_grads, a_lands, a_token = _reduce_start(
        [gw_a_in, gw_a_out.reshape(N_CHIP, e // N_CHIP, d)], a_kinds, "reduce_a_start")
    dh0 = _mm_dh(dproj0, wa_in, "a_proj_dx", a_token)
    grad_x, dshift0, dscale0, g_g0 = _norm_mod_bwd(dh0, x2d, dx1, g0, scale[0], "norm_mod0_bwd")

    dmod = jnp.concatenate([dshift0, dscale0, dgate0, dshift1, dscale1, dgate1], axis=1)
    small = [loss_part[0, 0:1], g_final_g, g_g0, g_g1, jnp.concatenate([g_w0, g_w1, g_w2], axis=0), g_conv_b, g_ln_g, g_ln_b,
             jnp.transpose(g_b_s_t), g_w_s, dmod]
    packed, offs = _pack(small, 1024)
    rows = packed.shape[0]
    gathered = _all_gather(packed, "gather_small")
    summed = _sum_devices(gathered, rows, "sum_small").reshape(-1)

    def take(k, shape):
        size = math.prod(shape)
        return summed[offs[k]:offs[k] + size].reshape(shape)

    loss = take(0, ())
    grad_final_g = take(1, (d,))
    grad_norm_g = jnp.concatenate([take(2, (1, d)), take(3, (1, d))], axis=0)
    grad_conv_w_full = take(4, (3, e))
    grad_a_conv_b = take(5, (1, e))
    grad_ln_g_full = take(6, (1, e))
    grad_ln_b_full = take(7, (1, e))
    grad_b_b_s = take(8, (1, GROUPS, CHUNK))
    grad_b_w_s = take(9, (1, GROUPS, CHUNK, CHUNK))
    grad_mod_b = take(10, (n_layer, 3 * d))
    grad_a_conv_w = lax.dynamic_slice_in_dim(grad_conv_w_full, chip * es, es, axis=1)[None]
    grad_b_ln_g = lax.dynamic_slice_in_dim(grad_ln_g_full, chip * es, es, axis=1)
    grad_b_ln_b = lax.dynamic_slice_in_dim(grad_ln_b_full, chip * es, es, axis=1)
    dmod_all = gathered.reshape(N_DEV, rows * 1024)[:, offs[10]:offs[10] + n_layer * 3 * d]
    dmod_all = dmod_all.reshape(N_DEV, n_layer, 3 * d)
    dmod_cols = jnp.transpose(lax.dynamic_slice_in_dim(dmod_all, chip * wd, wd, axis=2), (1, 0, 2))

    b_grads, b_lands = _reduce_wait(b_send, b_recv, b_grads, b_lands, b_kinds, summed, "reduce_b_wait")
    a_grads, a_lands = _reduce_wait(a_send, a_recv, a_grads, a_lands, a_kinds, summed, "reduce_a_wait")
    summed_halves = [
        _add_pieces(g, land, kind, chip_idx, core_idx, "add_pieces_" + nm)
        for g, land, kind, nm in [(a_grads[0], a_lands[0], "col", "a_w_in"), (b_grads[0], b_lands[0], "col", "b_w_in"),
                                  (a_grads[1], a_lands[1], "row", "a_w_out"), (b_grads[1], b_lands[1], "row", "b_w_out")]]
    g_a_w_in, g_b_w_in, g_a_w_out, g_b_w_out = _join_halves(summed_halves, "join_halves")

    grad_mod_w, delta_mod_w, new_m_mod_w, new_v_mod_w = _mod_w_update(
        jnp.transpose(c_act), dmod_cols, mod_w, m_mod_w, v_mod_w, "mod_w_update")
    upd = {}
    for nm, w, g, m, v in [("a_w_in", a_w_in, g_a_w_in, m_a_w_in, v_a_w_in), ("a_w_out", a_w_out, g_a_w_out, m_a_w_out, v_a_w_out),
                           ("b_w_in", b_w_in, g_b_w_in, m_b_w_in, v_b_w_in), ("b_w_out", b_w_out, g_b_w_out, m_b_w_out, v_b_w_out)]:
        upd[nm] = tuple(o[None] for o in _adamw_2d(w[0], g, m[0], v[0], "adamw_" + nm))
    small_w = [("mod_b", mod_b, grad_mod_b, m_mod_b, v_mod_b), ("norm_g", norm_g, grad_norm_g, m_norm_g, v_norm_g),
               ("a_conv_w", a_conv_w, grad_a_conv_w, m_a_conv_w, v_a_conv_w),
               ("a_conv_b", a_conv_b, grad_a_conv_b, m_a_conv_b, v_a_conv_b),
               ("b_ln_g", b_ln_g, grad_b_ln_g, m_b_ln_g, v_b_ln_g), ("b_ln_b", b_ln_b, grad_b_ln_b, m_b_ln_b, v_b_ln_b),
               ("b_w_s", b_w_s, grad_b_w_s, m_b_w_s, v_b_w_s), ("b_b_s", b_b_s, grad_b_b_s, m_b_b_s, v_b_b_s),
               ("final_g", final_g, grad_final_g, m_final_g, v_final_g)]

    def flat2d(a):
        return a.reshape(-1, a.shape[-1])

    res = _adamw_small([flat2d(t[1]) for t in small_w], [flat2d(t[2]) for t in small_w],
                       [flat2d(t[3]) for t in small_w], [flat2d(t[4]) for t in small_w], "adamw_small")
    for (nm, w, _, _, _), r3 in zip(small_w, res):
        upd[nm] = tuple(o.reshape(w.shape) for o in r3)
    upd["mod_w"] = (delta_mod_w, new_m_mod_w, new_v_mod_w)

    grads = {"mod_w": grad_mod_w, "mod_b": grad_mod_b, "norm_g": grad_norm_g, "a_w_in": g_a_w_in[None],
             "a_conv_w": grad_a_conv_w, "a_conv_b": grad_a_conv_b, "a_w_out": g_a_w_out[None], "b_w_in": g_b_w_in[None],
             "b_ln_g": grad_b_ln_g, "b_ln_b": grad_b_ln_b, "b_w_s": grad_b_w_s, "b_b_s": grad_b_b_s,
             "b_w_out": g_b_w_out[None], "final_g": grad_final_g}
    order = ["mod_w", "mod_b", "norm_g", "a_w_in", "a_conv_w", "a_conv_b", "a_w_out", "b_w_in", "b_ln_g", "b_ln_b",
             "b_w_s", "b_b_s", "b_w_out", "final_g"]
    return (loss, grad_x[None], *[grads[k] for k in order], *[upd[k][0] for k in order],
            *[upd[k][1] for k in order], *[upd[k][2] for k in order])
```

```python
import functools
import math

import jax
import jax.numpy as jnp
from jax import lax
from jax.experimental import pallas as pl
from jax.experimental.pallas import tpu as pltpu

F32 = jnp.float32
BF16 = jnp.bfloat16
MESH = pl.DeviceIdType.MESH

N_DEV = 8
N_CHIP = 4
SUBLANE = 8
PACK = 16
RMS_EPS = 1e-6
LN_EPS = 1e-5
CHUNK = 128
GROUPS = 8
ADAM_LR = 0.001
ADAM_B1 = 0.9
ADAM_B2 = 0.999
ADAM_EPS = 1e-08
ADAM_WD = 0.01
ADAM_STEP = 10
VMEM_LIMIT = 56 << 20

HBM_SPEC = pl.BlockSpec(memory_space=pltpu.HBM)
VMEM_SPEC = pl.BlockSpec(memory_space=pltpu.VMEM)
SEM_SPEC = pl.BlockSpec(memory_space=pltpu.SEMAPHORE)
ANY_SPEC = pl.BlockSpec(memory_space=pl.ANY)
EFFECT = pltpu.SideEffectType.DATAFLOW_SIDE_EFFECTING


def _params(*sem):
    return pltpu.CompilerParams(dimension_semantics=sem, vmem_limit_bytes=VMEM_LIMIT)


def _tile(n, want):
    if n <= want:
        return n
    t = want
    while n % t:
        t -= 128
    return t


def _sigmoid(x):
    return 1.0 / (1.0 + jnp.exp(-x))


def _silu_and_grad(x):
    s = _sigmoid(x)
    return x * s, s * (1.0 + x * (1.0 - s))


def _gelu_and_grad(x):
    cdf = 0.5 * (1.0 + lax.erf(x * (1.0 / math.sqrt(2.0))))
    pdf = jnp.exp(-0.5 * x * x) * (1.0 / math.sqrt(2.0 * math.pi))
    return x * cdf, cdf + x * pdf


def _gelu(x):
    return x * (0.5 * (1.0 + lax.erf(x * (1.0 / math.sqrt(2.0)))))


def _rms(x):
    r = lax.rsqrt(jnp.mean(x * x, axis=-1, keepdims=True) + RMS_EPS)
    return x * r, r


def _rms_bwd(dxn, xn, r):
    return r * (dxn - xn * jnp.mean(dxn * xn, axis=-1, keepdims=True))


def _colsum(a):
    return jnp.sum(a, axis=0, keepdims=True)


def _shift_down(cur, before, k):
    rolled = pltpu.roll(cur, k, 0)
    row = lax.broadcasted_iota(jnp.int32, before.shape, 0)
    head = jnp.where(row < k, pltpu.roll(before, k, 0), rolled[:SUBLANE])
    return jnp.concatenate([head, rolled[SUBLANE:]], axis=0)


def _shift_up(cur, after, k):
    n = cur.shape[0]
    rolled = pltpu.roll(cur, n - k, 0)
    row = lax.broadcasted_iota(jnp.int32, after.shape, 0)
    tail = jnp.where(row >= SUBLANE - k, pltpu.roll(after, SUBLANE - k, 0), rolled[n - SUBLANE:])
    return jnp.concatenate([rolled[:n - SUBLANE], tail], axis=0)


def _after_spec():
    return pl.BlockSpec((SUBLANE, 128), lambda *_: (0, 0))


def _mm_proj(h, w, n_split, name, after):
    s, d = h.shape
    e = w.shape[1] // n_split
    tm, tn = _tile(s, 1024), _tile(e, 2048)
    nj = e // tn

    def body(h_ref, w_ref, after_ref, o_ref):
        o_ref[...] = jnp.dot(h_ref[...], w_ref[...], preferred_element_type=F32).astype(BF16)

    return pl.pallas_call(
        body, name=name,
        out_shape=jax.ShapeDtypeStruct((n_split, s, e), BF16),
        grid=(s // tm, n_split * nj),
        in_specs=[pl.BlockSpec((tm, d), lambda i, j: (i, 0)), pl.BlockSpec((d, tn), lambda i, j: (0, j)),
                  _after_spec()],
        out_specs=pl.BlockSpec((None, tm, tn), lambda i, j: (j // nj, i, j % nj)),
        compiler_params=_params("parallel", "parallel"),
    )(h, w, after)


def _mm_proj_slab(h, w, proj, q_idx, n_split, name, after):
    s, d = h.shape
    e = w.shape[1] // n_split
    tm, tn = _tile(s, 1024), _tile(e, 2048)
    nj = e // tn

    def body(q_ref, h_ref, w_ref, *rest):
        o_ref = rest[-1]
        o_ref[...] = jnp.dot(h_ref[...], w_ref[...], preferred_element_type=F32).astype(BF16)

    in_specs = [pl.BlockSpec((tm, d), lambda i, j, qr: (i, 0)), pl.BlockSpec((d, tn), lambda i, j, qr: (0, qr[0] * nj + j)),
                _after_spec()]
    args = [q_idx, h, w, after]
    aliases = {}
    if proj is not None:
        in_specs.append(ANY_SPEC)
        args.append(proj)
        aliases = {4: 0}
    return pl.pallas_call(
        body, name=name,
        out_shape=jax.ShapeDtypeStruct((n_split, s, e), BF16),
        grid_spec=pltpu.PrefetchScalarGridSpec(
            num_scalar_prefetch=1, grid=(s // tm, nj), in_specs=in_specs,
            out_specs=pl.BlockSpec((None, tm, tn), lambda i, j, qr: (qr[0], i, j))),
        input_output_aliases=aliases,
        compiler_params=_params("parallel", "parallel"),
    )(*args)


def _mm_dh(dp, w, name, after):
    nq, s, e = dp.shape
    d = w.shape[0]
    tm, tn, tk = _tile(s, 1024), _tile(d, 1024), _tile(e, 2048)
    nkq = e // tk
    nk = nq * nkq

    def body(a_ref, b_ref, after_ref, o_ref):
        @pl.when(pl.program_id(2) == 0)
        def _():
            o_ref[...] = jnp.zeros_like(o_ref)
        o_ref[...] += lax.dot_general(a_ref[...], b_ref[...], (((1,), (1,)), ((), ())), preferred_element_type=F32)

    return pl.pallas_call(
        body, name=name,
        out_shape=jax.ShapeDtypeStruct((s, d), F32),
        grid=(s // tm, d // tn, nk),
        in_specs=[pl.BlockSpec((None, tm, tk), lambda i, j, k: (k // nkq, i, k % nkq)),
                  pl.BlockSpec((tn, tk), lambda i, j, k: (j, k)), _after_spec()],
        out_specs=pl.BlockSpec((tm, tn), lambda i, j, k: (i, j)),
        compiler_params=_params("parallel", "parallel", "arbitrary"),
    )(dp, w, after)


def _dw_body(n_t):
    def body(a_ref, b_ref, o_ref, acc_ref):
        t = pl.program_id(2)

        @pl.when(t == 0)
        def _():
            acc_ref[...] = jnp.zeros_like(acc_ref)
        acc_ref[...] += lax.dot_general(a_ref[...], b_ref[...], (((0,), (0,)), ((), ())), preferred_element_type=F32)

        @pl.when(t == n_t - 1)
        def _():
            o_ref[...] = acc_ref[...].astype(o_ref.dtype)
    return body


def _mm_dw_in(h, dp, name):
    s, d = h.shape
    nq, _, e = dp.shape
    tm, tn, tt = _tile(d, 1024), _tile(e, 1024), _tile(s, 2048)
    nj = e // tn

    return pl.pallas_call(
        _dw_body(s // tt), name=name,
        out_shape=jax.ShapeDtypeStruct((d, nq * e), BF16),
        grid=(d // tm, nq * nj, s // tt),
        in_specs=[pl.BlockSpec((tt, tm), lambda i, j, t: (t, i)),
                  pl.BlockSpec((None, tt, tn), lambda i, j, t: (j // nj, t, j % nj))],
        out_specs=pl.BlockSpec((tm, tn), lambda i, j, t: (i, j)),
        scratch_shapes=[pltpu.VMEM((tm, tn), F32)],
        compiler_params=_params("parallel", "parallel", "arbitrary"),
    )(h, dp)


def _mm_dw_out(y, dbr, name):
    s, e = y.shape
    d = dbr.shape[1]
    tm, tn, tt = _tile(e, 1024), _tile(d, 1024), _tile(s, 2048)

    return pl.pallas_call(
        _dw_body(s // tt), name=name,
        out_shape=jax.ShapeDtypeStruct((e, d), BF16),
        grid=(e // tm, d // tn, s // tt),
        in_specs=[pl.BlockSpec((tt, tm), lambda i, j, t: (t, i)), pl.BlockSpec((tt, tn), lambda i, j, t: (t, j))],
        out_specs=pl.BlockSpec((tm, tn), lambda i, j, t: (i, j)),
        scratch_shapes=[pltpu.VMEM((tm, tn), F32)],
        compiler_params=_params("parallel", "parallel", "arbitrary"),
    )(y, dbr)


def _row_spec(ts, d):
    return pl.BlockSpec((ts, d), lambda i: (i, 0))


def _vec_spec(d):
    return pl.BlockSpec((1, d), lambda i: (0, 0))


def _norm_mod(x, g, scale, shift, name):
    s, d = x.shape
    ts = _tile(s, 512)

    def body(x_ref, g_ref, sc_ref, sh_ref, h_ref):
        xn, _ = _rms(x_ref[...])
        h_ref[...] = ((xn * g_ref[...]) * (1.0 + sc_ref[...]) + sh_ref[...]).astype(BF16)

    return pl.pallas_call(
        body, name=name, out_shape=jax.ShapeDtypeStruct((s, d), BF16), grid=(s // ts,),
        in_specs=[_row_spec(ts, d), _vec_spec(d), _vec_spec(d), _vec_spec(d)],
        out_specs=_row_spec(ts, d), compiler_params=_params("parallel"),
    )(x, g, scale, shift)


def _residual_norm_mod(x, br, gate, g, scale, shift, name):
    s, d = x.shape
    ts = _tile(s, 512)

    def body(x_ref, br_ref, gate_ref, g_ref, sc_ref, sh_ref, x1_ref, h_ref):
        x1 = x_ref[...] + gate_ref[...] * br_ref[...]
        x1_ref[...] = x1
        xn, _ = _rms(x1)
        h_ref[...] = ((xn * g_ref[...]) * (1.0 + sc_ref[...]) + sh_ref[...]).astype(BF16)

    return pl.pallas_call(
        body, name=name,
        out_shape=(jax.ShapeDtypeStruct((s, d), F32), jax.ShapeDtypeStruct((s, d), BF16)), grid=(s // ts,),
        in_specs=[_row_spec(ts, d), _row_spec(ts, d)] + [_vec_spec(d)] * 4,
        out_specs=(_row_spec(ts, d), _row_spec(ts, d)), compiler_params=_params("parallel"),
    )(x, br, gate, g, scale, shift)


def _head(x1, br, gate, gf, target, name):
    s, d = x1.shape
    ts = _tile(s, 512)

    def body(x1_ref, br_ref, gate_ref, gf_ref, tg_ref, dx_ref, dbr_ref, loss_ref, dgf_ref, dgate_ref):
        @pl.when(pl.program_id(0) == 0)
        def _():
            loss_ref[...] = jnp.zeros_like(loss_ref)
            dgf_ref[...] = jnp.zeros_like(dgf_ref)
            dgate_ref[...] = jnp.zeros_like(dgate_ref)
        br = br_ref[...]
        gate = gate_ref[...]
        gf = gf_ref[...]
        xn, r = _rms(x1_ref[...] + gate * br)
        err = xn * gf - tg_ref[...]
        loss_ref[...] += 0.5 * jnp.sum(jnp.mean(err * err, axis=-1, keepdims=True))
        dout = err * (1.0 / d)
        dgf_ref[...] += _colsum(dout * xn)
        dx = _rms_bwd(dout * gf, xn, r)
        dx_ref[...] = dx
        dgate_ref[...] += _colsum(dx * br)
        dbr_ref[...] = (gate * dx).astype(BF16)

    return pl.pallas_call(
        body, name=name,
        out_shape=(jax.ShapeDtypeStruct((s, d), F32), jax.ShapeDtypeStruct((s, d), BF16),
                   jax.ShapeDtypeStruct((SUBLANE, 128), F32), jax.ShapeDtypeStruct((1, d), F32),
                   jax.ShapeDtypeStruct((1, d), F32)),
        grid=(s // ts,),
        in_specs=[_row_spec(ts, d), _row_spec(ts, d), _vec_spec(d), _vec_spec(d), _row_spec(ts, d)],
        out_specs=(_row_spec(ts, d), _row_spec(ts, d), pl.BlockSpec((SUBLANE, 128), lambda i: (0, 0)),
                   _vec_spec(d), _vec_spec(d)),
        compiler_params=_params("arbitrary"),
    )(x1, br, gate, gf, target)


def _norm_mod_bwd(dh, x, dx_in, g, scale, name, br=None, gate=None):
    s, d = x.shape
    ts = _tile(s, 512)
    has_branch = br is not None

    def body(*refs):
        if has_branch:
            (dh_ref, x_ref, dxin_ref, g_ref, sc_ref, br_ref, gate_ref,
             dx_ref, dsh_ref, dsc_ref, dg_ref, dbr_ref, dgate_ref) = refs
        else:
            dh_ref, x_ref, dxin_ref, g_ref, sc_ref, dx_ref, dsh_ref, dsc_ref, dg_ref = refs

        @pl.when(pl.program_id(0) == 0)
        def _():
            dsh_ref[...] = jnp.zeros_like(dsh_ref)
            dsc_ref[...] = jnp.zeros_like(dsc_ref)
            dg_ref[...] = jnp.zeros_like(dg_ref)
            if has_branch:
                dgate_ref[...] = jnp.zeros_like(dgate_ref)
        dh = dh_ref[...]
        g = g_ref[...]
        xn, r = _rms(x_ref[...])
        dsh_ref[...] += _colsum(dh)
        dsc_ref[...] += _colsum(dh * (xn * g))
        da = dh * (1.0 + sc_ref[...])
        dg_ref[...] += _colsum(da * xn)
        dx = dxin_ref[...] + _rms_bwd(da * g, xn, r)
        dx_ref[...] = dx
        if has_branch:
            dgate_ref[...] += _colsum(dx * br_ref[...])
            dbr_ref[...] = (gate_ref[...] * dx).astype(BF16)

    vec_out = jax.ShapeDtypeStruct((1, d), F32)
    out_shape = [jax.ShapeDtypeStruct((s, d), F32), vec_out, vec_out, vec_out]
    out_specs = [_row_spec(ts, d), _vec_spec(d), _vec_spec(d), _vec_spec(d)]
    in_specs = [_row_spec(ts, d)] * 3 + [_vec_spec(d)] * 2
    args = [dh, x, dx_in, g, scale]
    if has_branch:
        out_shape += [jax.ShapeDtypeStruct((s, d), BF16), vec_out]
        out_specs += [_row_spec(ts, d), _vec_spec(d)]
        in_specs += [_row_spec(ts, d), _vec_spec(d)]
        args += [br, gate]
    return pl.pallas_call(
        body, name=name, out_shape=tuple(out_shape), grid=(s // ts,),
        in_specs=in_specs, out_specs=tuple(out_specs), compiler_params=_params("arbitrary"),
    )(*args)


def _conv_tiles(s, e):
    return _tile(s, 512), _tile(e, 512)


def _f32(ref):
    return ref[...].astype(F32)


def _rows_before(halo_ref):
    return _f32(halo_ref)[PACK - SUBLANE:]


def _rows_after(halo_ref):
    return _f32(halo_ref)[:SUBLANE]


def _conv_fwd(proj, conv_w, conv_b, w_out, name, after):
    _, s, e = proj.shape
    d = w_out.shape[1]
    ts, te = _tile(s, 256), e
    hb = ts // PACK

    def body(bg_ref, cg_ref, xi_ref, z_ref, cgp_ref, xip_ref, w0_ref, w1_ref, w2_ref, b_ref, wo_ref, after_ref,
             y_ref, br_ref):
        cx = _f32(cg_ref) * _f32(xi_ref)
        before = jnp.where(pl.program_id(0) > 0, _rows_before(cgp_ref) * _rows_before(xip_ref), 0.0)
        conv = b_ref[...] + w2_ref[...] * cx
        conv = conv + w0_ref[...] * _shift_down(cx, before, 2)
        conv = conv + w1_ref[...] * _shift_down(cx, before, 1)
        z = _f32(z_ref)
        y = ((z * _sigmoid(z)) * _f32(bg_ref) * conv).astype(BF16)
        y_ref[...] = y
        br_ref[...] = jnp.dot(y, wo_ref[...], preferred_element_type=F32)

    def part(q):
        return pl.BlockSpec((None, ts, te), lambda i: (q, i, 0))

    def halo_before(q):
        return pl.BlockSpec((None, PACK, te), lambda i: (q, jnp.maximum(i * hb - 1, 0), 0))

    return pl.pallas_call(
        body, name=name,
        out_shape=(jax.ShapeDtypeStruct((s, e), BF16), jax.ShapeDtypeStruct((s, d), F32)), grid=(s // ts,),
        in_specs=[part(0), part(1), part(2), part(3), halo_before(1), halo_before(2)]
        + [_vec_spec(e)] * 4 + [pl.BlockSpec((e, d), lambda i: (0, 0)), _after_spec()],
        out_specs=(_row_spec(ts, e), _row_spec(ts, d)),
        compiler_params=_params("parallel"),
    )(proj, proj, proj, proj, proj, proj, *conv_w, conv_b, w_out, after)


def _conv_bwd(proj, dbr, w_out, conv_w, conv_b, name):
    _, s, e = proj.shape
    d = dbr.shape[1]
    ts, te = _conv_tiles(s, e)
    hb = ts // PACK
    n_i = s // ts
    last_halo = s // PACK - 1
    nt = (((1,), (1,)), ((), ()))

    def body(bg_ref, cg_ref, xi_ref, z_ref, dbr_ref, cgp_ref, xip_ref, bgn_ref, zn_ref, dbrn_ref, wo_ref,
             w0_ref, w1_ref, w2_ref, b_ref, dp_ref, dw0_ref, dw1_ref, dw2_ref, db_ref):
        i = pl.program_id(1)

        @pl.when(i == 0)
        def _():
            for acc in (dw0_ref, dw1_ref, dw2_ref, db_ref):
                acc[...] = jnp.zeros_like(acc)
        wo = wo_ref[...]
        dy = lax.dot_general(dbr_ref[...], wo, nt, preferred_element_type=F32)
        dyn = lax.dot_general(dbrn_ref[...], wo, nt, preferred_element_type=F32)[:SUBLANE]
        bg, cg, xi, z = _f32(bg_ref), _f32(cg_ref), _f32(xi_ref), _f32(z_ref)
        w0, w1, w2 = w0_ref[...], w1_ref[...], w2_ref[...]
        cx = cg * xi
        before = jnp.where(i > 0, _rows_before(cgp_ref) * _rows_before(xip_ref), 0.0)
        cx1 = _shift_down(cx, before, 1)
        cx2 = _shift_down(cx, before, 2)
        conv = b_ref[...] + w2 * cx
        conv = conv + w0 * cx2
        conv = conv + w1 * cx1
        sz, dsz = _silu_and_grad(z)
        dp_ref[3] = (dy * bg * conv * dsz).astype(BF16)
        dp_ref[0] = (dy * sz * conv).astype(BF16)
        dconv = dy * sz * bg
        zn = _rows_after(zn_ref)
        after = jnp.where(i < n_i - 1, dyn * (zn * _sigmoid(zn)) * _rows_after(bgn_ref), 0.0)
        db_ref[...] += _colsum(dconv)
        dw2_ref[...] += _colsum(dconv * cx)
        dw1_ref[...] += _colsum(dconv * cx1)
        dw0_ref[...] += _colsum(dconv * cx2)
        dcx = w2 * dconv + w1 * _shift_up(dconv, after, 1) + w0 * _shift_up(dconv, after, 2)
        dp_ref[1] = (dcx * xi).astype(BF16)
        dp_ref[2] = (dcx * cg).astype(BF16)

    def part(q):
        return pl.BlockSpec((None, ts, te), lambda j, i: (q, i, j))

    def halo_before(q):
        return pl.BlockSpec((None, PACK, te), lambda j, i: (q, jnp.maximum(i * hb - 1, 0), j))

    def halo_after(q):
        return pl.BlockSpec((None, PACK, te), lambda j, i: (q, jnp.minimum((i + 1) * hb, last_halo), j))

    return pl.pallas_call(
        body, name=name,
        out_shape=(jax.ShapeDtypeStruct((4, s, e), BF16),) + (jax.ShapeDtypeStruct((1, e), F32),) * 4,
        grid=(e // te, n_i),
        in_specs=[part(0), part(1), part(2), part(3), pl.BlockSpec((ts, d), lambda j, i: (i, 0)),
                  halo_before(1), halo_before(2), halo_after(0), halo_after(3),
                  pl.BlockSpec((PACK, d), lambda j, i: (jnp.minimum((i + 1) * hb, last_halo), 0)),
                  pl.BlockSpec((te, d), lambda j, i: (j, 0))]
        + [pl.BlockSpec((1, te), lambda j, i: (0, j))] * 4,
        out_specs=(pl.BlockSpec((4, ts, te), lambda j, i: (0, i, j)),) + (pl.BlockSpec((1, te), lambda j, i: (0, j)),) * 4,
        compiler_params=_params("parallel", "arbitrary"),
    )(proj, proj, proj, proj, dbr, proj, proj, proj, proj, dbr, w_out, *conv_w, conv_b)


def _tril(w):
    row = lax.broadcasted_iota(jnp.int32, w.shape, 0)
    col = lax.broadcasted_iota(jnp.int32, w.shape, 1)
    return jnp.where(row >= col, w, 0.0)


def _triu(w):
    row = lax.broadcasted_iota(jnp.int32, w.shape, 0)
    col = lax.broadcasted_iota(jnp.int32, w.shape, 1)
    return jnp.where(row <= col, w, 0.0)


def _layer_norm_fwd(v, g, b):
    mu = jnp.mean(v, axis=-1, keepdims=True)
    vc = v - mu
    rstd = lax.rsqrt(jnp.mean(vc * vc, axis=-1, keepdims=True) + LN_EPS)
    vhat = vc * rstd
    return vhat * g + b, vhat, rstd


GMLP_CHUNKS_PER_STEP = 2


def _gmlp_rows(s):
    return CHUNK * min(GMLP_CHUNKS_PER_STEP, s // CHUNK)


def _mix_positions(w_ref, src_scr, dst_scr, gw, mask, bias_ref=None):
    for gi in range(GROUPS):
        cols = slice(gi * gw, (gi + 1) * gw)
        wm = mask(w_ref[gi]).astype(BF16)
        for n in range(src_scr.shape[0] // CHUNK):
            rows = slice(n * CHUNK, (n + 1) * CHUNK)
            out = jnp.dot(wm, src_scr[rows, cols], preferred_element_type=F32)
            if bias_ref is not None:
                out = out + bias_ref[:, gi:gi + 1]
            dst_scr[rows, cols] = out


def _gmlp_fwd(proj, ln_g, ln_b, w_s, b_s_t, w_out, name):
    _, s, e = proj.shape
    d = w_out.shape[1]
    gw = e // GROUPS
    ts = _gmlp_rows(s)

    def body(pu_ref, pv_ref, pz_ref, g_ref, b_ref, ws_ref, bs_ref, wo_ref, y_ref, br_ref, vn_scr, mix_scr):
        vn, _, _ = _layer_norm_fwd(_gelu(_f32(pv_ref)), g_ref[...], b_ref[...])
        vn_scr[...] = vn.astype(BF16)
        _mix_positions(ws_ref, vn_scr, mix_scr, gw, _tril, bs_ref)
        z = _f32(pz_ref)
        y = ((z * _sigmoid(z)) * (_gelu(_f32(pu_ref)) * mix_scr[...])).astype(BF16)
        y_ref[...] = y
        br_ref[...] = jnp.dot(y, wo_ref[...], preferred_element_type=F32)

    def part(q):
        return pl.BlockSpec((None, ts, e), lambda i: (q, i, 0))

    return pl.pallas_call(
        body, name=name,
        out_shape=(jax.ShapeDtypeStruct((s, e), BF16), jax.ShapeDtypeStruct((s, d), F32)), grid=(s // ts,),
        in_specs=[part(0), part(1), part(2), _vec_spec(e), _vec_spec(e),
                  pl.BlockSpec((GROUPS, CHUNK, CHUNK), lambda i: (0, 0, 0)),
                  pl.BlockSpec((CHUNK, GROUPS), lambda i: (0, 0)), pl.BlockSpec((e, d), lambda i: (0, 0))],
        out_specs=(pl.BlockSpec((ts, e), lambda i: (i, 0)), pl.BlockSpec((ts, d), lambda i: (i, 0))),
        scratch_shapes=[pltpu.VMEM((ts, e), BF16), pltpu.VMEM((ts, e), F32)],
        compiler_params=_params("parallel"),
    )(proj, proj, proj, ln_g, ln_b, w_s, b_s_t, w_out)


def _gmlp_bwd(proj, dbr, w_out, ln_g, ln_b, w_s, w_s_t, b_s_t, name):
    _, s, e = proj.shape
    d = dbr.shape[1]
    gw = e // GROUPS
    ts = _gmlp_rows(s)
    n_i = s // ts

    def body(pu_ref, pv_ref, pz_ref, dbr_ref, wo_ref, g_ref, b_ref, ws_ref, wst_ref, bs_ref,
             dp_ref, dws_ref, dbs_ref, dlg_ref, dlb_ref, vn_scr, mix_scr, dm_scr, dvn_scr, dmacc_scr):
        i = pl.program_id(0)

        @pl.when(i == 0)
        def _():
            dws_ref[...] = jnp.zeros_like(dws_ref)
            dlg_ref[...] = jnp.zeros_like(dlg_ref)
            dlb_ref[...] = jnp.zeros_like(dlb_ref)
            dmacc_scr[...] = jnp.zeros_like(dmacc_scr)
        ln_g = g_ref[...]
        u, du_dpu = _gelu_and_grad(_f32(pu_ref))
        v, dv_dpv = _gelu_and_grad(_f32(pv_ref))
        vn, vhat, rstd = _layer_norm_fwd(v, ln_g, b_ref[...])
        vn_scr[...] = vn.astype(BF16)
        _mix_positions(ws_ref, vn_scr, mix_scr, gw, _tril, bs_ref)
        mixed = mix_scr[...]
        dy = lax.dot_general(dbr_ref[...], wo_ref[...], (((1,), (1,)), ((), ())), preferred_element_type=F32)
        sz, dsz = _silu_and_grad(_f32(pz_ref))
        ds = dy * sz
        dp_ref[2] = (dy * (u * mixed) * dsz).astype(BF16)
        dp_ref[0] = (ds * mixed * du_dpu).astype(BF16)
        dm = ds * u
        dm_scr[...] = dm.astype(BF16)
        for n in range(ts // CHUNK):
            rows = slice(n * CHUNK, (n + 1) * CHUNK)
            dmacc_scr[...] += dm[rows]
            for gi in range(GROUPS):
                cols = slice(gi * gw, (gi + 1) * gw)
                dws_ref[gi] += lax.dot_general(dm_scr[rows, cols], vn_scr[rows, cols], (((1,), (1,)), ((), ())),
                                               preferred_element_type=F32)
        _mix_positions(wst_ref, dm_scr, dvn_scr, gw, _triu)
        dvn = dvn_scr[...]
        dlg_ref[...] += _colsum(dvn * vhat)
        dlb_ref[...] += _colsum(dvn)
        dvh = dvn * ln_g
        dv = rstd * (dvh - jnp.mean(dvh, axis=-1, keepdims=True) - vhat * jnp.mean(dvh * vhat, axis=-1, keepdims=True))
        dp_ref[1] = (dv * dv_dpv).astype(BF16)

        @pl.when(i == n_i - 1)
        def _():
            for gi in range(GROUPS):
                dws_ref[gi] = _tril(dws_ref[gi])
                dbs_ref[:, gi:gi + 1] = jnp.sum(dmacc_scr[:, gi * gw:(gi + 1) * gw], axis=1, keepdims=True)

    def part(q):
        return pl.BlockSpec((None, ts, e), lambda i: (q, i, 0))

    w_spec = pl.BlockSpec((GROUPS, CHUNK, CHUNK), lambda i: (0, 0, 0))
    bs_spec = pl.BlockSpec((CHUNK, GROUPS), lambda i: (0, 0))
    return pl.pallas_call(
        body, name=name,
        out_shape=(jax.ShapeDtypeStruct((3, s, e), BF16), jax.ShapeDtypeStruct((GROUPS, CHUNK, CHUNK), F32),
                   jax.ShapeDtypeStruct((CHUNK, GROUPS), F32), jax.ShapeDtypeStruct((1, e), F32),
                   jax.ShapeDtypeStruct((1, e), F32)),
        grid=(n_i,),
        in_specs=[part(0), part(1), part(2), pl.BlockSpec((ts, d), lambda i: (i, 0)),
                  pl.BlockSpec((e, d), lambda i: (0, 0)), _vec_spec(e), _vec_spec(e), w_spec, w_spec, bs_spec],
        out_specs=(pl.BlockSpec((3, ts, e), lambda i: (0, i, 0)), w_spec, bs_spec, _vec_spec(e), _vec_spec(e)),
        scratch_shapes=[pltpu.VMEM((ts, e), BF16), pltpu.VMEM((ts, e), F32), pltpu.VMEM((ts, e), BF16),
                        pltpu.VMEM((ts, e), F32), pltpu.VMEM((CHUNK, e), F32)],
        compiler_params=_params("arbitrary"),
    )(proj, proj, proj, dbr, w_out, ln_g, ln_b, w_s, w_s_t, b_s_t)


def _mod_fwd(c_all, mod_w, mod_b_cols, name):
    n_layer, d, w = mod_w.shape

    def body(c_ref, w_ref, b_ref, ca_ref, o_ref):
        c = c_ref[...]
        ca = c * _sigmoid(c)
        ca_ref[...] = ca
        for li in range(n_layer):
            o_ref[li * N_DEV:(li + 1) * N_DEV, :] = (
                jnp.dot(ca, w_ref[li], preferred_element_type=F32, precision=lax.Precision.HIGHEST) + b_ref[li])

    return pl.pallas_call(
        body, name=name,
        out_shape=(jax.ShapeDtypeStruct((N_DEV, d), F32), jax.ShapeDtypeStruct((n_layer * N_DEV, w), F32)),
        in_specs=[VMEM_SPEC] * 3, out_specs=(VMEM_SPEC, VMEM_SPEC),
        compiler_params=pltpu.CompilerParams(vmem_limit_bytes=VMEM_LIMIT),
    )(c_all, mod_w, mod_b_cols)


def _adamw(w, g, m, v):
    m = ADAM_B1 * m + (1.0 - ADAM_B1) * g
    v = ADAM_B2 * v + (1.0 - ADAM_B2) * (g * g)
    m_hat = m / (1.0 - ADAM_B1 ** ADAM_STEP)
    v_hat = v / (1.0 - ADAM_B2 ** ADAM_STEP)
    delta = -ADAM_LR * (m_hat / (jnp.sqrt(v_hat) + ADAM_EPS) + ADAM_WD * w)
    return delta, m, v


def _adamw_2d(w, g, m, v, name):
    r, c = w.shape
    tr, tc = _tile(r, 512), _tile(c, 1024)

    def body(w_ref, g_ref, m_ref, v_ref, d_ref, nm_ref, nv_ref):
        d_ref[...], nm_ref[...], nv_ref[...] = _adamw(w_ref[...], g_ref[...], m_ref[...], v_ref[...])

    spec = pl.BlockSpec((tr, tc), lambda i, j: (i, j))
    shape = jax.ShapeDtypeStruct((r, c), F32)
    return pl.pallas_call(
        body, name=name, out_shape=(shape, shape, shape), grid=(r // tr, c // tc),
        in_specs=[spec] * 4, out_specs=(spec, spec, spec), compiler_params=_params("parallel", "parallel"),
    )(w, g, m, v)


def _mod_w_update(ca_t, dmod_cols, w, m, v, name):
    n_layer, d, wd = w.shape
    tr = _tile(d, 256)

    def body(ca_ref, dm_ref, w_ref, m_ref, v_ref, g_ref, d_ref, nm_ref, nv_ref):
        ca = ca_ref[...]
        dm = dm_ref[...]
        g = ca[:, 0:1] * dm[0:1, :]
        for b in range(1, N_DEV):
            g = g + ca[:, b:b + 1] * dm[b:b + 1, :]
        g_ref[...] = g
        d_ref[...], nm_ref[...], nv_ref[...] = _adamw(w_ref[...], g, m_ref[...], v_ref[...])

    spec = pl.BlockSpec((None, tr, wd), lambda l, i: (l, i, 0))
    shape = jax.ShapeDtypeStruct((n_layer, d, wd), F32)
    return pl.pallas_call(
        body, name=name, out_shape=(shape,) * 4, grid=(n_layer, d // tr),
        in_specs=[pl.BlockSpec((tr, N_DEV), lambda l, i: (i, 0)), pl.BlockSpec((None, N_DEV, wd), lambda l, i: (l, 0, 0)),
                  spec, spec, spec],
        out_specs=(spec,) * 4, compiler_params=_params("parallel", "parallel"),
    )(ca_t, dmod_cols, w, m, v)


def _adamw_small(ws, gs, ms, vs, name):
    n = len(ws)

    def body(*refs):
        ins, outs = refs[:4 * n], refs[4 * n:]
        for k in range(n):
            delta, nm, nv = _adamw(ins[k][...], ins[n + k][...], ins[2 * n + k][...], ins[3 * n + k][...])
            outs[3 * k][...] = delta
            outs[3 * k + 1][...] = nm
            outs[3 * k + 2][...] = nv

    out_shape = []
    for w in ws:
        out_shape += [jax.ShapeDtypeStruct(w.shape, F32)] * 3
    outs = pl.pallas_call(
        body, name=name, out_shape=tuple(out_shape),
        in_specs=[VMEM_SPEC] * (4 * n), out_specs=tuple([VMEM_SPEC] * (3 * n)),
        compiler_params=pltpu.CompilerParams(vmem_limit_bytes=VMEM_LIMIT),
    )(*ws, *gs, *ms, *vs)
    return [tuple(outs[3 * k:3 * k + 3]) for k in range(n)]


def _place():
    return lax.axis_index("x"), lax.axis_index("y"), lax.axis_index("c")


def _other_chips(x, y):
    return [(1 - x, y), (x, 1 - y), (1 - x, 1 - y)]


def _all_gather(block, name):
    m_per, n = block.shape

    def body(x_ref, out_ref, send_sems, recv_sems, local_sem):
        x, y, c = _place()
        me, sibling = (x, y, c), (x, y, 1 - c)
        chips = _other_chips(x, y)

        def rows(px, py, pc):
            return out_ref.at[pl.ds((4 * px + 2 * py + pc) * m_per, m_per), :]

        def copy(k, blk, to, src=None):
            return pltpu.make_async_remote_copy(
                src_ref=rows(*blk) if src is None else src, dst_ref=rows(*blk),
                send_sem=send_sems.at[k], recv_sem=recv_sems.at[k], device_id=to, device_id_type=MESH)

        mine = pltpu.make_async_copy(x_ref, rows(*me), local_sem)
        mine.start()
        first = [copy(0, me, sibling, src=x_ref)]
        first += [copy(1 + j, me, (*chip, c), src=x_ref) for j, chip in enumerate(chips)]
        for cp in first:
            cp.start()
        passed = [copy(4 + j, (*chip, c), sibling) for j, chip in enumerate(chips)]
        for j, chip in enumerate(chips):
            copy(1 + j, (*chip, c), me).wait_recv()
            passed[j].start()
        copy(0, sibling, me).wait_recv()
        for j, chip in enumerate(chips):
            copy(4 + j, (*chip, 1 - c), me).wait_recv()
        for cp in first + passed:
            cp.wait_send()
        mine.wait()

    return pl.pallas_call(
        body, name=name, out_shape=jax.ShapeDtypeStruct((N_DEV * m_per, n), F32),
        in_specs=[VMEM_SPEC], out_specs=VMEM_SPEC,
        scratch_shapes=[pltpu.SemaphoreType.DMA((7,)), pltpu.SemaphoreType.DMA((7,)), pltpu.SemaphoreType.DMA],
        compiler_params=pltpu.CompilerParams(vmem_limit_bytes=VMEM_LIMIT),
    )(block)


def _hbm(a):
    return pltpu.with_memory_space_constraint(a, pltpu.HBM)


def _place_own(shard, kind, chip_idx, name):
    r, cdim = shard.shape
    tr, tc = _tile(r, 512), _tile(cdim, 1024)
    nrb, ncb = r // tr, cdim // tc

    def body(k_ref, s_ref, o_ref):
        o_ref[...] = s_ref[...].astype(BF16)

    if kind == "col":
        full, o_map = (r, N_CHIP * cdim), lambda i, j, kr: (i, kr[0] * ncb + j)
    else:
        full, o_map = (N_CHIP * r, cdim), lambda i, j, kr: (kr[0] * nrb + i, j)
    return pl.pallas_call(
        body, name=name, out_shape=jax.ShapeDtypeStruct(full, BF16),
        grid_spec=pltpu.PrefetchScalarGridSpec(
            num_scalar_prefetch=1, grid=(nrb, ncb),
            in_specs=[pl.BlockSpec((tr, tc), lambda i, j, kr: (i, j))],
            out_specs=pl.BlockSpec((tr, tc), o_map)),
        compiler_params=_params("parallel", "parallel"),
    )(chip_idx, shard)


def _weight_window(ref, kind, shard_shape, k, half):
    r, cdim = shard_shape
    hr = r // 2
    if kind == "col":
        return ref.at[pl.ds(half * hr, hr), pl.ds(pl.multiple_of(k * cdim, 128), cdim)]
    return ref.at[pl.ds(pl.multiple_of(k * r + half * hr, 2 * SUBLANE), hr), :]


def _gather_weights_start(fulls, kinds, shard_shapes, groups, after, name):
    n, ng = len(fulls), len(groups)

    def body(*refs):
        ins = refs[:n]
        sems = refs[n + 1:n + 1 + 2 * ng]
        token = refs[2 * n + 1 + 2 * ng]
        x, y, c = _place()
        chips = _other_chips(x, y)
        for g, group in enumerate(groups):
            for pos, (w, j) in enumerate(group):
                own = _weight_window(ins[w], kinds[w], shard_shapes[w], 2 * x + y, c)
                for cc in range(2):
                    pltpu.make_async_remote_copy(
                        src_ref=own, dst_ref=own,
                        send_sem=sems[2 * g].at[2 * pos + cc], recv_sem=sems[2 * g + 1].at[2 * pos + c],
                        device_id=(*chips[j], cc), device_id_type=MESH).start()
        token[...] = jnp.zeros_like(token)

    sem_shapes = []
    for group in groups:
        sem_shapes += [pltpu.SemaphoreType.DMA((2 * len(group),))] * 2
    outs = pl.pallas_call(
        body, name=name,
        out_shape=tuple(sem_shapes) + tuple(pltpu.HBM(f.shape, f.dtype) for f in fulls)
        + (jax.ShapeDtypeStruct((SUBLANE, 128), F32),),
        in_specs=[HBM_SPEC] * n + [ANY_SPEC], out_specs=(SEM_SPEC,) * (2 * ng) + (HBM_SPEC,) * n + (VMEM_SPEC,),
        input_output_aliases={w: 2 * ng + w for w in range(n)},
        compiler_params=pltpu.CompilerParams(has_side_effects=EFFECT),
    )(*[_hbm(f) for f in fulls], after)
    sems = [(outs[2 * g], outs[2 * g + 1]) for g in range(ng)]
    return sems, list(outs[2 * ng:2 * ng + n]), outs[2 * ng + n]


def _gather_weights_wait(sems, group, fulls, kinds, shard_shapes, after, name):
    n = len(fulls)

    def body(*refs):
        ins = refs[:n]
        send_sems, recv_sems = refs[n], refs[n + 1]
        x, y, c = _place()
        chips = _other_chips(x, y)
        for pos, (w, j) in enumerate(group):
            own = _weight_window(ins[w], kinds[w], shard_shapes[w], 2 * x + y, c)
            for cc in range(2):
                landed = _weight_window(ins[w], kinds[w], shard_shapes[w], 2 * chips[j][0] + chips[j][1], cc)
                k = 2 * pos + cc
                pltpu.make_async_remote_copy(
                    src_ref=own, dst_ref=own, send_sem=send_sems.at[k], recv_sem=recv_sems.at[k],
                    device_id=(*chips[j], cc), device_id_type=MESH).wait_send()
                pltpu.make_async_remote_copy(
                    src_ref=landed, dst_ref=landed, send_sem=send_sems.at[k], recv_sem=recv_sems.at[k],
                    device_id=(*chips[j], cc), device_id_type=MESH).wait_recv()

    return list(pl.pallas_call(
        body, name=name, out_shape=tuple(pltpu.HBM(f.shape, f.dtype) for f in fulls),
        in_specs=[HBM_SPEC] * n + [SEM_SPEC, SEM_SPEC, ANY_SPEC], out_specs=(HBM_SPEC,) * n,
        input_output_aliases={w: w for w in range(n)},
        compiler_params=pltpu.CompilerParams(has_side_effects=EFFECT),
    )(*fulls, sems[0], sems[1], after))


def _grad_piece(ref, kind, h, cdim, k, half):
    if kind == "col":
        return ref.at[pl.ds(half * h, h), pl.ds(pl.multiple_of(k * cdim, 128), cdim)]
    return ref.at[k, pl.ds(half * h, h), :]


def _grad_dims(g, kind):
    return (g.shape[0] // 2, g.shape[1] // N_CHIP) if kind == "col" else (g.shape[1] // 2, g.shape[2])


def _reduce_start(grads, kinds, name):
    n = len(grads)
    dims = [_grad_dims(g, kind) for g, kind in zip(grads, kinds)]
    lands = [lax.empty((N_DEV - 1, h, cdim), g.dtype) for g, (h, cdim) in zip(grads, dims)]

    def body(*refs):
        g_ins, land_ins = refs[:n], refs[n:2 * n]
        send_sems, recv_sems = refs[2 * n], refs[2 * n + 1]
        token = refs[4 * n + 2]
        x, y, c = _place()
        for w in range(n):
            h, cdim = dims[w]
            base = (N_DEV - 1) * w
            pltpu.make_async_remote_copy(
                src_ref=_grad_piece(g_ins[w], kinds[w], h, cdim, 2 * x + y, 1 - c), dst_ref=land_ins[w].at[0],
                send_sem=send_sems.at[base], recv_sem=recv_sems.at[base],
                device_id=(x, y, 1 - c), device_id_type=MESH).start()
            for j, chip in enumerate(_other_chips(x, y)):
                for cc in range(2):
                    pltpu.make_async_remote_copy(
                        src_ref=_grad_piece(g_ins[w], kinds[w], h, cdim, 2 * chip[0] + chip[1], cc),
                        dst_ref=land_ins[w].at[1 + 2 * j + c],
                        send_sem=send_sems.at[base + 1 + 2 * j + cc], recv_sem=recv_sems.at[base + 1 + 2 * j + c],
                        device_id=(*chip, cc), device_id_type=MESH).start()
        token[...] = jnp.zeros_like(token)

    sems = pltpu.SemaphoreType.DMA(((N_DEV - 1) * n,))
    outs = pl.pallas_call(
        body, name=name,
        out_shape=(sems, sems) + tuple(pltpu.HBM(a.shape, a.dtype) for a in list(grads) + lands)
        + (jax.ShapeDtypeStruct((SUBLANE, 128), F32),),
        in_specs=[HBM_SPEC] * (2 * n), out_specs=(SEM_SPEC, SEM_SPEC) + (HBM_SPEC,) * (2 * n) + (VMEM_SPEC,),
        input_output_aliases={i: 2 + i for i in range(2 * n)},
        compiler_params=pltpu.CompilerParams(has_side_effects=EFFECT),
    )(*[_hbm(a) for a in list(grads) + lands])
    return outs[0], outs[1], list(outs[2:2 + n]), list(outs[2 + n:2 + 2 * n]), outs[2 + 2 * n]


def _reduce_wait(send_sems, recv_sems, grads, lands, kinds, after, name):
    n = len(grads)
    dims = [_grad_dims(g, kind) for g, kind in zip(grads, kinds)]

    def body(*refs):
        g_ins, land_ins = refs[:n], refs[n:2 * n]
        send_sems, recv_sems = refs[2 * n], refs[2 * n + 1]
        x, y, c = _place()
        for w in range(n):
            h, cdim = dims[w]
            piece = _grad_piece(g_ins[w], kinds[w], h, cdim, 2 * x + y, c)
            for s in range(N_DEV - 1):
                k = (N_DEV - 1) * w + s
                slot = land_ins[w].at[s]
                pltpu.make_async_remote_copy(
                    src_ref=piece, dst_ref=slot, send_sem=send_sems.at[k], recv_sem=recv_sems.at[k],
                    device_id=(x, y, 1 - c), device_id_type=MESH).wait_send()
                pltpu.make_async_remote_copy(
                    src_ref=piece, dst_ref=slot, send_sem=send_sems.at[k], recv_sem=recv_sems.at[k],
                    device_id=(x, y, 1 - c), device_id_type=MESH).wait_recv()

    outs = pl.pallas_call(
        body, name=name, out_shape=tuple(pltpu.HBM(a.shape, a.dtype) for a in list(grads) + list(lands)),
        in_specs=[HBM_SPEC] * (2 * n) + [SEM_SPEC, SEM_SPEC, ANY_SPEC], out_specs=(HBM_SPEC,) * (2 * n),
        input_output_aliases={i: i for i in range(2 * n)},
        compiler_params=pltpu.CompilerParams(has_side_effects=EFFECT),
    )(*grads, *lands, send_sems, recv_sems, after)
    return list(outs[:n]), list(outs[n:])


def _add_pieces(g, land, kind, chip_idx, core_idx, name):
    _, h, cdim = land.shape
    tr, tc = _tile(h, 256), _tile(cdim, 2048)
    nrb, ncb = h // tr, cdim // tc

    def body(k_ref, c_ref, g_ref, l_ref, o_ref):
        acc = g_ref[...].astype(F32)
        for s in range(N_DEV - 1):
            acc = acc + l_ref[s].astype(F32)
        o_ref[...] = acc

    if kind == "col":
        g_spec = pl.BlockSpec((tr, tc), lambda i, j, kr, cr: (cr[0] * nrb + i, kr[0] * ncb + j))
    else:
        g_spec = pl.BlockSpec((None, tr, tc), lambda i, j, kr, cr: (kr[0], cr[0] * nrb + i, j))
    return pl.pallas_call(
        body, name=name, out_shape=jax.ShapeDtypeStruct((2 * h, cdim), F32),
        grid_spec=pltpu.PrefetchScalarGridSpec(
            num_scalar_prefetch=2, grid=(nrb, ncb),
            in_specs=[g_spec, pl.BlockSpec((N_DEV - 1, tr, tc), lambda i, j, kr, cr: (0, i, j))],
            out_specs=pl.BlockSpec((tr, tc), lambda i, j, kr, cr: (cr[0] * nrb + i, j))),
        compiler_params=_params("parallel", "parallel"),
    )(chip_idx, core_idx, g, land)


def _join_halves(shards, name):
    n = len(shards)

    def body(*refs):
        ins, outs = refs[:n], refs[n:2 * n]
        send_sems, recv_sems = refs[2 * n:]
        x, y, c = _place()
        copies = []
        for w in range(n):
            h = shards[w].shape[0] // 2
            cp = pltpu.make_async_remote_copy(
                src_ref=ins[w].at[pl.ds(c * h, h), :], dst_ref=outs[w].at[pl.ds(c * h, h), :],
                send_sem=send_sems.at[w], recv_sem=recv_sems.at[w], device_id=(x, y, 1 - c), device_id_type=MESH)
            cp.start()
            copies.append(cp)
        for w in range(n):
            h = shards[w].shape[0] // 2
            theirs = outs[w].at[pl.ds((1 - c) * h, h), :]
            pltpu.make_async_remote_copy(
                src_ref=theirs, dst_ref=theirs, send_sem=send_sems.at[w], recv_sem=recv_sems.at[w],
                device_id=(x, y, 1 - c), device_id_type=MESH).wait_recv()
        for cp in copies:
            cp.wait_send()

    return pl.pallas_call(
        body, name=name, out_shape=tuple(jax.ShapeDtypeStruct(s.shape, s.dtype) for s in shards),
        in_specs=[HBM_SPEC] * n, out_specs=tuple([HBM_SPEC] * n),
        input_output_aliases={w: w for w in range(n)},
        scratch_shapes=[pltpu.SemaphoreType.DMA((n,))] * 2,
    )(*shards)


def _sum_devices(gathered, rows, name):
    n = gathered.shape[1]

    def body(g_ref, o_ref):
        acc = g_ref[0:rows, :]
        for dev in range(1, N_DEV):
            acc = acc + g_ref[dev * rows:(dev + 1) * rows, :]
        o_ref[...] = acc

    return pl.pallas_call(
        body, name=name, out_shape=jax.ShapeDtypeStruct((rows, n), F32),
        in_specs=[VMEM_SPEC], out_specs=VMEM_SPEC,
        compiler_params=pltpu.CompilerParams(vmem_limit_bytes=VMEM_LIMIT),
    )(gathered)


def _pack(vectors, width):
    flat = [v.reshape(-1) for v in vectors]
    offsets, total = [], 0
    for f in flat:
        offsets.append(total)
        total += f.shape[0]
    rows = -(-total // (width * SUBLANE)) * SUBLANE
    flat.append(jnp.zeros((rows * width - total,), F32))
    return jnp.concatenate(flat).reshape(rows, width), offsets


def kernel(x, c, mod_w, mod_b, norm_g, a_w_in, a_conv_w, a_conv_b, a_w_out, b_w_in, b_ln_g, b_ln_b, b_w_s, b_b_s, b_w_out, final_g, loss_target, m_mod_w, m_mod_b, m_norm_g, m_a_w_in, m_a_conv_w, m_a_conv_b, m_a_w_out, m_b_w_in, m_b_ln_g, m_b_ln_b, m_b_w_s, m_b_b_s, m_b_w_out, m_final_g, v_mod_w, v_mod_b, v_norm_g, v_a_w_in, v_a_conv_w, v_a_conv_b, v_a_w_out, v_b_w_in, v_b_ln_g, v_b_ln_b, v_b_w_s, v_b_b_s, v_b_w_out, v_final_g):
    seq, d = x.shape[1], x.shape[2]
    e = a_conv_b.shape[1]
    wd = mod_w.shape[2]
    ax, ay, ac = _place()
    chip = 2 * ax + ay
    dev = 2 * chip + ac
    chip_idx = jnp.reshape(chip, (1,)).astype(jnp.int32)
    core_idx = jnp.reshape(ac, (1,)).astype(jnp.int32)

    x2d = x[0]
    target = loss_target[0]

    w_names = ["a_w_in", "a_w_out", "b_w_in", "b_w_out"]
    w_kinds = ["col", "row", "col", "row"]
    w_shards = [a_w_in[0], a_w_out[0], b_w_in[0], b_w_out[0]]
    w_shapes = [sh.shape for sh in w_shards]
    placed = [_place_own(sh, kind, chip_idx, "place_" + nm) for nm, sh, kind in zip(w_names, w_shards, w_kinds)]
    a_groups, far_groups = [[(0, 0), (0, 1)]], [[(0, 2)]]
    whole_group = [[(0, 0), (0, 1), (0, 2)]]

    es = e // N_CHIP
    packed0, offs0 = _pack([c, a_conv_w, b_ln_g, b_ln_b], 1024)
    gathered0 = _all_gather(packed0, "gather_params").reshape(N_DEV, -1)
    c_all = gathered0[:, :d]
    per_chip = gathered0[0::2]

    def from_chips_cols(k, rows_):
        got = per_chip[:, offs0[k]:offs0[k] + rows_ * es].reshape(N_CHIP, rows_, es)
        return jnp.transpose(got, (1, 0, 2)).reshape(rows_, e)

    conv_w_full = from_chips_cols(1, 3)
    conv_w = [conv_w_full[k:k + 1] for k in range(3)]
    ln_g, ln_b = from_chips_cols(2, 1), from_chips_cols(3, 1)
    mod_b_cols = lax.dynamic_slice_in_dim(mod_b, chip * wd, wd, axis=1)[:, None, :]
    c_act, mod_part = _mod_fwd(c_all, mod_w, mod_b_cols, "mod_fwd")
    n_layer = mod_w.shape[0]
    mod_gathered = _all_gather(mod_part, "gather_mod")
    a_sems, (wa_in,), a_token = _gather_weights_start(
        placed[:1], w_kinds[:1], w_shapes[:1], a_groups, mod_gathered, "gather_a_w_in_start")
    mod_all = mod_gathered.reshape(N_CHIP, 2, n_layer, N_DEV, wd)[:, 0]
    mod_all = jnp.transpose(mod_all, (1, 2, 0, 3)).reshape(n_layer, N_DEV, N_CHIP * wd)
    mod_me = lax.dynamic_index_in_dim(mod_all, dev, axis=1, keepdims=False)
    shift = [mod_me[l:l + 1, 0:d] for l in range(n_layer)]
    scale = [mod_me[l:l + 1, d:2 * d] for l in range(n_layer)]
    gate = [mod_me[l:l + 1, 2 * d:3 * d] for l in range(n_layer)]

    g0, g1, gf = norm_g[0:1], norm_g[1:2], final_g[None, :]
    h0 = _norm_mod(x2d, g0, scale[0], shift[0], "norm_mod0")

    def slab(r):
        return jnp.bitwise_xor(chip_idx, r)

    def arrived(sems, group, weight, w, after, name):
        return _gather_weights_wait(sems, [(0, j) for _, j in group], [weight], [w_kinds[w]], [w_shapes[w]],
                                    after, name)[0]

    proj0 = _mm_proj_slab(h0, wa_in, None, slab(0), 4, "a_proj_own", a_token)
    wa_in = arrived(a_sems[0], a_groups[0], wa_in, 0, proj0, "gather_wait_near")
    far_sems, (wa_in,), far_token = _gather_weights_start(
        [wa_in], w_kinds[:1], w_shapes[:1], far_groups, wa_in, "gather_a_w_in_far_start")
    proj0 = _mm_proj_slab(h0, wa_in, proj0, slab(2), 4, "a_proj_x", far_token)
    proj0 = _mm_proj_slab(h0, wa_in, proj0, slab(1), 4, "a_proj_y", far_token)
    wa_in = arrived(far_sems[0], far_groups[0], wa_in, 0, proj0, "gather_wait_far")
    def start_whole(w, after, name):
        sems, (flight,), token = _gather_weights_start(
            placed[w:w + 1], w_kinds[w:w + 1], w_shapes[w:w + 1], whole_group, after, name)
        return sems[0], flight, token

    ao_sems, wa_out, ao_token = start_whole(1, wa_in, "gather_a_w_out_start")
    proj0 = _mm_proj_slab(h0, wa_in, proj0, slab(3), 4, "a_proj_far", ao_token)
    wa_out = arrived(ao_sems, whole_group[0], wa_out, 1, proj0, "gather_wait_a_w_out")
    bi_sems, wb_in, bi_token = start_whole(2, wa_out, "gather_b_w_in_start")
    y0, br0 = _conv_fwd(proj0, conv_w, a_conv_b, wa_out, "conv_fwd", bi_token)
    wb_in = arrived(bi_sems, whole_group[0], wb_in, 2, br0, "gather_wait_b_w_in")
    bo_sems, wb_out, bo_token = start_whole(3, wb_in, "gather_b_w_out_start")
    x1, h1 = _residual_norm_mod(x2d, br0, gate[0], g1, scale[1], shift[1], "residual_norm_mod1")
    proj1 = _mm_proj(h1, wb_in, 3, "b_proj", bo_token)
    b_s_t = jnp.transpose(b_b_s[0])
    wb_out = arrived(bo_sems, whole_group[0], wb_out, 3, proj1, "gather_wait_b_w_out")
    y1, br1 = _gmlp_fwd(proj1, ln_g, ln_b, b_w_s[0], b_s_t, wb_out, "gmlp_fwd")
    dx2, dbr1, loss_part, g_final_g, dgate1 = _head(x1, br1, gate[1], gf, target, "head")

    gw_b_out = _mm_dw_out(y1, dbr1, "b_out_dw")
    dproj1, g_w_s, g_b_s_t, g_ln_g, g_ln_b = _gmlp_bwd(
        proj1, dbr1, wb_out, ln_g, ln_b, b_w_s[0], jnp.swapaxes(b_w_s[0], 1, 2), b_s_t, "gmlp_bwd")
    gw_b_in = _mm_dw_in(h1, dproj1, "b_proj_dw")
    b_kinds = ["col", "row"]
    b_send, b_recv, b_grads, b_lands, b_token = _reduce_start(
        [gw_b_in, gw_b_out.reshape(N_CHIP, e // N_CHIP, d)], b_kinds, "reduce_b_start")
    dh1 = _mm_dh(dproj1, wb_in, "b_proj_dx", b_token)
    dx1, dshift1, dscale1, g_g1, dbr0, dgate0 = _norm_mod_bwd(
        dh1, x1, dx2, g1, scale[1], "norm_mod1_bwd", br=br0, gate=gate[0])

    gw_a_out = _mm_dw_out(y0, dbr0, "a_out_dw")
    dproj0, g_w0, g_w1, g_w2, g_conv_b = _conv_bwd(proj0, dbr0, wa_out, conv_w, a_conv_b, "conv_bwd")
    gw_a_in = _mm_dw_in(h0, dproj0, "a_proj_dw")
    a_kinds = ["col", "row"]
    a_send, a_recv, a_grads, a_lands, a_token = _reduce_start(
        [gw_a_in, gw_a_out.reshape(N_CHIP, e // N_CHIP, d)], a_kinds, "reduce_a_start")
    dh0 = _mm_dh(dproj0, wa_in, "a_proj_dx", a_token)
    grad_x, dshift0, dscale0, g_g0 = _norm_mod_bwd(dh0, x2d, dx1, g0, scale[0], "norm_mod0_bwd")

    dmod = jnp.concatenate([dshift0, dscale0, dgate0, dshift1, dscale1, dgate1], axis=1)
    small = [loss_part[0, 0:1], g_final_g, g_g0, g_g1, jnp.concatenate([g_w0, g_w1, g_w2], axis=0), g_conv_b, g_ln_g, g_ln_b,
             jnp.transpose(g_b_s_t), g_w_s, dmod]
    packed, offs = _pack(small, 1024)
    rows = packed.shape[0]
    gathered = _all_gather(packed, "gather_small")
    summed = _sum_devices(gathered, rows, "sum_small").reshape(-1)

    def take(k, shape):
        size = math.prod(shape)
        return summed[offs[k]:offs[k] + size].reshape(shape)

    loss = take(0, ())
    grad_final_g = take(1, (d,))
    grad_norm_g = jnp.concatenate([take(2, (1, d)), take(3, (1, d))], axis=0)
    grad_conv_w_full = take(4, (3, e))
    grad_a_conv_b = take(5, (1, e))
    grad_ln_g_full = take(6, (1, e))
    grad_ln_b_full = take(7, (1, e))
    grad_b_b_s = take(8, (1, GROUPS, CHUNK))
    grad_b_w_s = take(9, (1, GROUPS, CHUNK, CHUNK))
    grad_mod_b = take(10, (n_layer, 3 * d))
    grad_a_conv_w = lax.dynamic_slice_in_dim(grad_conv_w_full, chip * es, es, axis=1)[None]
    grad_b_ln_g = lax.dynamic_slice_in_dim(grad_ln_g_full, chip * es, es, axis=1)
    grad_b_ln_b = lax.dynamic_slice_in_dim(grad_ln_b_full, chip * es, es, axis=1)
    dmod_all = gathered.reshape(N_DEV, rows * 1024)[:, offs[10]:offs[10] + n_layer * 3 * d]
    dmod_all = dmod_all.reshape(N_DEV, n_layer, 3 * d)
    dmod_cols = jnp.transpose(lax.dynamic_slice_in_dim(dmod_all, chip * wd, wd, axis=2), (1, 0, 2))

    b_grads, b_lands = _reduce_wait(b_send, b_recv, b_grads, b_lands, b_kinds, summed, "reduce_b_wait")
    a_grads, a_lands = _reduce_wait(a_send, a_recv, a_grads, a_lands, a_kinds, summed, "reduce_a_wait")
    summed_halves = [
        _add_pieces(g, land, kind, chip_idx, core_idx, "add_pieces_" + nm)
        for g, land, kind, nm in [(a_grads[0], a_lands[0], "col", "a_w_in"), (b_grads[0], b_lands[0], "col", "b_w_in"),
                                  (a_grads[1], a_lands[1], "row", "a_w_out"), (b_grads[1], b_lands[1], "row", "b_w_out")]]
    g_a_w_in, g_b_w_in, g_a_w_out, g_b_w_out = _join_halves(summed_halves, "join_halves")

    grad_mod_w, delta_mod_w, new_m_mod_w, new_v_mod_w = _mod_w_update(
        jnp.transpose(c_act), dmod_cols, mod_w, m_mod_w, v_mod_w, "mod_w_update")
    upd = {}
    for nm, w, g, m, v in [("a_w_in", a_w_in, g_a_w_in, m_a_w_in, v_a_w_in), ("a_w_out", a_w_out, g_a_w_out, m_a_w_out, v_a_w_out),
                           ("b_w_in", b_w_in, g_b_w_in, m_b_w_in, v_b_w_in), ("b_w_out", b_w_out, g_b_w_out, m_b_w_out, v_b_w_out)]:
        upd[nm] = tuple(o[None] for o in _adamw_2d(w[0], g, m[0], v[0], "adamw_" + nm))
    small_w = [("mod_b", mod_b, grad_mod_b, m_mod_b, v_mod_b), ("norm_g", norm_g, grad_norm_g, m_norm_g, v_norm_g),
               ("a_conv_w", a_conv_w, grad_a_conv_w, m_a_conv_w, v_a_conv_w),
               ("a_conv_b", a_conv_b, grad_a_conv_b, m_a_conv_b, v_a_conv_b),
               ("b_ln_g", b_ln_g, grad_b_ln_g, m_b_ln_g, v_b_ln_g), ("b_ln_b", b_ln_b, grad_b_ln_b, m_b_ln_b, v_b_ln_b),
               ("b_w_s", b_w_s, grad_b_w_s, m_b_w_s, v_b_w_s), ("b_b_s", b_b_s, grad_b_b_s, m_b_b_s, v_b_b_s),
               ("final_g", final_g, grad_final_g, m_final_g, v_final_g)]

    def flat2d(a):
        return a.reshape(-1, a.shape[-1])

    res = _adamw_small([flat2d(t[1]) for t in small_w], [flat2d(t[2]) for t in small_w],
                       [flat2d(t[3]) for t in small_w], [flat2d(t[4]) for t in small_w], "adamw_small")
    for (nm, w, _, _, _), r3 in zip(small_w, res):
        upd[nm] = tuple(o.reshape(w.shape) for o in r3)
    upd["mod_w"] = (delta_mod_w, new_m_mod_w, new_v_mod_w)

    grads = {"mod_w": grad_mod_w, "mod_b": grad_mod_b, "norm_g": grad_norm_g, "a_w_in": g_a_w_in[None],
             "a_conv_w": grad_a_conv_w, "a_conv_b": grad_a_conv_b, "a_w_out": g_a_w_out[None], "b_w_in": g_b_w_in[None],
             "b_ln_g": grad_b_ln_g, "b_ln_b": grad_b_ln_b, "b_w_s": grad_b_w_s, "b_b_s": grad_b_b_s,
             "b_w_out": g_b_w_out[None], "final_g": grad_final_g}
    order = ["mod_w", "mod_b", "norm_g", "a_w_in", "a_conv_w", "a_conv_b", "a_w_out", "b_w_in", "b_ln_g", "b_ln_b",
             "b_w_s", "b_b_s", "b_w_out", "final_g"]
    return (loss, grad_x[None], *[grads[k] for k in order], *[upd[k][0] for k in order],
            *[upd[k][1] for k in order], *[upd[k][2] for k in order])
```

```python
import functools
import math

import jax
import jax.numpy as jnp
from jax import lax
from jax.experimental import pallas as pl
from jax.experimental.pallas import tpu as pltpu

F32 = jnp.float32
BF16 = jnp.bfloat16
MESH = pl.DeviceIdType.MESH

N_DEV = 8
N_CHIP = 4
SUBLANE = 8
PACK = 16
ACT = F32
RMS_EPS = 1e-6
LN_EPS = 1e-5
CHUNK = 128
GROUPS = 8
ADAM_LR = 0.001
ADAM_B1 = 0.9
ADAM_B2 = 0.999
ADAM_EPS = 1e-08
ADAM_WD = 0.01
ADAM_STEP = 10
VMEM_LIMIT = 56 << 20

HBM_SPEC = pl.BlockSpec(memory_space=pltpu.HBM)
VMEM_SPEC = pl.BlockSpec(memory_space=pltpu.VMEM)
SEM_SPEC = pl.BlockSpec(memory_space=pltpu.SEMAPHORE)
ANY_SPEC = pl.BlockSpec(memory_space=pl.ANY)
EFFECT = pltpu.SideEffectType.DATAFLOW_SIDE_EFFECTING


def _params(*sem):
    return pltpu.CompilerParams(dimension_semantics=sem, vmem_limit_bytes=VMEM_LIMIT)


def _tile(n, want):
    if n <= want:
        return n
    t = want
    while n % t:
        t -= 128
    return t


def _sigmoid(x):
    return 1.0 / (1.0 + jnp.exp(-x))


def _silu_and_grad(x):
    s = _sigmoid(x)
    return x * s, s * (1.0 + x * (1.0 - s))


def _gelu_and_grad(x):
    cdf = 0.5 * (1.0 + lax.erf(x * (1.0 / math.sqrt(2.0))))
    pdf = jnp.exp(-0.5 * x * x) * (1.0 / math.sqrt(2.0 * math.pi))
    return x * cdf, cdf + x * pdf


def _gelu(x):
    return x * (0.5 * (1.0 + lax.erf(x * (1.0 / math.sqrt(2.0)))))


def _rms(x):
    r = lax.rsqrt(jnp.mean(x * x, axis=-1, keepdims=True) + RMS_EPS)
    return x * r, r


def _rms_bwd(dxn, xn, r):
    return r * (dxn - xn * jnp.mean(dxn * xn, axis=-1, keepdims=True))


def _colsum(a):
    return jnp.sum(a, axis=0, keepdims=True)


def _shift_down(cur, before, k):
    rolled = pltpu.roll(cur, k, 0)
    row = lax.broadcasted_iota(jnp.int32, before.shape, 0)
    head = jnp.where(row < k, pltpu.roll(before, k, 0), rolled[:SUBLANE])
    return jnp.concatenate([head, rolled[SUBLANE:]], axis=0)


def _shift_up(cur, after, k):
    n = cur.shape[0]
    rolled = pltpu.roll(cur, n - k, 0)
    row = lax.broadcasted_iota(jnp.int32, after.shape, 0)
    tail = jnp.where(row >= SUBLANE - k, pltpu.roll(after, SUBLANE - k, 0), rolled[n - SUBLANE:])
    return jnp.concatenate([rolled[:n - SUBLANE], tail], axis=0)


def _after_spec():
    return pl.BlockSpec((SUBLANE, 128), lambda *_: (0, 0))


def _mm_proj(h, w, n_split, name, after):
    s, d = h.shape
    e = w.shape[1] // n_split
    tm, tn = _tile(s, 1024), _tile(e, 2048)
    nj = e // tn

    def body(h_ref, w_ref, after_ref, o_ref):
        o_ref[...] = jnp.dot(h_ref[...], w_ref[...], preferred_element_type=F32).astype(ACT)

    return pl.pallas_call(
        body, name=name,
        out_shape=jax.ShapeDtypeStruct((n_split, s, e), ACT),
        grid=(s // tm, n_split * nj),
        in_specs=[pl.BlockSpec((tm, d), lambda i, j: (i, 0)), pl.BlockSpec((d, tn), lambda i, j: (0, j)),
                  _after_spec()],
        out_specs=pl.BlockSpec((None, tm, tn), lambda i, j: (j // nj, i, j % nj)),
        compiler_params=_params("parallel", "parallel"),
    )(h, w, after)


def _mm_proj_slab(h, w, proj, q_idx, n_split, name, after):
    s, d = h.shape
    e = w.shape[1] // n_split
    tm, tn = _tile(s, 1024), _tile(e, 2048)
    nj = e // tn

    def body(q_ref, h_ref, w_ref, *rest):
        o_ref = rest[-1]
        o_ref[...] = jnp.dot(h_ref[...], w_ref[...], preferred_element_type=F32).astype(ACT)

    in_specs = [pl.BlockSpec((tm, d), lambda i, j, qr: (i, 0)), pl.BlockSpec((d, tn), lambda i, j, qr: (0, qr[0] * nj + j)),
                _after_spec()]
    args = [q_idx, h, w, after]
    aliases = {}
    if proj is not None:
        in_specs.append(ANY_SPEC)
        args.append(proj)
        aliases = {4: 0}
    return pl.pallas_call(
        body, name=name,
        out_shape=jax.ShapeDtypeStruct((n_split, s, e), ACT),
        grid_spec=pltpu.PrefetchScalarGridSpec(
            num_scalar_prefetch=1, grid=(s // tm, nj), in_specs=in_specs,
            out_specs=pl.BlockSpec((None, tm, tn), lambda i, j, qr: (qr[0], i, j))),
        input_output_aliases=aliases,
        compiler_params=_params("parallel", "parallel"),
    )(*args)


def _mm_dh(dp, w, name, after):
    nq, s, e = dp.shape
    d = w.shape[0]
    tm, tn, tk = _tile(s, 1024), _tile(d, 1024), _tile(e, 2048)
    nkq = e // tk
    nk = nq * nkq

    def body(a_ref, b_ref, after_ref, o_ref):
        @pl.when(pl.program_id(2) == 0)
        def _():
            o_ref[...] = jnp.zeros_like(o_ref)
        o_ref[...] += lax.dot_general(a_ref[...], b_ref[...], (((1,), (1,)), ((), ())), preferred_element_type=F32)

    return pl.pallas_call(
        body, name=name,
        out_shape=jax.ShapeDtypeStruct((s, d), F32),
        grid=(s // tm, d // tn, nk),
        in_specs=[pl.BlockSpec((None, tm, tk), lambda i, j, k: (k // nkq, i, k % nkq)),
                  pl.BlockSpec((tn, tk), lambda i, j, k: (j, k)), _after_spec()],
        out_specs=pl.BlockSpec((tm, tn), lambda i, j, k: (i, j)),
        compiler_params=_params("parallel", "parallel", "arbitrary"),
    )(dp, w, after)


def _dw_body(n_t):
    def body(a_ref, b_ref, o_ref, acc_ref):
        t = pl.program_id(2)

        @pl.when(t == 0)
        def _():
            acc_ref[...] = jnp.zeros_like(acc_ref)
        acc_ref[...] += lax.dot_general(a_ref[...], b_ref[...], (((0,), (0,)), ((), ())), preferred_element_type=F32)

        @pl.when(t == n_t - 1)
        def _():
            o_ref[...] = acc_ref[...].astype(o_ref.dtype)
    return body


def _mm_dw_in(h, dp, name):
    s, d = h.shape
    nq, _, e = dp.shape
    tm, tn, tt = _tile(d, 1024), _tile(e, 1024), _tile(s, 2048)
    nj = e // tn

    return pl.pallas_call(
        _dw_body(s // tt), name=name,
        out_shape=jax.ShapeDtypeStruct((d, nq * e), BF16),
        grid=(d // tm, nq * nj, s // tt),
        in_specs=[pl.BlockSpec((tt, tm), lambda i, j, t: (t, i)),
                  pl.BlockSpec((None, tt, tn), lambda i, j, t: (j // nj, t, j % nj))],
        out_specs=pl.BlockSpec((tm, tn), lambda i, j, t: (i, j)),
        scratch_shapes=[pltpu.VMEM((tm, tn), F32)],
        compiler_params=_params("parallel", "parallel", "arbitrary"),
    )(h, dp)


def _mm_dw_out(y, dbr, name):
    s, e = y.shape
    d = dbr.shape[1]
    tm, tn, tt = _tile(e, 1024), _tile(d, 1024), _tile(s, 2048)

    return pl.pallas_call(
        _dw_body(s // tt), name=name,
        out_shape=jax.ShapeDtypeStruct((e, d), BF16),
        grid=(e // tm, d // tn, s // tt),
        in_specs=[pl.BlockSpec((tt, tm), lambda i, j, t: (t, i)), pl.BlockSpec((tt, tn), lambda i, j, t: (t, j))],
        out_specs=pl.BlockSpec((tm, tn), lambda i, j, t: (i, j)),
        scratch_shapes=[pltpu.VMEM((tm, tn), F32)],
        compiler_params=_params("parallel", "parallel", "arbitrary"),
    )(y, dbr)


def _row_spec(ts, d):
    return pl.BlockSpec((ts, d), lambda i: (i, 0))


def _vec_spec(d):
    return pl.BlockSpec((1, d), lambda i: (0, 0))


def _norm_mod(x, g, scale, shift, name):
    s, d = x.shape
    ts = _tile(s, 512)

    def body(x_ref, g_ref, sc_ref, sh_ref, h_ref):
        xn, _ = _rms(x_ref[...])
        h_ref[...] = ((xn * g_ref[...]) * (1.0 + sc_ref[...]) + sh_ref[...]).astype(BF16)

    return pl.pallas_call(
        body, name=name, out_shape=jax.ShapeDtypeStruct((s, d), BF16), grid=(s // ts,),
        in_specs=[_row_spec(ts, d), _vec_spec(d), _vec_spec(d), _vec_spec(d)],
        out_specs=_row_spec(ts, d), compiler_params=_params("parallel"),
    )(x, g, scale, shift)


def _head_tile(x1, br, gate, gf, target):
    d = x1.shape[-1]
    xn, r = _rms(x1 + gate * br)
    err = xn * gf - target
    loss = 0.5 * jnp.sum(jnp.mean(err * err, axis=-1, keepdims=True))
    dout = err * (1.0 / d)
    dx = _rms_bwd(dout * gf, xn, r)
    return dx, gate * dx, loss, _colsum(dout * xn), _colsum(dx * br)


def _norm_mod_bwd(dh, x, dx_in, g, scale, name, br=None, gate=None):
    s, d = x.shape
    ts = _tile(s, 512)
    has_branch = br is not None

    def body(*refs):
        if has_branch:
            (dh_ref, x_ref, dxin_ref, g_ref, sc_ref, br_ref, gate_ref,
             dx_ref, dsh_ref, dsc_ref, dg_ref, dbr_ref, dgate_ref) = refs
        else:
            dh_ref, x_ref, dxin_ref, g_ref, sc_ref, dx_ref, dsh_ref, dsc_ref, dg_ref = refs

        @pl.when(pl.program_id(0) == 0)
        def _():
            dsh_ref[...] = jnp.zeros_like(dsh_ref)
            dsc_ref[...] = jnp.zeros_like(dsc_ref)
            dg_ref[...] = jnp.zeros_like(dg_ref)
            if has_branch:
                dgate_ref[...] = jnp.zeros_like(dgate_ref)
        dh = dh_ref[...]
        g = g_ref[...]
        xn, r = _rms(x_ref[...])
        dsh_ref[...] += _colsum(dh)
        dsc_ref[...] += _colsum(dh * (xn * g))
        da = dh * (1.0 + sc_ref[...])
        dg_ref[...] += _colsum(da * xn)
        dx = dxin_ref[...] + _rms_bwd(da * g, xn, r)
        dx_ref[...] = dx
        if has_branch:
            dgate_ref[...] += _colsum(dx * br_ref[...])
            dbr_ref[...] = (gate_ref[...] * dx).astype(BF16)

    vec_out = jax.ShapeDtypeStruct((1, d), F32)
    out_shape = [jax.ShapeDtypeStruct((s, d), F32), vec_out, vec_out, vec_out]
    out_specs = [_row_spec(ts, d), _vec_spec(d), _vec_spec(d), _vec_spec(d)]
    in_specs = [_row_spec(ts, d)] * 3 + [_vec_spec(d)] * 2
    args = [dh, x, dx_in, g, scale]
    if has_branch:
        out_shape += [jax.ShapeDtypeStruct((s, d), BF16), vec_out]
        out_specs += [_row_spec(ts, d), _vec_spec(d)]
        in_specs += [_row_spec(ts, d), _vec_spec(d)]
        args += [br, gate]
    return pl.pallas_call(
        body, name=name, out_shape=tuple(out_shape), grid=(s // ts,),
        in_specs=in_specs, out_specs=tuple(out_specs), compiler_params=_params("arbitrary"),
    )(*args)


def _conv_tiles(s, e):
    return _tile(s, 512), _tile(e, 512)


def _f32(ref):
    return ref[...].astype(F32)


def _rows_before(halo_ref):
    return _f32(halo_ref)[PACK - SUBLANE:]


def _rows_after(halo_ref):
    return _f32(halo_ref)[:SUBLANE]


def _conv_fwd(proj, conv_w, conv_b, w_out, x, gate, g_next, scale_next, shift_next, name, after):
    _, s, e = proj.shape
    d = w_out.shape[1]
    ts, te = _tile(s, 256), e
    hb = ts // PACK

    def body(bg_ref, cg_ref, xi_ref, z_ref, cgp_ref, xip_ref, w0_ref, w1_ref, w2_ref, b_ref, wo_ref, after_ref,
             x_ref, gate_ref, g_ref, sc_ref, sh_ref, y_ref, br_ref, x1_ref, h_ref):
        cx = _f32(cg_ref) * _f32(xi_ref)
        before = jnp.where(pl.program_id(0) > 0, _rows_before(cgp_ref) * _rows_before(xip_ref), 0.0)
        conv = b_ref[...] + w2_ref[...] * cx
        conv = conv + w0_ref[...] * _shift_down(cx, before, 2)
        conv = conv + w1_ref[...] * _shift_down(cx, before, 1)
        z = _f32(z_ref)
        y = ((z * _sigmoid(z)) * _f32(bg_ref) * conv).astype(BF16)
        y_ref[...] = y
        br = jnp.dot(y, wo_ref[...], preferred_element_type=F32)
        br_ref[...] = br
        x1 = x_ref[...] + gate_ref[...] * br
        x1_ref[...] = x1
        xn, _ = _rms(x1)
        h_ref[...] = ((xn * g_ref[...]) * (1.0 + sc_ref[...]) + sh_ref[...]).astype(BF16)

    def part(q):
        return pl.BlockSpec((None, ts, te), lambda i: (q, i, 0))

    def halo_before(q):
        return pl.BlockSpec((None, PACK, te), lambda i: (q, jnp.maximum(i * hb - 1, 0), 0))

    return pl.pallas_call(
        body, name=name,
        out_shape=(jax.ShapeDtypeStruct((s, e), BF16), jax.ShapeDtypeStruct((s, d), F32),
                   jax.ShapeDtypeStruct((s, d), F32), jax.ShapeDtypeStruct((s, d), BF16)), grid=(s // ts,),
        in_specs=[part(0), part(1), part(2), part(3), halo_before(1), halo_before(2)]
        + [_vec_spec(e)] * 4 + [pl.BlockSpec((e, d), lambda i: (0, 0)), _after_spec(), _row_spec(ts, d)]
        + [_vec_spec(d)] * 4,
        out_specs=(_row_spec(ts, e), _row_spec(ts, d), _row_spec(ts, d), _row_spec(ts, d)),
        compiler_params=_params("parallel"),
    )(proj, proj, proj, proj, proj, proj, *conv_w, conv_b, w_out, after, x, gate, g_next, scale_next, shift_next)


def _conv_bwd(proj, dbr, w_out, conv_w, conv_b, name):
    _, s, e = proj.shape
    d = dbr.shape[1]
    ts, te = _conv_tiles(s, e)
    hb = ts // PACK
    n_i = s // ts
    last_halo = s // PACK - 1
    nt = (((1,), (1,)), ((), ()))

    def body(bg_ref, cg_ref, xi_ref, z_ref, dbr_ref, cgp_ref, xip_ref, bgn_ref, zn_ref, dbrn_ref, wo_ref,
             w0_ref, w1_ref, w2_ref, b_ref, dp_ref, dw0_ref, dw1_ref, dw2_ref, db_ref):
        i = pl.program_id(1)

        @pl.when(i == 0)
        def _():
            for acc in (dw0_ref, dw1_ref, dw2_ref, db_ref):
                acc[...] = jnp.zeros_like(acc)
        wo = wo_ref[...]
        dy = lax.dot_general(dbr_ref[...], wo, nt, preferred_element_type=F32)
        dyn = lax.dot_general(dbrn_ref[...], wo, nt, preferred_element_type=F32)[:SUBLANE]
        bg, cg, xi, z = _f32(bg_ref), _f32(cg_ref), _f32(xi_ref), _f32(z_ref)
        w0, w1, w2 = w0_ref[...], w1_ref[...], w2_ref[...]
        cx = cg * xi
        before = jnp.where(i > 0, _rows_before(cgp_ref) * _rows_before(xip_ref), 0.0)
        cx1 = _shift_down(cx, before, 1)
        cx2 = _shift_down(cx, before, 2)
        conv = b_ref[...] + w2 * cx
        conv = conv + w0 * cx2
        conv = conv + w1 * cx1
        sz, dsz = _silu_and_grad(z)
        dp_ref[3] = (dy * bg * conv * dsz).astype(BF16)
        dp_ref[0] = (dy * sz * conv).astype(BF16)
        dconv = dy * sz * bg
        zn = _rows_after(zn_ref)
        after = jnp.where(i < n_i - 1, dyn * (zn * _sigmoid(zn)) * _rows_after(bgn_ref), 0.0)
        db_ref[...] += _colsum(dconv)
        dw2_ref[...] += _colsum(dconv * cx)
        dw1_ref[...] += _colsum(dconv * cx1)
        dw0_ref[...] += _colsum(dconv * cx2)
        dcx = w2 * dconv + w1 * _shift_up(dconv, after, 1) + w0 * _shift_up(dconv, after, 2)
        dp_ref[1] = (dcx * xi).astype(BF16)
        dp_ref[2] = (dcx * cg).astype(BF16)

    def part(q):
        return pl.BlockSpec((None, ts, te), lambda j, i: (q, i, j))

    def halo_before(q):
        return pl.BlockSpec((None, PACK, te), lambda j, i: (q, jnp.maximum(i * hb - 1, 0), j))

    def halo_after(q):
        return pl.BlockSpec((None, PACK, te), lambda j, i: (q, jnp.minimum((i + 1) * hb, last_halo), j))

    return pl.pallas_call(
        body, name=name,
        out_shape=(jax.ShapeDtypeStruct((4, s, e), BF16),) + (jax.ShapeDtypeStruct((1, e), F32),) * 4,
        grid=(e // te, n_i),
        in_specs=[part(0), part(1), part(2), part(3), pl.BlockSpec((ts, d), lambda j, i: (i, 0)),
                  halo_before(1), halo_before(2), halo_after(0), halo_after(3),
                  pl.BlockSpec((PACK, d), lambda j, i: (jnp.minimum((i + 1) * hb, last_halo), 0)),
                  pl.BlockSpec((te, d), lambda j, i: (j, 0))]
        + [pl.BlockSpec((1, te), lambda j, i: (0, j))] * 4,
        out_specs=(pl.BlockSpec((4, ts, te), lambda j, i: (0, i, j)),) + (pl.BlockSpec((1, te), lambda j, i: (0, j)),) * 4,
        compiler_params=_params("parallel", "arbitrary"),
    )(proj, proj, proj, proj, dbr, proj, proj, proj, proj, dbr, w_out, *conv_w, conv_b)


def _tril(w):
    row = lax.broadcasted_iota(jnp.int32, w.shape, 0)
    col = lax.broadcasted_iota(jnp.int32, w.shape, 1)
    return jnp.where(row >= col, w, 0.0)


def _triu(w):
    row = lax.broadcasted_iota(jnp.int32, w.shape, 0)
    col = lax.broadcasted_iota(jnp.int32, w.shape, 1)
    return jnp.where(row <= col, w, 0.0)


def _layer_norm_fwd(v, g, b):
    mu = jnp.mean(v, axis=-1, keepdims=True)
    vc = v - mu
    rstd = lax.rsqrt(jnp.mean(vc * vc, axis=-1, keepdims=True) + LN_EPS)
    vhat = vc * rstd
    return vhat * g + b, vhat, rstd


GMLP_CHUNKS_PER_STEP = 2


def _gmlp_rows(s):
    return CHUNK * min(GMLP_CHUNKS_PER_STEP, s // CHUNK)


def _mix_positions(w_ref, src_scr, dst_scr, gw, mask, bias_ref=None):
    for gi in range(GROUPS):
        cols = slice(gi * gw, (gi + 1) * gw)
        wm = mask(w_ref[gi]).astype(BF16)
        for n in range(src_scr.shape[0] // CHUNK):
            rows = slice(n * CHUNK, (n + 1) * CHUNK)
            out = jnp.dot(wm, src_scr[rows, cols], preferred_element_type=F32)
            if bias_ref is not None:
                out = out + bias_ref[:, gi:gi + 1]
            dst_scr[rows, cols] = out


def _gmlp_fwd_head(proj, ln_g, ln_b, w_s, b_s_t, w_out, x1, gate, gf, target, name):
    _, s, e = proj.shape
    d = w_out.shape[1]
    gw = e // GROUPS
    ts = _gmlp_rows(s)

    def body(pu_ref, pv_ref, pz_ref, g_ref, b_ref, ws_ref, bs_ref, wo_ref, x1_ref, gate_ref, gf_ref, tg_ref,
             y_ref, dx_ref, dbr_ref, loss_ref, dgf_ref, dgate_ref, vn_scr, mix_scr):
        @pl.when(pl.program_id(0) == 0)
        def _():
            loss_ref[...] = jnp.zeros_like(loss_ref)
            dgf_ref[...] = jnp.zeros_like(dgf_ref)
            dgate_ref[...] = jnp.zeros_like(dgate_ref)
        vn, _, _ = _layer_norm_fwd(_gelu(_f32(pv_ref)), g_ref[...], b_ref[...])
        vn_scr[...] = vn.astype(BF16)
        _mix_positions(ws_ref, vn_scr, mix_scr, gw, _tril, bs_ref)
        z = _f32(pz_ref)
        y = ((z * _sigmoid(z)) * (_gelu(_f32(pu_ref)) * mix_scr[...])).astype(BF16)
        y_ref[...] = y
        br = jnp.dot(y, wo_ref[...], preferred_element_type=F32)
        dx, dbr, loss, dgf, dgate = _head_tile(x1_ref[...], br, gate_ref[...], gf_ref[...], tg_ref[...])
        dx_ref[...] = dx
        dbr_ref[...] = dbr.astype(BF16)
        loss_ref[...] += loss
        dgf_ref[...] += dgf
        dgate_ref[...] += dgate

    def part(q):
        return pl.BlockSpec((None, ts, e), lambda i: (q, i, 0))

    return pl.pallas_call(
        body, name=name,
        out_shape=(jax.ShapeDtypeStruct((s, e), BF16), jax.ShapeDtypeStruct((s, d), F32),
                   jax.ShapeDtypeStruct((s, d), BF16), jax.ShapeDtypeStruct((SUBLANE, 128), F32),
                   jax.ShapeDtypeStruct((1, d), F32), jax.ShapeDtypeStruct((1, d), F32)),
        grid=(s // ts,),
        in_specs=[part(0), part(1), part(2), _vec_spec(e), _vec_spec(e),
                  pl.BlockSpec((GROUPS, CHUNK, CHUNK), lambda i: (0, 0, 0)),
                  pl.BlockSpec((CHUNK, GROUPS), lambda i: (0, 0)), pl.BlockSpec((e, d), lambda i: (0, 0)),
                  _row_spec(ts, d), _vec_spec(d), _vec_spec(d), _row_spec(ts, d)],
        out_specs=(_row_spec(ts, e), _row_spec(ts, d), _row_spec(ts, d),
                   pl.BlockSpec((SUBLANE, 128), lambda i: (0, 0)), _vec_spec(d), _vec_spec(d)),
        scratch_shapes=[pltpu.VMEM((ts, e), BF16), pltpu.VMEM((ts, e), F32)],
        compiler_params=_params("arbitrary"),
    )(proj, proj, proj, ln_g, ln_b, w_s, b_s_t, w_out, x1, gate, gf, target)


def _gmlp_bwd(proj, dbr, w_out, ln_g, ln_b, w_s, w_s_t, b_s_t, name):
    _, s, e = proj.shape
    d = dbr.shape[1]
    gw = e // GROUPS
    ts = _gmlp_rows(s)
    n_i = s // ts

    def body(pu_ref, pv_ref, pz_ref, dbr_ref, wo_ref, g_ref, b_ref, ws_ref, wst_ref, bs_ref,
             dp_ref, dws_ref, dbs_ref, dlg_ref, dlb_ref, vn_scr, mix_scr, dm_scr, dvn_scr, dmacc_scr):
        i = pl.program_id(0)

        @pl.when(i == 0)
        def _():
            dws_ref[...] = jnp.zeros_like(dws_ref)
            dlg_ref[...] = jnp.zeros_like(dlg_ref)
            dlb_ref[...] = jnp.zeros_like(dlb_ref)
            dmacc_scr[...] = jnp.zeros_like(dmacc_scr)
        ln_g = g_ref[...]
        u, du_dpu = _gelu_and_grad(_f32(pu_ref))
        v, dv_dpv = _gelu_and_grad(_f32(pv_ref))
        vn, vhat, rstd = _layer_norm_fwd(v, ln_g, b_ref[...])
        vn_scr[...] = vn.astype(BF16)
        _mix_positions(ws_ref, vn_scr, mix_scr, gw, _tril, bs_ref)
        mixed = mix_scr[...]
        dy = lax.dot_general(dbr_ref[...], wo_ref[...], (((1,), (1,)), ((), ())), preferred_element_type=F32)
        sz, dsz = _silu_and_grad(_f32(pz_ref))
        ds = dy * sz
        dp_ref[2] = (dy * (u * mixed) * dsz).astype(BF16)
        dp_ref[0] = (ds * mixed * du_dpu).astype(BF16)
        dm = ds * u
        dm_scr[...] = dm.astype(BF16)
        for n in range(ts // CHUNK):
            rows = slice(n * CHUNK, (n + 1) * CHUNK)
            dmacc_scr[...] += dm[rows]
            for gi in range(GROUPS):
                cols = slice(gi * gw, (gi + 1) * gw)
                dws_ref[gi] += lax.dot_general(dm_scr[rows, cols], vn_scr[rows, cols], (((1,), (1,)), ((), ())),
                                               preferred_element_type=F32)
        _mix_positions(wst_ref, dm_scr, dvn_scr, gw, _triu)
        dvn = dvn_scr[...]
        dlg_ref[...] += _colsum(dvn * vhat)
        dlb_ref[...] += _colsum(dvn)
        dvh = dvn * ln_g
        dv = rstd * (dvh - jnp.mean(dvh, axis=-1, keepdims=True) - vhat * jnp.mean(dvh * vhat, axis=-1, keepdims=True))
        dp_ref[1] = (dv * dv_dpv).astype(BF16)

        @pl.when(i == n_i - 1)
        def _():
            for gi in range(GROUPS):
                dws_ref[gi] = _tril(dws_ref[gi])
                dbs_ref[:, gi:gi + 1] = jnp.sum(dmacc_scr[:, gi * gw:(gi + 1) * gw], axis=1, keepdims=True)

    def part(q):
        return pl.BlockSpec((None, ts, e), lambda i: (q, i, 0))

    w_spec = pl.BlockSpec((GROUPS, CHUNK, CHUNK), lambda i: (0, 0, 0))
    bs_spec = pl.BlockSpec((CHUNK, GROUPS), lambda i: (0, 0))
    return pl.pallas_call(
        body, name=name,
        out_shape=(jax.ShapeDtypeStruct((3, s, e), BF16), jax.ShapeDtypeStruct((GROUPS, CHUNK, CHUNK), F32),
                   jax.ShapeDtypeStruct((CHUNK, GROUPS), F32), jax.ShapeDtypeStruct((1, e), F32),
                   jax.ShapeDtypeStruct((1, e), F32)),
        grid=(n_i,),
        in_specs=[part(0), part(1), part(2), pl.BlockSpec((ts, d), lambda i: (i, 0)),
                  pl.BlockSpec((e, d), lambda i: (0, 0)), _vec_spec(e), _vec_spec(e), w_spec, w_spec, bs_spec],
        out_specs=(pl.BlockSpec((3, ts, e), lambda i: (0, i, 0)), w_spec, bs_spec, _vec_spec(e), _vec_spec(e)),
        scratch_shapes=[pltpu.VMEM((ts, e), BF16), pltpu.VMEM((ts, e), F32), pltpu.VMEM((ts, e), BF16),
                        pltpu.VMEM((ts, e), F32), pltpu.VMEM((CHUNK, e), F32)],
        compiler_params=_params("arbitrary"),
    )(proj, proj, proj, dbr, w_out, ln_g, ln_b, w_s, w_s_t, b_s_t)


def _mod_fwd(c_all, mod_w, mod_b_cols, name):
    n_layer, d, w = mod_w.shape

    def body(c_ref, w_ref, b_ref, ca_ref, o_ref):
        c = c_ref[...]
        ca = c * _sigmoid(c)
        ca_ref[...] = ca
        for li in range(n_layer):
            o_ref[li * N_DEV:(li + 1) * N_DEV, :] = (
                jnp.dot(ca, w_ref[li], preferred_element_type=F32, precision=lax.Precision.HIGHEST) + b_ref[li])

    return pl.pallas_call(
        body, name=name,
        out_shape=(jax.ShapeDtypeStruct((N_DEV, d), F32), jax.ShapeDtypeStruct((n_layer * N_DEV, w), F32)),
        in_specs=[VMEM_SPEC] * 3, out_specs=(VMEM_SPEC, VMEM_SPEC),
        compiler_params=pltpu.CompilerParams(vmem_limit_bytes=VMEM_LIMIT),
    )(c_all, mod_w, mod_b_cols)


def _adamw(w, g, m, v):
    m = ADAM_B1 * m + (1.0 - ADAM_B1) * g
    v = ADAM_B2 * v + (1.0 - ADAM_B2) * (g * g)
    m_hat = m / (1.0 - ADAM_B1 ** ADAM_STEP)
    v_hat = v / (1.0 - ADAM_B2 ** ADAM_STEP)
    delta = -ADAM_LR * (m_hat / (jnp.sqrt(v_hat) + ADAM_EPS) + ADAM_WD * w)
    return delta, m, v


def _adamw_2d(w, g, m, v, name):
    r, c = w.shape
    tr, tc = _tile(r, 512), _tile(c, 1024)

    def body(w_ref, g_ref, m_ref, v_ref, go_ref, d_ref, nm_ref, nv_ref):
        g = g_ref[...]
        go_ref[...] = g
        d_ref[...], nm_ref[...], nv_ref[...] = _adamw(w_ref[...], g, m_ref[...], v_ref[...])

    spec = pl.BlockSpec((tr, tc), lambda i, j: (i, j))
    shape = jax.ShapeDtypeStruct((r, c), F32)
    return pl.pallas_call(
        body, name=name, out_shape=(shape,) * 4, grid=(r // tr, c // tc),
        in_specs=[spec] * 4, out_specs=(spec,) * 4, compiler_params=_params("parallel", "parallel"),
    )(w, g, m, v)


def _mod_w_update(ca_t, dmod_cols, w, m, v, name):
    n_layer, d, wd = w.shape
    tr = _tile(d, 256)

    def body(ca_ref, dm_ref, w_ref, m_ref, v_ref, g_ref, d_ref, nm_ref, nv_ref):
        ca = ca_ref[...]
        dm = dm_ref[...]
        g = ca[:, 0:1] * dm[0:1, :]
        for b in range(1, N_DEV):
            g = g + ca[:, b:b + 1] * dm[b:b + 1, :]
        g_ref[...] = g
        d_ref[...], nm_ref[...], nv_ref[...] = _adamw(w_ref[...], g, m_ref[...], v_ref[...])

    spec = pl.BlockSpec((None, tr, wd), lambda l, i: (l, i, 0))
    shape = jax.ShapeDtypeStruct((n_layer, d, wd), F32)
    return pl.pallas_call(
        body, name=name, out_shape=(shape,) * 4, grid=(n_layer, d // tr),
        in_specs=[pl.BlockSpec((tr, N_DEV), lambda l, i: (i, 0)), pl.BlockSpec((None, N_DEV, wd), lambda l, i: (l, 0, 0)),
                  spec, spec, spec],
        out_specs=(spec,) * 4, compiler_params=_params("parallel", "parallel"),
    )(ca_t, dmod_cols, w, m, v)


def _adamw_small(ws, gs, ms, vs, name):
    n = len(ws)

    def body(*refs):
        ins, outs = refs[:4 * n], refs[4 * n:]
        for k in range(n):
            delta, nm, nv = _adamw(ins[k][...], ins[n + k][...], ins[2 * n + k][...], ins[3 * n + k][...])
            outs[3 * k][...] = delta
            outs[3 * k + 1][...] = nm
            outs[3 * k + 2][...] = nv

    out_shape = []
    for w in ws:
        out_shape += [jax.ShapeDtypeStruct(w.shape, F32)] * 3
    outs = pl.pallas_call(
        body, name=name, out_shape=tuple(out_shape),
        in_specs=[VMEM_SPEC] * (4 * n), out_specs=tuple([VMEM_SPEC] * (3 * n)),
        compiler_params=pltpu.CompilerParams(vmem_limit_bytes=VMEM_LIMIT),
    )(*ws, *gs, *ms, *vs)
    return [tuple(outs[3 * k:3 * k + 3]) for k in range(n)]


def _place():
    return lax.axis_index("x"), lax.axis_index("y"), lax.axis_index("c")


def _other_chips(x, y):
    return [(1 - x, y), (x, 1 - y), (1 - x, 1 - y)]


def _all_gather(block, name):
    m_per, n = block.shape

    def body(x_ref, out_ref, send_sems, recv_sems, local_sem):
        x, y, c = _place()
        me, sibling = (x, y, c), (x, y, 1 - c)
        chips = _other_chips(x, y)

        def rows(px, py, pc):
            return out_ref.at[pl.ds((4 * px + 2 * py + pc) * m_per, m_per), :]

        def copy(k, blk, to, src=None):
            return pltpu.make_async_remote_copy(
                src_ref=rows(*blk) if src is None else src, dst_ref=rows(*blk),
                send_sem=send_sems.at[k], recv_sem=recv_sems.at[k], device_id=to, device_id_type=MESH)

        mine = pltpu.make_async_copy(x_ref, rows(*me), local_sem)
        mine.start()
        first = [copy(0, me, sibling, src=x_ref)]
        first += [copy(1 + j, me, (*chip, c), src=x_ref) for j, chip in enumerate(chips)]
        for cp in first:
            cp.start()
        passed = [copy(4 + j, (*chip, c), sibling) for j, chip in enumerate(chips)]
        for j, chip in enumerate(chips):
            copy(1 + j, (*chip, c), me).wait_recv()
            passed[j].start()
        copy(0, sibling, me).wait_recv()
        for j, chip in enumerate(chips):
            copy(4 + j, (*chip, 1 - c), me).wait_recv()
        for cp in first + passed:
            cp.wait_send()
        mine.wait()

    return pl.pallas_call(
        body, name=name, out_shape=jax.ShapeDtypeStruct((N_DEV * m_per, n), F32),
        in_specs=[VMEM_SPEC], out_specs=VMEM_SPEC,
        scratch_shapes=[pltpu.SemaphoreType.DMA((7,)), pltpu.SemaphoreType.DMA((7,)), pltpu.SemaphoreType.DMA],
        compiler_params=pltpu.CompilerParams(vmem_limit_bytes=VMEM_LIMIT),
    )(block)


def _hbm(a):
    return pltpu.with_memory_space_constraint(a, pltpu.HBM)


def _place_own(shard, kind, chip_idx, name):
    r, cdim = shard.shape
    tr, tc = _tile(r, 512), _tile(cdim, 1024)
    nrb, ncb = r // tr, cdim // tc

    def body(k_ref, s_ref, o_ref):
        o_ref[...] = s_ref[...].astype(BF16)

    if kind == "col":
        full, o_map = (r, N_CHIP * cdim), lambda i, j, kr: (i, kr[0] * ncb + j)
    else:
        full, o_map = (N_CHIP * r, cdim), lambda i, j, kr: (kr[0] * nrb + i, j)
    return pl.pallas_call(
        body, name=name, out_shape=jax.ShapeDtypeStruct(full, BF16),
        grid_spec=pltpu.PrefetchScalarGridSpec(
            num_scalar_prefetch=1, grid=(nrb, ncb),
            in_specs=[pl.BlockSpec((tr, tc), lambda i, j, kr: (i, j))],
            out_specs=pl.BlockSpec((tr, tc), o_map)),
        compiler_params=_params("parallel", "parallel"),
    )(chip_idx, shard)


def _weight_window(ref, kind, shard_shape, k, half):
    r, cdim = shard_shape
    hr = r // 2
    if kind == "col":
        return ref.at[pl.ds(half * hr, hr), pl.ds(pl.multiple_of(k * cdim, 128), cdim)]
    return ref.at[pl.ds(pl.multiple_of(k * r + half * hr, 2 * SUBLANE), hr), :]


def _gather_weights_start(fulls, kinds, shard_shapes, groups, after, name):
    n, ng = len(fulls), len(groups)

    def body(*refs):
        ins = refs[:n]
        sems = refs[n + 1:n + 1 + 2 * ng]
        token = refs[2 * n + 1 + 2 * ng]
        x, y, c = _place()
        chips = _other_chips(x, y)
        for g, group in enumerate(groups):
            for pos, (w, j) in enumerate(group):
                own = _weight_window(ins[w], kinds[w], shard_shapes[w], 2 * x + y, c)
                for cc in range(2):
                    pltpu.make_async_remote_copy(
                        src_ref=own, dst_ref=own,
                        send_sem=sems[2 * g].at[2 * pos + cc], recv_sem=sems[2 * g + 1].at[2 * pos + c],
                        device_id=(*chips[j], cc), device_id_type=MESH).start()
        token[...] = jnp.zeros_like(token)

    sem_shapes = []
    for group in groups:
        sem_shapes += [pltpu.SemaphoreType.DMA((2 * len(group),))] * 2
    outs = pl.pallas_call(
        body, name=name,
        out_shape=tuple(sem_shapes) + tuple(pltpu.HBM(f.shape, f.dtype) for f in fulls)
        + (jax.ShapeDtypeStruct((SUBLANE, 128), F32),),
        in_specs=[HBM_SPEC] * n + [ANY_SPEC], out_specs=(SEM_SPEC,) * (2 * ng) + (HBM_SPEC,) * n + (VMEM_SPEC,),
        input_output_aliases={w: 2 * ng + w for w in range(n)},
        compiler_params=pltpu.CompilerParams(has_side_effects=EFFECT),
    )(*[_hbm(f) for f in fulls], after)
    sems = [(outs[2 * g], outs[2 * g + 1]) for g in range(ng)]
    return sems, list(outs[2 * ng:2 * ng + n]), outs[2 * ng + n]


def _gather_weights_wait(sems, group, fulls, kinds, shard_shapes, after, name):
    n = len(fulls)

    def body(*refs):
        ins = refs[:n]
        send_sems, recv_sems = refs[n], refs[n + 1]
        x, y, c = _place()
        chips = _other_chips(x, y)
        for pos, (w, j) in enumerate(group):
            own = _weight_window(ins[w], kinds[w], shard_shapes[w], 2 * x + y, c)
            for cc in range(2):
                landed = _weight_window(ins[w], kinds[w], shard_shapes[w], 2 * chips[j][0] + chips[j][1], cc)
                k = 2 * pos + cc
                pltpu.make_async_remote_copy(
                    src_ref=own, dst_ref=own, send_sem=send_sems.at[k], recv_sem=recv_sems.at[k],
                    device_id=(*chips[j], cc), device_id_type=MESH).wait_send()
                pltpu.make_async_remote_copy(
                    src_ref=landed, dst_ref=landed, send_sem=send_sems.at[k], recv_sem=recv_sems.at[k],
                    device_id=(*chips[j], cc), device_id_type=MESH).wait_recv()

    return list(pl.pallas_call(
        body, name=name, out_shape=tuple(pltpu.HBM(f.shape, f.dtype) for f in fulls),
        in_specs=[HBM_SPEC] * n + [SEM_SPEC, SEM_SPEC, ANY_SPEC], out_specs=(HBM_SPEC,) * n,
        input_output_aliases={w: w for w in range(n)},
        compiler_params=pltpu.CompilerParams(has_side_effects=EFFECT),
    )(*fulls, sems[0], sems[1], after))


def _grad_piece(ref, kind, h, cdim, k, half):
    if kind == "col":
        return ref.at[pl.ds(half * h, h), pl.ds(pl.multiple_of(k * cdim, 128), cdim)]
    return ref.at[k, pl.ds(half * h, h), :]


def _grad_dims(g, kind):
    return (g.shape[0] // 2, g.shape[1] // N_CHIP) if kind == "col" else (g.shape[1] // 2, g.shape[2])


def _reduce_start(grads, kinds, name):
    n = len(grads)
    dims = [_grad_dims(g, kind) for g, kind in zip(grads, kinds)]
    lands = [lax.empty((N_DEV - 1, h, cdim), g.dtype) for g, (h, cdim) in zip(grads, dims)]

    def body(*refs):
        g_ins, land_ins = refs[:n], refs[n:2 * n]
        send_sems, recv_sems = refs[2 * n], refs[2 * n + 1]
        token = refs[4 * n + 2]
        x, y, c = _place()
        for w in range(n):
            h, cdim = dims[w]
            base = (N_DEV - 1) * w
            pltpu.make_async_remote_copy(
                src_ref=_grad_piece(g_ins[w], kinds[w], h, cdim, 2 * x + y, 1 - c), dst_ref=land_ins[w].at[0],
                send_sem=send_sems.at[base], recv_sem=recv_sems.at[base],
                device_id=(x, y, 1 - c), device_id_type=MESH).start()
            for j, chip in enumerate(_other_chips(x, y)):
                for cc in range(2):
                    pltpu.make_async_remote_copy(
                        src_ref=_grad_piece(g_ins[w], kinds[w], h, cdim, 2 * chip[0] + chip[1], cc),
                        dst_ref=land_ins[w].at[1 + 2 * j + c],
                        send_sem=send_sems.at[base + 1 + 2 * j + cc], recv_sem=recv_sems.at[base + 1 + 2 * j + c],
                        device_id=(*chip, cc), device_id_type=MESH).start()
        token[...] = jnp.zeros_like(token)

    sems = pltpu.SemaphoreType.DMA(((N_DEV - 1) * n,))
    outs = pl.pallas_call(
        body, name=name,
        out_shape=(sems, sems) + tuple(pltpu.HBM(a.shape, a.dtype) for a in list(grads) + lands)
        + (jax.ShapeDtypeStruct((SUBLANE, 128), F32),),
        in_specs=[HBM_SPEC] * (2 * n), out_specs=(SEM_SPEC, SEM_SPEC) + (HBM_SPEC,) * (2 * n) + (VMEM_SPEC,),
        input_output_aliases={i: 2 + i for i in range(2 * n)},
        compiler_params=pltpu.CompilerParams(has_side_effects=EFFECT),
    )(*[_hbm(a) for a in list(grads) + lands])
    return outs[0], outs[1], list(outs[2:2 + n]), list(outs[2 + n:2 + 2 * n]), outs[2 + 2 * n]


def _reduce_wait(send_sems, recv_sems, grads, lands, kinds, after, name):
    n = len(grads)
    dims = [_grad_dims(g, kind) for g, kind in zip(grads, kinds)]

    def body(*refs):
        g_ins, land_ins = refs[:n], refs[n:2 * n]
        send_sems, recv_sems = refs[2 * n], refs[2 * n + 1]
        x, y, c = _place()
        for w in range(n):
            h, cdim = dims[w]
            piece = _grad_piece(g_ins[w], kinds[w], h, cdim, 2 * x + y, c)
            for s in range(N_DEV - 1):
                k = (N_DEV - 1) * w + s
                slot = land_ins[w].at[s]
                pltpu.make_async_remote_copy(
                    src_ref=piece, dst_ref=slot, send_sem=send_sems.at[k], recv_sem=recv_sems.at[k],
                    device_id=(x, y, 1 - c), device_id_type=MESH).wait_send()
                pltpu.make_async_remote_copy(
                    src_ref=piece, dst_ref=slot, send_sem=send_sems.at[k], recv_sem=recv_sems.at[k],
                    device_id=(x, y, 1 - c), device_id_type=MESH).wait_recv()

    outs = pl.pallas_call(
        body, name=name, out_shape=tuple(pltpu.HBM(a.shape, a.dtype) for a in list(grads) + list(lands)),
        in_specs=[HBM_SPEC] * (2 * n) + [SEM_SPEC, SEM_SPEC, ANY_SPEC], out_specs=(HBM_SPEC,) * (2 * n),
        input_output_aliases={i: i for i in range(2 * n)},
        compiler_params=pltpu.CompilerParams(has_side_effects=EFFECT),
    )(*grads, *lands, send_sems, recv_sems, after)
    return list(outs[:n]), list(outs[n:])


def _add_pieces(g, land, kind, chip_idx, core_idx, name):
    _, h, cdim = land.shape
    tr, tc = _tile(h, 256), _tile(cdim, 2048)
    nrb, ncb = h // tr, cdim // tc

    def body(k_ref, c_ref, g_ref, l_ref, o_ref):
        acc = g_ref[...].astype(F32)
        for s in range(N_DEV - 1):
            acc = acc + l_ref[s].astype(F32)
        o_ref[...] = acc

    if kind == "col":
        g_spec = pl.BlockSpec((tr, tc), lambda i, j, kr, cr: (cr[0] * nrb + i, kr[0] * ncb + j))
    else:
        g_spec = pl.BlockSpec((None, tr, tc), lambda i, j, kr, cr: (kr[0], cr[0] * nrb + i, j))
    return pl.pallas_call(
        body, name=name, out_shape=jax.ShapeDtypeStruct((2 * h, cdim), F32),
        grid_spec=pltpu.PrefetchScalarGridSpec(
            num_scalar_prefetch=2, grid=(nrb, ncb),
            in_specs=[g_spec, pl.BlockSpec((N_DEV - 1, tr, tc), lambda i, j, kr, cr: (0, i, j))],
            out_specs=pl.BlockSpec((tr, tc), lambda i, j, kr, cr: (cr[0] * nrb + i, j))),
        compiler_params=_params("parallel", "parallel"),
    )(chip_idx, core_idx, g, land)


def _join_halves(shards, name):
    n = len(shards)

    def body(*refs):
        ins, outs = refs[:n], refs[n:2 * n]
        send_sems, recv_sems = refs[2 * n:]
        x, y, c = _place()
        copies = []
        for w in range(n):
            h = shards[w].shape[0] // 2
            cp = pltpu.make_async_remote_copy(
                src_ref=ins[w].at[pl.ds(c * h, h), :], dst_ref=outs[w].at[pl.ds(c * h, h), :],
                send_sem=send_sems.at[w], recv_sem=recv_sems.at[w], device_id=(x, y, 1 - c), device_id_type=MESH)
            cp.start()
            copies.append(cp)
        for w in range(n):
            h = shards[w].shape[0] // 2
            theirs = outs[w].at[pl.ds((1 - c) * h, h), :]
            pltpu.make_async_remote_copy(
                src_ref=theirs, dst_ref=theirs, send_sem=send_sems.at[w], recv_sem=recv_sems.at[w],
                device_id=(x, y, 1 - c), device_id_type=MESH).wait_recv()
        for cp in copies:
            cp.wait_send()

    return pl.pallas_call(
        body, name=name, out_shape=tuple(jax.ShapeDtypeStruct(s.shape, s.dtype) for s in shards),
        in_specs=[HBM_SPEC] * n, out_specs=tuple([HBM_SPEC] * n),
        input_output_aliases={w: w for w in range(n)},
        scratch_shapes=[pltpu.SemaphoreType.DMA((n,))] * 2,
    )(*shards)


def _sum_devices(gathered, rows, name):
    n = gathered.shape[1]

    def body(g_ref, o_ref):
        acc = g_ref[0:rows, :]
        for dev in range(1, N_DEV):
            acc = acc + g_ref[dev * rows:(dev + 1) * rows, :]
        o_ref[...] = acc

    return pl.pallas_call(
        body, name=name, out_shape=jax.ShapeDtypeStruct((rows, n), F32),
        in_specs=[VMEM_SPEC], out_specs=VMEM_SPEC,
        compiler_params=pltpu.CompilerParams(vmem_limit_bytes=VMEM_LIMIT),
    )(gathered)


def _pack(vectors, width):
    flat = [v.reshape(-1) for v in vectors]
    offsets, total = [], 0
    for f in flat:
        offsets.append(total)
        total += f.shape[0]
    rows = -(-total // (width * SUBLANE)) * SUBLANE
    flat.append(jnp.zeros((rows * width - total,), F32))
    return jnp.concatenate(flat).reshape(rows, width), offsets


def kernel(x, c, mod_w, mod_b, norm_g, a_w_in, a_conv_w, a_conv_b, a_w_out, b_w_in, b_ln_g, b_ln_b, b_w_s, b_b_s, b_w_out, final_g, loss_target, m_mod_w, m_mod_b, m_norm_g, m_a_w_in, m_a_conv_w, m_a_conv_b, m_a_w_out, m_b_w_in, m_b_ln_g, m_b_ln_b, m_b_w_s, m_b_b_s, m_b_w_out, m_final_g, v_mod_w, v_mod_b, v_norm_g, v_a_w_in, v_a_conv_w, v_a_conv_b, v_a_w_out, v_b_w_in, v_b_ln_g, v_b_ln_b, v_b_w_s, v_b_b_s, v_b_w_out, v_final_g):
    seq, d = x.shape[1], x.shape[2]
    e = a_conv_b.shape[1]
    wd = mod_w.shape[2]
    ax, ay, ac = _place()
    chip = 2 * ax + ay
    dev = 2 * chip + ac
    chip_idx = jnp.reshape(chip, (1,)).astype(jnp.int32)
    core_idx = jnp.reshape(ac, (1,)).astype(jnp.int32)

    x2d = x[0]
    target = loss_target[0]

    w_names = ["a_w_in", "a_w_out", "b_w_in", "b_w_out"]
    w_kinds = ["col", "row", "col", "row"]
    w_shards = [a_w_in[0], a_w_out[0], b_w_in[0], b_w_out[0]]
    w_shapes = [sh.shape for sh in w_shards]
    placed = [_place_own(sh, kind, chip_idx, "place_" + nm) for nm, sh, kind in zip(w_names, w_shards, w_kinds)]
    a_groups, far_groups = [[(0, 0), (0, 1)]], [[(0, 2)]]
    whole_group = [[(0, 0), (0, 1), (0, 2)]]

    es = e // N_CHIP
    packed0, offs0 = _pack([c, a_conv_w, b_ln_g, b_ln_b], 1024)
    gathered0 = _all_gather(packed0, "gather_params").reshape(N_DEV, -1)
    c_all = gathered0[:, :d]
    per_chip = gathered0[0::2]

    def from_chips_cols(k, rows_):
        got = per_chip[:, offs0[k]:offs0[k] + rows_ * es].reshape(N_CHIP, rows_, es)
        return jnp.transpose(got, (1, 0, 2)).reshape(rows_, e)

    conv_w_full = from_chips_cols(1, 3)
    conv_w = [conv_w_full[k:k + 1] for k in range(3)]
    ln_g, ln_b = from_chips_cols(2, 1), from_chips_cols(3, 1)
    mod_b_cols = lax.dynamic_slice_in_dim(mod_b, chip * wd, wd, axis=1)[:, None, :]
    c_act, mod_part = _mod_fwd(c_all, mod_w, mod_b_cols, "mod_fwd")
    n_layer = mod_w.shape[0]
    mod_gathered = _all_gather(mod_part, "gather_mod")
    a_sems, (wa_in,), a_token = _gather_weights_start(
        placed[:1], w_kinds[:1], w_shapes[:1], a_groups, mod_gathered, "gather_a_w_in_start")
    mod_all = mod_gathered.reshape(N_CHIP, 2, n_layer, N_DEV, wd)[:, 0]
    mod_all = jnp.transpose(mod_all, (1, 2, 0, 3)).reshape(n_layer, N_DEV, N_CHIP * wd)
    mod_me = lax.dynamic_index_in_dim(mod_all, dev, axis=1, keepdims=False)
    shift = [mod_me[l:l + 1, 0:d] for l in range(n_layer)]
    scale = [mod_me[l:l + 1, d:2 * d] for l in range(n_layer)]
    gate = [mod_me[l:l + 1, 2 * d:3 * d] for l in range(n_layer)]

    g0, g1, gf = norm_g[0:1], norm_g[1:2], final_g[None, :]
    h0 = _norm_mod(x2d, g0, scale[0], shift[0], "norm_mod0")

    def slab(r):
        return jnp.bitwise_xor(chip_idx, r)

    def arrived(sems, group, weight, w, after, name):
        return _gather_weights_wait(sems, [(0, j) for _, j in group], [weight], [w_kinds[w]], [w_shapes[w]],
                                    after, name)[0]

    proj0 = _mm_proj_slab(h0, wa_in, None, slab(0), 4, "a_proj_own", a_token)
    wa_in = arrived(a_sems[0], a_groups[0], wa_in, 0, proj0, "gather_wait_near")
    far_sems, (wa_in,), far_token = _gather_weights_start(
        [wa_in], w_kinds[:1], w_shapes[:1], far_groups, a_token, "gather_a_w_in_far_start")
    proj0 = _mm_proj_slab(h0, wa_in, proj0, slab(2), 4, "a_proj_x", far_token)
    proj0 = _mm_proj_slab(h0, wa_in, proj0, slab(1), 4, "a_proj_y", far_token)
    wa_in = arrived(far_sems[0], far_groups[0], wa_in, 0, proj0, "gather_wait_far")
    def start_whole(w, after, name):
        sems, (flight,), token = _gather_weights_start(
            placed[w:w + 1], w_kinds[w:w + 1], w_shapes[w:w + 1], whole_group, after, name)
        return sems[0], flight, token

    ao_sems, wa_out, ao_token = start_whole(1, wa_in, "gather_a_w_out_start")
    proj0 = _mm_proj_slab(h0, wa_in, proj0, slab(3), 4, "a_proj_far", ao_token)
    wa_out = arrived(ao_sems, whole_group[0], wa_out, 1, proj0, "gather_wait_a_w_out")
    bi_sems, wb_in, bi_token = start_whole(2, wa_out, "gather_b_w_in_start")
    y0, br0, x1, h1 = _conv_fwd(proj0, conv_w, a_conv_b, wa_out, x2d, gate[0], g1, scale[1], shift[1],
                                "conv_fwd", bi_token)
    wb_in = arrived(bi_sems, whole_group[0], wb_in, 2, h1, "gather_wait_b_w_in")
    bo_sems, wb_out, bo_token = start_whole(3, wb_in, "gather_b_w_out_start")
    proj1 = _mm_proj(h1, wb_in, 3, "b_proj", bo_token)
    b_s_t = jnp.transpose(b_b_s[0])
    wb_out = arrived(bo_sems, whole_group[0], wb_out, 3, proj1, "gather_wait_b_w_out")
    y1, dx2, dbr1, loss_part, g_final_g, dgate1 = _gmlp_fwd_head(
        proj1, ln_g, ln_b, b_w_s[0], b_s_t, wb_out, x1, gate[1], gf, target, "gmlp_fwd_head")

    gw_b_out = _mm_dw_out(y1, dbr1, "b_out_dw")
    dproj1, g_w_s, g_b_s_t, g_ln_g, g_ln_b = _gmlp_bwd(
        proj1, dbr1, wb_out, ln_g, ln_b, b_w_s[0], jnp.swapaxes(b_w_s[0], 1, 2), b_s_t, "gmlp_bwd")
    gw_b_in = _mm_dw_in(h1, dproj1, "b_proj_dw")
    b_kinds = ["col", "row"]
    b_send, b_recv, b_grads, b_lands, b_token = _reduce_start(
        [gw_b_in, gw_b_out.reshape(N_CHIP, e // N_CHIP, d)], b_kinds, "reduce_b_start")
    dh1 = _mm_dh(dproj1, wb_in, "b_proj_dx", b_token)
    dx1, dshift1, dscale1, g_g1, dbr0, dgate0 = _norm_mod_bwd(
        dh1, x1, dx2, g1, scale[1], "norm_mod1_bwd", br=br0, gate=gate[0])

    gw_a_out = _mm_dw_out(y0, dbr0, "a_out_dw")
    dproj0, g_w0, g_w1, g_w2, g_conv_b = _conv_bwd(proj0, dbr0, wa_out, conv_w, a_conv_b, "conv_bwd")
    gw_a_in = _mm_dw_in(h0, dproj0, "a_proj_dw")
    a_kinds = ["col", "row"]
    a_send, a_recv, a_grads, a_lands, a_token = _reduce_start(
        [gw_a_in, gw_a_out.reshape(N_CHIP, e // N_CHIP, d)], a_kinds, "reduce_a_start")
    dh0 = _mm_dh(dproj0, wa_in, "a_proj_dx", a_token)
    grad_x, dshift0, dscale0, g_g0 = _norm_mod_bwd(dh0, x2d, dx1, g0, scale[0], "norm_mod0_bwd")

    dmod = jnp.concatenate([dshift0, dscale0, dgate0, dshift1, dscale1, dgate1], axis=1)
    small = [loss_part[0, 0:1], g_final_g, g_g0, g_g1, jnp.concatenate([g_w0, g_w1, g_w2], axis=0), g_conv_b, g_ln_g, g_ln_b,
             jnp.transpose(g_b_s_t), g_w_s, dmod]
    packed, offs = _pack(small, 1024)
    rows = packed.shape[0]
    gathered = _all_gather(packed, "gather_small")
    summed = _sum_devices(gathered, rows, "sum_small").reshape(-1)

    def take(k, shape):
        size = math.prod(shape)
        return summed[offs[k]:offs[k] + size].reshape(shape)

    loss = take(0, ())
    grad_final_g = take(1, (d,))
    grad_norm_g = jnp.concatenate([take(2, (1, d)), take(3, (1, d))], axis=0)
    grad_conv_w_full = take(4, (3, e))
    grad_a_conv_b = take(5, (1, e))
    grad_ln_g_full = take(6, (1, e))
    grad_ln_b_full = take(7, (1, e))
    grad_b_b_s = take(8, (1, GROUPS, CHUNK))
    grad_b_w_s = take(9, (1, GROUPS, CHUNK, CHUNK))
    grad_mod_b = take(10, (n_layer, 3 * d))
    grad_a_conv_w = lax.dynamic_slice_in_dim(grad_conv_w_full, chip * es, es, axis=1)[None]
    grad_b_ln_g = lax.dynamic_slice_in_dim(grad_ln_g_full, chip * es, es, axis=1)
    grad_b_ln_b = lax.dynamic_slice_in_dim(grad_ln_b_full, chip * es, es, axis=1)
    dmod_all = gathered.reshape(N_DEV, rows * 1024)[:, offs[10]:offs[10] + n_layer * 3 * d]
    dmod_all = dmod_all.reshape(N_DEV, n_layer, 3 * d)
    dmod_cols = jnp.transpose(lax.dynamic_slice_in_dim(dmod_all, chip * wd, wd, axis=2), (1, 0, 2))

    b_grads, b_lands = _reduce_wait(b_send, b_recv, b_grads, b_lands, b_kinds, summed, "reduce_b_wait")
    a_grads, a_lands = _reduce_wait(a_send, a_recv, a_grads, a_lands, a_kinds, summed, "reduce_a_wait")
    summed_halves = [
        _add_pieces(g, land, kind, chip_idx, core_idx, "add_pieces_" + nm)
        for g, land, kind, nm in [(a_grads[0], a_lands[0], "col", "a_w_in"), (b_grads[0], b_lands[0], "col", "b_w_in"),
                                  (a_grads[1], a_lands[1], "row", "a_w_out"), (b_grads[1], b_lands[1], "row", "b_w_out")]]
    g_a_w_in, g_b_w_in, g_a_w_out, g_b_w_out = _join_halves(summed_halves, "join_halves")

    grad_mod_w, delta_mod_w, new_m_mod_w, new_v_mod_w = _mod_w_update(
        jnp.transpose(c_act), dmod_cols, mod_w, m_mod_w, v_mod_w, "mod_w_update")
    upd, big_grads = {}, {}
    for nm, w, g, m, v in [("a_w_in", a_w_in, g_a_w_in, m_a_w_in, v_a_w_in), ("a_w_out", a_w_out, g_a_w_out, m_a_w_out, v_a_w_out),
                           ("b_w_in", b_w_in, g_b_w_in, m_b_w_in, v_b_w_in), ("b_w_out", b_w_out, g_b_w_out, m_b_w_out, v_b_w_out)]:
        g_out, *rest = _adamw_2d(w[0], g, m[0], v[0], "adamw_" + nm)
        big_grads[nm] = g_out[None]
        upd[nm] = tuple(o[None] for o in rest)
    small_w = [("mod_b", mod_b, grad_mod_b, m_mod_b, v_mod_b), ("norm_g", norm_g, grad_norm_g, m_norm_g, v_norm_g),
               ("a_conv_w", a_conv_w, grad_a_conv_w, m_a_conv_w, v_a_conv_w),
               ("a_conv_b", a_conv_b, grad_a_conv_b, m_a_conv_b, v_a_conv_b),
               ("b_ln_g", b_ln_g, grad_b_ln_g, m_b_ln_g, v_b_ln_g), ("b_ln_b", b_ln_b, grad_b_ln_b, m_b_ln_b, v_b_ln_b),
               ("b_w_s", b_w_s, grad_b_w_s, m_b_w_s, v_b_w_s), ("b_b_s", b_b_s, grad_b_b_s, m_b_b_s, v_b_b_s),
               ("final_g", final_g, grad_final_g, m_final_g, v_final_g)]

    def flat2d(a):
        return a.reshape(-1, a.shape[-1])

    res = _adamw_small([flat2d(t[1]) for t in small_w], [flat2d(t[2]) for t in small_w],
                       [flat2d(t[3]) for t in small_w], [flat2d(t[4]) for t in small_w], "adamw_small")
    for (nm, w, _, _, _), r3 in zip(small_w, res):
        upd[nm] = tuple(o.reshape(w.shape) for o in r3)
    upd["mod_w"] = (delta_mod_w, new_m_mod_w, new_v_mod_w)

    grads = {"mod_w": grad_mod_w, "mod_b": grad_mod_b, "norm_g": grad_norm_g, "a_conv_w": grad_a_conv_w,
             "a_conv_b": grad_a_conv_b, "b_ln_g": grad_b_ln_g, "b_ln_b": grad_b_ln_b, "b_w_s": grad_b_w_s,
             "b_b_s": grad_b_b_s, "final_g": grad_final_g, **big_grads}
    order = ["mod_w", "mod_b", "norm_g", "a_w_in", "a_conv_w", "a_conv_b", "a_w_out", "b_w_in", "b_ln_g", "b_ln_b",
             "b_w_s", "b_b_s", "b_w_out", "final_g"]
    return (loss, grad_x[None], *[grads[k] for k in order], *[upd[k][0] for k in order],
            *[upd[k][1] for k in order], *[upd[k][2] for k in order])
```

```python
import functools
import math

import jax
import jax.numpy as jnp
from jax import lax
from jax.experimental import pallas as pl
from jax.experimental.pallas import tpu as pltpu

F32 = jnp.float32
BF16 = jnp.bfloat16
MESH = pl.DeviceIdType.MESH

N_DEV = 8
N_CHIP = 4
SUBLANE = 8
PACK = 16
ACT = F32
RMS_EPS = 1e-6
LN_EPS = 1e-5
CHUNK = 128
GROUPS = 8
ADAM_LR = 0.001
ADAM_B1 = 0.9
ADAM_B2 = 0.999
ADAM_EPS = 1e-08
ADAM_WD = 0.01
ADAM_STEP = 10
VMEM_LIMIT = 56 << 20

HBM_SPEC = pl.BlockSpec(memory_space=pltpu.HBM)
VMEM_SPEC = pl.BlockSpec(memory_space=pltpu.VMEM)
SEM_SPEC = pl.BlockSpec(memory_space=pltpu.SEMAPHORE)
ANY_SPEC = pl.BlockSpec(memory_space=pl.ANY)
EFFECT = pltpu.SideEffectType.DATAFLOW_SIDE_EFFECTING


def _params(*sem):
    return pltpu.CompilerParams(dimension_semantics=sem, vmem_limit_bytes=VMEM_LIMIT)


def _tile(n, want):
    if n <= want:
        return n
    t = want
    while n % t:
        t -= 128
    return t


def _sigmoid(x):
    return 1.0 / (1.0 + jnp.exp(-x))


def _silu_and_grad(x):
    s = _sigmoid(x)
    return x * s, s * (1.0 + x * (1.0 - s))


def _gelu_and_grad(x):
    cdf = 0.5 * (1.0 + lax.erf(x * (1.0 / math.sqrt(2.0))))
    pdf = jnp.exp(-0.5 * x * x) * (1.0 / math.sqrt(2.0 * math.pi))
    return x * cdf, cdf + x * pdf


def _gelu(x):
    return x * (0.5 * (1.0 + lax.erf(x * (1.0 / math.sqrt(2.0)))))


def _rms(x):
    r = lax.rsqrt(jnp.mean(x * x, axis=-1, keepdims=True) + RMS_EPS)
    return x * r, r


def _rms_bwd(dxn, xn, r):
    return r * (dxn - xn * jnp.mean(dxn * xn, axis=-1, keepdims=True))


def _colsum(a):
    return jnp.sum(a, axis=0, keepdims=True)


def _shift_down(cur, before, k):
    rolled = pltpu.roll(cur, k, 0)
    row = lax.broadcasted_iota(jnp.int32, before.shape, 0)
    head = jnp.where(row < k, pltpu.roll(before, k, 0), rolled[:SUBLANE])
    return jnp.concatenate([head, rolled[SUBLANE:]], axis=0)


def _shift_up(cur, after, k):
    n = cur.shape[0]
    rolled = pltpu.roll(cur, n - k, 0)
    row = lax.broadcasted_iota(jnp.int32, after.shape, 0)
    tail = jnp.where(row >= SUBLANE - k, pltpu.roll(after, SUBLANE - k, 0), rolled[n - SUBLANE:])
    return jnp.concatenate([rolled[:n - SUBLANE], tail], axis=0)


def _after_spec():
    return pl.BlockSpec((SUBLANE, 128), lambda *_: (0, 0))


def _mm_proj(h, w, n_split, name, after):
    s, d = h.shape
    e = w.shape[1] // n_split
    tm, tn = _tile(s, 1024), _tile(e, 2048)
    nj = e // tn

    def body(h_ref, w_ref, after_ref, o_ref):
        o_ref[...] = jnp.dot(h_ref[...], w_ref[...], preferred_element_type=F32).astype(ACT)

    return pl.pallas_call(
        body, name=name,
        out_shape=jax.ShapeDtypeStruct((n_split, s, e), ACT),
        grid=(s // tm, n_split * nj),
        in_specs=[pl.BlockSpec((tm, d), lambda i, j: (i, 0)), pl.BlockSpec((d, tn), lambda i, j: (0, j)),
                  _after_spec()],
        out_specs=pl.BlockSpec((None, tm, tn), lambda i, j: (j // nj, i, j % nj)),
        compiler_params=_params("parallel", "parallel"),
    )(h, w, after)


def _mm_proj_slab(h, w, proj, q_idx, n_split, name, after):
    s, d = h.shape
    e = w.shape[1] // n_split
    tm, tn = _tile(s, 1024), _tile(e, 2048)
    nj = e // tn

    def body(q_ref, h_ref, w_ref, *rest):
        o_ref = rest[-1]
        o_ref[...] = jnp.dot(h_ref[...], w_ref[...], preferred_element_type=F32).astype(ACT)

    in_specs = [pl.BlockSpec((tm, d), lambda i, j, qr: (i, 0)), pl.BlockSpec((d, tn), lambda i, j, qr: (0, qr[0] * nj + j)),
                _after_spec()]
    args = [q_idx, h, w, after]
    aliases = {}
    if proj is not None:
        in_specs.append(ANY_SPEC)
        args.append(proj)
        aliases = {4: 0}
    return pl.pallas_call(
        body, name=name,
        out_shape=jax.ShapeDtypeStruct((n_split, s, e), ACT),
        grid_spec=pltpu.PrefetchScalarGridSpec(
            num_scalar_prefetch=1, grid=(s // tm, nj), in_specs=in_specs,
            out_specs=pl.BlockSpec((None, tm, tn), lambda i, j, qr: (qr[0], i, j))),
        input_output_aliases=aliases,
        compiler_params=_params("parallel", "parallel"),
    )(*args)


def _mm_dh_norm_bwd(dp, w, x, dx_in, g, scale, name, after, br=None, gate=None):
    nq, s, e = dp.shape
    d = w.shape[0]
    has_branch = br is not None
    tm, tk = _tile(s, 512 if has_branch else 1024), _tile(e, 2048)
    nkq = e // tk
    nk = nq * nkq

    def body(*refs):
        if has_branch:
            (a_ref, b_ref, after_ref, x_ref, dxin_ref, g_ref, sc_ref, br_ref, gate_ref,
             dx_ref, dsh_ref, dsc_ref, dg_ref, dbr_ref, dgate_ref, acc_ref) = refs
        else:
            (a_ref, b_ref, after_ref, x_ref, dxin_ref, g_ref, sc_ref,
             dx_ref, dsh_ref, dsc_ref, dg_ref, acc_ref) = refs
        i, k = pl.program_id(0), pl.program_id(1)

        @pl.when(jnp.logical_and(i == 0, k == 0))
        def _():
            dsh_ref[...] = jnp.zeros_like(dsh_ref)
            dsc_ref[...] = jnp.zeros_like(dsc_ref)
            dg_ref[...] = jnp.zeros_like(dg_ref)
            if has_branch:
                dgate_ref[...] = jnp.zeros_like(dgate_ref)

        @pl.when(k == 0)
        def _():
            acc_ref[...] = jnp.zeros_like(acc_ref)
        acc_ref[...] += lax.dot_general(a_ref[...], b_ref[...], (((1,), (1,)), ((), ())), preferred_element_type=F32)

        @pl.when(k == nk - 1)
        def _():
            dh = acc_ref[...]
            g_vec = g_ref[...]
            xn, r = _rms(x_ref[...])
            dsh_ref[...] += _colsum(dh)
            dsc_ref[...] += _colsum(dh * (xn * g_vec))
            da = dh * (1.0 + sc_ref[...])
            dg_ref[...] += _colsum(da * xn)
            dx = dxin_ref[...] + _rms_bwd(da * g_vec, xn, r)
            dx_ref[...] = dx
            if has_branch:
                dgate_ref[...] += _colsum(dx * br_ref[...])
                dbr_ref[...] = (gate_ref[...] * dx).astype(BF16)

    rows = pl.BlockSpec((tm, d), lambda i, k: (i, 0))
    vec = pl.BlockSpec((1, d), lambda i, k: (0, 0))
    vec_out = jax.ShapeDtypeStruct((1, d), F32)
    in_specs = [pl.BlockSpec((None, tm, tk), lambda i, k: (k // nkq, i, k % nkq)),
                pl.BlockSpec((d, tk), lambda i, k: (0, k)), _after_spec(), rows, rows, vec, vec]
    out_shape = [jax.ShapeDtypeStruct((s, d), F32), vec_out, vec_out, vec_out]
    out_specs = [rows, vec, vec, vec]
    args = [dp, w, after, x, dx_in, g, scale]
    if has_branch:
        in_specs += [rows, vec]
        out_shape += [jax.ShapeDtypeStruct((s, d), BF16), vec_out]
        out_specs += [rows, vec]
        args += [br, gate]
    return pl.pallas_call(
        body, name=name, out_shape=tuple(out_shape), grid=(s // tm, nk),
        in_specs=in_specs, out_specs=tuple(out_specs), scratch_shapes=[pltpu.VMEM((tm, d), F32)],
        compiler_params=_params("arbitrary", "arbitrary"),
    )(*args)


def _dw_body(n_t):
    def body(a_ref, b_ref, o_ref, acc_ref):
        t = pl.program_id(2)

        @pl.when(t == 0)
        def _():
            acc_ref[...] = jnp.zeros_like(acc_ref)
        acc_ref[...] += lax.dot_general(a_ref[...], b_ref[...], (((0,), (0,)), ((), ())), preferred_element_type=F32)

        @pl.when(t == n_t - 1)
        def _():
            o_ref[...] = acc_ref[...].astype(o_ref.dtype)
    return body


def _mm_dw_in(h, dp, name):
    s, d = h.shape
    nq, _, e = dp.shape
    tm, tn, tt = _tile(d, 1024), _tile(e, 1024), _tile(s, 2048)
    nj = e // tn

    return pl.pallas_call(
        _dw_body(s // tt), name=name,
        out_shape=jax.ShapeDtypeStruct((d, nq * e), BF16),
        grid=(d // tm, nq * nj, s // tt),
        in_specs=[pl.BlockSpec((tt, tm), lambda i, j, t: (t, i)),
                  pl.BlockSpec((None, tt, tn), lambda i, j, t: (j // nj, t, j % nj))],
        out_specs=pl.BlockSpec((tm, tn), lambda i, j, t: (i, j)),
        scratch_shapes=[pltpu.VMEM((tm, tn), F32)],
        compiler_params=_params("parallel", "parallel", "arbitrary"),
    )(h, dp)


def _mm_dw_out(y, dbr, name):
    s, e = y.shape
    d = dbr.shape[1]
    tm, tn, tt = _tile(e, 1024), _tile(d, 1024), _tile(s, 2048)

    return pl.pallas_call(
        _dw_body(s // tt), name=name,
        out_shape=jax.ShapeDtypeStruct((e, d), BF16),
        grid=(e // tm, d // tn, s // tt),
        in_specs=[pl.BlockSpec((tt, tm), lambda i, j, t: (t, i)), pl.BlockSpec((tt, tn), lambda i, j, t: (t, j))],
        out_specs=pl.BlockSpec((tm, tn), lambda i, j, t: (i, j)),
        scratch_shapes=[pltpu.VMEM((tm, tn), F32)],
        compiler_params=_params("parallel", "parallel", "arbitrary"),
    )(y, dbr)


def _row_spec(ts, d):
    return pl.BlockSpec((ts, d), lambda i: (i, 0))


def _vec_spec(d):
    return pl.BlockSpec((1, d), lambda i: (0, 0))


def _norm_mod(x, g, scale, shift, name):
    s, d = x.shape
    ts = _tile(s, 512)

    def body(x_ref, g_ref, sc_ref, sh_ref, h_ref):
        xn, _ = _rms(x_ref[...])
        h_ref[...] = ((xn * g_ref[...]) * (1.0 + sc_ref[...]) + sh_ref[...]).astype(BF16)

    return pl.pallas_call(
        body, name=name, out_shape=jax.ShapeDtypeStruct((s, d), BF16), grid=(s // ts,),
        in_specs=[_row_spec(ts, d), _vec_spec(d), _vec_spec(d), _vec_spec(d)],
        out_specs=_row_spec(ts, d), compiler_params=_params("parallel"),
    )(x, g, scale, shift)


def _head_tile(x1, br, gate, gf, target):
    d = x1.shape[-1]
    xn, r = _rms(x1 + gate * br)
    err = xn * gf - target
    loss = 0.5 * jnp.sum(jnp.mean(err * err, axis=-1, keepdims=True))
    dout = err * (1.0 / d)
    dx = _rms_bwd(dout * gf, xn, r)
    return dx, gate * dx, loss, _colsum(dout * xn), _colsum(dx * br)


def _conv_tiles(s, e):
    return _tile(s, 512), _tile(e, 512)


def _f32(ref):
    return ref[...].astype(F32)


def _rows_before(halo_ref):
    return _f32(halo_ref)[PACK - SUBLANE:]


def _rows_after(halo_ref):
    return _f32(halo_ref)[:SUBLANE]


def _conv_fwd(proj, conv_w, conv_b, w_out, x, gate, g_next, scale_next, shift_next, name, after):
    _, s, e = proj.shape
    d = w_out.shape[1]
    ts, te = _tile(s, 256), e
    hb = ts // PACK

    def body(bg_ref, cg_ref, xi_ref, z_ref, cgp_ref, xip_ref, w0_ref, w1_ref, w2_ref, b_ref, wo_ref, after_ref,
             x_ref, gate_ref, g_ref, sc_ref, sh_ref, y_ref, br_ref, x1_ref, h_ref):
        cx = _f32(cg_ref) * _f32(xi_ref)
        before = jnp.where(pl.program_id(0) > 0, _rows_before(cgp_ref) * _rows_before(xip_ref), 0.0)
        conv = b_ref[...] + w2_ref[...] * cx
        conv = conv + w0_ref[...] * _shift_down(cx, before, 2)
        conv = conv + w1_ref[...] * _shift_down(cx, before, 1)
        z = _f32(z_ref)
        y = ((z * _sigmoid(z)) * _f32(bg_ref) * conv).astype(BF16)
        y_ref[...] = y
        br = jnp.dot(y, wo_ref[...], preferred_element_type=F32)
        br_ref[...] = br
        x1 = x_ref[...] + gate_ref[...] * br
        x1_ref[...] = x1
        xn, _ = _rms(x1)
        h_ref[...] = ((xn * g_ref[...]) * (1.0 + sc_ref[...]) + sh_ref[...]).astype(BF16)

    def part(q):
        return pl.BlockSpec((None, ts, te), lambda i: (q, i, 0))

    def halo_before(q):
        return pl.BlockSpec((None, PACK, te), lambda i: (q, jnp.maximum(i * hb - 1, 0), 0))

    return pl.pallas_call(
        body, name=name,
        out_shape=(jax.ShapeDtypeStruct((s, e), BF16), jax.ShapeDtypeStruct((s, d), F32),
                   jax.ShapeDtypeStruct((s, d), F32), jax.ShapeDtypeStruct((s, d), BF16)), grid=(s // ts,),
        in_specs=[part(0), part(1), part(2), part(3), halo_before(1), halo_before(2)]
        + [_vec_spec(e)] * 4 + [pl.BlockSpec((e, d), lambda i: (0, 0)), _after_spec(), _row_spec(ts, d)]
        + [_vec_spec(d)] * 4,
        out_specs=(_row_spec(ts, e), _row_spec(ts, d), _row_spec(ts, d), _row_spec(ts, d)),
        compiler_params=_params("parallel"),
    )(proj, proj, proj, proj, proj, proj, *conv_w, conv_b, w_out, after, x, gate, g_next, scale_next, shift_next)


def _conv_bwd(proj, dbr, w_out, conv_w, conv_b, name):
    _, s, e = proj.shape
    d = dbr.shape[1]
    ts, te = _conv_tiles(s, e)
    hb = ts // PACK
    n_i = s // ts
    last_halo = s // PACK - 1
    nt = (((1,), (1,)), ((), ()))

    def body(bg_ref, cg_ref, xi_ref, z_ref, dbr_ref, cgp_ref, xip_ref, bgn_ref, zn_ref, dbrn_ref, wo_ref,
             w0_ref, w1_ref, w2_ref, b_ref, dp_ref, dw0_ref, dw1_ref, dw2_ref, db_ref):
        i = pl.program_id(1)

        @pl.when(i == 0)
        def _():
            for acc in (dw0_ref, dw1_ref, dw2_ref, db_ref):
                acc[...] = jnp.zeros_like(acc)
        wo = wo_ref[...]
        dy = lax.dot_general(dbr_ref[...], wo, nt, preferred_element_type=F32)
        dyn = lax.dot_general(dbrn_ref[...], wo, nt, preferred_element_type=F32)[:SUBLANE]
        bg, cg, xi, z = _f32(bg_ref), _f32(cg_ref), _f32(xi_ref), _f32(z_ref)
        w0, w1, w2 = w0_ref[...], w1_ref[...], w2_ref[...]
        cx = cg * xi
        before = jnp.where(i > 0, _rows_before(cgp_ref) * _rows_before(xip_ref), 0.0)
        cx1 = _shift_down(cx, before, 1)
        cx2 = _shift_down(cx, before, 2)
        conv = b_ref[...] + w2 * cx
        conv = conv + w0 * cx2
        conv = conv + w1 * cx1
        sz, dsz = _silu_and_grad(z)
        dp_ref[3] = (dy * bg * conv * dsz).astype(BF16)
        dp_ref[0] = (dy * sz * conv).astype(BF16)
        dconv = dy * sz * bg
        zn = _rows_after(zn_ref)
        after = jnp.where(i < n_i - 1, dyn * (zn * _sigmoid(zn)) * _rows_after(bgn_ref), 0.0)
        db_ref[...] += _colsum(dconv)
        dw2_ref[...] += _colsum(dconv * cx)
        dw1_ref[...] += _colsum(dconv * cx1)
        dw0_ref[...] += _colsum(dconv * cx2)
        dcx = w2 * dconv + w1 * _shift_up(dconv, after, 1) + w0 * _shift_up(dconv, after, 2)
        dp_ref[1] = (dcx * xi).astype(BF16)
        dp_ref[2] = (dcx * cg).astype(BF16)

    def part(q):
        return pl.BlockSpec((None, ts, te), lambda j, i: (q, i, j))

    def halo_before(q):
        return pl.BlockSpec((None, PACK, te), lambda j, i: (q, jnp.maximum(i * hb - 1, 0), j))

    def halo_after(q):
        return pl.BlockSpec((None, PACK, te), lambda j, i: (q, jnp.minimum((i + 1) * hb, last_halo), j))

    return pl.pallas_call(
        body, name=name,
        out_shape=(jax.ShapeDtypeStruct((4, s, e), BF16),) + (jax.ShapeDtypeStruct((1, e), F32),) * 4,
        grid=(e // te, n_i),
        in_specs=[part(0), part(1), part(2), part(3), pl.BlockSpec((ts, d), lambda j, i: (i, 0)),
                  halo_before(1), halo_before(2), halo_after(0), halo_after(3),
                  pl.BlockSpec((PACK, d), lambda j, i: (jnp.minimum((i + 1) * hb, last_halo), 0)),
                  pl.BlockSpec((te, d), lambda j, i: (j, 0))]
        + [pl.BlockSpec((1, te), lambda j, i: (0, j))] * 4,
        out_specs=(pl.BlockSpec((4, ts, te), lambda j, i: (0, i, j)),) + (pl.BlockSpec((1, te), lambda j, i: (0, j)),) * 4,
        compiler_params=_params("parallel", "arbitrary"),
    )(proj, proj, proj, proj, dbr, proj, proj, proj, proj, dbr, w_out, *conv_w, conv_b)


def _tril(w):
    row = lax.broadcasted_iota(jnp.int32, w.shape, 0)
    col = lax.broadcasted_iota(jnp.int32, w.shape, 1)
    return jnp.where(row >= col, w, 0.0)


def _triu(w):
    row = lax.broadcasted_iota(jnp.int32, w.shape, 0)
    col = lax.broadcasted_iota(jnp.int32, w.shape, 1)
    return jnp.where(row <= col, w, 0.0)


def _layer_norm_fwd(v, g, b):
    mu = jnp.mean(v, axis=-1, keepdims=True)
    vc = v - mu
    rstd = lax.rsqrt(jnp.mean(vc * vc, axis=-1, keepdims=True) + LN_EPS)
    vhat = vc * rstd
    return vhat * g + b, vhat, rstd


GMLP_CHUNKS_PER_STEP = 2


def _gmlp_rows(s):
    return CHUNK * min(GMLP_CHUNKS_PER_STEP, s // CHUNK)


def _mix_positions(w_ref, src_scr, dst_scr, gw, mask, bias_ref=None):
    for gi in range(GROUPS):
        cols = slice(gi * gw, (gi + 1) * gw)
        wm = mask(w_ref[gi]).astype(BF16)
        for n in range(src_scr.shape[0] // CHUNK):
            rows = slice(n * CHUNK, (n + 1) * CHUNK)
            out = jnp.dot(wm, src_scr[rows, cols], preferred_element_type=F32)
            if bias_ref is not None:
                out = out + bias_ref[:, gi:gi + 1]
            dst_scr[rows, cols] = out


def _gmlp_fwd_head(proj, ln_g, ln_b, w_s, b_s_t, w_out, x1, gate, gf, target, name):
    _, s, e = proj.shape
    d = w_out.shape[1]
    gw = e // GROUPS
    ts = _gmlp_rows(s)

    def body(pu_ref, pv_ref, pz_ref, g_ref, b_ref, ws_ref, bs_ref, wo_ref, x1_ref, gate_ref, gf_ref, tg_ref,
             y_ref, dx_ref, dbr_ref, loss_ref, dgf_ref, dgate_ref, vn_scr, mix_scr):
        @pl.when(pl.program_id(0) == 0)
        def _():
            loss_ref[...] = jnp.zeros_like(loss_ref)
            dgf_ref[...] = jnp.zeros_like(dgf_ref)
            dgate_ref[...] = jnp.zeros_like(dgate_ref)
        vn, _, _ = _layer_norm_fwd(_gelu(_f32(pv_ref)), g_ref[...], b_ref[...])
        vn_scr[...] = vn.astype(BF16)
        _mix_positions(ws_ref, vn_scr, mix_scr, gw, _tril, bs_ref)
        z = _f32(pz_ref)
        y = ((z * _sigmoid(z)) * (_gelu(_f32(pu_ref)) * mix_scr[...])).astype(BF16)
        y_ref[...] = y
        br = jnp.dot(y, wo_ref[...], preferred_element_type=F32)
        dx, dbr, loss, dgf, dgate = _head_tile(x1_ref[...], br, gate_ref[...], gf_ref[...], tg_ref[...])
        dx_ref[...] = dx
        dbr_ref[...] = dbr.astype(BF16)
        loss_ref[...] += loss
        dgf_ref[...] += dgf
        dgate_ref[...] += dgate

    def part(q):
        return pl.BlockSpec((None, ts, e), lambda i: (q, i, 0))

    return pl.pallas_call(
        body, name=name,
        out_shape=(jax.ShapeDtypeStruct((s, e), BF16), jax.ShapeDtypeStruct((s, d), F32),
                   jax.ShapeDtypeStruct((s, d), BF16), jax.ShapeDtypeStruct((SUBLANE, 128), F32),
                   jax.ShapeDtypeStruct((1, d), F32), jax.ShapeDtypeStruct((1, d), F32)),
        grid=(s // ts,),
        in_specs=[part(0), part(1), part(2), _vec_spec(e), _vec_spec(e),
                  pl.BlockSpec((GROUPS, CHUNK, CHUNK), lambda i: (0, 0, 0)),
                  pl.BlockSpec((CHUNK, GROUPS), lambda i: (0, 0)), pl.BlockSpec((e, d), lambda i: (0, 0)),
                  _row_spec(ts, d), _vec_spec(d), _vec_spec(d), _row_spec(ts, d)],
        out_specs=(_row_spec(ts, e), _row_spec(ts, d), _row_spec(ts, d),
                   pl.BlockSpec((SUBLANE, 128), lambda i: (0, 0)), _vec_spec(d), _vec_spec(d)),
        scratch_shapes=[pltpu.VMEM((ts, e), BF16), pltpu.VMEM((ts, e), F32)],
        compiler_params=_params("arbitrary"),
    )(proj, proj, proj, ln_g, ln_b, w_s, b_s_t, w_out, x1, gate, gf, target)


def _gmlp_bwd(proj, dbr, w_out, ln_g, ln_b, w_s, w_s_t, b_s_t, name):
    _, s, e = proj.shape
    d = dbr.shape[1]
    gw = e // GROUPS
    ts = _gmlp_rows(s)
    n_i = s // ts

    def body(pu_ref, pv_ref, pz_ref, dbr_ref, wo_ref, g_ref, b_ref, ws_ref, wst_ref, bs_ref,
             dp_ref, dws_ref, dbs_ref, dlg_ref, dlb_ref, vn_scr, mix_scr, dm_scr, dvn_scr, dmacc_scr):
        i = pl.program_id(0)

        @pl.when(i == 0)
        def _():
            dws_ref[...] = jnp.zeros_like(dws_ref)
            dlg_ref[...] = jnp.zeros_like(dlg_ref)
            dlb_ref[...] = jnp.zeros_like(dlb_ref)
            dmacc_scr[...] = jnp.zeros_like(dmacc_scr)
        ln_g = g_ref[...]
        u, du_dpu = _gelu_and_grad(_f32(pu_ref))
        v, dv_dpv = _gelu_and_grad(_f32(pv_ref))
        vn, vhat, rstd = _layer_norm_fwd(v, ln_g, b_ref[...])
        vn_scr[...] = vn.astype(BF16)
        _mix_positions(ws_ref, vn_scr, mix_scr, gw, _tril, bs_ref)
        mixed = mix_scr[...]
        dy = lax.dot_general(dbr_ref[...], wo_ref[...], (((1,), (1,)), ((), ())), preferred_element_type=F32)
        sz, dsz = _silu_and_grad(_f32(pz_ref))
        ds = dy * sz
        dp_ref[2] = (dy * (u * mixed) * dsz).astype(BF16)
        dp_ref[0] = (ds * mixed * du_dpu).astype(BF16)
        dm = ds * u
        dm_scr[...] = dm.astype(BF16)
        for n in range(ts // CHUNK):
            rows = slice(n * CHUNK, (n + 1) * CHUNK)
            dmacc_scr[...] += dm[rows]
            for gi in range(GROUPS):
                cols = slice(gi * gw, (gi + 1) * gw)
                dws_ref[gi] += lax.dot_general(dm_scr[rows, cols], vn_scr[rows, cols], (((1,), (1,)), ((), ())),
                                               preferred_element_type=F32)
        _mix_positions(wst_ref, dm_scr, dvn_scr, gw, _triu)
        dvn = dvn_scr[...]
        dlg_ref[...] += _colsum(dvn * vhat)
        dlb_ref[...] += _colsum(dvn)
        dvh = dvn * ln_g
        dv = rstd * (dvh - jnp.mean(dvh, axis=-1, keepdims=True) - vhat * jnp.mean(dvh * vhat, axis=-1, keepdims=True))
        dp_ref[1] = (dv * dv_dpv).astype(BF16)

        @pl.when(i == n_i - 1)
        def _():
            for gi in range(GROUPS):
                dws_ref[gi] = _tril(dws_ref[gi])
                dbs_ref[:, gi:gi + 1] = jnp.sum(dmacc_scr[:, gi * gw:(gi + 1) * gw], axis=1, keepdims=True)

    def part(q):
        return pl.BlockSpec((None, ts, e), lambda i: (q, i, 0))

    w_spec = pl.BlockSpec((GROUPS, CHUNK, CHUNK), lambda i: (0, 0, 0))
    bs_spec = pl.BlockSpec((CHUNK, GROUPS), lambda i: (0, 0))
    return pl.pallas_call(
        body, name=name,
        out_shape=(jax.ShapeDtypeStruct((3, s, e), BF16), jax.ShapeDtypeStruct((GROUPS, CHUNK, CHUNK), F32),
                   jax.ShapeDtypeStruct((CHUNK, GROUPS), F32), jax.ShapeDtypeStruct((1, e), F32),
                   jax.ShapeDtypeStruct((1, e), F32)),
        grid=(n_i,),
        in_specs=[part(0), part(1), part(2), pl.BlockSpec((ts, d), lambda i: (i, 0)),
                  pl.BlockSpec((e, d), lambda i: (0, 0)), _vec_spec(e), _vec_spec(e), w_spec, w_spec, bs_spec],
        out_specs=(pl.BlockSpec((3, ts, e), lambda i: (0, i, 0)), w_spec, bs_spec, _vec_spec(e), _vec_spec(e)),
        scratch_shapes=[pltpu.VMEM((ts, e), BF16), pltpu.VMEM((ts, e), F32), pltpu.VMEM((ts, e), BF16),
                        pltpu.VMEM((ts, e), F32), pltpu.VMEM((CHUNK, e), F32)],
        compiler_params=_params("arbitrary"),
    )(proj, proj, proj, dbr, w_out, ln_g, ln_b, w_s, w_s_t, b_s_t)


def _mod_fwd(c_all, mod_w, mod_b_cols, name):
    n_layer, d, w = mod_w.shape

    def body(c_ref, w_ref, b_ref, ca_ref, o_ref):
        c = c_ref[...]
        ca = c * _sigmoid(c)
        ca_ref[...] = ca
        for li in range(n_layer):
            o_ref[li * N_DEV:(li + 1) * N_DEV, :] = (
                jnp.dot(ca, w_ref[li], preferred_element_type=F32, precision=lax.Precision.HIGHEST) + b_ref[li])

    return pl.pallas_call(
        body, name=name,
        out_shape=(jax.ShapeDtypeStruct((N_DEV, d), F32), jax.ShapeDtypeStruct((n_layer * N_DEV, w), F32)),
        in_specs=[VMEM_SPEC] * 3, out_specs=(VMEM_SPEC, VMEM_SPEC),
        compiler_params=pltpu.CompilerParams(vmem_limit_bytes=VMEM_LIMIT),
    )(c_all, mod_w, mod_b_cols)


def _adamw(w, g, m, v):
    m = ADAM_B1 * m + (1.0 - ADAM_B1) * g
    v = ADAM_B2 * v + (1.0 - ADAM_B2) * (g * g)
    m_hat = m / (1.0 - ADAM_B1 ** ADAM_STEP)
    v_hat = v / (1.0 - ADAM_B2 ** ADAM_STEP)
    delta = -ADAM_LR * (m_hat / (jnp.sqrt(v_hat) + ADAM_EPS) + ADAM_WD * w)
    return delta, m, v


def _adamw_2d(w, g, m, v, name):
    r, c = w.shape
    tr, tc = _tile(r, 512), _tile(c, 1024)

    def body(w_ref, g_ref, m_ref, v_ref, go_ref, d_ref, nm_ref, nv_ref):
        g = g_ref[...]
        go_ref[...] = g
        d_ref[...], nm_ref[...], nv_ref[...] = _adamw(w_ref[...], g, m_ref[...], v_ref[...])

    spec = pl.BlockSpec((tr, tc), lambda i, j: (i, j))
    shape = jax.ShapeDtypeStruct((r, c), F32)
    return pl.pallas_call(
        body, name=name, out_shape=(shape,) * 4, grid=(r // tr, c // tc),
        in_specs=[spec] * 4, out_specs=(spec,) * 4, compiler_params=_params("parallel", "parallel"),
    )(w, g, m, v)


def _mod_w_update(ca_t, dmod_cols, w, m, v, name):
    n_layer, d, wd = w.shape
    tr = _tile(d, 256)

    def body(ca_ref, dm_ref, w_ref, m_ref, v_ref, g_ref, d_ref, nm_ref, nv_ref):
        ca = ca_ref[...]
        dm = dm_ref[...]
        g = ca[:, 0:1] * dm[0:1, :]
        for b in range(1, N_DEV):
            g = g + ca[:, b:b + 1] * dm[b:b + 1, :]
        g_ref[...] = g
        d_ref[...], nm_ref[...], nv_ref[...] = _adamw(w_ref[...], g, m_ref[...], v_ref[...])

    spec = pl.BlockSpec((None, tr, wd), lambda l, i: (l, i, 0))
    shape = jax.ShapeDtypeStruct((n_layer, d, wd), F32)
    return pl.pallas_call(
        body, name=name, out_shape=(shape,) * 4, grid=(n_layer, d // tr),
        in_specs=[pl.BlockSpec((tr, N_DEV), lambda l, i: (i, 0)), pl.BlockSpec((None, N_DEV, wd), lambda l, i: (l, 0, 0)),
                  spec, spec, spec],
        out_specs=(spec,) * 4, compiler_params=_params("parallel", "parallel"),
    )(ca_t, dmod_cols, w, m, v)


def _adamw_small(ws, gs, ms, vs, name):
    n = len(ws)

    def body(*refs):
        ins, outs = refs[:4 * n], refs[4 * n:]
        for k in range(n):
            delta, nm, nv = _adamw(ins[k][...], ins[n + k][...], ins[2 * n + k][...], ins[3 * n + k][...])
            outs[3 * k][...] = delta
            outs[3 * k + 1][...] = nm
            outs[3 * k + 2][...] = nv

    out_shape = []
    for w in ws:
        out_shape += [jax.ShapeDtypeStruct(w.shape, F32)] * 3
    outs = pl.pallas_call(
        body, name=name, out_shape=tuple(out_shape),
        in_specs=[VMEM_SPEC] * (4 * n), out_specs=tuple([VMEM_SPEC] * (3 * n)),
        compiler_params=pltpu.CompilerParams(vmem_limit_bytes=VMEM_LIMIT),
    )(*ws, *gs, *ms, *vs)
    return [tuple(outs[3 * k:3 * k + 3]) for k in range(n)]


def _place():
    return lax.axis_index("x"), lax.axis_index("y"), lax.axis_index("c")


def _other_chips(x, y):
    return [(1 - x, y), (x, 1 - y), (1 - x, 1 - y)]


def _all_gather(block, name):
    m_per, n = block.shape

    def body(x_ref, out_ref, send_sems, recv_sems, local_sem):
        x, y, c = _place()
        me, sibling = (x, y, c), (x, y, 1 - c)
        chips = _other_chips(x, y)

        def rows(px, py, pc):
            return out_ref.at[pl.ds((4 * px + 2 * py + pc) * m_per, m_per), :]

        def copy(k, blk, to, src=None):
            return pltpu.make_async_remote_copy(
                src_ref=rows(*blk) if src is None else src, dst_ref=rows(*blk),
                send_sem=send_sems.at[k], recv_sem=recv_sems.at[k], device_id=to, device_id_type=MESH)

        mine = pltpu.make_async_copy(x_ref, rows(*me), local_sem)
        mine.start()
        first = [copy(0, me, sibling, src=x_ref)]
        first += [copy(1 + j, me, (*chip, c), src=x_ref) for j, chip in enumerate(chips)]
        for cp in first:
            cp.start()
        passed = [copy(4 + j, (*chip, c), sibling) for j, chip in enumerate(chips)]
        for j, chip in enumerate(chips):
            copy(1 + j, (*chip, c), me).wait_recv()
            passed[j].start()
        copy(0, sibling, me).wait_recv()
        for j, chip in enumerate(chips):
            copy(4 + j, (*chip, 1 - c), me).wait_recv()
        for cp in first + passed:
            cp.wait_send()
        mine.wait()

    return pl.pallas_call(
        body, name=name, out_shape=jax.ShapeDtypeStruct((N_DEV * m_per, n), F32),
        in_specs=[VMEM_SPEC], out_specs=VMEM_SPEC,
        scratch_shapes=[pltpu.SemaphoreType.DMA((7,)), pltpu.SemaphoreType.DMA((7,)), pltpu.SemaphoreType.DMA],
        compiler_params=pltpu.CompilerParams(vmem_limit_bytes=VMEM_LIMIT),
    )(block)


def _hbm(a):
    return pltpu.with_memory_space_constraint(a, pltpu.HBM)


def _place_own(shard, kind, chip_idx, name):
    r, cdim = shard.shape
    tr, tc = _tile(r, 512), _tile(cdim, 1024)
    nrb, ncb = r // tr, cdim // tc

    def body(k_ref, s_ref, o_ref):
        o_ref[...] = s_ref[...].astype(BF16)

    if kind == "col":
        full, o_map = (r, N_CHIP * cdim), lambda i, j, kr: (i, kr[0] * ncb + j)
    else:
        full, o_map = (N_CHIP * r, cdim), lambda i, j, kr: (kr[0] * nrb + i, j)
    return pl.pallas_call(
        body, name=name, out_shape=jax.ShapeDtypeStruct(full, BF16),
        grid_spec=pltpu.PrefetchScalarGridSpec(
            num_scalar_prefetch=1, grid=(nrb, ncb),
            in_specs=[pl.BlockSpec((tr, tc), lambda i, j, kr: (i, j))],
            out_specs=pl.BlockSpec((tr, tc), o_map)),
        compiler_params=_params("parallel", "parallel"),
    )(chip_idx, shard)


def _weight_window(ref, kind, shard_shape, k, half):
    r, cdim = shard_shape
    hr = r // 2
    if kind == "col":
        return ref.at[pl.ds(half * hr, hr), pl.ds(pl.multiple_of(k * cdim, 128), cdim)]
    return ref.at[pl.ds(pl.multiple_of(k * r + half * hr, 2 * SUBLANE), hr), :]


def _gather_weights_start(fulls, kinds, shard_shapes, groups, after, name):
    n, ng = len(fulls), len(groups)

    def body(*refs):
        ins = refs[:n]
        sems = refs[n + 1:n + 1 + 2 * ng]
        token = refs[2 * n + 1 + 2 * ng]
        x, y, c = _place()
        chips = _other_chips(x, y)
        for g, group in enumerate(groups):
            for pos, (w, j) in enumerate(group):
                own = _weight_window(ins[w], kinds[w], shard_shapes[w], 2 * x + y, c)
                for cc in range(2):
                    pltpu.make_async_remote_copy(
                        src_ref=own, dst_ref=own,
                        send_sem=sems[2 * g].at[2 * pos + cc], recv_sem=sems[2 * g + 1].at[2 * pos + c],
                        device_id=(*chips[j], cc), device_id_type=MESH).start()
        token[...] = jnp.zeros_like(token)

    sem_shapes = []
    for group in groups:
        sem_shapes += [pltpu.SemaphoreType.DMA((2 * len(group),))] * 2
    outs = pl.pallas_call(
        body, name=name,
        out_shape=tuple(sem_shapes) + tuple(pltpu.HBM(f.shape, f.dtype) for f in fulls)
        + (jax.ShapeDtypeStruct((SUBLANE, 128), F32),),
        in_specs=[HBM_SPEC] * n + [ANY_SPEC], out_specs=(SEM_SPEC,) * (2 * ng) + (HBM_SPEC,) * n + (VMEM_SPEC,),
        input_output_aliases={w: 2 * ng + w for w in range(n)},
        compiler_params=pltpu.CompilerParams(has_side_effects=EFFECT),
    )(*[_hbm(f) for f in fulls], after)
    sems = [(outs[2 * g], outs[2 * g + 1]) for g in range(ng)]
    return sems, list(outs[2 * ng:2 * ng + n]), outs[2 * ng + n]


def _gather_weights_wait(sems, group, fulls, kinds, shard_shapes, after, name):
    n = len(fulls)

    def body(*refs):
        ins = refs[:n]
        send_sems, recv_sems = refs[n], refs[n + 1]
        x, y, c = _place()
        chips = _other_chips(x, y)
        for pos, (w, j) in enumerate(group):
            own = _weight_window(ins[w], kinds[w], shard_shapes[w], 2 * x + y, c)
            for cc in range(2):
                landed = _weight_window(ins[w], kinds[w], shard_shapes[w], 2 * chips[j][0] + chips[j][1], cc)
                k = 2 * pos + cc
                pltpu.make_async_remote_copy(
                    src_ref=own, dst_ref=own, send_sem=send_sems.at[k], recv_sem=recv_sems.at[k],
                    device_id=(*chips[j], cc), device_id_type=MESH).wait_send()
                pltpu.make_async_remote_copy(
                    src_ref=landed, dst_ref=landed, send_sem=send_sems.at[k], recv_sem=recv_sems.at[k],
                    device_id=(*chips[j], cc), device_id_type=MESH).wait_recv()

    return list(pl.pallas_call(
        body, name=name, out_shape=tuple(pltpu.HBM(f.shape, f.dtype) for f in fulls),
        in_specs=[HBM_SPEC] * n + [SEM_SPEC, SEM_SPEC, ANY_SPEC], out_specs=(HBM_SPEC,) * n,
        input_output_aliases={w: w for w in range(n)},
        compiler_params=pltpu.CompilerParams(has_side_effects=EFFECT),
    )(*fulls, sems[0], sems[1], after))


def _grad_piece(ref, kind, h, cdim, k, half):
    if kind == "col":
        return ref.at[pl.ds(half * h, h), pl.ds(pl.multiple_of(k * cdim, 128), cdim)]
    return ref.at[k, pl.ds(half * h, h), :]


def _grad_dims(g, kind):
    return (g.shape[0] // 2, g.shape[1] // N_CHIP) if kind == "col" else (g.shape[1] // 2, g.shape[2])


def _reduce_start(grads, kinds, name):
    n = len(grads)
    dims = [_grad_dims(g, kind) for g, kind in zip(grads, kinds)]
    lands = [lax.empty((N_DEV - 1, h, cdim), g.dtype) for g, (h, cdim) in zip(grads, dims)]

    def body(*refs):
        g_ins, land_ins = refs[:n], refs[n:2 * n]
        send_sems, recv_sems = refs[2 * n], refs[2 * n + 1]
        token = refs[4 * n + 2]
        x, y, c = _place()
        for w in range(n):
            h, cdim = dims[w]
            base = (N_DEV - 1) * w
            pltpu.make_async_remote_copy(
                src_ref=_grad_piece(g_ins[w], kinds[w], h, cdim, 2 * x + y, 1 - c), dst_ref=land_ins[w].at[0],
                send_sem=send_sems.at[base], recv_sem=recv_sems.at[base],
                device_id=(x, y, 1 - c), device_id_type=MESH).start()
            for j, chip in enumerate(_other_chips(x, y)):
                for cc in range(2):
                    pltpu.make_async_remote_copy(
                        src_ref=_grad_piece(g_ins[w], kinds[w], h, cdim, 2 * chip[0] + chip[1], cc),
                        dst_ref=land_ins[w].at[1 + 2 * j + c],
                        send_sem=send_sems.at[base + 1 + 2 * j + cc], recv_sem=recv_sems.at[base + 1 + 2 * j + c],
                        device_id=(*chip, cc), device_id_type=MESH).start()
        token[...] = jnp.zeros_like(token)

    sems = pltpu.SemaphoreType.DMA(((N_DEV - 1) * n,))
    outs = pl.pallas_call(
        body, name=name,
        out_shape=(sems, sems) + tuple(pltpu.HBM(a.shape, a.dtype) for a in list(grads) + lands)
        + (jax.ShapeDtypeStruct((SUBLANE, 128), F32),),
        in_specs=[HBM_SPEC] * (2 * n), out_specs=(SEM_SPEC, SEM_SPEC) + (HBM_SPEC,) * (2 * n) + (VMEM_SPEC,),
        input_output_aliases={i: 2 + i for i in range(2 * n)},
        compiler_params=pltpu.CompilerParams(has_side_effects=EFFECT),
    )(*[_hbm(a) for a in list(grads) + lands])
    return outs[0], outs[1], list(outs[2:2 + n]), list(outs[2 + n:2 + 2 * n]), outs[2 + 2 * n]


def _reduce_wait(send_sems, recv_sems, grads, lands, kinds, after, name):
    n = len(grads)
    dims = [_grad_dims(g, kind) for g, kind in zip(grads, kinds)]

    def body(*refs):
        g_ins, land_ins = refs[:n], refs[n:2 * n]
        send_sems, recv_sems = refs[2 * n], refs[2 * n + 1]
        x, y, c = _place()
        for w in range(n):
            h, cdim = dims[w]
            piece = _grad_piece(g_ins[w], kinds[w], h, cdim, 2 * x + y, c)
            for s in range(N_DEV - 1):
                k = (N_DEV - 1) * w + s
                slot = land_ins[w].at[s]
                pltpu.make_async_remote_copy(
                    src_ref=piece, dst_ref=slot, send_sem=send_sems.at[k], recv_sem=recv_sems.at[k],
                    device_id=(x, y, 1 - c), device_id_type=MESH).wait_send()
                pltpu.make_async_remote_copy(
                    src_ref=piece, dst_ref=slot, send_sem=send_sems.at[k], recv_sem=recv_sems.at[k],
                    device_id=(x, y, 1 - c), device_id_type=MESH).wait_recv()

    outs = pl.pallas_call(
        body, name=name, out_shape=tuple(pltpu.HBM(a.shape, a.dtype) for a in list(grads) + list(lands)),
        in_specs=[HBM_SPEC] * (2 * n) + [SEM_SPEC, SEM_SPEC, ANY_SPEC], out_specs=(HBM_SPEC,) * (2 * n),
        input_output_aliases={i: i for i in range(2 * n)},
        compiler_params=pltpu.CompilerParams(has_side_effects=EFFECT),
    )(*grads, *lands, send_sems, recv_sems, after)
    return list(outs[:n]), list(outs[n:])


def _add_pieces(g, land, kind, chip_idx, core_idx, name):
    _, h, cdim = land.shape
    tr, tc = _tile(h, 256), _tile(cdim, 2048)
    nrb, ncb = h // tr, cdim // tc

    def body(k_ref, c_ref, g_ref, l_ref, o_ref):
        acc = g_ref[...].astype(F32)
        for s in range(N_DEV - 1):
            acc = acc + l_ref[s].astype(F32)
        o_ref[...] = acc

    if kind == "col":
        g_spec = pl.BlockSpec((tr, tc), lambda i, j, kr, cr: (cr[0] * nrb + i, kr[0] * ncb + j))
    else:
        g_spec = pl.BlockSpec((None, tr, tc), lambda i, j, kr, cr: (kr[0], cr[0] * nrb + i, j))
    return pl.pallas_call(
        body, name=name, out_shape=jax.ShapeDtypeStruct((2 * h, cdim), F32),
        grid_spec=pltpu.PrefetchScalarGridSpec(
            num_scalar_prefetch=2, grid=(nrb, ncb),
            in_specs=[g_spec, pl.BlockSpec((N_DEV - 1, tr, tc), lambda i, j, kr, cr: (0, i, j))],
            out_specs=pl.BlockSpec((tr, tc), lambda i, j, kr, cr: (cr[0] * nrb + i, j))),
        compiler_params=_params("parallel", "parallel"),
    )(chip_idx, core_idx, g, land)


def _join_halves(shards, name):
    n = len(shards)

    def body(*refs):
        ins, outs = refs[:n], refs[n:2 * n]
        send_sems, recv_sems = refs[2 * n:]
        x, y, c = _place()
        copies = []
        for w in range(n):
            h = shards[w].shape[0] // 2
            cp = pltpu.make_async_remote_copy(
                src_ref=ins[w].at[pl.ds(c * h, h), :], dst_ref=outs[w].at[pl.ds(c * h, h), :],
                send_sem=send_sems.at[w], recv_sem=recv_sems.at[w], device_id=(x, y, 1 - c), device_id_type=MESH)
            cp.start()
            copies.append(cp)
        for w in range(n):
            h = shards[w].shape[0] // 2
            theirs = outs[w].at[pl.ds((1 - c) * h, h), :]
            pltpu.make_async_remote_copy(
                src_ref=theirs, dst_ref=theirs, send_sem=send_sems.at[w], recv_sem=recv_sems.at[w],
                device_id=(x, y, 1 - c), device_id_type=MESH).wait_recv()
        for cp in copies:
            cp.wait_send()

    return pl.pallas_call(
        body, name=name, out_shape=tuple(jax.ShapeDtypeStruct(s.shape, s.dtype) for s in shards),
        in_specs=[HBM_SPEC] * n, out_specs=tuple([HBM_SPEC] * n),
        input_output_aliases={w: w for w in range(n)},
        scratch_shapes=[pltpu.SemaphoreType.DMA((n,))] * 2,
    )(*shards)


def _sum_devices(gathered, rows, name):
    n = gathered.shape[1]

    def body(g_ref, o_ref):
        acc = g_ref[0:rows, :]
        for dev in range(1, N_DEV):
            acc = acc + g_ref[dev * rows:(dev + 1) * rows, :]
        o_ref[...] = acc

    return pl.pallas_call(
        body, name=name, out_shape=jax.ShapeDtypeStruct((rows, n), F32),
        in_specs=[VMEM_SPEC], out_specs=VMEM_SPEC,
        compiler_params=pltpu.CompilerParams(vmem_limit_bytes=VMEM_LIMIT),
    )(gathered)


def _pack(vectors, width):
    flat = [v.reshape(-1) for v in vectors]
    offsets, total = [], 0
    for f in flat:
        offsets.append(total)
        total += f.shape[0]
    rows = -(-total // (width * SUBLANE)) * SUBLANE
    flat.append(jnp.zeros((rows * width - total,), F32))
    return jnp.concatenate(flat).reshape(rows, width), offsets


def kernel(x, c, mod_w, mod_b, norm_g, a_w_in, a_conv_w, a_conv_b, a_w_out, b_w_in, b_ln_g, b_ln_b, b_w_s, b_b_s, b_w_out, final_g, loss_target, m_mod_w, m_mod_b, m_norm_g, m_a_w_in, m_a_conv_w, m_a_conv_b, m_a_w_out, m_b_w_in, m_b_ln_g, m_b_ln_b, m_b_w_s, m_b_b_s, m_b_w_out, m_final_g, v_mod_w, v_mod_b, v_norm_g, v_a_w_in, v_a_conv_w, v_a_conv_b, v_a_w_out, v_b_w_in, v_b_ln_g, v_b_ln_b, v_b_w_s, v_b_b_s, v_b_w_out, v_final_g):
    seq, d = x.shape[1], x.shape[2]
    e = a_conv_b.shape[1]
    wd = mod_w.shape[2]
    ax, ay, ac = _place()
    chip = 2 * ax + ay
    dev = 2 * chip + ac
    chip_idx = jnp.reshape(chip, (1,)).astype(jnp.int32)
    core_idx = jnp.reshape(ac, (1,)).astype(jnp.int32)

    x2d = x[0]
    target = loss_target[0]

    w_names = ["a_w_in", "a_w_out", "b_w_in", "b_w_out"]
    w_kinds = ["col", "row", "col", "row"]
    w_shards = [a_w_in[0], a_w_out[0], b_w_in[0], b_w_out[0]]
    w_shapes = [sh.shape for sh in w_shards]
    placed = [_place_own(sh, kind, chip_idx, "place_" + nm) for nm, sh, kind in zip(w_names, w_shards, w_kinds)]
    a_groups, far_groups = [[(0, 0), (0, 1)]], [[(0, 2)]]
    whole_group = [[(0, 0), (0, 1), (0, 2)]]

    es = e // N_CHIP
    packed0, offs0 = _pack([c, a_conv_w, b_ln_g, b_ln_b], 1024)
    gathered0 = _all_gather(packed0, "gather_params").reshape(N_DEV, -1)
    c_all = gathered0[:, :d]
    per_chip = gathered0[0::2]

    def from_chips_cols(k, rows_):
        got = per_chip[:, offs0[k]:offs0[k] + rows_ * es].reshape(N_CHIP, rows_, es)
        return jnp.transpose(got, (1, 0, 2)).reshape(rows_, e)

    conv_w_full = from_chips_cols(1, 3)
    conv_w = [conv_w_full[k:k + 1] for k in range(3)]
    ln_g, ln_b = from_chips_cols(2, 1), from_chips_cols(3, 1)
    mod_b_cols = lax.dynamic_slice_in_dim(mod_b, chip * wd, wd, axis=1)[:, None, :]
    c_act, mod_part = _mod_fwd(c_all, mod_w, mod_b_cols, "mod_fwd")
    n_layer = mod_w.shape[0]
    mod_gathered = _all_gather(mod_part, "gather_mod")
    a_sems, (wa_in,), a_token = _gather_weights_start(
        placed[:1], w_kinds[:1], w_shapes[:1], a_groups, mod_gathered, "gather_a_w_in_start")
    mod_all = mod_gathered.reshape(N_CHIP, 2, n_layer, N_DEV, wd)[:, 0]
    mod_all = jnp.transpose(mod_all, (1, 2, 0, 3)).reshape(n_layer, N_DEV, N_CHIP * wd)
    mod_me = lax.dynamic_index_in_dim(mod_all, dev, axis=1, keepdims=False)
    shift = [mod_me[l:l + 1, 0:d] for l in range(n_layer)]
    scale = [mod_me[l:l + 1, d:2 * d] for l in range(n_layer)]
    gate = [mod_me[l:l + 1, 2 * d:3 * d] for l in range(n_layer)]

    g0, g1, gf = norm_g[0:1], norm_g[1:2], final_g[None, :]
    h0 = _norm_mod(x2d, g0, scale[0], shift[0], "norm_mod0")

    def slab(r):
        return jnp.bitwise_xor(chip_idx, r)

    def arrived(sems, group, weight, w, after, name):
        return _gather_weights_wait(sems, [(0, j) for _, j in group], [weight], [w_kinds[w]], [w_shapes[w]],
                                    after, name)[0]

    proj0 = _mm_proj_slab(h0, wa_in, None, slab(0), 4, "a_proj_own", a_token)
    wa_in = arrived(a_sems[0], a_groups[0], wa_in, 0, proj0, "gather_wait_near")
    far_sems, (wa_in,), far_token = _gather_weights_start(
        [wa_in], w_kinds[:1], w_shapes[:1], far_groups, a_token, "gather_a_w_in_far_start")
    proj0 = _mm_proj_slab(h0, wa_in, proj0, slab(2), 4, "a_proj_x", far_token)
    proj0 = _mm_proj_slab(h0, wa_in, proj0, slab(1), 4, "a_proj_y", far_token)
    wa_in = arrived(far_sems[0], far_groups[0], wa_in, 0, proj0, "gather_wait_far")
    def start_whole(w, after, name):
        sems, (flight,), token = _gather_weights_start(
            placed[w:w + 1], w_kinds[w:w + 1], w_shapes[w:w + 1], whole_group, after, name)
        return sems[0], flight, token

    ao_sems, wa_out, ao_token = start_whole(1, wa_in, "gather_a_w_out_start")
    proj0 = _mm_proj_slab(h0, wa_in, proj0, slab(3), 4, "a_proj_far", ao_token)
    wa_out = arrived(ao_sems, whole_group[0], wa_out, 1, proj0, "gather_wait_a_w_out")
    bi_sems, wb_in, bi_token = start_whole(2, wa_out, "gather_b_w_in_start")
    y0, br0, x1, h1 = _conv_fwd(proj0, conv_w, a_conv_b, wa_out, x2d, gate[0], g1, scale[1], shift[1],
                                "conv_fwd", bi_token)
    wb_in = arrived(bi_sems, whole_group[0], wb_in, 2, h1, "gather_wait_b_w_in")
    bo_sems, wb_out, bo_token = start_whole(3, wb_in, "gather_b_w_out_start")
    proj1 = _mm_proj(h1, wb_in, 3, "b_proj", bo_token)
    b_s_t = jnp.transpose(b_b_s[0])
    wb_out = arrived(bo_sems, whole_group[0], wb_out, 3, proj1, "gather_wait_b_w_out")
    y1, dx2, dbr1, loss_part, g_final_g, dgate1 = _gmlp_fwd_head(
        proj1, ln_g, ln_b, b_w_s[0], b_s_t, wb_out, x1, gate[1], gf, target, "gmlp_fwd_head")

    gw_b_out = _mm_dw_out(y1, dbr1, "b_out_dw")
    dproj1, g_w_s, g_b_s_t, g_ln_g, g_ln_b = _gmlp_bwd(
        proj1, dbr1, wb_out, ln_g, ln_b, b_w_s[0], jnp.swapaxes(b_w_s[0], 1, 2), b_s_t, "gmlp_bwd")
    gw_b_in = _mm_dw_in(h1, dproj1, "b_proj_dw")
    b_kinds = ["col", "row"]
    b_send, b_recv, b_grads, b_lands, b_token = _reduce_start(
        [gw_b_in, gw_b_out.reshape(N_CHIP, e // N_CHIP, d)], b_kinds, "reduce_b_start")
    dx1, dshift1, dscale1, g_g1, dbr0, dgate0 = _mm_dh_norm_bwd(
        dproj1, wb_in, x1, dx2, g1, scale[1], "b_proj_dx", b_token, br=br0, gate=gate[0])

    gw_a_out = _mm_dw_out(y0, dbr0, "a_out_dw")
    dproj0, g_w0, g_w1, g_w2, g_conv_b = _conv_bwd(proj0, dbr0, wa_out, conv_w, a_conv_b, "conv_bwd")
    gw_a_in = _mm_dw_in(h0, dproj0, "a_proj_dw")
    a_kinds = ["col", "row"]
    a_send, a_recv, a_grads, a_lands, a_token = _reduce_start(
        [gw_a_in, gw_a_out.reshape(N_CHIP, e // N_CHIP, d)], a_kinds, "reduce_a_start")
    grad_x, dshift0, dscale0, g_g0 = _mm_dh_norm_bwd(dproj0, wa_in, x2d, dx1, g0, scale[0], "a_proj_dx", a_token)

    dmod = jnp.concatenate([dshift0, dscale0, dgate0, dshift1, dscale1, dgate1], axis=1)
    small = [loss_part[0, 0:1], g_final_g, g_g0, g_g1, jnp.concatenate([g_w0, g_w1, g_w2], axis=0), g_conv_b, g_ln_g, g_ln_b,
             jnp.transpose(g_b_s_t), g_w_s, dmod]
    packed, offs = _pack(small, 1024)
    rows = packed.shape[0]
    gathered = _all_gather(packed, "gather_small")
    summed = _sum_devices(gathered, rows, "sum_small").reshape(-1)

    def take(k, shape):
        size = math.prod(shape)
        return summed[offs[k]:offs[k] + size].reshape(shape)

    loss = take(0, ())
    grad_final_g = take(1, (d,))
    grad_norm_g = jnp.concatenate([take(2, (1, d)), take(3, (1, d))], axis=0)
    grad_conv_w_full = take(4, (3, e))
    grad_a_conv_b = take(5, (1, e))
    grad_ln_g_full = take(6, (1, e))
    grad_ln_b_full = take(7, (1, e))
    grad_b_b_s = take(8, (1, GROUPS, CHUNK))
    grad_b_w_s = take(9, (1, GROUPS, CHUNK, CHUNK))
    grad_mod_b = take(10, (n_layer, 3 * d))
    grad_a_conv_w = lax.dynamic_slice_in_dim(grad_conv_w_full, chip * es, es, axis=1)[None]
    grad_b_ln_g = lax.dynamic_slice_in_dim(grad_ln_g_full, chip * es, es, axis=1)
    grad_b_ln_b = lax.dynamic_slice_in_dim(grad_ln_b_full, chip * es, es, axis=1)
    dmod_all = gathered.reshape(N_DEV, rows * 1024)[:, offs[10]:offs[10] + n_layer * 3 * d]
    dmod_all = dmod_all.reshape(N_DEV, n_layer, 3 * d)
    dmod_cols = jnp.transpose(lax.dynamic_slice_in_dim(dmod_all, chip * wd, wd, axis=2), (1, 0, 2))

    b_grads, b_lands = _reduce_wait(b_send, b_recv, b_grads, b_lands, b_kinds, summed, "reduce_b_wait")
    a_grads, a_lands = _reduce_wait(a_send, a_recv, a_grads, a_lands, a_kinds, summed, "reduce_a_wait")
    summed_halves = [
        _add_pieces(g, land, kind, chip_idx, core_idx, "add_pieces_" + nm)
        for g, land, kind, nm in [(a_grads[0], a_lands[0], "col", "a_w_in"), (b_grads[0], b_lands[0], "col", "b_w_in"),
                                  (a_grads[1], a_lands[1], "row", "a_w_out"), (b_grads[1], b_lands[1], "row", "b_w_out")]]
    g_a_w_in, g_b_w_in, g_a_w_out, g_b_w_out = _join_halves(summed_halves, "join_halves")

    grad_mod_w, delta_mod_w, new_m_mod_w, new_v_mod_w = _mod_w_update(
        jnp.transpose(c_act), dmod_cols, mod_w, m_mod_w, v_mod_w, "mod_w_update")
    upd, big_grads = {}, {}
    for nm, w, g, m, v in [("a_w_in", a_w_in, g_a_w_in, m_a_w_in, v_a_w_in), ("a_w_out", a_w_out, g_a_w_out, m_a_w_out, v_a_w_out),
                           ("b_w_in", b_w_in, g_b_w_in, m_b_w_in, v_b_w_in), ("b_w_out", b_w_out, g_b_w_out, m_b_w_out, v_b_w_out)]:
        g_out, *rest = _adamw_2d(w[0], g, m[0], v[0], "adamw_" + nm)
        big_grads[nm] = g_out[None]
        upd[nm] = tuple(o[None] for o in rest)
    small_w = [("mod_b", mod_b, grad_mod_b, m_mod_b, v_mod_b), ("norm_g", norm_g, grad_norm_g, m_norm_g, v_norm_g),
               ("a_conv_w", a_conv_w, grad_a_conv_w, m_a_conv_w, v_a_conv_w),
               ("a_conv_b", a_conv_b, grad_a_conv_b, m_a_conv_b, v_a_conv_b),
               ("b_ln_g", b_ln_g, grad_b_ln_g, m_b_ln_g, v_b_ln_g), ("b_ln_b", b_ln_b, grad_b_ln_b, m_b_ln_b, v_b_ln_b),
               ("b_w_s", b_w_s, grad_b_w_s, m_b_w_s, v_b_w_s), ("b_b_s", b_b_s, grad_b_b_s, m_b_b_s, v_b_b_s),
               ("final_g", final_g, grad_final_g, m_final_g, v_final_g)]

    def flat2d(a):
        return a.reshape(-1, a.shape[-1])

    res = _adamw_small([flat2d(t[1]) for t in small_w], [flat2d(t[2]) for t in small_w],
                       [flat2d(t[3]) for t in small_w], [flat2d(t[4]) for t in small_w], "adamw_small")
    for (nm, w, _, _, _), r3 in zip(small_w, res):
        upd[nm] = tuple(o.reshape(w.shape) for o in r3)
    upd["mod_w"] = (delta_mod_w, new_m_mod_w, new_v_mod_w)

    grads = {"mod_w": grad_mod_w, "mod_b": grad_mod_b, "norm_g": grad_norm_g, "a_conv_w": grad_a_conv_w,
             "a_conv_b": grad_a_conv_b, "b_ln_g": grad_b_ln_g, "b_ln_b": grad_b_ln_b, "b_w_s": grad_b_w_s,
             "b_b_s": grad_b_b_s, "final_g": grad_final_g, **big_grads}
    order = ["mod_w", "mod_b", "norm_g", "a_w_in", "a_conv_w", "a_conv_b", "a_w_out", "b_w_in", "b_ln_g", "b_ln_b",
             "b_w_s", "b_b_s", "b_w_out", "final_g"]
    return (loss, grad_x[None], *[grads[k] for k in order], *[upd[k][0] for k in order],
            *[upd[k][1] for k in order], *[upd[k][2] for k in order])
```

```python
import functools
import math

import jax
import jax.numpy as jnp
from jax import lax
from jax.experimental import pallas as pl
from jax.experimental.pallas import tpu as pltpu

F32 = jnp.float32
BF16 = jnp.bfloat16
MESH = pl.DeviceIdType.MESH

N_DEV = 8
N_CHIP = 4
SUBLANE = 8
PACK = 16
ACT = F32
RMS_EPS = 1e-6
LN_EPS = 1e-5
CHUNK = 128
GROUPS = 8
ADAM_LR = 0.001
ADAM_B1 = 0.9
ADAM_B2 = 0.999
ADAM_EPS = 1e-08
ADAM_WD = 0.01
ADAM_STEP = 10
VMEM_LIMIT = 56 << 20

HBM_SPEC = pl.BlockSpec(memory_space=pltpu.HBM)
VMEM_SPEC = pl.BlockSpec(memory_space=pltpu.VMEM)
SEM_SPEC = pl.BlockSpec(memory_space=pltpu.SEMAPHORE)
ANY_SPEC = pl.BlockSpec(memory_space=pl.ANY)
EFFECT = pltpu.SideEffectType.DATAFLOW_SIDE_EFFECTING


def _params(*sem):
    return pltpu.CompilerParams(dimension_semantics=sem, vmem_limit_bytes=VMEM_LIMIT)


def _tile(n, want):
    if n <= want:
        return n
    t = want
    while n % t:
        t -= 128
    return t


def _sigmoid(x):
    return 1.0 / (1.0 + jnp.exp(-x))


def _silu_and_grad(x):
    s = _sigmoid(x)
    return x * s, s * (1.0 + x * (1.0 - s))


def _gelu_and_grad(x):
    cdf = 0.5 * (1.0 + lax.erf(x * (1.0 / math.sqrt(2.0))))
    pdf = jnp.exp(-0.5 * x * x) * (1.0 / math.sqrt(2.0 * math.pi))
    return x * cdf, cdf + x * pdf


def _gelu(x):
    return x * (0.5 * (1.0 + lax.erf(x * (1.0 / math.sqrt(2.0)))))


def _rms(x):
    r = lax.rsqrt(jnp.mean(x * x, axis=-1, keepdims=True) + RMS_EPS)
    return x * r, r


def _rms_bwd(dxn, xn, r):
    return r * (dxn - xn * jnp.mean(dxn * xn, axis=-1, keepdims=True))


def _colsum(a):
    return jnp.sum(a, axis=0, keepdims=True)


def _shift_down(cur, before, k):
    rolled = pltpu.roll(cur, k, 0)
    row = lax.broadcasted_iota(jnp.int32, before.shape, 0)
    head = jnp.where(row < k, pltpu.roll(before, k, 0), rolled[:SUBLANE])
    return jnp.concatenate([head, rolled[SUBLANE:]], axis=0)


def _shift_up(cur, after, k):
    n = cur.shape[0]
    rolled = pltpu.roll(cur, n - k, 0)
    row = lax.broadcasted_iota(jnp.int32, after.shape, 0)
    tail = jnp.where(row >= SUBLANE - k, pltpu.roll(after, SUBLANE - k, 0), rolled[n - SUBLANE:])
    return jnp.concatenate([rolled[:n - SUBLANE], tail], axis=0)


def _after_spec():
    return pl.BlockSpec((SUBLANE, 128), lambda *_: (0, 0))


def _mm_proj(h, w, n_split, name, after):
    s, d = h.shape
    e = w.shape[1] // n_split
    tm, tn = _tile(s, 1024), _tile(e, 2048)
    nj = e // tn

    def body(h_ref, w_ref, after_ref, o_ref):
        o_ref[...] = jnp.dot(h_ref[...], w_ref[...], preferred_element_type=F32).astype(ACT)

    return pl.pallas_call(
        body, name=name,
        out_shape=jax.ShapeDtypeStruct((n_split, s, e), ACT),
        grid=(s // tm, n_split * nj),
        in_specs=[pl.BlockSpec((tm, d), lambda i, j: (i, 0)), pl.BlockSpec((d, tn), lambda i, j: (0, j)),
                  _after_spec()],
        out_specs=pl.BlockSpec((None, tm, tn), lambda i, j: (j // nj, i, j % nj)),
        compiler_params=_params("parallel", "parallel"),
    )(h, w, after)


def _mm_proj_slab(h, w, proj, q_idx, n_split, name, after):
    s, d = h.shape
    e = w.shape[1] // n_split
    tm, tn = _tile(s, 1024), _tile(e, 2048)
    nj = e // tn

    def body(q_ref, h_ref, w_ref, *rest):
        o_ref = rest[-1]
        o_ref[...] = jnp.dot(h_ref[...], w_ref[...], preferred_element_type=F32).astype(ACT)

    in_specs = [pl.BlockSpec((tm, d), lambda i, j, qr: (i, 0)), pl.BlockSpec((d, tn), lambda i, j, qr: (0, qr[0] * nj + j)),
                _after_spec()]
    args = [q_idx, h, w, after]
    aliases = {}
    if proj is not None:
        in_specs.append(ANY_SPEC)
        args.append(proj)
        aliases = {4: 0}
    return pl.pallas_call(
        body, name=name,
        out_shape=jax.ShapeDtypeStruct((n_split, s, e), ACT),
        grid_spec=pltpu.PrefetchScalarGridSpec(
            num_scalar_prefetch=1, grid=(s // tm, nj), in_specs=in_specs,
            out_specs=pl.BlockSpec((None, tm, tn), lambda i, j, qr: (qr[0], i, j))),
        input_output_aliases=aliases,
        compiler_params=_params("parallel", "parallel"),
    )(*args)


def _mm_dh_norm_bwd(dp, w, x, dx_in, g, scale, name, after, br=None, gate=None):
    nq, s, e = dp.shape
    d = w.shape[0]
    has_branch = br is not None
    tm, tk = _tile(s, 1024), _tile(e, 1024 if has_branch else 2048)
    nkq = e // tk
    nk = nq * nkq

    def body(*refs):
        if has_branch:
            (a_ref, b_ref, after_ref, x_ref, dxin_ref, g_ref, sc_ref, br_ref, gate_ref,
             dx_ref, dsh_ref, dsc_ref, dg_ref, dbr_ref, dgate_ref, acc_ref) = refs
        else:
            (a_ref, b_ref, after_ref, x_ref, dxin_ref, g_ref, sc_ref,
             dx_ref, dsh_ref, dsc_ref, dg_ref, acc_ref) = refs
        i, k = pl.program_id(0), pl.program_id(1)

        @pl.when(jnp.logical_and(i == 0, k == 0))
        def _():
            dsh_ref[...] = jnp.zeros_like(dsh_ref)
            dsc_ref[...] = jnp.zeros_like(dsc_ref)
            dg_ref[...] = jnp.zeros_like(dg_ref)
            if has_branch:
                dgate_ref[...] = jnp.zeros_like(dgate_ref)

        @pl.when(k == 0)
        def _():
            acc_ref[...] = jnp.zeros_like(acc_ref)
        acc_ref[...] += lax.dot_general(a_ref[...], b_ref[...], (((1,), (1,)), ((), ())), preferred_element_type=F32)

        @pl.when(k == nk - 1)
        def _():
            dh = acc_ref[...]
            g_vec = g_ref[...]
            xn, r = _rms(x_ref[...])
            dsh_ref[...] += _colsum(dh)
            dsc_ref[...] += _colsum(dh * (xn * g_vec))
            da = dh * (1.0 + sc_ref[...])
            dg_ref[...] += _colsum(da * xn)
            dx = dxin_ref[...] + _rms_bwd(da * g_vec, xn, r)
            dx_ref[...] = dx
            if has_branch:
                dgate_ref[...] += _colsum(dx * br_ref[...])
                dbr_ref[...] = (gate_ref[...] * dx).astype(BF16)

    rows = pl.BlockSpec((tm, d), lambda i, k: (i, 0))
    vec = pl.BlockSpec((1, d), lambda i, k: (0, 0))
    vec_out = jax.ShapeDtypeStruct((1, d), F32)
    in_specs = [pl.BlockSpec((None, tm, tk), lambda i, k: (k // nkq, i, k % nkq)),
                pl.BlockSpec((d, tk), lambda i, k: (0, k)), _after_spec(), rows, rows, vec, vec]
    out_shape = [jax.ShapeDtypeStruct((s, d), F32), vec_out, vec_out, vec_out]
    out_specs = [rows, vec, vec, vec]
    args = [dp, w, after, x, dx_in, g, scale]
    if has_branch:
        in_specs += [rows, vec]
        out_shape += [jax.ShapeDtypeStruct((s, d), BF16), vec_out]
        out_specs += [rows, vec]
        args += [br, gate]
    return pl.pallas_call(
        body, name=name, out_shape=tuple(out_shape), grid=(s // tm, nk),
        in_specs=in_specs, out_specs=tuple(out_specs), scratch_shapes=[pltpu.VMEM((tm, d), F32)],
        compiler_params=_params("arbitrary", "arbitrary"),
    )(*args)


def _dw_body(n_t):
    def body(a_ref, b_ref, o_ref, acc_ref):
        t = pl.program_id(2)

        @pl.when(t == 0)
        def _():
            acc_ref[...] = jnp.zeros_like(acc_ref)
        acc_ref[...] += lax.dot_general(a_ref[...], b_ref[...], (((0,), (0,)), ((), ())), preferred_element_type=F32)

        @pl.when(t == n_t - 1)
        def _():
            o_ref[...] = acc_ref[...].astype(o_ref.dtype)
    return body


def _mm_dw_in(h, dp, name):
    s, d = h.shape
    nq, _, e = dp.shape
    tm, tn, tt = _tile(d, 1024), _tile(e, 1024), _tile(s, 2048)
    nj = e // tn

    return pl.pallas_call(
        _dw_body(s // tt), name=name,
        out_shape=jax.ShapeDtypeStruct((d, nq * e), BF16),
        grid=(d // tm, nq * nj, s // tt),
        in_specs=[pl.BlockSpec((tt, tm), lambda i, j, t: (t, i)),
                  pl.BlockSpec((None, tt, tn), lambda i, j, t: (j // nj, t, j % nj))],
        out_specs=pl.BlockSpec((tm, tn), lambda i, j, t: (i, j)),
        scratch_shapes=[pltpu.VMEM((tm, tn), F32)],
        compiler_params=_params("parallel", "parallel", "arbitrary"),
    )(h, dp)


def _mm_dw_out(y, dbr, name):
    s, e = y.shape
    d = dbr.shape[1]
    tm, tn, tt = _tile(e, 1024), _tile(d, 1024), _tile(s, 2048)

    return pl.pallas_call(
        _dw_body(s // tt), name=name,
        out_shape=jax.ShapeDtypeStruct((e, d), BF16),
        grid=(e // tm, d // tn, s // tt),
        in_specs=[pl.BlockSpec((tt, tm), lambda i, j, t: (t, i)), pl.BlockSpec((tt, tn), lambda i, j, t: (t, j))],
        out_specs=pl.BlockSpec((tm, tn), lambda i, j, t: (i, j)),
        scratch_shapes=[pltpu.VMEM((tm, tn), F32)],
        compiler_params=_params("parallel", "parallel", "arbitrary"),
    )(y, dbr)


def _row_spec(ts, d):
    return pl.BlockSpec((ts, d), lambda i: (i, 0))


def _vec_spec(d):
    return pl.BlockSpec((1, d), lambda i: (0, 0))


def _norm_mod(x, g, scale, shift, name):
    s, d = x.shape
    ts = _tile(s, 512)

    def body(x_ref, g_ref, sc_ref, sh_ref, h_ref):
        xn, _ = _rms(x_ref[...])
        h_ref[...] = ((xn * g_ref[...]) * (1.0 + sc_ref[...]) + sh_ref[...]).astype(BF16)

    return pl.pallas_call(
        body, name=name, out_shape=jax.ShapeDtypeStruct((s, d), BF16), grid=(s // ts,),
        in_specs=[_row_spec(ts, d), _vec_spec(d), _vec_spec(d), _vec_spec(d)],
        out_specs=_row_spec(ts, d), compiler_params=_params("parallel"),
    )(x, g, scale, shift)


def _head_tile(x1, br, gate, gf, target):
    d = x1.shape[-1]
    xn, r = _rms(x1 + gate * br)
    err = xn * gf - target
    loss = 0.5 * jnp.sum(jnp.mean(err * err, axis=-1, keepdims=True))
    dout = err * (1.0 / d)
    dx = _rms_bwd(dout * gf, xn, r)
    return dx, gate * dx, loss, _colsum(dout * xn), _colsum(dx * br)


def _conv_tiles(s, e):
    return _tile(s, 512), _tile(e, 512)


def _f32(ref):
    return ref[...].astype(F32)


def _rows_before(halo_ref):
    return _f32(halo_ref)[PACK - SUBLANE:]


def _rows_after(halo_ref):
    return _f32(halo_ref)[:SUBLANE]


def _conv_fwd(proj, conv_w, conv_b, w_out, x, gate, g_next, scale_next, shift_next, name, after):
    _, s, e = proj.shape
    d = w_out.shape[1]
    ts, te = _tile(s, 256), e
    hb = ts // PACK

    def body(bg_ref, cg_ref, xi_ref, z_ref, cgp_ref, xip_ref, w0_ref, w1_ref, w2_ref, b_ref, wo_ref, after_ref,
             x_ref, gate_ref, g_ref, sc_ref, sh_ref, y_ref, br_ref, x1_ref, h_ref):
        cx = _f32(cg_ref) * _f32(xi_ref)
        before = jnp.where(pl.program_id(0) > 0, _rows_before(cgp_ref) * _rows_before(xip_ref), 0.0)
        conv = b_ref[...] + w2_ref[...] * cx
        conv = conv + w0_ref[...] * _shift_down(cx, before, 2)
        conv = conv + w1_ref[...] * _shift_down(cx, before, 1)
        z = _f32(z_ref)
        y = ((z * _sigmoid(z)) * _f32(bg_ref) * conv).astype(BF16)
        y_ref[...] = y
        br = jnp.dot(y, wo_ref[...], preferred_element_type=F32)
        br_ref[...] = br
        x1 = x_ref[...] + gate_ref[...] * br
        x1_ref[...] = x1
        xn, _ = _rms(x1)
        h_ref[...] = ((xn * g_ref[...]) * (1.0 + sc_ref[...]) + sh_ref[...]).astype(BF16)

    def part(q):
        return pl.BlockSpec((None, ts, te), lambda i: (q, i, 0))

    def halo_before(q):
        return pl.BlockSpec((None, PACK, te), lambda i: (q, jnp.maximum(i * hb - 1, 0), 0))

    return pl.pallas_call(
        body, name=name,
        out_shape=(jax.ShapeDtypeStruct((s, e), BF16), jax.ShapeDtypeStruct((s, d), F32),
                   jax.ShapeDtypeStruct((s, d), F32), jax.ShapeDtypeStruct((s, d), BF16)), grid=(s // ts,),
        in_specs=[part(0), part(1), part(2), part(3), halo_before(1), halo_before(2)]
        + [_vec_spec(e)] * 4 + [pl.BlockSpec((e, d), lambda i: (0, 0)), _after_spec(), _row_spec(ts, d)]
        + [_vec_spec(d)] * 4,
        out_specs=(_row_spec(ts, e), _row_spec(ts, d), _row_spec(ts, d), _row_spec(ts, d)),
        compiler_params=_params("parallel"),
    )(proj, proj, proj, proj, proj, proj, *conv_w, conv_b, w_out, after, x, gate, g_next, scale_next, shift_next)


def _conv_bwd(proj, dbr, w_out, conv_w, conv_b, name):
    _, s, e = proj.shape
    d = dbr.shape[1]
    ts, te = _conv_tiles(s, e)
    hb = ts // PACK
    n_i = s // ts
    last_halo = s // PACK - 1
    nt = (((1,), (1,)), ((), ()))

    def body(bg_ref, cg_ref, xi_ref, z_ref, dbr_ref, cgp_ref, xip_ref, bgn_ref, zn_ref, dbrn_ref, wo_ref,
             w0_ref, w1_ref, w2_ref, b_ref, dp_ref, dw0_ref, dw1_ref, dw2_ref, db_ref):
        i = pl.program_id(1)

        @pl.when(i == 0)
        def _():
            for acc in (dw0_ref, dw1_ref, dw2_ref, db_ref):
                acc[...] = jnp.zeros_like(acc)
        wo = wo_ref[...]
        dy = lax.dot_general(dbr_ref[...], wo, nt, preferred_element_type=F32)
        dyn = lax.dot_general(dbrn_ref[...], wo, nt, preferred_element_type=F32)[:SUBLANE]
        bg, cg, xi, z = _f32(bg_ref), _f32(cg_ref), _f32(xi_ref), _f32(z_ref)
        w0, w1, w2 = w0_ref[...], w1_ref[...], w2_ref[...]
        cx = cg * xi
        before = jnp.where(i > 0, _rows_before(cgp_ref) * _rows_before(xip_ref), 0.0)
        cx1 = _shift_down(cx, before, 1)
        cx2 = _shift_down(cx, before, 2)
        conv = b_ref[...] + w2 * cx
        conv = conv + w0 * cx2
        conv = conv + w1 * cx1
        sz, dsz = _silu_and_grad(z)
        dp_ref[3] = (dy * bg * conv * dsz).astype(BF16)
        dp_ref[0] = (dy * sz * conv).astype(BF16)
        dconv = dy * sz * bg
        zn = _rows_after(zn_ref)
        after = jnp.where(i < n_i - 1, dyn * (zn * _sigmoid(zn)) * _rows_after(bgn_ref), 0.0)
        db_ref[...] += _colsum(dconv)
        dw2_ref[...] += _colsum(dconv * cx)
        dw1_ref[...] += _colsum(dconv * cx1)
        dw0_ref[...] += _colsum(dconv * cx2)
        dcx = w2 * dconv + w1 * _shift_up(dconv, after, 1) + w0 * _shift_up(dconv, after, 2)
        dp_ref[1] = (dcx * xi).astype(BF16)
        dp_ref[2] = (dcx * cg).astype(BF16)

    def part(q):
        return pl.BlockSpec((None, ts, te), lambda j, i: (q, i, j))

    def halo_before(q):
        return pl.BlockSpec((None, PACK, te), lambda j, i: (q, jnp.maximum(i * hb - 1, 0), j))

    def halo_after(q):
        return pl.BlockSpec((None, PACK, te), lambda j, i: (q, jnp.minimum((i + 1) * hb, last_halo), j))

    return pl.pallas_call(
        body, name=name,
        out_shape=(jax.ShapeDtypeStruct((4, s, e), BF16),) + (jax.ShapeDtypeStruct((1, e), F32),) * 4,
        grid=(e // te, n_i),
        in_specs=[part(0), part(1), part(2), part(3), pl.BlockSpec((ts, d), lambda j, i: (i, 0)),
                  halo_before(1), halo_before(2), halo_after(0), halo_after(3),
                  pl.BlockSpec((PACK, d), lambda j, i: (jnp.minimum((i + 1) * hb, last_halo), 0)),
                  pl.BlockSpec((te, d), lambda j, i: (j, 0))]
        + [pl.BlockSpec((1, te), lambda j, i: (0, j))] * 4,
        out_specs=(pl.BlockSpec((4, ts, te), lambda j, i: (0, i, j)),) + (pl.BlockSpec((1, te), lambda j, i: (0, j)),) * 4,
        compiler_params=_params("parallel", "arbitrary"),
    )(proj, proj, proj, proj, dbr, proj, proj, proj, proj, dbr, w_out, *conv_w, conv_b)


def _tril(w):
    row = lax.broadcasted_iota(jnp.int32, w.shape, 0)
    col = lax.broadcasted_iota(jnp.int32, w.shape, 1)
    return jnp.where(row >= col, w, 0.0)


def _triu(w):
    row = lax.broadcasted_iota(jnp.int32, w.shape, 0)
    col = lax.broadcasted_iota(jnp.int32, w.shape, 1)
    return jnp.where(row <= col, w, 0.0)


def _layer_norm_fwd(v, g, b):
    mu = jnp.mean(v, axis=-1, keepdims=True)
    vc = v - mu
    rstd = lax.rsqrt(jnp.mean(vc * vc, axis=-1, keepdims=True) + LN_EPS)
    vhat = vc * rstd
    return vhat * g + b, vhat, rstd


GMLP_CHUNKS_PER_STEP = 2


def _gmlp_rows(s):
    return CHUNK * min(GMLP_CHUNKS_PER_STEP, s // CHUNK)


def _mix_positions(w_ref, src_scr, dst_scr, gw, mask, bias_ref=None):
    for gi in range(GROUPS):
        cols = slice(gi * gw, (gi + 1) * gw)
        wm = mask(w_ref[gi]).astype(BF16)
        for n in range(src_scr.shape[0] // CHUNK):
            rows = slice(n * CHUNK, (n + 1) * CHUNK)
            out = jnp.dot(wm, src_scr[rows, cols], preferred_element_type=F32)
            if bias_ref is not None:
                out = out + bias_ref[:, gi:gi + 1]
            dst_scr[rows, cols] = out


def _gmlp_fwd_head(proj, ln_g, ln_b, w_s, b_s_t, w_out, x1, gate, gf, target, name):
    _, s, e = proj.shape
    d = w_out.shape[1]
    gw = e // GROUPS
    ts = _gmlp_rows(s)

    def body(pu_ref, pv_ref, pz_ref, g_ref, b_ref, ws_ref, bs_ref, wo_ref, x1_ref, gate_ref, gf_ref, tg_ref,
             y_ref, dx_ref, dbr_ref, loss_ref, dgf_ref, dgate_ref, vn_scr, mix_scr):
        @pl.when(pl.program_id(0) == 0)
        def _():
            loss_ref[...] = jnp.zeros_like(loss_ref)
            dgf_ref[...] = jnp.zeros_like(dgf_ref)
            dgate_ref[...] = jnp.zeros_like(dgate_ref)
        vn, _, _ = _layer_norm_fwd(_gelu(_f32(pv_ref)), g_ref[...], b_ref[...])
        vn_scr[...] = vn.astype(BF16)
        _mix_positions(ws_ref, vn_scr, mix_scr, gw, _tril, bs_ref)
        z = _f32(pz_ref)
        y = ((z * _sigmoid(z)) * (_gelu(_f32(pu_ref)) * mix_scr[...])).astype(BF16)
        y_ref[...] = y
        br = jnp.dot(y, wo_ref[...], preferred_element_type=F32)
        dx, dbr, loss, dgf, dgate = _head_tile(x1_ref[...], br, gate_ref[...], gf_ref[...], tg_ref[...])
        dx_ref[...] = dx
        dbr_ref[...] = dbr.astype(BF16)
        loss_ref[...] += loss
        dgf_ref[...] += dgf
        dgate_ref[...] += dgate

    def part(q):
        return pl.BlockSpec((None, ts, e), lambda i: (q, i, 0))

    return pl.pallas_call(
        body, name=name,
        out_shape=(jax.ShapeDtypeStruct((s, e), BF16), jax.ShapeDtypeStruct((s, d), F32),
                   jax.ShapeDtypeStruct((s, d), BF16), jax.ShapeDtypeStruct((SUBLANE, 128), F32),
                   jax.ShapeDtypeStruct((1, d), F32), jax.ShapeDtypeStruct((1, d), F32)),
        grid=(s // ts,),
        in_specs=[part(0), part(1), part(2), _vec_spec(e), _vec_spec(e),
                  pl.BlockSpec((GROUPS, CHUNK, CHUNK), lambda i: (0, 0, 0)),
                  pl.BlockSpec((CHUNK, GROUPS), lambda i: (0, 0)), pl.BlockSpec((e, d), lambda i: (0, 0)),
                  _row_spec(ts, d), _vec_spec(d), _vec_spec(d), _row_spec(ts, d)],
        out_specs=(_row_spec(ts, e), _row_spec(ts, d), _row_spec(ts, d),
                   pl.BlockSpec((SUBLANE, 128), lambda i: (0, 0)), _vec_spec(d), _vec_spec(d)),
        scratch_shapes=[pltpu.VMEM((ts, e), BF16), pltpu.VMEM((ts, e), F32)],
        compiler_params=_params("arbitrary"),
    )(proj, proj, proj, ln_g, ln_b, w_s, b_s_t, w_out, x1, gate, gf, target)


def _gmlp_bwd(proj, dbr, w_out, ln_g, ln_b, w_s, w_s_t, b_s_t, name):
    _, s, e = proj.shape
    d = dbr.shape[1]
    gw = e // GROUPS
    ts = _gmlp_rows(s)
    n_i = s // ts

    def body(pu_ref, pv_ref, pz_ref, dbr_ref, wo_ref, g_ref, b_ref, ws_ref, wst_ref, bs_ref,
             dp_ref, dws_ref, dbs_ref, dlg_ref, dlb_ref, vn_scr, mix_scr, dm_scr, dvn_scr, dmacc_scr):
        i = pl.program_id(0)

        @pl.when(i == 0)
        def _():
            dws_ref[...] = jnp.zeros_like(dws_ref)
            dlg_ref[...] = jnp.zeros_like(dlg_ref)
            dlb_ref[...] = jnp.zeros_like(dlb_ref)
            dmacc_scr[...] = jnp.zeros_like(dmacc_scr)
        ln_g = g_ref[...]
        u, du_dpu = _gelu_and_grad(_f32(pu_ref))
        v, dv_dpv = _gelu_and_grad(_f32(pv_ref))
        vn, vhat, rstd = _layer_norm_fwd(v, ln_g, b_ref[...])
        vn_scr[...] = vn.astype(BF16)
        _mix_positions(ws_ref, vn_scr, mix_scr, gw, _tril, bs_ref)
        mixed = mix_scr[...]
        dy = lax.dot_general(dbr_ref[...], wo_ref[...], (((1,), (1,)), ((), ())), preferred_element_type=F32)
        sz, dsz = _silu_and_grad(_f32(pz_ref))
        ds = dy * sz
        dp_ref[2] = (dy * (u * mixed) * dsz).astype(BF16)
        dp_ref[0] = (ds * mixed * du_dpu).astype(BF16)
        dm = ds * u
        dm_scr[...] = dm.astype(BF16)
        for n in range(ts // CHUNK):
            rows = slice(n * CHUNK, (n + 1) * CHUNK)
            dmacc_scr[...] += dm[rows]
            for gi in range(GROUPS):
                cols = slice(gi * gw, (gi + 1) * gw)
                dws_ref[gi] += lax.dot_general(dm_scr[rows, cols], vn_scr[rows, cols], (((1,), (1,)), ((), ())),
                                               preferred_element_type=F32)
        _mix_positions(wst_ref, dm_scr, dvn_scr, gw, _triu)
        dvn = dvn_scr[...]
        dlg_ref[...] += _colsum(dvn * vhat)
        dlb_ref[...] += _colsum(dvn)
        dvh = dvn * ln_g
        dv = rstd * (dvh - jnp.mean(dvh, axis=-1, keepdims=True) - vhat * jnp.mean(dvh * vhat, axis=-1, keepdims=True))
        dp_ref[1] = (dv * dv_dpv).astype(BF16)

        @pl.when(i == n_i - 1)
        def _():
            for gi in range(GROUPS):
                dws_ref[gi] = _tril(dws_ref[gi])
                dbs_ref[:, gi:gi + 1] = jnp.sum(dmacc_scr[:, gi * gw:(gi + 1) * gw], axis=1, keepdims=True)

    def part(q):
        return pl.BlockSpec((None, ts, e), lambda i: (q, i, 0))

    w_spec = pl.BlockSpec((GROUPS, CHUNK, CHUNK), lambda i: (0, 0, 0))
    bs_spec = pl.BlockSpec((CHUNK, GROUPS), lambda i: (0, 0))
    return pl.pallas_call(
        body, name=name,
        out_shape=(jax.ShapeDtypeStruct((3, s, e), BF16), jax.ShapeDtypeStruct((GROUPS, CHUNK, CHUNK), F32),
                   jax.ShapeDtypeStruct((CHUNK, GROUPS), F32), jax.ShapeDtypeStruct((1, e), F32),
                   jax.ShapeDtypeStruct((1, e), F32)),
        grid=(n_i,),
        in_specs=[part(0), part(1), part(2), pl.BlockSpec((ts, d), lambda i: (i, 0)),
                  pl.BlockSpec((e, d), lambda i: (0, 0)), _vec_spec(e), _vec_spec(e), w_spec, w_spec, bs_spec],
        out_specs=(pl.BlockSpec((3, ts, e), lambda i: (0, i, 0)), w_spec, bs_spec, _vec_spec(e), _vec_spec(e)),
        scratch_shapes=[pltpu.VMEM((ts, e), BF16), pltpu.VMEM((ts, e), F32), pltpu.VMEM((ts, e), BF16),
                        pltpu.VMEM((ts, e), F32), pltpu.VMEM((CHUNK, e), F32)],
        compiler_params=_params("arbitrary"),
    )(proj, proj, proj, dbr, w_out, ln_g, ln_b, w_s, w_s_t, b_s_t)


def _mod_fwd(c_all, mod_w, mod_b_cols, name):
    n_layer, d, w = mod_w.shape

    def body(c_ref, w_ref, b_ref, ca_ref, o_ref):
        c = c_ref[...]
        ca = c * _sigmoid(c)
        ca_ref[...] = ca
        for li in range(n_layer):
            o_ref[li * N_DEV:(li + 1) * N_DEV, :] = (
                jnp.dot(ca, w_ref[li], preferred_element_type=F32, precision=lax.Precision.HIGHEST) + b_ref[li])

    return pl.pallas_call(
        body, name=name,
        out_shape=(jax.ShapeDtypeStruct((N_DEV, d), F32), jax.ShapeDtypeStruct((n_layer * N_DEV, w), F32)),
        in_specs=[VMEM_SPEC] * 3, out_specs=(VMEM_SPEC, VMEM_SPEC),
        compiler_params=pltpu.CompilerParams(vmem_limit_bytes=VMEM_LIMIT),
    )(c_all, mod_w, mod_b_cols)


def _adamw(w, g, m, v):
    m = ADAM_B1 * m + (1.0 - ADAM_B1) * g
    v = ADAM_B2 * v + (1.0 - ADAM_B2) * (g * g)
    m_hat = m / (1.0 - ADAM_B1 ** ADAM_STEP)
    v_hat = v / (1.0 - ADAM_B2 ** ADAM_STEP)
    delta = -ADAM_LR * (m_hat / (jnp.sqrt(v_hat) + ADAM_EPS) + ADAM_WD * w)
    return delta, m, v


def _adamw_2d(w, g, m, v, name):
    r, c = w.shape
    tr, tc = _tile(r, 512), _tile(c, 1024)

    def body(w_ref, g_ref, m_ref, v_ref, go_ref, d_ref, nm_ref, nv_ref):
        g = g_ref[...]
        go_ref[...] = g
        d_ref[...], nm_ref[...], nv_ref[...] = _adamw(w_ref[...], g, m_ref[...], v_ref[...])

    spec = pl.BlockSpec((tr, tc), lambda i, j: (i, j))
    shape = jax.ShapeDtypeStruct((r, c), F32)
    return pl.pallas_call(
        body, name=name, out_shape=(shape,) * 4, grid=(r // tr, c // tc),
        in_specs=[spec] * 4, out_specs=(spec,) * 4, compiler_params=_params("parallel", "parallel"),
    )(w, g, m, v)


def _mod_w_update(ca_t, dmod_cols, w, m, v, name):
    n_layer, d, wd = w.shape
    tr = _tile(d, 256)

    def body(ca_ref, dm_ref, w_ref, m_ref, v_ref, g_ref, d_ref, nm_ref, nv_ref):
        ca = ca_ref[...]
        dm = dm_ref[...]
        g = ca[:, 0:1] * dm[0:1, :]
        for b in range(1, N_DEV):
            g = g + ca[:, b:b + 1] * dm[b:b + 1, :]
        g_ref[...] = g
        d_ref[...], nm_ref[...], nv_ref[...] = _adamw(w_ref[...], g, m_ref[...], v_ref[...])

    spec = pl.BlockSpec((None, tr, wd), lambda l, i: (l, i, 0))
    shape = jax.ShapeDtypeStruct((n_layer, d, wd), F32)
    return pl.pallas_call(
        body, name=name, out_shape=(shape,) * 4, grid=(n_layer, d // tr),
        in_specs=[pl.BlockSpec((tr, N_DEV), lambda l, i: (i, 0)), pl.BlockSpec((None, N_DEV, wd), lambda l, i: (l, 0, 0)),
                  spec, spec, spec],
        out_specs=(spec,) * 4, compiler_params=_params("parallel", "parallel"),
    )(ca_t, dmod_cols, w, m, v)


def _adamw_small(ws, gs, ms, vs, name):
    n = len(ws)

    def body(*refs):
        ins, outs = refs[:4 * n], refs[4 * n:]
        for k in range(n):
            delta, nm, nv = _adamw(ins[k][...], ins[n + k][...], ins[2 * n + k][...], ins[3 * n + k][...])
            outs[3 * k][...] = delta
            outs[3 * k + 1][...] = nm
            outs[3 * k + 2][...] = nv

    out_shape = []
    for w in ws:
        out_shape += [jax.ShapeDtypeStruct(w.shape, F32)] * 3
    outs = pl.pallas_call(
        body, name=name, out_shape=tuple(out_shape),
        in_specs=[VMEM_SPEC] * (4 * n), out_specs=tuple([VMEM_SPEC] * (3 * n)),
        compiler_params=pltpu.CompilerParams(vmem_limit_bytes=VMEM_LIMIT),
    )(*ws, *gs, *ms, *vs)
    return [tuple(outs[3 * k:3 * k + 3]) for k in range(n)]


def _place():
    return lax.axis_index("x"), lax.axis_index("y"), lax.axis_index("c")


def _other_chips(x, y):
    return [(1 - x, y), (x, 1 - y), (1 - x, 1 - y)]


def _all_gather(block, name, after=None):
    m_per, n = block.shape

    def body(x_ref, *rest):
        out_ref, send_sems, recv_sems, local_sem = rest[-4:]
        x, y, c = _place()
        me, sibling = (x, y, c), (x, y, 1 - c)
        chips = _other_chips(x, y)

        def rows(px, py, pc):
            return out_ref.at[pl.ds((4 * px + 2 * py + pc) * m_per, m_per), :]

        def copy(k, blk, to, src=None):
            return pltpu.make_async_remote_copy(
                src_ref=rows(*blk) if src is None else src, dst_ref=rows(*blk),
                send_sem=send_sems.at[k], recv_sem=recv_sems.at[k], device_id=to, device_id_type=MESH)

        mine = pltpu.make_async_copy(x_ref, rows(*me), local_sem)
        mine.start()
        first = [copy(0, me, sibling, src=x_ref)]
        first += [copy(1 + j, me, (*chip, c), src=x_ref) for j, chip in enumerate(chips)]
        for cp in first:
            cp.start()
        passed = [copy(4 + j, (*chip, c), sibling) for j, chip in enumerate(chips)]
        for j, chip in enumerate(chips):
            copy(1 + j, (*chip, c), me).wait_recv()
            passed[j].start()
        copy(0, sibling, me).wait_recv()
        for j, chip in enumerate(chips):
            copy(4 + j, (*chip, 1 - c), me).wait_recv()
        for cp in first + passed:
            cp.wait_send()
        mine.wait()

    return pl.pallas_call(
        body, name=name, out_shape=jax.ShapeDtypeStruct((N_DEV * m_per, n), F32),
        in_specs=[VMEM_SPEC] + ([] if after is None else [ANY_SPEC]), out_specs=VMEM_SPEC,
        scratch_shapes=[pltpu.SemaphoreType.DMA((7,)), pltpu.SemaphoreType.DMA((7,)), pltpu.SemaphoreType.DMA],
        compiler_params=pltpu.CompilerParams(vmem_limit_bytes=VMEM_LIMIT),
    )(*([block] if after is None else [block, after]))


def _hbm(a):
    return pltpu.with_memory_space_constraint(a, pltpu.HBM)


def _place_own(shard, kind, chip_idx, name):
    r, cdim = shard.shape
    tr, tc = _tile(r, 512), _tile(cdim, 1024)
    nrb, ncb = r // tr, cdim // tc

    def body(k_ref, s_ref, o_ref):
        o_ref[...] = s_ref[...].astype(BF16)

    if kind == "col":
        full, o_map = (r, N_CHIP * cdim), lambda i, j, kr: (i, kr[0] * ncb + j)
    else:
        full, o_map = (N_CHIP * r, cdim), lambda i, j, kr: (kr[0] * nrb + i, j)
    return pl.pallas_call(
        body, name=name, out_shape=jax.ShapeDtypeStruct(full, BF16),
        grid_spec=pltpu.PrefetchScalarGridSpec(
            num_scalar_prefetch=1, grid=(nrb, ncb),
            in_specs=[pl.BlockSpec((tr, tc), lambda i, j, kr: (i, j))],
            out_specs=pl.BlockSpec((tr, tc), o_map)),
        compiler_params=_params("parallel", "parallel"),
    )(chip_idx, shard)


def _weight_window(ref, kind, shard_shape, k, half):
    r, cdim = shard_shape
    hr = r // 2
    if kind == "col":
        return ref.at[pl.ds(half * hr, hr), pl.ds(pl.multiple_of(k * cdim, 128), cdim)]
    return ref.at[pl.ds(pl.multiple_of(k * r + half * hr, 2 * SUBLANE), hr), :]


def _gather_weights_start(fulls, kinds, shard_shapes, groups, after, name):
    n, ng = len(fulls), len(groups)

    def body(*refs):
        ins = refs[:n]
        sems = refs[n + 1:n + 1 + 2 * ng]
        token = refs[2 * n + 1 + 2 * ng]
        x, y, c = _place()
        chips = _other_chips(x, y)
        for g, group in enumerate(groups):
            for pos, (w, j) in enumerate(group):
                own = _weight_window(ins[w], kinds[w], shard_shapes[w], 2 * x + y, c)
                for cc in range(2):
                    pltpu.make_async_remote_copy(
                        src_ref=own, dst_ref=own,
                        send_sem=sems[2 * g].at[2 * pos + cc], recv_sem=sems[2 * g + 1].at[2 * pos + c],
                        device_id=(*chips[j], cc), device_id_type=MESH).start()
        token[...] = jnp.zeros_like(token)

    sem_shapes = []
    for group in groups:
        sem_shapes += [pltpu.SemaphoreType.DMA((2 * len(group),))] * 2
    outs = pl.pallas_call(
        body, name=name,
        out_shape=tuple(sem_shapes) + tuple(pltpu.HBM(f.shape, f.dtype) for f in fulls)
        + (jax.ShapeDtypeStruct((SUBLANE, 128), F32),),
        in_specs=[HBM_SPEC] * n + [ANY_SPEC], out_specs=(SEM_SPEC,) * (2 * ng) + (HBM_SPEC,) * n + (VMEM_SPEC,),
        input_output_aliases={w: 2 * ng + w for w in range(n)},
        compiler_params=pltpu.CompilerParams(has_side_effects=EFFECT),
    )(*[_hbm(f) for f in fulls], after)
    sems = [(outs[2 * g], outs[2 * g + 1]) for g in range(ng)]
    return sems, list(outs[2 * ng:2 * ng + n]), outs[2 * ng + n]


def _gather_weights_wait(sems, group, fulls, kinds, shard_shapes, after, name):
    n = len(fulls)

    def body(*refs):
        ins = refs[:n]
        send_sems, recv_sems = refs[n], refs[n + 1]
        x, y, c = _place()
        chips = _other_chips(x, y)
        for pos, (w, j) in enumerate(group):
            own = _weight_window(ins[w], kinds[w], shard_shapes[w], 2 * x + y, c)
            for cc in range(2):
                landed = _weight_window(ins[w], kinds[w], shard_shapes[w], 2 * chips[j][0] + chips[j][1], cc)
                k = 2 * pos + cc
                pltpu.make_async_remote_copy(
                    src_ref=own, dst_ref=own, send_sem=send_sems.at[k], recv_sem=recv_sems.at[k],
                    device_id=(*chips[j], cc), device_id_type=MESH).wait_send()
                pltpu.make_async_remote_copy(
                    src_ref=landed, dst_ref=landed, send_sem=send_sems.at[k], recv_sem=recv_sems.at[k],
                    device_id=(*chips[j], cc), device_id_type=MESH).wait_recv()

    return list(pl.pallas_call(
        body, name=name, out_shape=tuple(pltpu.HBM(f.shape, f.dtype) for f in fulls),
        in_specs=[HBM_SPEC] * n + [SEM_SPEC, SEM_SPEC, ANY_SPEC], out_specs=(HBM_SPEC,) * n,
        input_output_aliases={w: w for w in range(n)},
        compiler_params=pltpu.CompilerParams(has_side_effects=EFFECT),
    )(*fulls, sems[0], sems[1], after))


def _grad_piece(ref, kind, h, cdim, k, half):
    if kind == "col":
        return ref.at[pl.ds(half * h, h), pl.ds(pl.multiple_of(k * cdim, 128), cdim)]
    return ref.at[k, pl.ds(half * h, h), :]


def _grad_dims(g, kind):
    return (g.shape[0] // 2, g.shape[1] // N_CHIP) if kind == "col" else (g.shape[1] // 2, g.shape[2])


def _reduce_start(grads, kinds, name):
    n = len(grads)
    dims = [_grad_dims(g, kind) for g, kind in zip(grads, kinds)]
    lands = [lax.empty((N_DEV - 1, h, cdim), g.dtype) for g, (h, cdim) in zip(grads, dims)]

    def body(*refs):
        g_ins, land_ins = refs[:n], refs[n:2 * n]
        send_sems, recv_sems = refs[2 * n], refs[2 * n + 1]
        token = refs[4 * n + 2]
        x, y, c = _place()
        for w in range(n):
            h, cdim = dims[w]
            base = (N_DEV - 1) * w
            pltpu.make_async_remote_copy(
                src_ref=_grad_piece(g_ins[w], kinds[w], h, cdim, 2 * x + y, 1 - c), dst_ref=land_ins[w].at[0],
                send_sem=send_sems.at[base], recv_sem=recv_sems.at[base],
                device_id=(x, y, 1 - c), device_id_type=MESH).start()
            for j, chip in enumerate(_other_chips(x, y)):
                for cc in range(2):
                    pltpu.make_async_remote_copy(
                        src_ref=_grad_piece(g_ins[w], kinds[w], h, cdim, 2 * chip[0] + chip[1], cc),
                        dst_ref=land_ins[w].at[1 + 2 * j + c],
                        send_sem=send_sems.at[base + 1 + 2 * j + cc], recv_sem=recv_sems.at[base + 1 + 2 * j + c],
                        device_id=(*chip, cc), device_id_type=MESH).start()
        token[...] = jnp.zeros_like(token)

    sems = pltpu.SemaphoreType.DMA(((N_DEV - 1) * n,))
    outs = pl.pallas_call(
        body, name=name,
        out_shape=(sems, sems) + tuple(pltpu.HBM(a.shape, a.dtype) for a in list(grads) + lands)
        + (jax.ShapeDtypeStruct((SUBLANE, 128), F32),),
        in_specs=[HBM_SPEC] * (2 * n), out_specs=(SEM_SPEC, SEM_SPEC) + (HBM_SPEC,) * (2 * n) + (VMEM_SPEC,),
        input_output_aliases={i: 2 + i for i in range(2 * n)},
        compiler_params=pltpu.CompilerParams(has_side_effects=EFFECT),
    )(*[_hbm(a) for a in list(grads) + lands])
    return outs[0], outs[1], list(outs[2:2 + n]), list(outs[2 + n:2 + 2 * n]), outs[2 + 2 * n]


def _reduce_wait(send_sems, recv_sems, grads, lands, kinds, after, name):
    n = len(grads)
    dims = [_grad_dims(g, kind) for g, kind in zip(grads, kinds)]

    def body(*refs):
        g_ins, land_ins = refs[:n], refs[n:2 * n]
        send_sems, recv_sems = refs[2 * n], refs[2 * n + 1]
        x, y, c = _place()
        for w in range(n):
            h, cdim = dims[w]
            piece = _grad_piece(g_ins[w], kinds[w], h, cdim, 2 * x + y, c)
            for s in range(N_DEV - 1):
                k = (N_DEV - 1) * w + s
                slot = land_ins[w].at[s]
                pltpu.make_async_remote_copy(
                    src_ref=piece, dst_ref=slot, send_sem=send_sems.at[k], recv_sem=recv_sems.at[k],
                    device_id=(x, y, 1 - c), device_id_type=MESH).wait_send()
                pltpu.make_async_remote_copy(
                    src_ref=piece, dst_ref=slot, send_sem=send_sems.at[k], recv_sem=recv_sems.at[k],
                    device_id=(x, y, 1 - c), device_id_type=MESH).wait_recv()

    outs = pl.pallas_call(
        body, name=name, out_shape=tuple(pltpu.HBM(a.shape, a.dtype) for a in list(grads) + list(lands)),
        in_specs=[HBM_SPEC] * (2 * n) + [SEM_SPEC, SEM_SPEC, ANY_SPEC], out_specs=(HBM_SPEC,) * (2 * n),
        input_output_aliases={i: i for i in range(2 * n)},
        compiler_params=pltpu.CompilerParams(has_side_effects=EFFECT),
    )(*grads, *lands, send_sems, recv_sems, after)
    return list(outs[:n]), list(outs[n:])


def _add_pieces(g, land, kind, chip_idx, core_idx, name):
    _, h, cdim = land.shape
    tr, tc = _tile(h, 256), _tile(cdim, 2048)
    nrb, ncb = h // tr, cdim // tc

    def body(k_ref, c_ref, g_ref, l_ref, o_ref):
        acc = g_ref[...].astype(F32)
        for s in range(N_DEV - 1):
            acc = acc + l_ref[s].astype(F32)
        o_ref[...] = acc

    if kind == "col":
        g_spec = pl.BlockSpec((tr, tc), lambda i, j, kr, cr: (cr[0] * nrb + i, kr[0] * ncb + j))
    else:
        g_spec = pl.BlockSpec((None, tr, tc), lambda i, j, kr, cr: (kr[0], cr[0] * nrb + i, j))
    return pl.pallas_call(
        body, name=name, out_shape=jax.ShapeDtypeStruct((2 * h, cdim), F32),
        grid_spec=pltpu.PrefetchScalarGridSpec(
            num_scalar_prefetch=2, grid=(nrb, ncb),
            in_specs=[g_spec, pl.BlockSpec((N_DEV - 1, tr, tc), lambda i, j, kr, cr: (0, i, j))],
            out_specs=pl.BlockSpec((tr, tc), lambda i, j, kr, cr: (cr[0] * nrb + i, j))),
        compiler_params=_params("parallel", "parallel"),
    )(chip_idx, core_idx, g, land)


def _join_halves(shards, name):
    n = len(shards)

    def body(*refs):
        ins, outs = refs[:n], refs[n:2 * n]
        send_sems, recv_sems = refs[2 * n:]
        x, y, c = _place()
        copies = []
        for w in range(n):
            h = shards[w].shape[0] // 2
            cp = pltpu.make_async_remote_copy(
                src_ref=ins[w].at[pl.ds(c * h, h), :], dst_ref=outs[w].at[pl.ds(c * h, h), :],
                send_sem=send_sems.at[w], recv_sem=recv_sems.at[w], device_id=(x, y, 1 - c), device_id_type=MESH)
            cp.start()
            copies.append(cp)
        for w in range(n):
            h = shards[w].shape[0] // 2
            theirs = outs[w].at[pl.ds((1 - c) * h, h), :]
            pltpu.make_async_remote_copy(
                src_ref=theirs, dst_ref=theirs, send_sem=send_sems.at[w], recv_sem=recv_sems.at[w],
                device_id=(x, y, 1 - c), device_id_type=MESH).wait_recv()
        for cp in copies:
            cp.wait_send()

    return pl.pallas_call(
        body, name=name, out_shape=tuple(jax.ShapeDtypeStruct(s.shape, s.dtype) for s in shards),
        in_specs=[HBM_SPEC] * n, out_specs=tuple([HBM_SPEC] * n),
        input_output_aliases={w: w for w in range(n)},
        scratch_shapes=[pltpu.SemaphoreType.DMA((n,))] * 2,
    )(*shards)


def _sum_devices(gathered, rows, name):
    n = gathered.shape[1]

    def body(g_ref, o_ref):
        acc = g_ref[0:rows, :]
        for dev in range(1, N_DEV):
            acc = acc + g_ref[dev * rows:(dev + 1) * rows, :]
        o_ref[...] = acc

    return pl.pallas_call(
        body, name=name, out_shape=jax.ShapeDtypeStruct((rows, n), F32),
        in_specs=[VMEM_SPEC], out_specs=VMEM_SPEC,
        compiler_params=pltpu.CompilerParams(vmem_limit_bytes=VMEM_LIMIT),
    )(gathered)


def _pack(vectors, width):
    flat = [v.reshape(-1) for v in vectors]
    offsets, total = [], 0
    for f in flat:
        offsets.append(total)
        total += f.shape[0]
    rows = -(-total // (width * SUBLANE)) * SUBLANE
    flat.append(jnp.zeros((rows * width - total,), F32))
    return jnp.concatenate(flat).reshape(rows, width), offsets


def kernel(x, c, mod_w, mod_b, norm_g, a_w_in, a_conv_w, a_conv_b, a_w_out, b_w_in, b_ln_g, b_ln_b, b_w_s, b_b_s, b_w_out, final_g, loss_target, m_mod_w, m_mod_b, m_norm_g, m_a_w_in, m_a_conv_w, m_a_conv_b, m_a_w_out, m_b_w_in, m_b_ln_g, m_b_ln_b, m_b_w_s, m_b_b_s, m_b_w_out, m_final_g, v_mod_w, v_mod_b, v_norm_g, v_a_w_in, v_a_conv_w, v_a_conv_b, v_a_w_out, v_b_w_in, v_b_ln_g, v_b_ln_b, v_b_w_s, v_b_b_s, v_b_w_out, v_final_g):
    seq, d = x.shape[1], x.shape[2]
    e = a_conv_b.shape[1]
    wd = mod_w.shape[2]
    ax, ay, ac = _place()
    chip = 2 * ax + ay
    dev = 2 * chip + ac
    chip_idx = jnp.reshape(chip, (1,)).astype(jnp.int32)
    core_idx = jnp.reshape(ac, (1,)).astype(jnp.int32)

    x2d = x[0]
    target = loss_target[0]

    w_names = ["a_w_in", "a_w_out", "b_w_in", "b_w_out"]
    w_kinds = ["col", "row", "col", "row"]
    w_shards = [a_w_in[0], a_w_out[0], b_w_in[0], b_w_out[0]]
    w_shapes = [sh.shape for sh in w_shards]
    placed = [_place_own(sh, kind, chip_idx, "place_" + nm) for nm, sh, kind in zip(w_names, w_shards, w_kinds)]
    a_groups, far_groups = [[(0, 0), (0, 1)]], [[(0, 2)]]
    whole_group = [[(0, 0), (0, 1), (0, 2)]]

    es = e // N_CHIP
    packed0, offs0 = _pack([c, a_conv_w, b_ln_g, b_ln_b], 1024)
    gathered0 = _all_gather(packed0, "gather_params").reshape(N_DEV, -1)
    c_all = gathered0[:, :d]
    per_chip = gathered0[0::2]

    def from_chips_cols(k, rows_):
        got = per_chip[:, offs0[k]:offs0[k] + rows_ * es].reshape(N_CHIP, rows_, es)
        return jnp.transpose(got, (1, 0, 2)).reshape(rows_, e)

    conv_w_full = from_chips_cols(1, 3)
    conv_w = [conv_w_full[k:k + 1] for k in range(3)]
    ln_g, ln_b = from_chips_cols(2, 1), from_chips_cols(3, 1)
    mod_b_cols = lax.dynamic_slice_in_dim(mod_b, chip * wd, wd, axis=1)[:, None, :]
    c_act, mod_part = _mod_fwd(c_all, mod_w, mod_b_cols, "mod_fwd")
    n_layer = mod_w.shape[0]
    mod_gathered = _all_gather(mod_part, "gather_mod")
    a_sems, (wa_in,), a_token = _gather_weights_start(
        placed[:1], w_kinds[:1], w_shapes[:1], a_groups, mod_gathered, "gather_a_w_in_start")
    mod_all = mod_gathered.reshape(N_CHIP, 2, n_layer, N_DEV, wd)[:, 0]
    mod_all = jnp.transpose(mod_all, (1, 2, 0, 3)).reshape(n_layer, N_DEV, N_CHIP * wd)
    mod_me = lax.dynamic_index_in_dim(mod_all, dev, axis=1, keepdims=False)
    shift = [mod_me[l:l + 1, 0:d] for l in range(n_layer)]
    scale = [mod_me[l:l + 1, d:2 * d] for l in range(n_layer)]
    gate = [mod_me[l:l + 1, 2 * d:3 * d] for l in range(n_layer)]

    g0, g1, gf = norm_g[0:1], norm_g[1:2], final_g[None, :]
    h0 = _norm_mod(x2d, g0, scale[0], shift[0], "norm_mod0")

    def slab(r):
        return jnp.bitwise_xor(chip_idx, r)

    def arrived(sems, group, weight, w, after, name):
        return _gather_weights_wait(sems, [(0, j) for _, j in group], [weight], [w_kinds[w]], [w_shapes[w]],
                                    after, name)[0]

    proj0 = _mm_proj_slab(h0, wa_in, None, slab(0), 4, "a_proj_own", a_token)
    wa_in = arrived(a_sems[0], a_groups[0], wa_in, 0, proj0, "gather_wait_near")
    far_sems, (wa_in,), far_token = _gather_weights_start(
        [wa_in], w_kinds[:1], w_shapes[:1], far_groups, a_token, "gather_a_w_in_far_start")
    proj0 = _mm_proj_slab(h0, wa_in, proj0, slab(2), 4, "a_proj_x", far_token)
    proj0 = _mm_proj_slab(h0, wa_in, proj0, slab(1), 4, "a_proj_y", far_token)
    wa_in = arrived(far_sems[0], far_groups[0], wa_in, 0, proj0, "gather_wait_far")
    def start_whole(w, after, name):
        sems, (flight,), token = _gather_weights_start(
            placed[w:w + 1], w_kinds[w:w + 1], w_shapes[w:w + 1], whole_group, after, name)
        return sems[0], flight, token

    ao_sems, wa_out, ao_token = start_whole(1, wa_in, "gather_a_w_out_start")
    proj0 = _mm_proj_slab(h0, wa_in, proj0, slab(3), 4, "a_proj_far", ao_token)
    wa_out = arrived(ao_sems, whole_group[0], wa_out, 1, proj0, "gather_wait_a_w_out")
    bi_sems, wb_in, bi_token = start_whole(2, wa_out, "gather_b_w_in_start")
    y0, br0, x1, h1 = _conv_fwd(proj0, conv_w, a_conv_b, wa_out, x2d, gate[0], g1, scale[1], shift[1],
                                "conv_fwd", bi_token)
    wb_in = arrived(bi_sems, whole_group[0], wb_in, 2, h1, "gather_wait_b_w_in")
    bo_sems, wb_out, bo_token = start_whole(3, wb_in, "gather_b_w_out_start")
    proj1 = _mm_proj(h1, wb_in, 3, "b_proj", bo_token)
    b_s_t = jnp.transpose(b_b_s[0])
    wb_out = arrived(bo_sems, whole_group[0], wb_out, 3, proj1, "gather_wait_b_w_out")
    y1, dx2, dbr1, loss_part, g_final_g, dgate1 = _gmlp_fwd_head(
        proj1, ln_g, ln_b, b_w_s[0], b_s_t, wb_out, x1, gate[1], gf, target, "gmlp_fwd_head")

    gw_b_out = _mm_dw_out(y1, dbr1, "b_out_dw")
    dproj1, g_w_s, g_b_s_t, g_ln_g, g_ln_b = _gmlp_bwd(
        proj1, dbr1, wb_out, ln_g, ln_b, b_w_s[0], jnp.swapaxes(b_w_s[0], 1, 2), b_s_t, "gmlp_bwd")
    gw_b_in = _mm_dw_in(h1, dproj1, "b_proj_dw")
    b_kinds = ["col", "row"]
    b_send, b_recv, b_grads, b_lands, b_token = _reduce_start(
        [gw_b_in, gw_b_out.reshape(N_CHIP, e // N_CHIP, d)], b_kinds, "reduce_b_start")
    dx1, dshift1, dscale1, g_g1, dbr0, dgate0 = _mm_dh_norm_bwd(
        dproj1, wb_in, x1, dx2, g1, scale[1], "b_proj_dx", b_token, br=br0, gate=gate[0])

    gw_a_out = _mm_dw_out(y0, dbr0, "a_out_dw")
    ao_send, ao_recv, ao_grads, ao_lands, _ = _reduce_start(
        [gw_a_out.reshape(N_CHIP, e // N_CHIP, d)], ["row"], "reduce_a_out_start")
    dproj0, g_w0, g_w1, g_w2, g_conv_b = _conv_bwd(proj0, dbr0, wa_out, conv_w, a_conv_b, "conv_bwd")
    gw_a_in = _mm_dw_in(h0, dproj0, "a_proj_dw")
    ai_send, ai_recv, ai_grads, ai_lands, ai_token = _reduce_start([gw_a_in], ["col"], "reduce_a_in_start")
    grad_x, dshift0, dscale0, g_g0 = _mm_dh_norm_bwd(dproj0, wa_in, x2d, dx1, g0, scale[0], "a_proj_dx", ai_token)

    def finish(send, recv, grads_, lands_, kinds_, names_, after, tag):
        grads_, lands_ = _reduce_wait(send, recv, grads_, lands_, kinds_, after, "reduce_" + tag + "_wait")
        halves = [_add_pieces(g, land, kind, chip_idx, core_idx, "add_pieces_" + nm)
                  for g, land, kind, nm in zip(grads_, lands_, kinds_, names_)]
        return _join_halves(halves, "join_" + tag)

    upd, big_grads = {}, {}

    def adamw_big(nm, w, g, m, v):
        g_out, *rest = _adamw_2d(w[0], g, m[0], v[0], "adamw_" + nm)
        big_grads[nm] = g_out[None]
        upd[nm] = tuple(o[None] for o in rest)

    g_b_w_in, g_b_w_out = finish(b_send, b_recv, b_grads, b_lands, b_kinds, ["b_w_in", "b_w_out"], grad_x, "b")
    adamw_big("b_w_in", b_w_in, g_b_w_in, m_b_w_in, v_b_w_in)
    adamw_big("b_w_out", b_w_out, g_b_w_out, m_b_w_out, v_b_w_out)

    dmod = jnp.concatenate([dshift0, dscale0, dgate0, dshift1, dscale1, dgate1], axis=1)
    small = [loss_part[0, 0:1], g_final_g, g_g0, g_g1, jnp.concatenate([g_w0, g_w1, g_w2], axis=0), g_conv_b, g_ln_g, g_ln_b,
             jnp.transpose(g_b_s_t), g_w_s, dmod]
    packed, offs = _pack(small, 1024)
    rows = packed.shape[0]
    gathered = _all_gather(packed, "gather_small", after=upd["b_w_out"][0])
    summed = _sum_devices(gathered, rows, "sum_small").reshape(-1)

    def take(k, shape):
        size = math.prod(shape)
        return summed[offs[k]:offs[k] + size].reshape(shape)

    loss = take(0, ())
    grad_final_g = take(1, (d,))
    grad_norm_g = jnp.concatenate([take(2, (1, d)), take(3, (1, d))], axis=0)
    grad_conv_w_full = take(4, (3, e))
    grad_a_conv_b = take(5, (1, e))
    grad_ln_g_full = take(6, (1, e))
    grad_ln_b_full = take(7, (1, e))
    grad_b_b_s = take(8, (1, GROUPS, CHUNK))
    grad_b_w_s = take(9, (1, GROUPS, CHUNK, CHUNK))
    grad_mod_b = take(10, (n_layer, 3 * d))
    grad_a_conv_w = lax.dynamic_slice_in_dim(grad_conv_w_full, chip * es, es, axis=1)[None]
    grad_b_ln_g = lax.dynamic_slice_in_dim(grad_ln_g_full, chip * es, es, axis=1)
    grad_b_ln_b = lax.dynamic_slice_in_dim(grad_ln_b_full, chip * es, es, axis=1)
    dmod_all = gathered.reshape(N_DEV, rows * 1024)[:, offs[10]:offs[10] + n_layer * 3 * d]
    dmod_all = dmod_all.reshape(N_DEV, n_layer, 3 * d)
    dmod_cols = jnp.transpose(lax.dynamic_slice_in_dim(dmod_all, chip * wd, wd, axis=2), (1, 0, 2))

    grad_mod_w, delta_mod_w, new_m_mod_w, new_v_mod_w = _mod_w_update(
        jnp.transpose(c_act), dmod_cols, mod_w, m_mod_w, v_mod_w, "mod_w_update")
    (g_a_w_out,) = finish(ao_send, ao_recv, ao_grads, ao_lands, ["row"], ["a_w_out"], delta_mod_w, "a_out")
    adamw_big("a_w_out", a_w_out, g_a_w_out, m_a_w_out, v_a_w_out)
    (g_a_w_in,) = finish(ai_send, ai_recv, ai_grads, ai_lands, ["col"], ["a_w_in"], upd["a_w_out"][0], "a_in")
    adamw_big("a_w_in", a_w_in, g_a_w_in, m_a_w_in, v_a_w_in)
    small_w = [("mod_b", mod_b, grad_mod_b, m_mod_b, v_mod_b), ("norm_g", norm_g, grad_norm_g, m_norm_g, v_norm_g),
               ("a_conv_w", a_conv_w, grad_a_conv_w, m_a_conv_w, v_a_conv_w),
               ("a_conv_b", a_conv_b, grad_a_conv_b, m_a_conv_b, v_a_conv_b),
               ("b_ln_g", b_ln_g, grad_b_ln_g, m_b_ln_g, v_b_ln_g), ("b_ln_b", b_ln_b, grad_b_ln_b, m_b_ln_b, v_b_ln_b),
               ("b_w_s", b_w_s, grad_b_w_s, m_b_w_s, v_b_w_s), ("b_b_s", b_b_s, grad_b_b_s, m_b_b_s, v_b_b_s),
               ("final_g", final_g, grad_final_g, m_final_g, v_final_g)]

    def flat2d(a):
        return a.reshape(-1, a.shape[-1])

    res = _adamw_small([flat2d(t[1]) for t in small_w], [flat2d(t[2]) for t in small_w],
                       [flat2d(t[3]) for t in small_w], [flat2d(t[4]) for t in small_w], "adamw_small")
    for (nm, w, _, _, _), r3 in zip(small_w, res):
        upd[nm] = tuple(o.reshape(w.shape) for o in r3)
    upd["mod_w"] = (delta_mod_w, new_m_mod_w, new_v_mod_w)

    grads = {"mod_w": grad_mod_w, "mod_b": grad_mod_b, "norm_g": grad_norm_g, "a_conv_w": grad_a_conv_w,
             "a_conv_b": grad_a_conv_b, "b_ln_g": grad_b_ln_g, "b_ln_b": grad_b_ln_b, "b_w_s": grad_b_w_s,
             "b_b_s": grad_b_b_s, "final_g": grad_final_g, **big_grads}
    order = ["mod_w", "mod_b", "norm_g", "a_w_in", "a_conv_w", "a_conv_b", "a_w_out", "b_w_in", "b_ln_g", "b_ln_b",
             "b_w_s", "b_b_s", "b_w_out", "final_g"]
    return (loss, grad_x[None], *[grads[k] for k in order], *[upd[k][0] for k in order],
            *[upd[k][1] for k in order], *[upd[k][2] for k in order])
```

```python
import functools
import math

import jax
import jax.numpy as jnp
from jax import lax
from jax.experimental import pallas as pl
from jax.experimental.pallas import tpu as pltpu

F32 = jnp.float32
BF16 = jnp.bfloat16
MESH = pl.DeviceIdType.MESH

N_DEV = 8
N_CHIP = 4
SUBLANE = 8
PACK = 16
ACT = F32
RMS_EPS = 1e-6
LN_EPS = 1e-5
CHUNK = 128
GROUPS = 8
ADAM_LR = 0.001
ADAM_B1 = 0.9
ADAM_B2 = 0.999
ADAM_EPS = 1e-08
ADAM_WD = 0.01
ADAM_STEP = 10
VMEM_LIMIT = 56 << 20

HBM_SPEC = pl.BlockSpec(memory_space=pltpu.HBM)
VMEM_SPEC = pl.BlockSpec(memory_space=pltpu.VMEM)
SEM_SPEC = pl.BlockSpec(memory_space=pltpu.SEMAPHORE)
ANY_SPEC = pl.BlockSpec(memory_space=pl.ANY)
EFFECT = pltpu.SideEffectType.DATAFLOW_SIDE_EFFECTING


def _params(*sem):
    return pltpu.CompilerParams(dimension_semantics=sem, vmem_limit_bytes=VMEM_LIMIT)


def _tile(n, want):
    if n <= want:
        return n
    t = want
    while n % t:
        t -= 128
    return t


def _sigmoid(x):
    return 0.5 * jnp.tanh(0.5 * x) + 0.5


def _silu_and_grad(x):
    s = _sigmoid(x)
    return x * s, s * (1.0 + x * (1.0 - s))


def _gelu_and_grad(x):
    cdf = 0.5 * (1.0 + lax.erf(x * (1.0 / math.sqrt(2.0))))
    pdf = jnp.exp(-0.5 * x * x) * (1.0 / math.sqrt(2.0 * math.pi))
    return x * cdf, cdf + x * pdf


def _gelu(x):
    return x * (0.5 * (1.0 + lax.erf(x * (1.0 / math.sqrt(2.0)))))


def _rms(x):
    r = lax.rsqrt(jnp.mean(x * x, axis=-1, keepdims=True) + RMS_EPS)
    return x * r, r


def _rms_bwd(dxn, xn, r):
    return r * (dxn - xn * jnp.mean(dxn * xn, axis=-1, keepdims=True))


def _colsum(a):
    return jnp.sum(a, axis=0, keepdims=True)


def _shift_down(cur, before, k):
    rolled = pltpu.roll(cur, k, 0)
    row = lax.broadcasted_iota(jnp.int32, before.shape, 0)
    head = jnp.where(row < k, pltpu.roll(before, k, 0), rolled[:SUBLANE])
    return jnp.concatenate([head, rolled[SUBLANE:]], axis=0)


def _shift_up(cur, after, k):
    n = cur.shape[0]
    rolled = pltpu.roll(cur, n - k, 0)
    row = lax.broadcasted_iota(jnp.int32, after.shape, 0)
    tail = jnp.where(row >= SUBLANE - k, pltpu.roll(after, SUBLANE - k, 0), rolled[n - SUBLANE:])
    return jnp.concatenate([rolled[:n - SUBLANE], tail], axis=0)


def _after_spec():
    return pl.BlockSpec((SUBLANE, 128), lambda *_: (0, 0))


def _mm_proj(h, w, n_split, name, after):
    s, d = h.shape
    e = w.shape[1] // n_split
    tm, tn = _tile(s, 1024), _tile(e, 2048)
    nj = e // tn

    def body(h_ref, w_ref, after_ref, o_ref):
        o_ref[...] = jnp.dot(h_ref[...], w_ref[...], preferred_element_type=F32).astype(ACT)

    return pl.pallas_call(
        body, name=name,
        out_shape=jax.ShapeDtypeStruct((n_split, s, e), ACT),
        grid=(s // tm, n_split * nj),
        in_specs=[pl.BlockSpec((tm, d), lambda i, j: (i, 0)), pl.BlockSpec((d, tn), lambda i, j: (0, j)),
                  _after_spec()],
        out_specs=pl.BlockSpec((None, tm, tn), lambda i, j: (j // nj, i, j % nj)),
        compiler_params=_params("parallel", "parallel"),
    )(h, w, after)


def _mm_proj_slab(h, w, proj, q_idx, n_split, name, after):
    s, d = h.shape
    e = w.shape[1] // n_split
    tm, tn = _tile(s, 1024), _tile(e, 2048)
    nj = e // tn

    def body(q_ref, h_ref, w_ref, *rest):
        o_ref = rest[-1]
        o_ref[...] = jnp.dot(h_ref[...], w_ref[...], preferred_element_type=F32).astype(ACT)

    in_specs = [pl.BlockSpec((tm, d), lambda i, j, qr: (i, 0)), pl.BlockSpec((d, tn), lambda i, j, qr: (0, qr[0] * nj + j)),
                _after_spec()]
    args = [q_idx, h, w, after]
    aliases = {}
    if proj is not None:
        in_specs.append(ANY_SPEC)
        args.append(proj)
        aliases = {4: 0}
    return pl.pallas_call(
        body, name=name,
        out_shape=jax.ShapeDtypeStruct((n_split, s, e), ACT),
        grid_spec=pltpu.PrefetchScalarGridSpec(
            num_scalar_prefetch=1, grid=(s // tm, nj), in_specs=in_specs,
            out_specs=pl.BlockSpec((None, tm, tn), lambda i, j, qr: (qr[0], i, j))),
        input_output_aliases=aliases,
        compiler_params=_params("parallel", "parallel"),
    )(*args)


def _mm_dh_norm_bwd(dp, w, x, dx_in, g, scale, name, after, br=None, gate=None):
    nq, s, e = dp.shape
    d = w.shape[0]
    has_branch = br is not None
    tm, tk = _tile(s, 1024), _tile(e, 1024 if has_branch else 2048)
    nkq = e // tk
    nk = nq * nkq

    def body(*refs):
        if has_branch:
            (a_ref, b_ref, after_ref, x_ref, dxin_ref, g_ref, sc_ref, br_ref, gate_ref,
             dx_ref, dsh_ref, dsc_ref, dg_ref, dbr_ref, dgate_ref, acc_ref) = refs
        else:
            (a_ref, b_ref, after_ref, x_ref, dxin_ref, g_ref, sc_ref,
             dx_ref, dsh_ref, dsc_ref, dg_ref, acc_ref) = refs
        i, k = pl.program_id(0), pl.program_id(1)

        @pl.when(jnp.logical_and(i == 0, k == 0))
        def _():
            dsh_ref[...] = jnp.zeros_like(dsh_ref)
            dsc_ref[...] = jnp.zeros_like(dsc_ref)
            dg_ref[...] = jnp.zeros_like(dg_ref)
            if has_branch:
                dgate_ref[...] = jnp.zeros_like(dgate_ref)

        @pl.when(k == 0)
        def _():
            acc_ref[...] = jnp.zeros_like(acc_ref)
        acc_ref[...] += lax.dot_general(a_ref[...], b_ref[...], (((1,), (1,)), ((), ())), preferred_element_type=F32)

        @pl.when(k == nk - 1)
        def _():
            dh = acc_ref[...]
            g_vec = g_ref[...]
            xn, r = _rms(x_ref[...])
            dsh_ref[...] += _colsum(dh)
            dsc_ref[...] += _colsum(dh * (xn * g_vec))
            da = dh * (1.0 + sc_ref[...])
            dg_ref[...] += _colsum(da * xn)
            dx = dxin_ref[...] + _rms_bwd(da * g_vec, xn, r)
            dx_ref[...] = dx
            if has_branch:
                dgate_ref[...] += _colsum(dx * br_ref[...])
                dbr_ref[...] = (gate_ref[...] * dx).astype(BF16)

    rows = pl.BlockSpec((tm, d), lambda i, k: (i, 0))
    vec = pl.BlockSpec((1, d), lambda i, k: (0, 0))
    vec_out = jax.ShapeDtypeStruct((1, d), F32)
    in_specs = [pl.BlockSpec((None, tm, tk), lambda i, k: (k // nkq, i, k % nkq)),
                pl.BlockSpec((d, tk), lambda i, k: (0, k)), _after_spec(), rows, rows, vec, vec]
    out_shape = [jax.ShapeDtypeStruct((s, d), F32), vec_out, vec_out, vec_out]
    out_specs = [rows, vec, vec, vec]
    args = [dp, w, after, x, dx_in, g, scale]
    if has_branch:
        in_specs += [rows, vec]
        out_shape += [jax.ShapeDtypeStruct((s, d), BF16), vec_out]
        out_specs += [rows, vec]
        args += [br, gate]
    return pl.pallas_call(
        body, name=name, out_shape=tuple(out_shape), grid=(s // tm, nk),
        in_specs=in_specs, out_specs=tuple(out_specs), scratch_shapes=[pltpu.VMEM((tm, d), F32)],
        compiler_params=_params("arbitrary", "arbitrary"),
    )(*args)


def _dw_body(n_t):
    def body(a_ref, b_ref, o_ref, acc_ref):
        t = pl.program_id(2)

        @pl.when(t == 0)
        def _():
            acc_ref[...] = jnp.zeros_like(acc_ref)
        acc_ref[...] += lax.dot_general(a_ref[...], b_ref[...], (((0,), (0,)), ((), ())), preferred_element_type=F32)

        @pl.when(t == n_t - 1)
        def _():
            o_ref[...] = acc_ref[...].astype(o_ref.dtype)
    return body


def _mm_dw_in(h, dp, name):
    s, d = h.shape
    nq, _, e = dp.shape
    tm, tn, tt = _tile(d, 1024), _tile(e, 1024), _tile(s, 2048)
    nj = e // tn

    return pl.pallas_call(
        _dw_body(s // tt), name=name,
        out_shape=jax.ShapeDtypeStruct((d, nq * e), BF16),
        grid=(d // tm, nq * nj, s // tt),
        in_specs=[pl.BlockSpec((tt, tm), lambda i, j, t: (t, i)),
                  pl.BlockSpec((None, tt, tn), lambda i, j, t: (j // nj, t, j % nj))],
        out_specs=pl.BlockSpec((tm, tn), lambda i, j, t: (i, j)),
        scratch_shapes=[pltpu.VMEM((tm, tn), F32)],
        compiler_params=_params("parallel", "parallel", "arbitrary"),
    )(h, dp)


def _mm_dw_out(y, dbr, name):
    s, e = y.shape
    d = dbr.shape[1]
    tm, tn, tt = _tile(e, 1024), _tile(d, 1024), _tile(s, 2048)

    return pl.pallas_call(
        _dw_body(s // tt), name=name,
        out_shape=jax.ShapeDtypeStruct((e, d), BF16),
        grid=(e // tm, d // tn, s // tt),
        in_specs=[pl.BlockSpec((tt, tm), lambda i, j, t: (t, i)), pl.BlockSpec((tt, tn), lambda i, j, t: (t, j))],
        out_specs=pl.BlockSpec((tm, tn), lambda i, j, t: (i, j)),
        scratch_shapes=[pltpu.VMEM((tm, tn), F32)],
        compiler_params=_params("parallel", "parallel", "arbitrary"),
    )(y, dbr)


def _row_spec(ts, d):
    return pl.BlockSpec((ts, d), lambda i: (i, 0))


def _vec_spec(d):
    return pl.BlockSpec((1, d), lambda i: (0, 0))


def _norm_mod(x, g, scale, shift, name):
    s, d = x.shape
    ts = _tile(s, 512)

    def body(x_ref, g_ref, sc_ref, sh_ref, h_ref):
        xn, _ = _rms(x_ref[...])
        h_ref[...] = ((xn * g_ref[...]) * (1.0 + sc_ref[...]) + sh_ref[...]).astype(BF16)

    return pl.pallas_call(
        body, name=name, out_shape=jax.ShapeDtypeStruct((s, d), BF16), grid=(s // ts,),
        in_specs=[_row_spec(ts, d), _vec_spec(d), _vec_spec(d), _vec_spec(d)],
        out_specs=_row_spec(ts, d), compiler_params=_params("parallel"),
    )(x, g, scale, shift)


def _head_tile(x1, br, gate, gf, target):
    d = x1.shape[-1]
    xn, r = _rms(x1 + gate * br)
    err = xn * gf - target
    loss = 0.5 * jnp.sum(jnp.mean(err * err, axis=-1, keepdims=True))
    dout = err * (1.0 / d)
    dx = _rms_bwd(dout * gf, xn, r)
    return dx, gate * dx, loss, _colsum(dout * xn), _colsum(dx * br)


def _conv_tiles(s, e):
    return _tile(s, 512), _tile(e, 512)


def _f32(ref):
    return ref[...].astype(F32)


def _rows_before(halo_ref):
    return _f32(halo_ref)[PACK - SUBLANE:]


def _rows_after(halo_ref):
    return _f32(halo_ref)[:SUBLANE]


def _conv_fwd(proj, conv_w, conv_b, w_out, x, gate, g_next, scale_next, shift_next, name, after):
    _, s, e = proj.shape
    d = w_out.shape[1]
    ts, te = _tile(s, 256), e
    hb = ts // PACK

    def body(bg_ref, cg_ref, xi_ref, z_ref, cgp_ref, xip_ref, w0_ref, w1_ref, w2_ref, b_ref, wo_ref, after_ref,
             x_ref, gate_ref, g_ref, sc_ref, sh_ref, y_ref, br_ref, x1_ref, h_ref):
        cx = _f32(cg_ref) * _f32(xi_ref)
        before = jnp.where(pl.program_id(0) > 0, _rows_before(cgp_ref) * _rows_before(xip_ref), 0.0)
        conv = b_ref[...] + w2_ref[...] * cx
        conv = conv + w0_ref[...] * _shift_down(cx, before, 2)
        conv = conv + w1_ref[...] * _shift_down(cx, before, 1)
        z = _f32(z_ref)
        y = ((z * _sigmoid(z)) * _f32(bg_ref) * conv).astype(BF16)
        y_ref[...] = y
        br = jnp.dot(y, wo_ref[...], preferred_element_type=F32)
        br_ref[...] = br
        x1 = x_ref[...] + gate_ref[...] * br
        x1_ref[...] = x1
        xn, _ = _rms(x1)
        h_ref[...] = ((xn * g_ref[...]) * (1.0 + sc_ref[...]) + sh_ref[...]).astype(BF16)

    def part(q):
        return pl.BlockSpec((None, ts, te), lambda i: (q, i, 0))

    def halo_before(q):
        return pl.BlockSpec((None, PACK, te), lambda i: (q, jnp.maximum(i * hb - 1, 0), 0))

    return pl.pallas_call(
        body, name=name,
        out_shape=(jax.ShapeDtypeStruct((s, e), BF16), jax.ShapeDtypeStruct((s, d), F32),
                   jax.ShapeDtypeStruct((s, d), F32), jax.ShapeDtypeStruct((s, d), BF16)), grid=(s // ts,),
        in_specs=[part(0), part(1), part(2), part(3), halo_before(1), halo_before(2)]
        + [_vec_spec(e)] * 4 + [pl.BlockSpec((e, d), lambda i: (0, 0)), _after_spec(), _row_spec(ts, d)]
        + [_vec_spec(d)] * 4,
        out_specs=(_row_spec(ts, e), _row_spec(ts, d), _row_spec(ts, d), _row_spec(ts, d)),
        compiler_params=_params("parallel"),
    )(proj, proj, proj, proj, proj, proj, *conv_w, conv_b, w_out, after, x, gate, g_next, scale_next, shift_next)


def _conv_bwd(proj, dbr, w_out, conv_w, conv_b, name, after):
    _, s, e = proj.shape
    d = dbr.shape[1]
    ts, te = _conv_tiles(s, e)
    hb = ts // PACK
    n_i = s // ts
    last_halo = s // PACK - 1
    nt = (((1,), (1,)), ((), ()))

    def body(bg_ref, cg_ref, xi_ref, z_ref, dbr_ref, cgp_ref, xip_ref, bgn_ref, zn_ref, dbrn_ref, wo_ref,
             w0_ref, w1_ref, w2_ref, b_ref, after_ref, dp_ref, dw0_ref, dw1_ref, dw2_ref, db_ref):
        i = pl.program_id(1)

        @pl.when(i == 0)
        def _():
            for acc in (dw0_ref, dw1_ref, dw2_ref, db_ref):
                acc[...] = jnp.zeros_like(acc)
        wo = wo_ref[...]
        dy = lax.dot_general(dbr_ref[...], wo, nt, preferred_element_type=F32)
        dyn = lax.dot_general(dbrn_ref[...], wo, nt, preferred_element_type=F32)[:SUBLANE]
        bg, cg, xi, z = _f32(bg_ref), _f32(cg_ref), _f32(xi_ref), _f32(z_ref)
        w0, w1, w2 = w0_ref[...], w1_ref[...], w2_ref[...]
        cx = cg * xi
        before = jnp.where(i > 0, _rows_before(cgp_ref) * _rows_before(xip_ref), 0.0)
        cx1 = _shift_down(cx, before, 1)
        cx2 = _shift_down(cx, before, 2)
        conv = b_ref[...] + w2 * cx
        conv = conv + w0 * cx2
        conv = conv + w1 * cx1
        sz, dsz = _silu_and_grad(z)
        dp_ref[3] = (dy * bg * conv * dsz).astype(BF16)
        dp_ref[0] = (dy * sz * conv).astype(BF16)
        dconv = dy * sz * bg
        zn = _rows_after(zn_ref)
        after = jnp.where(i < n_i - 1, dyn * (zn * _sigmoid(zn)) * _rows_after(bgn_ref), 0.0)
        db_ref[...] += _colsum(dconv)
        dw2_ref[...] += _colsum(dconv * cx)
        dw1_ref[...] += _colsum(dconv * cx1)
        dw0_ref[...] += _colsum(dconv * cx2)
        dcx = w2 * dconv + w1 * _shift_up(dconv, after, 1) + w0 * _shift_up(dconv, after, 2)
        dp_ref[1] = (dcx * xi).astype(BF16)
        dp_ref[2] = (dcx * cg).astype(BF16)

    def part(q):
        return pl.BlockSpec((None, ts, te), lambda j, i: (q, i, j))

    def halo_before(q):
        return pl.BlockSpec((None, PACK, te), lambda j, i: (q, jnp.maximum(i * hb - 1, 0), j))

    def halo_after(q):
        return pl.BlockSpec((None, PACK, te), lambda j, i: (q, jnp.minimum((i + 1) * hb, last_halo), j))

    return pl.pallas_call(
        body, name=name,
        out_shape=(jax.ShapeDtypeStruct((4, s, e), BF16),) + (jax.ShapeDtypeStruct((1, e), F32),) * 4,
        grid=(e // te, n_i),
        in_specs=[part(0), part(1), part(2), part(3), pl.BlockSpec((ts, d), lambda j, i: (i, 0)),
                  halo_before(1), halo_before(2), halo_after(0), halo_after(3),
                  pl.BlockSpec((PACK, d), lambda j, i: (jnp.minimum((i + 1) * hb, last_halo), 0)),
                  pl.BlockSpec((te, d), lambda j, i: (j, 0))]
        + [pl.BlockSpec((1, te), lambda j, i: (0, j))] * 4 + [_after_spec()],
        out_specs=(pl.BlockSpec((4, ts, te), lambda j, i: (0, i, j)),) + (pl.BlockSpec((1, te), lambda j, i: (0, j)),) * 4,
        compiler_params=_params("parallel", "arbitrary"),
    )(proj, proj, proj, proj, dbr, proj, proj, proj, proj, dbr, w_out, *conv_w, conv_b, after)


def _tril(w):
    row = lax.broadcasted_iota(jnp.int32, w.shape, 0)
    col = lax.broadcasted_iota(jnp.int32, w.shape, 1)
    return jnp.where(row >= col, w, 0.0)


def _triu(w):
    row = lax.broadcasted_iota(jnp.int32, w.shape, 0)
    col = lax.broadcasted_iota(jnp.int32, w.shape, 1)
    return jnp.where(row <= col, w, 0.0)


def _layer_norm_fwd(v, g, b):
    mu = jnp.mean(v, axis=-1, keepdims=True)
    vc = v - mu
    rstd = lax.rsqrt(jnp.mean(vc * vc, axis=-1, keepdims=True) + LN_EPS)
    vhat = vc * rstd
    return vhat * g + b, vhat, rstd


GMLP_CHUNKS_PER_STEP = 2


def _gmlp_rows(s):
    return CHUNK * min(GMLP_CHUNKS_PER_STEP, s // CHUNK)


def _mix_positions(w_ref, src_scr, dst_scr, gw, mask, bias_ref=None):
    for gi in range(GROUPS):
        cols = slice(gi * gw, (gi + 1) * gw)
        wm = mask(w_ref[gi]).astype(BF16)
        for n in range(src_scr.shape[0] // CHUNK):
            rows = slice(n * CHUNK, (n + 1) * CHUNK)
            out = jnp.dot(wm, src_scr[rows, cols], preferred_element_type=F32)
            if bias_ref is not None:
                out = out + bias_ref[:, gi:gi + 1]
            dst_scr[rows, cols] = out


def _gmlp_fwd_head(proj, ln_g, ln_b, w_s, b_s_t, w_out, x1, gate, gf, target, name):
    _, s, e = proj.shape
    d = w_out.shape[1]
    gw = e // GROUPS
    ts = _gmlp_rows(s)

    def body(pu_ref, pv_ref, pz_ref, g_ref, b_ref, ws_ref, bs_ref, wo_ref, x1_ref, gate_ref, gf_ref, tg_ref,
             y_ref, dx_ref, dbr_ref, loss_ref, dgf_ref, dgate_ref, vn_scr, mix_scr):
        @pl.when(pl.program_id(0) == 0)
        def _():
            loss_ref[...] = jnp.zeros_like(loss_ref)
            dgf_ref[...] = jnp.zeros_like(dgf_ref)
            dgate_ref[...] = jnp.zeros_like(dgate_ref)
        vn, _, _ = _layer_norm_fwd(_gelu(_f32(pv_ref)), g_ref[...], b_ref[...])
        vn_scr[...] = vn.astype(BF16)
        _mix_positions(ws_ref, vn_scr, mix_scr, gw, _tril, bs_ref)
        z = _f32(pz_ref)
        y = ((z * _sigmoid(z)) * (_gelu(_f32(pu_ref)) * mix_scr[...])).astype(BF16)
        y_ref[...] = y
        br = jnp.dot(y, wo_ref[...], preferred_element_type=F32)
        dx, dbr, loss, dgf, dgate = _head_tile(x1_ref[...], br, gate_ref[...], gf_ref[...], tg_ref[...])
        dx_ref[...] = dx
        dbr_ref[...] = dbr.astype(BF16)
        loss_ref[...] += loss
        dgf_ref[...] += dgf
        dgate_ref[...] += dgate

    def part(q):
        return pl.BlockSpec((None, ts, e), lambda i: (q, i, 0))

    return pl.pallas_call(
        body, name=name,
        out_shape=(jax.ShapeDtypeStruct((s, e), BF16), jax.ShapeDtypeStruct((s, d), F32),
                   jax.ShapeDtypeStruct((s, d), BF16), jax.ShapeDtypeStruct((SUBLANE, 128), F32),
                   jax.ShapeDtypeStruct((1, d), F32), jax.ShapeDtypeStruct((1, d), F32)),
        grid=(s // ts,),
        in_specs=[part(0), part(1), part(2), _vec_spec(e), _vec_spec(e),
                  pl.BlockSpec((GROUPS, CHUNK, CHUNK), lambda i: (0, 0, 0)),
                  pl.BlockSpec((CHUNK, GROUPS), lambda i: (0, 0)), pl.BlockSpec((e, d), lambda i: (0, 0)),
                  _row_spec(ts, d), _vec_spec(d), _vec_spec(d), _row_spec(ts, d)],
        out_specs=(_row_spec(ts, e), _row_spec(ts, d), _row_spec(ts, d),
                   pl.BlockSpec((SUBLANE, 128), lambda i: (0, 0)), _vec_spec(d), _vec_spec(d)),
        scratch_shapes=[pltpu.VMEM((ts, e), BF16), pltpu.VMEM((ts, e), F32)],
        compiler_params=_params("arbitrary"),
    )(proj, proj, proj, ln_g, ln_b, w_s, b_s_t, w_out, x1, gate, gf, target)


def _gmlp_bwd(proj, dbr, w_out, ln_g, ln_b, w_s, w_s_t, b_s_t, name):
    _, s, e = proj.shape
    d = dbr.shape[1]
    gw = e // GROUPS
    ts = _gmlp_rows(s)
    n_i = s // ts

    def body(pu_ref, pv_ref, pz_ref, dbr_ref, wo_ref, g_ref, b_ref, ws_ref, wst_ref, bs_ref,
             dp_ref, dws_ref, dbs_ref, dlg_ref, dlb_ref, vn_scr, mix_scr, dm_scr, dvn_scr, dmacc_scr):
        i = pl.program_id(0)

        @pl.when(i == 0)
        def _():
            dws_ref[...] = jnp.zeros_like(dws_ref)
            dlg_ref[...] = jnp.zeros_like(dlg_ref)
            dlb_ref[...] = jnp.zeros_like(dlb_ref)
            dmacc_scr[...] = jnp.zeros_like(dmacc_scr)
        ln_g = g_ref[...]
        u, du_dpu = _gelu_and_grad(_f32(pu_ref))
        v, dv_dpv = _gelu_and_grad(_f32(pv_ref))
        vn, vhat, rstd = _layer_norm_fwd(v, ln_g, b_ref[...])
        vn_scr[...] = vn.astype(BF16)
        _mix_positions(ws_ref, vn_scr, mix_scr, gw, _tril, bs_ref)
        mixed = mix_scr[...]
        dy = lax.dot_general(dbr_ref[...], wo_ref[...], (((1,), (1,)), ((), ())), preferred_element_type=F32)
        sz, dsz = _silu_and_grad(_f32(pz_ref))
        ds = dy * sz
        dp_ref[2] = (dy * (u * mixed) * dsz).astype(BF16)
        dp_ref[0] = (ds * mixed * du_dpu).astype(BF16)
        dm = ds * u
        dm_scr[...] = dm.astype(BF16)
        for n in range(ts // CHUNK):
            rows = slice(n * CHUNK, (n + 1) * CHUNK)
            dmacc_scr[...] += dm[rows]
            for gi in range(GROUPS):
                cols = slice(gi * gw, (gi + 1) * gw)
                dws_ref[gi] += lax.dot_general(dm_scr[rows, cols], vn_scr[rows, cols], (((1,), (1,)), ((), ())),
                                               preferred_element_type=F32)
        _mix_positions(wst_ref, dm_scr, dvn_scr, gw, _triu)
        dvn = dvn_scr[...]
        dlg_ref[...] += _colsum(dvn * vhat)
        dlb_ref[...] += _colsum(dvn)
        dvh = dvn * ln_g
        dv = rstd * (dvh - jnp.mean(dvh, axis=-1, keepdims=True) - vhat * jnp.mean(dvh * vhat, axis=-1, keepdims=True))
        dp_ref[1] = (dv * dv_dpv).astype(BF16)

        @pl.when(i == n_i - 1)
        def _():
            for gi in range(GROUPS):
                dws_ref[gi] = _tril(dws_ref[gi])
                dbs_ref[:, gi:gi + 1] = jnp.sum(dmacc_scr[:, gi * gw:(gi + 1) * gw], axis=1, keepdims=True)

    def part(q):
        return pl.BlockSpec((None, ts, e), lambda i: (q, i, 0))

    w_spec = pl.BlockSpec((GROUPS, CHUNK, CHUNK), lambda i: (0, 0, 0))
    bs_spec = pl.BlockSpec((CHUNK, GROUPS), lambda i: (0, 0))
    return pl.pallas_call(
        body, name=name,
        out_shape=(jax.ShapeDtypeStruct((3, s, e), BF16), jax.ShapeDtypeStruct((GROUPS, CHUNK, CHUNK), F32),
                   jax.ShapeDtypeStruct((CHUNK, GROUPS), F32), jax.ShapeDtypeStruct((1, e), F32),
                   jax.ShapeDtypeStruct((1, e), F32)),
        grid=(n_i,),
        in_specs=[part(0), part(1), part(2), pl.BlockSpec((ts, d), lambda i: (i, 0)),
                  pl.BlockSpec((e, d), lambda i: (0, 0)), _vec_spec(e), _vec_spec(e), w_spec, w_spec, bs_spec],
        out_specs=(pl.BlockSpec((3, ts, e), lambda i: (0, i, 0)), w_spec, bs_spec, _vec_spec(e), _vec_spec(e)),
        scratch_shapes=[pltpu.VMEM((ts, e), BF16), pltpu.VMEM((ts, e), F32), pltpu.VMEM((ts, e), BF16),
                        pltpu.VMEM((ts, e), F32), pltpu.VMEM((CHUNK, e), F32)],
        compiler_params=_params("arbitrary"),
    )(proj, proj, proj, dbr, w_out, ln_g, ln_b, w_s, w_s_t, b_s_t)


def _mod_fwd(c_all, mod_w, mod_b_cols, name):
    n_layer, d, w = mod_w.shape

    def body(c_ref, w_ref, b_ref, ca_ref, o_ref):
        c = c_ref[...]
        ca = c * _sigmoid(c)
        ca_ref[...] = ca
        for li in range(n_layer):
            o_ref[li * N_DEV:(li + 1) * N_DEV, :] = (
                jnp.dot(ca, w_ref[li], preferred_element_type=F32, precision=lax.Precision.HIGHEST) + b_ref[li])

    return pl.pallas_call(
        body, name=name,
        out_shape=(jax.ShapeDtypeStruct((N_DEV, d), F32), jax.ShapeDtypeStruct((n_layer * N_DEV, w), F32)),
        in_specs=[VMEM_SPEC] * 3, out_specs=(VMEM_SPEC, VMEM_SPEC),
        compiler_params=pltpu.CompilerParams(vmem_limit_bytes=VMEM_LIMIT),
    )(c_all, mod_w, mod_b_cols)


def _adamw(w, g, m, v):
    m = ADAM_B1 * m + (1.0 - ADAM_B1) * g
    v = ADAM_B2 * v + (1.0 - ADAM_B2) * (g * g)
    m_hat = m / (1.0 - ADAM_B1 ** ADAM_STEP)
    v_hat = v / (1.0 - ADAM_B2 ** ADAM_STEP)
    delta = -ADAM_LR * (m_hat / (jnp.sqrt(v_hat) + ADAM_EPS) + ADAM_WD * w)
    return delta, m, v


def _adamw_2d(w, g, m, v, name):
    r, c = w.shape
    tr, tc = _tile(r, 512), _tile(c, 1024)

    def body(w_ref, g_ref, m_ref, v_ref, go_ref, d_ref, nm_ref, nv_ref):
        g = g_ref[...]
        go_ref[...] = g
        d_ref[...], nm_ref[...], nv_ref[...] = _adamw(w_ref[...], g, m_ref[...], v_ref[...])

    spec = pl.BlockSpec((tr, tc), lambda i, j: (i, j))
    shape = jax.ShapeDtypeStruct((r, c), F32)
    return pl.pallas_call(
        body, name=name, out_shape=(shape,) * 4, grid=(r // tr, c // tc),
        in_specs=[spec] * 4, out_specs=(spec,) * 4, compiler_params=_params("parallel", "parallel"),
    )(w, g, m, v)


def _mod_w_update(ca_t, dmod_cols, w, m, v, name):
    n_layer, d, wd = w.shape
    tr = _tile(d, 256)

    def body(ca_ref, dm_ref, w_ref, m_ref, v_ref, g_ref, d_ref, nm_ref, nv_ref):
        ca = ca_ref[...]
        dm = dm_ref[...]
        g = ca[:, 0:1] * dm[0:1, :]
        for b in range(1, N_DEV):
            g = g + ca[:, b:b + 1] * dm[b:b + 1, :]
        g_ref[...] = g
        d_ref[...], nm_ref[...], nv_ref[...] = _adamw(w_ref[...], g, m_ref[...], v_ref[...])

    spec = pl.BlockSpec((None, tr, wd), lambda l, i: (l, i, 0))
    shape = jax.ShapeDtypeStruct((n_layer, d, wd), F32)
    return pl.pallas_call(
        body, name=name, out_shape=(shape,) * 4, grid=(n_layer, d // tr),
        in_specs=[pl.BlockSpec((tr, N_DEV), lambda l, i: (i, 0)), pl.BlockSpec((None, N_DEV, wd), lambda l, i: (l, 0, 0)),
                  spec, spec, spec],
        out_specs=(spec,) * 4, compiler_params=_params("parallel", "parallel"),
    )(ca_t, dmod_cols, w, m, v)


def _adamw_small(ws, gs, ms, vs, name):
    n = len(ws)

    def body(*refs):
        ins, outs = refs[:4 * n], refs[4 * n:]
        for k in range(n):
            delta, nm, nv = _adamw(ins[k][...], ins[n + k][...], ins[2 * n + k][...], ins[3 * n + k][...])
            outs[3 * k][...] = delta
            outs[3 * k + 1][...] = nm
            outs[3 * k + 2][...] = nv

    out_shape = []
    for w in ws:
        out_shape += [jax.ShapeDtypeStruct(w.shape, F32)] * 3
    outs = pl.pallas_call(
        body, name=name, out_shape=tuple(out_shape),
        in_specs=[VMEM_SPEC] * (4 * n), out_specs=tuple([VMEM_SPEC] * (3 * n)),
        compiler_params=pltpu.CompilerParams(vmem_limit_bytes=VMEM_LIMIT),
    )(*ws, *gs, *ms, *vs)
    return [tuple(outs[3 * k:3 * k + 3]) for k in range(n)]


def _place():
    return lax.axis_index("x"), lax.axis_index("y"), lax.axis_index("c")


def _other_chips(x, y):
    return [(1 - x, y), (x, 1 - y), (1 - x, 1 - y)]


def _all_gather(block, name, after=None):
    m_per, n = block.shape

    def body(x_ref, *rest):
        out_ref, send_sems, recv_sems, local_sem = rest[-4:]
        x, y, c = _place()
        me, sibling = (x, y, c), (x, y, 1 - c)
        chips = _other_chips(x, y)

        def rows(px, py, pc):
            return out_ref.at[pl.ds((4 * px + 2 * py + pc) * m_per, m_per), :]

        def copy(k, blk, to, src=None):
            return pltpu.make_async_remote_copy(
                src_ref=rows(*blk) if src is None else src, dst_ref=rows(*blk),
                send_sem=send_sems.at[k], recv_sem=recv_sems.at[k], device_id=to, device_id_type=MESH)

        mine = pltpu.make_async_copy(x_ref, rows(*me), local_sem)
        mine.start()
        first = [copy(0, me, sibling, src=x_ref)]
        first += [copy(1 + j, me, (*chip, c), src=x_ref) for j, chip in enumerate(chips)]
        for cp in first:
            cp.start()
        passed = [copy(4 + j, (*chip, c), sibling) for j, chip in enumerate(chips)]
        for j, chip in enumerate(chips):
            copy(1 + j, (*chip, c), me).wait_recv()
            passed[j].start()
        copy(0, sibling, me).wait_recv()
        for j, chip in enumerate(chips):
            copy(4 + j, (*chip, 1 - c), me).wait_recv()
        for cp in first + passed:
            cp.wait_send()
        mine.wait()

    return pl.pallas_call(
        body, name=name, out_shape=jax.ShapeDtypeStruct((N_DEV * m_per, n), F32),
        in_specs=[VMEM_SPEC] + ([] if after is None else [ANY_SPEC]), out_specs=VMEM_SPEC,
        scratch_shapes=[pltpu.SemaphoreType.DMA((7,)), pltpu.SemaphoreType.DMA((7,)), pltpu.SemaphoreType.DMA],
        compiler_params=pltpu.CompilerParams(vmem_limit_bytes=VMEM_LIMIT),
    )(*([block] if after is None else [block, after]))


def _hbm(a):
    return pltpu.with_memory_space_constraint(a, pltpu.HBM)


def _place_own(shard, kind, chip_idx, name):
    r, cdim = shard.shape
    tr, tc = _tile(r, 512), _tile(cdim, 1024)
    nrb, ncb = r // tr, cdim // tc

    def body(k_ref, s_ref, o_ref):
        o_ref[...] = s_ref[...].astype(BF16)

    if kind == "col":
        full, o_map = (r, N_CHIP * cdim), lambda i, j, kr: (i, kr[0] * ncb + j)
    else:
        full, o_map = (N_CHIP * r, cdim), lambda i, j, kr: (kr[0] * nrb + i, j)
    return pl.pallas_call(
        body, name=name, out_shape=jax.ShapeDtypeStruct(full, BF16),
        grid_spec=pltpu.PrefetchScalarGridSpec(
            num_scalar_prefetch=1, grid=(nrb, ncb),
            in_specs=[pl.BlockSpec((tr, tc), lambda i, j, kr: (i, j))],
            out_specs=pl.BlockSpec((tr, tc), o_map)),
        compiler_params=_params("parallel", "parallel"),
    )(chip_idx, shard)


def _weight_window(ref, kind, shard_shape, k, half):
    r, cdim = shard_shape
    hr = r // 2
    if kind == "col":
        return ref.at[pl.ds(half * hr, hr), pl.ds(pl.multiple_of(k * cdim, 128), cdim)]
    return ref.at[pl.ds(pl.multiple_of(k * r + half * hr, 2 * SUBLANE), hr), :]


def _gather_weights_start(fulls, kinds, shard_shapes, groups, after, name):
    n, ng = len(fulls), len(groups)

    def body(*refs):
        ins = refs[:n]
        sems = refs[n + 1:n + 1 + 2 * ng]
        token = refs[2 * n + 1 + 2 * ng]
        x, y, c = _place()
        chips = _other_chips(x, y)
        for g, group in enumerate(groups):
            for pos, (w, j) in enumerate(group):
                own = _weight_window(ins[w], kinds[w], shard_shapes[w], 2 * x + y, c)
                for cc in range(2):
                    pltpu.make_async_remote_copy(
                        src_ref=own, dst_ref=own,
                        send_sem=sems[2 * g].at[2 * pos + cc], recv_sem=sems[2 * g + 1].at[2 * pos + c],
                        device_id=(*chips[j], cc), device_id_type=MESH).start()
        token[...] = jnp.zeros_like(token)

    sem_shapes = []
    for group in groups:
        sem_shapes += [pltpu.SemaphoreType.DMA((2 * len(group),))] * 2
    outs = pl.pallas_call(
        body, name=name,
        out_shape=tuple(sem_shapes) + tuple(pltpu.HBM(f.shape, f.dtype) for f in fulls)
        + (jax.ShapeDtypeStruct((SUBLANE, 128), F32),),
        in_specs=[HBM_SPEC] * n + [ANY_SPEC], out_specs=(SEM_SPEC,) * (2 * ng) + (HBM_SPEC,) * n + (VMEM_SPEC,),
        input_output_aliases={w: 2 * ng + w for w in range(n)},
        compiler_params=pltpu.CompilerParams(has_side_effects=EFFECT),
    )(*[_hbm(f) for f in fulls], after)
    sems = [(outs[2 * g], outs[2 * g + 1]) for g in range(ng)]
    return sems, list(outs[2 * ng:2 * ng + n]), outs[2 * ng + n]


def _gather_weights_wait(sems, group, fulls, kinds, shard_shapes, after, name):
    n = len(fulls)

    def body(*refs):
        ins = refs[:n]
        send_sems, recv_sems = refs[n], refs[n + 1]
        x, y, c = _place()
        chips = _other_chips(x, y)
        for pos, (w, j) in enumerate(group):
            own = _weight_window(ins[w], kinds[w], shard_shapes[w], 2 * x + y, c)
            for cc in range(2):
                landed = _weight_window(ins[w], kinds[w], shard_shapes[w], 2 * chips[j][0] + chips[j][1], cc)
                k = 2 * pos + cc
                pltpu.make_async_remote_copy(
                    src_ref=own, dst_ref=own, send_sem=send_sems.at[k], recv_sem=recv_sems.at[k],
                    device_id=(*chips[j], cc), device_id_type=MESH).wait_send()
                pltpu.make_async_remote_copy(
                    src_ref=landed, dst_ref=landed, send_sem=send_sems.at[k], recv_sem=recv_sems.at[k],
                    device_id=(*chips[j], cc), device_id_type=MESH).wait_recv()

    return list(pl.pallas_call(
        body, name=name, out_shape=tuple(pltpu.HBM(f.shape, f.dtype) for f in fulls),
        in_specs=[HBM_SPEC] * n + [SEM_SPEC, SEM_SPEC, ANY_SPEC], out_specs=(HBM_SPEC,) * n,
        input_output_aliases={w: w for w in range(n)},
        compiler_params=pltpu.CompilerParams(has_side_effects=EFFECT),
    )(*fulls, sems[0], sems[1], after))


def _grad_piece(ref, kind, h, cdim, k, half):
    if kind == "col":
        return ref.at[pl.ds(half * h, h), pl.ds(pl.multiple_of(k * cdim, 128), cdim)]
    return ref.at[k, pl.ds(half * h, h), :]


def _grad_dims(g, kind):
    return (g.shape[0] // 2, g.shape[1] // N_CHIP) if kind == "col" else (g.shape[1] // 2, g.shape[2])


def _reduce_start(grads, kinds, name):
    n = len(grads)
    dims = [_grad_dims(g, kind) for g, kind in zip(grads, kinds)]
    lands = [lax.empty((N_DEV - 1, h, cdim), g.dtype) for g, (h, cdim) in zip(grads, dims)]

    def body(*refs):
        g_ins, land_ins = refs[:n], refs[n:2 * n]
        send_sems, recv_sems = refs[2 * n], refs[2 * n + 1]
        token = refs[4 * n + 2]
        x, y, c = _place()
        for w in range(n):
            h, cdim = dims[w]
            base = (N_DEV - 1) * w
            pltpu.make_async_remote_copy(
                src_ref=_grad_piece(g_ins[w], kinds[w], h, cdim, 2 * x + y, 1 - c), dst_ref=land_ins[w].at[0],
                send_sem=send_sems.at[base], recv_sem=recv_sems.at[base],
                device_id=(x, y, 1 - c), device_id_type=MESH).start()
            for j, chip in enumerate(_other_chips(x, y)):
                for cc in range(2):
                    pltpu.make_async_remote_copy(
                        src_ref=_grad_piece(g_ins[w], kinds[w], h, cdim, 2 * chip[0] + chip[1], cc),
                        dst_ref=land_ins[w].at[1 + 2 * j + c],
                        send_sem=send_sems.at[base + 1 + 2 * j + cc], recv_sem=recv_sems.at[base + 1 + 2 * j + c],
                        device_id=(*chip, cc), device_id_type=MESH).start()
        token[...] = jnp.zeros_like(token)

    sems = pltpu.SemaphoreType.DMA(((N_DEV - 1) * n,))
    outs = pl.pallas_call(
        body, name=name,
        out_shape=(sems, sems) + tuple(pltpu.HBM(a.shape, a.dtype) for a in list(grads) + lands)
        + (jax.ShapeDtypeStruct((SUBLANE, 128), F32),),
        in_specs=[HBM_SPEC] * (2 * n), out_specs=(SEM_SPEC, SEM_SPEC) + (HBM_SPEC,) * (2 * n) + (VMEM_SPEC,),
        input_output_aliases={i: 2 + i for i in range(2 * n)},
        compiler_params=pltpu.CompilerParams(has_side_effects=EFFECT),
    )(*[_hbm(a) for a in list(grads) + lands])
    return outs[0], outs[1], list(outs[2:2 + n]), list(outs[2 + n:2 + 2 * n]), outs[2 + 2 * n]


def _reduce_wait(send_sems, recv_sems, grads, lands, kinds, after, name):
    n = len(grads)
    dims = [_grad_dims(g, kind) for g, kind in zip(grads, kinds)]

    def body(*refs):
        g_ins, land_ins = refs[:n], refs[n:2 * n]
        send_sems, recv_sems = refs[2 * n], refs[2 * n + 1]
        x, y, c = _place()
        for w in range(n):
            h, cdim = dims[w]
            piece = _grad_piece(g_ins[w], kinds[w], h, cdim, 2 * x + y, c)
            for s in range(N_DEV - 1):
                k = (N_DEV - 1) * w + s
                slot = land_ins[w].at[s]
                pltpu.make_async_remote_copy(
                    src_ref=piece, dst_ref=slot, send_sem=send_sems.at[k], recv_sem=recv_sems.at[k],
                    device_id=(x, y, 1 - c), device_id_type=MESH).wait_send()
                pltpu.make_async_remote_copy(
                    src_ref=piece, dst_ref=slot, send_sem=send_sems.at[k], recv_sem=recv_sems.at[k],
                    device_id=(x, y, 1 - c), device_id_type=MESH).wait_recv()

    outs = pl.pallas_call(
        body, name=name, out_shape=tuple(pltpu.HBM(a.shape, a.dtype) for a in list(grads) + list(lands)),
        in_specs=[HBM_SPEC] * (2 * n) + [SEM_SPEC, SEM_SPEC, ANY_SPEC], out_specs=(HBM_SPEC,) * (2 * n),
        input_output_aliases={i: i for i in range(2 * n)},
        compiler_params=pltpu.CompilerParams(has_side_effects=EFFECT),
    )(*grads, *lands, send_sems, recv_sems, after)
    return list(outs[:n]), list(outs[n:])


def _add_pieces(g, land, kind, chip_idx, core_idx, name):
    _, h, cdim = land.shape
    tr, tc = _tile(h, 256), _tile(cdim, 2048)
    nrb, ncb = h // tr, cdim // tc

    def body(k_ref, c_ref, g_ref, l_ref, o_ref):
        acc = g_ref[...].astype(F32)
        for s in range(N_DEV - 1):
            acc = acc + l_ref[s].astype(F32)
        o_ref[...] = acc

    if kind == "col":
        g_spec = pl.BlockSpec((tr, tc), lambda i, j, kr, cr: (cr[0] * nrb + i, kr[0] * ncb + j))
    else:
        g_spec = pl.BlockSpec((None, tr, tc), lambda i, j, kr, cr: (kr[0], cr[0] * nrb + i, j))
    return pl.pallas_call(
        body, name=name, out_shape=jax.ShapeDtypeStruct((2 * h, cdim), F32),
        grid_spec=pltpu.PrefetchScalarGridSpec(
            num_scalar_prefetch=2, grid=(nrb, ncb),
            in_specs=[g_spec, pl.BlockSpec((N_DEV - 1, tr, tc), lambda i, j, kr, cr: (0, i, j))],
            out_specs=pl.BlockSpec((tr, tc), lambda i, j, kr, cr: (cr[0] * nrb + i, j))),
        compiler_params=_params("parallel", "parallel"),
    )(chip_idx, core_idx, g, land)


def _join_halves(shards, name):
    n = len(shards)

    def body(*refs):
        ins, outs = refs[:n], refs[n:2 * n]
        send_sems, recv_sems = refs[2 * n:]
        x, y, c = _place()
        copies = []
        for w in range(n):
            h = shards[w].shape[0] // 2
            cp = pltpu.make_async_remote_copy(
                src_ref=ins[w].at[pl.ds(c * h, h), :], dst_ref=outs[w].at[pl.ds(c * h, h), :],
                send_sem=send_sems.at[w], recv_sem=recv_sems.at[w], device_id=(x, y, 1 - c), device_id_type=MESH)
            cp.start()
            copies.append(cp)
        for w in range(n):
            h = shards[w].shape[0] // 2
            theirs = outs[w].at[pl.ds((1 - c) * h, h), :]
            pltpu.make_async_remote_copy(
                src_ref=theirs, dst_ref=theirs, send_sem=send_sems.at[w], recv_sem=recv_sems.at[w],
                device_id=(x, y, 1 - c), device_id_type=MESH).wait_recv()
        for cp in copies:
            cp.wait_send()

    return pl.pallas_call(
        body, name=name, out_shape=tuple(jax.ShapeDtypeStruct(s.shape, s.dtype) for s in shards),
        in_specs=[HBM_SPEC] * n, out_specs=tuple([HBM_SPEC] * n),
        input_output_aliases={w: w for w in range(n)},
        scratch_shapes=[pltpu.SemaphoreType.DMA((n,))] * 2,
    )(*shards)


def _sum_devices(gathered, rows, name):
    n = gathered.shape[1]

    def body(g_ref, o_ref):
        acc = g_ref[0:rows, :]
        for dev in range(1, N_DEV):
            acc = acc + g_ref[dev * rows:(dev + 1) * rows, :]
        o_ref[...] = acc

    return pl.pallas_call(
        body, name=name, out_shape=jax.ShapeDtypeStruct((rows, n), F32),
        in_specs=[VMEM_SPEC], out_specs=VMEM_SPEC,
        compiler_params=pltpu.CompilerParams(vmem_limit_bytes=VMEM_LIMIT),
    )(gathered)


def _pack(vectors, width):
    flat = [v.reshape(-1) for v in vectors]
    offsets, total = [], 0
    for f in flat:
        offsets.append(total)
        total += f.shape[0]
    rows = -(-total // (width * SUBLANE)) * SUBLANE
    flat.append(jnp.zeros((rows * width - total,), F32))
    return jnp.concatenate(flat).reshape(rows, width), offsets


def kernel(x, c, mod_w, mod_b, norm_g, a_w_in, a_conv_w, a_conv_b, a_w_out, b_w_in, b_ln_g, b_ln_b, b_w_s, b_b_s, b_w_out, final_g, loss_target, m_mod_w, m_mod_b, m_norm_g, m_a_w_in, m_a_conv_w, m_a_conv_b, m_a_w_out, m_b_w_in, m_b_ln_g, m_b_ln_b, m_b_w_s, m_b_b_s, m_b_w_out, m_final_g, v_mod_w, v_mod_b, v_norm_g, v_a_w_in, v_a_conv_w, v_a_conv_b, v_a_w_out, v_b_w_in, v_b_ln_g, v_b_ln_b, v_b_w_s, v_b_b_s, v_b_w_out, v_final_g):
    seq, d = x.shape[1], x.shape[2]
    e = a_conv_b.shape[1]
    wd = mod_w.shape[2]
    ax, ay, ac = _place()
    chip = 2 * ax + ay
    dev = 2 * chip + ac
    chip_idx = jnp.reshape(chip, (1,)).astype(jnp.int32)
    core_idx = jnp.reshape(ac, (1,)).astype(jnp.int32)

    x2d = x[0]
    target = loss_target[0]

    w_names = ["a_w_in", "a_w_out", "b_w_in", "b_w_out"]
    w_kinds = ["col", "row", "col", "row"]
    w_shards = [a_w_in[0], a_w_out[0], b_w_in[0], b_w_out[0]]
    w_shapes = [sh.shape for sh in w_shards]
    placed = [_place_own(sh, kind, chip_idx, "place_" + nm) for nm, sh, kind in zip(w_names, w_shards, w_kinds)]
    a_groups, far_groups = [[(0, 0), (0, 1)]], [[(0, 2)]]
    whole_group = [[(0, 0), (0, 1), (0, 2)]]

    es = e // N_CHIP
    packed0, offs0 = _pack([c, a_conv_w, b_ln_g, b_ln_b], 1024)
    gathered0 = _all_gather(packed0, "gather_params").reshape(N_DEV, -1)
    c_all = gathered0[:, :d]
    per_chip = gathered0[0::2]

    def from_chips_cols(k, rows_):
        got = per_chip[:, offs0[k]:offs0[k] + rows_ * es].reshape(N_CHIP, rows_, es)
        return jnp.transpose(got, (1, 0, 2)).reshape(rows_, e)

    conv_w_full = from_chips_cols(1, 3)
    conv_w = [conv_w_full[k:k + 1] for k in range(3)]
    ln_g, ln_b = from_chips_cols(2, 1), from_chips_cols(3, 1)
    mod_b_cols = lax.dynamic_slice_in_dim(mod_b, chip * wd, wd, axis=1)[:, None, :]
    c_act, mod_part = _mod_fwd(c_all, mod_w, mod_b_cols, "mod_fwd")
    n_layer = mod_w.shape[0]
    mod_gathered = _all_gather(mod_part, "gather_mod")
    a_sems, (wa_in,), a_token = _gather_weights_start(
        placed[:1], w_kinds[:1], w_shapes[:1], a_groups, mod_gathered, "gather_a_w_in_start")
    mod_all = mod_gathered.reshape(N_CHIP, 2, n_layer, N_DEV, wd)[:, 0]
    mod_all = jnp.transpose(mod_all, (1, 2, 0, 3)).reshape(n_layer, N_DEV, N_CHIP * wd)
    mod_me = lax.dynamic_index_in_dim(mod_all, dev, axis=1, keepdims=False)
    shift = [mod_me[l:l + 1, 0:d] for l in range(n_layer)]
    scale = [mod_me[l:l + 1, d:2 * d] for l in range(n_layer)]
    gate = [mod_me[l:l + 1, 2 * d:3 * d] for l in range(n_layer)]

    g0, g1, gf = norm_g[0:1], norm_g[1:2], final_g[None, :]
    h0 = _norm_mod(x2d, g0, scale[0], shift[0], "norm_mod0")

    def slab(r):
        return jnp.bitwise_xor(chip_idx, r)

    def arrived(sems, group, weight, w, after, name):
        return _gather_weights_wait(sems, [(0, j) for _, j in group], [weight], [w_kinds[w]], [w_shapes[w]],
                                    after, name)[0]

    proj0 = _mm_proj_slab(h0, wa_in, None, slab(0), 4, "a_proj_own", a_token)
    wa_in = arrived(a_sems[0], a_groups[0], wa_in, 0, proj0, "gather_wait_near")
    far_sems, (wa_in,), far_token = _gather_weights_start(
        [wa_in], w_kinds[:1], w_shapes[:1], far_groups, a_token, "gather_a_w_in_far_start")
    proj0 = _mm_proj_slab(h0, wa_in, proj0, slab(2), 4, "a_proj_x", far_token)
    proj0 = _mm_proj_slab(h0, wa_in, proj0, slab(1), 4, "a_proj_y", far_token)
    wa_in = arrived(far_sems[0], far_groups[0], wa_in, 0, proj0, "gather_wait_far")
    def start_whole(w, after, name):
        sems, (flight,), token = _gather_weights_start(
            placed[w:w + 1], w_kinds[w:w + 1], w_shapes[w:w + 1], whole_group, after, name)
        return sems[0], flight, token

    ao_sems, wa_out, ao_token = start_whole(1, wa_in, "gather_a_w_out_start")
    proj0 = _mm_proj_slab(h0, wa_in, proj0, slab(3), 4, "a_proj_far", ao_token)
    wa_out = arrived(ao_sems, whole_group[0], wa_out, 1, proj0, "gather_wait_a_w_out")
    bi_sems, wb_in, bi_token = start_whole(2, wa_out, "gather_b_w_in_start")
    y0, br0, x1, h1 = _conv_fwd(proj0, conv_w, a_conv_b, wa_out, x2d, gate[0], g1, scale[1], shift[1],
                                "conv_fwd", bi_token)
    wb_in = arrived(bi_sems, whole_group[0], wb_in, 2, h1, "gather_wait_b_w_in")
    bo_sems, wb_out, bo_token = start_whole(3, wb_in, "gather_b_w_out_start")
    proj1 = _mm_proj(h1, wb_in, 3, "b_proj", bo_token)
    b_s_t = jnp.transpose(b_b_s[0])
    wb_out = arrived(bo_sems, whole_group[0], wb_out, 3, proj1, "gather_wait_b_w_out")
    y1, dx2, dbr1, loss_part, g_final_g, dgate1 = _gmlp_fwd_head(
        proj1, ln_g, ln_b, b_w_s[0], b_s_t, wb_out, x1, gate[1], gf, target, "gmlp_fwd_head")

    gw_b_out = _mm_dw_out(y1, dbr1, "b_out_dw")
    dproj1, g_w_s, g_b_s_t, g_ln_g, g_ln_b = _gmlp_bwd(
        proj1, dbr1, wb_out, ln_g, ln_b, b_w_s[0], jnp.swapaxes(b_w_s[0], 1, 2), b_s_t, "gmlp_bwd")
    gw_b_in = _mm_dw_in(h1, dproj1, "b_proj_dw")
    b_kinds = ["col", "row"]
    b_send, b_recv, b_grads, b_lands, b_token = _reduce_start(
        [gw_b_in, gw_b_out.reshape(N_CHIP, e // N_CHIP, d)], b_kinds, "reduce_b_start")
    dx1, dshift1, dscale1, g_g1, dbr0, dgate0 = _mm_dh_norm_bwd(
        dproj1, wb_in, x1, dx2, g1, scale[1], "b_proj_dx", b_token, br=br0, gate=gate[0])

    gw_a_out = _mm_dw_out(y0, dbr0, "a_out_dw")
    ao_send, ao_recv, ao_grads, ao_lands, ao_grad_token = _reduce_start(
        [gw_a_out.reshape(N_CHIP, e // N_CHIP, d)], ["row"], "reduce_a_out_start")
    dproj0, g_w0, g_w1, g_w2, g_conv_b = _conv_bwd(proj0, dbr0, wa_out, conv_w, a_conv_b, "conv_bwd", ao_grad_token)
    gw_a_in = _mm_dw_in(h0, dproj0, "a_proj_dw")
    ai_send, ai_recv, ai_grads, ai_lands, ai_token = _reduce_start([gw_a_in], ["col"], "reduce_a_in_start")
    grad_x, dshift0, dscale0, g_g0 = _mm_dh_norm_bwd(dproj0, wa_in, x2d, dx1, g0, scale[0], "a_proj_dx", ai_token)

    def finish(send, recv, grads_, lands_, kinds_, names_, after, tag):
        grads_, lands_ = _reduce_wait(send, recv, grads_, lands_, kinds_, after, "reduce_" + tag + "_wait")
        halves = [_add_pieces(g, land, kind, chip_idx, core_idx, "add_pieces_" + nm)
                  for g, land, kind, nm in zip(grads_, lands_, kinds_, names_)]
        return _join_halves(halves, "join_" + tag)

    upd, big_grads = {}, {}

    def adamw_big(nm, w, g, m, v):
        g_out, *rest = _adamw_2d(w[0], g, m[0], v[0], "adamw_" + nm)
        big_grads[nm] = g_out[None]
        upd[nm] = tuple(o[None] for o in rest)

    g_b_w_in, g_b_w_out = finish(b_send, b_recv, b_grads, b_lands, b_kinds, ["b_w_in", "b_w_out"], grad_x, "b")
    adamw_big("b_w_in", b_w_in, g_b_w_in, m_b_w_in, v_b_w_in)
    adamw_big("b_w_out", b_w_out, g_b_w_out, m_b_w_out, v_b_w_out)

    dmod = jnp.concatenate([dshift0, dscale0, dgate0, dshift1, dscale1, dgate1], axis=1)
    small = [loss_part[0, 0:1], g_final_g, g_g0, g_g1, jnp.concatenate([g_w0, g_w1, g_w2], axis=0), g_conv_b, g_ln_g, g_ln_b,
             jnp.transpose(g_b_s_t), g_w_s, dmod]
    packed, offs = _pack(small, 1024)
    rows = packed.shape[0]
    gathered = _all_gather(packed, "gather_small", after=upd["b_w_out"][0])
    summed = _sum_devices(gathered, rows, "sum_small").reshape(-1)

    def take(k, shape):
        size = math.prod(shape)
        return summed[offs[k]:offs[k] + size].reshape(shape)

    loss = take(0, ())
    grad_final_g = take(1, (d,))
    grad_norm_g = jnp.concatenate([take(2, (1, d)), take(3, (1, d))], axis=0)
    grad_conv_w_full = take(4, (3, e))
    grad_a_conv_b = take(5, (1, e))
    grad_ln_g_full = take(6, (1, e))
    grad_ln_b_full = take(7, (1, e))
    grad_b_b_s = take(8, (1, GROUPS, CHUNK))
    grad_b_w_s = take(9, (1, GROUPS, CHUNK, CHUNK))
    grad_mod_b = take(10, (n_layer, 3 * d))
    grad_a_conv_w = lax.dynamic_slice_in_dim(grad_conv_w_full, chip * es, es, axis=1)[None]
    grad_b_ln_g = lax.dynamic_slice_in_dim(grad_ln_g_full, chip * es, es, axis=1)
    grad_b_ln_b = lax.dynamic_slice_in_dim(grad_ln_b_full, chip * es, es, axis=1)
    dmod_all = gathered.reshape(N_DEV, rows * 1024)[:, offs[10]:offs[10] + n_layer * 3 * d]
    dmod_all = dmod_all.reshape(N_DEV, n_layer, 3 * d)
    dmod_cols = jnp.transpose(lax.dynamic_slice_in_dim(dmod_all, chip * wd, wd, axis=2), (1, 0, 2))

    grad_mod_w, delta_mod_w, new_m_mod_w, new_v_mod_w = _mod_w_update(
        jnp.transpose(c_act), dmod_cols, mod_w, m_mod_w, v_mod_w, "mod_w_update")
    (g_a_w_out,) = finish(ao_send, ao_recv, ao_grads, ao_lands, ["row"], ["a_w_out"], delta_mod_w, "a_out")
    adamw_big("a_w_out", a_w_out, g_a_w_out, m_a_w_out, v_a_w_out)
    (g_a_w_in,) = finish(ai_send, ai_recv, ai_grads, ai_lands, ["col"], ["a_w_in"], upd["a_w_out"][0], "a_in")
    adamw_big("a_w_in", a_w_in, g_a_w_in, m_a_w_in, v_a_w_in)
    small_w = [("mod_b", mod_b, grad_mod_b, m_mod_b, v_mod_b), ("norm_g", norm_g, grad_norm_g, m_norm_g, v_norm_g),
               ("a_conv_w", a_conv_w, grad_a_conv_w, m_a_conv_w, v_a_conv_w),
               ("a_conv_b", a_conv_b, grad_a_conv_b, m_a_conv_b, v_a_conv_b),
               ("b_ln_g", b_ln_g, grad_b_ln_g, m_b_ln_g, v_b_ln_g), ("b_ln_b", b_ln_b, grad_b_ln_b, m_b_ln_b, v_b_ln_b),
               ("b_w_s", b_w_s, grad_b_w_s, m_b_w_s, v_b_w_s), ("b_b_s", b_b_s, grad_b_b_s, m_b_b_s, v_b_b_s),
               ("final_g", final_g, grad_final_g, m_final_g, v_final_g)]

    def flat2d(a):
        return a.reshape(-1, a.shape[-1])

    res = _adamw_small([flat2d(t[1]) for t in small_w], [flat2d(t[2]) for t in small_w],
                       [flat2d(t[3]) for t in small_w], [flat2d(t[4]) for t in small_w], "adamw_small")
    for (nm, w, _, _, _), r3 in zip(small_w, res):
        upd[nm] = tuple(o.reshape(w.shape) for o in r3)
    upd["mod_w"] = (delta_mod_w, new_m_mod_w, new_v_mod_w)

    grads = {"mod_w": grad_mod_w, "mod_b": grad_mod_b, "norm_g": grad_norm_g, "a_conv_w": grad_a_conv_w,
             "a_conv_b": grad_a_conv_b, "b_ln_g": grad_b_ln_g, "b_ln_b": grad_b_ln_b, "b_w_s": grad_b_w_s,
             "b_b_s": grad_b_b_s, "final_g": grad_final_g, **big_grads}
    order = ["mod_w", "mod_b", "norm_g", "a_w_in", "a_conv_w", "a_conv_b", "a_w_out", "b_w_in", "b_ln_g", "b_ln_b",
             "b_w_s", "b_b_s", "b_w_out", "final_g"]
    return (loss, grad_x[None], *[grads[k] for k in order], *[upd[k][0] for k in order],
            *[upd[k][1] for k in order], *[upd[k][2] for k in order])
```

```python
import functools
import math

import jax
import jax.numpy as jnp
from jax import lax
from jax.experimental import pallas as pl
from jax.experimental.pallas import tpu as pltpu

F32 = jnp.float32
BF16 = jnp.bfloat16
MESH = pl.DeviceIdType.MESH

N_DEV = 8
N_CHIP = 4
SUBLANE = 8
PACK = 16
ACT = F32
RMS_EPS = 1e-6
LN_EPS = 1e-5
CHUNK = 128
GROUPS = 8
ADAM_LR = 0.001
ADAM_B1 = 0.9
ADAM_B2 = 0.999
ADAM_EPS = 1e-08
ADAM_WD = 0.01
ADAM_STEP = 10
VMEM_LIMIT = 56 << 20

HBM_SPEC = pl.BlockSpec(memory_space=pltpu.HBM)
VMEM_SPEC = pl.BlockSpec(memory_space=pltpu.VMEM)
SEM_SPEC = pl.BlockSpec(memory_space=pltpu.SEMAPHORE)
ANY_SPEC = pl.BlockSpec(memory_space=pl.ANY)
EFFECT = pltpu.SideEffectType.DATAFLOW_SIDE_EFFECTING


def _params(*sem):
    return pltpu.CompilerParams(dimension_semantics=sem, vmem_limit_bytes=VMEM_LIMIT)


def _tile(n, want):
    if n <= want:
        return n
    t = want
    while n % t:
        t -= 128
    return t


def _sigmoid(x):
    return 0.5 * jnp.tanh(0.5 * x) + 0.5


def _silu_and_grad(x):
    s = _sigmoid(x)
    return x * s, s * (1.0 + x * (1.0 - s))


def _gelu_and_grad(x):
    cdf = 0.5 * (1.0 + lax.erf(x * (1.0 / math.sqrt(2.0))))
    pdf = jnp.exp(-0.5 * x * x) * (1.0 / math.sqrt(2.0 * math.pi))
    return x * cdf, cdf + x * pdf


def _gelu(x):
    return x * (0.5 * (1.0 + lax.erf(x * (1.0 / math.sqrt(2.0)))))


def _rms(x):
    r = lax.rsqrt(jnp.mean(x * x, axis=-1, keepdims=True) + RMS_EPS)
    return x * r, r


def _rms_bwd(dxn, xn, r):
    return r * (dxn - xn * jnp.mean(dxn * xn, axis=-1, keepdims=True))


def _colsum(a):
    return jnp.sum(a, axis=0, keepdims=True)


def _shift_down(cur, before, k):
    rolled = pltpu.roll(cur, k, 0)
    row = lax.broadcasted_iota(jnp.int32, before.shape, 0)
    head = jnp.where(row < k, pltpu.roll(before, k, 0), rolled[:SUBLANE])
    return jnp.concatenate([head, rolled[SUBLANE:]], axis=0)


def _shift_up(cur, after, k):
    n = cur.shape[0]
    rolled = pltpu.roll(cur, n - k, 0)
    row = lax.broadcasted_iota(jnp.int32, after.shape, 0)
    tail = jnp.where(row >= SUBLANE - k, pltpu.roll(after, SUBLANE - k, 0), rolled[n - SUBLANE:])
    return jnp.concatenate([rolled[:n - SUBLANE], tail], axis=0)


def _after_spec():
    return pl.BlockSpec((SUBLANE, 128), lambda *_: (0, 0))


def _mm_proj(h, w, n_split, name, after):
    s, d = h.shape
    e = w.shape[1] // n_split
    tm, tn = _tile(s, 1024), _tile(e, 2048)
    nj = e // tn

    def body(h_ref, w_ref, after_ref, o_ref):
        o_ref[...] = jnp.dot(h_ref[...], w_ref[...], preferred_element_type=F32).astype(ACT)

    return pl.pallas_call(
        body, name=name,
        out_shape=jax.ShapeDtypeStruct((n_split, s, e), ACT),
        grid=(s // tm, n_split * nj),
        in_specs=[pl.BlockSpec((tm, d), lambda i, j: (i, 0)), pl.BlockSpec((d, tn), lambda i, j: (0, j)),
                  _after_spec()],
        out_specs=pl.BlockSpec((None, tm, tn), lambda i, j: (j // nj, i, j % nj)),
        compiler_params=_params("parallel", "parallel"),
    )(h, w, after)


def _mm_proj_slab(h, w, proj, q_idx, n_split, name, after):
    s, d = h.shape
    e = w.shape[1] // n_split
    tm, tn = _tile(s, 1024), _tile(e, 2048)
    nj = e // tn

    def body(q_ref, h_ref, w_ref, *rest):
        o_ref = rest[-1]
        o_ref[...] = jnp.dot(h_ref[...], w_ref[...], preferred_element_type=F32).astype(ACT)

    in_specs = [pl.BlockSpec((tm, d), lambda i, j, qr: (i, 0)), pl.BlockSpec((d, tn), lambda i, j, qr: (0, qr[0] * nj + j)),
                _after_spec()]
    args = [q_idx, h, w, after]
    aliases = {}
    if proj is not None:
        in_specs.append(ANY_SPEC)
        args.append(proj)
        aliases = {4: 0}
    return pl.pallas_call(
        body, name=name,
        out_shape=jax.ShapeDtypeStruct((n_split, s, e), ACT),
        grid_spec=pltpu.PrefetchScalarGridSpec(
            num_scalar_prefetch=1, grid=(s // tm, nj), in_specs=in_specs,
            out_specs=pl.BlockSpec((None, tm, tn), lambda i, j, qr: (qr[0], i, j))),
        input_output_aliases=aliases,
        compiler_params=_params("parallel", "parallel"),
    )(*args)


def _mm_dh_norm_bwd(dp, w, x, dx_in, g, scale, name, after, br=None, gate=None):
    nq, s, e = dp.shape
    d = w.shape[0]
    has_branch = br is not None
    tm, tk = _tile(s, 1024), _tile(e, 1024 if has_branch else 2048)
    nkq = e // tk
    nk = nq * nkq

    def body(*refs):
        if has_branch:
            (a_ref, b_ref, after_ref, x_ref, dxin_ref, g_ref, sc_ref, br_ref, gate_ref,
             dx_ref, dsh_ref, dsc_ref, dg_ref, dbr_ref, dgate_ref, acc_ref) = refs
        else:
            (a_ref, b_ref, after_ref, x_ref, dxin_ref, g_ref, sc_ref,
             dx_ref, dsh_ref, dsc_ref, dg_ref, acc_ref) = refs
        i, k = pl.program_id(0), pl.program_id(1)

        @pl.when(jnp.logical_and(i == 0, k == 0))
        def _():
            dsh_ref[...] = jnp.zeros_like(dsh_ref)
            dsc_ref[...] = jnp.zeros_like(dsc_ref)
            dg_ref[...] = jnp.zeros_like(dg_ref)
            if has_branch:
                dgate_ref[...] = jnp.zeros_like(dgate_ref)

        @pl.when(k == 0)
        def _():
            acc_ref[...] = jnp.zeros_like(acc_ref)
        acc_ref[...] += lax.dot_general(a_ref[...], b_ref[...], (((1,), (1,)), ((), ())), preferred_element_type=F32)

        @pl.when(k == nk - 1)
        def _():
            dh = acc_ref[...]
            g_vec = g_ref[...]
            xn, r = _rms(x_ref[...])
            dsh_ref[...] += _colsum(dh)
            dsc_ref[...] += _colsum(dh * (xn * g_vec))
            da = dh * (1.0 + sc_ref[...])
            dg_ref[...] += _colsum(da * xn)
            dx = dxin_ref[...] + _rms_bwd(da * g_vec, xn, r)
            dx_ref[...] = dx
            if has_branch:
                dgate_ref[...] += _colsum(dx * br_ref[...])
                dbr_ref[...] = (gate_ref[...] * dx).astype(BF16)

    rows = pl.BlockSpec((tm, d), lambda i, k: (i, 0))
    vec = pl.BlockSpec((1, d), lambda i, k: (0, 0))
    vec_out = jax.ShapeDtypeStruct((1, d), F32)
    in_specs = [pl.BlockSpec((None, tm, tk), lambda i, k: (k // nkq, i, k % nkq)),
                pl.BlockSpec((d, tk), lambda i, k: (0, k)), _after_spec(), rows, rows, vec, vec]
    out_shape = [jax.ShapeDtypeStruct((s, d), F32), vec_out, vec_out, vec_out]
    out_specs = [rows, vec, vec, vec]
    args = [dp, w, after, x, dx_in, g, scale]
    if has_branch:
        in_specs += [rows, vec]
        out_shape += [jax.ShapeDtypeStruct((s, d), BF16), vec_out]
        out_specs += [rows, vec]
        args += [br, gate]
    return pl.pallas_call(
        body, name=name, out_shape=tuple(out_shape), grid=(s // tm, nk),
        in_specs=in_specs, out_specs=tuple(out_specs), scratch_shapes=[pltpu.VMEM((tm, d), F32)],
        compiler_params=_params("arbitrary", "arbitrary"),
    )(*args)


def _dw_body(n_t):
    def body(a_ref, b_ref, o_ref, acc_ref):
        t = pl.program_id(2)

        @pl.when(t == 0)
        def _():
            acc_ref[...] = jnp.zeros_like(acc_ref)
        acc_ref[...] += lax.dot_general(a_ref[...], b_ref[...], (((0,), (0,)), ((), ())), preferred_element_type=F32)

        @pl.when(t == n_t - 1)
        def _():
            o_ref[...] = acc_ref[...].astype(o_ref.dtype)
    return body


def _mm_dw_in(h, dp, name):
    s, d = h.shape
    nq, _, e = dp.shape
    tm, tn, tt = _tile(d, 1024), _tile(e, 1024), _tile(s, 2048)
    nj = e // tn

    return pl.pallas_call(
        _dw_body(s // tt), name=name,
        out_shape=jax.ShapeDtypeStruct((d, nq * e), BF16),
        grid=(d // tm, nq * nj, s // tt),
        in_specs=[pl.BlockSpec((tt, tm), lambda i, j, t: (t, i)),
                  pl.BlockSpec((None, tt, tn), lambda i, j, t: (j // nj, t, j % nj))],
        out_specs=pl.BlockSpec((tm, tn), lambda i, j, t: (i, j)),
        scratch_shapes=[pltpu.VMEM((tm, tn), F32)],
        compiler_params=_params("parallel", "parallel", "arbitrary"),
    )(h, dp)


def _mm_dw_out(y, dbr, name):
    s, e = y.shape
    d = dbr.shape[1]
    tm, tn, tt = _tile(e, 1024), _tile(d, 1024), _tile(s, 2048)

    return pl.pallas_call(
        _dw_body(s // tt), name=name,
        out_shape=jax.ShapeDtypeStruct((e, d), BF16),
        grid=(e // tm, d // tn, s // tt),
        in_specs=[pl.BlockSpec((tt, tm), lambda i, j, t: (t, i)), pl.BlockSpec((tt, tn), lambda i, j, t: (t, j))],
        out_specs=pl.BlockSpec((tm, tn), lambda i, j, t: (i, j)),
        scratch_shapes=[pltpu.VMEM((tm, tn), F32)],
        compiler_params=_params("parallel", "parallel", "arbitrary"),
    )(y, dbr)


def _row_spec(ts, d):
    return pl.BlockSpec((ts, d), lambda i: (i, 0))


def _vec_spec(d):
    return pl.BlockSpec((1, d), lambda i: (0, 0))


def _norm_mod(x, g, scale, shift, name):
    s, d = x.shape
    ts = _tile(s, 512)

    def body(x_ref, g_ref, sc_ref, sh_ref, h_ref):
        xn, _ = _rms(x_ref[...])
        h_ref[...] = ((xn * g_ref[...]) * (1.0 + sc_ref[...]) + sh_ref[...]).astype(BF16)

    return pl.pallas_call(
        body, name=name, out_shape=jax.ShapeDtypeStruct((s, d), BF16), grid=(s // ts,),
        in_specs=[_row_spec(ts, d), _vec_spec(d), _vec_spec(d), _vec_spec(d)],
        out_specs=_row_spec(ts, d), compiler_params=_params("parallel"),
    )(x, g, scale, shift)


def _head_tile(x1, br, gate, gf, target):
    d = x1.shape[-1]
    xn, r = _rms(x1 + gate * br)
    err = xn * gf - target
    loss = 0.5 * jnp.sum(jnp.mean(err * err, axis=-1, keepdims=True))
    dout = err * (1.0 / d)
    dx = _rms_bwd(dout * gf, xn, r)
    return dx, gate * dx, loss, _colsum(dout * xn), _colsum(dx * br)


def _f32(ref):
    return ref[...].astype(F32)


def _cols_f32(ref, cols):
    return ref[:, cols].astype(F32)


def _rows_before(halo_ref, cols=slice(None)):
    return _cols_f32(halo_ref, cols)[PACK - SUBLANE:]


def _rows_after(halo_ref, cols=slice(None)):
    return _cols_f32(halo_ref, cols)[:SUBLANE]


CONV_CHAIN_COLS = 512


def _conv_fwd(proj, conv_w, conv_b, w_out, x, gate, g_next, scale_next, shift_next, name, after):
    _, s, e = proj.shape
    d = w_out.shape[1]
    ts, te = _tile(s, 256), e
    cw = min(CONV_CHAIN_COLS, te)
    hb = ts // PACK

    def body(bg_ref, cg_ref, xi_ref, z_ref, cgp_ref, xip_ref, w0_ref, w1_ref, w2_ref, b_ref, wo_ref, after_ref,
             x_ref, gate_ref, g_ref, sc_ref, sh_ref, y_ref, br_ref, x1_ref, h_ref):
        first = pl.program_id(0) == 0
        br = None
        for c0 in range(0, te, cw):
            cols = slice(c0, c0 + cw)
            cx = _cols_f32(cg_ref, cols) * _cols_f32(xi_ref, cols)
            before = jnp.where(first, 0.0, _rows_before(cgp_ref, cols) * _rows_before(xip_ref, cols))
            conv = b_ref[:, cols] + w2_ref[:, cols] * cx
            conv = conv + w0_ref[:, cols] * _shift_down(cx, before, 2)
            conv = conv + w1_ref[:, cols] * _shift_down(cx, before, 1)
            z = _cols_f32(z_ref, cols)
            y = ((z * _sigmoid(z)) * _cols_f32(bg_ref, cols) * conv).astype(BF16)
            y_ref[:, cols] = y
            part_br = jnp.dot(y, wo_ref[cols, :], preferred_element_type=F32)
            br = part_br if br is None else br + part_br
        br_ref[...] = br
        x1 = x_ref[...] + gate_ref[...] * br
        x1_ref[...] = x1
        xn, _ = _rms(x1)
        h_ref[...] = ((xn * g_ref[...]) * (1.0 + sc_ref[...]) + sh_ref[...]).astype(BF16)

    def part(q):
        return pl.BlockSpec((None, ts, te), lambda i: (q, i, 0))

    def halo_before(q):
        return pl.BlockSpec((None, PACK, te), lambda i: (q, jnp.maximum(i * hb - 1, 0), 0))

    return pl.pallas_call(
        body, name=name,
        out_shape=(jax.ShapeDtypeStruct((s, e), BF16), jax.ShapeDtypeStruct((s, d), F32),
                   jax.ShapeDtypeStruct((s, d), F32), jax.ShapeDtypeStruct((s, d), BF16)), grid=(s // ts,),
        in_specs=[part(0), part(1), part(2), part(3), halo_before(1), halo_before(2)]
        + [_vec_spec(e)] * 4 + [pl.BlockSpec((e, d), lambda i: (0, 0)), _after_spec(), _row_spec(ts, d)]
        + [_vec_spec(d)] * 4,
        out_specs=(_row_spec(ts, e), _row_spec(ts, d), _row_spec(ts, d), _row_spec(ts, d)),
        compiler_params=_params("parallel"),
    )(proj, proj, proj, proj, proj, proj, *conv_w, conv_b, w_out, after, x, gate, g_next, scale_next, shift_next)


def _conv_bwd(proj, dbr, w_out, conv_w, conv_b, name, after):
    _, s, e = proj.shape
    d = dbr.shape[1]
    ts, te = _tile(s, 512), _tile(e, 1024)
    cw = min(CONV_CHAIN_COLS // 2, te)
    hb = ts // PACK
    n_i = s // ts
    last_halo = s // PACK - 1
    nt = (((1,), (1,)), ((), ()))

    def body(bg_ref, cg_ref, xi_ref, z_ref, dbr_ref, cgp_ref, xip_ref, bgn_ref, zn_ref, dbrn_ref, wo_ref,
             w0_ref, w1_ref, w2_ref, b_ref, after_ref, dp_ref, dw0_ref, dw1_ref, dw2_ref, db_ref):
        i = pl.program_id(1)

        @pl.when(i == 0)
        def _():
            for acc in (dw0_ref, dw1_ref, dw2_ref, db_ref):
                acc[...] = jnp.zeros_like(acc)
        for c0 in range(0, te, cw):
            cols = slice(c0, c0 + cw)
            wo = wo_ref[cols, :]
            dy = lax.dot_general(dbr_ref[...], wo, nt, preferred_element_type=F32)
            dyn = lax.dot_general(dbrn_ref[...], wo, nt, preferred_element_type=F32)[:SUBLANE]
            bg, cg = _cols_f32(bg_ref, cols), _cols_f32(cg_ref, cols)
            xi, z = _cols_f32(xi_ref, cols), _cols_f32(z_ref, cols)
            w0, w1, w2 = w0_ref[:, cols], w1_ref[:, cols], w2_ref[:, cols]
            cx = cg * xi
            before = jnp.where(i > 0, _rows_before(cgp_ref, cols) * _rows_before(xip_ref, cols), 0.0)
            cx1 = _shift_down(cx, before, 1)
            cx2 = _shift_down(cx, before, 2)
            conv = b_ref[:, cols] + w2 * cx
            conv = conv + w0 * cx2
            conv = conv + w1 * cx1
            sz, dsz = _silu_and_grad(z)
            dp_ref[3, :, cols] = (dy * bg * conv * dsz).astype(BF16)
            dp_ref[0, :, cols] = (dy * sz * conv).astype(BF16)
            dconv = dy * sz * bg
            zn = _rows_after(zn_ref, cols)
            after = jnp.where(i < n_i - 1, dyn * (zn * _sigmoid(zn)) * _rows_after(bgn_ref, cols), 0.0)
            db_ref[:, cols] += _colsum(dconv)
            dw2_ref[:, cols] += _colsum(dconv * cx)
            dw1_ref[:, cols] += _colsum(dconv * cx1)
            dw0_ref[:, cols] += _colsum(dconv * cx2)
            dcx = w2 * dconv + w1 * _shift_up(dconv, after, 1) + w0 * _shift_up(dconv, after, 2)
            dp_ref[1, :, cols] = (dcx * xi).astype(BF16)
            dp_ref[2, :, cols] = (dcx * cg).astype(BF16)

    def part(q):
        return pl.BlockSpec((None, ts, te), lambda j, i: (q, i, j))

    def halo_before(q):
        return pl.BlockSpec((None, PACK, te), lambda j, i: (q, jnp.maximum(i * hb - 1, 0), j))

    def halo_after(q):
        return pl.BlockSpec((None, PACK, te), lambda j, i: (q, jnp.minimum((i + 1) * hb, last_halo), j))

    return pl.pallas_call(
        body, name=name,
        out_shape=(jax.ShapeDtypeStruct((4, s, e), BF16),) + (jax.ShapeDtypeStruct((1, e), F32),) * 4,
        grid=(e // te, n_i),
        in_specs=[part(0), part(1), part(2), part(3), pl.BlockSpec((ts, d), lambda j, i: (i, 0)),
                  halo_before(1), halo_before(2), halo_after(0), halo_after(3),
                  pl.BlockSpec((PACK, d), lambda j, i: (jnp.minimum((i + 1) * hb, last_halo), 0)),
                  pl.BlockSpec((te, d), lambda j, i: (j, 0))]
        + [pl.BlockSpec((1, te), lambda j, i: (0, j))] * 4 + [_after_spec()],
        out_specs=(pl.BlockSpec((4, ts, te), lambda j, i: (0, i, j)),) + (pl.BlockSpec((1, te), lambda j, i: (0, j)),) * 4,
        compiler_params=_params("parallel", "arbitrary"),
    )(proj, proj, proj, proj, dbr, proj, proj, proj, proj, dbr, w_out, *conv_w, conv_b, after)


def _tril(w):
    row = lax.broadcasted_iota(jnp.int32, w.shape, 0)
    col = lax.broadcasted_iota(jnp.int32, w.shape, 1)
    return jnp.where(row >= col, w, 0.0)


def _triu(w):
    row = lax.broadcasted_iota(jnp.int32, w.shape, 0)
    col = lax.broadcasted_iota(jnp.int32, w.shape, 1)
    return jnp.where(row <= col, w, 0.0)


def _layer_norm_fwd(v, g, b):
    mu = jnp.mean(v, axis=-1, keepdims=True)
    vc = v - mu
    rstd = lax.rsqrt(jnp.mean(vc * vc, axis=-1, keepdims=True) + LN_EPS)
    vhat = vc * rstd
    return vhat * g + b, vhat, rstd


GMLP_CHUNKS_PER_STEP = 2


def _gmlp_rows(s):
    return CHUNK * min(GMLP_CHUNKS_PER_STEP, s // CHUNK)


def _mix_positions(w_ref, src_scr, dst_scr, gw, mask, bias_ref=None):
    for gi in range(GROUPS):
        cols = slice(gi * gw, (gi + 1) * gw)
        wm = mask(w_ref[gi]).astype(BF16)
        for n in range(src_scr.shape[0] // CHUNK):
            rows = slice(n * CHUNK, (n + 1) * CHUNK)
            out = jnp.dot(wm, src_scr[rows, cols], preferred_element_type=F32)
            if bias_ref is not None:
                out = out + bias_ref[:, gi:gi + 1]
            dst_scr[rows, cols] = out


def _gmlp_fwd_head(proj, ln_g, ln_b, w_s, b_s_t, w_out, x1, gate, gf, target, name):
    _, s, e = proj.shape
    d = w_out.shape[1]
    gw = e // GROUPS
    ts = _gmlp_rows(s)

    def body(pu_ref, pv_ref, pz_ref, g_ref, b_ref, ws_ref, bs_ref, wo_ref, x1_ref, gate_ref, gf_ref, tg_ref,
             y_ref, dx_ref, dbr_ref, loss_ref, dgf_ref, dgate_ref, vn_scr, mix_scr):
        @pl.when(pl.program_id(0) == 0)
        def _():
            loss_ref[...] = jnp.zeros_like(loss_ref)
            dgf_ref[...] = jnp.zeros_like(dgf_ref)
            dgate_ref[...] = jnp.zeros_like(dgate_ref)
        vn, _, _ = _layer_norm_fwd(_gelu(_f32(pv_ref)), g_ref[...], b_ref[...])
        vn_scr[...] = vn.astype(BF16)
        _mix_positions(ws_ref, vn_scr, mix_scr, gw, _tril, bs_ref)
        z = _f32(pz_ref)
        y = ((z * _sigmoid(z)) * (_gelu(_f32(pu_ref)) * mix_scr[...])).astype(BF16)
        y_ref[...] = y
        br = jnp.dot(y, wo_ref[...], preferred_element_type=F32)
        dx, dbr, loss, dgf, dgate = _head_tile(x1_ref[...], br, gate_ref[...], gf_ref[...], tg_ref[...])
        dx_ref[...] = dx
        dbr_ref[...] = dbr.astype(BF16)
        loss_ref[...] += loss
        dgf_ref[...] += dgf
        dgate_ref[...] += dgate

    def part(q):
        return pl.BlockSpec((None, ts, e), lambda i: (q, i, 0))

    return pl.pallas_call(
        body, name=name,
        out_shape=(jax.ShapeDtypeStruct((s, e), BF16), jax.ShapeDtypeStruct((s, d), F32),
                   jax.ShapeDtypeStruct((s, d), BF16), jax.ShapeDtypeStruct((SUBLANE, 128), F32),
                   jax.ShapeDtypeStruct((1, d), F32), jax.ShapeDtypeStruct((1, d), F32)),
        grid=(s // ts,),
        in_specs=[part(0), part(1), part(2), _vec_spec(e), _vec_spec(e),
                  pl.BlockSpec((GROUPS, CHUNK, CHUNK), lambda i: (0, 0, 0)),
                  pl.BlockSpec((CHUNK, GROUPS), lambda i: (0, 0)), pl.BlockSpec((e, d), lambda i: (0, 0)),
                  _row_spec(ts, d), _vec_spec(d), _vec_spec(d), _row_spec(ts, d)],
        out_specs=(_row_spec(ts, e), _row_spec(ts, d), _row_spec(ts, d),
                   pl.BlockSpec((SUBLANE, 128), lambda i: (0, 0)), _vec_spec(d), _vec_spec(d)),
        scratch_shapes=[pltpu.VMEM((ts, e), BF16), pltpu.VMEM((ts, e), F32)],
        compiler_params=_params("arbitrary"),
    )(proj, proj, proj, ln_g, ln_b, w_s, b_s_t, w_out, x1, gate, gf, target)


def _gmlp_bwd(proj, dbr, w_out, ln_g, ln_b, w_s, w_s_t, b_s_t, name):
    _, s, e = proj.shape
    d = dbr.shape[1]
    gw = e // GROUPS
    ts = _gmlp_rows(s)
    n_i = s // ts

    def body(pu_ref, pv_ref, pz_ref, dbr_ref, wo_ref, g_ref, b_ref, ws_ref, wst_ref, bs_ref,
             dp_ref, dws_ref, dbs_ref, dlg_ref, dlb_ref, vn_scr, mix_scr, dm_scr, dvn_scr, dmacc_scr):
        i = pl.program_id(0)

        @pl.when(i == 0)
        def _():
            dws_ref[...] = jnp.zeros_like(dws_ref)
            dlg_ref[...] = jnp.zeros_like(dlg_ref)
            dlb_ref[...] = jnp.zeros_like(dlb_ref)
            dmacc_scr[...] = jnp.zeros_like(dmacc_scr)
        ln_g = g_ref[...]
        u, du_dpu = _gelu_and_grad(_f32(pu_ref))
        v, dv_dpv = _gelu_and_grad(_f32(pv_ref))
        vn, vhat, rstd = _layer_norm_fwd(v, ln_g, b_ref[...])
        vn_scr[...] = vn.astype(BF16)
        _mix_positions(ws_ref, vn_scr, mix_scr, gw, _tril, bs_ref)
        mixed = mix_scr[...]
        dy = lax.dot_general(dbr_ref[...], wo_ref[...], (((1,), (1,)), ((), ())), preferred_element_type=F32)
        sz, dsz = _silu_and_grad(_f32(pz_ref))
        ds = dy * sz
        dp_ref[2] = (dy * (u * mixed) * dsz).astype(BF16)
        dp_ref[0] = (ds * mixed * du_dpu).astype(BF16)
        dm = ds * u
        dm_scr[...] = dm.astype(BF16)
        for n in range(ts // CHUNK):
            rows = slice(n * CHUNK, (n + 1) * CHUNK)
            dmacc_scr[...] += dm[rows]
            for gi in range(GROUPS):
                cols = slice(gi * gw, (gi + 1) * gw)
                dws_ref[gi] += lax.dot_general(dm_scr[rows, cols], vn_scr[rows, cols], (((1,), (1,)), ((), ())),
                                               preferred_element_type=F32)
        _mix_positions(wst_ref, dm_scr, dvn_scr, gw, _triu)
        dvn = dvn_scr[...]
        dlg_ref[...] += _colsum(dvn * vhat)
        dlb_ref[...] += _colsum(dvn)
        dvh = dvn * ln_g
        dv = rstd * (dvh - jnp.mean(dvh, axis=-1, keepdims=True) - vhat * jnp.mean(dvh * vhat, axis=-1, keepdims=True))
        dp_ref[1] = (dv * dv_dpv).astype(BF16)

        @pl.when(i == n_i - 1)
        def _():
            for gi in range(GROUPS):
                dws_ref[gi] = _tril(dws_ref[gi])
                dbs_ref[:, gi:gi + 1] = jnp.sum(dmacc_scr[:, gi * gw:(gi + 1) * gw], axis=1, keepdims=True)

    def part(q):
        return pl.BlockSpec((None, ts, e), lambda i: (q, i, 0))

    w_spec = pl.BlockSpec((GROUPS, CHUNK, CHUNK), lambda i: (0, 0, 0))
    bs_spec = pl.BlockSpec((CHUNK, GROUPS), lambda i: (0, 0))
    return pl.pallas_call(
        body, name=name,
        out_shape=(jax.ShapeDtypeStruct((3, s, e), BF16), jax.ShapeDtypeStruct((GROUPS, CHUNK, CHUNK), F32),
                   jax.ShapeDtypeStruct((CHUNK, GROUPS), F32), jax.ShapeDtypeStruct((1, e), F32),
                   jax.ShapeDtypeStruct((1, e), F32)),
        grid=(n_i,),
        in_specs=[part(0), part(1), part(2), pl.BlockSpec((ts, d), lambda i: (i, 0)),
                  pl.BlockSpec((e, d), lambda i: (0, 0)), _vec_spec(e), _vec_spec(e), w_spec, w_spec, bs_spec],
        out_specs=(pl.BlockSpec((3, ts, e), lambda i: (0, i, 0)), w_spec, bs_spec, _vec_spec(e), _vec_spec(e)),
        scratch_shapes=[pltpu.VMEM((ts, e), BF16), pltpu.VMEM((ts, e), F32), pltpu.VMEM((ts, e), BF16),
                        pltpu.VMEM((ts, e), F32), pltpu.VMEM((CHUNK, e), F32)],
        compiler_params=_params("arbitrary"),
    )(proj, proj, proj, dbr, w_out, ln_g, ln_b, w_s, w_s_t, b_s_t)


def _mod_fwd(c_all, mod_w, mod_b_cols, name):
    n_layer, d, w = mod_w.shape

    def body(c_ref, w_ref, b_ref, ca_ref, o_ref):
        c = c_ref[...]
        ca = c * _sigmoid(c)
        ca_ref[...] = ca
        for li in range(n_layer):
            o_ref[li * N_DEV:(li + 1) * N_DEV, :] = (
                jnp.dot(ca, w_ref[li], preferred_element_type=F32, precision=lax.Precision.HIGHEST) + b_ref[li])

    return pl.pallas_call(
        body, name=name,
        out_shape=(jax.ShapeDtypeStruct((N_DEV, d), F32), jax.ShapeDtypeStruct((n_layer * N_DEV, w), F32)),
        in_specs=[VMEM_SPEC] * 3, out_specs=(VMEM_SPEC, VMEM_SPEC),
        compiler_params=pltpu.CompilerParams(vmem_limit_bytes=VMEM_LIMIT),
    )(c_all, mod_w, mod_b_cols)


def _adamw(w, g, m, v):
    m = ADAM_B1 * m + (1.0 - ADAM_B1) * g
    v = ADAM_B2 * v + (1.0 - ADAM_B2) * (g * g)
    m_hat = m / (1.0 - ADAM_B1 ** ADAM_STEP)
    v_hat = v / (1.0 - ADAM_B2 ** ADAM_STEP)
    delta = -ADAM_LR * (m_hat / (jnp.sqrt(v_hat) + ADAM_EPS) + ADAM_WD * w)
    return delta, m, v


def _adamw_2d(w, g, m, v, name):
    r, c = w.shape
    tr, tc = _tile(r, 512), _tile(c, 1024)

    def body(w_ref, g_ref, m_ref, v_ref, go_ref, d_ref, nm_ref, nv_ref):
        g = g_ref[...]
        go_ref[...] = g
        d_ref[...], nm_ref[...], nv_ref[...] = _adamw(w_ref[...], g, m_ref[...], v_ref[...])

    spec = pl.BlockSpec((tr, tc), lambda i, j: (i, j))
    shape = jax.ShapeDtypeStruct((r, c), F32)
    return pl.pallas_call(
        body, name=name, out_shape=(shape,) * 4, grid=(r // tr, c // tc),
        in_specs=[spec] * 4, out_specs=(spec,) * 4, compiler_params=_params("parallel", "parallel"),
    )(w, g, m, v)


def _mod_w_update(ca_t, dmod_cols, w, m, v, name):
    n_layer, d, wd = w.shape
    tr = _tile(d, 256)

    def body(ca_ref, dm_ref, w_ref, m_ref, v_ref, g_ref, d_ref, nm_ref, nv_ref):
        ca = ca_ref[...]
        dm = dm_ref[...]
        g = ca[:, 0:1] * dm[0:1, :]
        for b in range(1, N_DEV):
            g = g + ca[:, b:b + 1] * dm[b:b + 1, :]
        g_ref[...] = g
        d_ref[...], nm_ref[...], nv_ref[...] = _adamw(w_ref[...], g, m_ref[...], v_ref[...])

    spec = pl.BlockSpec((None, tr, wd), lambda l, i: (l, i, 0))
    shape = jax.ShapeDtypeStruct((n_layer, d, wd), F32)
    return pl.pallas_call(
        body, name=name, out_shape=(shape,) * 4, grid=(n_layer, d // tr),
        in_specs=[pl.BlockSpec((tr, N_DEV), lambda l, i: (i, 0)), pl.BlockSpec((None, N_DEV, wd), lambda l, i: (l, 0, 0)),
                  spec, spec, spec],
        out_specs=(spec,) * 4, compiler_params=_params("parallel", "parallel"),
    )(ca_t, dmod_cols, w, m, v)


def _adamw_small(ws, gs, ms, vs, name):
    n = len(ws)

    def body(*refs):
        ins, outs = refs[:4 * n], refs[4 * n:]
        for k in range(n):
            delta, nm, nv = _adamw(ins[k][...], ins[n + k][...], ins[2 * n + k][...], ins[3 * n + k][...])
            outs[3 * k][...] = delta
            outs[3 * k + 1][...] = nm
            outs[3 * k + 2][...] = nv

    out_shape = []
    for w in ws:
        out_shape += [jax.ShapeDtypeStruct(w.shape, F32)] * 3
    outs = pl.pallas_call(
        body, name=name, out_shape=tuple(out_shape),
        in_specs=[VMEM_SPEC] * (4 * n), out_specs=tuple([VMEM_SPEC] * (3 * n)),
        compiler_params=pltpu.CompilerParams(vmem_limit_bytes=VMEM_LIMIT),
    )(*ws, *gs, *ms, *vs)
    return [tuple(outs[3 * k:3 * k + 3]) for k in range(n)]


def _place():
    return lax.axis_index("x"), lax.axis_index("y"), lax.axis_index("c")


def _other_chips(x, y):
    return [(1 - x, y), (x, 1 - y), (1 - x, 1 - y)]


def _all_gather(block, name, after=None):
    m_per, n = block.shape

    def body(x_ref, *rest):
        out_ref, send_sems, recv_sems, local_sem = rest[-4:]
        x, y, c = _place()
        me, sibling = (x, y, c), (x, y, 1 - c)
        chips = _other_chips(x, y)

        def rows(px, py, pc):
            return out_ref.at[pl.ds((4 * px + 2 * py + pc) * m_per, m_per), :]

        def copy(k, blk, to, src=None):
            return pltpu.make_async_remote_copy(
                src_ref=rows(*blk) if src is None else src, dst_ref=rows(*blk),
                send_sem=send_sems.at[k], recv_sem=recv_sems.at[k], device_id=to, device_id_type=MESH)

        mine = pltpu.make_async_copy(x_ref, rows(*me), local_sem)
        mine.start()
        first = [copy(0, me, sibling, src=x_ref)]
        first += [copy(1 + j, me, (*chip, c), src=x_ref) for j, chip in enumerate(chips)]
        for cp in first:
            cp.start()
        passed = [copy(4 + j, (*chip, c), sibling) for j, chip in enumerate(chips)]
        for j, chip in enumerate(chips):
            copy(1 + j, (*chip, c), me).wait_recv()
            passed[j].start()
        copy(0, sibling, me).wait_recv()
        for j, chip in enumerate(chips):
            copy(4 + j, (*chip, 1 - c), me).wait_recv()
        for cp in first + passed:
            cp.wait_send()
        mine.wait()

    return pl.pallas_call(
        body, name=name, out_shape=jax.ShapeDtypeStruct((N_DEV * m_per, n), F32),
        in_specs=[VMEM_SPEC] + ([] if after is None else [ANY_SPEC]), out_specs=VMEM_SPEC,
        scratch_shapes=[pltpu.SemaphoreType.DMA((7,)), pltpu.SemaphoreType.DMA((7,)), pltpu.SemaphoreType.DMA],
        compiler_params=pltpu.CompilerParams(vmem_limit_bytes=VMEM_LIMIT),
    )(*([block] if after is None else [block, after]))


def _hbm(a):
    return pltpu.with_memory_space_constraint(a, pltpu.HBM)


def _place_own(shard, kind, chip_idx, name):
    r, cdim = shard.shape
    tr, tc = _tile(r, 512), _tile(cdim, 1024)
    nrb, ncb = r // tr, cdim // tc

    def body(k_ref, s_ref, o_ref):
        o_ref[...] = s_ref[...].astype(BF16)

    if kind == "col":
        full, o_map = (r, N_CHIP * cdim), lambda i, j, kr: (i, kr[0] * ncb + j)
    else:
        full, o_map = (N_CHIP * r, cdim), lambda i, j, kr: (kr[0] * nrb + i, j)
    return pl.pallas_call(
        body, name=name, out_shape=jax.ShapeDtypeStruct(full, BF16),
        grid_spec=pltpu.PrefetchScalarGridSpec(
            num_scalar_prefetch=1, grid=(nrb, ncb),
            in_specs=[pl.BlockSpec((tr, tc), lambda i, j, kr: (i, j))],
            out_specs=pl.BlockSpec((tr, tc), o_map)),
        compiler_params=_params("parallel", "parallel"),
    )(chip_idx, shard)


def _weight_window(ref, kind, shard_shape, k, half):
    r, cdim = shard_shape
    hr = r // 2
    if kind == "col":
        return ref.at[pl.ds(half * hr, hr), pl.ds(pl.multiple_of(k * cdim, 128), cdim)]
    return ref.at[pl.ds(pl.multiple_of(k * r + half * hr, 2 * SUBLANE), hr), :]


def _gather_weights_start(fulls, kinds, shard_shapes, groups, after, name):
    n, ng = len(fulls), len(groups)

    def body(*refs):
        ins = refs[:n]
        sems = refs[n + 1:n + 1 + 2 * ng]
        token = refs[2 * n + 1 + 2 * ng]
        x, y, c = _place()
        chips = _other_chips(x, y)
        for g, group in enumerate(groups):
            for pos, (w, j) in enumerate(group):
                own = _weight_window(ins[w], kinds[w], shard_shapes[w], 2 * x + y, c)
                for cc in range(2):
                    pltpu.make_async_remote_copy(
                        src_ref=own, dst_ref=own,
                        send_sem=sems[2 * g].at[2 * pos + cc], recv_sem=sems[2 * g + 1].at[2 * pos + c],
                        device_id=(*chips[j], cc), device_id_type=MESH).start()
        token[...] = jnp.zeros_like(token)

    sem_shapes = []
    for group in groups:
        sem_shapes += [pltpu.SemaphoreType.DMA((2 * len(group),))] * 2
    outs = pl.pallas_call(
        body, name=name,
        out_shape=tuple(sem_shapes) + tuple(pltpu.HBM(f.shape, f.dtype) for f in fulls)
        + (jax.ShapeDtypeStruct((SUBLANE, 128), F32),),
        in_specs=[HBM_SPEC] * n + [ANY_SPEC], out_specs=(SEM_SPEC,) * (2 * ng) + (HBM_SPEC,) * n + (VMEM_SPEC,),
        input_output_aliases={w: 2 * ng + w for w in range(n)},
        compiler_params=pltpu.CompilerParams(has_side_effects=EFFECT),
    )(*[_hbm(f) for f in fulls], after)
    sems = [(outs[2 * g], outs[2 * g + 1]) for g in range(ng)]
    return sems, list(outs[2 * ng:2 * ng + n]), outs[2 * ng + n]


def _gather_weights_wait(sems, group, fulls, kinds, shard_shapes, after, name):
    n = len(fulls)

    def body(*refs):
        ins = refs[:n]
        send_sems, recv_sems = refs[n], refs[n + 1]
        x, y, c = _place()
        chips = _other_chips(x, y)
        for pos, (w, j) in enumerate(group):
            own = _weight_window(ins[w], kinds[w], shard_shapes[w], 2 * x + y, c)
            for cc in range(2):
                landed = _weight_window(ins[w], kinds[w], shard_shapes[w], 2 * chips[j][0] + chips[j][1], cc)
                k = 2 * pos + cc
                pltpu.make_async_remote_copy(
                    src_ref=own, dst_ref=own, send_sem=send_sems.at[k], recv_sem=recv_sems.at[k],
                    device_id=(*chips[j], cc), device_id_type=MESH).wait_send()
                pltpu.make_async_remote_copy(
                    src_ref=landed, dst_ref=landed, send_sem=send_sems.at[k], recv_sem=recv_sems.at[k],
                    device_id=(*chips[j], cc), device_id_type=MESH).wait_recv()

    return list(pl.pallas_call(
        body, name=name, out_shape=tuple(pltpu.HBM(f.shape, f.dtype) for f in fulls),
        in_specs=[HBM_SPEC] * n + [SEM_SPEC, SEM_SPEC, ANY_SPEC], out_specs=(HBM_SPEC,) * n,
        input_output_aliases={w: w for w in range(n)},
        compiler_params=pltpu.CompilerParams(has_side_effects=EFFECT),
    )(*fulls, sems[0], sems[1], after))


def _grad_piece(ref, kind, h, cdim, k, half):
    if kind == "col":
        return ref.at[pl.ds(half * h, h), pl.ds(pl.multiple_of(k * cdim, 128), cdim)]
    return ref.at[k, pl.ds(half * h, h), :]


def _grad_dims(g, kind):
    return (g.shape[0] // 2, g.shape[1] // N_CHIP) if kind == "col" else (g.shape[1] // 2, g.shape[2])


def _reduce_start(grads, kinds, name):
    n = len(grads)
    dims = [_grad_dims(g, kind) for g, kind in zip(grads, kinds)]
    lands = [lax.empty((N_DEV - 1, h, cdim), g.dtype) for g, (h, cdim) in zip(grads, dims)]

    def body(*refs):
        g_ins, land_ins = refs[:n], refs[n:2 * n]
        send_sems, recv_sems = refs[2 * n], refs[2 * n + 1]
        token = refs[4 * n + 2]
        x, y, c = _place()
        for w in range(n):
            h, cdim = dims[w]
            base = (N_DEV - 1) * w
            pltpu.make_async_remote_copy(
                src_ref=_grad_piece(g_ins[w], kinds[w], h, cdim, 2 * x + y, 1 - c), dst_ref=land_ins[w].at[0],
                send_sem=send_sems.at[base], recv_sem=recv_sems.at[base],
                device_id=(x, y, 1 - c), device_id_type=MESH).start()
            for j, chip in enumerate(_other_chips(x, y)):
                for cc in range(2):
                    pltpu.make_async_remote_copy(
                        src_ref=_grad_piece(g_ins[w], kinds[w], h, cdim, 2 * chip[0] + chip[1], cc),
                        dst_ref=land_ins[w].at[1 + 2 * j + c],
                        send_sem=send_sems.at[base + 1 + 2 * j + cc], recv_sem=recv_sems.at[base + 1 + 2 * j + c],
                        device_id=(*chip, cc), device_id_type=MESH).start()
        token[...] = jnp.zeros_like(token)

    sems = pltpu.SemaphoreType.DMA(((N_DEV - 1) * n,))
    outs = pl.pallas_call(
        body, name=name,
        out_shape=(sems, sems) + tuple(pltpu.HBM(a.shape, a.dtype) for a in list(grads) + lands)
        + (jax.ShapeDtypeStruct((SUBLANE, 128), F32),),
        in_specs=[HBM_SPEC] * (2 * n), out_specs=(SEM_SPEC, SEM_SPEC) + (HBM_SPEC,) * (2 * n) + (VMEM_SPEC,),
        input_output_aliases={i: 2 + i for i in range(2 * n)},
        compiler_params=pltpu.CompilerParams(has_side_effects=EFFECT),
    )(*[_hbm(a) for a in list(grads) + lands])
    return outs[0], outs[1], list(outs[2:2 + n]), list(outs[2 + n:2 + 2 * n]), outs[2 + 2 * n]


def _reduce_wait(send_sems, recv_sems, grads, lands, kinds, after, name):
    n = len(grads)
    dims = [_grad_dims(g, kind) for g, kind in zip(grads, kinds)]

    def body(*refs):
        g_ins, land_ins = refs[:n], refs[n:2 * n]
        send_sems, recv_sems = refs[2 * n], refs[2 * n + 1]
        x, y, c = _place()
        for w in range(n):
            h, cdim = dims[w]
            piece = _grad_piece(g_ins[w], kinds[w], h, cdim, 2 * x + y, c)
            for s in range(N_DEV - 1):
                k = (N_DEV - 1) * w + s
                slot = land_ins[w].at[s]
                pltpu.make_async_remote_copy(
                    src_ref=piece, dst_ref=slot, send_sem=send_sems.at[k], recv_sem=recv_sems.at[k],
                    device_id=(x, y, 1 - c), device_id_type=MESH).wait_send()
                pltpu.make_async_remote_copy(
                    src_ref=piece, dst_ref=slot, send_sem=send_sems.at[k], recv_sem=recv_sems.at[k],
                    device_id=(x, y, 1 - c), device_id_type=MESH).wait_recv()

    outs = pl.pallas_call(
        body, name=name, out_shape=tuple(pltpu.HBM(a.shape, a.dtype) for a in list(grads) + list(lands)),
        in_specs=[HBM_SPEC] * (2 * n) + [SEM_SPEC, SEM_SPEC, ANY_SPEC], out_specs=(HBM_SPEC,) * (2 * n),
        input_output_aliases={i: i for i in range(2 * n)},
        compiler_params=pltpu.CompilerParams(has_side_effects=EFFECT),
    )(*grads, *lands, send_sems, recv_sems, after)
    return list(outs[:n]), list(outs[n:])


def _add_pieces(g, land, kind, chip_idx, core_idx, name):
    _, h, cdim = land.shape
    tr, tc = _tile(h, 256), _tile(cdim, 2048)
    nrb, ncb = h // tr, cdim // tc

    def body(k_ref, c_ref, g_ref, l_ref, o_ref):
        acc = g_ref[...].astype(F32)
        for s in range(N_DEV - 1):
            acc = acc + l_ref[s].astype(F32)
        o_ref[...] = acc

    if kind == "col":
        g_spec = pl.BlockSpec((tr, tc), lambda i, j, kr, cr: (cr[0] * nrb + i, kr[0] * ncb + j))
    else:
        g_spec = pl.BlockSpec((None, tr, tc), lambda i, j, kr, cr: (kr[0], cr[0] * nrb + i, j))
    return pl.pallas_call(
        body, name=name, out_shape=jax.ShapeDtypeStruct((2 * h, cdim), F32),
        grid_spec=pltpu.PrefetchScalarGridSpec(
            num_scalar_prefetch=2, grid=(nrb, ncb),
            in_specs=[g_spec, pl.BlockSpec((N_DEV - 1, tr, tc), lambda i, j, kr, cr: (0, i, j))],
            out_specs=pl.BlockSpec((tr, tc), lambda i, j, kr, cr: (cr[0] * nrb + i, j))),
        compiler_params=_params("parallel", "parallel"),
    )(chip_idx, core_idx, g, land)


def _join_halves(shards, name):
    n = len(shards)

    def body(*refs):
        ins, outs = refs[:n], refs[n:2 * n]
        send_sems, recv_sems = refs[2 * n:]
        x, y, c = _place()
        copies = []
        for w in range(n):
            h = shards[w].shape[0] // 2
            cp = pltpu.make_async_remote_copy(
                src_ref=ins[w].at[pl.ds(c * h, h), :], dst_ref=outs[w].at[pl.ds(c * h, h), :],
                send_sem=send_sems.at[w], recv_sem=recv_sems.at[w], device_id=(x, y, 1 - c), device_id_type=MESH)
            cp.start()
            copies.append(cp)
        for w in range(n):
            h = shards[w].shape[0] // 2
            theirs = outs[w].at[pl.ds((1 - c) * h, h), :]
            pltpu.make_async_remote_copy(
                src_ref=theirs, dst_ref=theirs, send_sem=send_sems.at[w], recv_sem=recv_sems.at[w],
                device_id=(x, y, 1 - c), device_id_type=MESH).wait_recv()
        for cp in copies:
            cp.wait_send()

    return pl.pallas_call(
        body, name=name, out_shape=tuple(jax.ShapeDtypeStruct(s.shape, s.dtype) for s in shards),
        in_specs=[HBM_SPEC] * n, out_specs=tuple([HBM_SPEC] * n),
        input_output_aliases={w: w for w in range(n)},
        scratch_shapes=[pltpu.SemaphoreType.DMA((n,))] * 2,
    )(*shards)


def _sum_devices(gathered, rows, name):
    n = gathered.shape[1]

    def body(g_ref, o_ref):
        acc = g_ref[0:rows, :]
        for dev in range(1, N_DEV):
            acc = acc + g_ref[dev * rows:(dev + 1) * rows, :]
        o_ref[...] = acc

    return pl.pallas_call(
        body, name=name, out_shape=jax.ShapeDtypeStruct((rows, n), F32),
        in_specs=[VMEM_SPEC], out_specs=VMEM_SPEC,
        compiler_params=pltpu.CompilerParams(vmem_limit_bytes=VMEM_LIMIT),
    )(gathered)


def _pack(vectors, width):
    flat = [v.reshape(-1) for v in vectors]
    offsets, total = [], 0
    for f in flat:
        offsets.append(total)
        total += f.shape[0]
    rows = -(-total // (width * SUBLANE)) * SUBLANE
    flat.append(jnp.zeros((rows * width - total,), F32))
    return jnp.concatenate(flat).reshape(rows, width), offsets


def kernel(x, c, mod_w, mod_b, norm_g, a_w_in, a_conv_w, a_conv_b, a_w_out, b_w_in, b_ln_g, b_ln_b, b_w_s, b_b_s, b_w_out, final_g, loss_target, m_mod_w, m_mod_b, m_norm_g, m_a_w_in, m_a_conv_w, m_a_conv_b, m_a_w_out, m_b_w_in, m_b_ln_g, m_b_ln_b, m_b_w_s, m_b_b_s, m_b_w_out, m_final_g, v_mod_w, v_mod_b, v_norm_g, v_a_w_in, v_a_conv_w, v_a_conv_b, v_a_w_out, v_b_w_in, v_b_ln_g, v_b_ln_b, v_b_w_s, v_b_b_s, v_b_w_out, v_final_g):
    seq, d = x.shape[1], x.shape[2]
    e = a_conv_b.shape[1]
    wd = mod_w.shape[2]
    ax, ay, ac = _place()
    chip = 2 * ax + ay
    dev = 2 * chip + ac
    chip_idx = jnp.reshape(chip, (1,)).astype(jnp.int32)
    core_idx = jnp.reshape(ac, (1,)).astype(jnp.int32)

    x2d = x[0]
    target = loss_target[0]

    w_names = ["a_w_in", "a_w_out", "b_w_in", "b_w_out"]
    w_kinds = ["col", "row", "col", "row"]
    w_shards = [a_w_in[0], a_w_out[0], b_w_in[0], b_w_out[0]]
    w_shapes = [sh.shape for sh in w_shards]
    placed = [_place_own(sh, kind, chip_idx, "place_" + nm) for nm, sh, kind in zip(w_names, w_shards, w_kinds)]
    a_groups, far_groups = [[(0, 0), (0, 1)]], [[(0, 2)]]
    whole_group = [[(0, 0), (0, 1), (0, 2)]]

    es = e // N_CHIP
    packed0, offs0 = _pack([c, a_conv_w, b_ln_g, b_ln_b], 1024)
    gathered0 = _all_gather(packed0, "gather_params").reshape(N_DEV, -1)
    c_all = gathered0[:, :d]
    per_chip = gathered0[0::2]

    def from_chips_cols(k, rows_):
        got = per_chip[:, offs0[k]:offs0[k] + rows_ * es].reshape(N_CHIP, rows_, es)
        return jnp.transpose(got, (1, 0, 2)).reshape(rows_, e)

    conv_w_full = from_chips_cols(1, 3)
    conv_w = [conv_w_full[k:k + 1] for k in range(3)]
    ln_g, ln_b = from_chips_cols(2, 1), from_chips_cols(3, 1)
    mod_b_cols = lax.dynamic_slice_in_dim(mod_b, chip * wd, wd, axis=1)[:, None, :]
    c_act, mod_part = _mod_fwd(c_all, mod_w, mod_b_cols, "mod_fwd")
    n_layer = mod_w.shape[0]
    mod_gathered = _all_gather(mod_part, "gather_mod")
    a_sems, (wa_in,), a_token = _gather_weights_start(
        placed[:1], w_kinds[:1], w_shapes[:1], a_groups, mod_gathered, "gather_a_w_in_start")
    mod_all = mod_gathered.reshape(N_CHIP, 2, n_layer, N_DEV, wd)[:, 0]
    mod_all = jnp.transpose(mod_all, (1, 2, 0, 3)).reshape(n_layer, N_DEV, N_CHIP * wd)
    mod_me = lax.dynamic_index_in_dim(mod_all, dev, axis=1, keepdims=False)
    shift = [mod_me[l:l + 1, 0:d] for l in range(n_layer)]
    scale = [mod_me[l:l + 1, d:2 * d] for l in range(n_layer)]
    gate = [mod_me[l:l + 1, 2 * d:3 * d] for l in range(n_layer)]

    g0, g1, gf = norm_g[0:1], norm_g[1:2], final_g[None, :]
    h0 = _norm_mod(x2d, g0, scale[0], shift[0], "norm_mod0")

    def slab(r):
        return jnp.bitwise_xor(chip_idx, r)

    def arrived(sems, group, weight, w, after, name):
        return _gather_weights_wait(sems, [(0, j) for _, j in group], [weight], [w_kinds[w]], [w_shapes[w]],
                                    after, name)[0]

    proj0 = _mm_proj_slab(h0, wa_in, None, slab(0), 4, "a_proj_own", a_token)
    wa_in = arrived(a_sems[0], a_groups[0], wa_in, 0, proj0, "gather_wait_near")
    far_sems, (wa_in,), far_token = _gather_weights_start(
        [wa_in], w_kinds[:1], w_shapes[:1], far_groups, a_token, "gather_a_w_in_far_start")
    proj0 = _mm_proj_slab(h0, wa_in, proj0, slab(2), 4, "a_proj_x", far_token)
    proj0 = _mm_proj_slab(h0, wa_in, proj0, slab(1), 4, "a_proj_y", far_token)
    wa_in = arrived(far_sems[0], far_groups[0], wa_in, 0, proj0, "gather_wait_far")
    def start_whole(w, after, name):
        sems, (flight,), token = _gather_weights_start(
            placed[w:w + 1], w_kinds[w:w + 1], w_shapes[w:w + 1], whole_group, after, name)
        return sems[0], flight, token

    ao_sems, wa_out, ao_token = start_whole(1, wa_in, "gather_a_w_out_start")
    proj0 = _mm_proj_slab(h0, wa_in, proj0, slab(3), 4, "a_proj_far", ao_token)
    wa_out = arrived(ao_sems, whole_group[0], wa_out, 1, proj0, "gather_wait_a_w_out")
    bi_sems, wb_in, bi_token = start_whole(2, wa_out, "gather_b_w_in_start")
    y0, br0, x1, h1 = _conv_fwd(proj0, conv_w, a_conv_b, wa_out, x2d, gate[0], g1, scale[1], shift[1],
                                "conv_fwd", bi_token)
    wb_in = arrived(bi_sems, whole_group[0], wb_in, 2, h1, "gather_wait_b_w_in")
    bo_sems, wb_out, bo_token = start_whole(3, wb_in, "gather_b_w_out_start")
    proj1 = _mm_proj(h1, wb_in, 3, "b_proj", bo_token)
    b_s_t = jnp.transpose(b_b_s[0])
    wb_out = arrived(bo_sems, whole_group[0], wb_out, 3, proj1, "gather_wait_b_w_out")
    y1, dx2, dbr1, loss_part, g_final_g, dgate1 = _gmlp_fwd_head(
        proj1, ln_g, ln_b, b_w_s[0], b_s_t, wb_out, x1, gate[1], gf, target, "gmlp_fwd_head")

    gw_b_out = _mm_dw_out(y1, dbr1, "b_out_dw")
    dproj1, g_w_s, g_b_s_t, g_ln_g, g_ln_b = _gmlp_bwd(
        proj1, dbr1, wb_out, ln_g, ln_b, b_w_s[0], jnp.swapaxes(b_w_s[0], 1, 2), b_s_t, "gmlp_bwd")
    gw_b_in = _mm_dw_in(h1, dproj1, "b_proj_dw")
    b_kinds = ["col", "row"]
    b_send, b_recv, b_grads, b_lands, b_token = _reduce_start(
        [gw_b_in, gw_b_out.reshape(N_CHIP, e // N_CHIP, d)], b_kinds, "reduce_b_start")
    dx1, dshift1, dscale1, g_g1, dbr0, dgate0 = _mm_dh_norm_bwd(
        dproj1, wb_in, x1, dx2, g1, scale[1], "b_proj_dx", b_token, br=br0, gate=gate[0])

    gw_a_out = _mm_dw_out(y0, dbr0, "a_out_dw")
    ao_send, ao_recv, ao_grads, ao_lands, ao_grad_token = _reduce_start(
        [gw_a_out.reshape(N_CHIP, e // N_CHIP, d)], ["row"], "reduce_a_out_start")
    dproj0, g_w0, g_w1, g_w2, g_conv_b = _conv_bwd(proj0, dbr0, wa_out, conv_w, a_conv_b, "conv_bwd", ao_grad_token)
    gw_a_in = _mm_dw_in(h0, dproj0, "a_proj_dw")
    ai_send, ai_recv, ai_grads, ai_lands, ai_token = _reduce_start([gw_a_in], ["col"], "reduce_a_in_start")
    grad_x, dshift0, dscale0, g_g0 = _mm_dh_norm_bwd(dproj0, wa_in, x2d, dx1, g0, scale[0], "a_proj_dx", ai_token)

    def finish(send, recv, grads_, lands_, kinds_, names_, after, tag):
        grads_, lands_ = _reduce_wait(send, recv, grads_, lands_, kinds_, after, "reduce_" + tag + "_wait")
        halves = [_add_pieces(g, land, kind, chip_idx, core_idx, "add_pieces_" + nm)
                  for g, land, kind, nm in zip(grads_, lands_, kinds_, names_)]
        return _join_halves(halves, "join_" + tag)

    upd, big_grads = {}, {}

    def adamw_big(nm, w, g, m, v):
        g_out, *rest = _adamw_2d(w[0], g, m[0], v[0], "adamw_" + nm)
        big_grads[nm] = g_out[None]
        upd[nm] = tuple(o[None] for o in rest)

    g_b_w_in, g_b_w_out = finish(b_send, b_recv, b_grads, b_lands, b_kinds, ["b_w_in", "b_w_out"], grad_x, "b")
    adamw_big("b_w_in", b_w_in, g_b_w_in, m_b_w_in, v_b_w_in)
    adamw_big("b_w_out", b_w_out, g_b_w_out, m_b_w_out, v_b_w_out)

    dmod = jnp.concatenate([dshift0, dscale0, dgate0, dshift1, dscale1, dgate1], axis=1)
    small = [loss_part[0, 0:1], g_final_g, g_g0, g_g1, jnp.concatenate([g_w0, g_w1, g_w2], axis=0), g_conv_b, g_ln_g, g_ln_b,
             jnp.transpose(g_b_s_t), g_w_s, dmod]
    packed, offs = _pack(small, 1024)
    rows = packed.shape[0]
    gathered = _all_gather(packed, "gather_small", after=upd["b_w_out"][0])
    summed = _sum_devices(gathered, rows, "sum_small").reshape(-1)

    def take(k, shape):
        size = math.prod(shape)
        return summed[offs[k]:offs[k] + size].reshape(shape)

    loss = take(0, ())
    grad_final_g = take(1, (d,))
    grad_norm_g = jnp.concatenate([take(2, (1, d)), take(3, (1, d))], axis=0)
    grad_conv_w_full = take(4, (3, e))
    grad_a_conv_b = take(5, (1, e))
    grad_ln_g_full = take(6, (1, e))
    grad_ln_b_full = take(7, (1, e))
    grad_b_b_s = take(8, (1, GROUPS, CHUNK))
    grad_b_w_s = take(9, (1, GROUPS, CHUNK, CHUNK))
    grad_mod_b = take(10, (n_layer, 3 * d))
    grad_a_conv_w = lax.dynamic_slice_in_dim(grad_conv_w_full, chip * es, es, axis=1)[None]
    grad_b_ln_g = lax.dynamic_slice_in_dim(grad_ln_g_full, chip * es, es, axis=1)
    grad_b_ln_b = lax.dynamic_slice_in_dim(grad_ln_b_full, chip * es, es, axis=1)
    dmod_all = gathered.reshape(N_DEV, rows * 1024)[:, offs[10]:offs[10] + n_layer * 3 * d]
    dmod_all = dmod_all.reshape(N_DEV, n_layer, 3 * d)
    dmod_cols = jnp.transpose(lax.dynamic_slice_in_dim(dmod_all, chip * wd, wd, axis=2), (1, 0, 2))

    grad_mod_w, delta_mod_w, new_m_mod_w, new_v_mod_w = _mod_w_update(
        jnp.transpose(c_act), dmod_cols, mod_w, m_mod_w, v_mod_w, "mod_w_update")
    (g_a_w_out,) = finish(ao_send, ao_recv, ao_grads, ao_lands, ["row"], ["a_w_out"], delta_mod_w, "a_out")
    adamw_big("a_w_out", a_w_out, g_a_w_out, m_a_w_out, v_a_w_out)
    (g_a_w_in,) = finish(ai_send, ai_recv, ai_grads, ai_lands, ["col"], ["a_w_in"], upd["a_w_out"][0], "a_in")
    adamw_big("a_w_in", a_w_in, g_a_w_in, m_a_w_in, v_a_w_in)
    small_w = [("mod_b", mod_b, grad_mod_b, m_mod_b, v_mod_b), ("norm_g", norm_g, grad_norm_g, m_norm_g, v_norm_g),
               ("a_conv_w", a_conv_w, grad_a_conv_w, m_a_conv_w, v_a_conv_w),
               ("a_conv_b", a_conv_b, grad_a_conv_b, m_a_conv_b, v_a_conv_b),
               ("b_ln_g", b_ln_g, grad_b_ln_g, m_b_ln_g, v_b_ln_g), ("b_ln_b", b_ln_b, grad_b_ln_b, m_b_ln_b, v_b_ln_b),
               ("b_w_s", b_w_s, grad_b_w_s, m_b_w_s, v_b_w_s), ("b_b_s", b_b_s, grad_b_b_s, m_b_b_s, v_b_b_s),
               ("final_g", final_g, grad_final_g, m_final_g, v_final_g)]

    def flat2d(a):
        return a.reshape(-1, a.shape[-1])

    res = _adamw_small([flat2d(t[1]) for t in small_w], [flat2d(t[2]) for t in small_w],
                       [flat2d(t[3]) for t in small_w], [flat2d(t[4]) for t in small_w], "adamw_small")
    for (nm, w, _, _, _), r3 in zip(small_w, res):
        upd[nm] = tuple(o.reshape(w.shape) for o in r3)
    upd["mod_w"] = (delta_mod_w, new_m_mod_w, new_v_mod_w)

    grads = {"mod_w": grad_mod_w, "mod_b": grad_mod_b, "norm_g": grad_norm_g, "a_conv_w": grad_a_conv_w,
             "a_conv_b": grad_a_conv_b, "b_ln_g": grad_b_ln_g, "b_ln_b": grad_b_ln_b, "b_w_s": grad_b_w_s,
             "b_b_s": grad_b_b_s, "final_g": grad_final_g, **big_grads}
    order = ["mod_w", "mod_b", "norm_g", "a_w_in", "a_conv_w", "a_conv_b", "a_w_out", "b_w_in", "b_ln_g", "b_ln_b",
             "b_w_s", "b_b_s", "b_w_out", "final_g"]
    return (loss, grad_x[None], *[grads[k] for k in order], *[upd[k][0] for k in order],
            *[upd[k][1] for k in order], *[upd[k][2] for k in order])
```

```python
import functools
import math

import jax
import jax.numpy as jnp
from jax import lax
from jax.experimental import pallas as pl
from jax.experimental.pallas import tpu as pltpu

F32 = jnp.float32
BF16 = jnp.bfloat16
MESH = pl.DeviceIdType.MESH

N_DEV = 8
N_CHIP = 4
SUBLANE = 8
PACK = 16
ACT = F32
RMS_EPS = 1e-6
LN_EPS = 1e-5
CHUNK = 128
GROUPS = 8
ADAM_LR = 0.001
ADAM_B1 = 0.9
ADAM_B2 = 0.999
ADAM_EPS = 1e-08
ADAM_WD = 0.01
ADAM_STEP = 10
VMEM_LIMIT = 56 << 20

HBM_SPEC = pl.BlockSpec(memory_space=pltpu.HBM)
VMEM_SPEC = pl.BlockSpec(memory_space=pltpu.VMEM)
SEM_SPEC = pl.BlockSpec(memory_space=pltpu.SEMAPHORE)
ANY_SPEC = pl.BlockSpec(memory_space=pl.ANY)
EFFECT = pltpu.SideEffectType.DATAFLOW_SIDE_EFFECTING


def _params(*sem):
    return pltpu.CompilerParams(dimension_semantics=sem, vmem_limit_bytes=VMEM_LIMIT)


def _tile(n, want):
    if n <= want:
        return n
    t = want
    while n % t:
        t -= 128
    return t


def _sigmoid(x):
    return 0.5 * jnp.tanh(0.5 * x) + 0.5


def _silu_and_grad(x):
    s = _sigmoid(x)
    return x * s, s * (1.0 + x * (1.0 - s))


def _gelu_and_grad(x):
    cdf = 0.5 * (1.0 + lax.erf(x * (1.0 / math.sqrt(2.0))))
    pdf = jnp.exp(-0.5 * x * x) * (1.0 / math.sqrt(2.0 * math.pi))
    return x * cdf, cdf + x * pdf


def _gelu(x):
    return x * (0.5 * (1.0 + lax.erf(x * (1.0 / math.sqrt(2.0)))))


def _rms(x):
    r = lax.rsqrt(jnp.mean(x * x, axis=-1, keepdims=True) + RMS_EPS)
    return x * r, r


def _rms_bwd(dxn, xn, r):
    return r * (dxn - xn * jnp.mean(dxn * xn, axis=-1, keepdims=True))


def _colsum(a):
    return jnp.sum(a, axis=0, keepdims=True)


def _shift_down(cur, before, k):
    rolled = pltpu.roll(cur, k, 0)
    row = lax.broadcasted_iota(jnp.int32, before.shape, 0)
    head = jnp.where(row < k, pltpu.roll(before, k, 0), rolled[:SUBLANE])
    return jnp.concatenate([head, rolled[SUBLANE:]], axis=0)


def _shift_up(cur, after, k):
    n = cur.shape[0]
    rolled = pltpu.roll(cur, n - k, 0)
    row = lax.broadcasted_iota(jnp.int32, after.shape, 0)
    tail = jnp.where(row >= SUBLANE - k, pltpu.roll(after, SUBLANE - k, 0), rolled[n - SUBLANE:])
    return jnp.concatenate([rolled[:n - SUBLANE], tail], axis=0)


def _after_spec():
    return pl.BlockSpec((SUBLANE, 128), lambda *_: (0, 0))


def _mm_proj(h, w, n_split, name, after):
    s, d = h.shape
    e = w.shape[1] // n_split
    tm, tn = _tile(s, 1024), _tile(e, 2048)
    nj = e // tn

    def body(h_ref, w_ref, after_ref, o_ref):
        o_ref[...] = jnp.dot(h_ref[...], w_ref[...], preferred_element_type=F32).astype(ACT)

    return pl.pallas_call(
        body, name=name,
        out_shape=jax.ShapeDtypeStruct((n_split, s, e), ACT),
        grid=(s // tm, n_split * nj),
        in_specs=[pl.BlockSpec((tm, d), lambda i, j: (i, 0)), pl.BlockSpec((d, tn), lambda i, j: (0, j)),
                  _after_spec()],
        out_specs=pl.BlockSpec((None, tm, tn), lambda i, j: (j // nj, i, j % nj)),
        compiler_params=_params("parallel", "parallel"),
    )(h, w, after)


def _mm_proj_slab(h, w, proj, q_idx, n_split, name, after):
    s, d = h.shape
    e = w.shape[1] // n_split
    tm, tn = _tile(s, 1024), _tile(e, 2048)
    nj = e // tn

    def body(q_ref, h_ref, w_ref, *rest):
        o_ref = rest[-1]
        o_ref[...] = jnp.dot(h_ref[...], w_ref[...], preferred_element_type=F32).astype(ACT)

    in_specs = [pl.BlockSpec((tm, d), lambda i, j, qr: (i, 0)), pl.BlockSpec((d, tn), lambda i, j, qr: (0, qr[0] * nj + j)),
                _after_spec()]
    args = [q_idx, h, w, after]
    aliases = {}
    if proj is not None:
        in_specs.append(ANY_SPEC)
        args.append(proj)
        aliases = {4: 0}
    return pl.pallas_call(
        body, name=name,
        out_shape=jax.ShapeDtypeStruct((n_split, s, e), ACT),
        grid_spec=pltpu.PrefetchScalarGridSpec(
            num_scalar_prefetch=1, grid=(s // tm, nj), in_specs=in_specs,
            out_specs=pl.BlockSpec((None, tm, tn), lambda i, j, qr: (qr[0], i, j))),
        input_output_aliases=aliases,
        compiler_params=_params("parallel", "parallel"),
    )(*args)


def _mm_dh_norm_bwd(dp, w, x, dx_in, g, scale, name, after, br=None, gate=None):
    nq, s, e = dp.shape
    d = w.shape[0]
    has_branch = br is not None
    tm, tk = _tile(s, 1024), _tile(e, 1024 if has_branch else 2048)
    nkq = e // tk
    nk = nq * nkq

    def body(*refs):
        if has_branch:
            (a_ref, b_ref, after_ref, x_ref, dxin_ref, g_ref, sc_ref, br_ref, gate_ref,
             dx_ref, dsh_ref, dsc_ref, dg_ref, dbr_ref, dgate_ref, acc_ref) = refs
        else:
            (a_ref, b_ref, after_ref, x_ref, dxin_ref, g_ref, sc_ref,
             dx_ref, dsh_ref, dsc_ref, dg_ref, acc_ref) = refs
        i, k = pl.program_id(0), pl.program_id(1)

        @pl.when(jnp.logical_and(i == 0, k == 0))
        def _():
            dsh_ref[...] = jnp.zeros_like(dsh_ref)
            dsc_ref[...] = jnp.zeros_like(dsc_ref)
            dg_ref[...] = jnp.zeros_like(dg_ref)
            if has_branch:
                dgate_ref[...] = jnp.zeros_like(dgate_ref)

        @pl.when(k == 0)
        def _():
            acc_ref[...] = jnp.zeros_like(acc_ref)
        acc_ref[...] += lax.dot_general(a_ref[...], b_ref[...], (((1,), (1,)), ((), ())), preferred_element_type=F32)

        @pl.when(k == nk - 1)
        def _():
            dh = acc_ref[...]
            g_vec = g_ref[...]
            xn, r = _rms(x_ref[...])
            dsh_ref[...] += _colsum(dh)
            dsc_ref[...] += _colsum(dh * (xn * g_vec))
            da = dh * (1.0 + sc_ref[...])
            dg_ref[...] += _colsum(da * xn)
            dx = dxin_ref[...] + _rms_bwd(da * g_vec, xn, r)
            dx_ref[...] = dx
            if has_branch:
                dgate_ref[...] += _colsum(dx * br_ref[...])
                dbr_ref[...] = (gate_ref[...] * dx).astype(BF16)

    rows = pl.BlockSpec((tm, d), lambda i, k: (i, 0))
    vec = pl.BlockSpec((1, d), lambda i, k: (0, 0))
    vec_out = jax.ShapeDtypeStruct((1, d), F32)
    in_specs = [pl.BlockSpec((None, tm, tk), lambda i, k: (k // nkq, i, k % nkq)),
                pl.BlockSpec((d, tk), lambda i, k: (0, k)), _after_spec(), rows, rows, vec, vec]
    out_shape = [jax.ShapeDtypeStruct((s, d), F32), vec_out, vec_out, vec_out]
    out_specs = [rows, vec, vec, vec]
    args = [dp, w, after, x, dx_in, g, scale]
    if has_branch:
        in_specs += [rows, vec]
        out_shape += [jax.ShapeDtypeStruct((s, d), BF16), vec_out]
        out_specs += [rows, vec]
        args += [br, gate]
    return pl.pallas_call(
        body, name=name, out_shape=tuple(out_shape), grid=(s // tm, nk),
        in_specs=in_specs, out_specs=tuple(out_specs), scratch_shapes=[pltpu.VMEM((tm, d), F32)],
        compiler_params=_params("arbitrary", "arbitrary"),
    )(*args)


def _dw_body(n_t):
    def body(a_ref, b_ref, o_ref, acc_ref):
        t = pl.program_id(2)

        @pl.when(t == 0)
        def _():
            acc_ref[...] = jnp.zeros_like(acc_ref)
        acc_ref[...] += lax.dot_general(a_ref[...], b_ref[...], (((0,), (0,)), ((), ())), preferred_element_type=F32)

        @pl.when(t == n_t - 1)
        def _():
            o_ref[...] = acc_ref[...].astype(o_ref.dtype)
    return body


def _mm_dw_in(h, dp, name):
    s, d = h.shape
    nq, _, e = dp.shape
    tm, tn, tt = _tile(d, 1024), _tile(e, 1024), _tile(s, 2048)
    nj = e // tn

    return pl.pallas_call(
        _dw_body(s // tt), name=name,
        out_shape=jax.ShapeDtypeStruct((d, nq * e), BF16),
        grid=(d // tm, nq * nj, s // tt),
        in_specs=[pl.BlockSpec((tt, tm), lambda i, j, t: (t, i)),
                  pl.BlockSpec((None, tt, tn), lambda i, j, t: (j // nj, t, j % nj))],
        out_specs=pl.BlockSpec((tm, tn), lambda i, j, t: (i, j)),
        scratch_shapes=[pltpu.VMEM((tm, tn), F32)],
        compiler_params=_params("parallel", "parallel", "arbitrary"),
    )(h, dp)


def _mm_dw_out(y, dbr, name):
    s, e = y.shape
    d = dbr.shape[1]
    tm, tn, tt = _tile(e, 1024), _tile(d, 1024), _tile(s, 2048)

    return pl.pallas_call(
        _dw_body(s // tt), name=name,
        out_shape=jax.ShapeDtypeStruct((e, d), BF16),
        grid=(e // tm, d // tn, s // tt),
        in_specs=[pl.BlockSpec((tt, tm), lambda i, j, t: (t, i)), pl.BlockSpec((tt, tn), lambda i, j, t: (t, j))],
        out_specs=pl.BlockSpec((tm, tn), lambda i, j, t: (i, j)),
        scratch_shapes=[pltpu.VMEM((tm, tn), F32)],
        compiler_params=_params("parallel", "parallel", "arbitrary"),
    )(y, dbr)


def _row_spec(ts, d):
    return pl.BlockSpec((ts, d), lambda i: (i, 0))


def _vec_spec(d):
    return pl.BlockSpec((1, d), lambda i: (0, 0))


def _norm_mod(x, g, scale, shift, name):
    s, d = x.shape
    ts = _tile(s, 512)

    def body(x_ref, g_ref, sc_ref, sh_ref, h_ref):
        xn, _ = _rms(x_ref[...])
        h_ref[...] = ((xn * g_ref[...]) * (1.0 + sc_ref[...]) + sh_ref[...]).astype(BF16)

    return pl.pallas_call(
        body, name=name, out_shape=jax.ShapeDtypeStruct((s, d), BF16), grid=(s // ts,),
        in_specs=[_row_spec(ts, d), _vec_spec(d), _vec_spec(d), _vec_spec(d)],
        out_specs=_row_spec(ts, d), compiler_params=_params("parallel"),
    )(x, g, scale, shift)


def _head_tile(x1, br, gate, gf, target):
    d = x1.shape[-1]
    xn, r = _rms(x1 + gate * br)
    err = xn * gf - target
    loss = 0.5 * jnp.sum(jnp.mean(err * err, axis=-1, keepdims=True))
    dout = err * (1.0 / d)
    dx = _rms_bwd(dout * gf, xn, r)
    return dx, gate * dx, loss, _colsum(dout * xn), _colsum(dx * br)


def _f32(ref):
    return ref[...].astype(F32)


def _cols_f32(ref, cols):
    return ref[:, cols].astype(F32)


def _rows_before(halo_ref, cols=slice(None)):
    return _cols_f32(halo_ref, cols)[PACK - SUBLANE:]


def _rows_after(halo_ref, cols=slice(None)):
    return _cols_f32(halo_ref, cols)[:SUBLANE]


CONV_CHAIN_COLS = 512


def _conv_fwd(proj, conv_w, conv_b, w_out, x, gate, g_next, scale_next, shift_next, name, after):
    _, s, e = proj.shape
    d = w_out.shape[1]
    ts, te = _tile(s, 256), e
    cw = min(CONV_CHAIN_COLS, te)
    hb = ts // PACK

    def body(bg_ref, cg_ref, xi_ref, z_ref, cgp_ref, xip_ref, w0_ref, w1_ref, w2_ref, b_ref, wo_ref, after_ref,
             x_ref, gate_ref, g_ref, sc_ref, sh_ref, y_ref, br_ref, x1_ref, h_ref):
        first = pl.program_id(0) == 0
        br = None
        for c0 in range(0, te, cw):
            cols = slice(c0, c0 + cw)
            cx = _cols_f32(cg_ref, cols) * _cols_f32(xi_ref, cols)
            before = jnp.where(first, 0.0, _rows_before(cgp_ref, cols) * _rows_before(xip_ref, cols))
            conv = b_ref[:, cols] + w2_ref[:, cols] * cx
            conv = conv + w0_ref[:, cols] * _shift_down(cx, before, 2)
            conv = conv + w1_ref[:, cols] * _shift_down(cx, before, 1)
            z = _cols_f32(z_ref, cols)
            y = ((z * _sigmoid(z)) * _cols_f32(bg_ref, cols) * conv).astype(BF16)
            y_ref[:, cols] = y
            part_br = jnp.dot(y, wo_ref[cols, :], preferred_element_type=F32)
            br = part_br if br is None else br + part_br
        br_ref[...] = br
        x1 = x_ref[...] + gate_ref[...] * br
        x1_ref[...] = x1
        xn, _ = _rms(x1)
        h_ref[...] = ((xn * g_ref[...]) * (1.0 + sc_ref[...]) + sh_ref[...]).astype(BF16)

    def part(q):
        return pl.BlockSpec((None, ts, te), lambda i: (q, i, 0))

    def halo_before(q):
        return pl.BlockSpec((None, PACK, te), lambda i: (q, jnp.maximum(i * hb - 1, 0), 0))

    return pl.pallas_call(
        body, name=name,
        out_shape=(jax.ShapeDtypeStruct((s, e), BF16), jax.ShapeDtypeStruct((s, d), F32),
                   jax.ShapeDtypeStruct((s, d), F32), jax.ShapeDtypeStruct((s, d), BF16)), grid=(s // ts,),
        in_specs=[part(0), part(1), part(2), part(3), halo_before(1), halo_before(2)]
        + [_vec_spec(e)] * 4 + [pl.BlockSpec((e, d), lambda i: (0, 0)), _after_spec(), _row_spec(ts, d)]
        + [_vec_spec(d)] * 4,
        out_specs=(_row_spec(ts, e), _row_spec(ts, d), _row_spec(ts, d), _row_spec(ts, d)),
        compiler_params=_params("parallel"),
    )(proj, proj, proj, proj, proj, proj, *conv_w, conv_b, w_out, after, x, gate, g_next, scale_next, shift_next)


def _conv_bwd(proj, dbr, w_out, conv_w, conv_b, name, after):
    _, s, e = proj.shape
    d = dbr.shape[1]
    ts, te = _tile(s, 512), _tile(e, 1024)
    cw = min(CONV_CHAIN_COLS // 2, te)
    hb = ts // PACK
    n_i = s // ts
    last_halo = s // PACK - 1
    nt = (((1,), (1,)), ((), ()))

    def body(bg_ref, cg_ref, xi_ref, z_ref, dbr_ref, cgp_ref, xip_ref, bgn_ref, zn_ref, dbrn_ref, wo_ref,
             w0_ref, w1_ref, w2_ref, b_ref, after_ref, dp_ref, dw0_ref, dw1_ref, dw2_ref, db_ref):
        i = pl.program_id(1)

        @pl.when(i == 0)
        def _():
            for acc in (dw0_ref, dw1_ref, dw2_ref, db_ref):
                acc[...] = jnp.zeros_like(acc)
        for c0 in range(0, te, cw):
            cols = slice(c0, c0 + cw)
            wo = wo_ref[cols, :]
            dy = lax.dot_general(dbr_ref[...], wo, nt, preferred_element_type=F32)
            dyn = lax.dot_general(dbrn_ref[...], wo, nt, preferred_element_type=F32)[:SUBLANE]
            bg, cg = _cols_f32(bg_ref, cols), _cols_f32(cg_ref, cols)
            xi, z = _cols_f32(xi_ref, cols), _cols_f32(z_ref, cols)
            w0, w1, w2 = w0_ref[:, cols], w1_ref[:, cols], w2_ref[:, cols]
            cx = cg * xi
            before = jnp.where(i > 0, _rows_before(cgp_ref, cols) * _rows_before(xip_ref, cols), 0.0)
            cx1 = _shift_down(cx, before, 1)
            cx2 = _shift_down(cx, before, 2)
            conv = b_ref[:, cols] + w2 * cx
            conv = conv + w0 * cx2
            conv = conv + w1 * cx1
            sz, dsz = _silu_and_grad(z)
            dp_ref[3, :, cols] = (dy * bg * conv * dsz).astype(BF16)
            dp_ref[0, :, cols] = (dy * sz * conv).astype(BF16)
            dconv = dy * sz * bg
            zn = _rows_after(zn_ref, cols)
            after = jnp.where(i < n_i - 1, dyn * (zn * _sigmoid(zn)) * _rows_after(bgn_ref, cols), 0.0)
            db_ref[:, cols] += _colsum(dconv)
            dw2_ref[:, cols] += _colsum(dconv * cx)
            dw1_ref[:, cols] += _colsum(dconv * cx1)
            dw0_ref[:, cols] += _colsum(dconv * cx2)
            dcx = w2 * dconv + w1 * _shift_up(dconv, after, 1) + w0 * _shift_up(dconv, after, 2)
            dp_ref[1, :, cols] = (dcx * xi).astype(BF16)
            dp_ref[2, :, cols] = (dcx * cg).astype(BF16)

    def part(q):
        return pl.BlockSpec((None, ts, te), lambda j, i: (q, i, j))

    def halo_before(q):
        return pl.BlockSpec((None, PACK, te), lambda j, i: (q, jnp.maximum(i * hb - 1, 0), j))

    def halo_after(q):
        return pl.BlockSpec((None, PACK, te), lambda j, i: (q, jnp.minimum((i + 1) * hb, last_halo), j))

    return pl.pallas_call(
        body, name=name,
        out_shape=(jax.ShapeDtypeStruct((4, s, e), BF16),) + (jax.ShapeDtypeStruct((1, e), F32),) * 4,
        grid=(e // te, n_i),
        in_specs=[part(0), part(1), part(2), part(3), pl.BlockSpec((ts, d), lambda j, i: (i, 0)),
                  halo_before(1), halo_before(2), halo_after(0), halo_after(3),
                  pl.BlockSpec((PACK, d), lambda j, i: (jnp.minimum((i + 1) * hb, last_halo), 0)),
                  pl.BlockSpec((te, d), lambda j, i: (j, 0))]
        + [pl.BlockSpec((1, te), lambda j, i: (0, j))] * 4 + [_after_spec()],
        out_specs=(pl.BlockSpec((4, ts, te), lambda j, i: (0, i, j)),) + (pl.BlockSpec((1, te), lambda j, i: (0, j)),) * 4,
        compiler_params=_params("parallel", "arbitrary"),
    )(proj, proj, proj, proj, dbr, proj, proj, proj, proj, dbr, w_out, *conv_w, conv_b, after)


def _tril(w):
    row = lax.broadcasted_iota(jnp.int32, w.shape, 0)
    col = lax.broadcasted_iota(jnp.int32, w.shape, 1)
    return jnp.where(row >= col, w, 0.0)


def _triu(w):
    row = lax.broadcasted_iota(jnp.int32, w.shape, 0)
    col = lax.broadcasted_iota(jnp.int32, w.shape, 1)
    return jnp.where(row <= col, w, 0.0)


def _layer_norm_fwd(v, g, b):
    mu = jnp.mean(v, axis=-1, keepdims=True)
    vc = v - mu
    rstd = lax.rsqrt(jnp.mean(vc * vc, axis=-1, keepdims=True) + LN_EPS)
    vhat = vc * rstd
    return vhat * g + b, vhat, rstd


GMLP_CHUNKS_PER_STEP = 2


def _gmlp_rows(s):
    return CHUNK * min(GMLP_CHUNKS_PER_STEP, s // CHUNK)


def _mix_positions(w_ref, src_scr, dst_scr, gw, mask, bias_ref=None):
    for gi in range(GROUPS):
        cols = slice(gi * gw, (gi + 1) * gw)
        wm = mask(w_ref[gi]).astype(BF16)
        for n in range(src_scr.shape[0] // CHUNK):
            rows = slice(n * CHUNK, (n + 1) * CHUNK)
            out = jnp.dot(wm, src_scr[rows, cols], preferred_element_type=F32)
            if bias_ref is not None:
                out = out + bias_ref[:, gi:gi + 1]
            dst_scr[rows, cols] = out


def _gmlp_fwd_head(proj, ln_g, ln_b, w_s, b_s_t, w_out, x1, gate, gf, target, name):
    _, s, e = proj.shape
    d = w_out.shape[1]
    gw = e // GROUPS
    ts = _gmlp_rows(s)

    def body(pu_ref, pv_ref, pz_ref, g_ref, b_ref, ws_ref, bs_ref, wo_ref, x1_ref, gate_ref, gf_ref, tg_ref,
             y_ref, dx_ref, dbr_ref, loss_ref, dgf_ref, dgate_ref, vn_scr, mix_scr):
        @pl.when(pl.program_id(0) == 0)
        def _():
            loss_ref[...] = jnp.zeros_like(loss_ref)
            dgf_ref[...] = jnp.zeros_like(dgf_ref)
            dgate_ref[...] = jnp.zeros_like(dgate_ref)
        vn, _, _ = _layer_norm_fwd(_gelu(_f32(pv_ref)), g_ref[...], b_ref[...])
        vn_scr[...] = vn.astype(BF16)
        _mix_positions(ws_ref, vn_scr, mix_scr, gw, _tril, bs_ref)
        z = _f32(pz_ref)
        y = ((z * _sigmoid(z)) * (_gelu(_f32(pu_ref)) * mix_scr[...])).astype(BF16)
        y_ref[...] = y
        br = jnp.dot(y, wo_ref[...], preferred_element_type=F32)
        dx, dbr, loss, dgf, dgate = _head_tile(x1_ref[...], br, gate_ref[...], gf_ref[...], tg_ref[...])
        dx_ref[...] = dx
        dbr_ref[...] = dbr.astype(BF16)
        loss_ref[...] += loss
        dgf_ref[...] += dgf
        dgate_ref[...] += dgate

    def part(q):
        return pl.BlockSpec((None, ts, e), lambda i: (q, i, 0))

    return pl.pallas_call(
        body, name=name,
        out_shape=(jax.ShapeDtypeStruct((s, e), BF16), jax.ShapeDtypeStruct((s, d), F32),
                   jax.ShapeDtypeStruct((s, d), BF16), jax.ShapeDtypeStruct((SUBLANE, 128), F32),
                   jax.ShapeDtypeStruct((1, d), F32), jax.ShapeDtypeStruct((1, d), F32)),
        grid=(s // ts,),
        in_specs=[part(0), part(1), part(2), _vec_spec(e), _vec_spec(e),
                  pl.BlockSpec((GROUPS, CHUNK, CHUNK), lambda i: (0, 0, 0)),
                  pl.BlockSpec((CHUNK, GROUPS), lambda i: (0, 0)), pl.BlockSpec((e, d), lambda i: (0, 0)),
                  _row_spec(ts, d), _vec_spec(d), _vec_spec(d), _row_spec(ts, d)],
        out_specs=(_row_spec(ts, e), _row_spec(ts, d), _row_spec(ts, d),
                   pl.BlockSpec((SUBLANE, 128), lambda i: (0, 0)), _vec_spec(d), _vec_spec(d)),
        scratch_shapes=[pltpu.VMEM((ts, e), BF16), pltpu.VMEM((ts, e), F32)],
        compiler_params=_params("arbitrary"),
    )(proj, proj, proj, ln_g, ln_b, w_s, b_s_t, w_out, x1, gate, gf, target)


def _gmlp_bwd(proj, dbr, w_out, ln_g, ln_b, w_s, w_s_t, b_s_t, name):
    _, s, e = proj.shape
    d = dbr.shape[1]
    gw = e // GROUPS
    ts = _gmlp_rows(s)
    n_i = s // ts

    def body(pu_ref, pv_ref, pz_ref, dbr_ref, wo_ref, g_ref, b_ref, ws_ref, wst_ref, bs_ref,
             dp_ref, dws_ref, dbs_ref, dlg_ref, dlb_ref, vn_scr, mix_scr, dm_scr, dvn_scr, dmacc_scr):
        i = pl.program_id(0)

        @pl.when(i == 0)
        def _():
            dws_ref[...] = jnp.zeros_like(dws_ref)
            dlg_ref[...] = jnp.zeros_like(dlg_ref)
            dlb_ref[...] = jnp.zeros_like(dlb_ref)
            dmacc_scr[...] = jnp.zeros_like(dmacc_scr)
        ln_g = g_ref[...]
        u, du_dpu = _gelu_and_grad(_f32(pu_ref))
        v, dv_dpv = _gelu_and_grad(_f32(pv_ref))
        vn, vhat, rstd = _layer_norm_fwd(v, ln_g, b_ref[...])
        vn_scr[...] = vn.astype(BF16)
        _mix_positions(ws_ref, vn_scr, mix_scr, gw, _tril, bs_ref)
        mixed = mix_scr[...]
        dy = lax.dot_general(dbr_ref[...], wo_ref[...], (((1,), (1,)), ((), ())), preferred_element_type=F32)
        sz, dsz = _silu_and_grad(_f32(pz_ref))
        ds = dy * sz
        dp_ref[2] = (dy * (u * mixed) * dsz).astype(BF16)
        dp_ref[0] = (ds * mixed * du_dpu).astype(BF16)
        dm = ds * u
        dm_scr[...] = dm.astype(BF16)
        for n in range(ts // CHUNK):
            rows = slice(n * CHUNK, (n + 1) * CHUNK)
            dmacc_scr[...] += dm[rows]
            for gi in range(GROUPS):
                cols = slice(gi * gw, (gi + 1) * gw)
                dws_ref[gi] += lax.dot_general(dm_scr[rows, cols], vn_scr[rows, cols], (((1,), (1,)), ((), ())),
                                               preferred_element_type=F32)
        _mix_positions(wst_ref, dm_scr, dvn_scr, gw, _triu)
        dvn = dvn_scr[...]
        dlg_ref[...] += _colsum(dvn * vhat)
        dlb_ref[...] += _colsum(dvn)
        dvh = dvn * ln_g
        dv = rstd * (dvh - jnp.mean(dvh, axis=-1, keepdims=True) - vhat * jnp.mean(dvh * vhat, axis=-1, keepdims=True))
        dp_ref[1] = (dv * dv_dpv).astype(BF16)

        @pl.when(i == n_i - 1)
        def _():
            for gi in range(GROUPS):
                dws_ref[gi] = _tril(dws_ref[gi])
                dbs_ref[:, gi:gi + 1] = jnp.sum(dmacc_scr[:, gi * gw:(gi + 1) * gw], axis=1, keepdims=True)

    def part(q):
        return pl.BlockSpec((None, ts, e), lambda i: (q, i, 0))

    w_spec = pl.BlockSpec((GROUPS, CHUNK, CHUNK), lambda i: (0, 0, 0))
    bs_spec = pl.BlockSpec((CHUNK, GROUPS), lambda i: (0, 0))
    return pl.pallas_call(
        body, name=name,
        out_shape=(jax.ShapeDtypeStruct((3, s, e), BF16), jax.ShapeDtypeStruct((GROUPS, CHUNK, CHUNK), F32),
                   jax.ShapeDtypeStruct((CHUNK, GROUPS), F32), jax.ShapeDtypeStruct((1, e), F32),
                   jax.ShapeDtypeStruct((1, e), F32)),
        grid=(n_i,),
        in_specs=[part(0), part(1), part(2), pl.BlockSpec((ts, d), lambda i: (i, 0)),
                  pl.BlockSpec((e, d), lambda i: (0, 0)), _vec_spec(e), _vec_spec(e), w_spec, w_spec, bs_spec],
        out_specs=(pl.BlockSpec((3, ts, e), lambda i: (0, i, 0)), w_spec, bs_spec, _vec_spec(e), _vec_spec(e)),
        scratch_shapes=[pltpu.VMEM((ts, e), BF16), pltpu.VMEM((ts, e), F32), pltpu.VMEM((ts, e), BF16),
                        pltpu.VMEM((ts, e), F32), pltpu.VMEM((CHUNK, e), F32)],
        compiler_params=_params("arbitrary"),
    )(proj, proj, proj, dbr, w_out, ln_g, ln_b, w_s, w_s_t, b_s_t)


def _mod_fwd(c_all, mod_w, mod_b_cols, name):
    n_layer, d, w = mod_w.shape

    def body(c_ref, w_ref, b_ref, ca_ref, o_ref):
        c = c_ref[...]
        ca = c * _sigmoid(c)
        ca_ref[...] = ca
        for li in range(n_layer):
            o_ref[li * N_DEV:(li + 1) * N_DEV, :] = (
                jnp.dot(ca, w_ref[li], preferred_element_type=F32, precision=lax.Precision.HIGHEST) + b_ref[li])

    return pl.pallas_call(
        body, name=name,
        out_shape=(jax.ShapeDtypeStruct((N_DEV, d), F32), jax.ShapeDtypeStruct((n_layer * N_DEV, w), F32)),
        in_specs=[VMEM_SPEC] * 3, out_specs=(VMEM_SPEC, VMEM_SPEC),
        compiler_params=pltpu.CompilerParams(vmem_limit_bytes=VMEM_LIMIT),
    )(c_all, mod_w, mod_b_cols)


def _adamw(w, g, m, v):
    m = ADAM_B1 * m + (1.0 - ADAM_B1) * g
    v = ADAM_B2 * v + (1.0 - ADAM_B2) * (g * g)
    m_hat = m / (1.0 - ADAM_B1 ** ADAM_STEP)
    v_hat = v / (1.0 - ADAM_B2 ** ADAM_STEP)
    delta = -ADAM_LR * (m_hat / (jnp.sqrt(v_hat) + ADAM_EPS) + ADAM_WD * w)
    return delta, m, v


def _adamw_2d(w, g, m, v, name):
    r, c = w.shape
    tr, tc = _tile(r, 512), _tile(c, 1024)

    def body(w_ref, g_ref, m_ref, v_ref, go_ref, d_ref, nm_ref, nv_ref):
        g = g_ref[...]
        go_ref[...] = g
        d_ref[...], nm_ref[...], nv_ref[...] = _adamw(w_ref[...], g, m_ref[...], v_ref[...])

    spec = pl.BlockSpec((tr, tc), lambda i, j: (i, j))
    shape = jax.ShapeDtypeStruct((r, c), F32)
    return pl.pallas_call(
        body, name=name, out_shape=(shape,) * 4, grid=(r // tr, c // tc),
        in_specs=[spec] * 4, out_specs=(spec,) * 4, compiler_params=_params("parallel", "parallel"),
    )(w, g, m, v)


def _mod_w_update(ca_t, dmod_cols, w, m, v, name):
    n_layer, d, wd = w.shape
    tr = _tile(d, 256)

    def body(ca_ref, dm_ref, w_ref, m_ref, v_ref, g_ref, d_ref, nm_ref, nv_ref):
        ca = ca_ref[...]
        dm = dm_ref[...]
        g = ca[:, 0:1] * dm[0:1, :]
        for b in range(1, N_DEV):
            g = g + ca[:, b:b + 1] * dm[b:b + 1, :]
        g_ref[...] = g
        d_ref[...], nm_ref[...], nv_ref[...] = _adamw(w_ref[...], g, m_ref[...], v_ref[...])

    spec = pl.BlockSpec((None, tr, wd), lambda l, i: (l, i, 0))
    shape = jax.ShapeDtypeStruct((n_layer, d, wd), F32)
    return pl.pallas_call(
        body, name=name, out_shape=(shape,) * 4, grid=(n_layer, d // tr),
        in_specs=[pl.BlockSpec((tr, N_DEV), lambda l, i: (i, 0)), pl.BlockSpec((None, N_DEV, wd), lambda l, i: (l, 0, 0)),
                  spec, spec, spec],
        out_specs=(spec,) * 4, compiler_params=_params("parallel", "parallel"),
    )(ca_t, dmod_cols, w, m, v)


def _adamw_small(ws, gs, ms, vs, name):
    n = len(ws)

    def body(*refs):
        ins, outs = refs[:4 * n], refs[4 * n:]
        for k in range(n):
            delta, nm, nv = _adamw(ins[k][...], ins[n + k][...], ins[2 * n + k][...], ins[3 * n + k][...])
            outs[3 * k][...] = delta
            outs[3 * k + 1][...] = nm
            outs[3 * k + 2][...] = nv

    out_shape = []
    for w in ws:
        out_shape += [jax.ShapeDtypeStruct(w.shape, F32)] * 3
    outs = pl.pallas_call(
        body, name=name, out_shape=tuple(out_shape),
        in_specs=[VMEM_SPEC] * (4 * n), out_specs=tuple([VMEM_SPEC] * (3 * n)),
        compiler_params=pltpu.CompilerParams(vmem_limit_bytes=VMEM_LIMIT),
    )(*ws, *gs, *ms, *vs)
    return [tuple(outs[3 * k:3 * k + 3]) for k in range(n)]


def _place():
    return lax.axis_index("x"), lax.axis_index("y"), lax.axis_index("c")


def _other_chips(x, y):
    return [(1 - x, y), (x, 1 - y), (1 - x, 1 - y)]


def _all_gather(block, name, after=None):
    m_per, n = block.shape

    def body(x_ref, *rest):
        out_ref, send_sems, recv_sems, local_sem = rest[-4:]
        x, y, c = _place()
        me, sibling = (x, y, c), (x, y, 1 - c)
        chips = _other_chips(x, y)

        def rows(px, py, pc):
            return out_ref.at[pl.ds((4 * px + 2 * py + pc) * m_per, m_per), :]

        def copy(k, blk, to, src=None):
            return pltpu.make_async_remote_copy(
                src_ref=rows(*blk) if src is None else src, dst_ref=rows(*blk),
                send_sem=send_sems.at[k], recv_sem=recv_sems.at[k], device_id=to, device_id_type=MESH)

        mine = pltpu.make_async_copy(x_ref, rows(*me), local_sem)
        mine.start()
        first = [copy(0, me, sibling, src=x_ref)]
        first += [copy(1 + j, me, (*chip, c), src=x_ref) for j, chip in enumerate(chips)]
        for cp in first:
            cp.start()
        passed = [copy(4 + j, (*chip, c), sibling) for j, chip in enumerate(chips)]
        for j, chip in enumerate(chips):
            copy(1 + j, (*chip, c), me).wait_recv()
            passed[j].start()
        copy(0, sibling, me).wait_recv()
        for j, chip in enumerate(chips):
            copy(4 + j, (*chip, 1 - c), me).wait_recv()
        for cp in first + passed:
            cp.wait_send()
        mine.wait()

    return pl.pallas_call(
        body, name=name, out_shape=jax.ShapeDtypeStruct((N_DEV * m_per, n), F32),
        in_specs=[VMEM_SPEC] + ([] if after is None else [ANY_SPEC]), out_specs=VMEM_SPEC,
        scratch_shapes=[pltpu.SemaphoreType.DMA((7,)), pltpu.SemaphoreType.DMA((7,)), pltpu.SemaphoreType.DMA],
        compiler_params=pltpu.CompilerParams(vmem_limit_bytes=VMEM_LIMIT),
    )(*([block] if after is None else [block, after]))


def _hbm(a):
    return pltpu.with_memory_space_constraint(a, pltpu.HBM)


def _place_own(shard, kind, chip_idx, name):
    r, cdim = shard.shape
    tr, tc = _tile(r, 512), _tile(cdim, 1024)
    nrb, ncb = r // tr, cdim // tc

    def body(k_ref, s_ref, o_ref):
        o_ref[...] = s_ref[...].astype(BF16)

    if kind == "col":
        full, o_map = (r, N_CHIP * cdim), lambda i, j, kr: (i, kr[0] * ncb + j)
    else:
        full, o_map = (N_CHIP * r, cdim), lambda i, j, kr: (kr[0] * nrb + i, j)
    return pl.pallas_call(
        body, name=name, out_shape=jax.ShapeDtypeStruct(full, BF16),
        grid_spec=pltpu.PrefetchScalarGridSpec(
            num_scalar_prefetch=1, grid=(nrb, ncb),
            in_specs=[pl.BlockSpec((tr, tc), lambda i, j, kr: (i, j))],
            out_specs=pl.BlockSpec((tr, tc), o_map)),
        compiler_params=_params("parallel", "parallel"),
    )(chip_idx, shard)


def _weight_window(ref, kind, shard_shape, k, half):
    r, cdim = shard_shape
    hr = r // 2
    if kind == "col":
        return ref.at[pl.ds(half * hr, hr), pl.ds(pl.multiple_of(k * cdim, 128), cdim)]
    return ref.at[pl.ds(pl.multiple_of(k * r + half * hr, 2 * SUBLANE), hr), :]


def _target_cores(c, both_cores):
    return [(cc, cc, c) for cc in range(2)] if both_cores else [(c, 0, 0)]


def _gather_weights_start(fulls, kinds, shard_shapes, groups, after, name, both_cores=True):
    n, ng = len(fulls), len(groups)
    per = 2 if both_cores else 1

    def body(*refs):
        ins = refs[:n]
        sems = refs[n + 1:n + 1 + 2 * ng]
        token = refs[2 * n + 1 + 2 * ng]
        x, y, c = _place()
        chips = _other_chips(x, y)
        for g, group in enumerate(groups):
            for pos, (w, j) in enumerate(group):
                own = _weight_window(ins[w], kinds[w], shard_shapes[w], 2 * x + y, c)
                for cc, mine, theirs in _target_cores(c, both_cores):
                    pltpu.make_async_remote_copy(
                        src_ref=own, dst_ref=own,
                        send_sem=sems[2 * g].at[per * pos + mine], recv_sem=sems[2 * g + 1].at[per * pos + theirs],
                        device_id=(*chips[j], cc), device_id_type=MESH).start()
        token[...] = jnp.zeros_like(token)

    sem_shapes = []
    for group in groups:
        sem_shapes += [pltpu.SemaphoreType.DMA((per * len(group),))] * 2
    outs = pl.pallas_call(
        body, name=name,
        out_shape=tuple(sem_shapes) + tuple(pltpu.HBM(f.shape, f.dtype) for f in fulls)
        + (jax.ShapeDtypeStruct((SUBLANE, 128), F32),),
        in_specs=[HBM_SPEC] * n + [ANY_SPEC], out_specs=(SEM_SPEC,) * (2 * ng) + (HBM_SPEC,) * n + (VMEM_SPEC,),
        input_output_aliases={w: 2 * ng + w for w in range(n)},
        compiler_params=pltpu.CompilerParams(has_side_effects=EFFECT),
    )(*[_hbm(f) for f in fulls], after)
    sems = [(outs[2 * g], outs[2 * g + 1]) for g in range(ng)]
    return sems, list(outs[2 * ng:2 * ng + n]), outs[2 * ng + n]


def _gather_weights_wait(sems, group, fulls, kinds, shard_shapes, after, name, both_cores=True):
    n = len(fulls)
    per = 2 if both_cores else 1

    def body(*refs):
        ins = refs[:n]
        send_sems, recv_sems = refs[n], refs[n + 1]
        x, y, c = _place()
        chips = _other_chips(x, y)
        for pos, (w, j) in enumerate(group):
            own = _weight_window(ins[w], kinds[w], shard_shapes[w], 2 * x + y, c)
            for cc, mine, _ in _target_cores(c, both_cores):
                landed = _weight_window(ins[w], kinds[w], shard_shapes[w], 2 * chips[j][0] + chips[j][1], cc)
                k = per * pos + mine
                pltpu.make_async_remote_copy(
                    src_ref=own, dst_ref=own, send_sem=send_sems.at[k], recv_sem=recv_sems.at[k],
                    device_id=(*chips[j], cc), device_id_type=MESH).wait_send()
                pltpu.make_async_remote_copy(
                    src_ref=landed, dst_ref=landed, send_sem=send_sems.at[k], recv_sem=recv_sems.at[k],
                    device_id=(*chips[j], cc), device_id_type=MESH).wait_recv()

    return list(pl.pallas_call(
        body, name=name, out_shape=tuple(pltpu.HBM(f.shape, f.dtype) for f in fulls),
        in_specs=[HBM_SPEC] * n + [SEM_SPEC, SEM_SPEC, ANY_SPEC], out_specs=(HBM_SPEC,) * n,
        input_output_aliases={w: w for w in range(n)},
        compiler_params=pltpu.CompilerParams(has_side_effects=EFFECT),
    )(*fulls, sems[0], sems[1], after))


def _forward_to_sibling(full, kind, shard_shape, relations, name):
    nr = len(relations)

    def body(in_ref, out_ref, send_sems, recv_sems):
        x, y, c = _place()
        chips = _other_chips(x, y)
        copies = []
        for pos, j in enumerate(relations):
            k = 2 * chips[j][0] + chips[j][1]
            mine = _weight_window(in_ref, kind, shard_shape, k, c)
            cp = pltpu.make_async_remote_copy(
                src_ref=mine, dst_ref=_weight_window(out_ref, kind, shard_shape, k, c),
                send_sem=send_sems.at[pos], recv_sem=recv_sems.at[pos], device_id=(x, y, 1 - c), device_id_type=MESH)
            cp.start()
            copies.append(cp)
        for pos, j in enumerate(relations):
            theirs = _weight_window(out_ref, kind, shard_shape, 2 * chips[j][0] + chips[j][1], 1 - c)
            pltpu.make_async_remote_copy(
                src_ref=theirs, dst_ref=theirs, send_sem=send_sems.at[pos], recv_sem=recv_sems.at[pos],
                device_id=(x, y, 1 - c), device_id_type=MESH).wait_recv()
        for cp in copies:
            cp.wait_send()

    return pl.pallas_call(
        body, name=name, out_shape=jax.ShapeDtypeStruct(full.shape, full.dtype),
        in_specs=[HBM_SPEC], out_specs=HBM_SPEC, input_output_aliases={0: 0},
        scratch_shapes=[pltpu.SemaphoreType.DMA((nr,))] * 2,
    )(full)


def _grad_piece(ref, kind, h, cdim, k, half):
    if kind == "col":
        return ref.at[pl.ds(half * h, h), pl.ds(pl.multiple_of(k * cdim, 128), cdim)]
    return ref.at[k, pl.ds(half * h, h), :]


def _grad_dims(g, kind):
    return (g.shape[0] // 2, g.shape[1] // N_CHIP) if kind == "col" else (g.shape[1] // 2, g.shape[2])


def _reduce_start(grads, kinds, name):
    n = len(grads)
    dims = [_grad_dims(g, kind) for g, kind in zip(grads, kinds)]
    lands = [lax.empty((N_DEV - 1, h, cdim), g.dtype) for g, (h, cdim) in zip(grads, dims)]

    def body(*refs):
        g_ins, land_ins = refs[:n], refs[n:2 * n]
        send_sems, recv_sems = refs[2 * n], refs[2 * n + 1]
        token = refs[4 * n + 2]
        x, y, c = _place()
        for w in range(n):
            h, cdim = dims[w]
            base = (N_DEV - 1) * w
            pltpu.make_async_remote_copy(
                src_ref=_grad_piece(g_ins[w], kinds[w], h, cdim, 2 * x + y, 1 - c), dst_ref=land_ins[w].at[0],
                send_sem=send_sems.at[base], recv_sem=recv_sems.at[base],
                device_id=(x, y, 1 - c), device_id_type=MESH).start()
            for j, chip in enumerate(_other_chips(x, y)):
                for cc in range(2):
                    pltpu.make_async_remote_copy(
                        src_ref=_grad_piece(g_ins[w], kinds[w], h, cdim, 2 * chip[0] + chip[1], cc),
                        dst_ref=land_ins[w].at[1 + 2 * j + c],
                        send_sem=send_sems.at[base + 1 + 2 * j + cc], recv_sem=recv_sems.at[base + 1 + 2 * j + c],
                        device_id=(*chip, cc), device_id_type=MESH).start()
        token[...] = jnp.zeros_like(token)

    sems = pltpu.SemaphoreType.DMA(((N_DEV - 1) * n,))
    outs = pl.pallas_call(
        body, name=name,
        out_shape=(sems, sems) + tuple(pltpu.HBM(a.shape, a.dtype) for a in list(grads) + lands)
        + (jax.ShapeDtypeStruct((SUBLANE, 128), F32),),
        in_specs=[HBM_SPEC] * (2 * n), out_specs=(SEM_SPEC, SEM_SPEC) + (HBM_SPEC,) * (2 * n) + (VMEM_SPEC,),
        input_output_aliases={i: 2 + i for i in range(2 * n)},
        compiler_params=pltpu.CompilerParams(has_side_effects=EFFECT),
    )(*[_hbm(a) for a in list(grads) + lands])
    return outs[0], outs[1], list(outs[2:2 + n]), list(outs[2 + n:2 + 2 * n]), outs[2 + 2 * n]


def _reduce_wait(send_sems, recv_sems, grads, lands, kinds, after, name):
    n = len(grads)
    dims = [_grad_dims(g, kind) for g, kind in zip(grads, kinds)]

    def body(*refs):
        g_ins, land_ins = refs[:n], refs[n:2 * n]
        send_sems, recv_sems = refs[2 * n], refs[2 * n + 1]
        x, y, c = _place()
        for w in range(n):
            h, cdim = dims[w]
            piece = _grad_piece(g_ins[w], kinds[w], h, cdim, 2 * x + y, c)
            for s in range(N_DEV - 1):
                k = (N_DEV - 1) * w + s
                slot = land_ins[w].at[s]
                pltpu.make_async_remote_copy(
                    src_ref=piece, dst_ref=slot, send_sem=send_sems.at[k], recv_sem=recv_sems.at[k],
                    device_id=(x, y, 1 - c), device_id_type=MESH).wait_send()
                pltpu.make_async_remote_copy(
                    src_ref=piece, dst_ref=slot, send_sem=send_sems.at[k], recv_sem=recv_sems.at[k],
                    device_id=(x, y, 1 - c), device_id_type=MESH).wait_recv()

    outs = pl.pallas_call(
        body, name=name, out_shape=tuple(pltpu.HBM(a.shape, a.dtype) for a in list(grads) + list(lands)),
        in_specs=[HBM_SPEC] * (2 * n) + [SEM_SPEC, SEM_SPEC, ANY_SPEC], out_specs=(HBM_SPEC,) * (2 * n),
        input_output_aliases={i: i for i in range(2 * n)},
        compiler_params=pltpu.CompilerParams(has_side_effects=EFFECT),
    )(*grads, *lands, send_sems, recv_sems, after)
    return list(outs[:n]), list(outs[n:])


def _add_pieces(g, land, kind, chip_idx, core_idx, name):
    _, h, cdim = land.shape
    tr, tc = _tile(h, 256), _tile(cdim, 2048)
    nrb, ncb = h // tr, cdim // tc

    def body(k_ref, c_ref, g_ref, l_ref, o_ref):
        acc = g_ref[...].astype(F32)
        for s in range(N_DEV - 1):
            acc = acc + l_ref[s].astype(F32)
        o_ref[...] = acc

    if kind == "col":
        g_spec = pl.BlockSpec((tr, tc), lambda i, j, kr, cr: (cr[0] * nrb + i, kr[0] * ncb + j))
    else:
        g_spec = pl.BlockSpec((None, tr, tc), lambda i, j, kr, cr: (kr[0], cr[0] * nrb + i, j))
    return pl.pallas_call(
        body, name=name, out_shape=jax.ShapeDtypeStruct((2 * h, cdim), F32),
        grid_spec=pltpu.PrefetchScalarGridSpec(
            num_scalar_prefetch=2, grid=(nrb, ncb),
            in_specs=[g_spec, pl.BlockSpec((N_DEV - 1, tr, tc), lambda i, j, kr, cr: (0, i, j))],
            out_specs=pl.BlockSpec((tr, tc), lambda i, j, kr, cr: (cr[0] * nrb + i, j))),
        compiler_params=_params("parallel", "parallel"),
    )(chip_idx, core_idx, g, land)


def _join_halves(shards, name):
    n = len(shards)

    def body(*refs):
        ins, outs = refs[:n], refs[n:2 * n]
        send_sems, recv_sems = refs[2 * n:]
        x, y, c = _place()
        copies = []
        for w in range(n):
            h = shards[w].shape[0] // 2
            cp = pltpu.make_async_remote_copy(
                src_ref=ins[w].at[pl.ds(c * h, h), :], dst_ref=outs[w].at[pl.ds(c * h, h), :],
                send_sem=send_sems.at[w], recv_sem=recv_sems.at[w], device_id=(x, y, 1 - c), device_id_type=MESH)
            cp.start()
            copies.append(cp)
        for w in range(n):
            h = shards[w].shape[0] // 2
            theirs = outs[w].at[pl.ds((1 - c) * h, h), :]
            pltpu.make_async_remote_copy(
                src_ref=theirs, dst_ref=theirs, send_sem=send_sems.at[w], recv_sem=recv_sems.at[w],
                device_id=(x, y, 1 - c), device_id_type=MESH).wait_recv()
        for cp in copies:
            cp.wait_send()

    return pl.pallas_call(
        body, name=name, out_shape=tuple(jax.ShapeDtypeStruct(s.shape, s.dtype) for s in shards),
        in_specs=[HBM_SPEC] * n, out_specs=tuple([HBM_SPEC] * n),
        input_output_aliases={w: w for w in range(n)},
        scratch_shapes=[pltpu.SemaphoreType.DMA((n,))] * 2,
    )(*shards)


def _sum_devices(gathered, rows, name):
    n = gathered.shape[1]

    def body(g_ref, o_ref):
        acc = g_ref[0:rows, :]
        for dev in range(1, N_DEV):
            acc = acc + g_ref[dev * rows:(dev + 1) * rows, :]
        o_ref[...] = acc

    return pl.pallas_call(
        body, name=name, out_shape=jax.ShapeDtypeStruct((rows, n), F32),
        in_specs=[VMEM_SPEC], out_specs=VMEM_SPEC,
        compiler_params=pltpu.CompilerParams(vmem_limit_bytes=VMEM_LIMIT),
    )(gathered)


def _pack(vectors, width):
    flat = [v.reshape(-1) for v in vectors]
    offsets, total = [], 0
    for f in flat:
        offsets.append(total)
        total += f.shape[0]
    rows = -(-total // (width * SUBLANE)) * SUBLANE
    flat.append(jnp.zeros((rows * width - total,), F32))
    return jnp.concatenate(flat).reshape(rows, width), offsets


def kernel(x, c, mod_w, mod_b, norm_g, a_w_in, a_conv_w, a_conv_b, a_w_out, b_w_in, b_ln_g, b_ln_b, b_w_s, b_b_s, b_w_out, final_g, loss_target, m_mod_w, m_mod_b, m_norm_g, m_a_w_in, m_a_conv_w, m_a_conv_b, m_a_w_out, m_b_w_in, m_b_ln_g, m_b_ln_b, m_b_w_s, m_b_b_s, m_b_w_out, m_final_g, v_mod_w, v_mod_b, v_norm_g, v_a_w_in, v_a_conv_w, v_a_conv_b, v_a_w_out, v_b_w_in, v_b_ln_g, v_b_ln_b, v_b_w_s, v_b_b_s, v_b_w_out, v_final_g):
    seq, d = x.shape[1], x.shape[2]
    e = a_conv_b.shape[1]
    wd = mod_w.shape[2]
    ax, ay, ac = _place()
    chip = 2 * ax + ay
    dev = 2 * chip + ac
    chip_idx = jnp.reshape(chip, (1,)).astype(jnp.int32)
    core_idx = jnp.reshape(ac, (1,)).astype(jnp.int32)

    x2d = x[0]
    target = loss_target[0]

    w_names = ["a_w_in", "a_w_out", "b_w_in", "b_w_out"]
    w_kinds = ["col", "row", "col", "row"]
    w_shards = [a_w_in[0], a_w_out[0], b_w_in[0], b_w_out[0]]
    w_shapes = [sh.shape for sh in w_shards]
    placed = [_place_own(sh, kind, chip_idx, "place_" + nm) for nm, sh, kind in zip(w_names, w_shards, w_kinds)]
    a_groups, far_groups = [[(0, 0), (0, 1)]], [[(0, 2)]]
    whole_group = [[(0, 0), (0, 1), (0, 2)]]

    es = e // N_CHIP
    packed0, offs0 = _pack([c, a_conv_w, b_ln_g, b_ln_b], 1024)
    gathered0 = _all_gather(packed0, "gather_params").reshape(N_DEV, -1)
    c_all = gathered0[:, :d]
    per_chip = gathered0[0::2]

    def from_chips_cols(k, rows_):
        got = per_chip[:, offs0[k]:offs0[k] + rows_ * es].reshape(N_CHIP, rows_, es)
        return jnp.transpose(got, (1, 0, 2)).reshape(rows_, e)

    conv_w_full = from_chips_cols(1, 3)
    conv_w = [conv_w_full[k:k + 1] for k in range(3)]
    ln_g, ln_b = from_chips_cols(2, 1), from_chips_cols(3, 1)
    mod_b_cols = lax.dynamic_slice_in_dim(mod_b, chip * wd, wd, axis=1)[:, None, :]
    c_act, mod_part = _mod_fwd(c_all, mod_w, mod_b_cols, "mod_fwd")
    n_layer = mod_w.shape[0]
    mod_gathered = _all_gather(mod_part, "gather_mod")
    a_sems, (wa_in,), a_token = _gather_weights_start(
        placed[:1], w_kinds[:1], w_shapes[:1], a_groups, mod_gathered, "gather_a_w_in_start", both_cores=False)
    mod_all = mod_gathered.reshape(N_CHIP, 2, n_layer, N_DEV, wd)[:, 0]
    mod_all = jnp.transpose(mod_all, (1, 2, 0, 3)).reshape(n_layer, N_DEV, N_CHIP * wd)
    mod_me = lax.dynamic_index_in_dim(mod_all, dev, axis=1, keepdims=False)
    shift = [mod_me[l:l + 1, 0:d] for l in range(n_layer)]
    scale = [mod_me[l:l + 1, d:2 * d] for l in range(n_layer)]
    gate = [mod_me[l:l + 1, 2 * d:3 * d] for l in range(n_layer)]

    g0, g1, gf = norm_g[0:1], norm_g[1:2], final_g[None, :]
    h0 = _norm_mod(x2d, g0, scale[0], shift[0], "norm_mod0")

    def slab(r):
        return jnp.bitwise_xor(chip_idx, r)

    def arrived(sems, group, weight, w, after, name, both_cores=True):
        return _gather_weights_wait(sems, [(0, j) for _, j in group], [weight], [w_kinds[w]], [w_shapes[w]],
                                    after, name, both_cores)[0]

    proj0 = _mm_proj_slab(h0, wa_in, None, slab(0), 4, "a_proj_own", a_token)
    wa_in = arrived(a_sems[0], a_groups[0], wa_in, 0, proj0, "gather_wait_near", both_cores=False)
    wa_in = _forward_to_sibling(wa_in, w_kinds[0], w_shapes[0], [0, 1], "forward_near")
    far_sems, (wa_in,), far_token = _gather_weights_start(
        [wa_in], w_kinds[:1], w_shapes[:1], far_groups, a_token, "gather_a_w_in_far_start")
    proj0 = _mm_proj_slab(h0, wa_in, proj0, slab(2), 4, "a_proj_x", far_token)
    proj0 = _mm_proj_slab(h0, wa_in, proj0, slab(1), 4, "a_proj_y", far_token)
    wa_in = arrived(far_sems[0], far_groups[0], wa_in, 0, proj0, "gather_wait_far")
    def start_whole(w, after, name):
        sems, (flight,), token = _gather_weights_start(
            placed[w:w + 1], w_kinds[w:w + 1], w_shapes[w:w + 1], whole_group, after, name)
        return sems[0], flight, token

    ao_sems, wa_out, ao_token = start_whole(1, wa_in, "gather_a_w_out_start")
    proj0 = _mm_proj_slab(h0, wa_in, proj0, slab(3), 4, "a_proj_far", ao_token)
    wa_out = arrived(ao_sems, whole_group[0], wa_out, 1, proj0, "gather_wait_a_w_out")
    bi_sems, wb_in, bi_token = start_whole(2, wa_out, "gather_b_w_in_start")
    y0, br0, x1, h1 = _conv_fwd(proj0, conv_w, a_conv_b, wa_out, x2d, gate[0], g1, scale[1], shift[1],
                                "conv_fwd", bi_token)
    wb_in = arrived(bi_sems, whole_group[0], wb_in, 2, h1, "gather_wait_b_w_in")
    bo_sems, wb_out, bo_token = start_whole(3, wb_in, "gather_b_w_out_start")
    proj1 = _mm_proj(h1, wb_in, 3, "b_proj", bo_token)
    b_s_t = jnp.transpose(b_b_s[0])
    wb_out = arrived(bo_sems, whole_group[0], wb_out, 3, proj1, "gather_wait_b_w_out")
    y1, dx2, dbr1, loss_part, g_final_g, dgate1 = _gmlp_fwd_head(
        proj1, ln_g, ln_b, b_w_s[0], b_s_t, wb_out, x1, gate[1], gf, target, "gmlp_fwd_head")

    gw_b_out = _mm_dw_out(y1, dbr1, "b_out_dw")
    dproj1, g_w_s, g_b_s_t, g_ln_g, g_ln_b = _gmlp_bwd(
        proj1, dbr1, wb_out, ln_g, ln_b, b_w_s[0], jnp.swapaxes(b_w_s[0], 1, 2), b_s_t, "gmlp_bwd")
    gw_b_in = _mm_dw_in(h1, dproj1, "b_proj_dw")
    b_kinds = ["col", "row"]
    b_send, b_recv, b_grads, b_lands, b_token = _reduce_start(
        [gw_b_in, gw_b_out.reshape(N_CHIP, e // N_CHIP, d)], b_kinds, "reduce_b_start")
    dx1, dshift1, dscale1, g_g1, dbr0, dgate0 = _mm_dh_norm_bwd(
        dproj1, wb_in, x1, dx2, g1, scale[1], "b_proj_dx", b_token, br=br0, gate=gate[0])

    gw_a_out = _mm_dw_out(y0, dbr0, "a_out_dw")
    ao_send, ao_recv, ao_grads, ao_lands, ao_grad_token = _reduce_start(
        [gw_a_out.reshape(N_CHIP, e // N_CHIP, d)], ["row"], "reduce_a_out_start")
    dproj0, g_w0, g_w1, g_w2, g_conv_b = _conv_bwd(proj0, dbr0, wa_out, conv_w, a_conv_b, "conv_bwd", ao_grad_token)
    gw_a_in = _mm_dw_in(h0, dproj0, "a_proj_dw")
    ai_send, ai_recv, ai_grads, ai_lands, ai_token = _reduce_start([gw_a_in], ["col"], "reduce_a_in_start")
    grad_x, dshift0, dscale0, g_g0 = _mm_dh_norm_bwd(dproj0, wa_in, x2d, dx1, g0, scale[0], "a_proj_dx", ai_token)

    def finish(send, recv, grads_, lands_, kinds_, names_, after, tag):
        grads_, lands_ = _reduce_wait(send, recv, grads_, lands_, kinds_, after, "reduce_" + tag + "_wait")
        halves = [_add_pieces(g, land, kind, chip_idx, core_idx, "add_pieces_" + nm)
                  for g, land, kind, nm in zip(grads_, lands_, kinds_, names_)]
        return _join_halves(halves, "join_" + tag)

    upd, big_grads = {}, {}

    def adamw_big(nm, w, g, m, v):
        g_out, *rest = _adamw_2d(w[0], g, m[0], v[0], "adamw_" + nm)
        big_grads[nm] = g_out[None]
        upd[nm] = tuple(o[None] for o in rest)

    g_b_w_in, g_b_w_out = finish(b_send, b_recv, b_grads, b_lands, b_kinds, ["b_w_in", "b_w_out"], grad_x, "b")
    adamw_big("b_w_in", b_w_in, g_b_w_in, m_b_w_in, v_b_w_in)
    adamw_big("b_w_out", b_w_out, g_b_w_out, m_b_w_out, v_b_w_out)

    dmod = jnp.concatenate([dshift0, dscale0, dgate0, dshift1, dscale1, dgate1], axis=1)
    small = [loss_part[0, 0:1], g_final_g, g_g0, g_g1, jnp.concatenate([g_w0, g_w1, g_w2], axis=0), g_conv_b, g_ln_g, g_ln_b,
             jnp.transpose(g_b_s_t), g_w_s, dmod]
    packed, offs = _pack(small, 1024)
    rows = packed.shape[0]
    gathered = _all_gather(packed, "gather_small", after=upd["b_w_out"][0])
    summed = _sum_devices(gathered, rows, "sum_small").reshape(-1)

    def take(k, shape):
        size = math.prod(shape)
        return summed[offs[k]:offs[k] + size].reshape(shape)

    loss = take(0, ())
    grad_final_g = take(1, (d,))
    grad_norm_g = jnp.concatenate([take(2, (1, d)), take(3, (1, d))], axis=0)
    grad_conv_w_full = take(4, (3, e))
    grad_a_conv_b = take(5, (1, e))
    grad_ln_g_full = take(6, (1, e))
    grad_ln_b_full = take(7, (1, e))
    grad_b_b_s = take(8, (1, GROUPS, CHUNK))
    grad_b_w_s = take(9, (1, GROUPS, CHUNK, CHUNK))
    grad_mod_b = take(10, (n_layer, 3 * d))
    grad_a_conv_w = lax.dynamic_slice_in_dim(grad_conv_w_full, chip * es, es, axis=1)[None]
    grad_b_ln_g = lax.dynamic_slice_in_dim(grad_ln_g_full, chip * es, es, axis=1)
    grad_b_ln_b = lax.dynamic_slice_in_dim(grad_ln_b_full, chip * es, es, axis=1)
    dmod_all = gathered.reshape(N_DEV, rows * 1024)[:, offs[10]:offs[10] + n_layer * 3 * d]
    dmod_all = dmod_all.reshape(N_DEV, n_layer, 3 * d)
    dmod_cols = jnp.transpose(lax.dynamic_slice_in_dim(dmod_all, chip * wd, wd, axis=2), (1, 0, 2))

    grad_mod_w, delta_mod_w, new_m_mod_w, new_v_mod_w = _mod_w_update(
        jnp.transpose(c_act), dmod_cols, mod_w, m_mod_w, v_mod_w, "mod_w_update")
    (g_a_w_out,) = finish(ao_send, ao_recv, ao_grads, ao_lands, ["row"], ["a_w_out"], delta_mod_w, "a_out")
    adamw_big("a_w_out", a_w_out, g_a_w_out, m_a_w_out, v_a_w_out)
    (g_a_w_in,) = finish(ai_send, ai_recv, ai_grads, ai_lands, ["col"], ["a_w_in"], upd["a_w_out"][0], "a_in")
    adamw_big("a_w_in", a_w_in, g_a_w_in, m_a_w_in, v_a_w_in)
    small_w = [("mod_b", mod_b, grad_mod_b, m_mod_b, v_mod_b), ("norm_g", norm_g, grad_norm_g, m_norm_g, v_norm_g),
               ("a_conv_w", a_conv_w, grad_a_conv_w, m_a_conv_w, v_a_conv_w),
               ("a_conv_b", a_conv_b, grad_a_conv_b, m_a_conv_b, v_a_conv_b),
               ("b_ln_g", b_ln_g, grad_b_ln_g, m_b_ln_g, v_b_ln_g), ("b_ln_b", b_ln_b, grad_b_ln_b, m_b_ln_b, v_b_ln_b),
               ("b_w_s", b_w_s, grad_b_w_s, m_b_w_s, v_b_w_s), ("b_b_s", b_b_s, grad_b_b_s, m_b_b_s, v_b_b_s),
               ("final_g", final_g, grad_final_g, m_final_g, v_final_g)]

    def flat2d(a):
        return a.reshape(-1, a.shape[-1])

    res = _adamw_small([flat2d(t[1]) for t in small_w], [flat2d(t[2]) for t in small_w],
                       [flat2d(t[3]) for t in small_w], [flat2d(t[4]) for t in small_w], "adamw_small")
    for (nm, w, _, _, _), r3 in zip(small_w, res):
        upd[nm] = tuple(o.reshape(w.shape) for o in r3)
    upd["mod_w"] = (delta_mod_w, new_m_mod_w, new_v_mod_w)

    grads = {"mod_w": grad_mod_w, "mod_b": grad_mod_b, "norm_g": grad_norm_g, "a_conv_w": grad_a_conv_w,
             "a_conv_b": grad_a_conv_b, "b_ln_g": grad_b_ln_g, "b_ln_b": grad_b_ln_b, "b_w_s": grad_b_w_s,
             "b_b_s": grad_b_b_s, "final_g": grad_final_g, **big_grads}
    order = ["mod_w", "mod_b", "norm_g", "a_w_in", "a_conv_w", "a_conv_b", "a_w_out", "b_w_in", "b_ln_g", "b_ln_b",
             "b_w_s", "b_b_s", "b_w_out", "final_g"]
    return (loss, grad_x[None], *[grads[k] for k in order], *[upd[k][0] for k in order],
            *[upd[k][1] for k in order], *[upd[k][2] for k in order])
```

```python
import functools
import math

import jax
import jax.numpy as jnp
from jax import lax
from jax.experimental import pallas as pl
from jax.experimental.pallas import tpu as pltpu

F32 = jnp.float32
BF16 = jnp.bfloat16
MESH = pl.DeviceIdType.MESH

N_DEV = 8
N_CHIP = 4
SUBLANE = 8
PACK = 16
ACT = F32
RMS_EPS = 1e-6
LN_EPS = 1e-5
CHUNK = 128
GROUPS = 8
ADAM_LR = 0.001
ADAM_B1 = 0.9
ADAM_B2 = 0.999
ADAM_EPS = 1e-08
ADAM_WD = 0.01
ADAM_STEP = 10
VMEM_LIMIT = 56 << 20

HBM_SPEC = pl.BlockSpec(memory_space=pltpu.HBM)
VMEM_SPEC = pl.BlockSpec(memory_space=pltpu.VMEM)
SEM_SPEC = pl.BlockSpec(memory_space=pltpu.SEMAPHORE)
ANY_SPEC = pl.BlockSpec(memory_space=pl.ANY)
EFFECT = pltpu.SideEffectType.DATAFLOW_SIDE_EFFECTING


def _params(*sem):
    return pltpu.CompilerParams(dimension_semantics=sem, vmem_limit_bytes=VMEM_LIMIT)


def _tile(n, want):
    if n <= want:
        return n
    t = want
    while n % t:
        t -= 128
    return t


def _sigmoid(x):
    return 0.5 * jnp.tanh(0.5 * x) + 0.5


def _silu_and_grad(x):
    s = _sigmoid(x)
    return x * s, s * (1.0 + x * (1.0 - s))


def _gelu_and_grad(x):
    cdf = 0.5 * (1.0 + lax.erf(x * (1.0 / math.sqrt(2.0))))
    pdf = jnp.exp(-0.5 * x * x) * (1.0 / math.sqrt(2.0 * math.pi))
    return x * cdf, cdf + x * pdf


def _gelu(x):
    return x * (0.5 * (1.0 + lax.erf(x * (1.0 / math.sqrt(2.0)))))


def _rms(x):
    r = lax.rsqrt(jnp.mean(x * x, axis=-1, keepdims=True) + RMS_EPS)
    return x * r, r


def _rms_bwd(dxn, xn, r):
    return r * (dxn - xn * jnp.mean(dxn * xn, axis=-1, keepdims=True))


def _colsum(a):
    return jnp.sum(a, axis=0, keepdims=True)


def _shift_down(cur, before, k):
    rolled = pltpu.roll(cur, k, 0)
    row = lax.broadcasted_iota(jnp.int32, before.shape, 0)
    head = jnp.where(row < k, pltpu.roll(before, k, 0), rolled[:SUBLANE])
    return jnp.concatenate([head, rolled[SUBLANE:]], axis=0)


def _shift_up(cur, after, k):
    n = cur.shape[0]
    rolled = pltpu.roll(cur, n - k, 0)
    row = lax.broadcasted_iota(jnp.int32, after.shape, 0)
    tail = jnp.where(row >= SUBLANE - k, pltpu.roll(after, SUBLANE - k, 0), rolled[n - SUBLANE:])
    return jnp.concatenate([rolled[:n - SUBLANE], tail], axis=0)


def _after_spec():
    return pl.BlockSpec((SUBLANE, 128), lambda *_: (0, 0))


def _mm_proj(h, w, n_split, name, after):
    s, d = h.shape
    e = w.shape[1] // n_split
    tm, tn = _tile(s, 1024), _tile(e, 2048)
    nj = e // tn

    def body(h_ref, w_ref, after_ref, o_ref):
        o_ref[...] = jnp.dot(h_ref[...], w_ref[...], preferred_element_type=F32).astype(ACT)

    return pl.pallas_call(
        body, name=name,
        out_shape=jax.ShapeDtypeStruct((n_split, s, e), ACT),
        grid=(s // tm, n_split * nj),
        in_specs=[pl.BlockSpec((tm, d), lambda i, j: (i, 0)), pl.BlockSpec((d, tn), lambda i, j: (0, j)),
                  _after_spec()],
        out_specs=pl.BlockSpec((None, tm, tn), lambda i, j: (j // nj, i, j % nj)),
        compiler_params=_params("parallel", "parallel"),
    )(h, w, after)


def _mm_proj_slab(h, w, proj, q_idx, n_split, name, after):
    s, d = h.shape
    e = w.shape[1] // n_split
    tm, tn = _tile(s, 1024), _tile(e, 2048)
    nj = e // tn

    def body(q_ref, h_ref, w_ref, *rest):
        o_ref = rest[-1]
        o_ref[...] = jnp.dot(h_ref[...], w_ref[...], preferred_element_type=F32).astype(ACT)

    in_specs = [pl.BlockSpec((tm, d), lambda i, j, qr: (i, 0)), pl.BlockSpec((d, tn), lambda i, j, qr: (0, qr[0] * nj + j)),
                _after_spec()]
    args = [q_idx, h, w, after]
    aliases = {}
    if proj is not None:
        in_specs.append(ANY_SPEC)
        args.append(proj)
        aliases = {4: 0}
    return pl.pallas_call(
        body, name=name,
        out_shape=jax.ShapeDtypeStruct((n_split, s, e), ACT),
        grid_spec=pltpu.PrefetchScalarGridSpec(
            num_scalar_prefetch=1, grid=(s // tm, nj), in_specs=in_specs,
            out_specs=pl.BlockSpec((None, tm, tn), lambda i, j, qr: (qr[0], i, j))),
        input_output_aliases=aliases,
        compiler_params=_params("parallel", "parallel"),
    )(*args)


def _mm_dh_norm_bwd(dp, w, x, dx_in, g, scale, name, after, br=None, gate=None):
    nq, s, e = dp.shape
    d = w.shape[0]
    has_branch = br is not None
    tm, tk = _tile(s, 1024), _tile(e, 1024 if has_branch else 2048)
    nkq = e // tk
    nk = nq * nkq

    def body(*refs):
        if has_branch:
            (a_ref, b_ref, after_ref, x_ref, dxin_ref, g_ref, sc_ref, br_ref, gate_ref,
             dx_ref, dsh_ref, dsc_ref, dg_ref, dbr_ref, dgate_ref, acc_ref) = refs
        else:
            (a_ref, b_ref, after_ref, x_ref, dxin_ref, g_ref, sc_ref,
             dx_ref, dsh_ref, dsc_ref, dg_ref, acc_ref) = refs
        i, k = pl.program_id(0), pl.program_id(1)

        @pl.when(jnp.logical_and(i == 0, k == 0))
        def _():
            dsh_ref[...] = jnp.zeros_like(dsh_ref)
            dsc_ref[...] = jnp.zeros_like(dsc_ref)
            dg_ref[...] = jnp.zeros_like(dg_ref)
            if has_branch:
                dgate_ref[...] = jnp.zeros_like(dgate_ref)

        @pl.when(k == 0)
        def _():
            acc_ref[...] = jnp.zeros_like(acc_ref)
        acc_ref[...] += lax.dot_general(a_ref[...], b_ref[...], (((1,), (1,)), ((), ())), preferred_element_type=F32)

        @pl.when(k == nk - 1)
        def _():
            dh = acc_ref[...]
            g_vec = g_ref[...]
            xn, r = _rms(x_ref[...])
            dsh_ref[...] += _colsum(dh)
            dsc_ref[...] += _colsum(dh * (xn * g_vec))
            da = dh * (1.0 + sc_ref[...])
            dg_ref[...] += _colsum(da * xn)
            dx = dxin_ref[...] + _rms_bwd(da * g_vec, xn, r)
            dx_ref[...] = dx
            if has_branch:
                dgate_ref[...] += _colsum(dx * br_ref[...])
                dbr_ref[...] = (gate_ref[...] * dx).astype(BF16)

    rows = pl.BlockSpec((tm, d), lambda i, k: (i, 0))
    vec = pl.BlockSpec((1, d), lambda i, k: (0, 0))
    vec_out = jax.ShapeDtypeStruct((1, d), F32)
    in_specs = [pl.BlockSpec((None, tm, tk), lambda i, k: (k // nkq, i, k % nkq)),
                pl.BlockSpec((d, tk), lambda i, k: (0, k)), _after_spec(), rows, rows, vec, vec]
    out_shape = [jax.ShapeDtypeStruct((s, d), F32), vec_out, vec_out, vec_out]
    out_specs = [rows, vec, vec, vec]
    args = [dp, w, after, x, dx_in, g, scale]
    if has_branch:
        in_specs += [rows, vec]
        out_shape += [jax.ShapeDtypeStruct((s, d), BF16), vec_out]
        out_specs += [rows, vec]
        args += [br, gate]
    return pl.pallas_call(
        body, name=name, out_shape=tuple(out_shape), grid=(s // tm, nk),
        in_specs=in_specs, out_specs=tuple(out_specs), scratch_shapes=[pltpu.VMEM((tm, d), F32)],
        compiler_params=_params("arbitrary", "arbitrary"),
    )(*args)


def _dw_body(n_t):
    def body(a_ref, b_ref, o_ref, acc_ref):
        t = pl.program_id(2)

        @pl.when(t == 0)
        def _():
            acc_ref[...] = jnp.zeros_like(acc_ref)
        acc_ref[...] += lax.dot_general(a_ref[...], b_ref[...], (((0,), (0,)), ((), ())), preferred_element_type=F32)

        @pl.when(t == n_t - 1)
        def _():
            o_ref[...] = acc_ref[...].astype(o_ref.dtype)
    return body


def _mm_dw_in(h, dp, name):
    s, d = h.shape
    nq, _, e = dp.shape
    tm, tn, tt = _tile(d, 1024), _tile(e, 1024), _tile(s, 2048)
    nj = e // tn

    return pl.pallas_call(
        _dw_body(s // tt), name=name,
        out_shape=jax.ShapeDtypeStruct((d, nq * e), BF16),
        grid=(d // tm, nq * nj, s // tt),
        in_specs=[pl.BlockSpec((tt, tm), lambda i, j, t: (t, i)),
                  pl.BlockSpec((None, tt, tn), lambda i, j, t: (j // nj, t, j % nj))],
        out_specs=pl.BlockSpec((tm, tn), lambda i, j, t: (i, j)),
        scratch_shapes=[pltpu.VMEM((tm, tn), F32)],
        compiler_params=_params("parallel", "parallel", "arbitrary"),
    )(h, dp)


def _mm_dw_out(y, dbr, name):
    s, e = y.shape
    d = dbr.shape[1]
    tm, tn, tt = _tile(e, 1024), _tile(d, 1024), _tile(s, 2048)

    return pl.pallas_call(
        _dw_body(s // tt), name=name,
        out_shape=jax.ShapeDtypeStruct((e, d), BF16),
        grid=(e // tm, d // tn, s // tt),
        in_specs=[pl.BlockSpec((tt, tm), lambda i, j, t: (t, i)), pl.BlockSpec((tt, tn), lambda i, j, t: (t, j))],
        out_specs=pl.BlockSpec((tm, tn), lambda i, j, t: (i, j)),
        scratch_shapes=[pltpu.VMEM((tm, tn), F32)],
        compiler_params=_params("parallel", "parallel", "arbitrary"),
    )(y, dbr)


def _row_spec(ts, d):
    return pl.BlockSpec((ts, d), lambda i: (i, 0))


def _vec_spec(d):
    return pl.BlockSpec((1, d), lambda i: (0, 0))


def _norm_mod(x, g, scale, shift, name):
    s, d = x.shape
    ts = _tile(s, 512)

    def body(x_ref, g_ref, sc_ref, sh_ref, h_ref):
        xn, _ = _rms(x_ref[...])
        h_ref[...] = ((xn * g_ref[...]) * (1.0 + sc_ref[...]) + sh_ref[...]).astype(BF16)

    return pl.pallas_call(
        body, name=name, out_shape=jax.ShapeDtypeStruct((s, d), BF16), grid=(s // ts,),
        in_specs=[_row_spec(ts, d), _vec_spec(d), _vec_spec(d), _vec_spec(d)],
        out_specs=_row_spec(ts, d), compiler_params=_params("parallel"),
    )(x, g, scale, shift)


def _head_tile(x1, br, gate, gf, target):
    d = x1.shape[-1]
    xn, r = _rms(x1 + gate * br)
    err = xn * gf - target
    loss = 0.5 * jnp.sum(jnp.mean(err * err, axis=-1, keepdims=True))
    dout = err * (1.0 / d)
    dx = _rms_bwd(dout * gf, xn, r)
    return dx, gate * dx, loss, _colsum(dout * xn), _colsum(dx * br)


def _f32(ref):
    return ref[...].astype(F32)


def _cols_f32(ref, cols):
    return ref[:, cols].astype(F32)


def _rows_before(halo_ref, cols=slice(None)):
    return _cols_f32(halo_ref, cols)[PACK - SUBLANE:]


def _rows_after(halo_ref, cols=slice(None)):
    return _cols_f32(halo_ref, cols)[:SUBLANE]


CONV_CHAIN_COLS = 512


def _conv_fwd(proj, conv_w, conv_b, w_out, x, gate, g_next, scale_next, shift_next, name, after):
    _, s, e = proj.shape
    d = w_out.shape[1]
    ts, te = _tile(s, 256), e
    cw = min(CONV_CHAIN_COLS, te)
    hb = ts // PACK

    def body(bg_ref, cg_ref, xi_ref, z_ref, cgp_ref, xip_ref, w0_ref, w1_ref, w2_ref, b_ref, wo_ref, after_ref,
             x_ref, gate_ref, g_ref, sc_ref, sh_ref, y_ref, br_ref, x1_ref, h_ref):
        first = pl.program_id(0) == 0
        br = None
        for c0 in range(0, te, cw):
            cols = slice(c0, c0 + cw)
            cx = _cols_f32(cg_ref, cols) * _cols_f32(xi_ref, cols)
            before = jnp.where(first, 0.0, _rows_before(cgp_ref, cols) * _rows_before(xip_ref, cols))
            conv = b_ref[:, cols] + w2_ref[:, cols] * cx
            conv = conv + w0_ref[:, cols] * _shift_down(cx, before, 2)
            conv = conv + w1_ref[:, cols] * _shift_down(cx, before, 1)
            z = _cols_f32(z_ref, cols)
            y = ((z * _sigmoid(z)) * _cols_f32(bg_ref, cols) * conv).astype(BF16)
            y_ref[:, cols] = y
            part_br = jnp.dot(y, wo_ref[cols, :], preferred_element_type=F32)
            br = part_br if br is None else br + part_br
        br_ref[...] = br
        x1 = x_ref[...] + gate_ref[...] * br
        x1_ref[...] = x1
        xn, _ = _rms(x1)
        h_ref[...] = ((xn * g_ref[...]) * (1.0 + sc_ref[...]) + sh_ref[...]).astype(BF16)

    def part(q):
        return pl.BlockSpec((None, ts, te), lambda i: (q, i, 0))

    def halo_before(q):
        return pl.BlockSpec((None, PACK, te), lambda i: (q, jnp.maximum(i * hb - 1, 0), 0))

    return pl.pallas_call(
        body, name=name,
        out_shape=(jax.ShapeDtypeStruct((s, e), BF16), jax.ShapeDtypeStruct((s, d), F32),
                   jax.ShapeDtypeStruct((s, d), F32), jax.ShapeDtypeStruct((s, d), BF16)), grid=(s // ts,),
        in_specs=[part(0), part(1), part(2), part(3), halo_before(1), halo_before(2)]
        + [_vec_spec(e)] * 4 + [pl.BlockSpec((e, d), lambda i: (0, 0)), _after_spec(), _row_spec(ts, d)]
        + [_vec_spec(d)] * 4,
        out_specs=(_row_spec(ts, e), _row_spec(ts, d), _row_spec(ts, d), _row_spec(ts, d)),
        compiler_params=_params("parallel"),
    )(proj, proj, proj, proj, proj, proj, *conv_w, conv_b, w_out, after, x, gate, g_next, scale_next, shift_next)


def _conv_bwd(proj, dbr, w_out, conv_w, conv_b, name, after):
    _, s, e = proj.shape
    d = dbr.shape[1]
    ts, te = _tile(s, 512), _tile(e, 1024)
    cw = min(CONV_CHAIN_COLS // 2, te)
    hb = ts // PACK
    n_i = s // ts
    last_halo = s // PACK - 1
    nt = (((1,), (1,)), ((), ()))

    def body(bg_ref, cg_ref, xi_ref, z_ref, dbr_ref, cgp_ref, xip_ref, bgn_ref, zn_ref, dbrn_ref, wo_ref,
             w0_ref, w1_ref, w2_ref, b_ref, after_ref, dp_ref, dw0_ref, dw1_ref, dw2_ref, db_ref):
        i = pl.program_id(1)

        @pl.when(i == 0)
        def _():
            for acc in (dw0_ref, dw1_ref, dw2_ref, db_ref):
                acc[...] = jnp.zeros_like(acc)
        for c0 in range(0, te, cw):
            cols = slice(c0, c0 + cw)
            wo = wo_ref[cols, :]
            dy = lax.dot_general(dbr_ref[...], wo, nt, preferred_element_type=F32)
            dyn = lax.dot_general(dbrn_ref[...], wo, nt, preferred_element_type=F32)[:SUBLANE]
            bg, cg = _cols_f32(bg_ref, cols), _cols_f32(cg_ref, cols)
            xi, z = _cols_f32(xi_ref, cols), _cols_f32(z_ref, cols)
            w0, w1, w2 = w0_ref[:, cols], w1_ref[:, cols], w2_ref[:, cols]
            cx = cg * xi
            before = jnp.where(i > 0, _rows_before(cgp_ref, cols) * _rows_before(xip_ref, cols), 0.0)
            cx1 = _shift_down(cx, before, 1)
            cx2 = _shift_down(cx, before, 2)
            conv = b_ref[:, cols] + w2 * cx
            conv = conv + w0 * cx2
            conv = conv + w1 * cx1
            sz, dsz = _silu_and_grad(z)
            dp_ref[3, :, cols] = (dy * bg * conv * dsz).astype(BF16)
            dp_ref[0, :, cols] = (dy * sz * conv).astype(BF16)
            dconv = dy * sz * bg
            zn = _rows_after(zn_ref, cols)
            after = jnp.where(i < n_i - 1, dyn * (zn * _sigmoid(zn)) * _rows_after(bgn_ref, cols), 0.0)
            db_ref[:, cols] += _colsum(dconv)
            dw2_ref[:, cols] += _colsum(dconv * cx)
            dw1_ref[:, cols] += _colsum(dconv * cx1)
            dw0_ref[:, cols] += _colsum(dconv * cx2)
            dcx = w2 * dconv + w1 * _shift_up(dconv, after, 1) + w0 * _shift_up(dconv, after, 2)
            dp_ref[1, :, cols] = (dcx * xi).astype(BF16)
            dp_ref[2, :, cols] = (dcx * cg).astype(BF16)

    def part(q):
        return pl.BlockSpec((None, ts, te), lambda j, i: (q, i, j))

    def halo_before(q):
        return pl.BlockSpec((None, PACK, te), lambda j, i: (q, jnp.maximum(i * hb - 1, 0), j))

    def halo_after(q):
        return pl.BlockSpec((None, PACK, te), lambda j, i: (q, jnp.minimum((i + 1) * hb, last_halo), j))

    return pl.pallas_call(
        body, name=name,
        out_shape=(jax.ShapeDtypeStruct((4, s, e), BF16),) + (jax.ShapeDtypeStruct((1, e), F32),) * 4,
        grid=(e // te, n_i),
        in_specs=[part(0), part(1), part(2), part(3), pl.BlockSpec((ts, d), lambda j, i: (i, 0)),
                  halo_before(1), halo_before(2), halo_after(0), halo_after(3),
                  pl.BlockSpec((PACK, d), lambda j, i: (jnp.minimum((i + 1) * hb, last_halo), 0)),
                  pl.BlockSpec((te, d), lambda j, i: (j, 0))]
        + [pl.BlockSpec((1, te), lambda j, i: (0, j))] * 4 + [_after_spec()],
        out_specs=(pl.BlockSpec((4, ts, te), lambda j, i: (0, i, j)),) + (pl.BlockSpec((1, te), lambda j, i: (0, j)),) * 4,
        compiler_params=_params("parallel", "arbitrary"),
    )(proj, proj, proj, proj, dbr, proj, proj, proj, proj, dbr, w_out, *conv_w, conv_b, after)


def _tril(w):
    row = lax.broadcasted_iota(jnp.int32, w.shape, 0)
    col = lax.broadcasted_iota(jnp.int32, w.shape, 1)
    return jnp.where(row >= col, w, 0.0)


def _triu(w):
    row = lax.broadcasted_iota(jnp.int32, w.shape, 0)
    col = lax.broadcasted_iota(jnp.int32, w.shape, 1)
    return jnp.where(row <= col, w, 0.0)


def _layer_norm_fwd(v, g, b):
    mu = jnp.mean(v, axis=-1, keepdims=True)
    vc = v - mu
    rstd = lax.rsqrt(jnp.mean(vc * vc, axis=-1, keepdims=True) + LN_EPS)
    vhat = vc * rstd
    return vhat * g + b, vhat, rstd


GMLP_CHUNKS_PER_STEP = 2


def _gmlp_rows(s):
    return CHUNK * min(GMLP_CHUNKS_PER_STEP, s // CHUNK)


def _mix_group(w_ref, gi, src_scr, dst_scr, cols, mask, bias_ref=None):
    wm = mask(w_ref[gi]).astype(BF16)
    for n in range(src_scr.shape[0] // CHUNK):
        rows = slice(n * CHUNK, (n + 1) * CHUNK)
        out = jnp.dot(wm, src_scr[rows, cols], preferred_element_type=F32)
        if bias_ref is not None:
            out = out + bias_ref[:, gi:gi + 1]
        dst_scr[rows, cols] = out


def _gmlp_fwd_head(proj, ln_g, ln_b, w_s, b_s_t, w_out, x1, gate, gf, target, name):
    _, s, e = proj.shape
    d = w_out.shape[1]
    gw = e // GROUPS
    ts = _gmlp_rows(s)

    def body(pu_ref, pv_ref, pz_ref, g_ref, b_ref, ws_ref, bs_ref, wo_ref, x1_ref, gate_ref, gf_ref, tg_ref,
             y_ref, dx_ref, dbr_ref, loss_ref, dgf_ref, dgate_ref, vn_scr, mix_scr):
        @pl.when(pl.program_id(0) == 0)
        def _():
            loss_ref[...] = jnp.zeros_like(loss_ref)
            dgf_ref[...] = jnp.zeros_like(dgf_ref)
            dgate_ref[...] = jnp.zeros_like(dgate_ref)
        vn, _, _ = _layer_norm_fwd(_gelu(_f32(pv_ref)), g_ref[...], b_ref[...])
        vn_scr[...] = vn.astype(BF16)
        br = None
        for gi in range(GROUPS):
            cols = slice(gi * gw, (gi + 1) * gw)
            _mix_group(ws_ref, gi, vn_scr, mix_scr, cols, _tril, bs_ref)
            z = _cols_f32(pz_ref, cols)
            y = ((z * _sigmoid(z)) * (_gelu(_cols_f32(pu_ref, cols)) * mix_scr[:, cols])).astype(BF16)
            y_ref[:, cols] = y
            part_br = jnp.dot(y, wo_ref[cols, :], preferred_element_type=F32)
            br = part_br if br is None else br + part_br
        dx, dbr, loss, dgf, dgate = _head_tile(x1_ref[...], br, gate_ref[...], gf_ref[...], tg_ref[...])
        dx_ref[...] = dx
        dbr_ref[...] = dbr.astype(BF16)
        loss_ref[...] += loss
        dgf_ref[...] += dgf
        dgate_ref[...] += dgate

    def part(q):
        return pl.BlockSpec((None, ts, e), lambda i: (q, i, 0))

    return pl.pallas_call(
        body, name=name,
        out_shape=(jax.ShapeDtypeStruct((s, e), BF16), jax.ShapeDtypeStruct((s, d), F32),
                   jax.ShapeDtypeStruct((s, d), BF16), jax.ShapeDtypeStruct((SUBLANE, 128), F32),
                   jax.ShapeDtypeStruct((1, d), F32), jax.ShapeDtypeStruct((1, d), F32)),
        grid=(s // ts,),
        in_specs=[part(0), part(1), part(2), _vec_spec(e), _vec_spec(e),
                  pl.BlockSpec((GROUPS, CHUNK, CHUNK), lambda i: (0, 0, 0)),
                  pl.BlockSpec((CHUNK, GROUPS), lambda i: (0, 0)), pl.BlockSpec((e, d), lambda i: (0, 0)),
                  _row_spec(ts, d), _vec_spec(d), _vec_spec(d), _row_spec(ts, d)],
        out_specs=(_row_spec(ts, e), _row_spec(ts, d), _row_spec(ts, d),
                   pl.BlockSpec((SUBLANE, 128), lambda i: (0, 0)), _vec_spec(d), _vec_spec(d)),
        scratch_shapes=[pltpu.VMEM((ts, e), BF16), pltpu.VMEM((ts, e), F32)],
        compiler_params=_params("arbitrary"),
    )(proj, proj, proj, ln_g, ln_b, w_s, b_s_t, w_out, x1, gate, gf, target)


def _gmlp_bwd(proj, dbr, w_out, ln_g, ln_b, w_s, w_s_t, b_s_t, name):
    _, s, e = proj.shape
    d = dbr.shape[1]
    gw = e // GROUPS
    ts = _gmlp_rows(s)
    n_i = s // ts

    def body(pu_ref, pv_ref, pz_ref, dbr_ref, wo_ref, g_ref, b_ref, ws_ref, wst_ref, bs_ref,
             dp_ref, dws_ref, dbs_ref, dlg_ref, dlb_ref, vn_scr, mix_scr, dm_scr, dvn_scr, dmacc_scr):
        i = pl.program_id(0)

        @pl.when(i == 0)
        def _():
            dws_ref[...] = jnp.zeros_like(dws_ref)
            dlg_ref[...] = jnp.zeros_like(dlg_ref)
            dlb_ref[...] = jnp.zeros_like(dlb_ref)
            dmacc_scr[...] = jnp.zeros_like(dmacc_scr)
        ln_g = g_ref[...]
        nt = (((1,), (1,)), ((), ()))
        v, dv_dpv = _gelu_and_grad(_f32(pv_ref))
        vn, vhat, rstd = _layer_norm_fwd(v, ln_g, b_ref[...])
        vn_scr[...] = vn.astype(BF16)
        for gi in range(GROUPS):
            cols = slice(gi * gw, (gi + 1) * gw)
            _mix_group(ws_ref, gi, vn_scr, mix_scr, cols, _tril, bs_ref)
            mixed = mix_scr[:, cols]
            dy = lax.dot_general(dbr_ref[...], wo_ref[cols, :], nt, preferred_element_type=F32)
            u, du_dpu = _gelu_and_grad(_cols_f32(pu_ref, cols))
            sz, dsz = _silu_and_grad(_cols_f32(pz_ref, cols))
            ds = dy * sz
            dp_ref[2, :, cols] = (dy * (u * mixed) * dsz).astype(BF16)
            dp_ref[0, :, cols] = (ds * mixed * du_dpu).astype(BF16)
            dm = ds * u
            dm_scr[:, cols] = dm.astype(BF16)
            for n in range(ts // CHUNK):
                rows = slice(n * CHUNK, (n + 1) * CHUNK)
                dmacc_scr[:, cols] += dm[rows]
                dws_ref[gi] += lax.dot_general(dm_scr[rows, cols], vn_scr[rows, cols], nt, preferred_element_type=F32)
            _mix_group(wst_ref, gi, dm_scr, dvn_scr, cols, _triu)
        dvn = dvn_scr[...]
        dlg_ref[...] += _colsum(dvn * vhat)
        dlb_ref[...] += _colsum(dvn)
        dvh = dvn * ln_g
        dv = rstd * (dvh - jnp.mean(dvh, axis=-1, keepdims=True) - vhat * jnp.mean(dvh * vhat, axis=-1, keepdims=True))
        dp_ref[1] = (dv * dv_dpv).astype(BF16)

        @pl.when(i == n_i - 1)
        def _():
            for gi in range(GROUPS):
                dws_ref[gi] = _tril(dws_ref[gi])
                dbs_ref[:, gi:gi + 1] = jnp.sum(dmacc_scr[:, gi * gw:(gi + 1) * gw], axis=1, keepdims=True)

    def part(q):
        return pl.BlockSpec((None, ts, e), lambda i: (q, i, 0))

    w_spec = pl.BlockSpec((GROUPS, CHUNK, CHUNK), lambda i: (0, 0, 0))
    bs_spec = pl.BlockSpec((CHUNK, GROUPS), lambda i: (0, 0))
    return pl.pallas_call(
        body, name=name,
        out_shape=(jax.ShapeDtypeStruct((3, s, e), BF16), jax.ShapeDtypeStruct((GROUPS, CHUNK, CHUNK), F32),
                   jax.ShapeDtypeStruct((CHUNK, GROUPS), F32), jax.ShapeDtypeStruct((1, e), F32),
                   jax.ShapeDtypeStruct((1, e), F32)),
        grid=(n_i,),
        in_specs=[part(0), part(1), part(2), pl.BlockSpec((ts, d), lambda i: (i, 0)),
                  pl.BlockSpec((e, d), lambda i: (0, 0)), _vec_spec(e), _vec_spec(e), w_spec, w_spec, bs_spec],
        out_specs=(pl.BlockSpec((3, ts, e), lambda i: (0, i, 0)), w_spec, bs_spec, _vec_spec(e), _vec_spec(e)),
        scratch_shapes=[pltpu.VMEM((ts, e), BF16), pltpu.VMEM((ts, e), F32), pltpu.VMEM((ts, e), BF16),
                        pltpu.VMEM((ts, e), F32), pltpu.VMEM((CHUNK, e), F32)],
        compiler_params=_params("arbitrary"),
    )(proj, proj, proj, dbr, w_out, ln_g, ln_b, w_s, w_s_t, b_s_t)


def _mod_fwd(c_all, mod_w, mod_b_cols, name):
    n_layer, d, w = mod_w.shape

    def body(c_ref, w_ref, b_ref, ca_ref, o_ref):
        c = c_ref[...]
        ca = c * _sigmoid(c)
        ca_ref[...] = ca
        for li in range(n_layer):
            o_ref[li * N_DEV:(li + 1) * N_DEV, :] = (
                jnp.dot(ca, w_ref[li], preferred_element_type=F32, precision=lax.Precision.HIGHEST) + b_ref[li])

    return pl.pallas_call(
        body, name=name,
        out_shape=(jax.ShapeDtypeStruct((N_DEV, d), F32), jax.ShapeDtypeStruct((n_layer * N_DEV, w), F32)),
        in_specs=[VMEM_SPEC] * 3, out_specs=(VMEM_SPEC, VMEM_SPEC),
        compiler_params=pltpu.CompilerParams(vmem_limit_bytes=VMEM_LIMIT),
    )(c_all, mod_w, mod_b_cols)


def _adamw(w, g, m, v):
    m = ADAM_B1 * m + (1.0 - ADAM_B1) * g
    v = ADAM_B2 * v + (1.0 - ADAM_B2) * (g * g)
    m_hat = m / (1.0 - ADAM_B1 ** ADAM_STEP)
    v_hat = v / (1.0 - ADAM_B2 ** ADAM_STEP)
    delta = -ADAM_LR * (m_hat / (jnp.sqrt(v_hat) + ADAM_EPS) + ADAM_WD * w)
    return delta, m, v


def _adamw_2d(w, g, m, v, name):
    r, c = w.shape
    tr, tc = _tile(r, 512), _tile(c, 1024)

    def body(w_ref, g_ref, m_ref, v_ref, go_ref, d_ref, nm_ref, nv_ref):
        g = g_ref[...]
        go_ref[...] = g
        d_ref[...], nm_ref[...], nv_ref[...] = _adamw(w_ref[...], g, m_ref[...], v_ref[...])

    spec = pl.BlockSpec((tr, tc), lambda i, j: (i, j))
    shape = jax.ShapeDtypeStruct((r, c), F32)
    return pl.pallas_call(
        body, name=name, out_shape=(shape,) * 4, grid=(r // tr, c // tc),
        in_specs=[spec] * 4, out_specs=(spec,) * 4, compiler_params=_params("parallel", "parallel"),
    )(w, g, m, v)


def _mod_w_update(ca_t, dmod_cols, w, m, v, name):
    n_layer, d, wd = w.shape
    tr = _tile(d, 256)

    def body(ca_ref, dm_ref, w_ref, m_ref, v_ref, g_ref, d_ref, nm_ref, nv_ref):
        ca = ca_ref[...]
        dm = dm_ref[...]
        g = ca[:, 0:1] * dm[0:1, :]
        for b in range(1, N_DEV):
            g = g + ca[:, b:b + 1] * dm[b:b + 1, :]
        g_ref[...] = g
        d_ref[...], nm_ref[...], nv_ref[...] = _adamw(w_ref[...], g, m_ref[...], v_ref[...])

    spec = pl.BlockSpec((None, tr, wd), lambda l, i: (l, i, 0))
    shape = jax.ShapeDtypeStruct((n_layer, d, wd), F32)
    return pl.pallas_call(
        body, name=name, out_shape=(shape,) * 4, grid=(n_layer, d // tr),
        in_specs=[pl.BlockSpec((tr, N_DEV), lambda l, i: (i, 0)), pl.BlockSpec((None, N_DEV, wd), lambda l, i: (l, 0, 0)),
                  spec, spec, spec],
        out_specs=(spec,) * 4, compiler_params=_params("parallel", "parallel"),
    )(ca_t, dmod_cols, w, m, v)


def _adamw_small(ws, gs, ms, vs, name):
    n = len(ws)

    def body(*refs):
        ins, outs = refs[:4 * n], refs[4 * n:]
        for k in range(n):
            delta, nm, nv = _adamw(ins[k][...], ins[n + k][...], ins[2 * n + k][...], ins[3 * n + k][...])
            outs[3 * k][...] = delta
            outs[3 * k + 1][...] = nm
            outs[3 * k + 2][...] = nv

    out_shape = []
    for w in ws:
        out_shape += [jax.ShapeDtypeStruct(w.shape, F32)] * 3
    outs = pl.pallas_call(
        body, name=name, out_shape=tuple(out_shape),
        in_specs=[VMEM_SPEC] * (4 * n), out_specs=tuple([VMEM_SPEC] * (3 * n)),
        compiler_params=pltpu.CompilerParams(vmem_limit_bytes=VMEM_LIMIT),
    )(*ws, *gs, *ms, *vs)
    return [tuple(outs[3 * k:3 * k + 3]) for k in range(n)]


def _place():
    return lax.axis_index("x"), lax.axis_index("y"), lax.axis_index("c")


def _other_chips(x, y):
    return [(1 - x, y), (x, 1 - y), (1 - x, 1 - y)]


def _all_gather(block, name, after=None):
    m_per, n = block.shape

    def body(x_ref, *rest):
        out_ref, send_sems, recv_sems, local_sem = rest[-4:]
        x, y, c = _place()
        me, sibling = (x, y, c), (x, y, 1 - c)
        chips = _other_chips(x, y)

        def rows(px, py, pc):
            return out_ref.at[pl.ds((4 * px + 2 * py + pc) * m_per, m_per), :]

        def copy(k, blk, to, src=None):
            return pltpu.make_async_remote_copy(
                src_ref=rows(*blk) if src is None else src, dst_ref=rows(*blk),
                send_sem=send_sems.at[k], recv_sem=recv_sems.at[k], device_id=to, device_id_type=MESH)

        mine = pltpu.make_async_copy(x_ref, rows(*me), local_sem)
        mine.start()
        first = [copy(0, me, sibling, src=x_ref)]
        first += [copy(1 + j, me, (*chip, c), src=x_ref) for j, chip in enumerate(chips)]
        for cp in first:
            cp.start()
        passed = [copy(4 + j, (*chip, c), sibling) for j, chip in enumerate(chips)]
        for j, chip in enumerate(chips):
            copy(1 + j, (*chip, c), me).wait_recv()
            passed[j].start()
        copy(0, sibling, me).wait_recv()
        for j, chip in enumerate(chips):
            copy(4 + j, (*chip, 1 - c), me).wait_recv()
        for cp in first + passed:
            cp.wait_send()
        mine.wait()

    return pl.pallas_call(
        body, name=name, out_shape=jax.ShapeDtypeStruct((N_DEV * m_per, n), F32),
        in_specs=[VMEM_SPEC] + ([] if after is None else [ANY_SPEC]), out_specs=VMEM_SPEC,
        scratch_shapes=[pltpu.SemaphoreType.DMA((7,)), pltpu.SemaphoreType.DMA((7,)), pltpu.SemaphoreType.DMA],
        compiler_params=pltpu.CompilerParams(vmem_limit_bytes=VMEM_LIMIT),
    )(*([block] if after is None else [block, after]))


def _hbm(a):
    return pltpu.with_memory_space_constraint(a, pltpu.HBM)


def _place_own(shard, kind, chip_idx, name):
    r, cdim = shard.shape
    tr, tc = _tile(r, 512), _tile(cdim, 1024)
    nrb, ncb = r // tr, cdim // tc

    def body(k_ref, s_ref, o_ref):
        o_ref[...] = s_ref[...].astype(BF16)

    if kind == "col":
        full, o_map = (r, N_CHIP * cdim), lambda i, j, kr: (i, kr[0] * ncb + j)
    else:
        full, o_map = (N_CHIP * r, cdim), lambda i, j, kr: (kr[0] * nrb + i, j)
    return pl.pallas_call(
        body, name=name, out_shape=jax.ShapeDtypeStruct(full, BF16),
        grid_spec=pltpu.PrefetchScalarGridSpec(
            num_scalar_prefetch=1, grid=(nrb, ncb),
            in_specs=[pl.BlockSpec((tr, tc), lambda i, j, kr: (i, j))],
            out_specs=pl.BlockSpec((tr, tc), o_map)),
        compiler_params=_params("parallel", "parallel"),
    )(chip_idx, shard)


def _weight_window(ref, kind, shard_shape, k, half):
    r, cdim = shard_shape
    hr = r // 2
    if kind == "col":
        return ref.at[pl.ds(half * hr, hr), pl.ds(pl.multiple_of(k * cdim, 128), cdim)]
    return ref.at[pl.ds(pl.multiple_of(k * r + half * hr, 2 * SUBLANE), hr), :]


def _target_cores(c, both_cores):
    return [(cc, cc, c) for cc in range(2)] if both_cores else [(c, 0, 0)]


def _gather_weights_start(fulls, kinds, shard_shapes, groups, after, name, both_cores=True):
    n, ng = len(fulls), len(groups)
    per = 2 if both_cores else 1

    def body(*refs):
        ins = refs[:n]
        sems = refs[n + 1:n + 1 + 2 * ng]
        token = refs[2 * n + 1 + 2 * ng]
        x, y, c = _place()
        chips = _other_chips(x, y)
        for g, group in enumerate(groups):
            for pos, (w, j) in enumerate(group):
                own = _weight_window(ins[w], kinds[w], shard_shapes[w], 2 * x + y, c)
                for cc, mine, theirs in _target_cores(c, both_cores):
                    pltpu.make_async_remote_copy(
                        src_ref=own, dst_ref=own,
                        send_sem=sems[2 * g].at[per * pos + mine], recv_sem=sems[2 * g + 1].at[per * pos + theirs],
                        device_id=(*chips[j], cc), device_id_type=MESH).start()
        token[...] = jnp.zeros_like(token)

    sem_shapes = []
    for group in groups:
        sem_shapes += [pltpu.SemaphoreType.DMA((per * len(group),))] * 2
    outs = pl.pallas_call(
        body, name=name,
        out_shape=tuple(sem_shapes) + tuple(pltpu.HBM(f.shape, f.dtype) for f in fulls)
        + (jax.ShapeDtypeStruct((SUBLANE, 128), F32),),
        in_specs=[HBM_SPEC] * n + [ANY_SPEC], out_specs=(SEM_SPEC,) * (2 * ng) + (HBM_SPEC,) * n + (VMEM_SPEC,),
        input_output_aliases={w: 2 * ng + w for w in range(n)},
        compiler_params=pltpu.CompilerParams(has_side_effects=EFFECT),
    )(*[_hbm(f) for f in fulls], after)
    sems = [(outs[2 * g], outs[2 * g + 1]) for g in range(ng)]
    return sems, list(outs[2 * ng:2 * ng + n]), outs[2 * ng + n]


def _gather_weights_wait(sems, group, fulls, kinds, shard_shapes, after, name, both_cores=True):
    n = len(fulls)
    per = 2 if both_cores else 1

    def body(*refs):
        ins = refs[:n]
        send_sems, recv_sems = refs[n], refs[n + 1]
        x, y, c = _place()
        chips = _other_chips(x, y)
        for pos, (w, j) in enumerate(group):
            own = _weight_window(ins[w], kinds[w], shard_shapes[w], 2 * x + y, c)
            for cc, mine, _ in _target_cores(c, both_cores):
                landed = _weight_window(ins[w], kinds[w], shard_shapes[w], 2 * chips[j][0] + chips[j][1], cc)
                k = per * pos + mine
                pltpu.make_async_remote_copy(
                    src_ref=own, dst_ref=own, send_sem=send_sems.at[k], recv_sem=recv_sems.at[k],
                    device_id=(*chips[j], cc), device_id_type=MESH).wait_send()
                pltpu.make_async_remote_copy(
                    src_ref=landed, dst_ref=landed, send_sem=send_sems.at[k], recv_sem=recv_sems.at[k],
                    device_id=(*chips[j], cc), device_id_type=MESH).wait_recv()

    return list(pl.pallas_call(
        body, name=name, out_shape=tuple(pltpu.HBM(f.shape, f.dtype) for f in fulls),
        in_specs=[HBM_SPEC] * n + [SEM_SPEC, SEM_SPEC, ANY_SPEC], out_specs=(HBM_SPEC,) * n,
        input_output_aliases={w: w for w in range(n)},
        compiler_params=pltpu.CompilerParams(has_side_effects=EFFECT),
    )(*fulls, sems[0], sems[1], after))


def _forward_to_sibling(full, kind, shard_shape, relations, name):
    nr = len(relations)

    def body(in_ref, out_ref, send_sems, recv_sems):
        x, y, c = _place()
        chips = _other_chips(x, y)
        copies = []
        for pos, j in enumerate(relations):
            k = 2 * chips[j][0] + chips[j][1]
            mine = _weight_window(in_ref, kind, shard_shape, k, c)
            cp = pltpu.make_async_remote_copy(
                src_ref=mine, dst_ref=_weight_window(out_ref, kind, shard_shape, k, c),
                send_sem=send_sems.at[pos], recv_sem=recv_sems.at[pos], device_id=(x, y, 1 - c), device_id_type=MESH)
            cp.start()
            copies.append(cp)
        for pos, j in enumerate(relations):
            theirs = _weight_window(out_ref, kind, shard_shape, 2 * chips[j][0] + chips[j][1], 1 - c)
            pltpu.make_async_remote_copy(
                src_ref=theirs, dst_ref=theirs, send_sem=send_sems.at[pos], recv_sem=recv_sems.at[pos],
                device_id=(x, y, 1 - c), device_id_type=MESH).wait_recv()
        for cp in copies:
            cp.wait_send()

    return pl.pallas_call(
        body, name=name, out_shape=jax.ShapeDtypeStruct(full.shape, full.dtype),
        in_specs=[HBM_SPEC], out_specs=HBM_SPEC, input_output_aliases={0: 0},
        scratch_shapes=[pltpu.SemaphoreType.DMA((nr,))] * 2,
    )(full)


def _grad_piece(ref, kind, h, cdim, k, half):
    if kind == "col":
        return ref.at[pl.ds(half * h, h), pl.ds(pl.multiple_of(k * cdim, 128), cdim)]
    return ref.at[k, pl.ds(half * h, h), :]


def _grad_dims(g, kind):
    return (g.shape[0] // 2, g.shape[1] // N_CHIP) if kind == "col" else (g.shape[1] // 2, g.shape[2])


def _reduce_start(grads, kinds, name):
    n = len(grads)
    dims = [_grad_dims(g, kind) for g, kind in zip(grads, kinds)]
    lands = [lax.empty((N_DEV - 1, h, cdim), g.dtype) for g, (h, cdim) in zip(grads, dims)]

    def body(*refs):
        g_ins, land_ins = refs[:n], refs[n:2 * n]
        send_sems, recv_sems = refs[2 * n], refs[2 * n + 1]
        token = refs[4 * n + 2]
        x, y, c = _place()
        for w in range(n):
            h, cdim = dims[w]
            base = (N_DEV - 1) * w
            pltpu.make_async_remote_copy(
                src_ref=_grad_piece(g_ins[w], kinds[w], h, cdim, 2 * x + y, 1 - c), dst_ref=land_ins[w].at[0],
                send_sem=send_sems.at[base], recv_sem=recv_sems.at[base],
                device_id=(x, y, 1 - c), device_id_type=MESH).start()
            for j, chip in enumerate(_other_chips(x, y)):
                for cc in range(2):
                    pltpu.make_async_remote_copy(
                        src_ref=_grad_piece(g_ins[w], kinds[w], h, cdim, 2 * chip[0] + chip[1], cc),
                        dst_ref=land_ins[w].at[1 + 2 * j + c],
                        send_sem=send_sems.at[base + 1 + 2 * j + cc], recv_sem=recv_sems.at[base + 1 + 2 * j + c],
                        device_id=(*chip, cc), device_id_type=MESH).start()
        token[...] = jnp.zeros_like(token)

    sems = pltpu.SemaphoreType.DMA(((N_DEV - 1) * n,))
    outs = pl.pallas_call(
        body, name=name,
        out_shape=(sems, sems) + tuple(pltpu.HBM(a.shape, a.dtype) for a in list(grads) + lands)
        + (jax.ShapeDtypeStruct((SUBLANE, 128), F32),),
        in_specs=[HBM_SPEC] * (2 * n), out_specs=(SEM_SPEC, SEM_SPEC) + (HBM_SPEC,) * (2 * n) + (VMEM_SPEC,),
        input_output_aliases={i: 2 + i for i in range(2 * n)},
        compiler_params=pltpu.CompilerParams(has_side_effects=EFFECT),
    )(*[_hbm(a) for a in list(grads) + lands])
    return outs[0], outs[1], list(outs[2:2 + n]), list(outs[2 + n:2 + 2 * n]), outs[2 + 2 * n]


def _reduce_wait(send_sems, recv_sems, grads, lands, kinds, after, name):
    n = len(grads)
    dims = [_grad_dims(g, kind) for g, kind in zip(grads, kinds)]

    def body(*refs):
        g_ins, land_ins = refs[:n], refs[n:2 * n]
        send_sems, recv_sems = refs[2 * n], refs[2 * n + 1]
        x, y, c = _place()
        for w in range(n):
            h, cdim = dims[w]
            piece = _grad_piece(g_ins[w], kinds[w], h, cdim, 2 * x + y, c)
            for s in range(N_DEV - 1):
                k = (N_DEV - 1) * w + s
                slot = land_ins[w].at[s]
                pltpu.make_async_remote_copy(
                    src_ref=piece, dst_ref=slot, send_sem=send_sems.at[k], recv_sem=recv_sems.at[k],
                    device_id=(x, y, 1 - c), device_id_type=MESH).wait_send()
                pltpu.make_async_remote_copy(
                    src_ref=piece, dst_ref=slot, send_sem=send_sems.at[k], recv_sem=recv_sems.at[k],
                    device_id=(x, y, 1 - c), device_id_type=MESH).wait_recv()

    outs = pl.pallas_call(
        body, name=name, out_shape=tuple(pltpu.HBM(a.shape, a.dtype) for a in list(grads) + list(lands)),
        in_specs=[HBM_SPEC] * (2 * n) + [SEM_SPEC, SEM_SPEC, ANY_SPEC], out_specs=(HBM_SPEC,) * (2 * n),
        input_output_aliases={i: i for i in range(2 * n)},
        compiler_params=pltpu.CompilerParams(has_side_effects=EFFECT),
    )(*grads, *lands, send_sems, recv_sems, after)
    return list(outs[:n]), list(outs[n:])


def _add_pieces(g, land, kind, chip_idx, core_idx, name):
    _, h, cdim = land.shape
    tr, tc = _tile(h, 256), _tile(cdim, 2048)
    nrb, ncb = h // tr, cdim // tc

    def body(k_ref, c_ref, g_ref, l_ref, o_ref):
        acc = g_ref[...].astype(F32)
        for s in range(N_DEV - 1):
            acc = acc + l_ref[s].astype(F32)
        o_ref[...] = acc

    if kind == "col":
        g_spec = pl.BlockSpec((tr, tc), lambda i, j, kr, cr: (cr[0] * nrb + i, kr[0] * ncb + j))
    else:
        g_spec = pl.BlockSpec((None, tr, tc), lambda i, j, kr, cr: (kr[0], cr[0] * nrb + i, j))
    return pl.pallas_call(
        body, name=name, out_shape=jax.ShapeDtypeStruct((2 * h, cdim), F32),
        grid_spec=pltpu.PrefetchScalarGridSpec(
            num_scalar_prefetch=2, grid=(nrb, ncb),
            in_specs=[g_spec, pl.BlockSpec((N_DEV - 1, tr, tc), lambda i, j, kr, cr: (0, i, j))],
            out_specs=pl.BlockSpec((tr, tc), lambda i, j, kr, cr: (cr[0] * nrb + i, j))),
        compiler_params=_params("parallel", "parallel"),
    )(chip_idx, core_idx, g, land)


def _join_halves(shards, name):
    n = len(shards)

    def body(*refs):
        ins, outs = refs[:n], refs[n:2 * n]
        send_sems, recv_sems = refs[2 * n:]
        x, y, c = _place()
        copies = []
        for w in range(n):
            h = shards[w].shape[0] // 2
            cp = pltpu.make_async_remote_copy(
                src_ref=ins[w].at[pl.ds(c * h, h), :], dst_ref=outs[w].at[pl.ds(c * h, h), :],
                send_sem=send_sems.at[w], recv_sem=recv_sems.at[w], device_id=(x, y, 1 - c), device_id_type=MESH)
            cp.start()
            copies.append(cp)
        for w in range(n):
            h = shards[w].shape[0] // 2
            theirs = outs[w].at[pl.ds((1 - c) * h, h), :]
            pltpu.make_async_remote_copy(
                src_ref=theirs, dst_ref=theirs, send_sem=send_sems.at[w], recv_sem=recv_sems.at[w],
                device_id=(x, y, 1 - c), device_id_type=MESH).wait_recv()
        for cp in copies:
            cp.wait_send()

    return pl.pallas_call(
        body, name=name, out_shape=tuple(jax.ShapeDtypeStruct(s.shape, s.dtype) for s in shards),
        in_specs=[HBM_SPEC] * n, out_specs=tuple([HBM_SPEC] * n),
        input_output_aliases={w: w for w in range(n)},
        scratch_shapes=[pltpu.SemaphoreType.DMA((n,))] * 2,
    )(*shards)


def _sum_devices(gathered, rows, name):
    n = gathered.shape[1]

    def body(g_ref, o_ref):
        acc = g_ref[0:rows, :]
        for dev in range(1, N_DEV):
            acc = acc + g_ref[dev * rows:(dev + 1) * rows, :]
        o_ref[...] = acc

    return pl.pallas_call(
        body, name=name, out_shape=jax.ShapeDtypeStruct((rows, n), F32),
        in_specs=[VMEM_SPEC], out_specs=VMEM_SPEC,
        compiler_params=pltpu.CompilerParams(vmem_limit_bytes=VMEM_LIMIT),
    )(gathered)


def _pack(vectors, width):
    flat = [v.reshape(-1) for v in vectors]
    offsets, total = [], 0
    for f in flat:
        offsets.append(total)
        total += f.shape[0]
    rows = -(-total // (width * SUBLANE)) * SUBLANE
    flat.append(jnp.zeros((rows * width - total,), F32))
    return jnp.concatenate(flat).reshape(rows, width), offsets


def kernel(x, c, mod_w, mod_b, norm_g, a_w_in, a_conv_w, a_conv_b, a_w_out, b_w_in, b_ln_g, b_ln_b, b_w_s, b_b_s, b_w_out, final_g, loss_target, m_mod_w, m_mod_b, m_norm_g, m_a_w_in, m_a_conv_w, m_a_conv_b, m_a_w_out, m_b_w_in, m_b_ln_g, m_b_ln_b, m_b_w_s, m_b_b_s, m_b_w_out, m_final_g, v_mod_w, v_mod_b, v_norm_g, v_a_w_in, v_a_conv_w, v_a_conv_b, v_a_w_out, v_b_w_in, v_b_ln_g, v_b_ln_b, v_b_w_s, v_b_b_s, v_b_w_out, v_final_g):
    seq, d = x.shape[1], x.shape[2]
    e = a_conv_b.shape[1]
    wd = mod_w.shape[2]
    ax, ay, ac = _place()
    chip = 2 * ax + ay
    dev = 2 * chip + ac
    chip_idx = jnp.reshape(chip, (1,)).astype(jnp.int32)
    core_idx = jnp.reshape(ac, (1,)).astype(jnp.int32)

    x2d = x[0]
    target = loss_target[0]

    w_names = ["a_w_in", "a_w_out", "b_w_in", "b_w_out"]
    w_kinds = ["col", "row", "col", "row"]
    w_shards = [a_w_in[0], a_w_out[0], b_w_in[0], b_w_out[0]]
    w_shapes = [sh.shape for sh in w_shards]
    placed = [_place_own(sh, kind, chip_idx, "place_" + nm) for nm, sh, kind in zip(w_names, w_shards, w_kinds)]
    a_groups, far_groups = [[(0, 0), (0, 1)]], [[(0, 2)]]
    whole_group = [[(0, 0), (0, 1), (0, 2)]]

    es = e // N_CHIP
    packed0, offs0 = _pack([c, a_conv_w, b_ln_g, b_ln_b], 1024)
    gathered0 = _all_gather(packed0, "gather_params").reshape(N_DEV, -1)
    c_all = gathered0[:, :d]
    per_chip = gathered0[0::2]

    def from_chips_cols(k, rows_):
        got = per_chip[:, offs0[k]:offs0[k] + rows_ * es].reshape(N_CHIP, rows_, es)
        return jnp.transpose(got, (1, 0, 2)).reshape(rows_, e)

    conv_w_full = from_chips_cols(1, 3)
    conv_w = [conv_w_full[k:k + 1] for k in range(3)]
    ln_g, ln_b = from_chips_cols(2, 1), from_chips_cols(3, 1)
    mod_b_cols = lax.dynamic_slice_in_dim(mod_b, chip * wd, wd, axis=1)[:, None, :]
    c_act, mod_part = _mod_fwd(c_all, mod_w, mod_b_cols, "mod_fwd")
    n_layer = mod_w.shape[0]
    mod_gathered = _all_gather(mod_part, "gather_mod")
    a_sems, (wa_in,), a_token = _gather_weights_start(
        placed[:1], w_kinds[:1], w_shapes[:1], a_groups, mod_gathered, "gather_a_w_in_start", both_cores=False)
    mod_all = mod_gathered.reshape(N_CHIP, 2, n_layer, N_DEV, wd)[:, 0]
    mod_all = jnp.transpose(mod_all, (1, 2, 0, 3)).reshape(n_layer, N_DEV, N_CHIP * wd)
    mod_me = lax.dynamic_index_in_dim(mod_all, dev, axis=1, keepdims=False)
    shift = [mod_me[l:l + 1, 0:d] for l in range(n_layer)]
    scale = [mod_me[l:l + 1, d:2 * d] for l in range(n_layer)]
    gate = [mod_me[l:l + 1, 2 * d:3 * d] for l in range(n_layer)]

    g0, g1, gf = norm_g[0:1], norm_g[1:2], final_g[None, :]
    h0 = _norm_mod(x2d, g0, scale[0], shift[0], "norm_mod0")

    def slab(r):
        return jnp.bitwise_xor(chip_idx, r)

    def arrived(sems, group, weight, w, after, name, both_cores=True):
        return _gather_weights_wait(sems, [(0, j) for _, j in group], [weight], [w_kinds[w]], [w_shapes[w]],
                                    after, name, both_cores)[0]

    proj0 = _mm_proj_slab(h0, wa_in, None, slab(0), 4, "a_proj_own", a_token)
    wa_in = arrived(a_sems[0], a_groups[0], wa_in, 0, proj0, "gather_wait_near", both_cores=False)
    wa_in = _forward_to_sibling(wa_in, w_kinds[0], w_shapes[0], [0, 1], "forward_near")
    far_sems, (wa_in,), far_token = _gather_weights_start(
        [wa_in], w_kinds[:1], w_shapes[:1], far_groups, a_token, "gather_a_w_in_far_start")
    proj0 = _mm_proj_slab(h0, wa_in, proj0, slab(2), 4, "a_proj_x", far_token)
    proj0 = _mm_proj_slab(h0, wa_in, proj0, slab(1), 4, "a_proj_y", far_token)
    wa_in = arrived(far_sems[0], far_groups[0], wa_in, 0, proj0, "gather_wait_far")
    def start_whole(w, after, name):
        sems, (flight,), token = _gather_weights_start(
            placed[w:w + 1], w_kinds[w:w + 1], w_shapes[w:w + 1], whole_group, after, name)
        return sems[0], flight, token

    ao_sems, wa_out, ao_token = start_whole(1, wa_in, "gather_a_w_out_start")
    proj0 = _mm_proj_slab(h0, wa_in, proj0, slab(3), 4, "a_proj_far", ao_token)
    wa_out = arrived(ao_sems, whole_group[0], wa_out, 1, proj0, "gather_wait_a_w_out")
    bi_sems, wb_in, bi_token = start_whole(2, wa_out, "gather_b_w_in_start")
    y0, br0, x1, h1 = _conv_fwd(proj0, conv_w, a_conv_b, wa_out, x2d, gate[0], g1, scale[1], shift[1],
                                "conv_fwd", bi_token)
    wb_in = arrived(bi_sems, whole_group[0], wb_in, 2, h1, "gather_wait_b_w_in")
    bo_sems, wb_out, bo_token = start_whole(3, wb_in, "gather_b_w_out_start")
    proj1 = _mm_proj(h1, wb_in, 3, "b_proj", bo_token)
    b_s_t = jnp.transpose(b_b_s[0])
    wb_out = arrived(bo_sems, whole_group[0], wb_out, 3, proj1, "gather_wait_b_w_out")
    y1, dx2, dbr1, loss_part, g_final_g, dgate1 = _gmlp_fwd_head(
        proj1, ln_g, ln_b, b_w_s[0], b_s_t, wb_out, x1, gate[1], gf, target, "gmlp_fwd_head")

    gw_b_out = _mm_dw_out(y1, dbr1, "b_out_dw")
    dproj1, g_w_s, g_b_s_t, g_ln_g, g_ln_b = _gmlp_bwd(
        proj1, dbr1, wb_out, ln_g, ln_b, b_w_s[0], jnp.swapaxes(b_w_s[0], 1, 2), b_s_t, "gmlp_bwd")
    gw_b_in = _mm_dw_in(h1, dproj1, "b_proj_dw")
    b_kinds = ["col", "row"]
    b_send, b_recv, b_grads, b_lands, b_token = _reduce_start(
        [gw_b_in, gw_b_out.reshape(N_CHIP, e // N_CHIP, d)], b_kinds, "reduce_b_start")
    dx1, dshift1, dscale1, g_g1, dbr0, dgate0 = _mm_dh_norm_bwd(
        dproj1, wb_in, x1, dx2, g1, scale[1], "b_proj_dx", b_token, br=br0, gate=gate[0])

    gw_a_out = _mm_dw_out(y0, dbr0, "a_out_dw")
    ao_send, ao_recv, ao_grads, ao_lands, ao_grad_token = _reduce_start(
        [gw_a_out.reshape(N_CHIP, e // N_CHIP, d)], ["row"], "reduce_a_out_start")
    dproj0, g_w0, g_w1, g_w2, g_conv_b = _conv_bwd(proj0, dbr0, wa_out, conv_w, a_conv_b, "conv_bwd", ao_grad_token)
    gw_a_in = _mm_dw_in(h0, dproj0, "a_proj_dw")
    ai_send, ai_recv, ai_grads, ai_lands, ai_token = _reduce_start([gw_a_in], ["col"], "reduce_a_in_start")
    grad_x, dshift0, dscale0, g_g0 = _mm_dh_norm_bwd(dproj0, wa_in, x2d, dx1, g0, scale[0], "a_proj_dx", ai_token)

    def finish(send, recv, grads_, lands_, kinds_, names_, after, tag):
        grads_, lands_ = _reduce_wait(send, recv, grads_, lands_, kinds_, after, "reduce_" + tag + "_wait")
        halves = [_add_pieces(g, land, kind, chip_idx, core_idx, "add_pieces_" + nm)
                  for g, land, kind, nm in zip(grads_, lands_, kinds_, names_)]
        return _join_halves(halves, "join_" + tag)

    upd, big_grads = {}, {}

    def adamw_big(nm, w, g, m, v):
        g_out, *rest = _adamw_2d(w[0], g, m[0], v[0], "adamw_" + nm)
        big_grads[nm] = g_out[None]
        upd[nm] = tuple(o[None] for o in rest)

    g_b_w_in, g_b_w_out = finish(b_send, b_recv, b_grads, b_lands, b_kinds, ["b_w_in", "b_w_out"], grad_x, "b")
    adamw_big("b_w_in", b_w_in, g_b_w_in, m_b_w_in, v_b_w_in)
    adamw_big("b_w_out", b_w_out, g_b_w_out, m_b_w_out, v_b_w_out)

    dmod = jnp.concatenate([dshift0, dscale0, dgate0, dshift1, dscale1, dgate1], axis=1)
    small = [loss_part[0, 0:1], g_final_g, g_g0, g_g1, jnp.concatenate([g_w0, g_w1, g_w2], axis=0), g_conv_b, g_ln_g, g_ln_b,
             jnp.transpose(g_b_s_t), g_w_s, dmod]
    packed, offs = _pack(small, 1024)
    rows = packed.shape[0]
    gathered = _all_gather(packed, "gather_small", after=upd["b_w_out"][0])
    summed = _sum_devices(gathered, rows, "sum_small").reshape(-1)

    def take(k, shape):
        size = math.prod(shape)
        return summed[offs[k]:offs[k] + size].reshape(shape)

    loss = take(0, ())
    grad_final_g = take(1, (d,))
    grad_norm_g = jnp.concatenate([take(2, (1, d)), take(3, (1, d))], axis=0)
    grad_conv_w_full = take(4, (3, e))
    grad_a_conv_b = take(5, (1, e))
    grad_ln_g_full = take(6, (1, e))
    grad_ln_b_full = take(7, (1, e))
    grad_b_b_s = take(8, (1, GROUPS, CHUNK))
    grad_b_w_s = take(9, (1, GROUPS, CHUNK, CHUNK))
    grad_mod_b = take(10, (n_layer, 3 * d))
    grad_a_conv_w = lax.dynamic_slice_in_dim(grad_conv_w_full, chip * es, es, axis=1)[None]
    grad_b_ln_g = lax.dynamic_slice_in_dim(grad_ln_g_full, chip * es, es, axis=1)
    grad_b_ln_b = lax.dynamic_slice_in_dim(grad_ln_b_full, chip * es, es, axis=1)
    dmod_all = gathered.reshape(N_DEV, rows * 1024)[:, offs[10]:offs[10] + n_layer * 3 * d]
    dmod_all = dmod_all.reshape(N_DEV, n_layer, 3 * d)
    dmod_cols = jnp.transpose(lax.dynamic_slice_in_dim(dmod_all, chip * wd, wd, axis=2), (1, 0, 2))

    grad_mod_w, delta_mod_w, new_m_mod_w, new_v_mod_w = _mod_w_update(
        jnp.transpose(c_act), dmod_cols, mod_w, m_mod_w, v_mod_w, "mod_w_update")
    (g_a_w_out,) = finish(ao_send, ao_recv, ao_grads, ao_lands, ["row"], ["a_w_out"], delta_mod_w, "a_out")
    adamw_big("a_w_out", a_w_out, g_a_w_out, m_a_w_out, v_a_w_out)
    (g_a_w_in,) = finish(ai_send, ai_recv, ai_grads, ai_lands, ["col"], ["a_w_in"], upd["a_w_out"][0], "a_in")
    adamw_big("a_w_in", a_w_in, g_a_w_in, m_a_w_in, v_a_w_in)
    small_w = [("mod_b", mod_b, grad_mod_b, m_mod_b, v_mod_b), ("norm_g", norm_g, grad_norm_g, m_norm_g, v_norm_g),
               ("a_conv_w", a_conv_w, grad_a_conv_w, m_a_conv_w, v_a_conv_w),
               ("a_conv_b", a_conv_b, grad_a_conv_b, m_a_conv_b, v_a_conv_b),
               ("b_ln_g", b_ln_g, grad_b_ln_g, m_b_ln_g, v_b_ln_g), ("b_ln_b", b_ln_b, grad_b_ln_b, m_b_ln_b, v_b_ln_b),
               ("b_w_s", b_w_s, grad_b_w_s, m_b_w_s, v_b_w_s), ("b_b_s", b_b_s, grad_b_b_s, m_b_b_s, v_b_b_s),
               ("final_g", final_g, grad_final_g, m_final_g, v_final_g)]

    def flat2d(a):
        return a.reshape(-1, a.shape[-1])

    res = _adamw_small([flat2d(t[1]) for t in small_w], [flat2d(t[2]) for t in small_w],
                       [flat2d(t[3]) for t in small_w], [flat2d(t[4]) for t in small_w], "adamw_small")
    for (nm, w, _, _, _), r3 in zip(small_w, res):
        upd[nm] = tuple(o.reshape(w.shape) for o in r3)
    upd["mod_w"] = (delta_mod_w, new_m_mod_w, new_v_mod_w)

    grads = {"mod_w": grad_mod_w, "mod_b": grad_mod_b, "norm_g": grad_norm_g, "a_conv_w": grad_a_conv_w,
             "a_conv_b": grad_a_conv_b, "b_ln_g": grad_b_ln_g, "b_ln_b": grad_b_ln_b, "b_w_s": grad_b_w_s,
             "b_b_s": grad_b_b_s, "final_g": grad_final_g, **big_grads}
    order = ["mod_w", "mod_b", "norm_g", "a_w_in", "a_conv_w", "a_conv_b", "a_w_out", "b_w_in", "b_ln_g", "b_ln_b",
             "b_w_s", "b_b_s", "b_w_out", "final_g"]
    return (loss, grad_x[None], *[grads[k] for k in order], *[upd[k][0] for k in order],
            *[upd[k][1] for k in order], *[upd[k][2] for k in order])
```

```python
import functools
import math

import jax
import jax.numpy as jnp
from jax import lax
from jax.experimental import pallas as pl
from jax.experimental.pallas import tpu as pltpu

F32 = jnp.float32
BF16 = jnp.bfloat16
MESH = pl.DeviceIdType.MESH

N_DEV = 8
N_CHIP = 4
SUBLANE = 8
PACK = 16
CONV_ACT = BF16
GMLP_ACT = F32
RMS_EPS = 1e-6
LN_EPS = 1e-5
CHUNK = 128
GROUPS = 8
ADAM_LR = 0.001
ADAM_B1 = 0.9
ADAM_B2 = 0.999
ADAM_EPS = 1e-08
ADAM_WD = 0.01
ADAM_STEP = 10
VMEM_LIMIT = 56 << 20

HBM_SPEC = pl.BlockSpec(memory_space=pltpu.HBM)
VMEM_SPEC = pl.BlockSpec(memory_space=pltpu.VMEM)
SEM_SPEC = pl.BlockSpec(memory_space=pltpu.SEMAPHORE)
ANY_SPEC = pl.BlockSpec(memory_space=pl.ANY)
EFFECT = pltpu.SideEffectType.DATAFLOW_SIDE_EFFECTING


def _params(*sem):
    return pltpu.CompilerParams(dimension_semantics=sem, vmem_limit_bytes=VMEM_LIMIT)


def _tile(n, want):
    if n <= want:
        return n
    t = want
    while n % t:
        t -= 128
    return t


def _sigmoid(x):
    return 0.5 * jnp.tanh(0.5 * x) + 0.5


def _silu_and_grad(x):
    s = _sigmoid(x)
    return x * s, s * (1.0 + x * (1.0 - s))


def _gelu_and_grad(x):
    cdf = 0.5 * (1.0 + lax.erf(x * (1.0 / math.sqrt(2.0))))
    pdf = jnp.exp(-0.5 * x * x) * (1.0 / math.sqrt(2.0 * math.pi))
    return x * cdf, cdf + x * pdf


def _gelu(x):
    return x * (0.5 * (1.0 + lax.erf(x * (1.0 / math.sqrt(2.0)))))


def _rms(x):
    r = lax.rsqrt(jnp.mean(x * x, axis=-1, keepdims=True) + RMS_EPS)
    return x * r, r


def _rms_bwd(dxn, xn, r):
    return r * (dxn - xn * jnp.mean(dxn * xn, axis=-1, keepdims=True))


def _colsum(a):
    return jnp.sum(a, axis=0, keepdims=True)


def _shift_down(cur, before, k):
    rolled = pltpu.roll(cur, k, 0)
    row = lax.broadcasted_iota(jnp.int32, before.shape, 0)
    head = jnp.where(row < k, pltpu.roll(before, k, 0), rolled[:SUBLANE])
    return jnp.concatenate([head, rolled[SUBLANE:]], axis=0)


def _shift_up(cur, after, k):
    n = cur.shape[0]
    rolled = pltpu.roll(cur, n - k, 0)
    row = lax.broadcasted_iota(jnp.int32, after.shape, 0)
    tail = jnp.where(row >= SUBLANE - k, pltpu.roll(after, SUBLANE - k, 0), rolled[n - SUBLANE:])
    return jnp.concatenate([rolled[:n - SUBLANE], tail], axis=0)


def _after_spec():
    return pl.BlockSpec((SUBLANE, 128), lambda *_: (0, 0))


def _mm_proj(h, w, n_split, name, after):
    s, d = h.shape
    e = w.shape[1] // n_split
    tm, tn = _tile(s, 1024), _tile(e, 2048)
    nj = e // tn

    def body(h_ref, w_ref, after_ref, o_ref):
        o_ref[...] = jnp.dot(h_ref[...], w_ref[...], preferred_element_type=F32).astype(GMLP_ACT)

    return pl.pallas_call(
        body, name=name,
        out_shape=jax.ShapeDtypeStruct((n_split, s, e), GMLP_ACT),
        grid=(s // tm, n_split * nj),
        in_specs=[pl.BlockSpec((tm, d), lambda i, j: (i, 0)), pl.BlockSpec((d, tn), lambda i, j: (0, j)),
                  _after_spec()],
        out_specs=pl.BlockSpec((None, tm, tn), lambda i, j: (j // nj, i, j % nj)),
        compiler_params=_params("parallel", "parallel"),
    )(h, w, after)


def _mm_proj_slab(h, w, proj, q_idx, n_split, name, after):
    s, d = h.shape
    e = w.shape[1] // n_split
    tm, tn = _tile(s, 1024), _tile(e, 2048)
    nj = e // tn

    def body(q_ref, h_ref, w_ref, *rest):
        o_ref = rest[-1]
        o_ref[...] = jnp.dot(h_ref[...], w_ref[...], preferred_element_type=F32).astype(CONV_ACT)

    in_specs = [pl.BlockSpec((tm, d), lambda i, j, qr: (i, 0)), pl.BlockSpec((d, tn), lambda i, j, qr: (0, qr[0] * nj + j)),
                _after_spec()]
    args = [q_idx, h, w, after]
    aliases = {}
    if proj is not None:
        in_specs.append(ANY_SPEC)
        args.append(proj)
        aliases = {4: 0}
    return pl.pallas_call(
        body, name=name,
        out_shape=jax.ShapeDtypeStruct((n_split, s, e), CONV_ACT),
        grid_spec=pltpu.PrefetchScalarGridSpec(
            num_scalar_prefetch=1, grid=(s // tm, nj), in_specs=in_specs,
            out_specs=pl.BlockSpec((None, tm, tn), lambda i, j, qr: (qr[0], i, j))),
        input_output_aliases=aliases,
        compiler_params=_params("parallel", "parallel"),
    )(*args)


def _mm_dh_norm_bwd(dp, w, x, dx_in, g, scale, name, after, br=None, gate=None):
    nq, s, e = dp.shape
    d = w.shape[0]
    has_branch = br is not None
    tm, tk = _tile(s, 1024), _tile(e, 1024 if has_branch else 2048)
    nkq = e // tk
    nk = nq * nkq

    def body(*refs):
        if has_branch:
            (a_ref, b_ref, after_ref, x_ref, dxin_ref, g_ref, sc_ref, br_ref, gate_ref,
             dx_ref, dsh_ref, dsc_ref, dg_ref, dbr_ref, dgate_ref, acc_ref) = refs
        else:
            (a_ref, b_ref, after_ref, x_ref, dxin_ref, g_ref, sc_ref,
             dx_ref, dsh_ref, dsc_ref, dg_ref, acc_ref) = refs
        i, k = pl.program_id(0), pl.program_id(1)

        @pl.when(jnp.logical_and(i == 0, k == 0))
        def _():
            dsh_ref[...] = jnp.zeros_like(dsh_ref)
            dsc_ref[...] = jnp.zeros_like(dsc_ref)
            dg_ref[...] = jnp.zeros_like(dg_ref)
            if has_branch:
                dgate_ref[...] = jnp.zeros_like(dgate_ref)

        @pl.when(k == 0)
        def _():
            acc_ref[...] = jnp.zeros_like(acc_ref)
        acc_ref[...] += lax.dot_general(a_ref[...], b_ref[...], (((1,), (1,)), ((), ())), preferred_element_type=F32)

        @pl.when(k == nk - 1)
        def _():
            dh = acc_ref[...]
            g_vec = g_ref[...]
            xn, r = _rms(x_ref[...])
            dsh_ref[...] += _colsum(dh)
            dsc_ref[...] += _colsum(dh * (xn * g_vec))
            da = dh * (1.0 + sc_ref[...])
            dg_ref[...] += _colsum(da * xn)
            dx = dxin_ref[...] + _rms_bwd(da * g_vec, xn, r)
            dx_ref[...] = dx
            if has_branch:
                dgate_ref[...] += _colsum(dx * br_ref[...])
                dbr_ref[...] = (gate_ref[...] * dx).astype(BF16)

    rows = pl.BlockSpec((tm, d), lambda i, k: (i, 0))
    vec = pl.BlockSpec((1, d), lambda i, k: (0, 0))
    vec_out = jax.ShapeDtypeStruct((1, d), F32)
    in_specs = [pl.BlockSpec((None, tm, tk), lambda i, k: (k // nkq, i, k % nkq)),
                pl.BlockSpec((d, tk), lambda i, k: (0, k)), _after_spec(), rows, rows, vec, vec]
    out_shape = [jax.ShapeDtypeStruct((s, d), F32), vec_out, vec_out, vec_out]
    out_specs = [rows, vec, vec, vec]
    args = [dp, w, after, x, dx_in, g, scale]
    if has_branch:
        in_specs += [rows, vec]
        out_shape += [jax.ShapeDtypeStruct((s, d), BF16), vec_out]
        out_specs += [rows, vec]
        args += [br, gate]
    return pl.pallas_call(
        body, name=name, out_shape=tuple(out_shape), grid=(s // tm, nk),
        in_specs=in_specs, out_specs=tuple(out_specs), scratch_shapes=[pltpu.VMEM((tm, d), F32)],
        compiler_params=_params("arbitrary", "arbitrary"),
    )(*args)


def _dw_body(n_t):
    def body(a_ref, b_ref, o_ref, acc_ref):
        t = pl.program_id(2)

        @pl.when(t == 0)
        def _():
            acc_ref[...] = jnp.zeros_like(acc_ref)
        acc_ref[...] += lax.dot_general(a_ref[...], b_ref[...], (((0,), (0,)), ((), ())), preferred_element_type=F32)

        @pl.when(t == n_t - 1)
        def _():
            o_ref[...] = acc_ref[...].astype(o_ref.dtype)
    return body


def _mm_dw_in(h, dp, name):
    s, d = h.shape
    nq, _, e = dp.shape
    tm, tn, tt = _tile(d, 1024), _tile(e, 1024), _tile(s, 2048)
    nj = e // tn

    return pl.pallas_call(
        _dw_body(s // tt), name=name,
        out_shape=jax.ShapeDtypeStruct((d, nq * e), BF16),
        grid=(d // tm, nq * nj, s // tt),
        in_specs=[pl.BlockSpec((tt, tm), lambda i, j, t: (t, i)),
                  pl.BlockSpec((None, tt, tn), lambda i, j, t: (j // nj, t, j % nj))],
        out_specs=pl.BlockSpec((tm, tn), lambda i, j, t: (i, j)),
        scratch_shapes=[pltpu.VMEM((tm, tn), F32)],
        compiler_params=_params("parallel", "parallel", "arbitrary"),
    )(h, dp)


def _mm_dw_out(y, dbr, name):
    s, e = y.shape
    d = dbr.shape[1]
    tm, tn, tt = _tile(e, 1024), _tile(d, 1024), _tile(s, 2048)

    return pl.pallas_call(
        _dw_body(s // tt), name=name,
        out_shape=jax.ShapeDtypeStruct((e, d), BF16),
        grid=(e // tm, d // tn, s // tt),
        in_specs=[pl.BlockSpec((tt, tm), lambda i, j, t: (t, i)), pl.BlockSpec((tt, tn), lambda i, j, t: (t, j))],
        out_specs=pl.BlockSpec((tm, tn), lambda i, j, t: (i, j)),
        scratch_shapes=[pltpu.VMEM((tm, tn), F32)],
        compiler_params=_params("parallel", "parallel", "arbitrary"),
    )(y, dbr)


def _row_spec(ts, d):
    return pl.BlockSpec((ts, d), lambda i: (i, 0))


def _vec_spec(d):
    return pl.BlockSpec((1, d), lambda i: (0, 0))


def _norm_mod(x, g, scale, shift, name):
    s, d = x.shape
    ts = _tile(s, 512)

    def body(x_ref, g_ref, sc_ref, sh_ref, h_ref):
        xn, _ = _rms(x_ref[...])
        h_ref[...] = ((xn * g_ref[...]) * (1.0 + sc_ref[...]) + sh_ref[...]).astype(BF16)

    return pl.pallas_call(
        body, name=name, out_shape=jax.ShapeDtypeStruct((s, d), BF16), grid=(s // ts,),
        in_specs=[_row_spec(ts, d), _vec_spec(d), _vec_spec(d), _vec_spec(d)],
        out_specs=_row_spec(ts, d), compiler_params=_params("parallel"),
    )(x, g, scale, shift)


def _head_tile(x1, br, gate, gf, target):
    d = x1.shape[-1]
    xn, r = _rms(x1 + gate * br)
    err = xn * gf - target
    loss = 0.5 * jnp.sum(jnp.mean(err * err, axis=-1, keepdims=True))
    dout = err * (1.0 / d)
    dx = _rms_bwd(dout * gf, xn, r)
    return dx, gate * dx, loss, _colsum(dout * xn), _colsum(dx * br)


def _f32(ref):
    return ref[...].astype(F32)


def _cols_f32(ref, cols):
    return ref[:, cols].astype(F32)


def _rows_before(halo_ref, cols=slice(None)):
    return _cols_f32(halo_ref, cols)[PACK - SUBLANE:]


def _rows_after(halo_ref, cols=slice(None)):
    return _cols_f32(halo_ref, cols)[:SUBLANE]


CONV_CHAIN_COLS = 512


def _conv_fwd(proj, conv_w, conv_b, w_out, x, gate, g_next, scale_next, shift_next, name, after):
    _, s, e = proj.shape
    d = w_out.shape[1]
    ts, te = _tile(s, 256), e
    cw = min(CONV_CHAIN_COLS, te)
    hb = ts // PACK

    def body(bg_ref, cg_ref, xi_ref, z_ref, cgp_ref, xip_ref, w0_ref, w1_ref, w2_ref, b_ref, wo_ref, after_ref,
             x_ref, gate_ref, g_ref, sc_ref, sh_ref, y_ref, br_ref, x1_ref, h_ref):
        first = pl.program_id(0) == 0
        br = None
        for c0 in range(0, te, cw):
            cols = slice(c0, c0 + cw)
            cx = _cols_f32(cg_ref, cols) * _cols_f32(xi_ref, cols)
            before = jnp.where(first, 0.0, _rows_before(cgp_ref, cols) * _rows_before(xip_ref, cols))
            conv = b_ref[:, cols] + w2_ref[:, cols] * cx
            conv = conv + w0_ref[:, cols] * _shift_down(cx, before, 2)
            conv = conv + w1_ref[:, cols] * _shift_down(cx, before, 1)
            z = _cols_f32(z_ref, cols)
            y = ((z * _sigmoid(z)) * _cols_f32(bg_ref, cols) * conv).astype(BF16)
            y_ref[:, cols] = y
            part_br = jnp.dot(y, wo_ref[cols, :], preferred_element_type=F32)
            br = part_br if br is None else br + part_br
        br_ref[...] = br
        x1 = x_ref[...] + gate_ref[...] * br
        x1_ref[...] = x1
        xn, _ = _rms(x1)
        h_ref[...] = ((xn * g_ref[...]) * (1.0 + sc_ref[...]) + sh_ref[...]).astype(BF16)

    def part(q):
        return pl.BlockSpec((None, ts, te), lambda i: (q, i, 0))

    def halo_before(q):
        return pl.BlockSpec((None, PACK, te), lambda i: (q, jnp.maximum(i * hb - 1, 0), 0))

    return pl.pallas_call(
        body, name=name,
        out_shape=(jax.ShapeDtypeStruct((s, e), BF16), jax.ShapeDtypeStruct((s, d), F32),
                   jax.ShapeDtypeStruct((s, d), F32), jax.ShapeDtypeStruct((s, d), BF16)), grid=(s // ts,),
        in_specs=[part(0), part(1), part(2), part(3), halo_before(1), halo_before(2)]
        + [_vec_spec(e)] * 4 + [pl.BlockSpec((e, d), lambda i: (0, 0)), _after_spec(), _row_spec(ts, d)]
        + [_vec_spec(d)] * 4,
        out_specs=(_row_spec(ts, e), _row_spec(ts, d), _row_spec(ts, d), _row_spec(ts, d)),
        compiler_params=_params("parallel"),
    )(proj, proj, proj, proj, proj, proj, *conv_w, conv_b, w_out, after, x, gate, g_next, scale_next, shift_next)


def _conv_bwd(proj, dbr, w_out, conv_w, conv_b, name, after):
    _, s, e = proj.shape
    d = dbr.shape[1]
    ts, te = _tile(s, 512), _tile(e, 1024)
    cw = min(CONV_CHAIN_COLS // 2, te)
    hb = ts // PACK
    n_i = s // ts
    last_halo = s // PACK - 1
    nt = (((1,), (1,)), ((), ()))

    def body(bg_ref, cg_ref, xi_ref, z_ref, dbr_ref, cgp_ref, xip_ref, bgn_ref, zn_ref, dbrn_ref, wo_ref,
             w0_ref, w1_ref, w2_ref, b_ref, after_ref, dp_ref, dw0_ref, dw1_ref, dw2_ref, db_ref):
        i = pl.program_id(1)

        @pl.when(i == 0)
        def _():
            for acc in (dw0_ref, dw1_ref, dw2_ref, db_ref):
                acc[...] = jnp.zeros_like(acc)
        for c0 in range(0, te, cw):
            cols = slice(c0, c0 + cw)
            wo = wo_ref[cols, :]
            dy = lax.dot_general(dbr_ref[...], wo, nt, preferred_element_type=F32)
            dyn = lax.dot_general(dbrn_ref[...], wo, nt, preferred_element_type=F32)[:SUBLANE]
            bg, cg = _cols_f32(bg_ref, cols), _cols_f32(cg_ref, cols)
            xi, z = _cols_f32(xi_ref, cols), _cols_f32(z_ref, cols)
            w0, w1, w2 = w0_ref[:, cols], w1_ref[:, cols], w2_ref[:, cols]
            cx = cg * xi
            before = jnp.where(i > 0, _rows_before(cgp_ref, cols) * _rows_before(xip_ref, cols), 0.0)
            cx1 = _shift_down(cx, before, 1)
            cx2 = _shift_down(cx, before, 2)
            conv = b_ref[:, cols] + w2 * cx
            conv = conv + w0 * cx2
            conv = conv + w1 * cx1
            sz, dsz = _silu_and_grad(z)
            dp_ref[3, :, cols] = (dy * bg * conv * dsz).astype(BF16)
            dp_ref[0, :, cols] = (dy * sz * conv).astype(BF16)
            dconv = dy * sz * bg
            zn = _rows_after(zn_ref, cols)
            after = jnp.where(i < n_i - 1, dyn * (zn * _sigmoid(zn)) * _rows_after(bgn_ref, cols), 0.0)
            db_ref[:, cols] += _colsum(dconv)
            dw2_ref[:, cols] += _colsum(dconv * cx)
            dw1_ref[:, cols] += _colsum(dconv * cx1)
            dw0_ref[:, cols] += _colsum(dconv * cx2)
            dcx = w2 * dconv + w1 * _shift_up(dconv, after, 1) + w0 * _shift_up(dconv, after, 2)
            dp_ref[1, :, cols] = (dcx * xi).astype(BF16)
            dp_ref[2, :, cols] = (dcx * cg).astype(BF16)

    def part(q):
        return pl.BlockSpec((None, ts, te), lambda j, i: (q, i, j))

    def halo_before(q):
        return pl.BlockSpec((None, PACK, te), lambda j, i: (q, jnp.maximum(i * hb - 1, 0), j))

    def halo_after(q):
        return pl.BlockSpec((None, PACK, te), lambda j, i: (q, jnp.minimum((i + 1) * hb, last_halo), j))

    return pl.pallas_call(
        body, name=name,
        out_shape=(jax.ShapeDtypeStruct((4, s, e), BF16),) + (jax.ShapeDtypeStruct((1, e), F32),) * 4,
        grid=(e // te, n_i),
        in_specs=[part(0), part(1), part(2), part(3), pl.BlockSpec((ts, d), lambda j, i: (i, 0)),
                  halo_before(1), halo_before(2), halo_after(0), halo_after(3),
                  pl.BlockSpec((PACK, d), lambda j, i: (jnp.minimum((i + 1) * hb, last_halo), 0)),
                  pl.BlockSpec((te, d), lambda j, i: (j, 0))]
        + [pl.BlockSpec((1, te), lambda j, i: (0, j))] * 4 + [_after_spec()],
        out_specs=(pl.BlockSpec((4, ts, te), lambda j, i: (0, i, j)),) + (pl.BlockSpec((1, te), lambda j, i: (0, j)),) * 4,
        compiler_params=_params("parallel", "arbitrary"),
    )(proj, proj, proj, proj, dbr, proj, proj, proj, proj, dbr, w_out, *conv_w, conv_b, after)


def _tril(w):
    row = lax.broadcasted_iota(jnp.int32, w.shape, 0)
    col = lax.broadcasted_iota(jnp.int32, w.shape, 1)
    return jnp.where(row >= col, w, 0.0)


def _triu(w):
    row = lax.broadcasted_iota(jnp.int32, w.shape, 0)
    col = lax.broadcasted_iota(jnp.int32, w.shape, 1)
    return jnp.where(row <= col, w, 0.0)


def _layer_norm_fwd(v, g, b):
    mu = jnp.mean(v, axis=-1, keepdims=True)
    vc = v - mu
    rstd = lax.rsqrt(jnp.mean(vc * vc, axis=-1, keepdims=True) + LN_EPS)
    vhat = vc * rstd
    return vhat * g + b, vhat, rstd


GMLP_CHUNKS_PER_STEP = 2


def _gmlp_rows(s):
    return CHUNK * min(GMLP_CHUNKS_PER_STEP, s // CHUNK)


def _mix_positions(w_ref, src_scr, dst_scr, gw, mask, bias_ref=None):
    for gi in range(GROUPS):
        cols = slice(gi * gw, (gi + 1) * gw)
        wm = mask(w_ref[gi]).astype(BF16)
        for n in range(src_scr.shape[0] // CHUNK):
            rows = slice(n * CHUNK, (n + 1) * CHUNK)
            out = jnp.dot(wm, src_scr[rows, cols], preferred_element_type=F32)
            if bias_ref is not None:
                out = out + bias_ref[:, gi:gi + 1]
            dst_scr[rows, cols] = out


def _gmlp_fwd_head(proj, ln_g, ln_b, w_s, b_s_t, w_out, x1, gate, gf, target, name):
    _, s, e = proj.shape
    d = w_out.shape[1]
    gw = e // GROUPS
    ts = _gmlp_rows(s)

    def body(pu_ref, pv_ref, pz_ref, g_ref, b_ref, ws_ref, bs_ref, wo_ref, x1_ref, gate_ref, gf_ref, tg_ref,
             y_ref, dx_ref, dbr_ref, loss_ref, dgf_ref, dgate_ref, vn_scr, mix_scr):
        @pl.when(pl.program_id(0) == 0)
        def _():
            loss_ref[...] = jnp.zeros_like(loss_ref)
            dgf_ref[...] = jnp.zeros_like(dgf_ref)
            dgate_ref[...] = jnp.zeros_like(dgate_ref)
        vn, _, _ = _layer_norm_fwd(_gelu(_f32(pv_ref)), g_ref[...], b_ref[...])
        vn_scr[...] = vn.astype(BF16)
        _mix_positions(ws_ref, vn_scr, mix_scr, gw, _tril, bs_ref)
        z = _f32(pz_ref)
        y = ((z * _sigmoid(z)) * (_gelu(_f32(pu_ref)) * mix_scr[...])).astype(BF16)
        y_ref[...] = y
        br = jnp.dot(y, wo_ref[...], preferred_element_type=F32)
        dx, dbr, loss, dgf, dgate = _head_tile(x1_ref[...], br, gate_ref[...], gf_ref[...], tg_ref[...])
        dx_ref[...] = dx
        dbr_ref[...] = dbr.astype(BF16)
        loss_ref[...] += loss
        dgf_ref[...] += dgf
        dgate_ref[...] += dgate

    def part(q):
        return pl.BlockSpec((None, ts, e), lambda i: (q, i, 0))

    return pl.pallas_call(
        body, name=name,
        out_shape=(jax.ShapeDtypeStruct((s, e), BF16), jax.ShapeDtypeStruct((s, d), F32),
                   jax.ShapeDtypeStruct((s, d), BF16), jax.ShapeDtypeStruct((SUBLANE, 128), F32),
                   jax.ShapeDtypeStruct((1, d), F32), jax.ShapeDtypeStruct((1, d), F32)),
        grid=(s // ts,),
        in_specs=[part(0), part(1), part(2), _vec_spec(e), _vec_spec(e),
                  pl.BlockSpec((GROUPS, CHUNK, CHUNK), lambda i: (0, 0, 0)),
                  pl.BlockSpec((CHUNK, GROUPS), lambda i: (0, 0)), pl.BlockSpec((e, d), lambda i: (0, 0)),
                  _row_spec(ts, d), _vec_spec(d), _vec_spec(d), _row_spec(ts, d)],
        out_specs=(_row_spec(ts, e), _row_spec(ts, d), _row_spec(ts, d),
                   pl.BlockSpec((SUBLANE, 128), lambda i: (0, 0)), _vec_spec(d), _vec_spec(d)),
        scratch_shapes=[pltpu.VMEM((ts, e), BF16), pltpu.VMEM((ts, e), F32)],
        compiler_params=_params("arbitrary"),
    )(proj, proj, proj, ln_g, ln_b, w_s, b_s_t, w_out, x1, gate, gf, target)


def _gmlp_bwd(proj, dbr, w_out, ln_g, ln_b, w_s, w_s_t, b_s_t, name):
    _, s, e = proj.shape
    d = dbr.shape[1]
    gw = e // GROUPS
    ts = _gmlp_rows(s)
    n_i = s // ts

    def body(pu_ref, pv_ref, pz_ref, dbr_ref, wo_ref, g_ref, b_ref, ws_ref, wst_ref, bs_ref,
             dp_ref, dws_ref, dbs_ref, dlg_ref, dlb_ref, vn_scr, mix_scr, dm_scr, dvn_scr, dmacc_scr):
        i = pl.program_id(0)

        @pl.when(i == 0)
        def _():
            dws_ref[...] = jnp.zeros_like(dws_ref)
            dlg_ref[...] = jnp.zeros_like(dlg_ref)
            dlb_ref[...] = jnp.zeros_like(dlb_ref)
            dmacc_scr[...] = jnp.zeros_like(dmacc_scr)
        ln_g = g_ref[...]
        u, du_dpu = _gelu_and_grad(_f32(pu_ref))
        v, dv_dpv = _gelu_and_grad(_f32(pv_ref))
        vn, vhat, rstd = _layer_norm_fwd(v, ln_g, b_ref[...])
        vn_scr[...] = vn.astype(BF16)
        _mix_positions(ws_ref, vn_scr, mix_scr, gw, _tril, bs_ref)
        mixed = mix_scr[...]
        dy = lax.dot_general(dbr_ref[...], wo_ref[...], (((1,), (1,)), ((), ())), preferred_element_type=F32)
        sz, dsz = _silu_and_grad(_f32(pz_ref))
        ds = dy * sz
        dp_ref[2] = (dy * (u * mixed) * dsz).astype(BF16)
        dp_ref[0] = (ds * mixed * du_dpu).astype(BF16)
        dm = ds * u
        dm_scr[...] = dm.astype(BF16)
        for n in range(ts // CHUNK):
            rows = slice(n * CHUNK, (n + 1) * CHUNK)
            dmacc_scr[...] += dm[rows]
            for gi in range(GROUPS):
                cols = slice(gi * gw, (gi + 1) * gw)
                dws_ref[gi] += lax.dot_general(dm_scr[rows, cols], vn_scr[rows, cols], (((1,), (1,)), ((), ())),
                                               preferred_element_type=F32)
        _mix_positions(wst_ref, dm_scr, dvn_scr, gw, _triu)
        dvn = dvn_scr[...]
        dlg_ref[...] += _colsum(dvn * vhat)
        dlb_ref[...] += _colsum(dvn)
        dvh = dvn * ln_g
        dv = rstd * (dvh - jnp.mean(dvh, axis=-1, keepdims=True) - vhat * jnp.mean(dvh * vhat, axis=-1, keepdims=True))
        dp_ref[1] = (dv * dv_dpv).astype(BF16)

        @pl.when(i == n_i - 1)
        def _():
            for gi in range(GROUPS):
                dws_ref[gi] = _tril(dws_ref[gi])
                dbs_ref[:, gi:gi + 1] = jnp.sum(dmacc_scr[:, gi * gw:(gi + 1) * gw], axis=1, keepdims=True)

    def part(q):
        return pl.BlockSpec((None, ts, e), lambda i: (q, i, 0))

    w_spec = pl.BlockSpec((GROUPS, CHUNK, CHUNK), lambda i: (0, 0, 0))
    bs_spec = pl.BlockSpec((CHUNK, GROUPS), lambda i: (0, 0))
    return pl.pallas_call(
        body, name=name,
        out_shape=(jax.ShapeDtypeStruct((3, s, e), BF16), jax.ShapeDtypeStruct((GROUPS, CHUNK, CHUNK), F32),
                   jax.ShapeDtypeStruct((CHUNK, GROUPS), F32), jax.ShapeDtypeStruct((1, e), F32),
                   jax.ShapeDtypeStruct((1, e), F32)),
        grid=(n_i,),
        in_specs=[part(0), part(1), part(2), pl.BlockSpec((ts, d), lambda i: (i, 0)),
                  pl.BlockSpec((e, d), lambda i: (0, 0)), _vec_spec(e), _vec_spec(e), w_spec, w_spec, bs_spec],
        out_specs=(pl.BlockSpec((3, ts, e), lambda i: (0, i, 0)), w_spec, bs_spec, _vec_spec(e), _vec_spec(e)),
        scratch_shapes=[pltpu.VMEM((ts, e), BF16), pltpu.VMEM((ts, e), F32), pltpu.VMEM((ts, e), BF16),
                        pltpu.VMEM((ts, e), F32), pltpu.VMEM((CHUNK, e), F32)],
        compiler_params=_params("arbitrary"),
    )(proj, proj, proj, dbr, w_out, ln_g, ln_b, w_s, w_s_t, b_s_t)


def _mod_fwd(c_all, mod_w, mod_b_cols, name):
    n_layer, d, w = mod_w.shape

    def body(c_ref, w_ref, b_ref, ca_ref, o_ref):
        c = c_ref[...]
        ca = c * _sigmoid(c)
        ca_ref[...] = ca
        for li in range(n_layer):
            o_ref[li * N_DEV:(li + 1) * N_DEV, :] = (
                jnp.dot(ca, w_ref[li], preferred_element_type=F32, precision=lax.Precision.HIGHEST) + b_ref[li])

    return pl.pallas_call(
        body, name=name,
        out_shape=(jax.ShapeDtypeStruct((N_DEV, d), F32), jax.ShapeDtypeStruct((n_layer * N_DEV, w), F32)),
        in_specs=[VMEM_SPEC] * 3, out_specs=(VMEM_SPEC, VMEM_SPEC),
        compiler_params=pltpu.CompilerParams(vmem_limit_bytes=VMEM_LIMIT),
    )(c_all, mod_w, mod_b_cols)


def _adamw(w, g, m, v):
    m = ADAM_B1 * m + (1.0 - ADAM_B1) * g
    v = ADAM_B2 * v + (1.0 - ADAM_B2) * (g * g)
    m_hat = m / (1.0 - ADAM_B1 ** ADAM_STEP)
    v_hat = v / (1.0 - ADAM_B2 ** ADAM_STEP)
    delta = -ADAM_LR * (m_hat / (jnp.sqrt(v_hat) + ADAM_EPS) + ADAM_WD * w)
    return delta, m, v


def _adamw_2d(w, g, m, v, name):
    r, c = w.shape
    tr, tc = _tile(r, 512), _tile(c, 1024)

    def body(w_ref, g_ref, m_ref, v_ref, go_ref, d_ref, nm_ref, nv_ref):
        g = g_ref[...]
        go_ref[...] = g
        d_ref[...], nm_ref[...], nv_ref[...] = _adamw(w_ref[...], g, m_ref[...], v_ref[...])

    spec = pl.BlockSpec((tr, tc), lambda i, j: (i, j))
    shape = jax.ShapeDtypeStruct((r, c), F32)
    return pl.pallas_call(
        body, name=name, out_shape=(shape,) * 4, grid=(r // tr, c // tc),
        in_specs=[spec] * 4, out_specs=(spec,) * 4, compiler_params=_params("parallel", "parallel"),
    )(w, g, m, v)


def _mod_w_update(ca_t, dmod_cols, w, m, v, name):
    n_layer, d, wd = w.shape
    tr = _tile(d, 256)

    def body(ca_ref, dm_ref, w_ref, m_ref, v_ref, g_ref, d_ref, nm_ref, nv_ref):
        ca = ca_ref[...]
        dm = dm_ref[...]
        g = ca[:, 0:1] * dm[0:1, :]
        for b in range(1, N_DEV):
            g = g + ca[:, b:b + 1] * dm[b:b + 1, :]
        g_ref[...] = g
        d_ref[...], nm_ref[...], nv_ref[...] = _adamw(w_ref[...], g, m_ref[...], v_ref[...])

    spec = pl.BlockSpec((None, tr, wd), lambda l, i: (l, i, 0))
    shape = jax.ShapeDtypeStruct((n_layer, d, wd), F32)
    return pl.pallas_call(
        body, name=name, out_shape=(shape,) * 4, grid=(n_layer, d // tr),
        in_specs=[pl.BlockSpec((tr, N_DEV), lambda l, i: (i, 0)), pl.BlockSpec((None, N_DEV, wd), lambda l, i: (l, 0, 0)),
                  spec, spec, spec],
        out_specs=(spec,) * 4, compiler_params=_params("parallel", "parallel"),
    )(ca_t, dmod_cols, w, m, v)


def _adamw_small(ws, gs, ms, vs, name):
    n = len(ws)

    def body(*refs):
        ins, outs = refs[:4 * n], refs[4 * n:]
        for k in range(n):
            delta, nm, nv = _adamw(ins[k][...], ins[n + k][...], ins[2 * n + k][...], ins[3 * n + k][...])
            outs[3 * k][...] = delta
            outs[3 * k + 1][...] = nm
            outs[3 * k + 2][...] = nv

    out_shape = []
    for w in ws:
        out_shape += [jax.ShapeDtypeStruct(w.shape, F32)] * 3
    outs = pl.pallas_call(
        body, name=name, out_shape=tuple(out_shape),
        in_specs=[VMEM_SPEC] * (4 * n), out_specs=tuple([VMEM_SPEC] * (3 * n)),
        compiler_params=pltpu.CompilerParams(vmem_limit_bytes=VMEM_LIMIT),
    )(*ws, *gs, *ms, *vs)
    return [tuple(outs[3 * k:3 * k + 3]) for k in range(n)]


def _place():
    return lax.axis_index("x"), lax.axis_index("y"), lax.axis_index("c")


def _other_chips(x, y):
    return [(1 - x, y), (x, 1 - y), (1 - x, 1 - y)]


def _all_gather(block, name, after=None):
    m_per, n = block.shape

    def body(x_ref, *rest):
        out_ref, send_sems, recv_sems, local_sem = rest[-4:]
        x, y, c = _place()
        me, sibling = (x, y, c), (x, y, 1 - c)
        chips = _other_chips(x, y)

        def rows(px, py, pc):
            return out_ref.at[pl.ds((4 * px + 2 * py + pc) * m_per, m_per), :]

        def copy(k, blk, to, src=None):
            return pltpu.make_async_remote_copy(
                src_ref=rows(*blk) if src is None else src, dst_ref=rows(*blk),
                send_sem=send_sems.at[k], recv_sem=recv_sems.at[k], device_id=to, device_id_type=MESH)

        mine = pltpu.make_async_copy(x_ref, rows(*me), local_sem)
        mine.start()
        first = [copy(0, me, sibling, src=x_ref)]
        first += [copy(1 + j, me, (*chip, c), src=x_ref) for j, chip in enumerate(chips)]
        for cp in first:
            cp.start()
        passed = [copy(4 + j, (*chip, c), sibling) for j, chip in enumerate(chips)]
        for j, chip in enumerate(chips):
            copy(1 + j, (*chip, c), me).wait_recv()
            passed[j].start()
        copy(0, sibling, me).wait_recv()
        for j, chip in enumerate(chips):
            copy(4 + j, (*chip, 1 - c), me).wait_recv()
        for cp in first + passed:
            cp.wait_send()
        mine.wait()

    return pl.pallas_call(
        body, name=name, out_shape=jax.ShapeDtypeStruct((N_DEV * m_per, n), F32),
        in_specs=[VMEM_SPEC] + ([] if after is None else [ANY_SPEC]), out_specs=VMEM_SPEC,
        scratch_shapes=[pltpu.SemaphoreType.DMA((7,)), pltpu.SemaphoreType.DMA((7,)), pltpu.SemaphoreType.DMA],
        compiler_params=pltpu.CompilerParams(vmem_limit_bytes=VMEM_LIMIT),
    )(*([block] if after is None else [block, after]))


def _hbm(a):
    return pltpu.with_memory_space_constraint(a, pltpu.HBM)


def _place_own(shard, kind, chip_idx, name):
    r, cdim = shard.shape
    tr, tc = _tile(r, 512), _tile(cdim, 1024)
    nrb, ncb = r // tr, cdim // tc

    def body(k_ref, s_ref, o_ref):
        o_ref[...] = s_ref[...].astype(BF16)

    if kind == "col":
        full, o_map = (r, N_CHIP * cdim), lambda i, j, kr: (i, kr[0] * ncb + j)
    else:
        full, o_map = (N_CHIP * r, cdim), lambda i, j, kr: (kr[0] * nrb + i, j)
    return pl.pallas_call(
        body, name=name, out_shape=jax.ShapeDtypeStruct(full, BF16),
        grid_spec=pltpu.PrefetchScalarGridSpec(
            num_scalar_prefetch=1, grid=(nrb, ncb),
            in_specs=[pl.BlockSpec((tr, tc), lambda i, j, kr: (i, j))],
            out_specs=pl.BlockSpec((tr, tc), o_map)),
        compiler_params=_params("parallel", "parallel"),
    )(chip_idx, shard)


def _weight_window(ref, kind, shard_shape, k, half):
    r, cdim = shard_shape
    hr = r // 2
    if kind == "col":
        return ref.at[pl.ds(half * hr, hr), pl.ds(pl.multiple_of(k * cdim, 128), cdim)]
    return ref.at[pl.ds(pl.multiple_of(k * r + half * hr, 2 * SUBLANE), hr), :]


def _target_cores(c, both_cores):
    return [(cc, cc, c) for cc in range(2)] if both_cores else [(c, 0, 0)]


def _gather_weights_start(fulls, kinds, shard_shapes, groups, after, name, both_cores=True):
    n, ng = len(fulls), len(groups)
    per = 2 if both_cores else 1

    def body(*refs):
        ins = refs[:n]
        sems = refs[n + 1:n + 1 + 2 * ng]
        token = refs[2 * n + 1 + 2 * ng]
        x, y, c = _place()
        chips = _other_chips(x, y)
        for g, group in enumerate(groups):
            for pos, (w, j) in enumerate(group):
                own = _weight_window(ins[w], kinds[w], shard_shapes[w], 2 * x + y, c)
                for cc, mine, theirs in _target_cores(c, both_cores):
                    pltpu.make_async_remote_copy(
                        src_ref=own, dst_ref=own,
                        send_sem=sems[2 * g].at[per * pos + mine], recv_sem=sems[2 * g + 1].at[per * pos + theirs],
                        device_id=(*chips[j], cc), device_id_type=MESH).start()
        token[...] = jnp.zeros_like(token)

    sem_shapes = []
    for group in groups:
        sem_shapes += [pltpu.SemaphoreType.DMA((per * len(group),))] * 2
    outs = pl.pallas_call(
        body, name=name,
        out_shape=tuple(sem_shapes) + tuple(pltpu.HBM(f.shape, f.dtype) for f in fulls)
        + (jax.ShapeDtypeStruct((SUBLANE, 128), F32),),
        in_specs=[HBM_SPEC] * n + [ANY_SPEC], out_specs=(SEM_SPEC,) * (2 * ng) + (HBM_SPEC,) * n + (VMEM_SPEC,),
        input_output_aliases={w: 2 * ng + w for w in range(n)},
        compiler_params=pltpu.CompilerParams(has_side_effects=EFFECT),
    )(*[_hbm(f) for f in fulls], after)
    sems = [(outs[2 * g], outs[2 * g + 1]) for g in range(ng)]
    return sems, list(outs[2 * ng:2 * ng + n]), outs[2 * ng + n]


def _gather_weights_wait(sems, group, fulls, kinds, shard_shapes, after, name, both_cores=True):
    n = len(fulls)
    per = 2 if both_cores else 1

    def body(*refs):
        ins = refs[:n]
        send_sems, recv_sems = refs[n], refs[n + 1]
        x, y, c = _place()
        chips = _other_chips(x, y)
        for pos, (w, j) in enumerate(group):
            own = _weight_window(ins[w], kinds[w], shard_shapes[w], 2 * x + y, c)
            for cc, mine, _ in _target_cores(c, both_cores):
                landed = _weight_window(ins[w], kinds[w], shard_shapes[w], 2 * chips[j][0] + chips[j][1], cc)
                k = per * pos + mine
                pltpu.make_async_remote_copy(
                    src_ref=own, dst_ref=own, send_sem=send_sems.at[k], recv_sem=recv_sems.at[k],
                    device_id=(*chips[j], cc), device_id_type=MESH).wait_send()
                pltpu.make_async_remote_copy(
                    src_ref=landed, dst_ref=landed, send_sem=send_sems.at[k], recv_sem=recv_sems.at[k],
                    device_id=(*chips[j], cc), device_id_type=MESH).wait_recv()

    return list(pl.pallas_call(
        body, name=name, out_shape=tuple(pltpu.HBM(f.shape, f.dtype) for f in fulls),
        in_specs=[HBM_SPEC] * n + [SEM_SPEC, SEM_SPEC, ANY_SPEC], out_specs=(HBM_SPEC,) * n,
        input_output_aliases={w: w for w in range(n)},
        compiler_params=pltpu.CompilerParams(has_side_effects=EFFECT),
    )(*fulls, sems[0], sems[1], after))


def _forward_to_sibling(full, kind, shard_shape, relations, name):
    nr = len(relations)

    def body(in_ref, out_ref, send_sems, recv_sems):
        x, y, c = _place()
        chips = _other_chips(x, y)
        copies = []
        for pos, j in enumerate(relations):
            k = 2 * chips[j][0] + chips[j][1]
            mine = _weight_window(in_ref, kind, shard_shape, k, c)
            cp = pltpu.make_async_remote_copy(
                src_ref=mine, dst_ref=_weight_window(out_ref, kind, shard_shape, k, c),
                send_sem=send_sems.at[pos], recv_sem=recv_sems.at[pos], device_id=(x, y, 1 - c), device_id_type=MESH)
            cp.start()
            copies.append(cp)
        for pos, j in enumerate(relations):
            theirs = _weight_window(out_ref, kind, shard_shape, 2 * chips[j][0] + chips[j][1], 1 - c)
            pltpu.make_async_remote_copy(
                src_ref=theirs, dst_ref=theirs, send_sem=send_sems.at[pos], recv_sem=recv_sems.at[pos],
                device_id=(x, y, 1 - c), device_id_type=MESH).wait_recv()
        for cp in copies:
            cp.wait_send()

    return pl.pallas_call(
        body, name=name, out_shape=jax.ShapeDtypeStruct(full.shape, full.dtype),
        in_specs=[HBM_SPEC], out_specs=HBM_SPEC, input_output_aliases={0: 0},
        scratch_shapes=[pltpu.SemaphoreType.DMA((nr,))] * 2,
    )(full)


def _grad_piece(ref, kind, h, cdim, k, half):
    if kind == "col":
        return ref.at[pl.ds(half * h, h), pl.ds(pl.multiple_of(k * cdim, 128), cdim)]
    return ref.at[k, pl.ds(half * h, h), :]


def _grad_dims(g, kind):
    return (g.shape[0] // 2, g.shape[1] // N_CHIP) if kind == "col" else (g.shape[1] // 2, g.shape[2])


def _reduce_start(grads, kinds, name):
    n = len(grads)
    dims = [_grad_dims(g, kind) for g, kind in zip(grads, kinds)]
    lands = [lax.empty((N_DEV - 1, h, cdim), g.dtype) for g, (h, cdim) in zip(grads, dims)]

    def body(*refs):
        g_ins, land_ins = refs[:n], refs[n:2 * n]
        send_sems, recv_sems = refs[2 * n], refs[2 * n + 1]
        token = refs[4 * n + 2]
        x, y, c = _place()
        for w in range(n):
            h, cdim = dims[w]
            base = (N_DEV - 1) * w
            pltpu.make_async_remote_copy(
                src_ref=_grad_piece(g_ins[w], kinds[w], h, cdim, 2 * x + y, 1 - c), dst_ref=land_ins[w].at[0],
                send_sem=send_sems.at[base], recv_sem=recv_sems.at[base],
                device_id=(x, y, 1 - c), device_id_type=MESH).start()
            for j, chip in enumerate(_other_chips(x, y)):
                for cc in range(2):
                    pltpu.make_async_remote_copy(
                        src_ref=_grad_piece(g_ins[w], kinds[w], h, cdim, 2 * chip[0] + chip[1], cc),
                        dst_ref=land_ins[w].at[1 + 2 * j + c],
                        send_sem=send_sems.at[base + 1 + 2 * j + cc], recv_sem=recv_sems.at[base + 1 + 2 * j + c],
                        device_id=(*chip, cc), device_id_type=MESH).start()
        token[...] = jnp.zeros_like(token)

    sems = pltpu.SemaphoreType.DMA(((N_DEV - 1) * n,))
    outs = pl.pallas_call(
        body, name=name,
        out_shape=(sems, sems) + tuple(pltpu.HBM(a.shape, a.dtype) for a in list(grads) + lands)
        + (jax.ShapeDtypeStruct((SUBLANE, 128), F32),),
        in_specs=[HBM_SPEC] * (2 * n), out_specs=(SEM_SPEC, SEM_SPEC) + (HBM_SPEC,) * (2 * n) + (VMEM_SPEC,),
        input_output_aliases={i: 2 + i for i in range(2 * n)},
        compiler_params=pltpu.CompilerParams(has_side_effects=EFFECT),
    )(*[_hbm(a) for a in list(grads) + lands])
    return outs[0], outs[1], list(outs[2:2 + n]), list(outs[2 + n:2 + 2 * n]), outs[2 + 2 * n]


def _reduce_wait(send_sems, recv_sems, grads, lands, kinds, after, name):
    n = len(grads)
    dims = [_grad_dims(g, kind) for g, kind in zip(grads, kinds)]

    def body(*refs):
        g_ins, land_ins = refs[:n], refs[n:2 * n]
        send_sems, recv_sems = refs[2 * n], refs[2 * n + 1]
        x, y, c = _place()
        for w in range(n):
            h, cdim = dims[w]
            piece = _grad_piece(g_ins[w], kinds[w], h, cdim, 2 * x + y, c)
            for s in range(N_DEV - 1):
                k = (N_DEV - 1) * w + s
                slot = land_ins[w].at[s]
                pltpu.make_async_remote_copy(
                    src_ref=piece, dst_ref=slot, send_sem=send_sems.at[k], recv_sem=recv_sems.at[k],
                    device_id=(x, y, 1 - c), device_id_type=MESH).wait_send()
                pltpu.make_async_remote_copy(
                    src_ref=piece, dst_ref=slot, send_sem=send_sems.at[k], recv_sem=recv_sems.at[k],
                    device_id=(x, y, 1 - c), device_id_type=MESH).wait_recv()

    outs = pl.pallas_call(
        body, name=name, out_shape=tuple(pltpu.HBM(a.shape, a.dtype) for a in list(grads) + list(lands)),
        in_specs=[HBM_SPEC] * (2 * n) + [SEM_SPEC, SEM_SPEC, ANY_SPEC], out_specs=(HBM_SPEC,) * (2 * n),
        input_output_aliases={i: i for i in range(2 * n)},
        compiler_params=pltpu.CompilerParams(has_side_effects=EFFECT),
    )(*grads, *lands, send_sems, recv_sems, after)
    return list(outs[:n]), list(outs[n:])


def _add_pieces(g, land, kind, chip_idx, core_idx, name):
    _, h, cdim = land.shape
    tr, tc = _tile(h, 256), _tile(cdim, 2048)
    nrb, ncb = h // tr, cdim // tc

    def body(k_ref, c_ref, g_ref, l_ref, o_ref):
        acc = g_ref[...].astype(F32)
        for s in range(N_DEV - 1):
            acc = acc + l_ref[s].astype(F32)
        o_ref[...] = acc

    if kind == "col":
        g_spec = pl.BlockSpec((tr, tc), lambda i, j, kr, cr: (cr[0] * nrb + i, kr[0] * ncb + j))
    else:
        g_spec = pl.BlockSpec((None, tr, tc), lambda i, j, kr, cr: (kr[0], cr[0] * nrb + i, j))
    return pl.pallas_call(
        body, name=name, out_shape=jax.ShapeDtypeStruct((2 * h, cdim), F32),
        grid_spec=pltpu.PrefetchScalarGridSpec(
            num_scalar_prefetch=2, grid=(nrb, ncb),
            in_specs=[g_spec, pl.BlockSpec((N_DEV - 1, tr, tc), lambda i, j, kr, cr: (0, i, j))],
            out_specs=pl.BlockSpec((tr, tc), lambda i, j, kr, cr: (cr[0] * nrb + i, j))),
        compiler_params=_params("parallel", "parallel"),
    )(chip_idx, core_idx, g, land)


def _join_halves(shards, name):
    n = len(shards)

    def body(*refs):
        ins, outs = refs[:n], refs[n:2 * n]
        send_sems, recv_sems = refs[2 * n:]
        x, y, c = _place()
        copies = []
        for w in range(n):
            h = shards[w].shape[0] // 2
            cp = pltpu.make_async_remote_copy(
                src_ref=ins[w].at[pl.ds(c * h, h), :], dst_ref=outs[w].at[pl.ds(c * h, h), :],
                send_sem=send_sems.at[w], recv_sem=recv_sems.at[w], device_id=(x, y, 1 - c), device_id_type=MESH)
            cp.start()
            copies.append(cp)
        for w in range(n):
            h = shards[w].shape[0] // 2
            theirs = outs[w].at[pl.ds((1 - c) * h, h), :]
            pltpu.make_async_remote_copy(
                src_ref=theirs, dst_ref=theirs, send_sem=send_sems.at[w], recv_sem=recv_sems.at[w],
                device_id=(x, y, 1 - c), device_id_type=MESH).wait_recv()
        for cp in copies:
            cp.wait_send()

    return pl.pallas_call(
        body, name=name, out_shape=tuple(jax.ShapeDtypeStruct(s.shape, s.dtype) for s in shards),
        in_specs=[HBM_SPEC] * n, out_specs=tuple([HBM_SPEC] * n),
        input_output_aliases={w: w for w in range(n)},
        scratch_shapes=[pltpu.SemaphoreType.DMA((n,))] * 2,
    )(*shards)


def _sum_devices(gathered, rows, name):
    n = gathered.shape[1]

    def body(g_ref, o_ref):
        acc = g_ref[0:rows, :]
        for dev in range(1, N_DEV):
            acc = acc + g_ref[dev * rows:(dev + 1) * rows, :]
        o_ref[...] = acc

    return pl.pallas_call(
        body, name=name, out_shape=jax.ShapeDtypeStruct((rows, n), F32),
        in_specs=[VMEM_SPEC], out_specs=VMEM_SPEC,
        compiler_params=pltpu.CompilerParams(vmem_limit_bytes=VMEM_LIMIT),
    )(gathered)


def _pack(vectors, width):
    flat = [v.reshape(-1) for v in vectors]
    offsets, total = [], 0
    for f in flat:
        offsets.append(total)
        total += f.shape[0]
    rows = -(-total // (width * SUBLANE)) * SUBLANE
    flat.append(jnp.zeros((rows * width - total,), F32))
    return jnp.concatenate(flat).reshape(rows, width), offsets


def kernel(x, c, mod_w, mod_b, norm_g, a_w_in, a_conv_w, a_conv_b, a_w_out, b_w_in, b_ln_g, b_ln_b, b_w_s, b_b_s, b_w_out, final_g, loss_target, m_mod_w, m_mod_b, m_norm_g, m_a_w_in, m_a_conv_w, m_a_conv_b, m_a_w_out, m_b_w_in, m_b_ln_g, m_b_ln_b, m_b_w_s, m_b_b_s, m_b_w_out, m_final_g, v_mod_w, v_mod_b, v_norm_g, v_a_w_in, v_a_conv_w, v_a_conv_b, v_a_w_out, v_b_w_in, v_b_ln_g, v_b_ln_b, v_b_w_s, v_b_b_s, v_b_w_out, v_final_g):
    seq, d = x.shape[1], x.shape[2]
    e = a_conv_b.shape[1]
    wd = mod_w.shape[2]
    ax, ay, ac = _place()
    chip = 2 * ax + ay
    dev = 2 * chip + ac
    chip_idx = jnp.reshape(chip, (1,)).astype(jnp.int32)
    core_idx = jnp.reshape(ac, (1,)).astype(jnp.int32)

    x2d = x[0]
    target = loss_target[0]

    w_names = ["a_w_in", "a_w_out", "b_w_in", "b_w_out"]
    w_kinds = ["col", "row", "col", "row"]
    w_shards = [a_w_in[0], a_w_out[0], b_w_in[0], b_w_out[0]]
    w_shapes = [sh.shape for sh in w_shards]
    placed = [_place_own(sh, kind, chip_idx, "place_" + nm) for nm, sh, kind in zip(w_names, w_shards, w_kinds)]
    a_groups, far_groups = [[(0, 0), (0, 1)]], [[(0, 2)]]
    whole_group = [[(0, 0), (0, 1), (0, 2)]]

    es = e // N_CHIP
    packed0, offs0 = _pack([c, a_conv_w, b_ln_g, b_ln_b], 1024)
    gathered0 = _all_gather(packed0, "gather_params").reshape(N_DEV, -1)
    c_all = gathered0[:, :d]
    per_chip = gathered0[0::2]

    def from_chips_cols(k, rows_):
        got = per_chip[:, offs0[k]:offs0[k] + rows_ * es].reshape(N_CHIP, rows_, es)
        return jnp.transpose(got, (1, 0, 2)).reshape(rows_, e)

    conv_w_full = from_chips_cols(1, 3)
    conv_w = [conv_w_full[k:k + 1] for k in range(3)]
    ln_g, ln_b = from_chips_cols(2, 1), from_chips_cols(3, 1)
    mod_b_cols = lax.dynamic_slice_in_dim(mod_b, chip * wd, wd, axis=1)[:, None, :]
    c_act, mod_part = _mod_fwd(c_all, mod_w, mod_b_cols, "mod_fwd")
    n_layer = mod_w.shape[0]
    mod_gathered = _all_gather(mod_part, "gather_mod")
    a_sems, (wa_in,), a_token = _gather_weights_start(
        placed[:1], w_kinds[:1], w_shapes[:1], a_groups, mod_gathered, "gather_a_w_in_start", both_cores=False)
    mod_all = mod_gathered.reshape(N_CHIP, 2, n_layer, N_DEV, wd)[:, 0]
    mod_all = jnp.transpose(mod_all, (1, 2, 0, 3)).reshape(n_layer, N_DEV, N_CHIP * wd)
    mod_me = lax.dynamic_index_in_dim(mod_all, dev, axis=1, keepdims=False)
    shift = [mod_me[l:l + 1, 0:d] for l in range(n_layer)]
    scale = [mod_me[l:l + 1, d:2 * d] for l in range(n_layer)]
    gate = [mod_me[l:l + 1, 2 * d:3 * d] for l in range(n_layer)]

    g0, g1, gf = norm_g[0:1], norm_g[1:2], final_g[None, :]
    h0 = _norm_mod(x2d, g0, scale[0], shift[0], "norm_mod0")

    def slab(r):
        return jnp.bitwise_xor(chip_idx, r)

    def arrived(sems, group, weight, w, after, name, both_cores=True):
        return _gather_weights_wait(sems, [(0, j) for _, j in group], [weight], [w_kinds[w]], [w_shapes[w]],
                                    after, name, both_cores)[0]

    proj0 = _mm_proj_slab(h0, wa_in, None, slab(0), 4, "a_proj_own", a_token)
    wa_in = arrived(a_sems[0], a_groups[0], wa_in, 0, proj0, "gather_wait_near", both_cores=False)
    wa_in = _forward_to_sibling(wa_in, w_kinds[0], w_shapes[0], [0, 1], "forward_near")
    far_sems, (wa_in,), far_token = _gather_weights_start(
        [wa_in], w_kinds[:1], w_shapes[:1], far_groups, a_token, "gather_a_w_in_far_start")
    proj0 = _mm_proj_slab(h0, wa_in, proj0, slab(2), 4, "a_proj_x", far_token)
    proj0 = _mm_proj_slab(h0, wa_in, proj0, slab(1), 4, "a_proj_y", far_token)
    wa_in = arrived(far_sems[0], far_groups[0], wa_in, 0, proj0, "gather_wait_far")

    def start_whole(w, after, name, both_cores=True):
        sems, (flight,), token = _gather_weights_start(
            placed[w:w + 1], w_kinds[w:w + 1], w_shapes[w:w + 1], whole_group, after, name, both_cores)
        return sems[0], flight, token

    ao_sems, wa_out, ao_token = start_whole(1, wa_in, "gather_a_w_out_start")
    proj0 = _mm_proj_slab(h0, wa_in, proj0, slab(3), 4, "a_proj_far", ao_token)
    wa_out = arrived(ao_sems, whole_group[0], wa_out, 1, proj0, "gather_wait_a_w_out")
    bi_sems, wb_in, bi_token = start_whole(2, wa_out, "gather_b_w_in_start", both_cores=False)
    y0, br0, x1, h1 = _conv_fwd(proj0, conv_w, a_conv_b, wa_out, x2d, gate[0], g1, scale[1], shift[1],
                                "conv_fwd", bi_token)
    wb_in = arrived(bi_sems, whole_group[0], wb_in, 2, h1, "gather_wait_b_w_in", both_cores=False)
    wb_in = _forward_to_sibling(wb_in, w_kinds[2], w_shapes[2], [0, 1, 2], "forward_b_w_in")
    bo_sems, wb_out, bo_token = start_whole(3, wb_in, "gather_b_w_out_start")
    proj1 = _mm_proj(h1, wb_in, 3, "b_proj", bo_token)
    b_s_t = jnp.transpose(b_b_s[0])
    wb_out = arrived(bo_sems, whole_group[0], wb_out, 3, proj1, "gather_wait_b_w_out")
    y1, dx2, dbr1, loss_part, g_final_g, dgate1 = _gmlp_fwd_head(
        proj1, ln_g, ln_b, b_w_s[0], b_s_t, wb_out, x1, gate[1], gf, target, "gmlp_fwd_head")

    gw_b_out = _mm_dw_out(y1, dbr1, "b_out_dw")
    dproj1, g_w_s, g_b_s_t, g_ln_g, g_ln_b = _gmlp_bwd(
        proj1, dbr1, wb_out, ln_g, ln_b, b_w_s[0], jnp.swapaxes(b_w_s[0], 1, 2), b_s_t, "gmlp_bwd")
    gw_b_in = _mm_dw_in(h1, dproj1, "b_proj_dw")
    b_kinds = ["col", "row"]
    b_send, b_recv, b_grads, b_lands, b_token = _reduce_start(
        [gw_b_in, gw_b_out.reshape(N_CHIP, e // N_CHIP, d)], b_kinds, "reduce_b_start")
    dx1, dshift1, dscale1, g_g1, dbr0, dgate0 = _mm_dh_norm_bwd(
        dproj1, wb_in, x1, dx2, g1, scale[1], "b_proj_dx", b_token, br=br0, gate=gate[0])

    gw_a_out = _mm_dw_out(y0, dbr0, "a_out_dw")
    ao_send, ao_recv, ao_grads, ao_lands, ao_grad_token = _reduce_start(
        [gw_a_out.reshape(N_CHIP, e // N_CHIP, d)], ["row"], "reduce_a_out_start")
    dproj0, g_w0, g_w1, g_w2, g_conv_b = _conv_bwd(proj0, dbr0, wa_out, conv_w, a_conv_b, "conv_bwd", ao_grad_token)
    gw_a_in = _mm_dw_in(h0, dproj0, "a_proj_dw")
    ai_send, ai_recv, ai_grads, ai_lands, ai_token = _reduce_start([gw_a_in], ["col"], "reduce_a_in_start")
    grad_x, dshift0, dscale0, g_g0 = _mm_dh_norm_bwd(dproj0, wa_in, x2d, dx1, g0, scale[0], "a_proj_dx", ai_token)

    def finish(send, recv, grads_, lands_, kinds_, names_, after, tag):
        grads_, lands_ = _reduce_wait(send, recv, grads_, lands_, kinds_, after, "reduce_" + tag + "_wait")
        halves = [_add_pieces(g, land, kind, chip_idx, core_idx, "add_pieces_" + nm)
                  for g, land, kind, nm in zip(grads_, lands_, kinds_, names_)]
        return _join_halves(halves, "join_" + tag)

    upd, big_grads = {}, {}

    def adamw_big(nm, w, g, m, v):
        g_out, *rest = _adamw_2d(w[0], g, m[0], v[0], "adamw_" + nm)
        big_grads[nm] = g_out[None]
        upd[nm] = tuple(o[None] for o in rest)

    g_b_w_in, g_b_w_out = finish(b_send, b_recv, b_grads, b_lands, b_kinds, ["b_w_in", "b_w_out"], grad_x, "b")
    adamw_big("b_w_in", b_w_in, g_b_w_in, m_b_w_in, v_b_w_in)
    adamw_big("b_w_out", b_w_out, g_b_w_out, m_b_w_out, v_b_w_out)

    dmod = jnp.concatenate([dshift0, dscale0, dgate0, dshift1, dscale1, dgate1], axis=1)
    small = [loss_part[0, 0:1], g_final_g, g_g0, g_g1, jnp.concatenate([g_w0, g_w1, g_w2], axis=0), g_conv_b, g_ln_g, g_ln_b,
             jnp.transpose(g_b_s_t), g_w_s, dmod]
    packed, offs = _pack(small, 1024)
    rows = packed.shape[0]
    gathered = _all_gather(packed, "gather_small", after=upd["b_w_out"][0])
    summed = _sum_devices(gathered, rows, "sum_small").reshape(-1)

    def take(k, shape):
        size = math.prod(shape)
        return summed[offs[k]:offs[k] + size].reshape(shape)

    loss = take(0, ())
    grad_final_g = take(1, (d,))
    grad_norm_g = jnp.concatenate([take(2, (1, d)), take(3, (1, d))], axis=0)
    grad_conv_w_full = take(4, (3, e))
    grad_a_conv_b = take(5, (1, e))
    grad_ln_g_full = take(6, (1, e))
    grad_ln_b_full = take(7, (1, e))
    grad_b_b_s = take(8, (1, GROUPS, CHUNK))
    grad_b_w_s = take(9, (1, GROUPS, CHUNK, CHUNK))
    grad_mod_b = take(10, (n_layer, 3 * d))
    grad_a_conv_w = lax.dynamic_slice_in_dim(grad_conv_w_full, chip * es, es, axis=1)[None]
    grad_b_ln_g = lax.dynamic_slice_in_dim(grad_ln_g_full, chip * es, es, axis=1)
    grad_b_ln_b = lax.dynamic_slice_in_dim(grad_ln_b_full, chip * es, es, axis=1)
    dmod_all = gathered.reshape(N_DEV, rows * 1024)[:, offs[10]:offs[10] + n_layer * 3 * d]
    dmod_all = dmod_all.reshape(N_DEV, n_layer, 3 * d)
    dmod_cols = jnp.transpose(lax.dynamic_slice_in_dim(dmod_all, chip * wd, wd, axis=2), (1, 0, 2))

    grad_mod_w, delta_mod_w, new_m_mod_w, new_v_mod_w = _mod_w_update(
        jnp.transpose(c_act), dmod_cols, mod_w, m_mod_w, v_mod_w, "mod_w_update")
    (g_a_w_out,) = finish(ao_send, ao_recv, ao_grads, ao_lands, ["row"], ["a_w_out"], delta_mod_w, "a_out")
    adamw_big("a_w_out", a_w_out, g_a_w_out, m_a_w_out, v_a_w_out)
    (g_a_w_in,) = finish(ai_send, ai_recv, ai_grads, ai_lands, ["col"], ["a_w_in"], upd["a_w_out"][0], "a_in")
    adamw_big("a_w_in", a_w_in, g_a_w_in, m_a_w_in, v_a_w_in)
    small_w = [("mod_b", mod_b, grad_mod_b, m_mod_b, v_mod_b), ("norm_g", norm_g, grad_norm_g, m_norm_g, v_norm_g),
               ("a_conv_w", a_conv_w, grad_a_conv_w, m_a_conv_w, v_a_conv_w),
               ("a_conv_b", a_conv_b, grad_a_conv_b, m_a_conv_b, v_a_conv_b),
               ("b_ln_g", b_ln_g, grad_b_ln_g, m_b_ln_g, v_b_ln_g), ("b_ln_b", b_ln_b, grad_b_ln_b, m_b_ln_b, v_b_ln_b),
               ("b_w_s", b_w_s, grad_b_w_s, m_b_w_s, v_b_w_s), ("b_b_s", b_b_s, grad_b_b_s, m_b_b_s, v_b_b_s),
               ("final_g", final_g, grad_final_g, m_final_g, v_final_g)]

    def flat2d(a):
        return a.reshape(-1, a.shape[-1])

    res = _adamw_small([flat2d(t[1]) for t in small_w], [flat2d(t[2]) for t in small_w],
                       [flat2d(t[3]) for t in small_w], [flat2d(t[4]) for t in small_w], "adamw_small")
    for (nm, w, _, _, _), r3 in zip(small_w, res):
        upd[nm] = tuple(o.reshape(w.shape) for o in r3)
    upd["mod_w"] = (delta_mod_w, new_m_mod_w, new_v_mod_w)

    grads = {"mod_w": grad_mod_w, "mod_b": grad_mod_b, "norm_g": grad_norm_g, "a_conv_w": grad_a_conv_w,
             "a_conv_b": grad_a_conv_b, "b_ln_g": grad_b_ln_g, "b_ln_b": grad_b_ln_b, "b_w_s": grad_b_w_s,
             "b_b_s": grad_b_b_s, "final_g": grad_final_g, **big_grads}
    order = ["mod_w", "mod_b", "norm_g", "a_w_in", "a_conv_w", "a_conv_b", "a_w_out", "b_w_in", "b_ln_g", "b_ln_b",
             "b_w_s", "b_b_s", "b_w_out", "final_g"]
    return (loss, grad_x[None], *[grads[k] for k in order], *[upd[k][0] for k in order],
            *[upd[k][1] for k in order], *[upd[k][2] for k in order])
```

```python
import functools
import math

import jax
import jax.numpy as jnp
from jax import lax
from jax.experimental import pallas as pl
from jax.experimental.pallas import tpu as pltpu

F32 = jnp.float32
BF16 = jnp.bfloat16
MESH = pl.DeviceIdType.MESH

N_DEV = 8
N_CHIP = 4
SUBLANE = 8
PACK = 16
ACT = F32
RMS_EPS = 1e-6
LN_EPS = 1e-5
CHUNK = 128
GROUPS = 8
ADAM_LR = 0.001
ADAM_B1 = 0.9
ADAM_B2 = 0.999
ADAM_EPS = 1e-08
ADAM_WD = 0.01
ADAM_STEP = 10
VMEM_LIMIT = 56 << 20

HBM_SPEC = pl.BlockSpec(memory_space=pltpu.HBM)
VMEM_SPEC = pl.BlockSpec(memory_space=pltpu.VMEM)
SEM_SPEC = pl.BlockSpec(memory_space=pltpu.SEMAPHORE)
ANY_SPEC = pl.BlockSpec(memory_space=pl.ANY)
EFFECT = pltpu.SideEffectType.DATAFLOW_SIDE_EFFECTING


def _params(*sem):
    return pltpu.CompilerParams(dimension_semantics=sem, vmem_limit_bytes=VMEM_LIMIT)


def _tile(n, want):
    if n <= want:
        return n
    t = want
    while n % t:
        t -= 128
    return t


def _sigmoid(x):
    return 0.5 * jnp.tanh(0.5 * x) + 0.5


def _silu_and_grad(x):
    s = _sigmoid(x)
    return x * s, s * (1.0 + x * (1.0 - s))


def _gelu_and_grad(x):
    cdf = 0.5 * (1.0 + lax.erf(x * (1.0 / math.sqrt(2.0))))
    pdf = jnp.exp(-0.5 * x * x) * (1.0 / math.sqrt(2.0 * math.pi))
    return x * cdf, cdf + x * pdf


def _gelu(x):
    return x * (0.5 * (1.0 + lax.erf(x * (1.0 / math.sqrt(2.0)))))


def _rms(x):
    r = lax.rsqrt(jnp.mean(x * x, axis=-1, keepdims=True) + RMS_EPS)
    return x * r, r


def _rms_bwd(dxn, xn, r):
    return r * (dxn - xn * jnp.mean(dxn * xn, axis=-1, keepdims=True))


def _colsum(a):
    return jnp.sum(a, axis=0, keepdims=True)


def _shift_down(cur, before, k):
    rolled = pltpu.roll(cur, k, 0)
    row = lax.broadcasted_iota(jnp.int32, before.shape, 0)
    head = jnp.where(row < k, pltpu.roll(before, k, 0), rolled[:SUBLANE])
    return jnp.concatenate([head, rolled[SUBLANE:]], axis=0)


def _shift_up(cur, after, k):
    n = cur.shape[0]
    rolled = pltpu.roll(cur, n - k, 0)
    row = lax.broadcasted_iota(jnp.int32, after.shape, 0)
    tail = jnp.where(row >= SUBLANE - k, pltpu.roll(after, SUBLANE - k, 0), rolled[n - SUBLANE:])
    return jnp.concatenate([rolled[:n - SUBLANE], tail], axis=0)


def _after_spec():
    return pl.BlockSpec((SUBLANE, 128), lambda *_: (0, 0))


def _mm_proj(h, w, n_split, name, after):
    s, d = h.shape
    e = w.shape[1] // n_split
    tm, tn = _tile(s, 1024), _tile(e, 2048)
    nj = e // tn

    def body(h_ref, w_ref, after_ref, o_ref):
        o_ref[...] = jnp.dot(h_ref[...], w_ref[...], preferred_element_type=F32).astype(ACT)

    return pl.pallas_call(
        body, name=name,
        out_shape=jax.ShapeDtypeStruct((n_split, s, e), ACT),
        grid=(s // tm, n_split * nj),
        in_specs=[pl.BlockSpec((tm, d), lambda i, j: (i, 0)), pl.BlockSpec((d, tn), lambda i, j: (0, j)),
                  _after_spec()],
        out_specs=pl.BlockSpec((None, tm, tn), lambda i, j: (j // nj, i, j % nj)),
        compiler_params=_params("parallel", "parallel"),
    )(h, w, after)


def _mm_proj_slab(h, w, proj, q_idx, n_split, name, after):
    s, d = h.shape
    e = w.shape[1] // n_split
    tm, tn = _tile(s, 1024), _tile(e, 2048)
    nj = e // tn

    def body(q_ref, h_ref, w_ref, *rest):
        o_ref = rest[-1]
        o_ref[...] = jnp.dot(h_ref[...], w_ref[...], preferred_element_type=F32).astype(ACT)

    in_specs = [pl.BlockSpec((tm, d), lambda i, j, qr: (i, 0)), pl.BlockSpec((d, tn), lambda i, j, qr: (0, qr[0] * nj + j)),
                _after_spec()]
    args = [q_idx, h, w, after]
    aliases = {}
    if proj is not None:
        in_specs.append(ANY_SPEC)
        args.append(proj)
        aliases = {4: 0}
    return pl.pallas_call(
        body, name=name,
        out_shape=jax.ShapeDtypeStruct((n_split, s, e), ACT),
        grid_spec=pltpu.PrefetchScalarGridSpec(
            num_scalar_prefetch=1, grid=(s // tm, nj), in_specs=in_specs,
            out_specs=pl.BlockSpec((None, tm, tn), lambda i, j, qr: (qr[0], i, j))),
        input_output_aliases=aliases,
        compiler_params=_params("parallel", "parallel"),
    )(*args)


def _mm_dh_norm_bwd(dp, w, x, dx_in, g, scale, name, after, br=None, gate=None):
    nq, s, e = dp.shape
    d = w.shape[0]
    has_branch = br is not None
    tm, tk = _tile(s, 1024), _tile(e, 1024 if has_branch else 2048)
    nkq = e // tk
    nk = nq * nkq

    def body(*refs):
        if has_branch:
            (a_ref, b_ref, after_ref, x_ref, dxin_ref, g_ref, sc_ref, br_ref, gate_ref,
             dx_ref, dsh_ref, dsc_ref, dg_ref, dbr_ref, dgate_ref, acc_ref) = refs
        else:
            (a_ref, b_ref, after_ref, x_ref, dxin_ref, g_ref, sc_ref,
             dx_ref, dsh_ref, dsc_ref, dg_ref, acc_ref) = refs
        i, k = pl.program_id(0), pl.program_id(1)

        @pl.when(jnp.logical_and(i == 0, k == 0))
        def _():
            dsh_ref[...] = jnp.zeros_like(dsh_ref)
            dsc_ref[...] = jnp.zeros_like(dsc_ref)
            dg_ref[...] = jnp.zeros_like(dg_ref)
            if has_branch:
                dgate_ref[...] = jnp.zeros_like(dgate_ref)

        @pl.when(k == 0)
        def _():
            acc_ref[...] = jnp.zeros_like(acc_ref)
        acc_ref[...] += lax.dot_general(a_ref[...], b_ref[...], (((1,), (1,)), ((), ())), preferred_element_type=F32)

        @pl.when(k == nk - 1)
        def _():
            dh = acc_ref[...]
            g_vec = g_ref[...]
            xn, r = _rms(x_ref[...])
            dsh_ref[...] += _colsum(dh)
            dsc_ref[...] += _colsum(dh * (xn * g_vec))
            da = dh * (1.0 + sc_ref[...])
            dg_ref[...] += _colsum(da * xn)
            dx = dxin_ref[...] + _rms_bwd(da * g_vec, xn, r)
            dx_ref[...] = dx
            if has_branch:
                dgate_ref[...] += _colsum(dx * br_ref[...])
                dbr_ref[...] = (gate_ref[...] * dx).astype(BF16)

    rows = pl.BlockSpec((tm, d), lambda i, k: (i, 0))
    vec = pl.BlockSpec((1, d), lambda i, k: (0, 0))
    vec_out = jax.ShapeDtypeStruct((1, d), F32)
    in_specs = [pl.BlockSpec((None, tm, tk), lambda i, k: (k // nkq, i, k % nkq)),
                pl.BlockSpec((d, tk), lambda i, k: (0, k)), _after_spec(), rows, rows, vec, vec]
    out_shape = [jax.ShapeDtypeStruct((s, d), F32), vec_out, vec_out, vec_out]
    out_specs = [rows, vec, vec, vec]
    args = [dp, w, after, x, dx_in, g, scale]
    if has_branch:
        in_specs += [rows, vec]
        out_shape += [jax.ShapeDtypeStruct((s, d), BF16), vec_out]
        out_specs += [rows, vec]
        args += [br, gate]
    return pl.pallas_call(
        body, name=name, out_shape=tuple(out_shape), grid=(s // tm, nk),
        in_specs=in_specs, out_specs=tuple(out_specs), scratch_shapes=[pltpu.VMEM((tm, d), F32)],
        compiler_params=_params("arbitrary", "arbitrary"),
    )(*args)


def _dw_body(n_t):
    def body(a_ref, b_ref, o_ref, acc_ref):
        t = pl.program_id(2)

        @pl.when(t == 0)
        def _():
            acc_ref[...] = jnp.zeros_like(acc_ref)
        acc_ref[...] += lax.dot_general(a_ref[...], b_ref[...], (((0,), (0,)), ((), ())), preferred_element_type=F32)

        @pl.when(t == n_t - 1)
        def _():
            o_ref[...] = acc_ref[...].astype(o_ref.dtype)
    return body


def _mm_dw_in(h, dp, name):
    s, d = h.shape
    nq, _, e = dp.shape
    tm, tn, tt = _tile(d, 1024), _tile(e, 1024), _tile(s, 2048)
    nj = e // tn

    return pl.pallas_call(
        _dw_body(s // tt), name=name,
        out_shape=jax.ShapeDtypeStruct((d, nq * e), BF16),
        grid=(d // tm, nq * nj, s // tt),
        in_specs=[pl.BlockSpec((tt, tm), lambda i, j, t: (t, i)),
                  pl.BlockSpec((None, tt, tn), lambda i, j, t: (j // nj, t, j % nj))],
        out_specs=pl.BlockSpec((tm, tn), lambda i, j, t: (i, j)),
        scratch_shapes=[pltpu.VMEM((tm, tn), F32)],
        compiler_params=_params("parallel", "parallel", "arbitrary"),
    )(h, dp)


def _mm_dw_out(y, dbr, name):
    s, e = y.shape
    d = dbr.shape[1]
    tm, tn, tt = _tile(e, 1024), _tile(d, 1024), _tile(s, 2048)

    return pl.pallas_call(
        _dw_body(s // tt), name=name,
        out_shape=jax.ShapeDtypeStruct((e, d), BF16),
        grid=(e // tm, d // tn, s // tt),
        in_specs=[pl.BlockSpec((tt, tm), lambda i, j, t: (t, i)), pl.BlockSpec((tt, tn), lambda i, j, t: (t, j))],
        out_specs=pl.BlockSpec((tm, tn), lambda i, j, t: (i, j)),
        scratch_shapes=[pltpu.VMEM((tm, tn), F32)],
        compiler_params=_params("parallel", "parallel", "arbitrary"),
    )(y, dbr)


def _row_spec(ts, d):
    return pl.BlockSpec((ts, d), lambda i: (i, 0))


def _vec_spec(d):
    return pl.BlockSpec((1, d), lambda i: (0, 0))


def _norm_mod(x, g, scale, shift, name):
    s, d = x.shape
    ts = _tile(s, 512)

    def body(x_ref, g_ref, sc_ref, sh_ref, h_ref):
        xn, _ = _rms(x_ref[...])
        h_ref[...] = ((xn * g_ref[...]) * (1.0 + sc_ref[...]) + sh_ref[...]).astype(BF16)

    return pl.pallas_call(
        body, name=name, out_shape=jax.ShapeDtypeStruct((s, d), BF16), grid=(s // ts,),
        in_specs=[_row_spec(ts, d), _vec_spec(d), _vec_spec(d), _vec_spec(d)],
        out_specs=_row_spec(ts, d), compiler_params=_params("parallel"),
    )(x, g, scale, shift)


def _head_tile(x1, br, gate, gf, target):
    d = x1.shape[-1]
    xn, r = _rms(x1 + gate * br)
    err = xn * gf - target
    loss = 0.5 * jnp.sum(jnp.mean(err * err, axis=-1, keepdims=True))
    dout = err * (1.0 / d)
    dx = _rms_bwd(dout * gf, xn, r)
    return dx, gate * dx, loss, _colsum(dout * xn), _colsum(dx * br)


def _f32(ref):
    return ref[...].astype(F32)


def _cols_f32(ref, cols):
    return ref[:, cols].astype(F32)


def _rows_before(halo_ref, cols=slice(None)):
    return _cols_f32(halo_ref, cols)[PACK - SUBLANE:]


def _rows_after(halo_ref, cols=slice(None)):
    return _cols_f32(halo_ref, cols)[:SUBLANE]


CONV_CHAIN_COLS = 512


def _conv_fwd(proj, conv_w, conv_b, w_out, x, gate, g_next, scale_next, shift_next, name, after):
    _, s, e = proj.shape
    d = w_out.shape[1]
    ts, te = _tile(s, 256), e
    cw = min(CONV_CHAIN_COLS, te)
    hb = ts // PACK

    def body(bg_ref, cg_ref, xi_ref, z_ref, cgp_ref, xip_ref, w0_ref, w1_ref, w2_ref, b_ref, wo_ref, after_ref,
             x_ref, gate_ref, g_ref, sc_ref, sh_ref, y_ref, br_ref, x1_ref, h_ref):
        first = pl.program_id(0) == 0
        br = None
        for c0 in range(0, te, cw):
            cols = slice(c0, c0 + cw)
            cx = _cols_f32(cg_ref, cols) * _cols_f32(xi_ref, cols)
            before = jnp.where(first, 0.0, _rows_before(cgp_ref, cols) * _rows_before(xip_ref, cols))
            conv = b_ref[:, cols] + w2_ref[:, cols] * cx
            conv = conv + w0_ref[:, cols] * _shift_down(cx, before, 2)
            conv = conv + w1_ref[:, cols] * _shift_down(cx, before, 1)
            z = _cols_f32(z_ref, cols)
            y = ((z * _sigmoid(z)) * _cols_f32(bg_ref, cols) * conv).astype(BF16)
            y_ref[:, cols] = y
            part_br = jnp.dot(y, wo_ref[cols, :], preferred_element_type=F32)
            br = part_br if br is None else br + part_br
        br_ref[...] = br
        x1 = x_ref[...] + gate_ref[...] * br
        x1_ref[...] = x1
        xn, _ = _rms(x1)
        h_ref[...] = ((xn * g_ref[...]) * (1.0 + sc_ref[...]) + sh_ref[...]).astype(BF16)

    def part(q):
        return pl.BlockSpec((None, ts, te), lambda i: (q, i, 0))

    def halo_before(q):
        return pl.BlockSpec((None, PACK, te), lambda i: (q, jnp.maximum(i * hb - 1, 0), 0))

    return pl.pallas_call(
        body, name=name,
        out_shape=(jax.ShapeDtypeStruct((s, e), BF16), jax.ShapeDtypeStruct((s, d), F32),
                   jax.ShapeDtypeStruct((s, d), F32), jax.ShapeDtypeStruct((s, d), BF16)), grid=(s // ts,),
        in_specs=[part(0), part(1), part(2), part(3), halo_before(1), halo_before(2)]
        + [_vec_spec(e)] * 4 + [pl.BlockSpec((e, d), lambda i: (0, 0)), _after_spec(), _row_spec(ts, d)]
        + [_vec_spec(d)] * 4,
        out_specs=(_row_spec(ts, e), _row_spec(ts, d), _row_spec(ts, d), _row_spec(ts, d)),
        compiler_params=_params("parallel"),
    )(proj, proj, proj, proj, proj, proj, *conv_w, conv_b, w_out, after, x, gate, g_next, scale_next, shift_next)


def _conv_bwd(proj, dbr, w_out, conv_w, conv_b, name, after):
    _, s, e = proj.shape
    d = dbr.shape[1]
    ts, te = _tile(s, 512), _tile(e, 1024)
    cw = min(CONV_CHAIN_COLS // 2, te)
    hb = ts // PACK
    n_i = s // ts
    last_halo = s // PACK - 1
    nt = (((1,), (1,)), ((), ()))

    def body(bg_ref, cg_ref, xi_ref, z_ref, dbr_ref, cgp_ref, xip_ref, bgn_ref, zn_ref, dbrn_ref, wo_ref,
             w0_ref, w1_ref, w2_ref, b_ref, after_ref, dp_ref, dw0_ref, dw1_ref, dw2_ref, db_ref):
        i = pl.program_id(1)

        @pl.when(i == 0)
        def _():
            for acc in (dw0_ref, dw1_ref, dw2_ref, db_ref):
                acc[...] = jnp.zeros_like(acc)
        for c0 in range(0, te, cw):
            cols = slice(c0, c0 + cw)
            wo = wo_ref[cols, :]
            dy = lax.dot_general(dbr_ref[...], wo, nt, preferred_element_type=F32)
            dyn = lax.dot_general(dbrn_ref[...], wo, nt, preferred_element_type=F32)[:SUBLANE]
            bg, cg = _cols_f32(bg_ref, cols), _cols_f32(cg_ref, cols)
            xi, z = _cols_f32(xi_ref, cols), _cols_f32(z_ref, cols)
            w0, w1, w2 = w0_ref[:, cols], w1_ref[:, cols], w2_ref[:, cols]
            cx = cg * xi
            before = jnp.where(i > 0, _rows_before(cgp_ref, cols) * _rows_before(xip_ref, cols), 0.0)
            cx1 = _shift_down(cx, before, 1)
            cx2 = _shift_down(cx, before, 2)
            conv = b_ref[:, cols] + w2 * cx
            conv = conv + w0 * cx2
            conv = conv + w1 * cx1
            sz, dsz = _silu_and_grad(z)
            dp_ref[3, :, cols] = (dy * bg * conv * dsz).astype(BF16)
            dp_ref[0, :, cols] = (dy * sz * conv).astype(BF16)
            dconv = dy * sz * bg
            zn = _rows_after(zn_ref, cols)
            after = jnp.where(i < n_i - 1, dyn * (zn * _sigmoid(zn)) * _rows_after(bgn_ref, cols), 0.0)
            db_ref[:, cols] += _colsum(dconv)
            dw2_ref[:, cols] += _colsum(dconv * cx)
            dw1_ref[:, cols] += _colsum(dconv * cx1)
            dw0_ref[:, cols] += _colsum(dconv * cx2)
            dcx = w2 * dconv + w1 * _shift_up(dconv, after, 1) + w0 * _shift_up(dconv, after, 2)
            dp_ref[1, :, cols] = (dcx * xi).astype(BF16)
            dp_ref[2, :, cols] = (dcx * cg).astype(BF16)

    def part(q):
        return pl.BlockSpec((None, ts, te), lambda j, i: (q, i, j))

    def halo_before(q):
        return pl.BlockSpec((None, PACK, te), lambda j, i: (q, jnp.maximum(i * hb - 1, 0), j))

    def halo_after(q):
        return pl.BlockSpec((None, PACK, te), lambda j, i: (q, jnp.minimum((i + 1) * hb, last_halo), j))

    return pl.pallas_call(
        body, name=name,
        out_shape=(jax.ShapeDtypeStruct((4, s, e), BF16),) + (jax.ShapeDtypeStruct((1, e), F32),) * 4,
        grid=(e // te, n_i),
        in_specs=[part(0), part(1), part(2), part(3), pl.BlockSpec((ts, d), lambda j, i: (i, 0)),
                  halo_before(1), halo_before(2), halo_after(0), halo_after(3),
                  pl.BlockSpec((PACK, d), lambda j, i: (jnp.minimum((i + 1) * hb, last_halo), 0)),
                  pl.BlockSpec((te, d), lambda j, i: (j, 0))]
        + [pl.BlockSpec((1, te), lambda j, i: (0, j))] * 4 + [_after_spec()],
        out_specs=(pl.BlockSpec((4, ts, te), lambda j, i: (0, i, j)),) + (pl.BlockSpec((1, te), lambda j, i: (0, j)),) * 4,
        compiler_params=_params("parallel", "arbitrary"),
    )(proj, proj, proj, proj, dbr, proj, proj, proj, proj, dbr, w_out, *conv_w, conv_b, after)


def _tril(w):
    row = lax.broadcasted_iota(jnp.int32, w.shape, 0)
    col = lax.broadcasted_iota(jnp.int32, w.shape, 1)
    return jnp.where(row >= col, w, 0.0)


def _triu(w):
    row = lax.broadcasted_iota(jnp.int32, w.shape, 0)
    col = lax.broadcasted_iota(jnp.int32, w.shape, 1)
    return jnp.where(row <= col, w, 0.0)


def _layer_norm_fwd(v, g, b):
    mu = jnp.mean(v, axis=-1, keepdims=True)
    vc = v - mu
    rstd = lax.rsqrt(jnp.mean(vc * vc, axis=-1, keepdims=True) + LN_EPS)
    vhat = vc * rstd
    return vhat * g + b, vhat, rstd


GMLP_CHUNKS_PER_STEP = 2


def _gmlp_rows(s):
    return CHUNK * min(GMLP_CHUNKS_PER_STEP, s // CHUNK)


def _mix_positions(w_ref, src_scr, dst_scr, gw, mask, bias_ref=None):
    for gi in range(GROUPS):
        cols = slice(gi * gw, (gi + 1) * gw)
        wm = mask(w_ref[gi]).astype(BF16)
        for n in range(src_scr.shape[0] // CHUNK):
            rows = slice(n * CHUNK, (n + 1) * CHUNK)
            out = jnp.dot(wm, src_scr[rows, cols], preferred_element_type=F32)
            if bias_ref is not None:
                out = out + bias_ref[:, gi:gi + 1]
            dst_scr[rows, cols] = out


def _gmlp_fwd_head(proj, ln_g, ln_b, w_s, b_s_t, w_out, x1, gate, gf, target, name):
    _, s, e = proj.shape
    d = w_out.shape[1]
    gw = e // GROUPS
    ts = _gmlp_rows(s)

    def body(pu_ref, pv_ref, pz_ref, g_ref, b_ref, ws_ref, bs_ref, wo_ref, x1_ref, gate_ref, gf_ref, tg_ref,
             y_ref, dx_ref, dbr_ref, loss_ref, dgf_ref, dgate_ref, vn_scr, mix_scr):
        @pl.when(pl.program_id(0) == 0)
        def _():
            loss_ref[...] = jnp.zeros_like(loss_ref)
            dgf_ref[...] = jnp.zeros_like(dgf_ref)
            dgate_ref[...] = jnp.zeros_like(dgate_ref)
        vn, _, _ = _layer_norm_fwd(_gelu(_f32(pv_ref)), g_ref[...], b_ref[...])
        vn_scr[...] = vn.astype(BF16)
        _mix_positions(ws_ref, vn_scr, mix_scr, gw, _tril, bs_ref)
        z = _f32(pz_ref)
        y = ((z * _sigmoid(z)) * (_gelu(_f32(pu_ref)) * mix_scr[...])).astype(BF16)
        y_ref[...] = y
        br = jnp.dot(y, wo_ref[...], preferred_element_type=F32)
        dx, dbr, loss, dgf, dgate = _head_tile(x1_ref[...], br, gate_ref[...], gf_ref[...], tg_ref[...])
        dx_ref[...] = dx
        dbr_ref[...] = dbr.astype(BF16)
        loss_ref[...] += loss
        dgf_ref[...] += dgf
        dgate_ref[...] += dgate

    def part(q):
        return pl.BlockSpec((None, ts, e), lambda i: (q, i, 0))

    return pl.pallas_call(
        body, name=name,
        out_shape=(jax.ShapeDtypeStruct((s, e), BF16), jax.ShapeDtypeStruct((s, d), F32),
                   jax.ShapeDtypeStruct((s, d), BF16), jax.ShapeDtypeStruct((SUBLANE, 128), F32),
                   jax.ShapeDtypeStruct((1, d), F32), jax.ShapeDtypeStruct((1, d), F32)),
        grid=(s // ts,),
        in_specs=[part(0), part(1), part(2), _vec_spec(e), _vec_spec(e),
                  pl.BlockSpec((GROUPS, CHUNK, CHUNK), lambda i: (0, 0, 0)),
                  pl.BlockSpec((CHUNK, GROUPS), lambda i: (0, 0)), pl.BlockSpec((e, d), lambda i: (0, 0)),
                  _row_spec(ts, d), _vec_spec(d), _vec_spec(d), _row_spec(ts, d)],
        out_specs=(_row_spec(ts, e), _row_spec(ts, d), _row_spec(ts, d),
                   pl.BlockSpec((SUBLANE, 128), lambda i: (0, 0)), _vec_spec(d), _vec_spec(d)),
        scratch_shapes=[pltpu.VMEM((ts, e), BF16), pltpu.VMEM((ts, e), F32)],
        compiler_params=_params("arbitrary"),
    )(proj, proj, proj, ln_g, ln_b, w_s, b_s_t, w_out, x1, gate, gf, target)


def _gmlp_bwd(proj, dbr, w_out, ln_g, ln_b, w_s, w_s_t, b_s_t, name):
    _, s, e = proj.shape
    d = dbr.shape[1]
    gw = e // GROUPS
    ts = _gmlp_rows(s)
    n_i = s // ts

    def body(pu_ref, pv_ref, pz_ref, dbr_ref, wo_ref, g_ref, b_ref, ws_ref, wst_ref, bs_ref,
             dp_ref, dws_ref, dbs_ref, dlg_ref, dlb_ref, vn_scr, mix_scr, dm_scr, dvn_scr, dmacc_scr):
        i = pl.program_id(0)

        @pl.when(i == 0)
        def _():
            dws_ref[...] = jnp.zeros_like(dws_ref)
            dlg_ref[...] = jnp.zeros_like(dlg_ref)
            dlb_ref[...] = jnp.zeros_like(dlb_ref)
            dmacc_scr[...] = jnp.zeros_like(dmacc_scr)
        ln_g = g_ref[...]
        u, du_dpu = _gelu_and_grad(_f32(pu_ref))
        v, dv_dpv = _gelu_and_grad(_f32(pv_ref))
        vn, vhat, rstd = _layer_norm_fwd(v, ln_g, b_ref[...])
        vn_scr[...] = vn.astype(BF16)
        _mix_positions(ws_ref, vn_scr, mix_scr, gw, _tril, bs_ref)
        mixed = mix_scr[...]
        dy = lax.dot_general(dbr_ref[...], wo_ref[...], (((1,), (1,)), ((), ())), preferred_element_type=F32)
        sz, dsz = _silu_and_grad(_f32(pz_ref))
        ds = dy * sz
        dp_ref[2] = (dy * (u * mixed) * dsz).astype(BF16)
        dp_ref[0] = (ds * mixed * du_dpu).astype(BF16)
        dm = ds * u
        dm_scr[...] = dm.astype(BF16)
        for n in range(ts // CHUNK):
            rows = slice(n * CHUNK, (n + 1) * CHUNK)
            dmacc_scr[...] += dm[rows]
            for gi in range(GROUPS):
                cols = slice(gi * gw, (gi + 1) * gw)
                dws_ref[gi] += lax.dot_general(dm_scr[rows, cols], vn_scr[rows, cols], (((1,), (1,)), ((), ())),
                                               preferred_element_type=F32)
        _mix_positions(wst_ref, dm_scr, dvn_scr, gw, _triu)
        dvn = dvn_scr[...]
        dlg_ref[...] += _colsum(dvn * vhat)
        dlb_ref[...] += _colsum(dvn)
        dvh = dvn * ln_g
        dv = rstd * (dvh - jnp.mean(dvh, axis=-1, keepdims=True) - vhat * jnp.mean(dvh * vhat, axis=-1, keepdims=True))
        dp_ref[1] = (dv * dv_dpv).astype(BF16)

        @pl.when(i == n_i - 1)
        def _():
            for gi in range(GROUPS):
                dws_ref[gi] = _tril(dws_ref[gi])
                dbs_ref[:, gi:gi + 1] = jnp.sum(dmacc_scr[:, gi * gw:(gi + 1) * gw], axis=1, keepdims=True)

    def part(q):
        return pl.BlockSpec((None, ts, e), lambda i: (q, i, 0))

    w_spec = pl.BlockSpec((GROUPS, CHUNK, CHUNK), lambda i: (0, 0, 0))
    bs_spec = pl.BlockSpec((CHUNK, GROUPS), lambda i: (0, 0))
    return pl.pallas_call(
        body, name=name,
        out_shape=(jax.ShapeDtypeStruct((3, s, e), BF16), jax.ShapeDtypeStruct((GROUPS, CHUNK, CHUNK), F32),
                   jax.ShapeDtypeStruct((CHUNK, GROUPS), F32), jax.ShapeDtypeStruct((1, e), F32),
                   jax.ShapeDtypeStruct((1, e), F32)),
        grid=(n_i,),
        in_specs=[part(0), part(1), part(2), pl.BlockSpec((ts, d), lambda i: (i, 0)),
                  pl.BlockSpec((e, d), lambda i: (0, 0)), _vec_spec(e), _vec_spec(e), w_spec, w_spec, bs_spec],
        out_specs=(pl.BlockSpec((3, ts, e), lambda i: (0, i, 0)), w_spec, bs_spec, _vec_spec(e), _vec_spec(e)),
        scratch_shapes=[pltpu.VMEM((ts, e), BF16), pltpu.VMEM((ts, e), F32), pltpu.VMEM((ts, e), BF16),
                        pltpu.VMEM((ts, e), F32), pltpu.VMEM((CHUNK, e), F32)],
        compiler_params=_params("arbitrary"),
    )(proj, proj, proj, dbr, w_out, ln_g, ln_b, w_s, w_s_t, b_s_t)


def _mod_fwd(c_all, mod_w, mod_b_cols, name):
    n_layer, d, w = mod_w.shape

    def body(c_ref, w_ref, b_ref, ca_ref, o_ref):
        c = c_ref[...]
        ca = c * _sigmoid(c)
        ca_ref[...] = ca
        for li in range(n_layer):
            o_ref[li * N_DEV:(li + 1) * N_DEV, :] = (
                jnp.dot(ca, w_ref[li], preferred_element_type=F32, precision=lax.Precision.HIGHEST) + b_ref[li])

    return pl.pallas_call(
        body, name=name,
        out_shape=(jax.ShapeDtypeStruct((N_DEV, d), F32), jax.ShapeDtypeStruct((n_layer * N_DEV, w), F32)),
        in_specs=[VMEM_SPEC] * 3, out_specs=(VMEM_SPEC, VMEM_SPEC),
        compiler_params=pltpu.CompilerParams(vmem_limit_bytes=VMEM_LIMIT),
    )(c_all, mod_w, mod_b_cols)


def _adamw(w, g, m, v):
    m = ADAM_B1 * m + (1.0 - ADAM_B1) * g
    v = ADAM_B2 * v + (1.0 - ADAM_B2) * (g * g)
    m_hat = m / (1.0 - ADAM_B1 ** ADAM_STEP)
    v_hat = v / (1.0 - ADAM_B2 ** ADAM_STEP)
    delta = -ADAM_LR * (m_hat / (jnp.sqrt(v_hat) + ADAM_EPS) + ADAM_WD * w)
    return delta, m, v


def _adamw_2d(w, g, m, v, name):
    r, c = w.shape
    tr, tc = _tile(r, 512), _tile(c, 1024)

    def body(w_ref, g_ref, m_ref, v_ref, go_ref, d_ref, nm_ref, nv_ref):
        g = g_ref[...]
        go_ref[...] = g
        d_ref[...], nm_ref[...], nv_ref[...] = _adamw(w_ref[...], g, m_ref[...], v_ref[...])

    spec = pl.BlockSpec((tr, tc), lambda i, j: (i, j))
    shape = jax.ShapeDtypeStruct((r, c), F32)
    return pl.pallas_call(
        body, name=name, out_shape=(shape,) * 4, grid=(r // tr, c // tc),
        in_specs=[spec] * 4, out_specs=(spec,) * 4, compiler_params=_params("parallel", "parallel"),
    )(w, g, m, v)


def _mod_w_update(ca_t, dmod_cols, w, m, v, name):
    n_layer, d, wd = w.shape
    tr = _tile(d, 256)

    def body(ca_ref, dm_ref, w_ref, m_ref, v_ref, g_ref, d_ref, nm_ref, nv_ref):
        ca = ca_ref[...]
        dm = dm_ref[...]
        g = ca[:, 0:1] * dm[0:1, :]
        for b in range(1, N_DEV):
            g = g + ca[:, b:b + 1] * dm[b:b + 1, :]
        g_ref[...] = g
        d_ref[...], nm_ref[...], nv_ref[...] = _adamw(w_ref[...], g, m_ref[...], v_ref[...])

    spec = pl.BlockSpec((None, tr, wd), lambda l, i: (l, i, 0))
    shape = jax.ShapeDtypeStruct((n_layer, d, wd), F32)
    return pl.pallas_call(
        body, name=name, out_shape=(shape,) * 4, grid=(n_layer, d // tr),
        in_specs=[pl.BlockSpec((tr, N_DEV), lambda l, i: (i, 0)), pl.BlockSpec((None, N_DEV, wd), lambda l, i: (l, 0, 0)),
                  spec, spec, spec],
        out_specs=(spec,) * 4, compiler_params=_params("parallel", "parallel"),
    )(ca_t, dmod_cols, w, m, v)


def _adamw_small(ws, gs, ms, vs, name):
    n = len(ws)

    def body(*refs):
        ins, outs = refs[:4 * n], refs[4 * n:]
        for k in range(n):
            delta, nm, nv = _adamw(ins[k][...], ins[n + k][...], ins[2 * n + k][...], ins[3 * n + k][...])
            outs[3 * k][...] = delta
            outs[3 * k + 1][...] = nm
            outs[3 * k + 2][...] = nv

    out_shape = []
    for w in ws:
        out_shape += [jax.ShapeDtypeStruct(w.shape, F32)] * 3
    outs = pl.pallas_call(
        body, name=name, out_shape=tuple(out_shape),
        in_specs=[VMEM_SPEC] * (4 * n), out_specs=tuple([VMEM_SPEC] * (3 * n)),
        compiler_params=pltpu.CompilerParams(vmem_limit_bytes=VMEM_LIMIT),
    )(*ws, *gs, *ms, *vs)
    return [tuple(outs[3 * k:3 * k + 3]) for k in range(n)]


def _place():
    return lax.axis_index("x"), lax.axis_index("y"), lax.axis_index("c")


def _other_chips(x, y):
    return [(1 - x, y), (x, 1 - y), (1 - x, 1 - y)]


def _all_gather(block, name, after=None):
    m_per, n = block.shape

    def body(x_ref, *rest):
        out_ref, send_sems, recv_sems, local_sem = rest[-4:]
        x, y, c = _place()
        me, sibling = (x, y, c), (x, y, 1 - c)
        chips = _other_chips(x, y)

        def rows(px, py, pc):
            return out_ref.at[pl.ds((4 * px + 2 * py + pc) * m_per, m_per), :]

        def copy(k, blk, to, src=None):
            return pltpu.make_async_remote_copy(
                src_ref=rows(*blk) if src is None else src, dst_ref=rows(*blk),
                send_sem=send_sems.at[k], recv_sem=recv_sems.at[k], device_id=to, device_id_type=MESH)

        mine = pltpu.make_async_copy(x_ref, rows(*me), local_sem)
        mine.start()
        first = [copy(0, me, sibling, src=x_ref)]
        first += [copy(1 + j, me, (*chip, c), src=x_ref) for j, chip in enumerate(chips)]
        for cp in first:
            cp.start()
        passed = [copy(4 + j, (*chip, c), sibling) for j, chip in enumerate(chips)]
        for j, chip in enumerate(chips):
            copy(1 + j, (*chip, c), me).wait_recv()
            passed[j].start()
        copy(0, sibling, me).wait_recv()
        for j, chip in enumerate(chips):
            copy(4 + j, (*chip, 1 - c), me).wait_recv()
        for cp in first + passed:
            cp.wait_send()
        mine.wait()

    return pl.pallas_call(
        body, name=name, out_shape=jax.ShapeDtypeStruct((N_DEV * m_per, n), F32),
        in_specs=[VMEM_SPEC] + ([] if after is None else [ANY_SPEC]), out_specs=VMEM_SPEC,
        scratch_shapes=[pltpu.SemaphoreType.DMA((7,)), pltpu.SemaphoreType.DMA((7,)), pltpu.SemaphoreType.DMA],
        compiler_params=pltpu.CompilerParams(vmem_limit_bytes=VMEM_LIMIT),
    )(*([block] if after is None else [block, after]))


def _hbm(a):
    return pltpu.with_memory_space_constraint(a, pltpu.HBM)


def _all_gather_start(blocks, name):
    def body(in_ref, send_sems, recv_sems, thru_ref):
        x, y, c = _place()
        mine = in_ref.at[4 * x + 2 * y + c]
        pltpu.make_async_remote_copy(
            src_ref=mine, dst_ref=mine, send_sem=send_sems.at[0], recv_sem=recv_sems.at[0],
            device_id=(x, y, 1 - c), device_id_type=MESH).start()
        for j, chip in enumerate(_other_chips(x, y)):
            for cc in range(2):
                pltpu.make_async_remote_copy(
                    src_ref=mine, dst_ref=mine, send_sem=send_sems.at[1 + 2 * j + cc],
                    recv_sem=recv_sems.at[1 + 2 * j + c], device_id=(*chip, cc), device_id_type=MESH).start()

    sems = pltpu.SemaphoreType.DMA((N_DEV - 1,))
    return pl.pallas_call(
        body, name=name, out_shape=(sems, sems, pltpu.HBM(blocks.shape, blocks.dtype)),
        in_specs=[HBM_SPEC], out_specs=(SEM_SPEC, SEM_SPEC, HBM_SPEC), input_output_aliases={0: 2},
        compiler_params=pltpu.CompilerParams(has_side_effects=EFFECT),
    )(_hbm(blocks))


def _all_gather_wait(send_sems, recv_sems, blocks, after, name):
    def body(in_ref, send_sems, recv_sems, after_ref, out_ref):
        x, y, c = _place()
        one = in_ref.at[0]
        for k in range(N_DEV - 1):
            pltpu.make_async_remote_copy(
                src_ref=one, dst_ref=one, send_sem=send_sems.at[k], recv_sem=recv_sems.at[k],
                device_id=(x, y, 1 - c), device_id_type=MESH).wait_send()
            pltpu.make_async_remote_copy(
                src_ref=one, dst_ref=one, send_sem=send_sems.at[k], recv_sem=recv_sems.at[k],
                device_id=(x, y, 1 - c), device_id_type=MESH).wait_recv()

    return pl.pallas_call(
        body, name=name, out_shape=pltpu.HBM(blocks.shape, blocks.dtype),
        in_specs=[HBM_SPEC, SEM_SPEC, SEM_SPEC, ANY_SPEC], out_specs=HBM_SPEC, input_output_aliases={0: 0},
        compiler_params=pltpu.CompilerParams(has_side_effects=EFFECT),
    )(blocks, send_sems, recv_sems, after)


def _place_own(shard, kind, chip_idx, name):
    r, cdim = shard.shape
    tr, tc = _tile(r, 512), _tile(cdim, 1024)
    nrb, ncb = r // tr, cdim // tc

    def body(k_ref, s_ref, o_ref):
        o_ref[...] = s_ref[...].astype(BF16)

    if kind == "col":
        full, o_map = (r, N_CHIP * cdim), lambda i, j, kr: (i, kr[0] * ncb + j)
    else:
        full, o_map = (N_CHIP * r, cdim), lambda i, j, kr: (kr[0] * nrb + i, j)
    return pl.pallas_call(
        body, name=name, out_shape=jax.ShapeDtypeStruct(full, BF16),
        grid_spec=pltpu.PrefetchScalarGridSpec(
            num_scalar_prefetch=1, grid=(nrb, ncb),
            in_specs=[pl.BlockSpec((tr, tc), lambda i, j, kr: (i, j))],
            out_specs=pl.BlockSpec((tr, tc), o_map)),
        compiler_params=_params("parallel", "parallel"),
    )(chip_idx, shard)


def _weight_window(ref, kind, shard_shape, k, half):
    r, cdim = shard_shape
    hr = r // 2
    if kind == "col":
        return ref.at[pl.ds(half * hr, hr), pl.ds(pl.multiple_of(k * cdim, 128), cdim)]
    return ref.at[pl.ds(pl.multiple_of(k * r + half * hr, 2 * SUBLANE), hr), :]


def _target_cores(c, both_cores):
    return [(cc, cc, c) for cc in range(2)] if both_cores else [(c, 0, 0)]


def _gather_weights_start(fulls, kinds, shard_shapes, groups, after, name, both_cores=True):
    n, ng = len(fulls), len(groups)
    per = 2 if both_cores else 1

    def body(*refs):
        ins = refs[:n]
        sems = refs[n + 1:n + 1 + 2 * ng]
        token = refs[2 * n + 1 + 2 * ng]
        x, y, c = _place()
        chips = _other_chips(x, y)
        for g, group in enumerate(groups):
            for pos, (w, j) in enumerate(group):
                own = _weight_window(ins[w], kinds[w], shard_shapes[w], 2 * x + y, c)
                for cc, mine, theirs in _target_cores(c, both_cores):
                    pltpu.make_async_remote_copy(
                        src_ref=own, dst_ref=own,
                        send_sem=sems[2 * g].at[per * pos + mine], recv_sem=sems[2 * g + 1].at[per * pos + theirs],
                        device_id=(*chips[j], cc), device_id_type=MESH).start()
        token[...] = jnp.zeros_like(token)

    sem_shapes = []
    for group in groups:
        sem_shapes += [pltpu.SemaphoreType.DMA((per * len(group),))] * 2
    outs = pl.pallas_call(
        body, name=name,
        out_shape=tuple(sem_shapes) + tuple(pltpu.HBM(f.shape, f.dtype) for f in fulls)
        + (jax.ShapeDtypeStruct((SUBLANE, 128), F32),),
        in_specs=[HBM_SPEC] * n + [ANY_SPEC], out_specs=(SEM_SPEC,) * (2 * ng) + (HBM_SPEC,) * n + (VMEM_SPEC,),
        input_output_aliases={w: 2 * ng + w for w in range(n)},
        compiler_params=pltpu.CompilerParams(has_side_effects=EFFECT),
    )(*[_hbm(f) for f in fulls], after)
    sems = [(outs[2 * g], outs[2 * g + 1]) for g in range(ng)]
    return sems, list(outs[2 * ng:2 * ng + n]), outs[2 * ng + n]


def _gather_weights_wait(sems, group, fulls, kinds, shard_shapes, after, name, both_cores=True):
    n = len(fulls)
    per = 2 if both_cores else 1

    def body(*refs):
        ins = refs[:n]
        send_sems, recv_sems = refs[n], refs[n + 1]
        x, y, c = _place()
        chips = _other_chips(x, y)
        for pos, (w, j) in enumerate(group):
            own = _weight_window(ins[w], kinds[w], shard_shapes[w], 2 * x + y, c)
            for cc, mine, _ in _target_cores(c, both_cores):
                landed = _weight_window(ins[w], kinds[w], shard_shapes[w], 2 * chips[j][0] + chips[j][1], cc)
                k = per * pos + mine
                pltpu.make_async_remote_copy(
                    src_ref=own, dst_ref=own, send_sem=send_sems.at[k], recv_sem=recv_sems.at[k],
                    device_id=(*chips[j], cc), device_id_type=MESH).wait_send()
                pltpu.make_async_remote_copy(
                    src_ref=landed, dst_ref=landed, send_sem=send_sems.at[k], recv_sem=recv_sems.at[k],
                    device_id=(*chips[j], cc), device_id_type=MESH).wait_recv()

    return list(pl.pallas_call(
        body, name=name, out_shape=tuple(pltpu.HBM(f.shape, f.dtype) for f in fulls),
        in_specs=[HBM_SPEC] * n + [SEM_SPEC, SEM_SPEC, ANY_SPEC], out_specs=(HBM_SPEC,) * n,
        input_output_aliases={w: w for w in range(n)},
        compiler_params=pltpu.CompilerParams(has_side_effects=EFFECT),
    )(*fulls, sems[0], sems[1], after))


def _forward_to_sibling(full, kind, shard_shape, relations, name):
    nr = len(relations)

    def body(in_ref, out_ref, send_sems, recv_sems):
        x, y, c = _place()
        chips = _other_chips(x, y)
        copies = []
        for pos, j in enumerate(relations):
            k = 2 * chips[j][0] + chips[j][1]
            mine = _weight_window(in_ref, kind, shard_shape, k, c)
            cp = pltpu.make_async_remote_copy(
                src_ref=mine, dst_ref=_weight_window(out_ref, kind, shard_shape, k, c),
                send_sem=send_sems.at[pos], recv_sem=recv_sems.at[pos], device_id=(x, y, 1 - c), device_id_type=MESH)
            cp.start()
            copies.append(cp)
        for pos, j in enumerate(relations):
            theirs = _weight_window(out_ref, kind, shard_shape, 2 * chips[j][0] + chips[j][1], 1 - c)
            pltpu.make_async_remote_copy(
                src_ref=theirs, dst_ref=theirs, send_sem=send_sems.at[pos], recv_sem=recv_sems.at[pos],
                device_id=(x, y, 1 - c), device_id_type=MESH).wait_recv()
        for cp in copies:
            cp.wait_send()

    return pl.pallas_call(
        body, name=name, out_shape=jax.ShapeDtypeStruct(full.shape, full.dtype),
        in_specs=[HBM_SPEC], out_specs=HBM_SPEC, input_output_aliases={0: 0},
        scratch_shapes=[pltpu.SemaphoreType.DMA((nr,))] * 2,
    )(full)


def _grad_piece(ref, kind, h, cdim, k, half):
    if kind == "col":
        return ref.at[pl.ds(half * h, h), pl.ds(pl.multiple_of(k * cdim, 128), cdim)]
    return ref.at[k, pl.ds(half * h, h), :]


def _grad_dims(g, kind):
    return (g.shape[0] // 2, g.shape[1] // N_CHIP) if kind == "col" else (g.shape[1] // 2, g.shape[2])


def _reduce_start(grads, kinds, name):
    n = len(grads)
    dims = [_grad_dims(g, kind) for g, kind in zip(grads, kinds)]
    lands = [lax.empty((N_DEV - 1, h, cdim), g.dtype) for g, (h, cdim) in zip(grads, dims)]

    def body(*refs):
        g_ins, land_ins = refs[:n], refs[n:2 * n]
        send_sems, recv_sems = refs[2 * n], refs[2 * n + 1]
        token = refs[4 * n + 2]
        x, y, c = _place()
        for w in range(n):
            h, cdim = dims[w]
            base = (N_DEV - 1) * w
            pltpu.make_async_remote_copy(
                src_ref=_grad_piece(g_ins[w], kinds[w], h, cdim, 2 * x + y, 1 - c), dst_ref=land_ins[w].at[0],
                send_sem=send_sems.at[base], recv_sem=recv_sems.at[base],
                device_id=(x, y, 1 - c), device_id_type=MESH).start()
            for j, chip in enumerate(_other_chips(x, y)):
                for cc in range(2):
                    pltpu.make_async_remote_copy(
                        src_ref=_grad_piece(g_ins[w], kinds[w], h, cdim, 2 * chip[0] + chip[1], cc),
                        dst_ref=land_ins[w].at[1 + 2 * j + c],
                        send_sem=send_sems.at[base + 1 + 2 * j + cc], recv_sem=recv_sems.at[base + 1 + 2 * j + c],
                        device_id=(*chip, cc), device_id_type=MESH).start()
        token[...] = jnp.zeros_like(token)

    sems = pltpu.SemaphoreType.DMA(((N_DEV - 1) * n,))
    outs = pl.pallas_call(
        body, name=name,
        out_shape=(sems, sems) + tuple(pltpu.HBM(a.shape, a.dtype) for a in list(grads) + lands)
        + (jax.ShapeDtypeStruct((SUBLANE, 128), F32),),
        in_specs=[HBM_SPEC] * (2 * n), out_specs=(SEM_SPEC, SEM_SPEC) + (HBM_SPEC,) * (2 * n) + (VMEM_SPEC,),
        input_output_aliases={i: 2 + i for i in range(2 * n)},
        compiler_params=pltpu.CompilerParams(has_side_effects=EFFECT),
    )(*[_hbm(a) for a in list(grads) + lands])
    return outs[0], outs[1], list(outs[2:2 + n]), list(outs[2 + n:2 + 2 * n]), outs[2 + 2 * n]


def _reduce_wait(send_sems, recv_sems, grads, lands, kinds, after, name):
    n = len(grads)
    dims = [_grad_dims(g, kind) for g, kind in zip(grads, kinds)]

    def body(*refs):
        g_ins, land_ins = refs[:n], refs[n:2 * n]
        send_sems, recv_sems = refs[2 * n], refs[2 * n + 1]
        x, y, c = _place()
        for w in range(n):
            h, cdim = dims[w]
            piece = _grad_piece(g_ins[w], kinds[w], h, cdim, 2 * x + y, c)
            for s in range(N_DEV - 1):
                k = (N_DEV - 1) * w + s
                slot = land_ins[w].at[s]
                pltpu.make_async_remote_copy(
                    src_ref=piece, dst_ref=slot, send_sem=send_sems.at[k], recv_sem=recv_sems.at[k],
                    device_id=(x, y, 1 - c), device_id_type=MESH).wait_send()
                pltpu.make_async_remote_copy(
                    src_ref=piece, dst_ref=slot, send_sem=send_sems.at[k], recv_sem=recv_sems.at[k],
                    device_id=(x, y, 1 - c), device_id_type=MESH).wait_recv()

    outs = pl.pallas_call(
        body, name=name, out_shape=tuple(pltpu.HBM(a.shape, a.dtype) for a in list(grads) + list(lands)),
        in_specs=[HBM_SPEC] * (2 * n) + [SEM_SPEC, SEM_SPEC, ANY_SPEC], out_specs=(HBM_SPEC,) * (2 * n),
        input_output_aliases={i: i for i in range(2 * n)},
        compiler_params=pltpu.CompilerParams(has_side_effects=EFFECT),
    )(*grads, *lands, send_sems, recv_sems, after)
    return list(outs[:n]), list(outs[n:])


def _add_pieces(g, land, kind, chip_idx, core_idx, name):
    _, h, cdim = land.shape
    tr, tc = _tile(h, 256), _tile(cdim, 2048)
    nrb, ncb = h // tr, cdim // tc

    def body(k_ref, c_ref, g_ref, l_ref, o_ref):
        acc = g_ref[...].astype(F32)
        for s in range(N_DEV - 1):
            acc = acc + l_ref[s].astype(F32)
        o_ref[...] = acc

    if kind == "col":
        g_spec = pl.BlockSpec((tr, tc), lambda i, j, kr, cr: (cr[0] * nrb + i, kr[0] * ncb + j))
    else:
        g_spec = pl.BlockSpec((None, tr, tc), lambda i, j, kr, cr: (kr[0], cr[0] * nrb + i, j))
    return pl.pallas_call(
        body, name=name, out_shape=jax.ShapeDtypeStruct((2 * h, cdim), F32),
        grid_spec=pltpu.PrefetchScalarGridSpec(
            num_scalar_prefetch=2, grid=(nrb, ncb),
            in_specs=[g_spec, pl.BlockSpec((N_DEV - 1, tr, tc), lambda i, j, kr, cr: (0, i, j))],
            out_specs=pl.BlockSpec((tr, tc), lambda i, j, kr, cr: (cr[0] * nrb + i, j))),
        compiler_params=_params("parallel", "parallel"),
    )(chip_idx, core_idx, g, land)


def _join_halves(shards, name):
    n = len(shards)

    def body(*refs):
        ins, outs = refs[:n], refs[n:2 * n]
        send_sems, recv_sems = refs[2 * n:]
        x, y, c = _place()
        copies = []
        for w in range(n):
            h = shards[w].shape[0] // 2
            cp = pltpu.make_async_remote_copy(
                src_ref=ins[w].at[pl.ds(c * h, h), :], dst_ref=outs[w].at[pl.ds(c * h, h), :],
                send_sem=send_sems.at[w], recv_sem=recv_sems.at[w], device_id=(x, y, 1 - c), device_id_type=MESH)
            cp.start()
            copies.append(cp)
        for w in range(n):
            h = shards[w].shape[0] // 2
            theirs = outs[w].at[pl.ds((1 - c) * h, h), :]
            pltpu.make_async_remote_copy(
                src_ref=theirs, dst_ref=theirs, send_sem=send_sems.at[w], recv_sem=recv_sems.at[w],
                device_id=(x, y, 1 - c), device_id_type=MESH).wait_recv()
        for cp in copies:
            cp.wait_send()

    return pl.pallas_call(
        body, name=name, out_shape=tuple(jax.ShapeDtypeStruct(s.shape, s.dtype) for s in shards),
        in_specs=[HBM_SPEC] * n, out_specs=tuple([HBM_SPEC] * n),
        input_output_aliases={w: w for w in range(n)},
        scratch_shapes=[pltpu.SemaphoreType.DMA((n,))] * 2,
    )(*shards)


def _sum_devices(gathered, rows, name):
    n = gathered.shape[1]

    def body(g_ref, o_ref):
        acc = g_ref[0:rows, :]
        for dev in range(1, N_DEV):
            acc = acc + g_ref[dev * rows:(dev + 1) * rows, :]
        o_ref[...] = acc

    return pl.pallas_call(
        body, name=name, out_shape=jax.ShapeDtypeStruct((rows, n), F32),
        in_specs=[VMEM_SPEC], out_specs=VMEM_SPEC,
        compiler_params=pltpu.CompilerParams(vmem_limit_bytes=VMEM_LIMIT),
    )(gathered)


def _pack(vectors, width):
    flat = [v.reshape(-1) for v in vectors]
    offsets, total = [], 0
    for f in flat:
        offsets.append(total)
        total += f.shape[0]
    rows = -(-total // (width * SUBLANE)) * SUBLANE
    flat.append(jnp.zeros((rows * width - total,), F32))
    return jnp.concatenate(flat).reshape(rows, width), offsets


def kernel(x, c, mod_w, mod_b, norm_g, a_w_in, a_conv_w, a_conv_b, a_w_out, b_w_in, b_ln_g, b_ln_b, b_w_s, b_b_s, b_w_out, final_g, loss_target, m_mod_w, m_mod_b, m_norm_g, m_a_w_in, m_a_conv_w, m_a_conv_b, m_a_w_out, m_b_w_in, m_b_ln_g, m_b_ln_b, m_b_w_s, m_b_b_s, m_b_w_out, m_final_g, v_mod_w, v_mod_b, v_norm_g, v_a_w_in, v_a_conv_w, v_a_conv_b, v_a_w_out, v_b_w_in, v_b_ln_g, v_b_ln_b, v_b_w_s, v_b_b_s, v_b_w_out, v_final_g):
    seq, d = x.shape[1], x.shape[2]
    e = a_conv_b.shape[1]
    wd = mod_w.shape[2]
    ax, ay, ac = _place()
    chip = 2 * ax + ay
    dev = 2 * chip + ac
    chip_idx = jnp.reshape(chip, (1,)).astype(jnp.int32)
    core_idx = jnp.reshape(ac, (1,)).astype(jnp.int32)

    x2d = x[0]
    target = loss_target[0]

    w_names = ["a_w_in", "a_w_out", "b_w_in", "b_w_out"]
    w_kinds = ["col", "row", "col", "row"]
    w_shards = [a_w_in[0], a_w_out[0], b_w_in[0], b_w_out[0]]
    w_shapes = [sh.shape for sh in w_shards]
    placed = [_place_own(sh, kind, chip_idx, "place_" + nm) for nm, sh, kind in zip(w_names, w_shards, w_kinds)]
    a_groups, far_groups = [[(0, 0), (0, 1)]], [[(0, 2)]]
    whole_group = [[(0, 0), (0, 1), (0, 2)]]

    es = e // N_CHIP
    packed0, offs0 = _pack([c, a_conv_w, b_ln_g, b_ln_b], 1024)
    gathered0 = _all_gather(packed0, "gather_params").reshape(N_DEV, -1)
    c_all = gathered0[:, :d]
    per_chip = gathered0[0::2]

    def from_chips_cols(k, rows_):
        got = per_chip[:, offs0[k]:offs0[k] + rows_ * es].reshape(N_CHIP, rows_, es)
        return jnp.transpose(got, (1, 0, 2)).reshape(rows_, e)

    conv_w_full = from_chips_cols(1, 3)
    conv_w = [conv_w_full[k:k + 1] for k in range(3)]
    ln_g, ln_b = from_chips_cols(2, 1), from_chips_cols(3, 1)
    mod_b_cols = lax.dynamic_slice_in_dim(mod_b, chip * wd, wd, axis=1)[:, None, :]
    c_act, mod_part = _mod_fwd(c_all, mod_w, mod_b_cols, "mod_fwd")
    n_layer = mod_w.shape[0]
    mod_gathered = _all_gather(mod_part, "gather_mod")
    a_sems, (wa_in,), a_token = _gather_weights_start(
        placed[:1], w_kinds[:1], w_shapes[:1], a_groups, mod_gathered, "gather_a_w_in_start", both_cores=False)
    mod_all = mod_gathered.reshape(N_CHIP, 2, n_layer, N_DEV, wd)[:, 0]
    mod_all = jnp.transpose(mod_all, (1, 2, 0, 3)).reshape(n_layer, N_DEV, N_CHIP * wd)
    mod_me = lax.dynamic_index_in_dim(mod_all, dev, axis=1, keepdims=False)
    shift = [mod_me[l:l + 1, 0:d] for l in range(n_layer)]
    scale = [mod_me[l:l + 1, d:2 * d] for l in range(n_layer)]
    gate = [mod_me[l:l + 1, 2 * d:3 * d] for l in range(n_layer)]

    g0, g1, gf = norm_g[0:1], norm_g[1:2], final_g[None, :]
    h0 = _norm_mod(x2d, g0, scale[0], shift[0], "norm_mod0")

    def slab(r):
        return jnp.bitwise_xor(chip_idx, r)

    def arrived(sems, group, weight, w, after, name, both_cores=True):
        return _gather_weights_wait(sems, [(0, j) for _, j in group], [weight], [w_kinds[w]], [w_shapes[w]],
                                    after, name, both_cores)[0]

    proj0 = _mm_proj_slab(h0, wa_in, None, slab(0), 4, "a_proj_own", a_token)
    wa_in = arrived(a_sems[0], a_groups[0], wa_in, 0, proj0, "gather_wait_near", both_cores=False)
    wa_in = _forward_to_sibling(wa_in, w_kinds[0], w_shapes[0], [0, 1], "forward_near")
    far_sems, (wa_in,), far_token = _gather_weights_start(
        [wa_in], w_kinds[:1], w_shapes[:1], far_groups, a_token, "gather_a_w_in_far_start")
    proj0 = _mm_proj_slab(h0, wa_in, proj0, slab(2), 4, "a_proj_x", far_token)
    proj0 = _mm_proj_slab(h0, wa_in, proj0, slab(1), 4, "a_proj_y", far_token)
    wa_in = arrived(far_sems[0], far_groups[0], wa_in, 0, proj0, "gather_wait_far")
    def start_whole(w, after, name):
        sems, (flight,), token = _gather_weights_start(
            placed[w:w + 1], w_kinds[w:w + 1], w_shapes[w:w + 1], whole_group, after, name)
        return sems[0], flight, token

    ao_sems, wa_out, ao_token = start_whole(1, wa_in, "gather_a_w_out_start")
    proj0 = _mm_proj_slab(h0, wa_in, proj0, slab(3), 4, "a_proj_far", ao_token)
    wa_out = arrived(ao_sems, whole_group[0], wa_out, 1, proj0, "gather_wait_a_w_out")
    bi_sems, wb_in, bi_token = start_whole(2, wa_out, "gather_b_w_in_start")
    y0, br0, x1, h1 = _conv_fwd(proj0, conv_w, a_conv_b, wa_out, x2d, gate[0], g1, scale[1], shift[1],
                                "conv_fwd", bi_token)
    wb_in = arrived(bi_sems, whole_group[0], wb_in, 2, h1, "gather_wait_b_w_in")
    bo_sems, wb_out, bo_token = start_whole(3, wb_in, "gather_b_w_out_start")
    proj1 = _mm_proj(h1, wb_in, 3, "b_proj", bo_token)
    b_s_t = jnp.transpose(b_b_s[0])
    wb_out = arrived(bo_sems, whole_group[0], wb_out, 3, proj1, "gather_wait_b_w_out")
    y1, dx2, dbr1, loss_part, g_final_g, dgate1 = _gmlp_fwd_head(
        proj1, ln_g, ln_b, b_w_s[0], b_s_t, wb_out, x1, gate[1], gf, target, "gmlp_fwd_head")

    gw_b_out = _mm_dw_out(y1, dbr1, "b_out_dw")
    dproj1, g_w_s, g_b_s_t, g_ln_g, g_ln_b = _gmlp_bwd(
        proj1, dbr1, wb_out, ln_g, ln_b, b_w_s[0], jnp.swapaxes(b_w_s[0], 1, 2), b_s_t, "gmlp_bwd")
    gw_b_in = _mm_dw_in(h1, dproj1, "b_proj_dw")
    b_kinds = ["col", "row"]
    b_send, b_recv, b_grads, b_lands, b_token = _reduce_start(
        [gw_b_in, gw_b_out.reshape(N_CHIP, e // N_CHIP, d)], b_kinds, "reduce_b_start")
    dx1, dshift1, dscale1, g_g1, dbr0, dgate0 = _mm_dh_norm_bwd(
        dproj1, wb_in, x1, dx2, g1, scale[1], "b_proj_dx", b_token, br=br0, gate=gate[0])

    early = [loss_part[0, 0:1], g_final_g, g_g1, g_ln_g, g_ln_b, jnp.transpose(g_b_s_t), g_w_s,
             jnp.concatenate([dshift1, dscale1, dgate1, dgate0], axis=1)]
    packed_e, offs_e = _pack(early, 1024)
    rows_e = packed_e.shape[0]
    blocks_e = lax.dynamic_update_slice(jnp.zeros((N_DEV, rows_e, 1024), F32), packed_e[None], (dev, 0, 0))
    e_send, e_recv, blocks_e = _all_gather_start(blocks_e, "gather_small_start")

    gw_a_out = _mm_dw_out(y0, dbr0, "a_out_dw")
    ao_send, ao_recv, ao_grads, ao_lands, ao_grad_token = _reduce_start(
        [gw_a_out.reshape(N_CHIP, e // N_CHIP, d)], ["row"], "reduce_a_out_start")
    dproj0, g_w0, g_w1, g_w2, g_conv_b = _conv_bwd(proj0, dbr0, wa_out, conv_w, a_conv_b, "conv_bwd", ao_grad_token)
    gw_a_in = _mm_dw_in(h0, dproj0, "a_proj_dw")
    ai_send, ai_recv, ai_grads, ai_lands, ai_token = _reduce_start([gw_a_in], ["col"], "reduce_a_in_start")
    grad_x, dshift0, dscale0, g_g0 = _mm_dh_norm_bwd(dproj0, wa_in, x2d, dx1, g0, scale[0], "a_proj_dx", ai_token)

    def finish(send, recv, grads_, lands_, kinds_, names_, after, tag):
        grads_, lands_ = _reduce_wait(send, recv, grads_, lands_, kinds_, after, "reduce_" + tag + "_wait")
        halves = [_add_pieces(g, land, kind, chip_idx, core_idx, "add_pieces_" + nm)
                  for g, land, kind, nm in zip(grads_, lands_, kinds_, names_)]
        return _join_halves(halves, "join_" + tag)

    upd, big_grads = {}, {}

    def adamw_big(nm, w, g, m, v):
        g_out, *rest = _adamw_2d(w[0], g, m[0], v[0], "adamw_" + nm)
        big_grads[nm] = g_out[None]
        upd[nm] = tuple(o[None] for o in rest)

    g_b_w_in, g_b_w_out = finish(b_send, b_recv, b_grads, b_lands, b_kinds, ["b_w_in", "b_w_out"], grad_x, "b")
    adamw_big("b_w_in", b_w_in, g_b_w_in, m_b_w_in, v_b_w_in)
    adamw_big("b_w_out", b_w_out, g_b_w_out, m_b_w_out, v_b_w_out)

    late = [g_g0, jnp.concatenate([g_w0, g_w1, g_w2], axis=0), g_conv_b, jnp.concatenate([dshift0, dscale0], axis=1)]
    packed_l, offs_l = _pack(late, 1024)
    rows_l = packed_l.shape[0]
    gathered_l = _all_gather(packed_l, "gather_small_late", after=upd["b_w_out"][0])
    blocks_e = _all_gather_wait(e_send, e_recv, blocks_e, gathered_l, "gather_small_wait")
    gathered_e = blocks_e.reshape(N_DEV * rows_e, 1024)
    summed_e = _sum_devices(gathered_e, rows_e, "sum_small_early").reshape(-1)
    summed_l = _sum_devices(gathered_l, rows_l, "sum_small_late").reshape(-1)

    def take(summed, offs, k, shape):
        size = math.prod(shape)
        return summed[offs[k]:offs[k] + size].reshape(shape)

    def rows_of(gathered, rows_, offs, k, width):
        return gathered.reshape(N_DEV, rows_ * 1024)[:, offs[k]:offs[k] + width]

    loss = take(summed_e, offs_e, 0, ())
    grad_final_g = take(summed_e, offs_e, 1, (d,))
    grad_norm_g = jnp.concatenate([take(summed_l, offs_l, 0, (1, d)), take(summed_e, offs_e, 2, (1, d))], axis=0)
    grad_ln_g_full = take(summed_e, offs_e, 3, (1, e))
    grad_ln_b_full = take(summed_e, offs_e, 4, (1, e))
    grad_b_b_s = take(summed_e, offs_e, 5, (1, GROUPS, CHUNK))
    grad_b_w_s = take(summed_e, offs_e, 6, (1, GROUPS, CHUNK, CHUNK))
    grad_conv_w_full = take(summed_l, offs_l, 1, (3, e))
    grad_a_conv_b = take(summed_l, offs_l, 2, (1, e))
    grad_a_conv_w = lax.dynamic_slice_in_dim(grad_conv_w_full, chip * es, es, axis=1)[None]
    grad_b_ln_g = lax.dynamic_slice_in_dim(grad_ln_g_full, chip * es, es, axis=1)
    grad_b_ln_b = lax.dynamic_slice_in_dim(grad_ln_b_full, chip * es, es, axis=1)
    dmod_e = rows_of(gathered_e, rows_e, offs_e, 7, 4 * d)
    dmod_l = rows_of(gathered_l, rows_l, offs_l, 3, 2 * d)
    dmod_all = jnp.stack([jnp.concatenate([dmod_l, dmod_e[:, 3 * d:]], axis=1), dmod_e[:, :3 * d]], axis=1)
    mod_b_e = take(summed_e, offs_e, 7, (4 * d,))
    mod_b_l = take(summed_l, offs_l, 3, (2 * d,))
    grad_mod_b = jnp.stack([jnp.concatenate([mod_b_l, mod_b_e[3 * d:]]), mod_b_e[:3 * d]])
    dmod_cols = jnp.transpose(lax.dynamic_slice_in_dim(dmod_all, chip * wd, wd, axis=2), (1, 0, 2))

    grad_mod_w, delta_mod_w, new_m_mod_w, new_v_mod_w = _mod_w_update(
        jnp.transpose(c_act), dmod_cols, mod_w, m_mod_w, v_mod_w, "mod_w_update")
    (g_a_w_out,) = finish(ao_send, ao_recv, ao_grads, ao_lands, ["row"], ["a_w_out"], delta_mod_w, "a_out")
    adamw_big("a_w_out", a_w_out, g_a_w_out, m_a_w_out, v_a_w_out)
    (g_a_w_in,) = finish(ai_send, ai_recv, ai_grads, ai_lands, ["col"], ["a_w_in"], upd["a_w_out"][0], "a_in")
    adamw_big("a_w_in", a_w_in, g_a_w_in, m_a_w_in, v_a_w_in)
    small_w = [("mod_b", mod_b, grad_mod_b, m_mod_b, v_mod_b), ("norm_g", norm_g, grad_norm_g, m_norm_g, v_norm_g),
               ("a_conv_w", a_conv_w, grad_a_conv_w, m_a_conv_w, v_a_conv_w),
               ("a_conv_b", a_conv_b, grad_a_conv_b, m_a_conv_b, v_a_conv_b),
               ("b_ln_g", b_ln_g, grad_b_ln_g, m_b_ln_g, v_b_ln_g), ("b_ln_b", b_ln_b, grad_b_ln_b, m_b_ln_b, v_b_ln_b),
               ("b_w_s", b_w_s, grad_b_w_s, m_b_w_s, v_b_w_s), ("b_b_s", b_b_s, grad_b_b_s, m_b_b_s, v_b_b_s),
               ("final_g", final_g, grad_final_g, m_final_g, v_final_g)]

    def flat2d(a):
        return a.reshape(-1, a.shape[-1])

    res = _adamw_small([flat2d(t[1]) for t in small_w], [flat2d(t[2]) for t in small_w],
                       [flat2d(t[3]) for t in small_w], [flat2d(t[4]) for t in small_w], "adamw_small")
    for (nm, w, _, _, _), r3 in zip(small_w, res):
        upd[nm] = tuple(o.reshape(w.shape) for o in r3)
    upd["mod_w"] = (delta_mod_w, new_m_mod_w, new_v_mod_w)

    grads = {"mod_w": grad_mod_w, "mod_b": grad_mod_b, "norm_g": grad_norm_g, "a_conv_w": grad_a_conv_w,
             "a_conv_b": grad_a_conv_b, "b_ln_g": grad_b_ln_g, "b_ln_b": grad_b_ln_b, "b_w_s": grad_b_w_s,
             "b_b_s": grad_b_b_s, "final_g": grad_final_g, **big_grads}
    order = ["mod_w", "mod_b", "norm_g", "a_w_in", "a_conv_w", "a_conv_b", "a_w_out", "b_w_in", "b_ln_g", "b_ln_b",
             "b_w_s", "b_b_s", "b_w_out", "final_g"]
    return (loss, grad_x[None], *[grads[k] for k in order], *[upd[k][0] for k in order],
            *[upd[k][1] for k in order], *[upd[k][2] for k in order])
```

```python
import functools
import math

import jax
import jax.numpy as jnp
from jax import lax
from jax.experimental import pallas as pl
from jax.experimental.pallas import tpu as pltpu

F32 = jnp.float32
BF16 = jnp.bfloat16
MESH = pl.DeviceIdType.MESH

N_DEV = 8
N_CHIP = 4
SUBLANE = 8
PACK = 16
ACT = F32
RMS_EPS = 1e-6
LN_EPS = 1e-5
CHUNK = 128
GROUPS = 8
ADAM_LR = 0.001
ADAM_B1 = 0.9
ADAM_B2 = 0.999
ADAM_EPS = 1e-08
ADAM_WD = 0.01
ADAM_STEP = 10
VMEM_LIMIT = 56 << 20

HBM_SPEC = pl.BlockSpec(memory_space=pltpu.HBM)
VMEM_SPEC = pl.BlockSpec(memory_space=pltpu.VMEM)
SEM_SPEC = pl.BlockSpec(memory_space=pltpu.SEMAPHORE)
ANY_SPEC = pl.BlockSpec(memory_space=pl.ANY)
EFFECT = pltpu.SideEffectType.DATAFLOW_SIDE_EFFECTING


def _params(*sem):
    return pltpu.CompilerParams(dimension_semantics=sem, vmem_limit_bytes=VMEM_LIMIT)


def _tile(n, want):
    if n <= want:
        return n
    t = want
    while n % t:
        t -= 128
    return t


def _sigmoid(x):
    return 0.5 * jnp.tanh(0.5 * x) + 0.5


def _silu_and_grad(x):
    s = _sigmoid(x)
    return x * s, s * (1.0 + x * (1.0 - s))


def _gelu_and_grad(x):
    cdf = 0.5 * (1.0 + lax.erf(x * (1.0 / math.sqrt(2.0))))
    pdf = jnp.exp(-0.5 * x * x) * (1.0 / math.sqrt(2.0 * math.pi))
    return x * cdf, cdf + x * pdf


def _gelu(x):
    return x * (0.5 * (1.0 + lax.erf(x * (1.0 / math.sqrt(2.0)))))


def _rms(x):
    r = lax.rsqrt(jnp.mean(x * x, axis=-1, keepdims=True) + RMS_EPS)
    return x * r, r


def _rms_bwd(dxn, xn, r):
    return r * (dxn - xn * jnp.mean(dxn * xn, axis=-1, keepdims=True))


def _colsum(a):
    return jnp.sum(a, axis=0, keepdims=True)


def _shift_down(cur, before, k):
    rolled = pltpu.roll(cur, k, 0)
    row = lax.broadcasted_iota(jnp.int32, before.shape, 0)
    head = jnp.where(row < k, pltpu.roll(before, k, 0), rolled[:SUBLANE])
    return jnp.concatenate([head, rolled[SUBLANE:]], axis=0)


def _shift_up(cur, after, k):
    n = cur.shape[0]
    rolled = pltpu.roll(cur, n - k, 0)
    row = lax.broadcasted_iota(jnp.int32, after.shape, 0)
    tail = jnp.where(row >= SUBLANE - k, pltpu.roll(after, SUBLANE - k, 0), rolled[n - SUBLANE:])
    return jnp.concatenate([rolled[:n - SUBLANE], tail], axis=0)


def _after_spec():
    return pl.BlockSpec((SUBLANE, 128), lambda *_: (0, 0))


def _mm_proj(h, w, n_split, name, after):
    s, d = h.shape
    e = w.shape[1] // n_split
    tm, tn = _tile(s, 1024), _tile(e, 2048)
    nj = e // tn

    def body(h_ref, w_ref, after_ref, o_ref):
        o_ref[...] = jnp.dot(h_ref[...], w_ref[...], preferred_element_type=F32).astype(ACT)

    return pl.pallas_call(
        body, name=name,
        out_shape=jax.ShapeDtypeStruct((n_split, s, e), ACT),
        grid=(s // tm, n_split * nj),
        in_specs=[pl.BlockSpec((tm, d), lambda i, j: (i, 0)), pl.BlockSpec((d, tn), lambda i, j: (0, j)),
                  _after_spec()],
        out_specs=pl.BlockSpec((None, tm, tn), lambda i, j: (j // nj, i, j % nj)),
        compiler_params=_params("parallel", "parallel"),
    )(h, w, after)


def _mm_proj_slab(h, w, proj, q_idx, n_split, name, after):
    s, d = h.shape
    e = w.shape[1] // n_split
    tm, tn = _tile(s, 1024), _tile(e, 2048)
    nj = e // tn

    def body(q_ref, h_ref, w_ref, *rest):
        o_ref = rest[-1]
        o_ref[...] = jnp.dot(h_ref[...], w_ref[...], preferred_element_type=F32).astype(ACT)

    in_specs = [pl.BlockSpec((tm, d), lambda i, j, qr: (i, 0)), pl.BlockSpec((d, tn), lambda i, j, qr: (0, qr[0] * nj + j)),
                _after_spec()]
    args = [q_idx, h, w, after]
    aliases = {}
    if proj is not None:
        in_specs.append(ANY_SPEC)
        args.append(proj)
        aliases = {4: 0}
    return pl.pallas_call(
        body, name=name,
        out_shape=jax.ShapeDtypeStruct((n_split, s, e), ACT),
        grid_spec=pltpu.PrefetchScalarGridSpec(
            num_scalar_prefetch=1, grid=(s // tm, nj), in_specs=in_specs,
            out_specs=pl.BlockSpec((None, tm, tn), lambda i, j, qr: (qr[0], i, j))),
        input_output_aliases=aliases,
        compiler_params=_params("parallel", "parallel"),
    )(*args)


def _mm_dh_norm_bwd(dp, w, x, dx_in, g, scale, name, after, br=None, gate=None):
    nq, s, e = dp.shape
    d = w.shape[0]
    has_branch = br is not None
    tm, tk = _tile(s, 1024), _tile(e, 1024 if has_branch else 2048)
    nkq = e // tk
    nk = nq * nkq

    def body(*refs):
        if has_branch:
            (a_ref, b_ref, after_ref, x_ref, dxin_ref, g_ref, sc_ref, br_ref, gate_ref,
             dx_ref, dsh_ref, dsc_ref, dg_ref, dbr_ref, dgate_ref, acc_ref) = refs
        else:
            (a_ref, b_ref, after_ref, x_ref, dxin_ref, g_ref, sc_ref,
             dx_ref, dsh_ref, dsc_ref, dg_ref, acc_ref) = refs
        i, k = pl.program_id(0), pl.program_id(1)

        @pl.when(jnp.logical_and(i == 0, k == 0))
        def _():
            dsh_ref[...] = jnp.zeros_like(dsh_ref)
            dsc_ref[...] = jnp.zeros_like(dsc_ref)
            dg_ref[...] = jnp.zeros_like(dg_ref)
            if has_branch:
                dgate_ref[...] = jnp.zeros_like(dgate_ref)

        @pl.when(k == 0)
        def _():
            acc_ref[...] = jnp.zeros_like(acc_ref)
        acc_ref[...] += lax.dot_general(a_ref[...], b_ref[...], (((1,), (1,)), ((), ())), preferred_element_type=F32)

        @pl.when(k == nk - 1)
        def _():
            dh = acc_ref[...]
            g_vec = g_ref[...]
            xn, r = _rms(x_ref[...])
            dsh_ref[...] += _colsum(dh)
            dsc_ref[...] += _colsum(dh * (xn * g_vec))
            da = dh * (1.0 + sc_ref[...])
            dg_ref[...] += _colsum(da * xn)
            dx = dxin_ref[...] + _rms_bwd(da * g_vec, xn, r)
            dx_ref[...] = dx
            if has_branch:
                dgate_ref[...] += _colsum(dx * br_ref[...])
                dbr_ref[...] = (gate_ref[...] * dx).astype(BF16)

    rows = pl.BlockSpec((tm, d), lambda i, k: (i, 0))
    vec = pl.BlockSpec((1, d), lambda i, k: (0, 0))
    vec_out = jax.ShapeDtypeStruct((1, d), F32)
    in_specs = [pl.BlockSpec((None, tm, tk), lambda i, k: (k // nkq, i, k % nkq)),
                pl.BlockSpec((d, tk), lambda i, k: (0, k)), _after_spec(), rows, rows, vec, vec]
    out_shape = [jax.ShapeDtypeStruct((s, d), F32), vec_out, vec_out, vec_out]
    out_specs = [rows, vec, vec, vec]
    args = [dp, w, after, x, dx_in, g, scale]
    if has_branch:
        in_specs += [rows, vec]
        out_shape += [jax.ShapeDtypeStruct((s, d), BF16), vec_out]
        out_specs += [rows, vec]
        args += [br, gate]
    return pl.pallas_call(
        body, name=name, out_shape=tuple(out_shape), grid=(s // tm, nk),
        in_specs=in_specs, out_specs=tuple(out_specs), scratch_shapes=[pltpu.VMEM((tm, d), F32)],
        compiler_params=_params("arbitrary", "arbitrary"),
    )(*args)


def _dw_body(n_t):
    def body(a_ref, b_ref, o_ref, acc_ref):
        t = pl.program_id(2)

        @pl.when(t == 0)
        def _():
            acc_ref[...] = jnp.zeros_like(acc_ref)
        acc_ref[...] += lax.dot_general(a_ref[...], b_ref[...], (((0,), (0,)), ((), ())), preferred_element_type=F32)

        @pl.when(t == n_t - 1)
        def _():
            o_ref[...] = acc_ref[...].astype(o_ref.dtype)
    return body


def _mm_dw_in(h, dp, name):
    s, d = h.shape
    nq, _, e = dp.shape
    tm, tn, tt = _tile(d, 1024), _tile(e, 1024), _tile(s, 2048)
    nj = e // tn

    return pl.pallas_call(
        _dw_body(s // tt), name=name,
        out_shape=jax.ShapeDtypeStruct((d, nq * e), BF16),
        grid=(d // tm, nq * nj, s // tt),
        in_specs=[pl.BlockSpec((tt, tm), lambda i, j, t: (t, i)),
                  pl.BlockSpec((None, tt, tn), lambda i, j, t: (j // nj, t, j % nj))],
        out_specs=pl.BlockSpec((tm, tn), lambda i, j, t: (i, j)),
        scratch_shapes=[pltpu.VMEM((tm, tn), F32)],
        compiler_params=_params("parallel", "parallel", "arbitrary"),
    )(h, dp)


def _mm_dw_out(y, dbr, name):
    s, e = y.shape
    d = dbr.shape[1]
    tm, tn, tt = _tile(e, 1024), _tile(d, 1024), _tile(s, 2048)

    return pl.pallas_call(
        _dw_body(s // tt), name=name,
        out_shape=jax.ShapeDtypeStruct((e, d), BF16),
        grid=(e // tm, d // tn, s // tt),
        in_specs=[pl.BlockSpec((tt, tm), lambda i, j, t: (t, i)), pl.BlockSpec((tt, tn), lambda i, j, t: (t, j))],
        out_specs=pl.BlockSpec((tm, tn), lambda i, j, t: (i, j)),
        scratch_shapes=[pltpu.VMEM((tm, tn), F32)],
        compiler_params=_params("parallel", "parallel", "arbitrary"),
    )(y, dbr)


def _row_spec(ts, d):
    return pl.BlockSpec((ts, d), lambda i: (i, 0))


def _vec_spec(d):
    return pl.BlockSpec((1, d), lambda i: (0, 0))


def _norm_mod(x, g, scale, shift, name):
    s, d = x.shape
    ts = _tile(s, 512)

    def body(x_ref, g_ref, sc_ref, sh_ref, h_ref):
        xn, _ = _rms(x_ref[...])
        h_ref[...] = ((xn * g_ref[...]) * (1.0 + sc_ref[...]) + sh_ref[...]).astype(BF16)

    return pl.pallas_call(
        body, name=name, out_shape=jax.ShapeDtypeStruct((s, d), BF16), grid=(s // ts,),
        in_specs=[_row_spec(ts, d), _vec_spec(d), _vec_spec(d), _vec_spec(d)],
        out_specs=_row_spec(ts, d), compiler_params=_params("parallel"),
    )(x, g, scale, shift)


def _head_tile(x1, br, gate, gf, target):
    d = x1.shape[-1]
    xn, r = _rms(x1 + gate * br)
    err = xn * gf - target
    loss = 0.5 * jnp.sum(jnp.mean(err * err, axis=-1, keepdims=True))
    dout = err * (1.0 / d)
    dx = _rms_bwd(dout * gf, xn, r)
    return dx, gate * dx, loss, _colsum(dout * xn), _colsum(dx * br)


def _f32(ref):
    return ref[...].astype(F32)


def _cols_f32(ref, cols):
    return ref[:, cols].astype(F32)


def _rows_before(halo_ref, cols=slice(None)):
    return _cols_f32(halo_ref, cols)[PACK - SUBLANE:]


def _rows_after(halo_ref, cols=slice(None)):
    return _cols_f32(halo_ref, cols)[:SUBLANE]


CONV_CHAIN_COLS = 512


def _conv_fwd(proj, conv_w, conv_b, w_out, x, gate, g_next, scale_next, shift_next, name, after):
    _, s, e = proj.shape
    d = w_out.shape[1]
    ts, te = _tile(s, 256), e
    cw = min(CONV_CHAIN_COLS, te)
    hb = ts // PACK

    def body(bg_ref, cg_ref, xi_ref, z_ref, cgp_ref, xip_ref, w0_ref, w1_ref, w2_ref, b_ref, wo_ref, after_ref,
             x_ref, gate_ref, g_ref, sc_ref, sh_ref, y_ref, br_ref, x1_ref, h_ref):
        first = pl.program_id(0) == 0
        br = None
        for c0 in range(0, te, cw):
            cols = slice(c0, c0 + cw)
            cx = _cols_f32(cg_ref, cols) * _cols_f32(xi_ref, cols)
            before = jnp.where(first, 0.0, _rows_before(cgp_ref, cols) * _rows_before(xip_ref, cols))
            conv = b_ref[:, cols] + w2_ref[:, cols] * cx
            conv = conv + w0_ref[:, cols] * _shift_down(cx, before, 2)
            conv = conv + w1_ref[:, cols] * _shift_down(cx, before, 1)
            z = _cols_f32(z_ref, cols)
            y = ((z * _sigmoid(z)) * _cols_f32(bg_ref, cols) * conv).astype(BF16)
            y_ref[:, cols] = y
            part_br = jnp.dot(y, wo_ref[cols, :], preferred_element_type=F32)
            br = part_br if br is None else br + part_br
        br_ref[...] = br
        x1 = x_ref[...] + gate_ref[...] * br
        x1_ref[...] = x1
        xn, _ = _rms(x1)
        h_ref[...] = ((xn * g_ref[...]) * (1.0 + sc_ref[...]) + sh_ref[...]).astype(BF16)

    def part(q):
        return pl.BlockSpec((None, ts, te), lambda i: (q, i, 0))

    def halo_before(q):
        return pl.BlockSpec((None, PACK, te), lambda i: (q, jnp.maximum(i * hb - 1, 0), 0))

    return pl.pallas_call(
        body, name=name,
        out_shape=(jax.ShapeDtypeStruct((s, e), BF16), jax.ShapeDtypeStruct((s, d), F32),
                   jax.ShapeDtypeStruct((s, d), F32), jax.ShapeDtypeStruct((s, d), BF16)), grid=(s // ts,),
        in_specs=[part(0), part(1), part(2), part(3), halo_before(1), halo_before(2)]
        + [_vec_spec(e)] * 4 + [pl.BlockSpec((e, d), lambda i: (0, 0)), _after_spec(), _row_spec(ts, d)]
        + [_vec_spec(d)] * 4,
        out_specs=(_row_spec(ts, e), _row_spec(ts, d), _row_spec(ts, d), _row_spec(ts, d)),
        compiler_params=_params("parallel"),
    )(proj, proj, proj, proj, proj, proj, *conv_w, conv_b, w_out, after, x, gate, g_next, scale_next, shift_next)


def _conv_bwd(proj, dbr, w_out, conv_w, conv_b, name, after):
    _, s, e = proj.shape
    d = dbr.shape[1]
    ts, te = _tile(s, 512), _tile(e, 1024)
    cw = min(CONV_CHAIN_COLS // 2, te)
    hb = ts // PACK
    n_i = s // ts
    last_halo = s // PACK - 1
    nt = (((1,), (1,)), ((), ()))

    def body(bg_ref, cg_ref, xi_ref, z_ref, dbr_ref, cgp_ref, xip_ref, bgn_ref, zn_ref, dbrn_ref, wo_ref,
             w0_ref, w1_ref, w2_ref, b_ref, after_ref, dp_ref, dw0_ref, dw1_ref, dw2_ref, db_ref):
        i = pl.program_id(1)

        @pl.when(i == 0)
        def _():
            for acc in (dw0_ref, dw1_ref, dw2_ref, db_ref):
                acc[...] = jnp.zeros_like(acc)
        for c0 in range(0, te, cw):
            cols = slice(c0, c0 + cw)
            wo = wo_ref[cols, :]
            dy = lax.dot_general(dbr_ref[...], wo, nt, preferred_element_type=F32)
            dyn = lax.dot_general(dbrn_ref[...], wo, nt, preferred_element_type=F32)[:SUBLANE]
            bg, cg = _cols_f32(bg_ref, cols), _cols_f32(cg_ref, cols)
            xi, z = _cols_f32(xi_ref, cols), _cols_f32(z_ref, cols)
            w0, w1, w2 = w0_ref[:, cols], w1_ref[:, cols], w2_ref[:, cols]
            cx = cg * xi
            before = jnp.where(i > 0, _rows_before(cgp_ref, cols) * _rows_before(xip_ref, cols), 0.0)
            cx1 = _shift_down(cx, before, 1)
            cx2 = _shift_down(cx, before, 2)
            conv = b_ref[:, cols] + w2 * cx
            conv = conv + w0 * cx2
            conv = conv + w1 * cx1
            sz, dsz = _silu_and_grad(z)
            dp_ref[3, :, cols] = (dy * bg * conv * dsz).astype(BF16)
            dp_ref[0, :, cols] = (dy * sz * conv).astype(BF16)
            dconv = dy * sz * bg
            zn = _rows_after(zn_ref, cols)
            after = jnp.where(i < n_i - 1, dyn * (zn * _sigmoid(zn)) * _rows_after(bgn_ref, cols), 0.0)
            db_ref[:, cols] += _colsum(dconv)
            dw2_ref[:, cols] += _colsum(dconv * cx)
            dw1_ref[:, cols] += _colsum(dconv * cx1)
            dw0_ref[:, cols] += _colsum(dconv * cx2)
            dcx = w2 * dconv + w1 * _shift_up(dconv, after, 1) + w0 * _shift_up(dconv, after, 2)
            dp_ref[1, :, cols] = (dcx * xi).astype(BF16)
            dp_ref[2, :, cols] = (dcx * cg).astype(BF16)

    def part(q):
        return pl.BlockSpec((None, ts, te), lambda j, i: (q, i, j))

    def halo_before(q):
        return pl.BlockSpec((None, PACK, te), lambda j, i: (q, jnp.maximum(i * hb - 1, 0), j))

    def halo_after(q):
        return pl.BlockSpec((None, PACK, te), lambda j, i: (q, jnp.minimum((i + 1) * hb, last_halo), j))

    return pl.pallas_call(
        body, name=name,
        out_shape=(jax.ShapeDtypeStruct((4, s, e), BF16),) + (jax.ShapeDtypeStruct((1, e), F32),) * 4,
        grid=(e // te, n_i),
        in_specs=[part(0), part(1), part(2), part(3), pl.BlockSpec((ts, d), lambda j, i: (i, 0)),
                  halo_before(1), halo_before(2), halo_after(0), halo_after(3),
                  pl.BlockSpec((PACK, d), lambda j, i: (jnp.minimum((i + 1) * hb, last_halo), 0)),
                  pl.BlockSpec((te, d), lambda j, i: (j, 0))]
        + [pl.BlockSpec((1, te), lambda j, i: (0, j))] * 4 + [_after_spec()],
        out_specs=(pl.BlockSpec((4, ts, te), lambda j, i: (0, i, j)),) + (pl.BlockSpec((1, te), lambda j, i: (0, j)),) * 4,
        compiler_params=_params("parallel", "arbitrary"),
    )(proj, proj, proj, proj, dbr, proj, proj, proj, proj, dbr, w_out, *conv_w, conv_b, after)


def _tril(w):
    row = lax.broadcasted_iota(jnp.int32, w.shape, 0)
    col = lax.broadcasted_iota(jnp.int32, w.shape, 1)
    return jnp.where(row >= col, w, 0.0)


def _triu(w):
    row = lax.broadcasted_iota(jnp.int32, w.shape, 0)
    col = lax.broadcasted_iota(jnp.int32, w.shape, 1)
    return jnp.where(row <= col, w, 0.0)


def _layer_norm_fwd(v, g, b):
    mu = jnp.mean(v, axis=-1, keepdims=True)
    vc = v - mu
    rstd = lax.rsqrt(jnp.mean(vc * vc, axis=-1, keepdims=True) + LN_EPS)
    vhat = vc * rstd
    return vhat * g + b, vhat, rstd


GMLP_CHUNKS_PER_STEP = 2


def _gmlp_rows(s):
    return CHUNK * min(GMLP_CHUNKS_PER_STEP, s // CHUNK)


def _mix_positions(w_ref, src_scr, dst_scr, gw, mask, bias_ref=None):
    for gi in range(GROUPS):
        cols = slice(gi * gw, (gi + 1) * gw)
        wm = mask(w_ref[gi]).astype(BF16)
        for n in range(src_scr.shape[0] // CHUNK):
            rows = slice(n * CHUNK, (n + 1) * CHUNK)
            out = jnp.dot(wm, src_scr[rows, cols], preferred_element_type=F32)
            if bias_ref is not None:
                out = out + bias_ref[:, gi:gi + 1]
            dst_scr[rows, cols] = out


def _gmlp_fwd_head(proj, ln_g, ln_b, w_s, b_s_t, w_out, x1, gate, gf, target, name):
    _, s, e = proj.shape
    d = w_out.shape[1]
    gw = e // GROUPS
    ts = _gmlp_rows(s)

    def body(pu_ref, pv_ref, pz_ref, g_ref, b_ref, ws_ref, bs_ref, wo_ref, x1_ref, gate_ref, gf_ref, tg_ref,
             y_ref, dx_ref, dbr_ref, loss_ref, dgf_ref, dgate_ref, vn_scr, mix_scr):
        @pl.when(pl.program_id(0) == 0)
        def _():
            loss_ref[...] = jnp.zeros_like(loss_ref)
            dgf_ref[...] = jnp.zeros_like(dgf_ref)
            dgate_ref[...] = jnp.zeros_like(dgate_ref)
        vn, _, _ = _layer_norm_fwd(_gelu(_f32(pv_ref)), g_ref[...], b_ref[...])
        vn_scr[...] = vn.astype(BF16)
        _mix_positions(ws_ref, vn_scr, mix_scr, gw, _tril, bs_ref)
        z = _f32(pz_ref)
        y = ((z * _sigmoid(z)) * (_gelu(_f32(pu_ref)) * mix_scr[...])).astype(BF16)
        y_ref[...] = y
        br = jnp.dot(y, wo_ref[...], preferred_element_type=F32)
        dx, dbr, loss, dgf, dgate = _head_tile(x1_ref[...], br, gate_ref[...], gf_ref[...], tg_ref[...])
        dx_ref[...] = dx
        dbr_ref[...] = dbr.astype(BF16)
        loss_ref[...] += loss
        dgf_ref[...] += dgf
        dgate_ref[...] += dgate

    def part(q):
        return pl.BlockSpec((None, ts, e), lambda i: (q, i, 0))

    return pl.pallas_call(
        body, name=name,
        out_shape=(jax.ShapeDtypeStruct((s, e), BF16), jax.ShapeDtypeStruct((s, d), F32),
                   jax.ShapeDtypeStruct((s, d), BF16), jax.ShapeDtypeStruct((SUBLANE, 128), F32),
                   jax.ShapeDtypeStruct((1, d), F32), jax.ShapeDtypeStruct((1, d), F32)),
        grid=(s // ts,),
        in_specs=[part(0), part(1), part(2), _vec_spec(e), _vec_spec(e),
                  pl.BlockSpec((GROUPS, CHUNK, CHUNK), lambda i: (0, 0, 0)),
                  pl.BlockSpec((CHUNK, GROUPS), lambda i: (0, 0)), pl.BlockSpec((e, d), lambda i: (0, 0)),
                  _row_spec(ts, d), _vec_spec(d), _vec_spec(d), _row_spec(ts, d)],
        out_specs=(_row_spec(ts, e), _row_spec(ts, d), _row_spec(ts, d),
                   pl.BlockSpec((SUBLANE, 128), lambda i: (0, 0)), _vec_spec(d), _vec_spec(d)),
        scratch_shapes=[pltpu.VMEM((ts, e), BF16), pltpu.VMEM((ts, e), F32)],
        compiler_params=_params("arbitrary"),
    )(proj, proj, proj, ln_g, ln_b, w_s, b_s_t, w_out, x1, gate, gf, target)


def _gmlp_bwd(proj, dbr, w_out, ln_g, ln_b, w_s, w_s_t, b_s_t, name):
    _, s, e = proj.shape
    d = dbr.shape[1]
    gw = e // GROUPS
    ts = _gmlp_rows(s)
    n_i = s // ts

    def body(pu_ref, pv_ref, pz_ref, dbr_ref, wo_ref, g_ref, b_ref, ws_ref, wst_ref, bs_ref,
             dp_ref, dws_ref, dbs_ref, dlg_ref, dlb_ref, vn_scr, mix_scr, dm_scr, dvn_scr, dmacc_scr):
        i = pl.program_id(0)

        @pl.when(i == 0)
        def _():
            dws_ref[...] = jnp.zeros_like(dws_ref)
            dlg_ref[...] = jnp.zeros_like(dlg_ref)
            dlb_ref[...] = jnp.zeros_like(dlb_ref)
            dmacc_scr[...] = jnp.zeros_like(dmacc_scr)
        ln_g = g_ref[...]
        u, du_dpu = _gelu_and_grad(_f32(pu_ref))
        v, dv_dpv = _gelu_and_grad(_f32(pv_ref))
        vn, vhat, rstd = _layer_norm_fwd(v, ln_g, b_ref[...])
        vn_scr[...] = vn.astype(BF16)
        _mix_positions(ws_ref, vn_scr, mix_scr, gw, _tril, bs_ref)
        mixed = mix_scr[...]
        dy = lax.dot_general(dbr_ref[...], wo_ref[...], (((1,), (1,)), ((), ())), preferred_element_type=F32)
        sz, dsz = _silu_and_grad(_f32(pz_ref))
        ds = dy * sz
        dp_ref[2] = (dy * (u * mixed) * dsz).astype(BF16)
        dp_ref[0] = (ds * mixed * du_dpu).astype(BF16)
        dm = ds * u
        dm_scr[...] = dm.astype(BF16)
        for n in range(ts // CHUNK):
            rows = slice(n * CHUNK, (n + 1) * CHUNK)
            dmacc_scr[...] += dm[rows]
            for gi in range(GROUPS):
                cols = slice(gi * gw, (gi + 1) * gw)
                dws_ref[gi] += lax.dot_general(dm_scr[rows, cols], vn_scr[rows, cols], (((1,), (1,)), ((), ())),
                                               preferred_element_type=F32)
        _mix_positions(wst_ref, dm_scr, dvn_scr, gw, _triu)
        dvn = dvn_scr[...]
        dlg_ref[...] += _colsum(dvn * vhat)
        dlb_ref[...] += _colsum(dvn)
        dvh = dvn * ln_g
        dv = rstd * (dvh - jnp.mean(dvh, axis=-1, keepdims=True) - vhat * jnp.mean(dvh * vhat, axis=-1, keepdims=True))
        dp_ref[1] = (dv * dv_dpv).astype(BF16)

        @pl.when(i == n_i - 1)
        def _():
            for gi in range(GROUPS):
                dws_ref[gi] = _tril(dws_ref[gi])
                dbs_ref[:, gi:gi + 1] = jnp.sum(dmacc_scr[:, gi * gw:(gi + 1) * gw], axis=1, keepdims=True)

    def part(q):
        return pl.BlockSpec((None, ts, e), lambda i: (q, i, 0))

    w_spec = pl.BlockSpec((GROUPS, CHUNK, CHUNK), lambda i: (0, 0, 0))
    bs_spec = pl.BlockSpec((CHUNK, GROUPS), lambda i: (0, 0))
    return pl.pallas_call(
        body, name=name,
        out_shape=(jax.ShapeDtypeStruct((3, s, e), BF16), jax.ShapeDtypeStruct((GROUPS, CHUNK, CHUNK), F32),
                   jax.ShapeDtypeStruct((CHUNK, GROUPS), F32), jax.ShapeDtypeStruct((1, e), F32),
                   jax.ShapeDtypeStruct((1, e), F32)),
        grid=(n_i,),
        in_specs=[part(0), part(1), part(2), pl.BlockSpec((ts, d), lambda i: (i, 0)),
                  pl.BlockSpec((e, d), lambda i: (0, 0)), _vec_spec(e), _vec_spec(e), w_spec, w_spec, bs_spec],
        out_specs=(pl.BlockSpec((3, ts, e), lambda i: (0, i, 0)), w_spec, bs_spec, _vec_spec(e), _vec_spec(e)),
        scratch_shapes=[pltpu.VMEM((ts, e), BF16), pltpu.VMEM((ts, e), F32), pltpu.VMEM((ts, e), BF16),
                        pltpu.VMEM((ts, e), F32), pltpu.VMEM((CHUNK, e), F32)],
        compiler_params=_params("arbitrary"),
    )(proj, proj, proj, dbr, w_out, ln_g, ln_b, w_s, w_s_t, b_s_t)


def _mod_fwd(c_all, mod_w, mod_b_cols, name):
    n_layer, d, w = mod_w.shape

    def body(c_ref, w_ref, b_ref, ca_ref, o_ref):
        c = c_ref[...]
        ca = c * _sigmoid(c)
        ca_ref[...] = ca
        for li in range(n_layer):
            o_ref[li * N_DEV:(li + 1) * N_DEV, :] = (
                jnp.dot(ca, w_ref[li], preferred_element_type=F32, precision=lax.Precision.HIGHEST) + b_ref[li])

    return pl.pallas_call(
        body, name=name,
        out_shape=(jax.ShapeDtypeStruct((N_DEV, d), F32), jax.ShapeDtypeStruct((n_layer * N_DEV, w), F32)),
        in_specs=[VMEM_SPEC] * 3, out_specs=(VMEM_SPEC, VMEM_SPEC),
        compiler_params=pltpu.CompilerParams(vmem_limit_bytes=VMEM_LIMIT),
    )(c_all, mod_w, mod_b_cols)


def _adamw(w, g, m, v):
    m = ADAM_B1 * m + (1.0 - ADAM_B1) * g
    v = ADAM_B2 * v + (1.0 - ADAM_B2) * (g * g)
    m_hat = m / (1.0 - ADAM_B1 ** ADAM_STEP)
    v_hat = v / (1.0 - ADAM_B2 ** ADAM_STEP)
    delta = -ADAM_LR * (m_hat / (jnp.sqrt(v_hat) + ADAM_EPS) + ADAM_WD * w)
    return delta, m, v


def _adamw_2d(w, g, m, v, name):
    r, c = w.shape
    tr, tc = _tile(r, 512), _tile(c, 1024)

    def body(w_ref, g_ref, m_ref, v_ref, go_ref, d_ref, nm_ref, nv_ref):
        g = g_ref[...]
        go_ref[...] = g
        d_ref[...], nm_ref[...], nv_ref[...] = _adamw(w_ref[...], g, m_ref[...], v_ref[...])

    spec = pl.BlockSpec((tr, tc), lambda i, j: (i, j))
    shape = jax.ShapeDtypeStruct((r, c), F32)
    return pl.pallas_call(
        body, name=name, out_shape=(shape,) * 4, grid=(r // tr, c // tc),
        in_specs=[spec] * 4, out_specs=(spec,) * 4, compiler_params=_params("parallel", "parallel"),
    )(w, g, m, v)


def _mod_w_update(ca_t, dmod_cols, w, m, v, name):
    n_layer, d, wd = w.shape
    tr = _tile(d, 256)

    def body(ca_ref, dm_ref, w_ref, m_ref, v_ref, g_ref, d_ref, nm_ref, nv_ref):
        ca = ca_ref[...]
        dm = dm_ref[...]
        g = ca[:, 0:1] * dm[0:1, :]
        for b in range(1, N_DEV):
            g = g + ca[:, b:b + 1] * dm[b:b + 1, :]
        g_ref[...] = g
        d_ref[...], nm_ref[...], nv_ref[...] = _adamw(w_ref[...], g, m_ref[...], v_ref[...])

    spec = pl.BlockSpec((None, tr, wd), lambda l, i: (l, i, 0))
    shape = jax.ShapeDtypeStruct((n_layer, d, wd), F32)
    return pl.pallas_call(
        body, name=name, out_shape=(shape,) * 4, grid=(n_layer, d // tr),
        in_specs=[pl.BlockSpec((tr, N_DEV), lambda l, i: (i, 0)), pl.BlockSpec((None, N_DEV, wd), lambda l, i: (l, 0, 0)),
                  spec, spec, spec],
        out_specs=(spec,) * 4, compiler_params=_params("parallel", "parallel"),
    )(ca_t, dmod_cols, w, m, v)


def _adamw_small(ws, gs, ms, vs, name):
    n = len(ws)

    def body(*refs):
        ins, outs = refs[:4 * n], refs[4 * n:]
        for k in range(n):
            delta, nm, nv = _adamw(ins[k][...], ins[n + k][...], ins[2 * n + k][...], ins[3 * n + k][...])
            outs[3 * k][...] = delta
            outs[3 * k + 1][...] = nm
            outs[3 * k + 2][...] = nv

    out_shape = []
    for w in ws:
        out_shape += [jax.ShapeDtypeStruct(w.shape, F32)] * 3
    outs = pl.pallas_call(
        body, name=name, out_shape=tuple(out_shape),
        in_specs=[VMEM_SPEC] * (4 * n), out_specs=tuple([VMEM_SPEC] * (3 * n)),
        compiler_params=pltpu.CompilerParams(vmem_limit_bytes=VMEM_LIMIT),
    )(*ws, *gs, *ms, *vs)
    return [tuple(outs[3 * k:3 * k + 3]) for k in range(n)]


def _place():
    return lax.axis_index("x"), lax.axis_index("y"), lax.axis_index("c")


def _other_chips(x, y):
    return [(1 - x, y), (x, 1 - y), (1 - x, 1 - y)]


def _all_gather(block, name, after=None):
    m_per, n = block.shape

    def body(x_ref, *rest):
        out_ref, send_sems, recv_sems, local_sem = rest[-4:]
        x, y, c = _place()
        me, sibling = (x, y, c), (x, y, 1 - c)
        chips = _other_chips(x, y)

        def rows(px, py, pc):
            return out_ref.at[pl.ds((4 * px + 2 * py + pc) * m_per, m_per), :]

        def copy(k, blk, to, src=None):
            return pltpu.make_async_remote_copy(
                src_ref=rows(*blk) if src is None else src, dst_ref=rows(*blk),
                send_sem=send_sems.at[k], recv_sem=recv_sems.at[k], device_id=to, device_id_type=MESH)

        mine = pltpu.make_async_copy(x_ref, rows(*me), local_sem)
        mine.start()
        first = [copy(0, me, sibling, src=x_ref)]
        first += [copy(1 + j, me, (*chip, c), src=x_ref) for j, chip in enumerate(chips)]
        for cp in first:
            cp.start()
        passed = [copy(4 + j, (*chip, c), sibling) for j, chip in enumerate(chips)]
        for j, chip in enumerate(chips):
            copy(1 + j, (*chip, c), me).wait_recv()
            passed[j].start()
        copy(0, sibling, me).wait_recv()
        for j, chip in enumerate(chips):
            copy(4 + j, (*chip, 1 - c), me).wait_recv()
        for cp in first + passed:
            cp.wait_send()
        mine.wait()

    return pl.pallas_call(
        body, name=name, out_shape=jax.ShapeDtypeStruct((N_DEV * m_per, n), F32),
        in_specs=[VMEM_SPEC] + ([] if after is None else [ANY_SPEC]), out_specs=VMEM_SPEC,
        scratch_shapes=[pltpu.SemaphoreType.DMA((7,)), pltpu.SemaphoreType.DMA((7,)), pltpu.SemaphoreType.DMA],
        compiler_params=pltpu.CompilerParams(vmem_limit_bytes=VMEM_LIMIT),
    )(*([block] if after is None else [block, after]))


def _hbm(a):
    return pltpu.with_memory_space_constraint(a, pltpu.HBM)


def _all_gather_start(blocks, name):
    def body(in_ref, send_sems, recv_sems, thru_ref, token):
        token[...] = jnp.zeros_like(token)
        x, y, c = _place()
        mine = in_ref.at[4 * x + 2 * y + c]
        pltpu.make_async_remote_copy(
            src_ref=mine, dst_ref=mine, send_sem=send_sems.at[0], recv_sem=recv_sems.at[0],
            device_id=(x, y, 1 - c), device_id_type=MESH).start()
        for j, chip in enumerate(_other_chips(x, y)):
            for cc in range(2):
                pltpu.make_async_remote_copy(
                    src_ref=mine, dst_ref=mine, send_sem=send_sems.at[1 + 2 * j + cc],
                    recv_sem=recv_sems.at[1 + 2 * j + c], device_id=(*chip, cc), device_id_type=MESH).start()

    sems = pltpu.SemaphoreType.DMA((N_DEV - 1,))
    return pl.pallas_call(
        body, name=name,
        out_shape=(sems, sems, pltpu.HBM(blocks.shape, blocks.dtype), jax.ShapeDtypeStruct((SUBLANE, 128), F32)),
        in_specs=[HBM_SPEC], out_specs=(SEM_SPEC, SEM_SPEC, HBM_SPEC, VMEM_SPEC), input_output_aliases={0: 2},
        compiler_params=pltpu.CompilerParams(has_side_effects=EFFECT),
    )(_hbm(blocks))


def _all_gather_wait(send_sems, recv_sems, blocks, after, name):
    def body(in_ref, send_sems, recv_sems, after_ref, out_ref):
        x, y, c = _place()
        one = in_ref.at[0]
        for k in range(N_DEV - 1):
            pltpu.make_async_remote_copy(
                src_ref=one, dst_ref=one, send_sem=send_sems.at[k], recv_sem=recv_sems.at[k],
                device_id=(x, y, 1 - c), device_id_type=MESH).wait_send()
            pltpu.make_async_remote_copy(
                src_ref=one, dst_ref=one, send_sem=send_sems.at[k], recv_sem=recv_sems.at[k],
                device_id=(x, y, 1 - c), device_id_type=MESH).wait_recv()

    return pl.pallas_call(
        body, name=name, out_shape=pltpu.HBM(blocks.shape, blocks.dtype),
        in_specs=[HBM_SPEC, SEM_SPEC, SEM_SPEC, ANY_SPEC], out_specs=HBM_SPEC, input_output_aliases={0: 0},
        compiler_params=pltpu.CompilerParams(has_side_effects=EFFECT),
    )(blocks, send_sems, recv_sems, after)


def _place_own(shard, kind, chip_idx, name):
    r, cdim = shard.shape
    tr, tc = _tile(r, 512), _tile(cdim, 1024)
    nrb, ncb = r // tr, cdim // tc

    def body(k_ref, s_ref, o_ref):
        o_ref[...] = s_ref[...].astype(BF16)

    if kind == "col":
        full, o_map = (r, N_CHIP * cdim), lambda i, j, kr: (i, kr[0] * ncb + j)
    else:
        full, o_map = (N_CHIP * r, cdim), lambda i, j, kr: (kr[0] * nrb + i, j)
    return pl.pallas_call(
        body, name=name, out_shape=jax.ShapeDtypeStruct(full, BF16),
        grid_spec=pltpu.PrefetchScalarGridSpec(
            num_scalar_prefetch=1, grid=(nrb, ncb),
            in_specs=[pl.BlockSpec((tr, tc), lambda i, j, kr: (i, j))],
            out_specs=pl.BlockSpec((tr, tc), o_map)),
        compiler_params=_params("parallel", "parallel"),
    )(chip_idx, shard)


def _weight_window(ref, kind, shard_shape, k, half):
    r, cdim = shard_shape
    hr = r // 2
    if kind == "col":
        return ref.at[pl.ds(half * hr, hr), pl.ds(pl.multiple_of(k * cdim, 128), cdim)]
    return ref.at[pl.ds(pl.multiple_of(k * r + half * hr, 2 * SUBLANE), hr), :]


def _target_cores(c, both_cores):
    return [(cc, cc, c) for cc in range(2)] if both_cores else [(c, 0, 0)]


def _gather_weights_start(fulls, kinds, shard_shapes, groups, after, name, both_cores=True):
    n, ng = len(fulls), len(groups)
    per = 2 if both_cores else 1

    def body(*refs):
        ins = refs[:n]
        sems = refs[n + 1:n + 1 + 2 * ng]
        token = refs[2 * n + 1 + 2 * ng]
        x, y, c = _place()
        chips = _other_chips(x, y)
        for g, group in enumerate(groups):
            for pos, (w, j) in enumerate(group):
                own = _weight_window(ins[w], kinds[w], shard_shapes[w], 2 * x + y, c)
                for cc, mine, theirs in _target_cores(c, both_cores):
                    pltpu.make_async_remote_copy(
                        src_ref=own, dst_ref=own,
                        send_sem=sems[2 * g].at[per * pos + mine], recv_sem=sems[2 * g + 1].at[per * pos + theirs],
                        device_id=(*chips[j], cc), device_id_type=MESH).start()
        token[...] = jnp.zeros_like(token)

    sem_shapes = []
    for group in groups:
        sem_shapes += [pltpu.SemaphoreType.DMA((per * len(group),))] * 2
    outs = pl.pallas_call(
        body, name=name,
        out_shape=tuple(sem_shapes) + tuple(pltpu.HBM(f.shape, f.dtype) for f in fulls)
        + (jax.ShapeDtypeStruct((SUBLANE, 128), F32),),
        in_specs=[HBM_SPEC] * n + [ANY_SPEC], out_specs=(SEM_SPEC,) * (2 * ng) + (HBM_SPEC,) * n + (VMEM_SPEC,),
        input_output_aliases={w: 2 * ng + w for w in range(n)},
        compiler_params=pltpu.CompilerParams(has_side_effects=EFFECT),
    )(*[_hbm(f) for f in fulls], after)
    sems = [(outs[2 * g], outs[2 * g + 1]) for g in range(ng)]
    return sems, list(outs[2 * ng:2 * ng + n]), outs[2 * ng + n]


def _gather_weights_wait(sems, group, fulls, kinds, shard_shapes, after, name, both_cores=True):
    n = len(fulls)
    per = 2 if both_cores else 1

    def body(*refs):
        ins = refs[:n]
        send_sems, recv_sems = refs[n], refs[n + 1]
        x, y, c = _place()
        chips = _other_chips(x, y)
        for pos, (w, j) in enumerate(group):
            own = _weight_window(ins[w], kinds[w], shard_shapes[w], 2 * x + y, c)
            for cc, mine, _ in _target_cores(c, both_cores):
                landed = _weight_window(ins[w], kinds[w], shard_shapes[w], 2 * chips[j][0] + chips[j][1], cc)
                k = per * pos + mine
                pltpu.make_async_remote_copy(
                    src_ref=own, dst_ref=own, send_sem=send_sems.at[k], recv_sem=recv_sems.at[k],
                    device_id=(*chips[j], cc), device_id_type=MESH).wait_send()
                pltpu.make_async_remote_copy(
                    src_ref=landed, dst_ref=landed, send_sem=send_sems.at[k], recv_sem=recv_sems.at[k],
                    device_id=(*chips[j], cc), device_id_type=MESH).wait_recv()

    return list(pl.pallas_call(
        body, name=name, out_shape=tuple(pltpu.HBM(f.shape, f.dtype) for f in fulls),
        in_specs=[HBM_SPEC] * n + [SEM_SPEC, SEM_SPEC, ANY_SPEC], out_specs=(HBM_SPEC,) * n,
        input_output_aliases={w: w for w in range(n)},
        compiler_params=pltpu.CompilerParams(has_side_effects=EFFECT),
    )(*fulls, sems[0], sems[1], after))


def _forward_to_sibling(full, kind, shard_shape, relations, name):
    nr = len(relations)

    def body(in_ref, out_ref, send_sems, recv_sems):
        x, y, c = _place()
        chips = _other_chips(x, y)
        copies = []
        for pos, j in enumerate(relations):
            k = 2 * chips[j][0] + chips[j][1]
            mine = _weight_window(in_ref, kind, shard_shape, k, c)
            cp = pltpu.make_async_remote_copy(
                src_ref=mine, dst_ref=_weight_window(out_ref, kind, shard_shape, k, c),
                send_sem=send_sems.at[pos], recv_sem=recv_sems.at[pos], device_id=(x, y, 1 - c), device_id_type=MESH)
            cp.start()
            copies.append(cp)
        for pos, j in enumerate(relations):
            theirs = _weight_window(out_ref, kind, shard_shape, 2 * chips[j][0] + chips[j][1], 1 - c)
            pltpu.make_async_remote_copy(
                src_ref=theirs, dst_ref=theirs, send_sem=send_sems.at[pos], recv_sem=recv_sems.at[pos],
                device_id=(x, y, 1 - c), device_id_type=MESH).wait_recv()
        for cp in copies:
            cp.wait_send()

    return pl.pallas_call(
        body, name=name, out_shape=jax.ShapeDtypeStruct(full.shape, full.dtype),
        in_specs=[HBM_SPEC], out_specs=HBM_SPEC, input_output_aliases={0: 0},
        scratch_shapes=[pltpu.SemaphoreType.DMA((nr,))] * 2,
    )(full)


def _grad_piece(ref, kind, h, cdim, k, half):
    if kind == "col":
        return ref.at[pl.ds(half * h, h), pl.ds(pl.multiple_of(k * cdim, 128), cdim)]
    return ref.at[k, pl.ds(half * h, h), :]


def _grad_dims(g, kind):
    return (g.shape[0] // 2, g.shape[1] // N_CHIP) if kind == "col" else (g.shape[1] // 2, g.shape[2])


def _reduce_start(grads, kinds, name):
    n = len(grads)
    dims = [_grad_dims(g, kind) for g, kind in zip(grads, kinds)]
    lands = [lax.empty((N_DEV - 1, h, cdim), g.dtype) for g, (h, cdim) in zip(grads, dims)]

    def body(*refs):
        g_ins, land_ins = refs[:n], refs[n:2 * n]
        send_sems, recv_sems = refs[2 * n], refs[2 * n + 1]
        token = refs[4 * n + 2]
        x, y, c = _place()
        for w in range(n):
            h, cdim = dims[w]
            base = (N_DEV - 1) * w
            pltpu.make_async_remote_copy(
                src_ref=_grad_piece(g_ins[w], kinds[w], h, cdim, 2 * x + y, 1 - c), dst_ref=land_ins[w].at[0],
                send_sem=send_sems.at[base], recv_sem=recv_sems.at[base],
                device_id=(x, y, 1 - c), device_id_type=MESH).start()
            for j, chip in enumerate(_other_chips(x, y)):
                for cc in range(2):
                    pltpu.make_async_remote_copy(
                        src_ref=_grad_piece(g_ins[w], kinds[w], h, cdim, 2 * chip[0] + chip[1], cc),
                        dst_ref=land_ins[w].at[1 + 2 * j + c],
                        send_sem=send_sems.at[base + 1 + 2 * j + cc], recv_sem=recv_sems.at[base + 1 + 2 * j + c],
                        device_id=(*chip, cc), device_id_type=MESH).start()
        token[...] = jnp.zeros_like(token)

    sems = pltpu.SemaphoreType.DMA(((N_DEV - 1) * n,))
    outs = pl.pallas_call(
        body, name=name,
        out_shape=(sems, sems) + tuple(pltpu.HBM(a.shape, a.dtype) for a in list(grads) + lands)
        + (jax.ShapeDtypeStruct((SUBLANE, 128), F32),),
        in_specs=[HBM_SPEC] * (2 * n), out_specs=(SEM_SPEC, SEM_SPEC) + (HBM_SPEC,) * (2 * n) + (VMEM_SPEC,),
        input_output_aliases={i: 2 + i for i in range(2 * n)},
        compiler_params=pltpu.CompilerParams(has_side_effects=EFFECT),
    )(*[_hbm(a) for a in list(grads) + lands])
    return outs[0], outs[1], list(outs[2:2 + n]), list(outs[2 + n:2 + 2 * n]), outs[2 + 2 * n]


def _reduce_wait(send_sems, recv_sems, grads, lands, kinds, after, name):
    n = len(grads)
    dims = [_grad_dims(g, kind) for g, kind in zip(grads, kinds)]

    def body(*refs):
        g_ins, land_ins = refs[:n], refs[n:2 * n]
        send_sems, recv_sems = refs[2 * n], refs[2 * n + 1]
        x, y, c = _place()
        for w in range(n):
            h, cdim = dims[w]
            piece = _grad_piece(g_ins[w], kinds[w], h, cdim, 2 * x + y, c)
            for s in range(N_DEV - 1):
                k = (N_DEV - 1) * w + s
                slot = land_ins[w].at[s]
                pltpu.make_async_remote_copy(
                    src_ref=piece, dst_ref=slot, send_sem=send_sems.at[k], recv_sem=recv_sems.at[k],
                    device_id=(x, y, 1 - c), device_id_type=MESH).wait_send()
                pltpu.make_async_remote_copy(
                    src_ref=piece, dst_ref=slot, send_sem=send_sems.at[k], recv_sem=recv_sems.at[k],
                    device_id=(x, y, 1 - c), device_id_type=MESH).wait_recv()

    outs = pl.pallas_call(
        body, name=name, out_shape=tuple(pltpu.HBM(a.shape, a.dtype) for a in list(grads) + list(lands)),
        in_specs=[HBM_SPEC] * (2 * n) + [SEM_SPEC, SEM_SPEC, ANY_SPEC], out_specs=(HBM_SPEC,) * (2 * n),
        input_output_aliases={i: i for i in range(2 * n)},
        compiler_params=pltpu.CompilerParams(has_side_effects=EFFECT),
    )(*grads, *lands, send_sems, recv_sems, after)
    return list(outs[:n]), list(outs[n:])


def _add_pieces(g, land, kind, chip_idx, core_idx, name):
    _, h, cdim = land.shape
    tr, tc = _tile(h, 256), _tile(cdim, 2048)
    nrb, ncb = h // tr, cdim // tc

    def body(k_ref, c_ref, g_ref, l_ref, o_ref):
        acc = g_ref[...].astype(F32)
        for s in range(N_DEV - 1):
            acc = acc + l_ref[s].astype(F32)
        o_ref[...] = acc

    if kind == "col":
        g_spec = pl.BlockSpec((tr, tc), lambda i, j, kr, cr: (cr[0] * nrb + i, kr[0] * ncb + j))
    else:
        g_spec = pl.BlockSpec((None, tr, tc), lambda i, j, kr, cr: (kr[0], cr[0] * nrb + i, j))
    return pl.pallas_call(
        body, name=name, out_shape=jax.ShapeDtypeStruct((2 * h, cdim), F32),
        grid_spec=pltpu.PrefetchScalarGridSpec(
            num_scalar_prefetch=2, grid=(nrb, ncb),
            in_specs=[g_spec, pl.BlockSpec((N_DEV - 1, tr, tc), lambda i, j, kr, cr: (0, i, j))],
            out_specs=pl.BlockSpec((tr, tc), lambda i, j, kr, cr: (cr[0] * nrb + i, j))),
        compiler_params=_params("parallel", "parallel"),
    )(chip_idx, core_idx, g, land)


def _join_halves(shards, name):
    n = len(shards)

    def body(*refs):
        ins, outs = refs[:n], refs[n:2 * n]
        send_sems, recv_sems = refs[2 * n:]
        x, y, c = _place()
        copies = []
        for w in range(n):
            h = shards[w].shape[0] // 2
            cp = pltpu.make_async_remote_copy(
                src_ref=ins[w].at[pl.ds(c * h, h), :], dst_ref=outs[w].at[pl.ds(c * h, h), :],
                send_sem=send_sems.at[w], recv_sem=recv_sems.at[w], device_id=(x, y, 1 - c), device_id_type=MESH)
            cp.start()
            copies.append(cp)
        for w in range(n):
            h = shards[w].shape[0] // 2
            theirs = outs[w].at[pl.ds((1 - c) * h, h), :]
            pltpu.make_async_remote_copy(
                src_ref=theirs, dst_ref=theirs, send_sem=send_sems.at[w], recv_sem=recv_sems.at[w],
                device_id=(x, y, 1 - c), device_id_type=MESH).wait_recv()
        for cp in copies:
            cp.wait_send()

    return pl.pallas_call(
        body, name=name, out_shape=tuple(jax.ShapeDtypeStruct(s.shape, s.dtype) for s in shards),
        in_specs=[HBM_SPEC] * n, out_specs=tuple([HBM_SPEC] * n),
        input_output_aliases={w: w for w in range(n)},
        scratch_shapes=[pltpu.SemaphoreType.DMA((n,))] * 2,
    )(*shards)


def _sum_devices(gathered, rows, name):
    n = gathered.shape[1]

    def body(g_ref, o_ref):
        acc = g_ref[0:rows, :]
        for dev in range(1, N_DEV):
            acc = acc + g_ref[dev * rows:(dev + 1) * rows, :]
        o_ref[...] = acc

    return pl.pallas_call(
        body, name=name, out_shape=jax.ShapeDtypeStruct((rows, n), F32),
        in_specs=[VMEM_SPEC], out_specs=VMEM_SPEC,
        compiler_params=pltpu.CompilerParams(vmem_limit_bytes=VMEM_LIMIT),
    )(gathered)


def _pack(vectors, width):
    flat = [v.reshape(-1) for v in vectors]
    offsets, total = [], 0
    for f in flat:
        offsets.append(total)
        total += f.shape[0]
    rows = -(-total // (width * SUBLANE)) * SUBLANE
    flat.append(jnp.zeros((rows * width - total,), F32))
    return jnp.concatenate(flat).reshape(rows, width), offsets


def kernel(x, c, mod_w, mod_b, norm_g, a_w_in, a_conv_w, a_conv_b, a_w_out, b_w_in, b_ln_g, b_ln_b, b_w_s, b_b_s, b_w_out, final_g, loss_target, m_mod_w, m_mod_b, m_norm_g, m_a_w_in, m_a_conv_w, m_a_conv_b, m_a_w_out, m_b_w_in, m_b_ln_g, m_b_ln_b, m_b_w_s, m_b_b_s, m_b_w_out, m_final_g, v_mod_w, v_mod_b, v_norm_g, v_a_w_in, v_a_conv_w, v_a_conv_b, v_a_w_out, v_b_w_in, v_b_ln_g, v_b_ln_b, v_b_w_s, v_b_b_s, v_b_w_out, v_final_g):
    seq, d = x.shape[1], x.shape[2]
    e = a_conv_b.shape[1]
    wd = mod_w.shape[2]
    ax, ay, ac = _place()
    chip = 2 * ax + ay
    dev = 2 * chip + ac
    chip_idx = jnp.reshape(chip, (1,)).astype(jnp.int32)
    core_idx = jnp.reshape(ac, (1,)).astype(jnp.int32)

    x2d = x[0]
    target = loss_target[0]

    w_names = ["a_w_in", "a_w_out", "b_w_in", "b_w_out"]
    w_kinds = ["col", "row", "col", "row"]
    w_shards = [a_w_in[0], a_w_out[0], b_w_in[0], b_w_out[0]]
    w_shapes = [sh.shape for sh in w_shards]
    placed = [_place_own(sh, kind, chip_idx, "place_" + nm) for nm, sh, kind in zip(w_names, w_shards, w_kinds)]
    a_groups, far_groups = [[(0, 0), (0, 1)]], [[(0, 2)]]
    whole_group = [[(0, 0), (0, 1), (0, 2)]]

    es = e // N_CHIP
    packed0, offs0 = _pack([c, a_conv_w, b_ln_g, b_ln_b], 1024)
    gathered0 = _all_gather(packed0, "gather_params").reshape(N_DEV, -1)
    c_all = gathered0[:, :d]
    per_chip = gathered0[0::2]

    def from_chips_cols(k, rows_):
        got = per_chip[:, offs0[k]:offs0[k] + rows_ * es].reshape(N_CHIP, rows_, es)
        return jnp.transpose(got, (1, 0, 2)).reshape(rows_, e)

    conv_w_full = from_chips_cols(1, 3)
    conv_w = [conv_w_full[k:k + 1] for k in range(3)]
    ln_g, ln_b = from_chips_cols(2, 1), from_chips_cols(3, 1)
    mod_b_cols = lax.dynamic_slice_in_dim(mod_b, chip * wd, wd, axis=1)[:, None, :]
    c_act, mod_part = _mod_fwd(c_all, mod_w, mod_b_cols, "mod_fwd")
    n_layer = mod_w.shape[0]
    mod_gathered = _all_gather(mod_part, "gather_mod")
    a_sems, (wa_in,), a_token = _gather_weights_start(
        placed[:1], w_kinds[:1], w_shapes[:1], a_groups, mod_gathered, "gather_a_w_in_start", both_cores=False)
    mod_all = mod_gathered.reshape(N_CHIP, 2, n_layer, N_DEV, wd)[:, 0]
    mod_all = jnp.transpose(mod_all, (1, 2, 0, 3)).reshape(n_layer, N_DEV, N_CHIP * wd)
    mod_me = lax.dynamic_index_in_dim(mod_all, dev, axis=1, keepdims=False)
    shift = [mod_me[l:l + 1, 0:d] for l in range(n_layer)]
    scale = [mod_me[l:l + 1, d:2 * d] for l in range(n_layer)]
    gate = [mod_me[l:l + 1, 2 * d:3 * d] for l in range(n_layer)]

    g0, g1, gf = norm_g[0:1], norm_g[1:2], final_g[None, :]
    h0 = _norm_mod(x2d, g0, scale[0], shift[0], "norm_mod0")

    def slab(r):
        return jnp.bitwise_xor(chip_idx, r)

    def arrived(sems, group, weight, w, after, name, both_cores=True):
        return _gather_weights_wait(sems, [(0, j) for _, j in group], [weight], [w_kinds[w]], [w_shapes[w]],
                                    after, name, both_cores)[0]

    proj0 = _mm_proj_slab(h0, wa_in, None, slab(0), 4, "a_proj_own", a_token)
    wa_in = arrived(a_sems[0], a_groups[0], wa_in, 0, proj0, "gather_wait_near", both_cores=False)
    wa_in = _forward_to_sibling(wa_in, w_kinds[0], w_shapes[0], [0, 1], "forward_near")
    far_sems, (wa_in,), far_token = _gather_weights_start(
        [wa_in], w_kinds[:1], w_shapes[:1], far_groups, a_token, "gather_a_w_in_far_start")
    proj0 = _mm_proj_slab(h0, wa_in, proj0, slab(2), 4, "a_proj_x", far_token)
    proj0 = _mm_proj_slab(h0, wa_in, proj0, slab(1), 4, "a_proj_y", far_token)
    wa_in = arrived(far_sems[0], far_groups[0], wa_in, 0, proj0, "gather_wait_far")
    def start_whole(w, after, name):
        sems, (flight,), token = _gather_weights_start(
            placed[w:w + 1], w_kinds[w:w + 1], w_shapes[w:w + 1], whole_group, after, name)
        return sems[0], flight, token

    ao_sems, wa_out, ao_token = start_whole(1, wa_in, "gather_a_w_out_start")
    proj0 = _mm_proj_slab(h0, wa_in, proj0, slab(3), 4, "a_proj_far", ao_token)
    wa_out = arrived(ao_sems, whole_group[0], wa_out, 1, proj0, "gather_wait_a_w_out")
    bi_sems, wb_in, bi_token = start_whole(2, wa_out, "gather_b_w_in_start")
    y0, br0, x1, h1 = _conv_fwd(proj0, conv_w, a_conv_b, wa_out, x2d, gate[0], g1, scale[1], shift[1],
                                "conv_fwd", bi_token)
    wb_in = arrived(bi_sems, whole_group[0], wb_in, 2, h1, "gather_wait_b_w_in")
    bo_sems, wb_out, bo_token = start_whole(3, wb_in, "gather_b_w_out_start")
    proj1 = _mm_proj(h1, wb_in, 3, "b_proj", bo_token)
    b_s_t = jnp.transpose(b_b_s[0])
    wb_out = arrived(bo_sems, whole_group[0], wb_out, 3, proj1, "gather_wait_b_w_out")
    y1, dx2, dbr1, loss_part, g_final_g, dgate1 = _gmlp_fwd_head(
        proj1, ln_g, ln_b, b_w_s[0], b_s_t, wb_out, x1, gate[1], gf, target, "gmlp_fwd_head")

    gw_b_out = _mm_dw_out(y1, dbr1, "b_out_dw")
    dproj1, g_w_s, g_b_s_t, g_ln_g, g_ln_b = _gmlp_bwd(
        proj1, dbr1, wb_out, ln_g, ln_b, b_w_s[0], jnp.swapaxes(b_w_s[0], 1, 2), b_s_t, "gmlp_bwd")
    gw_b_in = _mm_dw_in(h1, dproj1, "b_proj_dw")
    b_kinds = ["col", "row"]
    b_send, b_recv, b_grads, b_lands, b_token = _reduce_start(
        [gw_b_in, gw_b_out.reshape(N_CHIP, e // N_CHIP, d)], b_kinds, "reduce_b_start")
    dx1, dshift1, dscale1, g_g1, dbr0, dgate0 = _mm_dh_norm_bwd(
        dproj1, wb_in, x1, dx2, g1, scale[1], "b_proj_dx", b_token, br=br0, gate=gate[0])

    early = [loss_part[0, 0:1], g_final_g, g_g1, g_ln_g, g_ln_b, jnp.transpose(g_b_s_t), g_w_s,
             jnp.concatenate([dshift1, dscale1, dgate1, dgate0], axis=1)]
    packed_e, offs_e = _pack(early, 1024)
    rows_e = packed_e.shape[0]
    blocks_e = lax.dynamic_update_slice(jnp.zeros((N_DEV, rows_e, 1024), F32), packed_e[None], (dev, 0, 0))
    e_send, e_recv, blocks_e, e_token = _all_gather_start(blocks_e, "gather_small_start")

    gw_a_out = _mm_dw_out(y0, dbr0, "a_out_dw")
    ao_send, ao_recv, ao_grads, ao_lands, ao_grad_token = _reduce_start(
        [gw_a_out.reshape(N_CHIP, e // N_CHIP, d)], ["row"], "reduce_a_out_start")
    dproj0, g_w0, g_w1, g_w2, g_conv_b = _conv_bwd(
        proj0, dbr0, wa_out, conv_w, a_conv_b, "conv_bwd", ao_grad_token + e_token)
    gw_a_in = _mm_dw_in(h0, dproj0, "a_proj_dw")
    ai_send, ai_recv, ai_grads, ai_lands, ai_token = _reduce_start([gw_a_in], ["col"], "reduce_a_in_start")
    grad_x, dshift0, dscale0, g_g0 = _mm_dh_norm_bwd(dproj0, wa_in, x2d, dx1, g0, scale[0], "a_proj_dx", ai_token)

    def finish(send, recv, grads_, lands_, kinds_, names_, after, tag):
        grads_, lands_ = _reduce_wait(send, recv, grads_, lands_, kinds_, after, "reduce_" + tag + "_wait")
        halves = [_add_pieces(g, land, kind, chip_idx, core_idx, "add_pieces_" + nm)
                  for g, land, kind, nm in zip(grads_, lands_, kinds_, names_)]
        return _join_halves(halves, "join_" + tag)

    upd, big_grads = {}, {}

    def adamw_big(nm, w, g, m, v):
        g_out, *rest = _adamw_2d(w[0], g, m[0], v[0], "adamw_" + nm)
        big_grads[nm] = g_out[None]
        upd[nm] = tuple(o[None] for o in rest)

    g_b_w_in, g_b_w_out = finish(b_send, b_recv, b_grads, b_lands, b_kinds, ["b_w_in", "b_w_out"], grad_x, "b")
    adamw_big("b_w_in", b_w_in, g_b_w_in, m_b_w_in, v_b_w_in)
    adamw_big("b_w_out", b_w_out, g_b_w_out, m_b_w_out, v_b_w_out)

    late = [g_g0, jnp.concatenate([g_w0, g_w1, g_w2], axis=0), g_conv_b, jnp.concatenate([dshift0, dscale0], axis=1)]
    packed_l, offs_l = _pack(late, 1024)
    rows_l = packed_l.shape[0]
    gathered_l = _all_gather(packed_l, "gather_small_late", after=upd["b_w_out"][0])
    blocks_e = _all_gather_wait(e_send, e_recv, blocks_e, gathered_l, "gather_small_wait")
    gathered_e = blocks_e.reshape(N_DEV * rows_e, 1024)
    summed_e = _sum_devices(gathered_e, rows_e, "sum_small_early").reshape(-1)
    summed_l = _sum_devices(gathered_l, rows_l, "sum_small_late").reshape(-1)

    def take(summed, offs, k, shape):
        size = math.prod(shape)
        return summed[offs[k]:offs[k] + size].reshape(shape)

    def rows_of(gathered, rows_, offs, k, width):
        return gathered.reshape(N_DEV, rows_ * 1024)[:, offs[k]:offs[k] + width]

    loss = take(summed_e, offs_e, 0, ())
    grad_final_g = take(summed_e, offs_e, 1, (d,))
    grad_norm_g = jnp.concatenate([take(summed_l, offs_l, 0, (1, d)), take(summed_e, offs_e, 2, (1, d))], axis=0)
    grad_ln_g_full = take(summed_e, offs_e, 3, (1, e))
    grad_ln_b_full = take(summed_e, offs_e, 4, (1, e))
    grad_b_b_s = take(summed_e, offs_e, 5, (1, GROUPS, CHUNK))
    grad_b_w_s = take(summed_e, offs_e, 6, (1, GROUPS, CHUNK, CHUNK))
    grad_conv_w_full = take(summed_l, offs_l, 1, (3, e))
    grad_a_conv_b = take(summed_l, offs_l, 2, (1, e))
    grad_a_conv_w = lax.dynamic_slice_in_dim(grad_conv_w_full, chip * es, es, axis=1)[None]
    grad_b_ln_g = lax.dynamic_slice_in_dim(grad_ln_g_full, chip * es, es, axis=1)
    grad_b_ln_b = lax.dynamic_slice_in_dim(grad_ln_b_full, chip * es, es, axis=1)
    dmod_e = rows_of(gathered_e, rows_e, offs_e, 7, 4 * d)
    dmod_l = rows_of(gathered_l, rows_l, offs_l, 3, 2 * d)
    dmod_all = jnp.stack([jnp.concatenate([dmod_l, dmod_e[:, 3 * d:]], axis=1), dmod_e[:, :3 * d]], axis=1)
    mod_b_e = take(summed_e, offs_e, 7, (4 * d,))
    mod_b_l = take(summed_l, offs_l, 3, (2 * d,))
    grad_mod_b = jnp.stack([jnp.concatenate([mod_b_l, mod_b_e[3 * d:]]), mod_b_e[:3 * d]])
    dmod_cols = jnp.transpose(lax.dynamic_slice_in_dim(dmod_all, chip * wd, wd, axis=2), (1, 0, 2))

    grad_mod_w, delta_mod_w, new_m_mod_w, new_v_mod_w = _mod_w_update(
        jnp.transpose(c_act), dmod_cols, mod_w, m_mod_w, v_mod_w, "mod_w_update")
    (g_a_w_out,) = finish(ao_send, ao_recv, ao_grads, ao_lands, ["row"], ["a_w_out"], delta_mod_w, "a_out")
    adamw_big("a_w_out", a_w_out, g_a_w_out, m_a_w_out, v_a_w_out)
    (g_a_w_in,) = finish(ai_send, ai_recv, ai_grads, ai_lands, ["col"], ["a_w_in"], upd["a_w_out"][0], "a_in")
    adamw_big("a_w_in", a_w_in, g_a_w_in, m_a_w_in, v_a_w_in)
    small_w = [("mod_b", mod_b, grad_mod_b, m_mod_b, v_mod_b), ("norm_g", norm_g, grad_norm_g, m_norm_g, v_norm_g),
               ("a_conv_w", a_conv_w, grad_a_conv_w, m_a_conv_w, v_a_conv_w),
               ("a_conv_b", a_conv_b, grad_a_conv_b, m_a_conv_b, v_a_conv_b),
               ("b_ln_g", b_ln_g, grad_b_ln_g, m_b_ln_g, v_b_ln_g), ("b_ln_b", b_ln_b, grad_b_ln_b, m_b_ln_b, v_b_ln_b),
               ("b_w_s", b_w_s, grad_b_w_s, m_b_w_s, v_b_w_s), ("b_b_s", b_b_s, grad_b_b_s, m_b_b_s, v_b_b_s),
               ("final_g", final_g, grad_final_g, m_final_g, v_final_g)]

    def flat2d(a):
        return a.reshape(-1, a.shape[-1])

    res = _adamw_small([flat2d(t[1]) for t in small_w], [flat2d(t[2]) for t in small_w],
                       [flat2d(t[3]) for t in small_w], [flat2d(t[4]) for t in small_w], "adamw_small")
    for (nm, w, _, _, _), r3 in zip(small_w, res):
        upd[nm] = tuple(o.reshape(w.shape) for o in r3)
    upd["mod_w"] = (delta_mod_w, new_m_mod_w, new_v_mod_w)

    grads = {"mod_w": grad_mod_w, "mod_b": grad_mod_b, "norm_g": grad_norm_g, "a_conv_w": grad_a_conv_w,
             "a_conv_b": grad_a_conv_b, "b_ln_g": grad_b_ln_g, "b_ln_b": grad_b_ln_b, "b_w_s": grad_b_w_s,
             "b_b_s": grad_b_b_s, "final_g": grad_final_g, **big_grads}
    order = ["mod_w", "mod_b", "norm_g", "a_w_in", "a_conv_w", "a_conv_b", "a_w_out", "b_w_in", "b_ln_g", "b_ln_b",
             "b_w_s", "b_b_s", "b_w_out", "final_g"]
    return (loss, grad_x[None], *[grads[k] for k in order], *[upd[k][0] for k in order],
            *[upd[k][1] for k in order], *[upd[k][2] for k in order])
```

```python
import functools
import math

import jax
import jax.numpy as jnp
from jax import lax
from jax.experimental import pallas as pl
from jax.experimental.pallas import tpu as pltpu

F32 = jnp.float32
BF16 = jnp.bfloat16
MESH = pl.DeviceIdType.MESH

N_DEV = 8
N_CHIP = 4
SUBLANE = 8
PACK = 16
ACT = F32
RMS_EPS = 1e-6
LN_EPS = 1e-5
CHUNK = 128
GROUPS = 8
ADAM_LR = 0.001
ADAM_B1 = 0.9
ADAM_B2 = 0.999
ADAM_EPS = 1e-08
ADAM_WD = 0.01
ADAM_STEP = 10
VMEM_LIMIT = 56 << 20

HBM_SPEC = pl.BlockSpec(memory_space=pltpu.HBM)
VMEM_SPEC = pl.BlockSpec(memory_space=pltpu.VMEM)
SEM_SPEC = pl.BlockSpec(memory_space=pltpu.SEMAPHORE)
ANY_SPEC = pl.BlockSpec(memory_space=pl.ANY)
EFFECT = pltpu.SideEffectType.DATAFLOW_SIDE_EFFECTING


def _params(*sem):
    return pltpu.CompilerParams(dimension_semantics=sem, vmem_limit_bytes=VMEM_LIMIT)


def _tile(n, want):
    if n <= want:
        return n
    t = want
    while n % t:
        t -= 128
    return t


def _sigmoid(x):
    return 0.5 * jnp.tanh(0.5 * x) + 0.5


def _silu_and_grad(x):
    s = _sigmoid(x)
    return x * s, s * (1.0 + x * (1.0 - s))


def _gelu_and_grad(x):
    cdf = 0.5 * (1.0 + lax.erf(x * (1.0 / math.sqrt(2.0))))
    pdf = jnp.exp(-0.5 * x * x) * (1.0 / math.sqrt(2.0 * math.pi))
    return x * cdf, cdf + x * pdf


def _gelu(x):
    return x * (0.5 * (1.0 + lax.erf(x * (1.0 / math.sqrt(2.0)))))


def _rms(x):
    r = lax.rsqrt(jnp.mean(x * x, axis=-1, keepdims=True) + RMS_EPS)
    return x * r, r


def _rms_bwd(dxn, xn, r):
    return r * (dxn - xn * jnp.mean(dxn * xn, axis=-1, keepdims=True))


def _colsum(a):
    return jnp.sum(a, axis=0, keepdims=True)


def _shift_down(cur, before, k):
    rolled = pltpu.roll(cur, k, 0)
    row = lax.broadcasted_iota(jnp.int32, before.shape, 0)
    head = jnp.where(row < k, pltpu.roll(before, k, 0), rolled[:SUBLANE])
    return jnp.concatenate([head, rolled[SUBLANE:]], axis=0)


def _shift_up(cur, after, k):
    n = cur.shape[0]
    rolled = pltpu.roll(cur, n - k, 0)
    row = lax.broadcasted_iota(jnp.int32, after.shape, 0)
    tail = jnp.where(row >= SUBLANE - k, pltpu.roll(after, SUBLANE - k, 0), rolled[n - SUBLANE:])
    return jnp.concatenate([rolled[:n - SUBLANE], tail], axis=0)


def _after_spec():
    return pl.BlockSpec((SUBLANE, 128), lambda *_: (0, 0))


def _mm_proj(h, w, n_split, name, after):
    s, d = h.shape
    e = w.shape[1] // n_split
    tm, tn = _tile(s, 1024), _tile(e, 2048)
    nj = e // tn

    def body(h_ref, w_ref, after_ref, o_ref):
        o_ref[...] = jnp.dot(h_ref[...], w_ref[...], preferred_element_type=F32).astype(ACT)

    return pl.pallas_call(
        body, name=name,
        out_shape=jax.ShapeDtypeStruct((n_split, s, e), ACT),
        grid=(s // tm, n_split * nj),
        in_specs=[pl.BlockSpec((tm, d), lambda i, j: (i, 0)), pl.BlockSpec((d, tn), lambda i, j: (0, j)),
                  _after_spec()],
        out_specs=pl.BlockSpec((None, tm, tn), lambda i, j: (j // nj, i, j % nj)),
        compiler_params=_params("parallel", "parallel"),
    )(h, w, after)


def _mm_proj_slab(h, w, proj, q_idx, n_split, name, after):
    s, d = h.shape
    e = w.shape[1] // n_split
    tm, tn = _tile(s, 1024), _tile(e, 2048)
    nj = e // tn

    def body(q_ref, h_ref, w_ref, *rest):
        o_ref = rest[-1]
        o_ref[...] = jnp.dot(h_ref[...], w_ref[...], preferred_element_type=F32).astype(ACT)

    in_specs = [pl.BlockSpec((tm, d), lambda i, j, qr: (i, 0)), pl.BlockSpec((d, tn), lambda i, j, qr: (0, qr[0] * nj + j)),
                _after_spec()]
    args = [q_idx, h, w, after]
    aliases = {}
    if proj is not None:
        in_specs.append(ANY_SPEC)
        args.append(proj)
        aliases = {4: 0}
    return pl.pallas_call(
        body, name=name,
        out_shape=jax.ShapeDtypeStruct((n_split, s, e), ACT),
        grid_spec=pltpu.PrefetchScalarGridSpec(
            num_scalar_prefetch=1, grid=(s // tm, nj), in_specs=in_specs,
            out_specs=pl.BlockSpec((None, tm, tn), lambda i, j, qr: (qr[0], i, j))),
        input_output_aliases=aliases,
        compiler_params=_params("parallel", "parallel"),
    )(*args)


def _mm_dh_norm_bwd(dp, w, x, dx_in, g, scale, name, after, br=None, gate=None):
    nq, s, e = dp.shape
    d = w.shape[0]
    has_branch = br is not None
    tm, tk = _tile(s, 512 if has_branch else 1024), _tile(e, 2048)
    nkq = e // tk
    nk = nq * nkq

    def body(*refs):
        if has_branch:
            (a_ref, b_ref, after_ref, x_ref, dxin_ref, g_ref, sc_ref, br_ref, gate_ref,
             dx_ref, dsh_ref, dsc_ref, dg_ref, dbr_ref, dgate_ref, acc_ref) = refs
        else:
            (a_ref, b_ref, after_ref, x_ref, dxin_ref, g_ref, sc_ref,
             dx_ref, dsh_ref, dsc_ref, dg_ref, acc_ref) = refs
        i, k = pl.program_id(0), pl.program_id(1)
        cur, prev = i % 2, (i + 1) % 2
        nt = (((1,), (1,)), ((), ()))

        @pl.when(jnp.logical_and(i == 0, k == 0))
        def _():
            dsh_ref[...] = jnp.zeros_like(dsh_ref)
            dsc_ref[...] = jnp.zeros_like(dsc_ref)
            dg_ref[...] = jnp.zeros_like(dg_ref)
            if has_branch:
                dgate_ref[...] = jnp.zeros_like(dgate_ref)
            acc_ref[1] = jnp.zeros((tm, d), F32)

        def epilogue():
            counts = i > 0
            dh = acc_ref[prev]
            g_vec = g_ref[...]
            xn, r = _rms(x_ref[...])
            dsh_ref[...] += jnp.where(counts, _colsum(dh), 0.0)
            dsc_ref[...] += jnp.where(counts, _colsum(dh * (xn * g_vec)), 0.0)
            da = dh * (1.0 + sc_ref[...])
            dg_ref[...] += jnp.where(counts, _colsum(da * xn), 0.0)
            dx = dxin_ref[...] + _rms_bwd(da * g_vec, xn, r)
            dx_ref[...] = dx
            if has_branch:
                dgate_ref[...] += jnp.where(counts, _colsum(dx * br_ref[...]), 0.0)
                dbr_ref[...] = (gate_ref[...] * dx).astype(BF16)

        @pl.when(jnp.logical_and(i < n_i, k == 0))
        def _():
            acc_ref[cur] = lax.dot_general(a_ref[...], b_ref[...], nt, preferred_element_type=F32)
            epilogue()

        @pl.when(jnp.logical_and(i < n_i, k > 0))
        def _():
            acc_ref[cur] += lax.dot_general(a_ref[...], b_ref[...], nt, preferred_element_type=F32)

        @pl.when(jnp.logical_and(i == n_i, k == 0))
        def _():
            epilogue()

    n_i = s // tm
    rows = pl.BlockSpec((tm, d), lambda i, k: (jnp.maximum(i - 1, 0), 0))
    vec = pl.BlockSpec((1, d), lambda i, k: (0, 0))
    vec_out = jax.ShapeDtypeStruct((1, d), F32)
    in_specs = [pl.BlockSpec((None, tm, tk), lambda i, k: (k // nkq, jnp.minimum(i, n_i - 1), k % nkq)),
                pl.BlockSpec((d, tk), lambda i, k: (0, k)), _after_spec(), rows, rows, vec, vec]
    out_shape = [jax.ShapeDtypeStruct((s, d), F32), vec_out, vec_out, vec_out]
    out_specs = [rows, vec, vec, vec]
    args = [dp, w, after, x, dx_in, g, scale]
    if has_branch:
        in_specs += [rows, vec]
        out_shape += [jax.ShapeDtypeStruct((s, d), BF16), vec_out]
        out_specs += [rows, vec]
        args += [br, gate]
    return pl.pallas_call(
        body, name=name, out_shape=tuple(out_shape), grid=(n_i + 1, nk),
        in_specs=in_specs, out_specs=tuple(out_specs), scratch_shapes=[pltpu.VMEM((2, tm, d), F32)],
        compiler_params=_params("arbitrary", "arbitrary"),
    )(*args)


def _dw_body(n_t):
    def body(a_ref, b_ref, o_ref, acc_ref):
        t = pl.program_id(2)

        @pl.when(t == 0)
        def _():
            acc_ref[...] = jnp.zeros_like(acc_ref)
        acc_ref[...] += lax.dot_general(a_ref[...], b_ref[...], (((0,), (0,)), ((), ())), preferred_element_type=F32)

        @pl.when(t == n_t - 1)
        def _():
            o_ref[...] = acc_ref[...].astype(o_ref.dtype)
    return body


def _mm_dw_in(h, dp, name):
    s, d = h.shape
    nq, _, e = dp.shape
    tm, tn, tt = _tile(d, 1024), _tile(e, 1024), _tile(s, 2048)
    nj = e // tn

    return pl.pallas_call(
        _dw_body(s // tt), name=name,
        out_shape=jax.ShapeDtypeStruct((d, nq * e), BF16),
        grid=(d // tm, nq * nj, s // tt),
        in_specs=[pl.BlockSpec((tt, tm), lambda i, j, t: (t, i)),
                  pl.BlockSpec((None, tt, tn), lambda i, j, t: (j // nj, t, j % nj))],
        out_specs=pl.BlockSpec((tm, tn), lambda i, j, t: (i, j)),
        scratch_shapes=[pltpu.VMEM((tm, tn), F32)],
        compiler_params=_params("parallel", "parallel", "arbitrary"),
    )(h, dp)


def _mm_dw_out(y, dbr, name):
    s, e = y.shape
    d = dbr.shape[1]
    tm, tn, tt = _tile(e, 1024), _tile(d, 1024), _tile(s, 2048)

    return pl.pallas_call(
        _dw_body(s // tt), name=name,
        out_shape=jax.ShapeDtypeStruct((e, d), BF16),
        grid=(e // tm, d // tn, s // tt),
        in_specs=[pl.BlockSpec((tt, tm), lambda i, j, t: (t, i)), pl.BlockSpec((tt, tn), lambda i, j, t: (t, j))],
        out_specs=pl.BlockSpec((tm, tn), lambda i, j, t: (i, j)),
        scratch_shapes=[pltpu.VMEM((tm, tn), F32)],
        compiler_params=_params("parallel", "parallel", "arbitrary"),
    )(y, dbr)


def _row_spec(ts, d):
    return pl.BlockSpec((ts, d), lambda i: (i, 0))


def _vec_spec(d):
    return pl.BlockSpec((1, d), lambda i: (0, 0))


def _norm_mod(x, g, scale, shift, name):
    s, d = x.shape
    ts = _tile(s, 512)

    def body(x_ref, g_ref, sc_ref, sh_ref, h_ref):
        xn, _ = _rms(x_ref[...])
        h_ref[...] = ((xn * g_ref[...]) * (1.0 + sc_ref[...]) + sh_ref[...]).astype(BF16)

    return pl.pallas_call(
        body, name=name, out_shape=jax.ShapeDtypeStruct((s, d), BF16), grid=(s // ts,),
        in_specs=[_row_spec(ts, d), _vec_spec(d), _vec_spec(d), _vec_spec(d)],
        out_specs=_row_spec(ts, d), compiler_params=_params("parallel"),
    )(x, g, scale, shift)


def _head_tile(x1, br, gate, gf, target):
    d = x1.shape[-1]
    xn, r = _rms(x1 + gate * br)
    err = xn * gf - target
    loss = 0.5 * jnp.sum(jnp.mean(err * err, axis=-1, keepdims=True))
    dout = err * (1.0 / d)
    dx = _rms_bwd(dout * gf, xn, r)
    return dx, gate * dx, loss, _colsum(dout * xn), _colsum(dx * br)


def _f32(ref):
    return ref[...].astype(F32)


def _cols_f32(ref, cols):
    return ref[:, cols].astype(F32)


def _rows_before(halo_ref, cols=slice(None)):
    return _cols_f32(halo_ref, cols)[PACK - SUBLANE:]


def _rows_after(halo_ref, cols=slice(None)):
    return _cols_f32(halo_ref, cols)[:SUBLANE]


CONV_CHAIN_COLS = 512


def _conv_fwd(proj, conv_w, conv_b, w_out, x, gate, g_next, scale_next, shift_next, name, after):
    _, s, e = proj.shape
    d = w_out.shape[1]
    ts, te = _tile(s, 256), e
    cw = min(CONV_CHAIN_COLS, te)
    hb = ts // PACK

    def body(bg_ref, cg_ref, xi_ref, z_ref, cgp_ref, xip_ref, w0_ref, w1_ref, w2_ref, b_ref, wo_ref, after_ref,
             x_ref, gate_ref, g_ref, sc_ref, sh_ref, y_ref, br_ref, x1_ref, h_ref):
        first = pl.program_id(0) == 0
        br = None
        for c0 in range(0, te, cw):
            cols = slice(c0, c0 + cw)
            cx = _cols_f32(cg_ref, cols) * _cols_f32(xi_ref, cols)
            before = jnp.where(first, 0.0, _rows_before(cgp_ref, cols) * _rows_before(xip_ref, cols))
            conv = b_ref[:, cols] + w2_ref[:, cols] * cx
            conv = conv + w0_ref[:, cols] * _shift_down(cx, before, 2)
            conv = conv + w1_ref[:, cols] * _shift_down(cx, before, 1)
            z = _cols_f32(z_ref, cols)
            y = ((z * _sigmoid(z)) * _cols_f32(bg_ref, cols) * conv).astype(BF16)
            y_ref[:, cols] = y
            part_br = jnp.dot(y, wo_ref[cols, :], preferred_element_type=F32)
            br = part_br if br is None else br + part_br
        br_ref[...] = br
        x1 = x_ref[...] + gate_ref[...] * br
        x1_ref[...] = x1
        xn, _ = _rms(x1)
        h_ref[...] = ((xn * g_ref[...]) * (1.0 + sc_ref[...]) + sh_ref[...]).astype(BF16)

    def part(q):
        return pl.BlockSpec((None, ts, te), lambda i: (q, i, 0))

    def halo_before(q):
        return pl.BlockSpec((None, PACK, te), lambda i: (q, jnp.maximum(i * hb - 1, 0), 0))

    return pl.pallas_call(
        body, name=name,
        out_shape=(jax.ShapeDtypeStruct((s, e), BF16), jax.ShapeDtypeStruct((s, d), F32),
                   jax.ShapeDtypeStruct((s, d), F32), jax.ShapeDtypeStruct((s, d), BF16)), grid=(s // ts,),
        in_specs=[part(0), part(1), part(2), part(3), halo_before(1), halo_before(2)]
        + [_vec_spec(e)] * 4 + [pl.BlockSpec((e, d), lambda i: (0, 0)), _after_spec(), _row_spec(ts, d)]
        + [_vec_spec(d)] * 4,
        out_specs=(_row_spec(ts, e), _row_spec(ts, d), _row_spec(ts, d), _row_spec(ts, d)),
        compiler_params=_params("parallel"),
    )(proj, proj, proj, proj, proj, proj, *conv_w, conv_b, w_out, after, x, gate, g_next, scale_next, shift_next)


def _conv_bwd(proj, dbr, w_out, conv_w, conv_b, name, after):
    _, s, e = proj.shape
    d = dbr.shape[1]
    ts, te = _tile(s, 512), _tile(e, 1024)
    cw = min(CONV_CHAIN_COLS // 2, te)
    hb = ts // PACK
    n_i = s // ts
    last_halo = s // PACK - 1
    nt = (((1,), (1,)), ((), ()))

    def body(bg_ref, cg_ref, xi_ref, z_ref, dbr_ref, cgp_ref, xip_ref, bgn_ref, zn_ref, dbrn_ref, wo_ref,
             w0_ref, w1_ref, w2_ref, b_ref, after_ref, dp_ref, dw0_ref, dw1_ref, dw2_ref, db_ref):
        i = pl.program_id(1)

        @pl.when(i == 0)
        def _():
            for acc in (dw0_ref, dw1_ref, dw2_ref, db_ref):
                acc[...] = jnp.zeros_like(acc)
        for c0 in range(0, te, cw):
            cols = slice(c0, c0 + cw)
            wo = wo_ref[cols, :]
            dy = lax.dot_general(dbr_ref[...], wo, nt, preferred_element_type=F32)
            dyn = lax.dot_general(dbrn_ref[...], wo, nt, preferred_element_type=F32)[:SUBLANE]
            bg, cg = _cols_f32(bg_ref, cols), _cols_f32(cg_ref, cols)
            xi, z = _cols_f32(xi_ref, cols), _cols_f32(z_ref, cols)
            w0, w1, w2 = w0_ref[:, cols], w1_ref[:, cols], w2_ref[:, cols]
            cx = cg * xi
            before = jnp.where(i > 0, _rows_before(cgp_ref, cols) * _rows_before(xip_ref, cols), 0.0)
            cx1 = _shift_down(cx, before, 1)
            cx2 = _shift_down(cx, before, 2)
            conv = b_ref[:, cols] + w2 * cx
            conv = conv + w0 * cx2
            conv = conv + w1 * cx1
            sz, dsz = _silu_and_grad(z)
            dp_ref[3, :, cols] = (dy * bg * conv * dsz).astype(BF16)
            dp_ref[0, :, cols] = (dy * sz * conv).astype(BF16)
            dconv = dy * sz * bg
            zn = _rows_after(zn_ref, cols)
            after = jnp.where(i < n_i - 1, dyn * (zn * _sigmoid(zn)) * _rows_after(bgn_ref, cols), 0.0)
            db_ref[:, cols] += _colsum(dconv)
            dw2_ref[:, cols] += _colsum(dconv * cx)
            dw1_ref[:, cols] += _colsum(dconv * cx1)
            dw0_ref[:, cols] += _colsum(dconv * cx2)
            dcx = w2 * dconv + w1 * _shift_up(dconv, after, 1) + w0 * _shift_up(dconv, after, 2)
            dp_ref[1, :, cols] = (dcx * xi).astype(BF16)
            dp_ref[2, :, cols] = (dcx * cg).astype(BF16)

    def part(q):
        return pl.BlockSpec((None, ts, te), lambda j, i: (q, i, j))

    def halo_before(q):
        return pl.BlockSpec((None, PACK, te), lambda j, i: (q, jnp.maximum(i * hb - 1, 0), j))

    def halo_after(q):
        return pl.BlockSpec((None, PACK, te), lambda j, i: (q, jnp.minimum((i + 1) * hb, last_halo), j))

    return pl.pallas_call(
        body, name=name,
        out_shape=(jax.ShapeDtypeStruct((4, s, e), BF16),) + (jax.ShapeDtypeStruct((1, e), F32),) * 4,
        grid=(e // te, n_i),
        in_specs=[part(0), part(1), part(2), part(3), pl.BlockSpec((ts, d), lambda j, i: (i, 0)),
                  halo_before(1), halo_before(2), halo_after(0), halo_after(3),
                  pl.BlockSpec((PACK, d), lambda j, i: (jnp.minimum((i + 1) * hb, last_halo), 0)),
                  pl.BlockSpec((te, d), lambda j, i: (j, 0))]
        + [pl.BlockSpec((1, te), lambda j, i: (0, j))] * 4 + [_after_spec()],
        out_specs=(pl.BlockSpec((4, ts, te), lambda j, i: (0, i, j)),) + (pl.BlockSpec((1, te), lambda j, i: (0, j)),) * 4,
        compiler_params=_params("parallel", "arbitrary"),
    )(proj, proj, proj, proj, dbr, proj, proj, proj, proj, dbr, w_out, *conv_w, conv_b, after)


def _tril(w):
    row = lax.broadcasted_iota(jnp.int32, w.shape, 0)
    col = lax.broadcasted_iota(jnp.int32, w.shape, 1)
    return jnp.where(row >= col, w, 0.0)


def _triu(w):
    row = lax.broadcasted_iota(jnp.int32, w.shape, 0)
    col = lax.broadcasted_iota(jnp.int32, w.shape, 1)
    return jnp.where(row <= col, w, 0.0)


def _layer_norm_fwd(v, g, b):
    mu = jnp.mean(v, axis=-1, keepdims=True)
    vc = v - mu
    rstd = lax.rsqrt(jnp.mean(vc * vc, axis=-1, keepdims=True) + LN_EPS)
    vhat = vc * rstd
    return vhat * g + b, vhat, rstd


GMLP_CHUNKS_PER_STEP = 2


def _gmlp_rows(s):
    return CHUNK * min(GMLP_CHUNKS_PER_STEP, s // CHUNK)


def _mix_positions(w_ref, src_scr, dst_scr, gw, mask, bias_ref=None):
    for gi in range(GROUPS):
        cols = slice(gi * gw, (gi + 1) * gw)
        wm = mask(w_ref[gi]).astype(BF16)
        for n in range(src_scr.shape[0] // CHUNK):
            rows = slice(n * CHUNK, (n + 1) * CHUNK)
            out = jnp.dot(wm, src_scr[rows, cols], preferred_element_type=F32)
            if bias_ref is not None:
                out = out + bias_ref[:, gi:gi + 1]
            dst_scr[rows, cols] = out


def _gmlp_fwd_head(proj, ln_g, ln_b, w_s, b_s_t, w_out, x1, gate, gf, target, name):
    _, s, e = proj.shape
    d = w_out.shape[1]
    gw = e // GROUPS
    ts = _gmlp_rows(s)

    def body(pu_ref, pv_ref, pz_ref, g_ref, b_ref, ws_ref, bs_ref, wo_ref, x1_ref, gate_ref, gf_ref, tg_ref,
             y_ref, dx_ref, dbr_ref, loss_ref, dgf_ref, dgate_ref, vn_scr, mix_scr):
        @pl.when(pl.program_id(0) == 0)
        def _():
            loss_ref[...] = jnp.zeros_like(loss_ref)
            dgf_ref[...] = jnp.zeros_like(dgf_ref)
            dgate_ref[...] = jnp.zeros_like(dgate_ref)
        vn, _, _ = _layer_norm_fwd(_gelu(_f32(pv_ref)), g_ref[...], b_ref[...])
        vn_scr[...] = vn.astype(BF16)
        _mix_positions(ws_ref, vn_scr, mix_scr, gw, _tril, bs_ref)
        z = _f32(pz_ref)
        y = ((z * _sigmoid(z)) * (_gelu(_f32(pu_ref)) * mix_scr[...])).astype(BF16)
        y_ref[...] = y
        br = jnp.dot(y, wo_ref[...], preferred_element_type=F32)
        dx, dbr, loss, dgf, dgate = _head_tile(x1_ref[...], br, gate_ref[...], gf_ref[...], tg_ref[...])
        dx_ref[...] = dx
        dbr_ref[...] = dbr.astype(BF16)
        loss_ref[...] += loss
        dgf_ref[...] += dgf
        dgate_ref[...] += dgate

    def part(q):
        return pl.BlockSpec((None, ts, e), lambda i: (q, i, 0))

    return pl.pallas_call(
        body, name=name,
        out_shape=(jax.ShapeDtypeStruct((s, e), BF16), jax.ShapeDtypeStruct((s, d), F32),
                   jax.ShapeDtypeStruct((s, d), BF16), jax.ShapeDtypeStruct((SUBLANE, 128), F32),
                   jax.ShapeDtypeStruct((1, d), F32), jax.ShapeDtypeStruct((1, d), F32)),
        grid=(s // ts,),
        in_specs=[part(0), part(1), part(2), _vec_spec(e), _vec_spec(e),
                  pl.BlockSpec((GROUPS, CHUNK, CHUNK), lambda i: (0, 0, 0)),
                  pl.BlockSpec((CHUNK, GROUPS), lambda i: (0, 0)), pl.BlockSpec((e, d), lambda i: (0, 0)),
                  _row_spec(ts, d), _vec_spec(d), _vec_spec(d), _row_spec(ts, d)],
        out_specs=(_row_spec(ts, e), _row_spec(ts, d), _row_spec(ts, d),
                   pl.BlockSpec((SUBLANE, 128), lambda i: (0, 0)), _vec_spec(d), _vec_spec(d)),
        scratch_shapes=[pltpu.VMEM((ts, e), BF16), pltpu.VMEM((ts, e), F32)],
        compiler_params=_params("arbitrary"),
    )(proj, proj, proj, ln_g, ln_b, w_s, b_s_t, w_out, x1, gate, gf, target)


def _gmlp_bwd(proj, dbr, w_out, ln_g, ln_b, w_s, w_s_t, b_s_t, name):
    _, s, e = proj.shape
    d = dbr.shape[1]
    gw = e // GROUPS
    ts = _gmlp_rows(s)
    n_i = s // ts

    def body(pu_ref, pv_ref, pz_ref, dbr_ref, wo_ref, g_ref, b_ref, ws_ref, wst_ref, bs_ref,
             dp_ref, dws_ref, dbs_ref, dlg_ref, dlb_ref, vn_scr, mix_scr, dm_scr, dvn_scr, dmacc_scr):
        i = pl.program_id(0)

        @pl.when(i == 0)
        def _():
            dws_ref[...] = jnp.zeros_like(dws_ref)
            dlg_ref[...] = jnp.zeros_like(dlg_ref)
            dlb_ref[...] = jnp.zeros_like(dlb_ref)
            dmacc_scr[...] = jnp.zeros_like(dmacc_scr)
        ln_g = g_ref[...]
        u, du_dpu = _gelu_and_grad(_f32(pu_ref))
        v, dv_dpv = _gelu_and_grad(_f32(pv_ref))
        vn, vhat, rstd = _layer_norm_fwd(v, ln_g, b_ref[...])
        vn_scr[...] = vn.astype(BF16)
        _mix_positions(ws_ref, vn_scr, mix_scr, gw, _tril, bs_ref)
        mixed = mix_scr[...]
        dy = lax.dot_general(dbr_ref[...], wo_ref[...], (((1,), (1,)), ((), ())), preferred_element_type=F32)
        sz, dsz = _silu_and_grad(_f32(pz_ref))
        ds = dy * sz
        dp_ref[2] = (dy * (u * mixed) * dsz).astype(BF16)
        dp_ref[0] = (ds * mixed * du_dpu).astype(BF16)
        dm = ds * u
        dm_scr[...] = dm.astype(BF16)
        for n in range(ts // CHUNK):
            rows = slice(n * CHUNK, (n + 1) * CHUNK)
            dmacc_scr[...] += dm[rows]
            for gi in range(GROUPS):
                cols = slice(gi * gw, (gi + 1) * gw)
                dws_ref[gi] += lax.dot_general(dm_scr[rows, cols], vn_scr[rows, cols], (((1,), (1,)), ((), ())),
                                               preferred_element_type=F32)
        _mix_positions(wst_ref, dm_scr, dvn_scr, gw, _triu)
        dvn = dvn_scr[...]
        dlg_ref[...] += _colsum(dvn * vhat)
        dlb_ref[...] += _colsum(dvn)
        dvh = dvn * ln_g
        dv = rstd * (dvh - jnp.mean(dvh, axis=-1, keepdims=True) - vhat * jnp.mean(dvh * vhat, axis=-1, keepdims=True))
        dp_ref[1] = (dv * dv_dpv).astype(BF16)

        @pl.when(i == n_i - 1)
        def _():
            for gi in range(GROUPS):
                dws_ref[gi] = _tril(dws_ref[gi])
                dbs_ref[:, gi:gi + 1] = jnp.sum(dmacc_scr[:, gi * gw:(gi + 1) * gw], axis=1, keepdims=True)

    def part(q):
        return pl.BlockSpec((None, ts, e), lambda i: (q, i, 0))

    w_spec = pl.BlockSpec((GROUPS, CHUNK, CHUNK), lambda i: (0, 0, 0))
    bs_spec = pl.BlockSpec((CHUNK, GROUPS), lambda i: (0, 0))
    return pl.pallas_call(
        body, name=name,
        out_shape=(jax.ShapeDtypeStruct((3, s, e), BF16), jax.ShapeDtypeStruct((GROUPS, CHUNK, CHUNK), F32),
                   jax.ShapeDtypeStruct((CHUNK, GROUPS), F32), jax.ShapeDtypeStruct((1, e), F32),
                   jax.ShapeDtypeStruct((1, e), F32)),
        grid=(n_i,),
        in_specs=[part(0), part(1), part(2), pl.BlockSpec((ts, d), lambda i: (i, 0)),
                  pl.BlockSpec((e, d), lambda i: (0, 0)), _vec_spec(e), _vec_spec(e), w_spec, w_spec, bs_spec],
        out_specs=(pl.BlockSpec((3, ts, e), lambda i: (0, i, 0)), w_spec, bs_spec, _vec_spec(e), _vec_spec(e)),
        scratch_shapes=[pltpu.VMEM((ts, e), BF16), pltpu.VMEM((ts, e), F32), pltpu.VMEM((ts, e), BF16),
                        pltpu.VMEM((ts, e), F32), pltpu.VMEM((CHUNK, e), F32)],
        compiler_params=_params("arbitrary"),
    )(proj, proj, proj, dbr, w_out, ln_g, ln_b, w_s, w_s_t, b_s_t)


def _mod_fwd(c_all, mod_w, mod_b_cols, name):
    n_layer, d, w = mod_w.shape

    def body(c_ref, w_ref, b_ref, ca_ref, o_ref):
        c = c_ref[...]
        ca = c * _sigmoid(c)
        ca_ref[...] = ca
        for li in range(n_layer):
            o_ref[li * N_DEV:(li + 1) * N_DEV, :] = (
                jnp.dot(ca, w_ref[li], preferred_element_type=F32, precision=lax.Precision.HIGHEST) + b_ref[li])

    return pl.pallas_call(
        body, name=name,
        out_shape=(jax.ShapeDtypeStruct((N_DEV, d), F32), jax.ShapeDtypeStruct((n_layer * N_DEV, w), F32)),
        in_specs=[VMEM_SPEC] * 3, out_specs=(VMEM_SPEC, VMEM_SPEC),
        compiler_params=pltpu.CompilerParams(vmem_limit_bytes=VMEM_LIMIT),
    )(c_all, mod_w, mod_b_cols)


def _adamw(w, g, m, v):
    m = ADAM_B1 * m + (1.0 - ADAM_B1) * g
    v = ADAM_B2 * v + (1.0 - ADAM_B2) * (g * g)
    m_hat = m / (1.0 - ADAM_B1 ** ADAM_STEP)
    v_hat = v / (1.0 - ADAM_B2 ** ADAM_STEP)
    delta = -ADAM_LR * (m_hat / (jnp.sqrt(v_hat) + ADAM_EPS) + ADAM_WD * w)
    return delta, m, v


def _adamw_2d(w, g, m, v, name):
    r, c = w.shape
    tr, tc = _tile(r, 512), _tile(c, 1024)

    def body(w_ref, g_ref, m_ref, v_ref, go_ref, d_ref, nm_ref, nv_ref):
        g = g_ref[...]
        go_ref[...] = g
        d_ref[...], nm_ref[...], nv_ref[...] = _adamw(w_ref[...], g, m_ref[...], v_ref[...])

    spec = pl.BlockSpec((tr, tc), lambda i, j: (i, j))
    shape = jax.ShapeDtypeStruct((r, c), F32)
    return pl.pallas_call(
        body, name=name, out_shape=(shape,) * 4, grid=(r // tr, c // tc),
        in_specs=[spec] * 4, out_specs=(spec,) * 4, compiler_params=_params("parallel", "parallel"),
    )(w, g, m, v)


def _mod_w_update(ca_t, dmod_cols, w, m, v, name):
    n_layer, d, wd = w.shape
    tr = _tile(d, 256)

    def body(ca_ref, dm_ref, w_ref, m_ref, v_ref, g_ref, d_ref, nm_ref, nv_ref):
        ca = ca_ref[...]
        dm = dm_ref[...]
        g = ca[:, 0:1] * dm[0:1, :]
        for b in range(1, N_DEV):
            g = g + ca[:, b:b + 1] * dm[b:b + 1, :]
        g_ref[...] = g
        d_ref[...], nm_ref[...], nv_ref[...] = _adamw(w_ref[...], g, m_ref[...], v_ref[...])

    spec = pl.BlockSpec((None, tr, wd), lambda l, i: (l, i, 0))
    shape = jax.ShapeDtypeStruct((n_layer, d, wd), F32)
    return pl.pallas_call(
        body, name=name, out_shape=(shape,) * 4, grid=(n_layer, d // tr),
        in_specs=[pl.BlockSpec((tr, N_DEV), lambda l, i: (i, 0)), pl.BlockSpec((None, N_DEV, wd), lambda l, i: (l, 0, 0)),
                  spec, spec, spec],
        out_specs=(spec,) * 4, compiler_params=_params("parallel", "parallel"),
    )(ca_t, dmod_cols, w, m, v)


def _adamw_small(ws, gs, ms, vs, name):
    n = len(ws)

    def body(*refs):
        ins, outs = refs[:4 * n], refs[4 * n:]
        for k in range(n):
            delta, nm, nv = _adamw(ins[k][...], ins[n + k][...], ins[2 * n + k][...], ins[3 * n + k][...])
            outs[3 * k][...] = delta
            outs[3 * k + 1][...] = nm
            outs[3 * k + 2][...] = nv

    out_shape = []
    for w in ws:
        out_shape += [jax.ShapeDtypeStruct(w.shape, F32)] * 3
    outs = pl.pallas_call(
        body, name=name, out_shape=tuple(out_shape),
        in_specs=[VMEM_SPEC] * (4 * n), out_specs=tuple([VMEM_SPEC] * (3 * n)),
        compiler_params=pltpu.CompilerParams(vmem_limit_bytes=VMEM_LIMIT),
    )(*ws, *gs, *ms, *vs)
    return [tuple(outs[3 * k:3 * k + 3]) for k in range(n)]


def _place():
    return lax.axis_index("x"), lax.axis_index("y"), lax.axis_index("c")


def _other_chips(x, y):
    return [(1 - x, y), (x, 1 - y), (1 - x, 1 - y)]


def _all_gather(block, name, after=None):
    m_per, n = block.shape

    def body(x_ref, *rest):
        out_ref, send_sems, recv_sems, local_sem = rest[-4:]
        x, y, c = _place()
        me, sibling = (x, y, c), (x, y, 1 - c)
        chips = _other_chips(x, y)

        def rows(px, py, pc):
            return out_ref.at[pl.ds((4 * px + 2 * py + pc) * m_per, m_per), :]

        def copy(k, blk, to, src=None):
            return pltpu.make_async_remote_copy(
                src_ref=rows(*blk) if src is None else src, dst_ref=rows(*blk),
                send_sem=send_sems.at[k], recv_sem=recv_sems.at[k], device_id=to, device_id_type=MESH)

        mine = pltpu.make_async_copy(x_ref, rows(*me), local_sem)
        mine.start()
        first = [copy(0, me, sibling, src=x_ref)]
        first += [copy(1 + j, me, (*chip, c), src=x_ref) for j, chip in enumerate(chips)]
        for cp in first:
            cp.start()
        passed = [copy(4 + j, (*chip, c), sibling) for j, chip in enumerate(chips)]
        for j, chip in enumerate(chips):
            copy(1 + j, (*chip, c), me).wait_recv()
            passed[j].start()
        copy(0, sibling, me).wait_recv()
        for j, chip in enumerate(chips):
            copy(4 + j, (*chip, 1 - c), me).wait_recv()
        for cp in first + passed:
            cp.wait_send()
        mine.wait()

    return pl.pallas_call(
        body, name=name, out_shape=jax.ShapeDtypeStruct((N_DEV * m_per, n), F32),
        in_specs=[VMEM_SPEC] + ([] if after is None else [ANY_SPEC]), out_specs=VMEM_SPEC,
        scratch_shapes=[pltpu.SemaphoreType.DMA((7,)), pltpu.SemaphoreType.DMA((7,)), pltpu.SemaphoreType.DMA],
        compiler_params=pltpu.CompilerParams(vmem_limit_bytes=VMEM_LIMIT),
    )(*([block] if after is None else [block, after]))


def _hbm(a):
    return pltpu.with_memory_space_constraint(a, pltpu.HBM)


def _all_gather_start(blocks, name):
    def body(in_ref, send_sems, recv_sems, thru_ref, token):
        token[...] = jnp.zeros_like(token)
        x, y, c = _place()
        mine = in_ref.at[4 * x + 2 * y + c]
        pltpu.make_async_remote_copy(
            src_ref=mine, dst_ref=mine, send_sem=send_sems.at[0], recv_sem=recv_sems.at[0],
            device_id=(x, y, 1 - c), device_id_type=MESH).start()
        for j, chip in enumerate(_other_chips(x, y)):
            for cc in range(2):
                pltpu.make_async_remote_copy(
                    src_ref=mine, dst_ref=mine, send_sem=send_sems.at[1 + 2 * j + cc],
                    recv_sem=recv_sems.at[1 + 2 * j + c], device_id=(*chip, cc), device_id_type=MESH).start()

    sems = pltpu.SemaphoreType.DMA((N_DEV - 1,))
    return pl.pallas_call(
        body, name=name,
        out_shape=(sems, sems, pltpu.HBM(blocks.shape, blocks.dtype), jax.ShapeDtypeStruct((SUBLANE, 128), F32)),
        in_specs=[HBM_SPEC], out_specs=(SEM_SPEC, SEM_SPEC, HBM_SPEC, VMEM_SPEC), input_output_aliases={0: 2},
        compiler_params=pltpu.CompilerParams(has_side_effects=EFFECT),
    )(_hbm(blocks))


def _all_gather_wait(send_sems, recv_sems, blocks, after, name):
    def body(in_ref, send_sems, recv_sems, after_ref, out_ref):
        x, y, c = _place()
        one = in_ref.at[0]
        for k in range(N_DEV - 1):
            pltpu.make_async_remote_copy(
                src_ref=one, dst_ref=one, send_sem=send_sems.at[k], recv_sem=recv_sems.at[k],
                device_id=(x, y, 1 - c), device_id_type=MESH).wait_send()
            pltpu.make_async_remote_copy(
                src_ref=one, dst_ref=one, send_sem=send_sems.at[k], recv_sem=recv_sems.at[k],
                device_id=(x, y, 1 - c), device_id_type=MESH).wait_recv()

    return pl.pallas_call(
        body, name=name, out_shape=pltpu.HBM(blocks.shape, blocks.dtype),
        in_specs=[HBM_SPEC, SEM_SPEC, SEM_SPEC, ANY_SPEC], out_specs=HBM_SPEC, input_output_aliases={0: 0},
        compiler_params=pltpu.CompilerParams(has_side_effects=EFFECT),
    )(blocks, send_sems, recv_sems, after)


def _place_own(shard, kind, chip_idx, name):
    r, cdim = shard.shape
    tr, tc = _tile(r, 512), _tile(cdim, 1024)
    nrb, ncb = r // tr, cdim // tc

    def body(k_ref, s_ref, o_ref):
        o_ref[...] = s_ref[...].astype(BF16)

    if kind == "col":
        full, o_map = (r, N_CHIP * cdim), lambda i, j, kr: (i, kr[0] * ncb + j)
    else:
        full, o_map = (N_CHIP * r, cdim), lambda i, j, kr: (kr[0] * nrb + i, j)
    return pl.pallas_call(
        body, name=name, out_shape=jax.ShapeDtypeStruct(full, BF16),
        grid_spec=pltpu.PrefetchScalarGridSpec(
            num_scalar_prefetch=1, grid=(nrb, ncb),
            in_specs=[pl.BlockSpec((tr, tc), lambda i, j, kr: (i, j))],
            out_specs=pl.BlockSpec((tr, tc), o_map)),
        compiler_params=_params("parallel", "parallel"),
    )(chip_idx, shard)


def _weight_window(ref, kind, shard_shape, k, half):
    r, cdim = shard_shape
    hr = r // 2
    if kind == "col":
        return ref.at[pl.ds(half * hr, hr), pl.ds(pl.multiple_of(k * cdim, 128), cdim)]
    return ref.at[pl.ds(pl.multiple_of(k * r + half * hr, 2 * SUBLANE), hr), :]


def _target_cores(c, both_cores):
    return [(cc, cc, c) for cc in range(2)] if both_cores else [(c, 0, 0)]


def _gather_weights_start(fulls, kinds, shard_shapes, groups, after, name, both_cores=True):
    n, ng = len(fulls), len(groups)
    per = 2 if both_cores else 1

    def body(*refs):
        ins = refs[:n]
        sems = refs[n + 1:n + 1 + 2 * ng]
        token = refs[2 * n + 1 + 2 * ng]
        x, y, c = _place()
        chips = _other_chips(x, y)
        for g, group in enumerate(groups):
            for pos, (w, j) in enumerate(group):
                own = _weight_window(ins[w], kinds[w], shard_shapes[w], 2 * x + y, c)
                for cc, mine, theirs in _target_cores(c, both_cores):
                    pltpu.make_async_remote_copy(
                        src_ref=own, dst_ref=own,
                        send_sem=sems[2 * g].at[per * pos + mine], recv_sem=sems[2 * g + 1].at[per * pos + theirs],
                        device_id=(*chips[j], cc), device_id_type=MESH).start()
        token[...] = jnp.zeros_like(token)

    sem_shapes = []
    for group in groups:
        sem_shapes += [pltpu.SemaphoreType.DMA((per * len(group),))] * 2
    outs = pl.pallas_call(
        body, name=name,
        out_shape=tuple(sem_shapes) + tuple(pltpu.HBM(f.shape, f.dtype) for f in fulls)
        + (jax.ShapeDtypeStruct((SUBLANE, 128), F32),),
        in_specs=[HBM_SPEC] * n + [ANY_SPEC], out_specs=(SEM_SPEC,) * (2 * ng) + (HBM_SPEC,) * n + (VMEM_SPEC,),
        input_output_aliases={w: 2 * ng + w for w in range(n)},
        compiler_params=pltpu.CompilerParams(has_side_effects=EFFECT),
    )(*[_hbm(f) for f in fulls], after)
    sems = [(outs[2 * g], outs[2 * g + 1]) for g in range(ng)]
    return sems, list(outs[2 * ng:2 * ng + n]), outs[2 * ng + n]


def _gather_weights_wait(sems, group, fulls, kinds, shard_shapes, after, name, both_cores=True):
    n = len(fulls)
    per = 2 if both_cores else 1

    def body(*refs):
        ins = refs[:n]
        send_sems, recv_sems = refs[n], refs[n + 1]
        x, y, c = _place()
        chips = _other_chips(x, y)
        for pos, (w, j) in enumerate(group):
            own = _weight_window(ins[w], kinds[w], shard_shapes[w], 2 * x + y, c)
            for cc, mine, _ in _target_cores(c, both_cores):
                landed = _weight_window(ins[w], kinds[w], shard_shapes[w], 2 * chips[j][0] + chips[j][1], cc)
                k = per * pos + mine
                pltpu.make_async_remote_copy(
                    src_ref=own, dst_ref=own, send_sem=send_sems.at[k], recv_sem=recv_sems.at[k],
                    device_id=(*chips[j], cc), device_id_type=MESH).wait_send()
                pltpu.make_async_remote_copy(
                    src_ref=landed, dst_ref=landed, send_sem=send_sems.at[k], recv_sem=recv_sems.at[k],
                    device_id=(*chips[j], cc), device_id_type=MESH).wait_recv()

    return list(pl.pallas_call(
        body, name=name, out_shape=tuple(pltpu.HBM(f.shape, f.dtype) for f in fulls),
        in_specs=[HBM_SPEC] * n + [SEM_SPEC, SEM_SPEC, ANY_SPEC], out_specs=(HBM_SPEC,) * n,
        input_output_aliases={w: w for w in range(n)},
        compiler_params=pltpu.CompilerParams(has_side_effects=EFFECT),
    )(*fulls, sems[0], sems[1], after))


def _forward_to_sibling(full, kind, shard_shape, relations, name):
    nr = len(relations)

    def body(in_ref, out_ref, send_sems, recv_sems):
        x, y, c = _place()
        chips = _other_chips(x, y)
        copies = []
        for pos, j in enumerate(relations):
            k = 2 * chips[j][0] + chips[j][1]
            mine = _weight_window(in_ref, kind, shard_shape, k, c)
            cp = pltpu.make_async_remote_copy(
                src_ref=mine, dst_ref=_weight_window(out_ref, kind, shard_shape, k, c),
                send_sem=send_sems.at[pos], recv_sem=recv_sems.at[pos], device_id=(x, y, 1 - c), device_id_type=MESH)
            cp.start()
            copies.append(cp)
        for pos, j in enumerate(relations):
            theirs = _weight_window(out_ref, kind, shard_shape, 2 * chips[j][0] + chips[j][1], 1 - c)
            pltpu.make_async_remote_copy(
                src_ref=theirs, dst_ref=theirs, send_sem=send_sems.at[pos], recv_sem=recv_sems.at[pos],
                device_id=(x, y, 1 - c), device_id_type=MESH).wait_recv()
        for cp in copies:
            cp.wait_send()

    return pl.pallas_call(
        body, name=name, out_shape=jax.ShapeDtypeStruct(full.shape, full.dtype),
        in_specs=[HBM_SPEC], out_specs=HBM_SPEC, input_output_aliases={0: 0},
        scratch_shapes=[pltpu.SemaphoreType.DMA((nr,))] * 2,
    )(full)


def _grad_piece(ref, kind, h, cdim, k, half):
    if kind == "col":
        return ref.at[pl.ds(half * h, h), pl.ds(pl.multiple_of(k * cdim, 128), cdim)]
    return ref.at[k, pl.ds(half * h, h), :]


def _grad_dims(g, kind):
    return (g.shape[0] // 2, g.shape[1] // N_CHIP) if kind == "col" else (g.shape[1] // 2, g.shape[2])


def _reduce_start(grads, kinds, name):
    n = len(grads)
    dims = [_grad_dims(g, kind) for g, kind in zip(grads, kinds)]
    lands = [lax.empty((N_DEV - 1, h, cdim), g.dtype) for g, (h, cdim) in zip(grads, dims)]

    def body(*refs):
        g_ins, land_ins = refs[:n], refs[n:2 * n]
        send_sems, recv_sems = refs[2 * n], refs[2 * n + 1]
        token = refs[4 * n + 2]
        x, y, c = _place()
        for w in range(n):
            h, cdim = dims[w]
            base = (N_DEV - 1) * w
            pltpu.make_async_remote_copy(
                src_ref=_grad_piece(g_ins[w], kinds[w], h, cdim, 2 * x + y, 1 - c), dst_ref=land_ins[w].at[0],
                send_sem=send_sems.at[base], recv_sem=recv_sems.at[base],
                device_id=(x, y, 1 - c), device_id_type=MESH).start()
            for j, chip in enumerate(_other_chips(x, y)):
                for cc in range(2):
                    pltpu.make_async_remote_copy(
                        src_ref=_grad_piece(g_ins[w], kinds[w], h, cdim, 2 * chip[0] + chip[1], cc),
                        dst_ref=land_ins[w].at[1 + 2 * j + c],
                        send_sem=send_sems.at[base + 1 + 2 * j + cc], recv_sem=recv_sems.at[base + 1 + 2 * j + c],
                        device_id=(*chip, cc), device_id_type=MESH).start()
        token[...] = jnp.zeros_like(token)

    sems = pltpu.SemaphoreType.DMA(((N_DEV - 1) * n,))
    outs = pl.pallas_call(
        body, name=name,
        out_shape=(sems, sems) + tuple(pltpu.HBM(a.shape, a.dtype) for a in list(grads) + lands)
        + (jax.ShapeDtypeStruct((SUBLANE, 128), F32),),
        in_specs=[HBM_SPEC] * (2 * n), out_specs=(SEM_SPEC, SEM_SPEC) + (HBM_SPEC,) * (2 * n) + (VMEM_SPEC,),
        input_output_aliases={i: 2 + i for i in range(2 * n)},
        compiler_params=pltpu.CompilerParams(has_side_effects=EFFECT),
    )(*[_hbm(a) for a in list(grads) + lands])
    return outs[0], outs[1], list(outs[2:2 + n]), list(outs[2 + n:2 + 2 * n]), outs[2 + 2 * n]


def _reduce_wait(send_sems, recv_sems, grads, lands, kinds, after, name):
    n = len(grads)
    dims = [_grad_dims(g, kind) for g, kind in zip(grads, kinds)]

    def body(*refs):
        g_ins, land_ins = refs[:n], refs[n:2 * n]
        send_sems, recv_sems = refs[2 * n], refs[2 * n + 1]
        x, y, c = _place()
        for w in range(n):
            h, cdim = dims[w]
            piece = _grad_piece(g_ins[w], kinds[w], h, cdim, 2 * x + y, c)
            for s in range(N_DEV - 1):
                k = (N_DEV - 1) * w + s
                slot = land_ins[w].at[s]
                pltpu.make_async_remote_copy(
                    src_ref=piece, dst_ref=slot, send_sem=send_sems.at[k], recv_sem=recv_sems.at[k],
                    device_id=(x, y, 1 - c), device_id_type=MESH).wait_send()
                pltpu.make_async_remote_copy(
                    src_ref=piece, dst_ref=slot, send_sem=send_sems.at[k], recv_sem=recv_sems.at[k],
                    device_id=(x, y, 1 - c), device_id_type=MESH).wait_recv()

    outs = pl.pallas_call(
        body, name=name, out_shape=tuple(pltpu.HBM(a.shape, a.dtype) for a in list(grads) + list(lands)),
        in_specs=[HBM_SPEC] * (2 * n) + [SEM_SPEC, SEM_SPEC, ANY_SPEC], out_specs=(HBM_SPEC,) * (2 * n),
        input_output_aliases={i: i for i in range(2 * n)},
        compiler_params=pltpu.CompilerParams(has_side_effects=EFFECT),
    )(*grads, *lands, send_sems, recv_sems, after)
    return list(outs[:n]), list(outs[n:])


def _add_pieces(g, land, kind, chip_idx, core_idx, name):
    _, h, cdim = land.shape
    tr, tc = _tile(h, 256), _tile(cdim, 2048)
    nrb, ncb = h // tr, cdim // tc

    def body(k_ref, c_ref, g_ref, l_ref, o_ref):
        acc = g_ref[...].astype(F32)
        for s in range(N_DEV - 1):
            acc = acc + l_ref[s].astype(F32)
        o_ref[...] = acc

    if kind == "col":
        g_spec = pl.BlockSpec((tr, tc), lambda i, j, kr, cr: (cr[0] * nrb + i, kr[0] * ncb + j))
    else:
        g_spec = pl.BlockSpec((None, tr, tc), lambda i, j, kr, cr: (kr[0], cr[0] * nrb + i, j))
    return pl.pallas_call(
        body, name=name, out_shape=jax.ShapeDtypeStruct((2 * h, cdim), F32),
        grid_spec=pltpu.PrefetchScalarGridSpec(
            num_scalar_prefetch=2, grid=(nrb, ncb),
            in_specs=[g_spec, pl.BlockSpec((N_DEV - 1, tr, tc), lambda i, j, kr, cr: (0, i, j))],
            out_specs=pl.BlockSpec((tr, tc), lambda i, j, kr, cr: (cr[0] * nrb + i, j))),
        compiler_params=_params("parallel", "parallel"),
    )(chip_idx, core_idx, g, land)


def _join_halves(shards, name):
    n = len(shards)

    def body(*refs):
        ins, outs = refs[:n], refs[n:2 * n]
        send_sems, recv_sems = refs[2 * n:]
        x, y, c = _place()
        copies = []
        for w in range(n):
            h = shards[w].shape[0] // 2
            cp = pltpu.make_async_remote_copy(
                src_ref=ins[w].at[pl.ds(c * h, h), :], dst_ref=outs[w].at[pl.ds(c * h, h), :],
                send_sem=send_sems.at[w], recv_sem=recv_sems.at[w], device_id=(x, y, 1 - c), device_id_type=MESH)
            cp.start()
            copies.append(cp)
        for w in range(n):
            h = shards[w].shape[0] // 2
            theirs = outs[w].at[pl.ds((1 - c) * h, h), :]
            pltpu.make_async_remote_copy(
                src_ref=theirs, dst_ref=theirs, send_sem=send_sems.at[w], recv_sem=recv_sems.at[w],
                device_id=(x, y, 1 - c), device_id_type=MESH).wait_recv()
        for cp in copies:
            cp.wait_send()

    return pl.pallas_call(
        body, name=name, out_shape=tuple(jax.ShapeDtypeStruct(s.shape, s.dtype) for s in shards),
        in_specs=[HBM_SPEC] * n, out_specs=tuple([HBM_SPEC] * n),
        input_output_aliases={w: w for w in range(n)},
        scratch_shapes=[pltpu.SemaphoreType.DMA((n,))] * 2,
    )(*shards)


def _sum_devices(gathered, rows, name):
    n = gathered.shape[1]

    def body(g_ref, o_ref):
        acc = g_ref[0:rows, :]
        for dev in range(1, N_DEV):
            acc = acc + g_ref[dev * rows:(dev + 1) * rows, :]
        o_ref[...] = acc

    return pl.pallas_call(
        body, name=name, out_shape=jax.ShapeDtypeStruct((rows, n), F32),
        in_specs=[VMEM_SPEC], out_specs=VMEM_SPEC,
        compiler_params=pltpu.CompilerParams(vmem_limit_bytes=VMEM_LIMIT),
    )(gathered)


def _pack(vectors, width):
    flat = [v.reshape(-1) for v in vectors]
    offsets, total = [], 0
    for f in flat:
        offsets.append(total)
        total += f.shape[0]
    rows = -(-total // (width * SUBLANE)) * SUBLANE
    flat.append(jnp.zeros((rows * width - total,), F32))
    return jnp.concatenate(flat).reshape(rows, width), offsets


def kernel(x, c, mod_w, mod_b, norm_g, a_w_in, a_conv_w, a_conv_b, a_w_out, b_w_in, b_ln_g, b_ln_b, b_w_s, b_b_s, b_w_out, final_g, loss_target, m_mod_w, m_mod_b, m_norm_g, m_a_w_in, m_a_conv_w, m_a_conv_b, m_a_w_out, m_b_w_in, m_b_ln_g, m_b_ln_b, m_b_w_s, m_b_b_s, m_b_w_out, m_final_g, v_mod_w, v_mod_b, v_norm_g, v_a_w_in, v_a_conv_w, v_a_conv_b, v_a_w_out, v_b_w_in, v_b_ln_g, v_b_ln_b, v_b_w_s, v_b_b_s, v_b_w_out, v_final_g):
    seq, d = x.shape[1], x.shape[2]
    e = a_conv_b.shape[1]
    wd = mod_w.shape[2]
    ax, ay, ac = _place()
    chip = 2 * ax + ay
    dev = 2 * chip + ac
    chip_idx = jnp.reshape(chip, (1,)).astype(jnp.int32)
    core_idx = jnp.reshape(ac, (1,)).astype(jnp.int32)

    x2d = x[0]
    target = loss_target[0]

    w_names = ["a_w_in", "a_w_out", "b_w_in", "b_w_out"]
    w_kinds = ["col", "row", "col", "row"]
    w_shards = [a_w_in[0], a_w_out[0], b_w_in[0], b_w_out[0]]
    w_shapes = [sh.shape for sh in w_shards]
    placed = [_place_own(sh, kind, chip_idx, "place_" + nm) for nm, sh, kind in zip(w_names, w_shards, w_kinds)]
    a_groups, far_groups = [[(0, 0), (0, 1)]], [[(0, 2)]]
    whole_group = [[(0, 0), (0, 1), (0, 2)]]

    es = e // N_CHIP
    packed0, offs0 = _pack([c, a_conv_w, b_ln_g, b_ln_b], 1024)
    gathered0 = _all_gather(packed0, "gather_params").reshape(N_DEV, -1)
    c_all = gathered0[:, :d]
    per_chip = gathered0[0::2]

    def from_chips_cols(k, rows_):
        got = per_chip[:, offs0[k]:offs0[k] + rows_ * es].reshape(N_CHIP, rows_, es)
        return jnp.transpose(got, (1, 0, 2)).reshape(rows_, e)

    conv_w_full = from_chips_cols(1, 3)
    conv_w = [conv_w_full[k:k + 1] for k in range(3)]
    ln_g, ln_b = from_chips_cols(2, 1), from_chips_cols(3, 1)
    mod_b_cols = lax.dynamic_slice_in_dim(mod_b, chip * wd, wd, axis=1)[:, None, :]
    c_act, mod_part = _mod_fwd(c_all, mod_w, mod_b_cols, "mod_fwd")
    n_layer = mod_w.shape[0]
    mod_gathered = _all_gather(mod_part, "gather_mod")
    a_sems, (wa_in,), a_token = _gather_weights_start(
        placed[:1], w_kinds[:1], w_shapes[:1], a_groups, mod_gathered, "gather_a_w_in_start", both_cores=False)
    mod_all = mod_gathered.reshape(N_CHIP, 2, n_layer, N_DEV, wd)[:, 0]
    mod_all = jnp.transpose(mod_all, (1, 2, 0, 3)).reshape(n_layer, N_DEV, N_CHIP * wd)
    mod_me = lax.dynamic_index_in_dim(mod_all, dev, axis=1, keepdims=False)
    shift = [mod_me[l:l + 1, 0:d] for l in range(n_layer)]
    scale = [mod_me[l:l + 1, d:2 * d] for l in range(n_layer)]
    gate = [mod_me[l:l + 1, 2 * d:3 * d] for l in range(n_layer)]

    g0, g1, gf = norm_g[0:1], norm_g[1:2], final_g[None, :]
    h0 = _norm_mod(x2d, g0, scale[0], shift[0], "norm_mod0")

    def slab(r):
        return jnp.bitwise_xor(chip_idx, r)

    def arrived(sems, group, weight, w, after, name, both_cores=True):
        return _gather_weights_wait(sems, [(0, j) for _, j in group], [weight], [w_kinds[w]], [w_shapes[w]],
                                    after, name, both_cores)[0]

    proj0 = _mm_proj_slab(h0, wa_in, None, slab(0), 4, "a_proj_own", a_token)
    wa_in = arrived(a_sems[0], a_groups[0], wa_in, 0, proj0, "gather_wait_near", both_cores=False)
    wa_in = _forward_to_sibling(wa_in, w_kinds[0], w_shapes[0], [0, 1], "forward_near")
    far_sems, (wa_in,), far_token = _gather_weights_start(
        [wa_in], w_kinds[:1], w_shapes[:1], far_groups, a_token, "gather_a_w_in_far_start")
    proj0 = _mm_proj_slab(h0, wa_in, proj0, slab(2), 4, "a_proj_x", far_token)
    proj0 = _mm_proj_slab(h0, wa_in, proj0, slab(1), 4, "a_proj_y", far_token)
    wa_in = arrived(far_sems[0], far_groups[0], wa_in, 0, proj0, "gather_wait_far")
    def start_whole(w, after, name):
        sems, (flight,), token = _gather_weights_start(
            placed[w:w + 1], w_kinds[w:w + 1], w_shapes[w:w + 1], whole_group, after, name)
        return sems[0], flight, token

    ao_sems, wa_out, ao_token = start_whole(1, wa_in, "gather_a_w_out_start")
    proj0 = _mm_proj_slab(h0, wa_in, proj0, slab(3), 4, "a_proj_far", ao_token)
    wa_out = arrived(ao_sems, whole_group[0], wa_out, 1, proj0, "gather_wait_a_w_out")
    bi_sems, wb_in, bi_token = start_whole(2, wa_out, "gather_b_w_in_start")
    y0, br0, x1, h1 = _conv_fwd(proj0, conv_w, a_conv_b, wa_out, x2d, gate[0], g1, scale[1], shift[1],
                                "conv_fwd", bi_token)
    wb_in = arrived(bi_sems, whole_group[0], wb_in, 2, h1, "gather_wait_b_w_in")
    bo_sems, wb_out, bo_token = start_whole(3, wb_in, "gather_b_w_out_start")
    proj1 = _mm_proj(h1, wb_in, 3, "b_proj", bo_token)
    b_s_t = jnp.transpose(b_b_s[0])
    wb_out = arrived(bo_sems, whole_group[0], wb_out, 3, proj1, "gather_wait_b_w_out")
    y1, dx2, dbr1, loss_part, g_final_g, dgate1 = _gmlp_fwd_head(
        proj1, ln_g, ln_b, b_w_s[0], b_s_t, wb_out, x1, gate[1], gf, target, "gmlp_fwd_head")

    gw_b_out = _mm_dw_out(y1, dbr1, "b_out_dw")
    dproj1, g_w_s, g_b_s_t, g_ln_g, g_ln_b = _gmlp_bwd(
        proj1, dbr1, wb_out, ln_g, ln_b, b_w_s[0], jnp.swapaxes(b_w_s[0], 1, 2), b_s_t, "gmlp_bwd")
    gw_b_in = _mm_dw_in(h1, dproj1, "b_proj_dw")
    b_kinds = ["col", "row"]
    b_send, b_recv, b_grads, b_lands, b_token = _reduce_start(
        [gw_b_in, gw_b_out.reshape(N_CHIP, e // N_CHIP, d)], b_kinds, "reduce_b_start")
    dx1, dshift1, dscale1, g_g1, dbr0, dgate0 = _mm_dh_norm_bwd(
        dproj1, wb_in, x1, dx2, g1, scale[1], "b_proj_dx", b_token, br=br0, gate=gate[0])

    early = [loss_part[0, 0:1], g_final_g, g_g1, g_ln_g, g_ln_b, jnp.transpose(g_b_s_t), g_w_s,
             jnp.concatenate([dshift1, dscale1, dgate1, dgate0], axis=1)]
    packed_e, offs_e = _pack(early, 1024)
    rows_e = packed_e.shape[0]
    blocks_e = lax.dynamic_update_slice(jnp.zeros((N_DEV, rows_e, 1024), F32), packed_e[None], (dev, 0, 0))
    e_send, e_recv, blocks_e, e_token = _all_gather_start(blocks_e, "gather_small_start")

    gw_a_out = _mm_dw_out(y0, dbr0, "a_out_dw")
    ao_send, ao_recv, ao_grads, ao_lands, ao_grad_token = _reduce_start(
        [gw_a_out.reshape(N_CHIP, e // N_CHIP, d)], ["row"], "reduce_a_out_start")
    dproj0, g_w0, g_w1, g_w2, g_conv_b = _conv_bwd(
        proj0, dbr0, wa_out, conv_w, a_conv_b, "conv_bwd", ao_grad_token + e_token)
    gw_a_in = _mm_dw_in(h0, dproj0, "a_proj_dw")
    ai_send, ai_recv, ai_grads, ai_lands, ai_token = _reduce_start([gw_a_in], ["col"], "reduce_a_in_start")
    grad_x, dshift0, dscale0, g_g0 = _mm_dh_norm_bwd(dproj0, wa_in, x2d, dx1, g0, scale[0], "a_proj_dx", ai_token)

    def finish(send, recv, grads_, lands_, kinds_, names_, after, tag):
        grads_, lands_ = _reduce_wait(send, recv, grads_, lands_, kinds_, after, "reduce_" + tag + "_wait")
        halves = [_add_pieces(g, land, kind, chip_idx, core_idx, "add_pieces_" + nm)
                  for g, land, kind, nm in zip(grads_, lands_, kinds_, names_)]
        return _join_halves(halves, "join_" + tag)

    upd, big_grads = {}, {}

    def adamw_big(nm, w, g, m, v):
        g_out, *rest = _adamw_2d(w[0], g, m[0], v[0], "adamw_" + nm)
        big_grads[nm] = g_out[None]
        upd[nm] = tuple(o[None] for o in rest)

    g_b_w_in, g_b_w_out = finish(b_send, b_recv, b_grads, b_lands, b_kinds, ["b_w_in", "b_w_out"], grad_x, "b")
    adamw_big("b_w_in", b_w_in, g_b_w_in, m_b_w_in, v_b_w_in)
    adamw_big("b_w_out", b_w_out, g_b_w_out, m_b_w_out, v_b_w_out)

    late = [g_g0, jnp.concatenate([g_w0, g_w1, g_w2], axis=0), g_conv_b, jnp.concatenate([dshift0, dscale0], axis=1)]
    packed_l, offs_l = _pack(late, 1024)
    rows_l = packed_l.shape[0]
    gathered_l = _all_gather(packed_l, "gather_small_late", after=upd["b_w_out"][0])
    blocks_e = _all_gather_wait(e_send, e_recv, blocks_e, gathered_l, "gather_small_wait")
    gathered_e = blocks_e.reshape(N_DEV * rows_e, 1024)
    summed_e = _sum_devices(gathered_e, rows_e, "sum_small_early").reshape(-1)
    summed_l = _sum_devices(gathered_l, rows_l, "sum_small_late").reshape(-1)

    def take(summed, offs, k, shape):
        size = math.prod(shape)
        return summed[offs[k]:offs[k] + size].reshape(shape)

    def rows_of(gathered, rows_, offs, k, width):
        return gathered.reshape(N_DEV, rows_ * 1024)[:, offs[k]:offs[k] + width]

    loss = take(summed_e, offs_e, 0, ())
    grad_final_g = take(summed_e, offs_e, 1, (d,))
    grad_norm_g = jnp.concatenate([take(summed_l, offs_l, 0, (1, d)), take(summed_e, offs_e, 2, (1, d))], axis=0)
    grad_ln_g_full = take(summed_e, offs_e, 3, (1, e))
    grad_ln_b_full = take(summed_e, offs_e, 4, (1, e))
    grad_b_b_s = take(summed_e, offs_e, 5, (1, GROUPS, CHUNK))
    grad_b_w_s = take(summed_e, offs_e, 6, (1, GROUPS, CHUNK, CHUNK))
    grad_conv_w_full = take(summed_l, offs_l, 1, (3, e))
    grad_a_conv_b = take(summed_l, offs_l, 2, (1, e))
    grad_a_conv_w = lax.dynamic_slice_in_dim(grad_conv_w_full, chip * es, es, axis=1)[None]
    grad_b_ln_g = lax.dynamic_slice_in_dim(grad_ln_g_full, chip * es, es, axis=1)
    grad_b_ln_b = lax.dynamic_slice_in_dim(grad_ln_b_full, chip * es, es, axis=1)
    dmod_e = rows_of(gathered_e, rows_e, offs_e, 7, 4 * d)
    dmod_l = rows_of(gathered_l, rows_l, offs_l, 3, 2 * d)
    dmod_all = jnp.stack([jnp.concatenate([dmod_l, dmod_e[:, 3 * d:]], axis=1), dmod_e[:, :3 * d]], axis=1)
    mod_b_e = take(summed_e, offs_e, 7, (4 * d,))
    mod_b_l = take(summed_l, offs_l, 3, (2 * d,))
    grad_mod_b = jnp.stack([jnp.concatenate([mod_b_l, mod_b_e[3 * d:]]), mod_b_e[:3 * d]])
    dmod_cols = jnp.transpose(lax.dynamic_slice_in_dim(dmod_all, chip * wd, wd, axis=2), (1, 0, 2))

    grad_mod_w, delta_mod_w, new_m_mod_w, new_v_mod_w = _mod_w_update(
        jnp.transpose(c_act), dmod_cols, mod_w, m_mod_w, v_mod_w, "mod_w_update")
    (g_a_w_out,) = finish(ao_send, ao_recv, ao_grads, ao_lands, ["row"], ["a_w_out"], delta_mod_w, "a_out")
    adamw_big("a_w_out", a_w_out, g_a_w_out, m_a_w_out, v_a_w_out)
    (g_a_w_in,) = finish(ai_send, ai_recv, ai_grads, ai_lands, ["col"], ["a_w_in"], upd["a_w_out"][0], "a_in")
    adamw_big("a_w_in", a_w_in, g_a_w_in, m_a_w_in, v_a_w_in)
    small_w = [("mod_b", mod_b, grad_mod_b, m_mod_b, v_mod_b), ("norm_g", norm_g, grad_norm_g, m_norm_g, v_norm_g),
               ("a_conv_w", a_conv_w, grad_a_conv_w, m_a_conv_w, v_a_conv_w),
               ("a_conv_b", a_conv_b, grad_a_conv_b, m_a_conv_b, v_a_conv_b),
               ("b_ln_g", b_ln_g, grad_b_ln_g, m_b_ln_g, v_b_ln_g), ("b_ln_b", b_ln_b, grad_b_ln_b, m_b_ln_b, v_b_ln_b),
               ("b_w_s", b_w_s, grad_b_w_s, m_b_w_s, v_b_w_s), ("b_b_s", b_b_s, grad_b_b_s, m_b_b_s, v_b_b_s),
               ("final_g", final_g, grad_final_g, m_final_g, v_final_g)]

    def flat2d(a):
        return a.reshape(-1, a.shape[-1])

    res = _adamw_small([flat2d(t[1]) for t in small_w], [flat2d(t[2]) for t in small_w],
                       [flat2d(t[3]) for t in small_w], [flat2d(t[4]) for t in small_w], "adamw_small")
    for (nm, w, _, _, _), r3 in zip(small_w, res):
        upd[nm] = tuple(o.reshape(w.shape) for o in r3)
    upd["mod_w"] = (delta_mod_w, new_m_mod_w, new_v_mod_w)

    grads = {"mod_w": grad_mod_w, "mod_b": grad_mod_b, "norm_g": grad_norm_g, "a_conv_w": grad_a_conv_w,
             "a_conv_b": grad_a_conv_b, "b_ln_g": grad_b_ln_g, "b_ln_b": grad_b_ln_b, "b_w_s": grad_b_w_s,
             "b_b_s": grad_b_b_s, "final_g": grad_final_g, **big_grads}
    order = ["mod_w", "mod_b", "norm_g", "a_w_in", "a_conv_w", "a_conv_b", "a_w_out", "b_w_in", "b_ln_g", "b_ln_b",
             "b_w_s", "b_b_s", "b_w_out", "final_g"]
    return (loss, grad_x[None], *[grads[k] for k in order], *[upd[k][0] for k in order],
            *[upd[k][1] for k in order], *[upd[k][2] for k in order])
```

```python
import functools
import math

import jax
import jax.numpy as jnp
from jax import lax
from jax.experimental import pallas as pl
from jax.experimental.pallas import tpu as pltpu

F32 = jnp.float32
BF16 = jnp.bfloat16
MESH = pl.DeviceIdType.MESH

N_DEV = 8
N_CHIP = 4
SUBLANE = 8
PACK = 16
ACT = F32
RMS_EPS = 1e-6
LN_EPS = 1e-5
CHUNK = 128
GROUPS = 8
ADAM_LR = 0.001
ADAM_B1 = 0.9
ADAM_B2 = 0.999
ADAM_EPS = 1e-08
ADAM_WD = 0.01
ADAM_STEP = 10
VMEM_LIMIT = 56 << 20

HBM_SPEC = pl.BlockSpec(memory_space=pltpu.HBM)
VMEM_SPEC = pl.BlockSpec(memory_space=pltpu.VMEM)
SEM_SPEC = pl.BlockSpec(memory_space=pltpu.SEMAPHORE)
ANY_SPEC = pl.BlockSpec(memory_space=pl.ANY)
EFFECT = pltpu.SideEffectType.DATAFLOW_SIDE_EFFECTING


def _params(*sem):
    return pltpu.CompilerParams(dimension_semantics=sem, vmem_limit_bytes=VMEM_LIMIT)


def _tile(n, want):
    if n <= want:
        return n
    t = want
    while n % t:
        t -= 128
    return t


def _sigmoid(x):
    return 0.5 * jnp.tanh(0.5 * x) + 0.5


def _silu_and_grad(x):
    s = _sigmoid(x)
    return x * s, s * (1.0 + x * (1.0 - s))


def _gelu_and_grad(x):
    cdf = 0.5 * (1.0 + lax.erf(x * (1.0 / math.sqrt(2.0))))
    pdf = jnp.exp(-0.5 * x * x) * (1.0 / math.sqrt(2.0 * math.pi))
    return x * cdf, cdf + x * pdf


def _gelu(x):
    return x * (0.5 * (1.0 + lax.erf(x * (1.0 / math.sqrt(2.0)))))


def _rms(x):
    r = lax.rsqrt(jnp.mean(x * x, axis=-1, keepdims=True) + RMS_EPS)
    return x * r, r


def _rms_bwd(dxn, xn, r):
    return r * (dxn - xn * jnp.mean(dxn * xn, axis=-1, keepdims=True))


def _colsum(a):
    return jnp.sum(a, axis=0, keepdims=True)


def _shift_down(cur, before, k):
    rolled = pltpu.roll(cur, k, 0)
    row = lax.broadcasted_iota(jnp.int32, before.shape, 0)
    head = jnp.where(row < k, pltpu.roll(before, k, 0), rolled[:SUBLANE])
    return jnp.concatenate([head, rolled[SUBLANE:]], axis=0)


def _shift_up(cur, after, k):
    n = cur.shape[0]
    rolled = pltpu.roll(cur, n - k, 0)
    row = lax.broadcasted_iota(jnp.int32, after.shape, 0)
    tail = jnp.where(row >= SUBLANE - k, pltpu.roll(after, SUBLANE - k, 0), rolled[n - SUBLANE:])
    return jnp.concatenate([rolled[:n - SUBLANE], tail], axis=0)


def _after_spec():
    return pl.BlockSpec((SUBLANE, 128), lambda *_: (0, 0))


def _mm_proj(h, w, n_split, name, after):
    s, d = h.shape
    e = w.shape[1] // n_split
    tm, tn = _tile(s, 1024), _tile(e, 2048)
    nj = e // tn

    def body(h_ref, w_ref, after_ref, o_ref):
        o_ref[...] = jnp.dot(h_ref[...], w_ref[...], preferred_element_type=F32).astype(ACT)

    return pl.pallas_call(
        body, name=name,
        out_shape=jax.ShapeDtypeStruct((n_split, s, e), ACT),
        grid=(s // tm, n_split * nj),
        in_specs=[pl.BlockSpec((tm, d), lambda i, j: (i, 0)), pl.BlockSpec((d, tn), lambda i, j: (0, j)),
                  _after_spec()],
        out_specs=pl.BlockSpec((None, tm, tn), lambda i, j: (j // nj, i, j % nj)),
        compiler_params=_params("parallel", "parallel"),
    )(h, w, after)


def _mm_proj_slab(h, w, proj, q_idx, n_split, name, after):
    s, d = h.shape
    e = w.shape[1] // n_split
    tm, tn = _tile(s, 1024), _tile(e, 2048)
    nj = e // tn

    def body(q_ref, h_ref, w_ref, *rest):
        o_ref = rest[-1]
        o_ref[...] = jnp.dot(h_ref[...], w_ref[...], preferred_element_type=F32).astype(ACT)

    in_specs = [pl.BlockSpec((tm, d), lambda i, j, qr: (i, 0)), pl.BlockSpec((d, tn), lambda i, j, qr: (0, qr[0] * nj + j)),
                _after_spec()]
    args = [q_idx, h, w, after]
    aliases = {}
    if proj is not None:
        in_specs.append(ANY_SPEC)
        args.append(proj)
        aliases = {4: 0}
    return pl.pallas_call(
        body, name=name,
        out_shape=jax.ShapeDtypeStruct((n_split, s, e), ACT),
        grid_spec=pltpu.PrefetchScalarGridSpec(
            num_scalar_prefetch=1, grid=(s // tm, nj), in_specs=in_specs,
            out_specs=pl.BlockSpec((None, tm, tn), lambda i, j, qr: (qr[0], i, j))),
        input_output_aliases=aliases,
        compiler_params=_params("parallel", "parallel"),
    )(*args)


def _mm_dh_norm_bwd(dp, w, x, dx_in, g, scale, name, after, br=None, gate=None):
    nq, s, e = dp.shape
    d = w.shape[0]
    has_branch = br is not None
    tm, tk = _tile(s, 1024), _tile(e, 1024 if has_branch else 2048)
    nkq = e // tk
    nk = nq * nkq

    def body(*refs):
        if has_branch:
            (a_ref, b_ref, after_ref, x_ref, dxin_ref, g_ref, sc_ref, br_ref, gate_ref,
             dx_ref, dsh_ref, dsc_ref, dg_ref, dbr_ref, dgate_ref, acc_ref) = refs
        else:
            (a_ref, b_ref, after_ref, x_ref, dxin_ref, g_ref, sc_ref,
             dx_ref, dsh_ref, dsc_ref, dg_ref, acc_ref) = refs
        i, k = pl.program_id(0), pl.program_id(1)

        @pl.when(jnp.logical_and(i == 0, k == 0))
        def _():
            dsh_ref[...] = jnp.zeros_like(dsh_ref)
            dsc_ref[...] = jnp.zeros_like(dsc_ref)
            dg_ref[...] = jnp.zeros_like(dg_ref)
            if has_branch:
                dgate_ref[...] = jnp.zeros_like(dgate_ref)

        @pl.when(k == 0)
        def _():
            acc_ref[...] = jnp.zeros_like(acc_ref)
        acc_ref[...] += lax.dot_general(a_ref[...], b_ref[...], (((1,), (1,)), ((), ())), preferred_element_type=F32)

        @pl.when(k == nk - 1)
        def _():
            dh = acc_ref[...]
            g_vec = g_ref[...]
            xn, r = _rms(x_ref[...])
            dsh_ref[...] += _colsum(dh)
            dsc_ref[...] += _colsum(dh * (xn * g_vec))
            da = dh * (1.0 + sc_ref[...])
            dg_ref[...] += _colsum(da * xn)
            dx = dxin_ref[...] + _rms_bwd(da * g_vec, xn, r)
            dx_ref[...] = dx
            if has_branch:
                dgate_ref[...] += _colsum(dx * br_ref[...])
                dbr_ref[...] = (gate_ref[...] * dx).astype(BF16)

    rows = pl.BlockSpec((tm, d), lambda i, k: (i, 0))
    vec = pl.BlockSpec((1, d), lambda i, k: (0, 0))
    vec_out = jax.ShapeDtypeStruct((1, d), F32)
    in_specs = [pl.BlockSpec((None, tm, tk), lambda i, k: (k // nkq, i, k % nkq)),
                pl.BlockSpec((d, tk), lambda i, k: (0, k)), _after_spec(), rows, rows, vec, vec]
    out_shape = [jax.ShapeDtypeStruct((s, d), F32), vec_out, vec_out, vec_out]
    out_specs = [rows, vec, vec, vec]
    args = [dp, w, after, x, dx_in, g, scale]
    if has_branch:
        in_specs += [rows, vec]
        out_shape += [jax.ShapeDtypeStruct((s, d), BF16), vec_out]
        out_specs += [rows, vec]
        args += [br, gate]
    return pl.pallas_call(
        body, name=name, out_shape=tuple(out_shape), grid=(s // tm, nk),
        in_specs=in_specs, out_specs=tuple(out_specs), scratch_shapes=[pltpu.VMEM((tm, d), F32)],
        compiler_params=_params("arbitrary", "arbitrary"),
    )(*args)


def _dw_body(n_t):
    def body(a_ref, b_ref, o_ref, acc_ref):
        t = pl.program_id(2)

        @pl.when(t == 0)
        def _():
            acc_ref[...] = jnp.zeros_like(acc_ref)
        acc_ref[...] += lax.dot_general(a_ref[...], b_ref[...], (((0,), (0,)), ((), ())), preferred_element_type=F32)

        @pl.when(t == n_t - 1)
        def _():
            o_ref[...] = acc_ref[...].astype(o_ref.dtype)
    return body


def _mm_dw_in(h, dp, name):
    s, d = h.shape
    nq, _, e = dp.shape
    tm, tn, tt = _tile(d, 1024), _tile(e, 1024), _tile(s, 2048)
    nj = e // tn

    return pl.pallas_call(
        _dw_body(s // tt), name=name,
        out_shape=jax.ShapeDtypeStruct((d, nq * e), BF16),
        grid=(d // tm, nq * nj, s // tt),
        in_specs=[pl.BlockSpec((tt, tm), lambda i, j, t: (t, i)),
                  pl.BlockSpec((None, tt, tn), lambda i, j, t: (j // nj, t, j % nj))],
        out_specs=pl.BlockSpec((tm, tn), lambda i, j, t: (i, j)),
        scratch_shapes=[pltpu.VMEM((tm, tn), F32)],
        compiler_params=_params("parallel", "parallel", "arbitrary"),
    )(h, dp)


def _mm_dw_out(y, dbr, name):
    s, e = y.shape
    d = dbr.shape[1]
    tm, tn, tt = _tile(e, 1024), _tile(d, 1024), _tile(s, 2048)

    return pl.pallas_call(
        _dw_body(s // tt), name=name,
        out_shape=jax.ShapeDtypeStruct((e, d), BF16),
        grid=(e // tm, d // tn, s // tt),
        in_specs=[pl.BlockSpec((tt, tm), lambda i, j, t: (t, i)), pl.BlockSpec((tt, tn), lambda i, j, t: (t, j))],
        out_specs=pl.BlockSpec((tm, tn), lambda i, j, t: (i, j)),
        scratch_shapes=[pltpu.VMEM((tm, tn), F32)],
        compiler_params=_params("parallel", "parallel", "arbitrary"),
    )(y, dbr)


def _row_spec(ts, d):
    return pl.BlockSpec((ts, d), lambda i: (i, 0))


def _vec_spec(d):
    return pl.BlockSpec((1, d), lambda i: (0, 0))


def _norm_mod(x, g, scale, shift, name):
    s, d = x.shape
    ts = _tile(s, 512)

    def body(x_ref, g_ref, sc_ref, sh_ref, h_ref):
        xn, _ = _rms(x_ref[...])
        h_ref[...] = ((xn * g_ref[...]) * (1.0 + sc_ref[...]) + sh_ref[...]).astype(BF16)

    return pl.pallas_call(
        body, name=name, out_shape=jax.ShapeDtypeStruct((s, d), BF16), grid=(s // ts,),
        in_specs=[_row_spec(ts, d), _vec_spec(d), _vec_spec(d), _vec_spec(d)],
        out_specs=_row_spec(ts, d), compiler_params=_params("parallel"),
    )(x, g, scale, shift)


def _head_tile(x1, br, gate, gf, target):
    d = x1.shape[-1]
    xn, r = _rms(x1 + gate * br)
    err = xn * gf - target
    loss = 0.5 * jnp.sum(jnp.mean(err * err, axis=-1, keepdims=True))
    dout = err * (1.0 / d)
    dx = _rms_bwd(dout * gf, xn, r)
    return dx, gate * dx, loss, _colsum(dout * xn), _colsum(dx * br)


def _f32(ref):
    return ref[...].astype(F32)


def _cols_f32(ref, cols):
    return ref[:, cols].astype(F32)


def _rows_before(halo_ref, cols=slice(None)):
    return _cols_f32(halo_ref, cols)[PACK - SUBLANE:]


def _rows_after(halo_ref, cols=slice(None)):
    return _cols_f32(halo_ref, cols)[:SUBLANE]


CONV_CHAIN_COLS = 512


def _conv_fwd(proj, conv_w, conv_b, w_out, x, gate, g_next, scale_next, shift_next, name, after):
    _, s, e = proj.shape
    d = w_out.shape[1]
    ts, te = _tile(s, 256), e
    cw = min(CONV_CHAIN_COLS, te)
    hb = ts // PACK

    def body(bg_ref, cg_ref, xi_ref, z_ref, cgp_ref, xip_ref, w0_ref, w1_ref, w2_ref, b_ref, wo_ref, after_ref,
             x_ref, gate_ref, g_ref, sc_ref, sh_ref, y_ref, br_ref, x1_ref, h_ref):
        first = pl.program_id(0) == 0
        br = None
        for c0 in range(0, te, cw):
            cols = slice(c0, c0 + cw)
            cx = _cols_f32(cg_ref, cols) * _cols_f32(xi_ref, cols)
            before = jnp.where(first, 0.0, _rows_before(cgp_ref, cols) * _rows_before(xip_ref, cols))
            conv = b_ref[:, cols] + w2_ref[:, cols] * cx
            conv = conv + w0_ref[:, cols] * _shift_down(cx, before, 2)
            conv = conv + w1_ref[:, cols] * _shift_down(cx, before, 1)
            z = _cols_f32(z_ref, cols)
            y = ((z * _sigmoid(z)) * _cols_f32(bg_ref, cols) * conv).astype(BF16)
            y_ref[:, cols] = y
            part_br = jnp.dot(y, wo_ref[cols, :], preferred_element_type=F32)
            br = part_br if br is None else br + part_br
        br_ref[...] = br
        x1 = x_ref[...] + gate_ref[...] * br
        x1_ref[...] = x1
        xn, _ = _rms(x1)
        h_ref[...] = ((xn * g_ref[...]) * (1.0 + sc_ref[...]) + sh_ref[...]).astype(BF16)

    def part(q):
        return pl.BlockSpec((None, ts, te), lambda i: (q, i, 0))

    def halo_before(q):
        return pl.BlockSpec((None, PACK, te), lambda i: (q, jnp.maximum(i * hb - 1, 0), 0))

    return pl.pallas_call(
        body, name=name,
        out_shape=(jax.ShapeDtypeStruct((s, e), BF16), jax.ShapeDtypeStruct((s, d), F32),
                   jax.ShapeDtypeStruct((s, d), F32), jax.ShapeDtypeStruct((s, d), BF16)), grid=(s // ts,),
        in_specs=[part(0), part(1), part(2), part(3), halo_before(1), halo_before(2)]
        + [_vec_spec(e)] * 4 + [pl.BlockSpec((e, d), lambda i: (0, 0)), _after_spec(), _row_spec(ts, d)]
        + [_vec_spec(d)] * 4,
        out_specs=(_row_spec(ts, e), _row_spec(ts, d), _row_spec(ts, d), _row_spec(ts, d)),
        compiler_params=_params("parallel"),
    )(proj, proj, proj, proj, proj, proj, *conv_w, conv_b, w_out, after, x, gate, g_next, scale_next, shift_next)


def _conv_bwd(proj, dbr, w_out, conv_w, conv_b, name, after):
    _, s, e = proj.shape
    d = dbr.shape[1]
    ts, te = _tile(s, 512), _tile(e, 1024)
    cw = min(CONV_CHAIN_COLS // 2, te)
    hb = ts // PACK
    n_i = s // ts
    last_halo = s // PACK - 1
    nt = (((1,), (1,)), ((), ()))

    def body(bg_ref, cg_ref, xi_ref, z_ref, dbr_ref, cgp_ref, xip_ref, bgn_ref, zn_ref, dbrn_ref, wo_ref,
             w0_ref, w1_ref, w2_ref, b_ref, after_ref, dp_ref, dw0_ref, dw1_ref, dw2_ref, db_ref):
        i = pl.program_id(1)

        @pl.when(i == 0)
        def _():
            for acc in (dw0_ref, dw1_ref, dw2_ref, db_ref):
                acc[...] = jnp.zeros_like(acc)
        for c0 in range(0, te, cw):
            cols = slice(c0, c0 + cw)
            wo = wo_ref[cols, :]
            dy = lax.dot_general(dbr_ref[...], wo, nt, preferred_element_type=F32)
            dyn = lax.dot_general(dbrn_ref[...], wo, nt, preferred_element_type=F32)[:SUBLANE]
            bg, cg = _cols_f32(bg_ref, cols), _cols_f32(cg_ref, cols)
            xi, z = _cols_f32(xi_ref, cols), _cols_f32(z_ref, cols)
            w0, w1, w2 = w0_ref[:, cols], w1_ref[:, cols], w2_ref[:, cols]
            cx = cg * xi
            before = jnp.where(i > 0, _rows_before(cgp_ref, cols) * _rows_before(xip_ref, cols), 0.0)
            cx1 = _shift_down(cx, before, 1)
            cx2 = _shift_down(cx, before, 2)
            conv = b_ref[:, cols] + w2 * cx
            conv = conv + w0 * cx2
            conv = conv + w1 * cx1
            sz, dsz = _silu_and_grad(z)
            dp_ref[3, :, cols] = (dy * bg * conv * dsz).astype(BF16)
            dp_ref[0, :, cols] = (dy * sz * conv).astype(BF16)
            dconv = dy * sz * bg
            zn = _rows_after(zn_ref, cols)
            after = jnp.where(i < n_i - 1, dyn * (zn * _sigmoid(zn)) * _rows_after(bgn_ref, cols), 0.0)
            db_ref[:, cols] += _colsum(dconv)
            dw2_ref[:, cols] += _colsum(dconv * cx)
            dw1_ref[:, cols] += _colsum(dconv * cx1)
            dw0_ref[:, cols] += _colsum(dconv * cx2)
            dcx = w2 * dconv + w1 * _shift_up(dconv, after, 1) + w0 * _shift_up(dconv, after, 2)
            dp_ref[1, :, cols] = (dcx * xi).astype(BF16)
            dp_ref[2, :, cols] = (dcx * cg).astype(BF16)

    def part(q):
        return pl.BlockSpec((None, ts, te), lambda j, i: (q, i, j))

    def halo_before(q):
        return pl.BlockSpec((None, PACK, te), lambda j, i: (q, jnp.maximum(i * hb - 1, 0), j))

    def halo_after(q):
        return pl.BlockSpec((None, PACK, te), lambda j, i: (q, jnp.minimum((i + 1) * hb, last_halo), j))

    return pl.pallas_call(
        body, name=name,
        out_shape=(jax.ShapeDtypeStruct((4, s, e), BF16),) + (jax.ShapeDtypeStruct((1, e), F32),) * 4,
        grid=(e // te, n_i),
        in_specs=[part(0), part(1), part(2), part(3), pl.BlockSpec((ts, d), lambda j, i: (i, 0)),
                  halo_before(1), halo_before(2), halo_after(0), halo_after(3),
                  pl.BlockSpec((PACK, d), lambda j, i: (jnp.minimum((i + 1) * hb, last_halo), 0)),
                  pl.BlockSpec((te, d), lambda j, i: (j, 0))]
        + [pl.BlockSpec((1, te), lambda j, i: (0, j))] * 4 + [_after_spec()],
        out_specs=(pl.BlockSpec((4, ts, te), lambda j, i: (0, i, j)),) + (pl.BlockSpec((1, te), lambda j, i: (0, j)),) * 4,
        compiler_params=_params("parallel", "arbitrary"),
    )(proj, proj, proj, proj, dbr, proj, proj, proj, proj, dbr, w_out, *conv_w, conv_b, after)


def _tril(w):
    row = lax.broadcasted_iota(jnp.int32, w.shape, 0)
    col = lax.broadcasted_iota(jnp.int32, w.shape, 1)
    return jnp.where(row >= col, w, 0.0)


def _triu(w):
    row = lax.broadcasted_iota(jnp.int32, w.shape, 0)
    col = lax.broadcasted_iota(jnp.int32, w.shape, 1)
    return jnp.where(row <= col, w, 0.0)


def _layer_norm_fwd(v, g, b):
    mu = jnp.mean(v, axis=-1, keepdims=True)
    vc = v - mu
    rstd = lax.rsqrt(jnp.mean(vc * vc, axis=-1, keepdims=True) + LN_EPS)
    vhat = vc * rstd
    return vhat * g + b, vhat, rstd


GMLP_CHUNKS_PER_STEP = 2


def _gmlp_rows(s):
    return CHUNK * min(GMLP_CHUNKS_PER_STEP, s // CHUNK)


def _mix_positions(w_ref, src_scr, dst_scr, gw, mask, bias_ref=None):
    for gi in range(GROUPS):
        cols = slice(gi * gw, (gi + 1) * gw)
        wm = mask(w_ref[gi]).astype(BF16)
        for n in range(src_scr.shape[0] // CHUNK):
            rows = slice(n * CHUNK, (n + 1) * CHUNK)
            out = jnp.dot(wm, src_scr[rows, cols], preferred_element_type=F32)
            if bias_ref is not None:
                out = out + bias_ref[:, gi:gi + 1]
            dst_scr[rows, cols] = out


def _gmlp_fwd_head(proj, ln_g, ln_b, w_s, b_s_t, w_out, x1, gate, gf, target, name):
    _, s, e = proj.shape
    d = w_out.shape[1]
    gw = e // GROUPS
    ts = _gmlp_rows(s)

    def body(pu_ref, pv_ref, pz_ref, g_ref, b_ref, ws_ref, bs_ref, wo_ref, x1_ref, gate_ref, gf_ref, tg_ref,
             y_ref, dx_ref, dbr_ref, loss_ref, dgf_ref, dgate_ref, vn_scr, mix_scr):
        @pl.when(pl.program_id(0) == 0)
        def _():
            loss_ref[...] = jnp.zeros_like(loss_ref)
            dgf_ref[...] = jnp.zeros_like(dgf_ref)
            dgate_ref[...] = jnp.zeros_like(dgate_ref)
        vn, _, _ = _layer_norm_fwd(_gelu(_f32(pv_ref)), g_ref[...], b_ref[...])
        vn_scr[...] = vn.astype(BF16)
        _mix_positions(ws_ref, vn_scr, mix_scr, gw, _tril, bs_ref)
        z = _f32(pz_ref)
        y = ((z * _sigmoid(z)) * (_gelu(_f32(pu_ref)) * mix_scr[...])).astype(BF16)
        y_ref[...] = y
        br = jnp.dot(y, wo_ref[...], preferred_element_type=F32)
        dx, dbr, loss, dgf, dgate = _head_tile(x1_ref[...], br, gate_ref[...], gf_ref[...], tg_ref[...])
        dx_ref[...] = dx
        dbr_ref[...] = dbr.astype(BF16)
        loss_ref[...] += loss
        dgf_ref[...] += dgf
        dgate_ref[...] += dgate

    def part(q):
        return pl.BlockSpec((None, ts, e), lambda i: (q, i, 0))

    return pl.pallas_call(
        body, name=name,
        out_shape=(jax.ShapeDtypeStruct((s, e), BF16), jax.ShapeDtypeStruct((s, d), F32),
                   jax.ShapeDtypeStruct((s, d), BF16), jax.ShapeDtypeStruct((SUBLANE, 128), F32),
                   jax.ShapeDtypeStruct((1, d), F32), jax.ShapeDtypeStruct((1, d), F32)),
        grid=(s // ts,),
        in_specs=[part(0), part(1), part(2), _vec_spec(e), _vec_spec(e),
                  pl.BlockSpec((GROUPS, CHUNK, CHUNK), lambda i: (0, 0, 0)),
                  pl.BlockSpec((CHUNK, GROUPS), lambda i: (0, 0)), pl.BlockSpec((e, d), lambda i: (0, 0)),
                  _row_spec(ts, d), _vec_spec(d), _vec_spec(d), _row_spec(ts, d)],
        out_specs=(_row_spec(ts, e), _row_spec(ts, d), _row_spec(ts, d),
                   pl.BlockSpec((SUBLANE, 128), lambda i: (0, 0)), _vec_spec(d), _vec_spec(d)),
        scratch_shapes=[pltpu.VMEM((ts, e), BF16), pltpu.VMEM((ts, e), F32)],
        compiler_params=_params("arbitrary"),
    )(proj, proj, proj, ln_g, ln_b, w_s, b_s_t, w_out, x1, gate, gf, target)


def _gmlp_bwd(proj, dbr, w_out, ln_g, ln_b, w_s, w_s_t, b_s_t, name):
    _, s, e = proj.shape
    d = dbr.shape[1]
    gw = e // GROUPS
    ts = _gmlp_rows(s)
    n_i = s // ts

    def body(pu_ref, pv_ref, pz_ref, dbr_ref, wo_ref, g_ref, b_ref, ws_ref, wst_ref, bs_ref,
             dp_ref, dws_ref, dbs_ref, dlg_ref, dlb_ref, vn_scr, mix_scr, dm_scr, dvn_scr, dmacc_scr):
        i = pl.program_id(0)

        @pl.when(i == 0)
        def _():
            dws_ref[...] = jnp.zeros_like(dws_ref)
            dlg_ref[...] = jnp.zeros_like(dlg_ref)
            dlb_ref[...] = jnp.zeros_like(dlb_ref)
            dmacc_scr[...] = jnp.zeros_like(dmacc_scr)
        ln_g = g_ref[...]
        u, du_dpu = _gelu_and_grad(_f32(pu_ref))
        v, dv_dpv = _gelu_and_grad(_f32(pv_ref))
        vn, vhat, rstd = _layer_norm_fwd(v, ln_g, b_ref[...])
        vn_scr[...] = vn.astype(BF16)
        _mix_positions(ws_ref, vn_scr, mix_scr, gw, _tril, bs_ref)
        mixed = mix_scr[...]
        dy = lax.dot_general(dbr_ref[...], wo_ref[...], (((1,), (1,)), ((), ())), preferred_element_type=F32)
        sz, dsz = _silu_and_grad(_f32(pz_ref))
        ds = dy * sz
        dp_ref[2] = (dy * (u * mixed) * dsz).astype(BF16)
        dp_ref[0] = (ds * mixed * du_dpu).astype(BF16)
        dm = ds * u
        dm_scr[...] = dm.astype(BF16)
        for n in range(ts // CHUNK):
            rows = slice(n * CHUNK, (n + 1) * CHUNK)
            dmacc_scr[...] += dm[rows]
            for gi in range(GROUPS):
                cols = slice(gi * gw, (gi + 1) * gw)
                dws_ref[gi] += lax.dot_general(dm_scr[rows, cols], vn_scr[rows, cols], (((1,), (1,)), ((), ())),
                                               preferred_element_type=F32)
        _mix_positions(wst_ref, dm_scr, dvn_scr, gw, _triu)
        dvn = dvn_scr[...]
        dlg_ref[...] += _colsum(dvn * vhat)
        dlb_ref[...] += _colsum(dvn)
        dvh = dvn * ln_g
        dv = rstd * (dvh - jnp.mean(dvh, axis=-1, keepdims=True) - vhat * jnp.mean(dvh * vhat, axis=-1, keepdims=True))
        dp_ref[1] = (dv * dv_dpv).astype(BF16)

        @pl.when(i == n_i - 1)
        def _():
            for gi in range(GROUPS):
                dws_ref[gi] = _tril(dws_ref[gi])
                dbs_ref[:, gi:gi + 1] = jnp.sum(dmacc_scr[:, gi * gw:(gi + 1) * gw], axis=1, keepdims=True)

    def part(q):
        return pl.BlockSpec((None, ts, e), lambda i: (q, i, 0))

    w_spec = pl.BlockSpec((GROUPS, CHUNK, CHUNK), lambda i: (0, 0, 0))
    bs_spec = pl.BlockSpec((CHUNK, GROUPS), lambda i: (0, 0))
    return pl.pallas_call(
        body, name=name,
        out_shape=(jax.ShapeDtypeStruct((3, s, e), BF16), jax.ShapeDtypeStruct((GROUPS, CHUNK, CHUNK), F32),
                   jax.ShapeDtypeStruct((CHUNK, GROUPS), F32), jax.ShapeDtypeStruct((1, e), F32),
                   jax.ShapeDtypeStruct((1, e), F32)),
        grid=(n_i,),
        in_specs=[part(0), part(1), part(2), pl.BlockSpec((ts, d), lambda i: (i, 0)),
                  pl.BlockSpec((e, d), lambda i: (0, 0)), _vec_spec(e), _vec_spec(e), w_spec, w_spec, bs_spec],
        out_specs=(pl.BlockSpec((3, ts, e), lambda i: (0, i, 0)), w_spec, bs_spec, _vec_spec(e), _vec_spec(e)),
        scratch_shapes=[pltpu.VMEM((ts, e), BF16), pltpu.VMEM((ts, e), F32), pltpu.VMEM((ts, e), BF16),
                        pltpu.VMEM((ts, e), F32), pltpu.VMEM((CHUNK, e), F32)],
        compiler_params=_params("arbitrary"),
    )(proj, proj, proj, dbr, w_out, ln_g, ln_b, w_s, w_s_t, b_s_t)


def _mod_fwd(c_all, mod_w, mod_b_cols, name):
    n_layer, d, w = mod_w.shape

    def body(c_ref, w_ref, b_ref, ca_ref, o_ref):
        c = c_ref[...]
        ca = c * _sigmoid(c)
        ca_ref[...] = ca
        for li in range(n_layer):
            o_ref[li * N_DEV:(li + 1) * N_DEV, :] = (
                jnp.dot(ca, w_ref[li], preferred_element_type=F32, precision=lax.Precision.HIGHEST) + b_ref[li])

    return pl.pallas_call(
        body, name=name,
        out_shape=(jax.ShapeDtypeStruct((N_DEV, d), F32), jax.ShapeDtypeStruct((n_layer * N_DEV, w), F32)),
        in_specs=[VMEM_SPEC] * 3, out_specs=(VMEM_SPEC, VMEM_SPEC),
        compiler_params=pltpu.CompilerParams(vmem_limit_bytes=VMEM_LIMIT),
    )(c_all, mod_w, mod_b_cols)


def _adamw(w, g, m, v):
    m = ADAM_B1 * m + (1.0 - ADAM_B1) * g
    v = ADAM_B2 * v + (1.0 - ADAM_B2) * (g * g)
    m_hat = m / (1.0 - ADAM_B1 ** ADAM_STEP)
    v_hat = v / (1.0 - ADAM_B2 ** ADAM_STEP)
    delta = -ADAM_LR * (m_hat / (jnp.sqrt(v_hat) + ADAM_EPS) + ADAM_WD * w)
    return delta, m, v


def _adamw_2d(w, g, m, v, name):
    r, c = w.shape
    tr, tc = _tile(r, 512), _tile(c, 1024)

    def body(w_ref, g_ref, m_ref, v_ref, go_ref, d_ref, nm_ref, nv_ref):
        g = g_ref[...]
        go_ref[...] = g
        d_ref[...], nm_ref[...], nv_ref[...] = _adamw(w_ref[...], g, m_ref[...], v_ref[...])

    spec = pl.BlockSpec((tr, tc), lambda i, j: (i, j))
    shape = jax.ShapeDtypeStruct((r, c), F32)
    return pl.pallas_call(
        body, name=name, out_shape=(shape,) * 4, grid=(r // tr, c // tc),
        in_specs=[spec] * 4, out_specs=(spec,) * 4, compiler_params=_params("parallel", "parallel"),
    )(w, g, m, v)


def _mod_w_update(ca_t, dmod_cols, w, m, v, name):
    n_layer, d, wd = w.shape
    tr = _tile(d, 256)

    def body(ca_ref, dm_ref, w_ref, m_ref, v_ref, g_ref, d_ref, nm_ref, nv_ref):
        ca = ca_ref[...]
        dm = dm_ref[...]
        g = ca[:, 0:1] * dm[0:1, :]
        for b in range(1, N_DEV):
            g = g + ca[:, b:b + 1] * dm[b:b + 1, :]
        g_ref[...] = g
        d_ref[...], nm_ref[...], nv_ref[...] = _adamw(w_ref[...], g, m_ref[...], v_ref[...])

    spec = pl.BlockSpec((None, tr, wd), lambda l, i: (l, i, 0))
    shape = jax.ShapeDtypeStruct((n_layer, d, wd), F32)
    return pl.pallas_call(
        body, name=name, out_shape=(shape,) * 4, grid=(n_layer, d // tr),
        in_specs=[pl.BlockSpec((tr, N_DEV), lambda l, i: (i, 0)), pl.BlockSpec((None, N_DEV, wd), lambda l, i: (l, 0, 0)),
                  spec, spec, spec],
        out_specs=(spec,) * 4, compiler_params=_params("parallel", "parallel"),
    )(ca_t, dmod_cols, w, m, v)


def _adamw_small(ws, gs, ms, vs, name):
    n = len(ws)

    def body(*refs):
        ins, outs = refs[:4 * n], refs[4 * n:]
        for k in range(n):
            delta, nm, nv = _adamw(ins[k][...], ins[n + k][...], ins[2 * n + k][...], ins[3 * n + k][...])
            outs[3 * k][...] = delta
            outs[3 * k + 1][...] = nm
            outs[3 * k + 2][...] = nv

    out_shape = []
    for w in ws:
        out_shape += [jax.ShapeDtypeStruct(w.shape, F32)] * 3
    outs = pl.pallas_call(
        body, name=name, out_shape=tuple(out_shape),
        in_specs=[VMEM_SPEC] * (4 * n), out_specs=tuple([VMEM_SPEC] * (3 * n)),
        compiler_params=pltpu.CompilerParams(vmem_limit_bytes=VMEM_LIMIT),
    )(*ws, *gs, *ms, *vs)
    return [tuple(outs[3 * k:3 * k + 3]) for k in range(n)]


def _place():
    return lax.axis_index("x"), lax.axis_index("y"), lax.axis_index("c")


def _other_chips(x, y):
    return [(1 - x, y), (x, 1 - y), (1 - x, 1 - y)]


def _all_gather(block, name, after=None):
    m_per, n = block.shape

    def body(x_ref, *rest):
        out_ref, send_sems, recv_sems, local_sem = rest[-4:]
        x, y, c = _place()

        def rows(dev_index):
            return out_ref.at[pl.ds(dev_index * m_per, m_per), :]

        def copy(k_send, k_recv, to):
            return pltpu.make_async_remote_copy(
                src_ref=x_ref, dst_ref=rows(4 * x + 2 * y + c), send_sem=send_sems.at[k_send],
                recv_sem=recv_sems.at[k_recv], device_id=to, device_id_type=MESH)

        mine = pltpu.make_async_copy(x_ref, rows(4 * x + 2 * y + c), local_sem)
        mine.start()
        copies = [copy(0, 0, (x, y, 1 - c))]
        for j, chip in enumerate(_other_chips(x, y)):
            copies += [copy(1 + 2 * j + cc, 1 + 2 * j + c, (*chip, cc)) for cc in range(2)]
        for cp in copies:
            cp.start()
        for k in range(N_DEV - 1):
            pltpu.make_async_remote_copy(
                src_ref=x_ref, dst_ref=rows(0), send_sem=send_sems.at[k], recv_sem=recv_sems.at[k],
                device_id=(x, y, 1 - c), device_id_type=MESH).wait_recv()
        for cp in copies:
            cp.wait_send()
        mine.wait()

    return pl.pallas_call(
        body, name=name, out_shape=jax.ShapeDtypeStruct((N_DEV * m_per, n), F32),
        in_specs=[VMEM_SPEC] + ([] if after is None else [ANY_SPEC]), out_specs=VMEM_SPEC,
        scratch_shapes=[pltpu.SemaphoreType.DMA((7,)), pltpu.SemaphoreType.DMA((7,)), pltpu.SemaphoreType.DMA],
        compiler_params=pltpu.CompilerParams(vmem_limit_bytes=VMEM_LIMIT),
    )(*([block] if after is None else [block, after]))


def _hbm(a):
    return pltpu.with_memory_space_constraint(a, pltpu.HBM)


def _all_gather_start(blocks, name):
    def body(in_ref, send_sems, recv_sems, thru_ref, token):
        token[...] = jnp.zeros_like(token)
        x, y, c = _place()
        mine = in_ref.at[4 * x + 2 * y + c]
        pltpu.make_async_remote_copy(
            src_ref=mine, dst_ref=mine, send_sem=send_sems.at[0], recv_sem=recv_sems.at[0],
            device_id=(x, y, 1 - c), device_id_type=MESH).start()
        for j, chip in enumerate(_other_chips(x, y)):
            for cc in range(2):
                pltpu.make_async_remote_copy(
                    src_ref=mine, dst_ref=mine, send_sem=send_sems.at[1 + 2 * j + cc],
                    recv_sem=recv_sems.at[1 + 2 * j + c], device_id=(*chip, cc), device_id_type=MESH).start()

    sems = pltpu.SemaphoreType.DMA((N_DEV - 1,))
    return pl.pallas_call(
        body, name=name,
        out_shape=(sems, sems, pltpu.HBM(blocks.shape, blocks.dtype), jax.ShapeDtypeStruct((SUBLANE, 128), F32)),
        in_specs=[HBM_SPEC], out_specs=(SEM_SPEC, SEM_SPEC, HBM_SPEC, VMEM_SPEC), input_output_aliases={0: 2},
        compiler_params=pltpu.CompilerParams(has_side_effects=EFFECT),
    )(_hbm(blocks))


def _all_gather_wait(send_sems, recv_sems, blocks, after, name):
    def body(in_ref, send_sems, recv_sems, after_ref, out_ref):
        x, y, c = _place()
        one = in_ref.at[0]
        for k in range(N_DEV - 1):
            pltpu.make_async_remote_copy(
                src_ref=one, dst_ref=one, send_sem=send_sems.at[k], recv_sem=recv_sems.at[k],
                device_id=(x, y, 1 - c), device_id_type=MESH).wait_send()
            pltpu.make_async_remote_copy(
                src_ref=one, dst_ref=one, send_sem=send_sems.at[k], recv_sem=recv_sems.at[k],
                device_id=(x, y, 1 - c), device_id_type=MESH).wait_recv()

    return pl.pallas_call(
        body, name=name, out_shape=pltpu.HBM(blocks.shape, blocks.dtype),
        in_specs=[HBM_SPEC, SEM_SPEC, SEM_SPEC, ANY_SPEC], out_specs=HBM_SPEC, input_output_aliases={0: 0},
        compiler_params=pltpu.CompilerParams(has_side_effects=EFFECT),
    )(blocks, send_sems, recv_sems, after)


def _place_own(shard, kind, chip_idx, name):
    r, cdim = shard.shape
    tr, tc = _tile(r, 512), _tile(cdim, 1024)
    nrb, ncb = r // tr, cdim // tc

    def body(k_ref, s_ref, o_ref):
        o_ref[...] = s_ref[...].astype(BF16)

    if kind == "col":
        full, o_map = (r, N_CHIP * cdim), lambda i, j, kr: (i, kr[0] * ncb + j)
    else:
        full, o_map = (N_CHIP * r, cdim), lambda i, j, kr: (kr[0] * nrb + i, j)
    return pl.pallas_call(
        body, name=name, out_shape=jax.ShapeDtypeStruct(full, BF16),
        grid_spec=pltpu.PrefetchScalarGridSpec(
            num_scalar_prefetch=1, grid=(nrb, ncb),
            in_specs=[pl.BlockSpec((tr, tc), lambda i, j, kr: (i, j))],
            out_specs=pl.BlockSpec((tr, tc), o_map)),
        compiler_params=_params("parallel", "parallel"),
    )(chip_idx, shard)


def _weight_window(ref, kind, shard_shape, k, half):
    r, cdim = shard_shape
    hr = r // 2
    if kind == "col":
        return ref.at[pl.ds(half * hr, hr), pl.ds(pl.multiple_of(k * cdim, 128), cdim)]
    return ref.at[pl.ds(pl.multiple_of(k * r + half * hr, 2 * SUBLANE), hr), :]


def _target_cores(c, both_cores):
    return [(cc, cc, c) for cc in range(2)] if both_cores else [(c, 0, 0)]


def _gather_weights_start(fulls, kinds, shard_shapes, groups, after, name, both_cores=True):
    n, ng = len(fulls), len(groups)
    per = 2 if both_cores else 1

    def body(*refs):
        ins = refs[:n]
        sems = refs[n + 1:n + 1 + 2 * ng]
        token = refs[2 * n + 1 + 2 * ng]
        x, y, c = _place()
        chips = _other_chips(x, y)
        for g, group in enumerate(groups):
            for pos, (w, j) in enumerate(group):
                own = _weight_window(ins[w], kinds[w], shard_shapes[w], 2 * x + y, c)
                for cc, mine, theirs in _target_cores(c, both_cores):
                    pltpu.make_async_remote_copy(
                        src_ref=own, dst_ref=own,
                        send_sem=sems[2 * g].at[per * pos + mine], recv_sem=sems[2 * g + 1].at[per * pos + theirs],
                        device_id=(*chips[j], cc), device_id_type=MESH).start()
        token[...] = jnp.zeros_like(token)

    sem_shapes = []
    for group in groups:
        sem_shapes += [pltpu.SemaphoreType.DMA((per * len(group),))] * 2
    outs = pl.pallas_call(
        body, name=name,
        out_shape=tuple(sem_shapes) + tuple(pltpu.HBM(f.shape, f.dtype) for f in fulls)
        + (jax.ShapeDtypeStruct((SUBLANE, 128), F32),),
        in_specs=[HBM_SPEC] * n + [ANY_SPEC], out_specs=(SEM_SPEC,) * (2 * ng) + (HBM_SPEC,) * n + (VMEM_SPEC,),
        input_output_aliases={w: 2 * ng + w for w in range(n)},
        compiler_params=pltpu.CompilerParams(has_side_effects=EFFECT),
    )(*[_hbm(f) for f in fulls], after)
    sems = [(outs[2 * g], outs[2 * g + 1]) for g in range(ng)]
    return sems, list(outs[2 * ng:2 * ng + n]), outs[2 * ng + n]


def _gather_weights_wait(sems, group, fulls, kinds, shard_shapes, after, name, both_cores=True):
    n = len(fulls)
    per = 2 if both_cores else 1

    def body(*refs):
        ins = refs[:n]
        send_sems, recv_sems = refs[n], refs[n + 1]
        x, y, c = _place()
        chips = _other_chips(x, y)
        for pos, (w, j) in enumerate(group):
            own = _weight_window(ins[w], kinds[w], shard_shapes[w], 2 * x + y, c)
            for cc, mine, _ in _target_cores(c, both_cores):
                landed = _weight_window(ins[w], kinds[w], shard_shapes[w], 2 * chips[j][0] + chips[j][1], cc)
                k = per * pos + mine
                pltpu.make_async_remote_copy(
                    src_ref=own, dst_ref=own, send_sem=send_sems.at[k], recv_sem=recv_sems.at[k],
                    device_id=(*chips[j], cc), device_id_type=MESH).wait_send()
                pltpu.make_async_remote_copy(
                    src_ref=landed, dst_ref=landed, send_sem=send_sems.at[k], recv_sem=recv_sems.at[k],
                    device_id=(*chips[j], cc), device_id_type=MESH).wait_recv()

    return list(pl.pallas_call(
        body, name=name, out_shape=tuple(pltpu.HBM(f.shape, f.dtype) for f in fulls),
        in_specs=[HBM_SPEC] * n + [SEM_SPEC, SEM_SPEC, ANY_SPEC], out_specs=(HBM_SPEC,) * n,
        input_output_aliases={w: w for w in range(n)},
        compiler_params=pltpu.CompilerParams(has_side_effects=EFFECT),
    )(*fulls, sems[0], sems[1], after))


def _forward_to_sibling(full, kind, shard_shape, relations, name):
    nr = len(relations)

    def body(in_ref, out_ref, send_sems, recv_sems):
        x, y, c = _place()
        chips = _other_chips(x, y)
        copies = []
        for pos, j in enumerate(relations):
            k = 2 * chips[j][0] + chips[j][1]
            mine = _weight_window(in_ref, kind, shard_shape, k, c)
            cp = pltpu.make_async_remote_copy(
                src_ref=mine, dst_ref=_weight_window(out_ref, kind, shard_shape, k, c),
                send_sem=send_sems.at[pos], recv_sem=recv_sems.at[pos], device_id=(x, y, 1 - c), device_id_type=MESH)
            cp.start()
            copies.append(cp)
        for pos, j in enumerate(relations):
            theirs = _weight_window(out_ref, kind, shard_shape, 2 * chips[j][0] + chips[j][1], 1 - c)
            pltpu.make_async_remote_copy(
                src_ref=theirs, dst_ref=theirs, send_sem=send_sems.at[pos], recv_sem=recv_sems.at[pos],
                device_id=(x, y, 1 - c), device_id_type=MESH).wait_recv()
        for cp in copies:
            cp.wait_send()

    return pl.pallas_call(
        body, name=name, out_shape=jax.ShapeDtypeStruct(full.shape, full.dtype),
        in_specs=[HBM_SPEC], out_specs=HBM_SPEC, input_output_aliases={0: 0},
        scratch_shapes=[pltpu.SemaphoreType.DMA((nr,))] * 2,
    )(full)


def _grad_piece(ref, kind, h, cdim, k, half):
    if kind == "col":
        return ref.at[pl.ds(half * h, h), pl.ds(pl.multiple_of(k * cdim, 128), cdim)]
    return ref.at[k, pl.ds(half * h, h), :]


def _grad_dims(g, kind):
    return (g.shape[0] // 2, g.shape[1] // N_CHIP) if kind == "col" else (g.shape[1] // 2, g.shape[2])


def _reduce_start(grads, kinds, name):
    n = len(grads)
    dims = [_grad_dims(g, kind) for g, kind in zip(grads, kinds)]
    lands = [lax.empty((N_DEV - 1, h, cdim), g.dtype) for g, (h, cdim) in zip(grads, dims)]

    def body(*refs):
        g_ins, land_ins = refs[:n], refs[n:2 * n]
        send_sems, recv_sems = refs[2 * n], refs[2 * n + 1]
        token = refs[4 * n + 2]
        x, y, c = _place()
        for w in range(n):
            h, cdim = dims[w]
            base = (N_DEV - 1) * w
            pltpu.make_async_remote_copy(
                src_ref=_grad_piece(g_ins[w], kinds[w], h, cdim, 2 * x + y, 1 - c), dst_ref=land_ins[w].at[0],
                send_sem=send_sems.at[base], recv_sem=recv_sems.at[base],
                device_id=(x, y, 1 - c), device_id_type=MESH).start()
            for j, chip in enumerate(_other_chips(x, y)):
                for cc in range(2):
                    pltpu.make_async_remote_copy(
                        src_ref=_grad_piece(g_ins[w], kinds[w], h, cdim, 2 * chip[0] + chip[1], cc),
                        dst_ref=land_ins[w].at[1 + 2 * j + c],
                        send_sem=send_sems.at[base + 1 + 2 * j + cc], recv_sem=recv_sems.at[base + 1 + 2 * j + c],
                        device_id=(*chip, cc), device_id_type=MESH).start()
        token[...] = jnp.zeros_like(token)

    sems = pltpu.SemaphoreType.DMA(((N_DEV - 1) * n,))
    outs = pl.pallas_call(
        body, name=name,
        out_shape=(sems, sems) + tuple(pltpu.HBM(a.shape, a.dtype) for a in list(grads) + lands)
        + (jax.ShapeDtypeStruct((SUBLANE, 128), F32),),
        in_specs=[HBM_SPEC] * (2 * n), out_specs=(SEM_SPEC, SEM_SPEC) + (HBM_SPEC,) * (2 * n) + (VMEM_SPEC,),
        input_output_aliases={i: 2 + i for i in range(2 * n)},
        compiler_params=pltpu.CompilerParams(has_side_effects=EFFECT),
    )(*[_hbm(a) for a in list(grads) + lands])
    return outs[0], outs[1], list(outs[2:2 + n]), list(outs[2 + n:2 + 2 * n]), outs[2 + 2 * n]


def _reduce_wait(send_sems, recv_sems, grads, lands, kinds, after, name):
    n = len(grads)
    dims = [_grad_dims(g, kind) for g, kind in zip(grads, kinds)]

    def body(*refs):
        g_ins, land_ins = refs[:n], refs[n:2 * n]
        send_sems, recv_sems = refs[2 * n], refs[2 * n + 1]
        x, y, c = _place()
        for w in range(n):
            h, cdim = dims[w]
            piece = _grad_piece(g_ins[w], kinds[w], h, cdim, 2 * x + y, c)
            for s in range(N_DEV - 1):
                k = (N_DEV - 1) * w + s
                slot = land_ins[w].at[s]
                pltpu.make_async_remote_copy(
                    src_ref=piece, dst_ref=slot, send_sem=send_sems.at[k], recv_sem=recv_sems.at[k],
                    device_id=(x, y, 1 - c), device_id_type=MESH).wait_send()
                pltpu.make_async_remote_copy(
                    src_ref=piece, dst_ref=slot, send_sem=send_sems.at[k], recv_sem=recv_sems.at[k],
                    device_id=(x, y, 1 - c), device_id_type=MESH).wait_recv()

    outs = pl.pallas_call(
        body, name=name, out_shape=tuple(pltpu.HBM(a.shape, a.dtype) for a in list(grads) + list(lands)),
        in_specs=[HBM_SPEC] * (2 * n) + [SEM_SPEC, SEM_SPEC, ANY_SPEC], out_specs=(HBM_SPEC,) * (2 * n),
        input_output_aliases={i: i for i in range(2 * n)},
        compiler_params=pltpu.CompilerParams(has_side_effects=EFFECT),
    )(*grads, *lands, send_sems, recv_sems, after)
    return list(outs[:n]), list(outs[n:])


def _add_pieces(g, land, kind, chip_idx, core_idx, name):
    _, h, cdim = land.shape
    tr, tc = _tile(h, 256), _tile(cdim, 2048)
    nrb, ncb = h // tr, cdim // tc

    def body(k_ref, c_ref, g_ref, l_ref, o_ref):
        acc = g_ref[...].astype(F32)
        for s in range(N_DEV - 1):
            acc = acc + l_ref[s].astype(F32)
        o_ref[...] = acc

    if kind == "col":
        g_spec = pl.BlockSpec((tr, tc), lambda i, j, kr, cr: (cr[0] * nrb + i, kr[0] * ncb + j))
    else:
        g_spec = pl.BlockSpec((None, tr, tc), lambda i, j, kr, cr: (kr[0], cr[0] * nrb + i, j))
    return pl.pallas_call(
        body, name=name, out_shape=jax.ShapeDtypeStruct((2 * h, cdim), F32),
        grid_spec=pltpu.PrefetchScalarGridSpec(
            num_scalar_prefetch=2, grid=(nrb, ncb),
            in_specs=[g_spec, pl.BlockSpec((N_DEV - 1, tr, tc), lambda i, j, kr, cr: (0, i, j))],
            out_specs=pl.BlockSpec((tr, tc), lambda i, j, kr, cr: (cr[0] * nrb + i, j))),
        compiler_params=_params("parallel", "parallel"),
    )(chip_idx, core_idx, g, land)


def _join_halves(shards, name):
    n = len(shards)

    def body(*refs):
        ins, outs = refs[:n], refs[n:2 * n]
        send_sems, recv_sems = refs[2 * n:]
        x, y, c = _place()
        copies = []
        for w in range(n):
            h = shards[w].shape[0] // 2
            cp = pltpu.make_async_remote_copy(
                src_ref=ins[w].at[pl.ds(c * h, h), :], dst_ref=outs[w].at[pl.ds(c * h, h), :],
                send_sem=send_sems.at[w], recv_sem=recv_sems.at[w], device_id=(x, y, 1 - c), device_id_type=MESH)
            cp.start()
            copies.append(cp)
        for w in range(n):
            h = shards[w].shape[0] // 2
            theirs = outs[w].at[pl.ds((1 - c) * h, h), :]
            pltpu.make_async_remote_copy(
                src_ref=theirs, dst_ref=theirs, send_sem=send_sems.at[w], recv_sem=recv_sems.at[w],
                device_id=(x, y, 1 - c), device_id_type=MESH).wait_recv()
        for cp in copies:
            cp.wait_send()

    return pl.pallas_call(
        body, name=name, out_shape=tuple(jax.ShapeDtypeStruct(s.shape, s.dtype) for s in shards),
        in_specs=[HBM_SPEC] * n, out_specs=tuple([HBM_SPEC] * n),
        input_output_aliases={w: w for w in range(n)},
        scratch_shapes=[pltpu.SemaphoreType.DMA((n,))] * 2,
    )(*shards)


def _sum_devices(gathered, rows, name):
    n = gathered.shape[1]

    def body(g_ref, o_ref):
        acc = g_ref[0:rows, :]
        for dev in range(1, N_DEV):
            acc = acc + g_ref[dev * rows:(dev + 1) * rows, :]
        o_ref[...] = acc

    return pl.pallas_call(
        body, name=name, out_shape=jax.ShapeDtypeStruct((rows, n), F32),
        in_specs=[VMEM_SPEC], out_specs=VMEM_SPEC,
        compiler_params=pltpu.CompilerParams(vmem_limit_bytes=VMEM_LIMIT),
    )(gathered)


def _pack(vectors, width):
    flat = [v.reshape(-1) for v in vectors]
    offsets, total = [], 0
    for f in flat:
        offsets.append(total)
        total += f.shape[0]
    rows = -(-total // (width * SUBLANE)) * SUBLANE
    flat.append(jnp.zeros((rows * width - total,), F32))
    return jnp.concatenate(flat).reshape(rows, width), offsets


def kernel(x, c, mod_w, mod_b, norm_g, a_w_in, a_conv_w, a_conv_b, a_w_out, b_w_in, b_ln_g, b_ln_b, b_w_s, b_b_s, b_w_out, final_g, loss_target, m_mod_w, m_mod_b, m_norm_g, m_a_w_in, m_a_conv_w, m_a_conv_b, m_a_w_out, m_b_w_in, m_b_ln_g, m_b_ln_b, m_b_w_s, m_b_b_s, m_b_w_out, m_final_g, v_mod_w, v_mod_b, v_norm_g, v_a_w_in, v_a_conv_w, v_a_conv_b, v_a_w_out, v_b_w_in, v_b_ln_g, v_b_ln_b, v_b_w_s, v_b_b_s, v_b_w_out, v_final_g):
    seq, d = x.shape[1], x.shape[2]
    e = a_conv_b.shape[1]
    wd = mod_w.shape[2]
    ax, ay, ac = _place()
    chip = 2 * ax + ay
    dev = 2 * chip + ac
    chip_idx = jnp.reshape(chip, (1,)).astype(jnp.int32)
    core_idx = jnp.reshape(ac, (1,)).astype(jnp.int32)

    x2d = x[0]
    target = loss_target[0]

    w_names = ["a_w_in", "a_w_out", "b_w_in", "b_w_out"]
    w_kinds = ["col", "row", "col", "row"]
    w_shards = [a_w_in[0], a_w_out[0], b_w_in[0], b_w_out[0]]
    w_shapes = [sh.shape for sh in w_shards]
    placed = [_place_own(sh, kind, chip_idx, "place_" + nm) for nm, sh, kind in zip(w_names, w_shards, w_kinds)]
    a_groups, far_groups = [[(0, 0), (0, 1)]], [[(0, 2)]]
    whole_group = [[(0, 0), (0, 1), (0, 2)]]

    es = e // N_CHIP
    packed0, offs0 = _pack([c, a_conv_w, b_ln_g, b_ln_b], 1024)
    gathered0 = _all_gather(packed0, "gather_params").reshape(N_DEV, -1)
    c_all = gathered0[:, :d]
    per_chip = gathered0[0::2]

    def from_chips_cols(k, rows_):
        got = per_chip[:, offs0[k]:offs0[k] + rows_ * es].reshape(N_CHIP, rows_, es)
        return jnp.transpose(got, (1, 0, 2)).reshape(rows_, e)

    conv_w_full = from_chips_cols(1, 3)
    conv_w = [conv_w_full[k:k + 1] for k in range(3)]
    ln_g, ln_b = from_chips_cols(2, 1), from_chips_cols(3, 1)
    mod_b_cols = lax.dynamic_slice_in_dim(mod_b, chip * wd, wd, axis=1)[:, None, :]
    c_act, mod_part = _mod_fwd(c_all, mod_w, mod_b_cols, "mod_fwd")
    n_layer = mod_w.shape[0]
    mod_gathered = _all_gather(mod_part, "gather_mod")
    a_sems, (wa_in,), a_token = _gather_weights_start(
        placed[:1], w_kinds[:1], w_shapes[:1], a_groups, mod_gathered, "gather_a_w_in_start", both_cores=False)
    mod_all = mod_gathered.reshape(N_CHIP, 2, n_layer, N_DEV, wd)[:, 0]
    mod_all = jnp.transpose(mod_all, (1, 2, 0, 3)).reshape(n_layer, N_DEV, N_CHIP * wd)
    mod_me = lax.dynamic_index_in_dim(mod_all, dev, axis=1, keepdims=False)
    shift = [mod_me[l:l + 1, 0:d] for l in range(n_layer)]
    scale = [mod_me[l:l + 1, d:2 * d] for l in range(n_layer)]
    gate = [mod_me[l:l + 1, 2 * d:3 * d] for l in range(n_layer)]

    g0, g1, gf = norm_g[0:1], norm_g[1:2], final_g[None, :]
    h0 = _norm_mod(x2d, g0, scale[0], shift[0], "norm_mod0")

    def slab(r):
        return jnp.bitwise_xor(chip_idx, r)

    def arrived(sems, group, weight, w, after, name, both_cores=True):
        return _gather_weights_wait(sems, [(0, j) for _, j in group], [weight], [w_kinds[w]], [w_shapes[w]],
                                    after, name, both_cores)[0]

    proj0 = _mm_proj_slab(h0, wa_in, None, slab(0), 4, "a_proj_own", a_token)
    wa_in = arrived(a_sems[0], a_groups[0], wa_in, 0, proj0, "gather_wait_near", both_cores=False)
    wa_in = _forward_to_sibling(wa_in, w_kinds[0], w_shapes[0], [0, 1], "forward_near")
    far_sems, (wa_in,), far_token = _gather_weights_start(
        [wa_in], w_kinds[:1], w_shapes[:1], far_groups, a_token, "gather_a_w_in_far_start")
    proj0 = _mm_proj_slab(h0, wa_in, proj0, slab(2), 4, "a_proj_x", far_token)
    proj0 = _mm_proj_slab(h0, wa_in, proj0, slab(1), 4, "a_proj_y", far_token)
    wa_in = arrived(far_sems[0], far_groups[0], wa_in, 0, proj0, "gather_wait_far")
    def start_whole(w, after, name):
        sems, (flight,), token = _gather_weights_start(
            placed[w:w + 1], w_kinds[w:w + 1], w_shapes[w:w + 1], whole_group, after, name)
        return sems[0], flight, token

    ao_sems, wa_out, ao_token = start_whole(1, wa_in, "gather_a_w_out_start")
    proj0 = _mm_proj_slab(h0, wa_in, proj0, slab(3), 4, "a_proj_far", ao_token)
    wa_out = arrived(ao_sems, whole_group[0], wa_out, 1, proj0, "gather_wait_a_w_out")
    bi_sems, wb_in, bi_token = start_whole(2, wa_out, "gather_b_w_in_start")
    y0, br0, x1, h1 = _conv_fwd(proj0, conv_w, a_conv_b, wa_out, x2d, gate[0], g1, scale[1], shift[1],
                                "conv_fwd", bi_token)
    wb_in = arrived(bi_sems, whole_group[0], wb_in, 2, h1, "gather_wait_b_w_in")
    bo_sems, wb_out, bo_token = start_whole(3, wb_in, "gather_b_w_out_start")
    proj1 = _mm_proj(h1, wb_in, 3, "b_proj", bo_token)
    b_s_t = jnp.transpose(b_b_s[0])
    wb_out = arrived(bo_sems, whole_group[0], wb_out, 3, proj1, "gather_wait_b_w_out")
    y1, dx2, dbr1, loss_part, g_final_g, dgate1 = _gmlp_fwd_head(
        proj1, ln_g, ln_b, b_w_s[0], b_s_t, wb_out, x1, gate[1], gf, target, "gmlp_fwd_head")

    gw_b_out = _mm_dw_out(y1, dbr1, "b_out_dw")
    dproj1, g_w_s, g_b_s_t, g_ln_g, g_ln_b = _gmlp_bwd(
        proj1, dbr1, wb_out, ln_g, ln_b, b_w_s[0], jnp.swapaxes(b_w_s[0], 1, 2), b_s_t, "gmlp_bwd")
    gw_b_in = _mm_dw_in(h1, dproj1, "b_proj_dw")
    b_kinds = ["col", "row"]
    b_send, b_recv, b_grads, b_lands, b_token = _reduce_start(
        [gw_b_in, gw_b_out.reshape(N_CHIP, e // N_CHIP, d)], b_kinds, "reduce_b_start")
    dx1, dshift1, dscale1, g_g1, dbr0, dgate0 = _mm_dh_norm_bwd(
        dproj1, wb_in, x1, dx2, g1, scale[1], "b_proj_dx", b_token, br=br0, gate=gate[0])

    early = [loss_part[0, 0:1], g_final_g, g_g1, g_ln_g, g_ln_b, jnp.transpose(g_b_s_t), g_w_s,
             jnp.concatenate([dshift1, dscale1, dgate1, dgate0], axis=1)]
    packed_e, offs_e = _pack(early, 1024)
    rows_e = packed_e.shape[0]
    blocks_e = lax.dynamic_update_slice(jnp.zeros((N_DEV, rows_e, 1024), F32), packed_e[None], (dev, 0, 0))
    e_send, e_recv, blocks_e, e_token = _all_gather_start(blocks_e, "gather_small_start")

    gw_a_out = _mm_dw_out(y0, dbr0, "a_out_dw")
    ao_send, ao_recv, ao_grads, ao_lands, ao_grad_token = _reduce_start(
        [gw_a_out.reshape(N_CHIP, e // N_CHIP, d)], ["row"], "reduce_a_out_start")
    dproj0, g_w0, g_w1, g_w2, g_conv_b = _conv_bwd(
        proj0, dbr0, wa_out, conv_w, a_conv_b, "conv_bwd", ao_grad_token + e_token)
    gw_a_in = _mm_dw_in(h0, dproj0, "a_proj_dw")
    ai_send, ai_recv, ai_grads, ai_lands, ai_token = _reduce_start([gw_a_in], ["col"], "reduce_a_in_start")
    grad_x, dshift0, dscale0, g_g0 = _mm_dh_norm_bwd(dproj0, wa_in, x2d, dx1, g0, scale[0], "a_proj_dx", ai_token)

    def finish(send, recv, grads_, lands_, kinds_, names_, after, tag):
        grads_, lands_ = _reduce_wait(send, recv, grads_, lands_, kinds_, after, "reduce_" + tag + "_wait")
        halves = [_add_pieces(g, land, kind, chip_idx, core_idx, "add_pieces_" + nm)
                  for g, land, kind, nm in zip(grads_, lands_, kinds_, names_)]
        return _join_halves(halves, "join_" + tag)

    upd, big_grads = {}, {}

    def adamw_big(nm, w, g, m, v):
        g_out, *rest = _adamw_2d(w[0], g, m[0], v[0], "adamw_" + nm)
        big_grads[nm] = g_out[None]
        upd[nm] = tuple(o[None] for o in rest)

    g_b_w_in, g_b_w_out = finish(b_send, b_recv, b_grads, b_lands, b_kinds, ["b_w_in", "b_w_out"], grad_x, "b")
    adamw_big("b_w_in", b_w_in, g_b_w_in, m_b_w_in, v_b_w_in)
    adamw_big("b_w_out", b_w_out, g_b_w_out, m_b_w_out, v_b_w_out)

    late = [g_g0, jnp.concatenate([g_w0, g_w1, g_w2], axis=0), g_conv_b, jnp.concatenate([dshift0, dscale0], axis=1)]
    packed_l, offs_l = _pack(late, 1024)
    rows_l = packed_l.shape[0]
    gathered_l = _all_gather(packed_l, "gather_small_late", after=upd["b_w_out"][0])
    blocks_e = _all_gather_wait(e_send, e_recv, blocks_e, gathered_l, "gather_small_wait")
    gathered_e = blocks_e.reshape(N_DEV * rows_e, 1024)
    summed_e = _sum_devices(gathered_e, rows_e, "sum_small_early").reshape(-1)
    summed_l = _sum_devices(gathered_l, rows_l, "sum_small_late").reshape(-1)

    def take(summed, offs, k, shape):
        size = math.prod(shape)
        return summed[offs[k]:offs[k] + size].reshape(shape)

    def rows_of(gathered, rows_, offs, k, width):
        return gathered.reshape(N_DEV, rows_ * 1024)[:, offs[k]:offs[k] + width]

    loss = take(summed_e, offs_e, 0, ())
    grad_final_g = take(summed_e, offs_e, 1, (d,))
    grad_norm_g = jnp.concatenate([take(summed_l, offs_l, 0, (1, d)), take(summed_e, offs_e, 2, (1, d))], axis=0)
    grad_ln_g_full = take(summed_e, offs_e, 3, (1, e))
    grad_ln_b_full = take(summed_e, offs_e, 4, (1, e))
    grad_b_b_s = take(summed_e, offs_e, 5, (1, GROUPS, CHUNK))
    grad_b_w_s = take(summed_e, offs_e, 6, (1, GROUPS, CHUNK, CHUNK))
    grad_conv_w_full = take(summed_l, offs_l, 1, (3, e))
    grad_a_conv_b = take(summed_l, offs_l, 2, (1, e))
    grad_a_conv_w = lax.dynamic_slice_in_dim(grad_conv_w_full, chip * es, es, axis=1)[None]
    grad_b_ln_g = lax.dynamic_slice_in_dim(grad_ln_g_full, chip * es, es, axis=1)
    grad_b_ln_b = lax.dynamic_slice_in_dim(grad_ln_b_full, chip * es, es, axis=1)
    dmod_e = rows_of(gathered_e, rows_e, offs_e, 7, 4 * d)
    dmod_l = rows_of(gathered_l, rows_l, offs_l, 3, 2 * d)
    dmod_all = jnp.stack([jnp.concatenate([dmod_l, dmod_e[:, 3 * d:]], axis=1), dmod_e[:, :3 * d]], axis=1)
    mod_b_e = take(summed_e, offs_e, 7, (4 * d,))
    mod_b_l = take(summed_l, offs_l, 3, (2 * d,))
    grad_mod_b = jnp.stack([jnp.concatenate([mod_b_l, mod_b_e[3 * d:]]), mod_b_e[:3 * d]])
    dmod_cols = jnp.transpose(lax.dynamic_slice_in_dim(dmod_all, chip * wd, wd, axis=2), (1, 0, 2))

    grad_mod_w, delta_mod_w, new_m_mod_w, new_v_mod_w = _mod_w_update(
        jnp.transpose(c_act), dmod_cols, mod_w, m_mod_w, v_mod_w, "mod_w_update")
    (g_a_w_out,) = finish(ao_send, ao_recv, ao_grads, ao_lands, ["row"], ["a_w_out"], delta_mod_w, "a_out")
    adamw_big("a_w_out", a_w_out, g_a_w_out, m_a_w_out, v_a_w_out)
    (g_a_w_in,) = finish(ai_send, ai_recv, ai_grads, ai_lands, ["col"], ["a_w_in"], upd["a_w_out"][0], "a_in")
    adamw_big("a_w_in", a_w_in, g_a_w_in, m_a_w_in, v_a_w_in)
    small_w = [("mod_b", mod_b, grad_mod_b, m_mod_b, v_mod_b), ("norm_g", norm_g, grad_norm_g, m_norm_g, v_norm_g),
               ("a_conv_w", a_conv_w, grad_a_conv_w, m_a_conv_w, v_a_conv_w),
               ("a_conv_b", a_conv_b, grad_a_conv_b, m_a_conv_b, v_a_conv_b),
               ("b_ln_g", b_ln_g, grad_b_ln_g, m_b_ln_g, v_b_ln_g), ("b_ln_b", b_ln_b, grad_b_ln_b, m_b_ln_b, v_b_ln_b),
               ("b_w_s", b_w_s, grad_b_w_s, m_b_w_s, v_b_w_s), ("b_b_s", b_b_s, grad_b_b_s, m_b_b_s, v_b_b_s),
               ("final_g", final_g, grad_final_g, m_final_g, v_final_g)]

    def flat2d(a):
        return a.reshape(-1, a.shape[-1])

    res = _adamw_small([flat2d(t[1]) for t in small_w], [flat2d(t[2]) for t in small_w],
                       [flat2d(t[3]) for t in small_w], [flat2d(t[4]) for t in small_w], "adamw_small")
    for (nm, w, _, _, _), r3 in zip(small_w, res):
        upd[nm] = tuple(o.reshape(w.shape) for o in r3)
    upd["mod_w"] = (delta_mod_w, new_m_mod_w, new_v_mod_w)

    grads = {"mod_w": grad_mod_w, "mod_b": grad_mod_b, "norm_g": grad_norm_g, "a_conv_w": grad_a_conv_w,
             "a_conv_b": grad_a_conv_b, "b_ln_g": grad_b_ln_g, "b_ln_b": grad_b_ln_b, "b_w_s": grad_b_w_s,
             "b_b_s": grad_b_b_s, "final_g": grad_final_g, **big_grads}
    order = ["mod_w", "mod_b", "norm_g", "a_w_in", "a_conv_w", "a_conv_b", "a_w_out", "b_w_in", "b_ln_g", "b_ln_b",
             "b_w_s", "b_b_s", "b_w_out", "final_g"]
    return (loss, grad_x[None], *[grads[k] for k in order], *[upd[k][0] for k in order],
            *[upd[k][1] for k in order], *[upd[k][2] for k in order])
```

```python
import functools
import math

import jax
import jax.numpy as jnp
from jax import lax
from jax.experimental import pallas as pl
from jax.experimental.pallas import tpu as pltpu

F32 = jnp.float32
BF16 = jnp.bfloat16
MESH = pl.DeviceIdType.MESH

N_DEV = 8
N_CHIP = 4
SUBLANE = 8
PACK = 16
ACT = F32
RMS_EPS = 1e-6
LN_EPS = 1e-5
CHUNK = 128
GROUPS = 8
ADAM_LR = 0.001
ADAM_B1 = 0.9
ADAM_B2 = 0.999
ADAM_EPS = 1e-08
ADAM_WD = 0.01
ADAM_STEP = 10
VMEM_LIMIT = 56 << 20

HBM_SPEC = pl.BlockSpec(memory_space=pltpu.HBM)
VMEM_SPEC = pl.BlockSpec(memory_space=pltpu.VMEM)
SEM_SPEC = pl.BlockSpec(memory_space=pltpu.SEMAPHORE)
ANY_SPEC = pl.BlockSpec(memory_space=pl.ANY)
EFFECT = pltpu.SideEffectType.DATAFLOW_SIDE_EFFECTING


def _params(*sem):
    return pltpu.CompilerParams(dimension_semantics=sem, vmem_limit_bytes=VMEM_LIMIT)


def _tile(n, want):
    if n <= want:
        return n
    t = want
    while n % t:
        t -= 128
    return t


def _sigmoid(x):
    return 0.5 * jnp.tanh(0.5 * x) + 0.5


def _silu_and_grad(x):
    s = _sigmoid(x)
    return x * s, s * (1.0 + x * (1.0 - s))


def _gelu_and_grad(x):
    cdf = 0.5 * (1.0 + lax.erf(x * (1.0 / math.sqrt(2.0))))
    pdf = jnp.exp(-0.5 * x * x) * (1.0 / math.sqrt(2.0 * math.pi))
    return x * cdf, cdf + x * pdf


def _gelu(x):
    return x * (0.5 * (1.0 + lax.erf(x * (1.0 / math.sqrt(2.0)))))


def _rms(x):
    r = lax.rsqrt(jnp.mean(x * x, axis=-1, keepdims=True) + RMS_EPS)
    return x * r, r


def _rms_bwd(dxn, xn, r):
    return r * (dxn - xn * jnp.mean(dxn * xn, axis=-1, keepdims=True))


def _colsum(a):
    return jnp.sum(a, axis=0, keepdims=True)


def _shift_down(cur, before, k):
    rolled = pltpu.roll(cur, k, 0)
    row = lax.broadcasted_iota(jnp.int32, before.shape, 0)
    head = jnp.where(row < k, pltpu.roll(before, k, 0), rolled[:SUBLANE])
    return jnp.concatenate([head, rolled[SUBLANE:]], axis=0)


def _shift_up(cur, after, k):
    n = cur.shape[0]
    rolled = pltpu.roll(cur, n - k, 0)
    row = lax.broadcasted_iota(jnp.int32, after.shape, 0)
    tail = jnp.where(row >= SUBLANE - k, pltpu.roll(after, SUBLANE - k, 0), rolled[n - SUBLANE:])
    return jnp.concatenate([rolled[:n - SUBLANE], tail], axis=0)


def _after_spec():
    return pl.BlockSpec((SUBLANE, 128), lambda *_: (0, 0))


def _mm_proj(h, w, n_split, name, after):
    s, d = h.shape
    e = w.shape[1] // n_split
    tm, tn = _tile(s, 1024), _tile(e, 2048)
    nj = e // tn

    def body(h_ref, w_ref, after_ref, o_ref):
        o_ref[...] = jnp.dot(h_ref[...], w_ref[...], preferred_element_type=F32).astype(ACT)

    return pl.pallas_call(
        body, name=name,
        out_shape=jax.ShapeDtypeStruct((n_split, s, e), ACT),
        grid=(s // tm, n_split * nj),
        in_specs=[pl.BlockSpec((tm, d), lambda i, j: (i, 0)), pl.BlockSpec((d, tn), lambda i, j: (0, j)),
                  _after_spec()],
        out_specs=pl.BlockSpec((None, tm, tn), lambda i, j: (j // nj, i, j % nj)),
        compiler_params=_params("parallel", "parallel"),
    )(h, w, after)


def _mm_proj_slab(h, w, proj, q_idx, n_split, name, after):
    s, d = h.shape
    e = w.shape[1] // n_split
    tm, tn = _tile(s, 1024), _tile(e, 2048)
    nj = e // tn

    def body(q_ref, h_ref, w_ref, *rest):
        o_ref = rest[-1]
        o_ref[...] = jnp.dot(h_ref[...], w_ref[...], preferred_element_type=F32).astype(ACT)

    in_specs = [pl.BlockSpec((tm, d), lambda i, j, qr: (i, 0)), pl.BlockSpec((d, tn), lambda i, j, qr: (0, qr[0] * nj + j)),
                _after_spec()]
    args = [q_idx, h, w, after]
    aliases = {}
    if proj is not None:
        in_specs.append(ANY_SPEC)
        args.append(proj)
        aliases = {4: 0}
    return pl.pallas_call(
        body, name=name,
        out_shape=jax.ShapeDtypeStruct((n_split, s, e), ACT),
        grid_spec=pltpu.PrefetchScalarGridSpec(
            num_scalar_prefetch=1, grid=(s // tm, nj), in_specs=in_specs,
            out_specs=pl.BlockSpec((None, tm, tn), lambda i, j, qr: (qr[0], i, j))),
        input_output_aliases=aliases,
        compiler_params=_params("parallel", "parallel"),
    )(*args)


def _mm_dh_norm_bwd(dp, w, x, dx_in, g, scale, name, after, br=None, gate=None):
    nq, s, e = dp.shape
    d = w.shape[0]
    has_branch = br is not None
    tm, tk = _tile(s, 1024), _tile(e, 1024 if has_branch else 2048)
    nkq = e // tk
    nk = nq * nkq

    def body(*refs):
        if has_branch:
            (a_ref, b_ref, after_ref, x_ref, dxin_ref, g_ref, sc_ref, br_ref, gate_ref,
             dx_ref, dsh_ref, dsc_ref, dg_ref, dbr_ref, dgate_ref, acc_ref) = refs
        else:
            (a_ref, b_ref, after_ref, x_ref, dxin_ref, g_ref, sc_ref,
             dx_ref, dsh_ref, dsc_ref, dg_ref, acc_ref) = refs
        i, k = pl.program_id(0), pl.program_id(1)

        @pl.when(jnp.logical_and(i == 0, k == 0))
        def _():
            dsh_ref[...] = jnp.zeros_like(dsh_ref)
            dsc_ref[...] = jnp.zeros_like(dsc_ref)
            dg_ref[...] = jnp.zeros_like(dg_ref)
            if has_branch:
                dgate_ref[...] = jnp.zeros_like(dgate_ref)

        @pl.when(k == 0)
        def _():
            acc_ref[...] = jnp.zeros_like(acc_ref)
        acc_ref[...] += lax.dot_general(a_ref[...], b_ref[...], (((1,), (1,)), ((), ())), preferred_element_type=F32)

        @pl.when(k == nk - 1)
        def _():
            dh = acc_ref[...]
            g_vec = g_ref[...]
            xn, r = _rms(x_ref[...])
            dsh_ref[...] += _colsum(dh)
            dsc_ref[...] += _colsum(dh * (xn * g_vec))
            da = dh * (1.0 + sc_ref[...])
            dg_ref[...] += _colsum(da * xn)
            dx = dxin_ref[...] + _rms_bwd(da * g_vec, xn, r)
            dx_ref[...] = dx
            if has_branch:
                dgate_ref[...] += _colsum(dx * br_ref[...])
                dbr_ref[...] = (gate_ref[...] * dx).astype(BF16)

    rows = pl.BlockSpec((tm, d), lambda i, k: (i, 0))
    vec = pl.BlockSpec((1, d), lambda i, k: (0, 0))
    vec_out = jax.ShapeDtypeStruct((1, d), F32)
    in_specs = [pl.BlockSpec((None, tm, tk), lambda i, k: (k // nkq, i, k % nkq)),
                pl.BlockSpec((d, tk), lambda i, k: (0, k)), _after_spec(), rows, rows, vec, vec]
    out_shape = [jax.ShapeDtypeStruct((s, d), F32), vec_out, vec_out, vec_out]
    out_specs = [rows, vec, vec, vec]
    args = [dp, w, after, x, dx_in, g, scale]
    if has_branch:
        in_specs += [rows, vec]
        out_shape += [jax.ShapeDtypeStruct((s, d), BF16), vec_out]
        out_specs += [rows, vec]
        args += [br, gate]
    return pl.pallas_call(
        body, name=name, out_shape=tuple(out_shape), grid=(s // tm, nk),
        in_specs=in_specs, out_specs=tuple(out_specs), scratch_shapes=[pltpu.VMEM((tm, d), F32)],
        compiler_params=_params("arbitrary", "arbitrary"),
    )(*args)


def _dw_body(n_t):
    def body(a_ref, b_ref, o_ref, acc_ref):
        t = pl.program_id(2)

        @pl.when(t == 0)
        def _():
            acc_ref[...] = jnp.zeros_like(acc_ref)
        acc_ref[...] += lax.dot_general(a_ref[...], b_ref[...], (((0,), (0,)), ((), ())), preferred_element_type=F32)

        @pl.when(t == n_t - 1)
        def _():
            o_ref[...] = acc_ref[...].astype(o_ref.dtype)
    return body


def _mm_dw_in(h, dp, name):
    s, d = h.shape
    nq, _, e = dp.shape
    tm, tn, tt = _tile(d, 1024), _tile(e, 2048), _tile(s, 2048)
    nj = e // tn

    return pl.pallas_call(
        _dw_body(s // tt), name=name,
        out_shape=jax.ShapeDtypeStruct((d, nq * e), BF16),
        grid=(d // tm, nq * nj, s // tt),
        in_specs=[pl.BlockSpec((tt, tm), lambda i, j, t: (t, i)),
                  pl.BlockSpec((None, tt, tn), lambda i, j, t: (j // nj, t, j % nj))],
        out_specs=pl.BlockSpec((tm, tn), lambda i, j, t: (i, j)),
        scratch_shapes=[pltpu.VMEM((tm, tn), F32)],
        compiler_params=_params("parallel", "parallel", "arbitrary"),
    )(h, dp)


def _mm_dw_out(y, dbr, name):
    s, e = y.shape
    d = dbr.shape[1]
    tm, tn, tt = _tile(e, 1024), _tile(d, 1024), _tile(s, 2048)

    return pl.pallas_call(
        _dw_body(s // tt), name=name,
        out_shape=jax.ShapeDtypeStruct((e, d), BF16),
        grid=(e // tm, d // tn, s // tt),
        in_specs=[pl.BlockSpec((tt, tm), lambda i, j, t: (t, i)), pl.BlockSpec((tt, tn), lambda i, j, t: (t, j))],
        out_specs=pl.BlockSpec((tm, tn), lambda i, j, t: (i, j)),
        scratch_shapes=[pltpu.VMEM((tm, tn), F32)],
        compiler_params=_params("parallel", "parallel", "arbitrary"),
    )(y, dbr)


def _row_spec(ts, d):
    return pl.BlockSpec((ts, d), lambda i: (i, 0))


def _vec_spec(d):
    return pl.BlockSpec((1, d), lambda i: (0, 0))


def _norm_mod(x, g, scale, shift, name):
    s, d = x.shape
    ts = _tile(s, 512)

    def body(x_ref, g_ref, sc_ref, sh_ref, h_ref):
        xn, _ = _rms(x_ref[...])
        h_ref[...] = ((xn * g_ref[...]) * (1.0 + sc_ref[...]) + sh_ref[...]).astype(BF16)

    return pl.pallas_call(
        body, name=name, out_shape=jax.ShapeDtypeStruct((s, d), BF16), grid=(s // ts,),
        in_specs=[_row_spec(ts, d), _vec_spec(d), _vec_spec(d), _vec_spec(d)],
        out_specs=_row_spec(ts, d), compiler_params=_params("parallel"),
    )(x, g, scale, shift)


def _head_tile(x1, br, gate, gf, target):
    d = x1.shape[-1]
    xn, r = _rms(x1 + gate * br)
    err = xn * gf - target
    loss = 0.5 * jnp.sum(jnp.mean(err * err, axis=-1, keepdims=True))
    dout = err * (1.0 / d)
    dx = _rms_bwd(dout * gf, xn, r)
    return dx, gate * dx, loss, _colsum(dout * xn), _colsum(dx * br)


def _f32(ref):
    return ref[...].astype(F32)


def _cols_f32(ref, cols):
    return ref[:, cols].astype(F32)


def _rows_before(halo_ref, cols=slice(None)):
    return _cols_f32(halo_ref, cols)[PACK - SUBLANE:]


def _rows_after(halo_ref, cols=slice(None)):
    return _cols_f32(halo_ref, cols)[:SUBLANE]


CONV_CHAIN_COLS = 512


def _conv_fwd(proj, conv_w, conv_b, w_out, x, gate, g_next, scale_next, shift_next, name, after):
    _, s, e = proj.shape
    d = w_out.shape[1]
    ts, te = _tile(s, 256), e
    cw = min(CONV_CHAIN_COLS, te)
    hb = ts // PACK

    def body(bg_ref, cg_ref, xi_ref, z_ref, cgp_ref, xip_ref, w0_ref, w1_ref, w2_ref, b_ref, wo_ref, after_ref,
             x_ref, gate_ref, g_ref, sc_ref, sh_ref, y_ref, br_ref, x1_ref, h_ref):
        first = pl.program_id(0) == 0
        br = None
        for c0 in range(0, te, cw):
            cols = slice(c0, c0 + cw)
            cx = _cols_f32(cg_ref, cols) * _cols_f32(xi_ref, cols)
            before = jnp.where(first, 0.0, _rows_before(cgp_ref, cols) * _rows_before(xip_ref, cols))
            conv = b_ref[:, cols] + w2_ref[:, cols] * cx
            conv = conv + w0_ref[:, cols] * _shift_down(cx, before, 2)
            conv = conv + w1_ref[:, cols] * _shift_down(cx, before, 1)
            z = _cols_f32(z_ref, cols)
            y = ((z * _sigmoid(z)) * _cols_f32(bg_ref, cols) * conv).astype(BF16)
            y_ref[:, cols] = y
            part_br = jnp.dot(y, wo_ref[cols, :], preferred_element_type=F32)
            br = part_br if br is None else br + part_br
        br_ref[...] = br
        x1 = x_ref[...] + gate_ref[...] * br
        x1_ref[...] = x1
        xn, _ = _rms(x1)
        h_ref[...] = ((xn * g_ref[...]) * (1.0 + sc_ref[...]) + sh_ref[...]).astype(BF16)

    def part(q):
        return pl.BlockSpec((None, ts, te), lambda i: (q, i, 0))

    def halo_before(q):
        return pl.BlockSpec((None, PACK, te), lambda i: (q, jnp.maximum(i * hb - 1, 0), 0))

    return pl.pallas_call(
        body, name=name,
        out_shape=(jax.ShapeDtypeStruct((s, e), BF16), jax.ShapeDtypeStruct((s, d), F32),
                   jax.ShapeDtypeStruct((s, d), F32), jax.ShapeDtypeStruct((s, d), BF16)), grid=(s // ts,),
        in_specs=[part(0), part(1), part(2), part(3), halo_before(1), halo_before(2)]
        + [_vec_spec(e)] * 4 + [pl.BlockSpec((e, d), lambda i: (0, 0)), _after_spec(), _row_spec(ts, d)]
        + [_vec_spec(d)] * 4,
        out_specs=(_row_spec(ts, e), _row_spec(ts, d), _row_spec(ts, d), _row_spec(ts, d)),
        compiler_params=_params("parallel"),
    )(proj, proj, proj, proj, proj, proj, *conv_w, conv_b, w_out, after, x, gate, g_next, scale_next, shift_next)


def _conv_bwd(proj, dbr, w_out, conv_w, conv_b, name, after):
    _, s, e = proj.shape
    d = dbr.shape[1]
    ts, te = _tile(s, 512), _tile(e, 1024)
    cw = min(CONV_CHAIN_COLS // 2, te)
    hb = ts // PACK
    n_i = s // ts
    last_halo = s // PACK - 1
    nt = (((1,), (1,)), ((), ()))

    def body(bg_ref, cg_ref, xi_ref, z_ref, dbr_ref, cgp_ref, xip_ref, bgn_ref, zn_ref, dbrn_ref, wo_ref,
             w0_ref, w1_ref, w2_ref, b_ref, after_ref, dp_ref, dw0_ref, dw1_ref, dw2_ref, db_ref):
        i = pl.program_id(1)

        @pl.when(i == 0)
        def _():
            for acc in (dw0_ref, dw1_ref, dw2_ref, db_ref):
                acc[...] = jnp.zeros_like(acc)
        for c0 in range(0, te, cw):
            cols = slice(c0, c0 + cw)
            wo = wo_ref[cols, :]
            dy = lax.dot_general(dbr_ref[...], wo, nt, preferred_element_type=F32)
            dyn = lax.dot_general(dbrn_ref[...], wo, nt, preferred_element_type=F32)[:SUBLANE]
            bg, cg = _cols_f32(bg_ref, cols), _cols_f32(cg_ref, cols)
            xi, z = _cols_f32(xi_ref, cols), _cols_f32(z_ref, cols)
            w0, w1, w2 = w0_ref[:, cols], w1_ref[:, cols], w2_ref[:, cols]
            cx = cg * xi
            before = jnp.where(i > 0, _rows_before(cgp_ref, cols) * _rows_before(xip_ref, cols), 0.0)
            cx1 = _shift_down(cx, before, 1)
            cx2 = _shift_down(cx, before, 2)
            conv = b_ref[:, cols] + w2 * cx
            conv = conv + w0 * cx2
            conv = conv + w1 * cx1
            sz, dsz = _silu_and_grad(z)
            dp_ref[3, :, cols] = (dy * bg * conv * dsz).astype(BF16)
            dp_ref[0, :, cols] = (dy * sz * conv).astype(BF16)
            dconv = dy * sz * bg
            zn = _rows_after(zn_ref, cols)
            after = jnp.where(i < n_i - 1, dyn * (zn * _sigmoid(zn)) * _rows_after(bgn_ref, cols), 0.0)
            db_ref[:, cols] += _colsum(dconv)
            dw2_ref[:, cols] += _colsum(dconv * cx)
            dw1_ref[:, cols] += _colsum(dconv * cx1)
            dw0_ref[:, cols] += _colsum(dconv * cx2)
            dcx = w2 * dconv + w1 * _shift_up(dconv, after, 1) + w0 * _shift_up(dconv, after, 2)
            dp_ref[1, :, cols] = (dcx * xi).astype(BF16)
            dp_ref[2, :, cols] = (dcx * cg).astype(BF16)

    def part(q):
        return pl.BlockSpec((None, ts, te), lambda j, i: (q, i, j))

    def halo_before(q):
        return pl.BlockSpec((None, PACK, te), lambda j, i: (q, jnp.maximum(i * hb - 1, 0), j))

    def halo_after(q):
        return pl.BlockSpec((None, PACK, te), lambda j, i: (q, jnp.minimum((i + 1) * hb, last_halo), j))

    return pl.pallas_call(
        body, name=name,
        out_shape=(jax.ShapeDtypeStruct((4, s, e), BF16),) + (jax.ShapeDtypeStruct((1, e), F32),) * 4,
        grid=(e // te, n_i),
        in_specs=[part(0), part(1), part(2), part(3), pl.BlockSpec((ts, d), lambda j, i: (i, 0)),
                  halo_before(1), halo_before(2), halo_after(0), halo_after(3),
                  pl.BlockSpec((PACK, d), lambda j, i: (jnp.minimum((i + 1) * hb, last_halo), 0)),
                  pl.BlockSpec((te, d), lambda j, i: (j, 0))]
        + [pl.BlockSpec((1, te), lambda j, i: (0, j))] * 4 + [_after_spec()],
        out_specs=(pl.BlockSpec((4, ts, te), lambda j, i: (0, i, j)),) + (pl.BlockSpec((1, te), lambda j, i: (0, j)),) * 4,
        compiler_params=_params("parallel", "arbitrary"),
    )(proj, proj, proj, proj, dbr, proj, proj, proj, proj, dbr, w_out, *conv_w, conv_b, after)


def _tril(w):
    row = lax.broadcasted_iota(jnp.int32, w.shape, 0)
    col = lax.broadcasted_iota(jnp.int32, w.shape, 1)
    return jnp.where(row >= col, w, 0.0)


def _triu(w):
    row = lax.broadcasted_iota(jnp.int32, w.shape, 0)
    col = lax.broadcasted_iota(jnp.int32, w.shape, 1)
    return jnp.where(row <= col, w, 0.0)


def _layer_norm_fwd(v, g, b):
    mu = jnp.mean(v, axis=-1, keepdims=True)
    vc = v - mu
    rstd = lax.rsqrt(jnp.mean(vc * vc, axis=-1, keepdims=True) + LN_EPS)
    vhat = vc * rstd
    return vhat * g + b, vhat, rstd


GMLP_CHUNKS_PER_STEP = 2


def _gmlp_rows(s):
    return CHUNK * min(GMLP_CHUNKS_PER_STEP, s // CHUNK)


def _mix_positions(w_ref, src_scr, dst_scr, gw, mask, bias_ref=None):
    for gi in range(GROUPS):
        cols = slice(gi * gw, (gi + 1) * gw)
        wm = mask(w_ref[gi]).astype(BF16)
        for n in range(src_scr.shape[0] // CHUNK):
            rows = slice(n * CHUNK, (n + 1) * CHUNK)
            out = jnp.dot(wm, src_scr[rows, cols], preferred_element_type=F32)
            if bias_ref is not None:
                out = out + bias_ref[:, gi:gi + 1]
            dst_scr[rows, cols] = out


def _gmlp_fwd_head(proj, ln_g, ln_b, w_s, b_s_t, w_out, x1, gate, gf, target, name):
    _, s, e = proj.shape
    d = w_out.shape[1]
    gw = e // GROUPS
    ts = _gmlp_rows(s)

    def body(pu_ref, pv_ref, pz_ref, g_ref, b_ref, ws_ref, bs_ref, wo_ref, x1_ref, gate_ref, gf_ref, tg_ref,
             y_ref, dx_ref, dbr_ref, loss_ref, dgf_ref, dgate_ref, vn_scr, mix_scr):
        @pl.when(pl.program_id(0) == 0)
        def _():
            loss_ref[...] = jnp.zeros_like(loss_ref)
            dgf_ref[...] = jnp.zeros_like(dgf_ref)
            dgate_ref[...] = jnp.zeros_like(dgate_ref)
        vn, _, _ = _layer_norm_fwd(_gelu(_f32(pv_ref)), g_ref[...], b_ref[...])
        vn_scr[...] = vn.astype(BF16)
        _mix_positions(ws_ref, vn_scr, mix_scr, gw, _tril, bs_ref)
        z = _f32(pz_ref)
        y = ((z * _sigmoid(z)) * (_gelu(_f32(pu_ref)) * mix_scr[...])).astype(BF16)
        y_ref[...] = y
        br = jnp.dot(y, wo_ref[...], preferred_element_type=F32)
        dx, dbr, loss, dgf, dgate = _head_tile(x1_ref[...], br, gate_ref[...], gf_ref[...], tg_ref[...])
        dx_ref[...] = dx
        dbr_ref[...] = dbr.astype(BF16)
        loss_ref[...] += loss
        dgf_ref[...] += dgf
        dgate_ref[...] += dgate

    def part(q):
        return pl.BlockSpec((None, ts, e), lambda i: (q, i, 0))

    return pl.pallas_call(
        body, name=name,
        out_shape=(jax.ShapeDtypeStruct((s, e), BF16), jax.ShapeDtypeStruct((s, d), F32),
                   jax.ShapeDtypeStruct((s, d), BF16), jax.ShapeDtypeStruct((SUBLANE, 128), F32),
                   jax.ShapeDtypeStruct((1, d), F32), jax.ShapeDtypeStruct((1, d), F32)),
        grid=(s // ts,),
        in_specs=[part(0), part(1), part(2), _vec_spec(e), _vec_spec(e),
                  pl.BlockSpec((GROUPS, CHUNK, CHUNK), lambda i: (0, 0, 0)),
                  pl.BlockSpec((CHUNK, GROUPS), lambda i: (0, 0)), pl.BlockSpec((e, d), lambda i: (0, 0)),
                  _row_spec(ts, d), _vec_spec(d), _vec_spec(d), _row_spec(ts, d)],
        out_specs=(_row_spec(ts, e), _row_spec(ts, d), _row_spec(ts, d),
                   pl.BlockSpec((SUBLANE, 128), lambda i: (0, 0)), _vec_spec(d), _vec_spec(d)),
        scratch_shapes=[pltpu.VMEM((ts, e), BF16), pltpu.VMEM((ts, e), F32)],
        compiler_params=_params("arbitrary"),
    )(proj, proj, proj, ln_g, ln_b, w_s, b_s_t, w_out, x1, gate, gf, target)


def _gmlp_bwd(proj, dbr, w_out, ln_g, ln_b, w_s, w_s_t, b_s_t, name):
    _, s, e = proj.shape
    d = dbr.shape[1]
    gw = e // GROUPS
    ts = _gmlp_rows(s)
    n_i = s // ts

    def body(pu_ref, pv_ref, pz_ref, dbr_ref, wo_ref, g_ref, b_ref, ws_ref, wst_ref, bs_ref,
             dp_ref, dws_ref, dbs_ref, dlg_ref, dlb_ref, vn_scr, mix_scr, dm_scr, dvn_scr, dmacc_scr):
        i = pl.program_id(0)

        @pl.when(i == 0)
        def _():
            dws_ref[...] = jnp.zeros_like(dws_ref)
            dlg_ref[...] = jnp.zeros_like(dlg_ref)
            dlb_ref[...] = jnp.zeros_like(dlb_ref)
            dmacc_scr[...] = jnp.zeros_like(dmacc_scr)
        ln_g = g_ref[...]
        u, du_dpu = _gelu_and_grad(_f32(pu_ref))
        v, dv_dpv = _gelu_and_grad(_f32(pv_ref))
        vn, vhat, rstd = _layer_norm_fwd(v, ln_g, b_ref[...])
        vn_scr[...] = vn.astype(BF16)
        _mix_positions(ws_ref, vn_scr, mix_scr, gw, _tril, bs_ref)
        mixed = mix_scr[...]
        dy = lax.dot_general(dbr_ref[...], wo_ref[...], (((1,), (1,)), ((), ())), preferred_element_type=F32)
        sz, dsz = _silu_and_grad(_f32(pz_ref))
        ds = dy * sz
        dp_ref[2] = (dy * (u * mixed) * dsz).astype(BF16)
        dp_ref[0] = (ds * mixed * du_dpu).astype(BF16)
        dm = ds * u
        dm_scr[...] = dm.astype(BF16)
        for n in range(ts // CHUNK):
            rows = slice(n * CHUNK, (n + 1) * CHUNK)
            dmacc_scr[...] += dm[rows]
            for gi in range(GROUPS):
                cols = slice(gi * gw, (gi + 1) * gw)
                dws_ref[gi] += lax.dot_general(dm_scr[rows, cols], vn_scr[rows, cols], (((1,), (1,)), ((), ())),
                                               preferred_element_type=F32)
        _mix_positions(wst_ref, dm_scr, dvn_scr, gw, _triu)
        dvn = dvn_scr[...]
        dlg_ref[...] += _colsum(dvn * vhat)
        dlb_ref[...] += _colsum(dvn)
        dvh = dvn * ln_g
        dv = rstd * (dvh - jnp.mean(dvh, axis=-1, keepdims=True) - vhat * jnp.mean(dvh * vhat, axis=-1, keepdims=True))
        dp_ref[1] = (dv * dv_dpv).astype(BF16)

        @pl.when(i == n_i - 1)
        def _():
            for gi in range(GROUPS):
                dws_ref[gi] = _tril(dws_ref[gi])
                dbs_ref[:, gi:gi + 1] = jnp.sum(dmacc_scr[:, gi * gw:(gi + 1) * gw], axis=1, keepdims=True)

    def part(q):
        return pl.BlockSpec((None, ts, e), lambda i: (q, i, 0))

    w_spec = pl.BlockSpec((GROUPS, CHUNK, CHUNK), lambda i: (0, 0, 0))
    bs_spec = pl.BlockSpec((CHUNK, GROUPS), lambda i: (0, 0))
    return pl.pallas_call(
        body, name=name,
        out_shape=(jax.ShapeDtypeStruct((3, s, e), BF16), jax.ShapeDtypeStruct((GROUPS, CHUNK, CHUNK), F32),
                   jax.ShapeDtypeStruct((CHUNK, GROUPS), F32), jax.ShapeDtypeStruct((1, e), F32),
                   jax.ShapeDtypeStruct((1, e), F32)),
        grid=(n_i,),
        in_specs=[part(0), part(1), part(2), pl.BlockSpec((ts, d), lambda i: (i, 0)),
                  pl.BlockSpec((e, d), lambda i: (0, 0)), _vec_spec(e), _vec_spec(e), w_spec, w_spec, bs_spec],
        out_specs=(pl.BlockSpec((3, ts, e), lambda i: (0, i, 0)), w_spec, bs_spec, _vec_spec(e), _vec_spec(e)),
        scratch_shapes=[pltpu.VMEM((ts, e), BF16), pltpu.VMEM((ts, e), F32), pltpu.VMEM((ts, e), BF16),
                        pltpu.VMEM((ts, e), F32), pltpu.VMEM((CHUNK, e), F32)],
        compiler_params=_params("arbitrary"),
    )(proj, proj, proj, dbr, w_out, ln_g, ln_b, w_s, w_s_t, b_s_t)


def _mod_fwd(c_all, mod_w, mod_b_cols, name):
    n_layer, d, w = mod_w.shape

    def body(c_ref, w_ref, b_ref, ca_ref, o_ref):
        c = c_ref[...]
        ca = c * _sigmoid(c)
        ca_ref[...] = ca
        for li in range(n_layer):
            o_ref[li * N_DEV:(li + 1) * N_DEV, :] = (
                jnp.dot(ca, w_ref[li], preferred_element_type=F32, precision=lax.Precision.HIGHEST) + b_ref[li])

    return pl.pallas_call(
        body, name=name,
        out_shape=(jax.ShapeDtypeStruct((N_DEV, d), F32), jax.ShapeDtypeStruct((n_layer * N_DEV, w), F32)),
        in_specs=[VMEM_SPEC] * 3, out_specs=(VMEM_SPEC, VMEM_SPEC),
        compiler_params=pltpu.CompilerParams(vmem_limit_bytes=VMEM_LIMIT),
    )(c_all, mod_w, mod_b_cols)


def _adamw(w, g, m, v):
    m = ADAM_B1 * m + (1.0 - ADAM_B1) * g
    v = ADAM_B2 * v + (1.0 - ADAM_B2) * (g * g)
    m_hat = m / (1.0 - ADAM_B1 ** ADAM_STEP)
    v_hat = v / (1.0 - ADAM_B2 ** ADAM_STEP)
    delta = -ADAM_LR * (m_hat / (jnp.sqrt(v_hat) + ADAM_EPS) + ADAM_WD * w)
    return delta, m, v


def _adamw_2d(w, g, m, v, name):
    r, c = w.shape
    tr, tc = _tile(r, 512), _tile(c, 1024)

    def body(w_ref, g_ref, m_ref, v_ref, go_ref, d_ref, nm_ref, nv_ref):
        g = g_ref[...]
        go_ref[...] = g
        d_ref[...], nm_ref[...], nv_ref[...] = _adamw(w_ref[...], g, m_ref[...], v_ref[...])

    spec = pl.BlockSpec((tr, tc), lambda i, j: (i, j))
    shape = jax.ShapeDtypeStruct((r, c), F32)
    return pl.pallas_call(
        body, name=name, out_shape=(shape,) * 4, grid=(r // tr, c // tc),
        in_specs=[spec] * 4, out_specs=(spec,) * 4, compiler_params=_params("parallel", "parallel"),
    )(w, g, m, v)


def _mod_w_update(ca_t, dmod_cols, w, m, v, name):
    n_layer, d, wd = w.shape
    tr = _tile(d, 256)

    def body(ca_ref, dm_ref, w_ref, m_ref, v_ref, g_ref, d_ref, nm_ref, nv_ref):
        ca = ca_ref[...]
        dm = dm_ref[...]
        g = ca[:, 0:1] * dm[0:1, :]
        for b in range(1, N_DEV):
            g = g + ca[:, b:b + 1] * dm[b:b + 1, :]
        g_ref[...] = g
        d_ref[...], nm_ref[...], nv_ref[...] = _adamw(w_ref[...], g, m_ref[...], v_ref[...])

    spec = pl.BlockSpec((None, tr, wd), lambda l, i: (l, i, 0))
    shape = jax.ShapeDtypeStruct((n_layer, d, wd), F32)
    return pl.pallas_call(
        body, name=name, out_shape=(shape,) * 4, grid=(n_layer, d // tr),
        in_specs=[pl.BlockSpec((tr, N_DEV), lambda l, i: (i, 0)), pl.BlockSpec((None, N_DEV, wd), lambda l, i: (l, 0, 0)),
                  spec, spec, spec],
        out_specs=(spec,) * 4, compiler_params=_params("parallel", "parallel"),
    )(ca_t, dmod_cols, w, m, v)


def _adamw_small(ws, gs, ms, vs, name):
    n = len(ws)

    def body(*refs):
        ins, outs = refs[:4 * n], refs[4 * n:]
        for k in range(n):
            delta, nm, nv = _adamw(ins[k][...], ins[n + k][...], ins[2 * n + k][...], ins[3 * n + k][...])
            outs[3 * k][...] = delta
            outs[3 * k + 1][...] = nm
            outs[3 * k + 2][...] = nv

    out_shape = []
    for w in ws:
        out_shape += [jax.ShapeDtypeStruct(w.shape, F32)] * 3
    outs = pl.pallas_call(
        body, name=name, out_shape=tuple(out_shape),
        in_specs=[VMEM_SPEC] * (4 * n), out_specs=tuple([VMEM_SPEC] * (3 * n)),
        compiler_params=pltpu.CompilerParams(vmem_limit_bytes=VMEM_LIMIT),
    )(*ws, *gs, *ms, *vs)
    return [tuple(outs[3 * k:3 * k + 3]) for k in range(n)]


def _place():
    return lax.axis_index("x"), lax.axis_index("y"), lax.axis_index("c")


def _other_chips(x, y):
    return [(1 - x, y), (x, 1 - y), (1 - x, 1 - y)]


def _all_gather(block, name, after=None):
    m_per, n = block.shape

    def body(x_ref, *rest):
        out_ref, send_sems, recv_sems, local_sem = rest[-4:]
        x, y, c = _place()
        me, sibling = (x, y, c), (x, y, 1 - c)
        chips = _other_chips(x, y)

        def rows(px, py, pc):
            return out_ref.at[pl.ds((4 * px + 2 * py + pc) * m_per, m_per), :]

        def copy(k, blk, to, src=None):
            return pltpu.make_async_remote_copy(
                src_ref=rows(*blk) if src is None else src, dst_ref=rows(*blk),
                send_sem=send_sems.at[k], recv_sem=recv_sems.at[k], device_id=to, device_id_type=MESH)

        mine = pltpu.make_async_copy(x_ref, rows(*me), local_sem)
        mine.start()
        first = [copy(0, me, sibling, src=x_ref)]
        first += [copy(1 + j, me, (*chip, c), src=x_ref) for j, chip in enumerate(chips)]
        for cp in first:
            cp.start()
        passed = [copy(4 + j, (*chip, c), sibling) for j, chip in enumerate(chips)]
        for j, chip in enumerate(chips):
            copy(1 + j, (*chip, c), me).wait_recv()
            passed[j].start()
        copy(0, sibling, me).wait_recv()
        for j, chip in enumerate(chips):
            copy(4 + j, (*chip, 1 - c), me).wait_recv()
        for cp in first + passed:
            cp.wait_send()
        mine.wait()

    return pl.pallas_call(
        body, name=name, out_shape=jax.ShapeDtypeStruct((N_DEV * m_per, n), F32),
        in_specs=[VMEM_SPEC] + ([] if after is None else [ANY_SPEC]), out_specs=VMEM_SPEC,
        scratch_shapes=[pltpu.SemaphoreType.DMA((7,)), pltpu.SemaphoreType.DMA((7,)), pltpu.SemaphoreType.DMA],
        compiler_params=pltpu.CompilerParams(vmem_limit_bytes=VMEM_LIMIT),
    )(*([block] if after is None else [block, after]))


def _hbm(a):
    return pltpu.with_memory_space_constraint(a, pltpu.HBM)


def _all_gather_start(blocks, name):
    def body(in_ref, send_sems, recv_sems, thru_ref, token):
        token[...] = jnp.zeros_like(token)
        x, y, c = _place()
        mine = in_ref.at[4 * x + 2 * y + c]
        pltpu.make_async_remote_copy(
            src_ref=mine, dst_ref=mine, send_sem=send_sems.at[0], recv_sem=recv_sems.at[0],
            device_id=(x, y, 1 - c), device_id_type=MESH).start()
        for j, chip in enumerate(_other_chips(x, y)):
            for cc in range(2):
                pltpu.make_async_remote_copy(
                    src_ref=mine, dst_ref=mine, send_sem=send_sems.at[1 + 2 * j + cc],
                    recv_sem=recv_sems.at[1 + 2 * j + c], device_id=(*chip, cc), device_id_type=MESH).start()

    sems = pltpu.SemaphoreType.DMA((N_DEV - 1,))
    return pl.pallas_call(
        body, name=name,
        out_shape=(sems, sems, pltpu.HBM(blocks.shape, blocks.dtype), jax.ShapeDtypeStruct((SUBLANE, 128), F32)),
        in_specs=[HBM_SPEC], out_specs=(SEM_SPEC, SEM_SPEC, HBM_SPEC, VMEM_SPEC), input_output_aliases={0: 2},
        compiler_params=pltpu.CompilerParams(has_side_effects=EFFECT),
    )(_hbm(blocks))


def _all_gather_wait(send_sems, recv_sems, blocks, after, name):
    def body(in_ref, send_sems, recv_sems, after_ref, out_ref):
        x, y, c = _place()
        one = in_ref.at[0]
        for k in range(N_DEV - 1):
            pltpu.make_async_remote_copy(
                src_ref=one, dst_ref=one, send_sem=send_sems.at[k], recv_sem=recv_sems.at[k],
                device_id=(x, y, 1 - c), device_id_type=MESH).wait_send()
            pltpu.make_async_remote_copy(
                src_ref=one, dst_ref=one, send_sem=send_sems.at[k], recv_sem=recv_sems.at[k],
                device_id=(x, y, 1 - c), device_id_type=MESH).wait_recv()

    return pl.pallas_call(
        body, name=name, out_shape=pltpu.HBM(blocks.shape, blocks.dtype),
        in_specs=[HBM_SPEC, SEM_SPEC, SEM_SPEC, ANY_SPEC], out_specs=HBM_SPEC, input_output_aliases={0: 0},
        compiler_params=pltpu.CompilerParams(has_side_effects=EFFECT),
    )(blocks, send_sems, recv_sems, after)


def _place_own(shard, kind, chip_idx, name):
    r, cdim = shard.shape
    tr, tc = _tile(r, 512), _tile(cdim, 1024)
    nrb, ncb = r // tr, cdim // tc

    def body(k_ref, s_ref, o_ref):
        o_ref[...] = s_ref[...].astype(BF16)

    if kind == "col":
        full, o_map = (r, N_CHIP * cdim), lambda i, j, kr: (i, kr[0] * ncb + j)
    else:
        full, o_map = (N_CHIP * r, cdim), lambda i, j, kr: (kr[0] * nrb + i, j)
    return pl.pallas_call(
        body, name=name, out_shape=jax.ShapeDtypeStruct(full, BF16),
        grid_spec=pltpu.PrefetchScalarGridSpec(
            num_scalar_prefetch=1, grid=(nrb, ncb),
            in_specs=[pl.BlockSpec((tr, tc), lambda i, j, kr: (i, j))],
            out_specs=pl.BlockSpec((tr, tc), o_map)),
        compiler_params=_params("parallel", "parallel"),
    )(chip_idx, shard)


def _weight_window(ref, kind, shard_shape, k, half):
    r, cdim = shard_shape
    hr = r // 2
    if kind == "col":
        return ref.at[pl.ds(half * hr, hr), pl.ds(pl.multiple_of(k * cdim, 128), cdim)]
    return ref.at[pl.ds(pl.multiple_of(k * r + half * hr, 2 * SUBLANE), hr), :]


def _target_cores(c, both_cores):
    return [(cc, cc, c) for cc in range(2)] if both_cores else [(c, 0, 0)]


def _gather_weights_start(fulls, kinds, shard_shapes, groups, after, name, both_cores=True):
    n, ng = len(fulls), len(groups)
    per = 2 if both_cores else 1

    def body(*refs):
        ins = refs[:n]
        sems = refs[n + 1:n + 1 + 2 * ng]
        token = refs[2 * n + 1 + 2 * ng]
        x, y, c = _place()
        chips = _other_chips(x, y)
        for g, group in enumerate(groups):
            for pos, (w, j) in enumerate(group):
                own = _weight_window(ins[w], kinds[w], shard_shapes[w], 2 * x + y, c)
                for cc, mine, theirs in _target_cores(c, both_cores):
                    pltpu.make_async_remote_copy(
                        src_ref=own, dst_ref=own,
                        send_sem=sems[2 * g].at[per * pos + mine], recv_sem=sems[2 * g + 1].at[per * pos + theirs],
                        device_id=(*chips[j], cc), device_id_type=MESH).start()
        token[...] = jnp.zeros_like(token)

    sem_shapes = []
    for group in groups:
        sem_shapes += [pltpu.SemaphoreType.DMA((per * len(group),))] * 2
    outs = pl.pallas_call(
        body, name=name,
        out_shape=tuple(sem_shapes) + tuple(pltpu.HBM(f.shape, f.dtype) for f in fulls)
        + (jax.ShapeDtypeStruct((SUBLANE, 128), F32),),
        in_specs=[HBM_SPEC] * n + [ANY_SPEC], out_specs=(SEM_SPEC,) * (2 * ng) + (HBM_SPEC,) * n + (VMEM_SPEC,),
        input_output_aliases={w: 2 * ng + w for w in range(n)},
        compiler_params=pltpu.CompilerParams(has_side_effects=EFFECT),
    )(*[_hbm(f) for f in fulls], after)
    sems = [(outs[2 * g], outs[2 * g + 1]) for g in range(ng)]
    return sems, list(outs[2 * ng:2 * ng + n]), outs[2 * ng + n]


def _gather_weights_wait(sems, group, fulls, kinds, shard_shapes, after, name, both_cores=True):
    n = len(fulls)
    per = 2 if both_cores else 1

    def body(*refs):
        ins = refs[:n]
        send_sems, recv_sems = refs[n], refs[n + 1]
        x, y, c = _place()
        chips = _other_chips(x, y)
        for pos, (w, j) in enumerate(group):
            own = _weight_window(ins[w], kinds[w], shard_shapes[w], 2 * x + y, c)
            for cc, mine, _ in _target_cores(c, both_cores):
                landed = _weight_window(ins[w], kinds[w], shard_shapes[w], 2 * chips[j][0] + chips[j][1], cc)
                k = per * pos + mine
                pltpu.make_async_remote_copy(
                    src_ref=own, dst_ref=own, send_sem=send_sems.at[k], recv_sem=recv_sems.at[k],
                    device_id=(*chips[j], cc), device_id_type=MESH).wait_send()
                pltpu.make_async_remote_copy(
                    src_ref=landed, dst_ref=landed, send_sem=send_sems.at[k], recv_sem=recv_sems.at[k],
                    device_id=(*chips[j], cc), device_id_type=MESH).wait_recv()

    return list(pl.pallas_call(
        body, name=name, out_shape=tuple(pltpu.HBM(f.shape, f.dtype) for f in fulls),
        in_specs=[HBM_SPEC] * n + [SEM_SPEC, SEM_SPEC, ANY_SPEC], out_specs=(HBM_SPEC,) * n,
        input_output_aliases={w: w for w in range(n)},
        compiler_params=pltpu.CompilerParams(has_side_effects=EFFECT),
    )(*fulls, sems[0], sems[1], after))


def _forward_to_sibling(full, kind, shard_shape, relations, name):
    nr = len(relations)

    def body(in_ref, out_ref, send_sems, recv_sems):
        x, y, c = _place()
        chips = _other_chips(x, y)
        copies = []
        for pos, j in enumerate(relations):
            k = 2 * chips[j][0] + chips[j][1]
            mine = _weight_window(in_ref, kind, shard_shape, k, c)
            cp = pltpu.make_async_remote_copy(
                src_ref=mine, dst_ref=_weight_window(out_ref, kind, shard_shape, k, c),
                send_sem=send_sems.at[pos], recv_sem=recv_sems.at[pos], device_id=(x, y, 1 - c), device_id_type=MESH)
            cp.start()
            copies.append(cp)
        for pos, j in enumerate(relations):
            theirs = _weight_window(out_ref, kind, shard_shape, 2 * chips[j][0] + chips[j][1], 1 - c)
            pltpu.make_async_remote_copy(
                src_ref=theirs, dst_ref=theirs, send_sem=send_sems.at[pos], recv_sem=recv_sems.at[pos],
                device_id=(x, y, 1 - c), device_id_type=MESH).wait_recv()
        for cp in copies:
            cp.wait_send()

    return pl.pallas_call(
        body, name=name, out_shape=jax.ShapeDtypeStruct(full.shape, full.dtype),
        in_specs=[HBM_SPEC], out_specs=HBM_SPEC, input_output_aliases={0: 0},
        scratch_shapes=[pltpu.SemaphoreType.DMA((nr,))] * 2,
    )(full)


def _grad_piece(ref, kind, h, cdim, k, half):
    if kind == "col":
        return ref.at[pl.ds(half * h, h), pl.ds(pl.multiple_of(k * cdim, 128), cdim)]
    return ref.at[k, pl.ds(half * h, h), :]


def _grad_dims(g, kind):
    return (g.shape[0] // 2, g.shape[1] // N_CHIP) if kind == "col" else (g.shape[1] // 2, g.shape[2])


def _reduce_start(grads, kinds, name):
    n = len(grads)
    dims = [_grad_dims(g, kind) for g, kind in zip(grads, kinds)]
    lands = [lax.empty((N_DEV - 1, h, cdim), g.dtype) for g, (h, cdim) in zip(grads, dims)]

    def body(*refs):
        g_ins, land_ins = refs[:n], refs[n:2 * n]
        send_sems, recv_sems = refs[2 * n], refs[2 * n + 1]
        token = refs[4 * n + 2]
        x, y, c = _place()
        for w in range(n):
            h, cdim = dims[w]
            base = (N_DEV - 1) * w
            pltpu.make_async_remote_copy(
                src_ref=_grad_piece(g_ins[w], kinds[w], h, cdim, 2 * x + y, 1 - c), dst_ref=land_ins[w].at[0],
                send_sem=send_sems.at[base], recv_sem=recv_sems.at[base],
                device_id=(x, y, 1 - c), device_id_type=MESH).start()
            for j, chip in enumerate(_other_chips(x, y)):
                for cc in range(2):
                    pltpu.make_async_remote_copy(
                        src_ref=_grad_piece(g_ins[w], kinds[w], h, cdim, 2 * chip[0] + chip[1], cc),
                        dst_ref=land_ins[w].at[1 + 2 * j + c],
                        send_sem=send_sems.at[base + 1 + 2 * j + cc], recv_sem=recv_sems.at[base + 1 + 2 * j + c],
                        device_id=(*chip, cc), device_id_type=MESH).start()
        token[...] = jnp.zeros_like(token)

    sems = pltpu.SemaphoreType.DMA(((N_DEV - 1) * n,))
    outs = pl.pallas_call(
        body, name=name,
        out_shape=(sems, sems) + tuple(pltpu.HBM(a.shape, a.dtype) for a in list(grads) + lands)
        + (jax.ShapeDtypeStruct((SUBLANE, 128), F32),),
        in_specs=[HBM_SPEC] * (2 * n), out_specs=(SEM_SPEC, SEM_SPEC) + (HBM_SPEC,) * (2 * n) + (VMEM_SPEC,),
        input_output_aliases={i: 2 + i for i in range(2 * n)},
        compiler_params=pltpu.CompilerParams(has_side_effects=EFFECT),
    )(*[_hbm(a) for a in list(grads) + lands])
    return outs[0], outs[1], list(outs[2:2 + n]), list(outs[2 + n:2 + 2 * n]), outs[2 + 2 * n]


def _reduce_wait(send_sems, recv_sems, grads, lands, kinds, after, name):
    n = len(grads)
    dims = [_grad_dims(g, kind) for g, kind in zip(grads, kinds)]

    def body(*refs):
        g_ins, land_ins = refs[:n], refs[n:2 * n]
        send_sems, recv_sems = refs[2 * n], refs[2 * n + 1]
        x, y, c = _place()
        for w in range(n):
            h, cdim = dims[w]
            piece = _grad_piece(g_ins[w], kinds[w], h, cdim, 2 * x + y, c)
            for s in range(N_DEV - 1):
                k = (N_DEV - 1) * w + s
                slot = land_ins[w].at[s]
                pltpu.make_async_remote_copy(
                    src_ref=piece, dst_ref=slot, send_sem=send_sems.at[k], recv_sem=recv_sems.at[k],
                    device_id=(x, y, 1 - c), device_id_type=MESH).wait_send()
                pltpu.make_async_remote_copy(
                    src_ref=piece, dst_ref=slot, send_sem=send_sems.at[k], recv_sem=recv_sems.at[k],
                    device_id=(x, y, 1 - c), device_id_type=MESH).wait_recv()

    outs = pl.pallas_call(
        body, name=name, out_shape=tuple(pltpu.HBM(a.shape, a.dtype) for a in list(grads) + list(lands)),
        in_specs=[HBM_SPEC] * (2 * n) + [SEM_SPEC, SEM_SPEC, ANY_SPEC], out_specs=(HBM_SPEC,) * (2 * n),
        input_output_aliases={i: i for i in range(2 * n)},
        compiler_params=pltpu.CompilerParams(has_side_effects=EFFECT),
    )(*grads, *lands, send_sems, recv_sems, after)
    return list(outs[:n]), list(outs[n:])


def _add_pieces(g, land, kind, chip_idx, core_idx, name):
    _, h, cdim = land.shape
    tr, tc = _tile(h, 256), _tile(cdim, 2048)
    nrb, ncb = h // tr, cdim // tc

    def body(k_ref, c_ref, g_ref, l_ref, o_ref):
        acc = g_ref[...].astype(F32)
        for s in range(N_DEV - 1):
            acc = acc + l_ref[s].astype(F32)
        o_ref[...] = acc

    if kind == "col":
        g_spec = pl.BlockSpec((tr, tc), lambda i, j, kr, cr: (cr[0] * nrb + i, kr[0] * ncb + j))
    else:
        g_spec = pl.BlockSpec((None, tr, tc), lambda i, j, kr, cr: (kr[0], cr[0] * nrb + i, j))
    return pl.pallas_call(
        body, name=name, out_shape=jax.ShapeDtypeStruct((2 * h, cdim), F32),
        grid_spec=pltpu.PrefetchScalarGridSpec(
            num_scalar_prefetch=2, grid=(nrb, ncb),
            in_specs=[g_spec, pl.BlockSpec((N_DEV - 1, tr, tc), lambda i, j, kr, cr: (0, i, j))],
            out_specs=pl.BlockSpec((tr, tc), lambda i, j, kr, cr: (cr[0] * nrb + i, j))),
        compiler_params=_params("parallel", "parallel"),
    )(chip_idx, core_idx, g, land)


def _join_halves(shards, name):
    n = len(shards)

    def body(*refs):
        ins, outs = refs[:n], refs[n:2 * n]
        send_sems, recv_sems = refs[2 * n:]
        x, y, c = _place()
        copies = []
        for w in range(n):
            h = shards[w].shape[0] // 2
            cp = pltpu.make_async_remote_copy(
                src_ref=ins[w].at[pl.ds(c * h, h), :], dst_ref=outs[w].at[pl.ds(c * h, h), :],
                send_sem=send_sems.at[w], recv_sem=recv_sems.at[w], device_id=(x, y, 1 - c), device_id_type=MESH)
            cp.start()
            copies.append(cp)
        for w in range(n):
            h = shards[w].shape[0] // 2
            theirs = outs[w].at[pl.ds((1 - c) * h, h), :]
            pltpu.make_async_remote_copy(
                src_ref=theirs, dst_ref=theirs, send_sem=send_sems.at[w], recv_sem=recv_sems.at[w],
                device_id=(x, y, 1 - c), device_id_type=MESH).wait_recv()
        for cp in copies:
            cp.wait_send()

    return pl.pallas_call(
        body, name=name, out_shape=tuple(jax.ShapeDtypeStruct(s.shape, s.dtype) for s in shards),
        in_specs=[HBM_SPEC] * n, out_specs=tuple([HBM_SPEC] * n),
        input_output_aliases={w: w for w in range(n)},
        scratch_shapes=[pltpu.SemaphoreType.DMA((n,))] * 2,
    )(*shards)


def _sum_devices(gathered, rows, name):
    n = gathered.shape[1]

    def body(g_ref, o_ref):
        acc = g_ref[0:rows, :]
        for dev in range(1, N_DEV):
            acc = acc + g_ref[dev * rows:(dev + 1) * rows, :]
        o_ref[...] = acc

    return pl.pallas_call(
        body, name=name, out_shape=jax.ShapeDtypeStruct((rows, n), F32),
        in_specs=[VMEM_SPEC], out_specs=VMEM_SPEC,
        compiler_params=pltpu.CompilerParams(vmem_limit_bytes=VMEM_LIMIT),
    )(gathered)


def _pack(vectors, width):
    flat = [v.reshape(-1) for v in vectors]
    offsets, total = [], 0
    for f in flat:
        offsets.append(total)
        total += f.shape[0]
    rows = -(-total // (width * SUBLANE)) * SUBLANE
    flat.append(jnp.zeros((rows * width - total,), F32))
    return jnp.concatenate(flat).reshape(rows, width), offsets


def kernel(x, c, mod_w, mod_b, norm_g, a_w_in, a_conv_w, a_conv_b, a_w_out, b_w_in, b_ln_g, b_ln_b, b_w_s, b_b_s, b_w_out, final_g, loss_target, m_mod_w, m_mod_b, m_norm_g, m_a_w_in, m_a_conv_w, m_a_conv_b, m_a_w_out, m_b_w_in, m_b_ln_g, m_b_ln_b, m_b_w_s, m_b_b_s, m_b_w_out, m_final_g, v_mod_w, v_mod_b, v_norm_g, v_a_w_in, v_a_conv_w, v_a_conv_b, v_a_w_out, v_b_w_in, v_b_ln_g, v_b_ln_b, v_b_w_s, v_b_b_s, v_b_w_out, v_final_g):
    seq, d = x.shape[1], x.shape[2]
    e = a_conv_b.shape[1]
    wd = mod_w.shape[2]
    ax, ay, ac = _place()
    chip = 2 * ax + ay
    dev = 2 * chip + ac
    chip_idx = jnp.reshape(chip, (1,)).astype(jnp.int32)
    core_idx = jnp.reshape(ac, (1,)).astype(jnp.int32)

    x2d = x[0]
    target = loss_target[0]

    w_names = ["a_w_in", "a_w_out", "b_w_in", "b_w_out"]
    w_kinds = ["col", "row", "col", "row"]
    w_shards = [a_w_in[0], a_w_out[0], b_w_in[0], b_w_out[0]]
    w_shapes = [sh.shape for sh in w_shards]
    placed = [_place_own(sh, kind, chip_idx, "place_" + nm) for nm, sh, kind in zip(w_names, w_shards, w_kinds)]
    a_groups, far_groups = [[(0, 0), (0, 1)]], [[(0, 2)]]
    whole_group = [[(0, 0), (0, 1), (0, 2)]]

    es = e // N_CHIP
    packed0, offs0 = _pack([c, a_conv_w, b_ln_g, b_ln_b], 1024)
    gathered0 = _all_gather(packed0, "gather_params").reshape(N_DEV, -1)
    c_all = gathered0[:, :d]
    per_chip = gathered0[0::2]

    def from_chips_cols(k, rows_):
        got = per_chip[:, offs0[k]:offs0[k] + rows_ * es].reshape(N_CHIP, rows_, es)
        return jnp.transpose(got, (1, 0, 2)).reshape(rows_, e)

    conv_w_full = from_chips_cols(1, 3)
    conv_w = [conv_w_full[k:k + 1] for k in range(3)]
    ln_g, ln_b = from_chips_cols(2, 1), from_chips_cols(3, 1)
    mod_b_cols = lax.dynamic_slice_in_dim(mod_b, chip * wd, wd, axis=1)[:, None, :]
    c_act, mod_part = _mod_fwd(c_all, mod_w, mod_b_cols, "mod_fwd")
    n_layer = mod_w.shape[0]
    mod_gathered = _all_gather(mod_part, "gather_mod")
    a_sems, (wa_in,), a_token = _gather_weights_start(
        placed[:1], w_kinds[:1], w_shapes[:1], a_groups, mod_gathered, "gather_a_w_in_start", both_cores=False)
    mod_all = mod_gathered.reshape(N_CHIP, 2, n_layer, N_DEV, wd)[:, 0]
    mod_all = jnp.transpose(mod_all, (1, 2, 0, 3)).reshape(n_layer, N_DEV, N_CHIP * wd)
    mod_me = lax.dynamic_index_in_dim(mod_all, dev, axis=1, keepdims=False)
    shift = [mod_me[l:l + 1, 0:d] for l in range(n_layer)]
    scale = [mod_me[l:l + 1, d:2 * d] for l in range(n_layer)]
    gate = [mod_me[l:l + 1, 2 * d:3 * d] for l in range(n_layer)]

    g0, g1, gf = norm_g[0:1], norm_g[1:2], final_g[None, :]
    h0 = _norm_mod(x2d, g0, scale[0], shift[0], "norm_mod0")

    def slab(r):
        return jnp.bitwise_xor(chip_idx, r)

    def arrived(sems, group, weight, w, after, name, both_cores=True):
        return _gather_weights_wait(sems, [(0, j) for _, j in group], [weight], [w_kinds[w]], [w_shapes[w]],
                                    after, name, both_cores)[0]

    proj0 = _mm_proj_slab(h0, wa_in, None, slab(0), 4, "a_proj_own", a_token)
    wa_in = arrived(a_sems[0], a_groups[0], wa_in, 0, proj0, "gather_wait_near", both_cores=False)
    wa_in = _forward_to_sibling(wa_in, w_kinds[0], w_shapes[0], [0, 1], "forward_near")
    far_sems, (wa_in,), far_token = _gather_weights_start(
        [wa_in], w_kinds[:1], w_shapes[:1], far_groups, a_token, "gather_a_w_in_far_start")
    proj0 = _mm_proj_slab(h0, wa_in, proj0, slab(2), 4, "a_proj_x", far_token)
    proj0 = _mm_proj_slab(h0, wa_in, proj0, slab(1), 4, "a_proj_y", far_token)
    wa_in = arrived(far_sems[0], far_groups[0], wa_in, 0, proj0, "gather_wait_far")
    def start_whole(w, after, name):
        sems, (flight,), token = _gather_weights_start(
            placed[w:w + 1], w_kinds[w:w + 1], w_shapes[w:w + 1], whole_group, after, name)
        return sems[0], flight, token

    ao_sems, wa_out, ao_token = start_whole(1, wa_in, "gather_a_w_out_start")
    proj0 = _mm_proj_slab(h0, wa_in, proj0, slab(3), 4, "a_proj_far", ao_token)
    wa_out = arrived(ao_sems, whole_group[0], wa_out, 1, proj0, "gather_wait_a_w_out")
    bi_sems, wb_in, bi_token = start_whole(2, wa_out, "gather_b_w_in_start")
    y0, br0, x1, h1 = _conv_fwd(proj0, conv_w, a_conv_b, wa_out, x2d, gate[0], g1, scale[1], shift[1],
                                "conv_fwd", bi_token)
    wb_in = arrived(bi_sems, whole_group[0], wb_in, 2, h1, "gather_wait_b_w_in")
    bo_sems, wb_out, bo_token = start_whole(3, wb_in, "gather_b_w_out_start")
    proj1 = _mm_proj(h1, wb_in, 3, "b_proj", bo_token)
    b_s_t = jnp.transpose(b_b_s[0])
    wb_out = arrived(bo_sems, whole_group[0], wb_out, 3, proj1, "gather_wait_b_w_out")
    y1, dx2, dbr1, loss_part, g_final_g, dgate1 = _gmlp_fwd_head(
        proj1, ln_g, ln_b, b_w_s[0], b_s_t, wb_out, x1, gate[1], gf, target, "gmlp_fwd_head")

    gw_b_out = _mm_dw_out(y1, dbr1, "b_out_dw")
    dproj1, g_w_s, g_b_s_t, g_ln_g, g_ln_b = _gmlp_bwd(
        proj1, dbr1, wb_out, ln_g, ln_b, b_w_s[0], jnp.swapaxes(b_w_s[0], 1, 2), b_s_t, "gmlp_bwd")
    gw_b_in = _mm_dw_in(h1, dproj1, "b_proj_dw")
    b_kinds = ["col", "row"]
    b_send, b_recv, b_grads, b_lands, b_token = _reduce_start(
        [gw_b_in, gw_b_out.reshape(N_CHIP, e // N_CHIP, d)], b_kinds, "reduce_b_start")
    dx1, dshift1, dscale1, g_g1, dbr0, dgate0 = _mm_dh_norm_bwd(
        dproj1, wb_in, x1, dx2, g1, scale[1], "b_proj_dx", b_token, br=br0, gate=gate[0])

    early = [loss_part[0, 0:1], g_final_g, g_g1, g_ln_g, g_ln_b, jnp.transpose(g_b_s_t), g_w_s,
             jnp.concatenate([dshift1, dscale1, dgate1, dgate0], axis=1)]
    packed_e, offs_e = _pack(early, 1024)
    rows_e = packed_e.shape[0]
    blocks_e = lax.dynamic_update_slice(jnp.zeros((N_DEV, rows_e, 1024), F32), packed_e[None], (dev, 0, 0))
    e_send, e_recv, blocks_e, e_token = _all_gather_start(blocks_e, "gather_small_start")

    gw_a_out = _mm_dw_out(y0, dbr0, "a_out_dw")
    ao_send, ao_recv, ao_grads, ao_lands, ao_grad_token = _reduce_start(
        [gw_a_out.reshape(N_CHIP, e // N_CHIP, d)], ["row"], "reduce_a_out_start")
    dproj0, g_w0, g_w1, g_w2, g_conv_b = _conv_bwd(
        proj0, dbr0, wa_out, conv_w, a_conv_b, "conv_bwd", ao_grad_token + e_token)
    gw_a_in = _mm_dw_in(h0, dproj0, "a_proj_dw")
    ai_send, ai_recv, ai_grads, ai_lands, ai_token = _reduce_start([gw_a_in], ["col"], "reduce_a_in_start")
    grad_x, dshift0, dscale0, g_g0 = _mm_dh_norm_bwd(dproj0, wa_in, x2d, dx1, g0, scale[0], "a_proj_dx", ai_token)

    def finish(send, recv, grads_, lands_, kinds_, names_, after, tag):
        grads_, lands_ = _reduce_wait(send, recv, grads_, lands_, kinds_, after, "reduce_" + tag + "_wait")
        halves = [_add_pieces(g, land, kind, chip_idx, core_idx, "add_pieces_" + nm)
                  for g, land, kind, nm in zip(grads_, lands_, kinds_, names_)]
        return _join_halves(halves, "join_" + tag)

    upd, big_grads = {}, {}

    def adamw_big(nm, w, g, m, v):
        g_out, *rest = _adamw_2d(w[0], g, m[0], v[0], "adamw_" + nm)
        big_grads[nm] = g_out[None]
        upd[nm] = tuple(o[None] for o in rest)

    g_b_w_in, g_b_w_out = finish(b_send, b_recv, b_grads, b_lands, b_kinds, ["b_w_in", "b_w_out"], grad_x, "b")
    adamw_big("b_w_in", b_w_in, g_b_w_in, m_b_w_in, v_b_w_in)
    adamw_big("b_w_out", b_w_out, g_b_w_out, m_b_w_out, v_b_w_out)

    late = [g_g0, jnp.concatenate([g_w0, g_w1, g_w2], axis=0), g_conv_b, jnp.concatenate([dshift0, dscale0], axis=1)]
    packed_l, offs_l = _pack(late, 1024)
    rows_l = packed_l.shape[0]
    gathered_l = _all_gather(packed_l, "gather_small_late", after=upd["b_w_out"][0])
    blocks_e = _all_gather_wait(e_send, e_recv, blocks_e, gathered_l, "gather_small_wait")
    gathered_e = blocks_e.reshape(N_DEV * rows_e, 1024)
    summed_e = _sum_devices(gathered_e, rows_e, "sum_small_early").reshape(-1)
    summed_l = _sum_devices(gathered_l, rows_l, "sum_small_late").reshape(-1)

    def take(summed, offs, k, shape):
        size = math.prod(shape)
        return summed[offs[k]:offs[k] + size].reshape(shape)

    def rows_of(gathered, rows_, offs, k, width):
        return gathered.reshape(N_DEV, rows_ * 1024)[:, offs[k]:offs[k] + width]

    loss = take(summed_e, offs_e, 0, ())
    grad_final_g = take(summed_e, offs_e, 1, (d,))
    grad_norm_g = jnp.concatenate([take(summed_l, offs_l, 0, (1, d)), take(summed_e, offs_e, 2, (1, d))], axis=0)
    grad_ln_g_full = take(summed_e, offs_e, 3, (1, e))
    grad_ln_b_full = take(summed_e, offs_e, 4, (1, e))
    grad_b_b_s = take(summed_e, offs_e, 5, (1, GROUPS, CHUNK))
    grad_b_w_s = take(summed_e, offs_e, 6, (1, GROUPS, CHUNK, CHUNK))
    grad_conv_w_full = take(summed_l, offs_l, 1, (3, e))
    grad_a_conv_b = take(summed_l, offs_l, 2, (1, e))
    grad_a_conv_w = lax.dynamic_slice_in_dim(grad_conv_w_full, chip * es, es, axis=1)[None]
    grad_b_ln_g = lax.dynamic_slice_in_dim(grad_ln_g_full, chip * es, es, axis=1)
    grad_b_ln_b = lax.dynamic_slice_in_dim(grad_ln_b_full, chip * es, es, axis=1)
    dmod_e = rows_of(gathered_e, rows_e, offs_e, 7, 4 * d)
    dmod_l = rows_of(gathered_l, rows_l, offs_l, 3, 2 * d)
    dmod_all = jnp.stack([jnp.concatenate([dmod_l, dmod_e[:, 3 * d:]], axis=1), dmod_e[:, :3 * d]], axis=1)
    mod_b_e = take(summed_e, offs_e, 7, (4 * d,))
    mod_b_l = take(summed_l, offs_l, 3, (2 * d,))
    grad_mod_b = jnp.stack([jnp.concatenate([mod_b_l, mod_b_e[3 * d:]]), mod_b_e[:3 * d]])
    dmod_cols = jnp.transpose(lax.dynamic_slice_in_dim(dmod_all, chip * wd, wd, axis=2), (1, 0, 2))

    grad_mod_w, delta_mod_w, new_m_mod_w, new_v_mod_w = _mod_w_update(
        jnp.transpose(c_act), dmod_cols, mod_w, m_mod_w, v_mod_w, "mod_w_update")
    (g_a_w_out,) = finish(ao_send, ao_recv, ao_grads, ao_lands, ["row"], ["a_w_out"], delta_mod_w, "a_out")
    adamw_big("a_w_out", a_w_out, g_a_w_out, m_a_w_out, v_a_w_out)
    (g_a_w_in,) = finish(ai_send, ai_recv, ai_grads, ai_lands, ["col"], ["a_w_in"], upd["a_w_out"][0], "a_in")
    adamw_big("a_w_in", a_w_in, g_a_w_in, m_a_w_in, v_a_w_in)
    small_w = [("mod_b", mod_b, grad_mod_b, m_mod_b, v_mod_b), ("norm_g", norm_g, grad_norm_g, m_norm_g, v_norm_g),
               ("a_conv_w", a_conv_w, grad_a_conv_w, m_a_conv_w, v_a_conv_w),
               ("a_conv_b", a_conv_b, grad_a_conv_b, m_a_conv_b, v_a_conv_b),
               ("b_ln_g", b_ln_g, grad_b_ln_g, m_b_ln_g, v_b_ln_g), ("b_ln_b", b_ln_b, grad_b_ln_b, m_b_ln_b, v_b_ln_b),
               ("b_w_s", b_w_s, grad_b_w_s, m_b_w_s, v_b_w_s), ("b_b_s", b_b_s, grad_b_b_s, m_b_b_s, v_b_b_s),
               ("final_g", final_g, grad_final_g, m_final_g, v_final_g)]

    def flat2d(a):
        return a.reshape(-1, a.shape[-1])

    res = _adamw_small([flat2d(t[1]) for t in small_w], [flat2d(t[2]) for t in small_w],
                       [flat2d(t[3]) for t in small_w], [flat2d(t[4]) for t in small_w], "adamw_small")
    for (nm, w, _, _, _), r3 in zip(small_w, res):
        upd[nm] = tuple(o.reshape(w.shape) for o in r3)
    upd["mod_w"] = (delta_mod_w, new_m_mod_w, new_v_mod_w)

    grads = {"mod_w": grad_mod_w, "mod_b": grad_mod_b, "norm_g": grad_norm_g, "a_conv_w": grad_a_conv_w,
             "a_conv_b": grad_a_conv_b, "b_ln_g": grad_b_ln_g, "b_ln_b": grad_b_ln_b, "b_w_s": grad_b_w_s,
             "b_b_s": grad_b_b_s, "final_g": grad_final_g, **big_grads}
    order = ["mod_w", "mod_b", "norm_g", "a_w_in", "a_conv_w", "a_conv_b", "a_w_out", "b_w_in", "b_ln_g", "b_ln_b",
             "b_w_s", "b_b_s", "b_w_out", "final_g"]
    return (loss, grad_x[None], *[grads[k] for k in order], *[upd[k][0] for k in order],
            *[upd[k][1] for k in order], *[upd[k][2] for k in order])
```

```python
import functools
import math

import jax
import jax.numpy as jnp
from jax import lax
from jax.experimental import pallas as pl
from jax.experimental.pallas import tpu as pltpu

F32 = jnp.float32
BF16 = jnp.bfloat16
MESH = pl.DeviceIdType.MESH

N_DEV = 8
N_CHIP = 4
SUBLANE = 8
PACK = 16
ACT = F32
RMS_EPS = 1e-6
LN_EPS = 1e-5
CHUNK = 128
GROUPS = 8
ADAM_LR = 0.001
ADAM_B1 = 0.9
ADAM_B2 = 0.999
ADAM_EPS = 1e-08
ADAM_WD = 0.01
ADAM_STEP = 10
VMEM_LIMIT = 56 << 20

HBM_SPEC = pl.BlockSpec(memory_space=pltpu.HBM)
VMEM_SPEC = pl.BlockSpec(memory_space=pltpu.VMEM)
SEM_SPEC = pl.BlockSpec(memory_space=pltpu.SEMAPHORE)
ANY_SPEC = pl.BlockSpec(memory_space=pl.ANY)
EFFECT = pltpu.SideEffectType.DATAFLOW_SIDE_EFFECTING


def _params(*sem):
    return pltpu.CompilerParams(dimension_semantics=sem, vmem_limit_bytes=VMEM_LIMIT)


def _tile(n, want):
    if n <= want:
        return n
    t = want
    while n % t:
        t -= 128
    return t


def _sigmoid(x):
    return 0.5 * jnp.tanh(0.5 * x) + 0.5


def _silu_and_grad(x):
    s = _sigmoid(x)
    return x * s, s * (1.0 + x * (1.0 - s))


def _gelu_and_grad(x):
    cdf = 0.5 * (1.0 + lax.erf(x * (1.0 / math.sqrt(2.0))))
    pdf = jnp.exp(-0.5 * x * x) * (1.0 / math.sqrt(2.0 * math.pi))
    return x * cdf, cdf + x * pdf


def _gelu(x):
    return x * (0.5 * (1.0 + lax.erf(x * (1.0 / math.sqrt(2.0)))))


def _rms(x):
    r = lax.rsqrt(jnp.mean(x * x, axis=-1, keepdims=True) + RMS_EPS)
    return x * r, r


def _rms_bwd(dxn, xn, r):
    return r * (dxn - xn * jnp.mean(dxn * xn, axis=-1, keepdims=True))


def _colsum(a):
    return jnp.sum(a, axis=0, keepdims=True)


def _shift_down(cur, before, k):
    rolled = pltpu.roll(cur, k, 0)
    row = lax.broadcasted_iota(jnp.int32, before.shape, 0)
    head = jnp.where(row < k, pltpu.roll(before, k, 0), rolled[:SUBLANE])
    return jnp.concatenate([head, rolled[SUBLANE:]], axis=0)


def _shift_up(cur, after, k):
    n = cur.shape[0]
    rolled = pltpu.roll(cur, n - k, 0)
    row = lax.broadcasted_iota(jnp.int32, after.shape, 0)
    tail = jnp.where(row >= SUBLANE - k, pltpu.roll(after, SUBLANE - k, 0), rolled[n - SUBLANE:])
    return jnp.concatenate([rolled[:n - SUBLANE], tail], axis=0)


def _after_spec():
    return pl.BlockSpec((SUBLANE, 128), lambda *_: (0, 0))


def _mm_proj(h, w, n_split, name, after):
    s, d = h.shape
    e = w.shape[1] // n_split
    tm, tn = _tile(s, 2048), _tile(e, 2048)
    nj = e // tn

    def body(h_ref, w_ref, after_ref, o_ref):
        o_ref[...] = jnp.dot(h_ref[...], w_ref[...], preferred_element_type=F32).astype(ACT)

    return pl.pallas_call(
        body, name=name,
        out_shape=jax.ShapeDtypeStruct((n_split, s, e), ACT),
        grid=(s // tm, n_split * nj),
        in_specs=[pl.BlockSpec((tm, d), lambda i, j: (i, 0)), pl.BlockSpec((d, tn), lambda i, j: (0, j)),
                  _after_spec()],
        out_specs=pl.BlockSpec((None, tm, tn), lambda i, j: (j // nj, i, j % nj)),
        compiler_params=_params("parallel", "parallel"),
    )(h, w, after)


def _mm_proj_slab(h, w, proj, q_idx, n_split, name, after):
    s, d = h.shape
    e = w.shape[1] // n_split
    tm, tn = _tile(s, 2048), _tile(e, 2048)
    nj = e // tn

    def body(q_ref, h_ref, w_ref, *rest):
        o_ref = rest[-1]
        o_ref[...] = jnp.dot(h_ref[...], w_ref[...], preferred_element_type=F32).astype(ACT)

    in_specs = [pl.BlockSpec((tm, d), lambda i, j, qr: (i, 0)), pl.BlockSpec((d, tn), lambda i, j, qr: (0, qr[0] * nj + j)),
                _after_spec()]
    args = [q_idx, h, w, after]
    aliases = {}
    if proj is not None:
        in_specs.append(ANY_SPEC)
        args.append(proj)
        aliases = {4: 0}
    return pl.pallas_call(
        body, name=name,
        out_shape=jax.ShapeDtypeStruct((n_split, s, e), ACT),
        grid_spec=pltpu.PrefetchScalarGridSpec(
            num_scalar_prefetch=1, grid=(s // tm, nj), in_specs=in_specs,
            out_specs=pl.BlockSpec((None, tm, tn), lambda i, j, qr: (qr[0], i, j))),
        input_output_aliases=aliases,
        compiler_params=_params("parallel", "parallel"),
    )(*args)


def _mm_dh_norm_bwd(dp, w, x, dx_in, g, scale, name, after, br=None, gate=None):
    nq, s, e = dp.shape
    d = w.shape[0]
    has_branch = br is not None
    tm, tk = _tile(s, 1024), _tile(e, 1024 if has_branch else 2048)
    nkq = e // tk
    nk = nq * nkq

    def body(*refs):
        if has_branch:
            (a_ref, b_ref, after_ref, x_ref, dxin_ref, g_ref, sc_ref, br_ref, gate_ref,
             dx_ref, dsh_ref, dsc_ref, dg_ref, dbr_ref, dgate_ref, acc_ref) = refs
        else:
            (a_ref, b_ref, after_ref, x_ref, dxin_ref, g_ref, sc_ref,
             dx_ref, dsh_ref, dsc_ref, dg_ref, acc_ref) = refs
        i, k = pl.program_id(0), pl.program_id(1)

        @pl.when(jnp.logical_and(i == 0, k == 0))
        def _():
            dsh_ref[...] = jnp.zeros_like(dsh_ref)
            dsc_ref[...] = jnp.zeros_like(dsc_ref)
            dg_ref[...] = jnp.zeros_like(dg_ref)
            if has_branch:
                dgate_ref[...] = jnp.zeros_like(dgate_ref)

        @pl.when(k == 0)
        def _():
            acc_ref[...] = jnp.zeros_like(acc_ref)
        acc_ref[...] += lax.dot_general(a_ref[...], b_ref[...], (((1,), (1,)), ((), ())), preferred_element_type=F32)

        @pl.when(k == nk - 1)
        def _():
            dh = acc_ref[...]
            g_vec = g_ref[...]
            xn, r = _rms(x_ref[...])
            dsh_ref[...] += _colsum(dh)
            dsc_ref[...] += _colsum(dh * (xn * g_vec))
            da = dh * (1.0 + sc_ref[...])
            dg_ref[...] += _colsum(da * xn)
            dx = dxin_ref[...] + _rms_bwd(da * g_vec, xn, r)
            dx_ref[...] = dx
            if has_branch:
                dgate_ref[...] += _colsum(dx * br_ref[...])
                dbr_ref[...] = (gate_ref[...] * dx).astype(BF16)

    rows = pl.BlockSpec((tm, d), lambda i, k: (i, 0))
    vec = pl.BlockSpec((1, d), lambda i, k: (0, 0))
    vec_out = jax.ShapeDtypeStruct((1, d), F32)
    in_specs = [pl.BlockSpec((None, tm, tk), lambda i, k: (k // nkq, i, k % nkq)),
                pl.BlockSpec((d, tk), lambda i, k: (0, k)), _after_spec(), rows, rows, vec, vec]
    out_shape = [jax.ShapeDtypeStruct((s, d), F32), vec_out, vec_out, vec_out]
    out_specs = [rows, vec, vec, vec]
    args = [dp, w, after, x, dx_in, g, scale]
    if has_branch:
        in_specs += [rows, vec]
        out_shape += [jax.ShapeDtypeStruct((s, d), BF16), vec_out]
        out_specs += [rows, vec]
        args += [br, gate]
    return pl.pallas_call(
        body, name=name, out_shape=tuple(out_shape), grid=(s // tm, nk),
        in_specs=in_specs, out_specs=tuple(out_specs), scratch_shapes=[pltpu.VMEM((tm, d), F32)],
        compiler_params=_params("arbitrary", "arbitrary"),
    )(*args)


def _dw_body(n_t):
    def body(a_ref, b_ref, o_ref, acc_ref):
        t = pl.program_id(2)

        @pl.when(t == 0)
        def _():
            acc_ref[...] = jnp.zeros_like(acc_ref)
        acc_ref[...] += lax.dot_general(a_ref[...], b_ref[...], (((0,), (0,)), ((), ())), preferred_element_type=F32)

        @pl.when(t == n_t - 1)
        def _():
            o_ref[...] = acc_ref[...].astype(o_ref.dtype)
    return body


def _mm_dw_in(h, dp, name):
    s, d = h.shape
    nq, _, e = dp.shape
    tm, tn, tt = _tile(d, 1024), _tile(e, 2048), _tile(s, 2048)
    nj = e // tn

    return pl.pallas_call(
        _dw_body(s // tt), name=name,
        out_shape=jax.ShapeDtypeStruct((d, nq * e), BF16),
        grid=(d // tm, nq * nj, s // tt),
        in_specs=[pl.BlockSpec((tt, tm), lambda i, j, t: (t, i)),
                  pl.BlockSpec((None, tt, tn), lambda i, j, t: (j // nj, t, j % nj))],
        out_specs=pl.BlockSpec((tm, tn), lambda i, j, t: (i, j)),
        scratch_shapes=[pltpu.VMEM((tm, tn), F32)],
        compiler_params=_params("parallel", "parallel", "arbitrary"),
    )(h, dp)


def _mm_dw_out(y, dbr, name):
    s, e = y.shape
    d = dbr.shape[1]
    tm, tn, tt = _tile(e, 1024), _tile(d, 1024), _tile(s, 2048)

    return pl.pallas_call(
        _dw_body(s // tt), name=name,
        out_shape=jax.ShapeDtypeStruct((e, d), BF16),
        grid=(e // tm, d // tn, s // tt),
        in_specs=[pl.BlockSpec((tt, tm), lambda i, j, t: (t, i)), pl.BlockSpec((tt, tn), lambda i, j, t: (t, j))],
        out_specs=pl.BlockSpec((tm, tn), lambda i, j, t: (i, j)),
        scratch_shapes=[pltpu.VMEM((tm, tn), F32)],
        compiler_params=_params("parallel", "parallel", "arbitrary"),
    )(y, dbr)


def _row_spec(ts, d):
    return pl.BlockSpec((ts, d), lambda i: (i, 0))


def _vec_spec(d):
    return pl.BlockSpec((1, d), lambda i: (0, 0))


def _norm_mod(x, g, scale, shift, name):
    s, d = x.shape
    ts = _tile(s, 512)

    def body(x_ref, g_ref, sc_ref, sh_ref, h_ref):
        xn, _ = _rms(x_ref[...])
        h_ref[...] = ((xn * g_ref[...]) * (1.0 + sc_ref[...]) + sh_ref[...]).astype(BF16)

    return pl.pallas_call(
        body, name=name, out_shape=jax.ShapeDtypeStruct((s, d), BF16), grid=(s // ts,),
        in_specs=[_row_spec(ts, d), _vec_spec(d), _vec_spec(d), _vec_spec(d)],
        out_specs=_row_spec(ts, d), compiler_params=_params("parallel"),
    )(x, g, scale, shift)


def _head_tile(x1, br, gate, gf, target):
    d = x1.shape[-1]
    xn, r = _rms(x1 + gate * br)
    err = xn * gf - target
    loss = 0.5 * jnp.sum(jnp.mean(err * err, axis=-1, keepdims=True))
    dout = err * (1.0 / d)
    dx = _rms_bwd(dout * gf, xn, r)
    return dx, gate * dx, loss, _colsum(dout * xn), _colsum(dx * br)


def _f32(ref):
    return ref[...].astype(F32)


def _cols_f32(ref, cols):
    return ref[:, cols].astype(F32)


def _rows_before(halo_ref, cols=slice(None)):
    return _cols_f32(halo_ref, cols)[PACK - SUBLANE:]


def _rows_after(halo_ref, cols=slice(None)):
    return _cols_f32(halo_ref, cols)[:SUBLANE]


CONV_CHAIN_COLS = 512


def _conv_fwd(proj, conv_w, conv_b, w_out, x, gate, g_next, scale_next, shift_next, name, after):
    _, s, e = proj.shape
    d = w_out.shape[1]
    ts, te = _tile(s, 256), e
    cw = min(CONV_CHAIN_COLS, te)
    hb = ts // PACK

    def body(bg_ref, cg_ref, xi_ref, z_ref, cgp_ref, xip_ref, w0_ref, w1_ref, w2_ref, b_ref, wo_ref, after_ref,
             x_ref, gate_ref, g_ref, sc_ref, sh_ref, y_ref, br_ref, x1_ref, h_ref):
        first = pl.program_id(0) == 0
        br = None
        for c0 in range(0, te, cw):
            cols = slice(c0, c0 + cw)
            cx = _cols_f32(cg_ref, cols) * _cols_f32(xi_ref, cols)
            before = jnp.where(first, 0.0, _rows_before(cgp_ref, cols) * _rows_before(xip_ref, cols))
            conv = b_ref[:, cols] + w2_ref[:, cols] * cx
            conv = conv + w0_ref[:, cols] * _shift_down(cx, before, 2)
            conv = conv + w1_ref[:, cols] * _shift_down(cx, before, 1)
            z = _cols_f32(z_ref, cols)
            y = ((z * _sigmoid(z)) * _cols_f32(bg_ref, cols) * conv).astype(BF16)
            y_ref[:, cols] = y
            part_br = jnp.dot(y, wo_ref[cols, :], preferred_element_type=F32)
            br = part_br if br is None else br + part_br
        br_ref[...] = br
        x1 = x_ref[...] + gate_ref[...] * br
        x1_ref[...] = x1
        xn, _ = _rms(x1)
        h_ref[...] = ((xn * g_ref[...]) * (1.0 + sc_ref[...]) + sh_ref[...]).astype(BF16)

    def part(q):
        return pl.BlockSpec((None, ts, te), lambda i: (q, i, 0))

    def halo_before(q):
        return pl.BlockSpec((None, PACK, te), lambda i: (q, jnp.maximum(i * hb - 1, 0), 0))

    return pl.pallas_call(
        body, name=name,
        out_shape=(jax.ShapeDtypeStruct((s, e), BF16), jax.ShapeDtypeStruct((s, d), F32),
                   jax.ShapeDtypeStruct((s, d), F32), jax.ShapeDtypeStruct((s, d), BF16)), grid=(s // ts,),
        in_specs=[part(0), part(1), part(2), part(3), halo_before(1), halo_before(2)]
        + [_vec_spec(e)] * 4 + [pl.BlockSpec((e, d), lambda i: (0, 0)), _after_spec(), _row_spec(ts, d)]
        + [_vec_spec(d)] * 4,
        out_specs=(_row_spec(ts, e), _row_spec(ts, d), _row_spec(ts, d), _row_spec(ts, d)),
        compiler_params=_params("parallel"),
    )(proj, proj, proj, proj, proj, proj, *conv_w, conv_b, w_out, after, x, gate, g_next, scale_next, shift_next)


def _conv_bwd(proj, dbr, w_out, conv_w, conv_b, name, after):
    _, s, e = proj.shape
    d = dbr.shape[1]
    ts, te = _tile(s, 512), _tile(e, 1024)
    cw = min(CONV_CHAIN_COLS // 2, te)
    hb = ts // PACK
    n_i = s // ts
    last_halo = s // PACK - 1
    nt = (((1,), (1,)), ((), ()))

    def body(bg_ref, cg_ref, xi_ref, z_ref, dbr_ref, cgp_ref, xip_ref, bgn_ref, zn_ref, dbrn_ref, wo_ref,
             w0_ref, w1_ref, w2_ref, b_ref, after_ref, dp_ref, dw0_ref, dw1_ref, dw2_ref, db_ref):
        i = pl.program_id(1)

        @pl.when(i == 0)
        def _():
            for acc in (dw0_ref, dw1_ref, dw2_ref, db_ref):
                acc[...] = jnp.zeros_like(acc)
        for c0 in range(0, te, cw):
            cols = slice(c0, c0 + cw)
            wo = wo_ref[cols, :]
            dy = lax.dot_general(dbr_ref[...], wo, nt, preferred_element_type=F32)
            dyn = lax.dot_general(dbrn_ref[...], wo, nt, preferred_element_type=F32)[:SUBLANE]
            bg, cg = _cols_f32(bg_ref, cols), _cols_f32(cg_ref, cols)
            xi, z = _cols_f32(xi_ref, cols), _cols_f32(z_ref, cols)
            w0, w1, w2 = w0_ref[:, cols], w1_ref[:, cols], w2_ref[:, cols]
            cx = cg * xi
            before = jnp.where(i > 0, _rows_before(cgp_ref, cols) * _rows_before(xip_ref, cols), 0.0)
            cx1 = _shift_down(cx, before, 1)
            cx2 = _shift_down(cx, before, 2)
            conv = b_ref[:, cols] + w2 * cx
            conv = conv + w0 * cx2
            conv = conv + w1 * cx1
            sz, dsz = _silu_and_grad(z)
            dp_ref[3, :, cols] = (dy * bg * conv * dsz).astype(BF16)
            dp_ref[0, :, cols] = (dy * sz * conv).astype(BF16)
            dconv = dy * sz * bg
            zn = _rows_after(zn_ref, cols)
            after = jnp.where(i < n_i - 1, dyn * (zn * _sigmoid(zn)) * _rows_after(bgn_ref, cols), 0.0)
            db_ref[:, cols] += _colsum(dconv)
            dw2_ref[:, cols] += _colsum(dconv * cx)
            dw1_ref[:, cols] += _colsum(dconv * cx1)
            dw0_ref[:, cols] += _colsum(dconv * cx2)
            dcx = w2 * dconv + w1 * _shift_up(dconv, after, 1) + w0 * _shift_up(dconv, after, 2)
            dp_ref[1, :, cols] = (dcx * xi).astype(BF16)
            dp_ref[2, :, cols] = (dcx * cg).astype(BF16)

    def part(q):
        return pl.BlockSpec((None, ts, te), lambda j, i: (q, i, j))

    def halo_before(q):
        return pl.BlockSpec((None, PACK, te), lambda j, i: (q, jnp.maximum(i * hb - 1, 0), j))

    def halo_after(q):
        return pl.BlockSpec((None, PACK, te), lambda j, i: (q, jnp.minimum((i + 1) * hb, last_halo), j))

    return pl.pallas_call(
        body, name=name,
        out_shape=(jax.ShapeDtypeStruct((4, s, e), BF16),) + (jax.ShapeDtypeStruct((1, e), F32),) * 4,
        grid=(e // te, n_i),
        in_specs=[part(0), part(1), part(2), part(3), pl.BlockSpec((ts, d), lambda j, i: (i, 0)),
                  halo_before(1), halo_before(2), halo_after(0), halo_after(3),
                  pl.BlockSpec((PACK, d), lambda j, i: (jnp.minimum((i + 1) * hb, last_halo), 0)),
                  pl.BlockSpec((te, d), lambda j, i: (j, 0))]
        + [pl.BlockSpec((1, te), lambda j, i: (0, j))] * 4 + [_after_spec()],
        out_specs=(pl.BlockSpec((4, ts, te), lambda j, i: (0, i, j)),) + (pl.BlockSpec((1, te), lambda j, i: (0, j)),) * 4,
        compiler_params=_params("parallel", "arbitrary"),
    )(proj, proj, proj, proj, dbr, proj, proj, proj, proj, dbr, w_out, *conv_w, conv_b, after)


def _tril(w):
    row = lax.broadcasted_iota(jnp.int32, w.shape, 0)
    col = lax.broadcasted_iota(jnp.int32, w.shape, 1)
    return jnp.where(row >= col, w, 0.0)


def _triu(w):
    row = lax.broadcasted_iota(jnp.int32, w.shape, 0)
    col = lax.broadcasted_iota(jnp.int32, w.shape, 1)
    return jnp.where(row <= col, w, 0.0)


def _layer_norm_fwd(v, g, b):
    mu = jnp.mean(v, axis=-1, keepdims=True)
    vc = v - mu
    rstd = lax.rsqrt(jnp.mean(vc * vc, axis=-1, keepdims=True) + LN_EPS)
    vhat = vc * rstd
    return vhat * g + b, vhat, rstd


GMLP_CHUNKS_PER_STEP = 2


def _gmlp_rows(s):
    return CHUNK * min(GMLP_CHUNKS_PER_STEP, s // CHUNK)


def _mix_positions(w_ref, src_scr, dst_scr, gw, mask, bias_ref=None):
    for gi in range(GROUPS):
        cols = slice(gi * gw, (gi + 1) * gw)
        wm = mask(w_ref[gi]).astype(BF16)
        for n in range(src_scr.shape[0] // CHUNK):
            rows = slice(n * CHUNK, (n + 1) * CHUNK)
            out = jnp.dot(wm, src_scr[rows, cols], preferred_element_type=F32)
            if bias_ref is not None:
                out = out + bias_ref[:, gi:gi + 1]
            dst_scr[rows, cols] = out


def _gmlp_fwd_head(proj, ln_g, ln_b, w_s, b_s_t, w_out, x1, gate, gf, target, name):
    _, s, e = proj.shape
    d = w_out.shape[1]
    gw = e // GROUPS
    ts = _gmlp_rows(s)

    def body(pu_ref, pv_ref, pz_ref, g_ref, b_ref, ws_ref, bs_ref, wo_ref, x1_ref, gate_ref, gf_ref, tg_ref,
             y_ref, dx_ref, dbr_ref, loss_ref, dgf_ref, dgate_ref, vn_scr, mix_scr):
        @pl.when(pl.program_id(0) == 0)
        def _():
            loss_ref[...] = jnp.zeros_like(loss_ref)
            dgf_ref[...] = jnp.zeros_like(dgf_ref)
            dgate_ref[...] = jnp.zeros_like(dgate_ref)
        vn, _, _ = _layer_norm_fwd(_gelu(_f32(pv_ref)), g_ref[...], b_ref[...])
        vn_scr[...] = vn.astype(BF16)
        _mix_positions(ws_ref, vn_scr, mix_scr, gw, _tril, bs_ref)
        z = _f32(pz_ref)
        y = ((z * _sigmoid(z)) * (_gelu(_f32(pu_ref)) * mix_scr[...])).astype(BF16)
        y_ref[...] = y
        br = jnp.dot(y, wo_ref[...], preferred_element_type=F32)
        dx, dbr, loss, dgf, dgate = _head_tile(x1_ref[...], br, gate_ref[...], gf_ref[...], tg_ref[...])
        dx_ref[...] = dx
        dbr_ref[...] = dbr.astype(BF16)
        loss_ref[...] += loss
        dgf_ref[...] += dgf
        dgate_ref[...] += dgate

    def part(q):
        return pl.BlockSpec((None, ts, e), lambda i: (q, i, 0))

    return pl.pallas_call(
        body, name=name,
        out_shape=(jax.ShapeDtypeStruct((s, e), BF16), jax.ShapeDtypeStruct((s, d), F32),
                   jax.ShapeDtypeStruct((s, d), BF16), jax.ShapeDtypeStruct((SUBLANE, 128), F32),
                   jax.ShapeDtypeStruct((1, d), F32), jax.ShapeDtypeStruct((1, d), F32)),
        grid=(s // ts,),
        in_specs=[part(0), part(1), part(2), _vec_spec(e), _vec_spec(e),
                  pl.BlockSpec((GROUPS, CHUNK, CHUNK), lambda i: (0, 0, 0)),
                  pl.BlockSpec((CHUNK, GROUPS), lambda i: (0, 0)), pl.BlockSpec((e, d), lambda i: (0, 0)),
                  _row_spec(ts, d), _vec_spec(d), _vec_spec(d), _row_spec(ts, d)],
        out_specs=(_row_spec(ts, e), _row_spec(ts, d), _row_spec(ts, d),
                   pl.BlockSpec((SUBLANE, 128), lambda i: (0, 0)), _vec_spec(d), _vec_spec(d)),
        scratch_shapes=[pltpu.VMEM((ts, e), BF16), pltpu.VMEM((ts, e), F32)],
        compiler_params=_params("arbitrary"),
    )(proj, proj, proj, ln_g, ln_b, w_s, b_s_t, w_out, x1, gate, gf, target)


def _gmlp_bwd(proj, dbr, w_out, ln_g, ln_b, w_s, w_s_t, b_s_t, name):
    _, s, e = proj.shape
    d = dbr.shape[1]
    gw = e // GROUPS
    ts = _gmlp_rows(s)
    n_i = s // ts

    def body(pu_ref, pv_ref, pz_ref, dbr_ref, wo_ref, g_ref, b_ref, ws_ref, wst_ref, bs_ref,
             dp_ref, dws_ref, dbs_ref, dlg_ref, dlb_ref, vn_scr, mix_scr, dm_scr, dvn_scr, dmacc_scr):
        i = pl.program_id(0)

        @pl.when(i == 0)
        def _():
            dws_ref[...] = jnp.zeros_like(dws_ref)
            dlg_ref[...] = jnp.zeros_like(dlg_ref)
            dlb_ref[...] = jnp.zeros_like(dlb_ref)
            dmacc_scr[...] = jnp.zeros_like(dmacc_scr)
        ln_g = g_ref[...]
        u, du_dpu = _gelu_and_grad(_f32(pu_ref))
        v, dv_dpv = _gelu_and_grad(_f32(pv_ref))
        vn, vhat, rstd = _layer_norm_fwd(v, ln_g, b_ref[...])
        vn_scr[...] = vn.astype(BF16)
        _mix_positions(ws_ref, vn_scr, mix_scr, gw, _tril, bs_ref)
        mixed = mix_scr[...]
        dy = lax.dot_general(dbr_ref[...], wo_ref[...], (((1,), (1,)), ((), ())), preferred_element_type=F32)
        sz, dsz = _silu_and_grad(_f32(pz_ref))
        ds = dy * sz
        dp_ref[2] = (dy * (u * mixed) * dsz).astype(BF16)
        dp_ref[0] = (ds * mixed * du_dpu).astype(BF16)
        dm = ds * u
        dm_scr[...] = dm.astype(BF16)
        for n in range(ts // CHUNK):
            rows = slice(n * CHUNK, (n + 1) * CHUNK)
            dmacc_scr[...] += dm[rows]
            for gi in range(GROUPS):
                cols = slice(gi * gw, (gi + 1) * gw)
                dws_ref[gi] += lax.dot_general(dm_scr[rows, cols], vn_scr[rows, cols], (((1,), (1,)), ((), ())),
                                               preferred_element_type=F32)
        _mix_positions(wst_ref, dm_scr, dvn_scr, gw, _triu)
        dvn = dvn_scr[...]
        dlg_ref[...] += _colsum(dvn * vhat)
        dlb_ref[...] += _colsum(dvn)
        dvh = dvn * ln_g
        dv = rstd * (dvh - jnp.mean(dvh, axis=-1, keepdims=True) - vhat * jnp.mean(dvh * vhat, axis=-1, keepdims=True))
        dp_ref[1] = (dv * dv_dpv).astype(BF16)

        @pl.when(i == n_i - 1)
        def _():
            for gi in range(GROUPS):
                dws_ref[gi] = _tril(dws_ref[gi])
                dbs_ref[:, gi:gi + 1] = jnp.sum(dmacc_scr[:, gi * gw:(gi + 1) * gw], axis=1, keepdims=True)

    def part(q):
        return pl.BlockSpec((None, ts, e), lambda i: (q, i, 0))

    w_spec = pl.BlockSpec((GROUPS, CHUNK, CHUNK), lambda i: (0, 0, 0))
    bs_spec = pl.BlockSpec((CHUNK, GROUPS), lambda i: (0, 0))
    return pl.pallas_call(
        body, name=name,
        out_shape=(jax.ShapeDtypeStruct((3, s, e), BF16), jax.ShapeDtypeStruct((GROUPS, CHUNK, CHUNK), F32),
                   jax.ShapeDtypeStruct((CHUNK, GROUPS), F32), jax.ShapeDtypeStruct((1, e), F32),
                   jax.ShapeDtypeStruct((1, e), F32)),
        grid=(n_i,),
        in_specs=[part(0), part(1), part(2), pl.BlockSpec((ts, d), lambda i: (i, 0)),
                  pl.BlockSpec((e, d), lambda i: (0, 0)), _vec_spec(e), _vec_spec(e), w_spec, w_spec, bs_spec],
        out_specs=(pl.BlockSpec((3, ts, e), lambda i: (0, i, 0)), w_spec, bs_spec, _vec_spec(e), _vec_spec(e)),
        scratch_shapes=[pltpu.VMEM((ts, e), BF16), pltpu.VMEM((ts, e), F32), pltpu.VMEM((ts, e), BF16),
                        pltpu.VMEM((ts, e), F32), pltpu.VMEM((CHUNK, e), F32)],
        compiler_params=_params("arbitrary"),
    )(proj, proj, proj, dbr, w_out, ln_g, ln_b, w_s, w_s_t, b_s_t)


def _mod_fwd(c_all, mod_w, mod_b_cols, name):
    n_layer, d, w = mod_w.shape

    def body(c_ref, w_ref, b_ref, ca_ref, o_ref):
        c = c_ref[...]
        ca = c * _sigmoid(c)
        ca_ref[...] = ca
        for li in range(n_layer):
            o_ref[li * N_DEV:(li + 1) * N_DEV, :] = (
                jnp.dot(ca, w_ref[li], preferred_element_type=F32, precision=lax.Precision.HIGHEST) + b_ref[li])

    return pl.pallas_call(
        body, name=name,
        out_shape=(jax.ShapeDtypeStruct((N_DEV, d), F32), jax.ShapeDtypeStruct((n_layer * N_DEV, w), F32)),
        in_specs=[VMEM_SPEC] * 3, out_specs=(VMEM_SPEC, VMEM_SPEC),
        compiler_params=pltpu.CompilerParams(vmem_limit_bytes=VMEM_LIMIT),
    )(c_all, mod_w, mod_b_cols)


def _adamw(w, g, m, v):
    m = ADAM_B1 * m + (1.0 - ADAM_B1) * g
    v = ADAM_B2 * v + (1.0 - ADAM_B2) * (g * g)
    m_hat = m / (1.0 - ADAM_B1 ** ADAM_STEP)
    v_hat = v / (1.0 - ADAM_B2 ** ADAM_STEP)
    delta = -ADAM_LR * (m_hat / (jnp.sqrt(v_hat) + ADAM_EPS) + ADAM_WD * w)
    return delta, m, v


def _adamw_2d(w, g, m, v, name):
    r, c = w.shape
    tr, tc = _tile(r, 512), _tile(c, 1024)

    def body(w_ref, g_ref, m_ref, v_ref, go_ref, d_ref, nm_ref, nv_ref):
        g = g_ref[...]
        go_ref[...] = g
        d_ref[...], nm_ref[...], nv_ref[...] = _adamw(w_ref[...], g, m_ref[...], v_ref[...])

    spec = pl.BlockSpec((tr, tc), lambda i, j: (i, j))
    shape = jax.ShapeDtypeStruct((r, c), F32)
    return pl.pallas_call(
        body, name=name, out_shape=(shape,) * 4, grid=(r // tr, c // tc),
        in_specs=[spec] * 4, out_specs=(spec,) * 4, compiler_params=_params("parallel", "parallel"),
    )(w, g, m, v)


def _mod_w_update(ca_t, dmod_cols, w, m, v, name):
    n_layer, d, wd = w.shape
    tr = _tile(d, 256)

    def body(ca_ref, dm_ref, w_ref, m_ref, v_ref, g_ref, d_ref, nm_ref, nv_ref):
        ca = ca_ref[...]
        dm = dm_ref[...]
        g = ca[:, 0:1] * dm[0:1, :]
        for b in range(1, N_DEV):
            g = g + ca[:, b:b + 1] * dm[b:b + 1, :]
        g_ref[...] = g
        d_ref[...], nm_ref[...], nv_ref[...] = _adamw(w_ref[...], g, m_ref[...], v_ref[...])

    spec = pl.BlockSpec((None, tr, wd), lambda l, i: (l, i, 0))
    shape = jax.ShapeDtypeStruct((n_layer, d, wd), F32)
    return pl.pallas_call(
        body, name=name, out_shape=(shape,) * 4, grid=(n_layer, d // tr),
        in_specs=[pl.BlockSpec((tr, N_DEV), lambda l, i: (i, 0)), pl.BlockSpec((None, N_DEV, wd), lambda l, i: (l, 0, 0)),
                  spec, spec, spec],
        out_specs=(spec,) * 4, compiler_params=_params("parallel", "parallel"),
    )(ca_t, dmod_cols, w, m, v)


def _adamw_small(ws, gs, ms, vs, name):
    n = len(ws)

    def body(*refs):
        ins, outs = refs[:4 * n], refs[4 * n:]
        for k in range(n):
            delta, nm, nv = _adamw(ins[k][...], ins[n + k][...], ins[2 * n + k][...], ins[3 * n + k][...])
            outs[3 * k][...] = delta
            outs[3 * k + 1][...] = nm
            outs[3 * k + 2][...] = nv

    out_shape = []
    for w in ws:
        out_shape += [jax.ShapeDtypeStruct(w.shape, F32)] * 3
    outs = pl.pallas_call(
        body, name=name, out_shape=tuple(out_shape),
        in_specs=[VMEM_SPEC] * (4 * n), out_specs=tuple([VMEM_SPEC] * (3 * n)),
        compiler_params=pltpu.CompilerParams(vmem_limit_bytes=VMEM_LIMIT),
    )(*ws, *gs, *ms, *vs)
    return [tuple(outs[3 * k:3 * k + 3]) for k in range(n)]


def _place():
    return lax.axis_index("x"), lax.axis_index("y"), lax.axis_index("c")


def _other_chips(x, y):
    return [(1 - x, y), (x, 1 - y), (1 - x, 1 - y)]


def _all_gather(block, name, after=None):
    m_per, n = block.shape

    def body(x_ref, *rest):
        out_ref, send_sems, recv_sems, local_sem = rest[-4:]
        x, y, c = _place()
        me, sibling = (x, y, c), (x, y, 1 - c)
        chips = _other_chips(x, y)

        def rows(px, py, pc):
            return out_ref.at[pl.ds((4 * px + 2 * py + pc) * m_per, m_per), :]

        def copy(k, blk, to, src=None):
            return pltpu.make_async_remote_copy(
                src_ref=rows(*blk) if src is None else src, dst_ref=rows(*blk),
                send_sem=send_sems.at[k], recv_sem=recv_sems.at[k], device_id=to, device_id_type=MESH)

        mine = pltpu.make_async_copy(x_ref, rows(*me), local_sem)
        mine.start()
        first = [copy(0, me, sibling, src=x_ref)]
        first += [copy(1 + j, me, (*chip, c), src=x_ref) for j, chip in enumerate(chips)]
        for cp in first:
            cp.start()
        passed = [copy(4 + j, (*chip, c), sibling) for j, chip in enumerate(chips)]
        for j, chip in enumerate(chips):
            copy(1 + j, (*chip, c), me).wait_recv()
            passed[j].start()
        copy(0, sibling, me).wait_recv()
        for j, chip in enumerate(chips):
            copy(4 + j, (*chip, 1 - c), me).wait_recv()
        for cp in first + passed:
            cp.wait_send()
        mine.wait()

    return pl.pallas_call(
        body, name=name, out_shape=jax.ShapeDtypeStruct((N_DEV * m_per, n), F32),
        in_specs=[VMEM_SPEC] + ([] if after is None else [ANY_SPEC]), out_specs=VMEM_SPEC,
        scratch_shapes=[pltpu.SemaphoreType.DMA((7,)), pltpu.SemaphoreType.DMA((7,)), pltpu.SemaphoreType.DMA],
        compiler_params=pltpu.CompilerParams(vmem_limit_bytes=VMEM_LIMIT),
    )(*([block] if after is None else [block, after]))


def _hbm(a):
    return pltpu.with_memory_space_constraint(a, pltpu.HBM)


def _all_gather_start(blocks, name):
    def body(in_ref, send_sems, recv_sems, thru_ref, token):
        token[...] = jnp.zeros_like(token)
        x, y, c = _place()
        mine = in_ref.at[4 * x + 2 * y + c]
        pltpu.make_async_remote_copy(
            src_ref=mine, dst_ref=mine, send_sem=send_sems.at[0], recv_sem=recv_sems.at[0],
            device_id=(x, y, 1 - c), device_id_type=MESH).start()
        for j, chip in enumerate(_other_chips(x, y)):
            for cc in range(2):
                pltpu.make_async_remote_copy(
                    src_ref=mine, dst_ref=mine, send_sem=send_sems.at[1 + 2 * j + cc],
                    recv_sem=recv_sems.at[1 + 2 * j + c], device_id=(*chip, cc), device_id_type=MESH).start()

    sems = pltpu.SemaphoreType.DMA((N_DEV - 1,))
    return pl.pallas_call(
        body, name=name,
        out_shape=(sems, sems, pltpu.HBM(blocks.shape, blocks.dtype), jax.ShapeDtypeStruct((SUBLANE, 128), F32)),
        in_specs=[HBM_SPEC], out_specs=(SEM_SPEC, SEM_SPEC, HBM_SPEC, VMEM_SPEC), input_output_aliases={0: 2},
        compiler_params=pltpu.CompilerParams(has_side_effects=EFFECT),
    )(_hbm(blocks))


def _all_gather_wait(send_sems, recv_sems, blocks, after, name):
    def body(in_ref, send_sems, recv_sems, after_ref, out_ref):
        x, y, c = _place()
        one = in_ref.at[0]
        for k in range(N_DEV - 1):
            pltpu.make_async_remote_copy(
                src_ref=one, dst_ref=one, send_sem=send_sems.at[k], recv_sem=recv_sems.at[k],
                device_id=(x, y, 1 - c), device_id_type=MESH).wait_send()
            pltpu.make_async_remote_copy(
                src_ref=one, dst_ref=one, send_sem=send_sems.at[k], recv_sem=recv_sems.at[k],
                device_id=(x, y, 1 - c), device_id_type=MESH).wait_recv()

    return pl.pallas_call(
        body, name=name, out_shape=pltpu.HBM(blocks.shape, blocks.dtype),
        in_specs=[HBM_SPEC, SEM_SPEC, SEM_SPEC, ANY_SPEC], out_specs=HBM_SPEC, input_output_aliases={0: 0},
        compiler_params=pltpu.CompilerParams(has_side_effects=EFFECT),
    )(blocks, send_sems, recv_sems, after)


def _place_own(shard, kind, chip_idx, name):
    r, cdim = shard.shape
    tr, tc = _tile(r, 512), _tile(cdim, 1024)
    nrb, ncb = r // tr, cdim // tc

    def body(k_ref, s_ref, o_ref):
        o_ref[...] = s_ref[...].astype(BF16)

    if kind == "col":
        full, o_map = (r, N_CHIP * cdim), lambda i, j, kr: (i, kr[0] * ncb + j)
    else:
        full, o_map = (N_CHIP * r, cdim), lambda i, j, kr: (kr[0] * nrb + i, j)
    return pl.pallas_call(
        body, name=name, out_shape=jax.ShapeDtypeStruct(full, BF16),
        grid_spec=pltpu.PrefetchScalarGridSpec(
            num_scalar_prefetch=1, grid=(nrb, ncb),
            in_specs=[pl.BlockSpec((tr, tc), lambda i, j, kr: (i, j))],
            out_specs=pl.BlockSpec((tr, tc), o_map)),
        compiler_params=_params("parallel", "parallel"),
    )(chip_idx, shard)


def _weight_window(ref, kind, shard_shape, k, half):
    r, cdim = shard_shape
    hr = r // 2
    if kind == "col":
        return ref.at[pl.ds(half * hr, hr), pl.ds(pl.multiple_of(k * cdim, 128), cdim)]
    return ref.at[pl.ds(pl.multiple_of(k * r + half * hr, 2 * SUBLANE), hr), :]


def _target_cores(c, both_cores):
    return [(cc, cc, c) for cc in range(2)] if both_cores else [(c, 0, 0)]


def _gather_weights_start(fulls, kinds, shard_shapes, groups, after, name, both_cores=True):
    n, ng = len(fulls), len(groups)
    per = 2 if both_cores else 1

    def body(*refs):
        ins = refs[:n]
        sems = refs[n + 1:n + 1 + 2 * ng]
        token = refs[2 * n + 1 + 2 * ng]
        x, y, c = _place()
        chips = _other_chips(x, y)
        for g, group in enumerate(groups):
            for pos, (w, j) in enumerate(group):
                own = _weight_window(ins[w], kinds[w], shard_shapes[w], 2 * x + y, c)
                for cc, mine, theirs in _target_cores(c, both_cores):
                    pltpu.make_async_remote_copy(
                        src_ref=own, dst_ref=own,
                        send_sem=sems[2 * g].at[per * pos + mine], recv_sem=sems[2 * g + 1].at[per * pos + theirs],
                        device_id=(*chips[j], cc), device_id_type=MESH).start()
        token[...] = jnp.zeros_like(token)

    sem_shapes = []
    for group in groups:
        sem_shapes += [pltpu.SemaphoreType.DMA((per * len(group),))] * 2
    outs = pl.pallas_call(
        body, name=name,
        out_shape=tuple(sem_shapes) + tuple(pltpu.HBM(f.shape, f.dtype) for f in fulls)
        + (jax.ShapeDtypeStruct((SUBLANE, 128), F32),),
        in_specs=[HBM_SPEC] * n + [ANY_SPEC], out_specs=(SEM_SPEC,) * (2 * ng) + (HBM_SPEC,) * n + (VMEM_SPEC,),
        input_output_aliases={w: 2 * ng + w for w in range(n)},
        compiler_params=pltpu.CompilerParams(has_side_effects=EFFECT),
    )(*[_hbm(f) for f in fulls], after)
    sems = [(outs[2 * g], outs[2 * g + 1]) for g in range(ng)]
    return sems, list(outs[2 * ng:2 * ng + n]), outs[2 * ng + n]


def _gather_weights_wait(sems, group, fulls, kinds, shard_shapes, after, name, both_cores=True):
    n = len(fulls)
    per = 2 if both_cores else 1

    def body(*refs):
        ins = refs[:n]
        send_sems, recv_sems = refs[n], refs[n + 1]
        x, y, c = _place()
        chips = _other_chips(x, y)
        for pos, (w, j) in enumerate(group):
            own = _weight_window(ins[w], kinds[w], shard_shapes[w], 2 * x + y, c)
            for cc, mine, _ in _target_cores(c, both_cores):
                landed = _weight_window(ins[w], kinds[w], shard_shapes[w], 2 * chips[j][0] + chips[j][1], cc)
                k = per * pos + mine
                pltpu.make_async_remote_copy(
                    src_ref=own, dst_ref=own, send_sem=send_sems.at[k], recv_sem=recv_sems.at[k],
                    device_id=(*chips[j], cc), device_id_type=MESH).wait_send()
                pltpu.make_async_remote_copy(
                    src_ref=landed, dst_ref=landed, send_sem=send_sems.at[k], recv_sem=recv_sems.at[k],
                    device_id=(*chips[j], cc), device_id_type=MESH).wait_recv()

    return list(pl.pallas_call(
        body, name=name, out_shape=tuple(pltpu.HBM(f.shape, f.dtype) for f in fulls),
        in_specs=[HBM_SPEC] * n + [SEM_SPEC, SEM_SPEC, ANY_SPEC], out_specs=(HBM_SPEC,) * n,
        input_output_aliases={w: w for w in range(n)},
        compiler_params=pltpu.CompilerParams(has_side_effects=EFFECT),
    )(*fulls, sems[0], sems[1], after))


def _forward_to_sibling(full, kind, shard_shape, relations, name):
    nr = len(relations)

    def body(in_ref, out_ref, send_sems, recv_sems):
        x, y, c = _place()
        chips = _other_chips(x, y)
        copies = []
        for pos, j in enumerate(relations):
            k = 2 * chips[j][0] + chips[j][1]
            mine = _weight_window(in_ref, kind, shard_shape, k, c)
            cp = pltpu.make_async_remote_copy(
                src_ref=mine, dst_ref=_weight_window(out_ref, kind, shard_shape, k, c),
                send_sem=send_sems.at[pos], recv_sem=recv_sems.at[pos], device_id=(x, y, 1 - c), device_id_type=MESH)
            cp.start()
            copies.append(cp)
        for pos, j in enumerate(relations):
            theirs = _weight_window(out_ref, kind, shard_shape, 2 * chips[j][0] + chips[j][1], 1 - c)
            pltpu.make_async_remote_copy(
                src_ref=theirs, dst_ref=theirs, send_sem=send_sems.at[pos], recv_sem=recv_sems.at[pos],
                device_id=(x, y, 1 - c), device_id_type=MESH).wait_recv()
        for cp in copies:
            cp.wait_send()

    return pl.pallas_call(
        body, name=name, out_shape=jax.ShapeDtypeStruct(full.shape, full.dtype),
        in_specs=[HBM_SPEC], out_specs=HBM_SPEC, input_output_aliases={0: 0},
        scratch_shapes=[pltpu.SemaphoreType.DMA((nr,))] * 2,
    )(full)


def _grad_piece(ref, kind, h, cdim, k, half):
    if kind == "col":
        return ref.at[pl.ds(half * h, h), pl.ds(pl.multiple_of(k * cdim, 128), cdim)]
    return ref.at[k, pl.ds(half * h, h), :]


def _grad_dims(g, kind):
    return (g.shape[0] // 2, g.shape[1] // N_CHIP) if kind == "col" else (g.shape[1] // 2, g.shape[2])


def _reduce_start(grads, kinds, name):
    n = len(grads)
    dims = [_grad_dims(g, kind) for g, kind in zip(grads, kinds)]
    lands = [lax.empty((N_DEV - 1, h, cdim), g.dtype) for g, (h, cdim) in zip(grads, dims)]

    def body(*refs):
        g_ins, land_ins = refs[:n], refs[n:2 * n]
        send_sems, recv_sems = refs[2 * n], refs[2 * n + 1]
        token = refs[4 * n + 2]
        x, y, c = _place()
        for w in range(n):
            h, cdim = dims[w]
            base = (N_DEV - 1) * w
            pltpu.make_async_remote_copy(
                src_ref=_grad_piece(g_ins[w], kinds[w], h, cdim, 2 * x + y, 1 - c), dst_ref=land_ins[w].at[0],
                send_sem=send_sems.at[base], recv_sem=recv_sems.at[base],
                device_id=(x, y, 1 - c), device_id_type=MESH).start()
            for j, chip in enumerate(_other_chips(x, y)):
                for cc in range(2):
                    pltpu.make_async_remote_copy(
                        src_ref=_grad_piece(g_ins[w], kinds[w], h, cdim, 2 * chip[0] + chip[1], cc),
                        dst_ref=land_ins[w].at[1 + 2 * j + c],
                        send_sem=send_sems.at[base + 1 + 2 * j + cc], recv_sem=recv_sems.at[base + 1 + 2 * j + c],
                        device_id=(*chip, cc), device_id_type=MESH).start()
        token[...] = jnp.zeros_like(token)

    sems = pltpu.SemaphoreType.DMA(((N_DEV - 1) * n,))
    outs = pl.pallas_call(
        body, name=name,
        out_shape=(sems, sems) + tuple(pltpu.HBM(a.shape, a.dtype) for a in list(grads) + lands)
        + (jax.ShapeDtypeStruct((SUBLANE, 128), F32),),
        in_specs=[HBM_SPEC] * (2 * n), out_specs=(SEM_SPEC, SEM_SPEC) + (HBM_SPEC,) * (2 * n) + (VMEM_SPEC,),
        input_output_aliases={i: 2 + i for i in range(2 * n)},
        compiler_params=pltpu.CompilerParams(has_side_effects=EFFECT),
    )(*[_hbm(a) for a in list(grads) + lands])
    return outs[0], outs[1], list(outs[2:2 + n]), list(outs[2 + n:2 + 2 * n]), outs[2 + 2 * n]


def _reduce_wait(send_sems, recv_sems, grads, lands, kinds, after, name):
    n = len(grads)
    dims = [_grad_dims(g, kind) for g, kind in zip(grads, kinds)]

    def body(*refs):
        g_ins, land_ins = refs[:n], refs[n:2 * n]
        send_sems, recv_sems = refs[2 * n], refs[2 * n + 1]
        x, y, c = _place()
        for w in range(n):
            h, cdim = dims[w]
            piece = _grad_piece(g_ins[w], kinds[w], h, cdim, 2 * x + y, c)
            for s in range(N_DEV - 1):
                k = (N_DEV - 1) * w + s
                slot = land_ins[w].at[s]
                pltpu.make_async_remote_copy(
                    src_ref=piece, dst_ref=slot, send_sem=send_sems.at[k], recv_sem=recv_sems.at[k],
                    device_id=(x, y, 1 - c), device_id_type=MESH).wait_send()
                pltpu.make_async_remote_copy(
                    src_ref=piece, dst_ref=slot, send_sem=send_sems.at[k], recv_sem=recv_sems.at[k],
                    device_id=(x, y, 1 - c), device_id_type=MESH).wait_recv()

    outs = pl.pallas_call(
        body, name=name, out_shape=tuple(pltpu.HBM(a.shape, a.dtype) for a in list(grads) + list(lands)),
        in_specs=[HBM_SPEC] * (2 * n) + [SEM_SPEC, SEM_SPEC, ANY_SPEC], out_specs=(HBM_SPEC,) * (2 * n),
        input_output_aliases={i: i for i in range(2 * n)},
        compiler_params=pltpu.CompilerParams(has_side_effects=EFFECT),
    )(*grads, *lands, send_sems, recv_sems, after)
    return list(outs[:n]), list(outs[n:])


def _add_pieces(g, land, kind, chip_idx, core_idx, name):
    _, h, cdim = land.shape
    tr, tc = _tile(h, 256), _tile(cdim, 2048)
    nrb, ncb = h // tr, cdim // tc

    def body(k_ref, c_ref, g_ref, l_ref, o_ref):
        acc = g_ref[...].astype(F32)
        for s in range(N_DEV - 1):
            acc = acc + l_ref[s].astype(F32)
        o_ref[...] = acc

    if kind == "col":
        g_spec = pl.BlockSpec((tr, tc), lambda i, j, kr, cr: (cr[0] * nrb + i, kr[0] * ncb + j))
    else:
        g_spec = pl.BlockSpec((None, tr, tc), lambda i, j, kr, cr: (kr[0], cr[0] * nrb + i, j))
    return pl.pallas_call(
        body, name=name, out_shape=jax.ShapeDtypeStruct((2 * h, cdim), F32),
        grid_spec=pltpu.PrefetchScalarGridSpec(
            num_scalar_prefetch=2, grid=(nrb, ncb),
            in_specs=[g_spec, pl.BlockSpec((N_DEV - 1, tr, tc), lambda i, j, kr, cr: (0, i, j))],
            out_specs=pl.BlockSpec((tr, tc), lambda i, j, kr, cr: (cr[0] * nrb + i, j))),
        compiler_params=_params("parallel", "parallel"),
    )(chip_idx, core_idx, g, land)


def _join_halves(shards, name):
    n = len(shards)

    def body(*refs):
        ins, outs = refs[:n], refs[n:2 * n]
        send_sems, recv_sems = refs[2 * n:]
        x, y, c = _place()
        copies = []
        for w in range(n):
            h = shards[w].shape[0] // 2
            cp = pltpu.make_async_remote_copy(
                src_ref=ins[w].at[pl.ds(c * h, h), :], dst_ref=outs[w].at[pl.ds(c * h, h), :],
                send_sem=send_sems.at[w], recv_sem=recv_sems.at[w], device_id=(x, y, 1 - c), device_id_type=MESH)
            cp.start()
            copies.append(cp)
        for w in range(n):
            h = shards[w].shape[0] // 2
            theirs = outs[w].at[pl.ds((1 - c) * h, h), :]
            pltpu.make_async_remote_copy(
                src_ref=theirs, dst_ref=theirs, send_sem=send_sems.at[w], recv_sem=recv_sems.at[w],
                device_id=(x, y, 1 - c), device_id_type=MESH).wait_recv()
        for cp in copies:
            cp.wait_send()

    return pl.pallas_call(
        body, name=name, out_shape=tuple(jax.ShapeDtypeStruct(s.shape, s.dtype) for s in shards),
        in_specs=[HBM_SPEC] * n, out_specs=tuple([HBM_SPEC] * n),
        input_output_aliases={w: w for w in range(n)},
        scratch_shapes=[pltpu.SemaphoreType.DMA((n,))] * 2,
    )(*shards)


def _sum_devices(gathered, rows, name):
    n = gathered.shape[1]

    def body(g_ref, o_ref):
        acc = g_ref[0:rows, :]
        for dev in range(1, N_DEV):
            acc = acc + g_ref[dev * rows:(dev + 1) * rows, :]
        o_ref[...] = acc

    return pl.pallas_call(
        body, name=name, out_shape=jax.ShapeDtypeStruct((rows, n), F32),
        in_specs=[VMEM_SPEC], out_specs=VMEM_SPEC,
        compiler_params=pltpu.CompilerParams(vmem_limit_bytes=VMEM_LIMIT),
    )(gathered)


def _pack(vectors, width):
    flat = [v.reshape(-1) for v in vectors]
    offsets, total = [], 0
    for f in flat:
        offsets.append(total)
        total += f.shape[0]
    rows = -(-total // (width * SUBLANE)) * SUBLANE
    flat.append(jnp.zeros((rows * width - total,), F32))
    return jnp.concatenate(flat).reshape(rows, width), offsets


def kernel(x, c, mod_w, mod_b, norm_g, a_w_in, a_conv_w, a_conv_b, a_w_out, b_w_in, b_ln_g, b_ln_b, b_w_s, b_b_s, b_w_out, final_g, loss_target, m_mod_w, m_mod_b, m_norm_g, m_a_w_in, m_a_conv_w, m_a_conv_b, m_a_w_out, m_b_w_in, m_b_ln_g, m_b_ln_b, m_b_w_s, m_b_b_s, m_b_w_out, m_final_g, v_mod_w, v_mod_b, v_norm_g, v_a_w_in, v_a_conv_w, v_a_conv_b, v_a_w_out, v_b_w_in, v_b_ln_g, v_b_ln_b, v_b_w_s, v_b_b_s, v_b_w_out, v_final_g):
    seq, d = x.shape[1], x.shape[2]
    e = a_conv_b.shape[1]
    wd = mod_w.shape[2]
    ax, ay, ac = _place()
    chip = 2 * ax + ay
    dev = 2 * chip + ac
    chip_idx = jnp.reshape(chip, (1,)).astype(jnp.int32)
    core_idx = jnp.reshape(ac, (1,)).astype(jnp.int32)

    x2d = x[0]
    target = loss_target[0]

    w_names = ["a_w_in", "a_w_out", "b_w_in", "b_w_out"]
    w_kinds = ["col", "row", "col", "row"]
    w_shards = [a_w_in[0], a_w_out[0], b_w_in[0], b_w_out[0]]
    w_shapes = [sh.shape for sh in w_shards]
    placed = [_place_own(sh, kind, chip_idx, "place_" + nm) for nm, sh, kind in zip(w_names, w_shards, w_kinds)]
    a_groups, far_groups = [[(0, 0), (0, 1)]], [[(0, 2)]]
    whole_group = [[(0, 0), (0, 1), (0, 2)]]

    es = e // N_CHIP
    packed0, offs0 = _pack([c, a_conv_w, b_ln_g, b_ln_b], 1024)
    gathered0 = _all_gather(packed0, "gather_params").reshape(N_DEV, -1)
    c_all = gathered0[:, :d]
    per_chip = gathered0[0::2]

    def from_chips_cols(k, rows_):
        got = per_chip[:, offs0[k]:offs0[k] + rows_ * es].reshape(N_CHIP, rows_, es)
        return jnp.transpose(got, (1, 0, 2)).reshape(rows_, e)

    conv_w_full = from_chips_cols(1, 3)
    conv_w = [conv_w_full[k:k + 1] for k in range(3)]
    ln_g, ln_b = from_chips_cols(2, 1), from_chips_cols(3, 1)
    mod_b_cols = lax.dynamic_slice_in_dim(mod_b, chip * wd, wd, axis=1)[:, None, :]
    c_act, mod_part = _mod_fwd(c_all, mod_w, mod_b_cols, "mod_fwd")
    n_layer = mod_w.shape[0]
    mod_gathered = _all_gather(mod_part, "gather_mod")
    a_sems, (wa_in,), a_token = _gather_weights_start(
        placed[:1], w_kinds[:1], w_shapes[:1], a_groups, mod_gathered, "gather_a_w_in_start", both_cores=False)
    mod_all = mod_gathered.reshape(N_CHIP, 2, n_layer, N_DEV, wd)[:, 0]
    mod_all = jnp.transpose(mod_all, (1, 2, 0, 3)).reshape(n_layer, N_DEV, N_CHIP * wd)
    mod_me = lax.dynamic_index_in_dim(mod_all, dev, axis=1, keepdims=False)
    shift = [mod_me[l:l + 1, 0:d] for l in range(n_layer)]
    scale = [mod_me[l:l + 1, d:2 * d] for l in range(n_layer)]
    gate = [mod_me[l:l + 1, 2 * d:3 * d] for l in range(n_layer)]

    g0, g1, gf = norm_g[0:1], norm_g[1:2], final_g[None, :]
    h0 = _norm_mod(x2d, g0, scale[0], shift[0], "norm_mod0")

    def slab(r):
        return jnp.bitwise_xor(chip_idx, r)

    def arrived(sems, group, weight, w, after, name, both_cores=True):
        return _gather_weights_wait(sems, [(0, j) for _, j in group], [weight], [w_kinds[w]], [w_shapes[w]],
                                    after, name, both_cores)[0]

    proj0 = _mm_proj_slab(h0, wa_in, None, slab(0), 4, "a_proj_own", a_token)
    wa_in = arrived(a_sems[0], a_groups[0], wa_in, 0, proj0, "gather_wait_near", both_cores=False)
    wa_in = _forward_to_sibling(wa_in, w_kinds[0], w_shapes[0], [0, 1], "forward_near")
    far_sems, (wa_in,), far_token = _gather_weights_start(
        [wa_in], w_kinds[:1], w_shapes[:1], far_groups, a_token, "gather_a_w_in_far_start")
    proj0 = _mm_proj_slab(h0, wa_in, proj0, slab(2), 4, "a_proj_x", far_token)
    proj0 = _mm_proj_slab(h0, wa_in, proj0, slab(1), 4, "a_proj_y", far_token)
    wa_in = arrived(far_sems[0], far_groups[0], wa_in, 0, proj0, "gather_wait_far")
    def start_whole(w, after, name):
        sems, (flight,), token = _gather_weights_start(
            placed[w:w + 1], w_kinds[w:w + 1], w_shapes[w:w + 1], whole_group, after, name)
        return sems[0], flight, token

    ao_sems, wa_out, ao_token = start_whole(1, wa_in, "gather_a_w_out_start")
    proj0 = _mm_proj_slab(h0, wa_in, proj0, slab(3), 4, "a_proj_far", ao_token)
    wa_out = arrived(ao_sems, whole_group[0], wa_out, 1, proj0, "gather_wait_a_w_out")
    bi_sems, wb_in, bi_token = start_whole(2, wa_out, "gather_b_w_in_start")
    y0, br0, x1, h1 = _conv_fwd(proj0, conv_w, a_conv_b, wa_out, x2d, gate[0], g1, scale[1], shift[1],
                                "conv_fwd", bi_token)
    wb_in = arrived(bi_sems, whole_group[0], wb_in, 2, h1, "gather_wait_b_w_in")
    bo_sems, wb_out, bo_token = start_whole(3, wb_in, "gather_b_w_out_start")
    proj1 = _mm_proj(h1, wb_in, 3, "b_proj", bo_token)
    b_s_t = jnp.transpose(b_b_s[0])
    wb_out = arrived(bo_sems, whole_group[0], wb_out, 3, proj1, "gather_wait_b_w_out")
    y1, dx2, dbr1, loss_part, g_final_g, dgate1 = _gmlp_fwd_head(
        proj1, ln_g, ln_b, b_w_s[0], b_s_t, wb_out, x1, gate[1], gf, target, "gmlp_fwd_head")

    gw_b_out = _mm_dw_out(y1, dbr1, "b_out_dw")
    dproj1, g_w_s, g_b_s_t, g_ln_g, g_ln_b = _gmlp_bwd(
        proj1, dbr1, wb_out, ln_g, ln_b, b_w_s[0], jnp.swapaxes(b_w_s[0], 1, 2), b_s_t, "gmlp_bwd")
    gw_b_in = _mm_dw_in(h1, dproj1, "b_proj_dw")
    b_kinds = ["col", "row"]
    b_send, b_recv, b_grads, b_lands, b_token = _reduce_start(
        [gw_b_in, gw_b_out.reshape(N_CHIP, e // N_CHIP, d)], b_kinds, "reduce_b_start")
    dx1, dshift1, dscale1, g_g1, dbr0, dgate0 = _mm_dh_norm_bwd(
        dproj1, wb_in, x1, dx2, g1, scale[1], "b_proj_dx", b_token, br=br0, gate=gate[0])

    early = [loss_part[0, 0:1], g_final_g, g_g1, g_ln_g, g_ln_b, jnp.transpose(g_b_s_t), g_w_s,
             jnp.concatenate([dshift1, dscale1, dgate1, dgate0], axis=1)]
    packed_e, offs_e = _pack(early, 1024)
    rows_e = packed_e.shape[0]
    blocks_e = lax.dynamic_update_slice(jnp.zeros((N_DEV, rows_e, 1024), F32), packed_e[None], (dev, 0, 0))
    e_send, e_recv, blocks_e, e_token = _all_gather_start(blocks_e, "gather_small_start")

    gw_a_out = _mm_dw_out(y0, dbr0, "a_out_dw")
    ao_send, ao_recv, ao_grads, ao_lands, ao_grad_token = _reduce_start(
        [gw_a_out.reshape(N_CHIP, e // N_CHIP, d)], ["row"], "reduce_a_out_start")
    dproj0, g_w0, g_w1, g_w2, g_conv_b = _conv_bwd(
        proj0, dbr0, wa_out, conv_w, a_conv_b, "conv_bwd", ao_grad_token + e_token)
    gw_a_in = _mm_dw_in(h0, dproj0, "a_proj_dw")
    ai_send, ai_recv, ai_grads, ai_lands, ai_token = _reduce_start([gw_a_in], ["col"], "reduce_a_in_start")
    grad_x, dshift0, dscale0, g_g0 = _mm_dh_norm_bwd(dproj0, wa_in, x2d, dx1, g0, scale[0], "a_proj_dx", ai_token)

    def finish(send, recv, grads_, lands_, kinds_, names_, after, tag):
        grads_, lands_ = _reduce_wait(send, recv, grads_, lands_, kinds_, after, "reduce_" + tag + "_wait")
        halves = [_add_pieces(g, land, kind, chip_idx, core_idx, "add_pieces_" + nm)
                  for g, land, kind, nm in zip(grads_, lands_, kinds_, names_)]
        return _join_halves(halves, "join_" + tag)

    upd, big_grads = {}, {}

    def adamw_big(nm, w, g, m, v):
        g_out, *rest = _adamw_2d(w[0], g, m[0], v[0], "adamw_" + nm)
        big_grads[nm] = g_out[None]
        upd[nm] = tuple(o[None] for o in rest)

    g_b_w_in, g_b_w_out = finish(b_send, b_recv, b_grads, b_lands, b_kinds, ["b_w_in", "b_w_out"], grad_x, "b")
    adamw_big("b_w_in", b_w_in, g_b_w_in, m_b_w_in, v_b_w_in)
    adamw_big("b_w_out", b_w_out, g_b_w_out, m_b_w_out, v_b_w_out)

    late = [g_g0, jnp.concatenate([g_w0, g_w1, g_w2], axis=0), g_conv_b, jnp.concatenate([dshift0, dscale0], axis=1)]
    packed_l, offs_l = _pack(late, 1024)
    rows_l = packed_l.shape[0]
    gathered_l = _all_gather(packed_l, "gather_small_late", after=upd["b_w_out"][0])
    blocks_e = _all_gather_wait(e_send, e_recv, blocks_e, gathered_l, "gather_small_wait")
    gathered_e = blocks_e.reshape(N_DEV * rows_e, 1024)
    summed_e = _sum_devices(gathered_e, rows_e, "sum_small_early").reshape(-1)
    summed_l = _sum_devices(gathered_l, rows_l, "sum_small_late").reshape(-1)

    def take(summed, offs, k, shape):
        size = math.prod(shape)
        return summed[offs[k]:offs[k] + size].reshape(shape)

    def rows_of(gathered, rows_, offs, k, width):
        return gathered.reshape(N_DEV, rows_ * 1024)[:, offs[k]:offs[k] + width]

    loss = take(summed_e, offs_e, 0, ())
    grad_final_g = take(summed_e, offs_e, 1, (d,))
    grad_norm_g = jnp.concatenate([take(summed_l, offs_l, 0, (1, d)), take(summed_e, offs_e, 2, (1, d))], axis=0)
    grad_ln_g_full = take(summed_e, offs_e, 3, (1, e))
    grad_ln_b_full = take(summed_e, offs_e, 4, (1, e))
    grad_b_b_s = take(summed_e, offs_e, 5, (1, GROUPS, CHUNK))
    grad_b_w_s = take(summed_e, offs_e, 6, (1, GROUPS, CHUNK, CHUNK))
    grad_conv_w_full = take(summed_l, offs_l, 1, (3, e))
    grad_a_conv_b = take(summed_l, offs_l, 2, (1, e))
    grad_a_conv_w = lax.dynamic_slice_in_dim(grad_conv_w_full, chip * es, es, axis=1)[None]
    grad_b_ln_g = lax.dynamic_slice_in_dim(grad_ln_g_full, chip * es, es, axis=1)
    grad_b_ln_b = lax.dynamic_slice_in_dim(grad_ln_b_full, chip * es, es, axis=1)
    dmod_e = rows_of(gathered_e, rows_e, offs_e, 7, 4 * d)
    dmod_l = rows_of(gathered_l, rows_l, offs_l, 3, 2 * d)
    dmod_all = jnp.stack([jnp.concatenate([dmod_l, dmod_e[:, 3 * d:]], axis=1), dmod_e[:, :3 * d]], axis=1)
    mod_b_e = take(summed_e, offs_e, 7, (4 * d,))
    mod_b_l = take(summed_l, offs_l, 3, (2 * d,))
    grad_mod_b = jnp.stack([jnp.concatenate([mod_b_l, mod_b_e[3 * d:]]), mod_b_e[:3 * d]])
    dmod_cols = jnp.transpose(lax.dynamic_slice_in_dim(dmod_all, chip * wd, wd, axis=2), (1, 0, 2))

    grad_mod_w, delta_mod_w, new_m_mod_w, new_v_mod_w = _mod_w_update(
        jnp.transpose(c_act), dmod_cols, mod_w, m_mod_w, v_mod_w, "mod_w_update")
    (g_a_w_out,) = finish(ao_send, ao_recv, ao_grads, ao_lands, ["row"], ["a_w_out"], delta_mod_w, "a_out")
    adamw_big("a_w_out", a_w_out, g_a_w_out, m_a_w_out, v_a_w_out)
    (g_a_w_in,) = finish(ai_send, ai_recv, ai_grads, ai_lands, ["col"], ["a_w_in"], upd["a_w_out"][0], "a_in")
    adamw_big("a_w_in", a_w_in, g_a_w_in, m_a_w_in, v_a_w_in)
    small_w = [("mod_b", mod_b, grad_mod_b, m_mod_b, v_mod_b), ("norm_g", norm_g, grad_norm_g, m_norm_g, v_norm_g),
               ("a_conv_w", a_conv_w, grad_a_conv_w, m_a_conv_w, v_a_conv_w),
               ("a_conv_b", a_conv_b, grad_a_conv_b, m_a_conv_b, v_a_conv_b),
               ("b_ln_g", b_ln_g, grad_b_ln_g, m_b_ln_g, v_b_ln_g), ("b_ln_b", b_ln_b, grad_b_ln_b, m_b_ln_b, v_b_ln_b),
               ("b_w_s", b_w_s, grad_b_w_s, m_b_w_s, v_b_w_s), ("b_b_s", b_b_s, grad_b_b_s, m_b_b_s, v_b_b_s),
               ("final_g", final_g, grad_final_g, m_final_g, v_final_g)]

    def flat2d(a):
        return a.reshape(-1, a.shape[-1])

    res = _adamw_small([flat2d(t[1]) for t in small_w], [flat2d(t[2]) for t in small_w],
                       [flat2d(t[3]) for t in small_w], [flat2d(t[4]) for t in small_w], "adamw_small")
    for (nm, w, _, _, _), r3 in zip(small_w, res):
        upd[nm] = tuple(o.reshape(w.shape) for o in r3)
    upd["mod_w"] = (delta_mod_w, new_m_mod_w, new_v_mod_w)

    grads = {"mod_w": grad_mod_w, "mod_b": grad_mod_b, "norm_g": grad_norm_g, "a_conv_w": grad_a_conv_w,
             "a_conv_b": grad_a_conv_b, "b_ln_g": grad_b_ln_g, "b_ln_b": grad_b_ln_b, "b_w_s": grad_b_w_s,
             "b_b_s": grad_b_b_s, "final_g": grad_final_g, **big_grads}
    order = ["mod_w", "mod_b", "norm_g", "a_w_in", "a_conv_w", "a_conv_b", "a_w_out", "b_w_in", "b_ln_g", "b_ln_b",
             "b_w_s", "b_b_s", "b_w_out", "final_g"]
    return (loss, grad_x[None], *[grads[k] for k in order], *[upd[k][0] for k in order],
            *[upd[k][1] for k in order], *[upd[k][2] for k in order])
```

```python
import functools
import math

import jax
import jax.numpy as jnp
from jax import lax
from jax.experimental import pallas as pl
from jax.experimental.pallas import tpu as pltpu

F32 = jnp.float32
BF16 = jnp.bfloat16
MESH = pl.DeviceIdType.MESH

N_DEV = 8
N_CHIP = 4
SUBLANE = 8
PACK = 16
ACT = F32
RMS_EPS = 1e-6
LN_EPS = 1e-5
CHUNK = 128
GROUPS = 8
ADAM_LR = 0.001
ADAM_B1 = 0.9
ADAM_B2 = 0.999
ADAM_EPS = 1e-08
ADAM_WD = 0.01
ADAM_STEP = 10
VMEM_LIMIT = 56 << 20

HBM_SPEC = pl.BlockSpec(memory_space=pltpu.HBM)
VMEM_SPEC = pl.BlockSpec(memory_space=pltpu.VMEM)
SEM_SPEC = pl.BlockSpec(memory_space=pltpu.SEMAPHORE)
ANY_SPEC = pl.BlockSpec(memory_space=pl.ANY)
EFFECT = pltpu.SideEffectType.DATAFLOW_SIDE_EFFECTING


def _params(*sem):
    return pltpu.CompilerParams(dimension_semantics=sem, vmem_limit_bytes=VMEM_LIMIT)


def _tile(n, want):
    if n <= want:
        return n
    t = want
    while n % t:
        t -= 128
    return t


def _sigmoid(x):
    return 0.5 * jnp.tanh(0.5 * x) + 0.5


def _silu_and_grad(x):
    s = _sigmoid(x)
    return x * s, s * (1.0 + x * (1.0 - s))


def _gelu_and_grad(x):
    cdf = 0.5 * (1.0 + lax.erf(x * (1.0 / math.sqrt(2.0))))
    pdf = jnp.exp(-0.5 * x * x) * (1.0 / math.sqrt(2.0 * math.pi))
    return x * cdf, cdf + x * pdf


def _gelu(x):
    return x * (0.5 * (1.0 + lax.erf(x * (1.0 / math.sqrt(2.0)))))


def _rms(x):
    r = lax.rsqrt(jnp.mean(x * x, axis=-1, keepdims=True) + RMS_EPS)
    return x * r, r


def _rms_bwd(dxn, xn, r):
    return r * (dxn - xn * jnp.mean(dxn * xn, axis=-1, keepdims=True))


def _colsum(a):
    return jnp.sum(a, axis=0, keepdims=True)


def _shift_down(cur, before, k):
    rolled = pltpu.roll(cur, k, 0)
    row = lax.broadcasted_iota(jnp.int32, before.shape, 0)
    head = jnp.where(row < k, pltpu.roll(before, k, 0), rolled[:SUBLANE])
    return jnp.concatenate([head, rolled[SUBLANE:]], axis=0)


def _shift_up(cur, after, k):
    n = cur.shape[0]
    rolled = pltpu.roll(cur, n - k, 0)
    row = lax.broadcasted_iota(jnp.int32, after.shape, 0)
    tail = jnp.where(row >= SUBLANE - k, pltpu.roll(after, SUBLANE - k, 0), rolled[n - SUBLANE:])
    return jnp.concatenate([rolled[:n - SUBLANE], tail], axis=0)


def _after_spec():
    return pl.BlockSpec((SUBLANE, 128), lambda *_: (0, 0))


def _mm_proj(h, w, n_split, name, after):
    s, d = h.shape
    e = w.shape[1] // n_split
    tm, tn = _tile(s, 2048), _tile(e, 2048)
    nj = e // tn

    def body(h_ref, w_ref, after_ref, o_ref):
        o_ref[...] = jnp.dot(h_ref[...], w_ref[...], preferred_element_type=F32).astype(ACT)

    return pl.pallas_call(
        body, name=name,
        out_shape=jax.ShapeDtypeStruct((n_split, s, e), ACT),
        grid=(s // tm, n_split * nj),
        in_specs=[pl.BlockSpec((tm, d), lambda i, j: (i, 0)), pl.BlockSpec((d, tn), lambda i, j: (0, j)),
                  _after_spec()],
        out_specs=pl.BlockSpec((None, tm, tn), lambda i, j: (j // nj, i, j % nj)),
        compiler_params=_params("parallel", "parallel"),
    )(h, w, after)


def _mm_proj_slab(h, w, proj, q_idx, n_split, name, after):
    s, d = h.shape
    e = w.shape[1] // n_split
    tm, tn = _tile(s, 1024), _tile(e, 2048)
    nj = e // tn

    def body(q_ref, h_ref, w_ref, *rest):
        o_ref = rest[-1]
        o_ref[...] = jnp.dot(h_ref[...], w_ref[...], preferred_element_type=F32).astype(ACT)

    in_specs = [pl.BlockSpec((tm, d), lambda i, j, qr: (i, 0)), pl.BlockSpec((d, tn), lambda i, j, qr: (0, qr[0] * nj + j)),
                _after_spec()]
    args = [q_idx, h, w, after]
    aliases = {}
    if proj is not None:
        in_specs.append(ANY_SPEC)
        args.append(proj)
        aliases = {4: 0}
    return pl.pallas_call(
        body, name=name,
        out_shape=jax.ShapeDtypeStruct((n_split, s, e), ACT),
        grid_spec=pltpu.PrefetchScalarGridSpec(
            num_scalar_prefetch=1, grid=(s // tm, nj), in_specs=in_specs,
            out_specs=pl.BlockSpec((None, tm, tn), lambda i, j, qr: (qr[0], i, j))),
        input_output_aliases=aliases,
        compiler_params=_params("parallel", "parallel"),
    )(*args)


def _mm_dh_norm_bwd(dp, w, x, dx_in, g, scale, name, after, br=None, gate=None):
    nq, s, e = dp.shape
    d = w.shape[0]
    has_branch = br is not None
    tm, tk = _tile(s, 1024), _tile(e, 1024 if has_branch else 2048)
    nkq = e // tk
    nk = nq * nkq

    def body(*refs):
        if has_branch:
            (a_ref, b_ref, after_ref, x_ref, dxin_ref, g_ref, sc_ref, br_ref, gate_ref,
             dx_ref, dsh_ref, dsc_ref, dg_ref, dbr_ref, dgate_ref, acc_ref) = refs
        else:
            (a_ref, b_ref, after_ref, x_ref, dxin_ref, g_ref, sc_ref,
             dx_ref, dsh_ref, dsc_ref, dg_ref, acc_ref) = refs
        i, k = pl.program_id(0), pl.program_id(1)

        @pl.when(jnp.logical_and(i == 0, k == 0))
        def _():
            dsh_ref[...] = jnp.zeros_like(dsh_ref)
            dsc_ref[...] = jnp.zeros_like(dsc_ref)
            dg_ref[...] = jnp.zeros_like(dg_ref)
            if has_branch:
                dgate_ref[...] = jnp.zeros_like(dgate_ref)

        @pl.when(k == 0)
        def _():
            acc_ref[...] = jnp.zeros_like(acc_ref)
        acc_ref[...] += lax.dot_general(a_ref[...], b_ref[...], (((1,), (1,)), ((), ())), preferred_element_type=F32)

        @pl.when(k == nk - 1)
        def _():
            dh = acc_ref[...]
            g_vec = g_ref[...]
            xn, r = _rms(x_ref[...])
            dsh_ref[...] += _colsum(dh)
            dsc_ref[...] += _colsum(dh * (xn * g_vec))
            da = dh * (1.0 + sc_ref[...])
            dg_ref[...] += _colsum(da * xn)
            dx = dxin_ref[...] + _rms_bwd(da * g_vec, xn, r)
            dx_ref[...] = dx
            if has_branch:
                dgate_ref[...] += _colsum(dx * br_ref[...])
                dbr_ref[...] = (gate_ref[...] * dx).astype(BF16)

    rows = pl.BlockSpec((tm, d), lambda i, k: (i, 0))
    vec = pl.BlockSpec((1, d), lambda i, k: (0, 0))
    vec_out = jax.ShapeDtypeStruct((1, d), F32)
    in_specs = [pl.BlockSpec((None, tm, tk), lambda i, k: (k // nkq, i, k % nkq)),
                pl.BlockSpec((d, tk), lambda i, k: (0, k)), _after_spec(), rows, rows, vec, vec]
    out_shape = [jax.ShapeDtypeStruct((s, d), F32), vec_out, vec_out, vec_out]
    out_specs = [rows, vec, vec, vec]
    args = [dp, w, after, x, dx_in, g, scale]
    if has_branch:
        in_specs += [rows, vec]
        out_shape += [jax.ShapeDtypeStruct((s, d), BF16), vec_out]
        out_specs += [rows, vec]
        args += [br, gate]
    return pl.pallas_call(
        body, name=name, out_shape=tuple(out_shape), grid=(s // tm, nk),
        in_specs=in_specs, out_specs=tuple(out_specs), scratch_shapes=[pltpu.VMEM((tm, d), F32)],
        compiler_params=_params("arbitrary", "arbitrary"),
    )(*args)


def _dw_body(n_t):
    def body(a_ref, b_ref, o_ref, acc_ref):
        t = pl.program_id(2)

        @pl.when(t == 0)
        def _():
            acc_ref[...] = jnp.zeros_like(acc_ref)
        acc_ref[...] += lax.dot_general(a_ref[...], b_ref[...], (((0,), (0,)), ((), ())), preferred_element_type=F32)

        @pl.when(t == n_t - 1)
        def _():
            o_ref[...] = acc_ref[...].astype(o_ref.dtype)
    return body


def _mm_dw_in(h, dp, name):
    s, d = h.shape
    nq, _, e = dp.shape
    tm, tn, tt = _tile(d, 1024), _tile(e, 2048), _tile(s, 2048)
    nj = e // tn

    return pl.pallas_call(
        _dw_body(s // tt), name=name,
        out_shape=jax.ShapeDtypeStruct((d, nq * e), BF16),
        grid=(d // tm, nq * nj, s // tt),
        in_specs=[pl.BlockSpec((tt, tm), lambda i, j, t: (t, i)),
                  pl.BlockSpec((None, tt, tn), lambda i, j, t: (j // nj, t, j % nj))],
        out_specs=pl.BlockSpec((tm, tn), lambda i, j, t: (i, j)),
        scratch_shapes=[pltpu.VMEM((tm, tn), F32)],
        compiler_params=_params("parallel", "parallel", "arbitrary"),
    )(h, dp)


def _mm_dw_out(y, dbr, name):
    s, e = y.shape
    d = dbr.shape[1]
    tm, tn, tt = _tile(e, 2048), _tile(d, 1024), _tile(s, 2048)

    return pl.pallas_call(
        _dw_body(s // tt), name=name,
        out_shape=jax.ShapeDtypeStruct((e, d), BF16),
        grid=(e // tm, d // tn, s // tt),
        in_specs=[pl.BlockSpec((tt, tm), lambda i, j, t: (t, i)), pl.BlockSpec((tt, tn), lambda i, j, t: (t, j))],
        out_specs=pl.BlockSpec((tm, tn), lambda i, j, t: (i, j)),
        scratch_shapes=[pltpu.VMEM((tm, tn), F32)],
        compiler_params=_params("parallel", "parallel", "arbitrary"),
    )(y, dbr)


def _row_spec(ts, d):
    return pl.BlockSpec((ts, d), lambda i: (i, 0))


def _vec_spec(d):
    return pl.BlockSpec((1, d), lambda i: (0, 0))


def _norm_mod(x, g, scale, shift, name):
    s, d = x.shape
    ts = _tile(s, 512)

    def body(x_ref, g_ref, sc_ref, sh_ref, h_ref):
        xn, _ = _rms(x_ref[...])
        h_ref[...] = ((xn * g_ref[...]) * (1.0 + sc_ref[...]) + sh_ref[...]).astype(BF16)

    return pl.pallas_call(
        body, name=name, out_shape=jax.ShapeDtypeStruct((s, d), BF16), grid=(s // ts,),
        in_specs=[_row_spec(ts, d), _vec_spec(d), _vec_spec(d), _vec_spec(d)],
        out_specs=_row_spec(ts, d), compiler_params=_params("parallel"),
    )(x, g, scale, shift)


def _head_tile(x1, br, gate, gf, target):
    d = x1.shape[-1]
    xn, r = _rms(x1 + gate * br)
    err = xn * gf - target
    loss = 0.5 * jnp.sum(jnp.mean(err * err, axis=-1, keepdims=True))
    dout = err * (1.0 / d)
    dx = _rms_bwd(dout * gf, xn, r)
    return dx, gate * dx, loss, _colsum(dout * xn), _colsum(dx * br)


def _f32(ref):
    return ref[...].astype(F32)


def _cols_f32(ref, cols):
    return ref[:, cols].astype(F32)


def _rows_before(halo_ref, cols=slice(None)):
    return _cols_f32(halo_ref, cols)[PACK - SUBLANE:]


def _rows_after(halo_ref, cols=slice(None)):
    return _cols_f32(halo_ref, cols)[:SUBLANE]


CONV_CHAIN_COLS = 512


def _conv_fwd(proj, conv_w, conv_b, w_out, x, gate, g_next, scale_next, shift_next, name, after):
    _, s, e = proj.shape
    d = w_out.shape[1]
    ts, te = _tile(s, 256), e
    cw = min(CONV_CHAIN_COLS, te)
    hb = ts // PACK

    def body(bg_ref, cg_ref, xi_ref, z_ref, cgp_ref, xip_ref, w0_ref, w1_ref, w2_ref, b_ref, wo_ref, after_ref,
             x_ref, gate_ref, g_ref, sc_ref, sh_ref, y_ref, br_ref, x1_ref, h_ref):
        first = pl.program_id(0) == 0
        br = None
        for c0 in range(0, te, cw):
            cols = slice(c0, c0 + cw)
            cx = _cols_f32(cg_ref, cols) * _cols_f32(xi_ref, cols)
            before = jnp.where(first, 0.0, _rows_before(cgp_ref, cols) * _rows_before(xip_ref, cols))
            conv = b_ref[:, cols] + w2_ref[:, cols] * cx
            conv = conv + w0_ref[:, cols] * _shift_down(cx, before, 2)
            conv = conv + w1_ref[:, cols] * _shift_down(cx, before, 1)
            z = _cols_f32(z_ref, cols)
            y = ((z * _sigmoid(z)) * _cols_f32(bg_ref, cols) * conv).astype(BF16)
            y_ref[:, cols] = y
            part_br = jnp.dot(y, wo_ref[cols, :], preferred_element_type=F32)
            br = part_br if br is None else br + part_br
        br_ref[...] = br
        x1 = x_ref[...] + gate_ref[...] * br
        x1_ref[...] = x1
        xn, _ = _rms(x1)
        h_ref[...] = ((xn * g_ref[...]) * (1.0 + sc_ref[...]) + sh_ref[...]).astype(BF16)

    def part(q):
        return pl.BlockSpec((None, ts, te), lambda i: (q, i, 0))

    def halo_before(q):
        return pl.BlockSpec((None, PACK, te), lambda i: (q, jnp.maximum(i * hb - 1, 0), 0))

    return pl.pallas_call(
        body, name=name,
        out_shape=(jax.ShapeDtypeStruct((s, e), BF16), jax.ShapeDtypeStruct((s, d), F32),
                   jax.ShapeDtypeStruct((s, d), F32), jax.ShapeDtypeStruct((s, d), BF16)), grid=(s // ts,),
        in_specs=[part(0), part(1), part(2), part(3), halo_before(1), halo_before(2)]
        + [_vec_spec(e)] * 4 + [pl.BlockSpec((e, d), lambda i: (0, 0)), _after_spec(), _row_spec(ts, d)]
        + [_vec_spec(d)] * 4,
        out_specs=(_row_spec(ts, e), _row_spec(ts, d), _row_spec(ts, d), _row_spec(ts, d)),
        compiler_params=_params("parallel"),
    )(proj, proj, proj, proj, proj, proj, *conv_w, conv_b, w_out, after, x, gate, g_next, scale_next, shift_next)


def _conv_bwd(proj, dbr, w_out, conv_w, conv_b, name, after):
    _, s, e = proj.shape
    d = dbr.shape[1]
    ts, te = _tile(s, 512), _tile(e, 1024)
    cw = min(CONV_CHAIN_COLS // 2, te)
    hb = ts // PACK
    n_i = s // ts
    last_halo = s // PACK - 1
    nt = (((1,), (1,)), ((), ()))

    def body(bg_ref, cg_ref, xi_ref, z_ref, dbr_ref, cgp_ref, xip_ref, bgn_ref, zn_ref, dbrn_ref, wo_ref,
             w0_ref, w1_ref, w2_ref, b_ref, after_ref, dp_ref, dw0_ref, dw1_ref, dw2_ref, db_ref):
        i = pl.program_id(1)

        @pl.when(i == 0)
        def _():
            for acc in (dw0_ref, dw1_ref, dw2_ref, db_ref):
                acc[...] = jnp.zeros_like(acc)
        for c0 in range(0, te, cw):
            cols = slice(c0, c0 + cw)
            wo = wo_ref[cols, :]
            dy = lax.dot_general(dbr_ref[...], wo, nt, preferred_element_type=F32)
            dyn = lax.dot_general(dbrn_ref[...], wo, nt, preferred_element_type=F32)[:SUBLANE]
            bg, cg = _cols_f32(bg_ref, cols), _cols_f32(cg_ref, cols)
            xi, z = _cols_f32(xi_ref, cols), _cols_f32(z_ref, cols)
            w0, w1, w2 = w0_ref[:, cols], w1_ref[:, cols], w2_ref[:, cols]
            cx = cg * xi
            before = jnp.where(i > 0, _rows_before(cgp_ref, cols) * _rows_before(xip_ref, cols), 0.0)
            cx1 = _shift_down(cx, before, 1)
            cx2 = _shift_down(cx, before, 2)
            conv = b_ref[:, cols] + w2 * cx
            conv = conv + w0 * cx2
            conv = conv + w1 * cx1
            sz, dsz = _silu_and_grad(z)
            dp_ref[3, :, cols] = (dy * bg * conv * dsz).astype(BF16)
            dp_ref[0, :, cols] = (dy * sz * conv).astype(BF16)
            dconv = dy * sz * bg
            zn = _rows_after(zn_ref, cols)
            after = jnp.where(i < n_i - 1, dyn * (zn * _sigmoid(zn)) * _rows_after(bgn_ref, cols), 0.0)
            db_ref[:, cols] += _colsum(dconv)
            dw2_ref[:, cols] += _colsum(dconv * cx)
            dw1_ref[:, cols] += _colsum(dconv * cx1)
            dw0_ref[:, cols] += _colsum(dconv * cx2)
            dcx = w2 * dconv + w1 * _shift_up(dconv, after, 1) + w0 * _shift_up(dconv, after, 2)
            dp_ref[1, :, cols] = (dcx * xi).astype(BF16)
            dp_ref[2, :, cols] = (dcx * cg).astype(BF16)

    def part(q):
        return pl.BlockSpec((None, ts, te), lambda j, i: (q, i, j))

    def halo_before(q):
        return pl.BlockSpec((None, PACK, te), lambda j, i: (q, jnp.maximum(i * hb - 1, 0), j))

    def halo_after(q):
        return pl.BlockSpec((None, PACK, te), lambda j, i: (q, jnp.minimum((i + 1) * hb, last_halo), j))

    return pl.pallas_call(
        body, name=name,
        out_shape=(jax.ShapeDtypeStruct((4, s, e), BF16),) + (jax.ShapeDtypeStruct((1, e), F32),) * 4,
        grid=(e // te, n_i),
        in_specs=[part(0), part(1), part(2), part(3), pl.BlockSpec((ts, d), lambda j, i: (i, 0)),
                  halo_before(1), halo_before(2), halo_after(0), halo_after(3),
                  pl.BlockSpec((PACK, d), lambda j, i: (jnp.minimum((i + 1) * hb, last_halo), 0)),
                  pl.BlockSpec((te, d), lambda j, i: (j, 0))]
        + [pl.BlockSpec((1, te), lambda j, i: (0, j))] * 4 + [_after_spec()],
        out_specs=(pl.BlockSpec((4, ts, te), lambda j, i: (0, i, j)),) + (pl.BlockSpec((1, te), lambda j, i: (0, j)),) * 4,
        compiler_params=_params("parallel", "arbitrary"),
    )(proj, proj, proj, proj, dbr, proj, proj, proj, proj, dbr, w_out, *conv_w, conv_b, after)


def _tril(w):
    row = lax.broadcasted_iota(jnp.int32, w.shape, 0)
    col = lax.broadcasted_iota(jnp.int32, w.shape, 1)
    return jnp.where(row >= col, w, 0.0)


def _triu(w):
    row = lax.broadcasted_iota(jnp.int32, w.shape, 0)
    col = lax.broadcasted_iota(jnp.int32, w.shape, 1)
    return jnp.where(row <= col, w, 0.0)


def _layer_norm_fwd(v, g, b):
    mu = jnp.mean(v, axis=-1, keepdims=True)
    vc = v - mu
    rstd = lax.rsqrt(jnp.mean(vc * vc, axis=-1, keepdims=True) + LN_EPS)
    vhat = vc * rstd
    return vhat * g + b, vhat, rstd


GMLP_CHUNKS_PER_STEP = 2


def _gmlp_rows(s):
    return CHUNK * min(GMLP_CHUNKS_PER_STEP, s // CHUNK)


def _mix_positions(w_ref, src_scr, dst_scr, gw, mask, bias_ref=None):
    for gi in range(GROUPS):
        cols = slice(gi * gw, (gi + 1) * gw)
        wm = mask(w_ref[gi]).astype(BF16)
        for n in range(src_scr.shape[0] // CHUNK):
            rows = slice(n * CHUNK, (n + 1) * CHUNK)
            out = jnp.dot(wm, src_scr[rows, cols], preferred_element_type=F32)
            if bias_ref is not None:
                out = out + bias_ref[:, gi:gi + 1]
            dst_scr[rows, cols] = out


def _gmlp_fwd_head(proj, ln_g, ln_b, w_s, b_s_t, w_out, x1, gate, gf, target, name):
    _, s, e = proj.shape
    d = w_out.shape[1]
    gw = e // GROUPS
    ts = _gmlp_rows(s)

    def body(pu_ref, pv_ref, pz_ref, g_ref, b_ref, ws_ref, bs_ref, wo_ref, x1_ref, gate_ref, gf_ref, tg_ref,
             y_ref, dx_ref, dbr_ref, loss_ref, dgf_ref, dgate_ref, vn_scr, mix_scr):
        @pl.when(pl.program_id(0) == 0)
        def _():
            loss_ref[...] = jnp.zeros_like(loss_ref)
            dgf_ref[...] = jnp.zeros_like(dgf_ref)
            dgate_ref[...] = jnp.zeros_like(dgate_ref)
        vn, _, _ = _layer_norm_fwd(_gelu(_f32(pv_ref)), g_ref[...], b_ref[...])
        vn_scr[...] = vn.astype(BF16)
        _mix_positions(ws_ref, vn_scr, mix_scr, gw, _tril, bs_ref)
        z = _f32(pz_ref)
        y = ((z * _sigmoid(z)) * (_gelu(_f32(pu_ref)) * mix_scr[...])).astype(BF16)
        y_ref[...] = y
        br = jnp.dot(y, wo_ref[...], preferred_element_type=F32)
        dx, dbr, loss, dgf, dgate = _head_tile(x1_ref[...], br, gate_ref[...], gf_ref[...], tg_ref[...])
        dx_ref[...] = dx
        dbr_ref[...] = dbr.astype(BF16)
        loss_ref[...] += loss
        dgf_ref[...] += dgf
        dgate_ref[...] += dgate

    def part(q):
        return pl.BlockSpec((None, ts, e), lambda i: (q, i, 0))

    return pl.pallas_call(
        body, name=name,
        out_shape=(jax.ShapeDtypeStruct((s, e), BF16), jax.ShapeDtypeStruct((s, d), F32),
                   jax.ShapeDtypeStruct((s, d), BF16), jax.ShapeDtypeStruct((SUBLANE, 128), F32),
                   jax.ShapeDtypeStruct((1, d), F32), jax.ShapeDtypeStruct((1, d), F32)),
        grid=(s // ts,),
        in_specs=[part(0), part(1), part(2), _vec_spec(e), _vec_spec(e),
                  pl.BlockSpec((GROUPS, CHUNK, CHUNK), lambda i: (0, 0, 0)),
                  pl.BlockSpec((CHUNK, GROUPS), lambda i: (0, 0)), pl.BlockSpec((e, d), lambda i: (0, 0)),
                  _row_spec(ts, d), _vec_spec(d), _vec_spec(d), _row_spec(ts, d)],
        out_specs=(_row_spec(ts, e), _row_spec(ts, d), _row_spec(ts, d),
                   pl.BlockSpec((SUBLANE, 128), lambda i: (0, 0)), _vec_spec(d), _vec_spec(d)),
        scratch_shapes=[pltpu.VMEM((ts, e), BF16), pltpu.VMEM((ts, e), F32)],
        compiler_params=_params("arbitrary"),
    )(proj, proj, proj, ln_g, ln_b, w_s, b_s_t, w_out, x1, gate, gf, target)


def _gmlp_bwd(proj, dbr, w_out, ln_g, ln_b, w_s, w_s_t, b_s_t, name):
    _, s, e = proj.shape
    d = dbr.shape[1]
    gw = e // GROUPS
    ts = _gmlp_rows(s)
    n_i = s // ts

    def body(pu_ref, pv_ref, pz_ref, dbr_ref, wo_ref, g_ref, b_ref, ws_ref, wst_ref, bs_ref,
             dp_ref, dws_ref, dbs_ref, dlg_ref, dlb_ref, vn_scr, mix_scr, dm_scr, dvn_scr, dmacc_scr):
        i = pl.program_id(0)

        @pl.when(i == 0)
        def _():
            dws_ref[...] = jnp.zeros_like(dws_ref)
            dlg_ref[...] = jnp.zeros_like(dlg_ref)
            dlb_ref[...] = jnp.zeros_like(dlb_ref)
            dmacc_scr[...] = jnp.zeros_like(dmacc_scr)
        ln_g = g_ref[...]
        u, du_dpu = _gelu_and_grad(_f32(pu_ref))
        v, dv_dpv = _gelu_and_grad(_f32(pv_ref))
        vn, vhat, rstd = _layer_norm_fwd(v, ln_g, b_ref[...])
        vn_scr[...] = vn.astype(BF16)
        _mix_positions(ws_ref, vn_scr, mix_scr, gw, _tril, bs_ref)
        mixed = mix_scr[...]
        dy = lax.dot_general(dbr_ref[...], wo_ref[...], (((1,), (1,)), ((), ())), preferred_element_type=F32)
        sz, dsz = _silu_and_grad(_f32(pz_ref))
        ds = dy * sz
        dp_ref[2] = (dy * (u * mixed) * dsz).astype(BF16)
        dp_ref[0] = (ds * mixed * du_dpu).astype(BF16)
        dm = ds * u
        dm_scr[...] = dm.astype(BF16)
        for n in range(ts // CHUNK):
            rows = slice(n * CHUNK, (n + 1) * CHUNK)
            dmacc_scr[...] += dm[rows]
            for gi in range(GROUPS):
                cols = slice(gi * gw, (gi + 1) * gw)
                dws_ref[gi] += lax.dot_general(dm_scr[rows, cols], vn_scr[rows, cols], (((1,), (1,)), ((), ())),
                                               preferred_element_type=F32)
        _mix_positions(wst_ref, dm_scr, dvn_scr, gw, _triu)
        dvn = dvn_scr[...]
        dlg_ref[...] += _colsum(dvn * vhat)
        dlb_ref[...] += _colsum(dvn)
        dvh = dvn * ln_g
        dv = rstd * (dvh - jnp.mean(dvh, axis=-1, keepdims=True) - vhat * jnp.mean(dvh * vhat, axis=-1, keepdims=True))
        dp_ref[1] = (dv * dv_dpv).astype(BF16)

        @pl.when(i == n_i - 1)
        def _():
            for gi in range(GROUPS):
                dws_ref[gi] = _tril(dws_ref[gi])
                dbs_ref[:, gi:gi + 1] = jnp.sum(dmacc_scr[:, gi * gw:(gi + 1) * gw], axis=1, keepdims=True)

    def part(q):
        return pl.BlockSpec((None, ts, e), lambda i: (q, i, 0))

    w_spec = pl.BlockSpec((GROUPS, CHUNK, CHUNK), lambda i: (0, 0, 0))
    bs_spec = pl.BlockSpec((CHUNK, GROUPS), lambda i: (0, 0))
    return pl.pallas_call(
        body, name=name,
        out_shape=(jax.ShapeDtypeStruct((3, s, e), BF16), jax.ShapeDtypeStruct((GROUPS, CHUNK, CHUNK), F32),
                   jax.ShapeDtypeStruct((CHUNK, GROUPS), F32), jax.ShapeDtypeStruct((1, e), F32),
                   jax.ShapeDtypeStruct((1, e), F32)),
        grid=(n_i,),
        in_specs=[part(0), part(1), part(2), pl.BlockSpec((ts, d), lambda i: (i, 0)),
                  pl.BlockSpec((e, d), lambda i: (0, 0)), _vec_spec(e), _vec_spec(e), w_spec, w_spec, bs_spec],
        out_specs=(pl.BlockSpec((3, ts, e), lambda i: (0, i, 0)), w_spec, bs_spec, _vec_spec(e), _vec_spec(e)),
        scratch_shapes=[pltpu.VMEM((ts, e), BF16), pltpu.VMEM((ts, e), F32), pltpu.VMEM((ts, e), BF16),
                        pltpu.VMEM((ts, e), F32), pltpu.VMEM((CHUNK, e), F32)],
        compiler_params=_params("arbitrary"),
    )(proj, proj, proj, dbr, w_out, ln_g, ln_b, w_s, w_s_t, b_s_t)


def _mod_fwd(c_all, mod_w, mod_b_cols, name):
    n_layer, d, w = mod_w.shape

    def body(c_ref, w_ref, b_ref, ca_ref, o_ref):
        c = c_ref[...]
        ca = c * _sigmoid(c)
        ca_ref[...] = ca
        for li in range(n_layer):
            o_ref[li * N_DEV:(li + 1) * N_DEV, :] = (
                jnp.dot(ca, w_ref[li], preferred_element_type=F32, precision=lax.Precision.HIGHEST) + b_ref[li])

    return pl.pallas_call(
        body, name=name,
        out_shape=(jax.ShapeDtypeStruct((N_DEV, d), F32), jax.ShapeDtypeStruct((n_layer * N_DEV, w), F32)),
        in_specs=[VMEM_SPEC] * 3, out_specs=(VMEM_SPEC, VMEM_SPEC),
        compiler_params=pltpu.CompilerParams(vmem_limit_bytes=VMEM_LIMIT),
    )(c_all, mod_w, mod_b_cols)


def _adamw(w, g, m, v):
    m = ADAM_B1 * m + (1.0 - ADAM_B1) * g
    v = ADAM_B2 * v + (1.0 - ADAM_B2) * (g * g)
    m_hat = m / (1.0 - ADAM_B1 ** ADAM_STEP)
    v_hat = v / (1.0 - ADAM_B2 ** ADAM_STEP)
    delta = -ADAM_LR * (m_hat / (jnp.sqrt(v_hat) + ADAM_EPS) + ADAM_WD * w)
    return delta, m, v


def _adamw_2d(w, g, m, v, name):
    r, c = w.shape
    tr, tc = _tile(r, 512), _tile(c, 1024)

    def body(w_ref, g_ref, m_ref, v_ref, go_ref, d_ref, nm_ref, nv_ref):
        g = g_ref[...]
        go_ref[...] = g
        d_ref[...], nm_ref[...], nv_ref[...] = _adamw(w_ref[...], g, m_ref[...], v_ref[...])

    spec = pl.BlockSpec((tr, tc), lambda i, j: (i, j))
    shape = jax.ShapeDtypeStruct((r, c), F32)
    return pl.pallas_call(
        body, name=name, out_shape=(shape,) * 4, grid=(r // tr, c // tc),
        in_specs=[spec] * 4, out_specs=(spec,) * 4, compiler_params=_params("parallel", "parallel"),
    )(w, g, m, v)


def _mod_w_update(ca_t, dmod_cols, w, m, v, name):
    n_layer, d, wd = w.shape
    tr = _tile(d, 256)

    def body(ca_ref, dm_ref, w_ref, m_ref, v_ref, g_ref, d_ref, nm_ref, nv_ref):
        ca = ca_ref[...]
        dm = dm_ref[...]
        g = ca[:, 0:1] * dm[0:1, :]
        for b in range(1, N_DEV):
            g = g + ca[:, b:b + 1] * dm[b:b + 1, :]
        g_ref[...] = g
        d_ref[...], nm_ref[...], nv_ref[...] = _adamw(w_ref[...], g, m_ref[...], v_ref[...])

    spec = pl.BlockSpec((None, tr, wd), lambda l, i: (l, i, 0))
    shape = jax.ShapeDtypeStruct((n_layer, d, wd), F32)
    return pl.pallas_call(
        body, name=name, out_shape=(shape,) * 4, grid=(n_layer, d // tr),
        in_specs=[pl.BlockSpec((tr, N_DEV), lambda l, i: (i, 0)), pl.BlockSpec((None, N_DEV, wd), lambda l, i: (l, 0, 0)),
                  spec, spec, spec],
        out_specs=(spec,) * 4, compiler_params=_params("parallel", "parallel"),
    )(ca_t, dmod_cols, w, m, v)


def _adamw_small(ws, gs, ms, vs, name):
    n = len(ws)

    def body(*refs):
        ins, outs = refs[:4 * n], refs[4 * n:]
        for k in range(n):
            delta, nm, nv = _adamw(ins[k][...], ins[n + k][...], ins[2 * n + k][...], ins[3 * n + k][...])
            outs[3 * k][...] = delta
            outs[3 * k + 1][...] = nm
            outs[3 * k + 2][...] = nv

    out_shape = []
    for w in ws:
        out_shape += [jax.ShapeDtypeStruct(w.shape, F32)] * 3
    outs = pl.pallas_call(
        body, name=name, out_shape=tuple(out_shape),
        in_specs=[VMEM_SPEC] * (4 * n), out_specs=tuple([VMEM_SPEC] * (3 * n)),
        compiler_params=pltpu.CompilerParams(vmem_limit_bytes=VMEM_LIMIT),
    )(*ws, *gs, *ms, *vs)
    return [tuple(outs[3 * k:3 * k + 3]) for k in range(n)]


def _place():
    return lax.axis_index("x"), lax.axis_index("y"), lax.axis_index("c")


def _other_chips(x, y):
    return [(1 - x, y), (x, 1 - y), (1 - x, 1 - y)]


def _all_gather(block, name, after=None):
    m_per, n = block.shape

    def body(x_ref, *rest):
        out_ref, send_sems, recv_sems, local_sem = rest[-4:]
        x, y, c = _place()
        me, sibling = (x, y, c), (x, y, 1 - c)
        chips = _other_chips(x, y)

        def rows(px, py, pc):
            return out_ref.at[pl.ds((4 * px + 2 * py + pc) * m_per, m_per), :]

        def copy(k, blk, to, src=None):
            return pltpu.make_async_remote_copy(
                src_ref=rows(*blk) if src is None else src, dst_ref=rows(*blk),
                send_sem=send_sems.at[k], recv_sem=recv_sems.at[k], device_id=to, device_id_type=MESH)

        mine = pltpu.make_async_copy(x_ref, rows(*me), local_sem)
        mine.start()
        first = [copy(0, me, sibling, src=x_ref)]
        first += [copy(1 + j, me, (*chip, c), src=x_ref) for j, chip in enumerate(chips)]
        for cp in first:
            cp.start()
        passed = [copy(4 + j, (*chip, c), sibling) for j, chip in enumerate(chips)]
        for j, chip in enumerate(chips):
            copy(1 + j, (*chip, c), me).wait_recv()
            passed[j].start()
        copy(0, sibling, me).wait_recv()
        for j, chip in enumerate(chips):
            copy(4 + j, (*chip, 1 - c), me).wait_recv()
        for cp in first + passed:
            cp.wait_send()
        mine.wait()

    return pl.pallas_call(
        body, name=name, out_shape=jax.ShapeDtypeStruct((N_DEV * m_per, n), F32),
        in_specs=[VMEM_SPEC] + ([] if after is None else [ANY_SPEC]), out_specs=VMEM_SPEC,
        scratch_shapes=[pltpu.SemaphoreType.DMA((7,)), pltpu.SemaphoreType.DMA((7,)), pltpu.SemaphoreType.DMA],
        compiler_params=pltpu.CompilerParams(vmem_limit_bytes=VMEM_LIMIT),
    )(*([block] if after is None else [block, after]))


def _hbm(a):
    return pltpu.with_memory_space_constraint(a, pltpu.HBM)


def _all_gather_start(blocks, name):
    def body(in_ref, send_sems, recv_sems, thru_ref, token):
        token[...] = jnp.zeros_like(token)
        x, y, c = _place()
        mine = in_ref.at[4 * x + 2 * y + c]
        pltpu.make_async_remote_copy(
            src_ref=mine, dst_ref=mine, send_sem=send_sems.at[0], recv_sem=recv_sems.at[0],
            device_id=(x, y, 1 - c), device_id_type=MESH).start()
        for j, chip in enumerate(_other_chips(x, y)):
            for cc in range(2):
                pltpu.make_async_remote_copy(
                    src_ref=mine, dst_ref=mine, send_sem=send_sems.at[1 + 2 * j + cc],
                    recv_sem=recv_sems.at[1 + 2 * j + c], device_id=(*chip, cc), device_id_type=MESH).start()

    sems = pltpu.SemaphoreType.DMA((N_DEV - 1,))
    return pl.pallas_call(
        body, name=name,
        out_shape=(sems, sems, pltpu.HBM(blocks.shape, blocks.dtype), jax.ShapeDtypeStruct((SUBLANE, 128), F32)),
        in_specs=[HBM_SPEC], out_specs=(SEM_SPEC, SEM_SPEC, HBM_SPEC, VMEM_SPEC), input_output_aliases={0: 2},
        compiler_params=pltpu.CompilerParams(has_side_effects=EFFECT),
    )(_hbm(blocks))


def _all_gather_wait(send_sems, recv_sems, blocks, after, name):
    def body(in_ref, send_sems, recv_sems, after_ref, out_ref):
        x, y, c = _place()
        one = in_ref.at[0]
        for k in range(N_DEV - 1):
            pltpu.make_async_remote_copy(
                src_ref=one, dst_ref=one, send_sem=send_sems.at[k], recv_sem=recv_sems.at[k],
                device_id=(x, y, 1 - c), device_id_type=MESH).wait_send()
            pltpu.make_async_remote_copy(
                src_ref=one, dst_ref=one, send_sem=send_sems.at[k], recv_sem=recv_sems.at[k],
                device_id=(x, y, 1 - c), device_id_type=MESH).wait_recv()

    return pl.pallas_call(
        body, name=name, out_shape=pltpu.HBM(blocks.shape, blocks.dtype),
        in_specs=[HBM_SPEC, SEM_SPEC, SEM_SPEC, ANY_SPEC], out_specs=HBM_SPEC, input_output_aliases={0: 0},
        compiler_params=pltpu.CompilerParams(has_side_effects=EFFECT),
    )(blocks, send_sems, recv_sems, after)


def _place_own(shard, kind, chip_idx, name):
    r, cdim = shard.shape
    tr, tc = _tile(r, 512), _tile(cdim, 1024)
    nrb, ncb = r // tr, cdim // tc

    def body(k_ref, s_ref, o_ref):
        o_ref[...] = s_ref[...].astype(BF16)

    if kind == "col":
        full, o_map = (r, N_CHIP * cdim), lambda i, j, kr: (i, kr[0] * ncb + j)
    else:
        full, o_map = (N_CHIP * r, cdim), lambda i, j, kr: (kr[0] * nrb + i, j)
    return pl.pallas_call(
        body, name=name, out_shape=jax.ShapeDtypeStruct(full, BF16),
        grid_spec=pltpu.PrefetchScalarGridSpec(
            num_scalar_prefetch=1, grid=(nrb, ncb),
            in_specs=[pl.BlockSpec((tr, tc), lambda i, j, kr: (i, j))],
            out_specs=pl.BlockSpec((tr, tc), o_map)),
        compiler_params=_params("parallel", "parallel"),
    )(chip_idx, shard)


def _weight_window(ref, kind, shard_shape, k, half):
    r, cdim = shard_shape
    hr = r // 2
    if kind == "col":
        return ref.at[pl.ds(half * hr, hr), pl.ds(pl.multiple_of(k * cdim, 128), cdim)]
    return ref.at[pl.ds(pl.multiple_of(k * r + half * hr, 2 * SUBLANE), hr), :]


def _target_cores(c, both_cores):
    return [(cc, cc, c) for cc in range(2)] if both_cores else [(c, 0, 0)]


def _gather_weights_start(fulls, kinds, shard_shapes, groups, after, name, both_cores=True):
    n, ng = len(fulls), len(groups)
    per = 2 if both_cores else 1

    def body(*refs):
        ins = refs[:n]
        sems = refs[n + 1:n + 1 + 2 * ng]
        token = refs[2 * n + 1 + 2 * ng]
        x, y, c = _place()
        chips = _other_chips(x, y)
        for g, group in enumerate(groups):
            for pos, (w, j) in enumerate(group):
                own = _weight_window(ins[w], kinds[w], shard_shapes[w], 2 * x + y, c)
                for cc, mine, theirs in _target_cores(c, both_cores):
                    pltpu.make_async_remote_copy(
                        src_ref=own, dst_ref=own,
                        send_sem=sems[2 * g].at[per * pos + mine], recv_sem=sems[2 * g + 1].at[per * pos + theirs],
                        device_id=(*chips[j], cc), device_id_type=MESH).start()
        token[...] = jnp.zeros_like(token)

    sem_shapes = []
    for group in groups:
        sem_shapes += [pltpu.SemaphoreType.DMA((per * len(group),))] * 2
    outs = pl.pallas_call(
        body, name=name,
        out_shape=tuple(sem_shapes) + tuple(pltpu.HBM(f.shape, f.dtype) for f in fulls)
        + (jax.ShapeDtypeStruct((SUBLANE, 128), F32),),
        in_specs=[HBM_SPEC] * n + [ANY_SPEC], out_specs=(SEM_SPEC,) * (2 * ng) + (HBM_SPEC,) * n + (VMEM_SPEC,),
        input_output_aliases={w: 2 * ng + w for w in range(n)},
        compiler_params=pltpu.CompilerParams(has_side_effects=EFFECT),
    )(*[_hbm(f) for f in fulls], after)
    sems = [(outs[2 * g], outs[2 * g + 1]) for g in range(ng)]
    return sems, list(outs[2 * ng:2 * ng + n]), outs[2 * ng + n]


def _gather_weights_wait(sems, group, fulls, kinds, shard_shapes, after, name, both_cores=True):
    n = len(fulls)
    per = 2 if both_cores else 1

    def body(*refs):
        ins = refs[:n]
        send_sems, recv_sems = refs[n], refs[n + 1]
        x, y, c = _place()
        chips = _other_chips(x, y)
        for pos, (w, j) in enumerate(group):
            own = _weight_window(ins[w], kinds[w], shard_shapes[w], 2 * x + y, c)
            for cc, mine, _ in _target_cores(c, both_cores):
                landed = _weight_window(ins[w], kinds[w], shard_shapes[w], 2 * chips[j][0] + chips[j][1], cc)
                k = per * pos + mine
                pltpu.make_async_remote_copy(
                    src_ref=own, dst_ref=own, send_sem=send_sems.at[k], recv_sem=recv_sems.at[k],
                    device_id=(*chips[j], cc), device_id_type=MESH).wait_send()
                pltpu.make_async_remote_copy(
                    src_ref=landed, dst_ref=landed, send_sem=send_sems.at[k], recv_sem=recv_sems.at[k],
                    device_id=(*chips[j], cc), device_id_type=MESH).wait_recv()

    return list(pl.pallas_call(
        body, name=name, out_shape=tuple(pltpu.HBM(f.shape, f.dtype) for f in fulls),
        in_specs=[HBM_SPEC] * n + [SEM_SPEC, SEM_SPEC, ANY_SPEC], out_specs=(HBM_SPEC,) * n,
        input_output_aliases={w: w for w in range(n)},
        compiler_params=pltpu.CompilerParams(has_side_effects=EFFECT),
    )(*fulls, sems[0], sems[1], after))


def _forward_to_sibling(full, kind, shard_shape, relations, name):
    nr = len(relations)

    def body(in_ref, out_ref, send_sems, recv_sems):
        x, y, c = _place()
        chips = _other_chips(x, y)
        copies = []
        for pos, j in enumerate(relations):
            k = 2 * chips[j][0] + chips[j][1]
            mine = _weight_window(in_ref, kind, shard_shape, k, c)
            cp = pltpu.make_async_remote_copy(
                src_ref=mine, dst_ref=_weight_window(out_ref, kind, shard_shape, k, c),
                send_sem=send_sems.at[pos], recv_sem=recv_sems.at[pos], device_id=(x, y, 1 - c), device_id_type=MESH)
            cp.start()
            copies.append(cp)
        for pos, j in enumerate(relations):
            theirs = _weight_window(out_ref, kind, shard_shape, 2 * chips[j][0] + chips[j][1], 1 - c)
            pltpu.make_async_remote_copy(
                src_ref=theirs, dst_ref=theirs, send_sem=send_sems.at[pos], recv_sem=recv_sems.at[pos],
                device_id=(x, y, 1 - c), device_id_type=MESH).wait_recv()
        for cp in copies:
            cp.wait_send()

    return pl.pallas_call(
        body, name=name, out_shape=jax.ShapeDtypeStruct(full.shape, full.dtype),
        in_specs=[HBM_SPEC], out_specs=HBM_SPEC, input_output_aliases={0: 0},
        scratch_shapes=[pltpu.SemaphoreType.DMA((nr,))] * 2,
    )(full)


def _grad_piece(ref, kind, h, cdim, k, half):
    if kind == "col":
        return ref.at[pl.ds(half * h, h), pl.ds(pl.multiple_of(k * cdim, 128), cdim)]
    return ref.at[k, pl.ds(half * h, h), :]


def _grad_dims(g, kind):
    return (g.shape[0] // 2, g.shape[1] // N_CHIP) if kind == "col" else (g.shape[1] // 2, g.shape[2])


def _reduce_start(grads, kinds, name):
    n = len(grads)
    dims = [_grad_dims(g, kind) for g, kind in zip(grads, kinds)]
    lands = [lax.empty((N_DEV - 1, h, cdim), g.dtype) for g, (h, cdim) in zip(grads, dims)]

    def body(*refs):
        g_ins, land_ins = refs[:n], refs[n:2 * n]
        send_sems, recv_sems = refs[2 * n], refs[2 * n + 1]
        token = refs[4 * n + 2]
        x, y, c = _place()
        for w in range(n):
            h, cdim = dims[w]
            base = (N_DEV - 1) * w
            pltpu.make_async_remote_copy(
                src_ref=_grad_piece(g_ins[w], kinds[w], h, cdim, 2 * x + y, 1 - c), dst_ref=land_ins[w].at[0],
                send_sem=send_sems.at[base], recv_sem=recv_sems.at[base],
                device_id=(x, y, 1 - c), device_id_type=MESH).start()
            for j, chip in enumerate(_other_chips(x, y)):
                for cc in range(2):
                    pltpu.make_async_remote_copy(
                        src_ref=_grad_piece(g_ins[w], kinds[w], h, cdim, 2 * chip[0] + chip[1], cc),
                        dst_ref=land_ins[w].at[1 + 2 * j + c],
                        send_sem=send_sems.at[base + 1 + 2 * j + cc], recv_sem=recv_sems.at[base + 1 + 2 * j + c],
                        device_id=(*chip, cc), device_id_type=MESH).start()
        token[...] = jnp.zeros_like(token)

    sems = pltpu.SemaphoreType.DMA(((N_DEV - 1) * n,))
    outs = pl.pallas_call(
        body, name=name,
        out_shape=(sems, sems) + tuple(pltpu.HBM(a.shape, a.dtype) for a in list(grads) + lands)
        + (jax.ShapeDtypeStruct((SUBLANE, 128), F32),),
        in_specs=[HBM_SPEC] * (2 * n), out_specs=(SEM_SPEC, SEM_SPEC) + (HBM_SPEC,) * (2 * n) + (VMEM_SPEC,),
        input_output_aliases={i: 2 + i for i in range(2 * n)},
        compiler_params=pltpu.CompilerParams(has_side_effects=EFFECT),
    )(*[_hbm(a) for a in list(grads) + lands])
    return outs[0], outs[1], list(outs[2:2 + n]), list(outs[2 + n:2 + 2 * n]), outs[2 + 2 * n]


def _reduce_wait(send_sems, recv_sems, grads, lands, kinds, after, name):
    n = len(grads)
    dims = [_grad_dims(g, kind) for g, kind in zip(grads, kinds)]

    def body(*refs):
        g_ins, land_ins = refs[:n], refs[n:2 * n]
        send_sems, recv_sems = refs[2 * n], refs[2 * n + 1]
        x, y, c = _place()
        for w in range(n):
            h, cdim = dims[w]
            piece = _grad_piece(g_ins[w], kinds[w], h, cdim, 2 * x + y, c)
            for s in range(N_DEV - 1):
                k = (N_DEV - 1) * w + s
                slot = land_ins[w].at[s]
                pltpu.make_async_remote_copy(
                    src_ref=piece, dst_ref=slot, send_sem=send_sems.at[k], recv_sem=recv_sems.at[k],
                    device_id=(x, y, 1 - c), device_id_type=MESH).wait_send()
                pltpu.make_async_remote_copy(
                    src_ref=piece, dst_ref=slot, send_sem=send_sems.at[k], recv_sem=recv_sems.at[k],
                    device_id=(x, y, 1 - c), device_id_type=MESH).wait_recv()

    outs = pl.pallas_call(
        body, name=name, out_shape=tuple(pltpu.HBM(a.shape, a.dtype) for a in list(grads) + list(lands)),
        in_specs=[HBM_SPEC] * (2 * n) + [SEM_SPEC, SEM_SPEC, ANY_SPEC], out_specs=(HBM_SPEC,) * (2 * n),
        input_output_aliases={i: i for i in range(2 * n)},
        compiler_params=pltpu.CompilerParams(has_side_effects=EFFECT),
    )(*grads, *lands, send_sems, recv_sems, after)
    return list(outs[:n]), list(outs[n:])


def _add_pieces(g, land, kind, chip_idx, core_idx, name):
    _, h, cdim = land.shape
    tr, tc = _tile(h, 256), _tile(cdim, 2048)
    nrb, ncb = h // tr, cdim // tc

    def body(k_ref, c_ref, g_ref, l_ref, o_ref):
        acc = g_ref[...].astype(F32)
        for s in range(N_DEV - 1):
            acc = acc + l_ref[s].astype(F32)
        o_ref[...] = acc

    if kind == "col":
        g_spec = pl.BlockSpec((tr, tc), lambda i, j, kr, cr: (cr[0] * nrb + i, kr[0] * ncb + j))
    else:
        g_spec = pl.BlockSpec((None, tr, tc), lambda i, j, kr, cr: (kr[0], cr[0] * nrb + i, j))
    return pl.pallas_call(
        body, name=name, out_shape=jax.ShapeDtypeStruct((2 * h, cdim), F32),
        grid_spec=pltpu.PrefetchScalarGridSpec(
            num_scalar_prefetch=2, grid=(nrb, ncb),
            in_specs=[g_spec, pl.BlockSpec((N_DEV - 1, tr, tc), lambda i, j, kr, cr: (0, i, j))],
            out_specs=pl.BlockSpec((tr, tc), lambda i, j, kr, cr: (cr[0] * nrb + i, j))),
        compiler_params=_params("parallel", "parallel"),
    )(chip_idx, core_idx, g, land)


def _join_halves(shards, name):
    n = len(shards)

    def body(*refs):
        ins, outs = refs[:n], refs[n:2 * n]
        send_sems, recv_sems = refs[2 * n:]
        x, y, c = _place()
        copies = []
        for w in range(n):
            h = shards[w].shape[0] // 2
            cp = pltpu.make_async_remote_copy(
                src_ref=ins[w].at[pl.ds(c * h, h), :], dst_ref=outs[w].at[pl.ds(c * h, h), :],
                send_sem=send_sems.at[w], recv_sem=recv_sems.at[w], device_id=(x, y, 1 - c), device_id_type=MESH)
            cp.start()
            copies.append(cp)
        for w in range(n):
            h = shards[w].shape[0] // 2
            theirs = outs[w].at[pl.ds((1 - c) * h, h), :]
            pltpu.make_async_remote_copy(
                src_ref=theirs, dst_ref=theirs, send_sem=send_sems.at[w], recv_sem=recv_sems.at[w],
                device_id=(x, y, 1 - c), device_id_type=MESH).wait_recv()
        for cp in copies:
            cp.wait_send()

    return pl.pallas_call(
        body, name=name, out_shape=tuple(jax.ShapeDtypeStruct(s.shape, s.dtype) for s in shards),
        in_specs=[HBM_SPEC] * n, out_specs=tuple([HBM_SPEC] * n),
        input_output_aliases={w: w for w in range(n)},
        scratch_shapes=[pltpu.SemaphoreType.DMA((n,))] * 2,
    )(*shards)


def _sum_devices(gathered, rows, name):
    n = gathered.shape[1]

    def body(g_ref, o_ref):
        acc = g_ref[0:rows, :]
        for dev in range(1, N_DEV):
            acc = acc + g_ref[dev * rows:(dev + 1) * rows, :]
        o_ref[...] = acc

    return pl.pallas_call(
        body, name=name, out_shape=jax.ShapeDtypeStruct((rows, n), F32),
        in_specs=[VMEM_SPEC], out_specs=VMEM_SPEC,
        compiler_params=pltpu.CompilerParams(vmem_limit_bytes=VMEM_LIMIT),
    )(gathered)


def _pack(vectors, width):
    flat = [v.reshape(-1) for v in vectors]
    offsets, total = [], 0
    for f in flat:
        offsets.append(total)
        total += f.shape[0]
    rows = -(-total // (width * SUBLANE)) * SUBLANE
    flat.append(jnp.zeros((rows * width - total,), F32))
    return jnp.concatenate(flat).reshape(rows, width), offsets


def kernel(x, c, mod_w, mod_b, norm_g, a_w_in, a_conv_w, a_conv_b, a_w_out, b_w_in, b_ln_g, b_ln_b, b_w_s, b_b_s, b_w_out, final_g, loss_target, m_mod_w, m_mod_b, m_norm_g, m_a_w_in, m_a_conv_w, m_a_conv_b, m_a_w_out, m_b_w_in, m_b_ln_g, m_b_ln_b, m_b_w_s, m_b_b_s, m_b_w_out, m_final_g, v_mod_w, v_mod_b, v_norm_g, v_a_w_in, v_a_conv_w, v_a_conv_b, v_a_w_out, v_b_w_in, v_b_ln_g, v_b_ln_b, v_b_w_s, v_b_b_s, v_b_w_out, v_final_g):
    seq, d = x.shape[1], x.shape[2]
    e = a_conv_b.shape[1]
    wd = mod_w.shape[2]
    ax, ay, ac = _place()
    chip = 2 * ax + ay
    dev = 2 * chip + ac
    chip_idx = jnp.reshape(chip, (1,)).astype(jnp.int32)
    core_idx = jnp.reshape(ac, (1,)).astype(jnp.int32)

    x2d = x[0]
    target = loss_target[0]

    w_names = ["a_w_in", "a_w_out", "b_w_in", "b_w_out"]
    w_kinds = ["col", "row", "col", "row"]
    w_shards = [a_w_in[0], a_w_out[0], b_w_in[0], b_w_out[0]]
    w_shapes = [sh.shape for sh in w_shards]
    placed = [_place_own(sh, kind, chip_idx, "place_" + nm) for nm, sh, kind in zip(w_names, w_shards, w_kinds)]
    a_groups, far_groups = [[(0, 0), (0, 1)]], [[(0, 2)]]
    whole_group = [[(0, 0), (0, 1), (0, 2)]]

    es = e // N_CHIP
    packed0, offs0 = _pack([c, a_conv_w, b_ln_g, b_ln_b], 1024)
    gathered0 = _all_gather(packed0, "gather_params").reshape(N_DEV, -1)
    c_all = gathered0[:, :d]
    per_chip = gathered0[0::2]

    def from_chips_cols(k, rows_):
        got = per_chip[:, offs0[k]:offs0[k] + rows_ * es].reshape(N_CHIP, rows_, es)
        return jnp.transpose(got, (1, 0, 2)).reshape(rows_, e)

    conv_w_full = from_chips_cols(1, 3)
    conv_w = [conv_w_full[k:k + 1] for k in range(3)]
    ln_g, ln_b = from_chips_cols(2, 1), from_chips_cols(3, 1)
    mod_b_cols = lax.dynamic_slice_in_dim(mod_b, chip * wd, wd, axis=1)[:, None, :]
    c_act, mod_part = _mod_fwd(c_all, mod_w, mod_b_cols, "mod_fwd")
    n_layer = mod_w.shape[0]
    mod_gathered = _all_gather(mod_part, "gather_mod")
    a_sems, (wa_in,), a_token = _gather_weights_start(
        placed[:1], w_kinds[:1], w_shapes[:1], a_groups, mod_gathered, "gather_a_w_in_start", both_cores=False)
    mod_all = mod_gathered.reshape(N_CHIP, 2, n_layer, N_DEV, wd)[:, 0]
    mod_all = jnp.transpose(mod_all, (1, 2, 0, 3)).reshape(n_layer, N_DEV, N_CHIP * wd)
    mod_me = lax.dynamic_index_in_dim(mod_all, dev, axis=1, keepdims=False)
    shift = [mod_me[l:l + 1, 0:d] for l in range(n_layer)]
    scale = [mod_me[l:l + 1, d:2 * d] for l in range(n_layer)]
    gate = [mod_me[l:l + 1, 2 * d:3 * d] for l in range(n_layer)]

    g0, g1, gf = norm_g[0:1], norm_g[1:2], final_g[None, :]
    h0 = _norm_mod(x2d, g0, scale[0], shift[0], "norm_mod0")

    def slab(r):
        return jnp.bitwise_xor(chip_idx, r)

    def arrived(sems, group, weight, w, after, name, both_cores=True):
        return _gather_weights_wait(sems, [(0, j) for _, j in group], [weight], [w_kinds[w]], [w_shapes[w]],
                                    after, name, both_cores)[0]

    proj0 = _mm_proj_slab(h0, wa_in, None, slab(0), 4, "a_proj_own", a_token)
    wa_in = arrived(a_sems[0], a_groups[0], wa_in, 0, proj0, "gather_wait_near", both_cores=False)
    wa_in = _forward_to_sibling(wa_in, w_kinds[0], w_shapes[0], [0, 1], "forward_near")
    far_sems, (wa_in,), far_token = _gather_weights_start(
        [wa_in], w_kinds[:1], w_shapes[:1], far_groups, a_token, "gather_a_w_in_far_start")
    proj0 = _mm_proj_slab(h0, wa_in, proj0, slab(2), 4, "a_proj_x", far_token)
    proj0 = _mm_proj_slab(h0, wa_in, proj0, slab(1), 4, "a_proj_y", far_token)
    wa_in = arrived(far_sems[0], far_groups[0], wa_in, 0, proj0, "gather_wait_far")
    def start_whole(w, after, name):
        sems, (flight,), token = _gather_weights_start(
            placed[w:w + 1], w_kinds[w:w + 1], w_shapes[w:w + 1], whole_group, after, name)
        return sems[0], flight, token

    ao_sems, wa_out, ao_token = start_whole(1, wa_in, "gather_a_w_out_start")
    proj0 = _mm_proj_slab(h0, wa_in, proj0, slab(3), 4, "a_proj_far", ao_token)
    wa_out = arrived(ao_sems, whole_group[0], wa_out, 1, proj0, "gather_wait_a_w_out")
    bi_sems, wb_in, bi_token = start_whole(2, wa_out, "gather_b_w_in_start")
    y0, br0, x1, h1 = _conv_fwd(proj0, conv_w, a_conv_b, wa_out, x2d, gate[0], g1, scale[1], shift[1],
                                "conv_fwd", bi_token)
    wb_in = arrived(bi_sems, whole_group[0], wb_in, 2, h1, "gather_wait_b_w_in")
    bo_sems, wb_out, bo_token = start_whole(3, wb_in, "gather_b_w_out_start")
    proj1 = _mm_proj(h1, wb_in, 3, "b_proj", bo_token)
    b_s_t = jnp.transpose(b_b_s[0])
    wb_out = arrived(bo_sems, whole_group[0], wb_out, 3, proj1, "gather_wait_b_w_out")
    y1, dx2, dbr1, loss_part, g_final_g, dgate1 = _gmlp_fwd_head(
        proj1, ln_g, ln_b, b_w_s[0], b_s_t, wb_out, x1, gate[1], gf, target, "gmlp_fwd_head")

    gw_b_out = _mm_dw_out(y1, dbr1, "b_out_dw")
    dproj1, g_w_s, g_b_s_t, g_ln_g, g_ln_b = _gmlp_bwd(
        proj1, dbr1, wb_out, ln_g, ln_b, b_w_s[0], jnp.swapaxes(b_w_s[0], 1, 2), b_s_t, "gmlp_bwd")
    gw_b_in = _mm_dw_in(h1, dproj1, "b_proj_dw")
    b_kinds = ["col", "row"]
    b_send, b_recv, b_grads, b_lands, b_token = _reduce_start(
        [gw_b_in, gw_b_out.reshape(N_CHIP, e // N_CHIP, d)], b_kinds, "reduce_b_start")
    dx1, dshift1, dscale1, g_g1, dbr0, dgate0 = _mm_dh_norm_bwd(
        dproj1, wb_in, x1, dx2, g1, scale[1], "b_proj_dx", b_token, br=br0, gate=gate[0])

    early = [loss_part[0, 0:1], g_final_g, g_g1, g_ln_g, g_ln_b, jnp.transpose(g_b_s_t), g_w_s,
             jnp.concatenate([dshift1, dscale1, dgate1, dgate0], axis=1)]
    packed_e, offs_e = _pack(early, 1024)
    rows_e = packed_e.shape[0]
    blocks_e = lax.dynamic_update_slice(jnp.zeros((N_DEV, rows_e, 1024), F32), packed_e[None], (dev, 0, 0))
    e_send, e_recv, blocks_e, e_token = _all_gather_start(blocks_e, "gather_small_start")

    gw_a_out = _mm_dw_out(y0, dbr0, "a_out_dw")
    ao_send, ao_recv, ao_grads, ao_lands, ao_grad_token = _reduce_start(
        [gw_a_out.reshape(N_CHIP, e // N_CHIP, d)], ["row"], "reduce_a_out_start")
    dproj0, g_w0, g_w1, g_w2, g_conv_b = _conv_bwd(
        proj0, dbr0, wa_out, conv_w, a_conv_b, "conv_bwd", ao_grad_token + e_token)
    gw_a_in = _mm_dw_in(h0, dproj0, "a_proj_dw")
    ai_send, ai_recv, ai_grads, ai_lands, ai_token = _reduce_start([gw_a_in], ["col"], "reduce_a_in_start")
    grad_x, dshift0, dscale0, g_g0 = _mm_dh_norm_bwd(dproj0, wa_in, x2d, dx1, g0, scale[0], "a_proj_dx", ai_token)

    def finish(send, recv, grads_, lands_, kinds_, names_, after, tag):
        grads_, lands_ = _reduce_wait(send, recv, grads_, lands_, kinds_, after, "reduce_" + tag + "_wait")
        halves = [_add_pieces(g, land, kind, chip_idx, core_idx, "add_pieces_" + nm)
                  for g, land, kind, nm in zip(grads_, lands_, kinds_, names_)]
        return _join_halves(halves, "join_" + tag)

    upd, big_grads = {}, {}

    def adamw_big(nm, w, g, m, v):
        g_out, *rest = _adamw_2d(w[0], g, m[0], v[0], "adamw_" + nm)
        big_grads[nm] = g_out[None]
        upd[nm] = tuple(o[None] for o in rest)

    g_b_w_in, g_b_w_out = finish(b_send, b_recv, b_grads, b_lands, b_kinds, ["b_w_in", "b_w_out"], grad_x, "b")
    adamw_big("b_w_in", b_w_in, g_b_w_in, m_b_w_in, v_b_w_in)
    adamw_big("b_w_out", b_w_out, g_b_w_out, m_b_w_out, v_b_w_out)

    late = [g_g0, jnp.concatenate([g_w0, g_w1, g_w2], axis=0), g_conv_b, jnp.concatenate([dshift0, dscale0], axis=1)]
    packed_l, offs_l = _pack(late, 1024)
    rows_l = packed_l.shape[0]
    gathered_l = _all_gather(packed_l, "gather_small_late", after=upd["b_w_out"][0])
    blocks_e = _all_gather_wait(e_send, e_recv, blocks_e, gathered_l, "gather_small_wait")
    gathered_e = blocks_e.reshape(N_DEV * rows_e, 1024)
    summed_e = _sum_devices(gathered_e, rows_e, "sum_small_early").reshape(-1)
    summed_l = _sum_devices(gathered_l, rows_l, "sum_small_late").reshape(-1)

    def take(summed, offs, k, shape):
        size = math.prod(shape)
        return summed[offs[k]:offs[k] + size].reshape(shape)

    def rows_of(gathered, rows_, offs, k, width):
        return gathered.reshape(N_DEV, rows_ * 1024)[:, offs[k]:offs[k] + width]

    loss = take(summed_e, offs_e, 0, ())
    grad_final_g = take(summed_e, offs_e, 1, (d,))
    grad_norm_g = jnp.concatenate([take(summed_l, offs_l, 0, (1, d)), take(summed_e, offs_e, 2, (1, d))], axis=0)
    grad_ln_g_full = take(summed_e, offs_e, 3, (1, e))
    grad_ln_b_full = take(summed_e, offs_e, 4, (1, e))
    grad_b_b_s = take(summed_e, offs_e, 5, (1, GROUPS, CHUNK))
    grad_b_w_s = take(summed_e, offs_e, 6, (1, GROUPS, CHUNK, CHUNK))
    grad_conv_w_full = take(summed_l, offs_l, 1, (3, e))
    grad_a_conv_b = take(summed_l, offs_l, 2, (1, e))
    grad_a_conv_w = lax.dynamic_slice_in_dim(grad_conv_w_full, chip * es, es, axis=1)[None]
    grad_b_ln_g = lax.dynamic_slice_in_dim(grad_ln_g_full, chip * es, es, axis=1)
    grad_b_ln_b = lax.dynamic_slice_in_dim(grad_ln_b_full, chip * es, es, axis=1)
    dmod_e = rows_of(gathered_e, rows_e, offs_e, 7, 4 * d)
    dmod_l = rows_of(gathered_l, rows_l, offs_l, 3, 2 * d)
    dmod_all = jnp.stack([jnp.concatenate([dmod_l, dmod_e[:, 3 * d:]], axis=1), dmod_e[:, :3 * d]], axis=1)
    mod_b_e = take(summed_e, offs_e, 7, (4 * d,))
    mod_b_l = take(summed_l, offs_l, 3, (2 * d,))
    grad_mod_b = jnp.stack([jnp.concatenate([mod_b_l, mod_b_e[3 * d:]]), mod_b_e[:3 * d]])
    dmod_cols = jnp.transpose(lax.dynamic_slice_in_dim(dmod_all, chip * wd, wd, axis=2), (1, 0, 2))

    grad_mod_w, delta_mod_w, new_m_mod_w, new_v_mod_w = _mod_w_update(
        jnp.transpose(c_act), dmod_cols, mod_w, m_mod_w, v_mod_w, "mod_w_update")
    (g_a_w_out,) = finish(ao_send, ao_recv, ao_grads, ao_lands, ["row"], ["a_w_out"], delta_mod_w, "a_out")
    adamw_big("a_w_out", a_w_out, g_a_w_out, m_a_w_out, v_a_w_out)
    (g_a_w_in,) = finish(ai_send, ai_recv, ai_grads, ai_lands, ["col"], ["a_w_in"], upd["a_w_out"][0], "a_in")
    adamw_big("a_w_in", a_w_in, g_a_w_in, m_a_w_in, v_a_w_in)
    small_w = [("mod_b", mod_b, grad_mod_b, m_mod_b, v_mod_b), ("norm_g", norm_g, grad_norm_g, m_norm_g, v_norm_g),
               ("a_conv_w", a_conv_w, grad_a_conv_w, m_a_conv_w, v_a_conv_w),
               ("a_conv_b", a_conv_b, grad_a_conv_b, m_a_conv_b, v_a_conv_b),
               ("b_ln_g", b_ln_g, grad_b_ln_g, m_b_ln_g, v_b_ln_g), ("b_ln_b", b_ln_b, grad_b_ln_b, m_b_ln_b, v_b_ln_b),
               ("b_w_s", b_w_s, grad_b_w_s, m_b_w_s, v_b_w_s), ("b_b_s", b_b_s, grad_b_b_s, m_b_b_s, v_b_b_s),
               ("final_g", final_g, grad_final_g, m_final_g, v_final_g)]

    def flat2d(a):
        return a.reshape(-1, a.shape[-1])

    res = _adamw_small([flat2d(t[1]) for t in small_w], [flat2d(t[2]) for t in small_w],
                       [flat2d(t[3]) for t in small_w], [flat2d(t[4]) for t in small_w], "adamw_small")
    for (nm, w, _, _, _), r3 in zip(small_w, res):
        upd[nm] = tuple(o.reshape(w.shape) for o in r3)
    upd["mod_w"] = (delta_mod_w, new_m_mod_w, new_v_mod_w)

    grads = {"mod_w": grad_mod_w, "mod_b": grad_mod_b, "norm_g": grad_norm_g, "a_conv_w": grad_a_conv_w,
             "a_conv_b": grad_a_conv_b, "b_ln_g": grad_b_ln_g, "b_ln_b": grad_b_ln_b, "b_w_s": grad_b_w_s,
             "b_b_s": grad_b_b_s, "final_g": grad_final_g, **big_grads}
    order = ["mod_w", "mod_b", "norm_g", "a_w_in", "a_conv_w", "a_conv_b", "a_w_out", "b_w_in", "b_ln_g", "b_ln_b",
             "b_w_s", "b_b_s", "b_w_out", "final_g"]
    return (loss, grad_x[None], *[grads[k] for k in order], *[upd[k][0] for k in order],
            *[upd[k][1] for k in order], *[upd[k][2] for k in order])
```

```python
import math

import jax
import jax.numpy as jnp
from jax import lax
from jax.experimental import pallas as pl
from jax.experimental.pallas import tpu as pltpu

F32 = jnp.float32
BF16 = jnp.bfloat16
MESH = pl.DeviceIdType.MESH

N_DEV = 8
N_CHIP = 4
SUBLANE = 8
PACK = 16
ACT = F32
RMS_EPS = 1e-6
LN_EPS = 1e-5
CHUNK = 128
GROUPS = 8
ADAM_LR = 0.001
ADAM_B1 = 0.9
ADAM_B2 = 0.999
ADAM_EPS = 1e-08
ADAM_WD = 0.01
ADAM_STEP = 10
VMEM_LIMIT = 56 << 20

HBM_SPEC = pl.BlockSpec(memory_space=pltpu.HBM)
VMEM_SPEC = pl.BlockSpec(memory_space=pltpu.VMEM)
SEM_SPEC = pl.BlockSpec(memory_space=pltpu.SEMAPHORE)
ANY_SPEC = pl.BlockSpec(memory_space=pl.ANY)
EFFECT = pltpu.SideEffectType.DATAFLOW_SIDE_EFFECTING


def _params(*sem):
    return pltpu.CompilerParams(dimension_semantics=sem, vmem_limit_bytes=VMEM_LIMIT)


def _tile(n, want):
    if n <= want:
        return n
    t = want
    while n % t:
        t -= 128
    return t


def _sigmoid(x):
    return 0.5 * jnp.tanh(0.5 * x) + 0.5


def _silu_and_grad(x):
    s = _sigmoid(x)
    return x * s, s * (1.0 + x * (1.0 - s))


def _gelu_and_grad(x):
    cdf = 0.5 * (1.0 + lax.erf(x * (1.0 / math.sqrt(2.0))))
    pdf = jnp.exp(-0.5 * x * x) * (1.0 / math.sqrt(2.0 * math.pi))
    return x * cdf, cdf + x * pdf


def _gelu(x):
    return x * (0.5 * (1.0 + lax.erf(x * (1.0 / math.sqrt(2.0)))))


def _rms(x):
    r = lax.rsqrt(jnp.mean(x * x, axis=-1, keepdims=True) + RMS_EPS)
    return x * r, r


def _rms_bwd(dxn, xn, r):
    return r * (dxn - xn * jnp.mean(dxn * xn, axis=-1, keepdims=True))


def _colsum(a):
    return jnp.sum(a, axis=0, keepdims=True)


def _shift_down(cur, before, k):
    rolled = pltpu.roll(cur, k, 0)
    row = lax.broadcasted_iota(jnp.int32, before.shape, 0)
    head = jnp.where(row < k, pltpu.roll(before, k, 0), rolled[:SUBLANE])
    return jnp.concatenate([head, rolled[SUBLANE:]], axis=0)


def _shift_up(cur, after, k):
    n = cur.shape[0]
    rolled = pltpu.roll(cur, n - k, 0)
    row = lax.broadcasted_iota(jnp.int32, after.shape, 0)
    tail = jnp.where(row >= SUBLANE - k, pltpu.roll(after, SUBLANE - k, 0), rolled[n - SUBLANE:])
    return jnp.concatenate([rolled[:n - SUBLANE], tail], axis=0)


def _after_spec():
    return pl.BlockSpec((SUBLANE, 128), lambda *_: (0, 0))


def _mm_proj(h, w, n_split, name, after):
    s, d = h.shape
    e = w.shape[1] // n_split
    tm, tn = _tile(s, 2048), _tile(e, 2048)
    nj = e // tn

    def body(h_ref, w_ref, after_ref, o_ref):
        o_ref[...] = jnp.dot(h_ref[...], w_ref[...], preferred_element_type=F32).astype(ACT)

    return pl.pallas_call(
        body, name=name,
        out_shape=jax.ShapeDtypeStruct((n_split, s, e), ACT),
        grid=(s // tm, n_split * nj),
        in_specs=[pl.BlockSpec((tm, d), lambda i, j: (i, 0)), pl.BlockSpec((d, tn), lambda i, j: (0, j)),
                  _after_spec()],
        out_specs=pl.BlockSpec((None, tm, tn), lambda i, j: (j // nj, i, j % nj)),
        compiler_params=_params("parallel", "parallel"),
    )(h, w, after)


def _mm_proj_slab(h, w, proj, q_idx, n_split, name, after):
    s, d = h.shape
    e = w.shape[1] // n_split
    tm, tn = _tile(s, 1024), _tile(e, 2048)
    nj = e // tn

    def body(q_ref, h_ref, w_ref, *rest):
        o_ref = rest[-1]
        o_ref[...] = jnp.dot(h_ref[...], w_ref[...], preferred_element_type=F32).astype(ACT)

    in_specs = [pl.BlockSpec((tm, d), lambda i, j, qr: (i, 0)), pl.BlockSpec((d, tn), lambda i, j, qr: (0, qr[0] * nj + j)),
                _after_spec()]
    args = [q_idx, h, w, after]
    aliases = {}
    if proj is not None:
        in_specs.append(ANY_SPEC)
        args.append(proj)
        aliases = {4: 0}
    return pl.pallas_call(
        body, name=name,
        out_shape=jax.ShapeDtypeStruct((n_split, s, e), ACT),
        grid_spec=pltpu.PrefetchScalarGridSpec(
            num_scalar_prefetch=1, grid=(s // tm, nj), in_specs=in_specs,
            out_specs=pl.BlockSpec((None, tm, tn), lambda i, j, qr: (qr[0], i, j))),
        input_output_aliases=aliases,
        compiler_params=_params("parallel", "parallel"),
    )(*args)


def _mm_dh_norm_bwd(dp, w, x, dx_in, g, scale, name, after, br=None, gate=None):
    nq, s, e = dp.shape
    d = w.shape[0]
    has_branch = br is not None
    tm, tk = _tile(s, 1024), _tile(e, 1024 if has_branch else 2048)
    nkq = e // tk
    nk = nq * nkq

    def body(*refs):
        if has_branch:
            (a_ref, b_ref, after_ref, x_ref, dxin_ref, g_ref, sc_ref, br_ref, gate_ref,
             dx_ref, dsh_ref, dsc_ref, dg_ref, dbr_ref, dgate_ref, acc_ref) = refs
        else:
            (a_ref, b_ref, after_ref, x_ref, dxin_ref, g_ref, sc_ref,
             dx_ref, dsh_ref, dsc_ref, dg_ref, acc_ref) = refs
        i, k = pl.program_id(0), pl.program_id(1)

        @pl.when(jnp.logical_and(i == 0, k == 0))
        def _():
            dsh_ref[...] = jnp.zeros_like(dsh_ref)
            dsc_ref[...] = jnp.zeros_like(dsc_ref)
            dg_ref[...] = jnp.zeros_like(dg_ref)
            if has_branch:
                dgate_ref[...] = jnp.zeros_like(dgate_ref)

        @pl.when(k == 0)
        def _():
            acc_ref[...] = jnp.zeros_like(acc_ref)
        acc_ref[...] += lax.dot_general(a_ref[...], b_ref[...], (((1,), (1,)), ((), ())), preferred_element_type=F32)

        @pl.when(k == nk - 1)
        def _():
            dh = acc_ref[...]
            g_vec = g_ref[...]
            xn, r = _rms(x_ref[...])
            dsh_ref[...] += _colsum(dh)
            dsc_ref[...] += _colsum(dh * (xn * g_vec))
            da = dh * (1.0 + sc_ref[...])
            dg_ref[...] += _colsum(da * xn)
            dx = dxin_ref[...] + _rms_bwd(da * g_vec, xn, r)
            dx_ref[...] = dx
            if has_branch:
                dgate_ref[...] += _colsum(dx * br_ref[...])
                dbr_ref[...] = (gate_ref[...] * dx).astype(BF16)

    rows = pl.BlockSpec((tm, d), lambda i, k: (i, 0))
    vec = pl.BlockSpec((1, d), lambda i, k: (0, 0))
    vec_out = jax.ShapeDtypeStruct((1, d), F32)
    in_specs = [pl.BlockSpec((None, tm, tk), lambda i, k: (k // nkq, i, k % nkq)),
                pl.BlockSpec((d, tk), lambda i, k: (0, k)), _after_spec(), rows, rows, vec, vec]
    out_shape = [jax.ShapeDtypeStruct((s, d), F32), vec_out, vec_out, vec_out]
    out_specs = [rows, vec, vec, vec]
    args = [dp, w, after, x, dx_in, g, scale]
    if has_branch:
        in_specs += [rows, vec]
        out_shape += [jax.ShapeDtypeStruct((s, d), BF16), vec_out]
        out_specs += [rows, vec]
        args += [br, gate]
    return pl.pallas_call(
        body, name=name, out_shape=tuple(out_shape), grid=(s // tm, nk),
        in_specs=in_specs, out_specs=tuple(out_specs), scratch_shapes=[pltpu.VMEM((tm, d), F32)],
        compiler_params=_params("arbitrary", "arbitrary"),
    )(*args)


def _dw_body(n_t):
    def body(a_ref, b_ref, o_ref, acc_ref):
        t = pl.program_id(2)

        @pl.when(t == 0)
        def _():
            acc_ref[...] = jnp.zeros_like(acc_ref)
        acc_ref[...] += lax.dot_general(a_ref[...], b_ref[...], (((0,), (0,)), ((), ())), preferred_element_type=F32)

        @pl.when(t == n_t - 1)
        def _():
            o_ref[...] = acc_ref[...].astype(o_ref.dtype)
    return body


def _mm_dw_in(h, dp, name):
    s, d = h.shape
    nq, _, e = dp.shape
    tm, tn, tt = _tile(d, 1024), _tile(e, 2048), _tile(s, 2048)
    nj = e // tn

    return pl.pallas_call(
        _dw_body(s // tt), name=name,
        out_shape=jax.ShapeDtypeStruct((d, nq * e), BF16),
        grid=(d // tm, nq * nj, s // tt),
        in_specs=[pl.BlockSpec((tt, tm), lambda i, j, t: (t, i)),
                  pl.BlockSpec((None, tt, tn), lambda i, j, t: (j // nj, t, j % nj))],
        out_specs=pl.BlockSpec((tm, tn), lambda i, j, t: (i, j)),
        scratch_shapes=[pltpu.VMEM((tm, tn), F32)],
        compiler_params=_params("parallel", "parallel", "arbitrary"),
    )(h, dp)


def _mm_dw_out(y, dbr, name):
    s, e = y.shape
    d = dbr.shape[1]
    tm, tn, tt = _tile(e, 1024), _tile(d, 1024), _tile(s, 2048)

    return pl.pallas_call(
        _dw_body(s // tt), name=name,
        out_shape=jax.ShapeDtypeStruct((e, d), BF16),
        grid=(e // tm, d // tn, s // tt),
        in_specs=[pl.BlockSpec((tt, tm), lambda i, j, t: (t, i)), pl.BlockSpec((tt, tn), lambda i, j, t: (t, j))],
        out_specs=pl.BlockSpec((tm, tn), lambda i, j, t: (i, j)),
        scratch_shapes=[pltpu.VMEM((tm, tn), F32)],
        compiler_params=_params("parallel", "parallel", "arbitrary"),
    )(y, dbr)


def _row_spec(ts, d):
    return pl.BlockSpec((ts, d), lambda i: (i, 0))


def _vec_spec(d):
    return pl.BlockSpec((1, d), lambda i: (0, 0))


def _norm_mod(x, g, scale, shift, name):
    s, d = x.shape
    ts = _tile(s, 512)

    def body(x_ref, g_ref, sc_ref, sh_ref, h_ref):
        xn, _ = _rms(x_ref[...])
        h_ref[...] = ((xn * g_ref[...]) * (1.0 + sc_ref[...]) + sh_ref[...]).astype(BF16)

    return pl.pallas_call(
        body, name=name, out_shape=jax.ShapeDtypeStruct((s, d), BF16), grid=(s // ts,),
        in_specs=[_row_spec(ts, d), _vec_spec(d), _vec_spec(d), _vec_spec(d)],
        out_specs=_row_spec(ts, d), compiler_params=_params("parallel"),
    )(x, g, scale, shift)


def _head_tile(x1, br, gate, gf, target):
    d = x1.shape[-1]
    xn, r = _rms(x1 + gate * br)
    err = xn * gf - target
    loss = 0.5 * jnp.sum(jnp.mean(err * err, axis=-1, keepdims=True))
    dout = err * (1.0 / d)
    dx = _rms_bwd(dout * gf, xn, r)
    return dx, gate * dx, loss, _colsum(dout * xn), _colsum(dx * br)


def _f32(ref):
    return ref[...].astype(F32)


def _cols_f32(ref, cols):
    return ref[:, cols].astype(F32)


def _rows_before(halo_ref, cols=slice(None)):
    return _cols_f32(halo_ref, cols)[PACK - SUBLANE:]


def _rows_after(halo_ref, cols=slice(None)):
    return _cols_f32(halo_ref, cols)[:SUBLANE]


CONV_CHAIN_COLS = 512


def _conv_fwd(proj, conv_w, conv_b, w_out, x, gate, g_next, scale_next, shift_next, name, after):
    _, s, e = proj.shape
    d = w_out.shape[1]
    ts, te = _tile(s, 256), e
    cw = min(CONV_CHAIN_COLS, te)
    hb = ts // PACK

    def body(bg_ref, cg_ref, xi_ref, z_ref, cgp_ref, xip_ref, w0_ref, w1_ref, w2_ref, b_ref, wo_ref, after_ref,
             x_ref, gate_ref, g_ref, sc_ref, sh_ref, y_ref, br_ref, x1_ref, h_ref):
        first = pl.program_id(0) == 0
        br = None
        for c0 in range(0, te, cw):
            cols = slice(c0, c0 + cw)
            cx = _cols_f32(cg_ref, cols) * _cols_f32(xi_ref, cols)
            before = jnp.where(first, 0.0, _rows_before(cgp_ref, cols) * _rows_before(xip_ref, cols))
            conv = b_ref[:, cols] + w2_ref[:, cols] * cx
            conv = conv + w0_ref[:, cols] * _shift_down(cx, before, 2)
            conv = conv + w1_ref[:, cols] * _shift_down(cx, before, 1)
            z = _cols_f32(z_ref, cols)
            y = ((z * _sigmoid(z)) * _cols_f32(bg_ref, cols) * conv).astype(BF16)
            y_ref[:, cols] = y
            part_br = jnp.dot(y, wo_ref[cols, :], preferred_element_type=F32)
            br = part_br if br is None else br + part_br
        br_ref[...] = br
        x1 = x_ref[...] + gate_ref[...] * br
        x1_ref[...] = x1
        xn, _ = _rms(x1)
        h_ref[...] = ((xn * g_ref[...]) * (1.0 + sc_ref[...]) + sh_ref[...]).astype(BF16)

    def part(q):
        return pl.BlockSpec((None, ts, te), lambda i: (q, i, 0))

    def halo_before(q):
        return pl.BlockSpec((None, PACK, te), lambda i: (q, jnp.maximum(i * hb - 1, 0), 0))

    return pl.pallas_call(
        body, name=name,
        out_shape=(jax.ShapeDtypeStruct((s, e), BF16), jax.ShapeDtypeStruct((s, d), F32),
                   jax.ShapeDtypeStruct((s, d), F32), jax.ShapeDtypeStruct((s, d), BF16)), grid=(s // ts,),
        in_specs=[part(0), part(1), part(2), part(3), halo_before(1), halo_before(2)]
        + [_vec_spec(e)] * 4 + [pl.BlockSpec((e, d), lambda i: (0, 0)), _after_spec(), _row_spec(ts, d)]
        + [_vec_spec(d)] * 4,
        out_specs=(_row_spec(ts, e), _row_spec(ts, d), _row_spec(ts, d), _row_spec(ts, d)),
        compiler_params=_params("parallel"),
    )(proj, proj, proj, proj, proj, proj, *conv_w, conv_b, w_out, after, x, gate, g_next, scale_next, shift_next)


def _conv_bwd(proj, dbr, w_out, conv_w, conv_b, name, after):
    _, s, e = proj.shape
    d = dbr.shape[1]
    ts, te = _tile(s, 512), _tile(e, 1024)
    cw = min(CONV_CHAIN_COLS // 2, te)
    hb = ts // PACK
    n_i = s // ts
    last_halo = s // PACK - 1
    nt = (((1,), (1,)), ((), ()))

    def body(bg_ref, cg_ref, xi_ref, z_ref, dbr_ref, cgp_ref, xip_ref, bgn_ref, zn_ref, dbrn_ref, wo_ref,
             w0_ref, w1_ref, w2_ref, b_ref, after_ref, dp_ref, dw0_ref, dw1_ref, dw2_ref, db_ref):
        i = pl.program_id(1)

        @pl.when(i == 0)
        def _():
            for acc in (dw0_ref, dw1_ref, dw2_ref, db_ref):
                acc[...] = jnp.zeros_like(acc)
        for c0 in range(0, te, cw):
            cols = slice(c0, c0 + cw)
            wo = wo_ref[cols, :]
            dy = lax.dot_general(dbr_ref[...], wo, nt, preferred_element_type=F32)
            dyn = lax.dot_general(dbrn_ref[...], wo, nt, preferred_element_type=F32)[:SUBLANE]
            bg, cg = _cols_f32(bg_ref, cols), _cols_f32(cg_ref, cols)
            xi, z = _cols_f32(xi_ref, cols), _cols_f32(z_ref, cols)
            w0, w1, w2 = w0_ref[:, cols], w1_ref[:, cols], w2_ref[:, cols]
            cx = cg * xi
            before = jnp.where(i > 0, _rows_before(cgp_ref, cols) * _rows_before(xip_ref, cols), 0.0)
            cx1 = _shift_down(cx, before, 1)
            cx2 = _shift_down(cx, before, 2)
            conv = b_ref[:, cols] + w2 * cx
            conv = conv + w0 * cx2
            conv = conv + w1 * cx1
            sz, dsz = _silu_and_grad(z)
            dp_ref[3, :, cols] = (dy * bg * conv * dsz).astype(BF16)
            dp_ref[0, :, cols] = (dy * sz * conv).astype(BF16)
            dconv = dy * sz * bg
            zn = _rows_after(zn_ref, cols)
            after = jnp.where(i < n_i - 1, dyn * (zn * _sigmoid(zn)) * _rows_after(bgn_ref, cols), 0.0)
            db_ref[:, cols] += _colsum(dconv)
            dw2_ref[:, cols] += _colsum(dconv * cx)
            dw1_ref[:, cols] += _colsum(dconv * cx1)
            dw0_ref[:, cols] += _colsum(dconv * cx2)
            dcx = w2 * dconv + w1 * _shift_up(dconv, after, 1) + w0 * _shift_up(dconv, after, 2)
            dp_ref[1, :, cols] = (dcx * xi).astype(BF16)
            dp_ref[2, :, cols] = (dcx * cg).astype(BF16)

    def part(q):
        return pl.BlockSpec((None, ts, te), lambda j, i: (q, i, j))

    def halo_before(q):
        return pl.BlockSpec((None, PACK, te), lambda j, i: (q, jnp.maximum(i * hb - 1, 0), j))

    def halo_after(q):
        return pl.BlockSpec((None, PACK, te), lambda j, i: (q, jnp.minimum((i + 1) * hb, last_halo), j))

    return pl.pallas_call(
        body, name=name,
        out_shape=(jax.ShapeDtypeStruct((4, s, e), BF16),) + (jax.ShapeDtypeStruct((1, e), F32),) * 4,
        grid=(e // te, n_i),
        in_specs=[part(0), part(1), part(2), part(3), pl.BlockSpec((ts, d), lambda j, i: (i, 0)),
                  halo_before(1), halo_before(2), halo_after(0), halo_after(3),
                  pl.BlockSpec((PACK, d), lambda j, i: (jnp.minimum((i + 1) * hb, last_halo), 0)),
                  pl.BlockSpec((te, d), lambda j, i: (j, 0))]
        + [pl.BlockSpec((1, te), lambda j, i: (0, j))] * 4 + [_after_spec()],
        out_specs=(pl.BlockSpec((4, ts, te), lambda j, i: (0, i, j)),) + (pl.BlockSpec((1, te), lambda j, i: (0, j)),) * 4,
        compiler_params=_params("parallel", "arbitrary"),
    )(proj, proj, proj, proj, dbr, proj, proj, proj, proj, dbr, w_out, *conv_w, conv_b, after)


def _tril(w):
    row = lax.broadcasted_iota(jnp.int32, w.shape, 0)
    col = lax.broadcasted_iota(jnp.int32, w.shape, 1)
    return jnp.where(row >= col, w, 0.0)


def _triu(w):
    row = lax.broadcasted_iota(jnp.int32, w.shape, 0)
    col = lax.broadcasted_iota(jnp.int32, w.shape, 1)
    return jnp.where(row <= col, w, 0.0)


def _layer_norm_fwd(v, g, b):
    mu = jnp.mean(v, axis=-1, keepdims=True)
    vc = v - mu
    rstd = lax.rsqrt(jnp.mean(vc * vc, axis=-1, keepdims=True) + LN_EPS)
    vhat = vc * rstd
    return vhat * g + b, vhat, rstd


GMLP_CHUNKS_PER_STEP = 2


def _gmlp_rows(s):
    return CHUNK * min(GMLP_CHUNKS_PER_STEP, s // CHUNK)


def _mix_positions(w_ref, src_scr, dst_scr, gw, mask, bias_ref=None):
    for gi in range(GROUPS):
        cols = slice(gi * gw, (gi + 1) * gw)
        wm = mask(w_ref[gi]).astype(BF16)
        for n in range(src_scr.shape[0] // CHUNK):
            rows = slice(n * CHUNK, (n + 1) * CHUNK)
            out = jnp.dot(wm, src_scr[rows, cols], preferred_element_type=F32)
            if bias_ref is not None:
                out = out + bias_ref[:, gi:gi + 1]
            dst_scr[rows, cols] = out


def _gmlp_fwd_head(proj, ln_g, ln_b, w_s, b_s_t, w_out, x1, gate, gf, target, name):
    _, s, e = proj.shape
    d = w_out.shape[1]
    gw = e // GROUPS
    ts = _gmlp_rows(s)

    def body(pu_ref, pv_ref, pz_ref, g_ref, b_ref, ws_ref, bs_ref, wo_ref, x1_ref, gate_ref, gf_ref, tg_ref,
             y_ref, dx_ref, dbr_ref, loss_ref, dgf_ref, dgate_ref, vn_scr, mix_scr):
        @pl.when(pl.program_id(0) == 0)
        def _():
            loss_ref[...] = jnp.zeros_like(loss_ref)
            dgf_ref[...] = jnp.zeros_like(dgf_ref)
            dgate_ref[...] = jnp.zeros_like(dgate_ref)
        vn, _, _ = _layer_norm_fwd(_gelu(_f32(pv_ref)), g_ref[...], b_ref[...])
        vn_scr[...] = vn.astype(BF16)
        _mix_positions(ws_ref, vn_scr, mix_scr, gw, _tril, bs_ref)
        z = _f32(pz_ref)
        y = ((z * _sigmoid(z)) * (_gelu(_f32(pu_ref)) * mix_scr[...])).astype(BF16)
        y_ref[...] = y
        br = jnp.dot(y, wo_ref[...], preferred_element_type=F32)
        dx, dbr, loss, dgf, dgate = _head_tile(x1_ref[...], br, gate_ref[...], gf_ref[...], tg_ref[...])
        dx_ref[...] = dx
        dbr_ref[...] = dbr.astype(BF16)
        loss_ref[...] += loss
        dgf_ref[...] += dgf
        dgate_ref[...] += dgate

    def part(q):
        return pl.BlockSpec((None, ts, e), lambda i: (q, i, 0))

    return pl.pallas_call(
        body, name=name,
        out_shape=(jax.ShapeDtypeStruct((s, e), BF16), jax.ShapeDtypeStruct((s, d), F32),
                   jax.ShapeDtypeStruct((s, d), BF16), jax.ShapeDtypeStruct((SUBLANE, 128), F32),
                   jax.ShapeDtypeStruct((1, d), F32), jax.ShapeDtypeStruct((1, d), F32)),
        grid=(s // ts,),
        in_specs=[part(0), part(1), part(2), _vec_spec(e), _vec_spec(e),
                  pl.BlockSpec((GROUPS, CHUNK, CHUNK), lambda i: (0, 0, 0)),
                  pl.BlockSpec((CHUNK, GROUPS), lambda i: (0, 0)), pl.BlockSpec((e, d), lambda i: (0, 0)),
                  _row_spec(ts, d), _vec_spec(d), _vec_spec(d), _row_spec(ts, d)],
        out_specs=(_row_spec(ts, e), _row_spec(ts, d), _row_spec(ts, d),
                   pl.BlockSpec((SUBLANE, 128), lambda i: (0, 0)), _vec_spec(d), _vec_spec(d)),
        scratch_shapes=[pltpu.VMEM((ts, e), BF16), pltpu.VMEM((ts, e), F32)],
        compiler_params=_params("arbitrary"),
    )(proj, proj, proj, ln_g, ln_b, w_s, b_s_t, w_out, x1, gate, gf, target)


def _gmlp_bwd(proj, dbr, w_out, ln_g, ln_b, w_s, w_s_t, b_s_t, name):
    _, s, e = proj.shape
    d = dbr.shape[1]
    gw = e // GROUPS
    ts = _gmlp_rows(s)
    n_i = s // ts

    def body(pu_ref, pv_ref, pz_ref, dbr_ref, wo_ref, g_ref, b_ref, ws_ref, wst_ref, bs_ref,
             dp_ref, dws_ref, dbs_ref, dlg_ref, dlb_ref, vn_scr, mix_scr, dm_scr, dvn_scr, dmacc_scr):
        i = pl.program_id(0)

        @pl.when(i == 0)
        def _():
            dws_ref[...] = jnp.zeros_like(dws_ref)
            dlg_ref[...] = jnp.zeros_like(dlg_ref)
            dlb_ref[...] = jnp.zeros_like(dlb_ref)
            dmacc_scr[...] = jnp.zeros_like(dmacc_scr)
        ln_g = g_ref[...]
        u, du_dpu = _gelu_and_grad(_f32(pu_ref))
        v, dv_dpv = _gelu_and_grad(_f32(pv_ref))
        vn, vhat, rstd = _layer_norm_fwd(v, ln_g, b_ref[...])
        vn_scr[...] = vn.astype(BF16)
        _mix_positions(ws_ref, vn_scr, mix_scr, gw, _tril, bs_ref)
        mixed = mix_scr[...]
        dy = lax.dot_general(dbr_ref[...], wo_ref[...], (((1,), (1,)), ((), ())), preferred_element_type=F32)
        sz, dsz = _silu_and_grad(_f32(pz_ref))
        ds = dy * sz
        dp_ref[2] = (dy * (u * mixed) * dsz).astype(BF16)
        dp_ref[0] = (ds * mixed * du_dpu).astype(BF16)
        dm = ds * u
        dm_scr[...] = dm.astype(BF16)
        for n in range(ts // CHUNK):
            rows = slice(n * CHUNK, (n + 1) * CHUNK)
            dmacc_scr[...] += dm[rows]
            for gi in range(GROUPS):
                cols = slice(gi * gw, (gi + 1) * gw)
                dws_ref[gi] += lax.dot_general(dm_scr[rows, cols], vn_scr[rows, cols], (((1,), (1,)), ((), ())),
                                               preferred_element_type=F32)
        _mix_positions(wst_ref, dm_scr, dvn_scr, gw, _triu)
        dvn = dvn_scr[...]
        dlg_ref[...] += _colsum(dvn * vhat)
        dlb_ref[...] += _colsum(dvn)
        dvh = dvn * ln_g
        dv = rstd * (dvh - jnp.mean(dvh, axis=-1, keepdims=True) - vhat * jnp.mean(dvh * vhat, axis=-1, keepdims=True))
        dp_ref[1] = (dv * dv_dpv).astype(BF16)

        @pl.when(i == n_i - 1)
        def _():
            for gi in range(GROUPS):
                dws_ref[gi] = _tril(dws_ref[gi])
                dbs_ref[:, gi:gi + 1] = jnp.sum(dmacc_scr[:, gi * gw:(gi + 1) * gw], axis=1, keepdims=True)

    def part(q):
        return pl.BlockSpec((None, ts, e), lambda i: (q, i, 0))

    w_spec = pl.BlockSpec((GROUPS, CHUNK, CHUNK), lambda i: (0, 0, 0))
    bs_spec = pl.BlockSpec((CHUNK, GROUPS), lambda i: (0, 0))
    return pl.pallas_call(
        body, name=name,
        out_shape=(jax.ShapeDtypeStruct((3, s, e), BF16), jax.ShapeDtypeStruct((GROUPS, CHUNK, CHUNK), F32),
                   jax.ShapeDtypeStruct((CHUNK, GROUPS), F32), jax.ShapeDtypeStruct((1, e), F32),
                   jax.ShapeDtypeStruct((1, e), F32)),
        grid=(n_i,),
        in_specs=[part(0), part(1), part(2), pl.BlockSpec((ts, d), lambda i: (i, 0)),
                  pl.BlockSpec((e, d), lambda i: (0, 0)), _vec_spec(e), _vec_spec(e), w_spec, w_spec, bs_spec],
        out_specs=(pl.BlockSpec((3, ts, e), lambda i: (0, i, 0)), w_spec, bs_spec, _vec_spec(e), _vec_spec(e)),
        scratch_shapes=[pltpu.VMEM((ts, e), BF16), pltpu.VMEM((ts, e), F32), pltpu.VMEM((ts, e), BF16),
                        pltpu.VMEM((ts, e), F32), pltpu.VMEM((CHUNK, e), F32)],
        compiler_params=_params("arbitrary"),
    )(proj, proj, proj, dbr, w_out, ln_g, ln_b, w_s, w_s_t, b_s_t)


def _mod_fwd(c_all, mod_w, mod_b_cols, name):
    n_layer, d, w = mod_w.shape

    def body(c_ref, w_ref, b_ref, ca_ref, o_ref):
        c = c_ref[...]
        ca = c * _sigmoid(c)
        ca_ref[...] = ca
        for li in range(n_layer):
            o_ref[li * N_DEV:(li + 1) * N_DEV, :] = (
                jnp.dot(ca, w_ref[li], preferred_element_type=F32, precision=lax.Precision.HIGHEST) + b_ref[li])

    return pl.pallas_call(
        body, name=name,
        out_shape=(jax.ShapeDtypeStruct((N_DEV, d), F32), jax.ShapeDtypeStruct((n_layer * N_DEV, w), F32)),
        in_specs=[VMEM_SPEC] * 3, out_specs=(VMEM_SPEC, VMEM_SPEC),
        compiler_params=pltpu.CompilerParams(vmem_limit_bytes=VMEM_LIMIT),
    )(c_all, mod_w, mod_b_cols)


def _adamw(w, g, m, v):
    m = ADAM_B1 * m + (1.0 - ADAM_B1) * g
    v = ADAM_B2 * v + (1.0 - ADAM_B2) * (g * g)
    m_hat = m / (1.0 - ADAM_B1 ** ADAM_STEP)
    v_hat = v / (1.0 - ADAM_B2 ** ADAM_STEP)
    delta = -ADAM_LR * (m_hat / (jnp.sqrt(v_hat) + ADAM_EPS) + ADAM_WD * w)
    return delta, m, v


def _adamw_2d(w, g, m, v, name):
    r, c = w.shape
    tr, tc = _tile(r, 512), _tile(c, 1024)

    def body(w_ref, g_ref, m_ref, v_ref, go_ref, d_ref, nm_ref, nv_ref):
        g = g_ref[...]
        go_ref[...] = g
        d_ref[...], nm_ref[...], nv_ref[...] = _adamw(w_ref[...], g, m_ref[...], v_ref[...])

    spec = pl.BlockSpec((tr, tc), lambda i, j: (i, j))
    shape = jax.ShapeDtypeStruct((r, c), F32)
    return pl.pallas_call(
        body, name=name, out_shape=(shape,) * 4, grid=(r // tr, c // tc),
        in_specs=[spec] * 4, out_specs=(spec,) * 4, compiler_params=_params("parallel", "parallel"),
    )(w, g, m, v)


def _mod_w_update(ca_t, dmod_cols, w, m, v, name):
    n_layer, d, wd = w.shape
    tr = _tile(d, 256)

    def body(ca_ref, dm_ref, w_ref, m_ref, v_ref, g_ref, d_ref, nm_ref, nv_ref):
        ca = ca_ref[...]
        dm = dm_ref[...]
        g = ca[:, 0:1] * dm[0:1, :]
        for b in range(1, N_DEV):
            g = g + ca[:, b:b + 1] * dm[b:b + 1, :]
        g_ref[...] = g
        d_ref[...], nm_ref[...], nv_ref[...] = _adamw(w_ref[...], g, m_ref[...], v_ref[...])

    spec = pl.BlockSpec((None, tr, wd), lambda l, i: (l, i, 0))
    shape = jax.ShapeDtypeStruct((n_layer, d, wd), F32)
    return pl.pallas_call(
        body, name=name, out_shape=(shape,) * 4, grid=(n_layer, d // tr),
        in_specs=[pl.BlockSpec((tr, N_DEV), lambda l, i: (i, 0)), pl.BlockSpec((None, N_DEV, wd), lambda l, i: (l, 0, 0)),
                  spec, spec, spec],
        out_specs=(spec,) * 4, compiler_params=_params("parallel", "parallel"),
    )(ca_t, dmod_cols, w, m, v)


def _adamw_small(ws, gs, ms, vs, name):
    n = len(ws)

    def body(*refs):
        ins, outs = refs[:4 * n], refs[4 * n:]
        for k in range(n):
            delta, nm, nv = _adamw(ins[k][...], ins[n + k][...], ins[2 * n + k][...], ins[3 * n + k][...])
            outs[3 * k][...] = delta
            outs[3 * k + 1][...] = nm
            outs[3 * k + 2][...] = nv

    out_shape = []
    for w in ws:
        out_shape += [jax.ShapeDtypeStruct(w.shape, F32)] * 3
    outs = pl.pallas_call(
        body, name=name, out_shape=tuple(out_shape),
        in_specs=[VMEM_SPEC] * (4 * n), out_specs=tuple([VMEM_SPEC] * (3 * n)),
        compiler_params=pltpu.CompilerParams(vmem_limit_bytes=VMEM_LIMIT),
    )(*ws, *gs, *ms, *vs)
    return [tuple(outs[3 * k:3 * k + 3]) for k in range(n)]


def _place():
    return lax.axis_index("x"), lax.axis_index("y"), lax.axis_index("c")


def _other_chips(x, y):
    return [(1 - x, y), (x, 1 - y), (1 - x, 1 - y)]


def _all_gather(block, name, after=None):
    m_per, n = block.shape

    def body(x_ref, *rest):
        out_ref, send_sems, recv_sems, local_sem = rest[-4:]
        x, y, c = _place()
        me, sibling = (x, y, c), (x, y, 1 - c)
        chips = _other_chips(x, y)

        def rows(px, py, pc):
            return out_ref.at[pl.ds((4 * px + 2 * py + pc) * m_per, m_per), :]

        def copy(k, blk, to, src=None):
            return pltpu.make_async_remote_copy(
                src_ref=rows(*blk) if src is None else src, dst_ref=rows(*blk),
                send_sem=send_sems.at[k], recv_sem=recv_sems.at[k], device_id=to, device_id_type=MESH)

        mine = pltpu.make_async_copy(x_ref, rows(*me), local_sem)
        mine.start()
        first = [copy(0, me, sibling, src=x_ref)]
        first += [copy(1 + j, me, (*chip, c), src=x_ref) for j, chip in enumerate(chips)]
        for cp in first:
            cp.start()
        passed = [copy(4 + j, (*chip, c), sibling) for j, chip in enumerate(chips)]
        for j, chip in enumerate(chips):
            copy(1 + j, (*chip, c), me).wait_recv()
            passed[j].start()
        copy(0, sibling, me).wait_recv()
        for j, chip in enumerate(chips):
            copy(4 + j, (*chip, 1 - c), me).wait_recv()
        for cp in first + passed:
            cp.wait_send()
        mine.wait()

    return pl.pallas_call(
        body, name=name, out_shape=jax.ShapeDtypeStruct((N_DEV * m_per, n), F32),
        in_specs=[VMEM_SPEC] + ([] if after is None else [ANY_SPEC]), out_specs=VMEM_SPEC,
        scratch_shapes=[pltpu.SemaphoreType.DMA((7,)), pltpu.SemaphoreType.DMA((7,)), pltpu.SemaphoreType.DMA],
        compiler_params=pltpu.CompilerParams(vmem_limit_bytes=VMEM_LIMIT),
    )(*([block] if after is None else [block, after]))


def _hbm(a):
    return pltpu.with_memory_space_constraint(a, pltpu.HBM)


def _all_gather_start(blocks, name):
    def body(in_ref, send_sems, recv_sems, thru_ref, token):
        token[...] = jnp.zeros_like(token)
        x, y, c = _place()
        mine = in_ref.at[4 * x + 2 * y + c]
        pltpu.make_async_remote_copy(
            src_ref=mine, dst_ref=mine, send_sem=send_sems.at[0], recv_sem=recv_sems.at[0],
            device_id=(x, y, 1 - c), device_id_type=MESH).start()
        for j, chip in enumerate(_other_chips(x, y)):
            for cc in range(2):
                pltpu.make_async_remote_copy(
                    src_ref=mine, dst_ref=mine, send_sem=send_sems.at[1 + 2 * j + cc],
                    recv_sem=recv_sems.at[1 + 2 * j + c], device_id=(*chip, cc), device_id_type=MESH).start()

    sems = pltpu.SemaphoreType.DMA((N_DEV - 1,))
    return pl.pallas_call(
        body, name=name,
        out_shape=(sems, sems, pltpu.HBM(blocks.shape, blocks.dtype), jax.ShapeDtypeStruct((SUBLANE, 128), F32)),
        in_specs=[HBM_SPEC], out_specs=(SEM_SPEC, SEM_SPEC, HBM_SPEC, VMEM_SPEC), input_output_aliases={0: 2},
        compiler_params=pltpu.CompilerParams(has_side_effects=EFFECT),
    )(_hbm(blocks))


def _all_gather_wait(send_sems, recv_sems, blocks, after, name):
    def body(in_ref, send_sems, recv_sems, after_ref, out_ref):
        x, y, c = _place()
        one = in_ref.at[0]
        for k in range(N_DEV - 1):
            pltpu.make_async_remote_copy(
                src_ref=one, dst_ref=one, send_sem=send_sems.at[k], recv_sem=recv_sems.at[k],
                device_id=(x, y, 1 - c), device_id_type=MESH).wait_send()
            pltpu.make_async_remote_copy(
                src_ref=one, dst_ref=one, send_sem=send_sems.at[k], recv_sem=recv_sems.at[k],
                device_id=(x, y, 1 - c), device_id_type=MESH).wait_recv()

    return pl.pallas_call(
        body, name=name, out_shape=pltpu.HBM(blocks.shape, blocks.dtype),
        in_specs=[HBM_SPEC, SEM_SPEC, SEM_SPEC, ANY_SPEC], out_specs=HBM_SPEC, input_output_aliases={0: 0},
        compiler_params=pltpu.CompilerParams(has_side_effects=EFFECT),
    )(blocks, send_sems, recv_sems, after)


def _place_own(shard, kind, chip_idx, name):
    r, cdim = shard.shape
    tr, tc = _tile(r, 512), _tile(cdim, 1024)
    nrb, ncb = r // tr, cdim // tc

    def body(k_ref, s_ref, o_ref):
        o_ref[...] = s_ref[...].astype(BF16)

    if kind == "col":
        full, o_map = (r, N_CHIP * cdim), lambda i, j, kr: (i, kr[0] * ncb + j)
    else:
        full, o_map = (N_CHIP * r, cdim), lambda i, j, kr: (kr[0] * nrb + i, j)
    return pl.pallas_call(
        body, name=name, out_shape=jax.ShapeDtypeStruct(full, BF16),
        grid_spec=pltpu.PrefetchScalarGridSpec(
            num_scalar_prefetch=1, grid=(nrb, ncb),
            in_specs=[pl.BlockSpec((tr, tc), lambda i, j, kr: (i, j))],
            out_specs=pl.BlockSpec((tr, tc), o_map)),
        compiler_params=_params("parallel", "parallel"),
    )(chip_idx, shard)


def _weight_window(ref, kind, shard_shape, k, half):
    r, cdim = shard_shape
    hr = r // 2
    if kind == "col":
        return ref.at[pl.ds(half * hr, hr), pl.ds(pl.multiple_of(k * cdim, 128), cdim)]
    return ref.at[pl.ds(pl.multiple_of(k * r + half * hr, 2 * SUBLANE), hr), :]


def _target_cores(c, both_cores):
    return [(cc, cc, c) for cc in range(2)] if both_cores else [(c, 0, 0)]


def _gather_weights_start(fulls, kinds, shard_shapes, groups, after, name, both_cores=True):
    n, ng = len(fulls), len(groups)
    per = 2 if both_cores else 1

    def body(*refs):
        ins = refs[:n]
        sems = refs[n + 1:n + 1 + 2 * ng]
        token = refs[2 * n + 1 + 2 * ng]
        x, y, c = _place()
        chips = _other_chips(x, y)
        for g, group in enumerate(groups):
            for pos, (w, j) in enumerate(group):
                own = _weight_window(ins[w], kinds[w], shard_shapes[w], 2 * x + y, c)
                for cc, mine, theirs in _target_cores(c, both_cores):
                    pltpu.make_async_remote_copy(
                        src_ref=own, dst_ref=own,
                        send_sem=sems[2 * g].at[per * pos + mine], recv_sem=sems[2 * g + 1].at[per * pos + theirs],
                        device_id=(*chips[j], cc), device_id_type=MESH).start()
        token[...] = jnp.zeros_like(token)

    sem_shapes = []
    for group in groups:
        sem_shapes += [pltpu.SemaphoreType.DMA((per * len(group),))] * 2
    outs = pl.pallas_call(
        body, name=name,
        out_shape=tuple(sem_shapes) + tuple(pltpu.HBM(f.shape, f.dtype) for f in fulls)
        + (jax.ShapeDtypeStruct((SUBLANE, 128), F32),),
        in_specs=[HBM_SPEC] * n + [ANY_SPEC], out_specs=(SEM_SPEC,) * (2 * ng) + (HBM_SPEC,) * n + (VMEM_SPEC,),
        input_output_aliases={w: 2 * ng + w for w in range(n)},
        compiler_params=pltpu.CompilerParams(has_side_effects=EFFECT),
    )(*[_hbm(f) for f in fulls], after)
    sems = [(outs[2 * g], outs[2 * g + 1]) for g in range(ng)]
    return sems, list(outs[2 * ng:2 * ng + n]), outs[2 * ng + n]


def _gather_weights_wait(sems, group, fulls, kinds, shard_shapes, after, name, both_cores=True):
    n = len(fulls)
    per = 2 if both_cores else 1

    def body(*refs):
        ins = refs[:n]
        send_sems, recv_sems = refs[n], refs[n + 1]
        x, y, c = _place()
        chips = _other_chips(x, y)
        for pos, (w, j) in enumerate(group):
            own = _weight_window(ins[w], kinds[w], shard_shapes[w], 2 * x + y, c)
            for cc, mine, _ in _target_cores(c, both_cores):
                landed = _weight_window(ins[w], kinds[w], shard_shapes[w], 2 * chips[j][0] + chips[j][1], cc)
                k = per * pos + mine
                pltpu.make_async_remote_copy(
                    src_ref=own, dst_ref=own, send_sem=send_sems.at[k], recv_sem=recv_sems.at[k],
                    device_id=(*chips[j], cc), device_id_type=MESH).wait_send()
                pltpu.make_async_remote_copy(
                    src_ref=landed, dst_ref=landed, send_sem=send_sems.at[k], recv_sem=recv_sems.at[k],
                    device_id=(*chips[j], cc), device_id_type=MESH).wait_recv()

    return list(pl.pallas_call(
        body, name=name, out_shape=tuple(pltpu.HBM(f.shape, f.dtype) for f in fulls),
        in_specs=[HBM_SPEC] * n + [SEM_SPEC, SEM_SPEC, ANY_SPEC], out_specs=(HBM_SPEC,) * n,
        input_output_aliases={w: w for w in range(n)},
        compiler_params=pltpu.CompilerParams(has_side_effects=EFFECT),
    )(*fulls, sems[0], sems[1], after))


def _forward_to_sibling(full, kind, shard_shape, relations, name):
    nr = len(relations)

    def body(in_ref, out_ref, send_sems, recv_sems):
        x, y, c = _place()
        chips = _other_chips(x, y)
        copies = []
        for pos, j in enumerate(relations):
            k = 2 * chips[j][0] + chips[j][1]
            mine = _weight_window(in_ref, kind, shard_shape, k, c)
            cp = pltpu.make_async_remote_copy(
                src_ref=mine, dst_ref=_weight_window(out_ref, kind, shard_shape, k, c),
                send_sem=send_sems.at[pos], recv_sem=recv_sems.at[pos], device_id=(x, y, 1 - c), device_id_type=MESH)
            cp.start()
            copies.append(cp)
        for pos, j in enumerate(relations):
            theirs = _weight_window(out_ref, kind, shard_shape, 2 * chips[j][0] + chips[j][1], 1 - c)
            pltpu.make_async_remote_copy(
                src_ref=theirs, dst_ref=theirs, send_sem=send_sems.at[pos], recv_sem=recv_sems.at[pos],
                device_id=(x, y, 1 - c), device_id_type=MESH).wait_recv()
        for cp in copies:
            cp.wait_send()

    return pl.pallas_call(
        body, name=name, out_shape=jax.ShapeDtypeStruct(full.shape, full.dtype),
        in_specs=[HBM_SPEC], out_specs=HBM_SPEC, input_output_aliases={0: 0},
        scratch_shapes=[pltpu.SemaphoreType.DMA((nr,))] * 2,
    )(full)


def _grad_piece(ref, kind, h, cdim, k, half):
    if kind == "col":
        return ref.at[pl.ds(half * h, h), pl.ds(pl.multiple_of(k * cdim, 128), cdim)]
    return ref.at[k, pl.ds(half * h, h), :]


def _grad_dims(g, kind):
    return (g.shape[0] // 2, g.shape[1] // N_CHIP) if kind == "col" else (g.shape[1] // 2, g.shape[2])


def _reduce_start(grads, kinds, name):
    n = len(grads)
    dims = [_grad_dims(g, kind) for g, kind in zip(grads, kinds)]
    lands = [lax.empty((N_DEV - 1, h, cdim), g.dtype) for g, (h, cdim) in zip(grads, dims)]

    def body(*refs):
        g_ins, land_ins = refs[:n], refs[n:2 * n]
        send_sems, recv_sems = refs[2 * n], refs[2 * n + 1]
        token = refs[4 * n + 2]
        x, y, c = _place()
        for w in range(n):
            h, cdim = dims[w]
            base = (N_DEV - 1) * w
            pltpu.make_async_remote_copy(
                src_ref=_grad_piece(g_ins[w], kinds[w], h, cdim, 2 * x + y, 1 - c), dst_ref=land_ins[w].at[0],
                send_sem=send_sems.at[base], recv_sem=recv_sems.at[base],
                device_id=(x, y, 1 - c), device_id_type=MESH).start()
            for j, chip in enumerate(_other_chips(x, y)):
                for cc in range(2):
                    pltpu.make_async_remote_copy(
                        src_ref=_grad_piece(g_ins[w], kinds[w], h, cdim, 2 * chip[0] + chip[1], cc),
                        dst_ref=land_ins[w].at[1 + 2 * j + c],
                        send_sem=send_sems.at[base + 1 + 2 * j + cc], recv_sem=recv_sems.at[base + 1 + 2 * j + c],
                        device_id=(*chip, cc), device_id_type=MESH).start()
        token[...] = jnp.zeros_like(token)

    sems = pltpu.SemaphoreType.DMA(((N_DEV - 1) * n,))
    outs = pl.pallas_call(
        body, name=name,
        out_shape=(sems, sems) + tuple(pltpu.HBM(a.shape, a.dtype) for a in list(grads) + lands)
        + (jax.ShapeDtypeStruct((SUBLANE, 128), F32),),
        in_specs=[HBM_SPEC] * (2 * n), out_specs=(SEM_SPEC, SEM_SPEC) + (HBM_SPEC,) * (2 * n) + (VMEM_SPEC,),
        input_output_aliases={i: 2 + i for i in range(2 * n)},
        compiler_params=pltpu.CompilerParams(has_side_effects=EFFECT),
    )(*[_hbm(a) for a in list(grads) + lands])
    return outs[0], outs[1], list(outs[2:2 + n]), list(outs[2 + n:2 + 2 * n]), outs[2 + 2 * n]


def _reduce_wait(send_sems, recv_sems, grads, lands, kinds, after, name):
    n = len(grads)
    dims = [_grad_dims(g, kind) for g, kind in zip(grads, kinds)]

    def body(*refs):
        g_ins, land_ins = refs[:n], refs[n:2 * n]
        send_sems, recv_sems = refs[2 * n], refs[2 * n + 1]
        x, y, c = _place()
        for w in range(n):
            h, cdim = dims[w]
            piece = _grad_piece(g_ins[w], kinds[w], h, cdim, 2 * x + y, c)
            for s in range(N_DEV - 1):
                k = (N_DEV - 1) * w + s
                slot = land_ins[w].at[s]
                pltpu.make_async_remote_copy(
                    src_ref=piece, dst_ref=slot, send_sem=send_sems.at[k], recv_sem=recv_sems.at[k],
                    device_id=(x, y, 1 - c), device_id_type=MESH).wait_send()
                pltpu.make_async_remote_copy(
                    src_ref=piece, dst_ref=slot, send_sem=send_sems.at[k], recv_sem=recv_sems.at[k],
                    device_id=(x, y, 1 - c), device_id_type=MESH).wait_recv()

    outs = pl.pallas_call(
        body, name=name, out_shape=tuple(pltpu.HBM(a.shape, a.dtype) for a in list(grads) + list(lands)),
        in_specs=[HBM_SPEC] * (2 * n) + [SEM_SPEC, SEM_SPEC, ANY_SPEC], out_specs=(HBM_SPEC,) * (2 * n),
        input_output_aliases={i: i for i in range(2 * n)},
        compiler_params=pltpu.CompilerParams(has_side_effects=EFFECT),
    )(*grads, *lands, send_sems, recv_sems, after)
    return list(outs[:n]), list(outs[n:])


def _add_pieces(g, land, kind, chip_idx, core_idx, name):
    _, h, cdim = land.shape
    tr, tc = _tile(h, 256), _tile(cdim, 2048)
    nrb, ncb = h // tr, cdim // tc

    def body(k_ref, c_ref, g_ref, l_ref, o_ref):
        acc = g_ref[...].astype(F32)
        for s in range(N_DEV - 1):
            acc = acc + l_ref[s].astype(F32)
        o_ref[...] = acc

    if kind == "col":
        g_spec = pl.BlockSpec((tr, tc), lambda i, j, kr, cr: (cr[0] * nrb + i, kr[0] * ncb + j))
    else:
        g_spec = pl.BlockSpec((None, tr, tc), lambda i, j, kr, cr: (kr[0], cr[0] * nrb + i, j))
    return pl.pallas_call(
        body, name=name, out_shape=jax.ShapeDtypeStruct((2 * h, cdim), F32),
        grid_spec=pltpu.PrefetchScalarGridSpec(
            num_scalar_prefetch=2, grid=(nrb, ncb),
            in_specs=[g_spec, pl.BlockSpec((N_DEV - 1, tr, tc), lambda i, j, kr, cr: (0, i, j))],
            out_specs=pl.BlockSpec((tr, tc), lambda i, j, kr, cr: (cr[0] * nrb + i, j))),
        compiler_params=_params("parallel", "parallel"),
    )(chip_idx, core_idx, g, land)


def _join_halves(shards, name):
    n = len(shards)

    def body(*refs):
        ins, outs = refs[:n], refs[n:2 * n]
        send_sems, recv_sems = refs[2 * n:]
        x, y, c = _place()
        copies = []
        for w in range(n):
            h = shards[w].shape[0] // 2
            cp = pltpu.make_async_remote_copy(
                src_ref=ins[w].at[pl.ds(c * h, h), :], dst_ref=outs[w].at[pl.ds(c * h, h), :],
                send_sem=send_sems.at[w], recv_sem=recv_sems.at[w], device_id=(x, y, 1 - c), device_id_type=MESH)
            cp.start()
            copies.append(cp)
        for w in range(n):
            h = shards[w].shape[0] // 2
            theirs = outs[w].at[pl.ds((1 - c) * h, h), :]
            pltpu.make_async_remote_copy(
                src_ref=theirs, dst_ref=theirs, send_sem=send_sems.at[w], recv_sem=recv_sems.at[w],
                device_id=(x, y, 1 - c), device_id_type=MESH).wait_recv()
        for cp in copies:
            cp.wait_send()

    return pl.pallas_call(
        body, name=name, out_shape=tuple(jax.ShapeDtypeStruct(s.shape, s.dtype) for s in shards),
        in_specs=[HBM_SPEC] * n, out_specs=tuple([HBM_SPEC] * n),
        input_output_aliases={w: w for w in range(n)},
        scratch_shapes=[pltpu.SemaphoreType.DMA((n,))] * 2,
    )(*shards)


def _sum_devices(gathered, rows, name):
    n = gathered.shape[1]

    def body(g_ref, o_ref):
        acc = g_ref[0:rows, :]
        for dev in range(1, N_DEV):
            acc = acc + g_ref[dev * rows:(dev + 1) * rows, :]
        o_ref[...] = acc

    return pl.pallas_call(
        body, name=name, out_shape=jax.ShapeDtypeStruct((rows, n), F32),
        in_specs=[VMEM_SPEC], out_specs=VMEM_SPEC,
        compiler_params=pltpu.CompilerParams(vmem_limit_bytes=VMEM_LIMIT),
    )(gathered)


def _pack(vectors, width):
    flat = [v.reshape(-1) for v in vectors]
    offsets, total = [], 0
    for f in flat:
        offsets.append(total)
        total += f.shape[0]
    rows = -(-total // (width * SUBLANE)) * SUBLANE
    flat.append(jnp.zeros((rows * width - total,), F32))
    return jnp.concatenate(flat).reshape(rows, width), offsets


def kernel(x, c, mod_w, mod_b, norm_g, a_w_in, a_conv_w, a_conv_b, a_w_out, b_w_in, b_ln_g, b_ln_b, b_w_s, b_b_s, b_w_out, final_g, loss_target, m_mod_w, m_mod_b, m_norm_g, m_a_w_in, m_a_conv_w, m_a_conv_b, m_a_w_out, m_b_w_in, m_b_ln_g, m_b_ln_b, m_b_w_s, m_b_b_s, m_b_w_out, m_final_g, v_mod_w, v_mod_b, v_norm_g, v_a_w_in, v_a_conv_w, v_a_conv_b, v_a_w_out, v_b_w_in, v_b_ln_g, v_b_ln_b, v_b_w_s, v_b_b_s, v_b_w_out, v_final_g):
    seq, d = x.shape[1], x.shape[2]
    e = a_conv_b.shape[1]
    wd = mod_w.shape[2]
    ax, ay, ac = _place()
    chip = 2 * ax + ay
    dev = 2 * chip + ac
    chip_idx = jnp.reshape(chip, (1,)).astype(jnp.int32)
    core_idx = jnp.reshape(ac, (1,)).astype(jnp.int32)

    x2d = x[0]
    target = loss_target[0]

    w_names = ["a_w_in", "a_w_out", "b_w_in", "b_w_out"]
    w_kinds = ["col", "row", "col", "row"]
    w_shards = [a_w_in[0], a_w_out[0], b_w_in[0], b_w_out[0]]
    w_shapes = [sh.shape for sh in w_shards]
    placed = [_place_own(sh, kind, chip_idx, "place_" + nm) for nm, sh, kind in zip(w_names, w_shards, w_kinds)]
    a_groups, far_groups = [[(0, 0), (0, 1)]], [[(0, 2)]]
    whole_group = [[(0, 0), (0, 1), (0, 2)]]

    es = e // N_CHIP
    packed0, offs0 = _pack([c, a_conv_w, b_ln_g, b_ln_b], 1024)
    gathered0 = _all_gather(packed0, "gather_params").reshape(N_DEV, -1)
    c_all = gathered0[:, :d]
    per_chip = gathered0[0::2]

    def from_chips_cols(k, rows_):
        got = per_chip[:, offs0[k]:offs0[k] + rows_ * es].reshape(N_CHIP, rows_, es)
        return jnp.transpose(got, (1, 0, 2)).reshape(rows_, e)

    conv_w_full = from_chips_cols(1, 3)
    conv_w = [conv_w_full[k:k + 1] for k in range(3)]
    ln_g, ln_b = from_chips_cols(2, 1), from_chips_cols(3, 1)
    mod_b_cols = lax.dynamic_slice_in_dim(mod_b, chip * wd, wd, axis=1)[:, None, :]
    c_act, mod_part = _mod_fwd(c_all, mod_w, mod_b_cols, "mod_fwd")
    n_layer = mod_w.shape[0]
    mod_gathered = _all_gather(mod_part, "gather_mod")
    a_sems, (wa_in,), a_token = _gather_weights_start(
        placed[:1], w_kinds[:1], w_shapes[:1], a_groups, mod_gathered, "gather_a_w_in_start", both_cores=False)
    mod_all = mod_gathered.reshape(N_CHIP, 2, n_layer, N_DEV, wd)[:, 0]
    mod_all = jnp.transpose(mod_all, (1, 2, 0, 3)).reshape(n_layer, N_DEV, N_CHIP * wd)
    mod_me = lax.dynamic_index_in_dim(mod_all, dev, axis=1, keepdims=False)
    shift = [mod_me[l:l + 1, 0:d] for l in range(n_layer)]
    scale = [mod_me[l:l + 1, d:2 * d] for l in range(n_layer)]
    gate = [mod_me[l:l + 1, 2 * d:3 * d] for l in range(n_layer)]

    g0, g1, gf = norm_g[0:1], norm_g[1:2], final_g[None, :]
    h0 = _norm_mod(x2d, g0, scale[0], shift[0], "norm_mod0")

    def slab(r):
        return jnp.bitwise_xor(chip_idx, r)

    def arrived(sems, group, weight, w, after, name, both_cores=True):
        return _gather_weights_wait(sems, [(0, j) for _, j in group], [weight], [w_kinds[w]], [w_shapes[w]],
                                    after, name, both_cores)[0]

    proj0 = _mm_proj_slab(h0, wa_in, None, slab(0), 4, "a_proj_own", a_token)
    wa_in = arrived(a_sems[0], a_groups[0], wa_in, 0, proj0, "gather_wait_near", both_cores=False)
    wa_in = _forward_to_sibling(wa_in, w_kinds[0], w_shapes[0], [0, 1], "forward_near")
    far_sems, (wa_in,), far_token = _gather_weights_start(
        [wa_in], w_kinds[:1], w_shapes[:1], far_groups, a_token, "gather_a_w_in_far_start")
    proj0 = _mm_proj_slab(h0, wa_in, proj0, slab(2), 4, "a_proj_x", far_token)
    proj0 = _mm_proj_slab(h0, wa_in, proj0, slab(1), 4, "a_proj_y", far_token)
    wa_in = arrived(far_sems[0], far_groups[0], wa_in, 0, proj0, "gather_wait_far")
    def start_whole(w, after, name):
        sems, (flight,), token = _gather_weights_start(
            placed[w:w + 1], w_kinds[w:w + 1], w_shapes[w:w + 1], whole_group, after, name)
        return sems[0], flight, token

    ao_sems, wa_out, ao_token = start_whole(1, wa_in, "gather_a_w_out_start")
    proj0 = _mm_proj_slab(h0, wa_in, proj0, slab(3), 4, "a_proj_far", ao_token)
    wa_out = arrived(ao_sems, whole_group[0], wa_out, 1, proj0, "gather_wait_a_w_out")
    bi_sems, wb_in, bi_token = start_whole(2, wa_out, "gather_b_w_in_start")
    y0, br0, x1, h1 = _conv_fwd(proj0, conv_w, a_conv_b, wa_out, x2d, gate[0], g1, scale[1], shift[1],
                                "conv_fwd", bi_token)
    wb_in = arrived(bi_sems, whole_group[0], wb_in, 2, h1, "gather_wait_b_w_in")
    bo_sems, wb_out, bo_token = start_whole(3, wb_in, "gather_b_w_out_start")
    proj1 = _mm_proj(h1, wb_in, 3, "b_proj", bo_token)
    b_s_t = jnp.transpose(b_b_s[0])
    wb_out = arrived(bo_sems, whole_group[0], wb_out, 3, proj1, "gather_wait_b_w_out")
    y1, dx2, dbr1, loss_part, g_final_g, dgate1 = _gmlp_fwd_head(
        proj1, ln_g, ln_b, b_w_s[0], b_s_t, wb_out, x1, gate[1], gf, target, "gmlp_fwd_head")

    gw_b_out = _mm_dw_out(y1, dbr1, "b_out_dw")
    dproj1, g_w_s, g_b_s_t, g_ln_g, g_ln_b = _gmlp_bwd(
        proj1, dbr1, wb_out, ln_g, ln_b, b_w_s[0], jnp.swapaxes(b_w_s[0], 1, 2), b_s_t, "gmlp_bwd")
    gw_b_in = _mm_dw_in(h1, dproj1, "b_proj_dw")
    b_kinds = ["col", "row"]
    b_send, b_recv, b_grads, b_lands, b_token = _reduce_start(
        [gw_b_in, gw_b_out.reshape(N_CHIP, e // N_CHIP, d)], b_kinds, "reduce_b_start")
    dx1, dshift1, dscale1, g_g1, dbr0, dgate0 = _mm_dh_norm_bwd(
        dproj1, wb_in, x1, dx2, g1, scale[1], "b_proj_dx", b_token, br=br0, gate=gate[0])

    early = [loss_part[0, 0:1], g_final_g, g_g1, g_ln_g, g_ln_b, jnp.transpose(g_b_s_t), g_w_s,
             jnp.concatenate([dshift1, dscale1, dgate1, dgate0], axis=1)]
    packed_e, offs_e = _pack(early, 1024)
    rows_e = packed_e.shape[0]
    blocks_e = lax.dynamic_update_slice(jnp.zeros((N_DEV, rows_e, 1024), F32), packed_e[None], (dev, 0, 0))
    e_send, e_recv, blocks_e, e_token = _all_gather_start(blocks_e, "gather_small_start")

    gw_a_out = _mm_dw_out(y0, dbr0, "a_out_dw")
    ao_send, ao_recv, ao_grads, ao_lands, ao_grad_token = _reduce_start(
        [gw_a_out.reshape(N_CHIP, e // N_CHIP, d)], ["row"], "reduce_a_out_start")
    dproj0, g_w0, g_w1, g_w2, g_conv_b = _conv_bwd(
        proj0, dbr0, wa_out, conv_w, a_conv_b, "conv_bwd", ao_grad_token + e_token)
    gw_a_in = _mm_dw_in(h0, dproj0, "a_proj_dw")
    ai_send, ai_recv, ai_grads, ai_lands, ai_token = _reduce_start([gw_a_in], ["col"], "reduce_a_in_start")
    grad_x, dshift0, dscale0, g_g0 = _mm_dh_norm_bwd(dproj0, wa_in, x2d, dx1, g0, scale[0], "a_proj_dx", ai_token)

    def finish(send, recv, grads_, lands_, kinds_, names_, after, tag):
        grads_, lands_ = _reduce_wait(send, recv, grads_, lands_, kinds_, after, "reduce_" + tag + "_wait")
        halves = [_add_pieces(g, land, kind, chip_idx, core_idx, "add_pieces_" + nm)
                  for g, land, kind, nm in zip(grads_, lands_, kinds_, names_)]
        return _join_halves(halves, "join_" + tag)

    upd, big_grads = {}, {}

    def adamw_big(nm, w, g, m, v):
        g_out, *rest = _adamw_2d(w[0], g, m[0], v[0], "adamw_" + nm)
        big_grads[nm] = g_out[None]
        upd[nm] = tuple(o[None] for o in rest)

    g_b_w_in, g_b_w_out = finish(b_send, b_recv, b_grads, b_lands, b_kinds, ["b_w_in", "b_w_out"], grad_x, "b")
    adamw_big("b_w_in", b_w_in, g_b_w_in, m_b_w_in, v_b_w_in)
    adamw_big("b_w_out", b_w_out, g_b_w_out, m_b_w_out, v_b_w_out)

    late = [g_g0, jnp.concatenate([g_w0, g_w1, g_w2], axis=0), g_conv_b, jnp.concatenate([dshift0, dscale0], axis=1)]
    packed_l, offs_l = _pack(late, 1024)
    rows_l = packed_l.shape[0]
    gathered_l = _all_gather(packed_l, "gather_small_late", after=upd["b_w_out"][0])
    blocks_e = _all_gather_wait(e_send, e_recv, blocks_e, gathered_l, "gather_small_wait")
    gathered_e = blocks_e.reshape(N_DEV * rows_e, 1024)
    summed_e = _sum_devices(gathered_e, rows_e, "sum_small_early").reshape(-1)
    summed_l = _sum_devices(gathered_l, rows_l, "sum_small_late").reshape(-1)

    def take(summed, offs, k, shape):
        size = math.prod(shape)
        return summed[offs[k]:offs[k] + size].reshape(shape)

    def rows_of(gathered, rows_, offs, k, width):
        return gathered.reshape(N_DEV, rows_ * 1024)[:, offs[k]:offs[k] + width]

    loss = take(summed_e, offs_e, 0, ())
    grad_final_g = take(summed_e, offs_e, 1, (d,))
    grad_norm_g = jnp.concatenate([take(summed_l, offs_l, 0, (1, d)), take(summed_e, offs_e, 2, (1, d))], axis=0)
    grad_ln_g_full = take(summed_e, offs_e, 3, (1, e))
    grad_ln_b_full = take(summed_e, offs_e, 4, (1, e))
    grad_b_b_s = take(summed_e, offs_e, 5, (1, GROUPS, CHUNK))
    grad_b_w_s = take(summed_e, offs_e, 6, (1, GROUPS, CHUNK, CHUNK))
    grad_conv_w_full = take(summed_l, offs_l, 1, (3, e))
    grad_a_conv_b = take(summed_l, offs_l, 2, (1, e))
    grad_a_conv_w = lax.dynamic_slice_in_dim(grad_conv_w_full, chip * es, es, axis=1)[None]
    grad_b_ln_g = lax.dynamic_slice_in_dim(grad_ln_g_full, chip * es, es, axis=1)
    grad_b_ln_b = lax.dynamic_slice_in_dim(grad_ln_b_full, chip * es, es, axis=1)
    dmod_e = rows_of(gathered_e, rows_e, offs_e, 7, 4 * d)
    dmod_l = rows_of(gathered_l, rows_l, offs_l, 3, 2 * d)
    dmod_all = jnp.stack([jnp.concatenate([dmod_l, dmod_e[:, 3 * d:]], axis=1), dmod_e[:, :3 * d]], axis=1)
    mod_b_e = take(summed_e, offs_e, 7, (4 * d,))
    mod_b_l = take(summed_l, offs_l, 3, (2 * d,))
    grad_mod_b = jnp.stack([jnp.concatenate([mod_b_l, mod_b_e[3 * d:]]), mod_b_e[:3 * d]])
    dmod_cols = jnp.transpose(lax.dynamic_slice_in_dim(dmod_all, chip * wd, wd, axis=2), (1, 0, 2))

    grad_mod_w, delta_mod_w, new_m_mod_w, new_v_mod_w = _mod_w_update(
        jnp.transpose(c_act), dmod_cols, mod_w, m_mod_w, v_mod_w, "mod_w_update")
    (g_a_w_out,) = finish(ao_send, ao_recv, ao_grads, ao_lands, ["row"], ["a_w_out"], delta_mod_w, "a_out")
    adamw_big("a_w_out", a_w_out, g_a_w_out, m_a_w_out, v_a_w_out)
    (g_a_w_in,) = finish(ai_send, ai_recv, ai_grads, ai_lands, ["col"], ["a_w_in"], upd["a_w_out"][0], "a_in")
    adamw_big("a_w_in", a_w_in, g_a_w_in, m_a_w_in, v_a_w_in)
    small_w = [("mod_b", mod_b, grad_mod_b, m_mod_b, v_mod_b), ("norm_g", norm_g, grad_norm_g, m_norm_g, v_norm_g),
               ("a_conv_w", a_conv_w, grad_a_conv_w, m_a_conv_w, v_a_conv_w),
               ("a_conv_b", a_conv_b, grad_a_conv_b, m_a_conv_b, v_a_conv_b),
               ("b_ln_g", b_ln_g, grad_b_ln_g, m_b_ln_g, v_b_ln_g), ("b_ln_b", b_ln_b, grad_b_ln_b, m_b_ln_b, v_b_ln_b),
               ("b_w_s", b_w_s, grad_b_w_s, m_b_w_s, v_b_w_s), ("b_b_s", b_b_s, grad_b_b_s, m_b_b_s, v_b_b_s),
               ("final_g", final_g, grad_final_g, m_final_g, v_final_g)]

    def flat2d(a):
        return a.reshape(-1, a.shape[-1])

    res = _adamw_small([flat2d(t[1]) for t in small_w], [flat2d(t[2]) for t in small_w],
                       [flat2d(t[3]) for t in small_w], [flat2d(t[4]) for t in small_w], "adamw_small")
    for (nm, w, _, _, _), r3 in zip(small_w, res):
        upd[nm] = tuple(o.reshape(w.shape) for o in r3)
    upd["mod_w"] = (delta_mod_w, new_m_mod_w, new_v_mod_w)

    grads = {"mod_w": grad_mod_w, "mod_b": grad_mod_b, "norm_g": grad_norm_g, "a_conv_w": grad_a_conv_w,
             "a_conv_b": grad_a_conv_b, "b_ln_g": grad_b_ln_g, "b_ln_b": grad_b_ln_b, "b_w_s": grad_b_w_s,
             "b_b_s": grad_b_b_s, "final_g": grad_final_g, **big_grads}
    order = ["mod_w", "mod_b", "norm_g", "a_w_in", "a_conv_w", "a_conv_b", "a_w_out", "b_w_in", "b_ln_g", "b_ln_b",
             "b_w_s", "b_b_s", "b_w_out", "final_g"]
    return (loss, grad_x[None], *[grads[k] for k in order], *[upd[k][0] for k in order],
            *[upd[k][1] for k in order], *[upd[k][2] for k in order])
```

```python
import math

import jax
import jax.numpy as jnp
from jax import lax
from jax.experimental import pallas as pl
from jax.experimental.pallas import tpu as pltpu

F32 = jnp.float32
BF16 = jnp.bfloat16
MESH = pl.DeviceIdType.MESH

N_DEV = 8
N_CHIP = 4
SUBLANE = 8
PACK = 16
ACT = F32
RMS_EPS = 1e-6
LN_EPS = 1e-5
CHUNK = 128
GROUPS = 8
ADAM_LR = 0.001
ADAM_B1 = 0.9
ADAM_B2 = 0.999
ADAM_EPS = 1e-08
ADAM_WD = 0.01
ADAM_STEP = 10
VMEM_LIMIT = 56 << 20

HBM_SPEC = pl.BlockSpec(memory_space=pltpu.HBM)
VMEM_SPEC = pl.BlockSpec(memory_space=pltpu.VMEM)
SEM_SPEC = pl.BlockSpec(memory_space=pltpu.SEMAPHORE)
ANY_SPEC = pl.BlockSpec(memory_space=pl.ANY)
EFFECT = pltpu.SideEffectType.DATAFLOW_SIDE_EFFECTING


def _params(*sem):
    return pltpu.CompilerParams(dimension_semantics=sem, vmem_limit_bytes=VMEM_LIMIT)


def _tile(n, want):
    if n <= want:
        return n
    t = want
    while n % t:
        t -= 128
    return t


def _sigmoid(x):
    return 0.5 * jnp.tanh(0.5 * x) + 0.5


def _silu_and_grad(x):
    s = _sigmoid(x)
    return x * s, s * (1.0 + x * (1.0 - s))


def _gelu_and_grad(x):
    cdf = 0.5 * (1.0 + lax.erf(x * (1.0 / math.sqrt(2.0))))
    pdf = jnp.exp(-0.5 * x * x) * (1.0 / math.sqrt(2.0 * math.pi))
    return x * cdf, cdf + x * pdf


def _gelu(x):
    return x * (0.5 * (1.0 + lax.erf(x * (1.0 / math.sqrt(2.0)))))


def _rms(x):
    r = lax.rsqrt(jnp.mean(x * x, axis=-1, keepdims=True) + RMS_EPS)
    return x * r, r


def _rms_bwd(dxn, xn, r):
    return r * (dxn - xn * jnp.mean(dxn * xn, axis=-1, keepdims=True))


def _colsum(a):
    return jnp.sum(a, axis=0, keepdims=True)


def _shift_down(cur, before, k):
    rolled = pltpu.roll(cur, k, 0)
    row = lax.broadcasted_iota(jnp.int32, before.shape, 0)
    head = jnp.where(row < k, pltpu.roll(before, k, 0), rolled[:SUBLANE])
    return jnp.concatenate([head, rolled[SUBLANE:]], axis=0)


def _shift_up(cur, after, k):
    n = cur.shape[0]
    rolled = pltpu.roll(cur, n - k, 0)
    row = lax.broadcasted_iota(jnp.int32, after.shape, 0)
    tail = jnp.where(row >= SUBLANE - k, pltpu.roll(after, SUBLANE - k, 0), rolled[n - SUBLANE:])
    return jnp.concatenate([rolled[:n - SUBLANE], tail], axis=0)


def _after_spec():
    return pl.BlockSpec((SUBLANE, 128), lambda *_: (0, 0))


def _mm_proj(h, w, n_split, name, after):
    s, d = h.shape
    e = w.shape[1] // n_split
    tm, tn = _tile(s, 2048), _tile(e, 2048)
    nj = e // tn

    def body(h_ref, w_ref, after_ref, o_ref):
        o_ref[...] = jnp.dot(h_ref[...], w_ref[...], preferred_element_type=F32).astype(ACT)

    return pl.pallas_call(
        body, name=name,
        out_shape=jax.ShapeDtypeStruct((n_split, s, e), ACT),
        grid=(s // tm, n_split * nj),
        in_specs=[pl.BlockSpec((tm, d), lambda i, j: (i, 0)), pl.BlockSpec((d, tn), lambda i, j: (0, j)),
                  _after_spec()],
        out_specs=pl.BlockSpec((None, tm, tn), lambda i, j: (j // nj, i, j % nj)),
        compiler_params=_params("parallel", "parallel"),
    )(h, w, after)


def _mm_proj_slab(h, w, proj, q_idx, n_split, name, after):
    s, d = h.shape
    e = w.shape[1] // n_split
    tm, tn = _tile(s, 1024), _tile(e, 2048)
    nj = e // tn

    def body(q_ref, h_ref, w_ref, *rest):
        o_ref = rest[-1]
        o_ref[...] = jnp.dot(h_ref[...], w_ref[...], preferred_element_type=F32).astype(ACT)

    in_specs = [pl.BlockSpec((tm, d), lambda i, j, qr: (i, 0)), pl.BlockSpec((d, tn), lambda i, j, qr: (0, qr[0] * nj + j)),
                _after_spec()]
    args = [q_idx, h, w, after]
    aliases = {}
    if proj is not None:
        in_specs.append(ANY_SPEC)
        args.append(proj)
        aliases = {4: 0}
    return pl.pallas_call(
        body, name=name,
        out_shape=jax.ShapeDtypeStruct((n_split, s, e), ACT),
        grid_spec=pltpu.PrefetchScalarGridSpec(
            num_scalar_prefetch=1, grid=(s // tm, nj), in_specs=in_specs,
            out_specs=pl.BlockSpec((None, tm, tn), lambda i, j, qr: (qr[0], i, j))),
        input_output_aliases=aliases,
        compiler_params=_params("parallel", "parallel"),
    )(*args)


def _mm_dh_norm_bwd(dp, w, x, dx_in, g, scale, name, after, br=None, gate=None):
    nq, s, e = dp.shape
    d = w.shape[0]
    has_branch = br is not None
    tm, tk = _tile(s, 1024), _tile(e, 1024 if has_branch else 2048)
    nkq = e // tk
    nk = nq * nkq

    def body(*refs):
        if has_branch:
            (a_ref, b_ref, after_ref, x_ref, dxin_ref, g_ref, sc_ref, br_ref, gate_ref,
             dx_ref, dsh_ref, dsc_ref, dg_ref, dbr_ref, dgate_ref, acc_ref) = refs
        else:
            (a_ref, b_ref, after_ref, x_ref, dxin_ref, g_ref, sc_ref,
             dx_ref, dsh_ref, dsc_ref, dg_ref, acc_ref) = refs
        i, k = pl.program_id(0), pl.program_id(1)

        @pl.when(jnp.logical_and(i == 0, k == 0))
        def _():
            dsh_ref[...] = jnp.zeros_like(dsh_ref)
            dsc_ref[...] = jnp.zeros_like(dsc_ref)
            dg_ref[...] = jnp.zeros_like(dg_ref)
            if has_branch:
                dgate_ref[...] = jnp.zeros_like(dgate_ref)

        @pl.when(k == 0)
        def _():
            acc_ref[...] = jnp.zeros_like(acc_ref)
        acc_ref[...] += lax.dot_general(a_ref[...], b_ref[...], (((1,), (1,)), ((), ())), preferred_element_type=F32)

        @pl.when(k == nk - 1)
        def _():
            dh = acc_ref[...]
            g_vec = g_ref[...]
            xn, r = _rms(x_ref[...])
            dsh_ref[...] += _colsum(dh)
            dsc_ref[...] += _colsum(dh * (xn * g_vec))
            da = dh * (1.0 + sc_ref[...])
            dg_ref[...] += _colsum(da * xn)
            dx = dxin_ref[...] + _rms_bwd(da * g_vec, xn, r)
            dx_ref[...] = dx
            if has_branch:
                dgate_ref[...] += _colsum(dx * br_ref[...])
                dbr_ref[...] = (gate_ref[...] * dx).astype(BF16)

    rows = pl.BlockSpec((tm, d), lambda i, k: (i, 0))
    vec = pl.BlockSpec((1, d), lambda i, k: (0, 0))
    vec_out = jax.ShapeDtypeStruct((1, d), F32)
    in_specs = [pl.BlockSpec((None, tm, tk), lambda i, k: (k // nkq, i, k % nkq)),
                pl.BlockSpec((d, tk), lambda i, k: (0, k)), _after_spec(), rows, rows, vec, vec]
    out_shape = [jax.ShapeDtypeStruct((s, d), F32), vec_out, vec_out, vec_out]
    out_specs = [rows, vec, vec, vec]
    args = [dp, w, after, x, dx_in, g, scale]
    if has_branch:
        in_specs += [rows, vec]
        out_shape += [jax.ShapeDtypeStruct((s, d), BF16), vec_out]
        out_specs += [rows, vec]
        args += [br, gate]
    return pl.pallas_call(
        body, name=name, out_shape=tuple(out_shape), grid=(s // tm, nk),
        in_specs=in_specs, out_specs=tuple(out_specs), scratch_shapes=[pltpu.VMEM((tm, d), F32)],
        compiler_params=_params("arbitrary", "arbitrary"),
    )(*args)


def _dw_body(n_t):
    def body(a_ref, b_ref, o_ref, acc_ref):
        t = pl.program_id(2)

        @pl.when(t == 0)
        def _():
            acc_ref[...] = jnp.zeros_like(acc_ref)
        acc_ref[...] += lax.dot_general(a_ref[...], b_ref[...], (((0,), (0,)), ((), ())), preferred_element_type=F32)

        @pl.when(t == n_t - 1)
        def _():
            o_ref[...] = acc_ref[...].astype(o_ref.dtype)
    return body


def _mm_dw_in(h, dp, name):
    s, d = h.shape
    nq, _, e = dp.shape
    tm, tn, tt = _tile(d, 1024), _tile(e, 2048), _tile(s, 2048)
    nj = e // tn

    return pl.pallas_call(
        _dw_body(s // tt), name=name,
        out_shape=jax.ShapeDtypeStruct((d, nq * e), BF16),
        grid=(d // tm, nq * nj, s // tt),
        in_specs=[pl.BlockSpec((tt, tm), lambda i, j, t: (t, i)),
                  pl.BlockSpec((None, tt, tn), lambda i, j, t: (j // nj, t, j % nj))],
        out_specs=pl.BlockSpec((tm, tn), lambda i, j, t: (i, j)),
        scratch_shapes=[pltpu.VMEM((tm, tn), F32)],
        compiler_params=_params("parallel", "parallel", "arbitrary"),
    )(h, dp)


def _mm_dw_out(y, dbr, name):
    s, e = y.shape
    d = dbr.shape[1]
    tm, tn, tt = _tile(e, 1024), _tile(d, 1024), _tile(s, 2048)

    return pl.pallas_call(
        _dw_body(s // tt), name=name,
        out_shape=jax.ShapeDtypeStruct((e, d), BF16),
        grid=(e // tm, d // tn, s // tt),
        in_specs=[pl.BlockSpec((tt, tm), lambda i, j, t: (t, i)), pl.BlockSpec((tt, tn), lambda i, j, t: (t, j))],
        out_specs=pl.BlockSpec((tm, tn), lambda i, j, t: (i, j)),
        scratch_shapes=[pltpu.VMEM((tm, tn), F32)],
        compiler_params=_params("parallel", "parallel", "arbitrary"),
    )(y, dbr)


def _row_spec(ts, d):
    return pl.BlockSpec((ts, d), lambda i: (i, 0))


def _vec_spec(d):
    return pl.BlockSpec((1, d), lambda i: (0, 0))


def _norm_mod(x, g, scale, shift, name):
    s, d = x.shape
    ts = _tile(s, 512)

    def body(x_ref, g_ref, sc_ref, sh_ref, h_ref):
        xn, _ = _rms(x_ref[...])
        h_ref[...] = ((xn * g_ref[...]) * (1.0 + sc_ref[...]) + sh_ref[...]).astype(BF16)

    return pl.pallas_call(
        body, name=name, out_shape=jax.ShapeDtypeStruct((s, d), BF16), grid=(s // ts,),
        in_specs=[_row_spec(ts, d), _vec_spec(d), _vec_spec(d), _vec_spec(d)],
        out_specs=_row_spec(ts, d), compiler_params=_params("parallel"),
    )(x, g, scale, shift)


def _head_tile(x1, br, gate, gf, target):
    d = x1.shape[-1]
    xn, r = _rms(x1 + gate * br)
    err = xn * gf - target
    loss = 0.5 * jnp.sum(jnp.mean(err * err, axis=-1, keepdims=True))
    dout = err * (1.0 / d)
    dx = _rms_bwd(dout * gf, xn, r)
    return dx, gate * dx, loss, _colsum(dout * xn), _colsum(dx * br)


def _f32(ref):
    return ref[...].astype(F32)


def _cols_f32(ref, cols):
    return ref[:, cols].astype(F32)


def _rows_before(halo_ref, cols=slice(None)):
    return _cols_f32(halo_ref, cols)[PACK - SUBLANE:]


def _rows_after(halo_ref, cols=slice(None)):
    return _cols_f32(halo_ref, cols)[:SUBLANE]


CONV_CHAIN_COLS = 512


def _conv_fwd(proj, conv_w, conv_b, w_out, x, gate, g_next, scale_next, shift_next, name, after):
    _, s, e = proj.shape
    d = w_out.shape[1]
    ts, te = _tile(s, 256), e
    cw = min(CONV_CHAIN_COLS, te)
    hb = ts // PACK

    def body(bg_ref, cg_ref, xi_ref, z_ref, cgp_ref, xip_ref, w0_ref, w1_ref, w2_ref, b_ref, wo_ref, after_ref,
             x_ref, gate_ref, g_ref, sc_ref, sh_ref, y_ref, br_ref, x1_ref, h_ref):
        first = pl.program_id(0) == 0
        br = None
        for c0 in range(0, te, cw):
            cols = slice(c0, c0 + cw)
            cx = _cols_f32(cg_ref, cols) * _cols_f32(xi_ref, cols)
            before = jnp.where(first, 0.0, _rows_before(cgp_ref, cols) * _rows_before(xip_ref, cols))
            conv = b_ref[:, cols] + w2_ref[:, cols] * cx
            conv = conv + w0_ref[:, cols] * _shift_down(cx, before, 2)
            conv = conv + w1_ref[:, cols] * _shift_down(cx, before, 1)
            z = _cols_f32(z_ref, cols)
            y = ((z * _sigmoid(z)) * _cols_f32(bg_ref, cols) * conv).astype(BF16)
            y_ref[:, cols] = y
            part_br = jnp.dot(y, wo_ref[cols, :], preferred_element_type=F32)
            br = part_br if br is None else br + part_br
        br_ref[...] = br
        x1 = x_ref[...] + gate_ref[...] * br
        x1_ref[...] = x1
        xn, _ = _rms(x1)
        h_ref[...] = ((xn * g_ref[...]) * (1.0 + sc_ref[...]) + sh_ref[...]).astype(BF16)

    def part(q):
        return pl.BlockSpec((None, ts, te), lambda i: (q, i, 0))

    def halo_before(q):
        return pl.BlockSpec((None, PACK, te), lambda i: (q, jnp.maximum(i * hb - 1, 0), 0))

    return pl.pallas_call(
        body, name=name,
        out_shape=(jax.ShapeDtypeStruct((s, e), BF16), jax.ShapeDtypeStruct((s, d), F32),
                   jax.ShapeDtypeStruct((s, d), F32), jax.ShapeDtypeStruct((s, d), BF16)), grid=(s // ts,),
        in_specs=[part(0), part(1), part(2), part(3), halo_before(1), halo_before(2)]
        + [_vec_spec(e)] * 4 + [pl.BlockSpec((e, d), lambda i: (0, 0)), _after_spec(), _row_spec(ts, d)]
        + [_vec_spec(d)] * 4,
        out_specs=(_row_spec(ts, e), _row_spec(ts, d), _row_spec(ts, d), _row_spec(ts, d)),
        compiler_params=_params("parallel"),
    )(proj, proj, proj, proj, proj, proj, *conv_w, conv_b, w_out, after, x, gate, g_next, scale_next, shift_next)


def _conv_bwd(proj, dbr, w_out, conv_w, conv_b, name, after):
    _, s, e = proj.shape
    d = dbr.shape[1]
    ts, te = _tile(s, 512), _tile(e, 1024)
    cw = min(CONV_CHAIN_COLS, te)
    hb = ts // PACK
    n_i = s // ts
    last_halo = s // PACK - 1
    nt = (((1,), (1,)), ((), ()))

    def body(bg_ref, cg_ref, xi_ref, z_ref, dbr_ref, cgp_ref, xip_ref, bgn_ref, zn_ref, dbrn_ref, wo_ref,
             w0_ref, w1_ref, w2_ref, b_ref, after_ref, dp_ref, dw0_ref, dw1_ref, dw2_ref, db_ref):
        i = pl.program_id(1)

        @pl.when(i == 0)
        def _():
            for acc in (dw0_ref, dw1_ref, dw2_ref, db_ref):
                acc[...] = jnp.zeros_like(acc)
        for c0 in range(0, te, cw):
            cols = slice(c0, c0 + cw)
            wo = wo_ref[cols, :]
            dy = lax.dot_general(dbr_ref[...], wo, nt, preferred_element_type=F32)
            dyn = lax.dot_general(dbrn_ref[...], wo, nt, preferred_element_type=F32)[:SUBLANE]
            bg, cg = _cols_f32(bg_ref, cols), _cols_f32(cg_ref, cols)
            xi, z = _cols_f32(xi_ref, cols), _cols_f32(z_ref, cols)
            w0, w1, w2 = w0_ref[:, cols], w1_ref[:, cols], w2_ref[:, cols]
            cx = cg * xi
            before = jnp.where(i > 0, _rows_before(cgp_ref, cols) * _rows_before(xip_ref, cols), 0.0)
            cx1 = _shift_down(cx, before, 1)
            cx2 = _shift_down(cx, before, 2)
            conv = b_ref[:, cols] + w2 * cx
            conv = conv + w0 * cx2
            conv = conv + w1 * cx1
            sz, dsz = _silu_and_grad(z)
            dp_ref[3, :, cols] = (dy * bg * conv * dsz).astype(BF16)
            dp_ref[0, :, cols] = (dy * sz * conv).astype(BF16)
            dconv = dy * sz * bg
            zn = _rows_after(zn_ref, cols)
            after = jnp.where(i < n_i - 1, dyn * (zn * _sigmoid(zn)) * _rows_after(bgn_ref, cols), 0.0)
            db_ref[:, cols] += _colsum(dconv)
            dw2_ref[:, cols] += _colsum(dconv * cx)
            dw1_ref[:, cols] += _colsum(dconv * cx1)
            dw0_ref[:, cols] += _colsum(dconv * cx2)
            dcx = w2 * dconv + w1 * _shift_up(dconv, after, 1) + w0 * _shift_up(dconv, after, 2)
            dp_ref[1, :, cols] = (dcx * xi).astype(BF16)
            dp_ref[2, :, cols] = (dcx * cg).astype(BF16)

    def part(q):
        return pl.BlockSpec((None, ts, te), lambda j, i: (q, i, j))

    def halo_before(q):
        return pl.BlockSpec((None, PACK, te), lambda j, i: (q, jnp.maximum(i * hb - 1, 0), j))

    def halo_after(q):
        return pl.BlockSpec((None, PACK, te), lambda j, i: (q, jnp.minimum((i + 1) * hb, last_halo), j))

    return pl.pallas_call(
        body, name=name,
        out_shape=(jax.ShapeDtypeStruct((4, s, e), BF16),) + (jax.ShapeDtypeStruct((1, e), F32),) * 4,
        grid=(e // te, n_i),
        in_specs=[part(0), part(1), part(2), part(3), pl.BlockSpec((ts, d), lambda j, i: (i, 0)),
                  halo_before(1), halo_before(2), halo_after(0), halo_after(3),
                  pl.BlockSpec((PACK, d), lambda j, i: (jnp.minimum((i + 1) * hb, last_halo), 0)),
                  pl.BlockSpec((te, d), lambda j, i: (j, 0))]
        + [pl.BlockSpec((1, te), lambda j, i: (0, j))] * 4 + [_after_spec()],
        out_specs=(pl.BlockSpec((4, ts, te), lambda j, i: (0, i, j)),) + (pl.BlockSpec((1, te), lambda j, i: (0, j)),) * 4,
        compiler_params=_params("parallel", "arbitrary"),
    )(proj, proj, proj, proj, dbr, proj, proj, proj, proj, dbr, w_out, *conv_w, conv_b, after)


def _tril(w):
    row = lax.broadcasted_iota(jnp.int32, w.shape, 0)
    col = lax.broadcasted_iota(jnp.int32, w.shape, 1)
    return jnp.where(row >= col, w, 0.0)


def _triu(w):
    row = lax.broadcasted_iota(jnp.int32, w.shape, 0)
    col = lax.broadcasted_iota(jnp.int32, w.shape, 1)
    return jnp.where(row <= col, w, 0.0)


def _layer_norm_fwd(v, g, b):
    mu = jnp.mean(v, axis=-1, keepdims=True)
    vc = v - mu
    rstd = lax.rsqrt(jnp.mean(vc * vc, axis=-1, keepdims=True) + LN_EPS)
    vhat = vc * rstd
    return vhat * g + b, vhat, rstd


GMLP_CHUNKS_PER_STEP = 2


def _gmlp_rows(s):
    return CHUNK * min(GMLP_CHUNKS_PER_STEP, s // CHUNK)


def _mix_positions(w_ref, src_scr, dst_scr, gw, mask, bias_ref=None):
    for gi in range(GROUPS):
        cols = slice(gi * gw, (gi + 1) * gw)
        wm = mask(w_ref[gi]).astype(BF16)
        for n in range(src_scr.shape[0] // CHUNK):
            rows = slice(n * CHUNK, (n + 1) * CHUNK)
            out = jnp.dot(wm, src_scr[rows, cols], preferred_element_type=F32)
            if bias_ref is not None:
                out = out + bias_ref[:, gi:gi + 1]
            dst_scr[rows, cols] = out


def _gmlp_fwd_head(proj, ln_g, ln_b, w_s, b_s_t, w_out, x1, gate, gf, target, name):
    _, s, e = proj.shape
    d = w_out.shape[1]
    gw = e // GROUPS
    ts = _gmlp_rows(s)

    def body(pu_ref, pv_ref, pz_ref, g_ref, b_ref, ws_ref, bs_ref, wo_ref, x1_ref, gate_ref, gf_ref, tg_ref,
             y_ref, dx_ref, dbr_ref, loss_ref, dgf_ref, dgate_ref, vn_scr, mix_scr):
        @pl.when(pl.program_id(0) == 0)
        def _():
            loss_ref[...] = jnp.zeros_like(loss_ref)
            dgf_ref[...] = jnp.zeros_like(dgf_ref)
            dgate_ref[...] = jnp.zeros_like(dgate_ref)
        vn, _, _ = _layer_norm_fwd(_gelu(_f32(pv_ref)), g_ref[...], b_ref[...])
        vn_scr[...] = vn.astype(BF16)
        _mix_positions(ws_ref, vn_scr, mix_scr, gw, _tril, bs_ref)
        z = _f32(pz_ref)
        y = ((z * _sigmoid(z)) * (_gelu(_f32(pu_ref)) * mix_scr[...])).astype(BF16)
        y_ref[...] = y
        br = jnp.dot(y, wo_ref[...], preferred_element_type=F32)
        dx, dbr, loss, dgf, dgate = _head_tile(x1_ref[...], br, gate_ref[...], gf_ref[...], tg_ref[...])
        dx_ref[...] = dx
        dbr_ref[...] = dbr.astype(BF16)
        loss_ref[...] += loss
        dgf_ref[...] += dgf
        dgate_ref[...] += dgate

    def part(q):
        return pl.BlockSpec((None, ts, e), lambda i: (q, i, 0))

    return pl.pallas_call(
        body, name=name,
        out_shape=(jax.ShapeDtypeStruct((s, e), BF16), jax.ShapeDtypeStruct((s, d), F32),
                   jax.ShapeDtypeStruct((s, d), BF16), jax.ShapeDtypeStruct((SUBLANE, 128), F32),
                   jax.ShapeDtypeStruct((1, d), F32), jax.ShapeDtypeStruct((1, d), F32)),
        grid=(s // ts,),
        in_specs=[part(0), part(1), part(2), _vec_spec(e), _vec_spec(e),
                  pl.BlockSpec((GROUPS, CHUNK, CHUNK), lambda i: (0, 0, 0)),
                  pl.BlockSpec((CHUNK, GROUPS), lambda i: (0, 0)), pl.BlockSpec((e, d), lambda i: (0, 0)),
                  _row_spec(ts, d), _vec_spec(d), _vec_spec(d), _row_spec(ts, d)],
        out_specs=(_row_spec(ts, e), _row_spec(ts, d), _row_spec(ts, d),
                   pl.BlockSpec((SUBLANE, 128), lambda i: (0, 0)), _vec_spec(d), _vec_spec(d)),
        scratch_shapes=[pltpu.VMEM((ts, e), BF16), pltpu.VMEM((ts, e), F32)],
        compiler_params=_params("arbitrary"),
    )(proj, proj, proj, ln_g, ln_b, w_s, b_s_t, w_out, x1, gate, gf, target)


def _gmlp_bwd(proj, dbr, w_out, ln_g, ln_b, w_s, w_s_t, b_s_t, name):
    _, s, e = proj.shape
    d = dbr.shape[1]
    gw = e // GROUPS
    ts = _gmlp_rows(s)
    n_i = s // ts

    def body(pu_ref, pv_ref, pz_ref, dbr_ref, wo_ref, g_ref, b_ref, ws_ref, wst_ref, bs_ref,
             dp_ref, dws_ref, dbs_ref, dlg_ref, dlb_ref, vn_scr, mix_scr, dm_scr, dvn_scr, dmacc_scr):
        i = pl.program_id(0)

        @pl.when(i == 0)
        def _():
            dws_ref[...] = jnp.zeros_like(dws_ref)
            dlg_ref[...] = jnp.zeros_like(dlg_ref)
            dlb_ref[...] = jnp.zeros_like(dlb_ref)
            dmacc_scr[...] = jnp.zeros_like(dmacc_scr)
        ln_g = g_ref[...]
        u, du_dpu = _gelu_and_grad(_f32(pu_ref))
        v, dv_dpv = _gelu_and_grad(_f32(pv_ref))
        vn, vhat, rstd = _layer_norm_fwd(v, ln_g, b_ref[...])
        vn_scr[...] = vn.astype(BF16)
        _mix_positions(ws_ref, vn_scr, mix_scr, gw, _tril, bs_ref)
        mixed = mix_scr[...]
        dy = lax.dot_general(dbr_ref[...], wo_ref[...], (((1,), (1,)), ((), ())), preferred_element_type=F32)
        sz, dsz = _silu_and_grad(_f32(pz_ref))
        ds = dy * sz
        dp_ref[2] = (dy * (u * mixed) * dsz).astype(BF16)
        dp_ref[0] = (ds * mixed * du_dpu).astype(BF16)
        dm = ds * u
        dm_scr[...] = dm.astype(BF16)
        for n in range(ts // CHUNK):
            rows = slice(n * CHUNK, (n + 1) * CHUNK)
            dmacc_scr[...] += dm[rows]
            for gi in range(GROUPS):
                cols = slice(gi * gw, (gi + 1) * gw)
                dws_ref[gi] += lax.dot_general(dm_scr[rows, cols], vn_scr[rows, cols], (((1,), (1,)), ((), ())),
                                               preferred_element_type=F32)
        _mix_positions(wst_ref, dm_scr, dvn_scr, gw, _triu)
        dvn = dvn_scr[...]
        dlg_ref[...] += _colsum(dvn * vhat)
        dlb_ref[...] += _colsum(dvn)
        dvh = dvn * ln_g
        dv = rstd * (dvh - jnp.mean(dvh, axis=-1, keepdims=True) - vhat * jnp.mean(dvh * vhat, axis=-1, keepdims=True))
        dp_ref[1] = (dv * dv_dpv).astype(BF16)

        @pl.when(i == n_i - 1)
        def _():
            for gi in range(GROUPS):
                dws_ref[gi] = _tril(dws_ref[gi])
                dbs_ref[:, gi:gi + 1] = jnp.sum(dmacc_scr[:, gi * gw:(gi + 1) * gw], axis=1, keepdims=True)

    def part(q):
        return pl.BlockSpec((None, ts, e), lambda i: (q, i, 0))

    w_spec = pl.BlockSpec((GROUPS, CHUNK, CHUNK), lambda i: (0, 0, 0))
    bs_spec = pl.BlockSpec((CHUNK, GROUPS), lambda i: (0, 0))
    return pl.pallas_call(
        body, name=name,
        out_shape=(jax.ShapeDtypeStruct((3, s, e), BF16), jax.ShapeDtypeStruct((GROUPS, CHUNK, CHUNK), F32),
                   jax.ShapeDtypeStruct((CHUNK, GROUPS), F32), jax.ShapeDtypeStruct((1, e), F32),
                   jax.ShapeDtypeStruct((1, e), F32)),
        grid=(n_i,),
        in_specs=[part(0), part(1), part(2), pl.BlockSpec((ts, d), lambda i: (i, 0)),
                  pl.BlockSpec((e, d), lambda i: (0, 0)), _vec_spec(e), _vec_spec(e), w_spec, w_spec, bs_spec],
        out_specs=(pl.BlockSpec((3, ts, e), lambda i: (0, i, 0)), w_spec, bs_spec, _vec_spec(e), _vec_spec(e)),
        scratch_shapes=[pltpu.VMEM((ts, e), BF16), pltpu.VMEM((ts, e), F32), pltpu.VMEM((ts, e), BF16),
                        pltpu.VMEM((ts, e), F32), pltpu.VMEM((CHUNK, e), F32)],
        compiler_params=_params("arbitrary"),
    )(proj, proj, proj, dbr, w_out, ln_g, ln_b, w_s, w_s_t, b_s_t)


def _mod_fwd(c_all, mod_w, mod_b_cols, name):
    n_layer, d, w = mod_w.shape

    def body(c_ref, w_ref, b_ref, ca_ref, o_ref):
        c = c_ref[...]
        ca = c * _sigmoid(c)
        ca_ref[...] = ca
        for li in range(n_layer):
            o_ref[li * N_DEV:(li + 1) * N_DEV, :] = (
                jnp.dot(ca, w_ref[li], preferred_element_type=F32, precision=lax.Precision.HIGHEST) + b_ref[li])

    return pl.pallas_call(
        body, name=name,
        out_shape=(jax.ShapeDtypeStruct((N_DEV, d), F32), jax.ShapeDtypeStruct((n_layer * N_DEV, w), F32)),
        in_specs=[VMEM_SPEC] * 3, out_specs=(VMEM_SPEC, VMEM_SPEC),
        compiler_params=pltpu.CompilerParams(vmem_limit_bytes=VMEM_LIMIT),
    )(c_all, mod_w, mod_b_cols)


def _adamw(w, g, m, v):
    m = ADAM_B1 * m + (1.0 - ADAM_B1) * g
    v = ADAM_B2 * v + (1.0 - ADAM_B2) * (g * g)
    m_hat = m / (1.0 - ADAM_B1 ** ADAM_STEP)
    v_hat = v / (1.0 - ADAM_B2 ** ADAM_STEP)
    delta = -ADAM_LR * (m_hat / (jnp.sqrt(v_hat) + ADAM_EPS) + ADAM_WD * w)
    return delta, m, v


def _adamw_2d(w, g, m, v, name):
    r, c = w.shape
    tr, tc = _tile(r, 512), _tile(c, 1024)

    def body(w_ref, g_ref, m_ref, v_ref, go_ref, d_ref, nm_ref, nv_ref):
        g = g_ref[...]
        go_ref[...] = g
        d_ref[...], nm_ref[...], nv_ref[...] = _adamw(w_ref[...], g, m_ref[...], v_ref[...])

    spec = pl.BlockSpec((tr, tc), lambda i, j: (i, j))
    shape = jax.ShapeDtypeStruct((r, c), F32)
    return pl.pallas_call(
        body, name=name, out_shape=(shape,) * 4, grid=(r // tr, c // tc),
        in_specs=[spec] * 4, out_specs=(spec,) * 4, compiler_params=_params("parallel", "parallel"),
    )(w, g, m, v)


def _mod_w_update(ca_t, dmod_cols, w, m, v, name):
    n_layer, d, wd = w.shape
    tr = _tile(d, 256)

    def body(ca_ref, dm_ref, w_ref, m_ref, v_ref, g_ref, d_ref, nm_ref, nv_ref):
        ca = ca_ref[...]
        dm = dm_ref[...]
        g = ca[:, 0:1] * dm[0:1, :]
        for b in range(1, N_DEV):
            g = g + ca[:, b:b + 1] * dm[b:b + 1, :]
        g_ref[...] = g
        d_ref[...], nm_ref[...], nv_ref[...] = _adamw(w_ref[...], g, m_ref[...], v_ref[...])

    spec = pl.BlockSpec((None, tr, wd), lambda l, i: (l, i, 0))
    shape = jax.ShapeDtypeStruct((n_layer, d, wd), F32)
    return pl.pallas_call(
        body, name=name, out_shape=(shape,) * 4, grid=(n_layer, d // tr),
        in_specs=[pl.BlockSpec((tr, N_DEV), lambda l, i: (i, 0)), pl.BlockSpec((None, N_DEV, wd), lambda l, i: (l, 0, 0)),
                  spec, spec, spec],
        out_specs=(spec,) * 4, compiler_params=_params("parallel", "parallel"),
    )(ca_t, dmod_cols, w, m, v)


def _adamw_small(ws, gs, ms, vs, name):
    n = len(ws)

    def body(*refs):
        ins, outs = refs[:4 * n], refs[4 * n:]
        for k in range(n):
            delta, nm, nv = _adamw(ins[k][...], ins[n + k][...], ins[2 * n + k][...], ins[3 * n + k][...])
            outs[3 * k][...] = delta
            outs[3 * k + 1][...] = nm
            outs[3 * k + 2][...] = nv

    out_shape = []
    for w in ws:
        out_shape += [jax.ShapeDtypeStruct(w.shape, F32)] * 3
    outs = pl.pallas_call(
        body, name=name, out_shape=tuple(out_shape),
        in_specs=[VMEM_SPEC] * (4 * n), out_specs=tuple([VMEM_SPEC] * (3 * n)),
        compiler_params=pltpu.CompilerParams(vmem_limit_bytes=VMEM_LIMIT),
    )(*ws, *gs, *ms, *vs)
    return [tuple(outs[3 * k:3 * k + 3]) for k in range(n)]


def _place():
    return lax.axis_index("x"), lax.axis_index("y"), lax.axis_index("c")


def _other_chips(x, y):
    return [(1 - x, y), (x, 1 - y), (1 - x, 1 - y)]


def _all_gather(block, name, after=None):
    m_per, n = block.shape

    def body(x_ref, *rest):
        out_ref, send_sems, recv_sems, local_sem = rest[-4:]
        x, y, c = _place()
        me, sibling = (x, y, c), (x, y, 1 - c)
        chips = _other_chips(x, y)

        def rows(px, py, pc):
            return out_ref.at[pl.ds((4 * px + 2 * py + pc) * m_per, m_per), :]

        def copy(k, blk, to, src=None):
            return pltpu.make_async_remote_copy(
                src_ref=rows(*blk) if src is None else src, dst_ref=rows(*blk),
                send_sem=send_sems.at[k], recv_sem=recv_sems.at[k], device_id=to, device_id_type=MESH)

        mine = pltpu.make_async_copy(x_ref, rows(*me), local_sem)
        mine.start()
        first = [copy(0, me, sibling, src=x_ref)]
        first += [copy(1 + j, me, (*chip, c), src=x_ref) for j, chip in enumerate(chips)]
        for cp in first:
            cp.start()
        passed = [copy(4 + j, (*chip, c), sibling) for j, chip in enumerate(chips)]
        for j, chip in enumerate(chips):
            copy(1 + j, (*chip, c), me).wait_recv()
            passed[j].start()
        copy(0, sibling, me).wait_recv()
        for j, chip in enumerate(chips):
            copy(4 + j, (*chip, 1 - c), me).wait_recv()
        for cp in first + passed:
            cp.wait_send()
        mine.wait()

    return pl.pallas_call(
        body, name=name, out_shape=jax.ShapeDtypeStruct((N_DEV * m_per, n), F32),
        in_specs=[VMEM_SPEC] + ([] if after is None else [ANY_SPEC]), out_specs=VMEM_SPEC,
        scratch_shapes=[pltpu.SemaphoreType.DMA((7,)), pltpu.SemaphoreType.DMA((7,)), pltpu.SemaphoreType.DMA],
        compiler_params=pltpu.CompilerParams(vmem_limit_bytes=VMEM_LIMIT),
    )(*([block] if after is None else [block, after]))


def _hbm(a):
    return pltpu.with_memory_space_constraint(a, pltpu.HBM)


def _all_gather_start(blocks, name):
    def body(in_ref, send_sems, recv_sems, thru_ref, token):
        token[...] = jnp.zeros_like(token)
        x, y, c = _place()
        mine = in_ref.at[4 * x + 2 * y + c]
        pltpu.make_async_remote_copy(
            src_ref=mine, dst_ref=mine, send_sem=send_sems.at[0], recv_sem=recv_sems.at[0],
            device_id=(x, y, 1 - c), device_id_type=MESH).start()
        for j, chip in enumerate(_other_chips(x, y)):
            for cc in range(2):
                pltpu.make_async_remote_copy(
                    src_ref=mine, dst_ref=mine, send_sem=send_sems.at[1 + 2 * j + cc],
                    recv_sem=recv_sems.at[1 + 2 * j + c], device_id=(*chip, cc), device_id_type=MESH).start()

    sems = pltpu.SemaphoreType.DMA((N_DEV - 1,))
    return pl.pallas_call(
        body, name=name,
        out_shape=(sems, sems, pltpu.HBM(blocks.shape, blocks.dtype), jax.ShapeDtypeStruct((SUBLANE, 128), F32)),
        in_specs=[HBM_SPEC], out_specs=(SEM_SPEC, SEM_SPEC, HBM_SPEC, VMEM_SPEC), input_output_aliases={0: 2},
        compiler_params=pltpu.CompilerParams(has_side_effects=EFFECT),
    )(_hbm(blocks))


def _all_gather_wait(send_sems, recv_sems, blocks, after, name):
    def body(in_ref, send_sems, recv_sems, after_ref, out_ref):
        x, y, c = _place()
        one = in_ref.at[0]
        for k in range(N_DEV - 1):
            pltpu.make_async_remote_copy(
                src_ref=one, dst_ref=one, send_sem=send_sems.at[k], recv_sem=recv_sems.at[k],
                device_id=(x, y, 1 - c), device_id_type=MESH).wait_send()
            pltpu.make_async_remote_copy(
                src_ref=one, dst_ref=one, send_sem=send_sems.at[k], recv_sem=recv_sems.at[k],
                device_id=(x, y, 1 - c), device_id_type=MESH).wait_recv()

    return pl.pallas_call(
        body, name=name, out_shape=pltpu.HBM(blocks.shape, blocks.dtype),
        in_specs=[HBM_SPEC, SEM_SPEC, SEM_SPEC, ANY_SPEC], out_specs=HBM_SPEC, input_output_aliases={0: 0},
        compiler_params=pltpu.CompilerParams(has_side_effects=EFFECT),
    )(blocks, send_sems, recv_sems, after)


def _place_own(shard, kind, chip_idx, name):
    r, cdim = shard.shape
    tr, tc = _tile(r, 512), _tile(cdim, 1024)
    nrb, ncb = r // tr, cdim // tc

    def body(k_ref, s_ref, o_ref):
        o_ref[...] = s_ref[...].astype(BF16)

    if kind == "col":
        full, o_map = (r, N_CHIP * cdim), lambda i, j, kr: (i, kr[0] * ncb + j)
    else:
        full, o_map = (N_CHIP * r, cdim), lambda i, j, kr: (kr[0] * nrb + i, j)
    return pl.pallas_call(
        body, name=name, out_shape=jax.ShapeDtypeStruct(full, BF16),
        grid_spec=pltpu.PrefetchScalarGridSpec(
            num_scalar_prefetch=1, grid=(nrb, ncb),
            in_specs=[pl.BlockSpec((tr, tc), lambda i, j, kr: (i, j))],
            out_specs=pl.BlockSpec((tr, tc), o_map)),
        compiler_params=_params("parallel", "parallel"),
    )(chip_idx, shard)


def _weight_window(ref, kind, shard_shape, k, half):
    r, cdim = shard_shape
    hr = r // 2
    if kind == "col":
        return ref.at[pl.ds(half * hr, hr), pl.ds(pl.multiple_of(k * cdim, 128), cdim)]
    return ref.at[pl.ds(pl.multiple_of(k * r + half * hr, 2 * SUBLANE), hr), :]


def _target_cores(c, both_cores):
    return [(cc, cc, c) for cc in range(2)] if both_cores else [(c, 0, 0)]


def _gather_weights_start(fulls, kinds, shard_shapes, groups, after, name, both_cores=True):
    n, ng = len(fulls), len(groups)
    per = 2 if both_cores else 1

    def body(*refs):
        ins = refs[:n]
        sems = refs[n + 1:n + 1 + 2 * ng]
        token = refs[2 * n + 1 + 2 * ng]
        x, y, c = _place()
        chips = _other_chips(x, y)
        for g, group in enumerate(groups):
            for pos, (w, j) in enumerate(group):
                own = _weight_window(ins[w], kinds[w], shard_shapes[w], 2 * x + y, c)
                for cc, mine, theirs in _target_cores(c, both_cores):
                    pltpu.make_async_remote_copy(
                        src_ref=own, dst_ref=own,
                        send_sem=sems[2 * g].at[per * pos + mine], recv_sem=sems[2 * g + 1].at[per * pos + theirs],
                        device_id=(*chips[j], cc), device_id_type=MESH).start()
        token[...] = jnp.zeros_like(token)

    sem_shapes = []
    for group in groups:
        sem_shapes += [pltpu.SemaphoreType.DMA((per * len(group),))] * 2
    outs = pl.pallas_call(
        body, name=name,
        out_shape=tuple(sem_shapes) + tuple(pltpu.HBM(f.shape, f.dtype) for f in fulls)
        + (jax.ShapeDtypeStruct((SUBLANE, 128), F32),),
        in_specs=[HBM_SPEC] * n + [ANY_SPEC], out_specs=(SEM_SPEC,) * (2 * ng) + (HBM_SPEC,) * n + (VMEM_SPEC,),
        input_output_aliases={w: 2 * ng + w for w in range(n)},
        compiler_params=pltpu.CompilerParams(has_side_effects=EFFECT),
    )(*[_hbm(f) for f in fulls], after)
    sems = [(outs[2 * g], outs[2 * g + 1]) for g in range(ng)]
    return sems, list(outs[2 * ng:2 * ng + n]), outs[2 * ng + n]


def _gather_weights_wait(sems, group, fulls, kinds, shard_shapes, after, name, both_cores=True):
    n = len(fulls)
    per = 2 if both_cores else 1

    def body(*refs):
        ins = refs[:n]
        send_sems, recv_sems = refs[n], refs[n + 1]
        x, y, c = _place()
        chips = _other_chips(x, y)
        for pos, (w, j) in enumerate(group):
            own = _weight_window(ins[w], kinds[w], shard_shapes[w], 2 * x + y, c)
            for cc, mine, _ in _target_cores(c, both_cores):
                landed = _weight_window(ins[w], kinds[w], shard_shapes[w], 2 * chips[j][0] + chips[j][1], cc)
                k = per * pos + mine
                pltpu.make_async_remote_copy(
                    src_ref=own, dst_ref=own, send_sem=send_sems.at[k], recv_sem=recv_sems.at[k],
                    device_id=(*chips[j], cc), device_id_type=MESH).wait_send()
                pltpu.make_async_remote_copy(
                    src_ref=landed, dst_ref=landed, send_sem=send_sems.at[k], recv_sem=recv_sems.at[k],
                    device_id=(*chips[j], cc), device_id_type=MESH).wait_recv()

    return list(pl.pallas_call(
        body, name=name, out_shape=tuple(pltpu.HBM(f.shape, f.dtype) for f in fulls),
        in_specs=[HBM_SPEC] * n + [SEM_SPEC, SEM_SPEC, ANY_SPEC], out_specs=(HBM_SPEC,) * n,
        input_output_aliases={w: w for w in range(n)},
        compiler_params=pltpu.CompilerParams(has_side_effects=EFFECT),
    )(*fulls, sems[0], sems[1], after))


def _forward_to_sibling(full, kind, shard_shape, relations, name):
    nr = len(relations)

    def body(in_ref, out_ref, send_sems, recv_sems):
        x, y, c = _place()
        chips = _other_chips(x, y)
        copies = []
        for pos, j in enumerate(relations):
            k = 2 * chips[j][0] + chips[j][1]
            mine = _weight_window(in_ref, kind, shard_shape, k, c)
            cp = pltpu.make_async_remote_copy(
                src_ref=mine, dst_ref=_weight_window(out_ref, kind, shard_shape, k, c),
                send_sem=send_sems.at[pos], recv_sem=recv_sems.at[pos], device_id=(x, y, 1 - c), device_id_type=MESH)
            cp.start()
            copies.append(cp)
        for pos, j in enumerate(relations):
            theirs = _weight_window(out_ref, kind, shard_shape, 2 * chips[j][0] + chips[j][1], 1 - c)
            pltpu.make_async_remote_copy(
                src_ref=theirs, dst_ref=theirs, send_sem=send_sems.at[pos], recv_sem=recv_sems.at[pos],
                device_id=(x, y, 1 - c), device_id_type=MESH).wait_recv()
        for cp in copies:
            cp.wait_send()

    return pl.pallas_call(
        body, name=name, out_shape=jax.ShapeDtypeStruct(full.shape, full.dtype),
        in_specs=[HBM_SPEC], out_specs=HBM_SPEC, input_output_aliases={0: 0},
        scratch_shapes=[pltpu.SemaphoreType.DMA((nr,))] * 2,
    )(full)


def _grad_piece(ref, kind, h, cdim, k, half):
    if kind == "col":
        return ref.at[pl.ds(half * h, h), pl.ds(pl.multiple_of(k * cdim, 128), cdim)]
    return ref.at[k, pl.ds(half * h, h), :]


def _grad_dims(g, kind):
    return (g.shape[0] // 2, g.shape[1] // N_CHIP) if kind == "col" else (g.shape[1] // 2, g.shape[2])


def _reduce_start(grads, kinds, name):
    n = len(grads)
    dims = [_grad_dims(g, kind) for g, kind in zip(grads, kinds)]
    lands = [lax.empty((N_DEV - 1, h, cdim), g.dtype) for g, (h, cdim) in zip(grads, dims)]

    def body(*refs):
        g_ins, land_ins = refs[:n], refs[n:2 * n]
        send_sems, recv_sems = refs[2 * n], refs[2 * n + 1]
        token = refs[4 * n + 2]
        x, y, c = _place()
        for w in range(n):
            h, cdim = dims[w]
            base = (N_DEV - 1) * w
            pltpu.make_async_remote_copy(
                src_ref=_grad_piece(g_ins[w], kinds[w], h, cdim, 2 * x + y, 1 - c), dst_ref=land_ins[w].at[0],
                send_sem=send_sems.at[base], recv_sem=recv_sems.at[base],
                device_id=(x, y, 1 - c), device_id_type=MESH).start()
            for j, chip in enumerate(_other_chips(x, y)):
                for cc in range(2):
                    pltpu.make_async_remote_copy(
                        src_ref=_grad_piece(g_ins[w], kinds[w], h, cdim, 2 * chip[0] + chip[1], cc),
                        dst_ref=land_ins[w].at[1 + 2 * j + c],
                        send_sem=send_sems.at[base + 1 + 2 * j + cc], recv_sem=recv_sems.at[base + 1 + 2 * j + c],
                        device_id=(*chip, cc), device_id_type=MESH).start()
        token[...] = jnp.zeros_like(token)

    sems = pltpu.SemaphoreType.DMA(((N_DEV - 1) * n,))
    outs = pl.pallas_call(
        body, name=name,
        out_shape=(sems, sems) + tuple(pltpu.HBM(a.shape, a.dtype) for a in list(grads) + lands)
        + (jax.ShapeDtypeStruct((SUBLANE, 128), F32),),
        in_specs=[HBM_SPEC] * (2 * n), out_specs=(SEM_SPEC, SEM_SPEC) + (HBM_SPEC,) * (2 * n) + (VMEM_SPEC,),
        input_output_aliases={i: 2 + i for i in range(2 * n)},
        compiler_params=pltpu.CompilerParams(has_side_effects=EFFECT),
    )(*[_hbm(a) for a in list(grads) + lands])
    return outs[0], outs[1], list(outs[2:2 + n]), list(outs[2 + n:2 + 2 * n]), outs[2 + 2 * n]


def _reduce_wait(send_sems, recv_sems, grads, lands, kinds, after, name):
    n = len(grads)
    dims = [_grad_dims(g, kind) for g, kind in zip(grads, kinds)]

    def body(*refs):
        g_ins, land_ins = refs[:n], refs[n:2 * n]
        send_sems, recv_sems = refs[2 * n], refs[2 * n + 1]
        x, y, c = _place()
        for w in range(n):
            h, cdim = dims[w]
            piece = _grad_piece(g_ins[w], kinds[w], h, cdim, 2 * x + y, c)
            for s in range(N_DEV - 1):
                k = (N_DEV - 1) * w + s
                slot = land_ins[w].at[s]
                pltpu.make_async_remote_copy(
                    src_ref=piece, dst_ref=slot, send_sem=send_sems.at[k], recv_sem=recv_sems.at[k],
                    device_id=(x, y, 1 - c), device_id_type=MESH).wait_send()
                pltpu.make_async_remote_copy(
                    src_ref=piece, dst_ref=slot, send_sem=send_sems.at[k], recv_sem=recv_sems.at[k],
                    device_id=(x, y, 1 - c), device_id_type=MESH).wait_recv()

    outs = pl.pallas_call(
        body, name=name, out_shape=tuple(pltpu.HBM(a.shape, a.dtype) for a in list(grads) + list(lands)),
        in_specs=[HBM_SPEC] * (2 * n) + [SEM_SPEC, SEM_SPEC, ANY_SPEC], out_specs=(HBM_SPEC,) * (2 * n),
        input_output_aliases={i: i for i in range(2 * n)},
        compiler_params=pltpu.CompilerParams(has_side_effects=EFFECT),
    )(*grads, *lands, send_sems, recv_sems, after)
    return list(outs[:n]), list(outs[n:])


def _add_pieces(g, land, kind, chip_idx, core_idx, name):
    _, h, cdim = land.shape
    tr, tc = _tile(h, 256), _tile(cdim, 2048)
    nrb, ncb = h // tr, cdim // tc

    def body(k_ref, c_ref, g_ref, l_ref, o_ref):
        acc = g_ref[...].astype(F32)
        for s in range(N_DEV - 1):
            acc = acc + l_ref[s].astype(F32)
        o_ref[...] = acc

    if kind == "col":
        g_spec = pl.BlockSpec((tr, tc), lambda i, j, kr, cr: (cr[0] * nrb + i, kr[0] * ncb + j))
    else:
        g_spec = pl.BlockSpec((None, tr, tc), lambda i, j, kr, cr: (kr[0], cr[0] * nrb + i, j))
    return pl.pallas_call(
        body, name=name, out_shape=jax.ShapeDtypeStruct((2 * h, cdim), F32),
        grid_spec=pltpu.PrefetchScalarGridSpec(
            num_scalar_prefetch=2, grid=(nrb, ncb),
            in_specs=[g_spec, pl.BlockSpec((N_DEV - 1, tr, tc), lambda i, j, kr, cr: (0, i, j))],
            out_specs=pl.BlockSpec((tr, tc), lambda i, j, kr, cr: (cr[0] * nrb + i, j))),
        compiler_params=_params("parallel", "parallel"),
    )(chip_idx, core_idx, g, land)


def _join_halves(shards, name):
    n = len(shards)

    def body(*refs):
        ins, outs = refs[:n], refs[n:2 * n]
        send_sems, recv_sems = refs[2 * n:]
        x, y, c = _place()
        copies = []
        for w in range(n):
            h = shards[w].shape[0] // 2
            cp = pltpu.make_async_remote_copy(
                src_ref=ins[w].at[pl.ds(c * h, h), :], dst_ref=outs[w].at[pl.ds(c * h, h), :],
                send_sem=send_sems.at[w], recv_sem=recv_sems.at[w], device_id=(x, y, 1 - c), device_id_type=MESH)
            cp.start()
            copies.append(cp)
        for w in range(n):
            h = shards[w].shape[0] // 2
            theirs = outs[w].at[pl.ds((1 - c) * h, h), :]
            pltpu.make_async_remote_copy(
                src_ref=theirs, dst_ref=theirs, send_sem=send_sems.at[w], recv_sem=recv_sems.at[w],
                device_id=(x, y, 1 - c), device_id_type=MESH).wait_recv()
        for cp in copies:
            cp.wait_send()

    return pl.pallas_call(
        body, name=name, out_shape=tuple(jax.ShapeDtypeStruct(s.shape, s.dtype) for s in shards),
        in_specs=[HBM_SPEC] * n, out_specs=tuple([HBM_SPEC] * n),
        input_output_aliases={w: w for w in range(n)},
        scratch_shapes=[pltpu.SemaphoreType.DMA((n,))] * 2,
    )(*shards)


def _sum_devices(gathered, rows, name):
    n = gathered.shape[1]

    def body(g_ref, o_ref):
        acc = g_ref[0:rows, :]
        for dev in range(1, N_DEV):
            acc = acc + g_ref[dev * rows:(dev + 1) * rows, :]
        o_ref[...] = acc

    return pl.pallas_call(
        body, name=name, out_shape=jax.ShapeDtypeStruct((rows, n), F32),
        in_specs=[VMEM_SPEC], out_specs=VMEM_SPEC,
        compiler_params=pltpu.CompilerParams(vmem_limit_bytes=VMEM_LIMIT),
    )(gathered)


def _pack(vectors, width):
    flat = [v.reshape(-1) for v in vectors]
    offsets, total = [], 0
    for f in flat:
        offsets.append(total)
        total += f.shape[0]
    rows = -(-total // (width * SUBLANE)) * SUBLANE
    flat.append(jnp.zeros((rows * width - total,), F32))
    return jnp.concatenate(flat).reshape(rows, width), offsets


def kernel(x, c, mod_w, mod_b, norm_g, a_w_in, a_conv_w, a_conv_b, a_w_out, b_w_in, b_ln_g, b_ln_b, b_w_s, b_b_s, b_w_out, final_g, loss_target, m_mod_w, m_mod_b, m_norm_g, m_a_w_in, m_a_conv_w, m_a_conv_b, m_a_w_out, m_b_w_in, m_b_ln_g, m_b_ln_b, m_b_w_s, m_b_b_s, m_b_w_out, m_final_g, v_mod_w, v_mod_b, v_norm_g, v_a_w_in, v_a_conv_w, v_a_conv_b, v_a_w_out, v_b_w_in, v_b_ln_g, v_b_ln_b, v_b_w_s, v_b_b_s, v_b_w_out, v_final_g):
    seq, d = x.shape[1], x.shape[2]
    e = a_conv_b.shape[1]
    wd = mod_w.shape[2]
    ax, ay, ac = _place()
    chip = 2 * ax + ay
    dev = 2 * chip + ac
    chip_idx = jnp.reshape(chip, (1,)).astype(jnp.int32)
    core_idx = jnp.reshape(ac, (1,)).astype(jnp.int32)

    x2d = x[0]
    target = loss_target[0]

    w_names = ["a_w_in", "a_w_out", "b_w_in", "b_w_out"]
    w_kinds = ["col", "row", "col", "row"]
    w_shards = [a_w_in[0], a_w_out[0], b_w_in[0], b_w_out[0]]
    w_shapes = [sh.shape for sh in w_shards]
    placed = [_place_own(sh, kind, chip_idx, "place_" + nm) for nm, sh, kind in zip(w_names, w_shards, w_kinds)]
    a_groups, far_groups = [[(0, 0), (0, 1)]], [[(0, 2)]]
    whole_group = [[(0, 0), (0, 1), (0, 2)]]

    es = e // N_CHIP
    packed0, offs0 = _pack([c, a_conv_w, b_ln_g, b_ln_b], 1024)
    gathered0 = _all_gather(packed0, "gather_params").reshape(N_DEV, -1)
    c_all = gathered0[:, :d]
    per_chip = gathered0[0::2]

    def from_chips_cols(k, rows_):
        got = per_chip[:, offs0[k]:offs0[k] + rows_ * es].reshape(N_CHIP, rows_, es)
        return jnp.transpose(got, (1, 0, 2)).reshape(rows_, e)

    conv_w_full = from_chips_cols(1, 3)
    conv_w = [conv_w_full[k:k + 1] for k in range(3)]
    ln_g, ln_b = from_chips_cols(2, 1), from_chips_cols(3, 1)
    mod_b_cols = lax.dynamic_slice_in_dim(mod_b, chip * wd, wd, axis=1)[:, None, :]
    c_act, mod_part = _mod_fwd(c_all, mod_w, mod_b_cols, "mod_fwd")
    n_layer = mod_w.shape[0]
    mod_gathered = _all_gather(mod_part, "gather_mod")
    a_sems, (wa_in,), a_token = _gather_weights_start(
        placed[:1], w_kinds[:1], w_shapes[:1], a_groups, mod_gathered, "gather_a_w_in_start", both_cores=False)
    mod_all = mod_gathered.reshape(N_CHIP, 2, n_layer, N_DEV, wd)[:, 0]
    mod_all = jnp.transpose(mod_all, (1, 2, 0, 3)).reshape(n_layer, N_DEV, N_CHIP * wd)
    mod_me = lax.dynamic_index_in_dim(mod_all, dev, axis=1, keepdims=False)
    shift = [mod_me[l:l + 1, 0:d] for l in range(n_layer)]
    scale = [mod_me[l:l + 1, d:2 * d] for l in range(n_layer)]
    gate = [mod_me[l:l + 1, 2 * d:3 * d] for l in range(n_layer)]

    g0, g1, gf = norm_g[0:1], norm_g[1:2], final_g[None, :]
    h0 = _norm_mod(x2d, g0, scale[0], shift[0], "norm_mod0")

    def slab(r):
        return jnp.bitwise_xor(chip_idx, r)

    def arrived(sems, group, weight, w, after, name, both_cores=True):
        return _gather_weights_wait(sems, [(0, j) for _, j in group], [weight], [w_kinds[w]], [w_shapes[w]],
                                    after, name, both_cores)[0]

    proj0 = _mm_proj_slab(h0, wa_in, None, slab(0), 4, "a_proj_own", a_token)
    wa_in = arrived(a_sems[0], a_groups[0], wa_in, 0, proj0, "gather_wait_near", both_cores=False)
    wa_in = _forward_to_sibling(wa_in, w_kinds[0], w_shapes[0], [0, 1], "forward_near")
    far_sems, (wa_in,), far_token = _gather_weights_start(
        [wa_in], w_kinds[:1], w_shapes[:1], far_groups, a_token, "gather_a_w_in_far_start")
    proj0 = _mm_proj_slab(h0, wa_in, proj0, slab(2), 4, "a_proj_x", far_token)
    proj0 = _mm_proj_slab(h0, wa_in, proj0, slab(1), 4, "a_proj_y", far_token)
    wa_in = arrived(far_sems[0], far_groups[0], wa_in, 0, proj0, "gather_wait_far")
    def start_whole(w, after, name):
        sems, (flight,), token = _gather_weights_start(
            placed[w:w + 1], w_kinds[w:w + 1], w_shapes[w:w + 1], whole_group, after, name)
        return sems[0], flight, token

    ao_sems, wa_out, ao_token = start_whole(1, wa_in, "gather_a_w_out_start")
    proj0 = _mm_proj_slab(h0, wa_in, proj0, slab(3), 4, "a_proj_far", ao_token)
    wa_out = arrived(ao_sems, whole_group[0], wa_out, 1, proj0, "gather_wait_a_w_out")
    bi_sems, wb_in, bi_token = start_whole(2, wa_out, "gather_b_w_in_start")
    y0, br0, x1, h1 = _conv_fwd(proj0, conv_w, a_conv_b, wa_out, x2d, gate[0], g1, scale[1], shift[1],
                                "conv_fwd", bi_token)
    wb_in = arrived(bi_sems, whole_group[0], wb_in, 2, h1, "gather_wait_b_w_in")
    bo_sems, wb_out, bo_token = start_whole(3, wb_in, "gather_b_w_out_start")
    proj1 = _mm_proj(h1, wb_in, 3, "b_proj", bo_token)
    b_s_t = jnp.transpose(b_b_s[0])
    wb_out = arrived(bo_sems, whole_group[0], wb_out, 3, proj1, "gather_wait_b_w_out")
    y1, dx2, dbr1, loss_part, g_final_g, dgate1 = _gmlp_fwd_head(
        proj1, ln_g, ln_b, b_w_s[0], b_s_t, wb_out, x1, gate[1], gf, target, "gmlp_fwd_head")

    gw_b_out = _mm_dw_out(y1, dbr1, "b_out_dw")
    dproj1, g_w_s, g_b_s_t, g_ln_g, g_ln_b = _gmlp_bwd(
        proj1, dbr1, wb_out, ln_g, ln_b, b_w_s[0], jnp.swapaxes(b_w_s[0], 1, 2), b_s_t, "gmlp_bwd")
    gw_b_in = _mm_dw_in(h1, dproj1, "b_proj_dw")
    b_kinds = ["col", "row"]
    b_send, b_recv, b_grads, b_lands, b_token = _reduce_start(
        [gw_b_in, gw_b_out.reshape(N_CHIP, e // N_CHIP, d)], b_kinds, "reduce_b_start")
    dx1, dshift1, dscale1, g_g1, dbr0, dgate0 = _mm_dh_norm_bwd(
        dproj1, wb_in, x1, dx2, g1, scale[1], "b_proj_dx", b_token, br=br0, gate=gate[0])

    early = [loss_part[0, 0:1], g_final_g, g_g1, g_ln_g, g_ln_b, jnp.transpose(g_b_s_t), g_w_s,
             jnp.concatenate([dshift1, dscale1, dgate1, dgate0], axis=1)]
    packed_e, offs_e = _pack(early, 1024)
    rows_e = packed_e.shape[0]
    blocks_e = lax.dynamic_update_slice(jnp.zeros((N_DEV, rows_e, 1024), F32), packed_e[None], (dev, 0, 0))
    e_send, e_recv, blocks_e, e_token = _all_gather_start(blocks_e, "gather_small_start")

    gw_a_out = _mm_dw_out(y0, dbr0, "a_out_dw")
    ao_send, ao_recv, ao_grads, ao_lands, ao_grad_token = _reduce_start(
        [gw_a_out.reshape(N_CHIP, e // N_CHIP, d)], ["row"], "reduce_a_out_start")
    dproj0, g_w0, g_w1, g_w2, g_conv_b = _conv_bwd(
        proj0, dbr0, wa_out, conv_w, a_conv_b, "conv_bwd", ao_grad_token + e_token)
    gw_a_in = _mm_dw_in(h0, dproj0, "a_proj_dw")
    ai_send, ai_recv, ai_grads, ai_lands, ai_token = _reduce_start([gw_a_in], ["col"], "reduce_a_in_start")
    grad_x, dshift0, dscale0, g_g0 = _mm_dh_norm_bwd(dproj0, wa_in, x2d, dx1, g0, scale[0], "a_proj_dx", ai_token)

    def finish(send, recv, grads_, lands_, kinds_, names_, after, tag):
        grads_, lands_ = _reduce_wait(send, recv, grads_, lands_, kinds_, after, "reduce_" + tag + "_wait")
        halves = [_add_pieces(g, land, kind, chip_idx, core_idx, "add_pieces_" + nm)
                  for g, land, kind, nm in zip(grads_, lands_, kinds_, names_)]
        return _join_halves(halves, "join_" + tag)

    upd, big_grads = {}, {}

    def adamw_big(nm, w, g, m, v):
        g_out, *rest = _adamw_2d(w[0], g, m[0], v[0], "adamw_" + nm)
        big_grads[nm] = g_out[None]
        upd[nm] = tuple(o[None] for o in rest)

    g_b_w_in, g_b_w_out = finish(b_send, b_recv, b_grads, b_lands, b_kinds, ["b_w_in", "b_w_out"], grad_x, "b")
    adamw_big("b_w_in", b_w_in, g_b_w_in, m_b_w_in, v_b_w_in)
    adamw_big("b_w_out", b_w_out, g_b_w_out, m_b_w_out, v_b_w_out)

    late = [g_g0, jnp.concatenate([g_w0, g_w1, g_w2], axis=0), g_conv_b, jnp.concatenate([dshift0, dscale0], axis=1)]
    packed_l, offs_l = _pack(late, 1024)
    rows_l = packed_l.shape[0]
    gathered_l = _all_gather(packed_l, "gather_small_late", after=upd["b_w_out"][0])
    blocks_e = _all_gather_wait(e_send, e_recv, blocks_e, gathered_l, "gather_small_wait")
    gathered_e = blocks_e.reshape(N_DEV * rows_e, 1024)
    summed_e = _sum_devices(gathered_e, rows_e, "sum_small_early").reshape(-1)
    summed_l = _sum_devices(gathered_l, rows_l, "sum_small_late").reshape(-1)

    def take(summed, offs, k, shape):
        size = math.prod(shape)
        return summed[offs[k]:offs[k] + size].reshape(shape)

    def rows_of(gathered, rows_, offs, k, width):
        return gathered.reshape(N_DEV, rows_ * 1024)[:, offs[k]:offs[k] + width]

    loss = take(summed_e, offs_e, 0, ())
    grad_final_g = take(summed_e, offs_e, 1, (d,))
    grad_norm_g = jnp.concatenate([take(summed_l, offs_l, 0, (1, d)), take(summed_e, offs_e, 2, (1, d))], axis=0)
    grad_ln_g_full = take(summed_e, offs_e, 3, (1, e))
    grad_ln_b_full = take(summed_e, offs_e, 4, (1, e))
    grad_b_b_s = take(summed_e, offs_e, 5, (1, GROUPS, CHUNK))
    grad_b_w_s = take(summed_e, offs_e, 6, (1, GROUPS, CHUNK, CHUNK))
    grad_conv_w_full = take(summed_l, offs_l, 1, (3, e))
    grad_a_conv_b = take(summed_l, offs_l, 2, (1, e))
    grad_a_conv_w = lax.dynamic_slice_in_dim(grad_conv_w_full, chip * es, es, axis=1)[None]
    grad_b_ln_g = lax.dynamic_slice_in_dim(grad_ln_g_full, chip * es, es, axis=1)
    grad_b_ln_b = lax.dynamic_slice_in_dim(grad_ln_b_full, chip * es, es, axis=1)
    dmod_e = rows_of(gathered_e, rows_e, offs_e, 7, 4 * d)
    dmod_l = rows_of(gathered_l, rows_l, offs_l, 3, 2 * d)
    dmod_all = jnp.stack([jnp.concatenate([dmod_l, dmod_e[:, 3 * d:]], axis=1), dmod_e[:, :3 * d]], axis=1)
    mod_b_e = take(summed_e, offs_e, 7, (4 * d,))
    mod_b_l = take(summed_l, offs_l, 3, (2 * d,))
    grad_mod_b = jnp.stack([jnp.concatenate([mod_b_l, mod_b_e[3 * d:]]), mod_b_e[:3 * d]])
    dmod_cols = jnp.transpose(lax.dynamic_slice_in_dim(dmod_all, chip * wd, wd, axis=2), (1, 0, 2))

    grad_mod_w, delta_mod_w, new_m_mod_w, new_v_mod_w = _mod_w_update(
        jnp.transpose(c_act), dmod_cols, mod_w, m_mod_w, v_mod_w, "mod_w_update")
    (g_a_w_out,) = finish(ao_send, ao_recv, ao_grads, ao_lands, ["row"], ["a_w_out"], delta_mod_w, "a_out")
    adamw_big("a_w_out", a_w_out, g_a_w_out, m_a_w_out, v_a_w_out)
    (g_a_w_in,) = finish(ai_send, ai_recv, ai_grads, ai_lands, ["col"], ["a_w_in"], upd["a_w_out"][0], "a_in")
    adamw_big("a_w_in", a_w_in, g_a_w_in, m_a_w_in, v_a_w_in)
    small_w = [("mod_b", mod_b, grad_mod_b, m_mod_b, v_mod_b), ("norm_g", norm_g, grad_norm_g, m_norm_g, v_norm_g),
               ("a_conv_w", a_conv_w, grad_a_conv_w, m_a_conv_w, v_a_conv_w),
               ("a_conv_b", a_conv_b, grad_a_conv_b, m_a_conv_b, v_a_conv_b),
               ("b_ln_g", b_ln_g, grad_b_ln_g, m_b_ln_g, v_b_ln_g), ("b_ln_b", b_ln_b, grad_b_ln_b, m_b_ln_b, v_b_ln_b),
               ("b_w_s", b_w_s, grad_b_w_s, m_b_w_s, v_b_w_s), ("b_b_s", b_b_s, grad_b_b_s, m_b_b_s, v_b_b_s),
               ("final_g", final_g, grad_final_g, m_final_g, v_final_g)]

    def flat2d(a):
        return a.reshape(-1, a.shape[-1])

    res = _adamw_small([flat2d(t[1]) for t in small_w], [flat2d(t[2]) for t in small_w],
                       [flat2d(t[3]) for t in small_w], [flat2d(t[4]) for t in small_w], "adamw_small")
    for (nm, w, _, _, _), r3 in zip(small_w, res):
        upd[nm] = tuple(o.reshape(w.shape) for o in r3)
    upd["mod_w"] = (delta_mod_w, new_m_mod_w, new_v_mod_w)

    grads = {"mod_w": grad_mod_w, "mod_b": grad_mod_b, "norm_g": grad_norm_g, "a_conv_w": grad_a_conv_w,
             "a_conv_b": grad_a_conv_b, "b_ln_g": grad_b_ln_g, "b_ln_b": grad_b_ln_b, "b_w_s": grad_b_w_s,
             "b_b_s": grad_b_b_s, "final_g": grad_final_g, **big_grads}
    order = ["mod_w", "mod_b", "norm_g", "a_w_in", "a_conv_w", "a_conv_b", "a_w_out", "b_w_in", "b_ln_g", "b_ln_b",
             "b_w_s", "b_b_s", "b_w_out", "final_g"]
    return (loss, grad_x[None], *[grads[k] for k in order], *[upd[k][0] for k in order],
            *[upd[k][1] for k in order], *[upd[k][2] for k in order])
```

```python
import math

import jax
import jax.numpy as jnp
from jax import lax
from jax.experimental import pallas as pl
from jax.experimental.pallas import tpu as pltpu

F32 = jnp.float32
BF16 = jnp.bfloat16
MESH = pl.DeviceIdType.MESH

N_DEV = 8
N_CHIP = 4
SUBLANE = 8
PACK = 16
ACT = F32
RMS_EPS = 1e-6
LN_EPS = 1e-5
CHUNK = 128
GROUPS = 8
ADAM_LR = 0.001
ADAM_B1 = 0.9
ADAM_B2 = 0.999
ADAM_EPS = 1e-08
ADAM_WD = 0.01
ADAM_STEP = 10
VMEM_LIMIT = 56 << 20

HBM_SPEC = pl.BlockSpec(memory_space=pltpu.HBM)
VMEM_SPEC = pl.BlockSpec(memory_space=pltpu.VMEM)
SEM_SPEC = pl.BlockSpec(memory_space=pltpu.SEMAPHORE)
ANY_SPEC = pl.BlockSpec(memory_space=pl.ANY)
EFFECT = pltpu.SideEffectType.DATAFLOW_SIDE_EFFECTING


def _params(*sem):
    return pltpu.CompilerParams(dimension_semantics=sem, vmem_limit_bytes=VMEM_LIMIT)


def _tile(n, want):
    if n <= want:
        return n
    t = want
    while n % t:
        t -= 128
    return t


def _sigmoid(x):
    return 0.5 * jnp.tanh(0.5 * x) + 0.5


def _silu_and_grad(x):
    s = _sigmoid(x)
    return x * s, s * (1.0 + x * (1.0 - s))


def _gelu_and_grad(x):
    cdf = 0.5 * (1.0 + lax.erf(x * (1.0 / math.sqrt(2.0))))
    pdf = jnp.exp(-0.5 * x * x) * (1.0 / math.sqrt(2.0 * math.pi))
    return x * cdf, cdf + x * pdf


def _gelu(x):
    return x * (0.5 * (1.0 + lax.erf(x * (1.0 / math.sqrt(2.0)))))


def _rms(x):
    r = lax.rsqrt(jnp.mean(x * x, axis=-1, keepdims=True) + RMS_EPS)
    return x * r, r


def _rms_bwd(dxn, xn, r):
    return r * (dxn - xn * jnp.mean(dxn * xn, axis=-1, keepdims=True))


def _colsum(a):
    return jnp.sum(a, axis=0, keepdims=True)


def _shift_down(cur, before, k):
    rolled = pltpu.roll(cur, k, 0)
    row = lax.broadcasted_iota(jnp.int32, before.shape, 0)
    head = jnp.where(row < k, pltpu.roll(before, k, 0), rolled[:SUBLANE])
    return jnp.concatenate([head, rolled[SUBLANE:]], axis=0)


def _shift_up(cur, after, k):
    n = cur.shape[0]
    rolled = pltpu.roll(cur, n - k, 0)
    row = lax.broadcasted_iota(jnp.int32, after.shape, 0)
    tail = jnp.where(row >= SUBLANE - k, pltpu.roll(after, SUBLANE - k, 0), rolled[n - SUBLANE:])
    return jnp.concatenate([rolled[:n - SUBLANE], tail], axis=0)


def _after_spec():
    return pl.BlockSpec((SUBLANE, 128), lambda *_: (0, 0))


def _mm_proj(h, w, n_split, name, after):
    s, d = h.shape
    e = w.shape[1] // n_split
    tm, tn = _tile(s, 2048), _tile(e, 2048)
    nj = e // tn

    def body(h_ref, w_ref, after_ref, o_ref):
        o_ref[...] = jnp.dot(h_ref[...], w_ref[...], preferred_element_type=F32).astype(ACT)

    return pl.pallas_call(
        body, name=name,
        out_shape=jax.ShapeDtypeStruct((n_split, s, e), ACT),
        grid=(s // tm, n_split * nj),
        in_specs=[pl.BlockSpec((tm, d), lambda i, j: (i, 0)), pl.BlockSpec((d, tn), lambda i, j: (0, j)),
                  _after_spec()],
        out_specs=pl.BlockSpec((None, tm, tn), lambda i, j: (j // nj, i, j % nj)),
        compiler_params=_params("parallel", "parallel"),
    )(h, w, after)


def _mm_proj_slab(h, w, proj, q_idx, n_split, name, after):
    s, d = h.shape
    e = w.shape[1] // n_split
    tm, tn = _tile(s, 1024), _tile(e, 2048)
    nj = e // tn

    def body(q_ref, h_ref, w_ref, after_ref, proj_ref, o_ref):
        o_ref[...] = jnp.dot(h_ref[...], w_ref[...], preferred_element_type=F32).astype(ACT)

    return pl.pallas_call(
        body, name=name,
        out_shape=jax.ShapeDtypeStruct((n_split, s, e), ACT),
        grid_spec=pltpu.PrefetchScalarGridSpec(
            num_scalar_prefetch=1, grid=(s // tm, nj),
            in_specs=[pl.BlockSpec((tm, d), lambda i, j, qr: (i, 0)),
                      pl.BlockSpec((d, tn), lambda i, j, qr: (0, qr[0] * nj + j)), _after_spec(), ANY_SPEC],
            out_specs=pl.BlockSpec((None, tm, tn), lambda i, j, qr: (qr[0], i, j))),
        input_output_aliases={4: 0},
        compiler_params=_params("parallel", "parallel"),
    )(q_idx, h, w, after, proj)


def _mm_dh_norm_bwd(dp, w, x, dx_in, g, scale, name, after, br=None, gate=None):
    nq, s, e = dp.shape
    d = w.shape[0]
    has_branch = br is not None
    tm, tk = _tile(s, 1024), _tile(e, 1024 if has_branch else 2048)
    nkq = e // tk
    nk = nq * nkq

    def body(*refs):
        if has_branch:
            (a_ref, b_ref, after_ref, x_ref, dxin_ref, g_ref, sc_ref, br_ref, gate_ref,
             dx_ref, dsh_ref, dsc_ref, dg_ref, dbr_ref, dgate_ref, acc_ref) = refs
        else:
            (a_ref, b_ref, after_ref, x_ref, dxin_ref, g_ref, sc_ref,
             dx_ref, dsh_ref, dsc_ref, dg_ref, acc_ref) = refs
        i, k = pl.program_id(0), pl.program_id(1)

        @pl.when(jnp.logical_and(i == 0, k == 0))
        def _():
            dsh_ref[...] = jnp.zeros_like(dsh_ref)
            dsc_ref[...] = jnp.zeros_like(dsc_ref)
            dg_ref[...] = jnp.zeros_like(dg_ref)
            if has_branch:
                dgate_ref[...] = jnp.zeros_like(dgate_ref)

        @pl.when(k == 0)
        def _():
            acc_ref[...] = jnp.zeros_like(acc_ref)
        acc_ref[...] += lax.dot_general(a_ref[...], b_ref[...], (((1,), (1,)), ((), ())), preferred_element_type=F32)

        @pl.when(k == nk - 1)
        def _():
            dh = acc_ref[...]
            g_vec = g_ref[...]
            xn, r = _rms(x_ref[...])
            dsh_ref[...] += _colsum(dh)
            dsc_ref[...] += _colsum(dh * (xn * g_vec))
            da = dh * (1.0 + sc_ref[...])
            dg_ref[...] += _colsum(da * xn)
            dx = dxin_ref[...] + _rms_bwd(da * g_vec, xn, r)
            dx_ref[...] = dx
            if has_branch:
                dgate_ref[...] += _colsum(dx * br_ref[...])
                dbr_ref[...] = (gate_ref[...] * dx).astype(BF16)

    rows = pl.BlockSpec((tm, d), lambda i, k: (i, 0))
    vec = pl.BlockSpec((1, d), lambda i, k: (0, 0))
    vec_out = jax.ShapeDtypeStruct((1, d), F32)
    in_specs = [pl.BlockSpec((None, tm, tk), lambda i, k: (k // nkq, i, k % nkq)),
                pl.BlockSpec((d, tk), lambda i, k: (0, k)), _after_spec(), rows, rows, vec, vec]
    out_shape = [jax.ShapeDtypeStruct((s, d), F32), vec_out, vec_out, vec_out]
    out_specs = [rows, vec, vec, vec]
    args = [dp, w, after, x, dx_in, g, scale]
    if has_branch:
        in_specs += [rows, vec]
        out_shape += [jax.ShapeDtypeStruct((s, d), BF16), vec_out]
        out_specs += [rows, vec]
        args += [br, gate]
    return pl.pallas_call(
        body, name=name, out_shape=tuple(out_shape), grid=(s // tm, nk),
        in_specs=in_specs, out_specs=tuple(out_specs), scratch_shapes=[pltpu.VMEM((tm, d), F32)],
        compiler_params=_params("arbitrary", "arbitrary"),
    )(*args)


def _dw_body(n_t):
    def body(a_ref, b_ref, o_ref, acc_ref):
        t = pl.program_id(2)

        @pl.when(t == 0)
        def _():
            acc_ref[...] = jnp.zeros_like(acc_ref)
        acc_ref[...] += lax.dot_general(a_ref[...], b_ref[...], (((0,), (0,)), ((), ())), preferred_element_type=F32)

        @pl.when(t == n_t - 1)
        def _():
            o_ref[...] = acc_ref[...].astype(o_ref.dtype)
    return body


def _mm_dw_in(h, dp, name):
    s, d = h.shape
    nq, _, e = dp.shape
    tm, tn, tt = _tile(d, 1024), _tile(e, 2048), _tile(s, 2048)
    nj = e // tn

    return pl.pallas_call(
        _dw_body(s // tt), name=name,
        out_shape=jax.ShapeDtypeStruct((d, nq * e), BF16),
        grid=(d // tm, nq * nj, s // tt),
        in_specs=[pl.BlockSpec((tt, tm), lambda i, j, t: (t, i)),
                  pl.BlockSpec((None, tt, tn), lambda i, j, t: (j // nj, t, j % nj))],
        out_specs=pl.BlockSpec((tm, tn), lambda i, j, t: (i, j)),
        scratch_shapes=[pltpu.VMEM((tm, tn), F32)],
        compiler_params=_params("parallel", "parallel", "arbitrary"),
    )(h, dp)


def _mm_dw_out(y, dbr, name):
    s, e = y.shape
    d = dbr.shape[1]
    tm, tn, tt = _tile(e, 1024), _tile(d, 1024), _tile(s, 2048)

    return pl.pallas_call(
        _dw_body(s // tt), name=name,
        out_shape=jax.ShapeDtypeStruct((e, d), BF16),
        grid=(e // tm, d // tn, s // tt),
        in_specs=[pl.BlockSpec((tt, tm), lambda i, j, t: (t, i)), pl.BlockSpec((tt, tn), lambda i, j, t: (t, j))],
        out_specs=pl.BlockSpec((tm, tn), lambda i, j, t: (i, j)),
        scratch_shapes=[pltpu.VMEM((tm, tn), F32)],
        compiler_params=_params("parallel", "parallel", "arbitrary"),
    )(y, dbr)


def _row_spec(ts, d):
    return pl.BlockSpec((ts, d), lambda i: (i, 0))


def _vec_spec(d):
    return pl.BlockSpec((1, d), lambda i: (0, 0))


def _norm_mod_proj_slab(x, g, scale, shift, w, q_idx, n_split, name, after):
    s, d = x.shape
    e = w.shape[1] // n_split
    tm = _tile(s, 1024)

    def body(q_ref, x_ref, g_ref, sc_ref, sh_ref, w_ref, after_ref, h_ref, o_ref):
        xn, _ = _rms(x_ref[...])
        h = ((xn * g_ref[...]) * (1.0 + sc_ref[...]) + sh_ref[...]).astype(BF16)
        h_ref[...] = h
        o_ref[...] = jnp.dot(h, w_ref[...], preferred_element_type=F32).astype(ACT)

    vec = pl.BlockSpec((1, d), lambda i, qr: (0, 0))
    rows = pl.BlockSpec((tm, d), lambda i, qr: (i, 0))
    return pl.pallas_call(
        body, name=name,
        out_shape=(jax.ShapeDtypeStruct((s, d), BF16), jax.ShapeDtypeStruct((n_split, s, e), ACT)),
        grid_spec=pltpu.PrefetchScalarGridSpec(
            num_scalar_prefetch=1, grid=(s // tm,),
            in_specs=[rows, vec, vec, vec, pl.BlockSpec((d, e), lambda i, qr: (0, qr[0])), _after_spec()],
            out_specs=(rows, pl.BlockSpec((None, tm, e), lambda i, qr: (qr[0], i, 0)))),
        compiler_params=_params("parallel"),
    )(q_idx, x, g, scale, shift, w, after)


def _head_tile(x1, br, gate, gf, target):
    d = x1.shape[-1]
    xn, r = _rms(x1 + gate * br)
    err = xn * gf - target
    loss = 0.5 * jnp.sum(jnp.mean(err * err, axis=-1, keepdims=True))
    dout = err * (1.0 / d)
    dx = _rms_bwd(dout * gf, xn, r)
    return dx, gate * dx, loss, _colsum(dout * xn), _colsum(dx * br)


def _f32(ref):
    return ref[...].astype(F32)


def _cols_f32(ref, cols):
    return ref[:, cols].astype(F32)


def _rows_before(halo_ref, cols=slice(None)):
    return _cols_f32(halo_ref, cols)[PACK - SUBLANE:]


def _rows_after(halo_ref, cols=slice(None)):
    return _cols_f32(halo_ref, cols)[:SUBLANE]


CONV_CHAIN_COLS = 512


def _conv_fwd(proj, conv_w, conv_b, w_out, x, gate, g_next, scale_next, shift_next, name, after):
    _, s, e = proj.shape
    d = w_out.shape[1]
    ts, te = _tile(s, 256), e
    cw = min(CONV_CHAIN_COLS, te)
    hb = ts // PACK

    def body(bg_ref, cg_ref, xi_ref, z_ref, cgp_ref, xip_ref, w0_ref, w1_ref, w2_ref, b_ref, wo_ref, after_ref,
             x_ref, gate_ref, g_ref, sc_ref, sh_ref, y_ref, br_ref, x1_ref, h_ref):
        first = pl.program_id(0) == 0
        br = None
        for c0 in range(0, te, cw):
            cols = slice(c0, c0 + cw)
            cx = _cols_f32(cg_ref, cols) * _cols_f32(xi_ref, cols)
            before = jnp.where(first, 0.0, _rows_before(cgp_ref, cols) * _rows_before(xip_ref, cols))
            conv = b_ref[:, cols] + w2_ref[:, cols] * cx
            conv = conv + w0_ref[:, cols] * _shift_down(cx, before, 2)
            conv = conv + w1_ref[:, cols] * _shift_down(cx, before, 1)
            z = _cols_f32(z_ref, cols)
            y = ((z * _sigmoid(z)) * _cols_f32(bg_ref, cols) * conv).astype(BF16)
            y_ref[:, cols] = y
            part_br = jnp.dot(y, wo_ref[cols, :], preferred_element_type=F32)
            br = part_br if br is None else br + part_br
        br_ref[...] = br
        x1 = x_ref[...] + gate_ref[...] * br
        x1_ref[...] = x1
        xn, _ = _rms(x1)
        h_ref[...] = ((xn * g_ref[...]) * (1.0 + sc_ref[...]) + sh_ref[...]).astype(BF16)

    def part(q):
        return pl.BlockSpec((None, ts, te), lambda i: (q, i, 0))

    def halo_before(q):
        return pl.BlockSpec((None, PACK, te), lambda i: (q, jnp.maximum(i * hb - 1, 0), 0))

    return pl.pallas_call(
        body, name=name,
        out_shape=(jax.ShapeDtypeStruct((s, e), BF16), jax.ShapeDtypeStruct((s, d), F32),
                   jax.ShapeDtypeStruct((s, d), F32), jax.ShapeDtypeStruct((s, d), BF16)), grid=(s // ts,),
        in_specs=[part(0), part(1), part(2), part(3), halo_before(1), halo_before(2)]
        + [_vec_spec(e)] * 4 + [pl.BlockSpec((e, d), lambda i: (0, 0)), _after_spec(), _row_spec(ts, d)]
        + [_vec_spec(d)] * 4,
        out_specs=(_row_spec(ts, e), _row_spec(ts, d), _row_spec(ts, d), _row_spec(ts, d)),
        compiler_params=_params("parallel"),
    )(proj, proj, proj, proj, proj, proj, *conv_w, conv_b, w_out, after, x, gate, g_next, scale_next, shift_next)


def _conv_bwd(proj, dbr, w_out, conv_w, conv_b, name, after):
    _, s, e = proj.shape
    d = dbr.shape[1]
    ts, te = _tile(s, 512), _tile(e, 1024)
    cw = min(CONV_CHAIN_COLS // 2, te)
    hb = ts // PACK
    n_i = s // ts
    last_halo = s // PACK - 1
    nt = (((1,), (1,)), ((), ()))

    def body(bg_ref, cg_ref, xi_ref, z_ref, dbr_ref, cgp_ref, xip_ref, bgn_ref, zn_ref, dbrn_ref, wo_ref,
             w0_ref, w1_ref, w2_ref, b_ref, after_ref, dp_ref, dw0_ref, dw1_ref, dw2_ref, db_ref):
        i = pl.program_id(1)

        @pl.when(i == 0)
        def _():
            for acc in (dw0_ref, dw1_ref, dw2_ref, db_ref):
                acc[...] = jnp.zeros_like(acc)
        for c0 in range(0, te, cw):
            cols = slice(c0, c0 + cw)
            wo = wo_ref[cols, :]
            dy = lax.dot_general(dbr_ref[...], wo, nt, preferred_element_type=F32)
            dyn = lax.dot_general(dbrn_ref[...], wo, nt, preferred_element_type=F32)[:SUBLANE]
            bg, cg = _cols_f32(bg_ref, cols), _cols_f32(cg_ref, cols)
            xi, z = _cols_f32(xi_ref, cols), _cols_f32(z_ref, cols)
            w0, w1, w2 = w0_ref[:, cols], w1_ref[:, cols], w2_ref[:, cols]
            cx = cg * xi
            before = jnp.where(i > 0, _rows_before(cgp_ref, cols) * _rows_before(xip_ref, cols), 0.0)
            cx1 = _shift_down(cx, before, 1)
            cx2 = _shift_down(cx, before, 2)
            conv = b_ref[:, cols] + w2 * cx
            conv = conv + w0 * cx2
            conv = conv + w1 * cx1
            sz, dsz = _silu_and_grad(z)
            dp_ref[3, :, cols] = (dy * bg * conv * dsz).astype(BF16)
            dp_ref[0, :, cols] = (dy * sz * conv).astype(BF16)
            dconv = dy * sz * bg
            zn = _rows_after(zn_ref, cols)
            after = jnp.where(i < n_i - 1, dyn * (zn * _sigmoid(zn)) * _rows_after(bgn_ref, cols), 0.0)
            db_ref[:, cols] += _colsum(dconv)
            dw2_ref[:, cols] += _colsum(dconv * cx)
            dw1_ref[:, cols] += _colsum(dconv * cx1)
            dw0_ref[:, cols] += _colsum(dconv * cx2)
            dcx = w2 * dconv + w1 * _shift_up(dconv, after, 1) + w0 * _shift_up(dconv, after, 2)
            dp_ref[1, :, cols] = (dcx * xi).astype(BF16)
            dp_ref[2, :, cols] = (dcx * cg).astype(BF16)

    def part(q):
        return pl.BlockSpec((None, ts, te), lambda j, i: (q, i, j))

    def halo_before(q):
        return pl.BlockSpec((None, PACK, te), lambda j, i: (q, jnp.maximum(i * hb - 1, 0), j))

    def halo_after(q):
        return pl.BlockSpec((None, PACK, te), lambda j, i: (q, jnp.minimum((i + 1) * hb, last_halo), j))

    return pl.pallas_call(
        body, name=name,
        out_shape=(jax.ShapeDtypeStruct((4, s, e), BF16),) + (jax.ShapeDtypeStruct((1, e), F32),) * 4,
        grid=(e // te, n_i),
        in_specs=[part(0), part(1), part(2), part(3), pl.BlockSpec((ts, d), lambda j, i: (i, 0)),
                  halo_before(1), halo_before(2), halo_after(0), halo_after(3),
                  pl.BlockSpec((PACK, d), lambda j, i: (jnp.minimum((i + 1) * hb, last_halo), 0)),
                  pl.BlockSpec((te, d), lambda j, i: (j, 0))]
        + [pl.BlockSpec((1, te), lambda j, i: (0, j))] * 4 + [_after_spec()],
        out_specs=(pl.BlockSpec((4, ts, te), lambda j, i: (0, i, j)),) + (pl.BlockSpec((1, te), lambda j, i: (0, j)),) * 4,
        compiler_params=_params("parallel", "arbitrary"),
    )(proj, proj, proj, proj, dbr, proj, proj, proj, proj, dbr, w_out, *conv_w, conv_b, after)


def _tril(w):
    row = lax.broadcasted_iota(jnp.int32, w.shape, 0)
    col = lax.broadcasted_iota(jnp.int32, w.shape, 1)
    return jnp.where(row >= col, w, 0.0)


def _triu(w):
    row = lax.broadcasted_iota(jnp.int32, w.shape, 0)
    col = lax.broadcasted_iota(jnp.int32, w.shape, 1)
    return jnp.where(row <= col, w, 0.0)


def _layer_norm_fwd(v, g, b):
    mu = jnp.mean(v, axis=-1, keepdims=True)
    vc = v - mu
    rstd = lax.rsqrt(jnp.mean(vc * vc, axis=-1, keepdims=True) + LN_EPS)
    vhat = vc * rstd
    return vhat * g + b, vhat, rstd


GMLP_CHUNKS_PER_STEP = 2


def _gmlp_rows(s):
    return CHUNK * min(GMLP_CHUNKS_PER_STEP, s // CHUNK)


def _mix_positions(w_ref, src_scr, dst_scr, gw, mask, bias_ref=None):
    for gi in range(GROUPS):
        cols = slice(gi * gw, (gi + 1) * gw)
        wm = mask(w_ref[gi]).astype(BF16)
        for n in range(src_scr.shape[0] // CHUNK):
            rows = slice(n * CHUNK, (n + 1) * CHUNK)
            out = jnp.dot(wm, src_scr[rows, cols], preferred_element_type=F32)
            if bias_ref is not None:
                out = out + bias_ref[:, gi:gi + 1]
            dst_scr[rows, cols] = out


def _gmlp_fwd_head(proj, ln_g, ln_b, w_s, b_s_t, w_out, x1, gate, gf, target, name):
    _, s, e = proj.shape
    d = w_out.shape[1]
    gw = e // GROUPS
    ts = _gmlp_rows(s)

    def body(pu_ref, pv_ref, pz_ref, g_ref, b_ref, ws_ref, bs_ref, wo_ref, x1_ref, gate_ref, gf_ref, tg_ref,
             y_ref, dx_ref, dbr_ref, loss_ref, dgf_ref, dgate_ref, vn_scr, mix_scr):
        @pl.when(pl.program_id(0) == 0)
        def _():
            loss_ref[...] = jnp.zeros_like(loss_ref)
            dgf_ref[...] = jnp.zeros_like(dgf_ref)
            dgate_ref[...] = jnp.zeros_like(dgate_ref)
        vn, _, _ = _layer_norm_fwd(_gelu(_f32(pv_ref)), g_ref[...], b_ref[...])
        vn_scr[...] = vn.astype(BF16)
        _mix_positions(ws_ref, vn_scr, mix_scr, gw, _tril, bs_ref)
        z = _f32(pz_ref)
        y = ((z * _sigmoid(z)) * (_gelu(_f32(pu_ref)) * mix_scr[...])).astype(BF16)
        y_ref[...] = y
        br = jnp.dot(y, wo_ref[...], preferred_element_type=F32)
        dx, dbr, loss, dgf, dgate = _head_tile(x1_ref[...], br, gate_ref[...], gf_ref[...], tg_ref[...])
        dx_ref[...] = dx
        dbr_ref[...] = dbr.astype(BF16)
        loss_ref[...] += loss
        dgf_ref[...] += dgf
        dgate_ref[...] += dgate

    def part(q):
        return pl.BlockSpec((None, ts, e), lambda i: (q, i, 0))

    return pl.pallas_call(
        body, name=name,
        out_shape=(jax.ShapeDtypeStruct((s, e), BF16), jax.ShapeDtypeStruct((s, d), F32),
                   jax.ShapeDtypeStruct((s, d), BF16), jax.ShapeDtypeStruct((SUBLANE, 128), F32),
                   jax.ShapeDtypeStruct((1, d), F32), jax.ShapeDtypeStruct((1, d), F32)),
        grid=(s // ts,),
        in_specs=[part(0), part(1), part(2), _vec_spec(e), _vec_spec(e),
                  pl.BlockSpec((GROUPS, CHUNK, CHUNK), lambda i: (0, 0, 0)),
                  pl.BlockSpec((CHUNK, GROUPS), lambda i: (0, 0)), pl.BlockSpec((e, d), lambda i: (0, 0)),
                  _row_spec(ts, d), _vec_spec(d), _vec_spec(d), _row_spec(ts, d)],
        out_specs=(_row_spec(ts, e), _row_spec(ts, d), _row_spec(ts, d),
                   pl.BlockSpec((SUBLANE, 128), lambda i: (0, 0)), _vec_spec(d), _vec_spec(d)),
        scratch_shapes=[pltpu.VMEM((ts, e), BF16), pltpu.VMEM((ts, e), F32)],
        compiler_params=_params("arbitrary"),
    )(proj, proj, proj, ln_g, ln_b, w_s, b_s_t, w_out, x1, gate, gf, target)


def _gmlp_bwd(proj, dbr, w_out, ln_g, ln_b, w_s, w_s_t, b_s_t, name):
    _, s, e = proj.shape
    d = dbr.shape[1]
    gw = e // GROUPS
    ts = _gmlp_rows(s)
    n_i = s // ts

    def body(pu_ref, pv_ref, pz_ref, dbr_ref, wo_ref, g_ref, b_ref, ws_ref, wst_ref, bs_ref,
             dp_ref, dws_ref, dbs_ref, dlg_ref, dlb_ref, vn_scr, mix_scr, dm_scr, dvn_scr, dmacc_scr):
        i = pl.program_id(0)

        @pl.when(i == 0)
        def _():
            dws_ref[...] = jnp.zeros_like(dws_ref)
            dlg_ref[...] = jnp.zeros_like(dlg_ref)
            dlb_ref[...] = jnp.zeros_like(dlb_ref)
            dmacc_scr[...] = jnp.zeros_like(dmacc_scr)
        ln_g = g_ref[...]
        u, du_dpu = _gelu_and_grad(_f32(pu_ref))
        v, dv_dpv = _gelu_and_grad(_f32(pv_ref))
        vn, vhat, rstd = _layer_norm_fwd(v, ln_g, b_ref[...])
        vn_scr[...] = vn.astype(BF16)
        _mix_positions(ws_ref, vn_scr, mix_scr, gw, _tril, bs_ref)
        mixed = mix_scr[...]
        dy = lax.dot_general(dbr_ref[...], wo_ref[...], (((1,), (1,)), ((), ())), preferred_element_type=F32)
        sz, dsz = _silu_and_grad(_f32(pz_ref))
        ds = dy * sz
        dp_ref[2] = (dy * (u * mixed) * dsz).astype(BF16)
        dp_ref[0] = (ds * mixed * du_dpu).astype(BF16)
        dm = ds * u
        dm_scr[...] = dm.astype(BF16)
        for n in range(ts // CHUNK):
            rows = slice(n * CHUNK, (n + 1) * CHUNK)
            dmacc_scr[...] += dm[rows]
            for gi in range(GROUPS):
                cols = slice(gi * gw, (gi + 1) * gw)
                dws_ref[gi] += lax.dot_general(dm_scr[rows, cols], vn_scr[rows, cols], (((1,), (1,)), ((), ())),
                                               preferred_element_type=F32)
        _mix_positions(wst_ref, dm_scr, dvn_scr, gw, _triu)
        dvn = dvn_scr[...]
        dlg_ref[...] += _colsum(dvn * vhat)
        dlb_ref[...] += _colsum(dvn)
        dvh = dvn * ln_g
        dv = rstd * (dvh - jnp.mean(dvh, axis=-1, keepdims=True) - vhat * jnp.mean(dvh * vhat, axis=-1, keepdims=True))
        dp_ref[1] = (dv * dv_dpv).astype(BF16)

        @pl.when(i == n_i - 1)
        def _():
            for gi in range(GROUPS):
                dws_ref[gi] = _tril(dws_ref[gi])
                dbs_ref[:, gi:gi + 1] = jnp.sum(dmacc_scr[:, gi * gw:(gi + 1) * gw], axis=1, keepdims=True)

    def part(q):
        return pl.BlockSpec((None, ts, e), lambda i: (q, i, 0))

    w_spec = pl.BlockSpec((GROUPS, CHUNK, CHUNK), lambda i: (0, 0, 0))
    bs_spec = pl.BlockSpec((CHUNK, GROUPS), lambda i: (0, 0))
    return pl.pallas_call(
        body, name=name,
        out_shape=(jax.ShapeDtypeStruct((3, s, e), BF16), jax.ShapeDtypeStruct((GROUPS, CHUNK, CHUNK), F32),
                   jax.ShapeDtypeStruct((CHUNK, GROUPS), F32), jax.ShapeDtypeStruct((1, e), F32),
                   jax.ShapeDtypeStruct((1, e), F32)),
        grid=(n_i,),
        in_specs=[part(0), part(1), part(2), pl.BlockSpec((ts, d), lambda i: (i, 0)),
                  pl.BlockSpec((e, d), lambda i: (0, 0)), _vec_spec(e), _vec_spec(e), w_spec, w_spec, bs_spec],
        out_specs=(pl.BlockSpec((3, ts, e), lambda i: (0, i, 0)), w_spec, bs_spec, _vec_spec(e), _vec_spec(e)),
        scratch_shapes=[pltpu.VMEM((ts, e), BF16), pltpu.VMEM((ts, e), F32), pltpu.VMEM((ts, e), BF16),
                        pltpu.VMEM((ts, e), F32), pltpu.VMEM((CHUNK, e), F32)],
        compiler_params=_params("arbitrary"),
    )(proj, proj, proj, dbr, w_out, ln_g, ln_b, w_s, w_s_t, b_s_t)


def _mod_fwd(c_all, mod_w, mod_b_cols, name):
    n_layer, d, w = mod_w.shape

    def body(c_ref, w_ref, b_ref, ca_ref, o_ref):
        c = c_ref[...]
        ca = c * _sigmoid(c)
        ca_ref[...] = ca
        for li in range(n_layer):
            o_ref[li * N_DEV:(li + 1) * N_DEV, :] = (
                jnp.dot(ca, w_ref[li], preferred_element_type=F32, precision=lax.Precision.HIGHEST) + b_ref[li])

    return pl.pallas_call(
        body, name=name,
        out_shape=(jax.ShapeDtypeStruct((N_DEV, d), F32), jax.ShapeDtypeStruct((n_layer * N_DEV, w), F32)),
        in_specs=[VMEM_SPEC] * 3, out_specs=(VMEM_SPEC, VMEM_SPEC),
        compiler_params=pltpu.CompilerParams(vmem_limit_bytes=VMEM_LIMIT),
    )(c_all, mod_w, mod_b_cols)


def _adamw(w, g, m, v):
    m = ADAM_B1 * m + (1.0 - ADAM_B1) * g
    v = ADAM_B2 * v + (1.0 - ADAM_B2) * (g * g)
    m_hat = m / (1.0 - ADAM_B1 ** ADAM_STEP)
    v_hat = v / (1.0 - ADAM_B2 ** ADAM_STEP)
    delta = -ADAM_LR * (m_hat / (jnp.sqrt(v_hat) + ADAM_EPS) + ADAM_WD * w)
    return delta, m, v


def _adamw_2d(w, g, m, v, name):
    r, c = w.shape
    tr, tc = _tile(r, 512), _tile(c, 1024)

    def body(w_ref, g_ref, m_ref, v_ref, go_ref, d_ref, nm_ref, nv_ref):
        g = g_ref[...]
        go_ref[...] = g
        d_ref[...], nm_ref[...], nv_ref[...] = _adamw(w_ref[...], g, m_ref[...], v_ref[...])

    spec = pl.BlockSpec((tr, tc), lambda i, j: (i, j))
    shape = jax.ShapeDtypeStruct((r, c), F32)
    return pl.pallas_call(
        body, name=name, out_shape=(shape,) * 4, grid=(r // tr, c // tc),
        in_specs=[spec] * 4, out_specs=(spec,) * 4, compiler_params=_params("parallel", "parallel"),
    )(w, g, m, v)


def _mod_w_update(ca_t, dmod_cols, w, m, v, name):
    n_layer, d, wd = w.shape
    tr = _tile(d, 256)

    def body(ca_ref, dm_ref, w_ref, m_ref, v_ref, g_ref, d_ref, nm_ref, nv_ref):
        ca = ca_ref[...]
        dm = dm_ref[...]
        g = ca[:, 0:1] * dm[0:1, :]
        for b in range(1, N_DEV):
            g = g + ca[:, b:b + 1] * dm[b:b + 1, :]
        g_ref[...] = g
        d_ref[...], nm_ref[...], nv_ref[...] = _adamw(w_ref[...], g, m_ref[...], v_ref[...])

    spec = pl.BlockSpec((None, tr, wd), lambda l, i: (l, i, 0))
    shape = jax.ShapeDtypeStruct((n_layer, d, wd), F32)
    return pl.pallas_call(
        body, name=name, out_shape=(shape,) * 4, grid=(n_layer, d // tr),
        in_specs=[pl.BlockSpec((tr, N_DEV), lambda l, i: (i, 0)), pl.BlockSpec((None, N_DEV, wd), lambda l, i: (l, 0, 0)),
                  spec, spec, spec],
        out_specs=(spec,) * 4, compiler_params=_params("parallel", "parallel"),
    )(ca_t, dmod_cols, w, m, v)


def _adamw_small(ws, gs, ms, vs, name):
    n = len(ws)

    def body(*refs):
        ins, outs = refs[:4 * n], refs[4 * n:]
        for k in range(n):
            delta, nm, nv = _adamw(ins[k][...], ins[n + k][...], ins[2 * n + k][...], ins[3 * n + k][...])
            outs[3 * k][...] = delta
            outs[3 * k + 1][...] = nm
            outs[3 * k + 2][...] = nv

    out_shape = []
    for w in ws:
        out_shape += [jax.ShapeDtypeStruct(w.shape, F32)] * 3
    outs = pl.pallas_call(
        body, name=name, out_shape=tuple(out_shape),
        in_specs=[VMEM_SPEC] * (4 * n), out_specs=tuple([VMEM_SPEC] * (3 * n)),
        compiler_params=pltpu.CompilerParams(vmem_limit_bytes=VMEM_LIMIT),
    )(*ws, *gs, *ms, *vs)
    return [tuple(outs[3 * k:3 * k + 3]) for k in range(n)]


def _place():
    return lax.axis_index("x"), lax.axis_index("y"), lax.axis_index("c")


def _other_chips(x, y):
    return [(1 - x, y), (x, 1 - y), (1 - x, 1 - y)]


def _all_gather(block, name, after=None):
    m_per, n = block.shape

    def body(x_ref, *rest):
        out_ref, send_sems, recv_sems, local_sem = rest[-4:]
        x, y, c = _place()
        me, sibling = (x, y, c), (x, y, 1 - c)
        chips = _other_chips(x, y)

        def rows(px, py, pc):
            return out_ref.at[pl.ds((4 * px + 2 * py + pc) * m_per, m_per), :]

        def copy(k, blk, to, src=None):
            return pltpu.make_async_remote_copy(
                src_ref=rows(*blk) if src is None else src, dst_ref=rows(*blk),
                send_sem=send_sems.at[k], recv_sem=recv_sems.at[k], device_id=to, device_id_type=MESH)

        mine = pltpu.make_async_copy(x_ref, rows(*me), local_sem)
        mine.start()
        first = [copy(0, me, sibling, src=x_ref)]
        first += [copy(1 + j, me, (*chip, c), src=x_ref) for j, chip in enumerate(chips)]
        for cp in first:
            cp.start()
        passed = [copy(4 + j, (*chip, c), sibling) for j, chip in enumerate(chips)]
        for j, chip in enumerate(chips):
            copy(1 + j, (*chip, c), me).wait_recv()
            passed[j].start()
        copy(0, sibling, me).wait_recv()
        for j, chip in enumerate(chips):
            copy(4 + j, (*chip, 1 - c), me).wait_recv()
        for cp in first + passed:
            cp.wait_send()
        mine.wait()

    return pl.pallas_call(
        body, name=name, out_shape=jax.ShapeDtypeStruct((N_DEV * m_per, n), F32),
        in_specs=[VMEM_SPEC] + ([] if after is None else [ANY_SPEC]), out_specs=VMEM_SPEC,
        scratch_shapes=[pltpu.SemaphoreType.DMA((7,)), pltpu.SemaphoreType.DMA((7,)), pltpu.SemaphoreType.DMA],
        compiler_params=pltpu.CompilerParams(vmem_limit_bytes=VMEM_LIMIT),
    )(*([block] if after is None else [block, after]))


def _hbm(a):
    return pltpu.with_memory_space_constraint(a, pltpu.HBM)


def _all_gather_start(blocks, name):
    def body(in_ref, send_sems, recv_sems, thru_ref, token):
        token[...] = jnp.zeros_like(token)
        x, y, c = _place()
        mine = in_ref.at[4 * x + 2 * y + c]
        pltpu.make_async_remote_copy(
            src_ref=mine, dst_ref=mine, send_sem=send_sems.at[0], recv_sem=recv_sems.at[0],
            device_id=(x, y, 1 - c), device_id_type=MESH).start()
        for j, chip in enumerate(_other_chips(x, y)):
            for cc in range(2):
                pltpu.make_async_remote_copy(
                    src_ref=mine, dst_ref=mine, send_sem=send_sems.at[1 + 2 * j + cc],
                    recv_sem=recv_sems.at[1 + 2 * j + c], device_id=(*chip, cc), device_id_type=MESH).start()

    sems = pltpu.SemaphoreType.DMA((N_DEV - 1,))
    return pl.pallas_call(
        body, name=name,
        out_shape=(sems, sems, pltpu.HBM(blocks.shape, blocks.dtype), jax.ShapeDtypeStruct((SUBLANE, 128), F32)),
        in_specs=[HBM_SPEC], out_specs=(SEM_SPEC, SEM_SPEC, HBM_SPEC, VMEM_SPEC), input_output_aliases={0: 2},
        compiler_params=pltpu.CompilerParams(has_side_effects=EFFECT),
    )(_hbm(blocks))


def _all_gather_wait(send_sems, recv_sems, blocks, after, name):
    def body(in_ref, send_sems, recv_sems, after_ref, out_ref):
        x, y, c = _place()
        one = in_ref.at[0]
        for k in range(N_DEV - 1):
            pltpu.make_async_remote_copy(
                src_ref=one, dst_ref=one, send_sem=send_sems.at[k], recv_sem=recv_sems.at[k],
                device_id=(x, y, 1 - c), device_id_type=MESH).wait_send()
            pltpu.make_async_remote_copy(
                src_ref=one, dst_ref=one, send_sem=send_sems.at[k], recv_sem=recv_sems.at[k],
                device_id=(x, y, 1 - c), device_id_type=MESH).wait_recv()

    return pl.pallas_call(
        body, name=name, out_shape=pltpu.HBM(blocks.shape, blocks.dtype),
        in_specs=[HBM_SPEC, SEM_SPEC, SEM_SPEC, ANY_SPEC], out_specs=HBM_SPEC, input_output_aliases={0: 0},
        compiler_params=pltpu.CompilerParams(has_side_effects=EFFECT),
    )(blocks, send_sems, recv_sems, after)


def _place_own(shard, kind, chip_idx, name):
    r, cdim = shard.shape
    tr, tc = _tile(r, 512), _tile(cdim, 1024)
    nrb, ncb = r // tr, cdim // tc

    def body(k_ref, s_ref, o_ref):
        o_ref[...] = s_ref[...].astype(BF16)

    if kind == "col":
        full, o_map = (r, N_CHIP * cdim), lambda i, j, kr: (i, kr[0] * ncb + j)
    else:
        full, o_map = (N_CHIP * r, cdim), lambda i, j, kr: (kr[0] * nrb + i, j)
    return pl.pallas_call(
        body, name=name, out_shape=jax.ShapeDtypeStruct(full, BF16),
        grid_spec=pltpu.PrefetchScalarGridSpec(
            num_scalar_prefetch=1, grid=(nrb, ncb),
            in_specs=[pl.BlockSpec((tr, tc), lambda i, j, kr: (i, j))],
            out_specs=pl.BlockSpec((tr, tc), o_map)),
        compiler_params=_params("parallel", "parallel"),
    )(chip_idx, shard)


def _weight_window(ref, kind, shard_shape, k, half):
    r, cdim = shard_shape
    hr = r // 2
    if kind == "col":
        return ref.at[pl.ds(half * hr, hr), pl.ds(pl.multiple_of(k * cdim, 128), cdim)]
    return ref.at[pl.ds(pl.multiple_of(k * r + half * hr, 2 * SUBLANE), hr), :]


def _target_cores(c, both_cores):
    return [(cc, cc, c) for cc in range(2)] if both_cores else [(c, 0, 0)]


def _gather_weights_start(fulls, kinds, shard_shapes, groups, after, name, both_cores=True):
    n, ng = len(fulls), len(groups)
    per = 2 if both_cores else 1

    def body(*refs):
        ins = refs[:n]
        sems = refs[n + 1:n + 1 + 2 * ng]
        token = refs[2 * n + 1 + 2 * ng]
        x, y, c = _place()
        chips = _other_chips(x, y)
        for g, group in enumerate(groups):
            for pos, (w, j) in enumerate(group):
                own = _weight_window(ins[w], kinds[w], shard_shapes[w], 2 * x + y, c)
                for cc, mine, theirs in _target_cores(c, both_cores):
                    pltpu.make_async_remote_copy(
                        src_ref=own, dst_ref=own,
                        send_sem=sems[2 * g].at[per * pos + mine], recv_sem=sems[2 * g + 1].at[per * pos + theirs],
                        device_id=(*chips[j], cc), device_id_type=MESH).start()
        token[...] = jnp.zeros_like(token)

    sem_shapes = []
    for group in groups:
        sem_shapes += [pltpu.SemaphoreType.DMA((per * len(group),))] * 2
    outs = pl.pallas_call(
        body, name=name,
        out_shape=tuple(sem_shapes) + tuple(pltpu.HBM(f.shape, f.dtype) for f in fulls)
        + (jax.ShapeDtypeStruct((SUBLANE, 128), F32),),
        in_specs=[HBM_SPEC] * n + [ANY_SPEC], out_specs=(SEM_SPEC,) * (2 * ng) + (HBM_SPEC,) * n + (VMEM_SPEC,),
        input_output_aliases={w: 2 * ng + w for w in range(n)},
        compiler_params=pltpu.CompilerParams(has_side_effects=EFFECT),
    )(*[_hbm(f) for f in fulls], after)
    sems = [(outs[2 * g], outs[2 * g + 1]) for g in range(ng)]
    return sems, list(outs[2 * ng:2 * ng + n]), outs[2 * ng + n]


def _gather_weights_wait(sems, group, fulls, kinds, shard_shapes, after, name, both_cores=True):
    n = len(fulls)
    per = 2 if both_cores else 1

    def body(*refs):
        ins = refs[:n]
        send_sems, recv_sems = refs[n], refs[n + 1]
        x, y, c = _place()
        chips = _other_chips(x, y)
        for pos, (w, j) in enumerate(group):
            own = _weight_window(ins[w], kinds[w], shard_shapes[w], 2 * x + y, c)
            for cc, mine, _ in _target_cores(c, both_cores):
                landed = _weight_window(ins[w], kinds[w], shard_shapes[w], 2 * chips[j][0] + chips[j][1], cc)
                k = per * pos + mine
                pltpu.make_async_remote_copy(
                    src_ref=own, dst_ref=own, send_sem=send_sems.at[k], recv_sem=recv_sems.at[k],
                    device_id=(*chips[j], cc), device_id_type=MESH).wait_send()
                pltpu.make_async_remote_copy(
                    src_ref=landed, dst_ref=landed, send_sem=send_sems.at[k], recv_sem=recv_sems.at[k],
                    device_id=(*chips[j], cc), device_id_type=MESH).wait_recv()

    return list(pl.pallas_call(
        body, name=name, out_shape=tuple(pltpu.HBM(f.shape, f.dtype) for f in fulls),
        in_specs=[HBM_SPEC] * n + [SEM_SPEC, SEM_SPEC, ANY_SPEC], out_specs=(HBM_SPEC,) * n,
        input_output_aliases={w: w for w in range(n)},
        compiler_params=pltpu.CompilerParams(has_side_effects=EFFECT),
    )(*fulls, sems[0], sems[1], after))


def _forward_to_sibling(full, kind, shard_shape, relations, name):
    nr = len(relations)

    def body(in_ref, out_ref, send_sems, recv_sems):
        x, y, c = _place()
        chips = _other_chips(x, y)
        copies = []
        for pos, j in enumerate(relations):
            k = 2 * chips[j][0] + chips[j][1]
            mine = _weight_window(in_ref, kind, shard_shape, k, c)
            cp = pltpu.make_async_remote_copy(
                src_ref=mine, dst_ref=_weight_window(out_ref, kind, shard_shape, k, c),
                send_sem=send_sems.at[pos], recv_sem=recv_sems.at[pos], device_id=(x, y, 1 - c), device_id_type=MESH)
            cp.start()
            copies.append(cp)
        for pos, j in enumerate(relations):
            theirs = _weight_window(out_ref, kind, shard_shape, 2 * chips[j][0] + chips[j][1], 1 - c)
            pltpu.make_async_remote_copy(
                src_ref=theirs, dst_ref=theirs, send_sem=send_sems.at[pos], recv_sem=recv_sems.at[pos],
                device_id=(x, y, 1 - c), device_id_type=MESH).wait_recv()
        for cp in copies:
            cp.wait_send()

    return pl.pallas_call(
        body, name=name, out_shape=jax.ShapeDtypeStruct(full.shape, full.dtype),
        in_specs=[HBM_SPEC], out_specs=HBM_SPEC, input_output_aliases={0: 0},
        scratch_shapes=[pltpu.SemaphoreType.DMA((nr,))] * 2,
    )(full)


def _grad_piece(ref, kind, h, cdim, k, half):
    if kind == "col":
        return ref.at[pl.ds(half * h, h), pl.ds(pl.multiple_of(k * cdim, 128), cdim)]
    return ref.at[k, pl.ds(half * h, h), :]


def _grad_dims(g, kind):
    return (g.shape[0] // 2, g.shape[1] // N_CHIP) if kind == "col" else (g.shape[1] // 2, g.shape[2])


def _reduce_start(grads, kinds, name):
    n = len(grads)
    dims = [_grad_dims(g, kind) for g, kind in zip(grads, kinds)]
    lands = [lax.empty((N_DEV - 1, h, cdim), g.dtype) for g, (h, cdim) in zip(grads, dims)]

    def body(*refs):
        g_ins, land_ins = refs[:n], refs[n:2 * n]
        send_sems, recv_sems = refs[2 * n], refs[2 * n + 1]
        token = refs[4 * n + 2]
        x, y, c = _place()
        for w in range(n):
            h, cdim = dims[w]
            base = (N_DEV - 1) * w
            pltpu.make_async_remote_copy(
                src_ref=_grad_piece(g_ins[w], kinds[w], h, cdim, 2 * x + y, 1 - c), dst_ref=land_ins[w].at[0],
                send_sem=send_sems.at[base], recv_sem=recv_sems.at[base],
                device_id=(x, y, 1 - c), device_id_type=MESH).start()
            for j, chip in enumerate(_other_chips(x, y)):
                for cc in range(2):
                    pltpu.make_async_remote_copy(
                        src_ref=_grad_piece(g_ins[w], kinds[w], h, cdim, 2 * chip[0] + chip[1], cc),
                        dst_ref=land_ins[w].at[1 + 2 * j + c],
                        send_sem=send_sems.at[base + 1 + 2 * j + cc], recv_sem=recv_sems.at[base + 1 + 2 * j + c],
                        device_id=(*chip, cc), device_id_type=MESH).start()
        token[...] = jnp.zeros_like(token)

    sems = pltpu.SemaphoreType.DMA(((N_DEV - 1) * n,))
    outs = pl.pallas_call(
        body, name=name,
        out_shape=(sems, sems) + tuple(pltpu.HBM(a.shape, a.dtype) for a in list(grads) + lands)
        + (jax.ShapeDtypeStruct((SUBLANE, 128), F32),),
        in_specs=[HBM_SPEC] * (2 * n), out_specs=(SEM_SPEC, SEM_SPEC) + (HBM_SPEC,) * (2 * n) + (VMEM_SPEC,),
        input_output_aliases={i: 2 + i for i in range(2 * n)},
        compiler_params=pltpu.CompilerParams(has_side_effects=EFFECT),
    )(*[_hbm(a) for a in list(grads) + lands])
    return outs[0], outs[1], list(outs[2:2 + n]), list(outs[2 + n:2 + 2 * n]), outs[2 + 2 * n]


def _reduce_wait(send_sems, recv_sems, grads, lands, kinds, after, name):
    n = len(grads)
    dims = [_grad_dims(g, kind) for g, kind in zip(grads, kinds)]

    def body(*refs):
        g_ins, land_ins = refs[:n], refs[n:2 * n]
        send_sems, recv_sems = refs[2 * n], refs[2 * n + 1]
        x, y, c = _place()
        for w in range(n):
            h, cdim = dims[w]
            piece = _grad_piece(g_ins[w], kinds[w], h, cdim, 2 * x + y, c)
            for s in range(N_DEV - 1):
                k = (N_DEV - 1) * w + s
                slot = land_ins[w].at[s]
                pltpu.make_async_remote_copy(
                    src_ref=piece, dst_ref=slot, send_sem=send_sems.at[k], recv_sem=recv_sems.at[k],
                    device_id=(x, y, 1 - c), device_id_type=MESH).wait_send()
                pltpu.make_async_remote_copy(
                    src_ref=piece, dst_ref=slot, send_sem=send_sems.at[k], recv_sem=recv_sems.at[k],
                    device_id=(x, y, 1 - c), device_id_type=MESH).wait_recv()

    outs = pl.pallas_call(
        body, name=name, out_shape=tuple(pltpu.HBM(a.shape, a.dtype) for a in list(grads) + list(lands)),
        in_specs=[HBM_SPEC] * (2 * n) + [SEM_SPEC, SEM_SPEC, ANY_SPEC], out_specs=(HBM_SPEC,) * (2 * n),
        input_output_aliases={i: i for i in range(2 * n)},
        compiler_params=pltpu.CompilerParams(has_side_effects=EFFECT),
    )(*grads, *lands, send_sems, recv_sems, after)
    return list(outs[:n]), list(outs[n:])


def _add_pieces(g, land, kind, chip_idx, core_idx, name):
    _, h, cdim = land.shape
    tr, tc = _tile(h, 256), _tile(cdim, 2048)
    nrb, ncb = h // tr, cdim // tc

    def body(k_ref, c_ref, g_ref, l_ref, o_ref):
        acc = g_ref[...].astype(F32)
        for s in range(N_DEV - 1):
            acc = acc + l_ref[s].astype(F32)
        o_ref[...] = acc

    if kind == "col":
        g_spec = pl.BlockSpec((tr, tc), lambda i, j, kr, cr: (cr[0] * nrb + i, kr[0] * ncb + j))
    else:
        g_spec = pl.BlockSpec((None, tr, tc), lambda i, j, kr, cr: (kr[0], cr[0] * nrb + i, j))
    return pl.pallas_call(
        body, name=name, out_shape=jax.ShapeDtypeStruct((2 * h, cdim), F32),
        grid_spec=pltpu.PrefetchScalarGridSpec(
            num_scalar_prefetch=2, grid=(nrb, ncb),
            in_specs=[g_spec, pl.BlockSpec((N_DEV - 1, tr, tc), lambda i, j, kr, cr: (0, i, j))],
            out_specs=pl.BlockSpec((tr, tc), lambda i, j, kr, cr: (cr[0] * nrb + i, j))),
        compiler_params=_params("parallel", "parallel"),
    )(chip_idx, core_idx, g, land)


def _join_halves(shards, name):
    n = len(shards)

    def body(*refs):
        ins, outs = refs[:n], refs[n:2 * n]
        send_sems, recv_sems = refs[2 * n:]
        x, y, c = _place()
        copies = []
        for w in range(n):
            h = shards[w].shape[0] // 2
            cp = pltpu.make_async_remote_copy(
                src_ref=ins[w].at[pl.ds(c * h, h), :], dst_ref=outs[w].at[pl.ds(c * h, h), :],
                send_sem=send_sems.at[w], recv_sem=recv_sems.at[w], device_id=(x, y, 1 - c), device_id_type=MESH)
            cp.start()
            copies.append(cp)
        for w in range(n):
            h = shards[w].shape[0] // 2
            theirs = outs[w].at[pl.ds((1 - c) * h, h), :]
            pltpu.make_async_remote_copy(
                src_ref=theirs, dst_ref=theirs, send_sem=send_sems.at[w], recv_sem=recv_sems.at[w],
                device_id=(x, y, 1 - c), device_id_type=MESH).wait_recv()
        for cp in copies:
            cp.wait_send()

    return pl.pallas_call(
        body, name=name, out_shape=tuple(jax.ShapeDtypeStruct(s.shape, s.dtype) for s in shards),
        in_specs=[HBM_SPEC] * n, out_specs=tuple([HBM_SPEC] * n),
        input_output_aliases={w: w for w in range(n)},
        scratch_shapes=[pltpu.SemaphoreType.DMA((n,))] * 2,
    )(*shards)


def _sum_devices(gathered, rows, name):
    n = gathered.shape[1]

    def body(g_ref, o_ref):
        acc = g_ref[0:rows, :]
        for dev in range(1, N_DEV):
            acc = acc + g_ref[dev * rows:(dev + 1) * rows, :]
        o_ref[...] = acc

    return pl.pallas_call(
        body, name=name, out_shape=jax.ShapeDtypeStruct((rows, n), F32),
        in_specs=[VMEM_SPEC], out_specs=VMEM_SPEC,
        compiler_params=pltpu.CompilerParams(vmem_limit_bytes=VMEM_LIMIT),
    )(gathered)


def _pack(vectors, width):
    flat = [v.reshape(-1) for v in vectors]
    offsets, total = [], 0
    for f in flat:
        offsets.append(total)
        total += f.shape[0]
    rows = -(-total // (width * SUBLANE)) * SUBLANE
    flat.append(jnp.zeros((rows * width - total,), F32))
    return jnp.concatenate(flat).reshape(rows, width), offsets


def kernel(x, c, mod_w, mod_b, norm_g, a_w_in, a_conv_w, a_conv_b, a_w_out, b_w_in, b_ln_g, b_ln_b, b_w_s, b_b_s, b_w_out, final_g, loss_target, m_mod_w, m_mod_b, m_norm_g, m_a_w_in, m_a_conv_w, m_a_conv_b, m_a_w_out, m_b_w_in, m_b_ln_g, m_b_ln_b, m_b_w_s, m_b_b_s, m_b_w_out, m_final_g, v_mod_w, v_mod_b, v_norm_g, v_a_w_in, v_a_conv_w, v_a_conv_b, v_a_w_out, v_b_w_in, v_b_ln_g, v_b_ln_b, v_b_w_s, v_b_b_s, v_b_w_out, v_final_g):
    seq, d = x.shape[1], x.shape[2]
    e = a_conv_b.shape[1]
    wd = mod_w.shape[2]
    ax, ay, ac = _place()
    chip = 2 * ax + ay
    dev = 2 * chip + ac
    chip_idx = jnp.reshape(chip, (1,)).astype(jnp.int32)
    core_idx = jnp.reshape(ac, (1,)).astype(jnp.int32)

    x2d = x[0]
    target = loss_target[0]

    w_names = ["a_w_in", "a_w_out", "b_w_in", "b_w_out"]
    w_kinds = ["col", "row", "col", "row"]
    w_shards = [a_w_in[0], a_w_out[0], b_w_in[0], b_w_out[0]]
    w_shapes = [sh.shape for sh in w_shards]
    placed = [_place_own(sh, kind, chip_idx, "place_" + nm) for nm, sh, kind in zip(w_names, w_shards, w_kinds)]
    a_groups, far_groups = [[(0, 0), (0, 1)]], [[(0, 2)]]
    whole_group = [[(0, 0), (0, 1), (0, 2)]]

    es = e // N_CHIP
    packed0, offs0 = _pack([c, a_conv_w, b_ln_g, b_ln_b], 1024)
    gathered0 = _all_gather(packed0, "gather_params").reshape(N_DEV, -1)
    c_all = gathered0[:, :d]
    per_chip = gathered0[0::2]

    def from_chips_cols(k, rows_):
        got = per_chip[:, offs0[k]:offs0[k] + rows_ * es].reshape(N_CHIP, rows_, es)
        return jnp.transpose(got, (1, 0, 2)).reshape(rows_, e)

    conv_w_full = from_chips_cols(1, 3)
    conv_w = [conv_w_full[k:k + 1] for k in range(3)]
    ln_g, ln_b = from_chips_cols(2, 1), from_chips_cols(3, 1)
    mod_b_cols = lax.dynamic_slice_in_dim(mod_b, chip * wd, wd, axis=1)[:, None, :]
    c_act, mod_part = _mod_fwd(c_all, mod_w, mod_b_cols, "mod_fwd")
    n_layer = mod_w.shape[0]
    mod_gathered = _all_gather(mod_part, "gather_mod")
    a_sems, (wa_in,), a_token = _gather_weights_start(
        placed[:1], w_kinds[:1], w_shapes[:1], a_groups, mod_gathered, "gather_a_w_in_start", both_cores=False)
    mod_all = mod_gathered.reshape(N_CHIP, 2, n_layer, N_DEV, wd)[:, 0]
    mod_all = jnp.transpose(mod_all, (1, 2, 0, 3)).reshape(n_layer, N_DEV, N_CHIP * wd)
    mod_me = lax.dynamic_index_in_dim(mod_all, dev, axis=1, keepdims=False)
    shift = [mod_me[l:l + 1, 0:d] for l in range(n_layer)]
    scale = [mod_me[l:l + 1, d:2 * d] for l in range(n_layer)]
    gate = [mod_me[l:l + 1, 2 * d:3 * d] for l in range(n_layer)]

    g0, g1, gf = norm_g[0:1], norm_g[1:2], final_g[None, :]

    def slab(r):
        return jnp.bitwise_xor(chip_idx, r)

    def arrived(sems, group, weight, w, after, name, both_cores=True):
        return _gather_weights_wait(sems, [(0, j) for _, j in group], [weight], [w_kinds[w]], [w_shapes[w]],
                                    after, name, both_cores)[0]

    h0, proj0 = _norm_mod_proj_slab(x2d, g0, scale[0], shift[0], wa_in, slab(0), 4, "norm_mod0_a_proj_own", a_token)
    wa_in = arrived(a_sems[0], a_groups[0], wa_in, 0, proj0, "gather_wait_near", both_cores=False)
    wa_in = _forward_to_sibling(wa_in, w_kinds[0], w_shapes[0], [0, 1], "forward_near")
    far_sems, (wa_in,), far_token = _gather_weights_start(
        [wa_in], w_kinds[:1], w_shapes[:1], far_groups, a_token, "gather_a_w_in_far_start")
    proj0 = _mm_proj_slab(h0, wa_in, proj0, slab(2), 4, "a_proj_x", far_token)
    proj0 = _mm_proj_slab(h0, wa_in, proj0, slab(1), 4, "a_proj_y", far_token)
    wa_in = arrived(far_sems[0], far_groups[0], wa_in, 0, proj0, "gather_wait_far")
    def start_whole(w, after, name):
        sems, (flight,), token = _gather_weights_start(
            placed[w:w + 1], w_kinds[w:w + 1], w_shapes[w:w + 1], whole_group, after, name)
        return sems[0], flight, token

    ao_sems, wa_out, ao_token = start_whole(1, wa_in, "gather_a_w_out_start")
    proj0 = _mm_proj_slab(h0, wa_in, proj0, slab(3), 4, "a_proj_far", ao_token)
    wa_out = arrived(ao_sems, whole_group[0], wa_out, 1, proj0, "gather_wait_a_w_out")
    bi_sems, wb_in, bi_token = start_whole(2, wa_out, "gather_b_w_in_start")
    y0, br0, x1, h1 = _conv_fwd(proj0, conv_w, a_conv_b, wa_out, x2d, gate[0], g1, scale[1], shift[1],
                                "conv_fwd", bi_token)
    wb_in = arrived(bi_sems, whole_group[0], wb_in, 2, h1, "gather_wait_b_w_in")
    bo_sems, wb_out, bo_token = start_whole(3, wb_in, "gather_b_w_out_start")
    proj1 = _mm_proj(h1, wb_in, 3, "b_proj", bo_token)
    b_s_t = jnp.transpose(b_b_s[0])
    wb_out = arrived(bo_sems, whole_group[0], wb_out, 3, proj1, "gather_wait_b_w_out")
    y1, dx2, dbr1, loss_part, g_final_g, dgate1 = _gmlp_fwd_head(
        proj1, ln_g, ln_b, b_w_s[0], b_s_t, wb_out, x1, gate[1], gf, target, "gmlp_fwd_head")

    gw_b_out = _mm_dw_out(y1, dbr1, "b_out_dw")
    dproj1, g_w_s, g_b_s_t, g_ln_g, g_ln_b = _gmlp_bwd(
        proj1, dbr1, wb_out, ln_g, ln_b, b_w_s[0], jnp.swapaxes(b_w_s[0], 1, 2), b_s_t, "gmlp_bwd")
    gw_b_in = _mm_dw_in(h1, dproj1, "b_proj_dw")
    b_kinds = ["col", "row"]
    b_send, b_recv, b_grads, b_lands, b_token = _reduce_start(
        [gw_b_in, gw_b_out.reshape(N_CHIP, e // N_CHIP, d)], b_kinds, "reduce_b_start")
    dx1, dshift1, dscale1, g_g1, dbr0, dgate0 = _mm_dh_norm_bwd(
        dproj1, wb_in, x1, dx2, g1, scale[1], "b_proj_dx", b_token, br=br0, gate=gate[0])

    early = [loss_part[0, 0:1], g_final_g, g_g1, g_ln_g, g_ln_b, jnp.transpose(g_b_s_t), g_w_s,
             jnp.concatenate([dshift1, dscale1, dgate1, dgate0], axis=1)]
    packed_e, offs_e = _pack(early, 1024)
    rows_e = packed_e.shape[0]
    blocks_e = lax.dynamic_update_slice(jnp.zeros((N_DEV, rows_e, 1024), F32), packed_e[None], (dev, 0, 0))
    e_send, e_recv, blocks_e, e_token = _all_gather_start(blocks_e, "gather_small_start")

    gw_a_out = _mm_dw_out(y0, dbr0, "a_out_dw")
    ao_send, ao_recv, ao_grads, ao_lands, ao_grad_token = _reduce_start(
        [gw_a_out.reshape(N_CHIP, e // N_CHIP, d)], ["row"], "reduce_a_out_start")
    dproj0, g_w0, g_w1, g_w2, g_conv_b = _conv_bwd(
        proj0, dbr0, wa_out, conv_w, a_conv_b, "conv_bwd", ao_grad_token + e_token)
    gw_a_in = _mm_dw_in(h0, dproj0, "a_proj_dw")
    ai_send, ai_recv, ai_grads, ai_lands, ai_token = _reduce_start([gw_a_in], ["col"], "reduce_a_in_start")
    grad_x, dshift0, dscale0, g_g0 = _mm_dh_norm_bwd(dproj0, wa_in, x2d, dx1, g0, scale[0], "a_proj_dx", ai_token)

    def finish(send, recv, grads_, lands_, kinds_, names_, after, tag):
        grads_, lands_ = _reduce_wait(send, recv, grads_, lands_, kinds_, after, "reduce_" + tag + "_wait")
        halves = [_add_pieces(g, land, kind, chip_idx, core_idx, "add_pieces_" + nm)
                  for g, land, kind, nm in zip(grads_, lands_, kinds_, names_)]
        return _join_halves(halves, "join_" + tag)

    upd, big_grads = {}, {}

    def adamw_big(nm, w, g, m, v):
        g_out, *rest = _adamw_2d(w[0], g, m[0], v[0], "adamw_" + nm)
        big_grads[nm] = g_out[None]
        upd[nm] = tuple(o[None] for o in rest)

    g_b_w_in, g_b_w_out = finish(b_send, b_recv, b_grads, b_lands, b_kinds, ["b_w_in", "b_w_out"], grad_x, "b")
    adamw_big("b_w_in", b_w_in, g_b_w_in, m_b_w_in, v_b_w_in)
    adamw_big("b_w_out", b_w_out, g_b_w_out, m_b_w_out, v_b_w_out)

    late = [g_g0, jnp.concatenate([g_w0, g_w1, g_w2], axis=0), g_conv_b, jnp.concatenate([dshift0, dscale0], axis=1)]
    packed_l, offs_l = _pack(late, 1024)
    rows_l = packed_l.shape[0]
    gathered_l = _all_gather(packed_l, "gather_small_late", after=upd["b_w_out"][0])
    blocks_e = _all_gather_wait(e_send, e_recv, blocks_e, gathered_l, "gather_small_wait")
    gathered_e = blocks_e.reshape(N_DEV * rows_e, 1024)
    summed_e = _sum_devices(gathered_e, rows_e, "sum_small_early").reshape(-1)
    summed_l = _sum_devices(gathered_l, rows_l, "sum_small_late").reshape(-1)

    def take(summed, offs, k, shape):
        size = math.prod(shape)
        return summed[offs[k]:offs[k] + size].reshape(shape)

    def rows_of(gathered, rows_, offs, k, width):
        return gathered.reshape(N_DEV, rows_ * 1024)[:, offs[k]:offs[k] + width]

    loss = take(summed_e, offs_e, 0, ())
    grad_final_g = take(summed_e, offs_e, 1, (d,))
    grad_norm_g = jnp.concatenate([take(summed_l, offs_l, 0, (1, d)), take(summed_e, offs_e, 2, (1, d))], axis=0)
    grad_ln_g_full = take(summed_e, offs_e, 3, (1, e))
    grad_ln_b_full = take(summed_e, offs_e, 4, (1, e))
    grad_b_b_s = take(summed_e, offs_e, 5, (1, GROUPS, CHUNK))
    grad_b_w_s = take(summed_e, offs_e, 6, (1, GROUPS, CHUNK, CHUNK))
    grad_conv_w_full = take(summed_l, offs_l, 1, (3, e))
    grad_a_conv_b = take(summed_l, offs_l, 2, (1, e))
    grad_a_conv_w = lax.dynamic_slice_in_dim(grad_conv_w_full, chip * es, es, axis=1)[None]
    grad_b_ln_g = lax.dynamic_slice_in_dim(grad_ln_g_full, chip * es, es, axis=1)
    grad_b_ln_b = lax.dynamic_slice_in_dim(grad_ln_b_full, chip * es, es, axis=1)
    dmod_e = rows_of(gathered_e, rows_e, offs_e, 7, 4 * d)
    dmod_l = rows_of(gathered_l, rows_l, offs_l, 3, 2 * d)
    dmod_all = jnp.stack([jnp.concatenate([dmod_l, dmod_e[:, 3 * d:]], axis=1), dmod_e[:, :3 * d]], axis=1)
    mod_b_e = take(summed_e, offs_e, 7, (4 * d,))
    mod_b_l = take(summed_l, offs_l, 3, (2 * d,))
    grad_mod_b = jnp.stack([jnp.concatenate([mod_b_l, mod_b_e[3 * d:]]), mod_b_e[:3 * d]])
    dmod_cols = jnp.transpose(lax.dynamic_slice_in_dim(dmod_all, chip * wd, wd, axis=2), (1, 0, 2))

    grad_mod_w, delta_mod_w, new_m_mod_w, new_v_mod_w = _mod_w_update(
        jnp.transpose(c_act), dmod_cols, mod_w, m_mod_w, v_mod_w, "mod_w_update")
    (g_a_w_out,) = finish(ao_send, ao_recv, ao_grads, ao_lands, ["row"], ["a_w_out"], delta_mod_w, "a_out")
    adamw_big("a_w_out", a_w_out, g_a_w_out, m_a_w_out, v_a_w_out)
    (g_a_w_in,) = finish(ai_send, ai_recv, ai_grads, ai_lands, ["col"], ["a_w_in"], upd["a_w_out"][0], "a_in")
    adamw_big("a_w_in", a_w_in, g_a_w_in, m_a_w_in, v_a_w_in)
    small_w = [("mod_b", mod_b, grad_mod_b, m_mod_b, v_mod_b), ("norm_g", norm_g, grad_norm_g, m_norm_g, v_norm_g),
               ("a_conv_w", a_conv_w, grad_a_conv_w, m_a_conv_w, v_a_conv_w),
               ("a_conv_b", a_conv_b, grad_a_conv_b, m_a_conv_b, v_a_conv_b),
               ("b_ln_g", b_ln_g, grad_b_ln_g, m_b_ln_g, v_b_ln_g), ("b_ln_b", b_ln_b, grad_b_ln_b, m_b_ln_b, v_b_ln_b),
               ("b_w_s", b_w_s, grad_b_w_s, m_b_w_s, v_b_w_s), ("b_b_s", b_b_s, grad_b_b_s, m_b_b_s, v_b_b_s),
               ("final_g", final_g, grad_final_g, m_final_g, v_final_g)]

    def flat2d(a):
        return a.reshape(-1, a.shape[-1])

    res = _adamw_small([flat2d(t[1]) for t in small_w], [flat2d(t[2]) for t in small_w],
                       [flat2d(t[3]) for t in small_w], [flat2d(t[4]) for t in small_w], "adamw_small")
    for (nm, w, _, _, _), r3 in zip(small_w, res):
        upd[nm] = tuple(o.reshape(w.shape) for o in r3)
    upd["mod_w"] = (delta_mod_w, new_m_mod_w, new_v_mod_w)

    grads = {"mod_w": grad_mod_w, "mod_b": grad_mod_b, "norm_g": grad_norm_g, "a_conv_w": grad_a_conv_w,
             "a_conv_b": grad_a_conv_b, "b_ln_g": grad_b_ln_g, "b_ln_b": grad_b_ln_b, "b_w_s": grad_b_w_s,
             "b_b_s": grad_b_b_s, "final_g": grad_final_g, **big_grads}
    order = ["mod_w", "mod_b", "norm_g", "a_w_in", "a_conv_w", "a_conv_b", "a_w_out", "b_w_in", "b_ln_g", "b_ln_b",
             "b_w_s", "b_b_s", "b_w_out", "final_g"]
    return (loss, grad_x[None], *[grads[k] for k in order], *[upd[k][0] for k in order],
            *[upd[k][1] for k in order], *[upd[k][2] for k in order])
```

```python
import math

import jax
import jax.numpy as jnp
from jax import lax
from jax.experimental import pallas as pl
from jax.experimental.pallas import tpu as pltpu

F32 = jnp.float32
BF16 = jnp.bfloat16
MESH = pl.DeviceIdType.MESH

N_DEV = 8
N_CHIP = 4
SUBLANE = 8
PACK = 16
ACT = F32
RMS_EPS = 1e-6
LN_EPS = 1e-5
CHUNK = 128
GROUPS = 8
ADAM_LR = 0.001
ADAM_B1 = 0.9
ADAM_B2 = 0.999
ADAM_EPS = 1e-08
ADAM_WD = 0.01
ADAM_STEP = 10
VMEM_LIMIT = 56 << 20

HBM_SPEC = pl.BlockSpec(memory_space=pltpu.HBM)
VMEM_SPEC = pl.BlockSpec(memory_space=pltpu.VMEM)
SEM_SPEC = pl.BlockSpec(memory_space=pltpu.SEMAPHORE)
ANY_SPEC = pl.BlockSpec(memory_space=pl.ANY)
EFFECT = pltpu.SideEffectType.DATAFLOW_SIDE_EFFECTING


def _params(*sem):
    return pltpu.CompilerParams(dimension_semantics=sem, vmem_limit_bytes=VMEM_LIMIT)


def _tile(n, want):
    if n <= want:
        return n
    t = want
    while n % t:
        t -= 128
    return t


def _sigmoid(x):
    return 0.5 * jnp.tanh(0.5 * x) + 0.5


def _silu_and_grad(x):
    s = _sigmoid(x)
    return x * s, s * (1.0 + x * (1.0 - s))


def _gelu_and_grad(x):
    cdf = 0.5 * (1.0 + lax.erf(x * (1.0 / math.sqrt(2.0))))
    pdf = jnp.exp(-0.5 * x * x) * (1.0 / math.sqrt(2.0 * math.pi))
    return x * cdf, cdf + x * pdf


def _gelu(x):
    return x * (0.5 * (1.0 + lax.erf(x * (1.0 / math.sqrt(2.0)))))


def _rms(x):
    r = lax.rsqrt(jnp.mean(x * x, axis=-1, keepdims=True) + RMS_EPS)
    return x * r, r


def _rms_bwd(dxn, xn, r):
    return r * (dxn - xn * jnp.mean(dxn * xn, axis=-1, keepdims=True))


def _colsum(a):
    return jnp.sum(a, axis=0, keepdims=True)


def _shift_down(cur, before, k):
    rolled = pltpu.roll(cur, k, 0)
    row = lax.broadcasted_iota(jnp.int32, before.shape, 0)
    head = jnp.where(row < k, pltpu.roll(before, k, 0), rolled[:SUBLANE])
    return jnp.concatenate([head, rolled[SUBLANE:]], axis=0)


def _shift_up(cur, after, k):
    n = cur.shape[0]
    rolled = pltpu.roll(cur, n - k, 0)
    row = lax.broadcasted_iota(jnp.int32, after.shape, 0)
    tail = jnp.where(row >= SUBLANE - k, pltpu.roll(after, SUBLANE - k, 0), rolled[n - SUBLANE:])
    return jnp.concatenate([rolled[:n - SUBLANE], tail], axis=0)


def _after_spec():
    return pl.BlockSpec((SUBLANE, 128), lambda *_: (0, 0))


def _mm_proj(h, w, n_split, name, after):
    s, d = h.shape
    e = w.shape[1] // n_split
    tm, tn = _tile(s, 2048), _tile(e, 2048)
    nj = e // tn

    def body(h_ref, w_ref, after_ref, o_ref):
        o_ref[...] = jnp.dot(h_ref[...], w_ref[...], preferred_element_type=F32).astype(ACT)

    return pl.pallas_call(
        body, name=name,
        out_shape=jax.ShapeDtypeStruct((n_split, s, e), ACT),
        grid=(s // tm, n_split * nj),
        in_specs=[pl.BlockSpec((tm, d), lambda i, j: (i, 0)), pl.BlockSpec((d, tn), lambda i, j: (0, j)),
                  _after_spec()],
        out_specs=pl.BlockSpec((None, tm, tn), lambda i, j: (j // nj, i, j % nj)),
        compiler_params=_params("parallel", "parallel"),
    )(h, w, after)


def _mm_proj_slab(h, w, proj, q_idx, n_split, name, after):
    s, d = h.shape
    e = w.shape[1] // n_split
    tm, tn = _tile(s, 1024), _tile(e, 2048)
    nj = e // tn

    def body(q_ref, h_ref, w_ref, after_ref, proj_ref, o_ref):
        o_ref[...] = jnp.dot(h_ref[...], w_ref[...], preferred_element_type=F32).astype(ACT)

    return pl.pallas_call(
        body, name=name,
        out_shape=jax.ShapeDtypeStruct((n_split, s, e), ACT),
        grid_spec=pltpu.PrefetchScalarGridSpec(
            num_scalar_prefetch=1, grid=(s // tm, nj),
            in_specs=[pl.BlockSpec((tm, d), lambda i, j, qr: (i, 0)),
                      pl.BlockSpec((d, tn), lambda i, j, qr: (0, qr[0] * nj + j)), _after_spec(), ANY_SPEC],
            out_specs=pl.BlockSpec((None, tm, tn), lambda i, j, qr: (qr[0], i, j))),
        input_output_aliases={4: 0},
        compiler_params=_params("parallel", "parallel"),
    )(q_idx, h, w, after, proj)


def _mm_dh_norm_bwd(dp, w, x, dx_in, g, scale, name, after, br=None, gate=None):
    nq, s, e = dp.shape
    d = w.shape[0]
    has_branch = br is not None
    tm, tk = _tile(s, 1024), _tile(e, 1024 if has_branch else 2048)
    nkq = e // tk
    nk = nq * nkq

    def body(*refs):
        if has_branch:
            (a_ref, b_ref, after_ref, x_ref, dxin_ref, g_ref, sc_ref, br_ref, gate_ref,
             dx_ref, dsh_ref, dsc_ref, dg_ref, dbr_ref, dgate_ref, acc_ref) = refs
        else:
            (a_ref, b_ref, after_ref, x_ref, dxin_ref, g_ref, sc_ref,
             dx_ref, dsh_ref, dsc_ref, dg_ref, acc_ref) = refs
        i, k = pl.program_id(0), pl.program_id(1)

        @pl.when(jnp.logical_and(i == 0, k == 0))
        def _():
            dsh_ref[...] = jnp.zeros_like(dsh_ref)
            dsc_ref[...] = jnp.zeros_like(dsc_ref)
            dg_ref[...] = jnp.zeros_like(dg_ref)
            if has_branch:
                dgate_ref[...] = jnp.zeros_like(dgate_ref)

        @pl.when(k == 0)
        def _():
            acc_ref[...] = jnp.zeros_like(acc_ref)
        acc_ref[...] += lax.dot_general(a_ref[...], b_ref[...], (((1,), (1,)), ((), ())), preferred_element_type=F32)

        @pl.when(k == nk - 1)
        def _():
            dh = acc_ref[...]
            g_vec = g_ref[...]
            xn, r = _rms(x_ref[...])
            dsh_ref[...] += _colsum(dh)
            dsc_ref[...] += _colsum(dh * (xn * g_vec))
            da = dh * (1.0 + sc_ref[...])
            dg_ref[...] += _colsum(da * xn)
            dx = dxin_ref[...] + _rms_bwd(da * g_vec, xn, r)
            dx_ref[...] = dx
            if has_branch:
                dgate_ref[...] += _colsum(dx * br_ref[...])
                dbr_ref[...] = (gate_ref[...] * dx).astype(BF16)

    rows = pl.BlockSpec((tm, d), lambda i, k: (i, 0))
    vec = pl.BlockSpec((1, d), lambda i, k: (0, 0))
    vec_out = jax.ShapeDtypeStruct((1, d), F32)
    in_specs = [pl.BlockSpec((None, tm, tk), lambda i, k: (k // nkq, i, k % nkq)),
                pl.BlockSpec((d, tk), lambda i, k: (0, k)), _after_spec(), rows, rows, vec, vec]
    out_shape = [jax.ShapeDtypeStruct((s, d), F32), vec_out, vec_out, vec_out]
    out_specs = [rows, vec, vec, vec]
    args = [dp, w, after, x, dx_in, g, scale]
    if has_branch:
        in_specs += [rows, vec]
        out_shape += [jax.ShapeDtypeStruct((s, d), BF16), vec_out]
        out_specs += [rows, vec]
        args += [br, gate]
    return pl.pallas_call(
        body, name=name, out_shape=tuple(out_shape), grid=(s // tm, nk),
        in_specs=in_specs, out_specs=tuple(out_specs), scratch_shapes=[pltpu.VMEM((tm, d), F32)],
        compiler_params=_params("arbitrary", "arbitrary"),
    )(*args)


def _dw_body(n_t):
    def body(a_ref, b_ref, o_ref, acc_ref):
        t = pl.program_id(2)

        @pl.when(t == 0)
        def _():
            acc_ref[...] = jnp.zeros_like(acc_ref)
        acc_ref[...] += lax.dot_general(a_ref[...], b_ref[...], (((0,), (0,)), ((), ())), preferred_element_type=F32)

        @pl.when(t == n_t - 1)
        def _():
            o_ref[...] = acc_ref[...].astype(o_ref.dtype)
    return body


def _mm_dw_in(h, dp, name):
    s, d = h.shape
    nq, _, e = dp.shape
    tm, tn, tt = _tile(d, 1024), _tile(e, 2048), _tile(s, 2048)
    nj = e // tn

    return pl.pallas_call(
        _dw_body(s // tt), name=name,
        out_shape=jax.ShapeDtypeStruct((d, nq * e), BF16),
        grid=(d // tm, nq * nj, s // tt),
        in_specs=[pl.BlockSpec((tt, tm), lambda i, j, t: (t, i)),
                  pl.BlockSpec((None, tt, tn), lambda i, j, t: (j // nj, t, j % nj))],
        out_specs=pl.BlockSpec((tm, tn), lambda i, j, t: (i, j)),
        scratch_shapes=[pltpu.VMEM((tm, tn), F32)],
        compiler_params=_params("parallel", "parallel", "arbitrary"),
    )(h, dp)


def _mm_dw_out(y, dbr, name):
    s, e = y.shape
    d = dbr.shape[1]
    tm, tn, tt = _tile(e, 1024), _tile(d, 1024), _tile(s, 2048)

    return pl.pallas_call(
        _dw_body(s // tt), name=name,
        out_shape=jax.ShapeDtypeStruct((e, d), BF16),
        grid=(e // tm, d // tn, s // tt),
        in_specs=[pl.BlockSpec((tt, tm), lambda i, j, t: (t, i)), pl.BlockSpec((tt, tn), lambda i, j, t: (t, j))],
        out_specs=pl.BlockSpec((tm, tn), lambda i, j, t: (i, j)),
        scratch_shapes=[pltpu.VMEM((tm, tn), F32)],
        compiler_params=_params("parallel", "parallel", "arbitrary"),
    )(y, dbr)


def _row_spec(ts, d):
    return pl.BlockSpec((ts, d), lambda i: (i, 0))


def _vec_spec(d):
    return pl.BlockSpec((1, d), lambda i: (0, 0))


def _norm_mod_proj_slab(x, g, scale, shift, w, q_idx, n_split, name, after):
    s, d = x.shape
    e = w.shape[1] // n_split
    tm = _tile(s, 1024)

    def body(q_ref, x_ref, g_ref, sc_ref, sh_ref, w_ref, after_ref, h_ref, o_ref):
        xn, _ = _rms(x_ref[...])
        h = ((xn * g_ref[...]) * (1.0 + sc_ref[...]) + sh_ref[...]).astype(BF16)
        h_ref[...] = h
        o_ref[...] = jnp.dot(h, w_ref[...], preferred_element_type=F32).astype(ACT)

    vec = pl.BlockSpec((1, d), lambda i, qr: (0, 0))
    rows = pl.BlockSpec((tm, d), lambda i, qr: (i, 0))
    return pl.pallas_call(
        body, name=name,
        out_shape=(jax.ShapeDtypeStruct((s, d), BF16), jax.ShapeDtypeStruct((n_split, s, e), ACT)),
        grid_spec=pltpu.PrefetchScalarGridSpec(
            num_scalar_prefetch=1, grid=(s // tm,),
            in_specs=[rows, vec, vec, vec, pl.BlockSpec((d, e), lambda i, qr: (0, qr[0])), _after_spec()],
            out_specs=(rows, pl.BlockSpec((None, tm, e), lambda i, qr: (qr[0], i, 0)))),
        compiler_params=_params("parallel"),
    )(q_idx, x, g, scale, shift, w, after)


def _head_tile(x1, br, gate, gf, target):
    d = x1.shape[-1]
    xn, r = _rms(x1 + gate * br)
    err = xn * gf - target
    loss = 0.5 * jnp.sum(jnp.mean(err * err, axis=-1, keepdims=True))
    dout = err * (1.0 / d)
    dx = _rms_bwd(dout * gf, xn, r)
    return dx, gate * dx, loss, _colsum(dout * xn), _colsum(dx * br)


def _f32(ref):
    return ref[...].astype(F32)


def _cols_f32(ref, cols):
    return ref[:, cols].astype(F32)


def _rows_before(halo_ref, cols=slice(None)):
    return _cols_f32(halo_ref, cols)[PACK - SUBLANE:]


def _rows_after(halo_ref, cols=slice(None)):
    return _cols_f32(halo_ref, cols)[:SUBLANE]


CONV_CHAIN_COLS = 512


def _conv_fwd(proj, conv_w, conv_b, w_out, x, gate, g_next, scale_next, shift_next, name, after):
    _, s, e = proj.shape
    d = w_out.shape[1]
    ts, te = _tile(s, 256), e
    cw = min(CONV_CHAIN_COLS, te)
    hb = ts // PACK

    def body(bg_ref, cg_ref, xi_ref, z_ref, cgp_ref, xip_ref, w0_ref, w1_ref, w2_ref, b_ref, wo_ref, after_ref,
             x_ref, gate_ref, g_ref, sc_ref, sh_ref, y_ref, br_ref, x1_ref, h_ref):
        first = pl.program_id(0) == 0
        br = None
        for c0 in range(0, te, cw):
            cols = slice(c0, c0 + cw)
            cx = _cols_f32(cg_ref, cols) * _cols_f32(xi_ref, cols)
            before = jnp.where(first, 0.0, _rows_before(cgp_ref, cols) * _rows_before(xip_ref, cols))
            conv = b_ref[:, cols] + w2_ref[:, cols] * cx
            conv = conv + w0_ref[:, cols] * _shift_down(cx, before, 2)
            conv = conv + w1_ref[:, cols] * _shift_down(cx, before, 1)
            z = _cols_f32(z_ref, cols)
            y = ((z * _sigmoid(z)) * _cols_f32(bg_ref, cols) * conv).astype(BF16)
            y_ref[:, cols] = y
            part_br = jnp.dot(y, wo_ref[cols, :], preferred_element_type=F32)
            br = part_br if br is None else br + part_br
        br_ref[...] = br
        x1 = x_ref[...] + gate_ref[...] * br
        x1_ref[...] = x1
        xn, _ = _rms(x1)
        h_ref[...] = ((xn * g_ref[...]) * (1.0 + sc_ref[...]) + sh_ref[...]).astype(BF16)

    def part(q):
        return pl.BlockSpec((None, ts, te), lambda i: (q, i, 0))

    def halo_before(q):
        return pl.BlockSpec((None, PACK, te), lambda i: (q, jnp.maximum(i * hb - 1, 0), 0))

    return pl.pallas_call(
        body, name=name,
        out_shape=(jax.ShapeDtypeStruct((s, e), BF16), jax.ShapeDtypeStruct((s, d), F32),
                   jax.ShapeDtypeStruct((s, d), F32), jax.ShapeDtypeStruct((s, d), BF16)), grid=(s // ts,),
        in_specs=[part(0), part(1), part(2), part(3), halo_before(1), halo_before(2)]
        + [_vec_spec(e)] * 4 + [pl.BlockSpec((e, d), lambda i: (0, 0)), _after_spec(), _row_spec(ts, d)]
        + [_vec_spec(d)] * 4,
        out_specs=(_row_spec(ts, e), _row_spec(ts, d), _row_spec(ts, d), _row_spec(ts, d)),
        compiler_params=_params("parallel"),
    )(proj, proj, proj, proj, proj, proj, *conv_w, conv_b, w_out, after, x, gate, g_next, scale_next, shift_next)


def _conv_bwd(proj, dbr, w_out, conv_w, conv_b, name, after):
    _, s, e = proj.shape
    d = dbr.shape[1]
    ts, te = _tile(s, 512), _tile(e, 1024)
    cw = min(CONV_CHAIN_COLS // 2, te)
    hb = ts // PACK
    n_i = s // ts
    last_halo = s // PACK - 1
    nt = (((1,), (1,)), ((), ()))

    def body(bg_ref, cg_ref, xi_ref, z_ref, dbr_ref, cgp_ref, xip_ref, bgn_ref, zn_ref, dbrn_ref, wo_ref,
             w0_ref, w1_ref, w2_ref, b_ref, after_ref, dp_ref, dw0_ref, dw1_ref, dw2_ref, db_ref):
        i = pl.program_id(1)

        @pl.when(i == 0)
        def _():
            for acc in (dw0_ref, dw1_ref, dw2_ref, db_ref):
                acc[...] = jnp.zeros_like(acc)
        for c0 in range(0, te, cw):
            cols = slice(c0, c0 + cw)
            wo = wo_ref[cols, :]
            dy = lax.dot_general(dbr_ref[...], wo, nt, preferred_element_type=F32)
            dyn = lax.dot_general(dbrn_ref[...], wo, nt, preferred_element_type=F32)[:SUBLANE]
            bg, cg = _cols_f32(bg_ref, cols), _cols_f32(cg_ref, cols)
            xi, z = _cols_f32(xi_ref, cols), _cols_f32(z_ref, cols)
            w0, w1, w2 = w0_ref[:, cols], w1_ref[:, cols], w2_ref[:, cols]
            cx = cg * xi
            before = jnp.where(i > 0, _rows_before(cgp_ref, cols) * _rows_before(xip_ref, cols), 0.0)
            cx1 = _shift_down(cx, before, 1)
            cx2 = _shift_down(cx, before, 2)
            conv = b_ref[:, cols] + w2 * cx
            conv = conv + w0 * cx2
            conv = conv + w1 * cx1
            sz, dsz = _silu_and_grad(z)
            dp_ref[3, :, cols] = (dy * bg * conv * dsz).astype(BF16)
            dp_ref[0, :, cols] = (dy * sz * conv).astype(BF16)
            dconv = dy * sz * bg
            zn = _rows_after(zn_ref, cols)
            after = jnp.where(i < n_i - 1, dyn * (zn * _sigmoid(zn)) * _rows_after(bgn_ref, cols), 0.0)
            db_ref[:, cols] += _colsum(dconv)
            dw2_ref[:, cols] += _colsum(dconv * cx)
            dw1_ref[:, cols] += _colsum(dconv * cx1)
            dw0_ref[:, cols] += _colsum(dconv * cx2)
            dcx = w2 * dconv + w1 * _shift_up(dconv, after, 1) + w0 * _shift_up(dconv, after, 2)
            dp_ref[1, :, cols] = (dcx * xi).astype(BF16)
            dp_ref[2, :, cols] = (dcx * cg).astype(BF16)

    def part(q):
        return pl.BlockSpec((None, ts, te), lambda j, i: (q, i, j))

    def halo_before(q):
        return pl.BlockSpec((None, PACK, te), lambda j, i: (q, jnp.maximum(i * hb - 1, 0), j))

    def halo_after(q):
        return pl.BlockSpec((None, PACK, te), lambda j, i: (q, jnp.minimum((i + 1) * hb, last_halo), j))

    return pl.pallas_call(
        body, name=name,
        out_shape=(jax.ShapeDtypeStruct((4, s, e), BF16),) + (jax.ShapeDtypeStruct((1, e), F32),) * 4,
        grid=(e // te, n_i),
        in_specs=[part(0), part(1), part(2), part(3), pl.BlockSpec((ts, d), lambda j, i: (i, 0)),
                  halo_before(1), halo_before(2), halo_after(0), halo_after(3),
                  pl.BlockSpec((PACK, d), lambda j, i: (jnp.minimum((i + 1) * hb, last_halo), 0)),
                  pl.BlockSpec((te, d), lambda j, i: (j, 0))]
        + [pl.BlockSpec((1, te), lambda j, i: (0, j))] * 4 + [_after_spec()],
        out_specs=(pl.BlockSpec((4, ts, te), lambda j, i: (0, i, j)),) + (pl.BlockSpec((1, te), lambda j, i: (0, j)),) * 4,
        compiler_params=_params("parallel", "arbitrary"),
    )(proj, proj, proj, proj, dbr, proj, proj, proj, proj, dbr, w_out, *conv_w, conv_b, after)


def _tril(w):
    row = lax.broadcasted_iota(jnp.int32, w.shape, 0)
    col = lax.broadcasted_iota(jnp.int32, w.shape, 1)
    return jnp.where(row >= col, w, 0.0)


def _triu(w):
    row = lax.broadcasted_iota(jnp.int32, w.shape, 0)
    col = lax.broadcasted_iota(jnp.int32, w.shape, 1)
    return jnp.where(row <= col, w, 0.0)


def _layer_norm_fwd(v, g, b):
    mu = jnp.mean(v, axis=-1, keepdims=True)
    vc = v - mu
    rstd = lax.rsqrt(jnp.mean(vc * vc, axis=-1, keepdims=True) + LN_EPS)
    vhat = vc * rstd
    return vhat * g + b, vhat, rstd


GMLP_CHUNKS_PER_STEP = 2


def _gmlp_rows(s):
    return CHUNK * min(GMLP_CHUNKS_PER_STEP, s // CHUNK)


def _mix_positions(w_ref, src_scr, dst_scr, gw, mask, bias_ref=None):
    for gi in range(GROUPS):
        cols = slice(gi * gw, (gi + 1) * gw)
        wm = mask(w_ref[gi]).astype(BF16)
        for n in range(src_scr.shape[0] // CHUNK):
            rows = slice(n * CHUNK, (n + 1) * CHUNK)
            out = jnp.dot(wm, src_scr[rows, cols], preferred_element_type=F32)
            if bias_ref is not None:
                out = out + bias_ref[:, gi:gi + 1]
            dst_scr[rows, cols] = out


def _gmlp_fwd_head(proj, ln_g, ln_b, w_s, b_s_t, w_out, x1, gate, gf, target, name):
    _, s, e = proj.shape
    d = w_out.shape[1]
    gw = e // GROUPS
    ts = CHUNK * min(2 * GMLP_CHUNKS_PER_STEP, s // CHUNK)

    def body(pu_ref, pv_ref, pz_ref, g_ref, b_ref, ws_ref, bs_ref, wo_ref, x1_ref, gate_ref, gf_ref, tg_ref,
             y_ref, dx_ref, dbr_ref, loss_ref, dgf_ref, dgate_ref, vn_scr, mix_scr):
        @pl.when(pl.program_id(0) == 0)
        def _():
            loss_ref[...] = jnp.zeros_like(loss_ref)
            dgf_ref[...] = jnp.zeros_like(dgf_ref)
            dgate_ref[...] = jnp.zeros_like(dgate_ref)
        vn, _, _ = _layer_norm_fwd(_gelu(_f32(pv_ref)), g_ref[...], b_ref[...])
        vn_scr[...] = vn.astype(BF16)
        _mix_positions(ws_ref, vn_scr, mix_scr, gw, _tril, bs_ref)
        z = _f32(pz_ref)
        y = ((z * _sigmoid(z)) * (_gelu(_f32(pu_ref)) * mix_scr[...])).astype(BF16)
        y_ref[...] = y
        br = jnp.dot(y, wo_ref[...], preferred_element_type=F32)
        dx, dbr, loss, dgf, dgate = _head_tile(x1_ref[...], br, gate_ref[...], gf_ref[...], tg_ref[...])
        dx_ref[...] = dx
        dbr_ref[...] = dbr.astype(BF16)
        loss_ref[...] += loss
        dgf_ref[...] += dgf
        dgate_ref[...] += dgate

    def part(q):
        return pl.BlockSpec((None, ts, e), lambda i: (q, i, 0))

    return pl.pallas_call(
        body, name=name,
        out_shape=(jax.ShapeDtypeStruct((s, e), BF16), jax.ShapeDtypeStruct((s, d), F32),
                   jax.ShapeDtypeStruct((s, d), BF16), jax.ShapeDtypeStruct((SUBLANE, 128), F32),
                   jax.ShapeDtypeStruct((1, d), F32), jax.ShapeDtypeStruct((1, d), F32)),
        grid=(s // ts,),
        in_specs=[part(0), part(1), part(2), _vec_spec(e), _vec_spec(e),
                  pl.BlockSpec((GROUPS, CHUNK, CHUNK), lambda i: (0, 0, 0), pipeline_mode=pl.Buffered(1)),
                  pl.BlockSpec((CHUNK, GROUPS), lambda i: (0, 0)),
                  pl.BlockSpec((e, d), lambda i: (0, 0), pipeline_mode=pl.Buffered(1)),
                  _row_spec(ts, d), _vec_spec(d), _vec_spec(d), _row_spec(ts, d)],
        out_specs=(_row_spec(ts, e), _row_spec(ts, d), _row_spec(ts, d),
                   pl.BlockSpec((SUBLANE, 128), lambda i: (0, 0)), _vec_spec(d), _vec_spec(d)),
        scratch_shapes=[pltpu.VMEM((ts, e), BF16), pltpu.VMEM((ts, e), F32)],
        compiler_params=_params("arbitrary"),
    )(proj, proj, proj, ln_g, ln_b, w_s, b_s_t, w_out, x1, gate, gf, target)


def _gmlp_bwd(proj, dbr, w_out, ln_g, ln_b, w_s, w_s_t, b_s_t, name):
    _, s, e = proj.shape
    d = dbr.shape[1]
    gw = e // GROUPS
    ts = _gmlp_rows(s)
    n_i = s // ts

    def body(pu_ref, pv_ref, pz_ref, dbr_ref, wo_ref, g_ref, b_ref, ws_ref, wst_ref, bs_ref,
             dp_ref, dws_ref, dbs_ref, dlg_ref, dlb_ref, vn_scr, mix_scr, dm_scr, dvn_scr, dmacc_scr):
        i = pl.program_id(0)

        @pl.when(i == 0)
        def _():
            dws_ref[...] = jnp.zeros_like(dws_ref)
            dlg_ref[...] = jnp.zeros_like(dlg_ref)
            dlb_ref[...] = jnp.zeros_like(dlb_ref)
            dmacc_scr[...] = jnp.zeros_like(dmacc_scr)
        ln_g = g_ref[...]
        u, du_dpu = _gelu_and_grad(_f32(pu_ref))
        v, dv_dpv = _gelu_and_grad(_f32(pv_ref))
        vn, vhat, rstd = _layer_norm_fwd(v, ln_g, b_ref[...])
        vn_scr[...] = vn.astype(BF16)
        _mix_positions(ws_ref, vn_scr, mix_scr, gw, _tril, bs_ref)
        mixed = mix_scr[...]
        dy = lax.dot_general(dbr_ref[...], wo_ref[...], (((1,), (1,)), ((), ())), preferred_element_type=F32)
        sz, dsz = _silu_and_grad(_f32(pz_ref))
        ds = dy * sz
        dp_ref[2] = (dy * (u * mixed) * dsz).astype(BF16)
        dp_ref[0] = (ds * mixed * du_dpu).astype(BF16)
        dm = ds * u
        dm_scr[...] = dm.astype(BF16)
        for n in range(ts // CHUNK):
            rows = slice(n * CHUNK, (n + 1) * CHUNK)
            dmacc_scr[...] += dm[rows]
            for gi in range(GROUPS):
                cols = slice(gi * gw, (gi + 1) * gw)
                dws_ref[gi] += lax.dot_general(dm_scr[rows, cols], vn_scr[rows, cols], (((1,), (1,)), ((), ())),
                                               preferred_element_type=F32)
        _mix_positions(wst_ref, dm_scr, dvn_scr, gw, _triu)
        dvn = dvn_scr[...]
        dlg_ref[...] += _colsum(dvn * vhat)
        dlb_ref[...] += _colsum(dvn)
        dvh = dvn * ln_g
        dv = rstd * (dvh - jnp.mean(dvh, axis=-1, keepdims=True) - vhat * jnp.mean(dvh * vhat, axis=-1, keepdims=True))
        dp_ref[1] = (dv * dv_dpv).astype(BF16)

        @pl.when(i == n_i - 1)
        def _():
            for gi in range(GROUPS):
                dws_ref[gi] = _tril(dws_ref[gi])
                dbs_ref[:, gi:gi + 1] = jnp.sum(dmacc_scr[:, gi * gw:(gi + 1) * gw], axis=1, keepdims=True)

    def part(q):
        return pl.BlockSpec((None, ts, e), lambda i: (q, i, 0))

    w_spec = pl.BlockSpec((GROUPS, CHUNK, CHUNK), lambda i: (0, 0, 0))
    bs_spec = pl.BlockSpec((CHUNK, GROUPS), lambda i: (0, 0))
    return pl.pallas_call(
        body, name=name,
        out_shape=(jax.ShapeDtypeStruct((3, s, e), BF16), jax.ShapeDtypeStruct((GROUPS, CHUNK, CHUNK), F32),
                   jax.ShapeDtypeStruct((CHUNK, GROUPS), F32), jax.ShapeDtypeStruct((1, e), F32),
                   jax.ShapeDtypeStruct((1, e), F32)),
        grid=(n_i,),
        in_specs=[part(0), part(1), part(2), pl.BlockSpec((ts, d), lambda i: (i, 0)),
                  pl.BlockSpec((e, d), lambda i: (0, 0)), _vec_spec(e), _vec_spec(e), w_spec, w_spec, bs_spec],
        out_specs=(pl.BlockSpec((3, ts, e), lambda i: (0, i, 0)), w_spec, bs_spec, _vec_spec(e), _vec_spec(e)),
        scratch_shapes=[pltpu.VMEM((ts, e), BF16), pltpu.VMEM((ts, e), F32), pltpu.VMEM((ts, e), BF16),
                        pltpu.VMEM((ts, e), F32), pltpu.VMEM((CHUNK, e), F32)],
        compiler_params=_params("arbitrary"),
    )(proj, proj, proj, dbr, w_out, ln_g, ln_b, w_s, w_s_t, b_s_t)


def _mod_fwd(c_all, mod_w, mod_b_cols, name):
    n_layer, d, w = mod_w.shape

    def body(c_ref, w_ref, b_ref, ca_ref, o_ref):
        c = c_ref[...]
        ca = c * _sigmoid(c)
        ca_ref[...] = ca
        for li in range(n_layer):
            o_ref[li * N_DEV:(li + 1) * N_DEV, :] = (
                jnp.dot(ca, w_ref[li], preferred_element_type=F32, precision=lax.Precision.HIGHEST) + b_ref[li])

    return pl.pallas_call(
        body, name=name,
        out_shape=(jax.ShapeDtypeStruct((N_DEV, d), F32), jax.ShapeDtypeStruct((n_layer * N_DEV, w), F32)),
        in_specs=[VMEM_SPEC] * 3, out_specs=(VMEM_SPEC, VMEM_SPEC),
        compiler_params=pltpu.CompilerParams(vmem_limit_bytes=VMEM_LIMIT),
    )(c_all, mod_w, mod_b_cols)


def _adamw(w, g, m, v):
    m = ADAM_B1 * m + (1.0 - ADAM_B1) * g
    v = ADAM_B2 * v + (1.0 - ADAM_B2) * (g * g)
    m_hat = m / (1.0 - ADAM_B1 ** ADAM_STEP)
    v_hat = v / (1.0 - ADAM_B2 ** ADAM_STEP)
    delta = -ADAM_LR * (m_hat / (jnp.sqrt(v_hat) + ADAM_EPS) + ADAM_WD * w)
    return delta, m, v


def _adamw_2d(w, g, m, v, name):
    r, c = w.shape
    tr, tc = _tile(r, 512), _tile(c, 1024)

    def body(w_ref, g_ref, m_ref, v_ref, go_ref, d_ref, nm_ref, nv_ref):
        g = g_ref[...]
        go_ref[...] = g
        d_ref[...], nm_ref[...], nv_ref[...] = _adamw(w_ref[...], g, m_ref[...], v_ref[...])

    spec = pl.BlockSpec((tr, tc), lambda i, j: (i, j))
    shape = jax.ShapeDtypeStruct((r, c), F32)
    return pl.pallas_call(
        body, name=name, out_shape=(shape,) * 4, grid=(r // tr, c // tc),
        in_specs=[spec] * 4, out_specs=(spec,) * 4, compiler_params=_params("parallel", "parallel"),
    )(w, g, m, v)


def _mod_w_update(ca_t, dmod_cols, w, m, v, name):
    n_layer, d, wd = w.shape
    tr = _tile(d, 256)

    def body(ca_ref, dm_ref, w_ref, m_ref, v_ref, g_ref, d_ref, nm_ref, nv_ref):
        ca = ca_ref[...]
        dm = dm_ref[...]
        g = ca[:, 0:1] * dm[0:1, :]
        for b in range(1, N_DEV):
            g = g + ca[:, b:b + 1] * dm[b:b + 1, :]
        g_ref[...] = g
        d_ref[...], nm_ref[...], nv_ref[...] = _adamw(w_ref[...], g, m_ref[...], v_ref[...])

    spec = pl.BlockSpec((None, tr, wd), lambda l, i: (l, i, 0))
    shape = jax.ShapeDtypeStruct((n_layer, d, wd), F32)
    return pl.pallas_call(
        body, name=name, out_shape=(shape,) * 4, grid=(n_layer, d // tr),
        in_specs=[pl.BlockSpec((tr, N_DEV), lambda l, i: (i, 0)), pl.BlockSpec((None, N_DEV, wd), lambda l, i: (l, 0, 0)),
                  spec, spec, spec],
        out_specs=(spec,) * 4, compiler_params=_params("parallel", "parallel"),
    )(ca_t, dmod_cols, w, m, v)


def _adamw_small(ws, gs, ms, vs, name):
    n = len(ws)

    def body(*refs):
        ins, outs = refs[:4 * n], refs[4 * n:]
        for k in range(n):
            delta, nm, nv = _adamw(ins[k][...], ins[n + k][...], ins[2 * n + k][...], ins[3 * n + k][...])
            outs[3 * k][...] = delta
            outs[3 * k + 1][...] = nm
            outs[3 * k + 2][...] = nv

    out_shape = []
    for w in ws:
        out_shape += [jax.ShapeDtypeStruct(w.shape, F32)] * 3
    outs = pl.pallas_call(
        body, name=name, out_shape=tuple(out_shape),
        in_specs=[VMEM_SPEC] * (4 * n), out_specs=tuple([VMEM_SPEC] * (3 * n)),
        compiler_params=pltpu.CompilerParams(vmem_limit_bytes=VMEM_LIMIT),
    )(*ws, *gs, *ms, *vs)
    return [tuple(outs[3 * k:3 * k + 3]) for k in range(n)]


def _place():
    return lax.axis_index("x"), lax.axis_index("y"), lax.axis_index("c")


def _other_chips(x, y):
    return [(1 - x, y), (x, 1 - y), (1 - x, 1 - y)]


def _all_gather(block, name, after=None):
    m_per, n = block.shape

    def body(x_ref, *rest):
        out_ref, send_sems, recv_sems, local_sem = rest[-4:]
        x, y, c = _place()
        me, sibling = (x, y, c), (x, y, 1 - c)
        chips = _other_chips(x, y)

        def rows(px, py, pc):
            return out_ref.at[pl.ds((4 * px + 2 * py + pc) * m_per, m_per), :]

        def copy(k, blk, to, src=None):
            return pltpu.make_async_remote_copy(
                src_ref=rows(*blk) if src is None else src, dst_ref=rows(*blk),
                send_sem=send_sems.at[k], recv_sem=recv_sems.at[k], device_id=to, device_id_type=MESH)

        mine = pltpu.make_async_copy(x_ref, rows(*me), local_sem)
        mine.start()
        first = [copy(0, me, sibling, src=x_ref)]
        first += [copy(1 + j, me, (*chip, c), src=x_ref) for j, chip in enumerate(chips)]
        for cp in first:
            cp.start()
        passed = [copy(4 + j, (*chip, c), sibling) for j, chip in enumerate(chips)]
        for j, chip in enumerate(chips):
            copy(1 + j, (*chip, c), me).wait_recv()
            passed[j].start()
        copy(0, sibling, me).wait_recv()
        for j, chip in enumerate(chips):
            copy(4 + j, (*chip, 1 - c), me).wait_recv()
        for cp in first + passed:
            cp.wait_send()
        mine.wait()

    return pl.pallas_call(
        body, name=name, out_shape=jax.ShapeDtypeStruct((N_DEV * m_per, n), F32),
        in_specs=[VMEM_SPEC] + ([] if after is None else [ANY_SPEC]), out_specs=VMEM_SPEC,
        scratch_shapes=[pltpu.SemaphoreType.DMA((7,)), pltpu.SemaphoreType.DMA((7,)), pltpu.SemaphoreType.DMA],
        compiler_params=pltpu.CompilerParams(vmem_limit_bytes=VMEM_LIMIT),
    )(*([block] if after is None else [block, after]))


def _hbm(a):
    return pltpu.with_memory_space_constraint(a, pltpu.HBM)


def _all_gather_start(blocks, name):
    def body(in_ref, send_sems, recv_sems, thru_ref, token):
        token[...] = jnp.zeros_like(token)
        x, y, c = _place()
        mine = in_ref.at[4 * x + 2 * y + c]
        pltpu.make_async_remote_copy(
            src_ref=mine, dst_ref=mine, send_sem=send_sems.at[0], recv_sem=recv_sems.at[0],
            device_id=(x, y, 1 - c), device_id_type=MESH).start()
        for j, chip in enumerate(_other_chips(x, y)):
            for cc in range(2):
                pltpu.make_async_remote_copy(
                    src_ref=mine, dst_ref=mine, send_sem=send_sems.at[1 + 2 * j + cc],
                    recv_sem=recv_sems.at[1 + 2 * j + c], device_id=(*chip, cc), device_id_type=MESH).start()

    sems = pltpu.SemaphoreType.DMA((N_DEV - 1,))
    return pl.pallas_call(
        body, name=name,
        out_shape=(sems, sems, pltpu.HBM(blocks.shape, blocks.dtype), jax.ShapeDtypeStruct((SUBLANE, 128), F32)),
        in_specs=[HBM_SPEC], out_specs=(SEM_SPEC, SEM_SPEC, HBM_SPEC, VMEM_SPEC), input_output_aliases={0: 2},
        compiler_params=pltpu.CompilerParams(has_side_effects=EFFECT),
    )(_hbm(blocks))


def _all_gather_wait(send_sems, recv_sems, blocks, after, name):
    def body(in_ref, send_sems, recv_sems, after_ref, out_ref):
        x, y, c = _place()
        one = in_ref.at[0]
        for k in range(N_DEV - 1):
            pltpu.make_async_remote_copy(
                src_ref=one, dst_ref=one, send_sem=send_sems.at[k], recv_sem=recv_sems.at[k],
                device_id=(x, y, 1 - c), device_id_type=MESH).wait_send()
            pltpu.make_async_remote_copy(
                src_ref=one, dst_ref=one, send_sem=send_sems.at[k], recv_sem=recv_sems.at[k],
                device_id=(x, y, 1 - c), device_id_type=MESH).wait_recv()

    return pl.pallas_call(
        body, name=name, out_shape=pltpu.HBM(blocks.shape, blocks.dtype),
        in_specs=[HBM_SPEC, SEM_SPEC, SEM_SPEC, ANY_SPEC], out_specs=HBM_SPEC, input_output_aliases={0: 0},
        compiler_params=pltpu.CompilerParams(has_side_effects=EFFECT),
    )(blocks, send_sems, recv_sems, after)


def _place_own(shard, kind, chip_idx, name):
    r, cdim = shard.shape
    tr, tc = _tile(r, 512), _tile(cdim, 1024)
    nrb, ncb = r // tr, cdim // tc

    def body(k_ref, s_ref, o_ref):
        o_ref[...] = s_ref[...].astype(BF16)

    if kind == "col":
        full, o_map = (r, N_CHIP * cdim), lambda i, j, kr: (i, kr[0] * ncb + j)
    else:
        full, o_map = (N_CHIP * r, cdim), lambda i, j, kr: (kr[0] * nrb + i, j)
    return pl.pallas_call(
        body, name=name, out_shape=jax.ShapeDtypeStruct(full, BF16),
        grid_spec=pltpu.PrefetchScalarGridSpec(
            num_scalar_prefetch=1, grid=(nrb, ncb),
            in_specs=[pl.BlockSpec((tr, tc), lambda i, j, kr: (i, j))],
            out_specs=pl.BlockSpec((tr, tc), o_map)),
        compiler_params=_params("parallel", "parallel"),
    )(chip_idx, shard)


def _weight_window(ref, kind, shard_shape, k, half):
    r, cdim = shard_shape
    hr = r // 2
    if kind == "col":
        return ref.at[pl.ds(half * hr, hr), pl.ds(pl.multiple_of(k * cdim, 128), cdim)]
    return ref.at[pl.ds(pl.multiple_of(k * r + half * hr, 2 * SUBLANE), hr), :]


def _target_cores(c, both_cores):
    return [(cc, cc, c) for cc in range(2)] if both_cores else [(c, 0, 0)]


def _gather_weights_start(fulls, kinds, shard_shapes, groups, after, name, both_cores=True):
    n, ng = len(fulls), len(groups)
    per = 2 if both_cores else 1

    def body(*refs):
        ins = refs[:n]
        sems = refs[n + 1:n + 1 + 2 * ng]
        token = refs[2 * n + 1 + 2 * ng]
        x, y, c = _place()
        chips = _other_chips(x, y)
        for g, group in enumerate(groups):
            for pos, (w, j) in enumerate(group):
                own = _weight_window(ins[w], kinds[w], shard_shapes[w], 2 * x + y, c)
                for cc, mine, theirs in _target_cores(c, both_cores):
                    pltpu.make_async_remote_copy(
                        src_ref=own, dst_ref=own,
                        send_sem=sems[2 * g].at[per * pos + mine], recv_sem=sems[2 * g + 1].at[per * pos + theirs],
                        device_id=(*chips[j], cc), device_id_type=MESH).start()
        token[...] = jnp.zeros_like(token)

    sem_shapes = []
    for group in groups:
        sem_shapes += [pltpu.SemaphoreType.DMA((per * len(group),))] * 2
    outs = pl.pallas_call(
        body, name=name,
        out_shape=tuple(sem_shapes) + tuple(pltpu.HBM(f.shape, f.dtype) for f in fulls)
        + (jax.ShapeDtypeStruct((SUBLANE, 128), F32),),
        in_specs=[HBM_SPEC] * n + [ANY_SPEC], out_specs=(SEM_SPEC,) * (2 * ng) + (HBM_SPEC,) * n + (VMEM_SPEC,),
        input_output_aliases={w: 2 * ng + w for w in range(n)},
        compiler_params=pltpu.CompilerParams(has_side_effects=EFFECT),
    )(*[_hbm(f) for f in fulls], after)
    sems = [(outs[2 * g], outs[2 * g + 1]) for g in range(ng)]
    return sems, list(outs[2 * ng:2 * ng + n]), outs[2 * ng + n]


def _gather_weights_wait(sems, group, fulls, kinds, shard_shapes, after, name, both_cores=True):
    n = len(fulls)
    per = 2 if both_cores else 1

    def body(*refs):
        ins = refs[:n]
        send_sems, recv_sems = refs[n], refs[n + 1]
        x, y, c = _place()
        chips = _other_chips(x, y)
        for pos, (w, j) in enumerate(group):
            own = _weight_window(ins[w], kinds[w], shard_shapes[w], 2 * x + y, c)
            for cc, mine, _ in _target_cores(c, both_cores):
                landed = _weight_window(ins[w], kinds[w], shard_shapes[w], 2 * chips[j][0] + chips[j][1], cc)
                k = per * pos + mine
                pltpu.make_async_remote_copy(
                    src_ref=own, dst_ref=own, send_sem=send_sems.at[k], recv_sem=recv_sems.at[k],
                    device_id=(*chips[j], cc), device_id_type=MESH).wait_send()
                pltpu.make_async_remote_copy(
                    src_ref=landed, dst_ref=landed, send_sem=send_sems.at[k], recv_sem=recv_sems.at[k],
                    device_id=(*chips[j], cc), device_id_type=MESH).wait_recv()

    return list(pl.pallas_call(
        body, name=name, out_shape=tuple(pltpu.HBM(f.shape, f.dtype) for f in fulls),
        in_specs=[HBM_SPEC] * n + [SEM_SPEC, SEM_SPEC, ANY_SPEC], out_specs=(HBM_SPEC,) * n,
        input_output_aliases={w: w for w in range(n)},
        compiler_params=pltpu.CompilerParams(has_side_effects=EFFECT),
    )(*fulls, sems[0], sems[1], after))


def _forward_to_sibling(full, kind, shard_shape, relations, name):
    nr = len(relations)

    def body(in_ref, out_ref, send_sems, recv_sems):
        x, y, c = _place()
        chips = _other_chips(x, y)
        copies = []
        for pos, j in enumerate(relations):
            k = 2 * chips[j][0] + chips[j][1]
            mine = _weight_window(in_ref, kind, shard_shape, k, c)
            cp = pltpu.make_async_remote_copy(
                src_ref=mine, dst_ref=_weight_window(out_ref, kind, shard_shape, k, c),
                send_sem=send_sems.at[pos], recv_sem=recv_sems.at[pos], device_id=(x, y, 1 - c), device_id_type=MESH)
            cp.start()
            copies.append(cp)
        for pos, j in enumerate(relations):
            theirs = _weight_window(out_ref, kind, shard_shape, 2 * chips[j][0] + chips[j][1], 1 - c)
            pltpu.make_async_remote_copy(
                src_ref=theirs, dst_ref=theirs, send_sem=send_sems.at[pos], recv_sem=recv_sems.at[pos],
                device_id=(x, y, 1 - c), device_id_type=MESH).wait_recv()
        for cp in copies:
            cp.wait_send()

    return pl.pallas_call(
        body, name=name, out_shape=jax.ShapeDtypeStruct(full.shape, full.dtype),
        in_specs=[HBM_SPEC], out_specs=HBM_SPEC, input_output_aliases={0: 0},
        scratch_shapes=[pltpu.SemaphoreType.DMA((nr,))] * 2,
    )(full)


def _grad_piece(ref, kind, h, cdim, k, half):
    if kind == "col":
        return ref.at[pl.ds(half * h, h), pl.ds(pl.multiple_of(k * cdim, 128), cdim)]
    return ref.at[k, pl.ds(half * h, h), :]


def _grad_dims(g, kind):
    return (g.shape[0] // 2, g.shape[1] // N_CHIP) if kind == "col" else (g.shape[1] // 2, g.shape[2])


def _reduce_start(grads, kinds, name):
    n = len(grads)
    dims = [_grad_dims(g, kind) for g, kind in zip(grads, kinds)]
    lands = [lax.empty((N_DEV - 1, h, cdim), g.dtype) for g, (h, cdim) in zip(grads, dims)]

    def body(*refs):
        g_ins, land_ins = refs[:n], refs[n:2 * n]
        send_sems, recv_sems = refs[2 * n], refs[2 * n + 1]
        token = refs[4 * n + 2]
        x, y, c = _place()
        for w in range(n):
            h, cdim = dims[w]
            base = (N_DEV - 1) * w
            pltpu.make_async_remote_copy(
                src_ref=_grad_piece(g_ins[w], kinds[w], h, cdim, 2 * x + y, 1 - c), dst_ref=land_ins[w].at[0],
                send_sem=send_sems.at[base], recv_sem=recv_sems.at[base],
                device_id=(x, y, 1 - c), device_id_type=MESH).start()
            for j, chip in enumerate(_other_chips(x, y)):
                for cc in range(2):
                    pltpu.make_async_remote_copy(
                        src_ref=_grad_piece(g_ins[w], kinds[w], h, cdim, 2 * chip[0] + chip[1], cc),
                        dst_ref=land_ins[w].at[1 + 2 * j + c],
                        send_sem=send_sems.at[base + 1 + 2 * j + cc], recv_sem=recv_sems.at[base + 1 + 2 * j + c],
                        device_id=(*chip, cc), device_id_type=MESH).start()
        token[...] = jnp.zeros_like(token)

    sems = pltpu.SemaphoreType.DMA(((N_DEV - 1) * n,))
    outs = pl.pallas_call(
        body, name=name,
        out_shape=(sems, sems) + tuple(pltpu.HBM(a.shape, a.dtype) for a in list(grads) + lands)
        + (jax.ShapeDtypeStruct((SUBLANE, 128), F32),),
        in_specs=[HBM_SPEC] * (2 * n), out_specs=(SEM_SPEC, SEM_SPEC) + (HBM_SPEC,) * (2 * n) + (VMEM_SPEC,),
        input_output_aliases={i: 2 + i for i in range(2 * n)},
        compiler_params=pltpu.CompilerParams(has_side_effects=EFFECT),
    )(*[_hbm(a) for a in list(grads) + lands])
    return outs[0], outs[1], list(outs[2:2 + n]), list(outs[2 + n:2 + 2 * n]), outs[2 + 2 * n]


def _reduce_wait(send_sems, recv_sems, grads, lands, kinds, after, name):
    n = len(grads)
    dims = [_grad_dims(g, kind) for g, kind in zip(grads, kinds)]

    def body(*refs):
        g_ins, land_ins = refs[:n], refs[n:2 * n]
        send_sems, recv_sems = refs[2 * n], refs[2 * n + 1]
        x, y, c = _place()
        for w in range(n):
            h, cdim = dims[w]
            piece = _grad_piece(g_ins[w], kinds[w], h, cdim, 2 * x + y, c)
            for s in range(N_DEV - 1):
                k = (N_DEV - 1) * w + s
                slot = land_ins[w].at[s]
                pltpu.make_async_remote_copy(
                    src_ref=piece, dst_ref=slot, send_sem=send_sems.at[k], recv_sem=recv_sems.at[k],
                    device_id=(x, y, 1 - c), device_id_type=MESH).wait_send()
                pltpu.make_async_remote_copy(
                    src_ref=piece, dst_ref=slot, send_sem=send_sems.at[k], recv_sem=recv_sems.at[k],
                    device_id=(x, y, 1 - c), device_id_type=MESH).wait_recv()

    outs = pl.pallas_call(
        body, name=name, out_shape=tuple(pltpu.HBM(a.shape, a.dtype) for a in list(grads) + list(lands)),
        in_specs=[HBM_SPEC] * (2 * n) + [SEM_SPEC, SEM_SPEC, ANY_SPEC], out_specs=(HBM_SPEC,) * (2 * n),
        input_output_aliases={i: i for i in range(2 * n)},
        compiler_params=pltpu.CompilerParams(has_side_effects=EFFECT),
    )(*grads, *lands, send_sems, recv_sems, after)
    return list(outs[:n]), list(outs[n:])


def _add_pieces(g, land, kind, chip_idx, core_idx, name):
    _, h, cdim = land.shape
    tr, tc = _tile(h, 256), _tile(cdim, 2048)
    nrb, ncb = h // tr, cdim // tc

    def body(k_ref, c_ref, g_ref, l_ref, o_ref):
        acc = g_ref[...].astype(F32)
        for s in range(N_DEV - 1):
            acc = acc + l_ref[s].astype(F32)
        o_ref[...] = acc

    if kind == "col":
        g_spec = pl.BlockSpec((tr, tc), lambda i, j, kr, cr: (cr[0] * nrb + i, kr[0] * ncb + j))
    else:
        g_spec = pl.BlockSpec((None, tr, tc), lambda i, j, kr, cr: (kr[0], cr[0] * nrb + i, j))
    return pl.pallas_call(
        body, name=name, out_shape=jax.ShapeDtypeStruct((2 * h, cdim), F32),
        grid_spec=pltpu.PrefetchScalarGridSpec(
            num_scalar_prefetch=2, grid=(nrb, ncb),
            in_specs=[g_spec, pl.BlockSpec((N_DEV - 1, tr, tc), lambda i, j, kr, cr: (0, i, j))],
            out_specs=pl.BlockSpec((tr, tc), lambda i, j, kr, cr: (cr[0] * nrb + i, j))),
        compiler_params=_params("parallel", "parallel"),
    )(chip_idx, core_idx, g, land)


def _join_halves(shards, name):
    n = len(shards)

    def body(*refs):
        ins, outs = refs[:n], refs[n:2 * n]
        send_sems, recv_sems = refs[2 * n:]
        x, y, c = _place()
        copies = []
        for w in range(n):
            h = shards[w].shape[0] // 2
            cp = pltpu.make_async_remote_copy(
                src_ref=ins[w].at[pl.ds(c * h, h), :], dst_ref=outs[w].at[pl.ds(c * h, h), :],
                send_sem=send_sems.at[w], recv_sem=recv_sems.at[w], device_id=(x, y, 1 - c), device_id_type=MESH)
            cp.start()
            copies.append(cp)
        for w in range(n):
            h = shards[w].shape[0] // 2
            theirs = outs[w].at[pl.ds((1 - c) * h, h), :]
            pltpu.make_async_remote_copy(
                src_ref=theirs, dst_ref=theirs, send_sem=send_sems.at[w], recv_sem=recv_sems.at[w],
                device_id=(x, y, 1 - c), device_id_type=MESH).wait_recv()
        for cp in copies:
            cp.wait_send()

    return pl.pallas_call(
        body, name=name, out_shape=tuple(jax.ShapeDtypeStruct(s.shape, s.dtype) for s in shards),
        in_specs=[HBM_SPEC] * n, out_specs=tuple([HBM_SPEC] * n),
        input_output_aliases={w: w for w in range(n)},
        scratch_shapes=[pltpu.SemaphoreType.DMA((n,))] * 2,
    )(*shards)


def _sum_devices(gathered, rows, name):
    n = gathered.shape[1]

    def body(g_ref, o_ref):
        acc = g_ref[0:rows, :]
        for dev in range(1, N_DEV):
            acc = acc + g_ref[dev * rows:(dev + 1) * rows, :]
        o_ref[...] = acc

    return pl.pallas_call(
        body, name=name, out_shape=jax.ShapeDtypeStruct((rows, n), F32),
        in_specs=[VMEM_SPEC], out_specs=VMEM_SPEC,
        compiler_params=pltpu.CompilerParams(vmem_limit_bytes=VMEM_LIMIT),
    )(gathered)


def _pack(vectors, width):
    flat = [v.reshape(-1) for v in vectors]
    offsets, total = [], 0
    for f in flat:
        offsets.append(total)
        total += f.shape[0]
    rows = -(-total // (width * SUBLANE)) * SUBLANE
    flat.append(jnp.zeros((rows * width - total,), F32))
    return jnp.concatenate(flat).reshape(rows, width), offsets


def kernel(x, c, mod_w, mod_b, norm_g, a_w_in, a_conv_w, a_conv_b, a_w_out, b_w_in, b_ln_g, b_ln_b, b_w_s, b_b_s, b_w_out, final_g, loss_target, m_mod_w, m_mod_b, m_norm_g, m_a_w_in, m_a_conv_w, m_a_conv_b, m_a_w_out, m_b_w_in, m_b_ln_g, m_b_ln_b, m_b_w_s, m_b_b_s, m_b_w_out, m_final_g, v_mod_w, v_mod_b, v_norm_g, v_a_w_in, v_a_conv_w, v_a_conv_b, v_a_w_out, v_b_w_in, v_b_ln_g, v_b_ln_b, v_b_w_s, v_b_b_s, v_b_w_out, v_final_g):
    seq, d = x.shape[1], x.shape[2]
    e = a_conv_b.shape[1]
    wd = mod_w.shape[2]
    ax, ay, ac = _place()
    chip = 2 * ax + ay
    dev = 2 * chip + ac
    chip_idx = jnp.reshape(chip, (1,)).astype(jnp.int32)
    core_idx = jnp.reshape(ac, (1,)).astype(jnp.int32)

    x2d = x[0]
    target = loss_target[0]

    w_names = ["a_w_in", "a_w_out", "b_w_in", "b_w_out"]
    w_kinds = ["col", "row", "col", "row"]
    w_shards = [a_w_in[0], a_w_out[0], b_w_in[0], b_w_out[0]]
    w_shapes = [sh.shape for sh in w_shards]
    placed = [_place_own(sh, kind, chip_idx, "place_" + nm) for nm, sh, kind in zip(w_names, w_shards, w_kinds)]
    a_groups, far_groups = [[(0, 0), (0, 1)]], [[(0, 2)]]
    whole_group = [[(0, 0), (0, 1), (0, 2)]]

    es = e // N_CHIP
    packed0, offs0 = _pack([c, a_conv_w, b_ln_g, b_ln_b], 1024)
    gathered0 = _all_gather(packed0, "gather_params").reshape(N_DEV, -1)
    c_all = gathered0[:, :d]
    per_chip = gathered0[0::2]

    def from_chips_cols(k, rows_):
        got = per_chip[:, offs0[k]:offs0[k] + rows_ * es].reshape(N_CHIP, rows_, es)
        return jnp.transpose(got, (1, 0, 2)).reshape(rows_, e)

    conv_w_full = from_chips_cols(1, 3)
    conv_w = [conv_w_full[k:k + 1] for k in range(3)]
    ln_g, ln_b = from_chips_cols(2, 1), from_chips_cols(3, 1)
    mod_b_cols = lax.dynamic_slice_in_dim(mod_b, chip * wd, wd, axis=1)[:, None, :]
    c_act, mod_part = _mod_fwd(c_all, mod_w, mod_b_cols, "mod_fwd")
    n_layer = mod_w.shape[0]
    mod_gathered = _all_gather(mod_part, "gather_mod")
    a_sems, (wa_in,), a_token = _gather_weights_start(
        placed[:1], w_kinds[:1], w_shapes[:1], a_groups, mod_gathered, "gather_a_w_in_start", both_cores=False)
    mod_all = mod_gathered.reshape(N_CHIP, 2, n_layer, N_DEV, wd)[:, 0]
    mod_all = jnp.transpose(mod_all, (1, 2, 0, 3)).reshape(n_layer, N_DEV, N_CHIP * wd)
    mod_me = lax.dynamic_index_in_dim(mod_all, dev, axis=1, keepdims=False)
    shift = [mod_me[l:l + 1, 0:d] for l in range(n_layer)]
    scale = [mod_me[l:l + 1, d:2 * d] for l in range(n_layer)]
    gate = [mod_me[l:l + 1, 2 * d:3 * d] for l in range(n_layer)]

    g0, g1, gf = norm_g[0:1], norm_g[1:2], final_g[None, :]

    def slab(r):
        return jnp.bitwise_xor(chip_idx, r)

    def arrived(sems, group, weight, w, after, name, both_cores=True):
        return _gather_weights_wait(sems, [(0, j) for _, j in group], [weight], [w_kinds[w]], [w_shapes[w]],
                                    after, name, both_cores)[0]

    h0, proj0 = _norm_mod_proj_slab(x2d, g0, scale[0], shift[0], wa_in, slab(0), 4, "norm_mod0_a_proj_own", a_token)
    wa_in = arrived(a_sems[0], a_groups[0], wa_in, 0, proj0, "gather_wait_near", both_cores=False)
    wa_in = _forward_to_sibling(wa_in, w_kinds[0], w_shapes[0], [0, 1], "forward_near")
    far_sems, (wa_in,), far_token = _gather_weights_start(
        [wa_in], w_kinds[:1], w_shapes[:1], far_groups, a_token, "gather_a_w_in_far_start")
    proj0 = _mm_proj_slab(h0, wa_in, proj0, slab(2), 4, "a_proj_x", far_token)
    proj0 = _mm_proj_slab(h0, wa_in, proj0, slab(1), 4, "a_proj_y", far_token)
    wa_in = arrived(far_sems[0], far_groups[0], wa_in, 0, proj0, "gather_wait_far")
    def start_whole(w, after, name):
        sems, (flight,), token = _gather_weights_start(
            placed[w:w + 1], w_kinds[w:w + 1], w_shapes[w:w + 1], whole_group, after, name)
        return sems[0], flight, token

    ao_sems, wa_out, ao_token = start_whole(1, wa_in, "gather_a_w_out_start")
    proj0 = _mm_proj_slab(h0, wa_in, proj0, slab(3), 4, "a_proj_far", ao_token)
    wa_out = arrived(ao_sems, whole_group[0], wa_out, 1, proj0, "gather_wait_a_w_out")
    bi_sems, wb_in, bi_token = start_whole(2, wa_out, "gather_b_w_in_start")
    y0, br0, x1, h1 = _conv_fwd(proj0, conv_w, a_conv_b, wa_out, x2d, gate[0], g1, scale[1], shift[1],
                                "conv_fwd", bi_token)
    wb_in = arrived(bi_sems, whole_group[0], wb_in, 2, h1, "gather_wait_b_w_in")
    bo_sems, wb_out, bo_token = start_whole(3, wb_in, "gather_b_w_out_start")
    proj1 = _mm_proj(h1, wb_in, 3, "b_proj", bo_token)
    b_s_t = jnp.transpose(b_b_s[0])
    wb_out = arrived(bo_sems, whole_group[0], wb_out, 3, proj1, "gather_wait_b_w_out")
    y1, dx2, dbr1, loss_part, g_final_g, dgate1 = _gmlp_fwd_head(
        proj1, ln_g, ln_b, b_w_s[0], b_s_t, wb_out, x1, gate[1], gf, target, "gmlp_fwd_head")

    gw_b_out = _mm_dw_out(y1, dbr1, "b_out_dw")
    dproj1, g_w_s, g_b_s_t, g_ln_g, g_ln_b = _gmlp_bwd(
        proj1, dbr1, wb_out, ln_g, ln_b, b_w_s[0], jnp.swapaxes(b_w_s[0], 1, 2), b_s_t, "gmlp_bwd")
    gw_b_in = _mm_dw_in(h1, dproj1, "b_proj_dw")
    b_kinds = ["col", "row"]
    b_send, b_recv, b_grads, b_lands, b_token = _reduce_start(
        [gw_b_in, gw_b_out.reshape(N_CHIP, e // N_CHIP, d)], b_kinds, "reduce_b_start")
    dx1, dshift1, dscale1, g_g1, dbr0, dgate0 = _mm_dh_norm_bwd(
        dproj1, wb_in, x1, dx2, g1, scale[1], "b_proj_dx", b_token, br=br0, gate=gate[0])

    early = [loss_part[0, 0:1], g_final_g, g_g1, g_ln_g, g_ln_b, jnp.transpose(g_b_s_t), g_w_s,
             jnp.concatenate([dshift1, dscale1, dgate1, dgate0], axis=1)]
    packed_e, offs_e = _pack(early, 1024)
    rows_e = packed_e.shape[0]
    blocks_e = lax.dynamic_update_slice(jnp.zeros((N_DEV, rows_e, 1024), F32), packed_e[None], (dev, 0, 0))
    e_send, e_recv, blocks_e, e_token = _all_gather_start(blocks_e, "gather_small_start")

    gw_a_out = _mm_dw_out(y0, dbr0, "a_out_dw")
    ao_send, ao_recv, ao_grads, ao_lands, ao_grad_token = _reduce_start(
        [gw_a_out.reshape(N_CHIP, e // N_CHIP, d)], ["row"], "reduce_a_out_start")
    dproj0, g_w0, g_w1, g_w2, g_conv_b = _conv_bwd(
        proj0, dbr0, wa_out, conv_w, a_conv_b, "conv_bwd", ao_grad_token + e_token)
    gw_a_in = _mm_dw_in(h0, dproj0, "a_proj_dw")
    ai_send, ai_recv, ai_grads, ai_lands, ai_token = _reduce_start([gw_a_in], ["col"], "reduce_a_in_start")
    grad_x, dshift0, dscale0, g_g0 = _mm_dh_norm_bwd(dproj0, wa_in, x2d, dx1, g0, scale[0], "a_proj_dx", ai_token)

    def finish(send, recv, grads_, lands_, kinds_, names_, after, tag):
        grads_, lands_ = _reduce_wait(send, recv, grads_, lands_, kinds_, after, "reduce_" + tag + "_wait")
        halves = [_add_pieces(g, land, kind, chip_idx, core_idx, "add_pieces_" + nm)
                  for g, land, kind, nm in zip(grads_, lands_, kinds_, names_)]
        return _join_halves(halves, "join_" + tag)

    upd, big_grads = {}, {}

    def adamw_big(nm, w, g, m, v):
        g_out, *rest = _adamw_2d(w[0], g, m[0], v[0], "adamw_" + nm)
        big_grads[nm] = g_out[None]
        upd[nm] = tuple(o[None] for o in rest)

    g_b_w_in, g_b_w_out = finish(b_send, b_recv, b_grads, b_lands, b_kinds, ["b_w_in", "b_w_out"], grad_x, "b")
    adamw_big("b_w_in", b_w_in, g_b_w_in, m_b_w_in, v_b_w_in)
    adamw_big("b_w_out", b_w_out, g_b_w_out, m_b_w_out, v_b_w_out)

    late = [g_g0, jnp.concatenate([g_w0, g_w1, g_w2], axis=0), g_conv_b, jnp.concatenate([dshift0, dscale0], axis=1)]
    packed_l, offs_l = _pack(late, 1024)
    rows_l = packed_l.shape[0]
    gathered_l = _all_gather(packed_l, "gather_small_late", after=upd["b_w_out"][0])
    blocks_e = _all_gather_wait(e_send, e_recv, blocks_e, gathered_l, "gather_small_wait")
    gathered_e = blocks_e.reshape(N_DEV * rows_e, 1024)
    summed_e = _sum_devices(gathered_e, rows_e, "sum_small_early").reshape(-1)
    summed_l = _sum_devices(gathered_l, rows_l, "sum_small_late").reshape(-1)

    def take(summed, offs, k, shape):
        size = math.prod(shape)
        return summed[offs[k]:offs[k] + size].reshape(shape)

    def rows_of(gathered, rows_, offs, k, width):
        return gathered.reshape(N_DEV, rows_ * 1024)[:, offs[k]:offs[k] + width]

    loss = take(summed_e, offs_e, 0, ())
    grad_final_g = take(summed_e, offs_e, 1, (d,))
    grad_norm_g = jnp.concatenate([take(summed_l, offs_l, 0, (1, d)), take(summed_e, offs_e, 2, (1, d))], axis=0)
    grad_ln_g_full = take(summed_e, offs_e, 3, (1, e))
    grad_ln_b_full = take(summed_e, offs_e, 4, (1, e))
    grad_b_b_s = take(summed_e, offs_e, 5, (1, GROUPS, CHUNK))
    grad_b_w_s = take(summed_e, offs_e, 6, (1, GROUPS, CHUNK, CHUNK))
    grad_conv_w_full = take(summed_l, offs_l, 1, (3, e))
    grad_a_conv_b = take(summed_l, offs_l, 2, (1, e))
    grad_a_conv_w = lax.dynamic_slice_in_dim(grad_conv_w_full, chip * es, es, axis=1)[None]
    grad_b_ln_g = lax.dynamic_slice_in_dim(grad_ln_g_full, chip * es, es, axis=1)
    grad_b_ln_b = lax.dynamic_slice_in_dim(grad_ln_b_full, chip * es, es, axis=1)
    dmod_e = rows_of(gathered_e, rows_e, offs_e, 7, 4 * d)
    dmod_l = rows_of(gathered_l, rows_l, offs_l, 3, 2 * d)
    dmod_all = jnp.stack([jnp.concatenate([dmod_l, dmod_e[:, 3 * d:]], axis=1), dmod_e[:, :3 * d]], axis=1)
    mod_b_e = take(summed_e, offs_e, 7, (4 * d,))
    mod_b_l = take(summed_l, offs_l, 3, (2 * d,))
    grad_mod_b = jnp.stack([jnp.concatenate([mod_b_l, mod_b_e[3 * d:]]), mod_b_e[:3 * d]])
    dmod_cols = jnp.transpose(lax.dynamic_slice_in_dim(dmod_all, chip * wd, wd, axis=2), (1, 0, 2))

    grad_mod_w, delta_mod_w, new_m_mod_w, new_v_mod_w = _mod_w_update(
        jnp.transpose(c_act), dmod_cols, mod_w, m_mod_w, v_mod_w, "mod_w_update")
    (g_a_w_out,) = finish(ao_send, ao_recv, ao_grads, ao_lands, ["row"], ["a_w_out"], delta_mod_w, "a_out")
    adamw_big("a_w_out", a_w_out, g_a_w_out, m_a_w_out, v_a_w_out)
    (g_a_w_in,) = finish(ai_send, ai_recv, ai_grads, ai_lands, ["col"], ["a_w_in"], upd["a_w_out"][0], "a_in")
    adamw_big("a_w_in", a_w_in, g_a_w_in, m_a_w_in, v_a_w_in)
    small_w = [("mod_b", mod_b, grad_mod_b, m_mod_b, v_mod_b), ("norm_g", norm_g, grad_norm_g, m_norm_g, v_norm_g),
               ("a_conv_w", a_conv_w, grad_a_conv_w, m_a_conv_w, v_a_conv_w),
               ("a_conv_b", a_conv_b, grad_a_conv_b, m_a_conv_b, v_a_conv_b),
               ("b_ln_g", b_ln_g, grad_b_ln_g, m_b_ln_g, v_b_ln_g), ("b_ln_b", b_ln_b, grad_b_ln_b, m_b_ln_b, v_b_ln_b),
               ("b_w_s", b_w_s, grad_b_w_s, m_b_w_s, v_b_w_s), ("b_b_s", b_b_s, grad_b_b_s, m_b_b_s, v_b_b_s),
               ("final_g", final_g, grad_final_g, m_final_g, v_final_g)]

    def flat2d(a):
        return a.reshape(-1, a.shape[-1])

    res = _adamw_small([flat2d(t[1]) for t in small_w], [flat2d(t[2]) for t in small_w],
                       [flat2d(t[3]) for t in small_w], [flat2d(t[4]) for t in small_w], "adamw_small")
    for (nm, w, _, _, _), r3 in zip(small_w, res):
        upd[nm] = tuple(o.reshape(w.shape) for o in r3)
    upd["mod_w"] = (delta_mod_w, new_m_mod_w, new_v_mod_w)

    grads = {"mod_w": grad_mod_w, "mod_b": grad_mod_b, "norm_g": grad_norm_g, "a_conv_w": grad_a_conv_w,
             "a_conv_b": grad_a_conv_b, "b_ln_g": grad_b_ln_g, "b_ln_b": grad_b_ln_b, "b_w_s": grad_b_w_s,
             "b_b_s": grad_b_b_s, "final_g": grad_final_g, **big_grads}
    order = ["mod_w", "mod_b", "norm_g", "a_w_in", "a_conv_w", "a_conv_b", "a_w_out", "b_w_in", "b_ln_g", "b_ln_b",
             "b_w_s", "b_b_s", "b_w_out", "final_g"]
    return (loss, grad_x[None], *[grads[k] for k in order], *[upd[k][0] for k in order],
            *[upd[k][1] for k in order], *[upd[k][2] for k in order])
```

```python
import math

import jax
import jax.numpy as jnp
from jax import lax
from jax.experimental import pallas as pl
from jax.experimental.pallas import tpu as pltpu

F32 = jnp.float32
BF16 = jnp.bfloat16
MESH = pl.DeviceIdType.MESH

N_DEV = 8
N_CHIP = 4
SUBLANE = 8
PACK = 16
ACT = F32
RMS_EPS = 1e-6
LN_EPS = 1e-5
CHUNK = 128
GROUPS = 8
ADAM_LR = 0.001
ADAM_B1 = 0.9
ADAM_B2 = 0.999
ADAM_EPS = 1e-08
ADAM_WD = 0.01
ADAM_STEP = 10
VMEM_LIMIT = 56 << 20

HBM_SPEC = pl.BlockSpec(memory_space=pltpu.HBM)
VMEM_SPEC = pl.BlockSpec(memory_space=pltpu.VMEM)
SEM_SPEC = pl.BlockSpec(memory_space=pltpu.SEMAPHORE)
ANY_SPEC = pl.BlockSpec(memory_space=pl.ANY)
EFFECT = pltpu.SideEffectType.DATAFLOW_SIDE_EFFECTING


def _params(*sem):
    return pltpu.CompilerParams(dimension_semantics=sem, vmem_limit_bytes=VMEM_LIMIT)


def _tile(n, want):
    if n <= want:
        return n
    t = want
    while n % t:
        t -= 128
    return t


def _sigmoid(x):
    return 0.5 * jnp.tanh(0.5 * x) + 0.5


def _silu_and_grad(x):
    s = _sigmoid(x)
    return x * s, s * (1.0 + x * (1.0 - s))


def _gelu_and_grad(x):
    cdf = 0.5 * (1.0 + lax.erf(x * (1.0 / math.sqrt(2.0))))
    pdf = jnp.exp(-0.5 * x * x) * (1.0 / math.sqrt(2.0 * math.pi))
    return x * cdf, cdf + x * pdf


def _gelu(x):
    return x * (0.5 * (1.0 + lax.erf(x * (1.0 / math.sqrt(2.0)))))


def _rms(x):
    r = lax.rsqrt(jnp.mean(x * x, axis=-1, keepdims=True) + RMS_EPS)
    return x * r, r


def _rms_bwd(dxn, xn, r):
    return r * (dxn - xn * jnp.mean(dxn * xn, axis=-1, keepdims=True))


def _colsum(a):
    return jnp.sum(a, axis=0, keepdims=True)


def _shift_down(cur, before, k):
    rolled = pltpu.roll(cur, k, 0)
    row = lax.broadcasted_iota(jnp.int32, before.shape, 0)
    head = jnp.where(row < k, pltpu.roll(before, k, 0), rolled[:SUBLANE])
    return jnp.concatenate([head, rolled[SUBLANE:]], axis=0)


def _shift_up(cur, after, k):
    n = cur.shape[0]
    rolled = pltpu.roll(cur, n - k, 0)
    row = lax.broadcasted_iota(jnp.int32, after.shape, 0)
    tail = jnp.where(row >= SUBLANE - k, pltpu.roll(after, SUBLANE - k, 0), rolled[n - SUBLANE:])
    return jnp.concatenate([rolled[:n - SUBLANE], tail], axis=0)


def _after_spec():
    return pl.BlockSpec((SUBLANE, 128), lambda *_: (0, 0))


def _mm_proj(h, w, n_split, name, after):
    s, d = h.shape
    e = w.shape[1] // n_split
    tm, tn = _tile(s, 2048), _tile(e, 2048)
    nj = e // tn

    def body(h_ref, w_ref, after_ref, o_ref):
        o_ref[...] = jnp.dot(h_ref[...], w_ref[...], preferred_element_type=F32).astype(ACT)

    return pl.pallas_call(
        body, name=name,
        out_shape=jax.ShapeDtypeStruct((n_split, s, e), ACT),
        grid=(s // tm, n_split * nj),
        in_specs=[pl.BlockSpec((tm, d), lambda i, j: (i, 0)), pl.BlockSpec((d, tn), lambda i, j: (0, j)),
                  _after_spec()],
        out_specs=pl.BlockSpec((None, tm, tn), lambda i, j: (j // nj, i, j % nj)),
        compiler_params=_params("parallel", "parallel"),
    )(h, w, after)


def _mm_proj_slab(h, w, proj, q_idx, n_split, name, after):
    s, d = h.shape
    e = w.shape[1] // n_split
    tm, tn = _tile(s, 1024), _tile(e, 2048)
    nj = e // tn

    def body(q_ref, h_ref, w_ref, after_ref, proj_ref, o_ref):
        o_ref[...] = jnp.dot(h_ref[...], w_ref[...], preferred_element_type=F32).astype(ACT)

    return pl.pallas_call(
        body, name=name,
        out_shape=jax.ShapeDtypeStruct((n_split, s, e), ACT),
        grid_spec=pltpu.PrefetchScalarGridSpec(
            num_scalar_prefetch=1, grid=(s // tm, nj),
            in_specs=[pl.BlockSpec((tm, d), lambda i, j, qr: (i, 0)),
                      pl.BlockSpec((d, tn), lambda i, j, qr: (0, qr[0] * nj + j)), _after_spec(), ANY_SPEC],
            out_specs=pl.BlockSpec((None, tm, tn), lambda i, j, qr: (qr[0], i, j))),
        input_output_aliases={4: 0},
        compiler_params=_params("parallel", "parallel"),
    )(q_idx, h, w, after, proj)


def _mm_dh_norm_bwd(dp, w, x, dx_in, g, scale, name, after, br=None, gate=None):
    nq, s, e = dp.shape
    d = w.shape[0]
    has_branch = br is not None
    tm, tk = _tile(s, 1024), _tile(e, 1024 if has_branch else 2048)
    nkq = e // tk
    nk = nq * nkq

    def body(*refs):
        if has_branch:
            (a_ref, b_ref, after_ref, x_ref, dxin_ref, g_ref, sc_ref, br_ref, gate_ref,
             dx_ref, dsh_ref, dsc_ref, dg_ref, dbr_ref, dgate_ref, acc_ref) = refs
        else:
            (a_ref, b_ref, after_ref, x_ref, dxin_ref, g_ref, sc_ref,
             dx_ref, dsh_ref, dsc_ref, dg_ref, acc_ref) = refs
        i, k = pl.program_id(0), pl.program_id(1)

        @pl.when(jnp.logical_and(i == 0, k == 0))
        def _():
            dsh_ref[...] = jnp.zeros_like(dsh_ref)
            dsc_ref[...] = jnp.zeros_like(dsc_ref)
            dg_ref[...] = jnp.zeros_like(dg_ref)
            if has_branch:
                dgate_ref[...] = jnp.zeros_like(dgate_ref)

        @pl.when(k == 0)
        def _():
            acc_ref[...] = jnp.zeros_like(acc_ref)
        acc_ref[...] += lax.dot_general(a_ref[...], b_ref[...], (((1,), (1,)), ((), ())), preferred_element_type=F32)

        @pl.when(k == nk - 1)
        def _():
            dh = acc_ref[...]
            g_vec = g_ref[...]
            xn, r = _rms(x_ref[...])
            dsh_ref[...] += _colsum(dh)
            dsc_ref[...] += _colsum(dh * (xn * g_vec))
            da = dh * (1.0 + sc_ref[...])
            dg_ref[...] += _colsum(da * xn)
            dx = dxin_ref[...] + _rms_bwd(da * g_vec, xn, r)
            dx_ref[...] = dx
            if has_branch:
                dgate_ref[...] += _colsum(dx * br_ref[...])
                dbr_ref[...] = (gate_ref[...] * dx).astype(BF16)

    rows = pl.BlockSpec((tm, d), lambda i, k: (i, 0))
    vec = pl.BlockSpec((1, d), lambda i, k: (0, 0))
    vec_out = jax.ShapeDtypeStruct((1, d), F32)
    in_specs = [pl.BlockSpec((None, tm, tk), lambda i, k: (k // nkq, i, k % nkq)),
                pl.BlockSpec((d, tk), lambda i, k: (0, k)), _after_spec(), rows, rows, vec, vec]
    out_shape = [jax.ShapeDtypeStruct((s, d), F32), vec_out, vec_out, vec_out]
    out_specs = [rows, vec, vec, vec]
    args = [dp, w, after, x, dx_in, g, scale]
    if has_branch:
        in_specs += [rows, vec]
        out_shape += [jax.ShapeDtypeStruct((s, d), BF16), vec_out]
        out_specs += [rows, vec]
        args += [br, gate]
    return pl.pallas_call(
        body, name=name, out_shape=tuple(out_shape), grid=(s // tm, nk),
        in_specs=in_specs, out_specs=tuple(out_specs), scratch_shapes=[pltpu.VMEM((tm, d), F32)],
        compiler_params=_params("arbitrary", "arbitrary"),
    )(*args)


def _dw_body(n_t):
    def body(a_ref, b_ref, o_ref, acc_ref):
        t = pl.program_id(2)

        @pl.when(t == 0)
        def _():
            acc_ref[...] = jnp.zeros_like(acc_ref)
        acc_ref[...] += lax.dot_general(a_ref[...], b_ref[...], (((0,), (0,)), ((), ())), preferred_element_type=F32)

        @pl.when(t == n_t - 1)
        def _():
            o_ref[...] = acc_ref[...].astype(o_ref.dtype)
    return body


def _mm_dw_in(h, dp, name):
    s, d = h.shape
    nq, _, e = dp.shape
    tm, tn, tt = _tile(d, 1024), _tile(e, 2048), _tile(s, 2048)
    nj = e // tn

    return pl.pallas_call(
        _dw_body(s // tt), name=name,
        out_shape=jax.ShapeDtypeStruct((d, nq * e), BF16),
        grid=(d // tm, nq * nj, s // tt),
        in_specs=[pl.BlockSpec((tt, tm), lambda i, j, t: (t, i)),
                  pl.BlockSpec((None, tt, tn), lambda i, j, t: (j // nj, t, j % nj))],
        out_specs=pl.BlockSpec((tm, tn), lambda i, j, t: (i, j)),
        scratch_shapes=[pltpu.VMEM((tm, tn), F32)],
        compiler_params=_params("parallel", "parallel", "arbitrary"),
    )(h, dp)


def _mm_dw_out(y, dbr, name):
    s, e = y.shape
    d = dbr.shape[1]
    tm, tn, tt = _tile(e, 1024), _tile(d, 1024), _tile(s, 2048)

    return pl.pallas_call(
        _dw_body(s // tt), name=name,
        out_shape=jax.ShapeDtypeStruct((e, d), BF16),
        grid=(e // tm, d // tn, s // tt),
        in_specs=[pl.BlockSpec((tt, tm), lambda i, j, t: (t, i)), pl.BlockSpec((tt, tn), lambda i, j, t: (t, j))],
        out_specs=pl.BlockSpec((tm, tn), lambda i, j, t: (i, j)),
        scratch_shapes=[pltpu.VMEM((tm, tn), F32)],
        compiler_params=_params("parallel", "parallel", "arbitrary"),
    )(y, dbr)


def _row_spec(ts, d):
    return pl.BlockSpec((ts, d), lambda i: (i, 0))


def _vec_spec(d):
    return pl.BlockSpec((1, d), lambda i: (0, 0))


def _norm_mod_proj_slab(x, g, scale, shift, w, q_idx, n_split, name, after):
    s, d = x.shape
    e = w.shape[1] // n_split
    tm = _tile(s, 1024)

    def body(q_ref, x_ref, g_ref, sc_ref, sh_ref, w_ref, after_ref, h_ref, o_ref):
        xn, _ = _rms(x_ref[...])
        h = ((xn * g_ref[...]) * (1.0 + sc_ref[...]) + sh_ref[...]).astype(BF16)
        h_ref[...] = h
        o_ref[...] = jnp.dot(h, w_ref[...], preferred_element_type=F32).astype(ACT)

    vec = pl.BlockSpec((1, d), lambda i, qr: (0, 0))
    rows = pl.BlockSpec((tm, d), lambda i, qr: (i, 0))
    return pl.pallas_call(
        body, name=name,
        out_shape=(jax.ShapeDtypeStruct((s, d), BF16), jax.ShapeDtypeStruct((n_split, s, e), ACT)),
        grid_spec=pltpu.PrefetchScalarGridSpec(
            num_scalar_prefetch=1, grid=(s // tm,),
            in_specs=[rows, vec, vec, vec, pl.BlockSpec((d, e), lambda i, qr: (0, qr[0])), _after_spec()],
            out_specs=(rows, pl.BlockSpec((None, tm, e), lambda i, qr: (qr[0], i, 0)))),
        compiler_params=_params("parallel"),
    )(q_idx, x, g, scale, shift, w, after)


def _head_tile(x1, br, gate, gf, target):
    d = x1.shape[-1]
    xn, r = _rms(x1 + gate * br)
    err = xn * gf - target
    loss = 0.5 * jnp.sum(jnp.mean(err * err, axis=-1, keepdims=True))
    dout = err * (1.0 / d)
    dx = _rms_bwd(dout * gf, xn, r)
    return dx, gate * dx, loss, _colsum(dout * xn), _colsum(dx * br)


def _f32(ref):
    return ref[...].astype(F32)


def _cols_f32(ref, cols):
    return ref[:, cols].astype(F32)


def _rows_before(halo_ref, cols=slice(None)):
    return _cols_f32(halo_ref, cols)[PACK - SUBLANE:]


def _rows_after(halo_ref, cols=slice(None)):
    return _cols_f32(halo_ref, cols)[:SUBLANE]


CONV_CHAIN_COLS = 512


def _conv_fwd(proj, conv_w, conv_b, w_out, x, gate, g_next, scale_next, shift_next, name, after):
    _, s, e = proj.shape
    d = w_out.shape[1]
    ts, te = _tile(s, 512), e
    cw = min(CONV_CHAIN_COLS, te)
    hb = ts // PACK

    def body(bg_ref, cg_ref, xi_ref, z_ref, cgp_ref, xip_ref, w0_ref, w1_ref, w2_ref, b_ref, wo_ref, after_ref,
             x_ref, gate_ref, g_ref, sc_ref, sh_ref, y_ref, br_ref, x1_ref, h_ref):
        first = pl.program_id(0) == 0
        br = None
        for c0 in range(0, te, cw):
            cols = slice(c0, c0 + cw)
            cx = _cols_f32(cg_ref, cols) * _cols_f32(xi_ref, cols)
            before = jnp.where(first, 0.0, _rows_before(cgp_ref, cols) * _rows_before(xip_ref, cols))
            conv = b_ref[:, cols] + w2_ref[:, cols] * cx
            conv = conv + w0_ref[:, cols] * _shift_down(cx, before, 2)
            conv = conv + w1_ref[:, cols] * _shift_down(cx, before, 1)
            z = _cols_f32(z_ref, cols)
            y = ((z * _sigmoid(z)) * _cols_f32(bg_ref, cols) * conv).astype(BF16)
            y_ref[:, cols] = y
            part_br = jnp.dot(y, wo_ref[cols, :], preferred_element_type=F32)
            br = part_br if br is None else br + part_br
        br_ref[...] = br
        x1 = x_ref[...] + gate_ref[...] * br
        x1_ref[...] = x1
        xn, _ = _rms(x1)
        h_ref[...] = ((xn * g_ref[...]) * (1.0 + sc_ref[...]) + sh_ref[...]).astype(BF16)

    def part(q):
        return pl.BlockSpec((None, ts, te), lambda i: (q, i, 0))

    def halo_before(q):
        return pl.BlockSpec((None, PACK, te), lambda i: (q, jnp.maximum(i * hb - 1, 0), 0))

    return pl.pallas_call(
        body, name=name,
        out_shape=(jax.ShapeDtypeStruct((s, e), BF16), jax.ShapeDtypeStruct((s, d), F32),
                   jax.ShapeDtypeStruct((s, d), F32), jax.ShapeDtypeStruct((s, d), BF16)), grid=(s // ts,),
        in_specs=[part(0), part(1), part(2), part(3), halo_before(1), halo_before(2)]
        + [_vec_spec(e)] * 4 + [pl.BlockSpec((e, d), lambda i: (0, 0), pipeline_mode=pl.Buffered(1)), _after_spec(),
                                _row_spec(ts, d)]
        + [_vec_spec(d)] * 4,
        out_specs=(_row_spec(ts, e), _row_spec(ts, d), _row_spec(ts, d), _row_spec(ts, d)),
        compiler_params=_params("parallel"),
    )(proj, proj, proj, proj, proj, proj, *conv_w, conv_b, w_out, after, x, gate, g_next, scale_next, shift_next)


def _conv_bwd(proj, dbr, w_out, conv_w, conv_b, name, after):
    _, s, e = proj.shape
    d = dbr.shape[1]
    ts, te = _tile(s, 512), _tile(e, 1024)
    cw = min(CONV_CHAIN_COLS // 2, te)
    hb = ts // PACK
    n_i = s // ts
    last_halo = s // PACK - 1
    nt = (((1,), (1,)), ((), ()))

    def body(bg_ref, cg_ref, xi_ref, z_ref, dbr_ref, cgp_ref, xip_ref, bgn_ref, zn_ref, dbrn_ref, wo_ref,
             w0_ref, w1_ref, w2_ref, b_ref, after_ref, dp_ref, dw0_ref, dw1_ref, dw2_ref, db_ref):
        i = pl.program_id(1)

        @pl.when(i == 0)
        def _():
            for acc in (dw0_ref, dw1_ref, dw2_ref, db_ref):
                acc[...] = jnp.zeros_like(acc)
        for c0 in range(0, te, cw):
            cols = slice(c0, c0 + cw)
            wo = wo_ref[cols, :]
            dy = lax.dot_general(dbr_ref[...], wo, nt, preferred_element_type=F32)
            dyn = lax.dot_general(dbrn_ref[...], wo, nt, preferred_element_type=F32)[:SUBLANE]
            bg, cg = _cols_f32(bg_ref, cols), _cols_f32(cg_ref, cols)
            xi, z = _cols_f32(xi_ref, cols), _cols_f32(z_ref, cols)
            w0, w1, w2 = w0_ref[:, cols], w1_ref[:, cols], w2_ref[:, cols]
            cx = cg * xi
            before = jnp.where(i > 0, _rows_before(cgp_ref, cols) * _rows_before(xip_ref, cols), 0.0)
            cx1 = _shift_down(cx, before, 1)
            cx2 = _shift_down(cx, before, 2)
            conv = b_ref[:, cols] + w2 * cx
            conv = conv + w0 * cx2
            conv = conv + w1 * cx1
            sz, dsz = _silu_and_grad(z)
            dp_ref[3, :, cols] = (dy * bg * conv * dsz).astype(BF16)
            dp_ref[0, :, cols] = (dy * sz * conv).astype(BF16)
            dconv = dy * sz * bg
            zn = _rows_after(zn_ref, cols)
            after = jnp.where(i < n_i - 1, dyn * (zn * _sigmoid(zn)) * _rows_after(bgn_ref, cols), 0.0)
            db_ref[:, cols] += _colsum(dconv)
            dw2_ref[:, cols] += _colsum(dconv * cx)
            dw1_ref[:, cols] += _colsum(dconv * cx1)
            dw0_ref[:, cols] += _colsum(dconv * cx2)
            dcx = w2 * dconv + w1 * _shift_up(dconv, after, 1) + w0 * _shift_up(dconv, after, 2)
            dp_ref[1, :, cols] = (dcx * xi).astype(BF16)
            dp_ref[2, :, cols] = (dcx * cg).astype(BF16)

    def part(q):
        return pl.BlockSpec((None, ts, te), lambda j, i: (q, i, j))

    def halo_before(q):
        return pl.BlockSpec((None, PACK, te), lambda j, i: (q, jnp.maximum(i * hb - 1, 0), j))

    def halo_after(q):
        return pl.BlockSpec((None, PACK, te), lambda j, i: (q, jnp.minimum((i + 1) * hb, last_halo), j))

    return pl.pallas_call(
        body, name=name,
        out_shape=(jax.ShapeDtypeStruct((4, s, e), BF16),) + (jax.ShapeDtypeStruct((1, e), F32),) * 4,
        grid=(e // te, n_i),
        in_specs=[part(0), part(1), part(2), part(3), pl.BlockSpec((ts, d), lambda j, i: (i, 0)),
                  halo_before(1), halo_before(2), halo_after(0), halo_after(3),
                  pl.BlockSpec((PACK, d), lambda j, i: (jnp.minimum((i + 1) * hb, last_halo), 0)),
                  pl.BlockSpec((te, d), lambda j, i: (j, 0))]
        + [pl.BlockSpec((1, te), lambda j, i: (0, j))] * 4 + [_after_spec()],
        out_specs=(pl.BlockSpec((4, ts, te), lambda j, i: (0, i, j)),) + (pl.BlockSpec((1, te), lambda j, i: (0, j)),) * 4,
        compiler_params=_params("parallel", "arbitrary"),
    )(proj, proj, proj, proj, dbr, proj, proj, proj, proj, dbr, w_out, *conv_w, conv_b, after)


def _tril(w):
    row = lax.broadcasted_iota(jnp.int32, w.shape, 0)
    col = lax.broadcasted_iota(jnp.int32, w.shape, 1)
    return jnp.where(row >= col, w, 0.0)


def _triu(w):
    row = lax.broadcasted_iota(jnp.int32, w.shape, 0)
    col = lax.broadcasted_iota(jnp.int32, w.shape, 1)
    return jnp.where(row <= col, w, 0.0)


def _layer_norm_fwd(v, g, b):
    mu = jnp.mean(v, axis=-1, keepdims=True)
    vc = v - mu
    rstd = lax.rsqrt(jnp.mean(vc * vc, axis=-1, keepdims=True) + LN_EPS)
    vhat = vc * rstd
    return vhat * g + b, vhat, rstd


GMLP_CHUNKS_PER_STEP = 2


def _gmlp_rows(s):
    return CHUNK * min(GMLP_CHUNKS_PER_STEP, s // CHUNK)


def _mix_positions(w_ref, src_scr, dst_scr, gw, mask, bias_ref=None):
    for gi in range(GROUPS):
        cols = slice(gi * gw, (gi + 1) * gw)
        wm = mask(w_ref[gi]).astype(BF16)
        for n in range(src_scr.shape[0] // CHUNK):
            rows = slice(n * CHUNK, (n + 1) * CHUNK)
            out = jnp.dot(wm, src_scr[rows, cols], preferred_element_type=F32)
            if bias_ref is not None:
                out = out + bias_ref[:, gi:gi + 1]
            dst_scr[rows, cols] = out


def _gmlp_fwd_head(proj, ln_g, ln_b, w_s, b_s_t, w_out, x1, gate, gf, target, name):
    _, s, e = proj.shape
    d = w_out.shape[1]
    gw = e // GROUPS
    ts = CHUNK * min(2 * GMLP_CHUNKS_PER_STEP, s // CHUNK)

    def body(pu_ref, pv_ref, pz_ref, g_ref, b_ref, ws_ref, bs_ref, wo_ref, x1_ref, gate_ref, gf_ref, tg_ref,
             y_ref, dx_ref, dbr_ref, loss_ref, dgf_ref, dgate_ref, vn_scr, mix_scr):
        @pl.when(pl.program_id(0) == 0)
        def _():
            loss_ref[...] = jnp.zeros_like(loss_ref)
            dgf_ref[...] = jnp.zeros_like(dgf_ref)
            dgate_ref[...] = jnp.zeros_like(dgate_ref)
        vn, _, _ = _layer_norm_fwd(_gelu(_f32(pv_ref)), g_ref[...], b_ref[...])
        vn_scr[...] = vn.astype(BF16)
        _mix_positions(ws_ref, vn_scr, mix_scr, gw, _tril, bs_ref)
        z = _f32(pz_ref)
        y = ((z * _sigmoid(z)) * (_gelu(_f32(pu_ref)) * mix_scr[...])).astype(BF16)
        y_ref[...] = y
        br = jnp.dot(y, wo_ref[...], preferred_element_type=F32)
        dx, dbr, loss, dgf, dgate = _head_tile(x1_ref[...], br, gate_ref[...], gf_ref[...], tg_ref[...])
        dx_ref[...] = dx
        dbr_ref[...] = dbr.astype(BF16)
        loss_ref[...] += loss
        dgf_ref[...] += dgf
        dgate_ref[...] += dgate

    def part(q):
        return pl.BlockSpec((None, ts, e), lambda i: (q, i, 0))

    return pl.pallas_call(
        body, name=name,
        out_shape=(jax.ShapeDtypeStruct((s, e), BF16), jax.ShapeDtypeStruct((s, d), F32),
                   jax.ShapeDtypeStruct((s, d), BF16), jax.ShapeDtypeStruct((SUBLANE, 128), F32),
                   jax.ShapeDtypeStruct((1, d), F32), jax.ShapeDtypeStruct((1, d), F32)),
        grid=(s // ts,),
        in_specs=[part(0), part(1), part(2), _vec_spec(e), _vec_spec(e),
                  pl.BlockSpec((GROUPS, CHUNK, CHUNK), lambda i: (0, 0, 0), pipeline_mode=pl.Buffered(1)),
                  pl.BlockSpec((CHUNK, GROUPS), lambda i: (0, 0)),
                  pl.BlockSpec((e, d), lambda i: (0, 0), pipeline_mode=pl.Buffered(1)),
                  _row_spec(ts, d), _vec_spec(d), _vec_spec(d), _row_spec(ts, d)],
        out_specs=(_row_spec(ts, e), _row_spec(ts, d), _row_spec(ts, d),
                   pl.BlockSpec((SUBLANE, 128), lambda i: (0, 0)), _vec_spec(d), _vec_spec(d)),
        scratch_shapes=[pltpu.VMEM((ts, e), BF16), pltpu.VMEM((ts, e), F32)],
        compiler_params=_params("arbitrary"),
    )(proj, proj, proj, ln_g, ln_b, w_s, b_s_t, w_out, x1, gate, gf, target)


def _gmlp_bwd(proj, dbr, w_out, ln_g, ln_b, w_s, w_s_t, b_s_t, name):
    _, s, e = proj.shape
    d = dbr.shape[1]
    gw = e // GROUPS
    ts = _gmlp_rows(s)
    n_i = s // ts

    def body(pu_ref, pv_ref, pz_ref, dbr_ref, wo_ref, g_ref, b_ref, ws_ref, wst_ref, bs_ref,
             dp_ref, dws_ref, dbs_ref, dlg_ref, dlb_ref, vn_scr, mix_scr, dm_scr, dvn_scr, dmacc_scr):
        i = pl.program_id(0)

        @pl.when(i == 0)
        def _():
            dws_ref[...] = jnp.zeros_like(dws_ref)
            dlg_ref[...] = jnp.zeros_like(dlg_ref)
            dlb_ref[...] = jnp.zeros_like(dlb_ref)
            dmacc_scr[...] = jnp.zeros_like(dmacc_scr)
        ln_g = g_ref[...]
        u, du_dpu = _gelu_and_grad(_f32(pu_ref))
        v, dv_dpv = _gelu_and_grad(_f32(pv_ref))
        vn, vhat, rstd = _layer_norm_fwd(v, ln_g, b_ref[...])
        vn_scr[...] = vn.astype(BF16)
        _mix_positions(ws_ref, vn_scr, mix_scr, gw, _tril, bs_ref)
        mixed = mix_scr[...]
        dy = lax.dot_general(dbr_ref[...], wo_ref[...], (((1,), (1,)), ((), ())), preferred_element_type=F32)
        sz, dsz = _silu_and_grad(_f32(pz_ref))
        ds = dy * sz
        dp_ref[2] = (dy * (u * mixed) * dsz).astype(BF16)
        dp_ref[0] = (ds * mixed * du_dpu).astype(BF16)
        dm = ds * u
        dm_scr[...] = dm.astype(BF16)
        for n in range(ts // CHUNK):
            rows = slice(n * CHUNK, (n + 1) * CHUNK)
            dmacc_scr[...] += dm[rows]
            for gi in range(GROUPS):
                cols = slice(gi * gw, (gi + 1) * gw)
                dws_ref[gi] += lax.dot_general(dm_scr[rows, cols], vn_scr[rows, cols], (((1,), (1,)), ((), ())),
                                               preferred_element_type=F32)
        _mix_positions(wst_ref, dm_scr, dvn_scr, gw, _triu)
        dvn = dvn_scr[...]
        dlg_ref[...] += _colsum(dvn * vhat)
        dlb_ref[...] += _colsum(dvn)
        dvh = dvn * ln_g
        dv = rstd * (dvh - jnp.mean(dvh, axis=-1, keepdims=True) - vhat * jnp.mean(dvh * vhat, axis=-1, keepdims=True))
        dp_ref[1] = (dv * dv_dpv).astype(BF16)

        @pl.when(i == n_i - 1)
        def _():
            for gi in range(GROUPS):
                dws_ref[gi] = _tril(dws_ref[gi])
                dbs_ref[:, gi:gi + 1] = jnp.sum(dmacc_scr[:, gi * gw:(gi + 1) * gw], axis=1, keepdims=True)

    def part(q):
        return pl.BlockSpec((None, ts, e), lambda i: (q, i, 0))

    w_spec = pl.BlockSpec((GROUPS, CHUNK, CHUNK), lambda i: (0, 0, 0))
    bs_spec = pl.BlockSpec((CHUNK, GROUPS), lambda i: (0, 0))
    return pl.pallas_call(
        body, name=name,
        out_shape=(jax.ShapeDtypeStruct((3, s, e), BF16), jax.ShapeDtypeStruct((GROUPS, CHUNK, CHUNK), F32),
                   jax.ShapeDtypeStruct((CHUNK, GROUPS), F32), jax.ShapeDtypeStruct((1, e), F32),
                   jax.ShapeDtypeStruct((1, e), F32)),
        grid=(n_i,),
        in_specs=[part(0), part(1), part(2), pl.BlockSpec((ts, d), lambda i: (i, 0)),
                  pl.BlockSpec((e, d), lambda i: (0, 0)), _vec_spec(e), _vec_spec(e), w_spec, w_spec, bs_spec],
        out_specs=(pl.BlockSpec((3, ts, e), lambda i: (0, i, 0)), w_spec, bs_spec, _vec_spec(e), _vec_spec(e)),
        scratch_shapes=[pltpu.VMEM((ts, e), BF16), pltpu.VMEM((ts, e), F32), pltpu.VMEM((ts, e), BF16),
                        pltpu.VMEM((ts, e), F32), pltpu.VMEM((CHUNK, e), F32)],
        compiler_params=_params("arbitrary"),
    )(proj, proj, proj, dbr, w_out, ln_g, ln_b, w_s, w_s_t, b_s_t)


def _mod_fwd(c_all, mod_w, mod_b_cols, name):
    n_layer, d, w = mod_w.shape

    def body(c_ref, w_ref, b_ref, ca_ref, o_ref):
        c = c_ref[...]
        ca = c * _sigmoid(c)
        ca_ref[...] = ca
        for li in range(n_layer):
            o_ref[li * N_DEV:(li + 1) * N_DEV, :] = (
                jnp.dot(ca, w_ref[li], preferred_element_type=F32, precision=lax.Precision.HIGHEST) + b_ref[li])

    return pl.pallas_call(
        body, name=name,
        out_shape=(jax.ShapeDtypeStruct((N_DEV, d), F32), jax.ShapeDtypeStruct((n_layer * N_DEV, w), F32)),
        in_specs=[VMEM_SPEC] * 3, out_specs=(VMEM_SPEC, VMEM_SPEC),
        compiler_params=pltpu.CompilerParams(vmem_limit_bytes=VMEM_LIMIT),
    )(c_all, mod_w, mod_b_cols)


def _adamw(w, g, m, v):
    m = ADAM_B1 * m + (1.0 - ADAM_B1) * g
    v = ADAM_B2 * v + (1.0 - ADAM_B2) * (g * g)
    m_hat = m / (1.0 - ADAM_B1 ** ADAM_STEP)
    v_hat = v / (1.0 - ADAM_B2 ** ADAM_STEP)
    delta = -ADAM_LR * (m_hat / (jnp.sqrt(v_hat) + ADAM_EPS) + ADAM_WD * w)
    return delta, m, v


def _adamw_2d(w, g, m, v, name):
    r, c = w.shape
    tr, tc = _tile(r, 512), _tile(c, 1024)

    def body(w_ref, g_ref, m_ref, v_ref, go_ref, d_ref, nm_ref, nv_ref):
        g = g_ref[...]
        go_ref[...] = g
        d_ref[...], nm_ref[...], nv_ref[...] = _adamw(w_ref[...], g, m_ref[...], v_ref[...])

    spec = pl.BlockSpec((tr, tc), lambda i, j: (i, j))
    shape = jax.ShapeDtypeStruct((r, c), F32)
    return pl.pallas_call(
        body, name=name, out_shape=(shape,) * 4, grid=(r // tr, c // tc),
        in_specs=[spec] * 4, out_specs=(spec,) * 4, compiler_params=_params("parallel", "parallel"),
    )(w, g, m, v)


def _mod_w_update(ca_t, dmod_cols, w, m, v, name):
    n_layer, d, wd = w.shape
    tr = _tile(d, 256)

    def body(ca_ref, dm_ref, w_ref, m_ref, v_ref, g_ref, d_ref, nm_ref, nv_ref):
        ca = ca_ref[...]
        dm = dm_ref[...]
        g = ca[:, 0:1] * dm[0:1, :]
        for b in range(1, N_DEV):
            g = g + ca[:, b:b + 1] * dm[b:b + 1, :]
        g_ref[...] = g
        d_ref[...], nm_ref[...], nv_ref[...] = _adamw(w_ref[...], g, m_ref[...], v_ref[...])

    spec = pl.BlockSpec((None, tr, wd), lambda l, i: (l, i, 0))
    shape = jax.ShapeDtypeStruct((n_layer, d, wd), F32)
    return pl.pallas_call(
        body, name=name, out_shape=(shape,) * 4, grid=(n_layer, d // tr),
        in_specs=[pl.BlockSpec((tr, N_DEV), lambda l, i: (i, 0)), pl.BlockSpec((None, N_DEV, wd), lambda l, i: (l, 0, 0)),
                  spec, spec, spec],
        out_specs=(spec,) * 4, compiler_params=_params("parallel", "parallel"),
    )(ca_t, dmod_cols, w, m, v)


def _adamw_small(ws, gs, ms, vs, name):
    n = len(ws)

    def body(*refs):
        ins, outs = refs[:4 * n], refs[4 * n:]
        for k in range(n):
            delta, nm, nv = _adamw(ins[k][...], ins[n + k][...], ins[2 * n + k][...], ins[3 * n + k][...])
            outs[3 * k][...] = delta
            outs[3 * k + 1][...] = nm
            outs[3 * k + 2][...] = nv

    out_shape = []
    for w in ws:
        out_shape += [jax.ShapeDtypeStruct(w.shape, F32)] * 3
    outs = pl.pallas_call(
        body, name=name, out_shape=tuple(out_shape),
        in_specs=[VMEM_SPEC] * (4 * n), out_specs=tuple([VMEM_SPEC] * (3 * n)),
        compiler_params=pltpu.CompilerParams(vmem_limit_bytes=VMEM_LIMIT),
    )(*ws, *gs, *ms, *vs)
    return [tuple(outs[3 * k:3 * k + 3]) for k in range(n)]


def _place():
    return lax.axis_index("x"), lax.axis_index("y"), lax.axis_index("c")


def _other_chips(x, y):
    return [(1 - x, y), (x, 1 - y), (1 - x, 1 - y)]


def _all_gather(block, name, after=None):
    m_per, n = block.shape

    def body(x_ref, *rest):
        out_ref, send_sems, recv_sems, local_sem = rest[-4:]
        x, y, c = _place()
        me, sibling = (x, y, c), (x, y, 1 - c)
        chips = _other_chips(x, y)

        def rows(px, py, pc):
            return out_ref.at[pl.ds((4 * px + 2 * py + pc) * m_per, m_per), :]

        def copy(k, blk, to, src=None):
            return pltpu.make_async_remote_copy(
                src_ref=rows(*blk) if src is None else src, dst_ref=rows(*blk),
                send_sem=send_sems.at[k], recv_sem=recv_sems.at[k], device_id=to, device_id_type=MESH)

        mine = pltpu.make_async_copy(x_ref, rows(*me), local_sem)
        mine.start()
        first = [copy(0, me, sibling, src=x_ref)]
        first += [copy(1 + j, me, (*chip, c), src=x_ref) for j, chip in enumerate(chips)]
        for cp in first:
            cp.start()
        passed = [copy(4 + j, (*chip, c), sibling) for j, chip in enumerate(chips)]
        for j, chip in enumerate(chips):
            copy(1 + j, (*chip, c), me).wait_recv()
            passed[j].start()
        copy(0, sibling, me).wait_recv()
        for j, chip in enumerate(chips):
            copy(4 + j, (*chip, 1 - c), me).wait_recv()
        for cp in first + passed:
            cp.wait_send()
        mine.wait()

    return pl.pallas_call(
        body, name=name, out_shape=jax.ShapeDtypeStruct((N_DEV * m_per, n), F32),
        in_specs=[VMEM_SPEC] + ([] if after is None else [ANY_SPEC]), out_specs=VMEM_SPEC,
        scratch_shapes=[pltpu.SemaphoreType.DMA((7,)), pltpu.SemaphoreType.DMA((7,)), pltpu.SemaphoreType.DMA],
        compiler_params=pltpu.CompilerParams(vmem_limit_bytes=VMEM_LIMIT),
    )(*([block] if after is None else [block, after]))


def _hbm(a):
    return pltpu.with_memory_space_constraint(a, pltpu.HBM)


def _all_gather_start(blocks, name):
    def body(in_ref, send_sems, recv_sems, thru_ref, token):
        token[...] = jnp.zeros_like(token)
        x, y, c = _place()
        mine = in_ref.at[4 * x + 2 * y + c]
        pltpu.make_async_remote_copy(
            src_ref=mine, dst_ref=mine, send_sem=send_sems.at[0], recv_sem=recv_sems.at[0],
            device_id=(x, y, 1 - c), device_id_type=MESH).start()
        for j, chip in enumerate(_other_chips(x, y)):
            for cc in range(2):
                pltpu.make_async_remote_copy(
                    src_ref=mine, dst_ref=mine, send_sem=send_sems.at[1 + 2 * j + cc],
                    recv_sem=recv_sems.at[1 + 2 * j + c], device_id=(*chip, cc), device_id_type=MESH).start()

    sems = pltpu.SemaphoreType.DMA((N_DEV - 1,))
    return pl.pallas_call(
        body, name=name,
        out_shape=(sems, sems, pltpu.HBM(blocks.shape, blocks.dtype), jax.ShapeDtypeStruct((SUBLANE, 128), F32)),
        in_specs=[HBM_SPEC], out_specs=(SEM_SPEC, SEM_SPEC, HBM_SPEC, VMEM_SPEC), input_output_aliases={0: 2},
        compiler_params=pltpu.CompilerParams(has_side_effects=EFFECT),
    )(_hbm(blocks))


def _all_gather_wait(send_sems, recv_sems, blocks, after, name):
    def body(in_ref, send_sems, recv_sems, after_ref, out_ref):
        x, y, c = _place()
        one = in_ref.at[0]
        for k in range(N_DEV - 1):
            pltpu.make_async_remote_copy(
                src_ref=one, dst_ref=one, send_sem=send_sems.at[k], recv_sem=recv_sems.at[k],
                device_id=(x, y, 1 - c), device_id_type=MESH).wait_send()
            pltpu.make_async_remote_copy(
                src_ref=one, dst_ref=one, send_sem=send_sems.at[k], recv_sem=recv_sems.at[k],
                device_id=(x, y, 1 - c), device_id_type=MESH).wait_recv()

    return pl.pallas_call(
        body, name=name, out_shape=pltpu.HBM(blocks.shape, blocks.dtype),
        in_specs=[HBM_SPEC, SEM_SPEC, SEM_SPEC, ANY_SPEC], out_specs=HBM_SPEC, input_output_aliases={0: 0},
        compiler_params=pltpu.CompilerParams(has_side_effects=EFFECT),
    )(blocks, send_sems, recv_sems, after)


def _place_own(shard, kind, chip_idx, name):
    r, cdim = shard.shape
    tr, tc = _tile(r, 512), _tile(cdim, 1024)
    nrb, ncb = r // tr, cdim // tc

    def body(k_ref, s_ref, o_ref):
        o_ref[...] = s_ref[...].astype(BF16)

    if kind == "col":
        full, o_map = (r, N_CHIP * cdim), lambda i, j, kr: (i, kr[0] * ncb + j)
    else:
        full, o_map = (N_CHIP * r, cdim), lambda i, j, kr: (kr[0] * nrb + i, j)
    return pl.pallas_call(
        body, name=name, out_shape=jax.ShapeDtypeStruct(full, BF16),
        grid_spec=pltpu.PrefetchScalarGridSpec(
            num_scalar_prefetch=1, grid=(nrb, ncb),
            in_specs=[pl.BlockSpec((tr, tc), lambda i, j, kr: (i, j))],
            out_specs=pl.BlockSpec((tr, tc), o_map)),
        compiler_params=_params("parallel", "parallel"),
    )(chip_idx, shard)


def _weight_window(ref, kind, shard_shape, k, half):
    r, cdim = shard_shape
    hr = r // 2
    if kind == "col":
        return ref.at[pl.ds(half * hr, hr), pl.ds(pl.multiple_of(k * cdim, 128), cdim)]
    return ref.at[pl.ds(pl.multiple_of(k * r + half * hr, 2 * SUBLANE), hr), :]


def _target_cores(c, both_cores):
    return [(cc, cc, c) for cc in range(2)] if both_cores else [(c, 0, 0)]


def _gather_weights_start(fulls, kinds, shard_shapes, groups, after, name, both_cores=True):
    n, ng = len(fulls), len(groups)
    per = 2 if both_cores else 1

    def body(*refs):
        ins = refs[:n]
        sems = refs[n + 1:n + 1 + 2 * ng]
        token = refs[2 * n + 1 + 2 * ng]
        x, y, c = _place()
        chips = _other_chips(x, y)
        for g, group in enumerate(groups):
            for pos, (w, j) in enumerate(group):
                own = _weight_window(ins[w], kinds[w], shard_shapes[w], 2 * x + y, c)
                for cc, mine, theirs in _target_cores(c, both_cores):
                    pltpu.make_async_remote_copy(
                        src_ref=own, dst_ref=own,
                        send_sem=sems[2 * g].at[per * pos + mine], recv_sem=sems[2 * g + 1].at[per * pos + theirs],
                        device_id=(*chips[j], cc), device_id_type=MESH).start()
        token[...] = jnp.zeros_like(token)

    sem_shapes = []
    for group in groups:
        sem_shapes += [pltpu.SemaphoreType.DMA((per * len(group),))] * 2
    outs = pl.pallas_call(
        body, name=name,
        out_shape=tuple(sem_shapes) + tuple(pltpu.HBM(f.shape, f.dtype) for f in fulls)
        + (jax.ShapeDtypeStruct((SUBLANE, 128), F32),),
        in_specs=[HBM_SPEC] * n + [ANY_SPEC], out_specs=(SEM_SPEC,) * (2 * ng) + (HBM_SPEC,) * n + (VMEM_SPEC,),
        input_output_aliases={w: 2 * ng + w for w in range(n)},
        compiler_params=pltpu.CompilerParams(has_side_effects=EFFECT),
    )(*[_hbm(f) for f in fulls], after)
    sems = [(outs[2 * g], outs[2 * g + 1]) for g in range(ng)]
    return sems, list(outs[2 * ng:2 * ng + n]), outs[2 * ng + n]


def _gather_weights_wait(sems, group, fulls, kinds, shard_shapes, after, name, both_cores=True):
    n = len(fulls)
    per = 2 if both_cores else 1

    def body(*refs):
        ins = refs[:n]
        send_sems, recv_sems = refs[n], refs[n + 1]
        x, y, c = _place()
        chips = _other_chips(x, y)
        for pos, (w, j) in enumerate(group):
            own = _weight_window(ins[w], kinds[w], shard_shapes[w], 2 * x + y, c)
            for cc, mine, _ in _target_cores(c, both_cores):
                landed = _weight_window(ins[w], kinds[w], shard_shapes[w], 2 * chips[j][0] + chips[j][1], cc)
                k = per * pos + mine
                pltpu.make_async_remote_copy(
                    src_ref=own, dst_ref=own, send_sem=send_sems.at[k], recv_sem=recv_sems.at[k],
                    device_id=(*chips[j], cc), device_id_type=MESH).wait_send()
                pltpu.make_async_remote_copy(
                    src_ref=landed, dst_ref=landed, send_sem=send_sems.at[k], recv_sem=recv_sems.at[k],
                    device_id=(*chips[j], cc), device_id_type=MESH).wait_recv()

    return list(pl.pallas_call(
        body, name=name, out_shape=tuple(pltpu.HBM(f.shape, f.dtype) for f in fulls),
        in_specs=[HBM_SPEC] * n + [SEM_SPEC, SEM_SPEC, ANY_SPEC], out_specs=(HBM_SPEC,) * n,
        input_output_aliases={w: w for w in range(n)},
        compiler_params=pltpu.CompilerParams(has_side_effects=EFFECT),
    )(*fulls, sems[0], sems[1], after))


def _forward_to_sibling(full, kind, shard_shape, relations, name):
    nr = len(relations)

    def body(in_ref, out_ref, send_sems, recv_sems):
        x, y, c = _place()
        chips = _other_chips(x, y)
        copies = []
        for pos, j in enumerate(relations):
            k = 2 * chips[j][0] + chips[j][1]
            mine = _weight_window(in_ref, kind, shard_shape, k, c)
            cp = pltpu.make_async_remote_copy(
                src_ref=mine, dst_ref=_weight_window(out_ref, kind, shard_shape, k, c),
                send_sem=send_sems.at[pos], recv_sem=recv_sems.at[pos], device_id=(x, y, 1 - c), device_id_type=MESH)
            cp.start()
            copies.append(cp)
        for pos, j in enumerate(relations):
            theirs = _weight_window(out_ref, kind, shard_shape, 2 * chips[j][0] + chips[j][1], 1 - c)
            pltpu.make_async_remote_copy(
                src_ref=theirs, dst_ref=theirs, send_sem=send_sems.at[pos], recv_sem=recv_sems.at[pos],
                device_id=(x, y, 1 - c), device_id_type=MESH).wait_recv()
        for cp in copies:
            cp.wait_send()

    return pl.pallas_call(
        body, name=name, out_shape=jax.ShapeDtypeStruct(full.shape, full.dtype),
        in_specs=[HBM_SPEC], out_specs=HBM_SPEC, input_output_aliases={0: 0},
        scratch_shapes=[pltpu.SemaphoreType.DMA((nr,))] * 2,
    )(full)


def _grad_piece(ref, kind, h, cdim, k, half):
    if kind == "col":
        return ref.at[pl.ds(half * h, h), pl.ds(pl.multiple_of(k * cdim, 128), cdim)]
    return ref.at[k, pl.ds(half * h, h), :]


def _grad_dims(g, kind):
    return (g.shape[0] // 2, g.shape[1] // N_CHIP) if kind == "col" else (g.shape[1] // 2, g.shape[2])


def _reduce_start(grads, kinds, name):
    n = len(grads)
    dims = [_grad_dims(g, kind) for g, kind in zip(grads, kinds)]
    lands = [lax.empty((N_DEV - 1, h, cdim), g.dtype) for g, (h, cdim) in zip(grads, dims)]

    def body(*refs):
        g_ins, land_ins = refs[:n], refs[n:2 * n]
        send_sems, recv_sems = refs[2 * n], refs[2 * n + 1]
        token = refs[4 * n + 2]
        x, y, c = _place()
        for w in range(n):
            h, cdim = dims[w]
            base = (N_DEV - 1) * w
            pltpu.make_async_remote_copy(
                src_ref=_grad_piece(g_ins[w], kinds[w], h, cdim, 2 * x + y, 1 - c), dst_ref=land_ins[w].at[0],
                send_sem=send_sems.at[base], recv_sem=recv_sems.at[base],
                device_id=(x, y, 1 - c), device_id_type=MESH).start()
            for j, chip in enumerate(_other_chips(x, y)):
                for cc in range(2):
                    pltpu.make_async_remote_copy(
                        src_ref=_grad_piece(g_ins[w], kinds[w], h, cdim, 2 * chip[0] + chip[1], cc),
                        dst_ref=land_ins[w].at[1 + 2 * j + c],
                        send_sem=send_sems.at[base + 1 + 2 * j + cc], recv_sem=recv_sems.at[base + 1 + 2 * j + c],
                        device_id=(*chip, cc), device_id_type=MESH).start()
        token[...] = jnp.zeros_like(token)

    sems = pltpu.SemaphoreType.DMA(((N_DEV - 1) * n,))
    outs = pl.pallas_call(
        body, name=name,
        out_shape=(sems, sems) + tuple(pltpu.HBM(a.shape, a.dtype) for a in list(grads) + lands)
        + (jax.ShapeDtypeStruct((SUBLANE, 128), F32),),
        in_specs=[HBM_SPEC] * (2 * n), out_specs=(SEM_SPEC, SEM_SPEC) + (HBM_SPEC,) * (2 * n) + (VMEM_SPEC,),
        input_output_aliases={i: 2 + i for i in range(2 * n)},
        compiler_params=pltpu.CompilerParams(has_side_effects=EFFECT),
    )(*[_hbm(a) for a in list(grads) + lands])
    return outs[0], outs[1], list(outs[2:2 + n]), list(outs[2 + n:2 + 2 * n]), outs[2 + 2 * n]


def _reduce_wait(send_sems, recv_sems, grads, lands, kinds, after, name):
    n = len(grads)
    dims = [_grad_dims(g, kind) for g, kind in zip(grads, kinds)]

    def body(*refs):
        g_ins, land_ins = refs[:n], refs[n:2 * n]
        send_sems, recv_sems = refs[2 * n], refs[2 * n + 1]
        x, y, c = _place()
        for w in range(n):
            h, cdim = dims[w]
            piece = _grad_piece(g_ins[w], kinds[w], h, cdim, 2 * x + y, c)
            for s in range(N_DEV - 1):
                k = (N_DEV - 1) * w + s
                slot = land_ins[w].at[s]
                pltpu.make_async_remote_copy(
                    src_ref=piece, dst_ref=slot, send_sem=send_sems.at[k], recv_sem=recv_sems.at[k],
                    device_id=(x, y, 1 - c), device_id_type=MESH).wait_send()
                pltpu.make_async_remote_copy(
                    src_ref=piece, dst_ref=slot, send_sem=send_sems.at[k], recv_sem=recv_sems.at[k],
                    device_id=(x, y, 1 - c), device_id_type=MESH).wait_recv()

    outs = pl.pallas_call(
        body, name=name, out_shape=tuple(pltpu.HBM(a.shape, a.dtype) for a in list(grads) + list(lands)),
        in_specs=[HBM_SPEC] * (2 * n) + [SEM_SPEC, SEM_SPEC, ANY_SPEC], out_specs=(HBM_SPEC,) * (2 * n),
        input_output_aliases={i: i for i in range(2 * n)},
        compiler_params=pltpu.CompilerParams(has_side_effects=EFFECT),
    )(*grads, *lands, send_sems, recv_sems, after)
    return list(outs[:n]), list(outs[n:])


def _add_pieces(g, land, kind, chip_idx, core_idx, name):
    _, h, cdim = land.shape
    tr, tc = _tile(h, 256), _tile(cdim, 2048)
    nrb, ncb = h // tr, cdim // tc

    def body(k_ref, c_ref, g_ref, l_ref, o_ref):
        acc = g_ref[...].astype(F32)
        for s in range(N_DEV - 1):
            acc = acc + l_ref[s].astype(F32)
        o_ref[...] = acc

    if kind == "col":
        g_spec = pl.BlockSpec((tr, tc), lambda i, j, kr, cr: (cr[0] * nrb + i, kr[0] * ncb + j))
    else:
        g_spec = pl.BlockSpec((None, tr, tc), lambda i, j, kr, cr: (kr[0], cr[0] * nrb + i, j))
    return pl.pallas_call(
        body, name=name, out_shape=jax.ShapeDtypeStruct((2 * h, cdim), F32),
        grid_spec=pltpu.PrefetchScalarGridSpec(
            num_scalar_prefetch=2, grid=(nrb, ncb),
            in_specs=[g_spec, pl.BlockSpec((N_DEV - 1, tr, tc), lambda i, j, kr, cr: (0, i, j))],
            out_specs=pl.BlockSpec((tr, tc), lambda i, j, kr, cr: (cr[0] * nrb + i, j))),
        compiler_params=_params("parallel", "parallel"),
    )(chip_idx, core_idx, g, land)


def _join_halves(shards, name):
    n = len(shards)

    def body(*refs):
        ins, outs = refs[:n], refs[n:2 * n]
        send_sems, recv_sems = refs[2 * n:]
        x, y, c = _place()
        copies = []
        for w in range(n):
            h = shards[w].shape[0] // 2
            cp = pltpu.make_async_remote_copy(
                src_ref=ins[w].at[pl.ds(c * h, h), :], dst_ref=outs[w].at[pl.ds(c * h, h), :],
                send_sem=send_sems.at[w], recv_sem=recv_sems.at[w], device_id=(x, y, 1 - c), device_id_type=MESH)
            cp.start()
            copies.append(cp)
        for w in range(n):
            h = shards[w].shape[0] // 2
            theirs = outs[w].at[pl.ds((1 - c) * h, h), :]
            pltpu.make_async_remote_copy(
                src_ref=theirs, dst_ref=theirs, send_sem=send_sems.at[w], recv_sem=recv_sems.at[w],
                device_id=(x, y, 1 - c), device_id_type=MESH).wait_recv()
        for cp in copies:
            cp.wait_send()

    return pl.pallas_call(
        body, name=name, out_shape=tuple(jax.ShapeDtypeStruct(s.shape, s.dtype) for s in shards),
        in_specs=[HBM_SPEC] * n, out_specs=tuple([HBM_SPEC] * n),
        input_output_aliases={w: w for w in range(n)},
        scratch_shapes=[pltpu.SemaphoreType.DMA((n,))] * 2,
    )(*shards)


def _sum_devices(gathered, rows, name):
    n = gathered.shape[1]

    def body(g_ref, o_ref):
        acc = g_ref[0:rows, :]
        for dev in range(1, N_DEV):
            acc = acc + g_ref[dev * rows:(dev + 1) * rows, :]
        o_ref[...] = acc

    return pl.pallas_call(
        body, name=name, out_shape=jax.ShapeDtypeStruct((rows, n), F32),
        in_specs=[VMEM_SPEC], out_specs=VMEM_SPEC,
        compiler_params=pltpu.CompilerParams(vmem_limit_bytes=VMEM_LIMIT),
    )(gathered)


def _pack(vectors, width):
    flat = [v.reshape(-1) for v in vectors]
    offsets, total = [], 0
    for f in flat:
        offsets.append(total)
        total += f.shape[0]
    rows = -(-total // (width * SUBLANE)) * SUBLANE
    flat.append(jnp.zeros((rows * width - total,), F32))
    return jnp.concatenate(flat).reshape(rows, width), offsets


def kernel(x, c, mod_w, mod_b, norm_g, a_w_in, a_conv_w, a_conv_b, a_w_out, b_w_in, b_ln_g, b_ln_b, b_w_s, b_b_s, b_w_out, final_g, loss_target, m_mod_w, m_mod_b, m_norm_g, m_a_w_in, m_a_conv_w, m_a_conv_b, m_a_w_out, m_b_w_in, m_b_ln_g, m_b_ln_b, m_b_w_s, m_b_b_s, m_b_w_out, m_final_g, v_mod_w, v_mod_b, v_norm_g, v_a_w_in, v_a_conv_w, v_a_conv_b, v_a_w_out, v_b_w_in, v_b_ln_g, v_b_ln_b, v_b_w_s, v_b_b_s, v_b_w_out, v_final_g):
    seq, d = x.shape[1], x.shape[2]
    e = a_conv_b.shape[1]
    wd = mod_w.shape[2]
    ax, ay, ac = _place()
    chip = 2 * ax + ay
    dev = 2 * chip + ac
    chip_idx = jnp.reshape(chip, (1,)).astype(jnp.int32)
    core_idx = jnp.reshape(ac, (1,)).astype(jnp.int32)

    x2d = x[0]
    target = loss_target[0]

    w_names = ["a_w_in", "a_w_out", "b_w_in", "b_w_out"]
    w_kinds = ["col", "row", "col", "row"]
    w_shards = [a_w_in[0], a_w_out[0], b_w_in[0], b_w_out[0]]
    w_shapes = [sh.shape for sh in w_shards]
    placed = [_place_own(sh, kind, chip_idx, "place_" + nm) for nm, sh, kind in zip(w_names, w_shards, w_kinds)]
    a_groups, far_groups = [[(0, 0), (0, 1)]], [[(0, 2)]]
    whole_group = [[(0, 0), (0, 1), (0, 2)]]

    es = e // N_CHIP
    packed0, offs0 = _pack([c, a_conv_w, b_ln_g, b_ln_b], 1024)
    gathered0 = _all_gather(packed0, "gather_params").reshape(N_DEV, -1)
    c_all = gathered0[:, :d]
    per_chip = gathered0[0::2]

    def from_chips_cols(k, rows_):
        got = per_chip[:, offs0[k]:offs0[k] + rows_ * es].reshape(N_CHIP, rows_, es)
        return jnp.transpose(got, (1, 0, 2)).reshape(rows_, e)

    conv_w_full = from_chips_cols(1, 3)
    conv_w = [conv_w_full[k:k + 1] for k in range(3)]
    ln_g, ln_b = from_chips_cols(2, 1), from_chips_cols(3, 1)
    mod_b_cols = lax.dynamic_slice_in_dim(mod_b, chip * wd, wd, axis=1)[:, None, :]
    c_act, mod_part = _mod_fwd(c_all, mod_w, mod_b_cols, "mod_fwd")
    n_layer = mod_w.shape[0]
    mod_gathered = _all_gather(mod_part, "gather_mod")
    a_sems, (wa_in,), a_token = _gather_weights_start(
        placed[:1], w_kinds[:1], w_shapes[:1], a_groups, mod_gathered, "gather_a_w_in_start", both_cores=False)
    mod_all = mod_gathered.reshape(N_CHIP, 2, n_layer, N_DEV, wd)[:, 0]
    mod_all = jnp.transpose(mod_all, (1, 2, 0, 3)).reshape(n_layer, N_DEV, N_CHIP * wd)
    mod_me = lax.dynamic_index_in_dim(mod_all, dev, axis=1, keepdims=False)
    shift = [mod_me[l:l + 1, 0:d] for l in range(n_layer)]
    scale = [mod_me[l:l + 1, d:2 * d] for l in range(n_layer)]
    gate = [mod_me[l:l + 1, 2 * d:3 * d] for l in range(n_layer)]

    g0, g1, gf = norm_g[0:1], norm_g[1:2], final_g[None, :]

    def slab(r):
        return jnp.bitwise_xor(chip_idx, r)

    def arrived(sems, group, weight, w, after, name, both_cores=True):
        return _gather_weights_wait(sems, [(0, j) for _, j in group], [weight], [w_kinds[w]], [w_shapes[w]],
                                    after, name, both_cores)[0]

    h0, proj0 = _norm_mod_proj_slab(x2d, g0, scale[0], shift[0], wa_in, slab(0), 4, "norm_mod0_a_proj_own", a_token)
    wa_in = arrived(a_sems[0], a_groups[0], wa_in, 0, proj0, "gather_wait_near", both_cores=False)
    wa_in = _forward_to_sibling(wa_in, w_kinds[0], w_shapes[0], [0, 1], "forward_near")
    far_sems, (wa_in,), far_token = _gather_weights_start(
        [wa_in], w_kinds[:1], w_shapes[:1], far_groups, a_token, "gather_a_w_in_far_start")
    proj0 = _mm_proj_slab(h0, wa_in, proj0, slab(2), 4, "a_proj_x", far_token)
    proj0 = _mm_proj_slab(h0, wa_in, proj0, slab(1), 4, "a_proj_y", far_token)
    wa_in = arrived(far_sems[0], far_groups[0], wa_in, 0, proj0, "gather_wait_far")
    def start_whole(w, after, name):
        sems, (flight,), token = _gather_weights_start(
            placed[w:w + 1], w_kinds[w:w + 1], w_shapes[w:w + 1], whole_group, after, name)
        return sems[0], flight, token

    ao_sems, wa_out, ao_token = start_whole(1, wa_in, "gather_a_w_out_start")
    proj0 = _mm_proj_slab(h0, wa_in, proj0, slab(3), 4, "a_proj_far", ao_token)
    wa_out = arrived(ao_sems, whole_group[0], wa_out, 1, proj0, "gather_wait_a_w_out")
    bi_sems, wb_in, bi_token = start_whole(2, wa_out, "gather_b_w_in_start")
    y0, br0, x1, h1 = _conv_fwd(proj0, conv_w, a_conv_b, wa_out, x2d, gate[0], g1, scale[1], shift[1],
                                "conv_fwd", bi_token)
    wb_in = arrived(bi_sems, whole_group[0], wb_in, 2, h1, "gather_wait_b_w_in")
    bo_sems, wb_out, bo_token = start_whole(3, wb_in, "gather_b_w_out_start")
    proj1 = _mm_proj(h1, wb_in, 3, "b_proj", bo_token)
    b_s_t = jnp.transpose(b_b_s[0])
    wb_out = arrived(bo_sems, whole_group[0], wb_out, 3, proj1, "gather_wait_b_w_out")
    y1, dx2, dbr1, loss_part, g_final_g, dgate1 = _gmlp_fwd_head(
        proj1, ln_g, ln_b, b_w_s[0], b_s_t, wb_out, x1, gate[1], gf, target, "gmlp_fwd_head")

    gw_b_out = _mm_dw_out(y1, dbr1, "b_out_dw")
    dproj1, g_w_s, g_b_s_t, g_ln_g, g_ln_b = _gmlp_bwd(
        proj1, dbr1, wb_out, ln_g, ln_b, b_w_s[0], jnp.swapaxes(b_w_s[0], 1, 2), b_s_t, "gmlp_bwd")
    gw_b_in = _mm_dw_in(h1, dproj1, "b_proj_dw")
    b_kinds = ["col", "row"]
    b_send, b_recv, b_grads, b_lands, b_token = _reduce_start(
        [gw_b_in, gw_b_out.reshape(N_CHIP, e // N_CHIP, d)], b_kinds, "reduce_b_start")
    dx1, dshift1, dscale1, g_g1, dbr0, dgate0 = _mm_dh_norm_bwd(
        dproj1, wb_in, x1, dx2, g1, scale[1], "b_proj_dx", b_token, br=br0, gate=gate[0])

    early = [loss_part[0, 0:1], g_final_g, g_g1, g_ln_g, g_ln_b, jnp.transpose(g_b_s_t), g_w_s,
             jnp.concatenate([dshift1, dscale1, dgate1, dgate0], axis=1)]
    packed_e, offs_e = _pack(early, 1024)
    rows_e = packed_e.shape[0]
    blocks_e = lax.dynamic_update_slice(jnp.zeros((N_DEV, rows_e, 1024), F32), packed_e[None], (dev, 0, 0))
    e_send, e_recv, blocks_e, e_token = _all_gather_start(blocks_e, "gather_small_start")

    gw_a_out = _mm_dw_out(y0, dbr0, "a_out_dw")
    ao_send, ao_recv, ao_grads, ao_lands, ao_grad_token = _reduce_start(
        [gw_a_out.reshape(N_CHIP, e // N_CHIP, d)], ["row"], "reduce_a_out_start")
    dproj0, g_w0, g_w1, g_w2, g_conv_b = _conv_bwd(
        proj0, dbr0, wa_out, conv_w, a_conv_b, "conv_bwd", ao_grad_token + e_token)
    gw_a_in = _mm_dw_in(h0, dproj0, "a_proj_dw")
    ai_send, ai_recv, ai_grads, ai_lands, ai_token = _reduce_start([gw_a_in], ["col"], "reduce_a_in_start")
    grad_x, dshift0, dscale0, g_g0 = _mm_dh_norm_bwd(dproj0, wa_in, x2d, dx1, g0, scale[0], "a_proj_dx", ai_token)

    def finish(send, recv, grads_, lands_, kinds_, names_, after, tag):
        grads_, lands_ = _reduce_wait(send, recv, grads_, lands_, kinds_, after, "reduce_" + tag + "_wait")
        halves = [_add_pieces(g, land, kind, chip_idx, core_idx, "add_pieces_" + nm)
                  for g, land, kind, nm in zip(grads_, lands_, kinds_, names_)]
        return _join_halves(halves, "join_" + tag)

    upd, big_grads = {}, {}

    def adamw_big(nm, w, g, m, v):
        g_out, *rest = _adamw_2d(w[0], g, m[0], v[0], "adamw_" + nm)
        big_grads[nm] = g_out[None]
        upd[nm] = tuple(o[None] for o in rest)

    g_b_w_in, g_b_w_out = finish(b_send, b_recv, b_grads, b_lands, b_kinds, ["b_w_in", "b_w_out"], grad_x, "b")
    adamw_big("b_w_in", b_w_in, g_b_w_in, m_b_w_in, v_b_w_in)
    adamw_big("b_w_out", b_w_out, g_b_w_out, m_b_w_out, v_b_w_out)

    late = [g_g0, jnp.concatenate([g_w0, g_w1, g_w2], axis=0), g_conv_b, jnp.concatenate([dshift0, dscale0], axis=1)]
    packed_l, offs_l = _pack(late, 1024)
    rows_l = packed_l.shape[0]
    gathered_l = _all_gather(packed_l, "gather_small_late", after=upd["b_w_out"][0])
    blocks_e = _all_gather_wait(e_send, e_recv, blocks_e, gathered_l, "gather_small_wait")
    gathered_e = blocks_e.reshape(N_DEV * rows_e, 1024)
    summed_e = _sum_devices(gathered_e, rows_e, "sum_small_early").reshape(-1)
    summed_l = _sum_devices(gathered_l, rows_l, "sum_small_late").reshape(-1)

    def take(summed, offs, k, shape):
        size = math.prod(shape)
        return summed[offs[k]:offs[k] + size].reshape(shape)

    def rows_of(gathered, rows_, offs, k, width):
        return gathered.reshape(N_DEV, rows_ * 1024)[:, offs[k]:offs[k] + width]

    loss = take(summed_e, offs_e, 0, ())
    grad_final_g = take(summed_e, offs_e, 1, (d,))
    grad_norm_g = jnp.concatenate([take(summed_l, offs_l, 0, (1, d)), take(summed_e, offs_e, 2, (1, d))], axis=0)
    grad_ln_g_full = take(summed_e, offs_e, 3, (1, e))
    grad_ln_b_full = take(summed_e, offs_e, 4, (1, e))
    grad_b_b_s = take(summed_e, offs_e, 5, (1, GROUPS, CHUNK))
    grad_b_w_s = take(summed_e, offs_e, 6, (1, GROUPS, CHUNK, CHUNK))
    grad_conv_w_full = take(summed_l, offs_l, 1, (3, e))
    grad_a_conv_b = take(summed_l, offs_l, 2, (1, e))
    grad_a_conv_w = lax.dynamic_slice_in_dim(grad_conv_w_full, chip * es, es, axis=1)[None]
    grad_b_ln_g = lax.dynamic_slice_in_dim(grad_ln_g_full, chip * es, es, axis=1)
    grad_b_ln_b = lax.dynamic_slice_in_dim(grad_ln_b_full, chip * es, es, axis=1)
    dmod_e = rows_of(gathered_e, rows_e, offs_e, 7, 4 * d)
    dmod_l = rows_of(gathered_l, rows_l, offs_l, 3, 2 * d)
    dmod_all = jnp.stack([jnp.concatenate([dmod_l, dmod_e[:, 3 * d:]], axis=1), dmod_e[:, :3 * d]], axis=1)
    mod_b_e = take(summed_e, offs_e, 7, (4 * d,))
    mod_b_l = take(summed_l, offs_l, 3, (2 * d,))
    grad_mod_b = jnp.stack([jnp.concatenate([mod_b_l, mod_b_e[3 * d:]]), mod_b_e[:3 * d]])
    dmod_cols = jnp.transpose(lax.dynamic_slice_in_dim(dmod_all, chip * wd, wd, axis=2), (1, 0, 2))

    grad_mod_w, delta_mod_w, new_m_mod_w, new_v_mod_w = _mod_w_update(
        jnp.transpose(c_act), dmod_cols, mod_w, m_mod_w, v_mod_w, "mod_w_update")
    (g_a_w_out,) = finish(ao_send, ao_recv, ao_grads, ao_lands, ["row"], ["a_w_out"], delta_mod_w, "a_out")
    adamw_big("a_w_out", a_w_out, g_a_w_out, m_a_w_out, v_a_w_out)
    (g_a_w_in,) = finish(ai_send, ai_recv, ai_grads, ai_lands, ["col"], ["a_w_in"], upd["a_w_out"][0], "a_in")
    adamw_big("a_w_in", a_w_in, g_a_w_in, m_a_w_in, v_a_w_in)
    small_w = [("mod_b", mod_b, grad_mod_b, m_mod_b, v_mod_b), ("norm_g", norm_g, grad_norm_g, m_norm_g, v_norm_g),
               ("a_conv_w", a_conv_w, grad_a_conv_w, m_a_conv_w, v_a_conv_w),
               ("a_conv_b", a_conv_b, grad_a_conv_b, m_a_conv_b, v_a_conv_b),
               ("b_ln_g", b_ln_g, grad_b_ln_g, m_b_ln_g, v_b_ln_g), ("b_ln_b", b_ln_b, grad_b_ln_b, m_b_ln_b, v_b_ln_b),
               ("b_w_s", b_w_s, grad_b_w_s, m_b_w_s, v_b_w_s), ("b_b_s", b_b_s, grad_b_b_s, m_b_b_s, v_b_b_s),
               ("final_g", final_g, grad_final_g, m_final_g, v_final_g)]

    def flat2d(a):
        return a.reshape(-1, a.shape[-1])

    res = _adamw_small([flat2d(t[1]) for t in small_w], [flat2d(t[2]) for t in small_w],
                       [flat2d(t[3]) for t in small_w], [flat2d(t[4]) for t in small_w], "adamw_small")
    for (nm, w, _, _, _), r3 in zip(small_w, res):
        upd[nm] = tuple(o.reshape(w.shape) for o in r3)
    upd["mod_w"] = (delta_mod_w, new_m_mod_w, new_v_mod_w)

    grads = {"mod_w": grad_mod_w, "mod_b": grad_mod_b, "norm_g": grad_norm_g, "a_conv_w": grad_a_conv_w,
             "a_conv_b": grad_a_conv_b, "b_ln_g": grad_b_ln_g, "b_ln_b": grad_b_ln_b, "b_w_s": grad_b_w_s,
             "b_b_s": grad_b_b_s, "final_g": grad_final_g, **big_grads}
    order = ["mod_w", "mod_b", "norm_g", "a_w_in", "a_conv_w", "a_conv_b", "a_w_out", "b_w_in", "b_ln_g", "b_ln_b",
             "b_w_s", "b_b_s", "b_w_out", "final_g"]
    return (loss, grad_x[None], *[grads[k] for k in order], *[upd[k][0] for k in order],
            *[upd[k][1] for k in order], *[upd[k][2] for k in order])
```

```python
import math

import jax
import jax.numpy as jnp
from jax import lax
from jax.experimental import pallas as pl
from jax.experimental.pallas import tpu as pltpu

F32 = jnp.float32
BF16 = jnp.bfloat16
MESH = pl.DeviceIdType.MESH

N_DEV = 8
N_CHIP = 4
SUBLANE = 8
PACK = 16
ACT = F32
RMS_EPS = 1e-6
LN_EPS = 1e-5
CHUNK = 128
GROUPS = 8
ADAM_LR = 0.001
ADAM_B1 = 0.9
ADAM_B2 = 0.999
ADAM_EPS = 1e-08
ADAM_WD = 0.01
ADAM_STEP = 10
VMEM_LIMIT = 56 << 20

HBM_SPEC = pl.BlockSpec(memory_space=pltpu.HBM)
VMEM_SPEC = pl.BlockSpec(memory_space=pltpu.VMEM)
SEM_SPEC = pl.BlockSpec(memory_space=pltpu.SEMAPHORE)
ANY_SPEC = pl.BlockSpec(memory_space=pl.ANY)
EFFECT = pltpu.SideEffectType.DATAFLOW_SIDE_EFFECTING


def _params(*sem):
    return pltpu.CompilerParams(dimension_semantics=sem, vmem_limit_bytes=VMEM_LIMIT)


def _tile(n, want):
    if n <= want:
        return n
    t = want
    while n % t:
        t -= 128
    return t


def _sigmoid(x):
    return 0.5 * jnp.tanh(0.5 * x) + 0.5


def _silu_and_grad(x):
    s = _sigmoid(x)
    return x * s, s * (1.0 + x * (1.0 - s))


def _gelu_and_grad(x):
    cdf = 0.5 * (1.0 + lax.erf(x * (1.0 / math.sqrt(2.0))))
    pdf = jnp.exp(-0.5 * x * x) * (1.0 / math.sqrt(2.0 * math.pi))
    return x * cdf, cdf + x * pdf


def _gelu(x):
    return x * (0.5 * (1.0 + lax.erf(x * (1.0 / math.sqrt(2.0)))))


def _rms(x):
    r = lax.rsqrt(jnp.mean(x * x, axis=-1, keepdims=True) + RMS_EPS)
    return x * r, r


def _rms_bwd(dxn, xn, r):
    return r * (dxn - xn * jnp.mean(dxn * xn, axis=-1, keepdims=True))


def _colsum(a):
    return jnp.sum(a, axis=0, keepdims=True)


def _shift_down(cur, before, k):
    rolled = pltpu.roll(cur, k, 0)
    row = lax.broadcasted_iota(jnp.int32, before.shape, 0)
    head = jnp.where(row < k, pltpu.roll(before, k, 0), rolled[:SUBLANE])
    return jnp.concatenate([head, rolled[SUBLANE:]], axis=0)


def _shift_up(cur, after, k):
    n = cur.shape[0]
    rolled = pltpu.roll(cur, n - k, 0)
    row = lax.broadcasted_iota(jnp.int32, after.shape, 0)
    tail = jnp.where(row >= SUBLANE - k, pltpu.roll(after, SUBLANE - k, 0), rolled[n - SUBLANE:])
    return jnp.concatenate([rolled[:n - SUBLANE], tail], axis=0)


def _after_spec():
    return pl.BlockSpec((SUBLANE, 128), lambda *_: (0, 0))


def _mm_proj(h, w, n_split, name, after):
    s, d = h.shape
    e = w.shape[1] // n_split
    tm, tn = _tile(s, 2048), _tile(e, 2048)
    nj = e // tn

    def body(h_ref, w_ref, after_ref, o_ref):
        o_ref[...] = jnp.dot(h_ref[...], w_ref[...], preferred_element_type=F32).astype(ACT)

    return pl.pallas_call(
        body, name=name,
        out_shape=jax.ShapeDtypeStruct((n_split, s, e), ACT),
        grid=(s // tm, n_split * nj),
        in_specs=[pl.BlockSpec((tm, d), lambda i, j: (i, 0)), pl.BlockSpec((d, tn), lambda i, j: (0, j)),
                  _after_spec()],
        out_specs=pl.BlockSpec((None, tm, tn), lambda i, j: (j // nj, i, j % nj)),
        compiler_params=_params("parallel", "parallel"),
    )(h, w, after)


def _mm_proj_slab(h, w, proj, q_idx, n_split, name, after):
    s, d = h.shape
    e = w.shape[1] // n_split
    tm, tn = _tile(s, 1024), _tile(e, 2048)
    nj = e // tn

    def body(q_ref, h_ref, w_ref, after_ref, proj_ref, o_ref):
        o_ref[...] = jnp.dot(h_ref[...], w_ref[...], preferred_element_type=F32).astype(ACT)

    return pl.pallas_call(
        body, name=name,
        out_shape=jax.ShapeDtypeStruct((n_split, s, e), ACT),
        grid_spec=pltpu.PrefetchScalarGridSpec(
            num_scalar_prefetch=1, grid=(s // tm, nj),
            in_specs=[pl.BlockSpec((tm, d), lambda i, j, qr: (i, 0)),
                      pl.BlockSpec((d, tn), lambda i, j, qr: (0, qr[0] * nj + j)), _after_spec(), ANY_SPEC],
            out_specs=pl.BlockSpec((None, tm, tn), lambda i, j, qr: (qr[0], i, j))),
        input_output_aliases={4: 0},
        compiler_params=_params("parallel", "parallel"),
    )(q_idx, h, w, after, proj)


def _mm_dh_norm_bwd(dp, w, x, dx_in, g, scale, name, after, br=None, gate=None):
    nq, s, e = dp.shape
    d = w.shape[0]
    has_branch = br is not None
    tm, tk = _tile(s, 1024), _tile(e, 1024 if has_branch else 2048)
    nkq = e // tk
    nk = nq * nkq

    def body(*refs):
        if has_branch:
            (a_ref, b_ref, after_ref, x_ref, dxin_ref, g_ref, sc_ref, br_ref, gate_ref,
             dx_ref, dsh_ref, dsc_ref, dg_ref, dbr_ref, dgate_ref, acc_ref) = refs
        else:
            (a_ref, b_ref, after_ref, x_ref, dxin_ref, g_ref, sc_ref,
             dx_ref, dsh_ref, dsc_ref, dg_ref, acc_ref) = refs
        i, k = pl.program_id(0), pl.program_id(1)

        @pl.when(jnp.logical_and(i == 0, k == 0))
        def _():
            dsh_ref[...] = jnp.zeros_like(dsh_ref)
            dsc_ref[...] = jnp.zeros_like(dsc_ref)
            dg_ref[...] = jnp.zeros_like(dg_ref)
            if has_branch:
                dgate_ref[...] = jnp.zeros_like(dgate_ref)

        @pl.when(k == 0)
        def _():
            acc_ref[...] = jnp.zeros_like(acc_ref)
        acc_ref[...] += lax.dot_general(a_ref[...], b_ref[...], (((1,), (1,)), ((), ())), preferred_element_type=F32)

        @pl.when(k == nk - 1)
        def _():
            dh = acc_ref[...]
            g_vec = g_ref[...]
            xn, r = _rms(x_ref[...])
            dsh_ref[...] += _colsum(dh)
            dsc_ref[...] += _colsum(dh * (xn * g_vec))
            da = dh * (1.0 + sc_ref[...])
            dg_ref[...] += _colsum(da * xn)
            dx = dxin_ref[...] + _rms_bwd(da * g_vec, xn, r)
            dx_ref[...] = dx
            if has_branch:
                dgate_ref[...] += _colsum(dx * br_ref[...])
                dbr_ref[...] = (gate_ref[...] * dx).astype(BF16)

    rows = pl.BlockSpec((tm, d), lambda i, k: (i, 0))
    vec = pl.BlockSpec((1, d), lambda i, k: (0, 0))
    vec_out = jax.ShapeDtypeStruct((1, d), F32)
    in_specs = [pl.BlockSpec((None, tm, tk), lambda i, k: (k // nkq, i, k % nkq)),
                pl.BlockSpec((d, tk), lambda i, k: (0, k)), _after_spec(), rows, rows, vec, vec]
    out_shape = [jax.ShapeDtypeStruct((s, d), F32), vec_out, vec_out, vec_out]
    out_specs = [rows, vec, vec, vec]
    args = [dp, w, after, x, dx_in, g, scale]
    if has_branch:
        in_specs += [rows, vec]
        out_shape += [jax.ShapeDtypeStruct((s, d), BF16), vec_out]
        out_specs += [rows, vec]
        args += [br, gate]
    return pl.pallas_call(
        body, name=name, out_shape=tuple(out_shape), grid=(s // tm, nk),
        in_specs=in_specs, out_specs=tuple(out_specs), scratch_shapes=[pltpu.VMEM((tm, d), F32)],
        compiler_params=_params("arbitrary", "arbitrary"),
    )(*args)


def _dw_body(n_t):
    def body(a_ref, b_ref, o_ref, acc_ref):
        t = pl.program_id(2)

        @pl.when(t == 0)
        def _():
            acc_ref[...] = jnp.zeros_like(acc_ref)
        acc_ref[...] += lax.dot_general(a_ref[...], b_ref[...], (((0,), (0,)), ((), ())), preferred_element_type=F32)

        @pl.when(t == n_t - 1)
        def _():
            o_ref[...] = acc_ref[...].astype(o_ref.dtype)
    return body


def _mm_dw_in(h, dp, name):
    s, d = h.shape
    nq, _, e = dp.shape
    tm, tn, tt = _tile(d, 1024), _tile(e, 2048), _tile(s, 2048)
    nj = e // tn

    return pl.pallas_call(
        _dw_body(s // tt), name=name,
        out_shape=jax.ShapeDtypeStruct((d, nq * e), BF16),
        grid=(d // tm, nq * nj, s // tt),
        in_specs=[pl.BlockSpec((tt, tm), lambda i, j, t: (t, i)),
                  pl.BlockSpec((None, tt, tn), lambda i, j, t: (j // nj, t, j % nj))],
        out_specs=pl.BlockSpec((tm, tn), lambda i, j, t: (i, j)),
        scratch_shapes=[pltpu.VMEM((tm, tn), F32)],
        compiler_params=_params("parallel", "parallel", "arbitrary"),
    )(h, dp)


def _mm_dw_out(y, dbr, name):
    s, e = y.shape
    d = dbr.shape[1]
    tm, tn, tt = _tile(e, 1024), _tile(d, 1024), _tile(s, 2048)

    return pl.pallas_call(
        _dw_body(s // tt), name=name,
        out_shape=jax.ShapeDtypeStruct((e, d), BF16),
        grid=(e // tm, d // tn, s // tt),
        in_specs=[pl.BlockSpec((tt, tm), lambda i, j, t: (t, i)), pl.BlockSpec((tt, tn), lambda i, j, t: (t, j))],
        out_specs=pl.BlockSpec((tm, tn), lambda i, j, t: (i, j)),
        scratch_shapes=[pltpu.VMEM((tm, tn), F32)],
        compiler_params=_params("parallel", "parallel", "arbitrary"),
    )(y, dbr)


def _row_spec(ts, d):
    return pl.BlockSpec((ts, d), lambda i: (i, 0))


def _vec_spec(d):
    return pl.BlockSpec((1, d), lambda i: (0, 0))


def _norm_mod_proj_slab(x, g, scale, shift, w, q_idx, n_split, name, after):
    s, d = x.shape
    e = w.shape[1] // n_split
    tm = _tile(s, 1024)

    def body(q_ref, x_ref, g_ref, sc_ref, sh_ref, w_ref, after_ref, h_ref, o_ref):
        xn, _ = _rms(x_ref[...])
        h = ((xn * g_ref[...]) * (1.0 + sc_ref[...]) + sh_ref[...]).astype(BF16)
        h_ref[...] = h
        o_ref[...] = jnp.dot(h, w_ref[...], preferred_element_type=F32).astype(ACT)

    vec = pl.BlockSpec((1, d), lambda i, qr: (0, 0))
    rows = pl.BlockSpec((tm, d), lambda i, qr: (i, 0))
    return pl.pallas_call(
        body, name=name,
        out_shape=(jax.ShapeDtypeStruct((s, d), BF16), jax.ShapeDtypeStruct((n_split, s, e), ACT)),
        grid_spec=pltpu.PrefetchScalarGridSpec(
            num_scalar_prefetch=1, grid=(s // tm,),
            in_specs=[rows, vec, vec, vec, pl.BlockSpec((d, e), lambda i, qr: (0, qr[0])), _after_spec()],
            out_specs=(rows, pl.BlockSpec((None, tm, e), lambda i, qr: (qr[0], i, 0)))),
        compiler_params=_params("parallel"),
    )(q_idx, x, g, scale, shift, w, after)


def _head_tile(x1, br, gate, gf, target):
    d = x1.shape[-1]
    xn, r = _rms(x1 + gate * br)
    err = xn * gf - target
    loss = 0.5 * jnp.sum(jnp.mean(err * err, axis=-1, keepdims=True))
    dout = err * (1.0 / d)
    dx = _rms_bwd(dout * gf, xn, r)
    return dx, gate * dx, loss, _colsum(dout * xn), _colsum(dx * br)


def _f32(ref):
    return ref[...].astype(F32)


def _cols_f32(ref, cols):
    return ref[:, cols].astype(F32)


def _rows_before(halo_ref, cols=slice(None)):
    return _cols_f32(halo_ref, cols)[PACK - SUBLANE:]


def _rows_after(halo_ref, cols=slice(None)):
    return _cols_f32(halo_ref, cols)[:SUBLANE]


CONV_CHAIN_COLS = 512


def _conv_fwd(proj, conv_w, conv_b, w_out, x, gate, g_next, scale_next, shift_next, name, after):
    _, s, e = proj.shape
    d = w_out.shape[1]
    ts, te = _tile(s, 256), e
    cw = min(CONV_CHAIN_COLS, te)
    hb = ts // PACK
    n_i = s // ts
    slots = 3

    def body(proj_ref, cgp_ref, xip_ref, w0_ref, w1_ref, w2_ref, b_ref, wo_ref, after_ref,
             x_ref, gate_ref, g_ref, sc_ref, sh_ref, y_ref, br_ref, x1_ref, h_ref, ring, ring_sems):
        i = pl.program_id(0)
        first = i == 0

        def fetch(step, slot):
            rows = pl.ds(step * ts if isinstance(step, int) else pl.multiple_of(step * ts, ts), ts)
            return [pltpu.make_async_copy(proj_ref.at[q, rows, :], ring.at[slot, q], ring_sems.at[slot, q])
                    for q in range(4)]

        @pl.when(first)
        def _():
            for step in range(min(slots - 1, n_i)):
                for cp in fetch(step, step):
                    cp.start()

        @pl.when(i + (slots - 1) < n_i)
        def _():
            for cp in fetch(i + (slots - 1), (i + (slots - 1)) % slots):
                cp.start()
        slot = i % slots
        for cp in fetch(i, slot):
            cp.wait()

        def part_f32(q, cols):
            return ring[slot, q, :, cols].astype(F32)

        br = None
        for c0 in range(0, te, cw):
            cols = slice(c0, c0 + cw)
            cx = part_f32(1, cols) * part_f32(2, cols)
            before = jnp.where(first, 0.0, _rows_before(cgp_ref, cols) * _rows_before(xip_ref, cols))
            conv = b_ref[:, cols] + w2_ref[:, cols] * cx
            conv = conv + w0_ref[:, cols] * _shift_down(cx, before, 2)
            conv = conv + w1_ref[:, cols] * _shift_down(cx, before, 1)
            z = part_f32(3, cols)
            y = ((z * _sigmoid(z)) * part_f32(0, cols) * conv).astype(BF16)
            y_ref[:, cols] = y
            part_br = jnp.dot(y, wo_ref[cols, :], preferred_element_type=F32)
            br = part_br if br is None else br + part_br
        br_ref[...] = br
        x1 = x_ref[...] + gate_ref[...] * br
        x1_ref[...] = x1
        xn, _ = _rms(x1)
        h_ref[...] = ((xn * g_ref[...]) * (1.0 + sc_ref[...]) + sh_ref[...]).astype(BF16)

    def halo_before(q):
        return pl.BlockSpec((None, PACK, te), lambda i: (q, jnp.maximum(i * hb - 1, 0), 0))

    return pl.pallas_call(
        body, name=name,
        out_shape=(jax.ShapeDtypeStruct((s, e), BF16), jax.ShapeDtypeStruct((s, d), F32),
                   jax.ShapeDtypeStruct((s, d), F32), jax.ShapeDtypeStruct((s, d), BF16)), grid=(n_i,),
        in_specs=[ANY_SPEC, halo_before(1), halo_before(2)]
        + [_vec_spec(e)] * 4 + [pl.BlockSpec((e, d), lambda i: (0, 0), pipeline_mode=pl.Buffered(1)), _after_spec(),
                                _row_spec(ts, d)]
        + [_vec_spec(d)] * 4,
        out_specs=(_row_spec(ts, e), _row_spec(ts, d), _row_spec(ts, d), _row_spec(ts, d)),
        scratch_shapes=[pltpu.VMEM((slots, 4, ts, te), proj.dtype), pltpu.SemaphoreType.DMA((slots, 4))],
        compiler_params=_params("arbitrary"),
    )(proj, proj, proj, *conv_w, conv_b, w_out, after, x, gate, g_next, scale_next, shift_next)


def _conv_bwd(proj, dbr, w_out, conv_w, conv_b, name, after):
    _, s, e = proj.shape
    d = dbr.shape[1]
    ts, te = _tile(s, 512), _tile(e, 1024)
    cw = min(CONV_CHAIN_COLS // 2, te)
    hb = ts // PACK
    n_i = s // ts
    last_halo = s // PACK - 1
    nt = (((1,), (1,)), ((), ()))

    def body(bg_ref, cg_ref, xi_ref, z_ref, dbr_ref, cgp_ref, xip_ref, bgn_ref, zn_ref, dbrn_ref, wo_ref,
             w0_ref, w1_ref, w2_ref, b_ref, after_ref, dp_ref, dw0_ref, dw1_ref, dw2_ref, db_ref):
        i = pl.program_id(1)

        @pl.when(i == 0)
        def _():
            for acc in (dw0_ref, dw1_ref, dw2_ref, db_ref):
                acc[...] = jnp.zeros_like(acc)
        for c0 in range(0, te, cw):
            cols = slice(c0, c0 + cw)
            wo = wo_ref[cols, :]
            dy = lax.dot_general(dbr_ref[...], wo, nt, preferred_element_type=F32)
            dyn = lax.dot_general(dbrn_ref[...], wo, nt, preferred_element_type=F32)[:SUBLANE]
            bg, cg = _cols_f32(bg_ref, cols), _cols_f32(cg_ref, cols)
            xi, z = _cols_f32(xi_ref, cols), _cols_f32(z_ref, cols)
            w0, w1, w2 = w0_ref[:, cols], w1_ref[:, cols], w2_ref[:, cols]
            cx = cg * xi
            before = jnp.where(i > 0, _rows_before(cgp_ref, cols) * _rows_before(xip_ref, cols), 0.0)
            cx1 = _shift_down(cx, before, 1)
            cx2 = _shift_down(cx, before, 2)
            conv = b_ref[:, cols] + w2 * cx
            conv = conv + w0 * cx2
            conv = conv + w1 * cx1
            sz, dsz = _silu_and_grad(z)
            dp_ref[3, :, cols] = (dy * bg * conv * dsz).astype(BF16)
            dp_ref[0, :, cols] = (dy * sz * conv).astype(BF16)
            dconv = dy * sz * bg
            zn = _rows_after(zn_ref, cols)
            after = jnp.where(i < n_i - 1, dyn * (zn * _sigmoid(zn)) * _rows_after(bgn_ref, cols), 0.0)
            db_ref[:, cols] += _colsum(dconv)
            dw2_ref[:, cols] += _colsum(dconv * cx)
            dw1_ref[:, cols] += _colsum(dconv * cx1)
            dw0_ref[:, cols] += _colsum(dconv * cx2)
            dcx = w2 * dconv + w1 * _shift_up(dconv, after, 1) + w0 * _shift_up(dconv, after, 2)
            dp_ref[1, :, cols] = (dcx * xi).astype(BF16)
            dp_ref[2, :, cols] = (dcx * cg).astype(BF16)

    def part(q):
        return pl.BlockSpec((None, ts, te), lambda j, i: (q, i, j))

    def halo_before(q):
        return pl.BlockSpec((None, PACK, te), lambda j, i: (q, jnp.maximum(i * hb - 1, 0), j))

    def halo_after(q):
        return pl.BlockSpec((None, PACK, te), lambda j, i: (q, jnp.minimum((i + 1) * hb, last_halo), j))

    return pl.pallas_call(
        body, name=name,
        out_shape=(jax.ShapeDtypeStruct((4, s, e), BF16),) + (jax.ShapeDtypeStruct((1, e), F32),) * 4,
        grid=(e // te, n_i),
        in_specs=[part(0), part(1), part(2), part(3), pl.BlockSpec((ts, d), lambda j, i: (i, 0)),
                  halo_before(1), halo_before(2), halo_after(0), halo_after(3),
                  pl.BlockSpec((PACK, d), lambda j, i: (jnp.minimum((i + 1) * hb, last_halo), 0)),
                  pl.BlockSpec((te, d), lambda j, i: (j, 0))]
        + [pl.BlockSpec((1, te), lambda j, i: (0, j))] * 4 + [_after_spec()],
        out_specs=(pl.BlockSpec((4, ts, te), lambda j, i: (0, i, j)),) + (pl.BlockSpec((1, te), lambda j, i: (0, j)),) * 4,
        compiler_params=_params("parallel", "arbitrary"),
    )(proj, proj, proj, proj, dbr, proj, proj, proj, proj, dbr, w_out, *conv_w, conv_b, after)


def _tril(w):
    row = lax.broadcasted_iota(jnp.int32, w.shape, 0)
    col = lax.broadcasted_iota(jnp.int32, w.shape, 1)
    return jnp.where(row >= col, w, 0.0)


def _triu(w):
    row = lax.broadcasted_iota(jnp.int32, w.shape, 0)
    col = lax.broadcasted_iota(jnp.int32, w.shape, 1)
    return jnp.where(row <= col, w, 0.0)


def _layer_norm_fwd(v, g, b):
    mu = jnp.mean(v, axis=-1, keepdims=True)
    vc = v - mu
    rstd = lax.rsqrt(jnp.mean(vc * vc, axis=-1, keepdims=True) + LN_EPS)
    vhat = vc * rstd
    return vhat * g + b, vhat, rstd


GMLP_CHUNKS_PER_STEP = 2


def _gmlp_rows(s):
    return CHUNK * min(GMLP_CHUNKS_PER_STEP, s // CHUNK)


def _mix_positions(w_ref, src_scr, dst_scr, gw, mask, bias_ref=None):
    for gi in range(GROUPS):
        cols = slice(gi * gw, (gi + 1) * gw)
        wm = mask(w_ref[gi]).astype(BF16)
        for n in range(src_scr.shape[0] // CHUNK):
            rows = slice(n * CHUNK, (n + 1) * CHUNK)
            out = jnp.dot(wm, src_scr[rows, cols], preferred_element_type=F32)
            if bias_ref is not None:
                out = out + bias_ref[:, gi:gi + 1]
            dst_scr[rows, cols] = out


def _gmlp_fwd_head(proj, ln_g, ln_b, w_s, b_s_t, w_out, x1, gate, gf, target, name):
    _, s, e = proj.shape
    d = w_out.shape[1]
    gw = e // GROUPS
    ts = CHUNK * min(2 * GMLP_CHUNKS_PER_STEP, s // CHUNK)

    def body(pu_ref, pv_ref, pz_ref, g_ref, b_ref, ws_ref, bs_ref, wo_ref, x1_ref, gate_ref, gf_ref, tg_ref,
             y_ref, dx_ref, dbr_ref, loss_ref, dgf_ref, dgate_ref, vn_scr, mix_scr):
        @pl.when(pl.program_id(0) == 0)
        def _():
            loss_ref[...] = jnp.zeros_like(loss_ref)
            dgf_ref[...] = jnp.zeros_like(dgf_ref)
            dgate_ref[...] = jnp.zeros_like(dgate_ref)
        vn, _, _ = _layer_norm_fwd(_gelu(_f32(pv_ref)), g_ref[...], b_ref[...])
        vn_scr[...] = vn.astype(BF16)
        _mix_positions(ws_ref, vn_scr, mix_scr, gw, _tril, bs_ref)
        z = _f32(pz_ref)
        y = ((z * _sigmoid(z)) * (_gelu(_f32(pu_ref)) * mix_scr[...])).astype(BF16)
        y_ref[...] = y
        br = jnp.dot(y, wo_ref[...], preferred_element_type=F32)
        dx, dbr, loss, dgf, dgate = _head_tile(x1_ref[...], br, gate_ref[...], gf_ref[...], tg_ref[...])
        dx_ref[...] = dx
        dbr_ref[...] = dbr.astype(BF16)
        loss_ref[...] += loss
        dgf_ref[...] += dgf
        dgate_ref[...] += dgate

    def part(q):
        return pl.BlockSpec((None, ts, e), lambda i: (q, i, 0))

    return pl.pallas_call(
        body, name=name,
        out_shape=(jax.ShapeDtypeStruct((s, e), BF16), jax.ShapeDtypeStruct((s, d), F32),
                   jax.ShapeDtypeStruct((s, d), BF16), jax.ShapeDtypeStruct((SUBLANE, 128), F32),
                   jax.ShapeDtypeStruct((1, d), F32), jax.ShapeDtypeStruct((1, d), F32)),
        grid=(s // ts,),
        in_specs=[part(0), part(1), part(2), _vec_spec(e), _vec_spec(e),
                  pl.BlockSpec((GROUPS, CHUNK, CHUNK), lambda i: (0, 0, 0), pipeline_mode=pl.Buffered(1)),
                  pl.BlockSpec((CHUNK, GROUPS), lambda i: (0, 0)),
                  pl.BlockSpec((e, d), lambda i: (0, 0), pipeline_mode=pl.Buffered(1)),
                  _row_spec(ts, d), _vec_spec(d), _vec_spec(d), _row_spec(ts, d)],
        out_specs=(_row_spec(ts, e), _row_spec(ts, d), _row_spec(ts, d),
                   pl.BlockSpec((SUBLANE, 128), lambda i: (0, 0)), _vec_spec(d), _vec_spec(d)),
        scratch_shapes=[pltpu.VMEM((ts, e), BF16), pltpu.VMEM((ts, e), F32)],
        compiler_params=_params("arbitrary"),
    )(proj, proj, proj, ln_g, ln_b, w_s, b_s_t, w_out, x1, gate, gf, target)


def _gmlp_bwd(proj, dbr, w_out, ln_g, ln_b, w_s, w_s_t, b_s_t, name):
    _, s, e = proj.shape
    d = dbr.shape[1]
    gw = e // GROUPS
    ts = _gmlp_rows(s)
    n_i = s // ts

    def body(pu_ref, pv_ref, pz_ref, dbr_ref, wo_ref, g_ref, b_ref, ws_ref, wst_ref, bs_ref,
             dp_ref, dws_ref, dbs_ref, dlg_ref, dlb_ref, vn_scr, mix_scr, dm_scr, dvn_scr, dmacc_scr):
        i = pl.program_id(0)

        @pl.when(i == 0)
        def _():
            dws_ref[...] = jnp.zeros_like(dws_ref)
            dlg_ref[...] = jnp.zeros_like(dlg_ref)
            dlb_ref[...] = jnp.zeros_like(dlb_ref)
            dmacc_scr[...] = jnp.zeros_like(dmacc_scr)
        ln_g = g_ref[...]
        u, du_dpu = _gelu_and_grad(_f32(pu_ref))
        v, dv_dpv = _gelu_and_grad(_f32(pv_ref))
        vn, vhat, rstd = _layer_norm_fwd(v, ln_g, b_ref[...])
        vn_scr[...] = vn.astype(BF16)
        _mix_positions(ws_ref, vn_scr, mix_scr, gw, _tril, bs_ref)
        mixed = mix_scr[...]
        dy = lax.dot_general(dbr_ref[...], wo_ref[...], (((1,), (1,)), ((), ())), preferred_element_type=F32)
        sz, dsz = _silu_and_grad(_f32(pz_ref))
        ds = dy * sz
        dp_ref[2] = (dy * (u * mixed) * dsz).astype(BF16)
        dp_ref[0] = (ds * mixed * du_dpu).astype(BF16)
        dm = ds * u
        dm_scr[...] = dm.astype(BF16)
        for n in range(ts // CHUNK):
            rows = slice(n * CHUNK, (n + 1) * CHUNK)
            dmacc_scr[...] += dm[rows]
            for gi in range(GROUPS):
                cols = slice(gi * gw, (gi + 1) * gw)
                dws_ref[gi] += lax.dot_general(dm_scr[rows, cols], vn_scr[rows, cols], (((1,), (1,)), ((), ())),
                                               preferred_element_type=F32)
        _mix_positions(wst_ref, dm_scr, dvn_scr, gw, _triu)
        dvn = dvn_scr[...]
        dlg_ref[...] += _colsum(dvn * vhat)
        dlb_ref[...] += _colsum(dvn)
        dvh = dvn * ln_g
        dv = rstd * (dvh - jnp.mean(dvh, axis=-1, keepdims=True) - vhat * jnp.mean(dvh * vhat, axis=-1, keepdims=True))
        dp_ref[1] = (dv * dv_dpv).astype(BF16)

        @pl.when(i == n_i - 1)
        def _():
            for gi in range(GROUPS):
                dws_ref[gi] = _tril(dws_ref[gi])
                dbs_ref[:, gi:gi + 1] = jnp.sum(dmacc_scr[:, gi * gw:(gi + 1) * gw], axis=1, keepdims=True)

    def part(q):
        return pl.BlockSpec((None, ts, e), lambda i: (q, i, 0))

    w_spec = pl.BlockSpec((GROUPS, CHUNK, CHUNK), lambda i: (0, 0, 0))
    bs_spec = pl.BlockSpec((CHUNK, GROUPS), lambda i: (0, 0))
    return pl.pallas_call(
        body, name=name,
        out_shape=(jax.ShapeDtypeStruct((3, s, e), BF16), jax.ShapeDtypeStruct((GROUPS, CHUNK, CHUNK), F32),
                   jax.ShapeDtypeStruct((CHUNK, GROUPS), F32), jax.ShapeDtypeStruct((1, e), F32),
                   jax.ShapeDtypeStruct((1, e), F32)),
        grid=(n_i,),
        in_specs=[part(0), part(1), part(2), pl.BlockSpec((ts, d), lambda i: (i, 0)),
                  pl.BlockSpec((e, d), lambda i: (0, 0)), _vec_spec(e), _vec_spec(e), w_spec, w_spec, bs_spec],
        out_specs=(pl.BlockSpec((3, ts, e), lambda i: (0, i, 0)), w_spec, bs_spec, _vec_spec(e), _vec_spec(e)),
        scratch_shapes=[pltpu.VMEM((ts, e), BF16), pltpu.VMEM((ts, e), F32), pltpu.VMEM((ts, e), BF16),
                        pltpu.VMEM((ts, e), F32), pltpu.VMEM((CHUNK, e), F32)],
        compiler_params=_params("arbitrary"),
    )(proj, proj, proj, dbr, w_out, ln_g, ln_b, w_s, w_s_t, b_s_t)


def _mod_fwd(c_all, mod_w, mod_b_cols, name):
    n_layer, d, w = mod_w.shape

    def body(c_ref, w_ref, b_ref, ca_ref, o_ref):
        c = c_ref[...]
        ca = c * _sigmoid(c)
        ca_ref[...] = ca
        for li in range(n_layer):
            o_ref[li * N_DEV:(li + 1) * N_DEV, :] = (
                jnp.dot(ca, w_ref[li], preferred_element_type=F32, precision=lax.Precision.HIGHEST) + b_ref[li])

    return pl.pallas_call(
        body, name=name,
        out_shape=(jax.ShapeDtypeStruct((N_DEV, d), F32), jax.ShapeDtypeStruct((n_layer * N_DEV, w), F32)),
        in_specs=[VMEM_SPEC] * 3, out_specs=(VMEM_SPEC, VMEM_SPEC),
        compiler_params=pltpu.CompilerParams(vmem_limit_bytes=VMEM_LIMIT),
    )(c_all, mod_w, mod_b_cols)


def _adamw(w, g, m, v):
    m = ADAM_B1 * m + (1.0 - ADAM_B1) * g
    v = ADAM_B2 * v + (1.0 - ADAM_B2) * (g * g)
    m_hat = m / (1.0 - ADAM_B1 ** ADAM_STEP)
    v_hat = v / (1.0 - ADAM_B2 ** ADAM_STEP)
    delta = -ADAM_LR * (m_hat / (jnp.sqrt(v_hat) + ADAM_EPS) + ADAM_WD * w)
    return delta, m, v


def _adamw_2d(w, g, m, v, name):
    r, c = w.shape
    tr, tc = _tile(r, 512), _tile(c, 1024)

    def body(w_ref, g_ref, m_ref, v_ref, go_ref, d_ref, nm_ref, nv_ref):
        g = g_ref[...]
        go_ref[...] = g
        d_ref[...], nm_ref[...], nv_ref[...] = _adamw(w_ref[...], g, m_ref[...], v_ref[...])

    spec = pl.BlockSpec((tr, tc), lambda i, j: (i, j))
    shape = jax.ShapeDtypeStruct((r, c), F32)
    return pl.pallas_call(
        body, name=name, out_shape=(shape,) * 4, grid=(r // tr, c // tc),
        in_specs=[spec] * 4, out_specs=(spec,) * 4, compiler_params=_params("parallel", "parallel"),
    )(w, g, m, v)


def _mod_w_update(ca_t, dmod_cols, w, m, v, name):
    n_layer, d, wd = w.shape
    tr = _tile(d, 256)

    def body(ca_ref, dm_ref, w_ref, m_ref, v_ref, g_ref, d_ref, nm_ref, nv_ref):
        ca = ca_ref[...]
        dm = dm_ref[...]
        g = ca[:, 0:1] * dm[0:1, :]
        for b in range(1, N_DEV):
            g = g + ca[:, b:b + 1] * dm[b:b + 1, :]
        g_ref[...] = g
        d_ref[...], nm_ref[...], nv_ref[...] = _adamw(w_ref[...], g, m_ref[...], v_ref[...])

    spec = pl.BlockSpec((None, tr, wd), lambda l, i: (l, i, 0))
    shape = jax.ShapeDtypeStruct((n_layer, d, wd), F32)
    return pl.pallas_call(
        body, name=name, out_shape=(shape,) * 4, grid=(n_layer, d // tr),
        in_specs=[pl.BlockSpec((tr, N_DEV), lambda l, i: (i, 0)), pl.BlockSpec((None, N_DEV, wd), lambda l, i: (l, 0, 0)),
                  spec, spec, spec],
        out_specs=(spec,) * 4, compiler_params=_params("parallel", "parallel"),
    )(ca_t, dmod_cols, w, m, v)


def _adamw_small(ws, gs, ms, vs, name):
    n = len(ws)

    def body(*refs):
        ins, outs = refs[:4 * n], refs[4 * n:]
        for k in range(n):
            delta, nm, nv = _adamw(ins[k][...], ins[n + k][...], ins[2 * n + k][...], ins[3 * n + k][...])
            outs[3 * k][...] = delta
            outs[3 * k + 1][...] = nm
            outs[3 * k + 2][...] = nv

    out_shape = []
    for w in ws:
        out_shape += [jax.ShapeDtypeStruct(w.shape, F32)] * 3
    outs = pl.pallas_call(
        body, name=name, out_shape=tuple(out_shape),
        in_specs=[VMEM_SPEC] * (4 * n), out_specs=tuple([VMEM_SPEC] * (3 * n)),
        compiler_params=pltpu.CompilerParams(vmem_limit_bytes=VMEM_LIMIT),
    )(*ws, *gs, *ms, *vs)
    return [tuple(outs[3 * k:3 * k + 3]) for k in range(n)]


def _place():
    return lax.axis_index("x"), lax.axis_index("y"), lax.axis_index("c")


def _other_chips(x, y):
    return [(1 - x, y), (x, 1 - y), (1 - x, 1 - y)]


def _all_gather(block, name, after=None):
    m_per, n = block.shape

    def body(x_ref, *rest):
        out_ref, send_sems, recv_sems, local_sem = rest[-4:]
        x, y, c = _place()
        me, sibling = (x, y, c), (x, y, 1 - c)
        chips = _other_chips(x, y)

        def rows(px, py, pc):
            return out_ref.at[pl.ds((4 * px + 2 * py + pc) * m_per, m_per), :]

        def copy(k, blk, to, src=None):
            return pltpu.make_async_remote_copy(
                src_ref=rows(*blk) if src is None else src, dst_ref=rows(*blk),
                send_sem=send_sems.at[k], recv_sem=recv_sems.at[k], device_id=to, device_id_type=MESH)

        mine = pltpu.make_async_copy(x_ref, rows(*me), local_sem)
        mine.start()
        first = [copy(0, me, sibling, src=x_ref)]
        first += [copy(1 + j, me, (*chip, c), src=x_ref) for j, chip in enumerate(chips)]
        for cp in first:
            cp.start()
        passed = [copy(4 + j, (*chip, c), sibling) for j, chip in enumerate(chips)]
        for j, chip in enumerate(chips):
            copy(1 + j, (*chip, c), me).wait_recv()
            passed[j].start()
        copy(0, sibling, me).wait_recv()
        for j, chip in enumerate(chips):
            copy(4 + j, (*chip, 1 - c), me).wait_recv()
        for cp in first + passed:
            cp.wait_send()
        mine.wait()

    return pl.pallas_call(
        body, name=name, out_shape=jax.ShapeDtypeStruct((N_DEV * m_per, n), F32),
        in_specs=[VMEM_SPEC] + ([] if after is None else [ANY_SPEC]), out_specs=VMEM_SPEC,
        scratch_shapes=[pltpu.SemaphoreType.DMA((7,)), pltpu.SemaphoreType.DMA((7,)), pltpu.SemaphoreType.DMA],
        compiler_params=pltpu.CompilerParams(vmem_limit_bytes=VMEM_LIMIT),
    )(*([block] if after is None else [block, after]))


def _hbm(a):
    return pltpu.with_memory_space_constraint(a, pltpu.HBM)


def _all_gather_start(blocks, name):
    def body(in_ref, send_sems, recv_sems, thru_ref, token):
        token[...] = jnp.zeros_like(token)
        x, y, c = _place()
        mine = in_ref.at[4 * x + 2 * y + c]
        pltpu.make_async_remote_copy(
            src_ref=mine, dst_ref=mine, send_sem=send_sems.at[0], recv_sem=recv_sems.at[0],
            device_id=(x, y, 1 - c), device_id_type=MESH).start()
        for j, chip in enumerate(_other_chips(x, y)):
            for cc in range(2):
                pltpu.make_async_remote_copy(
                    src_ref=mine, dst_ref=mine, send_sem=send_sems.at[1 + 2 * j + cc],
                    recv_sem=recv_sems.at[1 + 2 * j + c], device_id=(*chip, cc), device_id_type=MESH).start()

    sems = pltpu.SemaphoreType.DMA((N_DEV - 1,))
    return pl.pallas_call(
        body, name=name,
        out_shape=(sems, sems, pltpu.HBM(blocks.shape, blocks.dtype), jax.ShapeDtypeStruct((SUBLANE, 128), F32)),
        in_specs=[HBM_SPEC], out_specs=(SEM_SPEC, SEM_SPEC, HBM_SPEC, VMEM_SPEC), input_output_aliases={0: 2},
        compiler_params=pltpu.CompilerParams(has_side_effects=EFFECT),
    )(_hbm(blocks))


def _all_gather_wait(send_sems, recv_sems, blocks, after, name):
    def body(in_ref, send_sems, recv_sems, after_ref, out_ref):
        x, y, c = _place()
        one = in_ref.at[0]
        for k in range(N_DEV - 1):
            pltpu.make_async_remote_copy(
                src_ref=one, dst_ref=one, send_sem=send_sems.at[k], recv_sem=recv_sems.at[k],
                device_id=(x, y, 1 - c), device_id_type=MESH).wait_send()
            pltpu.make_async_remote_copy(
                src_ref=one, dst_ref=one, send_sem=send_sems.at[k], recv_sem=recv_sems.at[k],
                device_id=(x, y, 1 - c), device_id_type=MESH).wait_recv()

    return pl.pallas_call(
        body, name=name, out_shape=pltpu.HBM(blocks.shape, blocks.dtype),
        in_specs=[HBM_SPEC, SEM_SPEC, SEM_SPEC, ANY_SPEC], out_specs=HBM_SPEC, input_output_aliases={0: 0},
        compiler_params=pltpu.CompilerParams(has_side_effects=EFFECT),
    )(blocks, send_sems, recv_sems, after)


def _place_own(shard, kind, chip_idx, name):
    r, cdim = shard.shape
    tr, tc = _tile(r, 512), _tile(cdim, 1024)
    nrb, ncb = r // tr, cdim // tc

    def body(k_ref, s_ref, o_ref):
        o_ref[...] = s_ref[...].astype(BF16)

    if kind == "col":
        full, o_map = (r, N_CHIP * cdim), lambda i, j, kr: (i, kr[0] * ncb + j)
    else:
        full, o_map = (N_CHIP * r, cdim), lambda i, j, kr: (kr[0] * nrb + i, j)
    return pl.pallas_call(
        body, name=name, out_shape=jax.ShapeDtypeStruct(full, BF16),
        grid_spec=pltpu.PrefetchScalarGridSpec(
            num_scalar_prefetch=1, grid=(nrb, ncb),
            in_specs=[pl.BlockSpec((tr, tc), lambda i, j, kr: (i, j))],
            out_specs=pl.BlockSpec((tr, tc), o_map)),
        compiler_params=_params("parallel", "parallel"),
    )(chip_idx, shard)


def _weight_window(ref, kind, shard_shape, k, half):
    r, cdim = shard_shape
    hr = r // 2
    if kind == "col":
        return ref.at[pl.ds(half * hr, hr), pl.ds(pl.multiple_of(k * cdim, 128), cdim)]
    return ref.at[pl.ds(pl.multiple_of(k * r + half * hr, 2 * SUBLANE), hr), :]


def _target_cores(c, both_cores):
    return [(cc, cc, c) for cc in range(2)] if both_cores else [(c, 0, 0)]


def _gather_weights_start(fulls, kinds, shard_shapes, groups, after, name, both_cores=True):
    n, ng = len(fulls), len(groups)
    per = 2 if both_cores else 1

    def body(*refs):
        ins = refs[:n]
        sems = refs[n + 1:n + 1 + 2 * ng]
        token = refs[2 * n + 1 + 2 * ng]
        x, y, c = _place()
        chips = _other_chips(x, y)
        for g, group in enumerate(groups):
            for pos, (w, j) in enumerate(group):
                own = _weight_window(ins[w], kinds[w], shard_shapes[w], 2 * x + y, c)
                for cc, mine, theirs in _target_cores(c, both_cores):
                    pltpu.make_async_remote_copy(
                        src_ref=own, dst_ref=own,
                        send_sem=sems[2 * g].at[per * pos + mine], recv_sem=sems[2 * g + 1].at[per * pos + theirs],
                        device_id=(*chips[j], cc), device_id_type=MESH).start()
        token[...] = jnp.zeros_like(token)

    sem_shapes = []
    for group in groups:
        sem_shapes += [pltpu.SemaphoreType.DMA((per * len(group),))] * 2
    outs = pl.pallas_call(
        body, name=name,
        out_shape=tuple(sem_shapes) + tuple(pltpu.HBM(f.shape, f.dtype) for f in fulls)
        + (jax.ShapeDtypeStruct((SUBLANE, 128), F32),),
        in_specs=[HBM_SPEC] * n + [ANY_SPEC], out_specs=(SEM_SPEC,) * (2 * ng) + (HBM_SPEC,) * n + (VMEM_SPEC,),
        input_output_aliases={w: 2 * ng + w for w in range(n)},
        compiler_params=pltpu.CompilerParams(has_side_effects=EFFECT),
    )(*[_hbm(f) for f in fulls], after)
    sems = [(outs[2 * g], outs[2 * g + 1]) for g in range(ng)]
    return sems, list(outs[2 * ng:2 * ng + n]), outs[2 * ng + n]


def _gather_weights_wait(sems, group, fulls, kinds, shard_shapes, after, name, both_cores=True):
    n = len(fulls)
    per = 2 if both_cores else 1

    def body(*refs):
        ins = refs[:n]
        send_sems, recv_sems = refs[n], refs[n + 1]
        x, y, c = _place()
        chips = _other_chips(x, y)
        for pos, (w, j) in enumerate(group):
            own = _weight_window(ins[w], kinds[w], shard_shapes[w], 2 * x + y, c)
            for cc, mine, _ in _target_cores(c, both_cores):
                landed = _weight_window(ins[w], kinds[w], shard_shapes[w], 2 * chips[j][0] + chips[j][1], cc)
                k = per * pos + mine
                pltpu.make_async_remote_copy(
                    src_ref=own, dst_ref=own, send_sem=send_sems.at[k], recv_sem=recv_sems.at[k],
                    device_id=(*chips[j], cc), device_id_type=MESH).wait_send()
                pltpu.make_async_remote_copy(
                    src_ref=landed, dst_ref=landed, send_sem=send_sems.at[k], recv_sem=recv_sems.at[k],
                    device_id=(*chips[j], cc), device_id_type=MESH).wait_recv()

    return list(pl.pallas_call(
        body, name=name, out_shape=tuple(pltpu.HBM(f.shape, f.dtype) for f in fulls),
        in_specs=[HBM_SPEC] * n + [SEM_SPEC, SEM_SPEC, ANY_SPEC], out_specs=(HBM_SPEC,) * n,
        input_output_aliases={w: w for w in range(n)},
        compiler_params=pltpu.CompilerParams(has_side_effects=EFFECT),
    )(*fulls, sems[0], sems[1], after))


def _forward_to_sibling(full, kind, shard_shape, relations, name):
    nr = len(relations)

    def body(in_ref, out_ref, send_sems, recv_sems):
        x, y, c = _place()
        chips = _other_chips(x, y)
        copies = []
        for pos, j in enumerate(relations):
            k = 2 * chips[j][0] + chips[j][1]
            mine = _weight_window(in_ref, kind, shard_shape, k, c)
            cp = pltpu.make_async_remote_copy(
                src_ref=mine, dst_ref=_weight_window(out_ref, kind, shard_shape, k, c),
                send_sem=send_sems.at[pos], recv_sem=recv_sems.at[pos], device_id=(x, y, 1 - c), device_id_type=MESH)
            cp.start()
            copies.append(cp)
        for pos, j in enumerate(relations):
            theirs = _weight_window(out_ref, kind, shard_shape, 2 * chips[j][0] + chips[j][1], 1 - c)
            pltpu.make_async_remote_copy(
                src_ref=theirs, dst_ref=theirs, send_sem=send_sems.at[pos], recv_sem=recv_sems.at[pos],
                device_id=(x, y, 1 - c), device_id_type=MESH).wait_recv()
        for cp in copies:
            cp.wait_send()

    return pl.pallas_call(
        body, name=name, out_shape=jax.ShapeDtypeStruct(full.shape, full.dtype),
        in_specs=[HBM_SPEC], out_specs=HBM_SPEC, input_output_aliases={0: 0},
        scratch_shapes=[pltpu.SemaphoreType.DMA((nr,))] * 2,
    )(full)


def _grad_piece(ref, kind, h, cdim, k, half):
    if kind == "col":
        return ref.at[pl.ds(half * h, h), pl.ds(pl.multiple_of(k * cdim, 128), cdim)]
    return ref.at[k, pl.ds(half * h, h), :]


def _grad_dims(g, kind):
    return (g.shape[0] // 2, g.shape[1] // N_CHIP) if kind == "col" else (g.shape[1] // 2, g.shape[2])


def _reduce_start(grads, kinds, name):
    n = len(grads)
    dims = [_grad_dims(g, kind) for g, kind in zip(grads, kinds)]
    lands = [lax.empty((N_DEV - 1, h, cdim), g.dtype) for g, (h, cdim) in zip(grads, dims)]

    def body(*refs):
        g_ins, land_ins = refs[:n], refs[n:2 * n]
        send_sems, recv_sems = refs[2 * n], refs[2 * n + 1]
        token = refs[4 * n + 2]
        x, y, c = _place()
        for w in range(n):
            h, cdim = dims[w]
            base = (N_DEV - 1) * w
            pltpu.make_async_remote_copy(
                src_ref=_grad_piece(g_ins[w], kinds[w], h, cdim, 2 * x + y, 1 - c), dst_ref=land_ins[w].at[0],
                send_sem=send_sems.at[base], recv_sem=recv_sems.at[base],
                device_id=(x, y, 1 - c), device_id_type=MESH).start()
            for j, chip in enumerate(_other_chips(x, y)):
                for cc in range(2):
                    pltpu.make_async_remote_copy(
                        src_ref=_grad_piece(g_ins[w], kinds[w], h, cdim, 2 * chip[0] + chip[1], cc),
                        dst_ref=land_ins[w].at[1 + 2 * j + c],
                        send_sem=send_sems.at[base + 1 + 2 * j + cc], recv_sem=recv_sems.at[base + 1 + 2 * j + c],
                        device_id=(*chip, cc), device_id_type=MESH).start()
        token[...] = jnp.zeros_like(token)

    sems = pltpu.SemaphoreType.DMA(((N_DEV - 1) * n,))
    outs = pl.pallas_call(
        body, name=name,
        out_shape=(sems, sems) + tuple(pltpu.HBM(a.shape, a.dtype) for a in list(grads) + lands)
        + (jax.ShapeDtypeStruct((SUBLANE, 128), F32),),
        in_specs=[HBM_SPEC] * (2 * n), out_specs=(SEM_SPEC, SEM_SPEC) + (HBM_SPEC,) * (2 * n) + (VMEM_SPEC,),
        input_output_aliases={i: 2 + i for i in range(2 * n)},
        compiler_params=pltpu.CompilerParams(has_side_effects=EFFECT),
    )(*[_hbm(a) for a in list(grads) + lands])
    return outs[0], outs[1], list(outs[2:2 + n]), list(outs[2 + n:2 + 2 * n]), outs[2 + 2 * n]


def _reduce_wait(send_sems, recv_sems, grads, lands, kinds, after, name):
    n = len(grads)
    dims = [_grad_dims(g, kind) for g, kind in zip(grads, kinds)]

    def body(*refs):
        g_ins, land_ins = refs[:n], refs[n:2 * n]
        send_sems, recv_sems = refs[2 * n], refs[2 * n + 1]
        x, y, c = _place()
        for w in range(n):
            h, cdim = dims[w]
            piece = _grad_piece(g_ins[w], kinds[w], h, cdim, 2 * x + y, c)
            for s in range(N_DEV - 1):
                k = (N_DEV - 1) * w + s
                slot = land_ins[w].at[s]
                pltpu.make_async_remote_copy(
                    src_ref=piece, dst_ref=slot, send_sem=send_sems.at[k], recv_sem=recv_sems.at[k],
                    device_id=(x, y, 1 - c), device_id_type=MESH).wait_send()
                pltpu.make_async_remote_copy(
                    src_ref=piece, dst_ref=slot, send_sem=send_sems.at[k], recv_sem=recv_sems.at[k],
                    device_id=(x, y, 1 - c), device_id_type=MESH).wait_recv()

    outs = pl.pallas_call(
        body, name=name, out_shape=tuple(pltpu.HBM(a.shape, a.dtype) for a in list(grads) + list(lands)),
        in_specs=[HBM_SPEC] * (2 * n) + [SEM_SPEC, SEM_SPEC, ANY_SPEC], out_specs=(HBM_SPEC,) * (2 * n),
        input_output_aliases={i: i for i in range(2 * n)},
        compiler_params=pltpu.CompilerParams(has_side_effects=EFFECT),
    )(*grads, *lands, send_sems, recv_sems, after)
    return list(outs[:n]), list(outs[n:])


def _add_pieces(g, land, kind, chip_idx, core_idx, name):
    _, h, cdim = land.shape
    tr, tc = _tile(h, 256), _tile(cdim, 2048)
    nrb, ncb = h // tr, cdim // tc

    def body(k_ref, c_ref, g_ref, l_ref, o_ref):
        acc = g_ref[...].astype(F32)
        for s in range(N_DEV - 1):
            acc = acc + l_ref[s].astype(F32)
        o_ref[...] = acc

    if kind == "col":
        g_spec = pl.BlockSpec((tr, tc), lambda i, j, kr, cr: (cr[0] * nrb + i, kr[0] * ncb + j))
    else:
        g_spec = pl.BlockSpec((None, tr, tc), lambda i, j, kr, cr: (kr[0], cr[0] * nrb + i, j))
    return pl.pallas_call(
        body, name=name, out_shape=jax.ShapeDtypeStruct((2 * h, cdim), F32),
        grid_spec=pltpu.PrefetchScalarGridSpec(
            num_scalar_prefetch=2, grid=(nrb, ncb),
            in_specs=[g_spec, pl.BlockSpec((N_DEV - 1, tr, tc), lambda i, j, kr, cr: (0, i, j))],
            out_specs=pl.BlockSpec((tr, tc), lambda i, j, kr, cr: (cr[0] * nrb + i, j))),
        compiler_params=_params("parallel", "parallel"),
    )(chip_idx, core_idx, g, land)


def _join_halves(shards, name):
    n = len(shards)

    def body(*refs):
        ins, outs = refs[:n], refs[n:2 * n]
        send_sems, recv_sems = refs[2 * n:]
        x, y, c = _place()
        copies = []
        for w in range(n):
            h = shards[w].shape[0] // 2
            cp = pltpu.make_async_remote_copy(
                src_ref=ins[w].at[pl.ds(c * h, h), :], dst_ref=outs[w].at[pl.ds(c * h, h), :],
                send_sem=send_sems.at[w], recv_sem=recv_sems.at[w], device_id=(x, y, 1 - c), device_id_type=MESH)
            cp.start()
            copies.append(cp)
        for w in range(n):
            h = shards[w].shape[0] // 2
            theirs = outs[w].at[pl.ds((1 - c) * h, h), :]
            pltpu.make_async_remote_copy(
                src_ref=theirs, dst_ref=theirs, send_sem=send_sems.at[w], recv_sem=recv_sems.at[w],
                device_id=(x, y, 1 - c), device_id_type=MESH).wait_recv()
        for cp in copies:
            cp.wait_send()

    return pl.pallas_call(
        body, name=name, out_shape=tuple(jax.ShapeDtypeStruct(s.shape, s.dtype) for s in shards),
        in_specs=[HBM_SPEC] * n, out_specs=tuple([HBM_SPEC] * n),
        input_output_aliases={w: w for w in range(n)},
        scratch_shapes=[pltpu.SemaphoreType.DMA((n,))] * 2,
    )(*shards)


def _sum_devices(gathered, rows, name):
    n = gathered.shape[1]

    def body(g_ref, o_ref):
        acc = g_ref[0:rows, :]
        for dev in range(1, N_DEV):
            acc = acc + g_ref[dev * rows:(dev + 1) * rows, :]
        o_ref[...] = acc

    return pl.pallas_call(
        body, name=name, out_shape=jax.ShapeDtypeStruct((rows, n), F32),
        in_specs=[VMEM_SPEC], out_specs=VMEM_SPEC,
        compiler_params=pltpu.CompilerParams(vmem_limit_bytes=VMEM_LIMIT),
    )(gathered)


def _pack(vectors, width):
    flat = [v.reshape(-1) for v in vectors]
    offsets, total = [], 0
    for f in flat:
        offsets.append(total)
        total += f.shape[0]
    rows = -(-total // (width * SUBLANE)) * SUBLANE
    flat.append(jnp.zeros((rows * width - total,), F32))
    return jnp.concatenate(flat).reshape(rows, width), offsets


def kernel(x, c, mod_w, mod_b, norm_g, a_w_in, a_conv_w, a_conv_b, a_w_out, b_w_in, b_ln_g, b_ln_b, b_w_s, b_b_s, b_w_out, final_g, loss_target, m_mod_w, m_mod_b, m_norm_g, m_a_w_in, m_a_conv_w, m_a_conv_b, m_a_w_out, m_b_w_in, m_b_ln_g, m_b_ln_b, m_b_w_s, m_b_b_s, m_b_w_out, m_final_g, v_mod_w, v_mod_b, v_norm_g, v_a_w_in, v_a_conv_w, v_a_conv_b, v_a_w_out, v_b_w_in, v_b_ln_g, v_b_ln_b, v_b_w_s, v_b_b_s, v_b_w_out, v_final_g):
    seq, d = x.shape[1], x.shape[2]
    e = a_conv_b.shape[1]
    wd = mod_w.shape[2]
    ax, ay, ac = _place()
    chip = 2 * ax + ay
    dev = 2 * chip + ac
    chip_idx = jnp.reshape(chip, (1,)).astype(jnp.int32)
    core_idx = jnp.reshape(ac, (1,)).astype(jnp.int32)

    x2d = x[0]
    target = loss_target[0]

    w_names = ["a_w_in", "a_w_out", "b_w_in", "b_w_out"]
    w_kinds = ["col", "row", "col", "row"]
    w_shards = [a_w_in[0], a_w_out[0], b_w_in[0], b_w_out[0]]
    w_shapes = [sh.shape for sh in w_shards]
    placed = [_place_own(sh, kind, chip_idx, "place_" + nm) for nm, sh, kind in zip(w_names, w_shards, w_kinds)]
    a_groups, far_groups = [[(0, 0), (0, 1)]], [[(0, 2)]]
    whole_group = [[(0, 0), (0, 1), (0, 2)]]

    es = e // N_CHIP
    packed0, offs0 = _pack([c, a_conv_w, b_ln_g, b_ln_b], 1024)
    gathered0 = _all_gather(packed0, "gather_params").reshape(N_DEV, -1)
    c_all = gathered0[:, :d]
    per_chip = gathered0[0::2]

    def from_chips_cols(k, rows_):
        got = per_chip[:, offs0[k]:offs0[k] + rows_ * es].reshape(N_CHIP, rows_, es)
        return jnp.transpose(got, (1, 0, 2)).reshape(rows_, e)

    conv_w_full = from_chips_cols(1, 3)
    conv_w = [conv_w_full[k:k + 1] for k in range(3)]
    ln_g, ln_b = from_chips_cols(2, 1), from_chips_cols(3, 1)
    mod_b_cols = lax.dynamic_slice_in_dim(mod_b, chip * wd, wd, axis=1)[:, None, :]
    c_act, mod_part = _mod_fwd(c_all, mod_w, mod_b_cols, "mod_fwd")
    n_layer = mod_w.shape[0]
    mod_gathered = _all_gather(mod_part, "gather_mod")
    a_sems, (wa_in,), a_token = _gather_weights_start(
        placed[:1], w_kinds[:1], w_shapes[:1], a_groups, mod_gathered, "gather_a_w_in_start", both_cores=False)
    mod_all = mod_gathered.reshape(N_CHIP, 2, n_layer, N_DEV, wd)[:, 0]
    mod_all = jnp.transpose(mod_all, (1, 2, 0, 3)).reshape(n_layer, N_DEV, N_CHIP * wd)
    mod_me = lax.dynamic_index_in_dim(mod_all, dev, axis=1, keepdims=False)
    shift = [mod_me[l:l + 1, 0:d] for l in range(n_layer)]
    scale = [mod_me[l:l + 1, d:2 * d] for l in range(n_layer)]
    gate = [mod_me[l:l + 1, 2 * d:3 * d] for l in range(n_layer)]

    g0, g1, gf = norm_g[0:1], norm_g[1:2], final_g[None, :]

    def slab(r):
        return jnp.bitwise_xor(chip_idx, r)

    def arrived(sems, group, weight, w, after, name, both_cores=True):
        return _gather_weights_wait(sems, [(0, j) for _, j in group], [weight], [w_kinds[w]], [w_shapes[w]],
                                    after, name, both_cores)[0]

    h0, proj0 = _norm_mod_proj_slab(x2d, g0, scale[0], shift[0], wa_in, slab(0), 4, "norm_mod0_a_proj_own", a_token)
    wa_in = arrived(a_sems[0], a_groups[0], wa_in, 0, proj0, "gather_wait_near", both_cores=False)
    wa_in = _forward_to_sibling(wa_in, w_kinds[0], w_shapes[0], [0, 1], "forward_near")
    far_sems, (wa_in,), far_token = _gather_weights_start(
        [wa_in], w_kinds[:1], w_shapes[:1], far_groups, a_token, "gather_a_w_in_far_start")
    proj0 = _mm_proj_slab(h0, wa_in, proj0, slab(2), 4, "a_proj_x", far_token)
    proj0 = _mm_proj_slab(h0, wa_in, proj0, slab(1), 4, "a_proj_y", far_token)
    wa_in = arrived(far_sems[0], far_groups[0], wa_in, 0, proj0, "gather_wait_far")
    def start_whole(w, after, name):
        sems, (flight,), token = _gather_weights_start(
            placed[w:w + 1], w_kinds[w:w + 1], w_shapes[w:w + 1], whole_group, after, name)
        return sems[0], flight, token

    ao_sems, wa_out, ao_token = start_whole(1, wa_in, "gather_a_w_out_start")
    proj0 = _mm_proj_slab(h0, wa_in, proj0, slab(3), 4, "a_proj_far", ao_token)
    wa_out = arrived(ao_sems, whole_group[0], wa_out, 1, proj0, "gather_wait_a_w_out")
    bi_sems, wb_in, bi_token = start_whole(2, wa_out, "gather_b_w_in_start")
    y0, br0, x1, h1 = _conv_fwd(proj0, conv_w, a_conv_b, wa_out, x2d, gate[0], g1, scale[1], shift[1],
                                "conv_fwd", bi_token)
    wb_in = arrived(bi_sems, whole_group[0], wb_in, 2, h1, "gather_wait_b_w_in")
    bo_sems, wb_out, bo_token = start_whole(3, wb_in, "gather_b_w_out_start")
    proj1 = _mm_proj(h1, wb_in, 3, "b_proj", bo_token)
    b_s_t = jnp.transpose(b_b_s[0])
    wb_out = arrived(bo_sems, whole_group[0], wb_out, 3, proj1, "gather_wait_b_w_out")
    y1, dx2, dbr1, loss_part, g_final_g, dgate1 = _gmlp_fwd_head(
        proj1, ln_g, ln_b, b_w_s[0], b_s_t, wb_out, x1, gate[1], gf, target, "gmlp_fwd_head")

    gw_b_out = _mm_dw_out(y1, dbr1, "b_out_dw")
    dproj1, g_w_s, g_b_s_t, g_ln_g, g_ln_b = _gmlp_bwd(
        proj1, dbr1, wb_out, ln_g, ln_b, b_w_s[0], jnp.swapaxes(b_w_s[0], 1, 2), b_s_t, "gmlp_bwd")
    gw_b_in = _mm_dw_in(h1, dproj1, "b_proj_dw")
    b_kinds = ["col", "row"]
    b_send, b_recv, b_grads, b_lands, b_token = _reduce_start(
        [gw_b_in, gw_b_out.reshape(N_CHIP, e // N_CHIP, d)], b_kinds, "reduce_b_start")
    dx1, dshift1, dscale1, g_g1, dbr0, dgate0 = _mm_dh_norm_bwd(
        dproj1, wb_in, x1, dx2, g1, scale[1], "b_proj_dx", b_token, br=br0, gate=gate[0])

    early = [loss_part[0, 0:1], g_final_g, g_g1, g_ln_g, g_ln_b, jnp.transpose(g_b_s_t), g_w_s,
             jnp.concatenate([dshift1, dscale1, dgate1, dgate0], axis=1)]
    packed_e, offs_e = _pack(early, 1024)
    rows_e = packed_e.shape[0]
    blocks_e = lax.dynamic_update_slice(jnp.zeros((N_DEV, rows_e, 1024), F32), packed_e[None], (dev, 0, 0))
    e_send, e_recv, blocks_e, e_token = _all_gather_start(blocks_e, "gather_small_start")

    gw_a_out = _mm_dw_out(y0, dbr0, "a_out_dw")
    ao_send, ao_recv, ao_grads, ao_lands, ao_grad_token = _reduce_start(
        [gw_a_out.reshape(N_CHIP, e // N_CHIP, d)], ["row"], "reduce_a_out_start")
    dproj0, g_w0, g_w1, g_w2, g_conv_b = _conv_bwd(
        proj0, dbr0, wa_out, conv_w, a_conv_b, "conv_bwd", ao_grad_token + e_token)
    gw_a_in = _mm_dw_in(h0, dproj0, "a_proj_dw")
    ai_send, ai_recv, ai_grads, ai_lands, ai_token = _reduce_start([gw_a_in], ["col"], "reduce_a_in_start")
    grad_x, dshift0, dscale0, g_g0 = _mm_dh_norm_bwd(dproj0, wa_in, x2d, dx1, g0, scale[0], "a_proj_dx", ai_token)

    def finish(send, recv, grads_, lands_, kinds_, names_, after, tag):
        grads_, lands_ = _reduce_wait(send, recv, grads_, lands_, kinds_, after, "reduce_" + tag + "_wait")
        halves = [_add_pieces(g, land, kind, chip_idx, core_idx, "add_pieces_" + nm)
                  for g, land, kind, nm in zip(grads_, lands_, kinds_, names_)]
        return _join_halves(halves, "join_" + tag)

    upd, big_grads = {}, {}

    def adamw_big(nm, w, g, m, v):
        g_out, *rest = _adamw_2d(w[0], g, m[0], v[0], "adamw_" + nm)
        big_grads[nm] = g_out[None]
        upd[nm] = tuple(o[None] for o in rest)

    g_b_w_in, g_b_w_out = finish(b_send, b_recv, b_grads, b_lands, b_kinds, ["b_w_in", "b_w_out"], grad_x, "b")
    adamw_big("b_w_in", b_w_in, g_b_w_in, m_b_w_in, v_b_w_in)
    adamw_big("b_w_out", b_w_out, g_b_w_out, m_b_w_out, v_b_w_out)

    late = [g_g0, jnp.concatenate([g_w0, g_w1, g_w2], axis=0), g_conv_b, jnp.concatenate([dshift0, dscale0], axis=1)]
    packed_l, offs_l = _pack(late, 1024)
    rows_l = packed_l.shape[0]
    gathered_l = _all_gather(packed_l, "gather_small_late", after=upd["b_w_out"][0])
    blocks_e = _all_gather_wait(e_send, e_recv, blocks_e, gathered_l, "gather_small_wait")
    gathered_e = blocks_e.reshape(N_DEV * rows_e, 1024)
    summed_e = _sum_devices(gathered_e, rows_e, "sum_small_early").reshape(-1)
    summed_l = _sum_devices(gathered_l, rows_l, "sum_small_late").reshape(-1)

    def take(summed, offs, k, shape):
        size = math.prod(shape)
        return summed[offs[k]:offs[k] + size].reshape(shape)

    def rows_of(gathered, rows_, offs, k, width):
        return gathered.reshape(N_DEV, rows_ * 1024)[:, offs[k]:offs[k] + width]

    loss = take(summed_e, offs_e, 0, ())
    grad_final_g = take(summed_e, offs_e, 1, (d,))
    grad_norm_g = jnp.concatenate([take(summed_l, offs_l, 0, (1, d)), take(summed_e, offs_e, 2, (1, d))], axis=0)
    grad_ln_g_full = take(summed_e, offs_e, 3, (1, e))
    grad_ln_b_full = take(summed_e, offs_e, 4, (1, e))
    grad_b_b_s = take(summed_e, offs_e, 5, (1, GROUPS, CHUNK))
    grad_b_w_s = take(summed_e, offs_e, 6, (1, GROUPS, CHUNK, CHUNK))
    grad_conv_w_full = take(summed_l, offs_l, 1, (3, e))
    grad_a_conv_b = take(summed_l, offs_l, 2, (1, e))
    grad_a_conv_w = lax.dynamic_slice_in_dim(grad_conv_w_full, chip * es, es, axis=1)[None]
    grad_b_ln_g = lax.dynamic_slice_in_dim(grad_ln_g_full, chip * es, es, axis=1)
    grad_b_ln_b = lax.dynamic_slice_in_dim(grad_ln_b_full, chip * es, es, axis=1)
    dmod_e = rows_of(gathered_e, rows_e, offs_e, 7, 4 * d)
    dmod_l = rows_of(gathered_l, rows_l, offs_l, 3, 2 * d)
    dmod_all = jnp.stack([jnp.concatenate([dmod_l, dmod_e[:, 3 * d:]], axis=1), dmod_e[:, :3 * d]], axis=1)
    mod_b_e = take(summed_e, offs_e, 7, (4 * d,))
    mod_b_l = take(summed_l, offs_l, 3, (2 * d,))
    grad_mod_b = jnp.stack([jnp.concatenate([mod_b_l, mod_b_e[3 * d:]]), mod_b_e[:3 * d]])
    dmod_cols = jnp.transpose(lax.dynamic_slice_in_dim(dmod_all, chip * wd, wd, axis=2), (1, 0, 2))

    grad_mod_w, delta_mod_w, new_m_mod_w, new_v_mod_w = _mod_w_update(
        jnp.transpose(c_act), dmod_cols, mod_w, m_mod_w, v_mod_w, "mod_w_update")
    (g_a_w_out,) = finish(ao_send, ao_recv, ao_grads, ao_lands, ["row"], ["a_w_out"], delta_mod_w, "a_out")
    adamw_big("a_w_out", a_w_out, g_a_w_out, m_a_w_out, v_a_w_out)
    (g_a_w_in,) = finish(ai_send, ai_recv, ai_grads, ai_lands, ["col"], ["a_w_in"], upd["a_w_out"][0], "a_in")
    adamw_big("a_w_in", a_w_in, g_a_w_in, m_a_w_in, v_a_w_in)
    small_w = [("mod_b", mod_b, grad_mod_b, m_mod_b, v_mod_b), ("norm_g", norm_g, grad_norm_g, m_norm_g, v_norm_g),
               ("a_conv_w", a_conv_w, grad_a_conv_w, m_a_conv_w, v_a_conv_w),
               ("a_conv_b", a_conv_b, grad_a_conv_b, m_a_conv_b, v_a_conv_b),
               ("b_ln_g", b_ln_g, grad_b_ln_g, m_b_ln_g, v_b_ln_g), ("b_ln_b", b_ln_b, grad_b_ln_b, m_b_ln_b, v_b_ln_b),
               ("b_w_s", b_w_s, grad_b_w_s, m_b_w_s, v_b_w_s), ("b_b_s", b_b_s, grad_b_b_s, m_b_b_s, v_b_b_s),
               ("final_g", final_g, grad_final_g, m_final_g, v_final_g)]

    def flat2d(a):
        return a.reshape(-1, a.shape[-1])

    res = _adamw_small([flat2d(t[1]) for t in small_w], [flat2d(t[2]) for t in small_w],
                       [flat2d(t[3]) for t in small_w], [flat2d(t[4]) for t in small_w], "adamw_small")
    for (nm, w, _, _, _), r3 in zip(small_w, res):
        upd[nm] = tuple(o.reshape(w.shape) for o in r3)
    upd["mod_w"] = (delta_mod_w, new_m_mod_w, new_v_mod_w)

    grads = {"mod_w": grad_mod_w, "mod_b": grad_mod_b, "norm_g": grad_norm_g, "a_conv_w": grad_a_conv_w,
             "a_conv_b": grad_a_conv_b, "b_ln_g": grad_b_ln_g, "b_ln_b": grad_b_ln_b, "b_w_s": grad_b_w_s,
             "b_b_s": grad_b_b_s, "final_g": grad_final_g, **big_grads}
    order = ["mod_w", "mod_b", "norm_g", "a_w_in", "a_conv_w", "a_conv_b", "a_w_out", "b_w_in", "b_ln_g", "b_ln_b",
             "b_w_s", "b_b_s", "b_w_out", "final_g"]
    return (loss, grad_x[None], *[grads[k] for k in order], *[upd[k][0] for k in order],
            *[upd[k][1] for k in order], *[upd[k][2] for k in order])
```

```python
import math

import jax
import jax.numpy as jnp
from jax import lax
from jax.experimental import pallas as pl
from jax.experimental.pallas import tpu as pltpu

F32 = jnp.float32
BF16 = jnp.bfloat16
MESH = pl.DeviceIdType.MESH

N_DEV = 8
N_CHIP = 4
SUBLANE = 8
PACK = 16
ACT = F32
RMS_EPS = 1e-6
LN_EPS = 1e-5
CHUNK = 128
GROUPS = 8
ADAM_LR = 0.001
ADAM_B1 = 0.9
ADAM_B2 = 0.999
ADAM_EPS = 1e-08
ADAM_WD = 0.01
ADAM_STEP = 10
VMEM_LIMIT = 56 << 20

HBM_SPEC = pl.BlockSpec(memory_space=pltpu.HBM)
VMEM_SPEC = pl.BlockSpec(memory_space=pltpu.VMEM)
SEM_SPEC = pl.BlockSpec(memory_space=pltpu.SEMAPHORE)
ANY_SPEC = pl.BlockSpec(memory_space=pl.ANY)
EFFECT = pltpu.SideEffectType.DATAFLOW_SIDE_EFFECTING


def _params(*sem):
    return pltpu.CompilerParams(dimension_semantics=sem, vmem_limit_bytes=VMEM_LIMIT)


def _tile(n, want):
    if n <= want:
        return n
    t = want
    while n % t:
        t -= 128
    return t


def _sigmoid(x):
    return 0.5 * jnp.tanh(0.5 * x) + 0.5


def _silu_and_grad(x):
    s = _sigmoid(x)
    return x * s, s * (1.0 + x * (1.0 - s))


def _gelu_and_grad(x):
    cdf = 0.5 * (1.0 + lax.erf(x * (1.0 / math.sqrt(2.0))))
    pdf = jnp.exp(-0.5 * x * x) * (1.0 / math.sqrt(2.0 * math.pi))
    return x * cdf, cdf + x * pdf


def _gelu(x):
    return x * (0.5 * (1.0 + lax.erf(x * (1.0 / math.sqrt(2.0)))))


def _rms(x):
    r = lax.rsqrt(jnp.mean(x * x, axis=-1, keepdims=True) + RMS_EPS)
    return x * r, r


def _rms_bwd(dxn, xn, r):
    return r * (dxn - xn * jnp.mean(dxn * xn, axis=-1, keepdims=True))


def _colsum(a):
    return jnp.sum(a, axis=0, keepdims=True)


def _shift_down(cur, before, k):
    rolled = pltpu.roll(cur, k, 0)
    row = lax.broadcasted_iota(jnp.int32, before.shape, 0)
    head = jnp.where(row < k, pltpu.roll(before, k, 0), rolled[:SUBLANE])
    return jnp.concatenate([head, rolled[SUBLANE:]], axis=0)


def _shift_up(cur, after, k):
    n = cur.shape[0]
    rolled = pltpu.roll(cur, n - k, 0)
    row = lax.broadcasted_iota(jnp.int32, after.shape, 0)
    tail = jnp.where(row >= SUBLANE - k, pltpu.roll(after, SUBLANE - k, 0), rolled[n - SUBLANE:])
    return jnp.concatenate([rolled[:n - SUBLANE], tail], axis=0)


def _after_spec():
    return pl.BlockSpec((SUBLANE, 128), lambda *_: (0, 0))


def _mm_proj(h, w, n_split, name, after):
    s, d = h.shape
    e = w.shape[1] // n_split
    tm, tn = _tile(s, 2048), _tile(e, 2048)
    nj = e // tn

    def body(h_ref, w_ref, after_ref, o_ref):
        o_ref[...] = jnp.dot(h_ref[...], w_ref[...], preferred_element_type=F32).astype(ACT)

    return pl.pallas_call(
        body, name=name,
        out_shape=jax.ShapeDtypeStruct((n_split, s, e), ACT),
        grid=(s // tm, n_split * nj),
        in_specs=[pl.BlockSpec((tm, d), lambda i, j: (i, 0)), pl.BlockSpec((d, tn), lambda i, j: (0, j)),
                  _after_spec()],
        out_specs=pl.BlockSpec((None, tm, tn), lambda i, j: (j // nj, i, j % nj)),
        compiler_params=_params("parallel", "parallel"),
    )(h, w, after)


def _mm_proj_slab(h, w, proj, q_idx, n_split, name, after):
    s, d = h.shape
    e = w.shape[1] // n_split
    tm, tn = _tile(s, 1024), _tile(e, 2048)
    nj = e // tn

    def body(q_ref, h_ref, w_ref, after_ref, proj_ref, o_ref):
        o_ref[...] = jnp.dot(h_ref[...], w_ref[...], preferred_element_type=F32).astype(ACT)

    return pl.pallas_call(
        body, name=name,
        out_shape=jax.ShapeDtypeStruct((n_split, s, e), ACT),
        grid_spec=pltpu.PrefetchScalarGridSpec(
            num_scalar_prefetch=1, grid=(s // tm, nj),
            in_specs=[pl.BlockSpec((tm, d), lambda i, j, qr: (i, 0)),
                      pl.BlockSpec((d, tn), lambda i, j, qr: (0, qr[0] * nj + j)), _after_spec(), ANY_SPEC],
            out_specs=pl.BlockSpec((None, tm, tn), lambda i, j, qr: (qr[0], i, j))),
        input_output_aliases={4: 0},
        compiler_params=_params("parallel", "parallel"),
    )(q_idx, h, w, after, proj)


def _mm_dh_norm_bwd(dp, w, x, dx_in, g, scale, name, after, br=None, gate=None):
    nq, s, e = dp.shape
    d = w.shape[0]
    has_branch = br is not None
    tm, tk = _tile(s, 1024), _tile(e, 1024 if has_branch else 2048)
    nkq = e // tk
    nk = nq * nkq

    def body(*refs):
        if has_branch:
            (a_ref, b_ref, after_ref, x_ref, dxin_ref, g_ref, sc_ref, br_ref, gate_ref,
             dx_ref, dsh_ref, dsc_ref, dg_ref, dbr_ref, dgate_ref, acc_ref) = refs
        else:
            (a_ref, b_ref, after_ref, x_ref, dxin_ref, g_ref, sc_ref,
             dx_ref, dsh_ref, dsc_ref, dg_ref, acc_ref) = refs
        i, k = pl.program_id(0), pl.program_id(1)

        @pl.when(jnp.logical_and(i == 0, k == 0))
        def _():
            dsh_ref[...] = jnp.zeros_like(dsh_ref)
            dsc_ref[...] = jnp.zeros_like(dsc_ref)
            dg_ref[...] = jnp.zeros_like(dg_ref)
            if has_branch:
                dgate_ref[...] = jnp.zeros_like(dgate_ref)

        nt_dims = (((1,), (1,)), ((), ()))

        @pl.when(k == 0)
        def _():
            acc_ref[...] = lax.dot_general(a_ref[...], b_ref[...], nt_dims, preferred_element_type=F32)

        @pl.when(k > 0)
        def _():
            acc_ref[...] += lax.dot_general(a_ref[...], b_ref[...], nt_dims, preferred_element_type=F32)

        @pl.when(k == nk - 1)
        def _():
            dh = acc_ref[...]
            g_vec = g_ref[...]
            xn, r = _rms(x_ref[...])
            dsh_ref[...] += _colsum(dh)
            dsc_ref[...] += _colsum(dh * (xn * g_vec))
            da = dh * (1.0 + sc_ref[...])
            dg_ref[...] += _colsum(da * xn)
            dx = dxin_ref[...] + _rms_bwd(da * g_vec, xn, r)
            dx_ref[...] = dx
            if has_branch:
                dgate_ref[...] += _colsum(dx * br_ref[...])
                dbr_ref[...] = (gate_ref[...] * dx).astype(BF16)

    rows = pl.BlockSpec((tm, d), lambda i, k: (i, 0))
    vec = pl.BlockSpec((1, d), lambda i, k: (0, 0))
    vec_out = jax.ShapeDtypeStruct((1, d), F32)
    in_specs = [pl.BlockSpec((None, tm, tk), lambda i, k: (k // nkq, i, k % nkq)),
                pl.BlockSpec((d, tk), lambda i, k: (0, k)), _after_spec(), rows, rows, vec, vec]
    out_shape = [jax.ShapeDtypeStruct((s, d), F32), vec_out, vec_out, vec_out]
    out_specs = [rows, vec, vec, vec]
    args = [dp, w, after, x, dx_in, g, scale]
    if has_branch:
        in_specs += [rows, vec]
        out_shape += [jax.ShapeDtypeStruct((s, d), BF16), vec_out]
        out_specs += [rows, vec]
        args += [br, gate]
    return pl.pallas_call(
        body, name=name, out_shape=tuple(out_shape), grid=(s // tm, nk),
        in_specs=in_specs, out_specs=tuple(out_specs), scratch_shapes=[pltpu.VMEM((tm, d), F32)],
        compiler_params=_params("arbitrary", "arbitrary"),
    )(*args)


def _dw_body(n_t):
    def body(a_ref, b_ref, o_ref, acc_ref):
        t = pl.program_id(2)
        tn_dims = (((0,), (0,)), ((), ()))

        @pl.when(t == 0)
        def _():
            acc_ref[...] = lax.dot_general(a_ref[...], b_ref[...], tn_dims, preferred_element_type=F32)

        @pl.when(t > 0)
        def _():
            acc_ref[...] += lax.dot_general(a_ref[...], b_ref[...], tn_dims, preferred_element_type=F32)

        @pl.when(t == n_t - 1)
        def _():
            o_ref[...] = acc_ref[...].astype(o_ref.dtype)
    return body


def _mm_dw_in(h, dp, name):
    s, d = h.shape
    nq, _, e = dp.shape
    tm, tn, tt = _tile(d, 1024), _tile(e, 2048), _tile(s, 2048)
    nj = e // tn

    return pl.pallas_call(
        _dw_body(s // tt), name=name,
        out_shape=jax.ShapeDtypeStruct((d, nq * e), BF16),
        grid=(d // tm, nq * nj, s // tt),
        in_specs=[pl.BlockSpec((tt, tm), lambda i, j, t: (t, i)),
                  pl.BlockSpec((None, tt, tn), lambda i, j, t: (j // nj, t, j % nj))],
        out_specs=pl.BlockSpec((tm, tn), lambda i, j, t: (i, j)),
        scratch_shapes=[pltpu.VMEM((tm, tn), F32)],
        compiler_params=_params("parallel", "parallel", "arbitrary"),
    )(h, dp)


def _mm_dw_out(y, dbr, name):
    s, e = y.shape
    d = dbr.shape[1]
    tm, tn, tt = _tile(e, 1024), _tile(d, 1024), _tile(s, 2048)

    return pl.pallas_call(
        _dw_body(s // tt), name=name,
        out_shape=jax.ShapeDtypeStruct((e, d), BF16),
        grid=(e // tm, d // tn, s // tt),
        in_specs=[pl.BlockSpec((tt, tm), lambda i, j, t: (t, i)), pl.BlockSpec((tt, tn), lambda i, j, t: (t, j))],
        out_specs=pl.BlockSpec((tm, tn), lambda i, j, t: (i, j)),
        scratch_shapes=[pltpu.VMEM((tm, tn), F32)],
        compiler_params=_params("parallel", "parallel", "arbitrary"),
    )(y, dbr)


def _row_spec(ts, d):
    return pl.BlockSpec((ts, d), lambda i: (i, 0))


def _vec_spec(d):
    return pl.BlockSpec((1, d), lambda i: (0, 0))


def _norm_mod_proj_slab(x, g, scale, shift, w, q_idx, n_split, name, after):
    s, d = x.shape
    e = w.shape[1] // n_split
    tm = _tile(s, 1024)

    def body(q_ref, x_ref, g_ref, sc_ref, sh_ref, w_ref, after_ref, h_ref, o_ref):
        xn, _ = _rms(x_ref[...])
        h = ((xn * g_ref[...]) * (1.0 + sc_ref[...]) + sh_ref[...]).astype(BF16)
        h_ref[...] = h
        o_ref[...] = jnp.dot(h, w_ref[...], preferred_element_type=F32).astype(ACT)

    vec = pl.BlockSpec((1, d), lambda i, qr: (0, 0))
    rows = pl.BlockSpec((tm, d), lambda i, qr: (i, 0))
    return pl.pallas_call(
        body, name=name,
        out_shape=(jax.ShapeDtypeStruct((s, d), BF16), jax.ShapeDtypeStruct((n_split, s, e), ACT)),
        grid_spec=pltpu.PrefetchScalarGridSpec(
            num_scalar_prefetch=1, grid=(s // tm,),
            in_specs=[rows, vec, vec, vec, pl.BlockSpec((d, e), lambda i, qr: (0, qr[0])), _after_spec()],
            out_specs=(rows, pl.BlockSpec((None, tm, e), lambda i, qr: (qr[0], i, 0)))),
        compiler_params=_params("parallel"),
    )(q_idx, x, g, scale, shift, w, after)


def _head_tile(x1, br, gate, gf, target):
    d = x1.shape[-1]
    xn, r = _rms(x1 + gate * br)
    err = xn * gf - target
    loss = 0.5 * jnp.sum(jnp.mean(err * err, axis=-1, keepdims=True))
    dout = err * (1.0 / d)
    dx = _rms_bwd(dout * gf, xn, r)
    return dx, gate * dx, loss, _colsum(dout * xn), _colsum(dx * br)


def _f32(ref):
    return ref[...].astype(F32)


def _cols_f32(ref, cols):
    return ref[:, cols].astype(F32)


def _rows_before(halo_ref, cols=slice(None)):
    return _cols_f32(halo_ref, cols)[PACK - SUBLANE:]


def _rows_after(halo_ref, cols=slice(None)):
    return _cols_f32(halo_ref, cols)[:SUBLANE]


CONV_CHAIN_COLS = 512


def _conv_fwd(proj, conv_w, conv_b, w_out, x, gate, g_next, scale_next, shift_next, name, after):
    _, s, e = proj.shape
    d = w_out.shape[1]
    ts, te = _tile(s, 256), e
    cw = min(CONV_CHAIN_COLS, te)
    hb = ts // PACK
    n_i = s // ts
    slots = 3

    def body(proj_ref, cgp_ref, xip_ref, w0_ref, w1_ref, w2_ref, b_ref, wo_ref, after_ref,
             x_ref, gate_ref, g_ref, sc_ref, sh_ref, y_ref, br_ref, x1_ref, h_ref, ring, ring_sems):
        i = pl.program_id(0)
        first = i == 0

        def fetch(step, slot):
            rows = pl.ds(step * ts if isinstance(step, int) else pl.multiple_of(step * ts, ts), ts)
            return [pltpu.make_async_copy(proj_ref.at[q, rows, :], ring.at[slot, q], ring_sems.at[slot, q])
                    for q in range(4)]

        @pl.when(first)
        def _():
            for step in range(min(slots - 1, n_i)):
                for cp in fetch(step, step):
                    cp.start()

        @pl.when(i + (slots - 1) < n_i)
        def _():
            for cp in fetch(i + (slots - 1), (i + (slots - 1)) % slots):
                cp.start()
        slot = i % slots
        for cp in fetch(i, slot):
            cp.wait()

        def part_f32(q, cols):
            return ring[slot, q, :, cols].astype(F32)

        br = None
        for c0 in range(0, te, cw):
            cols = slice(c0, c0 + cw)
            cx = part_f32(1, cols) * part_f32(2, cols)
            before = jnp.where(first, 0.0, _rows_before(cgp_ref, cols) * _rows_before(xip_ref, cols))
            conv = b_ref[:, cols] + w2_ref[:, cols] * cx
            conv = conv + w0_ref[:, cols] * _shift_down(cx, before, 2)
            conv = conv + w1_ref[:, cols] * _shift_down(cx, before, 1)
            z = part_f32(3, cols)
            y = ((z * _sigmoid(z)) * part_f32(0, cols) * conv).astype(BF16)
            y_ref[:, cols] = y
            part_br = jnp.dot(y, wo_ref[cols, :], preferred_element_type=F32)
            br = part_br if br is None else br + part_br
        br_ref[...] = br
        x1 = x_ref[...] + gate_ref[...] * br
        x1_ref[...] = x1
        xn, _ = _rms(x1)
        h_ref[...] = ((xn * g_ref[...]) * (1.0 + sc_ref[...]) + sh_ref[...]).astype(BF16)

    def halo_before(q):
        return pl.BlockSpec((None, PACK, te), lambda i: (q, jnp.maximum(i * hb - 1, 0), 0))

    return pl.pallas_call(
        body, name=name,
        out_shape=(jax.ShapeDtypeStruct((s, e), BF16), jax.ShapeDtypeStruct((s, d), F32),
                   jax.ShapeDtypeStruct((s, d), F32), jax.ShapeDtypeStruct((s, d), BF16)), grid=(n_i,),
        in_specs=[ANY_SPEC, halo_before(1), halo_before(2)]
        + [_vec_spec(e)] * 4 + [pl.BlockSpec((e, d), lambda i: (0, 0), pipeline_mode=pl.Buffered(1)), _after_spec(),
                                _row_spec(ts, d)]
        + [_vec_spec(d)] * 4,
        out_specs=(_row_spec(ts, e), _row_spec(ts, d), _row_spec(ts, d), _row_spec(ts, d)),
        scratch_shapes=[pltpu.VMEM((slots, 4, ts, te), proj.dtype), pltpu.SemaphoreType.DMA((slots, 4))],
        compiler_params=_params("arbitrary"),
    )(proj, proj, proj, *conv_w, conv_b, w_out, after, x, gate, g_next, scale_next, shift_next)


def _conv_bwd(proj, dbr, w_out, conv_w, conv_b, name, after):
    _, s, e = proj.shape
    d = dbr.shape[1]
    ts, te = _tile(s, 512), _tile(e, 1024)
    cw = min(CONV_CHAIN_COLS // 2, te)
    hb = ts // PACK
    n_i = s // ts
    last_halo = s // PACK - 1
    nt = (((1,), (1,)), ((), ()))

    def body(bg_ref, cg_ref, xi_ref, z_ref, dbr_ref, cgp_ref, xip_ref, bgn_ref, zn_ref, dbrn_ref, wo_ref,
             w0_ref, w1_ref, w2_ref, b_ref, after_ref, dp_ref, dw0_ref, dw1_ref, dw2_ref, db_ref):
        i = pl.program_id(1)

        @pl.when(i == 0)
        def _():
            for acc in (dw0_ref, dw1_ref, dw2_ref, db_ref):
                acc[...] = jnp.zeros_like(acc)
        for c0 in range(0, te, cw):
            cols = slice(c0, c0 + cw)
            wo = wo_ref[cols, :]
            dy = lax.dot_general(dbr_ref[...], wo, nt, preferred_element_type=F32)
            dyn = lax.dot_general(dbrn_ref[...], wo, nt, preferred_element_type=F32)[:SUBLANE]
            bg, cg = _cols_f32(bg_ref, cols), _cols_f32(cg_ref, cols)
            xi, z = _cols_f32(xi_ref, cols), _cols_f32(z_ref, cols)
            w0, w1, w2 = w0_ref[:, cols], w1_ref[:, cols], w2_ref[:, cols]
            cx = cg * xi
            before = jnp.where(i > 0, _rows_before(cgp_ref, cols) * _rows_before(xip_ref, cols), 0.0)
            cx1 = _shift_down(cx, before, 1)
            cx2 = _shift_down(cx, before, 2)
            conv = b_ref[:, cols] + w2 * cx
            conv = conv + w0 * cx2
            conv = conv + w1 * cx1
            sz, dsz = _silu_and_grad(z)
            dp_ref[3, :, cols] = (dy * bg * conv * dsz).astype(BF16)
            dp_ref[0, :, cols] = (dy * sz * conv).astype(BF16)
            dconv = dy * sz * bg
            zn = _rows_after(zn_ref, cols)
            after = jnp.where(i < n_i - 1, dyn * (zn * _sigmoid(zn)) * _rows_after(bgn_ref, cols), 0.0)
            db_ref[:, cols] += _colsum(dconv)
            dw2_ref[:, cols] += _colsum(dconv * cx)
            dw1_ref[:, cols] += _colsum(dconv * cx1)
            dw0_ref[:, cols] += _colsum(dconv * cx2)
            dcx = w2 * dconv + w1 * _shift_up(dconv, after, 1) + w0 * _shift_up(dconv, after, 2)
            dp_ref[1, :, cols] = (dcx * xi).astype(BF16)
            dp_ref[2, :, cols] = (dcx * cg).astype(BF16)

    def part(q):
        return pl.BlockSpec((None, ts, te), lambda j, i: (q, i, j))

    def halo_before(q):
        return pl.BlockSpec((None, PACK, te), lambda j, i: (q, jnp.maximum(i * hb - 1, 0), j))

    def halo_after(q):
        return pl.BlockSpec((None, PACK, te), lambda j, i: (q, jnp.minimum((i + 1) * hb, last_halo), j))

    return pl.pallas_call(
        body, name=name,
        out_shape=(jax.ShapeDtypeStruct((4, s, e), BF16),) + (jax.ShapeDtypeStruct((1, e), F32),) * 4,
        grid=(e // te, n_i),
        in_specs=[part(0), part(1), part(2), part(3), pl.BlockSpec((ts, d), lambda j, i: (i, 0)),
                  halo_before(1), halo_before(2), halo_after(0), halo_after(3),
                  pl.BlockSpec((PACK, d), lambda j, i: (jnp.minimum((i + 1) * hb, last_halo), 0)),
                  pl.BlockSpec((te, d), lambda j, i: (j, 0))]
        + [pl.BlockSpec((1, te), lambda j, i: (0, j))] * 4 + [_after_spec()],
        out_specs=(pl.BlockSpec((4, ts, te), lambda j, i: (0, i, j)),) + (pl.BlockSpec((1, te), lambda j, i: (0, j)),) * 4,
        compiler_params=_params("parallel", "arbitrary"),
    )(proj, proj, proj, proj, dbr, proj, proj, proj, proj, dbr, w_out, *conv_w, conv_b, after)


def _tril(w):
    row = lax.broadcasted_iota(jnp.int32, w.shape, 0)
    col = lax.broadcasted_iota(jnp.int32, w.shape, 1)
    return jnp.where(row >= col, w, 0.0)


def _triu(w):
    row = lax.broadcasted_iota(jnp.int32, w.shape, 0)
    col = lax.broadcasted_iota(jnp.int32, w.shape, 1)
    return jnp.where(row <= col, w, 0.0)


def _layer_norm_fwd(v, g, b):
    mu = jnp.mean(v, axis=-1, keepdims=True)
    vc = v - mu
    rstd = lax.rsqrt(jnp.mean(vc * vc, axis=-1, keepdims=True) + LN_EPS)
    vhat = vc * rstd
    return vhat * g + b, vhat, rstd


GMLP_CHUNKS_PER_STEP = 2


def _gmlp_rows(s):
    return CHUNK * min(GMLP_CHUNKS_PER_STEP, s // CHUNK)


def _mix_positions(w_ref, src_scr, dst_scr, gw, mask, bias_ref=None):
    for gi in range(GROUPS):
        cols = slice(gi * gw, (gi + 1) * gw)
        wm = mask(w_ref[gi]).astype(BF16)
        for n in range(src_scr.shape[0] // CHUNK):
            rows = slice(n * CHUNK, (n + 1) * CHUNK)
            out = jnp.dot(wm, src_scr[rows, cols], preferred_element_type=F32)
            if bias_ref is not None:
                out = out + bias_ref[:, gi:gi + 1]
            dst_scr[rows, cols] = out


def _gmlp_fwd_head(proj, ln_g, ln_b, w_s, b_s_t, w_out, x1, gate, gf, target, name):
    _, s, e = proj.shape
    d = w_out.shape[1]
    gw = e // GROUPS
    ts = CHUNK * min(2 * GMLP_CHUNKS_PER_STEP, s // CHUNK)

    def body(pu_ref, pv_ref, pz_ref, g_ref, b_ref, ws_ref, bs_ref, wo_ref, x1_ref, gate_ref, gf_ref, tg_ref,
             y_ref, dx_ref, dbr_ref, loss_ref, dgf_ref, dgate_ref, vn_scr, mix_scr):
        @pl.when(pl.program_id(0) == 0)
        def _():
            loss_ref[...] = jnp.zeros_like(loss_ref)
            dgf_ref[...] = jnp.zeros_like(dgf_ref)
            dgate_ref[...] = jnp.zeros_like(dgate_ref)
        vn, _, _ = _layer_norm_fwd(_gelu(_f32(pv_ref)), g_ref[...], b_ref[...])
        vn_scr[...] = vn.astype(BF16)
        _mix_positions(ws_ref, vn_scr, mix_scr, gw, _tril, bs_ref)
        z = _f32(pz_ref)
        y = ((z * _sigmoid(z)) * (_gelu(_f32(pu_ref)) * mix_scr[...])).astype(BF16)
        y_ref[...] = y
        br = jnp.dot(y, wo_ref[...], preferred_element_type=F32)
        dx, dbr, loss, dgf, dgate = _head_tile(x1_ref[...], br, gate_ref[...], gf_ref[...], tg_ref[...])
        dx_ref[...] = dx
        dbr_ref[...] = dbr.astype(BF16)
        loss_ref[...] += loss
        dgf_ref[...] += dgf
        dgate_ref[...] += dgate

    def part(q):
        return pl.BlockSpec((None, ts, e), lambda i: (q, i, 0))

    return pl.pallas_call(
        body, name=name,
        out_shape=(jax.ShapeDtypeStruct((s, e), BF16), jax.ShapeDtypeStruct((s, d), F32),
                   jax.ShapeDtypeStruct((s, d), BF16), jax.ShapeDtypeStruct((SUBLANE, 128), F32),
                   jax.ShapeDtypeStruct((1, d), F32), jax.ShapeDtypeStruct((1, d), F32)),
        grid=(s // ts,),
        in_specs=[part(0), part(1), part(2), _vec_spec(e), _vec_spec(e),
                  pl.BlockSpec((GROUPS, CHUNK, CHUNK), lambda i: (0, 0, 0), pipeline_mode=pl.Buffered(1)),
                  pl.BlockSpec((CHUNK, GROUPS), lambda i: (0, 0)),
                  pl.BlockSpec((e, d), lambda i: (0, 0), pipeline_mode=pl.Buffered(1)),
                  _row_spec(ts, d), _vec_spec(d), _vec_spec(d), _row_spec(ts, d)],
        out_specs=(_row_spec(ts, e), _row_spec(ts, d), _row_spec(ts, d),
                   pl.BlockSpec((SUBLANE, 128), lambda i: (0, 0)), _vec_spec(d), _vec_spec(d)),
        scratch_shapes=[pltpu.VMEM((ts, e), BF16), pltpu.VMEM((ts, e), F32)],
        compiler_params=_params("arbitrary"),
    )(proj, proj, proj, ln_g, ln_b, w_s, b_s_t, w_out, x1, gate, gf, target)


def _gmlp_bwd(proj, dbr, w_out, ln_g, ln_b, w_s, w_s_t, b_s_t, name):
    _, s, e = proj.shape
    d = dbr.shape[1]
    gw = e // GROUPS
    ts = _gmlp_rows(s)
    n_i = s // ts

    def body(pu_ref, pv_ref, pz_ref, dbr_ref, wo_ref, g_ref, b_ref, ws_ref, wst_ref, bs_ref,
             dp_ref, dws_ref, dbs_ref, dlg_ref, dlb_ref, vn_scr, mix_scr, dm_scr, dvn_scr, dmacc_scr):
        i = pl.program_id(0)

        @pl.when(i == 0)
        def _():
            dws_ref[...] = jnp.zeros_like(dws_ref)
            dlg_ref[...] = jnp.zeros_like(dlg_ref)
            dlb_ref[...] = jnp.zeros_like(dlb_ref)
            dmacc_scr[...] = jnp.zeros_like(dmacc_scr)
        ln_g = g_ref[...]
        u, du_dpu = _gelu_and_grad(_f32(pu_ref))
        v, dv_dpv = _gelu_and_grad(_f32(pv_ref))
        vn, vhat, rstd = _layer_norm_fwd(v, ln_g, b_ref[...])
        vn_scr[...] = vn.astype(BF16)
        _mix_positions(ws_ref, vn_scr, mix_scr, gw, _tril, bs_ref)
        mixed = mix_scr[...]
        dy = lax.dot_general(dbr_ref[...], wo_ref[...], (((1,), (1,)), ((), ())), preferred_element_type=F32)
        sz, dsz = _silu_and_grad(_f32(pz_ref))
        ds = dy * sz
        dp_ref[2] = (dy * (u * mixed) * dsz).astype(BF16)
        dp_ref[0] = (ds * mixed * du_dpu).astype(BF16)
        dm = ds * u
        dm_scr[...] = dm.astype(BF16)
        for n in range(ts // CHUNK):
            rows = slice(n * CHUNK, (n + 1) * CHUNK)
            dmacc_scr[...] += dm[rows]
            for gi in range(GROUPS):
                cols = slice(gi * gw, (gi + 1) * gw)
                dws_ref[gi] += lax.dot_general(dm_scr[rows, cols], vn_scr[rows, cols], (((1,), (1,)), ((), ())),
                                               preferred_element_type=F32)
        _mix_positions(wst_ref, dm_scr, dvn_scr, gw, _triu)
        dvn = dvn_scr[...]
        dlg_ref[...] += _colsum(dvn * vhat)
        dlb_ref[...] += _colsum(dvn)
        dvh = dvn * ln_g
        dv = rstd * (dvh - jnp.mean(dvh, axis=-1, keepdims=True) - vhat * jnp.mean(dvh * vhat, axis=-1, keepdims=True))
        dp_ref[1] = (dv * dv_dpv).astype(BF16)

        @pl.when(i == n_i - 1)
        def _():
            for gi in range(GROUPS):
                dws_ref[gi] = _tril(dws_ref[gi])
                dbs_ref[:, gi:gi + 1] = jnp.sum(dmacc_scr[:, gi * gw:(gi + 1) * gw], axis=1, keepdims=True)

    def part(q):
        return pl.BlockSpec((None, ts, e), lambda i: (q, i, 0))

    w_spec = pl.BlockSpec((GROUPS, CHUNK, CHUNK), lambda i: (0, 0, 0))
    bs_spec = pl.BlockSpec((CHUNK, GROUPS), lambda i: (0, 0))
    return pl.pallas_call(
        body, name=name,
        out_shape=(jax.ShapeDtypeStruct((3, s, e), BF16), jax.ShapeDtypeStruct((GROUPS, CHUNK, CHUNK), F32),
                   jax.ShapeDtypeStruct((CHUNK, GROUPS), F32), jax.ShapeDtypeStruct((1, e), F32),
                   jax.ShapeDtypeStruct((1, e), F32)),
        grid=(n_i,),
        in_specs=[part(0), part(1), part(2), pl.BlockSpec((ts, d), lambda i: (i, 0)),
                  pl.BlockSpec((e, d), lambda i: (0, 0)), _vec_spec(e), _vec_spec(e), w_spec, w_spec, bs_spec],
        out_specs=(pl.BlockSpec((3, ts, e), lambda i: (0, i, 0)), w_spec, bs_spec, _vec_spec(e), _vec_spec(e)),
        scratch_shapes=[pltpu.VMEM((ts, e), BF16), pltpu.VMEM((ts, e), F32), pltpu.VMEM((ts, e), BF16),
                        pltpu.VMEM((ts, e), F32), pltpu.VMEM((CHUNK, e), F32)],
        compiler_params=_params("arbitrary"),
    )(proj, proj, proj, dbr, w_out, ln_g, ln_b, w_s, w_s_t, b_s_t)


def _mod_fwd(c_all, mod_w, mod_b_cols, name):
    n_layer, d, w = mod_w.shape

    def body(c_ref, w_ref, b_ref, ca_ref, o_ref):
        c = c_ref[...]
        ca = c * _sigmoid(c)
        ca_ref[...] = ca
        for li in range(n_layer):
            o_ref[li * N_DEV:(li + 1) * N_DEV, :] = (
                jnp.dot(ca, w_ref[li], preferred_element_type=F32, precision=lax.Precision.HIGHEST) + b_ref[li])

    return pl.pallas_call(
        body, name=name,
        out_shape=(jax.ShapeDtypeStruct((N_DEV, d), F32), jax.ShapeDtypeStruct((n_layer * N_DEV, w), F32)),
        in_specs=[VMEM_SPEC] * 3, out_specs=(VMEM_SPEC, VMEM_SPEC),
        compiler_params=pltpu.CompilerParams(vmem_limit_bytes=VMEM_LIMIT),
    )(c_all, mod_w, mod_b_cols)


def _adamw(w, g, m, v):
    m = ADAM_B1 * m + (1.0 - ADAM_B1) * g
    v = ADAM_B2 * v + (1.0 - ADAM_B2) * (g * g)
    m_hat = m / (1.0 - ADAM_B1 ** ADAM_STEP)
    v_hat = v / (1.0 - ADAM_B2 ** ADAM_STEP)
    delta = -ADAM_LR * (m_hat / (jnp.sqrt(v_hat) + ADAM_EPS) + ADAM_WD * w)
    return delta, m, v


def _adamw_2d(w, g, m, v, name):
    r, c = w.shape
    tr, tc = _tile(r, 512), _tile(c, 1024)

    def body(w_ref, g_ref, m_ref, v_ref, go_ref, d_ref, nm_ref, nv_ref):
        g = g_ref[...]
        go_ref[...] = g
        d_ref[...], nm_ref[...], nv_ref[...] = _adamw(w_ref[...], g, m_ref[...], v_ref[...])

    spec = pl.BlockSpec((tr, tc), lambda i, j: (i, j))
    shape = jax.ShapeDtypeStruct((r, c), F32)
    return pl.pallas_call(
        body, name=name, out_shape=(shape,) * 4, grid=(r // tr, c // tc),
        in_specs=[spec] * 4, out_specs=(spec,) * 4, compiler_params=_params("parallel", "parallel"),
    )(w, g, m, v)


def _mod_w_update(ca_t, dmod_cols, w, m, v, name):
    n_layer, d, wd = w.shape
    tr = _tile(d, 256)

    def body(ca_ref, dm_ref, w_ref, m_ref, v_ref, g_ref, d_ref, nm_ref, nv_ref):
        ca = ca_ref[...]
        dm = dm_ref[...]
        g = ca[:, 0:1] * dm[0:1, :]
        for b in range(1, N_DEV):
            g = g + ca[:, b:b + 1] * dm[b:b + 1, :]
        g_ref[...] = g
        d_ref[...], nm_ref[...], nv_ref[...] = _adamw(w_ref[...], g, m_ref[...], v_ref[...])

    spec = pl.BlockSpec((None, tr, wd), lambda l, i: (l, i, 0))
    shape = jax.ShapeDtypeStruct((n_layer, d, wd), F32)
    return pl.pallas_call(
        body, name=name, out_shape=(shape,) * 4, grid=(n_layer, d // tr),
        in_specs=[pl.BlockSpec((tr, N_DEV), lambda l, i: (i, 0)), pl.BlockSpec((None, N_DEV, wd), lambda l, i: (l, 0, 0)),
                  spec, spec, spec],
        out_specs=(spec,) * 4, compiler_params=_params("parallel", "parallel"),
    )(ca_t, dmod_cols, w, m, v)


def _adamw_small(ws, gs, ms, vs, name):
    n = len(ws)

    def body(*refs):
        ins, outs = refs[:4 * n], refs[4 * n:]
        for k in range(n):
            delta, nm, nv = _adamw(ins[k][...], ins[n + k][...], ins[2 * n + k][...], ins[3 * n + k][...])
            outs[3 * k][...] = delta
            outs[3 * k + 1][...] = nm
            outs[3 * k + 2][...] = nv

    out_shape = []
    for w in ws:
        out_shape += [jax.ShapeDtypeStruct(w.shape, F32)] * 3
    outs = pl.pallas_call(
        body, name=name, out_shape=tuple(out_shape),
        in_specs=[VMEM_SPEC] * (4 * n), out_specs=tuple([VMEM_SPEC] * (3 * n)),
        compiler_params=pltpu.CompilerParams(vmem_limit_bytes=VMEM_LIMIT),
    )(*ws, *gs, *ms, *vs)
    return [tuple(outs[3 * k:3 * k + 3]) for k in range(n)]


def _place():
    return lax.axis_index("x"), lax.axis_index("y"), lax.axis_index("c")


def _other_chips(x, y):
    return [(1 - x, y), (x, 1 - y), (1 - x, 1 - y)]


def _all_gather(block, name, after=None):
    m_per, n = block.shape

    def body(x_ref, *rest):
        out_ref, send_sems, recv_sems, local_sem = rest[-4:]
        x, y, c = _place()
        me, sibling = (x, y, c), (x, y, 1 - c)
        chips = _other_chips(x, y)

        def rows(px, py, pc):
            return out_ref.at[pl.ds((4 * px + 2 * py + pc) * m_per, m_per), :]

        def copy(k, blk, to, src=None):
            return pltpu.make_async_remote_copy(
                src_ref=rows(*blk) if src is None else src, dst_ref=rows(*blk),
                send_sem=send_sems.at[k], recv_sem=recv_sems.at[k], device_id=to, device_id_type=MESH)

        mine = pltpu.make_async_copy(x_ref, rows(*me), local_sem)
        mine.start()
        first = [copy(0, me, sibling, src=x_ref)]
        first += [copy(1 + j, me, (*chip, c), src=x_ref) for j, chip in enumerate(chips)]
        for cp in first:
            cp.start()
        passed = [copy(4 + j, (*chip, c), sibling) for j, chip in enumerate(chips)]
        for j, chip in enumerate(chips):
            copy(1 + j, (*chip, c), me).wait_recv()
            passed[j].start()
        copy(0, sibling, me).wait_recv()
        for j, chip in enumerate(chips):
            copy(4 + j, (*chip, 1 - c), me).wait_recv()
        for cp in first + passed:
            cp.wait_send()
        mine.wait()

    return pl.pallas_call(
        body, name=name, out_shape=jax.ShapeDtypeStruct((N_DEV * m_per, n), F32),
        in_specs=[VMEM_SPEC] + ([] if after is None else [ANY_SPEC]), out_specs=VMEM_SPEC,
        scratch_shapes=[pltpu.SemaphoreType.DMA((7,)), pltpu.SemaphoreType.DMA((7,)), pltpu.SemaphoreType.DMA],
        compiler_params=pltpu.CompilerParams(vmem_limit_bytes=VMEM_LIMIT),
    )(*([block] if after is None else [block, after]))


def _hbm(a):
    return pltpu.with_memory_space_constraint(a, pltpu.HBM)


def _all_gather_start(blocks, name):
    def body(in_ref, send_sems, recv_sems, thru_ref, token):
        token[...] = jnp.zeros_like(token)
        x, y, c = _place()
        mine = in_ref.at[4 * x + 2 * y + c]
        pltpu.make_async_remote_copy(
            src_ref=mine, dst_ref=mine, send_sem=send_sems.at[0], recv_sem=recv_sems.at[0],
            device_id=(x, y, 1 - c), device_id_type=MESH).start()
        for j, chip in enumerate(_other_chips(x, y)):
            for cc in range(2):
                pltpu.make_async_remote_copy(
                    src_ref=mine, dst_ref=mine, send_sem=send_sems.at[1 + 2 * j + cc],
                    recv_sem=recv_sems.at[1 + 2 * j + c], device_id=(*chip, cc), device_id_type=MESH).start()

    sems = pltpu.SemaphoreType.DMA((N_DEV - 1,))
    return pl.pallas_call(
        body, name=name,
        out_shape=(sems, sems, pltpu.HBM(blocks.shape, blocks.dtype), jax.ShapeDtypeStruct((SUBLANE, 128), F32)),
        in_specs=[HBM_SPEC], out_specs=(SEM_SPEC, SEM_SPEC, HBM_SPEC, VMEM_SPEC), input_output_aliases={0: 2},
        compiler_params=pltpu.CompilerParams(has_side_effects=EFFECT),
    )(_hbm(blocks))


def _all_gather_wait(send_sems, recv_sems, blocks, after, name):
    def body(in_ref, send_sems, recv_sems, after_ref, out_ref):
        x, y, c = _place()
        one = in_ref.at[0]
        for k in range(N_DEV - 1):
            pltpu.make_async_remote_copy(
                src_ref=one, dst_ref=one, send_sem=send_sems.at[k], recv_sem=recv_sems.at[k],
                device_id=(x, y, 1 - c), device_id_type=MESH).wait_send()
            pltpu.make_async_remote_copy(
                src_ref=one, dst_ref=one, send_sem=send_sems.at[k], recv_sem=recv_sems.at[k],
                device_id=(x, y, 1 - c), device_id_type=MESH).wait_recv()

    return pl.pallas_call(
        body, name=name, out_shape=pltpu.HBM(blocks.shape, blocks.dtype),
        in_specs=[HBM_SPEC, SEM_SPEC, SEM_SPEC, ANY_SPEC], out_specs=HBM_SPEC, input_output_aliases={0: 0},
        compiler_params=pltpu.CompilerParams(has_side_effects=EFFECT),
    )(blocks, send_sems, recv_sems, after)


def _place_own(shard, kind, chip_idx, name):
    r, cdim = shard.shape
    tr, tc = _tile(r, 512), _tile(cdim, 1024)
    nrb, ncb = r // tr, cdim // tc

    def body(k_ref, s_ref, o_ref):
        o_ref[...] = s_ref[...].astype(BF16)

    if kind == "col":
        full, o_map = (r, N_CHIP * cdim), lambda i, j, kr: (i, kr[0] * ncb + j)
    else:
        full, o_map = (N_CHIP * r, cdim), lambda i, j, kr: (kr[0] * nrb + i, j)
    return pl.pallas_call(
        body, name=name, out_shape=jax.ShapeDtypeStruct(full, BF16),
        grid_spec=pltpu.PrefetchScalarGridSpec(
            num_scalar_prefetch=1, grid=(nrb, ncb),
            in_specs=[pl.BlockSpec((tr, tc), lambda i, j, kr: (i, j))],
            out_specs=pl.BlockSpec((tr, tc), o_map)),
        compiler_params=_params("parallel", "parallel"),
    )(chip_idx, shard)


def _weight_window(ref, kind, shard_shape, k, half):
    r, cdim = shard_shape
    hr = r // 2
    if kind == "col":
        return ref.at[pl.ds(half * hr, hr), pl.ds(pl.multiple_of(k * cdim, 128), cdim)]
    return ref.at[pl.ds(pl.multiple_of(k * r + half * hr, 2 * SUBLANE), hr), :]


def _target_cores(c, both_cores):
    return [(cc, cc, c) for cc in range(2)] if both_cores else [(c, 0, 0)]


def _gather_weights_start(fulls, kinds, shard_shapes, groups, after, name, both_cores=True):
    n, ng = len(fulls), len(groups)
    per = 2 if both_cores else 1

    def body(*refs):
        ins = refs[:n]
        sems = refs[n + 1:n + 1 + 2 * ng]
        token = refs[2 * n + 1 + 2 * ng]
        x, y, c = _place()
        chips = _other_chips(x, y)
        for g, group in enumerate(groups):
            for pos, (w, j) in enumerate(group):
                own = _weight_window(ins[w], kinds[w], shard_shapes[w], 2 * x + y, c)
                for cc, mine, theirs in _target_cores(c, both_cores):
                    pltpu.make_async_remote_copy(
                        src_ref=own, dst_ref=own,
                        send_sem=sems[2 * g].at[per * pos + mine], recv_sem=sems[2 * g + 1].at[per * pos + theirs],
                        device_id=(*chips[j], cc), device_id_type=MESH).start()
        token[...] = jnp.zeros_like(token)

    sem_shapes = []
    for group in groups:
        sem_shapes += [pltpu.SemaphoreType.DMA((per * len(group),))] * 2
    outs = pl.pallas_call(
        body, name=name,
        out_shape=tuple(sem_shapes) + tuple(pltpu.HBM(f.shape, f.dtype) for f in fulls)
        + (jax.ShapeDtypeStruct((SUBLANE, 128), F32),),
        in_specs=[HBM_SPEC] * n + [ANY_SPEC], out_specs=(SEM_SPEC,) * (2 * ng) + (HBM_SPEC,) * n + (VMEM_SPEC,),
        input_output_aliases={w: 2 * ng + w for w in range(n)},
        compiler_params=pltpu.CompilerParams(has_side_effects=EFFECT),
    )(*[_hbm(f) for f in fulls], after)
    sems = [(outs[2 * g], outs[2 * g + 1]) for g in range(ng)]
    return sems, list(outs[2 * ng:2 * ng + n]), outs[2 * ng + n]


def _gather_weights_wait(sems, group, fulls, kinds, shard_shapes, after, name, both_cores=True):
    n = len(fulls)
    per = 2 if both_cores else 1

    def body(*refs):
        ins = refs[:n]
        send_sems, recv_sems = refs[n], refs[n + 1]
        x, y, c = _place()
        chips = _other_chips(x, y)
        for pos, (w, j) in enumerate(group):
            own = _weight_window(ins[w], kinds[w], shard_shapes[w], 2 * x + y, c)
            for cc, mine, _ in _target_cores(c, both_cores):
                landed = _weight_window(ins[w], kinds[w], shard_shapes[w], 2 * chips[j][0] + chips[j][1], cc)
                k = per * pos + mine
                pltpu.make_async_remote_copy(
                    src_ref=own, dst_ref=own, send_sem=send_sems.at[k], recv_sem=recv_sems.at[k],
                    device_id=(*chips[j], cc), device_id_type=MESH).wait_send()
                pltpu.make_async_remote_copy(
                    src_ref=landed, dst_ref=landed, send_sem=send_sems.at[k], recv_sem=recv_sems.at[k],
                    device_id=(*chips[j], cc), device_id_type=MESH).wait_recv()

    return list(pl.pallas_call(
        body, name=name, out_shape=tuple(pltpu.HBM(f.shape, f.dtype) for f in fulls),
        in_specs=[HBM_SPEC] * n + [SEM_SPEC, SEM_SPEC, ANY_SPEC], out_specs=(HBM_SPEC,) * n,
        input_output_aliases={w: w for w in range(n)},
        compiler_params=pltpu.CompilerParams(has_side_effects=EFFECT),
    )(*fulls, sems[0], sems[1], after))


def _forward_to_sibling(full, kind, shard_shape, relations, name):
    nr = len(relations)

    def body(in_ref, out_ref, send_sems, recv_sems):
        x, y, c = _place()
        chips = _other_chips(x, y)
        copies = []
        for pos, j in enumerate(relations):
            k = 2 * chips[j][0] + chips[j][1]
            mine = _weight_window(in_ref, kind, shard_shape, k, c)
            cp = pltpu.make_async_remote_copy(
                src_ref=mine, dst_ref=_weight_window(out_ref, kind, shard_shape, k, c),
                send_sem=send_sems.at[pos], recv_sem=recv_sems.at[pos], device_id=(x, y, 1 - c), device_id_type=MESH)
            cp.start()
            copies.append(cp)
        for pos, j in enumerate(relations):
            theirs = _weight_window(out_ref, kind, shard_shape, 2 * chips[j][0] + chips[j][1], 1 - c)
            pltpu.make_async_remote_copy(
                src_ref=theirs, dst_ref=theirs, send_sem=send_sems.at[pos], recv_sem=recv_sems.at[pos],
                device_id=(x, y, 1 - c), device_id_type=MESH).wait_recv()
        for cp in copies:
            cp.wait_send()

    return pl.pallas_call(
        body, name=name, out_shape=jax.ShapeDtypeStruct(full.shape, full.dtype),
        in_specs=[HBM_SPEC], out_specs=HBM_SPEC, input_output_aliases={0: 0},
        scratch_shapes=[pltpu.SemaphoreType.DMA((nr,))] * 2,
    )(full)


def _grad_piece(ref, kind, h, cdim, k, half):
    if kind == "col":
        return ref.at[pl.ds(half * h, h), pl.ds(pl.multiple_of(k * cdim, 128), cdim)]
    return ref.at[k, pl.ds(half * h, h), :]


def _grad_dims(g, kind):
    return (g.shape[0] // 2, g.shape[1] // N_CHIP) if kind == "col" else (g.shape[1] // 2, g.shape[2])


def _reduce_start(grads, kinds, name):
    n = len(grads)
    dims = [_grad_dims(g, kind) for g, kind in zip(grads, kinds)]
    lands = [lax.empty((N_DEV - 1, h, cdim), g.dtype) for g, (h, cdim) in zip(grads, dims)]

    def body(*refs):
        g_ins, land_ins = refs[:n], refs[n:2 * n]
        send_sems, recv_sems = refs[2 * n], refs[2 * n + 1]
        token = refs[4 * n + 2]
        x, y, c = _place()
        for w in range(n):
            h, cdim = dims[w]
            base = (N_DEV - 1) * w
            pltpu.make_async_remote_copy(
                src_ref=_grad_piece(g_ins[w], kinds[w], h, cdim, 2 * x + y, 1 - c), dst_ref=land_ins[w].at[0],
                send_sem=send_sems.at[base], recv_sem=recv_sems.at[base],
                device_id=(x, y, 1 - c), device_id_type=MESH).start()
            for j, chip in enumerate(_other_chips(x, y)):
                for cc in range(2):
                    pltpu.make_async_remote_copy(
                        src_ref=_grad_piece(g_ins[w], kinds[w], h, cdim, 2 * chip[0] + chip[1], cc),
                        dst_ref=land_ins[w].at[1 + 2 * j + c],
                        send_sem=send_sems.at[base + 1 + 2 * j + cc], recv_sem=recv_sems.at[base + 1 + 2 * j + c],
                        device_id=(*chip, cc), device_id_type=MESH).start()
        token[...] = jnp.zeros_like(token)

    sems = pltpu.SemaphoreType.DMA(((N_DEV - 1) * n,))
    outs = pl.pallas_call(
        body, name=name,
        out_shape=(sems, sems) + tuple(pltpu.HBM(a.shape, a.dtype) for a in list(grads) + lands)
        + (jax.ShapeDtypeStruct((SUBLANE, 128), F32),),
        in_specs=[HBM_SPEC] * (2 * n), out_specs=(SEM_SPEC, SEM_SPEC) + (HBM_SPEC,) * (2 * n) + (VMEM_SPEC,),
        input_output_aliases={i: 2 + i for i in range(2 * n)},
        compiler_params=pltpu.CompilerParams(has_side_effects=EFFECT),
    )(*[_hbm(a) for a in list(grads) + lands])
    return outs[0], outs[1], list(outs[2:2 + n]), list(outs[2 + n:2 + 2 * n]), outs[2 + 2 * n]


def _reduce_wait(send_sems, recv_sems, grads, lands, kinds, after, name):
    n = len(grads)
    dims = [_grad_dims(g, kind) for g, kind in zip(grads, kinds)]

    def body(*refs):
        g_ins, land_ins = refs[:n], refs[n:2 * n]
        send_sems, recv_sems = refs[2 * n], refs[2 * n + 1]
        x, y, c = _place()
        for w in range(n):
            h, cdim = dims[w]
            piece = _grad_piece(g_ins[w], kinds[w], h, cdim, 2 * x + y, c)
            for s in range(N_DEV - 1):
                k = (N_DEV - 1) * w + s
                slot = land_ins[w].at[s]
                pltpu.make_async_remote_copy(
                    src_ref=piece, dst_ref=slot, send_sem=send_sems.at[k], recv_sem=recv_sems.at[k],
                    device_id=(x, y, 1 - c), device_id_type=MESH).wait_send()
                pltpu.make_async_remote_copy(
                    src_ref=piece, dst_ref=slot, send_sem=send_sems.at[k], recv_sem=recv_sems.at[k],
                    device_id=(x, y, 1 - c), device_id_type=MESH).wait_recv()

    outs = pl.pallas_call(
        body, name=name, out_shape=tuple(pltpu.HBM(a.shape, a.dtype) for a in list(grads) + list(lands)),
        in_specs=[HBM_SPEC] * (2 * n) + [SEM_SPEC, SEM_SPEC, ANY_SPEC], out_specs=(HBM_SPEC,) * (2 * n),
        input_output_aliases={i: i for i in range(2 * n)},
        compiler_params=pltpu.CompilerParams(has_side_effects=EFFECT),
    )(*grads, *lands, send_sems, recv_sems, after)
    return list(outs[:n]), list(outs[n:])


def _add_pieces(g, land, kind, chip_idx, core_idx, name):
    _, h, cdim = land.shape
    tr, tc = _tile(h, 256), _tile(cdim, 2048)
    nrb, ncb = h // tr, cdim // tc

    def body(k_ref, c_ref, g_ref, l_ref, o_ref):
        acc = g_ref[...].astype(F32)
        for s in range(N_DEV - 1):
            acc = acc + l_ref[s].astype(F32)
        o_ref[...] = acc

    if kind == "col":
        g_spec = pl.BlockSpec((tr, tc), lambda i, j, kr, cr: (cr[0] * nrb + i, kr[0] * ncb + j))
    else:
        g_spec = pl.BlockSpec((None, tr, tc), lambda i, j, kr, cr: (kr[0], cr[0] * nrb + i, j))
    return pl.pallas_call(
        body, name=name, out_shape=jax.ShapeDtypeStruct((2 * h, cdim), F32),
        grid_spec=pltpu.PrefetchScalarGridSpec(
            num_scalar_prefetch=2, grid=(nrb, ncb),
            in_specs=[g_spec, pl.BlockSpec((N_DEV - 1, tr, tc), lambda i, j, kr, cr: (0, i, j))],
            out_specs=pl.BlockSpec((tr, tc), lambda i, j, kr, cr: (cr[0] * nrb + i, j))),
        compiler_params=_params("parallel", "parallel"),
    )(chip_idx, core_idx, g, land)


def _join_halves(shards, name):
    n = len(shards)

    def body(*refs):
        ins, outs = refs[:n], refs[n:2 * n]
        send_sems, recv_sems = refs[2 * n:]
        x, y, c = _place()
        copies = []
        for w in range(n):
            h = shards[w].shape[0] // 2
            cp = pltpu.make_async_remote_copy(
                src_ref=ins[w].at[pl.ds(c * h, h), :], dst_ref=outs[w].at[pl.ds(c * h, h), :],
                send_sem=send_sems.at[w], recv_sem=recv_sems.at[w], device_id=(x, y, 1 - c), device_id_type=MESH)
            cp.start()
            copies.append(cp)
        for w in range(n):
            h = shards[w].shape[0] // 2
            theirs = outs[w].at[pl.ds((1 - c) * h, h), :]
            pltpu.make_async_remote_copy(
                src_ref=theirs, dst_ref=theirs, send_sem=send_sems.at[w], recv_sem=recv_sems.at[w],
                device_id=(x, y, 1 - c), device_id_type=MESH).wait_recv()
        for cp in copies:
            cp.wait_send()

    return pl.pallas_call(
        body, name=name, out_shape=tuple(jax.ShapeDtypeStruct(s.shape, s.dtype) for s in shards),
        in_specs=[HBM_SPEC] * n, out_specs=tuple([HBM_SPEC] * n),
        input_output_aliases={w: w for w in range(n)},
        scratch_shapes=[pltpu.SemaphoreType.DMA((n,))] * 2,
    )(*shards)


def _sum_devices(gathered, rows, name):
    n = gathered.shape[1]

    def body(g_ref, o_ref):
        acc = g_ref[0:rows, :]
        for dev in range(1, N_DEV):
            acc = acc + g_ref[dev * rows:(dev + 1) * rows, :]
        o_ref[...] = acc

    return pl.pallas_call(
        body, name=name, out_shape=jax.ShapeDtypeStruct((rows, n), F32),
        in_specs=[VMEM_SPEC], out_specs=VMEM_SPEC,
        compiler_params=pltpu.CompilerParams(vmem_limit_bytes=VMEM_LIMIT),
    )(gathered)


def _pack(vectors, width):
    flat = [v.reshape(-1) for v in vectors]
    offsets, total = [], 0
    for f in flat:
        offsets.append(total)
        total += f.shape[0]
    rows = -(-total // (width * SUBLANE)) * SUBLANE
    flat.append(jnp.zeros((rows * width - total,), F32))
    return jnp.concatenate(flat).reshape(rows, width), offsets


def kernel(x, c, mod_w, mod_b, norm_g, a_w_in, a_conv_w, a_conv_b, a_w_out, b_w_in, b_ln_g, b_ln_b, b_w_s, b_b_s, b_w_out, final_g, loss_target, m_mod_w, m_mod_b, m_norm_g, m_a_w_in, m_a_conv_w, m_a_conv_b, m_a_w_out, m_b_w_in, m_b_ln_g, m_b_ln_b, m_b_w_s, m_b_b_s, m_b_w_out, m_final_g, v_mod_w, v_mod_b, v_norm_g, v_a_w_in, v_a_conv_w, v_a_conv_b, v_a_w_out, v_b_w_in, v_b_ln_g, v_b_ln_b, v_b_w_s, v_b_b_s, v_b_w_out, v_final_g):
    seq, d = x.shape[1], x.shape[2]
    e = a_conv_b.shape[1]
    wd = mod_w.shape[2]
    ax, ay, ac = _place()
    chip = 2 * ax + ay
    dev = 2 * chip + ac
    chip_idx = jnp.reshape(chip, (1,)).astype(jnp.int32)
    core_idx = jnp.reshape(ac, (1,)).astype(jnp.int32)

    x2d = x[0]
    target = loss_target[0]

    w_names = ["a_w_in", "a_w_out", "b_w_in", "b_w_out"]
    w_kinds = ["col", "row", "col", "row"]
    w_shards = [a_w_in[0], a_w_out[0], b_w_in[0], b_w_out[0]]
    w_shapes = [sh.shape for sh in w_shards]
    placed = [_place_own(sh, kind, chip_idx, "place_" + nm) for nm, sh, kind in zip(w_names, w_shards, w_kinds)]
    a_groups, far_groups = [[(0, 0), (0, 1)]], [[(0, 2)]]
    whole_group = [[(0, 0), (0, 1), (0, 2)]]

    es = e // N_CHIP
    packed0, offs0 = _pack([c, a_conv_w, b_ln_g, b_ln_b], 1024)
    gathered0 = _all_gather(packed0, "gather_params").reshape(N_DEV, -1)
    c_all = gathered0[:, :d]
    per_chip = gathered0[0::2]

    def from_chips_cols(k, rows_):
        got = per_chip[:, offs0[k]:offs0[k] + rows_ * es].reshape(N_CHIP, rows_, es)
        return jnp.transpose(got, (1, 0, 2)).reshape(rows_, e)

    conv_w_full = from_chips_cols(1, 3)
    conv_w = [conv_w_full[k:k + 1] for k in range(3)]
    ln_g, ln_b = from_chips_cols(2, 1), from_chips_cols(3, 1)
    mod_b_cols = lax.dynamic_slice_in_dim(mod_b, chip * wd, wd, axis=1)[:, None, :]
    c_act, mod_part = _mod_fwd(c_all, mod_w, mod_b_cols, "mod_fwd")
    n_layer = mod_w.shape[0]
    mod_gathered = _all_gather(mod_part, "gather_mod")
    a_sems, (wa_in,), a_token = _gather_weights_start(
        placed[:1], w_kinds[:1], w_shapes[:1], a_groups, mod_gathered, "gather_a_w_in_start", both_cores=False)
    mod_all = mod_gathered.reshape(N_CHIP, 2, n_layer, N_DEV, wd)[:, 0]
    mod_all = jnp.transpose(mod_all, (1, 2, 0, 3)).reshape(n_layer, N_DEV, N_CHIP * wd)
    mod_me = lax.dynamic_index_in_dim(mod_all, dev, axis=1, keepdims=False)
    shift = [mod_me[l:l + 1, 0:d] for l in range(n_layer)]
    scale = [mod_me[l:l + 1, d:2 * d] for l in range(n_layer)]
    gate = [mod_me[l:l + 1, 2 * d:3 * d] for l in range(n_layer)]

    g0, g1, gf = norm_g[0:1], norm_g[1:2], final_g[None, :]

    def slab(r):
        return jnp.bitwise_xor(chip_idx, r)

    def arrived(sems, group, weight, w, after, name, both_cores=True):
        return _gather_weights_wait(sems, [(0, j) for _, j in group], [weight], [w_kinds[w]], [w_shapes[w]],
                                    after, name, both_cores)[0]

    h0, proj0 = _norm_mod_proj_slab(x2d, g0, scale[0], shift[0], wa_in, slab(0), 4, "norm_mod0_a_proj_own", a_token)
    wa_in = arrived(a_sems[0], a_groups[0], wa_in, 0, proj0, "gather_wait_near", both_cores=False)
    wa_in = _forward_to_sibling(wa_in, w_kinds[0], w_shapes[0], [0, 1], "forward_near")
    far_sems, (wa_in,), far_token = _gather_weights_start(
        [wa_in], w_kinds[:1], w_shapes[:1], far_groups, a_token, "gather_a_w_in_far_start")
    proj0 = _mm_proj_slab(h0, wa_in, proj0, slab(2), 4, "a_proj_x", far_token)
    proj0 = _mm_proj_slab(h0, wa_in, proj0, slab(1), 4, "a_proj_y", far_token)
    wa_in = arrived(far_sems[0], far_groups[0], wa_in, 0, proj0, "gather_wait_far")
    def start_whole(w, after, name):
        sems, (flight,), token = _gather_weights_start(
            placed[w:w + 1], w_kinds[w:w + 1], w_shapes[w:w + 1], whole_group, after, name)
        return sems[0], flight, token

    ao_sems, wa_out, ao_token = start_whole(1, wa_in, "gather_a_w_out_start")
    proj0 = _mm_proj_slab(h0, wa_in, proj0, slab(3), 4, "a_proj_far", ao_token)
    wa_out = arrived(ao_sems, whole_group[0], wa_out, 1, proj0, "gather_wait_a_w_out")
    bi_sems, wb_in, bi_token = start_whole(2, wa_out, "gather_b_w_in_start")
    y0, br0, x1, h1 = _conv_fwd(proj0, conv_w, a_conv_b, wa_out, x2d, gate[0], g1, scale[1], shift[1],
                                "conv_fwd", bi_token)
    wb_in = arrived(bi_sems, whole_group[0], wb_in, 2, h1, "gather_wait_b_w_in")
    bo_sems, wb_out, bo_token = start_whole(3, wb_in, "gather_b_w_out_start")
    proj1 = _mm_proj(h1, wb_in, 3, "b_proj", bo_token)
    b_s_t = jnp.transpose(b_b_s[0])
    wb_out = arrived(bo_sems, whole_group[0], wb_out, 3, proj1, "gather_wait_b_w_out")
    y1, dx2, dbr1, loss_part, g_final_g, dgate1 = _gmlp_fwd_head(
        proj1, ln_g, ln_b, b_w_s[0], b_s_t, wb_out, x1, gate[1], gf, target, "gmlp_fwd_head")

    gw_b_out = _mm_dw_out(y1, dbr1, "b_out_dw")
    dproj1, g_w_s, g_b_s_t, g_ln_g, g_ln_b = _gmlp_bwd(
        proj1, dbr1, wb_out, ln_g, ln_b, b_w_s[0], jnp.swapaxes(b_w_s[0], 1, 2), b_s_t, "gmlp_bwd")
    gw_b_in = _mm_dw_in(h1, dproj1, "b_proj_dw")
    b_kinds = ["col", "row"]
    b_send, b_recv, b_grads, b_lands, b_token = _reduce_start(
        [gw_b_in, gw_b_out.reshape(N_CHIP, e // N_CHIP, d)], b_kinds, "reduce_b_start")
    dx1, dshift1, dscale1, g_g1, dbr0, dgate0 = _mm_dh_norm_bwd(
        dproj1, wb_in, x1, dx2, g1, scale[1], "b_proj_dx", b_token, br=br0, gate=gate[0])

    early = [loss_part[0, 0:1], g_final_g, g_g1, g_ln_g, g_ln_b, jnp.transpose(g_b_s_t), g_w_s,
             jnp.concatenate([dshift1, dscale1, dgate1, dgate0], axis=1)]
    packed_e, offs_e = _pack(early, 1024)
    rows_e = packed_e.shape[0]
    blocks_e = lax.dynamic_update_slice(jnp.zeros((N_DEV, rows_e, 1024), F32), packed_e[None], (dev, 0, 0))
    e_send, e_recv, blocks_e, e_token = _all_gather_start(blocks_e, "gather_small_start")

    gw_a_out = _mm_dw_out(y0, dbr0, "a_out_dw")
    ao_send, ao_recv, ao_grads, ao_lands, ao_grad_token = _reduce_start(
        [gw_a_out.reshape(N_CHIP, e // N_CHIP, d)], ["row"], "reduce_a_out_start")
    dproj0, g_w0, g_w1, g_w2, g_conv_b = _conv_bwd(
        proj0, dbr0, wa_out, conv_w, a_conv_b, "conv_bwd", ao_grad_token + e_token)
    gw_a_in = _mm_dw_in(h0, dproj0, "a_proj_dw")
    ai_send, ai_recv, ai_grads, ai_lands, ai_token = _reduce_start([gw_a_in], ["col"], "reduce_a_in_start")
    grad_x, dshift0, dscale0, g_g0 = _mm_dh_norm_bwd(dproj0, wa_in, x2d, dx1, g0, scale[0], "a_proj_dx", ai_token)

    def finish(send, recv, grads_, lands_, kinds_, names_, after, tag):
        grads_, lands_ = _reduce_wait(send, recv, grads_, lands_, kinds_, after, "reduce_" + tag + "_wait")
        halves = [_add_pieces(g, land, kind, chip_idx, core_idx, "add_pieces_" + nm)
                  for g, land, kind, nm in zip(grads_, lands_, kinds_, names_)]
        return _join_halves(halves, "join_" + tag)

    upd, big_grads = {}, {}

    def adamw_big(nm, w, g, m, v):
        g_out, *rest = _adamw_2d(w[0], g, m[0], v[0], "adamw_" + nm)
        big_grads[nm] = g_out[None]
        upd[nm] = tuple(o[None] for o in rest)

    g_b_w_in, g_b_w_out = finish(b_send, b_recv, b_grads, b_lands, b_kinds, ["b_w_in", "b_w_out"], grad_x, "b")
    adamw_big("b_w_in", b_w_in, g_b_w_in, m_b_w_in, v_b_w_in)
    adamw_big("b_w_out", b_w_out, g_b_w_out, m_b_w_out, v_b_w_out)

    late = [g_g0, jnp.concatenate([g_w0, g_w1, g_w2], axis=0), g_conv_b, jnp.concatenate([dshift0, dscale0], axis=1)]
    packed_l, offs_l = _pack(late, 1024)
    rows_l = packed_l.shape[0]
    gathered_l = _all_gather(packed_l, "gather_small_late", after=upd["b_w_out"][0])
    blocks_e = _all_gather_wait(e_send, e_recv, blocks_e, gathered_l, "gather_small_wait")
    gathered_e = blocks_e.reshape(N_DEV * rows_e, 1024)
    summed_e = _sum_devices(gathered_e, rows_e, "sum_small_early").reshape(-1)
    summed_l = _sum_devices(gathered_l, rows_l, "sum_small_late").reshape(-1)

    def take(summed, offs, k, shape):
        size = math.prod(shape)
        return summed[offs[k]:offs[k] + size].reshape(shape)

    def rows_of(gathered, rows_, offs, k, width):
        return gathered.reshape(N_DEV, rows_ * 1024)[:, offs[k]:offs[k] + width]

    loss = take(summed_e, offs_e, 0, ())
    grad_final_g = take(summed_e, offs_e, 1, (d,))
    grad_norm_g = jnp.concatenate([take(summed_l, offs_l, 0, (1, d)), take(summed_e, offs_e, 2, (1, d))], axis=0)
    grad_ln_g_full = take(summed_e, offs_e, 3, (1, e))
    grad_ln_b_full = take(summed_e, offs_e, 4, (1, e))
    grad_b_b_s = take(summed_e, offs_e, 5, (1, GROUPS, CHUNK))
    grad_b_w_s = take(summed_e, offs_e, 6, (1, GROUPS, CHUNK, CHUNK))
    grad_conv_w_full = take(summed_l, offs_l, 1, (3, e))
    grad_a_conv_b = take(summed_l, offs_l, 2, (1, e))
    grad_a_conv_w = lax.dynamic_slice_in_dim(grad_conv_w_full, chip * es, es, axis=1)[None]
    grad_b_ln_g = lax.dynamic_slice_in_dim(grad_ln_g_full, chip * es, es, axis=1)
    grad_b_ln_b = lax.dynamic_slice_in_dim(grad_ln_b_full, chip * es, es, axis=1)
    dmod_e = rows_of(gathered_e, rows_e, offs_e, 7, 4 * d)
    dmod_l = rows_of(gathered_l, rows_l, offs_l, 3, 2 * d)
    dmod_all = jnp.stack([jnp.concatenate([dmod_l, dmod_e[:, 3 * d:]], axis=1), dmod_e[:, :3 * d]], axis=1)
    mod_b_e = take(summed_e, offs_e, 7, (4 * d,))
    mod_b_l = take(summed_l, offs_l, 3, (2 * d,))
    grad_mod_b = jnp.stack([jnp.concatenate([mod_b_l, mod_b_e[3 * d:]]), mod_b_e[:3 * d]])
    dmod_cols = jnp.transpose(lax.dynamic_slice_in_dim(dmod_all, chip * wd, wd, axis=2), (1, 0, 2))

    grad_mod_w, delta_mod_w, new_m_mod_w, new_v_mod_w = _mod_w_update(
        jnp.transpose(c_act), dmod_cols, mod_w, m_mod_w, v_mod_w, "mod_w_update")
    (g_a_w_out,) = finish(ao_send, ao_recv, ao_grads, ao_lands, ["row"], ["a_w_out"], delta_mod_w, "a_out")
    adamw_big("a_w_out", a_w_out, g_a_w_out, m_a_w_out, v_a_w_out)
    (g_a_w_in,) = finish(ai_send, ai_recv, ai_grads, ai_lands, ["col"], ["a_w_in"], upd["a_w_out"][0], "a_in")
    adamw_big("a_w_in", a_w_in, g_a_w_in, m_a_w_in, v_a_w_in)
    small_w = [("mod_b", mod_b, grad_mod_b, m_mod_b, v_mod_b), ("norm_g", norm_g, grad_norm_g, m_norm_g, v_norm_g),
               ("a_conv_w", a_conv_w, grad_a_conv_w, m_a_conv_w, v_a_conv_w),
               ("a_conv_b", a_conv_b, grad_a_conv_b, m_a_conv_b, v_a_conv_b),
               ("b_ln_g", b_ln_g, grad_b_ln_g, m_b_ln_g, v_b_ln_g), ("b_ln_b", b_ln_b, grad_b_ln_b, m_b_ln_b, v_b_ln_b),
               ("b_w_s", b_w_s, grad_b_w_s, m_b_w_s, v_b_w_s), ("b_b_s", b_b_s, grad_b_b_s, m_b_b_s, v_b_b_s),
               ("final_g", final_g, grad_final_g, m_final_g, v_final_g)]

    def flat2d(a):
        return a.reshape(-1, a.shape[-1])

    res = _adamw_small([flat2d(t[1]) for t in small_w], [flat2d(t[2]) for t in small_w],
                       [flat2d(t[3]) for t in small_w], [flat2d(t[4]) for t in small_w], "adamw_small")
    for (nm, w, _, _, _), r3 in zip(small_w, res):
        upd[nm] = tuple(o.reshape(w.shape) for o in r3)
    upd["mod_w"] = (delta_mod_w, new_m_mod_w, new_v_mod_w)

    grads = {"mod_w": grad_mod_w, "mod_b": grad_mod_b, "norm_g": grad_norm_g, "a_conv_w": grad_a_conv_w,
             "a_conv_b": grad_a_conv_b, "b_ln_g": grad_b_ln_g, "b_ln_b": grad_b_ln_b, "b_w_s": grad_b_w_s,
             "b_b_s": grad_b_b_s, "final_g": grad_final_g, **big_grads}
    order = ["mod_w", "mod_b", "norm_g", "a_w_in", "a_conv_w", "a_conv_b", "a_w_out", "b_w_in", "b_ln_g", "b_ln_b",
             "b_w_s", "b_b_s", "b_w_out", "final_g"]
    return (loss, grad_x[None], *[grads[k] for k in order], *[upd[k][0] for k in order],
            *[upd[k][1] for k in order], *[upd[k][2] for k in order])
```
